```python
import math
import jax, jax.numpy as jnp
from jax import lax
import numpy as np


D_MODEL = 1024
BATCH = 32
SEQ = 2048
DEPTH = 1

SSD_HEAD_DIM = 64
SSD_WIDTH = D_MODEL
SSD_HEADS = SSD_WIDTH // SSD_HEAD_DIM
SSD_GROUPS = 4
SSD_STATE = 128
SSD_CONV = 5
SSD_CHUNK = 128
XBC_WIDTH = SSD_WIDTH + 2 * SSD_GROUPS * SSD_STATE
S5_WIDTH = D_MODEL // 2
S5_GROUP_CH = 16
S5_GROUPS = S5_WIDTH // S5_GROUP_CH
S5_STATE = 64
MIX_WIDTH = SSD_WIDTH + S5_WIDTH
IN_PROJ_WIDTH = SSD_WIDTH + XBC_WIDTH + 2 * SSD_HEADS + S5_WIDTH
D_FF = 256 * ((8 * D_MODEL // 3 + 255) // 256)
FFN_CONV = 3
EPS = 1e-6

kernel_name = 'hybrid_ssd_s5_encoder_block'


def rms_norm(x, w):
    xf = x.astype(jnp.float32)
    xf = xf * lax.rsqrt(jnp.mean(xf * xf, axis=-1, keepdims=True) + EPS)
    return (xf * w.astype(jnp.float32)).astype(x.dtype)


def dwconv(x, w, b):
    k = w.shape[0]
    y = lax.conv_general_dilated(
        x, w[:, None, :].astype(x.dtype), window_strides=(1,),
        padding=[(k // 2, k // 2)], dimension_numbers=('NWC', 'WIO', 'NWC'),
        feature_group_count=x.shape[-1])
    return y + b.astype(x.dtype)


def ssd_scan(xs, dt, a_log, bm, cm):
    bsz, l, h, p = xs.shape
    g, n = bm.shape[2], bm.shape[3]
    r = h // g
    nc = l // SSD_CHUNK
    a = -jnp.exp(a_log.astype(jnp.float32))
    xdt = (xs.astype(jnp.float32) * dt[..., None]).reshape(bsz, nc, SSD_CHUNK, g, r, p)
    bc = bm.astype(jnp.float32).reshape(bsz, nc, SSD_CHUNK, g, n)
    cc = cm.astype(jnp.float32).reshape(bsz, nc, SSD_CHUNK, g, n)
    a_cs = jnp.cumsum((dt * a).reshape(bsz, nc, SSD_CHUNK, g, r), axis=2)
    diff = a_cs[:, :, :, None] - a_cs[:, :, None]
    mask = jnp.tril(jnp.ones((SSD_CHUNK, SSD_CHUNK), dtype=bool))[None, None, :, :, None, None]
    seg = jnp.exp(jnp.where(mask, diff, -jnp.inf))
    scores = jnp.einsum('bcqgn,bcsgn->bcqsg', cc, bc)
    y_diag = jnp.einsum('bcqsgr,bcsgrp->bcqgrp', scores[..., None] * seg, xdt)
    decay_states = jnp.exp(a_cs[:, :, -1:] - a_cs)
    states = jnp.einsum('bcsgn,bcsgr,bcsgrp->bcgrpn', bc, decay_states, xdt)
    chunk_decay = jnp.exp(a_cs[:, :, -1])

    def step(carry, inp):
        dec, st = inp
        return dec[..., None, None] * carry + st, carry

    init = jnp.zeros((bsz, g, r, p, n), jnp.float32)
    _, prev = lax.scan(step, init, (jnp.moveaxis(chunk_decay, 1, 0), jnp.moveaxis(states, 1, 0)))
    prev = jnp.moveaxis(prev, 0, 1)
    y_off = jnp.einsum('bcqgn,bcgrpn,bcqgr->bcqgrp', cc, prev, jnp.exp(a_cs))
    return (y_diag + y_off).reshape(bsz, l, h, p)


def ssd_mixer(z, xbc, dt_raw, conv_w, conv_b, dt_bias_f, dt_bias_b, a_log_f, a_log_b, d, norm_w):
    bsz, l, _ = z.shape
    xbc = jax.nn.silu(dwconv(xbc, conv_w, conv_b))
    xs, bm, cm = jnp.split(xbc, [SSD_WIDTH, SSD_WIDTH + SSD_GROUPS * SSD_STATE], axis=-1)
    xs = xs.reshape(bsz, l, SSD_HEADS, SSD_HEAD_DIM)
    bm = bm.reshape(bsz, l, SSD_GROUPS, SSD_STATE)
    cm = cm.reshape(bsz, l, SSD_GROUPS, SSD_STATE)
    dt_raw = dt_raw.astype(jnp.float32)
    dtf = jax.nn.softplus(dt_raw[..., :SSD_HEADS] + dt_bias_f.astype(jnp.float32))
    dtb = jax.nn.softplus(dt_raw[..., SSD_HEADS:] + dt_bias_b.astype(jnp.float32))
    y_f = ssd_scan(xs, dtf, a_log_f, bm, cm)
    y_b = jnp.flip(ssd_scan(jnp.flip(xs, 1), jnp.flip(dtb, 1), a_log_b,
                            jnp.flip(bm, 1), jnp.flip(cm, 1)), 1)
    y = y_f + y_b + d.astype(jnp.float32)[:, None] * xs.astype(jnp.float32)
    y = y.reshape(bsz, l, SSD_WIDTH) * jax.nn.silu(z.astype(jnp.float32))
    return rms_norm(y, norm_w).astype(z.dtype)


def _complex_combine(left, right):
    a1r, a1i, b1r, b1i = left
    a2r, a2i, b2r, b2i = right
    return (a2r * a1r - a2i * a1i,
            a2r * a1i + a2i * a1r,
            a2r * b1r - a2i * b1i + b2r,
            a2r * b1i + a2i * b1r + b2i)


def s5_scan(u, lam_re, lam_im, log_step, b_re, b_im, c_re, c_im, reverse):
    lam_re = lam_re.astype(jnp.float32)
    lam_im = lam_im.astype(jnp.float32)
    step = jnp.exp(log_step.astype(jnp.float32))[:, None]
    mag = jnp.exp(lam_re * step)
    ar = mag * jnp.cos(lam_im * step)
    ai = mag * jnp.sin(lam_im * step)
    den = lam_re * lam_re + lam_im * lam_im
    cr = ((ar - 1.0) * lam_re + ai * lam_im) / den
    ci = (ai * lam_re - (ar - 1.0) * lam_im) / den
    b_re = b_re.astype(jnp.float32)
    b_im = b_im.astype(jnp.float32)
    bbr = cr[..., None] * b_re - ci[..., None] * b_im
    bbi = cr[..., None] * b_im + ci[..., None] * b_re
    bur = jnp.einsum('blgc,gpc->lbgp', u, bbr)
    bui = jnp.einsum('blgc,gpc->lbgp', u, bbi)
    seq_len = u.shape[1]
    a_r = jnp.broadcast_to(ar[None, None], (seq_len, 1) + ar.shape)
    a_i = jnp.broadcast_to(ai[None, None], (seq_len, 1) + ai.shape)
    _, _, hr, hi = lax.associative_scan(_complex_combine, (a_r, a_i, bur, bui),
                                        reverse=reverse, axis=0)
    return (jnp.einsum('lbgp,gcp->blgc', hr, c_re.astype(jnp.float32))
            - jnp.einsum('lbgp,gcp->blgc', hi, c_im.astype(jnp.float32)))


def s5_mixer(u, lam_re_f, lam_im_f, log_step_f, lam_re_b, lam_im_b, log_step_b,
             b_re, b_im, c_re_f, c_im_f, c_re_b, c_im_b, d, glu_w, glu_b, norm_w):
    bsz, l, _ = u.shape
    uf = u.astype(jnp.float32).reshape(bsz, l, S5_GROUPS, S5_GROUP_CH)
    y = (s5_scan(uf, lam_re_f, lam_im_f, log_step_f, b_re, b_im, c_re_f, c_im_f, False)
         + s5_scan(uf, lam_re_b, lam_im_b, log_step_b, b_re, b_im, c_re_b, c_im_b, True)
         + d.astype(jnp.float32).reshape(S5_GROUPS, S5_GROUP_CH) * uf)
    g = jax.nn.gelu(y)
    gl = jnp.einsum('blgc,gcd->blgd', g, glu_w.astype(jnp.float32)) + glu_b.astype(jnp.float32)
    out = gl[..., :S5_GROUP_CH] * jax.nn.sigmoid(gl[..., S5_GROUP_CH:])
    return rms_norm(out.reshape(bsz, l, S5_WIDTH), norm_w).astype(u.dtype)


def conv_ffn(h, w_up, conv_w, conv_b, w_down):
    up = dwconv(h @ w_up, conv_w, conv_b)
    val, gate = jnp.split(up, 2, axis=-1)
    return (jax.nn.silu(gate) * val) @ w_down


def _fwd_setup_inputs(seed: int = 0) -> dict:
    key = jax.random.key(seed)
    ks = iter(jax.random.split(key, 48))
    f32 = jnp.float32

    def nrm(shape, scale):
        return scale * jax.random.normal(next(ks), shape, f32)

    def unif(shape, lo, hi):
        return jax.random.uniform(next(ks), shape, f32, minval=lo, maxval=hi)

    x = jax.random.normal(next(ks), (BATCH, SEQ, D_MODEL), f32)
    norm_mix_w = 1.0 + nrm((DEPTH, D_MODEL), 0.02)
    w_in = nrm((DEPTH, D_MODEL, IN_PROJ_WIDTH), D_MODEL ** -0.5)
    ssd_conv_w = nrm((DEPTH, SSD_CONV, XBC_WIDTH), SSD_CONV ** -0.5)
    ssd_conv_b = nrm((DEPTH, XBC_WIDTH), 0.02)
    dt0_f = jnp.exp(unif((DEPTH, SSD_HEADS), math.log(1e-3), math.log(1e-1)))
    ssd_dt_bias_fwd = dt0_f + jnp.log(-jnp.expm1(-dt0_f))
    dt0_b = jnp.exp(unif((DEPTH, SSD_HEADS), math.log(1e-3), math.log(1e-1)))
    ssd_dt_bias_bwd = dt0_b + jnp.log(-jnp.expm1(-dt0_b))
    ssd_a_log_fwd = jnp.log(unif((DEPTH, SSD_HEADS), 1.0, 16.0))
    ssd_a_log_bwd = jnp.log(unif((DEPTH, SSD_HEADS), 1.0, 16.0))
    ssd_d = 1.0 + nrm((DEPTH, SSD_HEADS), 0.1)
    ssd_norm_w = 1.0 + nrm((DEPTH, SSD_WIDTH), 0.02)
    n_idx = math.pi * jnp.arange(S5_STATE, dtype=f32)
    s5_lambda_re_fwd = -0.5 + nrm((DEPTH, S5_GROUPS, S5_STATE), 0.01)
    s5_lambda_im_fwd = n_idx + nrm((DEPTH, S5_GROUPS, S5_STATE), 0.01)
    s5_log_step_fwd = unif((DEPTH, S5_GROUPS), math.log(1e-3), math.log(1e-1))
    s5_lambda_re_bwd = -0.5 + nrm((DEPTH, S5_GROUPS, S5_STATE), 0.01)
    s5_lambda_im_bwd = n_idx + nrm((DEPTH, S5_GROUPS, S5_STATE), 0.01)
    s5_log_step_bwd = unif((DEPTH, S5_GROUPS), math.log(1e-3), math.log(1e-1))
    s5_b_re = nrm((DEPTH, S5_GROUPS, S5_STATE, S5_GROUP_CH), (2 * S5_GROUP_CH) ** -0.5)
    s5_b_im = nrm((DEPTH, S5_GROUPS, S5_STATE, S5_GROUP_CH), (2 * S5_GROUP_CH) ** -0.5)
    s5_c_re_fwd = nrm((DEPTH, S5_GROUPS, S5_GROUP_CH, S5_STATE), (2 * S5_STATE) ** -0.5)
    s5_c_im_fwd = nrm((DEPTH, S5_GROUPS, S5_GROUP_CH, S5_STATE), (2 * S5_STATE) ** -0.5)
    s5_c_re_bwd = nrm((DEPTH, S5_GROUPS, S5_GROUP_CH, S5_STATE), (2 * S5_STATE) ** -0.5)
    s5_c_im_bwd = nrm((DEPTH, S5_GROUPS, S5_GROUP_CH, S5_STATE), (2 * S5_STATE) ** -0.5)
    s5_d = nrm((DEPTH, S5_WIDTH), 0.5)
    s5_glu_w = nrm((DEPTH, S5_GROUPS, S5_GROUP_CH, 2 * S5_GROUP_CH), S5_GROUP_CH ** -0.5)
    s5_glu_b = nrm((DEPTH, S5_GROUPS, 2 * S5_GROUP_CH), 0.02)
    s5_norm_w = 1.0 + nrm((DEPTH, S5_WIDTH), 0.02)
    w_out = nrm((DEPTH, MIX_WIDTH, D_MODEL), MIX_WIDTH ** -0.5)
    norm_ffn_w = 1.0 + nrm((DEPTH, D_MODEL), 0.02)
    ffn_w_up = nrm((DEPTH, D_MODEL, 2 * D_FF), D_MODEL ** -0.5)
    ffn_conv_w = nrm((DEPTH, FFN_CONV, 2 * D_FF), FFN_CONV ** -0.5)
    ffn_conv_b = nrm((DEPTH, 2 * D_FF), 0.02)
    ffn_w_down = nrm((DEPTH, D_FF, D_MODEL), D_FF ** -0.5)
    norm_final_w = 1.0 + nrm((D_MODEL,), 0.02)
    return {
        'x': x, 'norm_mix_w': norm_mix_w, 'w_in': w_in,
        'ssd_conv_w': ssd_conv_w, 'ssd_conv_b': ssd_conv_b,
        'ssd_dt_bias_fwd': ssd_dt_bias_fwd, 'ssd_dt_bias_bwd': ssd_dt_bias_bwd,
        'ssd_a_log_fwd': ssd_a_log_fwd, 'ssd_a_log_bwd': ssd_a_log_bwd,
        'ssd_d': ssd_d, 'ssd_norm_w': ssd_norm_w,
        's5_lambda_re_fwd': s5_lambda_re_fwd, 's5_lambda_im_fwd': s5_lambda_im_fwd,
        's5_log_step_fwd': s5_log_step_fwd,
        's5_lambda_re_bwd': s5_lambda_re_bwd, 's5_lambda_im_bwd': s5_lambda_im_bwd,
        's5_log_step_bwd': s5_log_step_bwd,
        's5_b_re': s5_b_re, 's5_b_im': s5_b_im,
        's5_c_re_fwd': s5_c_re_fwd, 's5_c_im_fwd': s5_c_im_fwd,
        's5_c_re_bwd': s5_c_re_bwd, 's5_c_im_bwd': s5_c_im_bwd,
        's5_d': s5_d, 's5_glu_w': s5_glu_w, 's5_glu_b': s5_glu_b, 's5_norm_w': s5_norm_w,
        'w_out': w_out, 'norm_ffn_w': norm_ffn_w,
        'ffn_w_up': ffn_w_up, 'ffn_conv_w': ffn_conv_w, 'ffn_conv_b': ffn_conv_b,
        'ffn_w_down': ffn_w_down, 'norm_final_w': norm_final_w,
    }


def _fwd_reference(x, norm_mix_w, w_in, ssd_conv_w, ssd_conv_b, ssd_dt_bias_fwd, ssd_dt_bias_bwd,
              ssd_a_log_fwd, ssd_a_log_bwd, ssd_d, ssd_norm_w,
              s5_lambda_re_fwd, s5_lambda_im_fwd, s5_log_step_fwd,
              s5_lambda_re_bwd, s5_lambda_im_bwd, s5_log_step_bwd,
              s5_b_re, s5_b_im, s5_c_re_fwd, s5_c_im_fwd, s5_c_re_bwd, s5_c_im_bwd,
              s5_d, s5_glu_w, s5_glu_b, s5_norm_w, w_out, norm_ffn_w,
              ffn_w_up, ffn_conv_w, ffn_conv_b, ffn_w_down, norm_final_w):
    h = x
    cuts = [SSD_WIDTH, SSD_WIDTH + XBC_WIDTH, SSD_WIDTH + XBC_WIDTH + 2 * SSD_HEADS]
    for i in range(DEPTH):
        hn = rms_norm(h, norm_mix_w[i])
        proj = hn @ w_in[i]
        z, xbc, dt_raw, u = jnp.split(proj, cuts, axis=-1)
        y_ssd = ssd_mixer(z, xbc, dt_raw, ssd_conv_w[i], ssd_conv_b[i],
                          ssd_dt_bias_fwd[i], ssd_dt_bias_bwd[i],
                          ssd_a_log_fwd[i], ssd_a_log_bwd[i], ssd_d[i], ssd_norm_w[i])
        y_s5 = s5_mixer(u, s5_lambda_re_fwd[i], s5_lambda_im_fwd[i], s5_log_step_fwd[i],
                        s5_lambda_re_bwd[i], s5_lambda_im_bwd[i], s5_log_step_bwd[i],
                        s5_b_re[i], s5_b_im[i], s5_c_re_fwd[i], s5_c_im_fwd[i],
                        s5_c_re_bwd[i], s5_c_im_bwd[i], s5_d[i], s5_glu_w[i], s5_glu_b[i],
                        s5_norm_w[i])
        h = h + jnp.concatenate([y_ssd, y_s5], axis=-1) @ w_out[i]
        h = h + conv_ffn(rms_norm(h, norm_ffn_w[i]), ffn_w_up[i], ffn_conv_w[i],
                         ffn_conv_b[i], ffn_w_down[i])
    return rms_norm(h, norm_final_w)


import jax as _jax
import jax.numpy as _jnp

TWIN_FORMAT = 'train_step'
FWD_PARAMS = ['x', 'norm_mix_w', 'w_in', 'ssd_conv_w', 'ssd_conv_b', 'ssd_dt_bias_fwd', 'ssd_dt_bias_bwd', 'ssd_a_log_fwd', 'ssd_a_log_bwd', 'ssd_d', 'ssd_norm_w', 's5_lambda_re_fwd', 's5_lambda_im_fwd', 's5_log_step_fwd', 's5_lambda_re_bwd', 's5_lambda_im_bwd', 's5_log_step_bwd', 's5_b_re', 's5_b_im', 's5_c_re_fwd', 's5_c_im_fwd', 's5_c_re_bwd', 's5_c_im_bwd', 's5_d', 's5_glu_w', 's5_glu_b', 's5_norm_w', 'w_out', 'norm_ffn_w', 'ffn_w_up', 'ffn_conv_w', 'ffn_conv_b', 'ffn_w_down', 'norm_final_w']
TWIN_WEIGHTS = ['norm_mix_w', 'w_in', 'ssd_conv_w', 'ssd_conv_b', 'ssd_dt_bias_fwd', 'ssd_dt_bias_bwd', 'ssd_a_log_fwd', 'ssd_a_log_bwd', 'ssd_d', 'ssd_norm_w', 's5_lambda_re_fwd', 's5_lambda_im_fwd', 's5_log_step_fwd', 's5_lambda_re_bwd', 's5_lambda_im_bwd', 's5_log_step_bwd', 's5_b_re', 's5_b_im', 's5_c_re_fwd', 's5_c_im_fwd', 's5_c_re_bwd', 's5_c_im_bwd', 's5_d', 's5_glu_w', 's5_glu_b', 's5_norm_w', 'w_out', 'norm_ffn_w', 'ffn_w_up', 'ffn_conv_w', 'ffn_conv_b', 'ffn_w_down', 'norm_final_w']
TWIN_DIFF_INPUT = 'x'
TWIN_INPUTS = ['x', 'norm_mix_w', 'w_in', 'ssd_conv_w', 'ssd_conv_b', 'ssd_dt_bias_fwd', 'ssd_dt_bias_bwd', 'ssd_a_log_fwd', 'ssd_a_log_bwd', 'ssd_d', 'ssd_norm_w', 's5_lambda_re_fwd', 's5_lambda_im_fwd', 's5_log_step_fwd', 's5_lambda_re_bwd', 's5_lambda_im_bwd', 's5_log_step_bwd', 's5_b_re', 's5_b_im', 's5_c_re_fwd', 's5_c_im_fwd', 's5_c_re_bwd', 's5_c_im_bwd', 's5_d', 's5_glu_w', 's5_glu_b', 's5_norm_w', 'w_out', 'norm_ffn_w', 'ffn_w_up', 'ffn_conv_w', 'ffn_conv_b', 'ffn_w_down', 'norm_final_w', 'loss_target', 'm_norm_mix_w', 'm_w_in', 'm_ssd_conv_w', 'm_ssd_conv_b', 'm_ssd_dt_bias_fwd', 'm_ssd_dt_bias_bwd', 'm_ssd_a_log_fwd', 'm_ssd_a_log_bwd', 'm_ssd_d', 'm_ssd_norm_w', 'm_s5_lambda_re_fwd', 'm_s5_lambda_im_fwd', 'm_s5_log_step_fwd', 'm_s5_lambda_re_bwd', 'm_s5_lambda_im_bwd', 'm_s5_log_step_bwd', 'm_s5_b_re', 'm_s5_b_im', 'm_s5_c_re_fwd', 'm_s5_c_im_fwd', 'm_s5_c_re_bwd', 'm_s5_c_im_bwd', 'm_s5_d', 'm_s5_glu_w', 'm_s5_glu_b', 'm_s5_norm_w', 'm_w_out', 'm_norm_ffn_w', 'm_ffn_w_up', 'm_ffn_conv_w', 'm_ffn_conv_b', 'm_ffn_w_down', 'm_norm_final_w', 'v_norm_mix_w', 'v_w_in', 'v_ssd_conv_w', 'v_ssd_conv_b', 'v_ssd_dt_bias_fwd', 'v_ssd_dt_bias_bwd', 'v_ssd_a_log_fwd', 'v_ssd_a_log_bwd', 'v_ssd_d', 'v_ssd_norm_w', 'v_s5_lambda_re_fwd', 'v_s5_lambda_im_fwd', 'v_s5_log_step_fwd', 'v_s5_lambda_re_bwd', 'v_s5_lambda_im_bwd', 'v_s5_log_step_bwd', 'v_s5_b_re', 'v_s5_b_im', 'v_s5_c_re_fwd', 'v_s5_c_im_fwd', 'v_s5_c_re_bwd', 'v_s5_c_im_bwd', 'v_s5_d', 'v_s5_glu_w', 'v_s5_glu_b', 'v_s5_norm_w', 'v_w_out', 'v_norm_ffn_w', 'v_ffn_w_up', 'v_ffn_conv_w', 'v_ffn_conv_b', 'v_ffn_w_down', 'v_norm_final_w']
TWIN_OUTPUTS = ['loss', 'grad_x', 'grad_norm_mix_w', 'grad_w_in', 'grad_ssd_conv_w', 'grad_ssd_conv_b', 'grad_ssd_dt_bias_fwd', 'grad_ssd_dt_bias_bwd', 'grad_ssd_a_log_fwd', 'grad_ssd_a_log_bwd', 'grad_ssd_d', 'grad_ssd_norm_w', 'grad_s5_lambda_re_fwd', 'grad_s5_lambda_im_fwd', 'grad_s5_log_step_fwd', 'grad_s5_lambda_re_bwd', 'grad_s5_lambda_im_bwd', 'grad_s5_log_step_bwd', 'grad_s5_b_re', 'grad_s5_b_im', 'grad_s5_c_re_fwd', 'grad_s5_c_im_fwd', 'grad_s5_c_re_bwd', 'grad_s5_c_im_bwd', 'grad_s5_d', 'grad_s5_glu_w', 'grad_s5_glu_b', 'grad_s5_norm_w', 'grad_w_out', 'grad_norm_ffn_w', 'grad_ffn_w_up', 'grad_ffn_conv_w', 'grad_ffn_conv_b', 'grad_ffn_w_down', 'grad_norm_final_w', 'delta_norm_mix_w', 'delta_w_in', 'delta_ssd_conv_w', 'delta_ssd_conv_b', 'delta_ssd_dt_bias_fwd', 'delta_ssd_dt_bias_bwd', 'delta_ssd_a_log_fwd', 'delta_ssd_a_log_bwd', 'delta_ssd_d', 'delta_ssd_norm_w', 'delta_s5_lambda_re_fwd', 'delta_s5_lambda_im_fwd', 'delta_s5_log_step_fwd', 'delta_s5_lambda_re_bwd', 'delta_s5_lambda_im_bwd', 'delta_s5_log_step_bwd', 'delta_s5_b_re', 'delta_s5_b_im', 'delta_s5_c_re_fwd', 'delta_s5_c_im_fwd', 'delta_s5_c_re_bwd', 'delta_s5_c_im_bwd', 'delta_s5_d', 'delta_s5_glu_w', 'delta_s5_glu_b', 'delta_s5_norm_w', 'delta_w_out', 'delta_norm_ffn_w', 'delta_ffn_w_up', 'delta_ffn_conv_w', 'delta_ffn_conv_b', 'delta_ffn_w_down', 'delta_norm_final_w', 'new_m_norm_mix_w', 'new_m_w_in', 'new_m_ssd_conv_w', 'new_m_ssd_conv_b', 'new_m_ssd_dt_bias_fwd', 'new_m_ssd_dt_bias_bwd', 'new_m_ssd_a_log_fwd', 'new_m_ssd_a_log_bwd', 'new_m_ssd_d', 'new_m_ssd_norm_w', 'new_m_s5_lambda_re_fwd', 'new_m_s5_lambda_im_fwd', 'new_m_s5_log_step_fwd', 'new_m_s5_lambda_re_bwd', 'new_m_s5_lambda_im_bwd', 'new_m_s5_log_step_bwd', 'new_m_s5_b_re', 'new_m_s5_b_im', 'new_m_s5_c_re_fwd', 'new_m_s5_c_im_fwd', 'new_m_s5_c_re_bwd', 'new_m_s5_c_im_bwd', 'new_m_s5_d', 'new_m_s5_glu_w', 'new_m_s5_glu_b', 'new_m_s5_norm_w', 'new_m_w_out', 'new_m_norm_ffn_w', 'new_m_ffn_w_up', 'new_m_ffn_conv_w', 'new_m_ffn_conv_b', 'new_m_ffn_w_down', 'new_m_norm_final_w', 'new_v_norm_mix_w', 'new_v_w_in', 'new_v_ssd_conv_w', 'new_v_ssd_conv_b', 'new_v_ssd_dt_bias_fwd', 'new_v_ssd_dt_bias_bwd', 'new_v_ssd_a_log_fwd', 'new_v_ssd_a_log_bwd', 'new_v_ssd_d', 'new_v_ssd_norm_w', 'new_v_s5_lambda_re_fwd', 'new_v_s5_lambda_im_fwd', 'new_v_s5_log_step_fwd', 'new_v_s5_lambda_re_bwd', 'new_v_s5_lambda_im_bwd', 'new_v_s5_log_step_bwd', 'new_v_s5_b_re', 'new_v_s5_b_im', 'new_v_s5_c_re_fwd', 'new_v_s5_c_im_fwd', 'new_v_s5_c_re_bwd', 'new_v_s5_c_im_bwd', 'new_v_s5_d', 'new_v_s5_glu_w', 'new_v_s5_glu_b', 'new_v_s5_norm_w', 'new_v_w_out', 'new_v_norm_ffn_w', 'new_v_ffn_w_up', 'new_v_ffn_conv_w', 'new_v_ffn_conv_b', 'new_v_ffn_w_down', 'new_v_norm_final_w']
TWIN_LEAF_KINDS = {'loss': 'loss', 'grad_x': 'grad_x', 'grad_norm_mix_w': 'grad_w', 'grad_w_in': 'grad_w', 'grad_ssd_conv_w': 'grad_w', 'grad_ssd_conv_b': 'grad_w', 'grad_ssd_dt_bias_fwd': 'grad_w', 'grad_ssd_dt_bias_bwd': 'grad_w', 'grad_ssd_a_log_fwd': 'grad_w', 'grad_ssd_a_log_bwd': 'grad_w', 'grad_ssd_d': 'grad_w', 'grad_ssd_norm_w': 'grad_w', 'grad_s5_lambda_re_fwd': 'grad_w', 'grad_s5_lambda_im_fwd': 'grad_w', 'grad_s5_log_step_fwd': 'grad_w', 'grad_s5_lambda_re_bwd': 'grad_w', 'grad_s5_lambda_im_bwd': 'grad_w', 'grad_s5_log_step_bwd': 'grad_w', 'grad_s5_b_re': 'grad_w', 'grad_s5_b_im': 'grad_w', 'grad_s5_c_re_fwd': 'grad_w', 'grad_s5_c_im_fwd': 'grad_w', 'grad_s5_c_re_bwd': 'grad_w', 'grad_s5_c_im_bwd': 'grad_w', 'grad_s5_d': 'grad_w', 'grad_s5_glu_w': 'grad_w', 'grad_s5_glu_b': 'grad_w', 'grad_s5_norm_w': 'grad_w', 'grad_w_out': 'grad_w', 'grad_norm_ffn_w': 'grad_w', 'grad_ffn_w_up': 'grad_w', 'grad_ffn_conv_w': 'grad_w', 'grad_ffn_conv_b': 'grad_w', 'grad_ffn_w_down': 'grad_w', 'grad_norm_final_w': 'grad_w', 'delta_norm_mix_w': 'delta_w', 'delta_w_in': 'delta_w', 'delta_ssd_conv_w': 'delta_w', 'delta_ssd_conv_b': 'delta_w', 'delta_ssd_dt_bias_fwd': 'delta_w', 'delta_ssd_dt_bias_bwd': 'delta_w', 'delta_ssd_a_log_fwd': 'delta_w', 'delta_ssd_a_log_bwd': 'delta_w', 'delta_ssd_d': 'delta_w', 'delta_ssd_norm_w': 'delta_w', 'delta_s5_lambda_re_fwd': 'delta_w', 'delta_s5_lambda_im_fwd': 'delta_w', 'delta_s5_log_step_fwd': 'delta_w', 'delta_s5_lambda_re_bwd': 'delta_w', 'delta_s5_lambda_im_bwd': 'delta_w', 'delta_s5_log_step_bwd': 'delta_w', 'delta_s5_b_re': 'delta_w', 'delta_s5_b_im': 'delta_w', 'delta_s5_c_re_fwd': 'delta_w', 'delta_s5_c_im_fwd': 'delta_w', 'delta_s5_c_re_bwd': 'delta_w', 'delta_s5_c_im_bwd': 'delta_w', 'delta_s5_d': 'delta_w', 'delta_s5_glu_w': 'delta_w', 'delta_s5_glu_b': 'delta_w', 'delta_s5_norm_w': 'delta_w', 'delta_w_out': 'delta_w', 'delta_norm_ffn_w': 'delta_w', 'delta_ffn_w_up': 'delta_w', 'delta_ffn_conv_w': 'delta_w', 'delta_ffn_conv_b': 'delta_w', 'delta_ffn_w_down': 'delta_w', 'delta_norm_final_w': 'delta_w', 'new_m_norm_mix_w': 'new_m', 'new_m_w_in': 'new_m', 'new_m_ssd_conv_w': 'new_m', 'new_m_ssd_conv_b': 'new_m', 'new_m_ssd_dt_bias_fwd': 'new_m', 'new_m_ssd_dt_bias_bwd': 'new_m', 'new_m_ssd_a_log_fwd': 'new_m', 'new_m_ssd_a_log_bwd': 'new_m', 'new_m_ssd_d': 'new_m', 'new_m_ssd_norm_w': 'new_m', 'new_m_s5_lambda_re_fwd': 'new_m', 'new_m_s5_lambda_im_fwd': 'new_m', 'new_m_s5_log_step_fwd': 'new_m', 'new_m_s5_lambda_re_bwd': 'new_m', 'new_m_s5_lambda_im_bwd': 'new_m', 'new_m_s5_log_step_bwd': 'new_m', 'new_m_s5_b_re': 'new_m', 'new_m_s5_b_im': 'new_m', 'new_m_s5_c_re_fwd': 'new_m', 'new_m_s5_c_im_fwd': 'new_m', 'new_m_s5_c_re_bwd': 'new_m', 'new_m_s5_c_im_bwd': 'new_m', 'new_m_s5_d': 'new_m', 'new_m_s5_glu_w': 'new_m', 'new_m_s5_glu_b': 'new_m', 'new_m_s5_norm_w': 'new_m', 'new_m_w_out': 'new_m', 'new_m_norm_ffn_w': 'new_m', 'new_m_ffn_w_up': 'new_m', 'new_m_ffn_conv_w': 'new_m', 'new_m_ffn_conv_b': 'new_m', 'new_m_ffn_w_down': 'new_m', 'new_m_norm_final_w': 'new_m', 'new_v_norm_mix_w': 'new_v', 'new_v_w_in': 'new_v', 'new_v_ssd_conv_w': 'new_v', 'new_v_ssd_conv_b': 'new_v', 'new_v_ssd_dt_bias_fwd': 'new_v', 'new_v_ssd_dt_bias_bwd': 'new_v', 'new_v_ssd_a_log_fwd': 'new_v', 'new_v_ssd_a_log_bwd': 'new_v', 'new_v_ssd_d': 'new_v', 'new_v_ssd_norm_w': 'new_v', 'new_v_s5_lambda_re_fwd': 'new_v', 'new_v_s5_lambda_im_fwd': 'new_v', 'new_v_s5_log_step_fwd': 'new_v', 'new_v_s5_lambda_re_bwd': 'new_v', 'new_v_s5_lambda_im_bwd': 'new_v', 'new_v_s5_log_step_bwd': 'new_v', 'new_v_s5_b_re': 'new_v', 'new_v_s5_b_im': 'new_v', 'new_v_s5_c_re_fwd': 'new_v', 'new_v_s5_c_im_fwd': 'new_v', 'new_v_s5_c_re_bwd': 'new_v', 'new_v_s5_c_im_bwd': 'new_v', 'new_v_s5_d': 'new_v', 'new_v_s5_glu_w': 'new_v', 'new_v_s5_glu_b': 'new_v', 'new_v_s5_norm_w': 'new_v', 'new_v_w_out': 'new_v', 'new_v_norm_ffn_w': 'new_v', 'new_v_ffn_w_up': 'new_v', 'new_v_ffn_conv_w': 'new_v', 'new_v_ffn_conv_b': 'new_v', 'new_v_ffn_w_down': 'new_v', 'new_v_norm_final_w': 'new_v'}


def _forward(args):
    return _fwd_reference(*[args[k] for k in FWD_PARAMS])


def _output_shape():
    out = _jax.eval_shape(lambda: _forward(_fwd_setup_inputs(0)))
    return out.shape, out.dtype

N_MICROBATCH = 1
ADAM_LR = 0.001
ADAM_B1 = 0.9
ADAM_B2 = 0.999
ADAM_EPS = 1e-08
ADAM_WD = 0.01
ADAM_STEP = 10
PER_EXAMPLE_BATCH_AXIS = {'x': 0, 'loss_target': 0}
SHARED_INPUTS = []
_WEIGHT_DTYPES = {'norm_mix_w': _jnp.float32, 'w_in': _jnp.float32, 'ssd_conv_w': _jnp.float32, 'ssd_conv_b': _jnp.float32, 'ssd_dt_bias_fwd': _jnp.float32, 'ssd_dt_bias_bwd': _jnp.float32, 'ssd_a_log_fwd': _jnp.float32, 'ssd_a_log_bwd': _jnp.float32, 'ssd_d': _jnp.float32, 'ssd_norm_w': _jnp.float32, 's5_lambda_re_fwd': _jnp.float32, 's5_lambda_im_fwd': _jnp.float32, 's5_log_step_fwd': _jnp.float32, 's5_lambda_re_bwd': _jnp.float32, 's5_lambda_im_bwd': _jnp.float32, 's5_log_step_bwd': _jnp.float32, 's5_b_re': _jnp.float32, 's5_b_im': _jnp.float32, 's5_c_re_fwd': _jnp.float32, 's5_c_im_fwd': _jnp.float32, 's5_c_re_bwd': _jnp.float32, 's5_c_im_bwd': _jnp.float32, 's5_d': _jnp.float32, 's5_glu_w': _jnp.float32, 's5_glu_b': _jnp.float32, 's5_norm_w': _jnp.float32, 'w_out': _jnp.float32, 'norm_ffn_w': _jnp.float32, 'ffn_w_up': _jnp.float32, 'ffn_conv_w': _jnp.float32, 'ffn_conv_b': _jnp.float32, 'ffn_w_down': _jnp.float32, 'norm_final_w': _jnp.float32}
MOMENT_SCALE = {'norm_mix_w': 2.606890e-01, 'w_in': 1.405088e-01, 'ssd_conv_w': 1.190555e-01, 'ssd_conv_b': 1.972832e-01, 'ssd_dt_bias_fwd': 2.201917e-01, 'ssd_dt_bias_bwd': 2.282245e-01, 'ssd_a_log_fwd': 5.749111e-01, 'ssd_a_log_bwd': 3.322031e-01, 'ssd_d': 4.713771e-01, 'ssd_norm_w': 1.710856e-01, 's5_lambda_re_fwd': 2.101231e-02, 's5_lambda_im_fwd': 1.822306e-02, 's5_log_step_fwd': 1.888141e+01, 's5_lambda_re_bwd': 1.627335e-02, 's5_lambda_im_bwd': 1.635117e-02, 's5_log_step_bwd': 1.579196e+01, 's5_b_re': 1.563731e-02, 's5_b_im': 1.515701e-02, 's5_c_re_fwd': 2.202777e-02, 's5_c_im_fwd': 2.183160e-02, 's5_c_re_bwd': 2.257965e-02, 's5_c_im_bwd': 2.081209e-02, 's5_d': 3.328659e-01, 's5_glu_w': 1.112014e-01, 's5_glu_b': 3.829436e-01, 's5_norm_w': 1.682629e-01, 'w_out': 1.945669e-01, 'norm_ffn_w': 1.493745e-01, 'ffn_w_up': 6.096743e-02, 'ffn_conv_w': 6.063833e-02, 'ffn_conv_b': 6.064828e-02, 'ffn_w_down': 9.924350e-02, 'norm_final_w': 6.399652e+01}


def _to_microbatches(a, axis):
    t = _jnp.moveaxis(a, axis, 0)
    t = t.reshape((N_MICROBATCH, t.shape[0] // N_MICROBATCH) + t.shape[1:])
    return _jnp.moveaxis(t, 1, axis + 1)


def setup_inputs(seed: int = 0) -> dict:
    inp = _fwd_setup_inputs(seed)
    key = _jax.random.fold_in(_jax.random.key(seed), 7919)
    shape, _ = _output_shape()
    out = dict(inp)
    out["loss_target"] = _jax.random.normal(_jax.random.fold_in(key, 0), shape, _jnp.float32)
    for i, name in enumerate(TWIN_WEIGHTS):
        w = inp[name].astype(_jnp.float32)
        if MOMENT_SCALE is None:
            s = _jnp.sqrt(_jnp.mean(_jnp.square(w)) + 1e-30)
        else:
            s = MOMENT_SCALE[name]
        km, kv = _jax.random.split(_jax.random.fold_in(key, i + 1))
        out[name] = w
        out["m_" + name] = s * _jax.random.normal(km, w.shape, _jnp.float32)
        out["v_" + name] = (s * s) * _jax.random.uniform(kv, w.shape, _jnp.float32, 0.5, 1.5)
    if N_MICROBATCH > 1:
        for name, axis in PER_EXAMPLE_BATCH_AXIS.items():
            out[name] = _to_microbatches(out[name], axis)
    return {'x': out['x'], 'norm_mix_w': out['norm_mix_w'], 'w_in': out['w_in'], 'ssd_conv_w': out['ssd_conv_w'], 'ssd_conv_b': out['ssd_conv_b'], 'ssd_dt_bias_fwd': out['ssd_dt_bias_fwd'], 'ssd_dt_bias_bwd': out['ssd_dt_bias_bwd'], 'ssd_a_log_fwd': out['ssd_a_log_fwd'], 'ssd_a_log_bwd': out['ssd_a_log_bwd'], 'ssd_d': out['ssd_d'], 'ssd_norm_w': out['ssd_norm_w'], 's5_lambda_re_fwd': out['s5_lambda_re_fwd'], 's5_lambda_im_fwd': out['s5_lambda_im_fwd'], 's5_log_step_fwd': out['s5_log_step_fwd'], 's5_lambda_re_bwd': out['s5_lambda_re_bwd'], 's5_lambda_im_bwd': out['s5_lambda_im_bwd'], 's5_log_step_bwd': out['s5_log_step_bwd'], 's5_b_re': out['s5_b_re'], 's5_b_im': out['s5_b_im'], 's5_c_re_fwd': out['s5_c_re_fwd'], 's5_c_im_fwd': out['s5_c_im_fwd'], 's5_c_re_bwd': out['s5_c_re_bwd'], 's5_c_im_bwd': out['s5_c_im_bwd'], 's5_d': out['s5_d'], 's5_glu_w': out['s5_glu_w'], 's5_glu_b': out['s5_glu_b'], 's5_norm_w': out['s5_norm_w'], 'w_out': out['w_out'], 'norm_ffn_w': out['norm_ffn_w'], 'ffn_w_up': out['ffn_w_up'], 'ffn_conv_w': out['ffn_conv_w'], 'ffn_conv_b': out['ffn_conv_b'], 'ffn_w_down': out['ffn_w_down'], 'norm_final_w': out['norm_final_w'], 'loss_target': out['loss_target'], 'm_norm_mix_w': out['m_norm_mix_w'], 'm_w_in': out['m_w_in'], 'm_ssd_conv_w': out['m_ssd_conv_w'], 'm_ssd_conv_b': out['m_ssd_conv_b'], 'm_ssd_dt_bias_fwd': out['m_ssd_dt_bias_fwd'], 'm_ssd_dt_bias_bwd': out['m_ssd_dt_bias_bwd'], 'm_ssd_a_log_fwd': out['m_ssd_a_log_fwd'], 'm_ssd_a_log_bwd': out['m_ssd_a_log_bwd'], 'm_ssd_d': out['m_ssd_d'], 'm_ssd_norm_w': out['m_ssd_norm_w'], 'm_s5_lambda_re_fwd': out['m_s5_lambda_re_fwd'], 'm_s5_lambda_im_fwd': out['m_s5_lambda_im_fwd'], 'm_s5_log_step_fwd': out['m_s5_log_step_fwd'], 'm_s5_lambda_re_bwd': out['m_s5_lambda_re_bwd'], 'm_s5_lambda_im_bwd': out['m_s5_lambda_im_bwd'], 'm_s5_log_step_bwd': out['m_s5_log_step_bwd'], 'm_s5_b_re': out['m_s5_b_re'], 'm_s5_b_im': out['m_s5_b_im'], 'm_s5_c_re_fwd': out['m_s5_c_re_fwd'], 'm_s5_c_im_fwd': out['m_s5_c_im_fwd'], 'm_s5_c_re_bwd': out['m_s5_c_re_bwd'], 'm_s5_c_im_bwd': out['m_s5_c_im_bwd'], 'm_s5_d': out['m_s5_d'], 'm_s5_glu_w': out['m_s5_glu_w'], 'm_s5_glu_b': out['m_s5_glu_b'], 'm_s5_norm_w': out['m_s5_norm_w'], 'm_w_out': out['m_w_out'], 'm_norm_ffn_w': out['m_norm_ffn_w'], 'm_ffn_w_up': out['m_ffn_w_up'], 'm_ffn_conv_w': out['m_ffn_conv_w'], 'm_ffn_conv_b': out['m_ffn_conv_b'], 'm_ffn_w_down': out['m_ffn_w_down'], 'm_norm_final_w': out['m_norm_final_w'], 'v_norm_mix_w': out['v_norm_mix_w'], 'v_w_in': out['v_w_in'], 'v_ssd_conv_w': out['v_ssd_conv_w'], 'v_ssd_conv_b': out['v_ssd_conv_b'], 'v_ssd_dt_bias_fwd': out['v_ssd_dt_bias_fwd'], 'v_ssd_dt_bias_bwd': out['v_ssd_dt_bias_bwd'], 'v_ssd_a_log_fwd': out['v_ssd_a_log_fwd'], 'v_ssd_a_log_bwd': out['v_ssd_a_log_bwd'], 'v_ssd_d': out['v_ssd_d'], 'v_ssd_norm_w': out['v_ssd_norm_w'], 'v_s5_lambda_re_fwd': out['v_s5_lambda_re_fwd'], 'v_s5_lambda_im_fwd': out['v_s5_lambda_im_fwd'], 'v_s5_log_step_fwd': out['v_s5_log_step_fwd'], 'v_s5_lambda_re_bwd': out['v_s5_lambda_re_bwd'], 'v_s5_lambda_im_bwd': out['v_s5_lambda_im_bwd'], 'v_s5_log_step_bwd': out['v_s5_log_step_bwd'], 'v_s5_b_re': out['v_s5_b_re'], 'v_s5_b_im': out['v_s5_b_im'], 'v_s5_c_re_fwd': out['v_s5_c_re_fwd'], 'v_s5_c_im_fwd': out['v_s5_c_im_fwd'], 'v_s5_c_re_bwd': out['v_s5_c_re_bwd'], 'v_s5_c_im_bwd': out['v_s5_c_im_bwd'], 'v_s5_d': out['v_s5_d'], 'v_s5_glu_w': out['v_s5_glu_w'], 'v_s5_glu_b': out['v_s5_glu_b'], 'v_s5_norm_w': out['v_s5_norm_w'], 'v_w_out': out['v_w_out'], 'v_norm_ffn_w': out['v_norm_ffn_w'], 'v_ffn_w_up': out['v_ffn_w_up'], 'v_ffn_conv_w': out['v_ffn_conv_w'], 'v_ffn_conv_b': out['v_ffn_conv_b'], 'v_ffn_w_down': out['v_ffn_w_down'], 'v_norm_final_w': out['v_norm_final_w']}


def _loss(weights, diff, rest, loss_target):
    with _jax.named_scope("forward"):
        args = {**rest, TWIN_DIFF_INPUT: diff, **{k: w.astype(_WEIGHT_DTYPES[k]) for k, w in weights.items()}}
        y = _forward(args)
    with _jax.named_scope("loss_head"):
        err = _jnp.square(y.astype(_jnp.float32) - loss_target)
        return 0.5 * _jnp.sum(_jnp.mean(err, axis=-1)) if err.ndim else 0.5 * err


def _adamw(w, g, m, v):
    m = ADAM_B1 * m + (1.0 - ADAM_B1) * g
    v = ADAM_B2 * v + (1.0 - ADAM_B2) * _jnp.square(g)
    m_hat = m / (1.0 - ADAM_B1 ** ADAM_STEP)
    v_hat = v / (1.0 - ADAM_B2 ** ADAM_STEP)
    delta = -ADAM_LR * (m_hat / (_jnp.sqrt(v_hat) + ADAM_EPS) + ADAM_WD * w)
    return delta, m, v


def reference(x, norm_mix_w, w_in, ssd_conv_w, ssd_conv_b, ssd_dt_bias_fwd, ssd_dt_bias_bwd, ssd_a_log_fwd, ssd_a_log_bwd, ssd_d, ssd_norm_w, s5_lambda_re_fwd, s5_lambda_im_fwd, s5_log_step_fwd, s5_lambda_re_bwd, s5_lambda_im_bwd, s5_log_step_bwd, s5_b_re, s5_b_im, s5_c_re_fwd, s5_c_im_fwd, s5_c_re_bwd, s5_c_im_bwd, s5_d, s5_glu_w, s5_glu_b, s5_norm_w, w_out, norm_ffn_w, ffn_w_up, ffn_conv_w, ffn_conv_b, ffn_w_down, norm_final_w, loss_target, m_norm_mix_w, m_w_in, m_ssd_conv_w, m_ssd_conv_b, m_ssd_dt_bias_fwd, m_ssd_dt_bias_bwd, m_ssd_a_log_fwd, m_ssd_a_log_bwd, m_ssd_d, m_ssd_norm_w, m_s5_lambda_re_fwd, m_s5_lambda_im_fwd, m_s5_log_step_fwd, m_s5_lambda_re_bwd, m_s5_lambda_im_bwd, m_s5_log_step_bwd, m_s5_b_re, m_s5_b_im, m_s5_c_re_fwd, m_s5_c_im_fwd, m_s5_c_re_bwd, m_s5_c_im_bwd, m_s5_d, m_s5_glu_w, m_s5_glu_b, m_s5_norm_w, m_w_out, m_norm_ffn_w, m_ffn_w_up, m_ffn_conv_w, m_ffn_conv_b, m_ffn_w_down, m_norm_final_w, v_norm_mix_w, v_w_in, v_ssd_conv_w, v_ssd_conv_b, v_ssd_dt_bias_fwd, v_ssd_dt_bias_bwd, v_ssd_a_log_fwd, v_ssd_a_log_bwd, v_ssd_d, v_ssd_norm_w, v_s5_lambda_re_fwd, v_s5_lambda_im_fwd, v_s5_log_step_fwd, v_s5_lambda_re_bwd, v_s5_lambda_im_bwd, v_s5_log_step_bwd, v_s5_b_re, v_s5_b_im, v_s5_c_re_fwd, v_s5_c_im_fwd, v_s5_c_re_bwd, v_s5_c_im_bwd, v_s5_d, v_s5_glu_w, v_s5_glu_b, v_s5_norm_w, v_w_out, v_norm_ffn_w, v_ffn_w_up, v_ffn_conv_w, v_ffn_conv_b, v_ffn_w_down, v_norm_final_w):
    given = dict(x=x, norm_mix_w=norm_mix_w, w_in=w_in, ssd_conv_w=ssd_conv_w, ssd_conv_b=ssd_conv_b, ssd_dt_bias_fwd=ssd_dt_bias_fwd, ssd_dt_bias_bwd=ssd_dt_bias_bwd, ssd_a_log_fwd=ssd_a_log_fwd, ssd_a_log_bwd=ssd_a_log_bwd, ssd_d=ssd_d, ssd_norm_w=ssd_norm_w, s5_lambda_re_fwd=s5_lambda_re_fwd, s5_lambda_im_fwd=s5_lambda_im_fwd, s5_log_step_fwd=s5_log_step_fwd, s5_lambda_re_bwd=s5_lambda_re_bwd, s5_lambda_im_bwd=s5_lambda_im_bwd, s5_log_step_bwd=s5_log_step_bwd, s5_b_re=s5_b_re, s5_b_im=s5_b_im, s5_c_re_fwd=s5_c_re_fwd, s5_c_im_fwd=s5_c_im_fwd, s5_c_re_bwd=s5_c_re_bwd, s5_c_im_bwd=s5_c_im_bwd, s5_d=s5_d, s5_glu_w=s5_glu_w, s5_glu_b=s5_glu_b, s5_norm_w=s5_norm_w, w_out=w_out, norm_ffn_w=norm_ffn_w, ffn_w_up=ffn_w_up, ffn_conv_w=ffn_conv_w, ffn_conv_b=ffn_conv_b, ffn_w_down=ffn_w_down, norm_final_w=norm_final_w, loss_target=loss_target, m_norm_mix_w=m_norm_mix_w, m_w_in=m_w_in, m_ssd_conv_w=m_ssd_conv_w, m_ssd_conv_b=m_ssd_conv_b, m_ssd_dt_bias_fwd=m_ssd_dt_bias_fwd, m_ssd_dt_bias_bwd=m_ssd_dt_bias_bwd, m_ssd_a_log_fwd=m_ssd_a_log_fwd, m_ssd_a_log_bwd=m_ssd_a_log_bwd, m_ssd_d=m_ssd_d, m_ssd_norm_w=m_ssd_norm_w, m_s5_lambda_re_fwd=m_s5_lambda_re_fwd, m_s5_lambda_im_fwd=m_s5_lambda_im_fwd, m_s5_log_step_fwd=m_s5_log_step_fwd, m_s5_lambda_re_bwd=m_s5_lambda_re_bwd, m_s5_lambda_im_bwd=m_s5_lambda_im_bwd, m_s5_log_step_bwd=m_s5_log_step_bwd, m_s5_b_re=m_s5_b_re, m_s5_b_im=m_s5_b_im, m_s5_c_re_fwd=m_s5_c_re_fwd, m_s5_c_im_fwd=m_s5_c_im_fwd, m_s5_c_re_bwd=m_s5_c_re_bwd, m_s5_c_im_bwd=m_s5_c_im_bwd, m_s5_d=m_s5_d, m_s5_glu_w=m_s5_glu_w, m_s5_glu_b=m_s5_glu_b, m_s5_norm_w=m_s5_norm_w, m_w_out=m_w_out, m_norm_ffn_w=m_norm_ffn_w, m_ffn_w_up=m_ffn_w_up, m_ffn_conv_w=m_ffn_conv_w, m_ffn_conv_b=m_ffn_conv_b, m_ffn_w_down=m_ffn_w_down, m_norm_final_w=m_norm_final_w, v_norm_mix_w=v_norm_mix_w, v_w_in=v_w_in, v_ssd_conv_w=v_ssd_conv_w, v_ssd_conv_b=v_ssd_conv_b, v_ssd_dt_bias_fwd=v_ssd_dt_bias_fwd, v_ssd_dt_bias_bwd=v_ssd_dt_bias_bwd, v_ssd_a_log_fwd=v_ssd_a_log_fwd, v_ssd_a_log_bwd=v_ssd_a_log_bwd, v_ssd_d=v_ssd_d, v_ssd_norm_w=v_ssd_norm_w, v_s5_lambda_re_fwd=v_s5_lambda_re_fwd, v_s5_lambda_im_fwd=v_s5_lambda_im_fwd, v_s5_log_step_fwd=v_s5_log_step_fwd, v_s5_lambda_re_bwd=v_s5_lambda_re_bwd, v_s5_lambda_im_bwd=v_s5_lambda_im_bwd, v_s5_log_step_bwd=v_s5_log_step_bwd, v_s5_b_re=v_s5_b_re, v_s5_b_im=v_s5_b_im, v_s5_c_re_fwd=v_s5_c_re_fwd, v_s5_c_im_fwd=v_s5_c_im_fwd, v_s5_c_re_bwd=v_s5_c_re_bwd, v_s5_c_im_bwd=v_s5_c_im_bwd, v_s5_d=v_s5_d, v_s5_glu_w=v_s5_glu_w, v_s5_glu_b=v_s5_glu_b, v_s5_norm_w=v_s5_norm_w, v_w_out=v_w_out, v_norm_ffn_w=v_norm_ffn_w, v_ffn_w_up=v_ffn_w_up, v_ffn_conv_w=v_ffn_conv_w, v_ffn_conv_b=v_ffn_conv_b, v_ffn_w_down=v_ffn_w_down, v_norm_final_w=v_norm_final_w)
    weights = {n: given[n] for n in TWIN_WEIGHTS}
    shared = {n: given[n] for n in SHARED_INPUTS}
    per_example = {n: given[n] for n in ['x']}
    grad_fn = _jax.value_and_grad(_loss, argnums=(0, 1))

    def one_microbatch(ex, loss_target):
        ex = dict(ex)
        diff = ex.pop(TWIN_DIFF_INPUT)
        return grad_fn(weights, diff, {**shared, **ex}, loss_target)

    if N_MICROBATCH == 1:
        loss, (grad_w, grad_x) = one_microbatch(per_example, given["loss_target"])
    else:
        def body(carry, xs):
            loss_sum, grad_sum = carry
            l_k, (gw_k, gx_k) = one_microbatch(xs[0], xs[1])
            with _jax.named_scope("update"):
                return (loss_sum + l_k, _jax.tree.map(_jnp.add, grad_sum, gw_k)), gx_k

        init = (_jnp.zeros((), _jnp.float32), _jax.tree.map(_jnp.zeros_like, weights))
        (loss, grad_w), grad_x = _jax.lax.scan(body, init, (per_example, given["loss_target"]))
    with _jax.named_scope("update"):
        delta_w, new_m, new_v = {}, {}, {}
        for n in TWIN_WEIGHTS:
            delta_w[n], new_m[n], new_v[n] = _adamw(weights[n], grad_w[n], given["m_" + n], given["v_" + n])
    return (loss, grad_x, *[grad_w[n] for n in TWIN_WEIGHTS], *[delta_w[n] for n in TWIN_WEIGHTS],
            *[new_m[n] for n in TWIN_WEIGHTS], *[new_v[n] for n in TWIN_WEIGHTS])
```

```python
import functools

import jax
import jax.numpy as jnp
import numpy as np
from jax import lax
from jax.experimental import pallas as pl
from jax.experimental.pallas import tpu as pltpu

F32, BF16 = jnp.float32, jnp.bfloat16
N_DEV = 8
D_MODEL = 1024
SSD_W, HEADS, HDIM, SGROUPS, HPG, NSTATE, SCONV, QC = 1024, 16, 64, 4, 4, 128, 5, 128
XBC_W = SSD_W + 2 * SGROUPS * NSTATE
S5_W, S5_G, S5_C, S5_P, S5_Q = 512, 32, 16, 64, 32
S5_QC = S5_Q * S5_C
DFF, FCONV = 2816, 3
EPS = 1e-6
ADAM_LR, ADAM_B1, ADAM_B2, ADAM_EPS, ADAM_WD, ADAM_STEP = 0.001, 0.9, 0.999, 1e-08, 0.01, 10
LANES = 128
MESH = pl.DeviceIdType.MESH

WEIGHTS = ['norm_mix_w', 'w_in', 'ssd_conv_w', 'ssd_conv_b', 'ssd_dt_bias_fwd', 'ssd_dt_bias_bwd', 'ssd_a_log_fwd',
           'ssd_a_log_bwd', 'ssd_d', 'ssd_norm_w', 's5_lambda_re_fwd', 's5_lambda_im_fwd', 's5_log_step_fwd',
           's5_lambda_re_bwd', 's5_lambda_im_bwd', 's5_log_step_bwd', 's5_b_re', 's5_b_im', 's5_c_re_fwd', 's5_c_im_fwd',
           's5_c_re_bwd', 's5_c_im_bwd', 's5_d', 's5_glu_w', 's5_glu_b', 's5_norm_w', 'w_out', 'norm_ffn_w', 'ffn_w_up',
           'ffn_conv_w', 'ffn_conv_b', 'ffn_w_down', 'norm_final_w']
SHARDED = {'w_in': 1, 'ssd_conv_w': 1, 'w_out': 0, 'ffn_w_up': 1, 'ffn_conv_w': 1, 'ffn_w_down': 0}
FULL_SHAPE = {'w_in': (1024, 3616), 'ssd_conv_w': (5, 2048), 'w_out': (1536, 1024), 'ffn_w_up': (1024, 5632),
              'ffn_conv_w': (3, 5632), 'ffn_w_down': (2816, 1024)}
PACK_ROWS = 512


def _pick(n, cap=1536):
    if n <= cap:
        return n
    return max(t for t in range(LANES, cap + 1, LANES) if n % t == 0)


def _params(sem):
    return pltpu.CompilerParams(dimension_semantics=sem)


def _bd(a, b, ca, cb):
    return lax.dot_general(a.astype(BF16), b.astype(BF16), (((ca,), (cb,)), ((), ())), preferred_element_type=F32)


@jax.custom_vjp
def dot_nn(a, b):
    return _bd(a, b, 1, 0)


dot_nn.defvjp(lambda a, b: (_bd(a, b, 1, 0), (a, b)),
              lambda r, g: (_bd(g, r[1], 1, 1).astype(r[0].dtype), _bd(r[0], g, 0, 0).astype(r[1].dtype)))


@jax.custom_vjp
def dot_nt(a, b):
    return _bd(a, b, 1, 1)


dot_nt.defvjp(lambda a, b: (_bd(a, b, 1, 1), (a, b)),
              lambda r, g: (_bd(g, r[1], 1, 0).astype(r[0].dtype), _bd(g, r[0], 0, 0).astype(r[1].dtype)))


@jax.custom_vjp
def dot_tn(a, b):
    return _bd(a, b, 0, 0)


dot_tn.defvjp(lambda a, b: (_bd(a, b, 0, 0), (a, b)),
              lambda r, g: (_bd(r[1], g, 1, 1).astype(r[0].dtype), _bd(r[0], g, 1, 0).astype(r[1].dtype)))


def _split3(x):
    hi = x.astype(BF16)
    r = x - hi.astype(F32)
    mid = r.astype(BF16)
    lo = (r - mid.astype(F32)).astype(BF16)
    return hi, mid, lo


def _cum_matrix(q, upper):
    ri = lax.broadcasted_iota(jnp.int32, (q, q), 0)
    ci = lax.broadcasted_iota(jnp.int32, (q, q), 1)
    return jnp.where((ci >= ri) if upper else (ci <= ri), 1.0, 0.0).astype(BF16)


def _exact_left(mat, x):
    return sum(jnp.dot(mat, p, preferred_element_type=F32) for p in _split3(x))


def _exact_right(x, mat):
    return sum(jnp.dot(p, mat, preferred_element_type=F32) for p in _split3(x))


@functools.partial(jax.custom_vjp, nondiff_argnums=(1,))
def cum_col(x, rev):
    return _exact_left(_cum_matrix(x.shape[0], rev), x)


cum_col.defvjp(lambda x, rev: (cum_col(x, rev), None),
               lambda rev, _, g: (_exact_left(_cum_matrix(g.shape[0], not rev), g),))


@functools.partial(jax.custom_vjp, nondiff_argnums=(1,))
def cum_row(x, rev):
    return _exact_right(x, _cum_matrix(x.shape[1], not rev))


cum_row.defvjp(lambda x, rev: (cum_row(x, rev), None),
               lambda rev, _, g: (_exact_right(g, _cum_matrix(g.shape[1], rev)),))


def _softplus(x):
    return jnp.maximum(x, 0.0) + jnp.log(1.0 + jnp.exp(-jnp.abs(x)))


def _silu(x):
    return x * jax.nn.sigmoid(x)


def _gelu(x):
    return 0.5 * x * (1.0 + jnp.tanh(0.7978845608028654 * (x + 0.044715 * (x * x * x))))


def _rms(x, w):
    xf = x.astype(F32)
    return xf * lax.rsqrt(jnp.mean(xf * xf, axis=-1, keepdims=True) + EPS) * w


def matmul_sum(a_list, b_list, *, name, out_dtype=F32, add=None, tm=512):
    m, n = a_list[0].shape[0], b_list[0].shape[1]
    tm, tn, k = min(tm, m), _pick(n), len(a_list)

    def body(*refs):
        acc = None
        for a_ref, b_ref in zip(refs[:k], refs[k:2 * k]):
            p = jnp.dot(a_ref[...].astype(BF16), b_ref[...].astype(BF16), preferred_element_type=F32)
            acc = p if acc is None else acc + p
        if add is not None:
            acc = acc + refs[2 * k][...]
        refs[-1][...] = acc.astype(out_dtype)

    in_specs = [pl.BlockSpec((tm, a.shape[1]), lambda i, j: (i, 0)) for a in a_list]
    in_specs += [pl.BlockSpec((b.shape[0], tn), lambda i, j: (0, j)) for b in b_list]
    args = list(a_list) + list(b_list)
    if add is not None:
        in_specs.append(pl.BlockSpec((tm, tn), lambda i, j: (i, j)))
        args.append(add)
    return pl.pallas_call(
        body, name=name, grid=(m // tm, n // tn), in_specs=in_specs,
        out_specs=pl.BlockSpec((tm, tn), lambda i, j: (i, j)),
        out_shape=jax.ShapeDtypeStruct((m, n), out_dtype),
        compiler_params=_params(("parallel", "parallel")))(*args)


def matmul_tn(a, b, *, name, tm=512):
    m, k = a.shape
    n = b.shape[1]
    tm, tk, tn = min(tm, m), _pick(k), _pick(n)

    def body(a_ref, b_ref, o_ref):
        @pl.when(pl.program_id(2) == 0)
        def _():
            o_ref[...] = jnp.zeros_like(o_ref)

        o_ref[...] += _bd(a_ref[...], b_ref[...], 0, 0)

    return pl.pallas_call(
        body, name=name, grid=(k // tk, n // tn, m // tm),
        in_specs=[pl.BlockSpec((tm, tk), lambda i, j, t: (t, i)), pl.BlockSpec((tm, tn), lambda i, j, t: (t, j))],
        out_specs=pl.BlockSpec((tk, tn), lambda i, j, t: (i, j)),
        out_shape=jax.ShapeDtypeStruct((k, n), F32),
        compiler_params=_params(("parallel", "parallel", "arbitrary")))(a, b)


def _row_spec(r, tm):
    if isinstance(r, tuple):
        arr, width, blk = r
        return arr, pl.BlockSpec((tm, width), lambda i, blk=blk: (i, blk))
    return r, pl.BlockSpec((tm, r.shape[1]), lambda i: (i, 0))


def _full_spec(p):
    return pl.BlockSpec(p.shape, lambda i: (0,) * p.ndim)


def rowmap_fwd(fn, rows, params, outs, *, name, tm=256):
    pairs = [_row_spec(r, tm) for r in rows]
    m = pairs[0][0].shape[0]
    tm = min(tm, m)
    pairs = [_row_spec(r, tm) for r in rows]
    nr, npar = len(rows), len(params)

    def body(*refs):
        res = fn(*[r[...] for r in refs[:nr + npar]])
        for o_ref, v in zip(refs[nr + npar:], res):
            o_ref[...] = v.astype(o_ref.dtype)

    return pl.pallas_call(
        body, name=name, grid=(m // tm,),
        in_specs=[s for _, s in pairs] + [_full_spec(p) for p in params],
        out_specs=[pl.BlockSpec((tm, c), lambda i: (i, 0)) for c, _ in outs],
        out_shape=[jax.ShapeDtypeStruct((m, c), dt) for c, dt in outs],
        compiler_params=_params(("parallel",)))(*[a for a, _ in pairs], *params)


def rowmap_bwd(fn, rows, params, cts, *, name, row_dtypes=None, add=None, tm=256):
    m = _row_spec(rows[0], tm)[0].shape[0]
    tm = min(tm, m)
    rp = [_row_spec(r, tm) for r in rows]
    cp = [_row_spec(c, tm) for c in cts]
    nr, npar, nc = len(rows), len(params), len(cts)
    row_dtypes = row_dtypes or [F32] * nr
    widths = [s.block_shape[1] for _, s in rp]

    def body(*refs):
        ins = [r[...] for r in refs[:nr + npar]]
        ins = [v.astype(F32) for v in ins]
        ct = tuple(r[...].astype(F32) for r in refs[nr + npar:nr + npar + nc])
        base = nr + npar + nc
        extra = None
        if add is not None:
            extra = refs[base][...]
            base += 1
        _, pull = jax.vjp(fn, *ins)
        grads = pull(ct)
        for j in range(nr):
            g = grads[j]
            if j == 0 and extra is not None:
                g = g + extra
            refs[base + j][...] = g.astype(refs[base + j].dtype)

        @pl.when(pl.program_id(0) == 0)
        def _():
            for j in range(npar):
                refs[base + nr + j][...] = jnp.zeros_like(refs[base + nr + j])

        for j in range(npar):
            refs[base + nr + j][...] += grads[nr + j]

    in_specs = [s for _, s in rp] + [_full_spec(p) for p in params] + [s for _, s in cp]
    args = [a for a, _ in rp] + list(params) + [a for a, _ in cp]
    if add is not None:
        in_specs.append(pl.BlockSpec((tm, widths[0]), lambda i: (i, 0)))
        args.append(add)
    out_specs = [pl.BlockSpec((tm, w), lambda i: (i, 0)) for w in widths] + [_full_spec(p) for p in params]
    out_shape = [jax.ShapeDtypeStruct((m, w), dt) for w, dt in zip(widths, row_dtypes)]
    out_shape += [jax.ShapeDtypeStruct(p.shape, F32) for p in params]
    return pl.pallas_call(
        body, name=name, grid=(m // tm,), in_specs=in_specs, out_specs=out_specs, out_shape=out_shape,
        compiler_params=_params(("arbitrary",)))(*args)


def loss_head(h, target, w, *, name, tm=256):
    m, d = h.shape
    tm = min(tm, m)

    def body(h_ref, t_ref, w_ref, loss_ref, dh_ref, dw_ref):
        y, pull = jax.vjp(_rms, h_ref[...], w_ref[...])
        err = y - t_ref[...]
        dh, dw = pull(err * (1.0 / d))

        @pl.when(pl.program_id(0) == 0)
        def _():
            loss_ref[...] = jnp.zeros_like(loss_ref)
            dw_ref[...] = jnp.zeros_like(dw_ref)

        loss_ref[...] += (0.5 / d) * jnp.sum(err * err, keepdims=True)
        dw_ref[...] += dw
        dh_ref[...] = dh

    row = pl.BlockSpec((tm, d), lambda i: (i, 0))
    return pl.pallas_call(
        body, name=name, grid=(m // tm,), in_specs=[row, row, _full_spec(w)],
        out_specs=[pl.BlockSpec((1, 1), lambda i: (0, 0)), row, _full_spec(w)],
        out_shape=[jax.ShapeDtypeStruct((1, 1), F32), jax.ShapeDtypeStruct((m, d), F32),
                   jax.ShapeDtypeStruct(w.shape, F32)],
        compiler_params=_params(("arbitrary",)))(h, target, w)


def _shift(x, s):
    if s == 0:
        return x
    n = x.shape[0]
    t = lax.broadcasted_iota(jnp.int32, x.shape, 0)
    rolled = pltpu.roll(x, (-s) % n, 0)
    return jnp.where((t + s >= 0) & (t + s < n), rolled, 0.0)


def _conv(x, w, b):
    k = w.shape[0]
    acc = b + w[k // 2:k // 2 + 1, :] * x
    for j in range(k):
        if j != k // 2:
            acc = acc + w[j:j + 1, :] * _shift(x, j - k // 2)
    return acc


def _conv_bwd(x, dc, w):
    k = w.shape[0]
    dx = None
    dws = []
    for j in range(k):
        s = j - k // 2
        term = w[j:j + 1, :] * _shift(dc, -s)
        dx = term if dx is None else dx + term
        dws.append(jnp.sum(dc * _shift(x, s), axis=0, keepdims=True))
    return dx, jnp.concatenate(dws, axis=0), jnp.sum(dc, axis=0, keepdims=True)


def _dsilu(c):
    s = jax.nn.sigmoid(c)
    return s * (1.0 + c * (1.0 - s))


def ssd_conv_fwd(xbc, w, b, *, bsz, name):
    t, c = xbc.shape
    seq, ct = t // bsz, 256

    def body(x_ref, w_ref, b_ref, o_ref):
        o_ref[...] = _silu(_conv(x_ref[...], w_ref[...], b_ref[...]))

    return pl.pallas_call(
        body, name=name, grid=(c // ct, bsz),
        in_specs=[pl.BlockSpec((seq, ct), lambda j, i: (i, j)), pl.BlockSpec((w.shape[0], ct), lambda j, i: (0, j)),
                  pl.BlockSpec((1, ct), lambda j, i: (0, j))],
        out_specs=pl.BlockSpec((seq, ct), lambda j, i: (i, j)),
        out_shape=jax.ShapeDtypeStruct((t, c), F32),
        compiler_params=_params(("parallel", "parallel")))(xbc, w, b)


def ssd_conv_bwd(xbc, dact, w, b, *, bsz, name):
    t, c = xbc.shape
    seq, ct, k = t // bsz, 256, w.shape[0]

    def body(x_ref, g_ref, w_ref, b_ref, dx_ref, dw_ref, db_ref):
        x, wv = x_ref[...], w_ref[...]
        dc = g_ref[...] * _dsilu(_conv(x, wv, b_ref[...]))
        dx, dw, db = _conv_bwd(x, dc, wv)
        dx_ref[...] = dx

        @pl.when(pl.program_id(1) == 0)
        def _():
            dw_ref[...] = jnp.zeros_like(dw_ref)
            db_ref[...] = jnp.zeros_like(db_ref)

        dw_ref[...] += dw
        db_ref[...] += db

    blk = pl.BlockSpec((seq, ct), lambda j, i: (i, j))
    wspec, bspec = pl.BlockSpec((k, ct), lambda j, i: (0, j)), pl.BlockSpec((1, ct), lambda j, i: (0, j))
    return pl.pallas_call(
        body, name=name, grid=(c // ct, bsz), in_specs=[blk, blk, wspec, bspec], out_specs=[blk, wspec, bspec],
        out_shape=[jax.ShapeDtypeStruct((t, c), F32), jax.ShapeDtypeStruct((k, c), F32),
                   jax.ShapeDtypeStruct((1, c), F32)],
        compiler_params=_params(("parallel", "arbitrary")))(xbc, dact, w, b)


def _ffn_specs(seq, ct, k, nblk):
    val = pl.BlockSpec((seq, ct), lambda j, i: (i, j))
    gate = pl.BlockSpec((seq, ct), lambda j, i: (i, nblk + j))
    wv, wg = pl.BlockSpec((k, ct), lambda j, i: (0, j)), pl.BlockSpec((k, ct), lambda j, i: (0, nblk + j))
    bv, bg = pl.BlockSpec((1, ct), lambda j, i: (0, j)), pl.BlockSpec((1, ct), lambda j, i: (0, nblk + j))
    return val, gate, wv, wg, bv, bg


def ffn_act_fwd(up, w, b, *, bsz, name):
    t = up.shape[0]
    half = up.shape[1] // 2
    seq, ct, k = t // bsz, 256, w.shape[0]
    val, gate, wv, wg, bv, bg = _ffn_specs(seq, ct, k, half // ct)

    def body(v_ref, g_ref, wv_ref, wg_ref, bv_ref, bg_ref, o_ref):
        vc = _conv(v_ref[...], wv_ref[...], bv_ref[...])
        gc = _conv(g_ref[...], wg_ref[...], bg_ref[...])
        o_ref[...] = (_silu(gc) * vc).astype(BF16)

    return pl.pallas_call(
        body, name=name, grid=(half // ct, bsz), in_specs=[val, gate, wv, wg, bv, bg], out_specs=val,
        out_shape=jax.ShapeDtypeStruct((t, half), BF16),
        compiler_params=_params(("parallel", "parallel")))(up, up, w, w, b, b)


def ffn_act_bwd(up, dact, w, b, *, bsz, name):
    t = up.shape[0]
    half = up.shape[1] // 2
    seq, ct, k = t // bsz, 256, w.shape[0]
    val, gate, wv, wg, bv, bg = _ffn_specs(seq, ct, k, half // ct)

    def body(v_ref, g_ref, wv_ref, wg_ref, bv_ref, bg_ref, d_ref, dv_ref, dg_ref, dwv_ref, dwg_ref, dbv_ref, dbg_ref):
        v, g = v_ref[...], g_ref[...]
        vc = _conv(v, wv_ref[...], bv_ref[...])
        gc = _conv(g, wg_ref[...], bg_ref[...])
        d = d_ref[...].astype(F32)
        dv, dwv, dbv = _conv_bwd(v, d * _silu(gc), wv_ref[...])
        dg, dwg, dbg = _conv_bwd(g, d * vc * _dsilu(gc), wg_ref[...])
        dv_ref[...] = dv.astype(BF16)
        dg_ref[...] = dg.astype(BF16)

        @pl.when(pl.program_id(1) == 0)
        def _():
            for r in (dwv_ref, dwg_ref, dbv_ref, dbg_ref):
                r[...] = jnp.zeros_like(r)

        dwv_ref[...] += dwv
        dwg_ref[...] += dwg
        dbv_ref[...] += dbv
        dbg_ref[...] += dbg

    return pl.pallas_call(
        body, name=name, grid=(half // ct, bsz), in_specs=[val, gate, wv, wg, bv, bg, val],
        out_specs=[val, val, wv, wv, bv, bv],
        out_shape=[jax.ShapeDtypeStruct((t, half), BF16), jax.ShapeDtypeStruct((t, half), BF16),
                   jax.ShapeDtypeStruct((k, half), F32), jax.ShapeDtypeStruct((k, half), F32),
                   jax.ShapeDtypeStruct((1, half), F32), jax.ShapeDtypeStruct((1, half), F32)],
        compiler_params=_params(("parallel", "arbitrary")))(up, up, w, w, b, b, dact)


def _sel_col(a, h):
    oh = (lax.broadcasted_iota(jnp.int32, (1, a.shape[1]), 1) == h).astype(F32)
    return jnp.sum(a * oh, axis=1, keepdims=True)


def _sel_row(a, h):
    oh = (lax.broadcasted_iota(jnp.int32, (a.shape[0], 1), 0) == h).astype(F32)
    return jnp.sum(a * oh, axis=0, keepdims=True)


def _ssd_chunk(xh, dtc, dtr, bm, cm, prev, bias_c, alog_c, dskip_c, bias_r, alog_r, rev):
    q = dtc.shape[0]
    ri = lax.broadcasted_iota(jnp.int32, (q, q), 0)
    ci = lax.broadcasted_iota(jnp.int32, (q, q), 1)
    mask = (ci >= ri) if rev else (ci <= ri)
    dt_c = _softplus(dtc + bias_c)
    dta_c = dt_c * (-jnp.exp(alog_c))
    cs_c = cum_col(dta_c, rev)
    dta_r = _softplus(dtr + bias_r) * (-jnp.exp(alog_r))
    cs_r = cum_row(dta_r, rev)
    scores = dot_nt(cm, bm)
    ys, news = [], []
    for h in range(HPG):
        hh = h + (HPG if rev else 0)
        csq = _sel_col(cs_c, hh)
        css = _sel_row(cs_r, hh)
        seg = jnp.exp(jnp.where(mask, csq - css, -1e30))
        xdt = xh[h] * _sel_col(dt_c, hh)
        y = dot_nn(scores * seg, xdt)
        tot = jnp.sum(_sel_col(dta_c, hh), axis=0, keepdims=True)
        y = y + dot_nt(cm, prev[h]) * jnp.exp(csq)
        if not rev:
            y = y + _sel_col(dskip_c, hh) * xh[h]
        ys.append(y)
        news.append(jnp.exp(tot) * prev[h] + dot_tn(xdt * jnp.exp(tot - csq), bm))
    return tuple(ys), tuple(news)


def _ssd_specs(seq):
    xs = pl.BlockSpec((None, seq, HPG * HDIM), lambda b, g: (b, 0, g))
    bm = pl.BlockSpec((None, seq, NSTATE), lambda b, g: (b, 0, SSD_W // NSTATE + g))
    cm = pl.BlockSpec((None, seq, NSTATE), lambda b, g: (b, 0, SSD_W // NSTATE + SGROUPS + g))
    dtc = pl.BlockSpec((None, None, seq, 2 * HPG), lambda b, g: (b, g, 0, 0))
    dtr = pl.BlockSpec((None, None, 2 * HPG, seq), lambda b, g: (b, g, 0, 0))
    pc = pl.BlockSpec((None, 1, 2 * HPG), lambda b, g: (g, 0, 0))
    pr = pl.BlockSpec((None, 2 * HPG, 1), lambda b, g: (g, 0, 0))
    return xs, bm, cm, dtc, dtr, pc, pr


def _head_cols(h):
    return slice(HDIM * h, HDIM * (h + 1))


def ssd_scan_fwd(act, dtc, dtr, pcs, prs, *, name):
    bsz, seq, _ = act.shape
    nc = seq // QC
    xs, bm, cm, dtcs, dtrs, pc, pr = _ssd_specs(seq)

    def body(x_ref, b_ref, c_ref, dtc_ref, dtr_ref, bc_ref, ac_ref, dk_ref, br_ref, ar_ref, y_ref):
        par = (bc_ref[...], ac_ref[...], dk_ref[...], br_ref[...], ar_ref[...])
        for rev in (False, True):
            def step(i, carry, rev=rev):
                rows = pl.ds(pl.multiple_of(((nc - 1 - i) if rev else i) * QC, QC), QC)
                xh = tuple(x_ref[rows, _head_cols(h)] for h in range(HPG))
                ys, new = _ssd_chunk(xh, dtc_ref[rows, :], dtr_ref[:, rows], b_ref[rows, :], c_ref[rows, :],
                                     carry, *par, rev)
                for h in range(HPG):
                    if rev:
                        y_ref[rows, _head_cols(h)] += ys[h]
                    else:
                        y_ref[rows, _head_cols(h)] = ys[h]
                return new

            lax.fori_loop(0, nc, step, tuple(jnp.zeros((HDIM, NSTATE), F32) for _ in range(HPG)))

    return pl.pallas_call(
        body, name=name, grid=(bsz, SGROUPS), in_specs=[xs, bm, cm, dtcs, dtrs, pc, pc, pc, pr, pr], out_specs=xs,
        out_shape=jax.ShapeDtypeStruct((bsz, seq, SSD_W), F32),
        compiler_params=_params(("parallel", "parallel")))(act, act, act, dtc, dtr, *pcs, *prs)


def ssd_scan_bwd(act, dtc, dtr, pcs, prs, dy, *, name):
    bsz, seq, _ = act.shape
    nc = seq // QC
    xs, bm, cm, dtcs, dtrs, pc, pr = _ssd_specs(seq)
    grp = pl.BlockSpec((None, seq, NSTATE), lambda b, g: (b, 0, g))
    dpc = pl.BlockSpec((None, None, 1, 2 * HPG), lambda b, g: (b, g, 0, 0))
    dpr = pl.BlockSpec((None, None, 2 * HPG, 1), lambda b, g: (b, g, 0, 0))

    def body(x_ref, b_ref, c_ref, dtc_ref, dtr_ref, bc_ref, ac_ref, dk_ref, br_ref, ar_ref, dy_ref,
             dx_ref, db_ref, dc_ref, ddtc_ref, ddtr_ref, gbc_ref, gac_ref, gdk_ref, gbr_ref, gar_ref, st_ref):
        par = (bc_ref[...], ac_ref[...], dk_ref[...], br_ref[...], ar_ref[...])
        pgrads = (gbc_ref, gac_ref, gdk_ref, gbr_ref, gar_ref)
        for r in pgrads:
            r[...] = jnp.zeros_like(r)
        zero = tuple(jnp.zeros((HDIM, NSTATE), F32) for _ in range(HPG))
        for rev in (False, True):
            def load(k):
                rows = pl.ds(pl.multiple_of(k * QC, QC), QC)
                xh = tuple(x_ref[rows, _head_cols(h)] for h in range(HPG))
                return rows, xh, dtc_ref[rows, :], dtr_ref[:, rows], b_ref[rows, :], c_ref[rows, :]

            def fstep(i, carry, rev=rev):
                k = (nc - 1 - i) if rev else i
                _, xh, a, b, c, d = load(k)
                for h in range(HPG):
                    st_ref[k, h] = carry[h]
                return _ssd_chunk(xh, a, b, c, d, carry, *par, rev)[1]

            lax.fori_loop(0, nc, fstep, zero)

            def bstep(i, dcarry, rev=rev):
                k = i if rev else (nc - 1 - i)
                rows, xh, a, b, c, d = load(k)
                prev = tuple(st_ref[k, h] for h in range(HPG))
                _, pull = jax.vjp(functools.partial(_ssd_chunk, rev=rev), xh, a, b, c, d, prev, *par)
                dyh = tuple(dy_ref[rows, _head_cols(h)] for h in range(HPG))
                gx, ga, gb, gc, gd, gprev, *gpar = pull((dyh, dcarry))
                for h in range(HPG):
                    if rev:
                        dx_ref[rows, _head_cols(h)] += gx[h]
                    else:
                        dx_ref[rows, _head_cols(h)] = gx[h]
                if rev:
                    ddtc_ref[rows, :] += ga
                    ddtr_ref[:, rows] += gb
                    db_ref[rows, :] += gc
                    dc_ref[rows, :] += gd
                else:
                    ddtc_ref[rows, :] = ga
                    ddtr_ref[:, rows] = gb
                    db_ref[rows, :] = gc
                    dc_ref[rows, :] = gd
                for r, g in zip(pgrads, gpar):
                    r[...] += g
                return gprev

            lax.fori_loop(0, nc, bstep, zero)

    out_shape = [jax.ShapeDtypeStruct((bsz, seq, SSD_W), F32),
                 jax.ShapeDtypeStruct((bsz, seq, SGROUPS * NSTATE), F32),
                 jax.ShapeDtypeStruct((bsz, seq, SGROUPS * NSTATE), F32),
                 jax.ShapeDtypeStruct(dtc.shape, F32), jax.ShapeDtypeStruct(dtr.shape, F32)]
    out_shape += [jax.ShapeDtypeStruct((bsz, SGROUPS, 1, 2 * HPG), F32)] * 3
    out_shape += [jax.ShapeDtypeStruct((bsz, SGROUPS, 2 * HPG, 1), F32)] * 2
    return pl.pallas_call(
        body, name=name, grid=(bsz, SGROUPS), in_specs=[xs, bm, cm, dtcs, dtrs, pc, pc, pc, pr, pr, xs],
        out_specs=[xs, grp, grp, dtcs, dtrs, dpc, dpc, dpc, dpr, dpr], out_shape=out_shape,
        scratch_shapes=[pltpu.VMEM((nc, HPG, HDIM, NSTATE), F32)],
        compiler_params=_params(("parallel", "parallel")))(act, act, act, dtc, dtr, *pcs, *prs, dy)


def _s5_direction(lam_re, lam_im, log_step, b_re, b_im, c_re, c_im, rev):
    q = S5_Q
    step = jnp.exp(log_step)[:, None]
    lr, li = lam_re * step, lam_im * step
    mag = jnp.exp(lr)
    ar, ai = mag * jnp.cos(li), mag * jnp.sin(li)
    den = lam_re * lam_re + lam_im * lam_im
    cr = ((ar - 1.0) * lam_re + ai * lam_im) / den
    ci = (ai * lam_re - (ar - 1.0) * lam_im) / den
    bbr = cr[..., None] * b_re - ci[..., None] * b_im
    bbi = cr[..., None] * b_im + ci[..., None] * b_re
    d = jnp.arange(q + 1, dtype=F32)[None, :, None]
    pm = jnp.exp(d * lr[:, None, :])
    pr, pi = pm * jnp.cos(d * li[:, None, :]), pm * jnp.sin(d * li[:, None, :])
    er = pr[..., None] * bbr[:, None] - pi[..., None] * bbi[:, None]
    ei = pr[..., None] * bbi[:, None] + pi[..., None] * bbr[:, None]
    hp = lax.Precision.HIGHEST
    k = (jnp.einsum('gcp,gdpz->gdcz', c_re, er[:, :q], precision=hp)
         - jnp.einsum('gcp,gdpz->gdcz', c_im, ei[:, :q], precision=hp))
    e = jnp.concatenate([er[:, :q], ei[:, :q]], axis=2)
    wt = jnp.transpose(e if rev else e[:, ::-1], (0, 1, 3, 2))
    p1r, p1i = pr[:, 1:], pi[:, 1:]
    if rev:
        p1r, p1i = p1r[:, ::-1], p1i[:, ::-1]
    m_re = c_re[:, None] * p1r[:, :, None, :] - c_im[:, None] * p1i[:, :, None, :]
    m_im = -c_re[:, None] * p1i[:, :, None, :] - c_im[:, None] * p1r[:, :, None, :]
    mt = jnp.transpose(jnp.concatenate([m_re, m_im], axis=-1), (0, 3, 1, 2))
    da = jnp.concatenate([pr[:, q], pr[:, q]], axis=-1)
    db = jnp.concatenate([-pi[:, q], pi[:, q]], axis=-1)
    return k, wt, mt, da, db


def _s5_operators(lf_re, lf_im, lsf, lb_re, lb_im, lsb, b_re, b_im, cf_re, cf_im, cb_re, cb_im):
    q = S5_Q
    kf, wtf, mtf, daf, dbf = _s5_direction(lf_re, lf_im, lsf, b_re, b_im, cf_re, cf_im, False)
    kb, wtb, mtb, dab, dbb = _s5_direction(lb_re, lb_im, lsb, b_re, b_im, cb_re, cb_im, True)
    t = np.arange(q)[:, None, None]
    s = np.arange(q)[None, :, None]
    d = np.arange(q)[None, None, :]
    ohf = jnp.asarray((t - s == d).astype(np.float32))
    ohb = jnp.asarray((s - t == d).astype(np.float32))
    hp = lax.Precision.HIGHEST
    tt = (jnp.einsum('tsd,gdcz->gsztc', ohf, kf, precision=hp) + jnp.einsum('tsd,gdcz->gsztc', ohb, kb, precision=hp))
    g = tt.shape[0]
    tt = tt.reshape(g, S5_QC, S5_QC)
    wt = jnp.concatenate([wtf.reshape(g, S5_QC, 2 * S5_P), wtb.reshape(g, S5_QC, 2 * S5_P)], axis=-1)
    mt = jnp.concatenate([mtf.reshape(g, 2 * S5_P, S5_QC), mtb.reshape(g, 2 * S5_P, S5_QC)], axis=1)
    return tt, wt, mt, jnp.concatenate([daf, dab], -1), jnp.concatenate([dbf, dbb], -1)


def _gspec(*shape):
    return pl.BlockSpec((None,) + shape, lambda g: (g,) + (0,) * len(shape))


def s5_state_in(u, wt, *, name):
    g, r, _ = u.shape

    def body(u_ref, w_ref, o_ref):
        o_ref[...] = _bd(u_ref[...], w_ref[...], 1, 0)

    return pl.pallas_call(
        body, name=name, grid=(g,), in_specs=[_gspec(r, S5_QC), _gspec(S5_QC, 4 * S5_P)],
        out_specs=_gspec(r, 4 * S5_P), out_shape=jax.ShapeDtypeStruct((g, r, 4 * S5_P), F32),
        compiler_params=_params(("parallel",)))(u, wt)


def _swap(h):
    return pltpu.roll(h, S5_P, 1)


def s5_carry_fwd(s, da, db, *, name):
    nck, rows, _ = s.shape
    w = 2 * S5_P

    def body(s_ref, da_ref, db_ref, h_ref):
        for rev, cols in ((False, slice(0, w)), (True, slice(w, 2 * w))):
            a, b = da_ref[:, cols], db_ref[:, cols]

            def step(i, h, rev=rev, cols=cols, a=a, b=b):
                k = (nck - 1 - i) if rev else i
                h_ref[k, :, cols] = h
                return a * h + b * _swap(h) + s_ref[k, :, cols]

            lax.fori_loop(0, nck, step, jnp.zeros((rows, w), F32))

    return pl.pallas_call(body, name=name, out_shape=jax.ShapeDtypeStruct(s.shape, F32))(s, da, db)


def s5_carry_bwd(hin, dh, da, db, *, name):
    nck, rows, _ = hin.shape
    w = 2 * S5_P

    def body(h_ref, dh_ref, da_ref, db_ref, ds_ref, gda_ref, gdb_ref):
        for rev, cols in ((False, slice(0, w)), (True, slice(w, 2 * w))):
            a, b = da_ref[:, cols], db_ref[:, cols]

            def step(i, carry, rev=rev, cols=cols, a=a, b=b):
                g, ga, gb = carry
                k = i if rev else (nck - 1 - i)
                ds_ref[k, :, cols] = g
                h = h_ref[k, :, cols]
                return (dh_ref[k, :, cols] + a * g + _swap(b * g), ga + g * h, gb + g * _swap(h))

            z = jnp.zeros((rows, w), F32)
            _, ga, gb = lax.fori_loop(0, nck, step, (z, z, z))
            gda_ref[:, cols] = ga
            gdb_ref[:, cols] = gb

    return pl.pallas_call(
        body, name=name,
        out_shape=[jax.ShapeDtypeStruct(hin.shape, F32), jax.ShapeDtypeStruct(da.shape, F32),
                   jax.ShapeDtypeStruct(da.shape, F32)])(hin, dh, da, db)


def s5_out(u, hin, tt, mt, *, name):
    g, r, _ = u.shape

    def body(u_ref, h_ref, t_ref, m_ref, o_ref):
        o_ref[...] = _bd(u_ref[...], t_ref[...], 1, 0) + _bd(h_ref[...], m_ref[...], 1, 0)

    return pl.pallas_call(
        body, name=name, grid=(g,),
        in_specs=[_gspec(r, S5_QC), _gspec(r, 4 * S5_P), _gspec(S5_QC, S5_QC), _gspec(4 * S5_P, S5_QC)],
        out_specs=_gspec(r, S5_QC), out_shape=jax.ShapeDtypeStruct((g, r, S5_QC), F32),
        compiler_params=_params(("parallel",)))(u, hin, tt, mt)


def s5_out_bwd(dy, u, hin, tt, mt, *, name):
    g, r, _ = u.shape

    def body(dy_ref, u_ref, h_ref, t_ref, m_ref, dh_ref, dt_ref, dm_ref, du_ref):
        dy_v = dy_ref[...]
        dh_ref[...] = _bd(dy_v, m_ref[...], 1, 1)
        dt_ref[...] = _bd(u_ref[...], dy_v, 0, 0)
        dm_ref[...] = _bd(h_ref[...], dy_v, 0, 0)
        du_ref[...] = _bd(dy_v, t_ref[...], 1, 1)

    return pl.pallas_call(
        body, name=name, grid=(g,),
        in_specs=[_gspec(r, S5_QC), _gspec(r, S5_QC), _gspec(r, 4 * S5_P), _gspec(S5_QC, S5_QC),
                  _gspec(4 * S5_P, S5_QC)],
        out_specs=[_gspec(r, 4 * S5_P), _gspec(S5_QC, S5_QC), _gspec(4 * S5_P, S5_QC), _gspec(r, S5_QC)],
        out_shape=[jax.ShapeDtypeStruct((g, r, 4 * S5_P), F32), jax.ShapeDtypeStruct((g, S5_QC, S5_QC), F32),
                   jax.ShapeDtypeStruct((g, 4 * S5_P, S5_QC), F32), jax.ShapeDtypeStruct((g, r, S5_QC), F32)],
        compiler_params=_params(("parallel",)))(dy, u, hin, tt, mt)


def s5_state_in_bwd(ds, u, wt, du1, *, name):
    g, r, _ = u.shape

    def body(ds_ref, u_ref, w_ref, du1_ref, du_ref, dw_ref):
        ds_v = ds_ref[...]
        du_ref[...] = du1_ref[...] + _bd(ds_v, w_ref[...], 1, 1)
        dw_ref[...] = _bd(u_ref[...], ds_v, 0, 0)

    return pl.pallas_call(
        body, name=name, grid=(g,),
        in_specs=[_gspec(r, 4 * S5_P), _gspec(r, S5_QC), _gspec(S5_QC, 4 * S5_P), _gspec(r, S5_QC)],
        out_specs=[_gspec(r, S5_QC), _gspec(S5_QC, 4 * S5_P)],
        out_shape=[jax.ShapeDtypeStruct((g, r, S5_QC), F32), jax.ShapeDtypeStruct((g, S5_QC, 4 * S5_P), F32)],
        compiler_params=_params(("parallel",)))(ds, u, wt, du1)


def _s5_post(ypre, u, dvec, wv, wg, bv, bg, nw):
    g = _gelu(ypre + dvec * u)
    out = (dot_nn(g, wv) + bv) * jax.nn.sigmoid(dot_nn(g, wg) + bg)
    return (_rms(out, nw),)


def _ssd_post(y, z, nw):
    return (_rms(y * _silu(z), nw),)


def _to_chunks(u, bsz):
    nck = u.shape[0] // bsz // S5_Q
    v = u.reshape(bsz, nck, S5_Q, S5_G, S5_C)
    return jnp.transpose(v, (3, 0, 1, 2, 4)).reshape(S5_G, bsz * nck, S5_QC)


def _from_chunks(y, bsz):
    nck = y.shape[1] // bsz
    v = y.reshape(S5_G, bsz, nck, S5_Q, S5_C)
    return jnp.transpose(v, (1, 2, 3, 0, 4)).reshape(bsz * nck * S5_Q, S5_W)


def _to_carry(s, bsz):
    nck = s.shape[1] // bsz
    return jnp.transpose(s.reshape(S5_G, bsz, nck, -1), (2, 0, 1, 3)).reshape(nck, S5_G * bsz, -1)


def _from_carry(h, bsz):
    nck = h.shape[0]
    return jnp.transpose(h.reshape(nck, S5_G, bsz, -1), (1, 2, 0, 3)).reshape(S5_G, bsz * nck, -1)


def _block_diag(w):
    eye = jnp.eye(S5_G, dtype=w.dtype)
    return jnp.einsum('gcd,gh->gchd', w, eye).reshape(S5_W, S5_W)


def _diag_blocks(w):
    v = w.reshape(S5_G, S5_C, S5_G, S5_C)
    return v[jnp.arange(S5_G), :, jnp.arange(S5_G), :]


def _dt_layouts(dt, bsz):
    seq = dt.shape[0] // bsz
    v = jnp.transpose(dt.reshape(bsz, seq, 2, SGROUPS, HPG), (0, 3, 1, 2, 4)).reshape(bsz, SGROUPS, seq, 2 * HPG)
    return v, jnp.transpose(v, (0, 1, 3, 2))


def _dt_from_layouts(dc, dr):
    bsz, _, seq, _ = dc.shape
    v = (dc + jnp.transpose(dr, (0, 1, 3, 2))).reshape(bsz, SGROUPS, seq, 2, HPG)
    return jnp.transpose(v, (0, 2, 3, 1, 4)).reshape(bsz * seq, 2 * HEADS)


def _head_params(f, b):
    v = jnp.concatenate([f.reshape(SGROUPS, HPG), b.reshape(SGROUPS, HPG)], axis=1)
    return v[:, None, :], v[:, :, None]


def _head_grads(gc, gr=None):
    v = gc.sum(0)[:, 0, :]
    if gr is not None:
        v = v + gr.sum(0)[:, :, 0]
    return v[:, :HPG].reshape(HEADS), v[:, HPG:].reshape(HEADS)


def local_step(x, target, w):
    bsz, seq, d = x.shape
    t = bsz * seq
    x2, tgt2 = x.reshape(t, d), target.reshape(t, d)
    g = {}
    row = lambda v: v.reshape(1, -1)
    bf = lambda v: v.astype(BF16)

    w_in = bf(w['w_in'])
    cuts = [0, SSD_W, SSD_W + XBC_W, SSD_W + XBC_W + 2 * HEADS, w_in.shape[1]]
    w_in_parts = [w_in[:, a:b] for a, b in zip(cuts[:-1], cuts[1:])]
    norm_mix = row(w['norm_mix_w'])
    (hn,) = rowmap_fwd(lambda a, nw: (_rms(a, nw),), [x2], [norm_mix], [(d, BF16)], name="rms_mix")
    z, xbc, dt, u = [matmul_sum([hn], [p], name=f"in_proj_{i}") for i, p in enumerate(w_in_parts)]

    conv_w, conv_b = w['ssd_conv_w'], row(w['ssd_conv_b'])
    act = ssd_conv_fwd(xbc, conv_w, conv_b, bsz=bsz, name="ssd_conv")
    dtc, dtr = _dt_layouts(dt, bsz)
    bias_c, bias_r = _head_params(w['ssd_dt_bias_fwd'], w['ssd_dt_bias_bwd'])
    alog_c, alog_r = _head_params(w['ssd_a_log_fwd'], w['ssd_a_log_bwd'])
    dskip_c, _ = _head_params(w['ssd_d'], jnp.zeros_like(w['ssd_d']))
    pcs, prs = (bias_c, alog_c, dskip_c), (bias_r, alog_r)
    act3 = act.reshape(bsz, seq, XBC_W)
    y_scan = ssd_scan_fwd(act3, dtc, dtr, pcs, prs, name="ssd_scan").reshape(t, SSD_W)
    ssd_nw = row(w['ssd_norm_w'])
    (y_ssd,) = rowmap_fwd(_ssd_post, [y_scan, z], [ssd_nw], [(SSD_W, BF16)], name="ssd_post")

    s5_names = ['s5_lambda_re_fwd', 's5_lambda_im_fwd', 's5_log_step_fwd', 's5_lambda_re_bwd', 's5_lambda_im_bwd',
                's5_log_step_bwd', 's5_b_re', 's5_b_im', 's5_c_re_fwd', 's5_c_im_fwd', 's5_c_re_bwd', 's5_c_im_bwd']
    (tt, wt, mt, da, db), s5_pull = jax.vjp(_s5_operators, *[w[n] for n in s5_names])
    tt_b, wt_b, mt_b = bf(tt), bf(wt), bf(mt)
    da_r, db_r = jnp.repeat(da, bsz, axis=0), jnp.repeat(db, bsz, axis=0)
    uc = _to_chunks(u, bsz)
    s_in = _to_carry(s5_state_in(uc, wt_b, name="s5_state_in"), bsz)
    hin_c = s5_carry_fwd(s_in, da_r, db_r, name="s5_carry")
    hin = _from_carry(hin_c, bsz)
    ypre = _from_chunks(s5_out(uc, hin, tt_b, mt_b, name="s5_out"), bsz)
    glu_w = w['s5_glu_w']
    s5_par = [row(w['s5_d']), _block_diag(glu_w[:, :, :S5_C]), _block_diag(glu_w[:, :, S5_C:]),
              row(w['s5_glu_b'][:, :S5_C]), row(w['s5_glu_b'][:, S5_C:]), row(w['s5_norm_w'])]
    (y_s5,) = rowmap_fwd(_s5_post, [ypre, u], s5_par, [(S5_W, BF16)], name="s5_post")

    w_out = bf(w['w_out'])
    h1 = matmul_sum([y_ssd, y_s5], [w_out[:SSD_W], w_out[SSD_W:]], add=x2, name="out_proj")
    norm_ffn = row(w['norm_ffn_w'])
    (hn2,) = rowmap_fwd(lambda a, nw: (_rms(a, nw),), [h1], [norm_ffn], [(d, BF16)], name="rms_ffn")
    w_up, w_down = bf(w['ffn_w_up']), bf(w['ffn_w_down'])
    up = matmul_sum([hn2], [w_up], name="ffn_up")
    fconv_w, fconv_b = w['ffn_conv_w'], row(w['ffn_conv_b'])
    fact = ffn_act_fwd(up, fconv_w, fconv_b, bsz=bsz, name="ffn_act")
    h2 = matmul_sum([fact], [w_down], add=h1, name="ffn_down")
    loss, dh2, g_nf = loss_head(h2, tgt2, row(w['norm_final_w']), name="loss_head")
    g['norm_final_w'] = g_nf.reshape(-1)

    dfact = matmul_sum([dh2], [w_down.T], name="ffn_down_dx")
    g['ffn_w_down'] = matmul_tn(fact, dh2, name="ffn_down_dw")
    dval, dgate, dwv, dwg, dbv, dbg = ffn_act_bwd(up, dfact, fconv_w, fconv_b, bsz=bsz, name="ffn_act_bwd")
    g['ffn_conv_w'] = jnp.concatenate([dwv, dwg], axis=1)
    g['ffn_conv_b'] = jnp.concatenate([dbv, dbg], axis=1).reshape(-1)
    w_up_t = w_up.T
    dhn2 = matmul_sum([dval, dgate], [w_up_t[:DFF], w_up_t[DFF:]], name="ffn_up_dx")
    g['ffn_w_up'] = jnp.concatenate([matmul_tn(hn2, dval, name="ffn_up_dw_val"),
                                     matmul_tn(hn2, dgate, name="ffn_up_dw_gate")], axis=1)
    dh1, g_nffn = rowmap_bwd(lambda a, nw: (_rms(a, nw),), [h1], [norm_ffn], [dhn2], add=dh2, name="rms_ffn_bwd")
    g['norm_ffn_w'] = g_nffn.reshape(-1)

    w_out_t = w_out.T
    dycat = matmul_sum([dh1], [w_out_t], name="out_proj_dx")
    g['w_out'] = jnp.concatenate([matmul_tn(y_ssd, dh1, name="out_proj_dw_ssd"),
                                  matmul_tn(y_s5, dh1, name="out_proj_dw_s5")], axis=0)
    dy_scan, dz, g_snw = rowmap_bwd(_ssd_post, [y_scan, z], [ssd_nw], [(dycat, SSD_W, 0)], name="ssd_post_bwd")
    g['ssd_norm_w'] = g_snw.reshape(-1)
    dypre, du_a, g_d, g_wv, g_wg, g_bv, g_bg, g_s5nw = rowmap_bwd(
        _s5_post, [ypre, u], s5_par, [(dycat, S5_W, SSD_W // S5_W)], name="s5_post_bwd")
    g['s5_d'], g['s5_norm_w'] = g_d.reshape(-1), g_s5nw.reshape(-1)
    g['s5_glu_w'] = jnp.concatenate([_diag_blocks(g_wv), _diag_blocks(g_wg)], axis=-1)
    g['s5_glu_b'] = jnp.concatenate([g_bv.reshape(S5_G, S5_C), g_bg.reshape(S5_G, S5_C)], axis=-1)

    dyc = _to_chunks(dypre, bsz)
    dhin, dtt, dmt, du1 = s5_out_bwd(dyc, uc, hin, tt_b, mt_b, name="s5_out_bwd")
    ds_c, gda, gdb = s5_carry_bwd(hin_c, _to_carry(dhin, bsz), da_r, db_r, name="s5_carry_bwd")
    duc, dwt = s5_state_in_bwd(_from_carry(ds_c, bsz), uc, wt_b, du1, name="s5_state_in_bwd")
    du = du_a + _from_chunks(duc, bsz)
    fold = lambda v: v.reshape(S5_G, bsz, -1).sum(1)
    for n, gv in zip(s5_names, s5_pull((dtt, dwt, dmt, fold(gda), fold(gdb)))):
        g[n] = gv

    dxs, dbm, dcm, ddtc, ddtr, gbc, gac, gdk, gbr, gar = ssd_scan_bwd(
        act3, dtc, dtr, pcs, prs, dy_scan.reshape(bsz, seq, SSD_W), name="ssd_scan_bwd")
    g['ssd_dt_bias_fwd'], g['ssd_dt_bias_bwd'] = _head_grads(gbc, gbr)
    g['ssd_a_log_fwd'], g['ssd_a_log_bwd'] = _head_grads(gac, gar)
    g['ssd_d'] = _head_grads(gdk)[0]
    dact = jnp.concatenate([dxs, dbm, dcm], axis=-1).reshape(t, XBC_W)
    dxbc, g['ssd_conv_w'], g_cb = ssd_conv_bwd(xbc, dact, conv_w, conv_b, bsz=bsz, name="ssd_conv_bwd")
    g['ssd_conv_b'] = g_cb.reshape(-1)
    ddt = _dt_from_layouts(ddtc, ddtr)

    dparts = [dz, dxbc, ddt, du]
    dhn = matmul_sum(dparts, [p.T for p in w_in_parts], name="in_proj_dx")
    g['w_in'] = jnp.concatenate([matmul_tn(hn, dp, name=f"in_proj_dw_{i}") for i, dp in enumerate(dparts)], axis=1)
    dx, g_nmix = rowmap_bwd(lambda a, nw: (_rms(a, nw),), [x2], [norm_mix], [dhn], add=dh1, name="rms_mix_bwd")
    g['norm_mix_w'] = g_nmix.reshape(-1)
    return loss, dx.reshape(bsz, seq, d), g


ANY = pl.BlockSpec(memory_space=pl.ANY)


def all_gather(shard, *, name):
    m_per, n = shard.shape

    def body(x_ref, out_ref, send_sems, recv_sems, local_sem):
        x, y, c = lax.axis_index("x"), lax.axis_index("y"), lax.axis_index("c")
        me, sibling = (x, y, c), (x, y, 1 - c)
        chips = [(1 - x, y), (x, 1 - y), (1 - x, 1 - y)]

        def rows(px, py, pc):
            return out_ref.at[pl.ds((4 * px + 2 * py + pc) * m_per, m_per), :]

        def copy(k, block, to, src=None):
            return pltpu.make_async_remote_copy(
                src_ref=rows(*block) if src is None else src, dst_ref=rows(*block),
                send_sem=send_sems.at[k], recv_sem=recv_sems.at[k], device_id=to, device_id_type=MESH)

        mine = pltpu.make_async_copy(x_ref, rows(*me), local_sem)
        mine.start()
        first = [copy(0, me, sibling, src=x_ref)]
        first += [copy(1 + j, me, (*chip, c), src=x_ref) for j, chip in enumerate(chips)]
        for cp in first:
            cp.start()
        passed = [copy(4 + j, (*chip, c), sibling) for j, chip in enumerate(chips)]
        for j, chip in enumerate(chips):
            copy(1 + j, (*chip, c), me).wait_recv()
            passed[j].start()
        copy(0, sibling, me).wait_recv()
        for j, chip in enumerate(chips):
            copy(4 + j, (*chip, 1 - c), me).wait_recv()
        for cp in first + passed:
            cp.wait_send()
        mine.wait()

    return pl.pallas_call(
        body, name=name, out_shape=jax.ShapeDtypeStruct((N_DEV * m_per, n), shard.dtype),
        in_specs=[ANY], out_specs=ANY,
        scratch_shapes=[pltpu.SemaphoreType.DMA((7,)), pltpu.SemaphoreType.DMA((7,)), pltpu.SemaphoreType.DMA],
    )(shard)


def exchange(send, *, name):
    _, r, n = send.shape

    def body(send_ref, recv_ref, send_sems, recv_sems, local_sem):
        x, y, c = lax.axis_index("x"), lax.axis_index("y"), lax.axis_index("c")
        me = 4 * x + 2 * y + c
        local = pltpu.make_async_copy(send_ref.at[me], recv_ref.at[me], local_sem)
        local.start()
        copies = []
        for k in range(1, N_DEV):
            px = (1 - x) if k & 4 else x
            py = (1 - y) if k & 2 else y
            pc = (1 - c) if k & 1 else c
            copies.append(pltpu.make_async_remote_copy(
                src_ref=send_ref.at[4 * px + 2 * py + pc], dst_ref=recv_ref.at[me],
                send_sem=send_sems.at[k - 1], recv_sem=recv_sems.at[k - 1],
                device_id=(px, py, pc), device_id_type=MESH))
        for cp in copies:
            cp.start()
        for cp in copies:
            cp.wait()
        local.wait()

    return pl.pallas_call(
        body, name=name, out_shape=jax.ShapeDtypeStruct(send.shape, send.dtype), in_specs=[ANY], out_specs=ANY,
        scratch_shapes=[pltpu.SemaphoreType.DMA((7,)), pltpu.SemaphoreType.DMA((7,)), pltpu.SemaphoreType.DMA],
    )(send)


def adamw(recv, w, m, v, *, name):
    _, r, n = recv.shape
    tr = min(PACK_ROWS, r)

    def body(r_ref, w_ref, m_ref, v_ref, g_ref, d_ref, nm_ref, nv_ref):
        g = r_ref[0]
        for s in range(1, N_DEV):
            g = g + r_ref[s]
        m_new = ADAM_B1 * m_ref[...] + (1.0 - ADAM_B1) * g
        v_new = ADAM_B2 * v_ref[...] + (1.0 - ADAM_B2) * jnp.square(g)
        m_hat = m_new / (1.0 - ADAM_B1 ** ADAM_STEP)
        v_hat = v_new / (1.0 - ADAM_B2 ** ADAM_STEP)
        g_ref[...] = g
        d_ref[...] = -ADAM_LR * (m_hat / (jnp.sqrt(v_hat) + ADAM_EPS) + ADAM_WD * w_ref[...])
        nm_ref[...] = m_new
        nv_ref[...] = v_new

    blk = pl.BlockSpec((tr, n), lambda i: (i, 0))
    return pl.pallas_call(
        body, name=name, grid=(r // tr,), in_specs=[pl.BlockSpec((N_DEV, tr, n), lambda i: (0, i, 0)), blk, blk, blk],
        out_specs=[blk] * 4, out_shape=[jax.ShapeDtypeStruct((r, n), F32)] * 4,
        compiler_params=_params(("parallel",)))(recv, w, m, v)


def _shard_rows(full, axis):
    if axis == 0:
        return full.reshape(N_DEV, -1)
    r, c = full.shape
    return jnp.transpose(full.reshape(r, N_DEV, c // N_DEV), (1, 0, 2)).reshape(N_DEV, -1)


def _unshard(blocks, axis):
    if axis == 0:
        return blocks.reshape(-1, blocks.shape[-1])
    return jnp.transpose(blocks, (1, 0, 2)).reshape(blocks.shape[1], -1)


def kernel(x, norm_mix_w, w_in, ssd_conv_w, ssd_conv_b, ssd_dt_bias_fwd, ssd_dt_bias_bwd, ssd_a_log_fwd, ssd_a_log_bwd, ssd_d, ssd_norm_w, s5_lambda_re_fwd, s5_lambda_im_fwd, s5_log_step_fwd, s5_lambda_re_bwd, s5_lambda_im_bwd, s5_log_step_bwd, s5_b_re, s5_b_im, s5_c_re_fwd, s5_c_im_fwd, s5_c_re_bwd, s5_c_im_bwd, s5_d, s5_glu_w, s5_glu_b, s5_norm_w, w_out, norm_ffn_w, ffn_w_up, ffn_conv_w, ffn_conv_b, ffn_w_down, norm_final_w, loss_target, m_norm_mix_w, m_w_in, m_ssd_conv_w, m_ssd_conv_b, m_ssd_dt_bias_fwd, m_ssd_dt_bias_bwd, m_ssd_a_log_fwd, m_ssd_a_log_bwd, m_ssd_d, m_ssd_norm_w, m_s5_lambda_re_fwd, m_s5_lambda_im_fwd, m_s5_log_step_fwd, m_s5_lambda_re_bwd, m_s5_lambda_im_bwd, m_s5_log_step_bwd, m_s5_b_re, m_s5_b_im, m_s5_c_re_fwd, m_s5_c_im_fwd, m_s5_c_re_bwd, m_s5_c_im_bwd, m_s5_d, m_s5_glu_w, m_s5_glu_b, m_s5_norm_w, m_w_out, m_norm_ffn_w, m_ffn_w_up, m_ffn_conv_w, m_ffn_conv_b, m_ffn_w_down, m_norm_final_w, v_norm_mix_w, v_w_in, v_ssd_conv_w, v_ssd_conv_b, v_ssd_dt_bias_fwd, v_ssd_dt_bias_bwd, v_ssd_a_log_fwd, v_ssd_a_log_bwd, v_ssd_d, v_ssd_norm_w, v_s5_lambda_re_fwd, v_s5_lambda_im_fwd, v_s5_log_step_fwd, v_s5_lambda_re_bwd, v_s5_lambda_im_bwd, v_s5_log_step_bwd, v_s5_b_re, v_s5_b_im, v_s5_c_re_fwd, v_s5_c_im_fwd, v_s5_c_re_bwd, v_s5_c_im_bwd, v_s5_d, v_s5_glu_w, v_s5_glu_b, v_s5_norm_w, v_w_out, v_norm_ffn_w, v_ffn_w_up, v_ffn_conv_w, v_ffn_conv_b, v_ffn_w_down, v_norm_final_w):
    args = dict(locals())
    strip = lambda n, v: v if n == 'norm_final_w' else v[0]
    w = {n: strip(n, args[n]) for n in WEIGHTS}

    mats = ['w_in', 'w_out', 'ffn_w_up', 'ffn_w_down']
    convs = ['ssd_conv_w', 'ffn_conv_w']
    bits = lambda v: lax.bitcast_convert_type(v, jnp.uint16).reshape(-1)
    flat = jnp.concatenate([bits(w[n].astype(BF16)) for n in mats] + [bits(w[n]) for n in convs])
    width = 1024
    rows = -(-flat.shape[0] // (16 * width)) * 16
    flat = jnp.pad(flat, (0, rows * width - flat.shape[0]))
    gathered = all_gather(flat.reshape(rows, width), name="weight_all_gather").reshape(N_DEV, rows * width)
    full, off = dict(w), 0
    for n in mats:
        r, c = w[n].shape
        blocks = lax.bitcast_convert_type(gathered[:, off:off + r * c], BF16)
        full[n] = _unshard(blocks.reshape(N_DEV, r, c), SHARDED[n])
        off += r * c
    n_conv = sum(w[n].size for n in convs)
    conv_f32 = lax.bitcast_convert_type(gathered[:, off:off + 2 * n_conv].reshape(N_DEV, n_conv, 2), F32)
    off = 0
    for n in convs:
        r, c = w[n].shape
        full[n] = _unshard(conv_f32[:, off:off + r * c].reshape(N_DEV, r, c), SHARDED[n])
        off += r * c

    loss, grad_x, g = local_step(x, loss_target, full)

    order = [n for n in WEIGHTS if n in SHARDED] + [n for n in WEIGHTS if n not in SHARDED]
    pieces = [_shard_rows(g[n], SHARDED[n]) for n in order if n in SHARDED]
    pieces += [jnp.broadcast_to(g[n].reshape(1, -1), (N_DEV, g[n].size)) for n in order if n not in SHARDED]
    pieces.append(jnp.broadcast_to(loss.reshape(1, 1), (N_DEV, 1)))
    total = sum(p.shape[1] for p in pieces)
    nrow = -(-total // (PACK_ROWS * LANES)) * PACK_ROWS
    pieces.append(jnp.zeros((N_DEV, nrow * LANES - total), F32))
    recv = exchange(jnp.concatenate(pieces, axis=1).reshape(N_DEV, nrow, LANES), name="grad_exchange")

    def pack(prefix):
        vals = [strip(n, args[prefix + n]).reshape(-1) for n in order]
        return jnp.pad(jnp.concatenate(vals), (0, nrow * LANES - total + 1)).reshape(nrow, LANES)

    packed = adamw(recv, pack(''), pack('m_'), pack('v_'), name="adamw")
    packed = [p.reshape(-1) for p in packed]
    outs, off = [{}, {}, {}, {}], 0
    for n in order:
        size = w[n].size
        for o, p in zip(outs, packed):
            o[n] = p[off:off + size].reshape(args[n].shape)
        off += size
    loss_out = packed[0][off].reshape(())
    return (loss_out, grad_x, *[o[n] for o in outs for n in WEIGHTS])
```

```python
import functools

import jax
import jax.numpy as jnp
import numpy as np
from jax import lax
from jax.experimental import pallas as pl
from jax.experimental.pallas import tpu as pltpu

F32, BF16 = jnp.float32, jnp.bfloat16
N_DEV = 8
D_MODEL = 1024
SSD_W, HEADS, HDIM, SGROUPS, HPG, NSTATE, SCONV, QC = 1024, 16, 64, 4, 4, 128, 5, 128
XBC_W = SSD_W + 2 * SGROUPS * NSTATE
S5_W, S5_G, S5_C, S5_P, S5_Q = 512, 32, 16, 64, 16
S5_QC = S5_Q * S5_C
CARRY_ROWS = 32
DFF, FCONV = 2816, 3
FFN_BLK, FFN_PAD = 704, 768
EPS = 1e-6
ADAM_LR, ADAM_B1, ADAM_B2, ADAM_EPS, ADAM_WD, ADAM_STEP = 0.001, 0.9, 0.999, 1e-08, 0.01, 10
LANES = 128
MESH = pl.DeviceIdType.MESH

WEIGHTS = ['norm_mix_w', 'w_in', 'ssd_conv_w', 'ssd_conv_b', 'ssd_dt_bias_fwd', 'ssd_dt_bias_bwd', 'ssd_a_log_fwd',
           'ssd_a_log_bwd', 'ssd_d', 'ssd_norm_w', 's5_lambda_re_fwd', 's5_lambda_im_fwd', 's5_log_step_fwd',
           's5_lambda_re_bwd', 's5_lambda_im_bwd', 's5_log_step_bwd', 's5_b_re', 's5_b_im', 's5_c_re_fwd', 's5_c_im_fwd',
           's5_c_re_bwd', 's5_c_im_bwd', 's5_d', 's5_glu_w', 's5_glu_b', 's5_norm_w', 'w_out', 'norm_ffn_w', 'ffn_w_up',
           'ffn_conv_w', 'ffn_conv_b', 'ffn_w_down', 'norm_final_w']
SHARDED = {'w_in': 1, 'ssd_conv_w': 1, 'w_out': 0, 'ffn_w_up': 1, 'ffn_conv_w': 1, 'ffn_w_down': 0}
FULL_SHAPE = {'w_in': (1024, 3616), 'ssd_conv_w': (5, 2048), 'w_out': (1536, 1024), 'ffn_w_up': (1024, 5632),
              'ffn_conv_w': (3, 5632), 'ffn_w_down': (2816, 1024)}
PACK_ROWS = 512


def _pick(n, cap=1536):
    if n <= cap:
        return n
    return max(t for t in range(LANES, cap + 1, LANES) if n % t == 0)


def _params(sem):
    return pltpu.CompilerParams(dimension_semantics=sem)


def _bd(a, b, ca, cb):
    return lax.dot_general(a.astype(BF16), b.astype(BF16), (((ca,), (cb,)), ((), ())), preferred_element_type=F32)


@jax.custom_vjp
def dot_nn(a, b):
    return _bd(a, b, 1, 0)


dot_nn.defvjp(lambda a, b: (_bd(a, b, 1, 0), (a, b)),
              lambda r, g: (_bd(g, r[1], 1, 1).astype(r[0].dtype), _bd(r[0], g, 0, 0).astype(r[1].dtype)))


@jax.custom_vjp
def dot_nt(a, b):
    return _bd(a, b, 1, 1)


dot_nt.defvjp(lambda a, b: (_bd(a, b, 1, 1), (a, b)),
              lambda r, g: (_bd(g, r[1], 1, 0).astype(r[0].dtype), _bd(g, r[0], 0, 0).astype(r[1].dtype)))


@jax.custom_vjp
def dot_tn(a, b):
    return _bd(a, b, 0, 0)


dot_tn.defvjp(lambda a, b: (_bd(a, b, 0, 0), (a, b)),
              lambda r, g: (_bd(r[1], g, 1, 1).astype(r[0].dtype), _bd(r[0], g, 1, 0).astype(r[1].dtype)))


def _split3(x):
    hi = x.astype(BF16)
    r = x - hi.astype(F32)
    mid = r.astype(BF16)
    lo = (r - mid.astype(F32)).astype(BF16)
    return hi, mid, lo


def _cum_matrix(q, upper):
    ri = lax.broadcasted_iota(jnp.int32, (q, q), 0)
    ci = lax.broadcasted_iota(jnp.int32, (q, q), 1)
    return jnp.where((ci >= ri) if upper else (ci <= ri), 1.0, 0.0).astype(BF16)


def _exact_left(mat, x):
    return sum(jnp.dot(mat, p, preferred_element_type=F32) for p in _split3(x))


def _exact_right(x, mat):
    return sum(jnp.dot(p, mat, preferred_element_type=F32) for p in _split3(x))


@functools.partial(jax.custom_vjp, nondiff_argnums=(1,))
def cum_col(x, rev):
    return _exact_left(_cum_matrix(x.shape[0], rev), x)


cum_col.defvjp(lambda x, rev: (cum_col(x, rev), None),
               lambda rev, _, g: (_exact_left(_cum_matrix(g.shape[0], not rev), g),))


@functools.partial(jax.custom_vjp, nondiff_argnums=(1,))
def cum_row(x, rev):
    return _exact_right(x, _cum_matrix(x.shape[1], not rev))


cum_row.defvjp(lambda x, rev: (cum_row(x, rev), None),
               lambda rev, _, g: (_exact_right(g, _cum_matrix(g.shape[1], rev)),))


def _softplus(x):
    return jnp.maximum(x, 0.0) + jnp.log(1.0 + jnp.exp(-jnp.abs(x)))


def _silu(x):
    return x * jax.nn.sigmoid(x)


def _gelu(x):
    return 0.5 * x * (1.0 + jnp.tanh(0.7978845608028654 * (x + 0.044715 * (x * x * x))))


def _rms(x, w):
    xf = x.astype(F32)
    return xf * lax.rsqrt(jnp.mean(xf * xf, axis=-1, keepdims=True) + EPS) * w


def matmul_sum(a_list, b_list, *, name, out_dtype=F32, add=None, tm=512, nt=False):
    a_arrs = [a[0] if isinstance(a, tuple) else a for a in a_list]
    b_arrs = [b[0] if isinstance(b, tuple) else b for b in b_list]
    m, n = a_arrs[0].shape[0], b_arrs[0].shape[-2 if nt else -1]
    tm, tn, k = min(tm, m), _pick(n), len(a_list)

    def body(*refs):
        acc = None
        for a_ref, b_ref in zip(refs[:k], refs[k:2 * k]):
            p = _bd(a_ref[...], b_ref[...], 1, 1 if nt else 0)
            acc = p if acc is None else acc + p
        if add is not None:
            acc = acc + refs[2 * k][...]
        refs[-1][...] = acc.astype(out_dtype)

    def a_spec(a):
        if isinstance(a, tuple):
            return pl.BlockSpec((tm, a[1]), lambda i, j, blk=a[2]: (i, blk))
        return pl.BlockSpec((tm, a.shape[1]), lambda i, j: (i, 0))

    def b_spec(b):
        arr, p = b if isinstance(b, tuple) else (b, None)
        kk = arr.shape[-1 if nt else -2]
        shape, idx = ((tn, kk), lambda j: (j, 0)) if nt else ((kk, tn), lambda j: (0, j))
        if p is None:
            return pl.BlockSpec(shape, lambda i, j: idx(j))
        return pl.BlockSpec((None,) + shape, lambda i, j, p=p: (p,) + idx(j))

    in_specs = [a_spec(a) for a in a_list] + [b_spec(b) for b in b_list]
    args = a_arrs + b_arrs
    if add is not None:
        in_specs.append(pl.BlockSpec((tm, tn), lambda i, j: (i, j)))
        args.append(add)
    return pl.pallas_call(
        body, name=name, grid=(m // tm, n // tn), in_specs=in_specs,
        out_specs=pl.BlockSpec((tm, tn), lambda i, j: (i, j)),
        out_shape=jax.ShapeDtypeStruct((m, n), out_dtype),
        compiler_params=_params(("parallel", "parallel")))(*args)


def matmul_cols(a, b3, *, name, out_dtype=F32, tm=512):
    m, kk = a.shape
    p, _, nb = b3.shape
    tm, tn = min(tm, m), _pick(nb, 768)
    per = nb // tn

    def body(a_ref, b_ref, o_ref):
        o_ref[...] = _bd(a_ref[...], b_ref[...], 1, 0).astype(out_dtype)

    return pl.pallas_call(
        body, name=name, grid=(m // tm, p * per),
        in_specs=[pl.BlockSpec((tm, kk), lambda i, j: (i, 0)),
                  pl.BlockSpec((None, kk, tn), lambda i, j: (j // per, 0, j % per))],
        out_specs=pl.BlockSpec((tm, tn), lambda i, j: (i, j)),
        out_shape=jax.ShapeDtypeStruct((m, p * nb), out_dtype),
        compiler_params=_params(("parallel", "parallel")))(a, b3)


def matmul_tn(a, b, *, name, tm=512, out_blocks=None):
    m, k = a.shape
    n = b.shape[1]
    nb = n // (out_blocks or 1)
    tm, tk, tn = min(tm, m), _pick(k), _pick(nb, 768 if out_blocks else 1536)
    per = nb // tn

    def body(a_ref, b_ref, o_ref):
        @pl.when(pl.program_id(2) == 0)
        def _():
            o_ref[...] = jnp.zeros_like(o_ref)

        o_ref[...] += _bd(a_ref[...], b_ref[...], 0, 0)

    if out_blocks:
        out_spec = pl.BlockSpec((None, tk, tn), lambda i, j, t: (j // per, i, j % per))
        out_shape = jax.ShapeDtypeStruct((out_blocks, k, nb), F32)
    else:
        out_spec = pl.BlockSpec((tk, tn), lambda i, j, t: (i, j))
        out_shape = jax.ShapeDtypeStruct((k, n), F32)
    return pl.pallas_call(
        body, name=name, grid=(k // tk, n // tn, m // tm),
        in_specs=[pl.BlockSpec((tm, tk), lambda i, j, t: (t, i)), pl.BlockSpec((tm, tn), lambda i, j, t: (t, j))],
        out_specs=out_spec, out_shape=out_shape,
        compiler_params=_params(("parallel", "parallel", "arbitrary")))(a, b)


def _row_spec(r, tm):
    if isinstance(r, tuple):
        arr, width, blk = r
        return arr, pl.BlockSpec((tm, width), lambda i, blk=blk: (i, blk))
    return r, pl.BlockSpec((tm, r.shape[1]), lambda i: (i, 0))


def _full_spec(p):
    return pl.BlockSpec(p.shape, lambda i: (0,) * p.ndim)


def rowmap_fwd(fn, rows, params, outs, *, name, tm=256):
    pairs = [_row_spec(r, tm) for r in rows]
    m = pairs[0][0].shape[0]
    tm = min(tm, m)
    pairs = [_row_spec(r, tm) for r in rows]
    nr, npar = len(rows), len(params)

    def body(*refs):
        res = fn(*[r[...] for r in refs[:nr + npar]])
        for o_ref, v in zip(refs[nr + npar:], res):
            o_ref[...] = v.astype(o_ref.dtype)

    return pl.pallas_call(
        body, name=name, grid=(m // tm,),
        in_specs=[s for _, s in pairs] + [_full_spec(p) for p in params],
        out_specs=[pl.BlockSpec((tm, c), lambda i: (i, 0)) for c, _ in outs],
        out_shape=[jax.ShapeDtypeStruct((m, c), dt) for c, dt in outs],
        compiler_params=_params(("parallel",)))(*[a for a, _ in pairs], *params)


def rowmap_bwd(fn, rows, params, cts, *, name, row_dtypes=None, add=None, tm=256):
    m = _row_spec(rows[0], tm)[0].shape[0]
    tm = min(tm, m)
    rp = [_row_spec(r, tm) for r in rows]
    cp = [_row_spec(c, tm) for c in cts]
    nr, npar, nc = len(rows), len(params), len(cts)
    row_dtypes = row_dtypes or [F32] * nr
    widths = [s.block_shape[1] for _, s in rp]

    def body(*refs):
        ins = [r[...] for r in refs[:nr + npar]]
        ins = [v.astype(F32) for v in ins]
        ct = tuple(r[...].astype(F32) for r in refs[nr + npar:nr + npar + nc])
        base = nr + npar + nc
        extra = None
        if add is not None:
            extra = refs[base][...]
            base += 1
        _, pull = jax.vjp(fn, *ins)
        grads = pull(ct)
        for j in range(nr):
            g = grads[j]
            if j == 0 and extra is not None:
                g = g + extra
            refs[base + j][...] = g.astype(refs[base + j].dtype)

        @pl.when(pl.program_id(0) == 0)
        def _():
            for j in range(npar):
                refs[base + nr + j][...] = jnp.zeros_like(refs[base + nr + j])

        for j in range(npar):
            refs[base + nr + j][...] += grads[nr + j]

    in_specs = [s for _, s in rp] + [_full_spec(p) for p in params] + [s for _, s in cp]
    args = [a for a, _ in rp] + list(params) + [a for a, _ in cp]
    if add is not None:
        in_specs.append(pl.BlockSpec((tm, widths[0]), lambda i: (i, 0)))
        args.append(add)
    out_specs = [pl.BlockSpec((tm, w), lambda i: (i, 0)) for w in widths] + [_full_spec(p) for p in params]
    out_shape = [jax.ShapeDtypeStruct((m, w), dt) for w, dt in zip(widths, row_dtypes)]
    out_shape += [jax.ShapeDtypeStruct(p.shape, F32) for p in params]
    return pl.pallas_call(
        body, name=name, grid=(m // tm,), in_specs=in_specs, out_specs=out_specs, out_shape=out_shape,
        compiler_params=_params(("arbitrary",)))(*args)


def loss_head(h, target, w, *, name, tm=256):
    m, d = h.shape
    tm = min(tm, m)

    def body(h_ref, t_ref, w_ref, loss_ref, dh_ref, dw_ref):
        y, pull = jax.vjp(_rms, h_ref[...], w_ref[...])
        err = y - t_ref[...]
        dh, dw = pull(err * (1.0 / d))

        @pl.when(pl.program_id(0) == 0)
        def _():
            loss_ref[...] = jnp.zeros_like(loss_ref)
            dw_ref[...] = jnp.zeros_like(dw_ref)

        loss_ref[...] += (0.5 / d) * jnp.sum(err * err, keepdims=True)
        dw_ref[...] += dw
        dh_ref[...] = dh

    row = pl.BlockSpec((tm, d), lambda i: (i, 0))
    return pl.pallas_call(
        body, name=name, grid=(m // tm,), in_specs=[row, row, _full_spec(w)],
        out_specs=[pl.BlockSpec((1, 1), lambda i: (0, 0)), row, _full_spec(w)],
        out_shape=[jax.ShapeDtypeStruct((1, 1), F32), jax.ShapeDtypeStruct((m, d), F32),
                   jax.ShapeDtypeStruct(w.shape, F32)],
        compiler_params=_params(("arbitrary",)))(h, target, w)


def _shift(x, s):
    if s == 0:
        return x
    n = x.shape[0]
    t = lax.broadcasted_iota(jnp.int32, x.shape, 0)
    rolled = pltpu.roll(x, (-s) % n, 0)
    return jnp.where((t + s >= 0) & (t + s < n), rolled, 0.0)


def _conv(x, w, b):
    k = w.shape[0]
    acc = b + w[k // 2:k // 2 + 1, :] * x
    for j in range(k):
        if j != k // 2:
            acc = acc + w[j:j + 1, :] * _shift(x, j - k // 2)
    return acc


def _conv_bwd(x, dc, w):
    k = w.shape[0]
    dx = None
    dws = []
    for j in range(k):
        s = j - k // 2
        term = w[j:j + 1, :] * _shift(dc, -s)
        dx = term if dx is None else dx + term
        dws.append(jnp.sum(dc * _shift(x, s), axis=0, keepdims=True))
    return dx, jnp.concatenate(dws, axis=0), jnp.sum(dc, axis=0, keepdims=True)


def _dsilu(c):
    s = jax.nn.sigmoid(c)
    return s * (1.0 + c * (1.0 - s))


def ssd_conv_fwd(xbc, w, b, *, bsz, name):
    t, c = xbc.shape
    seq, ct = t // bsz, 256

    def body(x_ref, w_ref, b_ref, o_ref):
        o_ref[...] = _silu(_conv(x_ref[...], w_ref[...], b_ref[...]))

    return pl.pallas_call(
        body, name=name, grid=(c // ct, bsz),
        in_specs=[pl.BlockSpec((seq, ct), lambda j, i: (i, j)), pl.BlockSpec((w.shape[0], ct), lambda j, i: (0, j)),
                  pl.BlockSpec((1, ct), lambda j, i: (0, j))],
        out_specs=pl.BlockSpec((seq, ct), lambda j, i: (i, j)),
        out_shape=jax.ShapeDtypeStruct((t, c), F32),
        compiler_params=_params(("parallel", "parallel")))(xbc, w, b)


def ssd_conv_bwd(xbc, dact, w, b, *, bsz, name):
    t, c = xbc.shape
    seq, ct, k = t // bsz, 256, w.shape[0]

    def body(x_ref, g_ref, w_ref, b_ref, dx_ref, dw_ref, db_ref):
        x, wv = x_ref[...], w_ref[...]
        dc = g_ref[...] * _dsilu(_conv(x, wv, b_ref[...]))
        dx, dw, db = _conv_bwd(x, dc, wv)
        dx_ref[...] = dx

        @pl.when(pl.program_id(1) == 0)
        def _():
            dw_ref[...] = jnp.zeros_like(dw_ref)
            db_ref[...] = jnp.zeros_like(db_ref)

        dw_ref[...] += dw
        db_ref[...] += db

    blk = pl.BlockSpec((seq, ct), lambda j, i: (i, j))
    wspec, bspec = pl.BlockSpec((k, ct), lambda j, i: (0, j)), pl.BlockSpec((1, ct), lambda j, i: (0, j))
    return pl.pallas_call(
        body, name=name, grid=(c // ct, bsz), in_specs=[blk, blk, wspec, bspec], out_specs=[blk, wspec, bspec],
        out_shape=[jax.ShapeDtypeStruct((t, c), F32), jax.ShapeDtypeStruct((k, c), F32),
                   jax.ShapeDtypeStruct((1, c), F32)],
        compiler_params=_params(("parallel", "arbitrary")))(xbc, dact, w, b)


def _ffn_specs(seq, ct, k, nblk):
    val = pl.BlockSpec((seq, ct), lambda j, i: (i, j))
    gate = pl.BlockSpec((seq, ct), lambda j, i: (i, nblk + j))
    wv, wg = pl.BlockSpec((k, ct), lambda j, i: (0, j)), pl.BlockSpec((k, ct), lambda j, i: (0, nblk + j))
    bv, bg = pl.BlockSpec((1, ct), lambda j, i: (0, j)), pl.BlockSpec((1, ct), lambda j, i: (0, nblk + j))
    return val, gate, wv, wg, bv, bg


def ffn_act_fwd(up, w, b, *, bsz, name):
    t = up.shape[0]
    half = up.shape[1] // 2
    seq, ct, k = t // bsz, 256, w.shape[0]
    val, gate, wv, wg, bv, bg = _ffn_specs(seq, ct, k, half // ct)

    def body(v_ref, g_ref, wv_ref, wg_ref, bv_ref, bg_ref, o_ref):
        vc = _conv(v_ref[...], wv_ref[...], bv_ref[...])
        gc = _conv(g_ref[...], wg_ref[...], bg_ref[...])
        o_ref[...] = (_silu(gc) * vc).astype(BF16)

    return pl.pallas_call(
        body, name=name, grid=(half // ct, bsz), in_specs=[val, gate, wv, wg, bv, bg], out_specs=val,
        out_shape=jax.ShapeDtypeStruct((t, half), BF16),
        compiler_params=_params(("parallel", "parallel")))(up, up, w, w, b, b)


def ffn_act_bwd(up, dact, w, b, *, bsz, name):
    t = up.shape[0]
    half = up.shape[1] // 2
    seq, ct, k = t // bsz, 256, w.shape[0]
    val, gate, wv, wg, bv, bg = _ffn_specs(seq, ct, k, half // ct)

    def body(v_ref, g_ref, wv_ref, wg_ref, bv_ref, bg_ref, d_ref, dv_ref, dg_ref, dwv_ref, dwg_ref, dbv_ref, dbg_ref):
        v, g = v_ref[...], g_ref[...]
        vc = _conv(v, wv_ref[...], bv_ref[...])
        gc = _conv(g, wg_ref[...], bg_ref[...])
        d = d_ref[...].astype(F32)
        dv, dwv, dbv = _conv_bwd(v, d * _silu(gc), wv_ref[...])
        dg, dwg, dbg = _conv_bwd(g, d * vc * _dsilu(gc), wg_ref[...])
        dv_ref[...] = dv.astype(BF16)
        dg_ref[...] = dg.astype(BF16)

        @pl.when(pl.program_id(1) == 0)
        def _():
            for r in (dwv_ref, dwg_ref, dbv_ref, dbg_ref):
                r[...] = jnp.zeros_like(r)

        dwv_ref[...] += dwv
        dwg_ref[...] += dwg
        dbv_ref[...] += dbv
        dbg_ref[...] += dbg

    return pl.pallas_call(
        body, name=name, grid=(half // ct, bsz), in_specs=[val, gate, wv, wg, bv, bg, val],
        out_specs=[val, val, wv, wv, bv, bv],
        out_shape=[jax.ShapeDtypeStruct((t, half), BF16), jax.ShapeDtypeStruct((t, half), BF16),
                   jax.ShapeDtypeStruct((k, half), F32), jax.ShapeDtypeStruct((k, half), F32),
                   jax.ShapeDtypeStruct((1, half), F32), jax.ShapeDtypeStruct((1, half), F32)],
        compiler_params=_params(("parallel", "arbitrary")))(up, up, w, w, b, b, dact)


@functools.partial(jax.custom_vjp, nondiff_argnums=(1, 2))
def _take_col(a, h, n):
    return a[:, h:h + 1]


_take_col.defvjp(lambda a, h, n: (a[:, h:h + 1], None),
                 lambda h, n, _, g: (g * (lax.broadcasted_iota(jnp.int32, (1, n), 1) == h).astype(F32),))


@functools.partial(jax.custom_vjp, nondiff_argnums=(1, 2))
def _take_row(a, h, n):
    return a[h:h + 1, :]


_take_row.defvjp(lambda a, h, n: (a[h:h + 1, :], None),
                 lambda h, n, _, g: (g * (lax.broadcasted_iota(jnp.int32, (n, 1), 0) == h).astype(F32),))


def _sel_col(a, h):
    return _take_col(a, h, a.shape[1])


def _sel_row(a, h):
    return _take_row(a, h, a.shape[0])


def _ssd_chunk(xh, dtc, dtr, bm, cm, prev, bias_c, alog_c, dskip_c, bias_r, alog_r, rev):
    q = dtc.shape[0]
    ri = lax.broadcasted_iota(jnp.int32, (q, q), 0)
    ci = lax.broadcasted_iota(jnp.int32, (q, q), 1)
    mask = (ci >= ri) if rev else (ci <= ri)
    dt_c = _softplus(dtc + bias_c)
    dta_c = dt_c * (-jnp.exp(alog_c))
    cs_c = cum_col(dta_c, rev)
    dta_r = _softplus(dtr + bias_r) * (-jnp.exp(alog_r))
    cs_r = cum_row(dta_r, rev)
    scores = dot_nt(cm, bm)
    ys, news = [], []
    for h in range(HPG):
        hh = h + (HPG if rev else 0)
        csq = _sel_col(cs_c, hh)
        css = _sel_row(cs_r, hh)
        seg = jnp.exp(jnp.where(mask, csq - css, -1e30))
        xdt = xh[h] * _sel_col(dt_c, hh)
        y = dot_nn(scores * seg, xdt)
        tot = jnp.sum(_sel_col(dta_c, hh), axis=0, keepdims=True)
        y = y + dot_nt(cm, prev[h]) * jnp.exp(csq)
        if not rev:
            y = y + _sel_col(dskip_c, hh) * xh[h]
        ys.append(y)
        news.append(jnp.exp(tot) * prev[h] + dot_tn(xdt * jnp.exp(tot - csq), bm))
    return tuple(ys), tuple(news)


def _ssd_specs(seq):
    xs = pl.BlockSpec((None, seq, HPG * HDIM), lambda b, g: (b, 0, g))
    bm = pl.BlockSpec((None, seq, NSTATE), lambda b, g: (b, 0, SSD_W // NSTATE + g))
    cm = pl.BlockSpec((None, seq, NSTATE), lambda b, g: (b, 0, SSD_W // NSTATE + SGROUPS + g))
    dtc = pl.BlockSpec((None, None, seq, 2 * HPG), lambda b, g: (b, g, 0, 0))
    dtr = pl.BlockSpec((None, None, 2 * HPG, seq), lambda b, g: (b, g, 0, 0))
    pc = pl.BlockSpec((None, 1, 2 * HPG), lambda b, g: (g, 0, 0))
    pr = pl.BlockSpec((None, 2 * HPG, 1), lambda b, g: (g, 0, 0))
    return xs, bm, cm, dtc, dtr, pc, pr


def _head_cols(h):
    return slice(HDIM * h, HDIM * (h + 1))


def ssd_scan_fwd(act, dtc, dtr, pcs, prs, *, name):
    bsz, seq, _ = act.shape
    nc = seq // QC
    xs, bm, cm, dtcs, dtrs, pc, pr = _ssd_specs(seq)

    def body(x_ref, b_ref, c_ref, dtc_ref, dtr_ref, bc_ref, ac_ref, dk_ref, br_ref, ar_ref, y_ref):
        par = (bc_ref[...], ac_ref[...], dk_ref[...], br_ref[...], ar_ref[...])
        for rev in (False, True):
            def step(i, carry, rev=rev):
                rows = pl.ds(pl.multiple_of(((nc - 1 - i) if rev else i) * QC, QC), QC)
                xh = tuple(x_ref[rows, _head_cols(h)] for h in range(HPG))
                ys, new = _ssd_chunk(xh, dtc_ref[rows, :], dtr_ref[:, rows], b_ref[rows, :], c_ref[rows, :],
                                     carry, *par, rev)
                for h in range(HPG):
                    if rev:
                        y_ref[rows, _head_cols(h)] += ys[h]
                    else:
                        y_ref[rows, _head_cols(h)] = ys[h]
                return new

            lax.fori_loop(0, nc, step, tuple(jnp.zeros((HDIM, NSTATE), F32) for _ in range(HPG)))

    return pl.pallas_call(
        body, name=name, grid=(bsz, SGROUPS), in_specs=[xs, bm, cm, dtcs, dtrs, pc, pc, pc, pr, pr], out_specs=xs,
        out_shape=jax.ShapeDtypeStruct((bsz, seq, SSD_W), F32),
        compiler_params=_params(("parallel", "parallel")))(act, act, act, dtc, dtr, *pcs, *prs)


def ssd_scan_bwd(act, dtc, dtr, pcs, prs, dy, *, name):
    bsz, seq, _ = act.shape
    nc = seq // QC
    xs, bm, cm, dtcs, dtrs, pc, pr = _ssd_specs(seq)
    grp = pl.BlockSpec((None, seq, NSTATE), lambda b, g: (b, 0, g))
    dpc = pl.BlockSpec((None, None, 1, 2 * HPG), lambda b, g: (b, g, 0, 0))
    dpr = pl.BlockSpec((None, None, 2 * HPG, 1), lambda b, g: (b, g, 0, 0))

    def body(x_ref, b_ref, c_ref, dtc_ref, dtr_ref, bc_ref, ac_ref, dk_ref, br_ref, ar_ref, dy_ref,
             dx_ref, db_ref, dc_ref, ddtc_ref, ddtr_ref, gbc_ref, gac_ref, gdk_ref, gbr_ref, gar_ref, st_ref):
        par = (bc_ref[...], ac_ref[...], dk_ref[...], br_ref[...], ar_ref[...])
        pgrads = (gbc_ref, gac_ref, gdk_ref, gbr_ref, gar_ref)
        for r in pgrads:
            r[...] = jnp.zeros_like(r)
        zero = tuple(jnp.zeros((HDIM, NSTATE), F32) for _ in range(HPG))
        for rev in (False, True):
            def load(k):
                rows = pl.ds(pl.multiple_of(k * QC, QC), QC)
                xh = tuple(x_ref[rows, _head_cols(h)] for h in range(HPG))
                return rows, xh, dtc_ref[rows, :], dtr_ref[:, rows], b_ref[rows, :], c_ref[rows, :]

            def fstep(i, carry, rev=rev):
                k = (nc - 1 - i) if rev else i
                _, xh, a, b, c, d = load(k)
                for h in range(HPG):
                    st_ref[k, h] = carry[h]
                return _ssd_chunk(xh, a, b, c, d, carry, *par, rev)[1]

            lax.fori_loop(0, nc, fstep, zero)

            def bstep(i, dcarry, rev=rev):
                k = i if rev else (nc - 1 - i)
                rows, xh, a, b, c, d = load(k)
                prev = tuple(st_ref[k, h] for h in range(HPG))
                _, pull = jax.vjp(functools.partial(_ssd_chunk, rev=rev), xh, a, b, c, d, prev, *par)
                dyh = tuple(dy_ref[rows, _head_cols(h)] for h in range(HPG))
                gx, ga, gb, gc, gd, gprev, *gpar = pull((dyh, dcarry))
                for h in range(HPG):
                    if rev:
                        dx_ref[rows, _head_cols(h)] += gx[h]
                    else:
                        dx_ref[rows, _head_cols(h)] = gx[h]
                if rev:
                    ddtc_ref[rows, :] += ga
                    ddtr_ref[:, rows] += gb
                    db_ref[rows, :] += gc
                    dc_ref[rows, :] += gd
                else:
                    ddtc_ref[rows, :] = ga
                    ddtr_ref[:, rows] = gb
                    db_ref[rows, :] = gc
                    dc_ref[rows, :] = gd
                for r, g in zip(pgrads, gpar):
                    r[...] += g
                return gprev

            lax.fori_loop(0, nc, bstep, zero)

    out_shape = [jax.ShapeDtypeStruct((bsz, seq, SSD_W), F32),
                 jax.ShapeDtypeStruct((bsz, seq, SGROUPS * NSTATE), F32),
                 jax.ShapeDtypeStruct((bsz, seq, SGROUPS * NSTATE), F32),
                 jax.ShapeDtypeStruct(dtc.shape, F32), jax.ShapeDtypeStruct(dtr.shape, F32)]
    out_shape += [jax.ShapeDtypeStruct((bsz, SGROUPS, 1, 2 * HPG), F32)] * 3
    out_shape += [jax.ShapeDtypeStruct((bsz, SGROUPS, 2 * HPG, 1), F32)] * 2
    return pl.pallas_call(
        body, name=name, grid=(bsz, SGROUPS), in_specs=[xs, bm, cm, dtcs, dtrs, pc, pc, pc, pr, pr, xs],
        out_specs=[xs, grp, grp, dtcs, dtrs, dpc, dpc, dpc, dpr, dpr], out_shape=out_shape,
        scratch_shapes=[pltpu.VMEM((nc, HPG, HDIM, NSTATE), F32)],
        compiler_params=_params(("parallel", "parallel")))(act, act, act, dtc, dtr, *pcs, *prs, dy)


def _s5_direction(lam_re, lam_im, log_step, b_re, b_im, c_re, c_im, rev):
    q = S5_Q
    step = jnp.exp(log_step)[:, None]
    lr, li = lam_re * step, lam_im * step
    mag = jnp.exp(lr)
    ar, ai = mag * jnp.cos(li), mag * jnp.sin(li)
    den = lam_re * lam_re + lam_im * lam_im
    cr = ((ar - 1.0) * lam_re + ai * lam_im) / den
    ci = (ai * lam_re - (ar - 1.0) * lam_im) / den
    bbr = cr[..., None] * b_re - ci[..., None] * b_im
    bbi = cr[..., None] * b_im + ci[..., None] * b_re
    d = jnp.arange(q + 1, dtype=F32)[None, :, None]
    pm = jnp.exp(d * lr[:, None, :])
    pr, pi = pm * jnp.cos(d * li[:, None, :]), pm * jnp.sin(d * li[:, None, :])
    er = pr[..., None] * bbr[:, None] - pi[..., None] * bbi[:, None]
    ei = pr[..., None] * bbi[:, None] + pi[..., None] * bbr[:, None]
    hp = lax.Precision.HIGHEST
    k = (jnp.einsum('gcp,gdpz->gdcz', c_re, er[:, :q], precision=hp)
         - jnp.einsum('gcp,gdpz->gdcz', c_im, ei[:, :q], precision=hp))
    e = jnp.concatenate([er[:, :q], ei[:, :q]], axis=2)
    wt = jnp.transpose(e if rev else e[:, ::-1], (0, 1, 3, 2))
    p1r, p1i = pr[:, 1:], pi[:, 1:]
    if rev:
        p1r, p1i = p1r[:, ::-1], p1i[:, ::-1]
    m_re = c_re[:, None] * p1r[:, :, None, :] - c_im[:, None] * p1i[:, :, None, :]
    m_im = -c_re[:, None] * p1i[:, :, None, :] - c_im[:, None] * p1r[:, :, None, :]
    mt = jnp.transpose(jnp.concatenate([m_re, m_im], axis=-1), (0, 3, 1, 2))
    da = jnp.concatenate([pr[:, q], pr[:, q]], axis=-1)
    db = jnp.concatenate([-pi[:, q], pi[:, q]], axis=-1)
    return k, wt, mt, da, db


def _s5_operators(lf_re, lf_im, lsf, lb_re, lb_im, lsb, b_re, b_im, cf_re, cf_im, cb_re, cb_im):
    q = S5_Q
    kf, wtf, mtf, daf, dbf = _s5_direction(lf_re, lf_im, lsf, b_re, b_im, cf_re, cf_im, False)
    kb, wtb, mtb, dab, dbb = _s5_direction(lb_re, lb_im, lsb, b_re, b_im, cb_re, cb_im, True)
    t = np.arange(q)[:, None, None]
    s = np.arange(q)[None, :, None]
    d = np.arange(q)[None, None, :]
    ohf = jnp.asarray((t - s == d).astype(np.float32))
    ohb = jnp.asarray((s - t == d).astype(np.float32))
    hp = lax.Precision.HIGHEST
    tt = (jnp.einsum('tsd,gdcz->gsztc', ohf, kf, precision=hp) + jnp.einsum('tsd,gdcz->gsztc', ohb, kb, precision=hp))
    g = tt.shape[0]
    tt = tt.reshape(g, S5_QC, S5_QC)
    wt = jnp.concatenate([wtf.reshape(g, S5_QC, 2 * S5_P), wtb.reshape(g, S5_QC, 2 * S5_P)], axis=-1)
    mt = jnp.concatenate([mtf.reshape(g, 2 * S5_P, S5_QC), mtb.reshape(g, 2 * S5_P, S5_QC)], axis=1)
    return tt, wt, mt, jnp.concatenate([daf, dab], -1), jnp.concatenate([dbf, dbb], -1)


def _gspec(*shape):
    return pl.BlockSpec((None,) + shape, lambda g: (g,) + (0,) * len(shape))


def s5_state_in(u, wt, *, name):
    g, r, _ = u.shape

    def body(u_ref, w_ref, o_ref):
        o_ref[...] = _bd(u_ref[...], w_ref[...], 1, 0)

    return pl.pallas_call(
        body, name=name, grid=(g,), in_specs=[_gspec(r, S5_QC), _gspec(S5_QC, 4 * S5_P)],
        out_specs=_gspec(r, 4 * S5_P), out_shape=jax.ShapeDtypeStruct((g, r, 4 * S5_P), F32),
        compiler_params=_params(("parallel",)))(u, wt)


def _swap(h):
    return pltpu.roll(h, S5_P, 1)


def s5_carry_fwd(s, da, db, *, name):
    nck, rows, _ = s.shape
    w = 2 * S5_P

    def body(s_ref, da_ref, db_ref, h_ref):
        for rev, cols in ((False, slice(0, w)), (True, slice(w, 2 * w))):
            a, b = da_ref[:, cols], db_ref[:, cols]

            def step(i, h, rev=rev, cols=cols, a=a, b=b):
                k = (nck - 1 - i) if rev else i
                h_ref[k, :, cols] = h
                return a * h + b * _swap(h) + s_ref[k, :, cols]

            lax.fori_loop(0, nck, step, jnp.zeros((rows, w), F32))

    rt = min(CARRY_ROWS, rows)
    big, small = pl.BlockSpec((nck, rt, 2 * w), lambda i: (0, i, 0)), pl.BlockSpec((rt, 2 * w), lambda i: (i, 0))
    rows = rt
    return pl.pallas_call(
        body, name=name, grid=(s.shape[1] // rt,), in_specs=[big, small, small], out_specs=big,
        out_shape=jax.ShapeDtypeStruct(s.shape, F32), compiler_params=_params(("parallel",)))(s, da, db)


def s5_carry_bwd(hin, dh, da, db, *, name):
    nck, rows, _ = hin.shape
    w = 2 * S5_P

    def body(h_ref, dh_ref, da_ref, db_ref, ds_ref, gda_ref, gdb_ref):
        for rev, cols in ((False, slice(0, w)), (True, slice(w, 2 * w))):
            a, b = da_ref[:, cols], db_ref[:, cols]

            def step(i, carry, rev=rev, cols=cols, a=a, b=b):
                g, ga, gb = carry
                k = i if rev else (nck - 1 - i)
                ds_ref[k, :, cols] = g
                h = h_ref[k, :, cols]
                return (dh_ref[k, :, cols] + a * g + _swap(b * g), ga + g * h, gb + g * _swap(h))

            z = jnp.zeros((rows, w), F32)
            _, ga, gb = lax.fori_loop(0, nck, step, (z, z, z))
            gda_ref[:, cols] = ga
            gdb_ref[:, cols] = gb

    rt = min(CARRY_ROWS, rows)
    big, small = pl.BlockSpec((nck, rt, 2 * w), lambda i: (0, i, 0)), pl.BlockSpec((rt, 2 * w), lambda i: (i, 0))
    rows = rt
    return pl.pallas_call(
        body, name=name, grid=(hin.shape[1] // rt,), in_specs=[big, big, small, small], out_specs=[big, small, small],
        out_shape=[jax.ShapeDtypeStruct(hin.shape, F32), jax.ShapeDtypeStruct(da.shape, F32),
                   jax.ShapeDtypeStruct(da.shape, F32)],
        compiler_params=_params(("parallel",)))(hin, dh, da, db)


def s5_out(u, hin, tt, mt, *, name):
    g, r, _ = u.shape

    def body(u_ref, h_ref, t_ref, m_ref, o_ref):
        o_ref[...] = _bd(u_ref[...], t_ref[...], 1, 0) + _bd(h_ref[...], m_ref[...], 1, 0)

    return pl.pallas_call(
        body, name=name, grid=(g,),
        in_specs=[_gspec(r, S5_QC), _gspec(r, 4 * S5_P), _gspec(S5_QC, S5_QC), _gspec(4 * S5_P, S5_QC)],
        out_specs=_gspec(r, S5_QC), out_shape=jax.ShapeDtypeStruct((g, r, S5_QC), F32),
        compiler_params=_params(("parallel",)))(u, hin, tt, mt)


def s5_out_bwd(dy, u, hin, tt, mt, *, name):
    g, r, _ = u.shape

    def body(dy_ref, u_ref, h_ref, t_ref, m_ref, dh_ref, dt_ref, dm_ref, du_ref):
        dy_v = dy_ref[...]
        dh_ref[...] = _bd(dy_v, m_ref[...], 1, 1)
        dt_ref[...] = _bd(u_ref[...], dy_v, 0, 0)
        dm_ref[...] = _bd(h_ref[...], dy_v, 0, 0)
        du_ref[...] = _bd(dy_v, t_ref[...], 1, 1)

    return pl.pallas_call(
        body, name=name, grid=(g,),
        in_specs=[_gspec(r, S5_QC), _gspec(r, S5_QC), _gspec(r, 4 * S5_P), _gspec(S5_QC, S5_QC),
                  _gspec(4 * S5_P, S5_QC)],
        out_specs=[_gspec(r, 4 * S5_P), _gspec(S5_QC, S5_QC), _gspec(4 * S5_P, S5_QC), _gspec(r, S5_QC)],
        out_shape=[jax.ShapeDtypeStruct((g, r, 4 * S5_P), F32), jax.ShapeDtypeStruct((g, S5_QC, S5_QC), F32),
                   jax.ShapeDtypeStruct((g, 4 * S5_P, S5_QC), F32), jax.ShapeDtypeStruct((g, r, S5_QC), F32)],
        compiler_params=_params(("parallel",)))(dy, u, hin, tt, mt)


def s5_state_in_bwd(ds, u, wt, du1, *, name):
    g, r, _ = u.shape

    def body(ds_ref, u_ref, w_ref, du1_ref, du_ref, dw_ref):
        ds_v = ds_ref[...]
        du_ref[...] = du1_ref[...] + _bd(ds_v, w_ref[...], 1, 1)
        dw_ref[...] = _bd(u_ref[...], ds_v, 0, 0)

    return pl.pallas_call(
        body, name=name, grid=(g,),
        in_specs=[_gspec(r, 4 * S5_P), _gspec(r, S5_QC), _gspec(S5_QC, 4 * S5_P), _gspec(r, S5_QC)],
        out_specs=[_gspec(r, S5_QC), _gspec(S5_QC, 4 * S5_P)],
        out_shape=[jax.ShapeDtypeStruct((g, r, S5_QC), F32), jax.ShapeDtypeStruct((g, S5_QC, 4 * S5_P), F32)],
        compiler_params=_params(("parallel",)))(ds, u, wt, du1)


def _s5_post(ypre, u, dvec, wv, wg, bv, bg, nw):
    g = _gelu(ypre + dvec * u)
    out = (dot_nn(g, wv) + bv) * jax.nn.sigmoid(dot_nn(g, wg) + bg)
    return (_rms(out, nw),)


def _ssd_post(y, z, nw):
    return (_rms(y * _silu(z), nw),)


def _to_chunks(u, bsz):
    nck = u.shape[0] // bsz // S5_Q
    v = u.reshape(bsz, nck, S5_Q, S5_G, S5_C)
    return jnp.transpose(v, (3, 0, 1, 2, 4)).reshape(S5_G, bsz * nck, S5_QC)


def _from_chunks(y, bsz):
    nck = y.shape[1] // bsz
    v = y.reshape(S5_G, bsz, nck, S5_Q, S5_C)
    return jnp.transpose(v, (1, 2, 3, 0, 4)).reshape(bsz * nck * S5_Q, S5_W)


def _to_carry(s, bsz):
    nck = s.shape[1] // bsz
    return jnp.transpose(s.reshape(S5_G, bsz, nck, -1), (2, 0, 1, 3)).reshape(nck, S5_G * bsz, -1)


def _from_carry(h, bsz):
    nck = h.shape[0]
    return jnp.transpose(h.reshape(nck, S5_G, bsz, -1), (1, 2, 0, 3)).reshape(S5_G, bsz * nck, -1)


def _block_diag(w):
    eye = jnp.eye(S5_G, dtype=w.dtype)
    return jnp.einsum('gcd,gh->gchd', w, eye).reshape(S5_W, S5_W)


def _diag_blocks(w):
    v = w.reshape(S5_G, S5_C, S5_G, S5_C)
    return v[jnp.arange(S5_G), :, jnp.arange(S5_G), :]


def _dt_layouts(dt, bsz):
    seq = dt.shape[0] // bsz
    v = jnp.transpose(dt.reshape(bsz, seq, 2, SGROUPS, HPG), (0, 3, 1, 2, 4)).reshape(bsz, SGROUPS, seq, 2 * HPG)
    return v, jnp.transpose(v, (0, 1, 3, 2))


def _dt_from_layouts(dc, dr):
    bsz, _, seq, _ = dc.shape
    v = (dc + jnp.transpose(dr, (0, 1, 3, 2))).reshape(bsz, SGROUPS, seq, 2, HPG)
    return jnp.transpose(v, (0, 2, 3, 1, 4)).reshape(bsz * seq, 2 * HEADS)


def _head_params(f, b):
    v = jnp.concatenate([f.reshape(SGROUPS, HPG), b.reshape(SGROUPS, HPG)], axis=1)
    return v[:, None, :], v[:, :, None]


def _head_grads(gc, gr=None):
    v = gc.sum(0)[:, 0, :]
    if gr is not None:
        v = v + gr.sum(0)[:, :, 0]
    return v[:, :HPG].reshape(HEADS), v[:, HPG:].reshape(HEADS)


def local_step(x, target, w):
    bsz, seq, d = x.shape
    t = bsz * seq
    x2, tgt2 = x.reshape(t, d), target.reshape(t, d)
    g = {}
    row = lambda v: v.reshape(1, -1)
    bf = lambda v: v.astype(BF16)

    w_in = _unshard(bf(w['w_in']), SHARDED['w_in'])
    cuts = [0, SSD_W, SSD_W + XBC_W, SSD_W + XBC_W + 2 * HEADS, w_in.shape[1]]
    w_in_parts = [w_in[:, a:b] for a, b in zip(cuts[:-1], cuts[1:])]
    norm_mix = row(w['norm_mix_w'])
    (hn,) = rowmap_fwd(lambda a, nw: (_rms(a, nw),), [x2], [norm_mix], [(d, BF16)], name="rms_mix")
    z, xbc, dt, u = [matmul_sum([hn], [p], name=f"in_proj_{i}") for i, p in enumerate(w_in_parts)]

    conv_w, conv_b = _unshard(w['ssd_conv_w'], SHARDED['ssd_conv_w']), row(w['ssd_conv_b'])
    act = ssd_conv_fwd(xbc, conv_w, conv_b, bsz=bsz, name="ssd_conv")
    dtc, dtr = _dt_layouts(dt, bsz)
    bias_c, bias_r = _head_params(w['ssd_dt_bias_fwd'], w['ssd_dt_bias_bwd'])
    alog_c, alog_r = _head_params(w['ssd_a_log_fwd'], w['ssd_a_log_bwd'])
    dskip_c, _ = _head_params(w['ssd_d'], jnp.zeros_like(w['ssd_d']))
    pcs, prs = (bias_c, alog_c, dskip_c), (bias_r, alog_r)
    act3 = act.reshape(bsz, seq, XBC_W)
    y_scan = ssd_scan_fwd(act3, dtc, dtr, pcs, prs, name="ssd_scan").reshape(t, SSD_W)
    ssd_nw = row(w['ssd_norm_w'])
    (y_ssd,) = rowmap_fwd(_ssd_post, [y_scan, z], [ssd_nw], [(SSD_W, BF16)], name="ssd_post")

    s5_names = ['s5_lambda_re_fwd', 's5_lambda_im_fwd', 's5_log_step_fwd', 's5_lambda_re_bwd', 's5_lambda_im_bwd',
                's5_log_step_bwd', 's5_b_re', 's5_b_im', 's5_c_re_fwd', 's5_c_im_fwd', 's5_c_re_bwd', 's5_c_im_bwd']
    (tt, wt, mt, da, db), s5_pull = jax.vjp(_s5_operators, *[w[n] for n in s5_names])
    tt_b, wt_b, mt_b = bf(tt), bf(wt), bf(mt)
    da_r, db_r = jnp.repeat(da, bsz, axis=0), jnp.repeat(db, bsz, axis=0)
    uc = _to_chunks(u, bsz)
    s_in = _to_carry(s5_state_in(uc, wt_b, name="s5_state_in"), bsz)
    hin_c = s5_carry_fwd(s_in, da_r, db_r, name="s5_carry")
    hin = _from_carry(hin_c, bsz)
    ypre = _from_chunks(s5_out(uc, hin, tt_b, mt_b, name="s5_out"), bsz)
    glu_w = w['s5_glu_w']
    s5_par = [row(w['s5_d']), _block_diag(glu_w[:, :, :S5_C]), _block_diag(glu_w[:, :, S5_C:]),
              row(w['s5_glu_b'][:, :S5_C]), row(w['s5_glu_b'][:, S5_C:]), row(w['s5_norm_w'])]
    (y_s5,) = rowmap_fwd(_s5_post, [ypre, u], s5_par, [(S5_W, BF16)], name="s5_post")

    w_out = bf(w['w_out']).reshape(SSD_W + S5_W, d)
    h1 = matmul_sum([y_ssd, y_s5], [w_out[:SSD_W], w_out[SSD_W:]], add=x2, name="out_proj")
    norm_ffn = row(w['norm_ffn_w'])
    (hn2,) = rowmap_fwd(lambda a, nw: (_rms(a, nw),), [h1], [norm_ffn], [(d, BF16)], name="rms_ffn")
    pad_c = FFN_PAD - FFN_BLK
    half = N_DEV // 2
    w_up3 = jnp.pad(bf(w['ffn_w_up']), ((0, 0), (0, 0), (0, pad_c)))
    w_down = jnp.pad(bf(w['ffn_w_down']).reshape(half, FFN_BLK, d), ((0, 0), (0, pad_c), (0, 0)))
    w_down = w_down.reshape(half * FFN_PAD, d)
    fconv_w = jnp.pad(w['ffn_conv_w'], ((0, 0), (0, 0), (0, pad_c)))
    fconv_w = jnp.transpose(fconv_w, (1, 0, 2)).reshape(FCONV, N_DEV * FFN_PAD)
    fconv_b = row(jnp.pad(w['ffn_conv_b'].reshape(N_DEV, FFN_BLK), ((0, 0), (0, pad_c))))
    up = matmul_cols(hn2, w_up3, name="ffn_up")
    fact = ffn_act_fwd(up, fconv_w, fconv_b, bsz=bsz, name="ffn_act")
    h2 = matmul_sum([fact], [w_down], add=h1, name="ffn_down")
    loss, dh2, g_nf = loss_head(h2, tgt2, row(w['norm_final_w']), name="loss_head")
    g['norm_final_w'] = g_nf.reshape(-1)

    dfact = matmul_sum([dh2], [w_down], nt=True, name="ffn_down_dx")
    g_down = matmul_tn(fact, dh2, name="ffn_down_dw").reshape(half, FFN_PAD, d)[:, :FFN_BLK]
    g['ffn_w_down'] = g_down.reshape(N_DEV, FFN_BLK // 2, d)
    dval, dgate, dwv, dwg, dbv, dbg = ffn_act_bwd(up, dfact, fconv_w, fconv_b, bsz=bsz, name="ffn_act_bwd")
    g_cw = jnp.concatenate([dwv, dwg], axis=1).reshape(FCONV, N_DEV, FFN_PAD)[:, :, :FFN_BLK]
    g['ffn_conv_w'] = jnp.transpose(g_cw, (1, 0, 2))
    g['ffn_conv_b'] = jnp.concatenate([dbv, dbg], axis=1).reshape(N_DEV, FFN_PAD)[:, :FFN_BLK].reshape(-1)
    windows = [(dval, FFN_PAD, p) for p in range(half)] + [(dgate, FFN_PAD, p) for p in range(half)]
    dhn2 = matmul_sum(windows, [(w_up3, p) for p in range(N_DEV)], nt=True, name="ffn_up_dx")
    g['ffn_w_up'] = jnp.concatenate([matmul_tn(hn2, dval, out_blocks=half, name="ffn_up_dw_val"),
                                     matmul_tn(hn2, dgate, out_blocks=half, name="ffn_up_dw_gate")],
                                    axis=0)[:, :, :FFN_BLK]
    dh1, g_nffn = rowmap_bwd(lambda a, nw: (_rms(a, nw),), [h1], [norm_ffn], [dhn2], add=dh2, name="rms_ffn_bwd")
    g['norm_ffn_w'] = g_nffn.reshape(-1)

    dycat = matmul_sum([dh1], [w_out], nt=True, name="out_proj_dx")
    g['w_out'] = jnp.concatenate([matmul_tn(y_ssd, dh1, name="out_proj_dw_ssd"),
                                  matmul_tn(y_s5, dh1, name="out_proj_dw_s5")], axis=0).reshape(w['w_out'].shape)
    dy_scan, dz, g_snw = rowmap_bwd(_ssd_post, [y_scan, z], [ssd_nw], [(dycat, SSD_W, 0)], name="ssd_post_bwd")
    g['ssd_norm_w'] = g_snw.reshape(-1)
    dypre, du_a, g_d, g_wv, g_wg, g_bv, g_bg, g_s5nw = rowmap_bwd(
        _s5_post, [ypre, u], s5_par, [(dycat, S5_W, SSD_W // S5_W)], name="s5_post_bwd")
    g['s5_d'], g['s5_norm_w'] = g_d.reshape(-1), g_s5nw.reshape(-1)
    g['s5_glu_w'] = jnp.concatenate([_diag_blocks(g_wv), _diag_blocks(g_wg)], axis=-1)
    g['s5_glu_b'] = jnp.concatenate([g_bv.reshape(S5_G, S5_C), g_bg.reshape(S5_G, S5_C)], axis=-1)

    dyc = _to_chunks(dypre, bsz)
    dhin, dtt, dmt, du1 = s5_out_bwd(dyc, uc, hin, tt_b, mt_b, name="s5_out_bwd")
    ds_c, gda, gdb = s5_carry_bwd(hin_c, _to_carry(dhin, bsz), da_r, db_r, name="s5_carry_bwd")
    duc, dwt = s5_state_in_bwd(_from_carry(ds_c, bsz), uc, wt_b, du1, name="s5_state_in_bwd")
    du = du_a + _from_chunks(duc, bsz)
    fold = lambda v: v.reshape(S5_G, bsz, -1).sum(1)
    for n, gv in zip(s5_names, s5_pull((dtt, dwt, dmt, fold(gda), fold(gdb)))):
        g[n] = gv

    dxs, dbm, dcm, ddtc, ddtr, gbc, gac, gdk, gbr, gar = ssd_scan_bwd(
        act3, dtc, dtr, pcs, prs, dy_scan.reshape(bsz, seq, SSD_W), name="ssd_scan_bwd")
    g['ssd_dt_bias_fwd'], g['ssd_dt_bias_bwd'] = _head_grads(gbc, gbr)
    g['ssd_a_log_fwd'], g['ssd_a_log_bwd'] = _head_grads(gac, gar)
    g['ssd_d'] = _head_grads(gdk)[0]
    dact = jnp.concatenate([dxs, dbm, dcm], axis=-1).reshape(t, XBC_W)
    dxbc, g_cw, g_cb = ssd_conv_bwd(xbc, dact, conv_w, conv_b, bsz=bsz, name="ssd_conv_bwd")
    g['ssd_conv_w'] = _shard_rows(g_cw, SHARDED['ssd_conv_w']).reshape(w['ssd_conv_w'].shape)
    g['ssd_conv_b'] = g_cb.reshape(-1)
    ddt = _dt_from_layouts(ddtc, ddtr)

    dparts = [dz, dxbc, ddt, du]
    dhn = matmul_sum(dparts, w_in_parts, nt=True, name="in_proj_dx")
    g_in = jnp.concatenate([matmul_tn(hn, dp, name=f"in_proj_dw_{i}") for i, dp in enumerate(dparts)], axis=1)
    g['w_in'] = _shard_rows(g_in, SHARDED['w_in']).reshape(w['w_in'].shape)
    dx, g_nmix = rowmap_bwd(lambda a, nw: (_rms(a, nw),), [x2], [norm_mix], [dhn], add=dh1, name="rms_mix_bwd")
    g['norm_mix_w'] = g_nmix.reshape(-1)
    return loss, dx.reshape(bsz, seq, d), g


ANY = pl.BlockSpec(memory_space=pl.ANY)


def all_gather(shard, *, name):
    m_per, n = shard.shape

    def body(x_ref, out_ref, send_sems, recv_sems, local_sem):
        x, y, c = lax.axis_index("x"), lax.axis_index("y"), lax.axis_index("c")
        me, sibling = (x, y, c), (x, y, 1 - c)
        chips = [(1 - x, y), (x, 1 - y), (1 - x, 1 - y)]

        def rows(px, py, pc):
            return out_ref.at[pl.ds((4 * px + 2 * py + pc) * m_per, m_per), :]

        def copy(k, block, to, src=None):
            return pltpu.make_async_remote_copy(
                src_ref=rows(*block) if src is None else src, dst_ref=rows(*block),
                send_sem=send_sems.at[k], recv_sem=recv_sems.at[k], device_id=to, device_id_type=MESH)

        mine = pltpu.make_async_copy(x_ref, rows(*me), local_sem)
        mine.start()
        first = [copy(0, me, sibling, src=x_ref)]
        first += [copy(1 + j, me, (*chip, c), src=x_ref) for j, chip in enumerate(chips)]
        for cp in first:
            cp.start()
        passed = [copy(4 + j, (*chip, c), sibling) for j, chip in enumerate(chips)]
        for j, chip in enumerate(chips):
            copy(1 + j, (*chip, c), me).wait_recv()
            passed[j].start()
        copy(0, sibling, me).wait_recv()
        for j, chip in enumerate(chips):
            copy(4 + j, (*chip, 1 - c), me).wait_recv()
        for cp in first + passed:
            cp.wait_send()
        mine.wait()

    return pl.pallas_call(
        body, name=name, out_shape=jax.ShapeDtypeStruct((N_DEV * m_per, n), shard.dtype),
        in_specs=[ANY], out_specs=ANY,
        scratch_shapes=[pltpu.SemaphoreType.DMA((7,)), pltpu.SemaphoreType.DMA((7,)), pltpu.SemaphoreType.DMA],
    )(shard)


def exchange(sends, *, name):
    n = len(sends)

    def body(*refs):
        send_refs, recv_refs = refs[:n], refs[n:2 * n]
        send_sems, recv_sems, local_sems = refs[2 * n:]
        x, y, c = lax.axis_index("x"), lax.axis_index("y"), lax.axis_index("c")
        me = 4 * x + 2 * y + c
        local = [pltpu.make_async_copy(send_refs[j].at[me], recv_refs[j].at[me], local_sems.at[j]) for j in range(n)]
        for cp in local:
            cp.start()
        copies = []
        for k in range(1, N_DEV):
            px = (1 - x) if k & 4 else x
            py = (1 - y) if k & 2 else y
            pc = (1 - c) if k & 1 else c
            for j in range(n):
                copies.append(pltpu.make_async_remote_copy(
                    src_ref=send_refs[j].at[4 * px + 2 * py + pc], dst_ref=recv_refs[j].at[me],
                    send_sem=send_sems.at[k - 1, j], recv_sem=recv_sems.at[k - 1, j],
                    device_id=(px, py, pc), device_id_type=MESH))
        for cp in copies:
            cp.start()
        for cp in copies:
            cp.wait()
        for cp in local:
            cp.wait()

    return pl.pallas_call(
        body, name=name, out_shape=[jax.ShapeDtypeStruct(s.shape, s.dtype) for s in sends],
        in_specs=[ANY] * n, out_specs=[ANY] * n,
        scratch_shapes=[pltpu.SemaphoreType.DMA((N_DEV - 1, n)), pltpu.SemaphoreType.DMA((N_DEV - 1, n)),
                        pltpu.SemaphoreType.DMA((n,))],
    )(*sends)


def _adam_rows(r, c):
    fits = [t for t in range(8, r + 1, 8) if r % t == 0 and N_DEV * t * c * 4 <= 6 * 2 ** 20]
    return max(fits) if fits else r


def adamw(recv, w, m, v, *, name):
    _, r, n = recv.shape
    tr = _adam_rows(r, n)

    def body(r_ref, w_ref, m_ref, v_ref, g_ref, d_ref, nm_ref, nv_ref):
        g = r_ref[0]
        for s in range(1, N_DEV):
            g = g + r_ref[s]
        m_new = ADAM_B1 * m_ref[...] + (1.0 - ADAM_B1) * g
        v_new = ADAM_B2 * v_ref[...] + (1.0 - ADAM_B2) * jnp.square(g)
        m_hat = m_new / (1.0 - ADAM_B1 ** ADAM_STEP)
        v_hat = v_new / (1.0 - ADAM_B2 ** ADAM_STEP)
        g_ref[...] = g
        d_ref[...] = -ADAM_LR * (m_hat / (jnp.sqrt(v_hat) + ADAM_EPS) + ADAM_WD * w_ref[...])
        nm_ref[...] = m_new
        nv_ref[...] = v_new

    blk = pl.BlockSpec((tr, n), lambda i: (i, 0))
    return pl.pallas_call(
        body, name=name, grid=(r // tr,), in_specs=[pl.BlockSpec((N_DEV, tr, n), lambda i: (0, i, 0)), blk, blk, blk],
        out_specs=[blk] * 4, out_shape=[jax.ShapeDtypeStruct((r, n), F32)] * 4,
        compiler_params=_params(("parallel",)))(recv, w, m, v)


def _shard_rows(full, axis):
    if axis == 0:
        return full.reshape(N_DEV, -1)
    r, c = full.shape
    return jnp.transpose(full.reshape(r, N_DEV, c // N_DEV), (1, 0, 2)).reshape(N_DEV, -1)


def _unshard(blocks, axis):
    if axis == 0:
        return blocks.reshape(-1, blocks.shape[-1])
    return jnp.transpose(blocks, (1, 0, 2)).reshape(blocks.shape[1], -1)


def kernel(x, norm_mix_w, w_in, ssd_conv_w, ssd_conv_b, ssd_dt_bias_fwd, ssd_dt_bias_bwd, ssd_a_log_fwd, ssd_a_log_bwd, ssd_d, ssd_norm_w, s5_lambda_re_fwd, s5_lambda_im_fwd, s5_log_step_fwd, s5_lambda_re_bwd, s5_lambda_im_bwd, s5_log_step_bwd, s5_b_re, s5_b_im, s5_c_re_fwd, s5_c_im_fwd, s5_c_re_bwd, s5_c_im_bwd, s5_d, s5_glu_w, s5_glu_b, s5_norm_w, w_out, norm_ffn_w, ffn_w_up, ffn_conv_w, ffn_conv_b, ffn_w_down, norm_final_w, loss_target, m_norm_mix_w, m_w_in, m_ssd_conv_w, m_ssd_conv_b, m_ssd_dt_bias_fwd, m_ssd_dt_bias_bwd, m_ssd_a_log_fwd, m_ssd_a_log_bwd, m_ssd_d, m_ssd_norm_w, m_s5_lambda_re_fwd, m_s5_lambda_im_fwd, m_s5_log_step_fwd, m_s5_lambda_re_bwd, m_s5_lambda_im_bwd, m_s5_log_step_bwd, m_s5_b_re, m_s5_b_im, m_s5_c_re_fwd, m_s5_c_im_fwd, m_s5_c_re_bwd, m_s5_c_im_bwd, m_s5_d, m_s5_glu_w, m_s5_glu_b, m_s5_norm_w, m_w_out, m_norm_ffn_w, m_ffn_w_up, m_ffn_conv_w, m_ffn_conv_b, m_ffn_w_down, m_norm_final_w, v_norm_mix_w, v_w_in, v_ssd_conv_w, v_ssd_conv_b, v_ssd_dt_bias_fwd, v_ssd_dt_bias_bwd, v_ssd_a_log_fwd, v_ssd_a_log_bwd, v_ssd_d, v_ssd_norm_w, v_s5_lambda_re_fwd, v_s5_lambda_im_fwd, v_s5_log_step_fwd, v_s5_lambda_re_bwd, v_s5_lambda_im_bwd, v_s5_log_step_bwd, v_s5_b_re, v_s5_b_im, v_s5_c_re_fwd, v_s5_c_im_fwd, v_s5_c_re_bwd, v_s5_c_im_bwd, v_s5_d, v_s5_glu_w, v_s5_glu_b, v_s5_norm_w, v_w_out, v_norm_ffn_w, v_ffn_w_up, v_ffn_conv_w, v_ffn_conv_b, v_ffn_w_down, v_norm_final_w):
    args = dict(locals())
    strip = lambda n, v: v if n == 'norm_final_w' else v[0]
    w = {n: strip(n, args[n]) for n in WEIGHTS}

    mats = ['w_in', 'w_out', 'ffn_w_up', 'ffn_w_down']
    convs = ['ssd_conv_w', 'ffn_conv_w']
    bits = lambda v: lax.bitcast_convert_type(v, jnp.uint16).reshape(-1)
    flat = jnp.concatenate([bits(w[n].astype(BF16)) for n in mats] + [bits(w[n]) for n in convs])
    width = 1024
    rows = -(-flat.shape[0] // (16 * width)) * 16
    flat = jnp.pad(flat, (0, rows * width - flat.shape[0]))
    gathered = all_gather(flat.reshape(rows, width), name="weight_all_gather").reshape(N_DEV, rows * width)
    full, off = dict(w), 0
    for n in mats:
        r, c = w[n].shape
        full[n] = lax.bitcast_convert_type(gathered[:, off:off + r * c], BF16).reshape(N_DEV, r, c)
        off += r * c
    n_conv = sum(w[n].size for n in convs)
    conv_f32 = lax.bitcast_convert_type(gathered[:, off:off + 2 * n_conv].reshape(N_DEV, n_conv, 2), F32)
    off = 0
    for n in convs:
        r, c = w[n].shape
        full[n] = conv_f32[:, off:off + r * c].reshape(N_DEV, r, c)
        off += r * c

    loss, grad_x, g = local_step(x, loss_target, full)

    small = convs + [n for n in WEIGHTS if n not in SHARDED]
    pieces = [g[n].reshape(N_DEV, -1) if n in SHARDED else jnp.broadcast_to(g[n].reshape(1, -1), (N_DEV, g[n].size))
              for n in small]
    pieces.append(jnp.broadcast_to(loss.reshape(1, 1), (N_DEV, 1)))
    total = sum(p.shape[1] for p in pieces)
    nrow = -(-total // (PACK_ROWS * LANES)) * PACK_ROWS
    pieces.append(jnp.zeros((N_DEV, nrow * LANES - total), F32))
    packed_send = jnp.concatenate(pieces, axis=1).reshape(N_DEV, nrow, LANES)
    recvs = exchange([g[n] for n in mats] + [packed_send], name="grad_exchange")

    outs = [{}, {}, {}, {}]
    for n, recv in zip(mats, recvs):
        res = adamw(recv, w[n], strip(n, args['m_' + n]), strip(n, args['v_' + n]), name="adamw_" + n)
        for o, p in zip(outs, res):
            o[n] = p.reshape(args[n].shape)

    def pack(prefix):
        vals = [strip(n, args[prefix + n]).reshape(-1) for n in small]
        return jnp.pad(jnp.concatenate(vals), (0, nrow * LANES - total + 1)).reshape(nrow, LANES)

    packed = adamw(recvs[-1], pack(''), pack('m_'), pack('v_'), name="adamw_small")
    packed = [p.reshape(-1) for p in packed]
    off = 0
    for n in small:
        size = w[n].size
        for o, p in zip(outs, packed):
            o[n] = p[off:off + size].reshape(args[n].shape)
        off += size
    loss_out = packed[0][off].reshape(())
    return (loss_out, grad_x, *[o[n] for o in outs for n in WEIGHTS])
```

```python
import functools

import jax
import jax.numpy as jnp
import numpy as np
from jax import lax
from jax.experimental import pallas as pl
from jax.experimental.pallas import tpu as pltpu

F32, BF16 = jnp.float32, jnp.bfloat16
N_DEV = 8
D_MODEL = 1024
SSD_W, HEADS, HDIM, SGROUPS, HPG, NSTATE, SCONV, QC = 1024, 16, 64, 4, 4, 128, 5, 128
XBC_W = SSD_W + 2 * SGROUPS * NSTATE
S5_W, S5_G, S5_C, S5_P, S5_Q = 512, 32, 16, 64, 16
S5_QC = S5_Q * S5_C
CARRY_ROWS = 32
DFF, FCONV = 2816, 3
FFN_BLK, FFN_PAD = 704, 768
EPS = 1e-6
ADAM_LR, ADAM_B1, ADAM_B2, ADAM_EPS, ADAM_WD, ADAM_STEP = 0.001, 0.9, 0.999, 1e-08, 0.01, 10
LANES = 128
MESH = pl.DeviceIdType.MESH

WEIGHTS = ['norm_mix_w', 'w_in', 'ssd_conv_w', 'ssd_conv_b', 'ssd_dt_bias_fwd', 'ssd_dt_bias_bwd', 'ssd_a_log_fwd',
           'ssd_a_log_bwd', 'ssd_d', 'ssd_norm_w', 's5_lambda_re_fwd', 's5_lambda_im_fwd', 's5_log_step_fwd',
           's5_lambda_re_bwd', 's5_lambda_im_bwd', 's5_log_step_bwd', 's5_b_re', 's5_b_im', 's5_c_re_fwd', 's5_c_im_fwd',
           's5_c_re_bwd', 's5_c_im_bwd', 's5_d', 's5_glu_w', 's5_glu_b', 's5_norm_w', 'w_out', 'norm_ffn_w', 'ffn_w_up',
           'ffn_conv_w', 'ffn_conv_b', 'ffn_w_down', 'norm_final_w']
SHARDED = {'w_in': 1, 'ssd_conv_w': 1, 'w_out': 0, 'ffn_w_up': 1, 'ffn_conv_w': 1, 'ffn_w_down': 0}
FULL_SHAPE = {'w_in': (1024, 3616), 'ssd_conv_w': (5, 2048), 'w_out': (1536, 1024), 'ffn_w_up': (1024, 5632),
              'ffn_conv_w': (3, 5632), 'ffn_w_down': (2816, 1024)}
PACK_ROWS = 512


def _pick(n, cap=1536):
    if n <= cap:
        return n
    return max(t for t in range(LANES, cap + 1, LANES) if n % t == 0)


def _params(sem):
    return pltpu.CompilerParams(dimension_semantics=sem)


def _bd(a, b, ca, cb):
    return lax.dot_general(a.astype(BF16), b.astype(BF16), (((ca,), (cb,)), ((), ())), preferred_element_type=F32)


@jax.custom_vjp
def dot_nn(a, b):
    return _bd(a, b, 1, 0)


dot_nn.defvjp(lambda a, b: (_bd(a, b, 1, 0), (a, b)),
              lambda r, g: (_bd(g, r[1], 1, 1).astype(r[0].dtype), _bd(r[0], g, 0, 0).astype(r[1].dtype)))


@jax.custom_vjp
def dot_nt(a, b):
    return _bd(a, b, 1, 1)


dot_nt.defvjp(lambda a, b: (_bd(a, b, 1, 1), (a, b)),
              lambda r, g: (_bd(g, r[1], 1, 0).astype(r[0].dtype), _bd(g, r[0], 0, 0).astype(r[1].dtype)))


@jax.custom_vjp
def dot_tn(a, b):
    return _bd(a, b, 0, 0)


dot_tn.defvjp(lambda a, b: (_bd(a, b, 0, 0), (a, b)),
              lambda r, g: (_bd(r[1], g, 1, 1).astype(r[0].dtype), _bd(r[0], g, 1, 0).astype(r[1].dtype)))


def _split3(x):
    hi = x.astype(BF16)
    r = x - hi.astype(F32)
    mid = r.astype(BF16)
    lo = (r - mid.astype(F32)).astype(BF16)
    return hi, mid, lo


def _cum_matrix(q, upper):
    ri = lax.broadcasted_iota(jnp.int32, (q, q), 0)
    ci = lax.broadcasted_iota(jnp.int32, (q, q), 1)
    return jnp.where((ci >= ri) if upper else (ci <= ri), 1.0, 0.0).astype(BF16)


def _exact_left(mat, x):
    return sum(jnp.dot(mat, p, preferred_element_type=F32) for p in _split3(x))


def _exact_right(x, mat):
    return sum(jnp.dot(p, mat, preferred_element_type=F32) for p in _split3(x))


@functools.partial(jax.custom_vjp, nondiff_argnums=(1,))
def cum_col(x, rev):
    return _exact_left(_cum_matrix(x.shape[0], rev), x)


cum_col.defvjp(lambda x, rev: (cum_col(x, rev), None),
               lambda rev, _, g: (_exact_left(_cum_matrix(g.shape[0], not rev), g),))


@functools.partial(jax.custom_vjp, nondiff_argnums=(1,))
def cum_row(x, rev):
    return _exact_right(x, _cum_matrix(x.shape[1], not rev))


cum_row.defvjp(lambda x, rev: (cum_row(x, rev), None),
               lambda rev, _, g: (_exact_right(g, _cum_matrix(g.shape[1], rev)),))


def _softplus(x):
    return jnp.maximum(x, 0.0) + jnp.log(1.0 + jnp.exp(-jnp.abs(x)))


def _silu(x):
    return x * jax.nn.sigmoid(x)


def _gelu(x):
    return 0.5 * x * (1.0 + jnp.tanh(0.7978845608028654 * (x + 0.044715 * (x * x * x))))


def _rms(x, w):
    xf = x.astype(F32)
    return xf * lax.rsqrt(jnp.mean(xf * xf, axis=-1, keepdims=True) + EPS) * w


def matmul_sum(a_list, b_list, *, name, out_dtype=F32, add=None, tm=512, nt=False):
    a_arrs = [a[0] if isinstance(a, tuple) else a for a in a_list]
    b_arrs = [b[0] if isinstance(b, tuple) else b for b in b_list]
    m, n = a_arrs[0].shape[0], b_arrs[0].shape[-2 if nt else -1]
    tm, tn, k = min(tm, m), _pick(n), len(a_list)

    def body(*refs):
        acc = None
        for a_ref, b_ref in zip(refs[:k], refs[k:2 * k]):
            p = _bd(a_ref[...], b_ref[...], 1, 1 if nt else 0)
            acc = p if acc is None else acc + p
        if add is not None:
            acc = acc + refs[2 * k][...]
        refs[-1][...] = acc.astype(out_dtype)

    def a_spec(a):
        if isinstance(a, tuple):
            return pl.BlockSpec((tm, a[1]), lambda i, j, blk=a[2]: (i, blk))
        return pl.BlockSpec((tm, a.shape[1]), lambda i, j: (i, 0))

    def b_spec(b):
        arr, p = b if isinstance(b, tuple) else (b, None)
        kk = arr.shape[-1 if nt else -2]
        shape, idx = ((tn, kk), lambda j: (j, 0)) if nt else ((kk, tn), lambda j: (0, j))
        if p is None:
            return pl.BlockSpec(shape, lambda i, j: idx(j))
        return pl.BlockSpec((None,) + shape, lambda i, j, p=p: (p,) + idx(j))

    in_specs = [a_spec(a) for a in a_list] + [b_spec(b) for b in b_list]
    args = a_arrs + b_arrs
    if add is not None:
        in_specs.append(pl.BlockSpec((tm, tn), lambda i, j: (i, j)))
        args.append(add)
    return pl.pallas_call(
        body, name=name, grid=(m // tm, n // tn), in_specs=in_specs,
        out_specs=pl.BlockSpec((tm, tn), lambda i, j: (i, j)),
        out_shape=jax.ShapeDtypeStruct((m, n), out_dtype),
        compiler_params=_params(("parallel", "parallel")))(*args)


def matmul_cols(a, b3, *, name, out_dtype=F32, tm=512):
    m, kk = a.shape
    p, _, nb = b3.shape
    tm, tn = min(tm, m), _pick(nb, 768)
    per = nb // tn

    def body(a_ref, b_ref, o_ref):
        o_ref[...] = _bd(a_ref[...], b_ref[...], 1, 0).astype(out_dtype)

    return pl.pallas_call(
        body, name=name, grid=(m // tm, p * per),
        in_specs=[pl.BlockSpec((tm, kk), lambda i, j: (i, 0)),
                  pl.BlockSpec((None, kk, tn), lambda i, j: (j // per, 0, j % per))],
        out_specs=pl.BlockSpec((tm, tn), lambda i, j: (i, j)),
        out_shape=jax.ShapeDtypeStruct((m, p * nb), out_dtype),
        compiler_params=_params(("parallel", "parallel")))(a, b3)


def matmul_tn(a, b, *, name, tm=512, out_blocks=None):
    m, k = a.shape
    n = b.shape[1]
    nb = n // (out_blocks or 1)
    tm, tk, tn = min(tm, m), _pick(k), _pick(nb, 768 if out_blocks else 1536)
    per = nb // tn

    def body(a_ref, b_ref, o_ref):
        @pl.when(pl.program_id(2) == 0)
        def _():
            o_ref[...] = jnp.zeros_like(o_ref)

        o_ref[...] += _bd(a_ref[...], b_ref[...], 0, 0)

    if out_blocks:
        out_spec = pl.BlockSpec((None, tk, tn), lambda i, j, t: (j // per, i, j % per))
        out_shape = jax.ShapeDtypeStruct((out_blocks, k, nb), F32)
    else:
        out_spec = pl.BlockSpec((tk, tn), lambda i, j, t: (i, j))
        out_shape = jax.ShapeDtypeStruct((k, n), F32)
    return pl.pallas_call(
        body, name=name, grid=(k // tk, n // tn, m // tm),
        in_specs=[pl.BlockSpec((tm, tk), lambda i, j, t: (t, i)), pl.BlockSpec((tm, tn), lambda i, j, t: (t, j))],
        out_specs=out_spec, out_shape=out_shape,
        compiler_params=_params(("parallel", "parallel", "arbitrary")))(a, b)


def _row_spec(r, tm):
    if isinstance(r, tuple):
        arr, width, blk = r
        return arr, pl.BlockSpec((tm, width), lambda i, blk=blk: (i, blk))
    return r, pl.BlockSpec((tm, r.shape[1]), lambda i: (i, 0))


def _full_spec(p):
    return pl.BlockSpec(p.shape, lambda i: (0,) * p.ndim)


def rowmap_fwd(fn, rows, params, outs, *, name, tm=256):
    pairs = [_row_spec(r, tm) for r in rows]
    m = pairs[0][0].shape[0]
    tm = min(tm, m)
    pairs = [_row_spec(r, tm) for r in rows]
    nr, npar = len(rows), len(params)

    def body(*refs):
        res = fn(*[r[...] for r in refs[:nr + npar]])
        for o_ref, v in zip(refs[nr + npar:], res):
            o_ref[...] = v.astype(o_ref.dtype)

    return pl.pallas_call(
        body, name=name, grid=(m // tm,),
        in_specs=[s for _, s in pairs] + [_full_spec(p) for p in params],
        out_specs=[pl.BlockSpec((tm, c), lambda i: (i, 0)) for c, _ in outs],
        out_shape=[jax.ShapeDtypeStruct((m, c), dt) for c, dt in outs],
        compiler_params=_params(("parallel",)))(*[a for a, _ in pairs], *params)


def rowmap_bwd(fn, rows, params, cts, *, name, row_dtypes=None, add=None, tm=256):
    m = _row_spec(rows[0], tm)[0].shape[0]
    tm = min(tm, m)
    rp = [_row_spec(r, tm) for r in rows]
    cp = [_row_spec(c, tm) for c in cts]
    nr, npar, nc = len(rows), len(params), len(cts)
    row_dtypes = row_dtypes or [F32] * nr
    widths = [s.block_shape[1] for _, s in rp]

    def body(*refs):
        ins = [r[...] for r in refs[:nr + npar]]
        ins = [v.astype(F32) for v in ins]
        ct = tuple(r[...].astype(F32) for r in refs[nr + npar:nr + npar + nc])
        base = nr + npar + nc
        extra = None
        if add is not None:
            extra = refs[base][...]
            base += 1
        _, pull = jax.vjp(fn, *ins)
        grads = pull(ct)
        for j in range(nr):
            g = grads[j]
            if j == 0 and extra is not None:
                g = g + extra
            refs[base + j][...] = g.astype(refs[base + j].dtype)

        @pl.when(pl.program_id(0) == 0)
        def _():
            for j in range(npar):
                refs[base + nr + j][...] = jnp.zeros_like(refs[base + nr + j])

        for j in range(npar):
            refs[base + nr + j][...] += grads[nr + j]

    in_specs = [s for _, s in rp] + [_full_spec(p) for p in params] + [s for _, s in cp]
    args = [a for a, _ in rp] + list(params) + [a for a, _ in cp]
    if add is not None:
        in_specs.append(pl.BlockSpec((tm, widths[0]), lambda i: (i, 0)))
        args.append(add)
    out_specs = [pl.BlockSpec((tm, w), lambda i: (i, 0)) for w in widths] + [_full_spec(p) for p in params]
    out_shape = [jax.ShapeDtypeStruct((m, w), dt) for w, dt in zip(widths, row_dtypes)]
    out_shape += [jax.ShapeDtypeStruct(p.shape, F32) for p in params]
    return pl.pallas_call(
        body, name=name, grid=(m // tm,), in_specs=in_specs, out_specs=out_specs, out_shape=out_shape,
        compiler_params=_params(("arbitrary",)))(*args)


def loss_head(h, target, w, *, name, tm=256):
    m, d = h.shape
    tm = min(tm, m)

    def body(h_ref, t_ref, w_ref, loss_ref, dh_ref, dw_ref):
        y, pull = jax.vjp(_rms, h_ref[...], w_ref[...])
        err = y - t_ref[...]
        dh, dw = pull(err * (1.0 / d))

        @pl.when(pl.program_id(0) == 0)
        def _():
            loss_ref[...] = jnp.zeros_like(loss_ref)
            dw_ref[...] = jnp.zeros_like(dw_ref)

        loss_ref[...] += (0.5 / d) * jnp.sum(err * err, keepdims=True)
        dw_ref[...] += dw
        dh_ref[...] = dh

    row = pl.BlockSpec((tm, d), lambda i: (i, 0))
    return pl.pallas_call(
        body, name=name, grid=(m // tm,), in_specs=[row, row, _full_spec(w)],
        out_specs=[pl.BlockSpec((1, 1), lambda i: (0, 0)), row, _full_spec(w)],
        out_shape=[jax.ShapeDtypeStruct((1, 1), F32), jax.ShapeDtypeStruct((m, d), F32),
                   jax.ShapeDtypeStruct(w.shape, F32)],
        compiler_params=_params(("arbitrary",)))(h, target, w)


def _shift(x, s):
    if s == 0:
        return x
    n = x.shape[0]
    t = lax.broadcasted_iota(jnp.int32, x.shape, 0)
    rolled = pltpu.roll(x, (-s) % n, 0)
    return jnp.where((t + s >= 0) & (t + s < n), rolled, 0.0)


def _conv(x, w, b):
    k = w.shape[0]
    acc = b + w[k // 2:k // 2 + 1, :] * x
    for j in range(k):
        if j != k // 2:
            acc = acc + w[j:j + 1, :] * _shift(x, j - k // 2)
    return acc


def _conv_bwd(x, dc, w):
    k = w.shape[0]
    dx = None
    dws = []
    for j in range(k):
        s = j - k // 2
        term = w[j:j + 1, :] * _shift(dc, -s)
        dx = term if dx is None else dx + term
        dws.append(jnp.sum(dc * _shift(x, s), axis=0, keepdims=True))
    return dx, jnp.concatenate(dws, axis=0), jnp.sum(dc, axis=0, keepdims=True)


def _dsilu(c):
    s = jax.nn.sigmoid(c)
    return s * (1.0 + c * (1.0 - s))


def ssd_conv_fwd(xbc, w, b, *, bsz, name):
    t, c = xbc.shape
    seq, ct = t // bsz, 256

    def body(x_ref, w_ref, b_ref, o_ref):
        o_ref[...] = _silu(_conv(x_ref[...], w_ref[...], b_ref[...]))

    return pl.pallas_call(
        body, name=name, grid=(c // ct, bsz),
        in_specs=[pl.BlockSpec((seq, ct), lambda j, i: (i, j)), pl.BlockSpec((w.shape[0], ct), lambda j, i: (0, j)),
                  pl.BlockSpec((1, ct), lambda j, i: (0, j))],
        out_specs=pl.BlockSpec((seq, ct), lambda j, i: (i, j)),
        out_shape=jax.ShapeDtypeStruct((t, c), F32),
        compiler_params=_params(("parallel", "parallel")))(xbc, w, b)


def ssd_conv_bwd(xbc, dact, w, b, *, bsz, name):
    t, c = xbc.shape
    seq, ct, k = t // bsz, 256, w.shape[0]

    def body(x_ref, g_ref, w_ref, b_ref, dx_ref, dw_ref, db_ref):
        x, wv = x_ref[...], w_ref[...]
        dc = g_ref[...] * _dsilu(_conv(x, wv, b_ref[...]))
        dx, dw, db = _conv_bwd(x, dc, wv)
        dx_ref[...] = dx

        @pl.when(pl.program_id(1) == 0)
        def _():
            dw_ref[...] = jnp.zeros_like(dw_ref)
            db_ref[...] = jnp.zeros_like(db_ref)

        dw_ref[...] += dw
        db_ref[...] += db

    blk = pl.BlockSpec((seq, ct), lambda j, i: (i, j))
    wspec, bspec = pl.BlockSpec((k, ct), lambda j, i: (0, j)), pl.BlockSpec((1, ct), lambda j, i: (0, j))
    return pl.pallas_call(
        body, name=name, grid=(c // ct, bsz), in_specs=[blk, blk, wspec, bspec], out_specs=[blk, wspec, bspec],
        out_shape=[jax.ShapeDtypeStruct((t, c), F32), jax.ShapeDtypeStruct((k, c), F32),
                   jax.ShapeDtypeStruct((1, c), F32)],
        compiler_params=_params(("parallel", "arbitrary")))(xbc, dact, w, b)


def _ffn_specs(seq, ct, k, nblk):
    val = pl.BlockSpec((seq, ct), lambda j, i: (i, j))
    gate = pl.BlockSpec((seq, ct), lambda j, i: (i, nblk + j))
    wv, wg = pl.BlockSpec((k, ct), lambda j, i: (0, j)), pl.BlockSpec((k, ct), lambda j, i: (0, nblk + j))
    bv, bg = pl.BlockSpec((1, ct), lambda j, i: (0, j)), pl.BlockSpec((1, ct), lambda j, i: (0, nblk + j))
    return val, gate, wv, wg, bv, bg


def ffn_act_fwd(up, w, b, *, bsz, name):
    t = up.shape[0]
    half = up.shape[1] // 2
    seq, ct, k = t // bsz, 256, w.shape[0]
    val, gate, wv, wg, bv, bg = _ffn_specs(seq, ct, k, half // ct)

    def body(v_ref, g_ref, wv_ref, wg_ref, bv_ref, bg_ref, o_ref):
        vc = _conv(v_ref[...], wv_ref[...], bv_ref[...])
        gc = _conv(g_ref[...], wg_ref[...], bg_ref[...])
        o_ref[...] = (_silu(gc) * vc).astype(BF16)

    return pl.pallas_call(
        body, name=name, grid=(half // ct, bsz), in_specs=[val, gate, wv, wg, bv, bg], out_specs=val,
        out_shape=jax.ShapeDtypeStruct((t, half), BF16),
        compiler_params=_params(("parallel", "parallel")))(up, up, w, w, b, b)


def ffn_act_bwd(up, dact, w, b, *, bsz, name):
    t = up.shape[0]
    half = up.shape[1] // 2
    seq, ct, k = t // bsz, 256, w.shape[0]
    val, gate, wv, wg, bv, bg = _ffn_specs(seq, ct, k, half // ct)

    def body(v_ref, g_ref, wv_ref, wg_ref, bv_ref, bg_ref, d_ref, dv_ref, dg_ref, dwv_ref, dwg_ref, dbv_ref, dbg_ref):
        v, g = v_ref[...], g_ref[...]
        vc = _conv(v, wv_ref[...], bv_ref[...])
        gc = _conv(g, wg_ref[...], bg_ref[...])
        d = d_ref[...].astype(F32)
        dv, dwv, dbv = _conv_bwd(v, d * _silu(gc), wv_ref[...])
        dg, dwg, dbg = _conv_bwd(g, d * vc * _dsilu(gc), wg_ref[...])
        dv_ref[...] = dv.astype(BF16)
        dg_ref[...] = dg.astype(BF16)

        @pl.when(pl.program_id(1) == 0)
        def _():
            for r in (dwv_ref, dwg_ref, dbv_ref, dbg_ref):
                r[...] = jnp.zeros_like(r)

        dwv_ref[...] += dwv
        dwg_ref[...] += dwg
        dbv_ref[...] += dbv
        dbg_ref[...] += dbg

    return pl.pallas_call(
        body, name=name, grid=(half // ct, bsz), in_specs=[val, gate, wv, wg, bv, bg, val],
        out_specs=[val, val, wv, wv, bv, bv],
        out_shape=[jax.ShapeDtypeStruct((t, half), BF16), jax.ShapeDtypeStruct((t, half), BF16),
                   jax.ShapeDtypeStruct((k, half), F32), jax.ShapeDtypeStruct((k, half), F32),
                   jax.ShapeDtypeStruct((1, half), F32), jax.ShapeDtypeStruct((1, half), F32)],
        compiler_params=_params(("parallel", "arbitrary")))(up, up, w, w, b, b, dact)


def _sel_col(a, h):
    oh = (lax.broadcasted_iota(jnp.int32, (1, a.shape[1]), 1) == h).astype(F32)
    return jnp.sum(a * oh, axis=1, keepdims=True)


def _sel_row(a, h):
    oh = (lax.broadcasted_iota(jnp.int32, (a.shape[0], 1), 0) == h).astype(F32)
    return jnp.sum(a * oh, axis=0, keepdims=True)


def _ssd_chunk(xh, dtc, dtr, bm, cm, prev, bias_c, alog_c, dskip_c, bias_r, alog_r, rev):
    q = dtc.shape[0]
    ri = lax.broadcasted_iota(jnp.int32, (q, q), 0)
    ci = lax.broadcasted_iota(jnp.int32, (q, q), 1)
    mask = (ci >= ri) if rev else (ci <= ri)
    dt_c = _softplus(dtc + bias_c)
    dta_c = dt_c * (-jnp.exp(alog_c))
    cs_c = cum_col(dta_c, rev)
    dta_r = _softplus(dtr + bias_r) * (-jnp.exp(alog_r))
    cs_r = cum_row(dta_r, rev)
    scores = dot_nt(cm, bm)
    ys, news = [], []
    for h in range(HPG):
        hh = h + (HPG if rev else 0)
        csq = _sel_col(cs_c, hh)
        css = _sel_row(cs_r, hh)
        seg = jnp.exp(jnp.where(mask, csq - css, -1e30))
        xdt = xh[h] * _sel_col(dt_c, hh)
        y = dot_nn(scores * seg, xdt)
        tot = jnp.sum(_sel_col(dta_c, hh), axis=0, keepdims=True)
        y = y + dot_nt(cm, prev[h]) * jnp.exp(csq)
        if not rev:
            y = y + _sel_col(dskip_c, hh) * xh[h]
        ys.append(y)
        news.append(jnp.exp(tot) * prev[h] + dot_tn(xdt * jnp.exp(tot - csq), bm))
    return tuple(ys), tuple(news)


def _ssd_specs(seq):
    xs = pl.BlockSpec((None, seq, HPG * HDIM), lambda b, g: (b, 0, g))
    bm = pl.BlockSpec((None, seq, NSTATE), lambda b, g: (b, 0, SSD_W // NSTATE + g))
    cm = pl.BlockSpec((None, seq, NSTATE), lambda b, g: (b, 0, SSD_W // NSTATE + SGROUPS + g))
    dtc = pl.BlockSpec((None, None, seq, 2 * HPG), lambda b, g: (b, g, 0, 0))
    dtr = pl.BlockSpec((None, None, 2 * HPG, seq), lambda b, g: (b, g, 0, 0))
    pc = pl.BlockSpec((None, 1, 2 * HPG), lambda b, g: (g, 0, 0))
    pr = pl.BlockSpec((None, 2 * HPG, 1), lambda b, g: (g, 0, 0))
    return xs, bm, cm, dtc, dtr, pc, pr


def _head_cols(h):
    return slice(HDIM * h, HDIM * (h + 1))


def ssd_scan_fwd(act, dtc, dtr, pcs, prs, *, name):
    bsz, seq, _ = act.shape
    nc = seq // QC
    xs, bm, cm, dtcs, dtrs, pc, pr = _ssd_specs(seq)

    def body(x_ref, b_ref, c_ref, dtc_ref, dtr_ref, bc_ref, ac_ref, dk_ref, br_ref, ar_ref, y_ref):
        par = (bc_ref[...], ac_ref[...], dk_ref[...], br_ref[...], ar_ref[...])
        y_ref[...] = jnp.zeros_like(y_ref)

        def step(i, carry):
            new = []
            for rev in (False, True):
                rows = pl.ds(pl.multiple_of(((nc - 1 - i) if rev else i) * QC, QC), QC)
                xh = tuple(x_ref[rows, _head_cols(h)] for h in range(HPG))
                ys, nw = _ssd_chunk(xh, dtc_ref[rows, :], dtr_ref[:, rows], b_ref[rows, :], c_ref[rows, :],
                                    carry[rev], *par, rev)
                for h in range(HPG):
                    y_ref[rows, _head_cols(h)] += ys[h]
                new.append(nw)
            return tuple(new)

        zero = tuple(jnp.zeros((HDIM, NSTATE), F32) for _ in range(HPG))
        lax.fori_loop(0, nc, step, (zero, zero))

    return pl.pallas_call(
        body, name=name, grid=(bsz, SGROUPS), in_specs=[xs, bm, cm, dtcs, dtrs, pc, pc, pc, pr, pr], out_specs=xs,
        out_shape=jax.ShapeDtypeStruct((bsz, seq, SSD_W), F32),
        compiler_params=_params(("parallel", "parallel")))(act, act, act, dtc, dtr, *pcs, *prs)


def ssd_scan_bwd(act, dtc, dtr, pcs, prs, dy, *, name):
    bsz, seq, _ = act.shape
    nc = seq // QC
    xs, bm, cm, dtcs, dtrs, pc, pr = _ssd_specs(seq)
    grp = pl.BlockSpec((None, seq, NSTATE), lambda b, g: (b, 0, g))
    dpc = pl.BlockSpec((None, None, 1, 2 * HPG), lambda b, g: (b, g, 0, 0))
    dpr = pl.BlockSpec((None, None, 2 * HPG, 1), lambda b, g: (b, g, 0, 0))

    def body(x_ref, b_ref, c_ref, dtc_ref, dtr_ref, bc_ref, ac_ref, dk_ref, br_ref, ar_ref, dy_ref,
             dx_ref, db_ref, dc_ref, ddtc_ref, ddtr_ref, gbc_ref, gac_ref, gdk_ref, gbr_ref, gar_ref, st_ref):
        par = (bc_ref[...], ac_ref[...], dk_ref[...], br_ref[...], ar_ref[...])
        pgrads = (gbc_ref, gac_ref, gdk_ref, gbr_ref, gar_ref)
        for r in pgrads + (dx_ref, db_ref, dc_ref, ddtc_ref, ddtr_ref):
            r[...] = jnp.zeros_like(r)
        zero = tuple(jnp.zeros((HDIM, NSTATE), F32) for _ in range(HPG))

        def load(k):
            rows = pl.ds(pl.multiple_of(k * QC, QC), QC)
            xh = tuple(x_ref[rows, _head_cols(h)] for h in range(HPG))
            return rows, xh, dtc_ref[rows, :], dtr_ref[:, rows], b_ref[rows, :], c_ref[rows, :]

        def fstep(i, carry):
            new = []
            for rev in (False, True):
                k = (nc - 1 - i) if rev else i
                _, xh, a, b, c, d = load(k)
                for h in range(HPG):
                    st_ref[int(rev), k, h] = carry[rev][h]
                new.append(_ssd_chunk(xh, a, b, c, d, carry[rev], *par, rev)[1])
            return tuple(new)

        lax.fori_loop(0, nc, fstep, (zero, zero))

        def bstep(i, dcarry):
            new = []
            for rev in (False, True):
                k = i if rev else (nc - 1 - i)
                rows, xh, a, b, c, d = load(k)
                prev = tuple(st_ref[int(rev), k, h] for h in range(HPG))
                _, pull = jax.vjp(functools.partial(_ssd_chunk, rev=rev), xh, a, b, c, d, prev, *par)
                dyh = tuple(dy_ref[rows, _head_cols(h)] for h in range(HPG))
                gx, ga, gb, gc, gd, gprev, *gpar = pull((dyh, dcarry[rev]))
                for h in range(HPG):
                    dx_ref[rows, _head_cols(h)] += gx[h]
                ddtc_ref[rows, :] += ga
                ddtr_ref[:, rows] += gb
                db_ref[rows, :] += gc
                dc_ref[rows, :] += gd
                for r, g in zip(pgrads, gpar):
                    r[...] += g
                new.append(gprev)
            return tuple(new)

        lax.fori_loop(0, nc, bstep, (zero, zero))

    out_shape = [jax.ShapeDtypeStruct((bsz, seq, SSD_W), F32),
                 jax.ShapeDtypeStruct((bsz, seq, SGROUPS * NSTATE), F32),
                 jax.ShapeDtypeStruct((bsz, seq, SGROUPS * NSTATE), F32),
                 jax.ShapeDtypeStruct(dtc.shape, F32), jax.ShapeDtypeStruct(dtr.shape, F32)]
    out_shape += [jax.ShapeDtypeStruct((bsz, SGROUPS, 1, 2 * HPG), F32)] * 3
    out_shape += [jax.ShapeDtypeStruct((bsz, SGROUPS, 2 * HPG, 1), F32)] * 2
    return pl.pallas_call(
        body, name=name, grid=(bsz, SGROUPS), in_specs=[xs, bm, cm, dtcs, dtrs, pc, pc, pc, pr, pr, xs],
        out_specs=[xs, grp, grp, dtcs, dtrs, dpc, dpc, dpc, dpr, dpr], out_shape=out_shape,
        scratch_shapes=[pltpu.VMEM((2, nc, HPG, HDIM, NSTATE), F32)],
        compiler_params=_params(("parallel", "parallel")))(act, act, act, dtc, dtr, *pcs, *prs, dy)


def _s5_direction(lam_re, lam_im, log_step, b_re, b_im, c_re, c_im, rev):
    q = S5_Q
    step = jnp.exp(log_step)[:, None]
    lr, li = lam_re * step, lam_im * step
    mag = jnp.exp(lr)
    ar, ai = mag * jnp.cos(li), mag * jnp.sin(li)
    den = lam_re * lam_re + lam_im * lam_im
    cr = ((ar - 1.0) * lam_re + ai * lam_im) / den
    ci = (ai * lam_re - (ar - 1.0) * lam_im) / den
    bbr = cr[..., None] * b_re - ci[..., None] * b_im
    bbi = cr[..., None] * b_im + ci[..., None] * b_re
    d = jnp.arange(q + 1, dtype=F32)[None, :, None]
    pm = jnp.exp(d * lr[:, None, :])
    pr, pi = pm * jnp.cos(d * li[:, None, :]), pm * jnp.sin(d * li[:, None, :])
    er = pr[..., None] * bbr[:, None] - pi[..., None] * bbi[:, None]
    ei = pr[..., None] * bbi[:, None] + pi[..., None] * bbr[:, None]
    hp = lax.Precision.HIGHEST
    k = (jnp.einsum('gcp,gdpz->gdcz', c_re, er[:, :q], precision=hp)
         - jnp.einsum('gcp,gdpz->gdcz', c_im, ei[:, :q], precision=hp))
    e = jnp.concatenate([er[:, :q], ei[:, :q]], axis=2)
    wt = jnp.transpose(e if rev else e[:, ::-1], (0, 1, 3, 2))
    p1r, p1i = pr[:, 1:], pi[:, 1:]
    if rev:
        p1r, p1i = p1r[:, ::-1], p1i[:, ::-1]
    m_re = c_re[:, None] * p1r[:, :, None, :] - c_im[:, None] * p1i[:, :, None, :]
    m_im = -c_re[:, None] * p1i[:, :, None, :] - c_im[:, None] * p1r[:, :, None, :]
    mt = jnp.transpose(jnp.concatenate([m_re, m_im], axis=-1), (0, 3, 1, 2))
    da = jnp.concatenate([pr[:, q], pr[:, q]], axis=-1)
    db = jnp.concatenate([-pi[:, q], pi[:, q]], axis=-1)
    return k, wt, mt, da, db


def _s5_operators(lf_re, lf_im, lsf, lb_re, lb_im, lsb, b_re, b_im, cf_re, cf_im, cb_re, cb_im):
    q = S5_Q
    kf, wtf, mtf, daf, dbf = _s5_direction(lf_re, lf_im, lsf, b_re, b_im, cf_re, cf_im, False)
    kb, wtb, mtb, dab, dbb = _s5_direction(lb_re, lb_im, lsb, b_re, b_im, cb_re, cb_im, True)
    t = np.arange(q)[:, None, None]
    s = np.arange(q)[None, :, None]
    d = np.arange(q)[None, None, :]
    ohf = jnp.asarray((t - s == d).astype(np.float32))
    ohb = jnp.asarray((s - t == d).astype(np.float32))
    hp = lax.Precision.HIGHEST
    tt = (jnp.einsum('tsd,gdcz->gsztc', ohf, kf, precision=hp) + jnp.einsum('tsd,gdcz->gsztc', ohb, kb, precision=hp))
    g = tt.shape[0]
    tt = tt.reshape(g, S5_QC, S5_QC)
    wt = jnp.concatenate([wtf.reshape(g, S5_QC, 2 * S5_P), wtb.reshape(g, S5_QC, 2 * S5_P)], axis=-1)
    mt = jnp.concatenate([mtf.reshape(g, 2 * S5_P, S5_QC), mtb.reshape(g, 2 * S5_P, S5_QC)], axis=1)
    return tt, wt, mt, jnp.concatenate([daf, dab], -1), jnp.concatenate([dbf, dbb], -1)


def _gspec(*shape):
    return pl.BlockSpec((None,) + shape, lambda g: (g,) + (0,) * len(shape))


def s5_state_in(u, wt, *, name):
    g, r, _ = u.shape

    def body(u_ref, w_ref, o_ref):
        o_ref[...] = _bd(u_ref[...], w_ref[...], 1, 0)

    return pl.pallas_call(
        body, name=name, grid=(g,), in_specs=[_gspec(r, S5_QC), _gspec(S5_QC, 4 * S5_P)],
        out_specs=_gspec(r, 4 * S5_P), out_shape=jax.ShapeDtypeStruct((g, r, 4 * S5_P), F32),
        compiler_params=_params(("parallel",)))(u, wt)


def _swap(h):
    return pltpu.roll(h, S5_P, 1)


def s5_carry_fwd(s, da, db, *, name):
    nck, rows, _ = s.shape
    w = 2 * S5_P

    def body(s_ref, da_ref, db_ref, h_ref):
        for rev, cols in ((False, slice(0, w)), (True, slice(w, 2 * w))):
            a, b = da_ref[:, cols], db_ref[:, cols]

            def step(i, h, rev=rev, cols=cols, a=a, b=b):
                k = (nck - 1 - i) if rev else i
                h_ref[k, :, cols] = h
                return a * h + b * _swap(h) + s_ref[k, :, cols]

            lax.fori_loop(0, nck, step, jnp.zeros((rows, w), F32))

    rt = min(CARRY_ROWS, rows)
    big, small = pl.BlockSpec((nck, rt, 2 * w), lambda i: (0, i, 0)), pl.BlockSpec((rt, 2 * w), lambda i: (i, 0))
    rows = rt
    return pl.pallas_call(
        body, name=name, grid=(s.shape[1] // rt,), in_specs=[big, small, small], out_specs=big,
        out_shape=jax.ShapeDtypeStruct(s.shape, F32), compiler_params=_params(("parallel",)))(s, da, db)


def s5_carry_bwd(hin, dh, da, db, *, name):
    nck, rows, _ = hin.shape
    w = 2 * S5_P

    def body(h_ref, dh_ref, da_ref, db_ref, ds_ref, gda_ref, gdb_ref):
        for rev, cols in ((False, slice(0, w)), (True, slice(w, 2 * w))):
            a, b = da_ref[:, cols], db_ref[:, cols]

            def step(i, carry, rev=rev, cols=cols, a=a, b=b):
                g, ga, gb = carry
                k = i if rev else (nck - 1 - i)
                ds_ref[k, :, cols] = g
                h = h_ref[k, :, cols]
                return (dh_ref[k, :, cols] + a * g + _swap(b * g), ga + g * h, gb + g * _swap(h))

            z = jnp.zeros((rows, w), F32)
            _, ga, gb = lax.fori_loop(0, nck, step, (z, z, z))
            gda_ref[:, cols] = ga
            gdb_ref[:, cols] = gb

    rt = min(CARRY_ROWS, rows)
    big, small = pl.BlockSpec((nck, rt, 2 * w), lambda i: (0, i, 0)), pl.BlockSpec((rt, 2 * w), lambda i: (i, 0))
    rows = rt
    return pl.pallas_call(
        body, name=name, grid=(hin.shape[1] // rt,), in_specs=[big, big, small, small], out_specs=[big, small, small],
        out_shape=[jax.ShapeDtypeStruct(hin.shape, F32), jax.ShapeDtypeStruct(da.shape, F32),
                   jax.ShapeDtypeStruct(da.shape, F32)],
        compiler_params=_params(("parallel",)))(hin, dh, da, db)


def s5_out(u, hin, tt, mt, *, name):
    g, r, _ = u.shape

    def body(u_ref, h_ref, t_ref, m_ref, o_ref):
        o_ref[...] = _bd(u_ref[...], t_ref[...], 1, 0) + _bd(h_ref[...], m_ref[...], 1, 0)

    return pl.pallas_call(
        body, name=name, grid=(g,),
        in_specs=[_gspec(r, S5_QC), _gspec(r, 4 * S5_P), _gspec(S5_QC, S5_QC), _gspec(4 * S5_P, S5_QC)],
        out_specs=_gspec(r, S5_QC), out_shape=jax.ShapeDtypeStruct((g, r, S5_QC), F32),
        compiler_params=_params(("parallel",)))(u, hin, tt, mt)


def s5_out_bwd(dy, u, hin, tt, mt, *, name):
    g, r, _ = u.shape

    def body(dy_ref, u_ref, h_ref, t_ref, m_ref, dh_ref, dt_ref, dm_ref, du_ref):
        dy_v = dy_ref[...]
        dh_ref[...] = _bd(dy_v, m_ref[...], 1, 1)
        dt_ref[...] = _bd(u_ref[...], dy_v, 0, 0)
        dm_ref[...] = _bd(h_ref[...], dy_v, 0, 0)
        du_ref[...] = _bd(dy_v, t_ref[...], 1, 1)

    return pl.pallas_call(
        body, name=name, grid=(g,),
        in_specs=[_gspec(r, S5_QC), _gspec(r, S5_QC), _gspec(r, 4 * S5_P), _gspec(S5_QC, S5_QC),
                  _gspec(4 * S5_P, S5_QC)],
        out_specs=[_gspec(r, 4 * S5_P), _gspec(S5_QC, S5_QC), _gspec(4 * S5_P, S5_QC), _gspec(r, S5_QC)],
        out_shape=[jax.ShapeDtypeStruct((g, r, 4 * S5_P), F32), jax.ShapeDtypeStruct((g, S5_QC, S5_QC), F32),
                   jax.ShapeDtypeStruct((g, 4 * S5_P, S5_QC), F32), jax.ShapeDtypeStruct((g, r, S5_QC), F32)],
        compiler_params=_params(("parallel",)))(dy, u, hin, tt, mt)


def s5_state_in_bwd(ds, u, wt, du1, *, name):
    g, r, _ = u.shape

    def body(ds_ref, u_ref, w_ref, du1_ref, du_ref, dw_ref):
        ds_v = ds_ref[...]
        du_ref[...] = du1_ref[...] + _bd(ds_v, w_ref[...], 1, 1)
        dw_ref[...] = _bd(u_ref[...], ds_v, 0, 0)

    return pl.pallas_call(
        body, name=name, grid=(g,),
        in_specs=[_gspec(r, 4 * S5_P), _gspec(r, S5_QC), _gspec(S5_QC, 4 * S5_P), _gspec(r, S5_QC)],
        out_specs=[_gspec(r, S5_QC), _gspec(S5_QC, 4 * S5_P)],
        out_shape=[jax.ShapeDtypeStruct((g, r, S5_QC), F32), jax.ShapeDtypeStruct((g, S5_QC, 4 * S5_P), F32)],
        compiler_params=_params(("parallel",)))(ds, u, wt, du1)


def _s5_post(ypre, u, dvec, wv, wg, bv, bg, nw):
    g = _gelu(ypre + dvec * u)
    out = (dot_nn(g, wv) + bv) * jax.nn.sigmoid(dot_nn(g, wg) + bg)
    return (_rms(out, nw),)


def _ssd_post(y, z, nw):
    return (_rms(y * _silu(z), nw),)


def _to_chunks(u, bsz):
    nck = u.shape[0] // bsz // S5_Q
    v = u.reshape(bsz, nck, S5_Q, S5_G, S5_C)
    return jnp.transpose(v, (3, 0, 1, 2, 4)).reshape(S5_G, bsz * nck, S5_QC)


def _from_chunks(y, bsz):
    nck = y.shape[1] // bsz
    v = y.reshape(S5_G, bsz, nck, S5_Q, S5_C)
    return jnp.transpose(v, (1, 2, 3, 0, 4)).reshape(bsz * nck * S5_Q, S5_W)


def _to_carry(s, bsz):
    nck = s.shape[1] // bsz
    return jnp.transpose(s.reshape(S5_G, bsz, nck, -1), (2, 0, 1, 3)).reshape(nck, S5_G * bsz, -1)


def _from_carry(h, bsz):
    nck = h.shape[0]
    return jnp.transpose(h.reshape(nck, S5_G, bsz, -1), (1, 2, 0, 3)).reshape(S5_G, bsz * nck, -1)


def _block_diag(w):
    eye = jnp.eye(S5_G, dtype=w.dtype)
    return jnp.einsum('gcd,gh->gchd', w, eye).reshape(S5_W, S5_W)


def _diag_blocks(w):
    v = w.reshape(S5_G, S5_C, S5_G, S5_C)
    return v[jnp.arange(S5_G), :, jnp.arange(S5_G), :]


def _dt_layouts(dt, bsz):
    seq = dt.shape[0] // bsz
    v = jnp.transpose(dt.reshape(bsz, seq, 2, SGROUPS, HPG), (0, 3, 1, 2, 4)).reshape(bsz, SGROUPS, seq, 2 * HPG)
    return v, jnp.transpose(v, (0, 1, 3, 2))


def _dt_from_layouts(dc, dr):
    bsz, _, seq, _ = dc.shape
    v = (dc + jnp.transpose(dr, (0, 1, 3, 2))).reshape(bsz, SGROUPS, seq, 2, HPG)
    return jnp.transpose(v, (0, 2, 3, 1, 4)).reshape(bsz * seq, 2 * HEADS)


def _head_params(f, b):
    v = jnp.concatenate([f.reshape(SGROUPS, HPG), b.reshape(SGROUPS, HPG)], axis=1)
    return v[:, None, :], v[:, :, None]


def _head_grads(gc, gr=None):
    v = gc.sum(0)[:, 0, :]
    if gr is not None:
        v = v + gr.sum(0)[:, :, 0]
    return v[:, :HPG].reshape(HEADS), v[:, HPG:].reshape(HEADS)


def local_step(x, target, w):
    bsz, seq, d = x.shape
    t = bsz * seq
    x2, tgt2 = x.reshape(t, d), target.reshape(t, d)
    g = {}
    row = lambda v: v.reshape(1, -1)
    bf = lambda v: v.astype(BF16)

    w_in = _unshard(bf(w['w_in']), SHARDED['w_in'])
    cuts = [0, SSD_W, SSD_W + XBC_W, SSD_W + XBC_W + 2 * HEADS, w_in.shape[1]]
    w_in_parts = [w_in[:, a:b] for a, b in zip(cuts[:-1], cuts[1:])]
    norm_mix = row(w['norm_mix_w'])
    (hn,) = rowmap_fwd(lambda a, nw: (_rms(a, nw),), [x2], [norm_mix], [(d, BF16)], name="rms_mix")
    z, xbc, dt, u = [matmul_sum([hn], [p], name=f"in_proj_{i}") for i, p in enumerate(w_in_parts)]

    conv_w, conv_b = _unshard(w['ssd_conv_w'], SHARDED['ssd_conv_w']), row(w['ssd_conv_b'])
    act = ssd_conv_fwd(xbc, conv_w, conv_b, bsz=bsz, name="ssd_conv")
    dtc, dtr = _dt_layouts(dt, bsz)
    bias_c, bias_r = _head_params(w['ssd_dt_bias_fwd'], w['ssd_dt_bias_bwd'])
    alog_c, alog_r = _head_params(w['ssd_a_log_fwd'], w['ssd_a_log_bwd'])
    dskip_c, _ = _head_params(w['ssd_d'], jnp.zeros_like(w['ssd_d']))
    pcs, prs = (bias_c, alog_c, dskip_c), (bias_r, alog_r)
    act3 = act.reshape(bsz, seq, XBC_W)
    y_scan = ssd_scan_fwd(act3, dtc, dtr, pcs, prs, name="ssd_scan").reshape(t, SSD_W)
    ssd_nw = row(w['ssd_norm_w'])
    (y_ssd,) = rowmap_fwd(_ssd_post, [y_scan, z], [ssd_nw], [(SSD_W, BF16)], name="ssd_post")

    s5_names = ['s5_lambda_re_fwd', 's5_lambda_im_fwd', 's5_log_step_fwd', 's5_lambda_re_bwd', 's5_lambda_im_bwd',
                's5_log_step_bwd', 's5_b_re', 's5_b_im', 's5_c_re_fwd', 's5_c_im_fwd', 's5_c_re_bwd', 's5_c_im_bwd']
    (tt, wt, mt, da, db), s5_pull = jax.vjp(_s5_operators, *[w[n] for n in s5_names])
    tt_b, wt_b, mt_b = bf(tt), bf(wt), bf(mt)
    da_r, db_r = jnp.repeat(da, bsz, axis=0), jnp.repeat(db, bsz, axis=0)
    uc = _to_chunks(u, bsz)
    s_in = _to_carry(s5_state_in(uc, wt_b, name="s5_state_in"), bsz)
    hin_c = s5_carry_fwd(s_in, da_r, db_r, name="s5_carry")
    hin = _from_carry(hin_c, bsz)
    ypre = _from_chunks(s5_out(uc, hin, tt_b, mt_b, name="s5_out"), bsz)
    glu_w = w['s5_glu_w']
    s5_par = [row(w['s5_d']), _block_diag(glu_w[:, :, :S5_C]), _block_diag(glu_w[:, :, S5_C:]),
              row(w['s5_glu_b'][:, :S5_C]), row(w['s5_glu_b'][:, S5_C:]), row(w['s5_norm_w'])]
    (y_s5,) = rowmap_fwd(_s5_post, [ypre, u], s5_par, [(S5_W, BF16)], name="s5_post")

    w_out = bf(w['w_out']).reshape(SSD_W + S5_W, d)
    h1 = matmul_sum([y_ssd, y_s5], [w_out[:SSD_W], w_out[SSD_W:]], add=x2, name="out_proj")
    norm_ffn = row(w['norm_ffn_w'])
    (hn2,) = rowmap_fwd(lambda a, nw: (_rms(a, nw),), [h1], [norm_ffn], [(d, BF16)], name="rms_ffn")
    pad_c = FFN_PAD - FFN_BLK
    half = N_DEV // 2
    w_up3 = jnp.pad(bf(w['ffn_w_up']), ((0, 0), (0, 0), (0, pad_c)))
    w_down = jnp.pad(bf(w['ffn_w_down']).reshape(half, FFN_BLK, d), ((0, 0), (0, pad_c), (0, 0)))
    w_down = w_down.reshape(half * FFN_PAD, d)
    fconv_w = jnp.pad(w['ffn_conv_w'], ((0, 0), (0, 0), (0, pad_c)))
    fconv_w = jnp.transpose(fconv_w, (1, 0, 2)).reshape(FCONV, N_DEV * FFN_PAD)
    fconv_b = row(jnp.pad(w['ffn_conv_b'].reshape(N_DEV, FFN_BLK), ((0, 0), (0, pad_c))))
    up = matmul_cols(hn2, w_up3, name="ffn_up")
    fact = ffn_act_fwd(up, fconv_w, fconv_b, bsz=bsz, name="ffn_act")
    h2 = matmul_sum([fact], [w_down], add=h1, name="ffn_down")
    loss, dh2, g_nf = loss_head(h2, tgt2, row(w['norm_final_w']), name="loss_head")
    g['norm_final_w'] = g_nf.reshape(-1)

    dfact = matmul_sum([dh2], [w_down], nt=True, name="ffn_down_dx")
    g_down = matmul_tn(fact, dh2, name="ffn_down_dw").reshape(half, FFN_PAD, d)[:, :FFN_BLK]
    g['ffn_w_down'] = g_down.reshape(N_DEV, FFN_BLK // 2, d)
    dval, dgate, dwv, dwg, dbv, dbg = ffn_act_bwd(up, dfact, fconv_w, fconv_b, bsz=bsz, name="ffn_act_bwd")
    g_cw = jnp.concatenate([dwv, dwg], axis=1).reshape(FCONV, N_DEV, FFN_PAD)[:, :, :FFN_BLK]
    g['ffn_conv_w'] = jnp.transpose(g_cw, (1, 0, 2))
    g['ffn_conv_b'] = jnp.concatenate([dbv, dbg], axis=1).reshape(N_DEV, FFN_PAD)[:, :FFN_BLK].reshape(-1)
    windows = [(dval, FFN_PAD, p) for p in range(half)] + [(dgate, FFN_PAD, p) for p in range(half)]
    dhn2 = matmul_sum(windows, [(w_up3, p) for p in range(N_DEV)], nt=True, name="ffn_up_dx")
    g['ffn_w_up'] = jnp.concatenate([matmul_tn(hn2, dval, out_blocks=half, name="ffn_up_dw_val"),
                                     matmul_tn(hn2, dgate, out_blocks=half, name="ffn_up_dw_gate")],
                                    axis=0)[:, :, :FFN_BLK]
    dh1, g_nffn = rowmap_bwd(lambda a, nw: (_rms(a, nw),), [h1], [norm_ffn], [dhn2], add=dh2, name="rms_ffn_bwd")
    g['norm_ffn_w'] = g_nffn.reshape(-1)

    dycat = matmul_sum([dh1], [w_out], nt=True, name="out_proj_dx")
    g['w_out'] = jnp.concatenate([matmul_tn(y_ssd, dh1, name="out_proj_dw_ssd"),
                                  matmul_tn(y_s5, dh1, name="out_proj_dw_s5")], axis=0).reshape(w['w_out'].shape)
    dy_scan, dz, g_snw = rowmap_bwd(_ssd_post, [y_scan, z], [ssd_nw], [(dycat, SSD_W, 0)], name="ssd_post_bwd")
    g['ssd_norm_w'] = g_snw.reshape(-1)
    dypre, du_a, g_d, g_wv, g_wg, g_bv, g_bg, g_s5nw = rowmap_bwd(
        _s5_post, [ypre, u], s5_par, [(dycat, S5_W, SSD_W // S5_W)], name="s5_post_bwd")
    g['s5_d'], g['s5_norm_w'] = g_d.reshape(-1), g_s5nw.reshape(-1)
    g['s5_glu_w'] = jnp.concatenate([_diag_blocks(g_wv), _diag_blocks(g_wg)], axis=-1)
    g['s5_glu_b'] = jnp.concatenate([g_bv.reshape(S5_G, S5_C), g_bg.reshape(S5_G, S5_C)], axis=-1)

    dyc = _to_chunks(dypre, bsz)
    dhin, dtt, dmt, du1 = s5_out_bwd(dyc, uc, hin, tt_b, mt_b, name="s5_out_bwd")
    ds_c, gda, gdb = s5_carry_bwd(hin_c, _to_carry(dhin, bsz), da_r, db_r, name="s5_carry_bwd")
    duc, dwt = s5_state_in_bwd(_from_carry(ds_c, bsz), uc, wt_b, du1, name="s5_state_in_bwd")
    du = du_a + _from_chunks(duc, bsz)
    fold = lambda v: v.reshape(S5_G, bsz, -1).sum(1)
    for n, gv in zip(s5_names, s5_pull((dtt, dwt, dmt, fold(gda), fold(gdb)))):
        g[n] = gv

    dxs, dbm, dcm, ddtc, ddtr, gbc, gac, gdk, gbr, gar = ssd_scan_bwd(
        act3, dtc, dtr, pcs, prs, dy_scan.reshape(bsz, seq, SSD_W), name="ssd_scan_bwd")
    g['ssd_dt_bias_fwd'], g['ssd_dt_bias_bwd'] = _head_grads(gbc, gbr)
    g['ssd_a_log_fwd'], g['ssd_a_log_bwd'] = _head_grads(gac, gar)
    g['ssd_d'] = _head_grads(gdk)[0]
    dact = jnp.concatenate([dxs, dbm, dcm], axis=-1).reshape(t, XBC_W)
    dxbc, g_cw, g_cb = ssd_conv_bwd(xbc, dact, conv_w, conv_b, bsz=bsz, name="ssd_conv_bwd")
    g['ssd_conv_w'] = _shard_rows(g_cw, SHARDED['ssd_conv_w']).reshape(w['ssd_conv_w'].shape)
    g['ssd_conv_b'] = g_cb.reshape(-1)
    ddt = _dt_from_layouts(ddtc, ddtr)

    dparts = [dz, dxbc, ddt, du]
    dhn = matmul_sum(dparts, w_in_parts, nt=True, name="in_proj_dx")
    g_in = jnp.concatenate([matmul_tn(hn, dp, name=f"in_proj_dw_{i}") for i, dp in enumerate(dparts)], axis=1)
    g['w_in'] = _shard_rows(g_in, SHARDED['w_in']).reshape(w['w_in'].shape)
    dx, g_nmix = rowmap_bwd(lambda a, nw: (_rms(a, nw),), [x2], [norm_mix], [dhn], add=dh1, name="rms_mix_bwd")
    g['norm_mix_w'] = g_nmix.reshape(-1)
    return loss, dx.reshape(bsz, seq, d), g


ANY = pl.BlockSpec(memory_space=pl.ANY)


def all_gather(shards, *, name):
    n = len(shards)

    def body(*refs):
        x_refs, out_refs = refs[:n], refs[n:2 * n]
        send_sems, recv_sems, local_sems = refs[2 * n:]
        x, y, c = lax.axis_index("x"), lax.axis_index("y"), lax.axis_index("c")
        me, sibling = (x, y, c), (x, y, 1 - c)
        chips = [(1 - x, y), (x, 1 - y), (1 - x, 1 - y)]

        def copy(k, j, block, to, own=False):
            dst = out_refs[j].at[4 * block[0] + 2 * block[1] + block[2]]
            return pltpu.make_async_remote_copy(
                src_ref=x_refs[j] if own else dst, dst_ref=dst,
                send_sem=send_sems.at[k, j], recv_sem=recv_sems.at[k, j], device_id=to, device_id_type=MESH)

        mine = [pltpu.make_async_copy(x_refs[j], out_refs[j].at[4 * x + 2 * y + c], local_sems.at[j]) for j in range(n)]
        first = [copy(0, j, me, sibling, own=True) for j in range(n)]
        first += [copy(1 + i, j, me, (*chip, c), own=True) for i, chip in enumerate(chips) for j in range(n)]
        for cp in mine + first:
            cp.start()
        passed = []
        for i, chip in enumerate(chips):
            for j in range(n):
                copy(1 + i, j, (*chip, c), me).wait_recv()
                passed.append(copy(4 + i, j, (*chip, c), sibling))
                passed[-1].start()
        for j in range(n):
            copy(0, j, sibling, me).wait_recv()
        for i, chip in enumerate(chips):
            for j in range(n):
                copy(4 + i, j, (*chip, 1 - c), me).wait_recv()
        for cp in first + passed:
            cp.wait_send()
        for cp in mine:
            cp.wait()

    return pl.pallas_call(
        body, name=name, out_shape=[jax.ShapeDtypeStruct((N_DEV,) + s.shape, s.dtype) for s in shards],
        in_specs=[ANY] * n, out_specs=[ANY] * n,
        scratch_shapes=[pltpu.SemaphoreType.DMA((7, n)), pltpu.SemaphoreType.DMA((7, n)),
                        pltpu.SemaphoreType.DMA((n,))],
    )(*shards)


def exchange(sends, *, name):
    n = len(sends)

    def body(*refs):
        send_refs, recv_refs = refs[:n], refs[n:2 * n]
        send_sems, recv_sems, local_sems = refs[2 * n:]
        x, y, c = lax.axis_index("x"), lax.axis_index("y"), lax.axis_index("c")
        me = 4 * x + 2 * y + c
        local = [pltpu.make_async_copy(send_refs[j].at[me], recv_refs[j].at[me], local_sems.at[j]) for j in range(n)]
        for cp in local:
            cp.start()
        copies = []
        for k in range(1, N_DEV):
            px = (1 - x) if k & 4 else x
            py = (1 - y) if k & 2 else y
            pc = (1 - c) if k & 1 else c
            for j in range(n):
                copies.append(pltpu.make_async_remote_copy(
                    src_ref=send_refs[j].at[4 * px + 2 * py + pc], dst_ref=recv_refs[j].at[me],
                    send_sem=send_sems.at[k - 1, j], recv_sem=recv_sems.at[k - 1, j],
                    device_id=(px, py, pc), device_id_type=MESH))
        for cp in copies:
            cp.start()
        for cp in copies:
            cp.wait()
        for cp in local:
            cp.wait()

    return pl.pallas_call(
        body, name=name, out_shape=[jax.ShapeDtypeStruct(s.shape, s.dtype) for s in sends],
        in_specs=[ANY] * n, out_specs=[ANY] * n,
        scratch_shapes=[pltpu.SemaphoreType.DMA((N_DEV - 1, n)), pltpu.SemaphoreType.DMA((N_DEV - 1, n)),
                        pltpu.SemaphoreType.DMA((n,))],
    )(*sends)


def _adam_rows(r, c):
    fits = [t for t in range(8, r + 1, 8) if r % t == 0 and N_DEV * t * c * 4 <= 6 * 2 ** 20]
    return max(fits) if fits else r


def adamw(recv, w, m, v, *, name):
    _, r, n = recv.shape
    tr = _adam_rows(r, n)

    def body(r_ref, w_ref, m_ref, v_ref, g_ref, d_ref, nm_ref, nv_ref):
        g = r_ref[0].astype(F32)
        for s in range(1, N_DEV):
            g = g + r_ref[s].astype(F32)
        m_new = ADAM_B1 * m_ref[...] + (1.0 - ADAM_B1) * g
        v_new = ADAM_B2 * v_ref[...] + (1.0 - ADAM_B2) * jnp.square(g)
        m_hat = m_new / (1.0 - ADAM_B1 ** ADAM_STEP)
        v_hat = v_new / (1.0 - ADAM_B2 ** ADAM_STEP)
        g_ref[...] = g
        d_ref[...] = -ADAM_LR * (m_hat / (jnp.sqrt(v_hat) + ADAM_EPS) + ADAM_WD * w_ref[...])
        nm_ref[...] = m_new
        nv_ref[...] = v_new

    blk = pl.BlockSpec((tr, n), lambda i: (i, 0))
    return pl.pallas_call(
        body, name=name, grid=(r // tr,), in_specs=[pl.BlockSpec((N_DEV, tr, n), lambda i: (0, i, 0)), blk, blk, blk],
        out_specs=[blk] * 4, out_shape=[jax.ShapeDtypeStruct((r, n), F32)] * 4,
        compiler_params=_params(("parallel",)))(recv, w, m, v)


def _shard_rows(full, axis):
    if axis == 0:
        return full.reshape(N_DEV, -1)
    r, c = full.shape
    return jnp.transpose(full.reshape(r, N_DEV, c // N_DEV), (1, 0, 2)).reshape(N_DEV, -1)


def _unshard(blocks, axis):
    if axis == 0:
        return blocks.reshape(-1, blocks.shape[-1])
    return jnp.transpose(blocks, (1, 0, 2)).reshape(blocks.shape[1], -1)


def kernel(x, norm_mix_w, w_in, ssd_conv_w, ssd_conv_b, ssd_dt_bias_fwd, ssd_dt_bias_bwd, ssd_a_log_fwd, ssd_a_log_bwd, ssd_d, ssd_norm_w, s5_lambda_re_fwd, s5_lambda_im_fwd, s5_log_step_fwd, s5_lambda_re_bwd, s5_lambda_im_bwd, s5_log_step_bwd, s5_b_re, s5_b_im, s5_c_re_fwd, s5_c_im_fwd, s5_c_re_bwd, s5_c_im_bwd, s5_d, s5_glu_w, s5_glu_b, s5_norm_w, w_out, norm_ffn_w, ffn_w_up, ffn_conv_w, ffn_conv_b, ffn_w_down, norm_final_w, loss_target, m_norm_mix_w, m_w_in, m_ssd_conv_w, m_ssd_conv_b, m_ssd_dt_bias_fwd, m_ssd_dt_bias_bwd, m_ssd_a_log_fwd, m_ssd_a_log_bwd, m_ssd_d, m_ssd_norm_w, m_s5_lambda_re_fwd, m_s5_lambda_im_fwd, m_s5_log_step_fwd, m_s5_lambda_re_bwd, m_s5_lambda_im_bwd, m_s5_log_step_bwd, m_s5_b_re, m_s5_b_im, m_s5_c_re_fwd, m_s5_c_im_fwd, m_s5_c_re_bwd, m_s5_c_im_bwd, m_s5_d, m_s5_glu_w, m_s5_glu_b, m_s5_norm_w, m_w_out, m_norm_ffn_w, m_ffn_w_up, m_ffn_conv_w, m_ffn_conv_b, m_ffn_w_down, m_norm_final_w, v_norm_mix_w, v_w_in, v_ssd_conv_w, v_ssd_conv_b, v_ssd_dt_bias_fwd, v_ssd_dt_bias_bwd, v_ssd_a_log_fwd, v_ssd_a_log_bwd, v_ssd_d, v_ssd_norm_w, v_s5_lambda_re_fwd, v_s5_lambda_im_fwd, v_s5_log_step_fwd, v_s5_lambda_re_bwd, v_s5_lambda_im_bwd, v_s5_log_step_bwd, v_s5_b_re, v_s5_b_im, v_s5_c_re_fwd, v_s5_c_im_fwd, v_s5_c_re_bwd, v_s5_c_im_bwd, v_s5_d, v_s5_glu_w, v_s5_glu_b, v_s5_norm_w, v_w_out, v_norm_ffn_w, v_ffn_w_up, v_ffn_conv_w, v_ffn_conv_b, v_ffn_w_down, v_norm_final_w):
    args = dict(locals())
    strip = lambda n, v: v if n == 'norm_final_w' else v[0]
    w = {n: strip(n, args[n]) for n in WEIGHTS}

    mats = ['w_in', 'w_out', 'ffn_w_up', 'ffn_w_down']
    convs = ['ssd_conv_w', 'ffn_conv_w']
    stacks = all_gather([w[n].astype(BF16) for n in mats] + [w[n] for n in convs], name="weight_all_gather")
    full = dict(w)
    full.update(zip(mats + convs, stacks))

    loss, grad_x, g = local_step(x, loss_target, full)

    small = convs + [n for n in WEIGHTS if n not in SHARDED]
    pieces = [g[n].reshape(N_DEV, -1) if n in SHARDED else jnp.broadcast_to(g[n].reshape(1, -1), (N_DEV, g[n].size))
              for n in small]
    pieces.append(jnp.broadcast_to(loss.reshape(1, 1), (N_DEV, 1)))
    total = sum(p.shape[1] for p in pieces)
    nrow = -(-total // (PACK_ROWS * LANES)) * PACK_ROWS
    pieces.append(jnp.zeros((N_DEV, nrow * LANES - total), F32))
    packed_send = jnp.concatenate(pieces, axis=1).reshape(N_DEV, nrow, LANES)
    recvs = exchange([g[n].astype(BF16) for n in mats] + [packed_send], name="grad_exchange")

    outs = [{}, {}, {}, {}]
    for n, recv in zip(mats, recvs):
        res = adamw(recv, w[n], strip(n, args['m_' + n]), strip(n, args['v_' + n]), name="adamw_" + n)
        for o, p in zip(outs, res):
            o[n] = p.reshape(args[n].shape)

    def pack(prefix):
        vals = [strip(n, args[prefix + n]).reshape(-1) for n in small]
        return jnp.pad(jnp.concatenate(vals), (0, nrow * LANES - total + 1)).reshape(nrow, LANES)

    packed = adamw(recvs[-1], pack(''), pack('m_'), pack('v_'), name="adamw_small")
    packed = [p.reshape(-1) for p in packed]
    off = 0
    for n in small:
        size = w[n].size
        for o, p in zip(outs, packed):
            o[n] = p[off:off + size].reshape(args[n].shape)
        off += size
    loss_out = packed[0][off].reshape(())
    return (loss_out, grad_x, *[o[n] for o in outs for n in WEIGHTS])
```

```python
import functools

import jax
import jax.numpy as jnp
import numpy as np
from jax import lax
from jax.experimental import pallas as pl
from jax.experimental.pallas import tpu as pltpu

F32, BF16 = jnp.float32, jnp.bfloat16
N_DEV = 8
D_MODEL = 1024
SSD_W, HEADS, HDIM, SGROUPS, HPG, NSTATE, SCONV, QC = 1024, 16, 64, 4, 4, 128, 5, 128
XBC_W = SSD_W + 2 * SGROUPS * NSTATE
S5_W, S5_G, S5_C, S5_P, S5_Q = 512, 32, 16, 64, 16
S5_QC = S5_Q * S5_C
CARRY_ROWS = 32
DFF, FCONV = 2816, 3
FFN_BLK, FFN_PAD = 704, 768
EPS = 1e-6
ADAM_LR, ADAM_B1, ADAM_B2, ADAM_EPS, ADAM_WD, ADAM_STEP = 0.001, 0.9, 0.999, 1e-08, 0.01, 10
LANES = 128
MESH = pl.DeviceIdType.MESH

WEIGHTS = ['norm_mix_w', 'w_in', 'ssd_conv_w', 'ssd_conv_b', 'ssd_dt_bias_fwd', 'ssd_dt_bias_bwd', 'ssd_a_log_fwd',
           'ssd_a_log_bwd', 'ssd_d', 'ssd_norm_w', 's5_lambda_re_fwd', 's5_lambda_im_fwd', 's5_log_step_fwd',
           's5_lambda_re_bwd', 's5_lambda_im_bwd', 's5_log_step_bwd', 's5_b_re', 's5_b_im', 's5_c_re_fwd', 's5_c_im_fwd',
           's5_c_re_bwd', 's5_c_im_bwd', 's5_d', 's5_glu_w', 's5_glu_b', 's5_norm_w', 'w_out', 'norm_ffn_w', 'ffn_w_up',
           'ffn_conv_w', 'ffn_conv_b', 'ffn_w_down', 'norm_final_w']
SHARDED = {'w_in': 1, 'ssd_conv_w': 1, 'w_out': 0, 'ffn_w_up': 1, 'ffn_conv_w': 1, 'ffn_w_down': 0}
FULL_SHAPE = {'w_in': (1024, 3616), 'ssd_conv_w': (5, 2048), 'w_out': (1536, 1024), 'ffn_w_up': (1024, 5632),
              'ffn_conv_w': (3, 5632), 'ffn_w_down': (2816, 1024)}
PACK_ROWS = 512


def _pick(n, cap=1536):
    if n <= cap:
        return n
    return max(t for t in range(LANES, cap + 1, LANES) if n % t == 0)


def _params(sem):
    return pltpu.CompilerParams(dimension_semantics=sem)


def _bd(a, b, ca, cb):
    return lax.dot_general(a.astype(BF16), b.astype(BF16), (((ca,), (cb,)), ((), ())), preferred_element_type=F32)


@jax.custom_vjp
def dot_nn(a, b):
    return _bd(a, b, 1, 0)


dot_nn.defvjp(lambda a, b: (_bd(a, b, 1, 0), (a, b)),
              lambda r, g: (_bd(g, r[1], 1, 1).astype(r[0].dtype), _bd(r[0], g, 0, 0).astype(r[1].dtype)))


@jax.custom_vjp
def dot_nt(a, b):
    return _bd(a, b, 1, 1)


dot_nt.defvjp(lambda a, b: (_bd(a, b, 1, 1), (a, b)),
              lambda r, g: (_bd(g, r[1], 1, 0).astype(r[0].dtype), _bd(g, r[0], 0, 0).astype(r[1].dtype)))


@jax.custom_vjp
def dot_tn(a, b):
    return _bd(a, b, 0, 0)


dot_tn.defvjp(lambda a, b: (_bd(a, b, 0, 0), (a, b)),
              lambda r, g: (_bd(r[1], g, 1, 1).astype(r[0].dtype), _bd(r[0], g, 1, 0).astype(r[1].dtype)))


def _split3(x):
    hi = x.astype(BF16)
    r = x - hi.astype(F32)
    mid = r.astype(BF16)
    lo = (r - mid.astype(F32)).astype(BF16)
    return hi, mid, lo


def _cum_matrix(q, upper):
    ri = lax.broadcasted_iota(jnp.int32, (q, q), 0)
    ci = lax.broadcasted_iota(jnp.int32, (q, q), 1)
    return jnp.where((ci >= ri) if upper else (ci <= ri), 1.0, 0.0).astype(BF16)


def _exact_left(mat, x):
    return sum(jnp.dot(mat, p, preferred_element_type=F32) for p in _split3(x))


def _exact_right(x, mat):
    return sum(jnp.dot(p, mat, preferred_element_type=F32) for p in _split3(x))


@functools.partial(jax.custom_vjp, nondiff_argnums=(1,))
def cum_col(x, rev):
    return _exact_left(_cum_matrix(x.shape[0], rev), x)


cum_col.defvjp(lambda x, rev: (cum_col(x, rev), None),
               lambda rev, _, g: (_exact_left(_cum_matrix(g.shape[0], not rev), g),))


@functools.partial(jax.custom_vjp, nondiff_argnums=(1,))
def cum_row(x, rev):
    return _exact_right(x, _cum_matrix(x.shape[1], not rev))


cum_row.defvjp(lambda x, rev: (cum_row(x, rev), None),
               lambda rev, _, g: (_exact_right(g, _cum_matrix(g.shape[1], rev)),))


def _softplus(x):
    return jnp.maximum(x, 0.0) + jnp.log(1.0 + jnp.exp(-jnp.abs(x)))


def _silu(x):
    return x * jax.nn.sigmoid(x)


def _gelu(x):
    return 0.5 * x * (1.0 + jnp.tanh(0.7978845608028654 * (x + 0.044715 * (x * x * x))))


def _rms(x, w):
    xf = x.astype(F32)
    return xf * lax.rsqrt(jnp.mean(xf * xf, axis=-1, keepdims=True) + EPS) * w


def matmul_sum(a_list, b_list, *, name, out_dtype=F32, add=None, tm=512, nt=False):
    a_arrs = [a[0] if isinstance(a, tuple) else a for a in a_list]
    b_arrs = [b[0] if isinstance(b, tuple) else b for b in b_list]
    m, n = a_arrs[0].shape[0], b_arrs[0].shape[-2 if nt else -1]
    tm, tn, k = min(tm, m), _pick(n), len(a_list)

    def body(*refs):
        acc = None
        for a_ref, b_ref in zip(refs[:k], refs[k:2 * k]):
            p = _bd(a_ref[...], b_ref[...], 1, 1 if nt else 0)
            acc = p if acc is None else acc + p
        if add is not None:
            acc = acc + refs[2 * k][...]
        refs[-1][...] = acc.astype(out_dtype)

    def a_spec(a):
        if isinstance(a, tuple):
            return pl.BlockSpec((tm, a[1]), lambda i, j, blk=a[2]: (i, blk))
        return pl.BlockSpec((tm, a.shape[1]), lambda i, j: (i, 0))

    def b_spec(b):
        arr, p = b if isinstance(b, tuple) else (b, None)
        kk = arr.shape[-1 if nt else -2]
        shape, idx = ((tn, kk), lambda j: (j, 0)) if nt else ((kk, tn), lambda j: (0, j))
        if p is None:
            return pl.BlockSpec(shape, lambda i, j: idx(j))
        return pl.BlockSpec((None,) + shape, lambda i, j, p=p: (p,) + idx(j))

    in_specs = [a_spec(a) for a in a_list] + [b_spec(b) for b in b_list]
    args = a_arrs + b_arrs
    if add is not None:
        in_specs.append(pl.BlockSpec((tm, tn), lambda i, j: (i, j)))
        args.append(add)
    return pl.pallas_call(
        body, name=name, grid=(m // tm, n // tn), in_specs=in_specs,
        out_specs=pl.BlockSpec((tm, tn), lambda i, j: (i, j)),
        out_shape=jax.ShapeDtypeStruct((m, n), out_dtype),
        compiler_params=_params(("parallel", "parallel")))(*args)


def matmul_cols(a, b3, *, name, out_dtype=F32, tm=512):
    m, kk = a.shape
    p, _, nb = b3.shape
    tm, tn = min(tm, m), _pick(nb, 768)
    per = nb // tn

    def body(a_ref, b_ref, o_ref):
        o_ref[...] = _bd(a_ref[...], b_ref[...], 1, 0).astype(out_dtype)

    return pl.pallas_call(
        body, name=name, grid=(m // tm, p * per),
        in_specs=[pl.BlockSpec((tm, kk), lambda i, j: (i, 0)),
                  pl.BlockSpec((None, kk, tn), lambda i, j: (j // per, 0, j % per))],
        out_specs=pl.BlockSpec((tm, tn), lambda i, j: (i, j)),
        out_shape=jax.ShapeDtypeStruct((m, p * nb), out_dtype),
        compiler_params=_params(("parallel", "parallel")))(a, b3)


def matmul_tn(a, b, *, name, tm=512, out_blocks=None):
    m, k = a.shape
    n = b.shape[1]
    nb = n // (out_blocks or 1)
    tm, tk, tn = min(tm, m), _pick(k), _pick(nb, 768 if out_blocks else 1536)
    per = nb // tn

    def body(a_ref, b_ref, o_ref):
        @pl.when(pl.program_id(2) == 0)
        def _():
            o_ref[...] = jnp.zeros_like(o_ref)

        o_ref[...] += _bd(a_ref[...], b_ref[...], 0, 0)

    if out_blocks:
        out_spec = pl.BlockSpec((None, tk, tn), lambda i, j, t: (j // per, i, j % per))
        out_shape = jax.ShapeDtypeStruct((out_blocks, k, nb), F32)
    else:
        out_spec = pl.BlockSpec((tk, tn), lambda i, j, t: (i, j))
        out_shape = jax.ShapeDtypeStruct((k, n), F32)
    return pl.pallas_call(
        body, name=name, grid=(k // tk, n // tn, m // tm),
        in_specs=[pl.BlockSpec((tm, tk), lambda i, j, t: (t, i)), pl.BlockSpec((tm, tn), lambda i, j, t: (t, j))],
        out_specs=out_spec, out_shape=out_shape,
        compiler_params=_params(("parallel", "parallel", "arbitrary")))(a, b)


def _row_spec(r, tm):
    if isinstance(r, tuple):
        arr, width, blk = r
        return arr, pl.BlockSpec((tm, width), lambda i, blk=blk: (i, blk))
    return r, pl.BlockSpec((tm, r.shape[1]), lambda i: (i, 0))


def _full_spec(p):
    return pl.BlockSpec(p.shape, lambda i: (0,) * p.ndim)


def rowmap_fwd(fn, rows, params, outs, *, name, tm=256):
    pairs = [_row_spec(r, tm) for r in rows]
    m = pairs[0][0].shape[0]
    tm = min(tm, m)
    pairs = [_row_spec(r, tm) for r in rows]
    nr, npar = len(rows), len(params)

    def body(*refs):
        res = fn(*[r[...] for r in refs[:nr + npar]])
        for o_ref, v in zip(refs[nr + npar:], res):
            o_ref[...] = v.astype(o_ref.dtype)

    return pl.pallas_call(
        body, name=name, grid=(m // tm,),
        in_specs=[s for _, s in pairs] + [_full_spec(p) for p in params],
        out_specs=[pl.BlockSpec((tm, c), lambda i: (i, 0)) for c, _ in outs],
        out_shape=[jax.ShapeDtypeStruct((m, c), dt) for c, dt in outs],
        compiler_params=_params(("parallel",)))(*[a for a, _ in pairs], *params)


def rowmap_bwd(fn, rows, params, cts, *, name, row_dtypes=None, add=None, tm=256):
    m = _row_spec(rows[0], tm)[0].shape[0]
    tm = min(tm, m)
    rp = [_row_spec(r, tm) for r in rows]
    cp = [_row_spec(c, tm) for c in cts]
    nr, npar, nc = len(rows), len(params), len(cts)
    row_dtypes = row_dtypes or [F32] * nr
    widths = [s.block_shape[1] for _, s in rp]

    def body(*refs):
        ins = [r[...] for r in refs[:nr + npar]]
        ins = [v.astype(F32) for v in ins]
        ct = tuple(r[...].astype(F32) for r in refs[nr + npar:nr + npar + nc])
        base = nr + npar + nc
        extra = None
        if add is not None:
            extra = refs[base][...]
            base += 1
        _, pull = jax.vjp(fn, *ins)
        grads = pull(ct)
        for j in range(nr):
            g = grads[j]
            if j == 0 and extra is not None:
                g = g + extra
            refs[base + j][...] = g.astype(refs[base + j].dtype)

        @pl.when(pl.program_id(0) == 0)
        def _():
            for j in range(npar):
                refs[base + nr + j][...] = jnp.zeros_like(refs[base + nr + j])

        for j in range(npar):
            refs[base + nr + j][...] += grads[nr + j]

    in_specs = [s for _, s in rp] + [_full_spec(p) for p in params] + [s for _, s in cp]
    args = [a for a, _ in rp] + list(params) + [a for a, _ in cp]
    if add is not None:
        in_specs.append(pl.BlockSpec((tm, widths[0]), lambda i: (i, 0)))
        args.append(add)
    out_specs = [pl.BlockSpec((tm, w), lambda i: (i, 0)) for w in widths] + [_full_spec(p) for p in params]
    out_shape = [jax.ShapeDtypeStruct((m, w), dt) for w, dt in zip(widths, row_dtypes)]
    out_shape += [jax.ShapeDtypeStruct(p.shape, F32) for p in params]
    return pl.pallas_call(
        body, name=name, grid=(m // tm,), in_specs=in_specs, out_specs=out_specs, out_shape=out_shape,
        compiler_params=_params(("arbitrary",)))(*args)


def loss_head(h, target, w, *, name, tm=256):
    m, d = h.shape
    tm = min(tm, m)

    def body(h_ref, t_ref, w_ref, loss_ref, dh_ref, dw_ref):
        y, pull = jax.vjp(_rms, h_ref[...], w_ref[...])
        err = y - t_ref[...]
        dh, dw = pull(err * (1.0 / d))

        @pl.when(pl.program_id(0) == 0)
        def _():
            loss_ref[...] = jnp.zeros_like(loss_ref)
            dw_ref[...] = jnp.zeros_like(dw_ref)

        loss_ref[...] += (0.5 / d) * jnp.sum(err * err, keepdims=True)
        dw_ref[...] += dw
        dh_ref[...] = dh

    row = pl.BlockSpec((tm, d), lambda i: (i, 0))
    return pl.pallas_call(
        body, name=name, grid=(m // tm,), in_specs=[row, row, _full_spec(w)],
        out_specs=[pl.BlockSpec((1, 1), lambda i: (0, 0)), row, _full_spec(w)],
        out_shape=[jax.ShapeDtypeStruct((1, 1), F32), jax.ShapeDtypeStruct((m, d), F32),
                   jax.ShapeDtypeStruct(w.shape, F32)],
        compiler_params=_params(("arbitrary",)))(h, target, w)


def _shift(x, s):
    if s == 0:
        return x
    n = x.shape[0]
    t = lax.broadcasted_iota(jnp.int32, x.shape, 0)
    rolled = pltpu.roll(x, (-s) % n, 0)
    return jnp.where((t + s >= 0) & (t + s < n), rolled, 0.0)


def _conv(x, w, b):
    k = w.shape[0]
    acc = b + w[k // 2:k // 2 + 1, :] * x
    for j in range(k):
        if j != k // 2:
            acc = acc + w[j:j + 1, :] * _shift(x, j - k // 2)
    return acc


def _conv_bwd(x, dc, w):
    k = w.shape[0]
    dx = None
    dws = []
    for j in range(k):
        s = j - k // 2
        term = w[j:j + 1, :] * _shift(dc, -s)
        dx = term if dx is None else dx + term
        dws.append(jnp.sum(dc * _shift(x, s), axis=0, keepdims=True))
    return dx, jnp.concatenate(dws, axis=0), jnp.sum(dc, axis=0, keepdims=True)


def _dsilu(c):
    s = jax.nn.sigmoid(c)
    return s * (1.0 + c * (1.0 - s))


def ssd_conv_fwd(xbc, w, b, *, bsz, name):
    t, c = xbc.shape
    seq, ct = t // bsz, 256

    def body(x_ref, w_ref, b_ref, o_ref):
        o_ref[...] = _silu(_conv(x_ref[...], w_ref[...], b_ref[...]))

    return pl.pallas_call(
        body, name=name, grid=(c // ct, bsz),
        in_specs=[pl.BlockSpec((seq, ct), lambda j, i: (i, j)), pl.BlockSpec((w.shape[0], ct), lambda j, i: (0, j)),
                  pl.BlockSpec((1, ct), lambda j, i: (0, j))],
        out_specs=pl.BlockSpec((seq, ct), lambda j, i: (i, j)),
        out_shape=jax.ShapeDtypeStruct((t, c), F32),
        compiler_params=_params(("parallel", "parallel")))(xbc, w, b)


def ssd_conv_bwd(xbc, dact, w, b, *, bsz, name):
    t, c = xbc.shape
    seq, ct, k = t // bsz, 256, w.shape[0]

    def body(x_ref, g_ref, w_ref, b_ref, dx_ref, dw_ref, db_ref):
        x, wv = x_ref[...], w_ref[...]
        dc = g_ref[...] * _dsilu(_conv(x, wv, b_ref[...]))
        dx, dw, db = _conv_bwd(x, dc, wv)
        dx_ref[...] = dx

        @pl.when(pl.program_id(1) == 0)
        def _():
            dw_ref[...] = jnp.zeros_like(dw_ref)
            db_ref[...] = jnp.zeros_like(db_ref)

        dw_ref[...] += dw
        db_ref[...] += db

    blk = pl.BlockSpec((seq, ct), lambda j, i: (i, j))
    wspec, bspec = pl.BlockSpec((k, ct), lambda j, i: (0, j)), pl.BlockSpec((1, ct), lambda j, i: (0, j))
    return pl.pallas_call(
        body, name=name, grid=(c // ct, bsz), in_specs=[blk, blk, wspec, bspec], out_specs=[blk, wspec, bspec],
        out_shape=[jax.ShapeDtypeStruct((t, c), F32), jax.ShapeDtypeStruct((k, c), F32),
                   jax.ShapeDtypeStruct((1, c), F32)],
        compiler_params=_params(("parallel", "arbitrary")))(xbc, dact, w, b)


def _ffn_specs(seq, ct, k, nblk):
    val = pl.BlockSpec((seq, ct), lambda j, i: (i, j))
    gate = pl.BlockSpec((seq, ct), lambda j, i: (i, nblk + j))
    wv, wg = pl.BlockSpec((k, ct), lambda j, i: (0, j)), pl.BlockSpec((k, ct), lambda j, i: (0, nblk + j))
    bv, bg = pl.BlockSpec((1, ct), lambda j, i: (0, j)), pl.BlockSpec((1, ct), lambda j, i: (0, nblk + j))
    return val, gate, wv, wg, bv, bg


def ffn_act_fwd(up, w, b, *, bsz, name):
    t = up.shape[0]
    half = up.shape[1] // 2
    seq, ct, k = t // bsz, 256, w.shape[0]
    val, gate, wv, wg, bv, bg = _ffn_specs(seq, ct, k, half // ct)

    def body(v_ref, g_ref, wv_ref, wg_ref, bv_ref, bg_ref, o_ref):
        vc = _conv(v_ref[...], wv_ref[...], bv_ref[...])
        gc = _conv(g_ref[...], wg_ref[...], bg_ref[...])
        o_ref[...] = (_silu(gc) * vc).astype(BF16)

    return pl.pallas_call(
        body, name=name, grid=(half // ct, bsz), in_specs=[val, gate, wv, wg, bv, bg], out_specs=val,
        out_shape=jax.ShapeDtypeStruct((t, half), BF16),
        compiler_params=_params(("parallel", "parallel")))(up, up, w, w, b, b)


def ffn_act_bwd(up, dact, w, b, *, bsz, name):
    t = up.shape[0]
    half = up.shape[1] // 2
    seq, ct, k = t // bsz, 256, w.shape[0]
    val, gate, wv, wg, bv, bg = _ffn_specs(seq, ct, k, half // ct)

    def body(v_ref, g_ref, wv_ref, wg_ref, bv_ref, bg_ref, d_ref, dv_ref, dg_ref, dwv_ref, dwg_ref, dbv_ref, dbg_ref):
        v, g = v_ref[...], g_ref[...]
        vc = _conv(v, wv_ref[...], bv_ref[...])
        gc = _conv(g, wg_ref[...], bg_ref[...])
        d = d_ref[...].astype(F32)
        dv, dwv, dbv = _conv_bwd(v, d * _silu(gc), wv_ref[...])
        dg, dwg, dbg = _conv_bwd(g, d * vc * _dsilu(gc), wg_ref[...])
        dv_ref[...] = dv.astype(BF16)
        dg_ref[...] = dg.astype(BF16)

        @pl.when(pl.program_id(1) == 0)
        def _():
            for r in (dwv_ref, dwg_ref, dbv_ref, dbg_ref):
                r[...] = jnp.zeros_like(r)

        dwv_ref[...] += dwv
        dwg_ref[...] += dwg
        dbv_ref[...] += dbv
        dbg_ref[...] += dbg

    return pl.pallas_call(
        body, name=name, grid=(half // ct, bsz), in_specs=[val, gate, wv, wg, bv, bg, val],
        out_specs=[val, val, wv, wv, bv, bv],
        out_shape=[jax.ShapeDtypeStruct((t, half), BF16), jax.ShapeDtypeStruct((t, half), BF16),
                   jax.ShapeDtypeStruct((k, half), F32), jax.ShapeDtypeStruct((k, half), F32),
                   jax.ShapeDtypeStruct((1, half), F32), jax.ShapeDtypeStruct((1, half), F32)],
        compiler_params=_params(("parallel", "arbitrary")))(up, up, w, w, b, b, dact)


def _sel_row(a, h):
    oh = (lax.broadcasted_iota(jnp.int32, (a.shape[0], 1), 0) == h).astype(F32)
    return jnp.sum(a * oh, axis=0, keepdims=True)


def _ssd_chunk(xp, dtr, bm, cm, prev, bias_r, alog_r, dskip_r, rev):
    q = dtr.shape[1]
    ri = lax.broadcasted_iota(jnp.int32, (q, q), 0)
    ci = lax.broadcasted_iota(jnp.int32, (q, q), 1)
    mask = (ci >= ri) if rev else (ci <= ri)
    lane_lo, row_lo = ci < HDIM, ri < HDIM
    dt_r = _softplus(dtr + bias_r)
    dta_r = dt_r * (-jnp.exp(alog_r))
    cs_r = cum_row(dta_r, rev)
    scores = dot_nt(cm, bm)

    def per_row(v):
        return jnp.broadcast_to(v, (q, q)).T

    ys, news = [], []
    for p in range(HPG // 2):
        ha = 2 * p + (HPG if rev else 0)
        hb = ha + 1
        cs_a, cs_b = _sel_row(cs_r, ha), _sel_row(cs_r, hb)
        csq_a, csq_b = per_row(cs_a), per_row(cs_b)
        seg_a = jnp.exp(jnp.where(mask, csq_a - cs_a, -1e30))
        seg_b = jnp.exp(jnp.where(mask, csq_b - cs_b, -1e30))
        csq = jnp.where(lane_lo, csq_a, csq_b)
        xdt = xp[p] * jnp.where(lane_lo, per_row(_sel_row(dt_r, ha)), per_row(_sel_row(dt_r, hb)))
        tot_a = jnp.sum(_sel_row(dta_r, ha), axis=1, keepdims=True)
        tot_b = jnp.sum(_sel_row(dta_r, hb), axis=1, keepdims=True)
        y = jnp.where(lane_lo, dot_nn(scores * seg_a, xdt), dot_nn(scores * seg_b, xdt))
        y = y + dot_nt(cm, prev[p]) * jnp.exp(csq)
        if not rev:
            y = y + jnp.where(lane_lo, _sel_row(dskip_r, ha), _sel_row(dskip_r, hb)) * xp[p]
        ys.append(y)
        st = dot_tn(xdt * jnp.exp(jnp.where(lane_lo, tot_a, tot_b) - csq), bm)
        news.append(jnp.exp(jnp.where(row_lo, tot_a, tot_b)) * prev[p] + st)
    return tuple(ys), tuple(news)


NPAIR = HPG // 2


def _ssd_specs(seq, nc):
    xs = pl.BlockSpec((None, seq, HPG * HDIM), lambda b, g: (b, 0, g))
    bm = pl.BlockSpec((None, seq, NSTATE), lambda b, g: (b, 0, SSD_W // NSTATE + g))
    cm = pl.BlockSpec((None, seq, NSTATE), lambda b, g: (b, 0, SSD_W // NSTATE + SGROUPS + g))
    dtr = pl.BlockSpec((None, None, 2 * HPG, seq), lambda b, g: (b, g, 0, 0))
    pr = pl.BlockSpec((None, 2 * HPG, 1), lambda b, g: (g, 0, 0))
    st = pl.BlockSpec((None, None, 2, nc, NPAIR, 2 * HDIM, NSTATE), lambda b, g: (b, g, 0, 0, 0, 0, 0))
    return xs, bm, cm, dtr, pr, st


def _pair_cols(p):
    return slice(2 * HDIM * p, 2 * HDIM * (p + 1))


def ssd_scan_fwd(act, dtr, prs, *, name):
    bsz, seq, _ = act.shape
    nc = seq // QC
    xs, bm, cm, dtrs, pr, st = _ssd_specs(seq, nc)

    def body(x_ref, b_ref, c_ref, dtr_ref, br_ref, ar_ref, dk_ref, y_ref, st_ref):
        par = (br_ref[...], ar_ref[...], dk_ref[...])
        y_ref[...] = jnp.zeros_like(y_ref)

        def step(i, carry):
            new = []
            for rev in (False, True):
                k = (nc - 1 - i) if rev else i
                rows = pl.ds(pl.multiple_of(k * QC, QC), QC)
                xp = tuple(x_ref[rows, _pair_cols(p)] for p in range(NPAIR))
                for p in range(NPAIR):
                    st_ref[int(rev), k, p] = carry[rev][p]
                ys, nw = _ssd_chunk(xp, dtr_ref[:, rows], b_ref[rows, :], c_ref[rows, :], carry[rev], *par, rev)
                for p in range(NPAIR):
                    y_ref[rows, _pair_cols(p)] += ys[p]
                new.append(nw)
            return tuple(new)

        zero = tuple(jnp.zeros((2 * HDIM, NSTATE), F32) for _ in range(NPAIR))
        lax.fori_loop(0, nc, step, (zero, zero))

    return pl.pallas_call(
        body, name=name, grid=(bsz, SGROUPS), in_specs=[xs, bm, cm, dtrs, pr, pr, pr], out_specs=[xs, st],
        out_shape=[jax.ShapeDtypeStruct((bsz, seq, SSD_W), F32),
                   jax.ShapeDtypeStruct((bsz, SGROUPS, 2, nc, NPAIR, 2 * HDIM, NSTATE), F32)],
        compiler_params=_params(("parallel", "parallel")))(act, act, act, dtr, *prs)


def ssd_scan_bwd(act, dtr, prs, states, dy, *, name):
    bsz, seq, _ = act.shape
    nc = seq // QC
    xs, bm, cm, dtrs, pr, st = _ssd_specs(seq, nc)
    grp = pl.BlockSpec((None, seq, NSTATE), lambda b, g: (b, 0, g))
    dpr = pl.BlockSpec((None, None, 2 * HPG, 1), lambda b, g: (b, g, 0, 0))

    def body(x_ref, b_ref, c_ref, dtr_ref, br_ref, ar_ref, dk_ref, st_ref, dy_ref,
             dx_ref, db_ref, dc_ref, ddtr_ref, gbr_ref, gar_ref, gdk_ref):
        par = (br_ref[...], ar_ref[...], dk_ref[...])
        pgrads = (gbr_ref, gar_ref, gdk_ref)
        for r in pgrads + (dx_ref, db_ref, dc_ref, ddtr_ref):
            r[...] = jnp.zeros_like(r)

        def bstep(i, dcarry):
            new = []
            for rev in (False, True):
                k = i if rev else (nc - 1 - i)
                rows = pl.ds(pl.multiple_of(k * QC, QC), QC)
                xp = tuple(x_ref[rows, _pair_cols(p)] for p in range(NPAIR))
                prev = tuple(st_ref[int(rev), k, p] for p in range(NPAIR))
                _, pull = jax.vjp(functools.partial(_ssd_chunk, rev=rev), xp, dtr_ref[:, rows], b_ref[rows, :],
                                  c_ref[rows, :], prev, *par)
                dyp = tuple(dy_ref[rows, _pair_cols(p)] for p in range(NPAIR))
                gx, gdt, gb, gc, gprev, *gpar = pull((dyp, dcarry[rev]))
                for p in range(NPAIR):
                    dx_ref[rows, _pair_cols(p)] += gx[p]
                ddtr_ref[:, rows] += gdt
                db_ref[rows, :] += gb
                dc_ref[rows, :] += gc
                for r, g in zip(pgrads, gpar):
                    r[...] += g
                new.append(gprev)
            return tuple(new)

        zero = tuple(jnp.zeros((2 * HDIM, NSTATE), F32) for _ in range(NPAIR))
        lax.fori_loop(0, nc, bstep, (zero, zero))

    out_shape = [jax.ShapeDtypeStruct((bsz, seq, SSD_W), F32),
                 jax.ShapeDtypeStruct((bsz, seq, SGROUPS * NSTATE), F32),
                 jax.ShapeDtypeStruct((bsz, seq, SGROUPS * NSTATE), F32),
                 jax.ShapeDtypeStruct(dtr.shape, F32)]
    out_shape += [jax.ShapeDtypeStruct((bsz, SGROUPS, 2 * HPG, 1), F32)] * 3
    return pl.pallas_call(
        body, name=name, grid=(bsz, SGROUPS), in_specs=[xs, bm, cm, dtrs, pr, pr, pr, st, xs],
        out_specs=[xs, grp, grp, dtrs, dpr, dpr, dpr], out_shape=out_shape,
        compiler_params=_params(("parallel", "parallel")))(act, act, act, dtr, *prs, states, dy)


def _s5_direction(lam_re, lam_im, log_step, b_re, b_im, c_re, c_im, rev):
    q = S5_Q
    step = jnp.exp(log_step)[:, None]
    lr, li = lam_re * step, lam_im * step
    mag = jnp.exp(lr)
    ar, ai = mag * jnp.cos(li), mag * jnp.sin(li)
    den = lam_re * lam_re + lam_im * lam_im
    cr = ((ar - 1.0) * lam_re + ai * lam_im) / den
    ci = (ai * lam_re - (ar - 1.0) * lam_im) / den
    bbr = cr[..., None] * b_re - ci[..., None] * b_im
    bbi = cr[..., None] * b_im + ci[..., None] * b_re
    d = jnp.arange(q + 1, dtype=F32)[None, :, None]
    pm = jnp.exp(d * lr[:, None, :])
    pr, pi = pm * jnp.cos(d * li[:, None, :]), pm * jnp.sin(d * li[:, None, :])
    er = pr[..., None] * bbr[:, None] - pi[..., None] * bbi[:, None]
    ei = pr[..., None] * bbi[:, None] + pi[..., None] * bbr[:, None]
    hp = lax.Precision.HIGHEST
    k = (jnp.einsum('gcp,gdpz->gdcz', c_re, er[:, :q], precision=hp)
         - jnp.einsum('gcp,gdpz->gdcz', c_im, ei[:, :q], precision=hp))
    e = jnp.concatenate([er[:, :q], ei[:, :q]], axis=2)
    wt = jnp.transpose(e if rev else e[:, ::-1], (0, 1, 3, 2))
    p1r, p1i = pr[:, 1:], pi[:, 1:]
    if rev:
        p1r, p1i = p1r[:, ::-1], p1i[:, ::-1]
    m_re = c_re[:, None] * p1r[:, :, None, :] - c_im[:, None] * p1i[:, :, None, :]
    m_im = -c_re[:, None] * p1i[:, :, None, :] - c_im[:, None] * p1r[:, :, None, :]
    mt = jnp.transpose(jnp.concatenate([m_re, m_im], axis=-1), (0, 3, 1, 2))
    da = jnp.concatenate([pr[:, q], pr[:, q]], axis=-1)
    db = jnp.concatenate([-pi[:, q], pi[:, q]], axis=-1)
    return k, wt, mt, da, db


def _s5_operators(lf_re, lf_im, lsf, lb_re, lb_im, lsb, b_re, b_im, cf_re, cf_im, cb_re, cb_im):
    q = S5_Q
    kf, wtf, mtf, daf, dbf = _s5_direction(lf_re, lf_im, lsf, b_re, b_im, cf_re, cf_im, False)
    kb, wtb, mtb, dab, dbb = _s5_direction(lb_re, lb_im, lsb, b_re, b_im, cb_re, cb_im, True)
    t = np.arange(q)[:, None, None]
    s = np.arange(q)[None, :, None]
    d = np.arange(q)[None, None, :]
    ohf = jnp.asarray((t - s == d).astype(np.float32))
    ohb = jnp.asarray((s - t == d).astype(np.float32))
    hp = lax.Precision.HIGHEST
    tt = (jnp.einsum('tsd,gdcz->gsztc', ohf, kf, precision=hp) + jnp.einsum('tsd,gdcz->gsztc', ohb, kb, precision=hp))
    g = tt.shape[0]
    tt = tt.reshape(g, S5_QC, S5_QC)
    wt = jnp.concatenate([wtf.reshape(g, S5_QC, 2 * S5_P), wtb.reshape(g, S5_QC, 2 * S5_P)], axis=-1)
    mt = jnp.concatenate([mtf.reshape(g, 2 * S5_P, S5_QC), mtb.reshape(g, 2 * S5_P, S5_QC)], axis=1)
    return tt, wt, mt, jnp.concatenate([daf, dab], -1), jnp.concatenate([dbf, dbb], -1)


def _gspec(*shape):
    return pl.BlockSpec((None,) + shape, lambda g: (g,) + (0,) * len(shape))


def s5_state_in(u, wt, *, name):
    g, r, _ = u.shape

    def body(u_ref, w_ref, o_ref):
        o_ref[...] = _bd(u_ref[...], w_ref[...], 1, 0)

    return pl.pallas_call(
        body, name=name, grid=(g,), in_specs=[_gspec(r, S5_QC), _gspec(S5_QC, 4 * S5_P)],
        out_specs=_gspec(r, 4 * S5_P), out_shape=jax.ShapeDtypeStruct((g, r, 4 * S5_P), F32),
        compiler_params=_params(("parallel",)))(u, wt)


def _swap(h):
    return pltpu.roll(h, S5_P, 1)


def s5_carry_fwd(s, da, db, *, name):
    nck, rows, _ = s.shape
    w = 2 * S5_P

    def body(s_ref, da_ref, db_ref, h_ref):
        dirs = ((False, slice(0, w)), (True, slice(w, 2 * w)))
        coef = [(da_ref[:, cols], db_ref[:, cols]) for _, cols in dirs]

        def step(i, hs):
            new = []
            for (rev, cols), (a, b), h in zip(dirs, coef, hs):
                k = (nck - 1 - i) if rev else i
                h_ref[k, :, cols] = h
                new.append(a * h + b * _swap(h) + s_ref[k, :, cols])
            return tuple(new)

        z = jnp.zeros((rows, w), F32)
        lax.fori_loop(0, nck, step, (z, z), unroll=2)

    rt = min(CARRY_ROWS, rows)
    big, small = pl.BlockSpec((nck, rt, 2 * w), lambda i: (0, i, 0)), pl.BlockSpec((rt, 2 * w), lambda i: (i, 0))
    rows = rt
    return pl.pallas_call(
        body, name=name, grid=(s.shape[1] // rt,), in_specs=[big, small, small], out_specs=big,
        out_shape=jax.ShapeDtypeStruct(s.shape, F32), compiler_params=_params(("parallel",)))(s, da, db)


def s5_carry_bwd(hin, dh, da, db, *, name):
    nck, rows, _ = hin.shape
    w = 2 * S5_P

    def body(h_ref, dh_ref, da_ref, db_ref, ds_ref, gda_ref, gdb_ref):
        dirs = ((False, slice(0, w)), (True, slice(w, 2 * w)))
        coef = [(da_ref[:, cols], db_ref[:, cols]) for _, cols in dirs]

        def step(i, carries):
            new = []
            for (rev, cols), (a, b), (g, ga, gb) in zip(dirs, coef, carries):
                k = i if rev else (nck - 1 - i)
                ds_ref[k, :, cols] = g
                h = h_ref[k, :, cols]
                new.append((dh_ref[k, :, cols] + a * g + _swap(b * g), ga + g * h, gb + g * _swap(h)))
            return tuple(new)

        z = jnp.zeros((rows, w), F32)
        res = lax.fori_loop(0, nck, step, ((z, z, z), (z, z, z)), unroll=2)
        for (_, cols), (_, ga, gb) in zip(dirs, res):
            gda_ref[:, cols] = ga
            gdb_ref[:, cols] = gb

    rt = min(CARRY_ROWS, rows)
    big, small = pl.BlockSpec((nck, rt, 2 * w), lambda i: (0, i, 0)), pl.BlockSpec((rt, 2 * w), lambda i: (i, 0))
    rows = rt
    return pl.pallas_call(
        body, name=name, grid=(hin.shape[1] // rt,), in_specs=[big, big, small, small], out_specs=[big, small, small],
        out_shape=[jax.ShapeDtypeStruct(hin.shape, F32), jax.ShapeDtypeStruct(da.shape, F32),
                   jax.ShapeDtypeStruct(da.shape, F32)],
        compiler_params=_params(("parallel",)))(hin, dh, da, db)


def s5_out(u, hin, tt, mt, *, name):
    g, r, _ = u.shape

    def body(u_ref, h_ref, t_ref, m_ref, o_ref):
        o_ref[...] = _bd(u_ref[...], t_ref[...], 1, 0) + _bd(h_ref[...], m_ref[...], 1, 0)

    return pl.pallas_call(
        body, name=name, grid=(g,),
        in_specs=[_gspec(r, S5_QC), _gspec(r, 4 * S5_P), _gspec(S5_QC, S5_QC), _gspec(4 * S5_P, S5_QC)],
        out_specs=_gspec(r, S5_QC), out_shape=jax.ShapeDtypeStruct((g, r, S5_QC), F32),
        compiler_params=_params(("parallel",)))(u, hin, tt, mt)


def s5_out_bwd(dy, u, hin, tt, mt, *, name):
    g, r, _ = u.shape

    def body(dy_ref, u_ref, h_ref, t_ref, m_ref, dh_ref, dt_ref, dm_ref, du_ref):
        dy_v = dy_ref[...]
        dh_ref[...] = _bd(dy_v, m_ref[...], 1, 1)
        dt_ref[...] = _bd(u_ref[...], dy_v, 0, 0)
        dm_ref[...] = _bd(h_ref[...], dy_v, 0, 0)
        du_ref[...] = _bd(dy_v, t_ref[...], 1, 1)

    return pl.pallas_call(
        body, name=name, grid=(g,),
        in_specs=[_gspec(r, S5_QC), _gspec(r, S5_QC), _gspec(r, 4 * S5_P), _gspec(S5_QC, S5_QC),
                  _gspec(4 * S5_P, S5_QC)],
        out_specs=[_gspec(r, 4 * S5_P), _gspec(S5_QC, S5_QC), _gspec(4 * S5_P, S5_QC), _gspec(r, S5_QC)],
        out_shape=[jax.ShapeDtypeStruct((g, r, 4 * S5_P), F32), jax.ShapeDtypeStruct((g, S5_QC, S5_QC), F32),
                   jax.ShapeDtypeStruct((g, 4 * S5_P, S5_QC), F32), jax.ShapeDtypeStruct((g, r, S5_QC), F32)],
        compiler_params=_params(("parallel",)))(dy, u, hin, tt, mt)


def s5_state_in_bwd(ds, u, wt, du1, *, name):
    g, r, _ = u.shape

    def body(ds_ref, u_ref, w_ref, du1_ref, du_ref, dw_ref):
        ds_v = ds_ref[...]
        du_ref[...] = du1_ref[...] + _bd(ds_v, w_ref[...], 1, 1)
        dw_ref[...] = _bd(u_ref[...], ds_v, 0, 0)

    return pl.pallas_call(
        body, name=name, grid=(g,),
        in_specs=[_gspec(r, 4 * S5_P), _gspec(r, S5_QC), _gspec(S5_QC, 4 * S5_P), _gspec(r, S5_QC)],
        out_specs=[_gspec(r, S5_QC), _gspec(S5_QC, 4 * S5_P)],
        out_shape=[jax.ShapeDtypeStruct((g, r, S5_QC), F32), jax.ShapeDtypeStruct((g, S5_QC, 4 * S5_P), F32)],
        compiler_params=_params(("parallel",)))(ds, u, wt, du1)


def _s5_post(ypre, u, dvec, wv, wg, bv, bg, nw):
    g = _gelu(ypre + dvec * u)
    out = (dot_nn(g, wv) + bv) * jax.nn.sigmoid(dot_nn(g, wg) + bg)
    return (_rms(out, nw),)


def _ssd_post(y, z, nw):
    return (_rms(y * _silu(z), nw),)


def _to_chunks(u, bsz):
    nck = u.shape[0] // bsz // S5_Q
    v = u.reshape(bsz, nck, S5_Q, S5_G, S5_C)
    return jnp.transpose(v, (3, 0, 1, 2, 4)).reshape(S5_G, bsz * nck, S5_QC)


def _from_chunks(y, bsz):
    nck = y.shape[1] // bsz
    v = y.reshape(S5_G, bsz, nck, S5_Q, S5_C)
    return jnp.transpose(v, (1, 2, 3, 0, 4)).reshape(bsz * nck * S5_Q, S5_W)


def _to_carry(s, bsz):
    nck = s.shape[1] // bsz
    return jnp.transpose(s.reshape(S5_G, bsz, nck, -1), (2, 0, 1, 3)).reshape(nck, S5_G * bsz, -1)


def _from_carry(h, bsz):
    nck = h.shape[0]
    return jnp.transpose(h.reshape(nck, S5_G, bsz, -1), (1, 2, 0, 3)).reshape(S5_G, bsz * nck, -1)


def _block_diag(w):
    eye = jnp.eye(S5_G, dtype=w.dtype)
    return jnp.einsum('gcd,gh->gchd', w, eye).reshape(S5_W, S5_W)


def _diag_blocks(w):
    v = w.reshape(S5_G, S5_C, S5_G, S5_C)
    return v[jnp.arange(S5_G), :, jnp.arange(S5_G), :]


def _dt_rows(dt, bsz):
    seq = dt.shape[0] // bsz
    return jnp.transpose(dt.reshape(bsz, seq, 2, SGROUPS, HPG), (0, 3, 2, 4, 1)).reshape(bsz, SGROUPS, 2 * HPG, seq)


def _dt_from_rows(dr):
    bsz, _, _, seq = dr.shape
    return jnp.transpose(dr.reshape(bsz, SGROUPS, 2, HPG, seq), (0, 4, 2, 1, 3)).reshape(bsz * seq, 2 * HEADS)


def _head_params(f, b):
    return jnp.concatenate([f.reshape(SGROUPS, HPG), b.reshape(SGROUPS, HPG)], axis=1)[:, :, None]


def _head_grads(gr):
    v = gr.sum(0)[:, :, 0]
    return v[:, :HPG].reshape(HEADS), v[:, HPG:].reshape(HEADS)


def local_step(x, target, w):
    bsz, seq, d = x.shape
    t = bsz * seq
    x2, tgt2 = x.reshape(t, d), target.reshape(t, d)
    g = {}
    row = lambda v: v.reshape(1, -1)
    bf = lambda v: v.astype(BF16)

    w_in = _unshard(bf(w['w_in']), SHARDED['w_in'])
    cuts = [0, SSD_W, SSD_W + XBC_W, SSD_W + XBC_W + 2 * HEADS, w_in.shape[1]]
    w_in_parts = [w_in[:, a:b] for a, b in zip(cuts[:-1], cuts[1:])]
    norm_mix = row(w['norm_mix_w'])
    (hn,) = rowmap_fwd(lambda a, nw: (_rms(a, nw),), [x2], [norm_mix], [(d, BF16)], name="rms_mix")
    z, xbc, dt, u = [matmul_sum([hn], [p], name=f"in_proj_{i}") for i, p in enumerate(w_in_parts)]

    conv_w, conv_b = _unshard(w['ssd_conv_w'], SHARDED['ssd_conv_w']), row(w['ssd_conv_b'])
    act = ssd_conv_fwd(xbc, conv_w, conv_b, bsz=bsz, name="ssd_conv")
    dtr = _dt_rows(dt, bsz)
    prs = (_head_params(w['ssd_dt_bias_fwd'], w['ssd_dt_bias_bwd']),
           _head_params(w['ssd_a_log_fwd'], w['ssd_a_log_bwd']),
           _head_params(w['ssd_d'], jnp.zeros_like(w['ssd_d'])))
    act3 = act.reshape(bsz, seq, XBC_W)
    y_scan, ssd_states = ssd_scan_fwd(act3, dtr, prs, name="ssd_scan")
    y_scan = y_scan.reshape(t, SSD_W)
    ssd_nw = row(w['ssd_norm_w'])
    (y_ssd,) = rowmap_fwd(_ssd_post, [y_scan, z], [ssd_nw], [(SSD_W, BF16)], name="ssd_post")

    s5_names = ['s5_lambda_re_fwd', 's5_lambda_im_fwd', 's5_log_step_fwd', 's5_lambda_re_bwd', 's5_lambda_im_bwd',
                's5_log_step_bwd', 's5_b_re', 's5_b_im', 's5_c_re_fwd', 's5_c_im_fwd', 's5_c_re_bwd', 's5_c_im_bwd']
    (tt, wt, mt, da, db), s5_pull = jax.vjp(_s5_operators, *[w[n] for n in s5_names])
    tt_b, wt_b, mt_b = bf(tt), bf(wt), bf(mt)
    da_r, db_r = jnp.repeat(da, bsz, axis=0), jnp.repeat(db, bsz, axis=0)
    uc = _to_chunks(u, bsz)
    s_in = _to_carry(s5_state_in(uc, wt_b, name="s5_state_in"), bsz)
    hin_c = s5_carry_fwd(s_in, da_r, db_r, name="s5_carry")
    hin = _from_carry(hin_c, bsz)
    ypre = _from_chunks(s5_out(uc, hin, tt_b, mt_b, name="s5_out"), bsz)
    glu_w = w['s5_glu_w']
    s5_par = [row(w['s5_d']), _block_diag(glu_w[:, :, :S5_C]), _block_diag(glu_w[:, :, S5_C:]),
              row(w['s5_glu_b'][:, :S5_C]), row(w['s5_glu_b'][:, S5_C:]), row(w['s5_norm_w'])]
    (y_s5,) = rowmap_fwd(_s5_post, [ypre, u], s5_par, [(S5_W, BF16)], name="s5_post")

    w_out = bf(w['w_out']).reshape(SSD_W + S5_W, d)
    h1 = matmul_sum([y_ssd, y_s5], [w_out[:SSD_W], w_out[SSD_W:]], add=x2, name="out_proj")
    norm_ffn = row(w['norm_ffn_w'])
    (hn2,) = rowmap_fwd(lambda a, nw: (_rms(a, nw),), [h1], [norm_ffn], [(d, BF16)], name="rms_ffn")
    pad_c = FFN_PAD - FFN_BLK
    half = N_DEV // 2
    w_up3 = jnp.pad(bf(w['ffn_w_up']), ((0, 0), (0, 0), (0, pad_c)))
    w_down = jnp.pad(bf(w['ffn_w_down']).reshape(half, FFN_BLK, d), ((0, 0), (0, pad_c), (0, 0)))
    w_down = w_down.reshape(half * FFN_PAD, d)
    fconv_w = jnp.pad(w['ffn_conv_w'], ((0, 0), (0, 0), (0, pad_c)))
    fconv_w = jnp.transpose(fconv_w, (1, 0, 2)).reshape(FCONV, N_DEV * FFN_PAD)
    fconv_b = row(jnp.pad(w['ffn_conv_b'].reshape(N_DEV, FFN_BLK), ((0, 0), (0, pad_c))))
    up = matmul_cols(hn2, w_up3, name="ffn_up")
    fact = ffn_act_fwd(up, fconv_w, fconv_b, bsz=bsz, name="ffn_act")
    h2 = matmul_sum([fact], [w_down], add=h1, name="ffn_down")
    loss, dh2, g_nf = loss_head(h2, tgt2, row(w['norm_final_w']), name="loss_head")
    g['norm_final_w'] = g_nf.reshape(-1)

    dfact = matmul_sum([dh2], [w_down], nt=True, name="ffn_down_dx")
    g_down = matmul_tn(fact, dh2, name="ffn_down_dw").reshape(half, FFN_PAD, d)[:, :FFN_BLK]
    g['ffn_w_down'] = g_down.reshape(N_DEV, FFN_BLK // 2, d)
    dval, dgate, dwv, dwg, dbv, dbg = ffn_act_bwd(up, dfact, fconv_w, fconv_b, bsz=bsz, name="ffn_act_bwd")
    g_cw = jnp.concatenate([dwv, dwg], axis=1).reshape(FCONV, N_DEV, FFN_PAD)[:, :, :FFN_BLK]
    g['ffn_conv_w'] = jnp.transpose(g_cw, (1, 0, 2))
    g['ffn_conv_b'] = jnp.concatenate([dbv, dbg], axis=1).reshape(N_DEV, FFN_PAD)[:, :FFN_BLK].reshape(-1)
    windows = [(dval, FFN_PAD, p) for p in range(half)] + [(dgate, FFN_PAD, p) for p in range(half)]
    dhn2 = matmul_sum(windows, [(w_up3, p) for p in range(N_DEV)], nt=True, name="ffn_up_dx")
    g['ffn_w_up'] = jnp.concatenate([matmul_tn(hn2, dval, out_blocks=half, name="ffn_up_dw_val"),
                                     matmul_tn(hn2, dgate, out_blocks=half, name="ffn_up_dw_gate")],
                                    axis=0)[:, :, :FFN_BLK]
    dh1, g_nffn = rowmap_bwd(lambda a, nw: (_rms(a, nw),), [h1], [norm_ffn], [dhn2], add=dh2, name="rms_ffn_bwd")
    g['norm_ffn_w'] = g_nffn.reshape(-1)

    dycat = matmul_sum([dh1], [w_out], nt=True, name="out_proj_dx")
    g['w_out'] = jnp.concatenate([matmul_tn(y_ssd, dh1, name="out_proj_dw_ssd"),
                                  matmul_tn(y_s5, dh1, name="out_proj_dw_s5")], axis=0).reshape(w['w_out'].shape)
    dy_scan, dz, g_snw = rowmap_bwd(_ssd_post, [y_scan, z], [ssd_nw], [(dycat, SSD_W, 0)], name="ssd_post_bwd")
    g['ssd_norm_w'] = g_snw.reshape(-1)
    dypre, du_a, g_d, g_wv, g_wg, g_bv, g_bg, g_s5nw = rowmap_bwd(
        _s5_post, [ypre, u], s5_par, [(dycat, S5_W, SSD_W // S5_W)], name="s5_post_bwd")
    g['s5_d'], g['s5_norm_w'] = g_d.reshape(-1), g_s5nw.reshape(-1)
    g['s5_glu_w'] = jnp.concatenate([_diag_blocks(g_wv), _diag_blocks(g_wg)], axis=-1)
    g['s5_glu_b'] = jnp.concatenate([g_bv.reshape(S5_G, S5_C), g_bg.reshape(S5_G, S5_C)], axis=-1)

    dyc = _to_chunks(dypre, bsz)
    dhin, dtt, dmt, du1 = s5_out_bwd(dyc, uc, hin, tt_b, mt_b, name="s5_out_bwd")
    ds_c, gda, gdb = s5_carry_bwd(hin_c, _to_carry(dhin, bsz), da_r, db_r, name="s5_carry_bwd")
    duc, dwt = s5_state_in_bwd(_from_carry(ds_c, bsz), uc, wt_b, du1, name="s5_state_in_bwd")
    du = du_a + _from_chunks(duc, bsz)
    fold = lambda v: v.reshape(S5_G, bsz, -1).sum(1)
    for n, gv in zip(s5_names, s5_pull((dtt, dwt, dmt, fold(gda), fold(gdb)))):
        g[n] = gv

    dxs, dbm, dcm, ddtr, gbr, gar, gdk = ssd_scan_bwd(
        act3, dtr, prs, ssd_states, dy_scan.reshape(bsz, seq, SSD_W), name="ssd_scan_bwd")
    g['ssd_dt_bias_fwd'], g['ssd_dt_bias_bwd'] = _head_grads(gbr)
    g['ssd_a_log_fwd'], g['ssd_a_log_bwd'] = _head_grads(gar)
    g['ssd_d'] = _head_grads(gdk)[0]
    dact = jnp.concatenate([dxs, dbm, dcm], axis=-1).reshape(t, XBC_W)
    dxbc, g_cw, g_cb = ssd_conv_bwd(xbc, dact, conv_w, conv_b, bsz=bsz, name="ssd_conv_bwd")
    g['ssd_conv_w'] = _shard_rows(g_cw, SHARDED['ssd_conv_w']).reshape(w['ssd_conv_w'].shape)
    g['ssd_conv_b'] = g_cb.reshape(-1)
    ddt = _dt_from_rows(ddtr)

    dparts = [dz, dxbc, ddt, du]
    dhn = matmul_sum(dparts, w_in_parts, nt=True, name="in_proj_dx")
    g_in = jnp.concatenate([matmul_tn(hn, dp, name=f"in_proj_dw_{i}") for i, dp in enumerate(dparts)], axis=1)
    g['w_in'] = _shard_rows(g_in, SHARDED['w_in']).reshape(w['w_in'].shape)
    dx, g_nmix = rowmap_bwd(lambda a, nw: (_rms(a, nw),), [x2], [norm_mix], [dhn], add=dh1, name="rms_mix_bwd")
    g['norm_mix_w'] = g_nmix.reshape(-1)
    return loss, dx.reshape(bsz, seq, d), g


ANY = pl.BlockSpec(memory_space=pl.ANY)


def all_gather(shards, *, name):
    n = len(shards)

    def body(*refs):
        x_refs, out_refs = refs[:n], refs[n:2 * n]
        send_sems, recv_sems, local_sems = refs[2 * n:]
        x, y, c = lax.axis_index("x"), lax.axis_index("y"), lax.axis_index("c")
        me, sibling = (x, y, c), (x, y, 1 - c)
        chips = [(1 - x, y), (x, 1 - y), (1 - x, 1 - y)]

        def copy(k, j, block, to, own=False):
            dst = out_refs[j].at[4 * block[0] + 2 * block[1] + block[2]]
            return pltpu.make_async_remote_copy(
                src_ref=x_refs[j] if own else dst, dst_ref=dst,
                send_sem=send_sems.at[k, j], recv_sem=recv_sems.at[k, j], device_id=to, device_id_type=MESH)

        mine = [pltpu.make_async_copy(x_refs[j], out_refs[j].at[4 * x + 2 * y + c], local_sems.at[j]) for j in range(n)]
        first = [copy(0, j, me, sibling, own=True) for j in range(n)]
        first += [copy(1 + i, j, me, (*chip, c), own=True) for i, chip in enumerate(chips) for j in range(n)]
        for cp in mine + first:
            cp.start()
        passed = []
        for i, chip in enumerate(chips):
            for j in range(n):
                copy(1 + i, j, (*chip, c), me).wait_recv()
                passed.append(copy(4 + i, j, (*chip, c), sibling))
                passed[-1].start()
        for j in range(n):
            copy(0, j, sibling, me).wait_recv()
        for i, chip in enumerate(chips):
            for j in range(n):
                copy(4 + i, j, (*chip, 1 - c), me).wait_recv()
        for cp in first + passed:
            cp.wait_send()
        for cp in mine:
            cp.wait()

    return pl.pallas_call(
        body, name=name, out_shape=[jax.ShapeDtypeStruct((N_DEV,) + s.shape, s.dtype) for s in shards],
        in_specs=[ANY] * n, out_specs=[ANY] * n,
        scratch_shapes=[pltpu.SemaphoreType.DMA((7, n)), pltpu.SemaphoreType.DMA((7, n)),
                        pltpu.SemaphoreType.DMA((n,))],
    )(*shards)


def exchange(sends, *, name):
    n = len(sends)

    def body(*refs):
        send_refs, recv_refs = refs[:n], refs[n:2 * n]
        send_sems, recv_sems, local_sems = refs[2 * n:]
        x, y, c = lax.axis_index("x"), lax.axis_index("y"), lax.axis_index("c")
        me = 4 * x + 2 * y + c
        local = [pltpu.make_async_copy(send_refs[j].at[me], recv_refs[j].at[me], local_sems.at[j]) for j in range(n)]
        for cp in local:
            cp.start()
        copies = []
        for k in range(1, N_DEV):
            px = (1 - x) if k & 4 else x
            py = (1 - y) if k & 2 else y
            pc = (1 - c) if k & 1 else c
            for j in range(n):
                copies.append(pltpu.make_async_remote_copy(
                    src_ref=send_refs[j].at[4 * px + 2 * py + pc], dst_ref=recv_refs[j].at[me],
                    send_sem=send_sems.at[k - 1, j], recv_sem=recv_sems.at[k - 1, j],
                    device_id=(px, py, pc), device_id_type=MESH))
        for cp in copies:
            cp.start()
        for cp in copies:
            cp.wait()
        for cp in local:
            cp.wait()

    return pl.pallas_call(
        body, name=name, out_shape=[jax.ShapeDtypeStruct(s.shape, s.dtype) for s in sends],
        in_specs=[ANY] * n, out_specs=[ANY] * n,
        scratch_shapes=[pltpu.SemaphoreType.DMA((N_DEV - 1, n)), pltpu.SemaphoreType.DMA((N_DEV - 1, n)),
                        pltpu.SemaphoreType.DMA((n,))],
    )(*sends)


def _adam_rows(r, c):
    fits = [t for t in range(8, r + 1, 8) if r % t == 0 and N_DEV * t * c * 4 <= 6 * 2 ** 20]
    return max(fits) if fits else r


def adamw(recv, w, m, v, *, name):
    _, r, n = recv.shape
    tr = _adam_rows(r, n)

    def body(r_ref, w_ref, m_ref, v_ref, g_ref, d_ref, nm_ref, nv_ref):
        g = r_ref[0].astype(F32)
        for s in range(1, N_DEV):
            g = g + r_ref[s].astype(F32)
        m_new = ADAM_B1 * m_ref[...] + (1.0 - ADAM_B1) * g
        v_new = ADAM_B2 * v_ref[...] + (1.0 - ADAM_B2) * jnp.square(g)
        m_hat = m_new / (1.0 - ADAM_B1 ** ADAM_STEP)
        v_hat = v_new / (1.0 - ADAM_B2 ** ADAM_STEP)
        g_ref[...] = g
        d_ref[...] = -ADAM_LR * (m_hat / (jnp.sqrt(v_hat) + ADAM_EPS) + ADAM_WD * w_ref[...])
        nm_ref[...] = m_new
        nv_ref[...] = v_new

    blk = pl.BlockSpec((tr, n), lambda i: (i, 0))
    return pl.pallas_call(
        body, name=name, grid=(r // tr,), in_specs=[pl.BlockSpec((N_DEV, tr, n), lambda i: (0, i, 0)), blk, blk, blk],
        out_specs=[blk] * 4, out_shape=[jax.ShapeDtypeStruct((r, n), F32)] * 4,
        compiler_params=_params(("parallel",)))(recv, w, m, v)


def _shard_rows(full, axis):
    if axis == 0:
        return full.reshape(N_DEV, -1)
    r, c = full.shape
    return jnp.transpose(full.reshape(r, N_DEV, c // N_DEV), (1, 0, 2)).reshape(N_DEV, -1)


def _unshard(blocks, axis):
    if axis == 0:
        return blocks.reshape(-1, blocks.shape[-1])
    return jnp.transpose(blocks, (1, 0, 2)).reshape(blocks.shape[1], -1)


def kernel(x, norm_mix_w, w_in, ssd_conv_w, ssd_conv_b, ssd_dt_bias_fwd, ssd_dt_bias_bwd, ssd_a_log_fwd, ssd_a_log_bwd, ssd_d, ssd_norm_w, s5_lambda_re_fwd, s5_lambda_im_fwd, s5_log_step_fwd, s5_lambda_re_bwd, s5_lambda_im_bwd, s5_log_step_bwd, s5_b_re, s5_b_im, s5_c_re_fwd, s5_c_im_fwd, s5_c_re_bwd, s5_c_im_bwd, s5_d, s5_glu_w, s5_glu_b, s5_norm_w, w_out, norm_ffn_w, ffn_w_up, ffn_conv_w, ffn_conv_b, ffn_w_down, norm_final_w, loss_target, m_norm_mix_w, m_w_in, m_ssd_conv_w, m_ssd_conv_b, m_ssd_dt_bias_fwd, m_ssd_dt_bias_bwd, m_ssd_a_log_fwd, m_ssd_a_log_bwd, m_ssd_d, m_ssd_norm_w, m_s5_lambda_re_fwd, m_s5_lambda_im_fwd, m_s5_log_step_fwd, m_s5_lambda_re_bwd, m_s5_lambda_im_bwd, m_s5_log_step_bwd, m_s5_b_re, m_s5_b_im, m_s5_c_re_fwd, m_s5_c_im_fwd, m_s5_c_re_bwd, m_s5_c_im_bwd, m_s5_d, m_s5_glu_w, m_s5_glu_b, m_s5_norm_w, m_w_out, m_norm_ffn_w, m_ffn_w_up, m_ffn_conv_w, m_ffn_conv_b, m_ffn_w_down, m_norm_final_w, v_norm_mix_w, v_w_in, v_ssd_conv_w, v_ssd_conv_b, v_ssd_dt_bias_fwd, v_ssd_dt_bias_bwd, v_ssd_a_log_fwd, v_ssd_a_log_bwd, v_ssd_d, v_ssd_norm_w, v_s5_lambda_re_fwd, v_s5_lambda_im_fwd, v_s5_log_step_fwd, v_s5_lambda_re_bwd, v_s5_lambda_im_bwd, v_s5_log_step_bwd, v_s5_b_re, v_s5_b_im, v_s5_c_re_fwd, v_s5_c_im_fwd, v_s5_c_re_bwd, v_s5_c_im_bwd, v_s5_d, v_s5_glu_w, v_s5_glu_b, v_s5_norm_w, v_w_out, v_norm_ffn_w, v_ffn_w_up, v_ffn_conv_w, v_ffn_conv_b, v_ffn_w_down, v_norm_final_w):
    args = dict(locals())
    strip = lambda n, v: v if n == 'norm_final_w' else v[0]
    w = {n: strip(n, args[n]) for n in WEIGHTS}

    mats = ['w_in', 'w_out', 'ffn_w_up', 'ffn_w_down']
    convs = ['ssd_conv_w', 'ffn_conv_w']
    stacks = all_gather([w[n].astype(BF16) for n in mats] + [w[n] for n in convs], name="weight_all_gather")
    full = dict(w)
    full.update(zip(mats + convs, stacks))

    loss, grad_x, g = local_step(x, loss_target, full)

    small = convs + [n for n in WEIGHTS if n not in SHARDED]
    pieces = [g[n].reshape(N_DEV, -1) if n in SHARDED else jnp.broadcast_to(g[n].reshape(1, -1), (N_DEV, g[n].size))
              for n in small]
    pieces.append(jnp.broadcast_to(loss.reshape(1, 1), (N_DEV, 1)))
    total = sum(p.shape[1] for p in pieces)
    nrow = -(-total // (PACK_ROWS * LANES)) * PACK_ROWS
    pieces.append(jnp.zeros((N_DEV, nrow * LANES - total), F32))
    packed_send = jnp.concatenate(pieces, axis=1).reshape(N_DEV, nrow, LANES)
    recvs = exchange([g[n].astype(BF16) for n in mats] + [packed_send], name="grad_exchange")

    outs = [{}, {}, {}, {}]
    for n, recv in zip(mats, recvs):
        res = adamw(recv, w[n], strip(n, args['m_' + n]), strip(n, args['v_' + n]), name="adamw_" + n)
        for o, p in zip(outs, res):
            o[n] = p.reshape(args[n].shape)

    def pack(prefix):
        vals = [strip(n, args[prefix + n]).reshape(-1) for n in small]
        return jnp.pad(jnp.concatenate(vals), (0, nrow * LANES - total + 1)).reshape(nrow, LANES)

    packed = adamw(recvs[-1], pack(''), pack('m_'), pack('v_'), name="adamw_small")
    packed = [p.reshape(-1) for p in packed]
    off = 0
    for n in small:
        size = w[n].size
        for o, p in zip(outs, packed):
            o[n] = p[off:off + size].reshape(args[n].shape)
        off += size
    loss_out = packed[0][off].reshape(())
    return (loss_out, grad_x, *[o[n] for o in outs for n in WEIGHTS])
```

```python
import functools

import jax
import jax.numpy as jnp
from jax import lax
from jax.experimental import pallas as pl
from jax.experimental.pallas import tpu as pltpu

F32, BF16 = jnp.float32, jnp.bfloat16
N_DEV = 8
D_MODEL = 1024
SSD_W, HEADS, HDIM, SGROUPS, HPG, NSTATE, SCONV, QC = 1024, 16, 64, 4, 4, 128, 5, 128
XBC_W = SSD_W + 2 * SGROUPS * NSTATE
S5_W, S5_G, S5_C, S5_P, S5_Q = 512, 32, 16, 64, 16
S5_QC = S5_Q * S5_C
CARRY_ROWS = 32
DFF, FCONV = 2816, 3
FFN_BLK, FFN_PAD = 704, 768
EPS = 1e-6
ADAM_LR, ADAM_B1, ADAM_B2, ADAM_EPS, ADAM_WD, ADAM_STEP = 0.001, 0.9, 0.999, 1e-08, 0.01, 10
LANES = 128
MESH = pl.DeviceIdType.MESH

WEIGHTS = ['norm_mix_w', 'w_in', 'ssd_conv_w', 'ssd_conv_b', 'ssd_dt_bias_fwd', 'ssd_dt_bias_bwd', 'ssd_a_log_fwd',
           'ssd_a_log_bwd', 'ssd_d', 'ssd_norm_w', 's5_lambda_re_fwd', 's5_lambda_im_fwd', 's5_log_step_fwd',
           's5_lambda_re_bwd', 's5_lambda_im_bwd', 's5_log_step_bwd', 's5_b_re', 's5_b_im', 's5_c_re_fwd', 's5_c_im_fwd',
           's5_c_re_bwd', 's5_c_im_bwd', 's5_d', 's5_glu_w', 's5_glu_b', 's5_norm_w', 'w_out', 'norm_ffn_w', 'ffn_w_up',
           'ffn_conv_w', 'ffn_conv_b', 'ffn_w_down', 'norm_final_w']
SHARDED = {'w_in': 1, 'ssd_conv_w': 1, 'w_out': 0, 'ffn_w_up': 1, 'ffn_conv_w': 1, 'ffn_w_down': 0}
FULL_SHAPE = {'w_in': (1024, 3616), 'ssd_conv_w': (5, 2048), 'w_out': (1536, 1024), 'ffn_w_up': (1024, 5632),
              'ffn_conv_w': (3, 5632), 'ffn_w_down': (2816, 1024)}
PACK_ROWS = 512


def _pick(n, cap=1536):
    if n <= cap:
        return n
    return max(t for t in range(LANES, cap + 1, LANES) if n % t == 0)


def _params(sem):
    return pltpu.CompilerParams(dimension_semantics=sem)


def _bd(a, b, ca, cb):
    return lax.dot_general(a.astype(BF16), b.astype(BF16), (((ca,), (cb,)), ((), ())), preferred_element_type=F32)


@jax.custom_vjp
def dot_nn(a, b):
    return _bd(a, b, 1, 0)


dot_nn.defvjp(lambda a, b: (_bd(a, b, 1, 0), (a, b)),
              lambda r, g: (_bd(g, r[1], 1, 1).astype(r[0].dtype), _bd(r[0], g, 0, 0).astype(r[1].dtype)))


@jax.custom_vjp
def dot_nt(a, b):
    return _bd(a, b, 1, 1)


dot_nt.defvjp(lambda a, b: (_bd(a, b, 1, 1), (a, b)),
              lambda r, g: (_bd(g, r[1], 1, 0).astype(r[0].dtype), _bd(g, r[0], 0, 0).astype(r[1].dtype)))


@jax.custom_vjp
def dot_tn(a, b):
    return _bd(a, b, 0, 0)


dot_tn.defvjp(lambda a, b: (_bd(a, b, 0, 0), (a, b)),
              lambda r, g: (_bd(r[1], g, 1, 1).astype(r[0].dtype), _bd(r[0], g, 1, 0).astype(r[1].dtype)))


def _split3(x):
    hi = x.astype(BF16)
    r = x - hi.astype(F32)
    mid = r.astype(BF16)
    lo = (r - mid.astype(F32)).astype(BF16)
    return hi, mid, lo


def _cum_matrix(q, upper):
    ri = lax.broadcasted_iota(jnp.int32, (q, q), 0)
    ci = lax.broadcasted_iota(jnp.int32, (q, q), 1)
    return jnp.where((ci >= ri) if upper else (ci <= ri), 1.0, 0.0).astype(BF16)


def _exact_right(x, mat):
    return sum(jnp.dot(p, mat, preferred_element_type=F32) for p in _split3(x))


@functools.partial(jax.custom_vjp, nondiff_argnums=(1,))
def cum_row(x, rev):
    return _exact_right(x, _cum_matrix(x.shape[1], not rev))


cum_row.defvjp(lambda x, rev: (cum_row(x, rev), None),
               lambda rev, _, g: (_exact_right(g, _cum_matrix(g.shape[1], rev)),))


def _softplus(x):
    return jnp.maximum(x, 0.0) + jnp.log(1.0 + jnp.exp(-jnp.abs(x)))


def _silu(x):
    return x * jax.nn.sigmoid(x)


def _gelu(x):
    return 0.5 * x * (1.0 + jnp.tanh(0.7978845608028654 * (x + 0.044715 * (x * x * x))))


def _rms(x, w):
    xf = x.astype(F32)
    return xf * lax.rsqrt(jnp.mean(xf * xf, axis=-1, keepdims=True) + EPS) * w


def matmul_sum(a_list, b_list, *, name, out_dtype=F32, add=None, tm=512, nt=False):
    a_arrs = [a[0] if isinstance(a, tuple) else a for a in a_list]
    b_arrs = [b[0] if isinstance(b, tuple) else b for b in b_list]
    m, n = a_arrs[0].shape[0], b_arrs[0].shape[-2 if nt else -1]
    tm, tn, k = min(tm, m), _pick(n), len(a_list)

    def body(*refs):
        acc = None
        for a_ref, b_ref in zip(refs[:k], refs[k:2 * k]):
            p = _bd(a_ref[...], b_ref[...], 1, 1 if nt else 0)
            acc = p if acc is None else acc + p
        if add is not None:
            acc = acc + refs[2 * k][...]
        refs[-1][...] = acc.astype(out_dtype)

    def a_spec(a):
        if isinstance(a, tuple):
            return pl.BlockSpec((tm, a[1]), lambda i, j, blk=a[2]: (i, blk))
        return pl.BlockSpec((tm, a.shape[1]), lambda i, j: (i, 0))

    def b_spec(b):
        arr, p = b if isinstance(b, tuple) else (b, None)
        kk = arr.shape[-1 if nt else -2]
        shape, idx = ((tn, kk), lambda j: (j, 0)) if nt else ((kk, tn), lambda j: (0, j))
        if p is None:
            return pl.BlockSpec(shape, lambda i, j: idx(j))
        return pl.BlockSpec((None,) + shape, lambda i, j, p=p: (p,) + idx(j))

    in_specs = [a_spec(a) for a in a_list] + [b_spec(b) for b in b_list]
    args = a_arrs + b_arrs
    if add is not None:
        in_specs.append(pl.BlockSpec((tm, tn), lambda i, j: (i, j)))
        args.append(add)
    return pl.pallas_call(
        body, name=name, grid=(m // tm, n // tn), in_specs=in_specs,
        out_specs=pl.BlockSpec((tm, tn), lambda i, j: (i, j)),
        out_shape=jax.ShapeDtypeStruct((m, n), out_dtype),
        compiler_params=_params(("parallel", "parallel")))(*args)


def matmul_cols(a, b3, *, name, out_dtype=F32, tm=1024):
    m, kk = a.shape
    p, _, nb = b3.shape
    tm, tn = min(tm, m), _pick(nb, 768)
    per = nb // tn

    def body(a_ref, b_ref, o_ref):
        o_ref[...] = _bd(a_ref[...], b_ref[...], 1, 0).astype(out_dtype)

    return pl.pallas_call(
        body, name=name, grid=(m // tm, p * per),
        in_specs=[pl.BlockSpec((tm, kk), lambda i, j: (i, 0)),
                  pl.BlockSpec((None, kk, tn), lambda i, j: (j // per, 0, j % per))],
        out_specs=pl.BlockSpec((tm, tn), lambda i, j: (i, j)),
        out_shape=jax.ShapeDtypeStruct((m, p * nb), out_dtype),
        compiler_params=_params(("parallel", "parallel")))(a, b3)


def matmul_tn(a, b, *, name, tm=1024, out_blocks=None):
    m, k = a.shape
    n = b.shape[1]
    nb = n // (out_blocks or 1)
    tm, tk, tn = min(tm, m), _pick(k), _pick(nb, 768 if out_blocks else 1536)
    per = nb // tn

    def body(a_ref, b_ref, o_ref):
        @pl.when(pl.program_id(2) == 0)
        def _():
            o_ref[...] = jnp.zeros_like(o_ref)

        o_ref[...] += _bd(a_ref[...], b_ref[...], 0, 0)

    if out_blocks:
        out_spec = pl.BlockSpec((None, tk, tn), lambda i, j, t: (j // per, i, j % per))
        out_shape = jax.ShapeDtypeStruct((out_blocks, k, nb), F32)
    else:
        out_spec = pl.BlockSpec((tk, tn), lambda i, j, t: (i, j))
        out_shape = jax.ShapeDtypeStruct((k, n), F32)
    return pl.pallas_call(
        body, name=name, grid=(k // tk, n // tn, m // tm),
        in_specs=[pl.BlockSpec((tm, tk), lambda i, j, t: (t, i)), pl.BlockSpec((tm, tn), lambda i, j, t: (t, j))],
        out_specs=out_spec, out_shape=out_shape,
        compiler_params=_params(("parallel", "parallel", "arbitrary")))(a, b)


def _row_spec(r, tm):
    if isinstance(r, tuple):
        arr, width, blk = r
        return arr, pl.BlockSpec((tm, width), lambda i, blk=blk: (i, blk))
    return r, pl.BlockSpec((tm, r.shape[1]), lambda i: (i, 0))


def _full_spec(p):
    return pl.BlockSpec(p.shape, lambda i: (0,) * p.ndim)


def rowmap_fwd(fn, rows, params, outs, *, name, tm=256):
    pairs = [_row_spec(r, tm) for r in rows]
    m = pairs[0][0].shape[0]
    tm = min(tm, m)
    pairs = [_row_spec(r, tm) for r in rows]
    nr, npar = len(rows), len(params)

    def body(*refs):
        res = fn(*[r[...] for r in refs[:nr + npar]])
        for o_ref, v in zip(refs[nr + npar:], res):
            o_ref[...] = v.astype(o_ref.dtype)

    return pl.pallas_call(
        body, name=name, grid=(m // tm,),
        in_specs=[s for _, s in pairs] + [_full_spec(p) for p in params],
        out_specs=[pl.BlockSpec((tm, c), lambda i: (i, 0)) for c, _ in outs],
        out_shape=[jax.ShapeDtypeStruct((m, c), dt) for c, dt in outs],
        compiler_params=_params(("parallel",)))(*[a for a, _ in pairs], *params)


def rowmap_bwd(fn, rows, params, cts, *, name, row_dtypes=None, add=None, tm=256):
    m = _row_spec(rows[0], tm)[0].shape[0]
    tm = min(tm, m)
    rp = [_row_spec(r, tm) for r in rows]
    cp = [_row_spec(c, tm) for c in cts]
    nr, npar, nc = len(rows), len(params), len(cts)
    row_dtypes = row_dtypes or [F32] * nr
    widths = [s.block_shape[1] for _, s in rp]

    def body(*refs):
        ins = [r[...] for r in refs[:nr + npar]]
        ins = [v.astype(F32) for v in ins]
        ct = tuple(r[...].astype(F32) for r in refs[nr + npar:nr + npar + nc])
        base = nr + npar + nc
        extra = None
        if add is not None:
            extra = refs[base][...]
            base += 1
        _, pull = jax.vjp(fn, *ins)
        grads = pull(ct)
        for j in range(nr):
            g = grads[j]
            if j == 0 and extra is not None:
                g = g + extra
            refs[base + j][...] = g.astype(refs[base + j].dtype)

        @pl.when(pl.program_id(0) == 0)
        def _():
            for j in range(npar):
                refs[base + nr + j][...] = jnp.zeros_like(refs[base + nr + j])

        for j in range(npar):
            refs[base + nr + j][...] += grads[nr + j]

    in_specs = [s for _, s in rp] + [_full_spec(p) for p in params] + [s for _, s in cp]
    args = [a for a, _ in rp] + list(params) + [a for a, _ in cp]
    if add is not None:
        in_specs.append(pl.BlockSpec((tm, widths[0]), lambda i: (i, 0)))
        args.append(add)
    out_specs = [pl.BlockSpec((tm, w), lambda i: (i, 0)) for w in widths] + [_full_spec(p) for p in params]
    out_shape = [jax.ShapeDtypeStruct((m, w), dt) for w, dt in zip(widths, row_dtypes)]
    out_shape += [jax.ShapeDtypeStruct(p.shape, F32) for p in params]
    return pl.pallas_call(
        body, name=name, grid=(m // tm,), in_specs=in_specs, out_specs=out_specs, out_shape=out_shape,
        compiler_params=_params(("arbitrary",)))(*args)


def loss_head(h, target, w, *, name, tm=256):
    m, d = h.shape
    tm = min(tm, m)

    def body(h_ref, t_ref, w_ref, loss_ref, dh_ref, dw_ref):
        y, pull = jax.vjp(_rms, h_ref[...], w_ref[...])
        err = y - t_ref[...]
        dh, dw = pull(err * (1.0 / d))

        @pl.when(pl.program_id(0) == 0)
        def _():
            loss_ref[...] = jnp.zeros_like(loss_ref)
            dw_ref[...] = jnp.zeros_like(dw_ref)

        loss_ref[...] += (0.5 / d) * jnp.sum(err * err, keepdims=True)
        dw_ref[...] += dw
        dh_ref[...] = dh

    row = pl.BlockSpec((tm, d), lambda i: (i, 0))
    return pl.pallas_call(
        body, name=name, grid=(m // tm,), in_specs=[row, row, _full_spec(w)],
        out_specs=[pl.BlockSpec((1, 1), lambda i: (0, 0)), row, _full_spec(w)],
        out_shape=[jax.ShapeDtypeStruct((1, 1), F32), jax.ShapeDtypeStruct((m, d), F32),
                   jax.ShapeDtypeStruct(w.shape, F32)],
        compiler_params=_params(("arbitrary",)))(h, target, w)


def _shift(x, s):
    if s == 0:
        return x
    n = x.shape[0]
    t = lax.broadcasted_iota(jnp.int32, x.shape, 0)
    rolled = pltpu.roll(x, (-s) % n, 0)
    return jnp.where((t + s >= 0) & (t + s < n), rolled, 0.0)


def _conv(x, w, b):
    k = w.shape[0]
    acc = b + w[k // 2:k // 2 + 1, :] * x
    for j in range(k):
        if j != k // 2:
            acc = acc + w[j:j + 1, :] * _shift(x, j - k // 2)
    return acc


def _conv_bwd(x, dc, w):
    k = w.shape[0]
    dx = None
    dws = []
    for j in range(k):
        s = j - k // 2
        term = w[j:j + 1, :] * _shift(dc, -s)
        dx = term if dx is None else dx + term
        dws.append(jnp.sum(dc * _shift(x, s), axis=0, keepdims=True))
    return dx, jnp.concatenate(dws, axis=0), jnp.sum(dc, axis=0, keepdims=True)


def _dsilu(c):
    s = jax.nn.sigmoid(c)
    return s * (1.0 + c * (1.0 - s))


def ssd_conv_fwd(xbc, w, b, *, bsz, name):
    t, c = xbc.shape
    seq, ct = t // bsz, 256

    def body(x_ref, w_ref, b_ref, o_ref):
        o_ref[...] = _silu(_conv(x_ref[...], w_ref[...], b_ref[...]))

    return pl.pallas_call(
        body, name=name, grid=(c // ct, bsz),
        in_specs=[pl.BlockSpec((seq, ct), lambda j, i: (i, j)), pl.BlockSpec((w.shape[0], ct), lambda j, i: (0, j)),
                  pl.BlockSpec((1, ct), lambda j, i: (0, j))],
        out_specs=pl.BlockSpec((seq, ct), lambda j, i: (i, j)),
        out_shape=jax.ShapeDtypeStruct((t, c), F32),
        compiler_params=_params(("parallel", "parallel")))(xbc, w, b)


def ssd_conv_bwd(xbc, dact, w, b, *, bsz, name):
    t, c = xbc.shape
    seq, ct, k = t // bsz, 256, w.shape[0]

    def body(x_ref, g_ref, w_ref, b_ref, dx_ref, dw_ref, db_ref):
        x, wv = x_ref[...], w_ref[...]
        dc = g_ref[...] * _dsilu(_conv(x, wv, b_ref[...]))
        dx, dw, db = _conv_bwd(x, dc, wv)
        dx_ref[...] = dx

        @pl.when(pl.program_id(1) == 0)
        def _():
            dw_ref[...] = jnp.zeros_like(dw_ref)
            db_ref[...] = jnp.zeros_like(db_ref)

        dw_ref[...] += dw
        db_ref[...] += db

    blk = pl.BlockSpec((seq, ct), lambda j, i: (i, j))
    wspec, bspec = pl.BlockSpec((k, ct), lambda j, i: (0, j)), pl.BlockSpec((1, ct), lambda j, i: (0, j))
    return pl.pallas_call(
        body, name=name, grid=(c // ct, bsz), in_specs=[blk, blk, wspec, bspec], out_specs=[blk, wspec, bspec],
        out_shape=[jax.ShapeDtypeStruct((t, c), F32), jax.ShapeDtypeStruct((k, c), F32),
                   jax.ShapeDtypeStruct((1, c), F32)],
        compiler_params=_params(("parallel", "arbitrary")))(xbc, dact, w, b)


def _ffn_specs(seq, ct, k, nblk):
    val = pl.BlockSpec((seq, ct), lambda j, i: (i, j))
    gate = pl.BlockSpec((seq, ct), lambda j, i: (i, nblk + j))
    wv, wg = pl.BlockSpec((k, ct), lambda j, i: (0, j)), pl.BlockSpec((k, ct), lambda j, i: (0, nblk + j))
    bv, bg = pl.BlockSpec((1, ct), lambda j, i: (0, j)), pl.BlockSpec((1, ct), lambda j, i: (0, nblk + j))
    return val, gate, wv, wg, bv, bg


def ffn_act_fwd(up, w, b, *, bsz, name):
    t = up.shape[0]
    half = up.shape[1] // 2
    seq, ct, k = t // bsz, 256, w.shape[0]
    val, gate, wv, wg, bv, bg = _ffn_specs(seq, ct, k, half // ct)

    def body(v_ref, g_ref, wv_ref, wg_ref, bv_ref, bg_ref, o_ref):
        vc = _conv(v_ref[...], wv_ref[...], bv_ref[...])
        gc = _conv(g_ref[...], wg_ref[...], bg_ref[...])
        o_ref[...] = (_silu(gc) * vc).astype(BF16)

    return pl.pallas_call(
        body, name=name, grid=(half // ct, bsz), in_specs=[val, gate, wv, wg, bv, bg], out_specs=val,
        out_shape=jax.ShapeDtypeStruct((t, half), BF16),
        compiler_params=_params(("parallel", "parallel")))(up, up, w, w, b, b)


def ffn_act_bwd(up, dact, w, b, *, bsz, name):
    t = up.shape[0]
    half = up.shape[1] // 2
    seq, ct, k = t // bsz, 256, w.shape[0]
    val, gate, wv, wg, bv, bg = _ffn_specs(seq, ct, k, half // ct)

    def body(v_ref, g_ref, wv_ref, wg_ref, bv_ref, bg_ref, d_ref, dv_ref, dg_ref, dwv_ref, dwg_ref, dbv_ref, dbg_ref):
        v, g = v_ref[...], g_ref[...]
        vc = _conv(v, wv_ref[...], bv_ref[...])
        gc = _conv(g, wg_ref[...], bg_ref[...])
        d = d_ref[...].astype(F32)
        dv, dwv, dbv = _conv_bwd(v, d * _silu(gc), wv_ref[...])
        dg, dwg, dbg = _conv_bwd(g, d * vc * _dsilu(gc), wg_ref[...])
        dv_ref[...] = dv.astype(BF16)
        dg_ref[...] = dg.astype(BF16)

        @pl.when(pl.program_id(1) == 0)
        def _():
            for r in (dwv_ref, dwg_ref, dbv_ref, dbg_ref):
                r[...] = jnp.zeros_like(r)

        dwv_ref[...] += dwv
        dwg_ref[...] += dwg
        dbv_ref[...] += dbv
        dbg_ref[...] += dbg

    return pl.pallas_call(
        body, name=name, grid=(half // ct, bsz), in_specs=[val, gate, wv, wg, bv, bg, val],
        out_specs=[val, val, wv, wv, bv, bv],
        out_shape=[jax.ShapeDtypeStruct((t, half), BF16), jax.ShapeDtypeStruct((t, half), BF16),
                   jax.ShapeDtypeStruct((k, half), F32), jax.ShapeDtypeStruct((k, half), F32),
                   jax.ShapeDtypeStruct((1, half), F32), jax.ShapeDtypeStruct((1, half), F32)],
        compiler_params=_params(("parallel", "arbitrary")))(up, up, w, w, b, b, dact)


def _sel_row(a, h):
    oh = (lax.broadcasted_iota(jnp.int32, (a.shape[0], 1), 0) == h).astype(F32)
    return jnp.sum(a * oh, axis=0, keepdims=True)


def _ssd_chunk(xp, dtr, bm, cm, prev, bias_r, alog_r, dskip_r, rev):
    q = dtr.shape[1]
    ri = lax.broadcasted_iota(jnp.int32, (q, q), 0)
    ci = lax.broadcasted_iota(jnp.int32, (q, q), 1)
    mask = (ci >= ri) if rev else (ci <= ri)
    lane_lo, row_lo = ci < HDIM, ri < HDIM
    dt_r = _softplus(dtr + bias_r)
    dta_r = dt_r * (-jnp.exp(alog_r))
    cs_r = cum_row(dta_r, rev)
    scores = dot_nt(cm, bm)

    def per_row(v):
        return jnp.broadcast_to(v, (q, q)).T

    ys, news = [], []
    for p in range(HPG // 2):
        ha = 2 * p + (HPG if rev else 0)
        hb = ha + 1
        cs_a, cs_b = _sel_row(cs_r, ha), _sel_row(cs_r, hb)
        csq_a, csq_b = per_row(cs_a), per_row(cs_b)
        seg_a = jnp.exp(jnp.where(mask, csq_a - cs_a, -1e30))
        seg_b = jnp.exp(jnp.where(mask, csq_b - cs_b, -1e30))
        csq = jnp.where(lane_lo, csq_a, csq_b)
        xdt = xp[p] * jnp.where(lane_lo, per_row(_sel_row(dt_r, ha)), per_row(_sel_row(dt_r, hb)))
        tot_a = jnp.sum(_sel_row(dta_r, ha), axis=1, keepdims=True)
        tot_b = jnp.sum(_sel_row(dta_r, hb), axis=1, keepdims=True)
        y = jnp.where(lane_lo, dot_nn(scores * seg_a, xdt), dot_nn(scores * seg_b, xdt))
        y = y + dot_nt(cm, prev[p]) * jnp.exp(csq)
        if not rev:
            y = y + jnp.where(lane_lo, _sel_row(dskip_r, ha), _sel_row(dskip_r, hb)) * xp[p]
        ys.append(y)
        st = dot_tn(xdt * jnp.exp(jnp.where(lane_lo, tot_a, tot_b) - csq), bm)
        news.append(jnp.exp(jnp.where(row_lo, tot_a, tot_b)) * prev[p] + st)
    return tuple(ys), tuple(news)


NPAIR = HPG // 2


def _ssd_specs(seq, nc):
    xs = pl.BlockSpec((None, seq, HPG * HDIM), lambda b, g: (b, 0, g))
    bm = pl.BlockSpec((None, seq, NSTATE), lambda b, g: (b, 0, SSD_W // NSTATE + g))
    cm = pl.BlockSpec((None, seq, NSTATE), lambda b, g: (b, 0, SSD_W // NSTATE + SGROUPS + g))
    dtr = pl.BlockSpec((None, None, 2 * HPG, seq), lambda b, g: (b, g, 0, 0))
    pr = pl.BlockSpec((None, 2 * HPG, 1), lambda b, g: (g, 0, 0))
    st = pl.BlockSpec((None, None, 2, nc, NPAIR, 2 * HDIM, NSTATE), lambda b, g: (b, g, 0, 0, 0, 0, 0))
    return xs, bm, cm, dtr, pr, st


def _pair_cols(p):
    return slice(2 * HDIM * p, 2 * HDIM * (p + 1))


def ssd_scan_fwd(act, dtr, prs, *, name):
    bsz, seq, _ = act.shape
    nc = seq // QC
    xs, bm, cm, dtrs, pr, st = _ssd_specs(seq, nc)

    def body(x_ref, b_ref, c_ref, dtr_ref, br_ref, ar_ref, dk_ref, y_ref, st_ref):
        par = (br_ref[...], ar_ref[...], dk_ref[...])
        y_ref[...] = jnp.zeros_like(y_ref)

        def step(i, carry):
            new = []
            for rev in (False, True):
                k = (nc - 1 - i) if rev else i
                rows = pl.ds(pl.multiple_of(k * QC, QC), QC)
                xp = tuple(x_ref[rows, _pair_cols(p)] for p in range(NPAIR))
                for p in range(NPAIR):
                    st_ref[int(rev), k, p] = carry[rev][p]
                ys, nw = _ssd_chunk(xp, dtr_ref[:, rows], b_ref[rows, :], c_ref[rows, :], carry[rev], *par, rev)
                for p in range(NPAIR):
                    y_ref[rows, _pair_cols(p)] += ys[p]
                new.append(nw)
            return tuple(new)

        zero = tuple(jnp.zeros((2 * HDIM, NSTATE), F32) for _ in range(NPAIR))
        lax.fori_loop(0, nc, step, (zero, zero))

    return pl.pallas_call(
        body, name=name, grid=(bsz, SGROUPS), in_specs=[xs, bm, cm, dtrs, pr, pr, pr], out_specs=[xs, st],
        out_shape=[jax.ShapeDtypeStruct((bsz, seq, SSD_W), F32),
                   jax.ShapeDtypeStruct((bsz, SGROUPS, 2, nc, NPAIR, 2 * HDIM, NSTATE), F32)],
        compiler_params=_params(("parallel", "parallel")))(act, act, act, dtr, *prs)


def ssd_scan_bwd(act, dtr, prs, states, dy, *, name):
    bsz, seq, _ = act.shape
    nc = seq // QC
    xs, bm, cm, dtrs, pr, st = _ssd_specs(seq, nc)
    grp = pl.BlockSpec((None, seq, NSTATE), lambda b, g: (b, 0, g))
    dpr = pl.BlockSpec((None, None, 2 * HPG, 1), lambda b, g: (b, g, 0, 0))

    def body(x_ref, b_ref, c_ref, dtr_ref, br_ref, ar_ref, dk_ref, st_ref, dy_ref,
             dx_ref, db_ref, dc_ref, ddtr_ref, gbr_ref, gar_ref, gdk_ref):
        par = (br_ref[...], ar_ref[...], dk_ref[...])
        pgrads = (gbr_ref, gar_ref, gdk_ref)
        for r in pgrads + (dx_ref, db_ref, dc_ref, ddtr_ref):
            r[...] = jnp.zeros_like(r)

        def bstep(i, dcarry):
            new = []
            for rev in (False, True):
                k = i if rev else (nc - 1 - i)
                rows = pl.ds(pl.multiple_of(k * QC, QC), QC)
                xp = tuple(x_ref[rows, _pair_cols(p)] for p in range(NPAIR))
                prev = tuple(st_ref[int(rev), k, p] for p in range(NPAIR))
                _, pull = jax.vjp(functools.partial(_ssd_chunk, rev=rev), xp, dtr_ref[:, rows], b_ref[rows, :],
                                  c_ref[rows, :], prev, *par)
                dyp = tuple(dy_ref[rows, _pair_cols(p)] for p in range(NPAIR))
                gx, gdt, gb, gc, gprev, *gpar = pull((dyp, dcarry[rev]))
                for p in range(NPAIR):
                    dx_ref[rows, _pair_cols(p)] += gx[p]
                ddtr_ref[:, rows] += gdt
                db_ref[rows, :] += gb
                dc_ref[rows, :] += gc
                for r, g in zip(pgrads, gpar):
                    r[...] += g
                new.append(gprev)
            return tuple(new)

        zero = tuple(jnp.zeros((2 * HDIM, NSTATE), F32) for _ in range(NPAIR))
        lax.fori_loop(0, nc, bstep, (zero, zero))

    out_shape = [jax.ShapeDtypeStruct((bsz, seq, SSD_W), F32),
                 jax.ShapeDtypeStruct((bsz, seq, SGROUPS * NSTATE), F32),
                 jax.ShapeDtypeStruct((bsz, seq, SGROUPS * NSTATE), F32),
                 jax.ShapeDtypeStruct(dtr.shape, F32)]
    out_shape += [jax.ShapeDtypeStruct((bsz, SGROUPS, 2 * HPG, 1), F32)] * 3
    return pl.pallas_call(
        body, name=name, grid=(bsz, SGROUPS), in_specs=[xs, bm, cm, dtrs, pr, pr, pr, st, xs],
        out_specs=[xs, grp, grp, dtrs, dpr, dpr, dpr], out_shape=out_shape,
        compiler_params=_params(("parallel", "parallel")))(act, act, act, dtr, *prs, states, dy)


def _s5_direction(lam_re, lam_im, log_step, b_re, b_im, c_re, c_im, rev):
    q = S5_Q
    step = jnp.exp(log_step)[:, None]
    lr, li = lam_re * step, lam_im * step
    mag = jnp.exp(lr)
    ar, ai = mag * jnp.cos(li), mag * jnp.sin(li)
    den = lam_re * lam_re + lam_im * lam_im
    cr = ((ar - 1.0) * lam_re + ai * lam_im) / den
    ci = (ai * lam_re - (ar - 1.0) * lam_im) / den
    bbr = cr[..., None] * b_re - ci[..., None] * b_im
    bbi = cr[..., None] * b_im + ci[..., None] * b_re
    d = jnp.arange(q + 1, dtype=F32)[None, :, None]
    pm = jnp.exp(d * lr[:, None, :])
    pr, pi = pm * jnp.cos(d * li[:, None, :]), pm * jnp.sin(d * li[:, None, :])
    er = pr[..., None] * bbr[:, None] - pi[..., None] * bbi[:, None]
    ei = pr[..., None] * bbi[:, None] + pi[..., None] * bbr[:, None]
    hp = lax.Precision.HIGHEST
    k = (jnp.einsum('gcp,gdpz->gdcz', c_re, er[:, :q], precision=hp)
         - jnp.einsum('gcp,gdpz->gdcz', c_im, ei[:, :q], precision=hp))
    e = jnp.concatenate([er[:, :q], ei[:, :q]], axis=2)
    wt = jnp.transpose(e if rev else e[:, ::-1], (0, 1, 3, 2))
    p1r, p1i = pr[:, 1:], pi[:, 1:]
    if rev:
        p1r, p1i = p1r[:, ::-1], p1i[:, ::-1]
    m_re = c_re[:, None] * p1r[:, :, None, :] - c_im[:, None] * p1i[:, :, None, :]
    m_im = -c_re[:, None] * p1i[:, :, None, :] - c_im[:, None] * p1r[:, :, None, :]
    mt = jnp.transpose(jnp.concatenate([m_re, m_im], axis=-1), (0, 3, 1, 2))
    da = jnp.concatenate([pr[:, q], pr[:, q]], axis=-1)
    db = jnp.concatenate([-pi[:, q], pi[:, q]], axis=-1)
    return k, wt, mt, da, db


def _s5_operators(lf_re, lf_im, lsf, lb_re, lb_im, lsb, b_re, b_im, cf_re, cf_im, cb_re, cb_im):
    q = S5_Q
    kf, wtf, mtf, daf, dbf = _s5_direction(lf_re, lf_im, lsf, b_re, b_im, cf_re, cf_im, False)
    kb, wtb, mtb, dab, dbb = _s5_direction(lb_re, lb_im, lsb, b_re, b_im, cb_re, cb_im, True)
    g = kf.shape[0]
    lags = jnp.concatenate([kb[:, :0:-1], kf[:, :1] + kb[:, :1], kf[:, 1:]], axis=1)
    rev_pad = jnp.pad(lags[:, ::-1], ((0, 0), (0, 1), (0, 0), (0, 0)))
    tiled = jnp.broadcast_to(rev_pad[:, None], (g, q, 2 * q, S5_C, S5_C)).reshape(g, 2 * q * q, S5_C, S5_C)
    skew = tiled[:, :q * (2 * q - 1)].reshape(g, q, 2 * q - 1, S5_C, S5_C)[:, :, q - 1:]
    tt = jnp.transpose(skew, (0, 2, 4, 1, 3)).reshape(g, S5_QC, S5_QC)
    wt = jnp.concatenate([wtf.reshape(g, S5_QC, 2 * S5_P), wtb.reshape(g, S5_QC, 2 * S5_P)], axis=-1)
    mt = jnp.concatenate([mtf.reshape(g, 2 * S5_P, S5_QC), mtb.reshape(g, 2 * S5_P, S5_QC)], axis=1)
    return tt, wt, mt, jnp.concatenate([daf, dab], -1), jnp.concatenate([dbf, dbb], -1)


def _gspec(*shape):
    return pl.BlockSpec((None,) + shape, lambda g: (g,) + (0,) * len(shape))


def s5_state_in(u, wt, *, name):
    g, r, _ = u.shape

    def body(u_ref, w_ref, o_ref):
        o_ref[...] = _bd(u_ref[...], w_ref[...], 1, 0)

    return pl.pallas_call(
        body, name=name, grid=(g,), in_specs=[_gspec(r, S5_QC), _gspec(S5_QC, 4 * S5_P)],
        out_specs=_gspec(r, 4 * S5_P), out_shape=jax.ShapeDtypeStruct((g, r, 4 * S5_P), F32),
        compiler_params=_params(("parallel",)))(u, wt)


def _swap(h):
    return pltpu.roll(h, S5_P, 1)


def s5_carry_fwd(s, da, db, *, name):
    nck, rows, _ = s.shape
    w = 2 * S5_P

    def body(s_ref, da_ref, db_ref, h_ref):
        dirs = ((False, slice(0, w)), (True, slice(w, 2 * w)))
        coef = [(da_ref[:, cols], db_ref[:, cols]) for _, cols in dirs]

        def step(i, hs):
            new = []
            for (rev, cols), (a, b), h in zip(dirs, coef, hs):
                k = (nck - 1 - i) if rev else i
                h_ref[k, :, cols] = h
                new.append(a * h + b * _swap(h) + s_ref[k, :, cols])
            return tuple(new)

        z = jnp.zeros((rows, w), F32)
        lax.fori_loop(0, nck, step, (z, z), unroll=2)

    rt = min(CARRY_ROWS, rows)
    big, small = pl.BlockSpec((nck, rt, 2 * w), lambda i: (0, i, 0)), pl.BlockSpec((rt, 2 * w), lambda i: (i, 0))
    rows = rt
    return pl.pallas_call(
        body, name=name, grid=(s.shape[1] // rt,), in_specs=[big, small, small], out_specs=big,
        out_shape=jax.ShapeDtypeStruct(s.shape, F32), compiler_params=_params(("parallel",)))(s, da, db)


def s5_carry_bwd(hin, dh, da, db, *, name):
    nck, rows, _ = hin.shape
    w = 2 * S5_P

    def body(h_ref, dh_ref, da_ref, db_ref, ds_ref, gda_ref, gdb_ref):
        dirs = ((False, slice(0, w)), (True, slice(w, 2 * w)))
        coef = [(da_ref[:, cols], db_ref[:, cols]) for _, cols in dirs]

        def step(i, carries):
            new = []
            for (rev, cols), (a, b), (g, ga, gb) in zip(dirs, coef, carries):
                k = i if rev else (nck - 1 - i)
                ds_ref[k, :, cols] = g
                h = h_ref[k, :, cols]
                new.append((dh_ref[k, :, cols] + a * g + _swap(b * g), ga + g * h, gb + g * _swap(h)))
            return tuple(new)

        z = jnp.zeros((rows, w), F32)
        res = lax.fori_loop(0, nck, step, ((z, z, z), (z, z, z)), unroll=2)
        for (_, cols), (_, ga, gb) in zip(dirs, res):
            gda_ref[:, cols] = ga
            gdb_ref[:, cols] = gb

    rt = min(CARRY_ROWS, rows)
    big, small = pl.BlockSpec((nck, rt, 2 * w), lambda i: (0, i, 0)), pl.BlockSpec((rt, 2 * w), lambda i: (i, 0))
    rows = rt
    return pl.pallas_call(
        body, name=name, grid=(hin.shape[1] // rt,), in_specs=[big, big, small, small], out_specs=[big, small, small],
        out_shape=[jax.ShapeDtypeStruct(hin.shape, F32), jax.ShapeDtypeStruct(da.shape, F32),
                   jax.ShapeDtypeStruct(da.shape, F32)],
        compiler_params=_params(("parallel",)))(hin, dh, da, db)


def s5_out(u, hin, tt, mt, *, name):
    g, r, _ = u.shape

    def body(u_ref, h_ref, t_ref, m_ref, o_ref):
        o_ref[...] = _bd(u_ref[...], t_ref[...], 1, 0) + _bd(h_ref[...], m_ref[...], 1, 0)

    return pl.pallas_call(
        body, name=name, grid=(g,),
        in_specs=[_gspec(r, S5_QC), _gspec(r, 4 * S5_P), _gspec(S5_QC, S5_QC), _gspec(4 * S5_P, S5_QC)],
        out_specs=_gspec(r, S5_QC), out_shape=jax.ShapeDtypeStruct((g, r, S5_QC), F32),
        compiler_params=_params(("parallel",)))(u, hin, tt, mt)


def s5_out_bwd(dy, u, hin, tt, mt, *, name):
    g, r, _ = u.shape

    def body(dy_ref, u_ref, h_ref, t_ref, m_ref, dh_ref, dt_ref, dm_ref, du_ref):
        dy_v = dy_ref[...]
        dh_ref[...] = _bd(dy_v, m_ref[...], 1, 1)
        dt_ref[...] = _bd(u_ref[...], dy_v, 0, 0)
        dm_ref[...] = _bd(h_ref[...], dy_v, 0, 0)
        du_ref[...] = _bd(dy_v, t_ref[...], 1, 1)

    return pl.pallas_call(
        body, name=name, grid=(g,),
        in_specs=[_gspec(r, S5_QC), _gspec(r, S5_QC), _gspec(r, 4 * S5_P), _gspec(S5_QC, S5_QC),
                  _gspec(4 * S5_P, S5_QC)],
        out_specs=[_gspec(r, 4 * S5_P), _gspec(S5_QC, S5_QC), _gspec(4 * S5_P, S5_QC), _gspec(r, S5_QC)],
        out_shape=[jax.ShapeDtypeStruct((g, r, 4 * S5_P), F32), jax.ShapeDtypeStruct((g, S5_QC, S5_QC), F32),
                   jax.ShapeDtypeStruct((g, 4 * S5_P, S5_QC), F32), jax.ShapeDtypeStruct((g, r, S5_QC), F32)],
        compiler_params=_params(("parallel",)))(dy, u, hin, tt, mt)


def s5_state_in_bwd(ds, u, wt, du1, *, name):
    g, r, _ = u.shape

    def body(ds_ref, u_ref, w_ref, du1_ref, du_ref, dw_ref):
        ds_v = ds_ref[...]
        du_ref[...] = du1_ref[...] + _bd(ds_v, w_ref[...], 1, 1)
        dw_ref[...] = _bd(u_ref[...], ds_v, 0, 0)

    return pl.pallas_call(
        body, name=name, grid=(g,),
        in_specs=[_gspec(r, 4 * S5_P), _gspec(r, S5_QC), _gspec(S5_QC, 4 * S5_P), _gspec(r, S5_QC)],
        out_specs=[_gspec(r, S5_QC), _gspec(S5_QC, 4 * S5_P)],
        out_shape=[jax.ShapeDtypeStruct((g, r, S5_QC), F32), jax.ShapeDtypeStruct((g, S5_QC, 4 * S5_P), F32)],
        compiler_params=_params(("parallel",)))(ds, u, wt, du1)


def _s5_post(ypre, u, dvec, wv, wg, bv, bg, nw):
    g = _gelu(ypre + dvec * u)
    out = (dot_nn(g, wv) + bv) * jax.nn.sigmoid(dot_nn(g, wg) + bg)
    return (_rms(out, nw),)


def _ssd_post(y, z, nw):
    return (_rms(y * _silu(z), nw),)


def _to_chunks(u, bsz):
    nck = u.shape[0] // bsz // S5_Q
    v = u.reshape(bsz, nck, S5_Q, S5_G, S5_C)
    return jnp.transpose(v, (3, 0, 1, 2, 4)).reshape(S5_G, bsz * nck, S5_QC)


def _from_chunks(y, bsz):
    nck = y.shape[1] // bsz
    v = y.reshape(S5_G, bsz, nck, S5_Q, S5_C)
    return jnp.transpose(v, (1, 2, 3, 0, 4)).reshape(bsz * nck * S5_Q, S5_W)


def _to_carry(s, bsz):
    nck = s.shape[1] // bsz
    return jnp.transpose(s.reshape(S5_G, bsz, nck, -1), (2, 0, 1, 3)).reshape(nck, S5_G * bsz, -1)


def _from_carry(h, bsz):
    nck = h.shape[0]
    return jnp.transpose(h.reshape(nck, S5_G, bsz, -1), (1, 2, 0, 3)).reshape(S5_G, bsz * nck, -1)


def _block_diag(w):
    eye = jnp.eye(S5_G, dtype=w.dtype)
    return jnp.einsum('gcd,gh->gchd', w, eye).reshape(S5_W, S5_W)


def _diag_blocks(w):
    v = w.reshape(S5_G, S5_C, S5_G, S5_C)
    return v[jnp.arange(S5_G), :, jnp.arange(S5_G), :]


def _dt_rows(dt, bsz):
    seq = dt.shape[0] // bsz
    return jnp.transpose(dt.reshape(bsz, seq, 2, SGROUPS, HPG), (0, 3, 2, 4, 1)).reshape(bsz, SGROUPS, 2 * HPG, seq)


def _dt_from_rows(dr):
    bsz, _, _, seq = dr.shape
    return jnp.transpose(dr.reshape(bsz, SGROUPS, 2, HPG, seq), (0, 4, 2, 1, 3)).reshape(bsz * seq, 2 * HEADS)


def _head_params(f, b):
    return jnp.concatenate([f.reshape(SGROUPS, HPG), b.reshape(SGROUPS, HPG)], axis=1)[:, :, None]


def _head_grads(gr):
    v = gr.sum(0)[:, :, 0]
    return v[:, :HPG].reshape(HEADS), v[:, HPG:].reshape(HEADS)


def local_step(x, target, w):
    bsz, seq, d = x.shape
    t = bsz * seq
    x2, tgt2 = x.reshape(t, d), target.reshape(t, d)
    g = {}
    row = lambda v: v.reshape(1, -1)
    bf = lambda v: v.astype(BF16)

    w_in = _unshard(bf(w['w_in']), SHARDED['w_in'])
    cuts = [0, SSD_W, SSD_W + XBC_W, SSD_W + XBC_W + 2 * HEADS, w_in.shape[1]]
    w_in_parts = [w_in[:, a:b] for a, b in zip(cuts[:-1], cuts[1:])]
    norm_mix = row(w['norm_mix_w']) + w.get('token', 0.0)
    (hn,) = rowmap_fwd(lambda a, nw: (_rms(a, nw),), [x2], [norm_mix], [(d, BF16)], name="rms_mix")
    z, xbc, dt, u = [matmul_sum([hn], [p], tm=1024, name=f"in_proj_{i}") for i, p in enumerate(w_in_parts)]

    conv_w, conv_b = _unshard(w['ssd_conv_w'], SHARDED['ssd_conv_w']), row(w['ssd_conv_b'])
    act = ssd_conv_fwd(xbc, conv_w, conv_b, bsz=bsz, name="ssd_conv")
    dtr = _dt_rows(dt, bsz)
    prs = (_head_params(w['ssd_dt_bias_fwd'], w['ssd_dt_bias_bwd']),
           _head_params(w['ssd_a_log_fwd'], w['ssd_a_log_bwd']),
           _head_params(w['ssd_d'], jnp.zeros_like(w['ssd_d'])))
    act3 = act.reshape(bsz, seq, XBC_W)
    y_scan, ssd_states = ssd_scan_fwd(act3, dtr, prs, name="ssd_scan")
    y_scan = y_scan.reshape(t, SSD_W)
    ssd_nw = row(w['ssd_norm_w'])
    (y_ssd,) = rowmap_fwd(_ssd_post, [y_scan, z], [ssd_nw], [(SSD_W, BF16)], name="ssd_post")

    s5_names = ['s5_lambda_re_fwd', 's5_lambda_im_fwd', 's5_log_step_fwd', 's5_lambda_re_bwd', 's5_lambda_im_bwd',
                's5_log_step_bwd', 's5_b_re', 's5_b_im', 's5_c_re_fwd', 's5_c_im_fwd', 's5_c_re_bwd', 's5_c_im_bwd']
    (tt, wt, mt, da, db), s5_pull = jax.vjp(_s5_operators, *[w[n] for n in s5_names])
    tt_b, wt_b, mt_b = bf(tt), bf(wt), bf(mt)
    da_r, db_r = jnp.repeat(da, bsz, axis=0), jnp.repeat(db, bsz, axis=0)
    uc = _to_chunks(u, bsz)
    s_in = _to_carry(s5_state_in(uc, wt_b, name="s5_state_in"), bsz)
    hin_c = s5_carry_fwd(s_in, da_r, db_r, name="s5_carry")
    hin = _from_carry(hin_c, bsz)
    ypre = _from_chunks(s5_out(uc, hin, tt_b, mt_b, name="s5_out"), bsz)
    glu_w = w['s5_glu_w']
    s5_par = [row(w['s5_d']), _block_diag(glu_w[:, :, :S5_C]), _block_diag(glu_w[:, :, S5_C:]),
              row(w['s5_glu_b'][:, :S5_C]), row(w['s5_glu_b'][:, S5_C:]), row(w['s5_norm_w'])]
    (y_s5,) = rowmap_fwd(_s5_post, [ypre, u], s5_par, [(S5_W, BF16)], name="s5_post")

    if 'late' in w:
        w = {**w, **w['late'](y_s5)}
    w_out = bf(w['w_out']).reshape(SSD_W + S5_W, d)
    h1 = matmul_sum([y_ssd, y_s5], [w_out[:SSD_W], w_out[SSD_W:]], add=x2, name="out_proj")
    norm_ffn = row(w['norm_ffn_w'])
    (hn2,) = rowmap_fwd(lambda a, nw: (_rms(a, nw),), [h1], [norm_ffn], [(d, BF16)], name="rms_ffn")
    pad_c = FFN_PAD - FFN_BLK
    half = N_DEV // 2
    w_up3 = jnp.pad(bf(w['ffn_w_up']), ((0, 0), (0, 0), (0, pad_c)))
    w_down = jnp.pad(bf(w['ffn_w_down']).reshape(half, FFN_BLK, d), ((0, 0), (0, pad_c), (0, 0)))
    w_down = w_down.reshape(half * FFN_PAD, d)
    fconv_w = jnp.pad(w['ffn_conv_w'], ((0, 0), (0, 0), (0, pad_c)))
    fconv_w = jnp.transpose(fconv_w, (1, 0, 2)).reshape(FCONV, N_DEV * FFN_PAD)
    fconv_b = row(jnp.pad(w['ffn_conv_b'].reshape(N_DEV, FFN_BLK), ((0, 0), (0, pad_c))))
    up = matmul_cols(hn2, w_up3, name="ffn_up")
    fact = ffn_act_fwd(up, fconv_w, fconv_b, bsz=bsz, name="ffn_act")
    h2 = matmul_sum([fact], [w_down], add=h1, name="ffn_down")
    loss, dh2, g_nf = loss_head(h2, tgt2, row(w['norm_final_w']), name="loss_head")
    g['norm_final_w'] = g_nf.reshape(-1)

    dfact = matmul_sum([dh2], [w_down], nt=True, tm=1024, name="ffn_down_dx")
    g_down = matmul_tn(fact, dh2, name="ffn_down_dw").reshape(half, FFN_PAD, d)[:, :FFN_BLK]
    g['ffn_w_down'] = g_down.reshape(N_DEV, FFN_BLK // 2, d)
    dval, dgate, dwv, dwg, dbv, dbg = ffn_act_bwd(up, dfact, fconv_w, fconv_b, bsz=bsz, name="ffn_act_bwd")
    g_cw = jnp.concatenate([dwv, dwg], axis=1).reshape(FCONV, N_DEV, FFN_PAD)[:, :, :FFN_BLK]
    g['ffn_conv_w'] = jnp.transpose(g_cw, (1, 0, 2))
    g['ffn_conv_b'] = jnp.concatenate([dbv, dbg], axis=1).reshape(N_DEV, FFN_PAD)[:, :FFN_BLK].reshape(-1)
    windows = [(dval, FFN_PAD, p) for p in range(half)] + [(dgate, FFN_PAD, p) for p in range(half)]
    dhn2 = matmul_sum(windows, [(w_up3, p) for p in range(N_DEV)], nt=True, name="ffn_up_dx")
    g['ffn_w_up'] = jnp.concatenate([matmul_tn(hn2, dval, out_blocks=half, name="ffn_up_dw_val"),
                                     matmul_tn(hn2, dgate, out_blocks=half, name="ffn_up_dw_gate")],
                                    axis=0)[:, :, :FFN_BLK]
    dh1, g_nffn = rowmap_bwd(lambda a, nw: (_rms(a, nw),), [h1], [norm_ffn], [dhn2], add=dh2, name="rms_ffn_bwd")
    g['norm_ffn_w'] = g_nffn.reshape(-1)

    dycat = matmul_sum([dh1], [w_out], nt=True, tm=1024, name="out_proj_dx")
    g['w_out'] = jnp.concatenate([matmul_tn(y_ssd, dh1, name="out_proj_dw_ssd"),
                                  matmul_tn(y_s5, dh1, name="out_proj_dw_s5")], axis=0).reshape(w['w_out'].shape)
    dy_scan, dz, g_snw = rowmap_bwd(_ssd_post, [y_scan, z], [ssd_nw], [(dycat, SSD_W, 0)], name="ssd_post_bwd")
    g['ssd_norm_w'] = g_snw.reshape(-1)
    dypre, du_a, g_d, g_wv, g_wg, g_bv, g_bg, g_s5nw = rowmap_bwd(
        _s5_post, [ypre, u], s5_par, [(dycat, S5_W, SSD_W // S5_W)], name="s5_post_bwd")
    g['s5_d'], g['s5_norm_w'] = g_d.reshape(-1), g_s5nw.reshape(-1)
    g['s5_glu_w'] = jnp.concatenate([_diag_blocks(g_wv), _diag_blocks(g_wg)], axis=-1)
    g['s5_glu_b'] = jnp.concatenate([g_bv.reshape(S5_G, S5_C), g_bg.reshape(S5_G, S5_C)], axis=-1)

    dyc = _to_chunks(dypre, bsz)
    dhin, dtt, dmt, du1 = s5_out_bwd(dyc, uc, hin, tt_b, mt_b, name="s5_out_bwd")
    ds_c, gda, gdb = s5_carry_bwd(hin_c, _to_carry(dhin, bsz), da_r, db_r, name="s5_carry_bwd")
    duc, dwt = s5_state_in_bwd(_from_carry(ds_c, bsz), uc, wt_b, du1, name="s5_state_in_bwd")
    du = du_a + _from_chunks(duc, bsz)
    fold = lambda v: v.reshape(S5_G, bsz, -1).sum(1)
    for n, gv in zip(s5_names, s5_pull((dtt, dwt, dmt, fold(gda), fold(gdb)))):
        g[n] = gv

    dxs, dbm, dcm, ddtr, gbr, gar, gdk = ssd_scan_bwd(
        act3, dtr, prs, ssd_states, dy_scan.reshape(bsz, seq, SSD_W), name="ssd_scan_bwd")
    g['ssd_dt_bias_fwd'], g['ssd_dt_bias_bwd'] = _head_grads(gbr)
    g['ssd_a_log_fwd'], g['ssd_a_log_bwd'] = _head_grads(gar)
    g['ssd_d'] = _head_grads(gdk)[0]
    dact = jnp.concatenate([dxs, dbm, dcm], axis=-1).reshape(t, XBC_W)
    dxbc, g_cw, g_cb = ssd_conv_bwd(xbc, dact, conv_w, conv_b, bsz=bsz, name="ssd_conv_bwd")
    g['ssd_conv_w'] = _shard_rows(g_cw, SHARDED['ssd_conv_w']).reshape(w['ssd_conv_w'].shape)
    g['ssd_conv_b'] = g_cb.reshape(-1)
    ddt = _dt_from_rows(ddtr)

    dparts = [dz, dxbc, ddt, du]
    dhn = matmul_sum(dparts, w_in_parts, nt=True, name="in_proj_dx")
    g_in = jnp.concatenate([matmul_tn(hn, dp, name=f"in_proj_dw_{i}") for i, dp in enumerate(dparts)], axis=1)
    g['w_in'] = _shard_rows(g_in, SHARDED['w_in']).reshape(w['w_in'].shape)
    dx, g_nmix = rowmap_bwd(lambda a, nw: (_rms(a, nw),), [x2], [norm_mix], [dhn], add=dh1, name="rms_mix_bwd")
    g['norm_mix_w'] = g_nmix.reshape(-1)
    return loss, dx.reshape(bsz, seq, d), g


ANY = pl.BlockSpec(memory_space=pl.ANY)


def all_gather(shards, *, name):
    n = len(shards)

    def body(*refs):
        x_refs, out_refs = refs[:n], refs[n:2 * n]
        send_sems, recv_sems, local_sems = refs[2 * n:]
        x, y, c = lax.axis_index("x"), lax.axis_index("y"), lax.axis_index("c")
        me, sibling = (x, y, c), (x, y, 1 - c)
        chips = [(1 - x, y), (x, 1 - y), (1 - x, 1 - y)]

        def copy(k, j, block, to, own=False):
            dst = out_refs[j].at[4 * block[0] + 2 * block[1] + block[2]]
            return pltpu.make_async_remote_copy(
                src_ref=x_refs[j] if own else dst, dst_ref=dst,
                send_sem=send_sems.at[k, j], recv_sem=recv_sems.at[k, j], device_id=to, device_id_type=MESH)

        mine = [pltpu.make_async_copy(x_refs[j], out_refs[j].at[4 * x + 2 * y + c], local_sems.at[j]) for j in range(n)]
        first = [copy(0, j, me, sibling, own=True) for j in range(n)]
        first += [copy(1 + i, j, me, (*chip, c), own=True) for i, chip in enumerate(chips) for j in range(n)]
        for cp in mine + first:
            cp.start()
        passed = []
        for i, chip in enumerate(chips):
            for j in range(n):
                copy(1 + i, j, (*chip, c), me).wait_recv()
                passed.append(copy(4 + i, j, (*chip, c), sibling))
                passed[-1].start()
        for j in range(n):
            copy(0, j, sibling, me).wait_recv()
        for i, chip in enumerate(chips):
            for j in range(n):
                copy(4 + i, j, (*chip, 1 - c), me).wait_recv()
        for cp in first + passed:
            cp.wait_send()
        for cp in mine:
            cp.wait()

    return pl.pallas_call(
        body, name=name, out_shape=[jax.ShapeDtypeStruct((N_DEV,) + s.shape, s.dtype) for s in shards],
        in_specs=[ANY] * n, out_specs=[ANY] * n,
        scratch_shapes=[pltpu.SemaphoreType.DMA((7, n)), pltpu.SemaphoreType.DMA((7, n)),
                        pltpu.SemaphoreType.DMA((n,))],
    )(*shards)


def exchange(sends, *, name):
    n = len(sends)

    def body(*refs):
        send_refs, recv_refs = refs[:n], refs[n:2 * n]
        send_sems, recv_sems, local_sems = refs[2 * n:]
        x, y, c = lax.axis_index("x"), lax.axis_index("y"), lax.axis_index("c")
        me = 4 * x + 2 * y + c
        local = [pltpu.make_async_copy(send_refs[j].at[me], recv_refs[j].at[me], local_sems.at[j]) for j in range(n)]
        for cp in local:
            cp.start()
        copies = []
        for k in range(1, N_DEV):
            px = (1 - x) if k & 4 else x
            py = (1 - y) if k & 2 else y
            pc = (1 - c) if k & 1 else c
            for j in range(n):
                copies.append(pltpu.make_async_remote_copy(
                    src_ref=send_refs[j].at[4 * px + 2 * py + pc], dst_ref=recv_refs[j].at[me],
                    send_sem=send_sems.at[k - 1, j], recv_sem=recv_sems.at[k - 1, j],
                    device_id=(px, py, pc), device_id_type=MESH))
        for cp in copies:
            cp.start()
        for cp in copies:
            cp.wait()
        for cp in local:
            cp.wait()

    return pl.pallas_call(
        body, name=name, out_shape=[jax.ShapeDtypeStruct(s.shape, s.dtype) for s in sends],
        in_specs=[ANY] * n, out_specs=[ANY] * n,
        scratch_shapes=[pltpu.SemaphoreType.DMA((N_DEV - 1, n)), pltpu.SemaphoreType.DMA((N_DEV - 1, n)),
                        pltpu.SemaphoreType.DMA((n,))],
    )(*sends)


HBM_SPEC = pl.BlockSpec(memory_space=pltpu.HBM)
SEM_SPEC = pl.BlockSpec(memory_space=pltpu.SEMAPHORE)
SPLIT_PARAMS = pltpu.CompilerParams(has_side_effects=pltpu.SideEffectType.DATAFLOW_SIDE_EFFECTING)


def _peer_copies(src_refs, land_refs, send_sems, recv_sems, indexed):
    x, y, c = lax.axis_index("x"), lax.axis_index("y"), lax.axis_index("c")
    me = 4 * x + 2 * y + c
    copies = []
    for k in range(1, N_DEV):
        px = (1 - x) if k & 4 else x
        py = (1 - y) if k & 2 else y
        pc = (1 - c) if k & 1 else c
        for j, (src, land) in enumerate(zip(src_refs, land_refs)):
            sem = (k - 1) * len(src_refs) + j
            copies.append(pltpu.make_async_remote_copy(
                src_ref=src.at[4 * px + 2 * py + pc] if indexed else src, dst_ref=land.at[me],
                send_sem=send_sems.at[sem], recv_sem=recv_sems.at[sem],
                device_id=(px, py, pc), device_id_type=MESH))
    return copies


def scatter_start(srcs, *, name, indexed):
    n = len(srcs)
    lands = [lax.empty(s.shape if indexed else (N_DEV,) + s.shape, s.dtype) for s in srcs]

    def body(*refs):
        send_sems, recv_sems = refs[2 * n], refs[2 * n + 1]
        for cp in _peer_copies(refs[:n], refs[n:2 * n], send_sems, recv_sems, indexed):
            cp.start()
        refs[-1][...] = jnp.zeros_like(refs[-1])

    hbm = lambda a: pltpu.HBM(a.shape, a.dtype)
    sems = pltpu.SemaphoreType.DMA(((N_DEV - 1) * n,))
    res = pl.pallas_call(
        body, name=name,
        out_shape=(sems, sems, *[hbm(a) for a in srcs + lands], jax.ShapeDtypeStruct((8, LANES), F32)),
        in_specs=[HBM_SPEC] * (2 * n),
        out_specs=(SEM_SPEC, SEM_SPEC, *[HBM_SPEC] * (2 * n), pl.BlockSpec(memory_space=pltpu.VMEM)),
        input_output_aliases={i: 2 + i for i in range(2 * n)}, compiler_params=SPLIT_PARAMS,
    )(*[pltpu.with_memory_space_constraint(a, pltpu.HBM) for a in srcs + lands])
    return res[0], res[1], list(res[2:2 + n]), list(res[2 + n:2 + 2 * n]), res[-1]


def scatter_wait(send_sems, recv_sems, srcs, lands, after, *, name, indexed):
    n = len(srcs)

    def body(*refs):
        for cp in _peer_copies(refs[:n], refs[n:2 * n], refs[2 * n], refs[2 * n + 1], indexed):
            cp.wait_send()
            cp.wait_recv()

    hbm = lambda a: pltpu.HBM(a.shape, a.dtype)
    res = pl.pallas_call(
        body, name=name, out_shape=tuple(hbm(a) for a in srcs + lands),
        in_specs=[HBM_SPEC] * (2 * n) + [SEM_SPEC, SEM_SPEC, ANY], out_specs=tuple([HBM_SPEC] * (2 * n)),
        input_output_aliases={i: i for i in range(2 * n)}, compiler_params=SPLIT_PARAMS,
    )(*srcs, *lands, send_sems, recv_sems, after)
    return list(res[:n]), list(res[n:])


def _adam_rows(r, c):
    fits = [t for t in range(8, r + 1, 8) if r % t == 0 and N_DEV * t * c * 4 <= 6 * 2 ** 20]
    return max(fits) if fits else r


def adamw(recv, w, m, v, *, name):
    _, r, n = recv.shape
    tr = _adam_rows(r, n)

    def body(r_ref, w_ref, m_ref, v_ref, g_ref, d_ref, nm_ref, nv_ref):
        g = r_ref[0].astype(F32)
        for s in range(1, N_DEV):
            g = g + r_ref[s].astype(F32)
        m_new = ADAM_B1 * m_ref[...] + (1.0 - ADAM_B1) * g
        v_new = ADAM_B2 * v_ref[...] + (1.0 - ADAM_B2) * jnp.square(g)
        m_hat = m_new / (1.0 - ADAM_B1 ** ADAM_STEP)
        v_hat = v_new / (1.0 - ADAM_B2 ** ADAM_STEP)
        g_ref[...] = g
        d_ref[...] = -ADAM_LR * (m_hat / (jnp.sqrt(v_hat) + ADAM_EPS) + ADAM_WD * w_ref[...])
        nm_ref[...] = m_new
        nv_ref[...] = v_new

    blk = pl.BlockSpec((tr, n), lambda i: (i, 0))
    return pl.pallas_call(
        body, name=name, grid=(r // tr,), in_specs=[pl.BlockSpec((N_DEV, tr, n), lambda i: (0, i, 0)), blk, blk, blk],
        out_specs=[blk] * 4, out_shape=[jax.ShapeDtypeStruct((r, n), F32)] * 4,
        compiler_params=_params(("parallel",)))(recv, w, m, v)


def _shard_rows(full, axis):
    if axis == 0:
        return full.reshape(N_DEV, -1)
    r, c = full.shape
    return jnp.transpose(full.reshape(r, N_DEV, c // N_DEV), (1, 0, 2)).reshape(N_DEV, -1)


def _unshard(blocks, axis):
    if axis == 0:
        return blocks.reshape(-1, blocks.shape[-1])
    return jnp.transpose(blocks, (1, 0, 2)).reshape(blocks.shape[1], -1)


def kernel(x, norm_mix_w, w_in, ssd_conv_w, ssd_conv_b, ssd_dt_bias_fwd, ssd_dt_bias_bwd, ssd_a_log_fwd, ssd_a_log_bwd, ssd_d, ssd_norm_w, s5_lambda_re_fwd, s5_lambda_im_fwd, s5_log_step_fwd, s5_lambda_re_bwd, s5_lambda_im_bwd, s5_log_step_bwd, s5_b_re, s5_b_im, s5_c_re_fwd, s5_c_im_fwd, s5_c_re_bwd, s5_c_im_bwd, s5_d, s5_glu_w, s5_glu_b, s5_norm_w, w_out, norm_ffn_w, ffn_w_up, ffn_conv_w, ffn_conv_b, ffn_w_down, norm_final_w, loss_target, m_norm_mix_w, m_w_in, m_ssd_conv_w, m_ssd_conv_b, m_ssd_dt_bias_fwd, m_ssd_dt_bias_bwd, m_ssd_a_log_fwd, m_ssd_a_log_bwd, m_ssd_d, m_ssd_norm_w, m_s5_lambda_re_fwd, m_s5_lambda_im_fwd, m_s5_log_step_fwd, m_s5_lambda_re_bwd, m_s5_lambda_im_bwd, m_s5_log_step_bwd, m_s5_b_re, m_s5_b_im, m_s5_c_re_fwd, m_s5_c_im_fwd, m_s5_c_re_bwd, m_s5_c_im_bwd, m_s5_d, m_s5_glu_w, m_s5_glu_b, m_s5_norm_w, m_w_out, m_norm_ffn_w, m_ffn_w_up, m_ffn_conv_w, m_ffn_conv_b, m_ffn_w_down, m_norm_final_w, v_norm_mix_w, v_w_in, v_ssd_conv_w, v_ssd_conv_b, v_ssd_dt_bias_fwd, v_ssd_dt_bias_bwd, v_ssd_a_log_fwd, v_ssd_a_log_bwd, v_ssd_d, v_ssd_norm_w, v_s5_lambda_re_fwd, v_s5_lambda_im_fwd, v_s5_log_step_fwd, v_s5_lambda_re_bwd, v_s5_lambda_im_bwd, v_s5_log_step_bwd, v_s5_b_re, v_s5_b_im, v_s5_c_re_fwd, v_s5_c_im_fwd, v_s5_c_re_bwd, v_s5_c_im_bwd, v_s5_d, v_s5_glu_w, v_s5_glu_b, v_s5_norm_w, v_w_out, v_norm_ffn_w, v_ffn_w_up, v_ffn_conv_w, v_ffn_conv_b, v_ffn_w_down, v_norm_final_w):
    args = dict(locals())
    strip = lambda n, v: v if n == 'norm_final_w' else v[0]
    w = {n: strip(n, args[n]) for n in WEIGHTS}

    mats = ['w_in', 'w_out', 'ffn_w_up', 'ffn_w_down']
    convs = ['ssd_conv_w', 'ffn_conv_w']
    shard = lambda n: w[n].astype(BF16) if n in mats else w[n]
    early, late = ['w_in', 'ssd_conv_w'], ['w_out', 'ffn_w_up', 'ffn_w_down', 'ffn_conv_w']
    full = dict(w)
    full.update(zip(early, all_gather([shard(n) for n in early], name="weight_all_gather")))
    ssem, rsem, src_thru, land_thru, token = scatter_start([shard(n) for n in late], name="weight_gather_start",
                                                           indexed=False)
    me = 4 * lax.axis_index("x") + 2 * lax.axis_index("y") + lax.axis_index("c")

    def late_weights(after):
        own, landed = scatter_wait(ssem, rsem, src_thru, land_thru, after, name="weight_gather_wait", indexed=False)
        return {n: lax.dynamic_update_index_in_dim(l, o, me, 0) for n, o, l in zip(late, own, landed)}

    full['late'], full['token'] = late_weights, token[:1, :1]

    loss, grad_x, g = local_step(x, loss_target, full)

    small = convs + [n for n in WEIGHTS if n not in SHARDED]
    pieces = [g[n].reshape(N_DEV, -1) if n in SHARDED else jnp.broadcast_to(g[n].reshape(1, -1), (N_DEV, g[n].size))
              for n in small]
    pieces.append(jnp.broadcast_to(loss.reshape(1, 1), (N_DEV, 1)))
    total = sum(p.shape[1] for p in pieces)
    nrow = -(-total // (PACK_ROWS * LANES)) * PACK_ROWS
    pieces.append(jnp.zeros((N_DEV, nrow * LANES - total), F32))
    packed_send = jnp.concatenate(pieces, axis=1).reshape(N_DEV, nrow, LANES)
    recvs = exchange([g[n].astype(BF16) for n in mats] + [packed_send], name="grad_exchange")

    outs = [{}, {}, {}, {}]
    for n, recv in zip(mats, recvs):
        res = adamw(recv, w[n], strip(n, args['m_' + n]), strip(n, args['v_' + n]), name="adamw_" + n)
        for o, p in zip(outs, res):
            o[n] = p.reshape(args[n].shape)

    def pack(prefix):
        vals = [strip(n, args[prefix + n]).reshape(-1) for n in small]
        return jnp.pad(jnp.concatenate(vals), (0, nrow * LANES - total + 1)).reshape(nrow, LANES)

    packed = adamw(recvs[-1], pack(''), pack('m_'), pack('v_'), name="adamw_small")
    packed = [p.reshape(-1) for p in packed]
    off = 0
    for n in small:
        size = w[n].size
        for o, p in zip(outs, packed):
            o[n] = p[off:off + size].reshape(args[n].shape)
        off += size
    loss_out = packed[0][off].reshape(())
    return (loss_out, grad_x, *[o[n] for o in outs for n in WEIGHTS])
```

```python
import functools

import jax
import jax.numpy as jnp
from jax import lax
from jax.experimental import pallas as pl
from jax.experimental.pallas import tpu as pltpu

F32, BF16 = jnp.float32, jnp.bfloat16
N_DEV = 8
D_MODEL = 1024
SSD_W, HEADS, HDIM, SGROUPS, HPG, NSTATE, SCONV, QC = 1024, 16, 64, 4, 4, 128, 5, 128
XBC_W = SSD_W + 2 * SGROUPS * NSTATE
S5_W, S5_G, S5_C, S5_P, S5_Q = 512, 32, 16, 64, 16
S5_QC = S5_Q * S5_C
CARRY_ROWS = 32
DFF, FCONV = 2816, 3
FFN_BLK, FFN_PAD = 704, 768
EPS = 1e-6
ADAM_LR, ADAM_B1, ADAM_B2, ADAM_EPS, ADAM_WD, ADAM_STEP = 0.001, 0.9, 0.999, 1e-08, 0.01, 10
LANES = 128
MESH = pl.DeviceIdType.MESH

WEIGHTS = ['norm_mix_w', 'w_in', 'ssd_conv_w', 'ssd_conv_b', 'ssd_dt_bias_fwd', 'ssd_dt_bias_bwd', 'ssd_a_log_fwd',
           'ssd_a_log_bwd', 'ssd_d', 'ssd_norm_w', 's5_lambda_re_fwd', 's5_lambda_im_fwd', 's5_log_step_fwd',
           's5_lambda_re_bwd', 's5_lambda_im_bwd', 's5_log_step_bwd', 's5_b_re', 's5_b_im', 's5_c_re_fwd', 's5_c_im_fwd',
           's5_c_re_bwd', 's5_c_im_bwd', 's5_d', 's5_glu_w', 's5_glu_b', 's5_norm_w', 'w_out', 'norm_ffn_w', 'ffn_w_up',
           'ffn_conv_w', 'ffn_conv_b', 'ffn_w_down', 'norm_final_w']
SHARDED = {'w_in': 1, 'ssd_conv_w': 1, 'w_out': 0, 'ffn_w_up': 1, 'ffn_conv_w': 1, 'ffn_w_down': 0}
FULL_SHAPE = {'w_in': (1024, 3616), 'ssd_conv_w': (5, 2048), 'w_out': (1536, 1024), 'ffn_w_up': (1024, 5632),
              'ffn_conv_w': (3, 5632), 'ffn_w_down': (2816, 1024)}
PACK_ROWS = 512


def _pick(n, cap=1536):
    if n <= cap:
        return n
    return max(t for t in range(LANES, cap + 1, LANES) if n % t == 0)


def _params(sem):
    return pltpu.CompilerParams(dimension_semantics=sem)


def _bd(a, b, ca, cb):
    return lax.dot_general(a.astype(BF16), b.astype(BF16), (((ca,), (cb,)), ((), ())), preferred_element_type=F32)


@jax.custom_vjp
def dot_nn(a, b):
    return _bd(a, b, 1, 0)


dot_nn.defvjp(lambda a, b: (_bd(a, b, 1, 0), (a, b)),
              lambda r, g: (_bd(g, r[1], 1, 1).astype(r[0].dtype), _bd(r[0], g, 0, 0).astype(r[1].dtype)))


@jax.custom_vjp
def dot_nt(a, b):
    return _bd(a, b, 1, 1)


dot_nt.defvjp(lambda a, b: (_bd(a, b, 1, 1), (a, b)),
              lambda r, g: (_bd(g, r[1], 1, 0).astype(r[0].dtype), _bd(g, r[0], 0, 0).astype(r[1].dtype)))


@jax.custom_vjp
def dot_tn(a, b):
    return _bd(a, b, 0, 0)


dot_tn.defvjp(lambda a, b: (_bd(a, b, 0, 0), (a, b)),
              lambda r, g: (_bd(r[1], g, 1, 1).astype(r[0].dtype), _bd(r[0], g, 1, 0).astype(r[1].dtype)))


def _split3(x):
    hi = x.astype(BF16)
    r = x - hi.astype(F32)
    mid = r.astype(BF16)
    lo = (r - mid.astype(F32)).astype(BF16)
    return hi, mid, lo


def _cum_matrix(q, upper):
    ri = lax.broadcasted_iota(jnp.int32, (q, q), 0)
    ci = lax.broadcasted_iota(jnp.int32, (q, q), 1)
    return jnp.where((ci >= ri) if upper else (ci <= ri), 1.0, 0.0).astype(BF16)


def _exact_right(x, mat):
    return sum(jnp.dot(p, mat, preferred_element_type=F32) for p in _split3(x))


@functools.partial(jax.custom_vjp, nondiff_argnums=(1,))
def cum_row(x, rev):
    return _exact_right(x, _cum_matrix(x.shape[1], not rev))


cum_row.defvjp(lambda x, rev: (cum_row(x, rev), None),
               lambda rev, _, g: (_exact_right(g, _cum_matrix(g.shape[1], rev)),))


def _softplus(x):
    return jnp.maximum(x, 0.0) + jnp.log(1.0 + jnp.exp(-jnp.abs(x)))


def _silu(x):
    return x * jax.nn.sigmoid(x)


def _gelu(x):
    return 0.5 * x * (1.0 + jnp.tanh(0.7978845608028654 * (x + 0.044715 * (x * x * x))))


def _rms(x, w):
    xf = x.astype(F32)
    return xf * lax.rsqrt(jnp.mean(xf * xf, axis=-1, keepdims=True) + EPS) * w


def matmul_sum(a_list, b_list, *, name, out_dtype=F32, add=None, tm=512, nt=False):
    a_arrs = [a[0] if isinstance(a, tuple) else a for a in a_list]
    b_arrs = [b[0] if isinstance(b, tuple) else b for b in b_list]
    m, n = a_arrs[0].shape[0], b_arrs[0].shape[-2 if nt else -1]
    tm, tn, k = min(tm, m), _pick(n), len(a_list)

    def body(*refs):
        acc = None
        for a_ref, b_ref in zip(refs[:k], refs[k:2 * k]):
            p = _bd(a_ref[...], b_ref[...], 1, 1 if nt else 0)
            acc = p if acc is None else acc + p
        if add is not None:
            acc = acc + refs[2 * k][...]
        refs[-1][...] = acc.astype(out_dtype)

    def a_spec(a):
        if isinstance(a, tuple):
            return pl.BlockSpec((tm, a[1]), lambda i, j, blk=a[2]: (i, blk))
        return pl.BlockSpec((tm, a.shape[1]), lambda i, j: (i, 0))

    def b_spec(b):
        arr, p = b if isinstance(b, tuple) else (b, None)
        kk = arr.shape[-1 if nt else -2]
        shape, idx = ((tn, kk), lambda j: (j, 0)) if nt else ((kk, tn), lambda j: (0, j))
        if p is None:
            return pl.BlockSpec(shape, lambda i, j: idx(j))
        return pl.BlockSpec((None,) + shape, lambda i, j, p=p: (p,) + idx(j))

    in_specs = [a_spec(a) for a in a_list] + [b_spec(b) for b in b_list]
    args = a_arrs + b_arrs
    if add is not None:
        in_specs.append(pl.BlockSpec((tm, tn), lambda i, j: (i, j)))
        args.append(add)
    return pl.pallas_call(
        body, name=name, grid=(m // tm, n // tn), in_specs=in_specs,
        out_specs=pl.BlockSpec((tm, tn), lambda i, j: (i, j)),
        out_shape=jax.ShapeDtypeStruct((m, n), out_dtype),
        compiler_params=_params(("parallel", "parallel")))(*args)


def matmul_cols(a, b3, *, name, out_dtype=F32, tm=1024):
    m, kk = a.shape
    p, _, nb = b3.shape
    tm, tn = min(tm, m), _pick(nb, 768)
    per = nb // tn

    def body(a_ref, b_ref, o_ref):
        o_ref[...] = _bd(a_ref[...], b_ref[...], 1, 0).astype(out_dtype)

    return pl.pallas_call(
        body, name=name, grid=(m // tm, p * per),
        in_specs=[pl.BlockSpec((tm, kk), lambda i, j: (i, 0)),
                  pl.BlockSpec((None, kk, tn), lambda i, j: (j // per, 0, j % per))],
        out_specs=pl.BlockSpec((tm, tn), lambda i, j: (i, j)),
        out_shape=jax.ShapeDtypeStruct((m, p * nb), out_dtype),
        compiler_params=_params(("parallel", "parallel")))(a, b3)


def matmul_tn(a, b, *, name, tm=1024, out_blocks=None):
    m, k = a.shape
    n = b.shape[1]
    nb = n // (out_blocks or 1)
    tm, tk, tn = min(tm, m), _pick(k), _pick(nb, 768 if out_blocks else 1536)
    per = nb // tn

    def body(a_ref, b_ref, o_ref):
        @pl.when(pl.program_id(2) == 0)
        def _():
            o_ref[...] = jnp.zeros_like(o_ref)

        o_ref[...] += _bd(a_ref[...], b_ref[...], 0, 0)

    if out_blocks:
        out_spec = pl.BlockSpec((None, tk, tn), lambda i, j, t: (j // per, i, j % per))
        out_shape = jax.ShapeDtypeStruct((out_blocks, k, nb), F32)
    else:
        out_spec = pl.BlockSpec((tk, tn), lambda i, j, t: (i, j))
        out_shape = jax.ShapeDtypeStruct((k, n), F32)
    return pl.pallas_call(
        body, name=name, grid=(k // tk, n // tn, m // tm),
        in_specs=[pl.BlockSpec((tm, tk), lambda i, j, t: (t, i)), pl.BlockSpec((tm, tn), lambda i, j, t: (t, j))],
        out_specs=out_spec, out_shape=out_shape,
        compiler_params=_params(("parallel", "parallel", "arbitrary")))(a, b)


def _row_spec(r, tm):
    if isinstance(r, tuple):
        arr, width, blk = r
        return arr, pl.BlockSpec((tm, width), lambda i, blk=blk: (i, blk))
    return r, pl.BlockSpec((tm, r.shape[1]), lambda i: (i, 0))


def _full_spec(p):
    return pl.BlockSpec(p.shape, lambda i: (0,) * p.ndim)


def rowmap_fwd(fn, rows, params, outs, *, name, tm=256):
    pairs = [_row_spec(r, tm) for r in rows]
    m = pairs[0][0].shape[0]
    tm = min(tm, m)
    pairs = [_row_spec(r, tm) for r in rows]
    nr, npar = len(rows), len(params)

    def body(*refs):
        res = fn(*[r[...] for r in refs[:nr + npar]])
        for o_ref, v in zip(refs[nr + npar:], res):
            o_ref[...] = v.astype(o_ref.dtype)

    return pl.pallas_call(
        body, name=name, grid=(m // tm,),
        in_specs=[s for _, s in pairs] + [_full_spec(p) for p in params],
        out_specs=[pl.BlockSpec((tm, c), lambda i: (i, 0)) for c, _ in outs],
        out_shape=[jax.ShapeDtypeStruct((m, c), dt) for c, dt in outs],
        compiler_params=_params(("parallel",)))(*[a for a, _ in pairs], *params)


def rowmap_bwd(fn, rows, params, cts, *, name, row_dtypes=None, add=None, tm=256):
    m = _row_spec(rows[0], tm)[0].shape[0]
    tm = min(tm, m)
    rp = [_row_spec(r, tm) for r in rows]
    cp = [_row_spec(c, tm) for c in cts]
    nr, npar, nc = len(rows), len(params), len(cts)
    row_dtypes = row_dtypes or [F32] * nr
    widths = [s.block_shape[1] for _, s in rp]

    def body(*refs):
        ins = [r[...] for r in refs[:nr + npar]]
        ins = [v.astype(F32) for v in ins]
        ct = tuple(r[...].astype(F32) for r in refs[nr + npar:nr + npar + nc])
        base = nr + npar + nc
        extra = None
        if add is not None:
            extra = refs[base][...]
            base += 1
        _, pull = jax.vjp(fn, *ins)
        grads = pull(ct)
        for j in range(nr):
            g = grads[j]
            if j == 0 and extra is not None:
                g = g + extra
            refs[base + j][...] = g.astype(refs[base + j].dtype)

        @pl.when(pl.program_id(0) == 0)
        def _():
            for j in range(npar):
                refs[base + nr + j][...] = jnp.zeros_like(refs[base + nr + j])

        for j in range(npar):
            refs[base + nr + j][...] += grads[nr + j]

    in_specs = [s for _, s in rp] + [_full_spec(p) for p in params] + [s for _, s in cp]
    args = [a for a, _ in rp] + list(params) + [a for a, _ in cp]
    if add is not None:
        in_specs.append(pl.BlockSpec((tm, widths[0]), lambda i: (i, 0)))
        args.append(add)
    out_specs = [pl.BlockSpec((tm, w), lambda i: (i, 0)) for w in widths] + [_full_spec(p) for p in params]
    out_shape = [jax.ShapeDtypeStruct((m, w), dt) for w, dt in zip(widths, row_dtypes)]
    out_shape += [jax.ShapeDtypeStruct(p.shape, F32) for p in params]
    return pl.pallas_call(
        body, name=name, grid=(m // tm,), in_specs=in_specs, out_specs=out_specs, out_shape=out_shape,
        compiler_params=_params(("arbitrary",)))(*args)


def loss_head(h, target, w, *, name, tm=256):
    m, d = h.shape
    tm = min(tm, m)

    def body(h_ref, t_ref, w_ref, loss_ref, dh_ref, dw_ref):
        y, pull = jax.vjp(_rms, h_ref[...], w_ref[...])
        err = y - t_ref[...]
        dh, dw = pull(err * (1.0 / d))

        @pl.when(pl.program_id(0) == 0)
        def _():
            loss_ref[...] = jnp.zeros_like(loss_ref)
            dw_ref[...] = jnp.zeros_like(dw_ref)

        loss_ref[...] += (0.5 / d) * jnp.sum(err * err, keepdims=True)
        dw_ref[...] += dw
        dh_ref[...] = dh

    row = pl.BlockSpec((tm, d), lambda i: (i, 0))
    return pl.pallas_call(
        body, name=name, grid=(m // tm,), in_specs=[row, row, _full_spec(w)],
        out_specs=[pl.BlockSpec((1, 1), lambda i: (0, 0)), row, _full_spec(w)],
        out_shape=[jax.ShapeDtypeStruct((1, 1), F32), jax.ShapeDtypeStruct((m, d), F32),
                   jax.ShapeDtypeStruct(w.shape, F32)],
        compiler_params=_params(("arbitrary",)))(h, target, w)


def _shift(x, s):
    if s == 0:
        return x
    n = x.shape[0]
    t = lax.broadcasted_iota(jnp.int32, x.shape, 0)
    rolled = pltpu.roll(x, (-s) % n, 0)
    return jnp.where((t + s >= 0) & (t + s < n), rolled, 0.0)


def _conv(x, w, b):
    k = w.shape[0]
    acc = b + w[k // 2:k // 2 + 1, :] * x
    for j in range(k):
        if j != k // 2:
            acc = acc + w[j:j + 1, :] * _shift(x, j - k // 2)
    return acc


def _conv_bwd(x, dc, w):
    k = w.shape[0]
    dx = None
    dws = []
    for j in range(k):
        s = j - k // 2
        term = w[j:j + 1, :] * _shift(dc, -s)
        dx = term if dx is None else dx + term
        dws.append(jnp.sum(dc * _shift(x, s), axis=0, keepdims=True))
    return dx, jnp.concatenate(dws, axis=0), jnp.sum(dc, axis=0, keepdims=True)


def _dsilu(c):
    s = jax.nn.sigmoid(c)
    return s * (1.0 + c * (1.0 - s))


def ssd_conv_fwd(xbc, w, b, *, bsz, name):
    t, c = xbc.shape
    seq, ct = t // bsz, 256

    def body(x_ref, w_ref, b_ref, o_ref):
        o_ref[...] = _silu(_conv(x_ref[...], w_ref[...], b_ref[...]))

    return pl.pallas_call(
        body, name=name, grid=(c // ct, bsz),
        in_specs=[pl.BlockSpec((seq, ct), lambda j, i: (i, j)), pl.BlockSpec((w.shape[0], ct), lambda j, i: (0, j)),
                  pl.BlockSpec((1, ct), lambda j, i: (0, j))],
        out_specs=pl.BlockSpec((seq, ct), lambda j, i: (i, j)),
        out_shape=jax.ShapeDtypeStruct((t, c), F32),
        compiler_params=_params(("parallel", "parallel")))(xbc, w, b)


def ssd_conv_bwd(xbc, dact, w, b, *, bsz, name):
    t, c = xbc.shape
    seq, ct, k = t // bsz, 256, w.shape[0]

    def body(x_ref, g_ref, w_ref, b_ref, dx_ref, dw_ref, db_ref):
        x, wv = x_ref[...], w_ref[...]
        dc = g_ref[...] * _dsilu(_conv(x, wv, b_ref[...]))
        dx, dw, db = _conv_bwd(x, dc, wv)
        dx_ref[...] = dx

        @pl.when(pl.program_id(1) == 0)
        def _():
            dw_ref[...] = jnp.zeros_like(dw_ref)
            db_ref[...] = jnp.zeros_like(db_ref)

        dw_ref[...] += dw
        db_ref[...] += db

    blk = pl.BlockSpec((seq, ct), lambda j, i: (i, j))
    wspec, bspec = pl.BlockSpec((k, ct), lambda j, i: (0, j)), pl.BlockSpec((1, ct), lambda j, i: (0, j))
    return pl.pallas_call(
        body, name=name, grid=(c // ct, bsz), in_specs=[blk, blk, wspec, bspec], out_specs=[blk, wspec, bspec],
        out_shape=[jax.ShapeDtypeStruct((t, c), F32), jax.ShapeDtypeStruct((k, c), F32),
                   jax.ShapeDtypeStruct((1, c), F32)],
        compiler_params=_params(("parallel", "arbitrary")))(xbc, dact, w, b)


def _ffn_specs(seq, ct, k, nblk):
    val = pl.BlockSpec((seq, ct), lambda j, i: (i, j))
    gate = pl.BlockSpec((seq, ct), lambda j, i: (i, nblk + j))
    wv, wg = pl.BlockSpec((k, ct), lambda j, i: (0, j)), pl.BlockSpec((k, ct), lambda j, i: (0, nblk + j))
    bv, bg = pl.BlockSpec((1, ct), lambda j, i: (0, j)), pl.BlockSpec((1, ct), lambda j, i: (0, nblk + j))
    return val, gate, wv, wg, bv, bg


def ffn_act_fwd(up, w, b, *, bsz, name):
    t = up.shape[0]
    half = up.shape[1] // 2
    seq, ct, k = t // bsz, 256, w.shape[0]
    val, gate, wv, wg, bv, bg = _ffn_specs(seq, ct, k, half // ct)

    def body(v_ref, g_ref, wv_ref, wg_ref, bv_ref, bg_ref, o_ref):
        vc = _conv(v_ref[...], wv_ref[...], bv_ref[...])
        gc = _conv(g_ref[...], wg_ref[...], bg_ref[...])
        o_ref[...] = (_silu(gc) * vc).astype(BF16)

    return pl.pallas_call(
        body, name=name, grid=(half // ct, bsz), in_specs=[val, gate, wv, wg, bv, bg], out_specs=val,
        out_shape=jax.ShapeDtypeStruct((t, half), BF16),
        compiler_params=_params(("parallel", "parallel")))(up, up, w, w, b, b)


def ffn_act_bwd(up, dact, w, b, *, bsz, name):
    t = up.shape[0]
    half = up.shape[1] // 2
    seq, ct, k = t // bsz, 256, w.shape[0]
    val, gate, wv, wg, bv, bg = _ffn_specs(seq, ct, k, half // ct)

    def body(v_ref, g_ref, wv_ref, wg_ref, bv_ref, bg_ref, d_ref, dv_ref, dg_ref, dwv_ref, dwg_ref, dbv_ref, dbg_ref):
        v, g = v_ref[...], g_ref[...]
        vc = _conv(v, wv_ref[...], bv_ref[...])
        gc = _conv(g, wg_ref[...], bg_ref[...])
        d = d_ref[...].astype(F32)
        dv, dwv, dbv = _conv_bwd(v, d * _silu(gc), wv_ref[...])
        dg, dwg, dbg = _conv_bwd(g, d * vc * _dsilu(gc), wg_ref[...])
        dv_ref[...] = dv.astype(BF16)
        dg_ref[...] = dg.astype(BF16)

        @pl.when(pl.program_id(1) == 0)
        def _():
            for r in (dwv_ref, dwg_ref, dbv_ref, dbg_ref):
                r[...] = jnp.zeros_like(r)

        dwv_ref[...] += dwv
        dwg_ref[...] += dwg
        dbv_ref[...] += dbv
        dbg_ref[...] += dbg

    return pl.pallas_call(
        body, name=name, grid=(half // ct, bsz), in_specs=[val, gate, wv, wg, bv, bg, val],
        out_specs=[val, val, wv, wv, bv, bv],
        out_shape=[jax.ShapeDtypeStruct((t, half), BF16), jax.ShapeDtypeStruct((t, half), BF16),
                   jax.ShapeDtypeStruct((k, half), F32), jax.ShapeDtypeStruct((k, half), F32),
                   jax.ShapeDtypeStruct((1, half), F32), jax.ShapeDtypeStruct((1, half), F32)],
        compiler_params=_params(("parallel", "arbitrary")))(up, up, w, w, b, b, dact)


def _sel_row(a, h):
    oh = (lax.broadcasted_iota(jnp.int32, (a.shape[0], 1), 0) == h).astype(F32)
    return jnp.sum(a * oh, axis=0, keepdims=True)


def _ssd_chunk(xp, dtr, bm, cm, prev, bias_r, alog_r, dskip_r, rev):
    q = dtr.shape[1]
    ri = lax.broadcasted_iota(jnp.int32, (q, q), 0)
    ci = lax.broadcasted_iota(jnp.int32, (q, q), 1)
    mask = (ci >= ri) if rev else (ci <= ri)
    lane_lo, row_lo = ci < HDIM, ri < HDIM
    dt_r = _softplus(dtr + bias_r)
    dta_r = dt_r * (-jnp.exp(alog_r))
    cs_r = cum_row(dta_r, rev)
    scores = dot_nt(cm, bm)

    def per_row(v):
        return jnp.broadcast_to(v, (q, q)).T

    ys, news = [], []
    for p in range(HPG // 2):
        ha = 2 * p + (HPG if rev else 0)
        hb = ha + 1
        cs_a, cs_b = _sel_row(cs_r, ha), _sel_row(cs_r, hb)
        csq_a, csq_b = per_row(cs_a), per_row(cs_b)
        seg_a = jnp.exp(jnp.where(mask, csq_a - cs_a, -1e30))
        seg_b = jnp.exp(jnp.where(mask, csq_b - cs_b, -1e30))
        csq = jnp.where(lane_lo, csq_a, csq_b)
        xdt = xp[p] * jnp.where(lane_lo, per_row(_sel_row(dt_r, ha)), per_row(_sel_row(dt_r, hb)))
        tot_a = jnp.sum(_sel_row(dta_r, ha), axis=1, keepdims=True)
        tot_b = jnp.sum(_sel_row(dta_r, hb), axis=1, keepdims=True)
        y = jnp.where(lane_lo, dot_nn(scores * seg_a, xdt), dot_nn(scores * seg_b, xdt))
        y = y + dot_nt(cm, prev[p]) * jnp.exp(csq)
        if not rev:
            y = y + jnp.where(lane_lo, _sel_row(dskip_r, ha), _sel_row(dskip_r, hb)) * xp[p]
        ys.append(y)
        st = dot_tn(xdt * jnp.exp(jnp.where(lane_lo, tot_a, tot_b) - csq), bm)
        news.append(jnp.exp(jnp.where(row_lo, tot_a, tot_b)) * prev[p] + st)
    return tuple(ys), tuple(news)


NPAIR = HPG // 2


def _ssd_specs(seq, nc):
    xs = pl.BlockSpec((None, seq, HPG * HDIM), lambda b, g: (b, 0, g))
    bm = pl.BlockSpec((None, seq, NSTATE), lambda b, g: (b, 0, SSD_W // NSTATE + g))
    cm = pl.BlockSpec((None, seq, NSTATE), lambda b, g: (b, 0, SSD_W // NSTATE + SGROUPS + g))
    dtr = pl.BlockSpec((None, None, 2 * HPG, seq), lambda b, g: (b, g, 0, 0))
    pr = pl.BlockSpec((None, 2 * HPG, 1), lambda b, g: (g, 0, 0))
    st = pl.BlockSpec((None, None, 2, nc, NPAIR, 2 * HDIM, NSTATE), lambda b, g: (b, g, 0, 0, 0, 0, 0))
    return xs, bm, cm, dtr, pr, st


def _pair_cols(p):
    return slice(2 * HDIM * p, 2 * HDIM * (p + 1))


def ssd_scan_fwd(act, dtr, prs, *, name):
    bsz, seq, _ = act.shape
    nc = seq // QC
    xs, bm, cm, dtrs, pr, st = _ssd_specs(seq, nc)

    def body(x_ref, b_ref, c_ref, dtr_ref, br_ref, ar_ref, dk_ref, y_ref, st_ref):
        par = (br_ref[...], ar_ref[...], dk_ref[...])
        y_ref[...] = jnp.zeros_like(y_ref)

        def step(i, carry):
            new = []
            for rev in (False, True):
                k = (nc - 1 - i) if rev else i
                rows = pl.ds(pl.multiple_of(k * QC, QC), QC)
                xp = tuple(x_ref[rows, _pair_cols(p)] for p in range(NPAIR))
                for p in range(NPAIR):
                    st_ref[int(rev), k, p] = carry[rev][p]
                ys, nw = _ssd_chunk(xp, dtr_ref[:, rows], b_ref[rows, :], c_ref[rows, :], carry[rev], *par, rev)
                for p in range(NPAIR):
                    y_ref[rows, _pair_cols(p)] += ys[p]
                new.append(nw)
            return tuple(new)

        zero = tuple(jnp.zeros((2 * HDIM, NSTATE), F32) for _ in range(NPAIR))
        lax.fori_loop(0, nc, step, (zero, zero))

    return pl.pallas_call(
        body, name=name, grid=(bsz, SGROUPS), in_specs=[xs, bm, cm, dtrs, pr, pr, pr], out_specs=[xs, st],
        out_shape=[jax.ShapeDtypeStruct((bsz, seq, SSD_W), F32),
                   jax.ShapeDtypeStruct((bsz, SGROUPS, 2, nc, NPAIR, 2 * HDIM, NSTATE), F32)],
        compiler_params=_params(("parallel", "parallel")))(act, act, act, dtr, *prs)


def ssd_scan_bwd(act, dtr, prs, states, dy, *, name):
    bsz, seq, _ = act.shape
    nc = seq // QC
    xs, bm, cm, dtrs, pr, st = _ssd_specs(seq, nc)
    grp = pl.BlockSpec((None, seq, NSTATE), lambda b, g: (b, 0, g))
    dpr = pl.BlockSpec((None, None, 2 * HPG, 1), lambda b, g: (b, g, 0, 0))

    def body(x_ref, b_ref, c_ref, dtr_ref, br_ref, ar_ref, dk_ref, st_ref, dy_ref,
             dx_ref, db_ref, dc_ref, ddtr_ref, gbr_ref, gar_ref, gdk_ref):
        par = (br_ref[...], ar_ref[...], dk_ref[...])
        pgrads = (gbr_ref, gar_ref, gdk_ref)
        for r in pgrads + (dx_ref, db_ref, dc_ref, ddtr_ref):
            r[...] = jnp.zeros_like(r)

        def bstep(i, dcarry):
            new = []
            for rev in (False, True):
                k = i if rev else (nc - 1 - i)
                rows = pl.ds(pl.multiple_of(k * QC, QC), QC)
                xp = tuple(x_ref[rows, _pair_cols(p)] for p in range(NPAIR))
                prev = tuple(st_ref[int(rev), k, p] for p in range(NPAIR))
                _, pull = jax.vjp(functools.partial(_ssd_chunk, rev=rev), xp, dtr_ref[:, rows], b_ref[rows, :],
                                  c_ref[rows, :], prev, *par)
                dyp = tuple(dy_ref[rows, _pair_cols(p)] for p in range(NPAIR))
                gx, gdt, gb, gc, gprev, *gpar = pull((dyp, dcarry[rev]))
                for p in range(NPAIR):
                    dx_ref[rows, _pair_cols(p)] += gx[p]
                ddtr_ref[:, rows] += gdt
                db_ref[rows, :] += gb
                dc_ref[rows, :] += gc
                for r, g in zip(pgrads, gpar):
                    r[...] += g
                new.append(gprev)
            return tuple(new)

        zero = tuple(jnp.zeros((2 * HDIM, NSTATE), F32) for _ in range(NPAIR))
        lax.fori_loop(0, nc, bstep, (zero, zero))

    out_shape = [jax.ShapeDtypeStruct((bsz, seq, SSD_W), F32),
                 jax.ShapeDtypeStruct((bsz, seq, SGROUPS * NSTATE), F32),
                 jax.ShapeDtypeStruct((bsz, seq, SGROUPS * NSTATE), F32),
                 jax.ShapeDtypeStruct(dtr.shape, F32)]
    out_shape += [jax.ShapeDtypeStruct((bsz, SGROUPS, 2 * HPG, 1), F32)] * 3
    return pl.pallas_call(
        body, name=name, grid=(bsz, SGROUPS), in_specs=[xs, bm, cm, dtrs, pr, pr, pr, st, xs],
        out_specs=[xs, grp, grp, dtrs, dpr, dpr, dpr], out_shape=out_shape,
        compiler_params=_params(("parallel", "parallel")))(act, act, act, dtr, *prs, states, dy)


def _s5_direction(lam_re, lam_im, log_step, b_re, b_im, c_re, c_im, rev):
    q = S5_Q
    step = jnp.exp(log_step)[:, None]
    lr, li = lam_re * step, lam_im * step
    mag = jnp.exp(lr)
    ar, ai = mag * jnp.cos(li), mag * jnp.sin(li)
    den = lam_re * lam_re + lam_im * lam_im
    cr = ((ar - 1.0) * lam_re + ai * lam_im) / den
    ci = (ai * lam_re - (ar - 1.0) * lam_im) / den
    bbr = cr[..., None] * b_re - ci[..., None] * b_im
    bbi = cr[..., None] * b_im + ci[..., None] * b_re
    d = jnp.arange(q + 1, dtype=F32)[None, :, None]
    pm = jnp.exp(d * lr[:, None, :])
    pr, pi = pm * jnp.cos(d * li[:, None, :]), pm * jnp.sin(d * li[:, None, :])
    er = pr[..., None] * bbr[:, None] - pi[..., None] * bbi[:, None]
    ei = pr[..., None] * bbi[:, None] + pi[..., None] * bbr[:, None]
    hp = lax.Precision.HIGHEST
    k = (jnp.einsum('gcp,gdpz->gdcz', c_re, er[:, :q], precision=hp)
         - jnp.einsum('gcp,gdpz->gdcz', c_im, ei[:, :q], precision=hp))
    e = jnp.concatenate([er[:, :q], ei[:, :q]], axis=2)
    wt = jnp.transpose(e if rev else e[:, ::-1], (0, 1, 3, 2))
    p1r, p1i = pr[:, 1:], pi[:, 1:]
    if rev:
        p1r, p1i = p1r[:, ::-1], p1i[:, ::-1]
    m_re = c_re[:, None] * p1r[:, :, None, :] - c_im[:, None] * p1i[:, :, None, :]
    m_im = -c_re[:, None] * p1i[:, :, None, :] - c_im[:, None] * p1r[:, :, None, :]
    mt = jnp.transpose(jnp.concatenate([m_re, m_im], axis=-1), (0, 3, 1, 2))
    da = jnp.concatenate([pr[:, q], pr[:, q]], axis=-1)
    db = jnp.concatenate([-pi[:, q], pi[:, q]], axis=-1)
    return k, wt, mt, da, db


def _s5_operators(lf_re, lf_im, lsf, lb_re, lb_im, lsb, b_re, b_im, cf_re, cf_im, cb_re, cb_im):
    kf, wtf, mtf, daf, dbf = _s5_direction(lf_re, lf_im, lsf, b_re, b_im, cf_re, cf_im, False)
    kb, wtb, mtb, dab, dbb = _s5_direction(lb_re, lb_im, lsb, b_re, b_im, cb_re, cb_im, True)
    g = kf.shape[0]
    lags = jnp.concatenate([kb[:, :0:-1], kf[:, :1] + kb[:, :1], kf[:, 1:]], axis=1)
    tt = jnp.transpose(lags, (0, 1, 3, 2))
    wt = jnp.concatenate([wtf.reshape(g, S5_QC, 2 * S5_P), wtb.reshape(g, S5_QC, 2 * S5_P)], axis=-1)
    mt = jnp.concatenate([mtf.reshape(g, 2 * S5_P, S5_QC), mtb.reshape(g, 2 * S5_P, S5_QC)], axis=1)
    return tt, wt, mt, jnp.concatenate([daf, dab], -1), jnp.concatenate([dbf, dbb], -1)


def _gspec(*shape):
    return pl.BlockSpec((None,) + shape, lambda g: (g,) + (0,) * len(shape))


S5_HALVES = S5_QC // LANES


def _toeplitz_block(s, t):
    per = LANES // S5_C
    return t // per, slice(s * S5_C, (s + 1) * S5_C), slice((t % per) * S5_C, (t % per + 1) * S5_C)


def s5_toeplitz(kt, *, name):
    g = kt.shape[0]

    def body(k_ref, t_ref):
        for s in range(S5_Q):
            for t in range(S5_Q):
                t_ref[_toeplitz_block(s, t)] = k_ref[t - s + S5_Q - 1]

    return pl.pallas_call(
        body, name=name, grid=(g,), in_specs=[_gspec(2 * S5_Q - 1, S5_C, S5_C)],
        out_specs=_gspec(S5_HALVES, S5_QC, LANES), out_shape=jax.ShapeDtypeStruct((g, S5_HALVES, S5_QC, LANES), F32),
        compiler_params=_params(("parallel",)))(kt)


def s5_toeplitz_bwd(dtt, *, name):
    g = dtt.shape[0]

    def body(d_ref, k_ref):
        for j in range(2 * S5_Q - 1):
            acc = None
            for s in range(S5_Q):
                t = j - (S5_Q - 1) + s
                if 0 <= t < S5_Q:
                    blk = d_ref[_toeplitz_block(s, t)]
                    acc = blk if acc is None else acc + blk
            k_ref[j] = acc

    return pl.pallas_call(
        body, name=name, grid=(g,), in_specs=[_gspec(S5_HALVES, S5_QC, LANES)],
        out_specs=_gspec(2 * S5_Q - 1, S5_C, S5_C), out_shape=jax.ShapeDtypeStruct((g, 2 * S5_Q - 1, S5_C, S5_C), F32),
        compiler_params=_params(("parallel",)))(dtt)


def s5_state_in(u, wt, *, name):
    g, r, _ = u.shape

    def body(u_ref, w_ref, o_ref):
        o_ref[...] = _bd(u_ref[...], w_ref[...], 1, 0)

    return pl.pallas_call(
        body, name=name, grid=(g,), in_specs=[_gspec(r, S5_QC), _gspec(S5_QC, 4 * S5_P)],
        out_specs=_gspec(r, 4 * S5_P), out_shape=jax.ShapeDtypeStruct((g, r, 4 * S5_P), F32),
        compiler_params=_params(("parallel",)))(u, wt)


def _swap(h):
    return pltpu.roll(h, S5_P, 1)


def s5_carry_fwd(s, da, db, *, name):
    nck, rows, _ = s.shape
    w = 2 * S5_P

    def body(s_ref, da_ref, db_ref, h_ref):
        dirs = ((False, slice(0, w)), (True, slice(w, 2 * w)))
        coef = [(da_ref[:, cols], db_ref[:, cols]) for _, cols in dirs]

        def step(i, hs):
            new = []
            for (rev, cols), (a, b), h in zip(dirs, coef, hs):
                k = (nck - 1 - i) if rev else i
                h_ref[k, :, cols] = h
                new.append(a * h + b * _swap(h) + s_ref[k, :, cols])
            return tuple(new)

        z = jnp.zeros((rows, w), F32)
        lax.fori_loop(0, nck, step, (z, z), unroll=2)

    rt = min(CARRY_ROWS, rows)
    big, small = pl.BlockSpec((nck, rt, 2 * w), lambda i: (0, i, 0)), pl.BlockSpec((rt, 2 * w), lambda i: (i, 0))
    rows = rt
    return pl.pallas_call(
        body, name=name, grid=(s.shape[1] // rt,), in_specs=[big, small, small], out_specs=big,
        out_shape=jax.ShapeDtypeStruct(s.shape, F32), compiler_params=_params(("parallel",)))(s, da, db)


def s5_carry_bwd(hin, dh, da, db, *, name):
    nck, rows, _ = hin.shape
    w = 2 * S5_P

    def body(h_ref, dh_ref, da_ref, db_ref, ds_ref, gda_ref, gdb_ref):
        dirs = ((False, slice(0, w)), (True, slice(w, 2 * w)))
        coef = [(da_ref[:, cols], db_ref[:, cols]) for _, cols in dirs]

        def step(i, carries):
            new = []
            for (rev, cols), (a, b), (g, ga, gb) in zip(dirs, coef, carries):
                k = i if rev else (nck - 1 - i)
                ds_ref[k, :, cols] = g
                h = h_ref[k, :, cols]
                new.append((dh_ref[k, :, cols] + a * g + _swap(b * g), ga + g * h, gb + g * _swap(h)))
            return tuple(new)

        z = jnp.zeros((rows, w), F32)
        res = lax.fori_loop(0, nck, step, ((z, z, z), (z, z, z)), unroll=2)
        for (_, cols), (_, ga, gb) in zip(dirs, res):
            gda_ref[:, cols] = ga
            gdb_ref[:, cols] = gb

    rt = min(CARRY_ROWS, rows)
    big, small = pl.BlockSpec((nck, rt, 2 * w), lambda i: (0, i, 0)), pl.BlockSpec((rt, 2 * w), lambda i: (i, 0))
    rows = rt
    return pl.pallas_call(
        body, name=name, grid=(hin.shape[1] // rt,), in_specs=[big, big, small, small], out_specs=[big, small, small],
        out_shape=[jax.ShapeDtypeStruct(hin.shape, F32), jax.ShapeDtypeStruct(da.shape, F32),
                   jax.ShapeDtypeStruct(da.shape, F32)],
        compiler_params=_params(("parallel",)))(hin, dh, da, db)


def s5_out(u, hin, tt, mt, *, name):
    g, r, _ = u.shape

    def body(u_ref, h_ref, t_ref, m_ref, o_ref):
        u_v, h_v = u_ref[...], h_ref[...]
        for half in range(S5_HALVES):
            cols = slice(half * LANES, (half + 1) * LANES)
            o_ref[:, cols] = _bd(u_v, t_ref[half], 1, 0) + _bd(h_v, m_ref[:, cols], 1, 0)

    return pl.pallas_call(
        body, name=name, grid=(g,),
        in_specs=[_gspec(r, S5_QC), _gspec(r, 4 * S5_P), _gspec(S5_HALVES, S5_QC, LANES), _gspec(4 * S5_P, S5_QC)],
        out_specs=_gspec(r, S5_QC), out_shape=jax.ShapeDtypeStruct((g, r, S5_QC), F32),
        compiler_params=_params(("parallel",)))(u, hin, tt, mt)


def s5_out_bwd(dy, u, hin, tt, mt, *, name):
    g, r, _ = u.shape

    def body(dy_ref, u_ref, h_ref, t_ref, m_ref, dh_ref, dt_ref, dm_ref, du_ref):
        dy_v, u_v = dy_ref[...], u_ref[...]
        dh_ref[...] = _bd(dy_v, m_ref[...], 1, 1)
        dm_ref[...] = _bd(h_ref[...], dy_v, 0, 0)
        du = None
        for half in range(S5_HALVES):
            dy_h = dy_v[:, half * LANES:(half + 1) * LANES]
            dt_ref[half] = _bd(u_v, dy_h, 0, 0)
            part = _bd(dy_h, t_ref[half], 1, 1)
            du = part if du is None else du + part
        du_ref[...] = du

    tspec = _gspec(S5_HALVES, S5_QC, LANES)
    return pl.pallas_call(
        body, name=name, grid=(g,),
        in_specs=[_gspec(r, S5_QC), _gspec(r, S5_QC), _gspec(r, 4 * S5_P), tspec, _gspec(4 * S5_P, S5_QC)],
        out_specs=[_gspec(r, 4 * S5_P), tspec, _gspec(4 * S5_P, S5_QC), _gspec(r, S5_QC)],
        out_shape=[jax.ShapeDtypeStruct((g, r, 4 * S5_P), F32), jax.ShapeDtypeStruct((g, S5_HALVES, S5_QC, LANES), F32),
                   jax.ShapeDtypeStruct((g, 4 * S5_P, S5_QC), F32), jax.ShapeDtypeStruct((g, r, S5_QC), F32)],
        compiler_params=_params(("parallel",)))(dy, u, hin, tt, mt)


def s5_state_in_bwd(ds, u, wt, du1, *, name):
    g, r, _ = u.shape

    def body(ds_ref, u_ref, w_ref, du1_ref, du_ref, dw_ref):
        ds_v = ds_ref[...]
        du_ref[...] = du1_ref[...] + _bd(ds_v, w_ref[...], 1, 1)
        dw_ref[...] = _bd(u_ref[...], ds_v, 0, 0)

    return pl.pallas_call(
        body, name=name, grid=(g,),
        in_specs=[_gspec(r, 4 * S5_P), _gspec(r, S5_QC), _gspec(S5_QC, 4 * S5_P), _gspec(r, S5_QC)],
        out_specs=[_gspec(r, S5_QC), _gspec(S5_QC, 4 * S5_P)],
        out_shape=[jax.ShapeDtypeStruct((g, r, S5_QC), F32), jax.ShapeDtypeStruct((g, S5_QC, 4 * S5_P), F32)],
        compiler_params=_params(("parallel",)))(ds, u, wt, du1)


def _s5_post(ypre, u, dvec, wv, wg, bv, bg, nw):
    g = _gelu(ypre + dvec * u)
    out = (dot_nn(g, wv) + bv) * jax.nn.sigmoid(dot_nn(g, wg) + bg)
    return (_rms(out, nw),)


def _ssd_post(y, z, nw):
    return (_rms(y * _silu(z), nw),)


def _to_chunks(u, bsz):
    nck = u.shape[0] // bsz // S5_Q
    v = u.reshape(bsz, nck, S5_Q, S5_G, S5_C)
    return jnp.transpose(v, (3, 0, 1, 2, 4)).reshape(S5_G, bsz * nck, S5_QC)


def _from_chunks(y, bsz):
    nck = y.shape[1] // bsz
    v = y.reshape(S5_G, bsz, nck, S5_Q, S5_C)
    return jnp.transpose(v, (1, 2, 3, 0, 4)).reshape(bsz * nck * S5_Q, S5_W)


def _to_carry(s, bsz):
    nck = s.shape[1] // bsz
    return jnp.transpose(s.reshape(S5_G, bsz, nck, -1), (2, 0, 1, 3)).reshape(nck, S5_G * bsz, -1)


def _from_carry(h, bsz):
    nck = h.shape[0]
    return jnp.transpose(h.reshape(nck, S5_G, bsz, -1), (1, 2, 0, 3)).reshape(S5_G, bsz * nck, -1)


def _block_diag(w):
    eye = jnp.eye(S5_G, dtype=w.dtype)
    return jnp.einsum('gcd,gh->gchd', w, eye).reshape(S5_W, S5_W)


def _diag_blocks(w):
    v = w.reshape(S5_G, S5_C, S5_G, S5_C)
    return v[jnp.arange(S5_G), :, jnp.arange(S5_G), :]


def _dt_rows(dt, bsz):
    seq = dt.shape[0] // bsz
    return jnp.transpose(dt.reshape(bsz, seq, 2, SGROUPS, HPG), (0, 3, 2, 4, 1)).reshape(bsz, SGROUPS, 2 * HPG, seq)


def _dt_from_rows(dr):
    bsz, _, _, seq = dr.shape
    return jnp.transpose(dr.reshape(bsz, SGROUPS, 2, HPG, seq), (0, 4, 2, 1, 3)).reshape(bsz * seq, 2 * HEADS)


def _head_params(f, b):
    return jnp.concatenate([f.reshape(SGROUPS, HPG), b.reshape(SGROUPS, HPG)], axis=1)[:, :, None]


def _head_grads(gr):
    v = gr.sum(0)[:, :, 0]
    return v[:, :HPG].reshape(HEADS), v[:, HPG:].reshape(HEADS)


def local_step(x, target, w):
    bsz, seq, d = x.shape
    t = bsz * seq
    x2, tgt2 = x.reshape(t, d), target.reshape(t, d)
    g = {}
    row = lambda v: v.reshape(1, -1)
    bf = lambda v: v.astype(BF16)

    w_in = _unshard(bf(w['w_in']), SHARDED['w_in'])
    cuts = [0, SSD_W, SSD_W + XBC_W, SSD_W + XBC_W + 2 * HEADS, w_in.shape[1]]
    w_in_parts = [w_in[:, a:b] for a, b in zip(cuts[:-1], cuts[1:])]
    norm_mix = row(w['norm_mix_w']) + w.get('token', 0.0)
    (hn,) = rowmap_fwd(lambda a, nw: (_rms(a, nw),), [x2], [norm_mix], [(d, BF16)], name="rms_mix")
    z, xbc, dt, u = [matmul_sum([hn], [p], tm=1024, name=f"in_proj_{i}") for i, p in enumerate(w_in_parts)]

    conv_w, conv_b = _unshard(w['ssd_conv_w'], SHARDED['ssd_conv_w']), row(w['ssd_conv_b'])
    act = ssd_conv_fwd(xbc, conv_w, conv_b, bsz=bsz, name="ssd_conv")
    dtr = _dt_rows(dt, bsz)
    prs = (_head_params(w['ssd_dt_bias_fwd'], w['ssd_dt_bias_bwd']),
           _head_params(w['ssd_a_log_fwd'], w['ssd_a_log_bwd']),
           _head_params(w['ssd_d'], jnp.zeros_like(w['ssd_d'])))
    act3 = act.reshape(bsz, seq, XBC_W)
    y_scan, ssd_states = ssd_scan_fwd(act3, dtr, prs, name="ssd_scan")
    y_scan = y_scan.reshape(t, SSD_W)
    ssd_nw = row(w['ssd_norm_w'])
    (y_ssd,) = rowmap_fwd(_ssd_post, [y_scan, z], [ssd_nw], [(SSD_W, BF16)], name="ssd_post")

    s5_names = ['s5_lambda_re_fwd', 's5_lambda_im_fwd', 's5_log_step_fwd', 's5_lambda_re_bwd', 's5_lambda_im_bwd',
                's5_log_step_bwd', 's5_b_re', 's5_b_im', 's5_c_re_fwd', 's5_c_im_fwd', 's5_c_re_bwd', 's5_c_im_bwd']
    (kt, wt, mt, da, db), s5_pull = jax.vjp(_s5_operators, *[w[n] for n in s5_names])
    tt_b, wt_b, mt_b = s5_toeplitz(kt, name="s5_toeplitz"), bf(wt), bf(mt)
    da_r, db_r = jnp.repeat(da, bsz, axis=0), jnp.repeat(db, bsz, axis=0)
    uc = _to_chunks(u, bsz)
    s_in = _to_carry(s5_state_in(uc, wt_b, name="s5_state_in"), bsz)
    hin_c = s5_carry_fwd(s_in, da_r, db_r, name="s5_carry")
    hin = _from_carry(hin_c, bsz)
    ypre = _from_chunks(s5_out(uc, hin, tt_b, mt_b, name="s5_out"), bsz)
    glu_w = w['s5_glu_w']
    s5_par = [row(w['s5_d']), _block_diag(glu_w[:, :, :S5_C]), _block_diag(glu_w[:, :, S5_C:]),
              row(w['s5_glu_b'][:, :S5_C]), row(w['s5_glu_b'][:, S5_C:]), row(w['s5_norm_w'])]
    (y_s5,) = rowmap_fwd(_s5_post, [ypre, u], s5_par, [(S5_W, BF16)], name="s5_post")

    if 'late' in w:
        w = {**w, **w['late'](y_s5)}
    w_out = bf(w['w_out']).reshape(SSD_W + S5_W, d)
    h1 = matmul_sum([y_ssd, y_s5], [w_out[:SSD_W], w_out[SSD_W:]], add=x2, name="out_proj")
    norm_ffn = row(w['norm_ffn_w'])
    (hn2,) = rowmap_fwd(lambda a, nw: (_rms(a, nw),), [h1], [norm_ffn], [(d, BF16)], name="rms_ffn")
    pad_c = FFN_PAD - FFN_BLK
    half = N_DEV // 2
    w_up3 = jnp.pad(bf(w['ffn_w_up']), ((0, 0), (0, 0), (0, pad_c)))
    w_down = jnp.pad(bf(w['ffn_w_down']).reshape(half, FFN_BLK, d), ((0, 0), (0, pad_c), (0, 0)))
    w_down = w_down.reshape(half * FFN_PAD, d)
    fconv_w = jnp.pad(w['ffn_conv_w'], ((0, 0), (0, 0), (0, pad_c)))
    fconv_w = jnp.transpose(fconv_w, (1, 0, 2)).reshape(FCONV, N_DEV * FFN_PAD)
    fconv_b = row(jnp.pad(w['ffn_conv_b'].reshape(N_DEV, FFN_BLK), ((0, 0), (0, pad_c))))
    up = matmul_cols(hn2, w_up3, name="ffn_up")
    fact = ffn_act_fwd(up, fconv_w, fconv_b, bsz=bsz, name="ffn_act")
    h2 = matmul_sum([fact], [w_down], add=h1, name="ffn_down")
    loss, dh2, g_nf = loss_head(h2, tgt2, row(w['norm_final_w']), name="loss_head")
    g['norm_final_w'] = g_nf.reshape(-1)

    dfact = matmul_sum([dh2], [w_down], nt=True, tm=1024, name="ffn_down_dx")
    g_down = matmul_tn(fact, dh2, name="ffn_down_dw").reshape(half, FFN_PAD, d)[:, :FFN_BLK]
    g['ffn_w_down'] = g_down.reshape(N_DEV, FFN_BLK // 2, d)
    dval, dgate, dwv, dwg, dbv, dbg = ffn_act_bwd(up, dfact, fconv_w, fconv_b, bsz=bsz, name="ffn_act_bwd")
    g_cw = jnp.concatenate([dwv, dwg], axis=1).reshape(FCONV, N_DEV, FFN_PAD)[:, :, :FFN_BLK]
    g['ffn_conv_w'] = jnp.transpose(g_cw, (1, 0, 2))
    g['ffn_conv_b'] = jnp.concatenate([dbv, dbg], axis=1).reshape(N_DEV, FFN_PAD)[:, :FFN_BLK].reshape(-1)
    windows = [(dval, FFN_PAD, p) for p in range(half)] + [(dgate, FFN_PAD, p) for p in range(half)]
    dhn2 = matmul_sum(windows, [(w_up3, p) for p in range(N_DEV)], nt=True, name="ffn_up_dx")
    g['ffn_w_up'] = jnp.concatenate([matmul_tn(hn2, dval, out_blocks=half, name="ffn_up_dw_val"),
                                     matmul_tn(hn2, dgate, out_blocks=half, name="ffn_up_dw_gate")],
                                    axis=0)[:, :, :FFN_BLK]
    send_early = w.get('on_grads')
    if send_early:
        norm_ffn = norm_ffn + send_early(g, ['ffn_w_up', 'ffn_w_down'])
    dh1, g_nffn = rowmap_bwd(lambda a, nw: (_rms(a, nw),), [h1], [norm_ffn], [dhn2], add=dh2, name="rms_ffn_bwd")
    g['norm_ffn_w'] = g_nffn.reshape(-1)

    dycat = matmul_sum([dh1], [w_out], nt=True, tm=1024, name="out_proj_dx")
    g['w_out'] = jnp.concatenate([matmul_tn(y_ssd, dh1, name="out_proj_dw_ssd"),
                                  matmul_tn(y_s5, dh1, name="out_proj_dw_s5")], axis=0).reshape(w['w_out'].shape)
    if send_early:
        ssd_nw = ssd_nw + send_early(g, ['w_out'])
    dy_scan, dz, g_snw = rowmap_bwd(_ssd_post, [y_scan, z], [ssd_nw], [(dycat, SSD_W, 0)], name="ssd_post_bwd")
    g['ssd_norm_w'] = g_snw.reshape(-1)
    dypre, du_a, g_d, g_wv, g_wg, g_bv, g_bg, g_s5nw = rowmap_bwd(
        _s5_post, [ypre, u], s5_par, [(dycat, S5_W, SSD_W // S5_W)], name="s5_post_bwd")
    g['s5_d'], g['s5_norm_w'] = g_d.reshape(-1), g_s5nw.reshape(-1)
    g['s5_glu_w'] = jnp.concatenate([_diag_blocks(g_wv), _diag_blocks(g_wg)], axis=-1)
    g['s5_glu_b'] = jnp.concatenate([g_bv.reshape(S5_G, S5_C), g_bg.reshape(S5_G, S5_C)], axis=-1)

    dyc = _to_chunks(dypre, bsz)
    dhin, dtt, dmt, du1 = s5_out_bwd(dyc, uc, hin, tt_b, mt_b, name="s5_out_bwd")
    ds_c, gda, gdb = s5_carry_bwd(hin_c, _to_carry(dhin, bsz), da_r, db_r, name="s5_carry_bwd")
    duc, dwt = s5_state_in_bwd(_from_carry(ds_c, bsz), uc, wt_b, du1, name="s5_state_in_bwd")
    du = du_a + _from_chunks(duc, bsz)
    fold = lambda v: v.reshape(S5_G, bsz, -1).sum(1)
    dkt = s5_toeplitz_bwd(dtt, name="s5_toeplitz_bwd")
    for n, gv in zip(s5_names, s5_pull((dkt, dwt, dmt, fold(gda), fold(gdb)))):
        g[n] = gv

    dxs, dbm, dcm, ddtr, gbr, gar, gdk = ssd_scan_bwd(
        act3, dtr, prs, ssd_states, dy_scan.reshape(bsz, seq, SSD_W), name="ssd_scan_bwd")
    g['ssd_dt_bias_fwd'], g['ssd_dt_bias_bwd'] = _head_grads(gbr)
    g['ssd_a_log_fwd'], g['ssd_a_log_bwd'] = _head_grads(gar)
    g['ssd_d'] = _head_grads(gdk)[0]
    dact = jnp.concatenate([dxs, dbm, dcm], axis=-1).reshape(t, XBC_W)
    dxbc, g_cw, g_cb = ssd_conv_bwd(xbc, dact, conv_w, conv_b, bsz=bsz, name="ssd_conv_bwd")
    g['ssd_conv_w'] = _shard_rows(g_cw, SHARDED['ssd_conv_w']).reshape(w['ssd_conv_w'].shape)
    g['ssd_conv_b'] = g_cb.reshape(-1)
    ddt = _dt_from_rows(ddtr)

    dparts = [dz, dxbc, ddt, du]
    dhn = matmul_sum(dparts, w_in_parts, nt=True, name="in_proj_dx")
    g_in = jnp.concatenate([matmul_tn(hn, dp, name=f"in_proj_dw_{i}") for i, dp in enumerate(dparts)], axis=1)
    g['w_in'] = _shard_rows(g_in, SHARDED['w_in']).reshape(w['w_in'].shape)
    dx, g_nmix = rowmap_bwd(lambda a, nw: (_rms(a, nw),), [x2], [norm_mix], [dhn], add=dh1, name="rms_mix_bwd")
    g['norm_mix_w'] = g_nmix.reshape(-1)
    return loss, dx.reshape(bsz, seq, d), g


ANY = pl.BlockSpec(memory_space=pl.ANY)


def all_gather(shards, *, name):
    n = len(shards)

    def body(*refs):
        x_refs, out_refs = refs[:n], refs[n:2 * n]
        send_sems, recv_sems, local_sems = refs[2 * n:]
        x, y, c = lax.axis_index("x"), lax.axis_index("y"), lax.axis_index("c")
        me, sibling = (x, y, c), (x, y, 1 - c)
        chips = [(1 - x, y), (x, 1 - y), (1 - x, 1 - y)]

        def copy(k, j, block, to, own=False):
            dst = out_refs[j].at[4 * block[0] + 2 * block[1] + block[2]]
            return pltpu.make_async_remote_copy(
                src_ref=x_refs[j] if own else dst, dst_ref=dst,
                send_sem=send_sems.at[k, j], recv_sem=recv_sems.at[k, j], device_id=to, device_id_type=MESH)

        mine = [pltpu.make_async_copy(x_refs[j], out_refs[j].at[4 * x + 2 * y + c], local_sems.at[j]) for j in range(n)]
        first = [copy(0, j, me, sibling, own=True) for j in range(n)]
        first += [copy(1 + i, j, me, (*chip, c), own=True) for i, chip in enumerate(chips) for j in range(n)]
        for cp in mine + first:
            cp.start()
        passed = []
        for i, chip in enumerate(chips):
            for j in range(n):
                copy(1 + i, j, (*chip, c), me).wait_recv()
                passed.append(copy(4 + i, j, (*chip, c), sibling))
                passed[-1].start()
        for j in range(n):
            copy(0, j, sibling, me).wait_recv()
        for i, chip in enumerate(chips):
            for j in range(n):
                copy(4 + i, j, (*chip, 1 - c), me).wait_recv()
        for cp in first + passed:
            cp.wait_send()
        for cp in mine:
            cp.wait()

    return pl.pallas_call(
        body, name=name, out_shape=[jax.ShapeDtypeStruct((N_DEV,) + s.shape, s.dtype) for s in shards],
        in_specs=[ANY] * n, out_specs=[ANY] * n,
        scratch_shapes=[pltpu.SemaphoreType.DMA((7, n)), pltpu.SemaphoreType.DMA((7, n)),
                        pltpu.SemaphoreType.DMA((n,))],
    )(*shards)


def exchange(sends, *, name):
    n = len(sends)

    def body(*refs):
        send_refs, recv_refs = refs[:n], refs[n:2 * n]
        send_sems, recv_sems, local_sems = refs[2 * n:]
        x, y, c = lax.axis_index("x"), lax.axis_index("y"), lax.axis_index("c")
        me = 4 * x + 2 * y + c
        local = [pltpu.make_async_copy(send_refs[j].at[me], recv_refs[j].at[me], local_sems.at[j]) for j in range(n)]
        for cp in local:
            cp.start()
        copies = []
        for k in range(1, N_DEV):
            px = (1 - x) if k & 4 else x
            py = (1 - y) if k & 2 else y
            pc = (1 - c) if k & 1 else c
            for j in range(n):
                copies.append(pltpu.make_async_remote_copy(
                    src_ref=send_refs[j].at[4 * px + 2 * py + pc], dst_ref=recv_refs[j].at[me],
                    send_sem=send_sems.at[k - 1, j], recv_sem=recv_sems.at[k - 1, j],
                    device_id=(px, py, pc), device_id_type=MESH))
        for cp in copies:
            cp.start()
        for cp in copies:
            cp.wait()
        for cp in local:
            cp.wait()

    return pl.pallas_call(
        body, name=name, out_shape=[jax.ShapeDtypeStruct(s.shape, s.dtype) for s in sends],
        in_specs=[ANY] * n, out_specs=[ANY] * n,
        scratch_shapes=[pltpu.SemaphoreType.DMA((N_DEV - 1, n)), pltpu.SemaphoreType.DMA((N_DEV - 1, n)),
                        pltpu.SemaphoreType.DMA((n,))],
    )(*sends)


HBM_SPEC = pl.BlockSpec(memory_space=pltpu.HBM)
SEM_SPEC = pl.BlockSpec(memory_space=pltpu.SEMAPHORE)
SPLIT_PARAMS = pltpu.CompilerParams(has_side_effects=pltpu.SideEffectType.DATAFLOW_SIDE_EFFECTING)


def _peer_copies(src_refs, land_refs, send_sems, recv_sems, indexed):
    x, y, c = lax.axis_index("x"), lax.axis_index("y"), lax.axis_index("c")
    me = 4 * x + 2 * y + c
    copies = []
    for k in range(1, N_DEV):
        px = (1 - x) if k & 4 else x
        py = (1 - y) if k & 2 else y
        pc = (1 - c) if k & 1 else c
        for j, (src, land) in enumerate(zip(src_refs, land_refs)):
            sem = (k - 1) * len(src_refs) + j
            copies.append(pltpu.make_async_remote_copy(
                src_ref=src.at[4 * px + 2 * py + pc] if indexed else src, dst_ref=land.at[me],
                send_sem=send_sems.at[sem], recv_sem=recv_sems.at[sem],
                device_id=(px, py, pc), device_id_type=MESH))
    return copies


def scatter_start(srcs, *, name, indexed):
    n = len(srcs)
    lands = [lax.empty(s.shape if indexed else (N_DEV,) + s.shape, s.dtype) for s in srcs]

    def body(*refs):
        send_sems, recv_sems = refs[2 * n], refs[2 * n + 1]
        for cp in _peer_copies(refs[:n], refs[n:2 * n], send_sems, recv_sems, indexed):
            cp.start()
        refs[-1][...] = jnp.zeros_like(refs[-1])

    hbm = lambda a: pltpu.HBM(a.shape, a.dtype)
    sems = pltpu.SemaphoreType.DMA(((N_DEV - 1) * n,))
    res = pl.pallas_call(
        body, name=name,
        out_shape=(sems, sems, *[hbm(a) for a in srcs + lands], jax.ShapeDtypeStruct((8, LANES), F32)),
        in_specs=[HBM_SPEC] * (2 * n),
        out_specs=(SEM_SPEC, SEM_SPEC, *[HBM_SPEC] * (2 * n), pl.BlockSpec(memory_space=pltpu.VMEM)),
        input_output_aliases={i: 2 + i for i in range(2 * n)}, compiler_params=SPLIT_PARAMS,
    )(*[pltpu.with_memory_space_constraint(a, pltpu.HBM) for a in srcs + lands])
    return res[0], res[1], list(res[2:2 + n]), list(res[2 + n:2 + 2 * n]), res[-1]


def scatter_wait(send_sems, recv_sems, srcs, lands, after, *, name, indexed):
    n = len(srcs)

    def body(*refs):
        for cp in _peer_copies(refs[:n], refs[n:2 * n], refs[2 * n], refs[2 * n + 1], indexed):
            cp.wait_send()
            cp.wait_recv()

    hbm = lambda a: pltpu.HBM(a.shape, a.dtype)
    res = pl.pallas_call(
        body, name=name, out_shape=tuple(hbm(a) for a in srcs + lands),
        in_specs=[HBM_SPEC] * (2 * n) + [SEM_SPEC, SEM_SPEC, ANY], out_specs=tuple([HBM_SPEC] * (2 * n)),
        input_output_aliases={i: i for i in range(2 * n)}, compiler_params=SPLIT_PARAMS,
    )(*srcs, *lands, send_sems, recv_sems, after)
    return list(res[:n]), list(res[n:])


def _adam_rows(r, c):
    fits = [t for t in range(8, r + 1, 8) if r % t == 0 and N_DEV * t * c * 4 <= 6 * 2 ** 20]
    return max(fits) if fits else r


def adamw(recv, w, m, v, *, name):
    _, r, n = recv.shape
    tr = _adam_rows(r, n)

    def body(r_ref, w_ref, m_ref, v_ref, g_ref, d_ref, nm_ref, nv_ref):
        g = r_ref[0].astype(F32)
        for s in range(1, N_DEV):
            g = g + r_ref[s].astype(F32)
        m_new = ADAM_B1 * m_ref[...] + (1.0 - ADAM_B1) * g
        v_new = ADAM_B2 * v_ref[...] + (1.0 - ADAM_B2) * jnp.square(g)
        m_hat = m_new / (1.0 - ADAM_B1 ** ADAM_STEP)
        v_hat = v_new / (1.0 - ADAM_B2 ** ADAM_STEP)
        g_ref[...] = g
        d_ref[...] = -ADAM_LR * (m_hat / (jnp.sqrt(v_hat) + ADAM_EPS) + ADAM_WD * w_ref[...])
        nm_ref[...] = m_new
        nv_ref[...] = v_new

    blk = pl.BlockSpec((tr, n), lambda i: (i, 0))
    return pl.pallas_call(
        body, name=name, grid=(r // tr,), in_specs=[pl.BlockSpec((N_DEV, tr, n), lambda i: (0, i, 0)), blk, blk, blk],
        out_specs=[blk] * 4, out_shape=[jax.ShapeDtypeStruct((r, n), F32)] * 4,
        compiler_params=_params(("parallel",)))(recv, w, m, v)


def _shard_rows(full, axis):
    if axis == 0:
        return full.reshape(N_DEV, -1)
    r, c = full.shape
    return jnp.transpose(full.reshape(r, N_DEV, c // N_DEV), (1, 0, 2)).reshape(N_DEV, -1)


def _unshard(blocks, axis):
    if axis == 0:
        return blocks.reshape(-1, blocks.shape[-1])
    return jnp.transpose(blocks, (1, 0, 2)).reshape(blocks.shape[1], -1)


def kernel(x, norm_mix_w, w_in, ssd_conv_w, ssd_conv_b, ssd_dt_bias_fwd, ssd_dt_bias_bwd, ssd_a_log_fwd, ssd_a_log_bwd, ssd_d, ssd_norm_w, s5_lambda_re_fwd, s5_lambda_im_fwd, s5_log_step_fwd, s5_lambda_re_bwd, s5_lambda_im_bwd, s5_log_step_bwd, s5_b_re, s5_b_im, s5_c_re_fwd, s5_c_im_fwd, s5_c_re_bwd, s5_c_im_bwd, s5_d, s5_glu_w, s5_glu_b, s5_norm_w, w_out, norm_ffn_w, ffn_w_up, ffn_conv_w, ffn_conv_b, ffn_w_down, norm_final_w, loss_target, m_norm_mix_w, m_w_in, m_ssd_conv_w, m_ssd_conv_b, m_ssd_dt_bias_fwd, m_ssd_dt_bias_bwd, m_ssd_a_log_fwd, m_ssd_a_log_bwd, m_ssd_d, m_ssd_norm_w, m_s5_lambda_re_fwd, m_s5_lambda_im_fwd, m_s5_log_step_fwd, m_s5_lambda_re_bwd, m_s5_lambda_im_bwd, m_s5_log_step_bwd, m_s5_b_re, m_s5_b_im, m_s5_c_re_fwd, m_s5_c_im_fwd, m_s5_c_re_bwd, m_s5_c_im_bwd, m_s5_d, m_s5_glu_w, m_s5_glu_b, m_s5_norm_w, m_w_out, m_norm_ffn_w, m_ffn_w_up, m_ffn_conv_w, m_ffn_conv_b, m_ffn_w_down, m_norm_final_w, v_norm_mix_w, v_w_in, v_ssd_conv_w, v_ssd_conv_b, v_ssd_dt_bias_fwd, v_ssd_dt_bias_bwd, v_ssd_a_log_fwd, v_ssd_a_log_bwd, v_ssd_d, v_ssd_norm_w, v_s5_lambda_re_fwd, v_s5_lambda_im_fwd, v_s5_log_step_fwd, v_s5_lambda_re_bwd, v_s5_lambda_im_bwd, v_s5_log_step_bwd, v_s5_b_re, v_s5_b_im, v_s5_c_re_fwd, v_s5_c_im_fwd, v_s5_c_re_bwd, v_s5_c_im_bwd, v_s5_d, v_s5_glu_w, v_s5_glu_b, v_s5_norm_w, v_w_out, v_norm_ffn_w, v_ffn_w_up, v_ffn_conv_w, v_ffn_conv_b, v_ffn_w_down, v_norm_final_w):
    args = dict(locals())
    strip = lambda n, v: v if n == 'norm_final_w' else v[0]
    w = {n: strip(n, args[n]) for n in WEIGHTS}

    mats = ['w_in', 'w_out', 'ffn_w_up', 'ffn_w_down']
    convs = ['ssd_conv_w', 'ffn_conv_w']
    shard = lambda n: w[n].astype(BF16) if n in mats else w[n]
    early, late = ['w_in', 'ssd_conv_w'], ['w_out', 'ffn_w_up', 'ffn_w_down', 'ffn_conv_w']
    full = dict(w)
    full.update(zip(early, all_gather([shard(n) for n in early], name="weight_all_gather")))
    ssem, rsem, src_thru, land_thru, token = scatter_start([shard(n) for n in late], name="weight_gather_start",
                                                           indexed=False)
    me = 4 * lax.axis_index("x") + 2 * lax.axis_index("y") + lax.axis_index("c")

    def late_weights(after):
        own, landed = scatter_wait(ssem, rsem, src_thru, land_thru, after, name="weight_gather_wait", indexed=False)
        return {n: lax.dynamic_update_index_in_dim(l, o, me, 0) for n, o, l in zip(late, own, landed)}

    full['late'], full['token'] = late_weights, token[:1, :1]

    pending = []

    def send_early(grads, names):
        started = scatter_start([grads[n].astype(BF16) for n in names], name="grad_start_" + names[0], indexed=True)
        pending.append((names,) + started[:4])
        return started[4][:1, :1]

    full['on_grads'] = send_early
    loss, grad_x, g = local_step(x, loss_target, full)

    small = convs + [n for n in WEIGHTS if n not in SHARDED]
    pieces = [g[n].reshape(N_DEV, -1) if n in SHARDED else jnp.broadcast_to(g[n].reshape(1, -1), (N_DEV, g[n].size))
              for n in small]
    pieces.append(jnp.broadcast_to(loss.reshape(1, 1), (N_DEV, 1)))
    total = sum(p.shape[1] for p in pieces)
    nrow = -(-total // (PACK_ROWS * LANES)) * PACK_ROWS
    pieces.append(jnp.zeros((N_DEV, nrow * LANES - total), F32))
    packed_send = jnp.concatenate(pieces, axis=1).reshape(N_DEV, nrow, LANES)
    recvs = exchange([g['w_in'].astype(BF16), packed_send], name="grad_exchange")
    recv = {'w_in': recvs[0]}
    for names, ssem, rsem, src_thru, land_thru in pending:
        own, landed = scatter_wait(ssem, rsem, src_thru, land_thru, recvs[-1], name="grad_wait_" + names[0],
                                   indexed=True)
        for n, o, l in zip(names, own, landed):
            recv[n] = lax.dynamic_update_index_in_dim(l, lax.dynamic_index_in_dim(o, me, 0, keepdims=False), me, 0)

    outs = [{}, {}, {}, {}]
    for n in mats:
        res = adamw(recv[n], w[n], strip(n, args['m_' + n]), strip(n, args['v_' + n]), name="adamw_" + n)
        for o, p in zip(outs, res):
            o[n] = p.reshape(args[n].shape)

    def pack(prefix):
        vals = [strip(n, args[prefix + n]).reshape(-1) for n in small]
        return jnp.pad(jnp.concatenate(vals), (0, nrow * LANES - total + 1)).reshape(nrow, LANES)

    packed = adamw(recvs[-1], pack(''), pack('m_'), pack('v_'), name="adamw_small")
    packed = [p.reshape(-1) for p in packed]
    off = 0
    for n in small:
        size = w[n].size
        for o, p in zip(outs, packed):
            o[n] = p[off:off + size].reshape(args[n].shape)
        off += size
    loss_out = packed[0][off].reshape(())
    return (loss_out, grad_x, *[o[n] for o in outs for n in WEIGHTS])
```

```python
import functools

import jax
import jax.numpy as jnp
from jax import lax
from jax.experimental import pallas as pl
from jax.experimental.pallas import tpu as pltpu

F32, BF16 = jnp.float32, jnp.bfloat16
N_DEV = 8
D_MODEL = 1024
SSD_W, HEADS, HDIM, SGROUPS, HPG, NSTATE, SCONV, QC = 1024, 16, 64, 4, 4, 128, 5, 128
XBC_W = SSD_W + 2 * SGROUPS * NSTATE
S5_W, S5_G, S5_C, S5_P, S5_Q = 512, 32, 16, 64, 16
S5_QC = S5_Q * S5_C
CARRY_ROWS = 32
DFF, FCONV = 2816, 3
FFN_BLK, FFN_PAD = 704, 768
EPS = 1e-6
ADAM_LR, ADAM_B1, ADAM_B2, ADAM_EPS, ADAM_WD, ADAM_STEP = 0.001, 0.9, 0.999, 1e-08, 0.01, 10
LANES = 128
MESH = pl.DeviceIdType.MESH

WEIGHTS = ['norm_mix_w', 'w_in', 'ssd_conv_w', 'ssd_conv_b', 'ssd_dt_bias_fwd', 'ssd_dt_bias_bwd', 'ssd_a_log_fwd',
           'ssd_a_log_bwd', 'ssd_d', 'ssd_norm_w', 's5_lambda_re_fwd', 's5_lambda_im_fwd', 's5_log_step_fwd',
           's5_lambda_re_bwd', 's5_lambda_im_bwd', 's5_log_step_bwd', 's5_b_re', 's5_b_im', 's5_c_re_fwd', 's5_c_im_fwd',
           's5_c_re_bwd', 's5_c_im_bwd', 's5_d', 's5_glu_w', 's5_glu_b', 's5_norm_w', 'w_out', 'norm_ffn_w', 'ffn_w_up',
           'ffn_conv_w', 'ffn_conv_b', 'ffn_w_down', 'norm_final_w']
SHARDED = {'w_in': 1, 'ssd_conv_w': 1, 'w_out': 0, 'ffn_w_up': 1, 'ffn_conv_w': 1, 'ffn_w_down': 0}
FULL_SHAPE = {'w_in': (1024, 3616), 'ssd_conv_w': (5, 2048), 'w_out': (1536, 1024), 'ffn_w_up': (1024, 5632),
              'ffn_conv_w': (3, 5632), 'ffn_w_down': (2816, 1024)}
PACK_ROWS = 512


def _pick(n, cap=1536):
    if n <= cap:
        return n
    return max(t for t in range(LANES, cap + 1, LANES) if n % t == 0)


def _params(sem):
    return pltpu.CompilerParams(dimension_semantics=sem)


def _bd(a, b, ca, cb):
    return lax.dot_general(a.astype(BF16), b.astype(BF16), (((ca,), (cb,)), ((), ())), preferred_element_type=F32)


@jax.custom_vjp
def dot_nn(a, b):
    return _bd(a, b, 1, 0)


dot_nn.defvjp(lambda a, b: (_bd(a, b, 1, 0), (a, b)),
              lambda r, g: (_bd(g, r[1], 1, 1).astype(r[0].dtype), _bd(r[0], g, 0, 0).astype(r[1].dtype)))


@jax.custom_vjp
def dot_nt(a, b):
    return _bd(a, b, 1, 1)


dot_nt.defvjp(lambda a, b: (_bd(a, b, 1, 1), (a, b)),
              lambda r, g: (_bd(g, r[1], 1, 0).astype(r[0].dtype), _bd(g, r[0], 0, 0).astype(r[1].dtype)))


@jax.custom_vjp
def dot_tn(a, b):
    return _bd(a, b, 0, 0)


dot_tn.defvjp(lambda a, b: (_bd(a, b, 0, 0), (a, b)),
              lambda r, g: (_bd(r[1], g, 1, 1).astype(r[0].dtype), _bd(r[0], g, 1, 0).astype(r[1].dtype)))


def _split3(x):
    hi = x.astype(BF16)
    r = x - hi.astype(F32)
    mid = r.astype(BF16)
    lo = (r - mid.astype(F32)).astype(BF16)
    return hi, mid, lo


def _cum_matrix(q, upper):
    ri = lax.broadcasted_iota(jnp.int32, (q, q), 0)
    ci = lax.broadcasted_iota(jnp.int32, (q, q), 1)
    return jnp.where((ci >= ri) if upper else (ci <= ri), 1.0, 0.0).astype(BF16)


def _exact_right(x, mat):
    return sum(jnp.dot(p, mat, preferred_element_type=F32) for p in _split3(x))


@functools.partial(jax.custom_vjp, nondiff_argnums=(1,))
def cum_row(x, rev):
    return _exact_right(x, _cum_matrix(x.shape[1], not rev))


cum_row.defvjp(lambda x, rev: (cum_row(x, rev), None),
               lambda rev, _, g: (_exact_right(g, _cum_matrix(g.shape[1], rev)),))


def _softplus(x):
    return jnp.maximum(x, 0.0) + jnp.log(1.0 + jnp.exp(-jnp.abs(x)))


def _silu(x):
    return x * jax.nn.sigmoid(x)


def _gelu(x):
    return 0.5 * x * (1.0 + jnp.tanh(0.7978845608028654 * (x + 0.044715 * (x * x * x))))


def _rms(x, w):
    xf = x.astype(F32)
    return xf * lax.rsqrt(jnp.mean(xf * xf, axis=-1, keepdims=True) + EPS) * w


def matmul_sum(a_list, b_list, *, name, out_dtype=F32, add=None, tm=512, nt=False):
    a_arrs = [a[0] if isinstance(a, tuple) else a for a in a_list]
    b_arrs = [b[0] if isinstance(b, tuple) else b for b in b_list]
    m, n = a_arrs[0].shape[0], b_arrs[0].shape[-2 if nt else -1]
    tm, tn, k = min(tm, m), _pick(n), len(a_list)

    def body(*refs):
        acc = None
        for a_ref, b_ref in zip(refs[:k], refs[k:2 * k]):
            p = _bd(a_ref[...], b_ref[...], 1, 1 if nt else 0)
            acc = p if acc is None else acc + p
        if add is not None:
            acc = acc + refs[2 * k][...]
        refs[-1][...] = acc.astype(out_dtype)

    def a_spec(a):
        if isinstance(a, tuple):
            return pl.BlockSpec((tm, a[1]), lambda i, j, blk=a[2]: (i, blk))
        return pl.BlockSpec((tm, a.shape[1]), lambda i, j: (i, 0))

    def b_spec(b):
        arr, p = b if isinstance(b, tuple) else (b, None)
        kk = arr.shape[-1 if nt else -2]
        shape, idx = ((tn, kk), lambda j: (j, 0)) if nt else ((kk, tn), lambda j: (0, j))
        if p is None:
            return pl.BlockSpec(shape, lambda i, j: idx(j))
        return pl.BlockSpec((None,) + shape, lambda i, j, p=p: (p,) + idx(j))

    in_specs = [a_spec(a) for a in a_list] + [b_spec(b) for b in b_list]
    args = a_arrs + b_arrs
    if add is not None:
        in_specs.append(pl.BlockSpec((tm, tn), lambda i, j: (i, j)))
        args.append(add)
    return pl.pallas_call(
        body, name=name, grid=(m // tm, n // tn), in_specs=in_specs,
        out_specs=pl.BlockSpec((tm, tn), lambda i, j: (i, j)),
        out_shape=jax.ShapeDtypeStruct((m, n), out_dtype),
        compiler_params=_params(("parallel", "parallel")))(*args)


def matmul_cols(a, b3, *, name, out_dtype=F32, tm=1024):
    m, kk = a.shape
    p, _, nb = b3.shape
    tm, tn = min(tm, m), _pick(nb, 768)
    per = nb // tn

    def body(a_ref, b_ref, o_ref):
        o_ref[...] = _bd(a_ref[...], b_ref[...], 1, 0).astype(out_dtype)

    return pl.pallas_call(
        body, name=name, grid=(m // tm, p * per),
        in_specs=[pl.BlockSpec((tm, kk), lambda i, j: (i, 0)),
                  pl.BlockSpec((None, kk, tn), lambda i, j: (j // per, 0, j % per))],
        out_specs=pl.BlockSpec((tm, tn), lambda i, j: (i, j)),
        out_shape=jax.ShapeDtypeStruct((m, p * nb), out_dtype),
        compiler_params=_params(("parallel", "parallel")))(a, b3)


def matmul_tn(a, b, *, name, tm=1024, out_blocks=None):
    m, k = a.shape
    n = b.shape[1]
    nb = n // (out_blocks or 1)
    tm, tk, tn = min(tm, m), _pick(k), _pick(nb, 768 if out_blocks else 1536)
    per = nb // tn

    def body(a_ref, b_ref, o_ref):
        @pl.when(pl.program_id(2) == 0)
        def _():
            o_ref[...] = jnp.zeros_like(o_ref)

        o_ref[...] += _bd(a_ref[...], b_ref[...], 0, 0)

    if out_blocks:
        out_spec = pl.BlockSpec((None, tk, tn), lambda i, j, t: (j // per, i, j % per))
        out_shape = jax.ShapeDtypeStruct((out_blocks, k, nb), F32)
    else:
        out_spec = pl.BlockSpec((tk, tn), lambda i, j, t: (i, j))
        out_shape = jax.ShapeDtypeStruct((k, n), F32)
    return pl.pallas_call(
        body, name=name, grid=(k // tk, n // tn, m // tm),
        in_specs=[pl.BlockSpec((tm, tk), lambda i, j, t: (t, i)), pl.BlockSpec((tm, tn), lambda i, j, t: (t, j))],
        out_specs=out_spec, out_shape=out_shape,
        compiler_params=_params(("parallel", "parallel", "arbitrary")))(a, b)


def _row_spec(r, tm):
    if isinstance(r, tuple):
        arr, width, blk = r
        return arr, pl.BlockSpec((tm, width), lambda i, blk=blk: (i, blk))
    return r, pl.BlockSpec((tm, r.shape[1]), lambda i: (i, 0))


def _full_spec(p):
    return pl.BlockSpec(p.shape, lambda i: (0,) * p.ndim)


def rowmap_fwd(fn, rows, params, outs, *, name, tm=256):
    pairs = [_row_spec(r, tm) for r in rows]
    m = pairs[0][0].shape[0]
    tm = min(tm, m)
    pairs = [_row_spec(r, tm) for r in rows]
    nr, npar = len(rows), len(params)

    def body(*refs):
        res = fn(*[r[...] for r in refs[:nr + npar]])
        for o_ref, v in zip(refs[nr + npar:], res):
            o_ref[...] = v.astype(o_ref.dtype)

    return pl.pallas_call(
        body, name=name, grid=(m // tm,),
        in_specs=[s for _, s in pairs] + [_full_spec(p) for p in params],
        out_specs=[pl.BlockSpec((tm, c), lambda i: (i, 0)) for c, _ in outs],
        out_shape=[jax.ShapeDtypeStruct((m, c), dt) for c, dt in outs],
        compiler_params=_params(("parallel",)))(*[a for a, _ in pairs], *params)


def rowmap_bwd(fn, rows, params, cts, *, name, row_dtypes=None, add=None, tm=256):
    m = _row_spec(rows[0], tm)[0].shape[0]
    tm = min(tm, m)
    rp = [_row_spec(r, tm) for r in rows]
    cp = [_row_spec(c, tm) for c in cts]
    nr, npar, nc = len(rows), len(params), len(cts)
    row_dtypes = row_dtypes or [F32] * nr
    widths = [s.block_shape[1] for _, s in rp]

    def body(*refs):
        ins = [r[...] for r in refs[:nr + npar]]
        ins = [v.astype(F32) for v in ins]
        ct = tuple(r[...].astype(F32) for r in refs[nr + npar:nr + npar + nc])
        base = nr + npar + nc
        extra = None
        if add is not None:
            extra = refs[base][...]
            base += 1
        _, pull = jax.vjp(fn, *ins)
        grads = pull(ct)
        for j in range(nr):
            g = grads[j]
            if j == 0 and extra is not None:
                g = g + extra
            refs[base + j][...] = g.astype(refs[base + j].dtype)

        @pl.when(pl.program_id(0) == 0)
        def _():
            for j in range(npar):
                refs[base + nr + j][...] = jnp.zeros_like(refs[base + nr + j])

        for j in range(npar):
            refs[base + nr + j][...] += grads[nr + j]

    in_specs = [s for _, s in rp] + [_full_spec(p) for p in params] + [s for _, s in cp]
    args = [a for a, _ in rp] + list(params) + [a for a, _ in cp]
    if add is not None:
        in_specs.append(pl.BlockSpec((tm, widths[0]), lambda i: (i, 0)))
        args.append(add)
    out_specs = [pl.BlockSpec((tm, w), lambda i: (i, 0)) for w in widths] + [_full_spec(p) for p in params]
    out_shape = [jax.ShapeDtypeStruct((m, w), dt) for w, dt in zip(widths, row_dtypes)]
    out_shape += [jax.ShapeDtypeStruct(p.shape, F32) for p in params]
    return pl.pallas_call(
        body, name=name, grid=(m // tm,), in_specs=in_specs, out_specs=out_specs, out_shape=out_shape,
        compiler_params=_params(("arbitrary",)))(*args)


def loss_head(h, target, w, *, name, tm=256):
    m, d = h.shape
    tm = min(tm, m)

    def body(h_ref, t_ref, w_ref, loss_ref, dh_ref, dw_ref):
        y, pull = jax.vjp(_rms, h_ref[...], w_ref[...])
        err = y - t_ref[...]
        dh, dw = pull(err * (1.0 / d))

        @pl.when(pl.program_id(0) == 0)
        def _():
            loss_ref[...] = jnp.zeros_like(loss_ref)
            dw_ref[...] = jnp.zeros_like(dw_ref)

        loss_ref[...] += (0.5 / d) * jnp.sum(err * err, keepdims=True)
        dw_ref[...] += dw
        dh_ref[...] = dh

    row = pl.BlockSpec((tm, d), lambda i: (i, 0))
    return pl.pallas_call(
        body, name=name, grid=(m // tm,), in_specs=[row, row, _full_spec(w)],
        out_specs=[pl.BlockSpec((1, 1), lambda i: (0, 0)), row, _full_spec(w)],
        out_shape=[jax.ShapeDtypeStruct((1, 1), F32), jax.ShapeDtypeStruct((m, d), F32),
                   jax.ShapeDtypeStruct(w.shape, F32)],
        compiler_params=_params(("arbitrary",)))(h, target, w)


def _shift(x, s):
    if s == 0:
        return x
    n = x.shape[0]
    t = lax.broadcasted_iota(jnp.int32, x.shape, 0)
    rolled = pltpu.roll(x, (-s) % n, 0)
    return jnp.where((t + s >= 0) & (t + s < n), rolled, 0.0)


def _conv(x, w, b):
    k = w.shape[0]
    acc = b + w[k // 2:k // 2 + 1, :] * x
    for j in range(k):
        if j != k // 2:
            acc = acc + w[j:j + 1, :] * _shift(x, j - k // 2)
    return acc


def _conv_bwd(x, dc, w):
    k = w.shape[0]
    dx = None
    dws = []
    for j in range(k):
        s = j - k // 2
        term = w[j:j + 1, :] * _shift(dc, -s)
        dx = term if dx is None else dx + term
        dws.append(jnp.sum(dc * _shift(x, s), axis=0, keepdims=True))
    return dx, jnp.concatenate(dws, axis=0), jnp.sum(dc, axis=0, keepdims=True)


def _dsilu(c):
    s = jax.nn.sigmoid(c)
    return s * (1.0 + c * (1.0 - s))


def ssd_conv_fwd(xbc, w, b, *, bsz, name):
    t, c = xbc.shape
    seq, ct = t // bsz, 256

    def body(x_ref, w_ref, b_ref, o_ref):
        o_ref[...] = _silu(_conv(x_ref[...], w_ref[...], b_ref[...]))

    return pl.pallas_call(
        body, name=name, grid=(c // ct, bsz),
        in_specs=[pl.BlockSpec((seq, ct), lambda j, i: (i, j)), pl.BlockSpec((w.shape[0], ct), lambda j, i: (0, j)),
                  pl.BlockSpec((1, ct), lambda j, i: (0, j))],
        out_specs=pl.BlockSpec((seq, ct), lambda j, i: (i, j)),
        out_shape=jax.ShapeDtypeStruct((t, c), F32),
        compiler_params=_params(("parallel", "parallel")))(xbc, w, b)


def ssd_conv_bwd(xbc, dact, w, b, *, bsz, name):
    t, c = xbc.shape
    seq, ct, k = t // bsz, 256, w.shape[0]

    def body(x_ref, g_ref, w_ref, b_ref, dx_ref, dw_ref, db_ref):
        x, wv = x_ref[...], w_ref[...]
        dc = g_ref[...] * _dsilu(_conv(x, wv, b_ref[...]))
        dx, dw, db = _conv_bwd(x, dc, wv)
        dx_ref[...] = dx

        @pl.when(pl.program_id(1) == 0)
        def _():
            dw_ref[...] = jnp.zeros_like(dw_ref)
            db_ref[...] = jnp.zeros_like(db_ref)

        dw_ref[...] += dw
        db_ref[...] += db

    blk = pl.BlockSpec((seq, ct), lambda j, i: (i, j))
    wspec, bspec = pl.BlockSpec((k, ct), lambda j, i: (0, j)), pl.BlockSpec((1, ct), lambda j, i: (0, j))
    return pl.pallas_call(
        body, name=name, grid=(c // ct, bsz), in_specs=[blk, blk, wspec, bspec], out_specs=[blk, wspec, bspec],
        out_shape=[jax.ShapeDtypeStruct((t, c), F32), jax.ShapeDtypeStruct((k, c), F32),
                   jax.ShapeDtypeStruct((1, c), F32)],
        compiler_params=_params(("parallel", "arbitrary")))(xbc, dact, w, b)


def _ffn_specs(seq, ct, k, nblk):
    val = pl.BlockSpec((seq, ct), lambda j, i: (i, j))
    gate = pl.BlockSpec((seq, ct), lambda j, i: (i, nblk + j))
    wv, wg = pl.BlockSpec((k, ct), lambda j, i: (0, j)), pl.BlockSpec((k, ct), lambda j, i: (0, nblk + j))
    bv, bg = pl.BlockSpec((1, ct), lambda j, i: (0, j)), pl.BlockSpec((1, ct), lambda j, i: (0, nblk + j))
    return val, gate, wv, wg, bv, bg


def ffn_act_fwd(up, w, b, *, bsz, name):
    t = up.shape[0]
    half = up.shape[1] // 2
    seq, ct, k = t // bsz, 256, w.shape[0]
    val, gate, wv, wg, bv, bg = _ffn_specs(seq, ct, k, half // ct)

    def body(v_ref, g_ref, wv_ref, wg_ref, bv_ref, bg_ref, o_ref):
        vc = _conv(v_ref[...], wv_ref[...], bv_ref[...])
        gc = _conv(g_ref[...], wg_ref[...], bg_ref[...])
        o_ref[...] = (_silu(gc) * vc).astype(BF16)

    return pl.pallas_call(
        body, name=name, grid=(half // ct, bsz), in_specs=[val, gate, wv, wg, bv, bg], out_specs=val,
        out_shape=jax.ShapeDtypeStruct((t, half), BF16),
        compiler_params=_params(("parallel", "parallel")))(up, up, w, w, b, b)


def ffn_act_bwd(up, dact, w, b, *, bsz, name):
    t = up.shape[0]
    half = up.shape[1] // 2
    seq, ct, k = t // bsz, 256, w.shape[0]
    val, gate, wv, wg, bv, bg = _ffn_specs(seq, ct, k, half // ct)

    def body(v_ref, g_ref, wv_ref, wg_ref, bv_ref, bg_ref, d_ref, dv_ref, dg_ref, dwv_ref, dwg_ref, dbv_ref, dbg_ref):
        v, g = v_ref[...], g_ref[...]
        vc = _conv(v, wv_ref[...], bv_ref[...])
        gc = _conv(g, wg_ref[...], bg_ref[...])
        d = d_ref[...].astype(F32)
        dv, dwv, dbv = _conv_bwd(v, d * _silu(gc), wv_ref[...])
        dg, dwg, dbg = _conv_bwd(g, d * vc * _dsilu(gc), wg_ref[...])
        dv_ref[...] = dv.astype(BF16)
        dg_ref[...] = dg.astype(BF16)

        @pl.when(pl.program_id(1) == 0)
        def _():
            for r in (dwv_ref, dwg_ref, dbv_ref, dbg_ref):
                r[...] = jnp.zeros_like(r)

        dwv_ref[...] += dwv
        dwg_ref[...] += dwg
        dbv_ref[...] += dbv
        dbg_ref[...] += dbg

    return pl.pallas_call(
        body, name=name, grid=(half // ct, bsz), in_specs=[val, gate, wv, wg, bv, bg, val],
        out_specs=[val, val, wv, wv, bv, bv],
        out_shape=[jax.ShapeDtypeStruct((t, half), BF16), jax.ShapeDtypeStruct((t, half), BF16),
                   jax.ShapeDtypeStruct((k, half), F32), jax.ShapeDtypeStruct((k, half), F32),
                   jax.ShapeDtypeStruct((1, half), F32), jax.ShapeDtypeStruct((1, half), F32)],
        compiler_params=_params(("parallel", "arbitrary")))(up, up, w, w, b, b, dact)


def _sel_row(a, h):
    oh = (lax.broadcasted_iota(jnp.int32, (a.shape[0], 1), 0) == h).astype(F32)
    return jnp.sum(a * oh, axis=0, keepdims=True)


def _ssd_chunk(xp, dtr, bm, cm, prev, bias_r, alog_r, dskip_r, rev):
    q = dtr.shape[1]
    ri = lax.broadcasted_iota(jnp.int32, (q, q), 0)
    ci = lax.broadcasted_iota(jnp.int32, (q, q), 1)
    mask = (ci >= ri) if rev else (ci <= ri)
    lane_lo, row_lo = ci < HDIM, ri < HDIM
    dt_r = _softplus(dtr + bias_r)
    dta_r = dt_r * (-jnp.exp(alog_r))
    cs_r = cum_row(dta_r, rev)
    scores = dot_nt(cm, bm)

    def per_row(v):
        return jnp.broadcast_to(v, (q, q)).T

    ys, news = [], []
    for p in range(HPG // 2):
        ha = 2 * p + (HPG if rev else 0)
        hb = ha + 1
        cs_a, cs_b = _sel_row(cs_r, ha), _sel_row(cs_r, hb)
        csq_a, csq_b = per_row(cs_a), per_row(cs_b)
        seg_a = jnp.exp(jnp.where(mask, csq_a - cs_a, -1e30))
        seg_b = jnp.exp(jnp.where(mask, csq_b - cs_b, -1e30))
        csq = jnp.where(lane_lo, csq_a, csq_b)
        xdt = xp[p] * jnp.where(lane_lo, per_row(_sel_row(dt_r, ha)), per_row(_sel_row(dt_r, hb)))
        tot_a = jnp.sum(_sel_row(dta_r, ha), axis=1, keepdims=True)
        tot_b = jnp.sum(_sel_row(dta_r, hb), axis=1, keepdims=True)
        y = jnp.where(lane_lo, dot_nn(scores * seg_a, xdt), dot_nn(scores * seg_b, xdt))
        y = y + dot_nt(cm, prev[p]) * jnp.exp(csq)
        if not rev:
            y = y + jnp.where(lane_lo, _sel_row(dskip_r, ha), _sel_row(dskip_r, hb)) * xp[p]
        ys.append(y)
        st = dot_tn(xdt * jnp.exp(jnp.where(lane_lo, tot_a, tot_b) - csq), bm)
        news.append(jnp.exp(jnp.where(row_lo, tot_a, tot_b)) * prev[p] + st)
    return tuple(ys), tuple(news)


NPAIR = HPG // 2


def _ssd_specs(seq, nc):
    xs = pl.BlockSpec((None, seq, HPG * HDIM), lambda b, g: (b, 0, g))
    bm = pl.BlockSpec((None, seq, NSTATE), lambda b, g: (b, 0, SSD_W // NSTATE + g))
    cm = pl.BlockSpec((None, seq, NSTATE), lambda b, g: (b, 0, SSD_W // NSTATE + SGROUPS + g))
    dtr = pl.BlockSpec((None, None, 2 * HPG, seq), lambda b, g: (b, g, 0, 0))
    pr = pl.BlockSpec((None, 2 * HPG, 1), lambda b, g: (g, 0, 0))
    st = pl.BlockSpec((None, None, 2, nc, NPAIR, 2 * HDIM, NSTATE), lambda b, g: (b, g, 0, 0, 0, 0, 0))
    return xs, bm, cm, dtr, pr, st


def _pair_cols(p):
    return slice(2 * HDIM * p, 2 * HDIM * (p + 1))


def ssd_scan_fwd(act, dtr, prs, *, name):
    bsz, seq, _ = act.shape
    nc = seq // QC
    xs, bm, cm, dtrs, pr, st = _ssd_specs(seq, nc)

    def body(x_ref, b_ref, c_ref, dtr_ref, br_ref, ar_ref, dk_ref, y_ref, st_ref):
        par = (br_ref[...], ar_ref[...], dk_ref[...])
        y_ref[...] = jnp.zeros_like(y_ref)

        def step(i, carry):
            new = []
            for rev in (False, True):
                k = (nc - 1 - i) if rev else i
                rows = pl.ds(pl.multiple_of(k * QC, QC), QC)
                xp = tuple(x_ref[rows, _pair_cols(p)] for p in range(NPAIR))
                for p in range(NPAIR):
                    st_ref[int(rev), k, p] = carry[rev][p]
                ys, nw = _ssd_chunk(xp, dtr_ref[:, rows], b_ref[rows, :], c_ref[rows, :], carry[rev], *par, rev)
                for p in range(NPAIR):
                    y_ref[rows, _pair_cols(p)] += ys[p]
                new.append(nw)
            return tuple(new)

        zero = tuple(jnp.zeros((2 * HDIM, NSTATE), F32) for _ in range(NPAIR))
        lax.fori_loop(0, nc, step, (zero, zero))

    return pl.pallas_call(
        body, name=name, grid=(bsz, SGROUPS), in_specs=[xs, bm, cm, dtrs, pr, pr, pr], out_specs=[xs, st],
        out_shape=[jax.ShapeDtypeStruct((bsz, seq, SSD_W), F32),
                   jax.ShapeDtypeStruct((bsz, SGROUPS, 2, nc, NPAIR, 2 * HDIM, NSTATE), F32)],
        compiler_params=_params(("parallel", "parallel")))(act, act, act, dtr, *prs)


def ssd_scan_bwd(act, dtr, prs, states, dy, *, name):
    bsz, seq, _ = act.shape
    nc = seq // QC
    xs, bm, cm, dtrs, pr, st = _ssd_specs(seq, nc)
    grp = pl.BlockSpec((None, seq, NSTATE), lambda b, g: (b, 0, g))
    dpr = pl.BlockSpec((None, None, 2 * HPG, 1), lambda b, g: (b, g, 0, 0))

    def body(x_ref, b_ref, c_ref, dtr_ref, br_ref, ar_ref, dk_ref, st_ref, dy_ref,
             dx_ref, db_ref, dc_ref, ddtr_ref, gbr_ref, gar_ref, gdk_ref):
        par = (br_ref[...], ar_ref[...], dk_ref[...])
        pgrads = (gbr_ref, gar_ref, gdk_ref)
        for r in pgrads + (dx_ref, db_ref, dc_ref, ddtr_ref):
            r[...] = jnp.zeros_like(r)

        def bstep(i, dcarry):
            new = []
            for rev in (False, True):
                k = i if rev else (nc - 1 - i)
                rows = pl.ds(pl.multiple_of(k * QC, QC), QC)
                xp = tuple(x_ref[rows, _pair_cols(p)] for p in range(NPAIR))
                prev = tuple(st_ref[int(rev), k, p] for p in range(NPAIR))
                _, pull = jax.vjp(functools.partial(_ssd_chunk, rev=rev), xp, dtr_ref[:, rows], b_ref[rows, :],
                                  c_ref[rows, :], prev, *par)
                dyp = tuple(dy_ref[rows, _pair_cols(p)] for p in range(NPAIR))
                gx, gdt, gb, gc, gprev, *gpar = pull((dyp, dcarry[rev]))
                for p in range(NPAIR):
                    dx_ref[rows, _pair_cols(p)] += gx[p]
                ddtr_ref[:, rows] += gdt
                db_ref[rows, :] += gb
                dc_ref[rows, :] += gc
                for r, g in zip(pgrads, gpar):
                    r[...] += g
                new.append(gprev)
            return tuple(new)

        zero = tuple(jnp.zeros((2 * HDIM, NSTATE), F32) for _ in range(NPAIR))
        lax.fori_loop(0, nc, bstep, (zero, zero))

    out_shape = [jax.ShapeDtypeStruct((bsz, seq, SSD_W), F32),
                 jax.ShapeDtypeStruct((bsz, seq, SGROUPS * NSTATE), F32),
                 jax.ShapeDtypeStruct((bsz, seq, SGROUPS * NSTATE), F32),
                 jax.ShapeDtypeStruct(dtr.shape, F32)]
    out_shape += [jax.ShapeDtypeStruct((bsz, SGROUPS, 2 * HPG, 1), F32)] * 3
    return pl.pallas_call(
        body, name=name, grid=(bsz, SGROUPS), in_specs=[xs, bm, cm, dtrs, pr, pr, pr, st, xs],
        out_specs=[xs, grp, grp, dtrs, dpr, dpr, dpr], out_shape=out_shape,
        compiler_params=_params(("parallel", "parallel")))(act, act, act, dtr, *prs, states, dy)


def _s5_direction(lam_re, lam_im, log_step, b_re, b_im, c_re, c_im, rev):
    q = S5_Q
    step = jnp.exp(log_step)[:, None]
    lr, li = lam_re * step, lam_im * step
    mag = jnp.exp(lr)
    ar, ai = mag * jnp.cos(li), mag * jnp.sin(li)
    den = lam_re * lam_re + lam_im * lam_im
    cr = ((ar - 1.0) * lam_re + ai * lam_im) / den
    ci = (ai * lam_re - (ar - 1.0) * lam_im) / den
    bbr = cr[..., None] * b_re - ci[..., None] * b_im
    bbi = cr[..., None] * b_im + ci[..., None] * b_re
    d = jnp.arange(q + 1, dtype=F32)[None, :, None]
    pm = jnp.exp(d * lr[:, None, :])
    pr, pi = pm * jnp.cos(d * li[:, None, :]), pm * jnp.sin(d * li[:, None, :])
    er = pr[..., None] * bbr[:, None] - pi[..., None] * bbi[:, None]
    ei = pr[..., None] * bbi[:, None] + pi[..., None] * bbr[:, None]
    hp = lax.Precision.HIGHEST
    k = (jnp.einsum('gcp,gdpz->gdcz', c_re, er[:, :q], precision=hp)
         - jnp.einsum('gcp,gdpz->gdcz', c_im, ei[:, :q], precision=hp))
    e = jnp.concatenate([er[:, :q], ei[:, :q]], axis=2)
    wt = jnp.transpose(e if rev else e[:, ::-1], (0, 1, 3, 2))
    p1r, p1i = pr[:, 1:], pi[:, 1:]
    if rev:
        p1r, p1i = p1r[:, ::-1], p1i[:, ::-1]
    m_re = c_re[:, None] * p1r[:, :, None, :] - c_im[:, None] * p1i[:, :, None, :]
    m_im = -c_re[:, None] * p1i[:, :, None, :] - c_im[:, None] * p1r[:, :, None, :]
    mt = jnp.transpose(jnp.concatenate([m_re, m_im], axis=-1), (0, 3, 1, 2))
    da = jnp.concatenate([pr[:, q], pr[:, q]], axis=-1)
    db = jnp.concatenate([-pi[:, q], pi[:, q]], axis=-1)
    return k, wt, mt, da, db


def _s5_operators(lf_re, lf_im, lsf, lb_re, lb_im, lsb, b_re, b_im, cf_re, cf_im, cb_re, cb_im):
    kf, wtf, mtf, daf, dbf = _s5_direction(lf_re, lf_im, lsf, b_re, b_im, cf_re, cf_im, False)
    kb, wtb, mtb, dab, dbb = _s5_direction(lb_re, lb_im, lsb, b_re, b_im, cb_re, cb_im, True)
    g = kf.shape[0]
    lags = jnp.concatenate([kb[:, :0:-1], kf[:, :1] + kb[:, :1], kf[:, 1:]], axis=1)
    tt = jnp.transpose(lags, (0, 1, 3, 2))
    wt = jnp.concatenate([wtf.reshape(g, S5_QC, 2 * S5_P), wtb.reshape(g, S5_QC, 2 * S5_P)], axis=-1)
    mt = jnp.concatenate([mtf.reshape(g, 2 * S5_P, S5_QC), mtb.reshape(g, 2 * S5_P, S5_QC)], axis=1)
    return tt, wt, mt, jnp.concatenate([daf, dab], -1), jnp.concatenate([dbf, dbb], -1)


def _gspec(*shape):
    return pl.BlockSpec((None,) + shape, lambda g: (g,) + (0,) * len(shape))


S5_HALVES = S5_QC // LANES


def _toeplitz_block(s, t):
    per = LANES // S5_C
    return t // per, slice(s * S5_C, (s + 1) * S5_C), slice((t % per) * S5_C, (t % per + 1) * S5_C)


def s5_toeplitz(kt, *, name):
    g = kt.shape[0]

    def body(k_ref, t_ref):
        for s in range(S5_Q):
            for t in range(S5_Q):
                t_ref[_toeplitz_block(s, t)] = k_ref[t - s + S5_Q - 1]

    return pl.pallas_call(
        body, name=name, grid=(g,), in_specs=[_gspec(2 * S5_Q - 1, S5_C, S5_C)],
        out_specs=_gspec(S5_HALVES, S5_QC, LANES), out_shape=jax.ShapeDtypeStruct((g, S5_HALVES, S5_QC, LANES), F32),
        compiler_params=_params(("parallel",)))(kt)


def s5_toeplitz_bwd(dtt, *, name):
    g = dtt.shape[0]

    def body(d_ref, k_ref):
        for j in range(2 * S5_Q - 1):
            acc = None
            for s in range(S5_Q):
                t = j - (S5_Q - 1) + s
                if 0 <= t < S5_Q:
                    blk = d_ref[_toeplitz_block(s, t)]
                    acc = blk if acc is None else acc + blk
            k_ref[j] = acc

    return pl.pallas_call(
        body, name=name, grid=(g,), in_specs=[_gspec(S5_HALVES, S5_QC, LANES)],
        out_specs=_gspec(2 * S5_Q - 1, S5_C, S5_C), out_shape=jax.ShapeDtypeStruct((g, 2 * S5_Q - 1, S5_C, S5_C), F32),
        compiler_params=_params(("parallel",)))(dtt)


def s5_state_in(u, wt, *, name):
    g, r, _ = u.shape

    def body(u_ref, w_ref, o_ref):
        o_ref[...] = _bd(u_ref[...], w_ref[...], 1, 0)

    return pl.pallas_call(
        body, name=name, grid=(g,), in_specs=[_gspec(r, S5_QC), _gspec(S5_QC, 4 * S5_P)],
        out_specs=_gspec(r, 4 * S5_P), out_shape=jax.ShapeDtypeStruct((g, r, 4 * S5_P), F32),
        compiler_params=_params(("parallel",)))(u, wt)


def _swap(h):
    return pltpu.roll(h, S5_P, 1)


def s5_carry_fwd(s, da, db, *, name):
    nck, rows, _ = s.shape
    w = 2 * S5_P

    def body(s_ref, da_ref, db_ref, h_ref):
        dirs = ((False, slice(0, w)), (True, slice(w, 2 * w)))
        coef = [(da_ref[:, cols], db_ref[:, cols]) for _, cols in dirs]

        def step(i, hs):
            new = []
            for (rev, cols), (a, b), h in zip(dirs, coef, hs):
                k = (nck - 1 - i) if rev else i
                h_ref[k, :, cols] = h
                new.append(a * h + b * _swap(h) + s_ref[k, :, cols])
            return tuple(new)

        z = jnp.zeros((rows, w), F32)
        lax.fori_loop(0, nck, step, (z, z), unroll=2)

    rt = min(CARRY_ROWS, rows)
    big, small = pl.BlockSpec((nck, rt, 2 * w), lambda i: (0, i, 0)), pl.BlockSpec((rt, 2 * w), lambda i: (i, 0))
    rows = rt
    return pl.pallas_call(
        body, name=name, grid=(s.shape[1] // rt,), in_specs=[big, small, small], out_specs=big,
        out_shape=jax.ShapeDtypeStruct(s.shape, F32), compiler_params=_params(("parallel",)))(s, da, db)


def s5_carry_bwd(hin, dh, da, db, *, name):
    nck, rows, _ = hin.shape
    w = 2 * S5_P

    def body(h_ref, dh_ref, da_ref, db_ref, ds_ref, gda_ref, gdb_ref):
        dirs = ((False, slice(0, w)), (True, slice(w, 2 * w)))
        coef = [(da_ref[:, cols], db_ref[:, cols]) for _, cols in dirs]

        def step(i, carries):
            new = []
            for (rev, cols), (a, b), (g, ga, gb) in zip(dirs, coef, carries):
                k = i if rev else (nck - 1 - i)
                ds_ref[k, :, cols] = g
                h = h_ref[k, :, cols]
                new.append((dh_ref[k, :, cols] + a * g + _swap(b * g), ga + g * h, gb + g * _swap(h)))
            return tuple(new)

        z = jnp.zeros((rows, w), F32)
        res = lax.fori_loop(0, nck, step, ((z, z, z), (z, z, z)), unroll=2)
        for (_, cols), (_, ga, gb) in zip(dirs, res):
            gda_ref[:, cols] = ga
            gdb_ref[:, cols] = gb

    rt = min(CARRY_ROWS, rows)
    big, small = pl.BlockSpec((nck, rt, 2 * w), lambda i: (0, i, 0)), pl.BlockSpec((rt, 2 * w), lambda i: (i, 0))
    rows = rt
    return pl.pallas_call(
        body, name=name, grid=(hin.shape[1] // rt,), in_specs=[big, big, small, small], out_specs=[big, small, small],
        out_shape=[jax.ShapeDtypeStruct(hin.shape, F32), jax.ShapeDtypeStruct(da.shape, F32),
                   jax.ShapeDtypeStruct(da.shape, F32)],
        compiler_params=_params(("parallel",)))(hin, dh, da, db)


def s5_out(u, hin, tt, mt, *, name):
    g, r, _ = u.shape

    def body(u_ref, h_ref, t_ref, m_ref, o_ref):
        u_v, h_v = u_ref[...], h_ref[...]
        for half in range(S5_HALVES):
            cols = slice(half * LANES, (half + 1) * LANES)
            o_ref[:, cols] = _bd(u_v, t_ref[half], 1, 0) + _bd(h_v, m_ref[:, cols], 1, 0)

    return pl.pallas_call(
        body, name=name, grid=(g,),
        in_specs=[_gspec(r, S5_QC), _gspec(r, 4 * S5_P), _gspec(S5_HALVES, S5_QC, LANES), _gspec(4 * S5_P, S5_QC)],
        out_specs=_gspec(r, S5_QC), out_shape=jax.ShapeDtypeStruct((g, r, S5_QC), F32),
        compiler_params=_params(("parallel",)))(u, hin, tt, mt)


def s5_out_bwd(dy, u, hin, tt, mt, *, name):
    g, r, _ = u.shape

    def body(dy_ref, u_ref, h_ref, t_ref, m_ref, dh_ref, dt_ref, dm_ref, du_ref):
        dy_v, u_v = dy_ref[...], u_ref[...]
        dh_ref[...] = _bd(dy_v, m_ref[...], 1, 1)
        dm_ref[...] = _bd(h_ref[...], dy_v, 0, 0)
        du = None
        for half in range(S5_HALVES):
            dy_h = dy_v[:, half * LANES:(half + 1) * LANES]
            dt_ref[half] = _bd(u_v, dy_h, 0, 0)
            part = _bd(dy_h, t_ref[half], 1, 1)
            du = part if du is None else du + part
        du_ref[...] = du

    tspec = _gspec(S5_HALVES, S5_QC, LANES)
    return pl.pallas_call(
        body, name=name, grid=(g,),
        in_specs=[_gspec(r, S5_QC), _gspec(r, S5_QC), _gspec(r, 4 * S5_P), tspec, _gspec(4 * S5_P, S5_QC)],
        out_specs=[_gspec(r, 4 * S5_P), tspec, _gspec(4 * S5_P, S5_QC), _gspec(r, S5_QC)],
        out_shape=[jax.ShapeDtypeStruct((g, r, 4 * S5_P), F32), jax.ShapeDtypeStruct((g, S5_HALVES, S5_QC, LANES), F32),
                   jax.ShapeDtypeStruct((g, 4 * S5_P, S5_QC), F32), jax.ShapeDtypeStruct((g, r, S5_QC), F32)],
        compiler_params=_params(("parallel",)))(dy, u, hin, tt, mt)


def s5_state_in_bwd(ds, u, wt, du1, *, name):
    g, r, _ = u.shape

    def body(ds_ref, u_ref, w_ref, du1_ref, du_ref, dw_ref):
        ds_v = ds_ref[...]
        du_ref[...] = du1_ref[...] + _bd(ds_v, w_ref[...], 1, 1)
        dw_ref[...] = _bd(u_ref[...], ds_v, 0, 0)

    return pl.pallas_call(
        body, name=name, grid=(g,),
        in_specs=[_gspec(r, 4 * S5_P), _gspec(r, S5_QC), _gspec(S5_QC, 4 * S5_P), _gspec(r, S5_QC)],
        out_specs=[_gspec(r, S5_QC), _gspec(S5_QC, 4 * S5_P)],
        out_shape=[jax.ShapeDtypeStruct((g, r, S5_QC), F32), jax.ShapeDtypeStruct((g, S5_QC, 4 * S5_P), F32)],
        compiler_params=_params(("parallel",)))(ds, u, wt, du1)


def _s5_post(ypre, u, dvec, wv, wg, bv, bg, nw):
    g = _gelu(ypre + dvec * u)
    out = (dot_nn(g, wv) + bv) * jax.nn.sigmoid(dot_nn(g, wg) + bg)
    return (_rms(out, nw),)


def _ssd_post(y, z, nw):
    return (_rms(y * _silu(z), nw),)


def _to_chunks(u, bsz):
    nck = u.shape[0] // bsz // S5_Q
    v = u.reshape(bsz, nck, S5_Q, S5_G, S5_C)
    return jnp.transpose(v, (3, 0, 1, 2, 4)).reshape(S5_G, bsz * nck, S5_QC)


def _from_chunks(y, bsz):
    nck = y.shape[1] // bsz
    v = y.reshape(S5_G, bsz, nck, S5_Q, S5_C)
    return jnp.transpose(v, (1, 2, 3, 0, 4)).reshape(bsz * nck * S5_Q, S5_W)


def _to_carry(s, bsz):
    nck = s.shape[1] // bsz
    return jnp.transpose(s.reshape(S5_G, bsz, nck, -1), (2, 0, 1, 3)).reshape(nck, S5_G * bsz, -1)


def _from_carry(h, bsz):
    nck = h.shape[0]
    return jnp.transpose(h.reshape(nck, S5_G, bsz, -1), (1, 2, 0, 3)).reshape(S5_G, bsz * nck, -1)


def _block_diag(w):
    eye = jnp.eye(S5_G, dtype=w.dtype)
    return jnp.einsum('gcd,gh->gchd', w, eye).reshape(S5_W, S5_W)


def _diag_blocks(w):
    v = w.reshape(S5_G, S5_C, S5_G, S5_C)
    return v[jnp.arange(S5_G), :, jnp.arange(S5_G), :]


def _dt_rows(dt, bsz):
    seq = dt.shape[0] // bsz
    return jnp.transpose(dt.reshape(bsz, seq, 2, SGROUPS, HPG), (0, 3, 2, 4, 1)).reshape(bsz, SGROUPS, 2 * HPG, seq)


def _dt_from_rows(dr):
    bsz, _, _, seq = dr.shape
    return jnp.transpose(dr.reshape(bsz, SGROUPS, 2, HPG, seq), (0, 4, 2, 1, 3)).reshape(bsz * seq, 2 * HEADS)


def _head_params(f, b):
    return jnp.concatenate([f.reshape(SGROUPS, HPG), b.reshape(SGROUPS, HPG)], axis=1)[:, :, None]


def _head_grads(gr):
    v = gr.sum(0)[:, :, 0]
    return v[:, :HPG].reshape(HEADS), v[:, HPG:].reshape(HEADS)


def local_step(x, target, w):
    bsz, seq, d = x.shape
    t = bsz * seq
    x2, tgt2 = x.reshape(t, d), target.reshape(t, d)
    g = {}
    row = lambda v: v.reshape(1, -1)
    bf = lambda v: v.astype(BF16)

    w_in = _unshard(bf(w['w_in']), SHARDED['w_in'])
    cuts = [0, SSD_W, SSD_W + XBC_W, SSD_W + XBC_W + 2 * HEADS, w_in.shape[1]]
    w_in_parts = [w_in[:, a:b] for a, b in zip(cuts[:-1], cuts[1:])]
    norm_mix = row(w['norm_mix_w']) + w.get('token', 0.0)
    (hn,) = rowmap_fwd(lambda a, nw: (_rms(a, nw),), [x2], [norm_mix], [(d, BF16)], name="rms_mix")
    z, xbc, dt, u = [matmul_sum([hn], [p], tm=1024, name=f"in_proj_{i}") for i, p in enumerate(w_in_parts)]

    conv_w, conv_b = _unshard(w['ssd_conv_w'], SHARDED['ssd_conv_w']), row(w['ssd_conv_b'])
    act = ssd_conv_fwd(xbc, conv_w, conv_b, bsz=bsz, name="ssd_conv")
    dtr = _dt_rows(dt, bsz)
    prs = (_head_params(w['ssd_dt_bias_fwd'], w['ssd_dt_bias_bwd']),
           _head_params(w['ssd_a_log_fwd'], w['ssd_a_log_bwd']),
           _head_params(w['ssd_d'], jnp.zeros_like(w['ssd_d'])))
    act3 = act.reshape(bsz, seq, XBC_W)
    y_scan, ssd_states = ssd_scan_fwd(act3, dtr, prs, name="ssd_scan")
    y_scan = y_scan.reshape(t, SSD_W)
    ssd_nw = row(w['ssd_norm_w'])
    (y_ssd,) = rowmap_fwd(_ssd_post, [y_scan, z], [ssd_nw], [(SSD_W, BF16)], name="ssd_post")

    s5_names = ['s5_lambda_re_fwd', 's5_lambda_im_fwd', 's5_log_step_fwd', 's5_lambda_re_bwd', 's5_lambda_im_bwd',
                's5_log_step_bwd', 's5_b_re', 's5_b_im', 's5_c_re_fwd', 's5_c_im_fwd', 's5_c_re_bwd', 's5_c_im_bwd']
    (kt, wt, mt, da, db), s5_pull = jax.vjp(_s5_operators, *[w[n] for n in s5_names])
    tt_b, wt_b, mt_b = s5_toeplitz(kt, name="s5_toeplitz"), bf(wt), bf(mt)
    da_r, db_r = jnp.repeat(da, bsz, axis=0), jnp.repeat(db, bsz, axis=0)
    uc = _to_chunks(u, bsz)
    s_in = _to_carry(s5_state_in(uc, wt_b, name="s5_state_in"), bsz)
    hin_c = s5_carry_fwd(s_in, da_r, db_r, name="s5_carry")
    hin = _from_carry(hin_c, bsz)
    ypre = _from_chunks(s5_out(uc, hin, tt_b, mt_b, name="s5_out"), bsz)
    glu_w = w['s5_glu_w']
    s5_par = [row(w['s5_d']), _block_diag(glu_w[:, :, :S5_C]), _block_diag(glu_w[:, :, S5_C:]),
              row(w['s5_glu_b'][:, :S5_C]), row(w['s5_glu_b'][:, S5_C:]), row(w['s5_norm_w'])]
    (y_s5,) = rowmap_fwd(_s5_post, [ypre, u], s5_par, [(S5_W, BF16)], name="s5_post")

    if 'late' in w:
        w = {**w, **w['late'](y_s5)}
    w_out = bf(w['w_out']).reshape(SSD_W + S5_W, d)
    h1 = matmul_sum([y_ssd, y_s5], [w_out[:SSD_W], w_out[SSD_W:]], add=x2, name="out_proj")
    norm_ffn = row(w['norm_ffn_w'])
    (hn2,) = rowmap_fwd(lambda a, nw: (_rms(a, nw),), [h1], [norm_ffn], [(d, BF16)], name="rms_ffn")
    pad_c = FFN_PAD - FFN_BLK
    half = N_DEV // 2
    w_up3 = jnp.pad(bf(w['ffn_w_up']), ((0, 0), (0, 0), (0, pad_c)))
    w_down = jnp.pad(bf(w['ffn_w_down']).reshape(half, FFN_BLK, d), ((0, 0), (0, pad_c), (0, 0)))
    w_down = w_down.reshape(half * FFN_PAD, d)
    fconv_w = jnp.pad(w['ffn_conv_w'], ((0, 0), (0, 0), (0, pad_c)))
    fconv_w = jnp.transpose(fconv_w, (1, 0, 2)).reshape(FCONV, N_DEV * FFN_PAD)
    fconv_b = row(jnp.pad(w['ffn_conv_b'].reshape(N_DEV, FFN_BLK), ((0, 0), (0, pad_c))))
    up = matmul_cols(hn2, w_up3, name="ffn_up")
    fact = ffn_act_fwd(up, fconv_w, fconv_b, bsz=bsz, name="ffn_act")
    h2 = matmul_sum([fact], [w_down], add=h1, name="ffn_down")
    loss, dh2, g_nf = loss_head(h2, tgt2, row(w['norm_final_w']), name="loss_head")
    g['norm_final_w'] = g_nf.reshape(-1)

    dfact = matmul_sum([dh2], [w_down], nt=True, tm=1024, name="ffn_down_dx")
    g_down = matmul_tn(fact, dh2, name="ffn_down_dw").reshape(half, FFN_PAD, d)[:, :FFN_BLK]
    g['ffn_w_down'] = g_down.reshape(N_DEV, FFN_BLK // 2, d)
    dval, dgate, dwv, dwg, dbv, dbg = ffn_act_bwd(up, dfact, fconv_w, fconv_b, bsz=bsz, name="ffn_act_bwd")
    g_cw = jnp.concatenate([dwv, dwg], axis=1).reshape(FCONV, N_DEV, FFN_PAD)[:, :, :FFN_BLK]
    g['ffn_conv_w'] = jnp.transpose(g_cw, (1, 0, 2))
    g['ffn_conv_b'] = jnp.concatenate([dbv, dbg], axis=1).reshape(N_DEV, FFN_PAD)[:, :FFN_BLK].reshape(-1)
    windows = [(dval, FFN_PAD, p) for p in range(half)] + [(dgate, FFN_PAD, p) for p in range(half)]
    dhn2 = matmul_sum(windows, [(w_up3, p) for p in range(N_DEV)], nt=True, name="ffn_up_dx")
    g['ffn_w_up'] = jnp.concatenate([matmul_tn(hn2, dval, out_blocks=half, name="ffn_up_dw_val"),
                                     matmul_tn(hn2, dgate, out_blocks=half, name="ffn_up_dw_gate")],
                                    axis=0)[:, :, :FFN_BLK]
    send_early = w.get('on_grads')
    if send_early:
        norm_ffn = norm_ffn + send_early(g, ['ffn_w_up', 'ffn_w_down'])
    dh1, g_nffn = rowmap_bwd(lambda a, nw: (_rms(a, nw),), [h1], [norm_ffn], [dhn2], add=dh2, name="rms_ffn_bwd")
    g['norm_ffn_w'] = g_nffn.reshape(-1)

    dycat = matmul_sum([dh1], [w_out], nt=True, tm=1024, name="out_proj_dx")
    g['w_out'] = jnp.concatenate([matmul_tn(y_ssd, dh1, name="out_proj_dw_ssd"),
                                  matmul_tn(y_s5, dh1, name="out_proj_dw_s5")], axis=0).reshape(w['w_out'].shape)
    if send_early:
        ssd_nw = ssd_nw + send_early(g, ['w_out'])
    dy_scan, dz, g_snw = rowmap_bwd(_ssd_post, [y_scan, z], [ssd_nw], [(dycat, SSD_W, 0)], name="ssd_post_bwd")
    g['ssd_norm_w'] = g_snw.reshape(-1)
    dypre, du_a, g_d, g_wv, g_wg, g_bv, g_bg, g_s5nw = rowmap_bwd(
        _s5_post, [ypre, u], s5_par, [(dycat, S5_W, SSD_W // S5_W)], name="s5_post_bwd")
    g['s5_d'], g['s5_norm_w'] = g_d.reshape(-1), g_s5nw.reshape(-1)
    g['s5_glu_w'] = jnp.concatenate([_diag_blocks(g_wv), _diag_blocks(g_wg)], axis=-1)
    g['s5_glu_b'] = jnp.concatenate([g_bv.reshape(S5_G, S5_C), g_bg.reshape(S5_G, S5_C)], axis=-1)

    dyc = _to_chunks(dypre, bsz)
    dhin, dtt, dmt, du1 = s5_out_bwd(dyc, uc, hin, tt_b, mt_b, name="s5_out_bwd")
    ds_c, gda, gdb = s5_carry_bwd(hin_c, _to_carry(dhin, bsz), da_r, db_r, name="s5_carry_bwd")
    duc, dwt = s5_state_in_bwd(_from_carry(ds_c, bsz), uc, wt_b, du1, name="s5_state_in_bwd")
    du = du_a + _from_chunks(duc, bsz)
    fold = lambda v: v.reshape(S5_G, bsz, -1).sum(1)
    dkt = s5_toeplitz_bwd(dtt, name="s5_toeplitz_bwd")
    for n, gv in zip(s5_names, s5_pull((dkt, dwt, dmt, fold(gda), fold(gdb)))):
        g[n] = gv

    dxs, dbm, dcm, ddtr, gbr, gar, gdk = ssd_scan_bwd(
        act3, dtr, prs, ssd_states, dy_scan.reshape(bsz, seq, SSD_W), name="ssd_scan_bwd")
    g['ssd_dt_bias_fwd'], g['ssd_dt_bias_bwd'] = _head_grads(gbr)
    g['ssd_a_log_fwd'], g['ssd_a_log_bwd'] = _head_grads(gar)
    g['ssd_d'] = _head_grads(gdk)[0]
    dact = jnp.concatenate([dxs, dbm, dcm], axis=-1).reshape(t, XBC_W)
    dxbc, g_cw, g_cb = ssd_conv_bwd(xbc, dact, conv_w, conv_b, bsz=bsz, name="ssd_conv_bwd")
    g['ssd_conv_w'] = _shard_rows(g_cw, SHARDED['ssd_conv_w']).reshape(w['ssd_conv_w'].shape)
    g['ssd_conv_b'] = g_cb.reshape(-1)
    ddt = _dt_from_rows(ddtr)

    dparts = [dz, dxbc, ddt, du]
    g_in = jnp.concatenate([matmul_tn(hn, dp, name=f"in_proj_dw_{i}") for i, dp in enumerate(dparts)], axis=1)
    g['w_in'] = _shard_rows(g_in, SHARDED['w_in']).reshape(w['w_in'].shape)
    if send_early:
        dparts[2] = ddt + send_early(g, ['w_in'], loss=loss)
    dhn = matmul_sum(dparts, w_in_parts, nt=True, name="in_proj_dx")
    dx, g_nmix = rowmap_bwd(lambda a, nw: (_rms(a, nw),), [x2], [norm_mix], [dhn], add=dh1, name="rms_mix_bwd")
    g['norm_mix_w'] = g_nmix.reshape(-1)
    return loss, dx.reshape(bsz, seq, d), g


ANY = pl.BlockSpec(memory_space=pl.ANY)


def all_gather(shards, *, name):
    n = len(shards)

    def body(*refs):
        x_refs, out_refs = refs[:n], refs[n:2 * n]
        send_sems, recv_sems, local_sems = refs[2 * n:]
        x, y, c = lax.axis_index("x"), lax.axis_index("y"), lax.axis_index("c")
        me, sibling = (x, y, c), (x, y, 1 - c)
        chips = [(1 - x, y), (x, 1 - y), (1 - x, 1 - y)]

        def copy(k, j, block, to, own=False):
            dst = out_refs[j].at[4 * block[0] + 2 * block[1] + block[2]]
            return pltpu.make_async_remote_copy(
                src_ref=x_refs[j] if own else dst, dst_ref=dst,
                send_sem=send_sems.at[k, j], recv_sem=recv_sems.at[k, j], device_id=to, device_id_type=MESH)

        mine = [pltpu.make_async_copy(x_refs[j], out_refs[j].at[4 * x + 2 * y + c], local_sems.at[j]) for j in range(n)]
        first = [copy(0, j, me, sibling, own=True) for j in range(n)]
        first += [copy(1 + i, j, me, (*chip, c), own=True) for i, chip in enumerate(chips) for j in range(n)]
        for cp in mine + first:
            cp.start()
        passed = []
        for i, chip in enumerate(chips):
            for j in range(n):
                copy(1 + i, j, (*chip, c), me).wait_recv()
                passed.append(copy(4 + i, j, (*chip, c), sibling))
                passed[-1].start()
        for j in range(n):
            copy(0, j, sibling, me).wait_recv()
        for i, chip in enumerate(chips):
            for j in range(n):
                copy(4 + i, j, (*chip, 1 - c), me).wait_recv()
        for cp in first + passed:
            cp.wait_send()
        for cp in mine:
            cp.wait()

    return pl.pallas_call(
        body, name=name, out_shape=[jax.ShapeDtypeStruct((N_DEV,) + s.shape, s.dtype) for s in shards],
        in_specs=[ANY] * n, out_specs=[ANY] * n,
        scratch_shapes=[pltpu.SemaphoreType.DMA((7, n)), pltpu.SemaphoreType.DMA((7, n)),
                        pltpu.SemaphoreType.DMA((n,))],
    )(*shards)


def exchange(sends, *, name):
    n = len(sends)

    def body(*refs):
        send_refs, recv_refs = refs[:n], refs[n:2 * n]
        send_sems, recv_sems, local_sems = refs[2 * n:]
        x, y, c = lax.axis_index("x"), lax.axis_index("y"), lax.axis_index("c")
        me = 4 * x + 2 * y + c
        local = [pltpu.make_async_copy(send_refs[j].at[me], recv_refs[j].at[me], local_sems.at[j]) for j in range(n)]
        for cp in local:
            cp.start()
        copies = []
        for k in range(1, N_DEV):
            px = (1 - x) if k & 4 else x
            py = (1 - y) if k & 2 else y
            pc = (1 - c) if k & 1 else c
            for j in range(n):
                copies.append(pltpu.make_async_remote_copy(
                    src_ref=send_refs[j].at[4 * px + 2 * py + pc], dst_ref=recv_refs[j].at[me],
                    send_sem=send_sems.at[k - 1, j], recv_sem=recv_sems.at[k - 1, j],
                    device_id=(px, py, pc), device_id_type=MESH))
        for cp in copies:
            cp.start()
        for cp in copies:
            cp.wait()
        for cp in local:
            cp.wait()

    return pl.pallas_call(
        body, name=name, out_shape=[jax.ShapeDtypeStruct(s.shape, s.dtype) for s in sends],
        in_specs=[ANY] * n, out_specs=[ANY] * n,
        scratch_shapes=[pltpu.SemaphoreType.DMA((N_DEV - 1, n)), pltpu.SemaphoreType.DMA((N_DEV - 1, n)),
                        pltpu.SemaphoreType.DMA((n,))],
    )(*sends)


HBM_SPEC = pl.BlockSpec(memory_space=pltpu.HBM)
SEM_SPEC = pl.BlockSpec(memory_space=pltpu.SEMAPHORE)
SPLIT_PARAMS = pltpu.CompilerParams(has_side_effects=pltpu.SideEffectType.DATAFLOW_SIDE_EFFECTING)


def _peer_copies(src_refs, land_refs, send_sems, recv_sems, indexed):
    x, y, c = lax.axis_index("x"), lax.axis_index("y"), lax.axis_index("c")
    me = 4 * x + 2 * y + c
    copies = []
    for k in range(1, N_DEV):
        px = (1 - x) if k & 4 else x
        py = (1 - y) if k & 2 else y
        pc = (1 - c) if k & 1 else c
        for j, (src, land) in enumerate(zip(src_refs, land_refs)):
            sem = (k - 1) * len(src_refs) + j
            copies.append(pltpu.make_async_remote_copy(
                src_ref=src.at[4 * px + 2 * py + pc] if indexed else src, dst_ref=land.at[me],
                send_sem=send_sems.at[sem], recv_sem=recv_sems.at[sem],
                device_id=(px, py, pc), device_id_type=MESH))
    return copies


def scatter_start(srcs, *, name, indexed):
    n = len(srcs)
    lands = [lax.empty(s.shape if indexed else (N_DEV,) + s.shape, s.dtype) for s in srcs]

    def body(*refs):
        send_sems, recv_sems = refs[2 * n], refs[2 * n + 1]
        for cp in _peer_copies(refs[:n], refs[n:2 * n], send_sems, recv_sems, indexed):
            cp.start()
        refs[-1][...] = jnp.zeros_like(refs[-1])

    hbm = lambda a: pltpu.HBM(a.shape, a.dtype)
    sems = pltpu.SemaphoreType.DMA(((N_DEV - 1) * n,))
    res = pl.pallas_call(
        body, name=name,
        out_shape=(sems, sems, *[hbm(a) for a in srcs + lands], jax.ShapeDtypeStruct((8, LANES), F32)),
        in_specs=[HBM_SPEC] * (2 * n),
        out_specs=(SEM_SPEC, SEM_SPEC, *[HBM_SPEC] * (2 * n), pl.BlockSpec(memory_space=pltpu.VMEM)),
        input_output_aliases={i: 2 + i for i in range(2 * n)}, compiler_params=SPLIT_PARAMS,
    )(*[pltpu.with_memory_space_constraint(a, pltpu.HBM) for a in srcs + lands])
    return res[0], res[1], list(res[2:2 + n]), list(res[2 + n:2 + 2 * n]), res[-1]


def scatter_wait(send_sems, recv_sems, srcs, lands, after, *, name, indexed):
    n = len(srcs)

    def body(*refs):
        for cp in _peer_copies(refs[:n], refs[n:2 * n], refs[2 * n], refs[2 * n + 1], indexed):
            cp.wait_send()
            cp.wait_recv()

    hbm = lambda a: pltpu.HBM(a.shape, a.dtype)
    res = pl.pallas_call(
        body, name=name, out_shape=tuple(hbm(a) for a in srcs + lands),
        in_specs=[HBM_SPEC] * (2 * n) + [SEM_SPEC, SEM_SPEC, ANY], out_specs=tuple([HBM_SPEC] * (2 * n)),
        input_output_aliases={i: i for i in range(2 * n)}, compiler_params=SPLIT_PARAMS,
    )(*srcs, *lands, send_sems, recv_sems, after)
    return list(res[:n]), list(res[n:])


def _adam_rows(r, c):
    fits = [t for t in range(8, r + 1, 8) if r % t == 0 and N_DEV * t * c * 4 <= 6 * 2 ** 20]
    return max(fits) if fits else r


def adamw(recv, w, m, v, *, name):
    _, r, n = recv.shape
    tr = _adam_rows(r, n)

    def body(r_ref, w_ref, m_ref, v_ref, g_ref, d_ref, nm_ref, nv_ref):
        g = r_ref[0].astype(F32)
        for s in range(1, N_DEV):
            g = g + r_ref[s].astype(F32)
        m_new = ADAM_B1 * m_ref[...] + (1.0 - ADAM_B1) * g
        v_new = ADAM_B2 * v_ref[...] + (1.0 - ADAM_B2) * jnp.square(g)
        m_hat = m_new / (1.0 - ADAM_B1 ** ADAM_STEP)
        v_hat = v_new / (1.0 - ADAM_B2 ** ADAM_STEP)
        g_ref[...] = g
        d_ref[...] = -ADAM_LR * (m_hat / (jnp.sqrt(v_hat) + ADAM_EPS) + ADAM_WD * w_ref[...])
        nm_ref[...] = m_new
        nv_ref[...] = v_new

    blk = pl.BlockSpec((tr, n), lambda i: (i, 0))
    return pl.pallas_call(
        body, name=name, grid=(r // tr,), in_specs=[pl.BlockSpec((N_DEV, tr, n), lambda i: (0, i, 0)), blk, blk, blk],
        out_specs=[blk] * 4, out_shape=[jax.ShapeDtypeStruct((r, n), F32)] * 4,
        compiler_params=_params(("parallel",)))(recv, w, m, v)


def _shard_rows(full, axis):
    if axis == 0:
        return full.reshape(N_DEV, -1)
    r, c = full.shape
    return jnp.transpose(full.reshape(r, N_DEV, c // N_DEV), (1, 0, 2)).reshape(N_DEV, -1)


def _unshard(blocks, axis):
    if axis == 0:
        return blocks.reshape(-1, blocks.shape[-1])
    return jnp.transpose(blocks, (1, 0, 2)).reshape(blocks.shape[1], -1)


def kernel(x, norm_mix_w, w_in, ssd_conv_w, ssd_conv_b, ssd_dt_bias_fwd, ssd_dt_bias_bwd, ssd_a_log_fwd, ssd_a_log_bwd, ssd_d, ssd_norm_w, s5_lambda_re_fwd, s5_lambda_im_fwd, s5_log_step_fwd, s5_lambda_re_bwd, s5_lambda_im_bwd, s5_log_step_bwd, s5_b_re, s5_b_im, s5_c_re_fwd, s5_c_im_fwd, s5_c_re_bwd, s5_c_im_bwd, s5_d, s5_glu_w, s5_glu_b, s5_norm_w, w_out, norm_ffn_w, ffn_w_up, ffn_conv_w, ffn_conv_b, ffn_w_down, norm_final_w, loss_target, m_norm_mix_w, m_w_in, m_ssd_conv_w, m_ssd_conv_b, m_ssd_dt_bias_fwd, m_ssd_dt_bias_bwd, m_ssd_a_log_fwd, m_ssd_a_log_bwd, m_ssd_d, m_ssd_norm_w, m_s5_lambda_re_fwd, m_s5_lambda_im_fwd, m_s5_log_step_fwd, m_s5_lambda_re_bwd, m_s5_lambda_im_bwd, m_s5_log_step_bwd, m_s5_b_re, m_s5_b_im, m_s5_c_re_fwd, m_s5_c_im_fwd, m_s5_c_re_bwd, m_s5_c_im_bwd, m_s5_d, m_s5_glu_w, m_s5_glu_b, m_s5_norm_w, m_w_out, m_norm_ffn_w, m_ffn_w_up, m_ffn_conv_w, m_ffn_conv_b, m_ffn_w_down, m_norm_final_w, v_norm_mix_w, v_w_in, v_ssd_conv_w, v_ssd_conv_b, v_ssd_dt_bias_fwd, v_ssd_dt_bias_bwd, v_ssd_a_log_fwd, v_ssd_a_log_bwd, v_ssd_d, v_ssd_norm_w, v_s5_lambda_re_fwd, v_s5_lambda_im_fwd, v_s5_log_step_fwd, v_s5_lambda_re_bwd, v_s5_lambda_im_bwd, v_s5_log_step_bwd, v_s5_b_re, v_s5_b_im, v_s5_c_re_fwd, v_s5_c_im_fwd, v_s5_c_re_bwd, v_s5_c_im_bwd, v_s5_d, v_s5_glu_w, v_s5_glu_b, v_s5_norm_w, v_w_out, v_norm_ffn_w, v_ffn_w_up, v_ffn_conv_w, v_ffn_conv_b, v_ffn_w_down, v_norm_final_w):
    args = dict(locals())
    strip = lambda n, v: v if n == 'norm_final_w' else v[0]
    w = {n: strip(n, args[n]) for n in WEIGHTS}

    mats = ['w_in', 'w_out', 'ffn_w_up', 'ffn_w_down']
    convs = ['ssd_conv_w', 'ffn_conv_w']
    shard = lambda n: w[n].astype(BF16) if n in mats else w[n]
    early, late = ['w_in', 'ssd_conv_w'], ['w_out', 'ffn_w_up', 'ffn_w_down', 'ffn_conv_w']
    full = dict(w)
    full.update(zip(early, all_gather([shard(n) for n in early], name="weight_all_gather")))
    ssem, rsem, src_thru, land_thru, token = scatter_start([shard(n) for n in late], name="weight_gather_start",
                                                           indexed=False)
    me = 4 * lax.axis_index("x") + 2 * lax.axis_index("y") + lax.axis_index("c")

    def late_weights(after):
        own, landed = scatter_wait(ssem, rsem, src_thru, land_thru, after, name="weight_gather_wait", indexed=False)
        return {n: lax.dynamic_update_index_in_dim(l, o, me, 0) for n, o, l in zip(late, own, landed)}

    full['late'], full['token'] = late_weights, token[:1, :1]

    pending = []
    last = 'norm_mix_w'
    small = convs + [n for n in WEIGHTS if n not in SHARDED and n != last]
    total = sum(w[n].size for n in small) + 1
    nrow = -(-total // (PACK_ROWS * LANES)) * PACK_ROWS

    def send_early(grads, names, loss=None):
        srcs = [grads[n].astype(BF16) for n in names]
        if loss is not None:
            pieces = [grads[n].reshape(N_DEV, -1) if n in SHARDED else
                      jnp.broadcast_to(grads[n].reshape(1, -1), (N_DEV, grads[n].size)) for n in small]
            pieces += [jnp.broadcast_to(loss.reshape(1, 1), (N_DEV, 1)), jnp.zeros((N_DEV, nrow * LANES - total), F32)]
            srcs.append(jnp.concatenate(pieces, axis=1).reshape(N_DEV, nrow, LANES))
            names = names + ['small']
        started = scatter_start(srcs, name="grad_start_" + names[0], indexed=True)
        pending.append((names,) + started[:4])
        return started[4][:1, :1]

    full['on_grads'] = send_early
    loss, grad_x, g = local_step(x, loss_target, full)

    last_send = jnp.broadcast_to(g[last].reshape(1, -1, LANES), (N_DEV, g[last].size // LANES, LANES))
    recv = {last: exchange([last_send], name="grad_exchange")[0]}
    for names, ssem, rsem, src_thru, land_thru in pending:
        own, landed = scatter_wait(ssem, rsem, src_thru, land_thru, recv[last], name="grad_wait_" + names[0],
                                   indexed=True)
        for n, o, l in zip(names, own, landed):
            recv[n] = lax.dynamic_update_index_in_dim(l, lax.dynamic_index_in_dim(o, me, 0, keepdims=False), me, 0)

    outs = [{}, {}, {}, {}]
    for n in mats + [last]:
        shape = recv[n].shape[1:]
        res = adamw(recv[n], *[strip(n, args[p + n]).reshape(shape) for p in ('', 'm_', 'v_')], name="adamw_" + n)
        for o, p in zip(outs, res):
            o[n] = p.reshape(args[n].shape)

    def pack(prefix):
        vals = [strip(n, args[prefix + n]).reshape(-1) for n in small]
        return jnp.pad(jnp.concatenate(vals), (0, nrow * LANES - total + 1)).reshape(nrow, LANES)

    packed = adamw(recv['small'], pack(''), pack('m_'), pack('v_'), name="adamw_small")
    packed = [p.reshape(-1) for p in packed]
    off = 0
    for n in small:
        size = w[n].size
        for o, p in zip(outs, packed):
            o[n] = p[off:off + size].reshape(args[n].shape)
        off += size
    loss_out = packed[0][off].reshape(())
    return (loss_out, grad_x, *[o[n] for o in outs for n in WEIGHTS])
```

```python
import functools

import jax
import jax.numpy as jnp
from jax import lax
from jax.experimental import pallas as pl
from jax.experimental.pallas import tpu as pltpu

F32, BF16 = jnp.float32, jnp.bfloat16
N_DEV = 8
D_MODEL = 1024
SSD_W, HEADS, HDIM, SGROUPS, HPG, NSTATE, SCONV, QC = 1024, 16, 64, 4, 4, 128, 5, 128
XBC_W = SSD_W + 2 * SGROUPS * NSTATE
S5_W, S5_G, S5_C, S5_P, S5_Q = 512, 32, 16, 64, 16
S5_QC = S5_Q * S5_C
CARRY_ROWS = 32
DFF, FCONV = 2816, 3
FFN_BLK, FFN_PAD = 704, 768
EPS = 1e-6
ADAM_LR, ADAM_B1, ADAM_B2, ADAM_EPS, ADAM_WD, ADAM_STEP = 0.001, 0.9, 0.999, 1e-08, 0.01, 10
LANES = 128
MESH = pl.DeviceIdType.MESH

WEIGHTS = ['norm_mix_w', 'w_in', 'ssd_conv_w', 'ssd_conv_b', 'ssd_dt_bias_fwd', 'ssd_dt_bias_bwd', 'ssd_a_log_fwd',
           'ssd_a_log_bwd', 'ssd_d', 'ssd_norm_w', 's5_lambda_re_fwd', 's5_lambda_im_fwd', 's5_log_step_fwd',
           's5_lambda_re_bwd', 's5_lambda_im_bwd', 's5_log_step_bwd', 's5_b_re', 's5_b_im', 's5_c_re_fwd', 's5_c_im_fwd',
           's5_c_re_bwd', 's5_c_im_bwd', 's5_d', 's5_glu_w', 's5_glu_b', 's5_norm_w', 'w_out', 'norm_ffn_w', 'ffn_w_up',
           'ffn_conv_w', 'ffn_conv_b', 'ffn_w_down', 'norm_final_w']
SHARDED = {'w_in': 1, 'ssd_conv_w': 1, 'w_out': 0, 'ffn_w_up': 1, 'ffn_conv_w': 1, 'ffn_w_down': 0}
FULL_SHAPE = {'w_in': (1024, 3616), 'ssd_conv_w': (5, 2048), 'w_out': (1536, 1024), 'ffn_w_up': (1024, 5632),
              'ffn_conv_w': (3, 5632), 'ffn_w_down': (2816, 1024)}
PACK_ROWS = 512


def _pick(n, cap=1536):
    if n <= cap:
        return n
    return max(t for t in range(LANES, cap + 1, LANES) if n % t == 0)


def _params(sem):
    return pltpu.CompilerParams(dimension_semantics=sem)


def _bd(a, b, ca, cb):
    return lax.dot_general(a.astype(BF16), b.astype(BF16), (((ca,), (cb,)), ((), ())), preferred_element_type=F32)


@jax.custom_vjp
def dot_nn(a, b):
    return _bd(a, b, 1, 0)


dot_nn.defvjp(lambda a, b: (_bd(a, b, 1, 0), (a, b)),
              lambda r, g: (_bd(g, r[1], 1, 1).astype(r[0].dtype), _bd(r[0], g, 0, 0).astype(r[1].dtype)))


@jax.custom_vjp
def dot_nt(a, b):
    return _bd(a, b, 1, 1)


dot_nt.defvjp(lambda a, b: (_bd(a, b, 1, 1), (a, b)),
              lambda r, g: (_bd(g, r[1], 1, 0).astype(r[0].dtype), _bd(g, r[0], 0, 0).astype(r[1].dtype)))


@jax.custom_vjp
def dot_tn(a, b):
    return _bd(a, b, 0, 0)


dot_tn.defvjp(lambda a, b: (_bd(a, b, 0, 0), (a, b)),
              lambda r, g: (_bd(r[1], g, 1, 1).astype(r[0].dtype), _bd(r[0], g, 1, 0).astype(r[1].dtype)))


def _split3(x):
    hi = x.astype(BF16)
    r = x - hi.astype(F32)
    mid = r.astype(BF16)
    lo = (r - mid.astype(F32)).astype(BF16)
    return hi, mid, lo


def _cum_matrix(q, upper):
    ri = lax.broadcasted_iota(jnp.int32, (q, q), 0)
    ci = lax.broadcasted_iota(jnp.int32, (q, q), 1)
    return jnp.where((ci >= ri) if upper else (ci <= ri), 1.0, 0.0).astype(BF16)


def _exact_right(x, mat):
    return sum(jnp.dot(p, mat, preferred_element_type=F32) for p in _split3(x))


@functools.partial(jax.custom_vjp, nondiff_argnums=(1,))
def cum_row(x, rev):
    return _exact_right(x, _cum_matrix(x.shape[1], not rev))


cum_row.defvjp(lambda x, rev: (cum_row(x, rev), None),
               lambda rev, _, g: (_exact_right(g, _cum_matrix(g.shape[1], rev)),))


def _softplus(x):
    return jnp.maximum(x, 0.0) + jnp.log(1.0 + jnp.exp(-jnp.abs(x)))


def _silu(x):
    return x * jax.nn.sigmoid(x)


def _gelu(x):
    return 0.5 * x * (1.0 + jnp.tanh(0.7978845608028654 * (x + 0.044715 * (x * x * x))))


def _rms(x, w):
    xf = x.astype(F32)
    return xf * lax.rsqrt(jnp.mean(xf * xf, axis=-1, keepdims=True) + EPS) * w


def matmul_sum(a_list, b_list, *, name, out_dtype=F32, add=None, tm=512, nt=False):
    a_arrs = [a[0] if isinstance(a, tuple) else a for a in a_list]
    b_arrs = [b[0] if isinstance(b, tuple) else b for b in b_list]
    m, n = a_arrs[0].shape[0], b_arrs[0].shape[-2 if nt else -1]
    tm, tn, k = min(tm, m), _pick(n), len(a_list)

    def body(*refs):
        acc = None
        for a_ref, b_ref in zip(refs[:k], refs[k:2 * k]):
            p = _bd(a_ref[...], b_ref[...], 1, 1 if nt else 0)
            acc = p if acc is None else acc + p
        if add is not None:
            acc = acc + refs[2 * k][...]
        refs[-1][...] = acc.astype(out_dtype)

    def a_spec(a):
        if isinstance(a, tuple):
            return pl.BlockSpec((tm, a[1]), lambda i, j, blk=a[2]: (i, blk))
        return pl.BlockSpec((tm, a.shape[1]), lambda i, j: (i, 0))

    def b_spec(b):
        arr, p = b if isinstance(b, tuple) else (b, None)
        kk = arr.shape[-1 if nt else -2]
        shape, idx = ((tn, kk), lambda j: (j, 0)) if nt else ((kk, tn), lambda j: (0, j))
        if p is None:
            return pl.BlockSpec(shape, lambda i, j: idx(j))
        return pl.BlockSpec((None,) + shape, lambda i, j, p=p: (p,) + idx(j))

    in_specs = [a_spec(a) for a in a_list] + [b_spec(b) for b in b_list]
    args = a_arrs + b_arrs
    if add is not None:
        in_specs.append(pl.BlockSpec((tm, tn), lambda i, j: (i, j)))
        args.append(add)
    return pl.pallas_call(
        body, name=name, grid=(m // tm, n // tn), in_specs=in_specs,
        out_specs=pl.BlockSpec((tm, tn), lambda i, j: (i, j)),
        out_shape=jax.ShapeDtypeStruct((m, n), out_dtype),
        compiler_params=_params(("parallel", "parallel")))(*args)


def matmul_cols(a, b3, *, name, out_dtype=F32, tm=1024):
    m, kk = a.shape
    p, _, nb = b3.shape
    tm, tn = min(tm, m), _pick(nb, 768)
    per = nb // tn

    def body(a_ref, b_ref, o_ref):
        o_ref[...] = _bd(a_ref[...], b_ref[...], 1, 0).astype(out_dtype)

    return pl.pallas_call(
        body, name=name, grid=(m // tm, p * per),
        in_specs=[pl.BlockSpec((tm, kk), lambda i, j: (i, 0)),
                  pl.BlockSpec((None, kk, tn), lambda i, j: (j // per, 0, j % per))],
        out_specs=pl.BlockSpec((tm, tn), lambda i, j: (i, j)),
        out_shape=jax.ShapeDtypeStruct((m, p * nb), out_dtype),
        compiler_params=_params(("parallel", "parallel")))(a, b3)


def matmul_tn(a, b, *, name, tm=1024, out_blocks=None):
    m, k = a.shape
    n = b.shape[1]
    nb = n // (out_blocks or 1)
    tm, tk, tn = min(tm, m), _pick(k), _pick(nb, 768 if out_blocks else 1536)
    per = nb // tn

    def body(a_ref, b_ref, o_ref):
        @pl.when(pl.program_id(2) == 0)
        def _():
            o_ref[...] = jnp.zeros_like(o_ref)

        o_ref[...] += _bd(a_ref[...], b_ref[...], 0, 0)

    if out_blocks:
        out_spec = pl.BlockSpec((None, tk, tn), lambda i, j, t: (j // per, i, j % per))
        out_shape = jax.ShapeDtypeStruct((out_blocks, k, nb), F32)
    else:
        out_spec = pl.BlockSpec((tk, tn), lambda i, j, t: (i, j))
        out_shape = jax.ShapeDtypeStruct((k, n), F32)
    return pl.pallas_call(
        body, name=name, grid=(k // tk, n // tn, m // tm),
        in_specs=[pl.BlockSpec((tm, tk), lambda i, j, t: (t, i)), pl.BlockSpec((tm, tn), lambda i, j, t: (t, j))],
        out_specs=out_spec, out_shape=out_shape,
        compiler_params=_params(("parallel", "parallel", "arbitrary")))(a, b)


def _row_spec(r, tm):
    if isinstance(r, tuple):
        arr, width, blk = r
        return arr, pl.BlockSpec((tm, width), lambda i, blk=blk: (i, blk))
    return r, pl.BlockSpec((tm, r.shape[1]), lambda i: (i, 0))


def _full_spec(p):
    return pl.BlockSpec(p.shape, lambda i: (0,) * p.ndim)


def rowmap_fwd(fn, rows, params, outs, *, name, tm=256):
    pairs = [_row_spec(r, tm) for r in rows]
    m = pairs[0][0].shape[0]
    tm = min(tm, m)
    pairs = [_row_spec(r, tm) for r in rows]
    nr, npar = len(rows), len(params)

    def body(*refs):
        res = fn(*[r[...] for r in refs[:nr + npar]])
        for o_ref, v in zip(refs[nr + npar:], res):
            o_ref[...] = v.astype(o_ref.dtype)

    return pl.pallas_call(
        body, name=name, grid=(m // tm,),
        in_specs=[s for _, s in pairs] + [_full_spec(p) for p in params],
        out_specs=[pl.BlockSpec((tm, c), lambda i: (i, 0)) for c, _ in outs],
        out_shape=[jax.ShapeDtypeStruct((m, c), dt) for c, dt in outs],
        compiler_params=_params(("parallel",)))(*[a for a, _ in pairs], *params)


def rowmap_bwd(fn, rows, params, cts, *, name, row_dtypes=None, add=None, tm=256):
    m = _row_spec(rows[0], tm)[0].shape[0]
    tm = min(tm, m)
    rp = [_row_spec(r, tm) for r in rows]
    cp = [_row_spec(c, tm) for c in cts]
    nr, npar, nc = len(rows), len(params), len(cts)
    row_dtypes = row_dtypes or [F32] * nr
    widths = [s.block_shape[1] for _, s in rp]

    def body(*refs):
        ins = [r[...] for r in refs[:nr + npar]]
        ins = [v.astype(F32) for v in ins]
        ct = tuple(r[...].astype(F32) for r in refs[nr + npar:nr + npar + nc])
        base = nr + npar + nc
        extra = None
        if add is not None:
            extra = refs[base][...]
            base += 1
        _, pull = jax.vjp(fn, *ins)
        grads = pull(ct)
        for j in range(nr):
            g = grads[j]
            if j == 0 and extra is not None:
                g = g + extra
            refs[base + j][...] = g.astype(refs[base + j].dtype)

        @pl.when(pl.program_id(0) == 0)
        def _():
            for j in range(npar):
                refs[base + nr + j][...] = jnp.zeros_like(refs[base + nr + j])

        for j in range(npar):
            refs[base + nr + j][...] += grads[nr + j]

    in_specs = [s for _, s in rp] + [_full_spec(p) for p in params] + [s for _, s in cp]
    args = [a for a, _ in rp] + list(params) + [a for a, _ in cp]
    if add is not None:
        in_specs.append(pl.BlockSpec((tm, widths[0]), lambda i: (i, 0)))
        args.append(add)
    out_specs = [pl.BlockSpec((tm, w), lambda i: (i, 0)) for w in widths] + [_full_spec(p) for p in params]
    out_shape = [jax.ShapeDtypeStruct((m, w), dt) for w, dt in zip(widths, row_dtypes)]
    out_shape += [jax.ShapeDtypeStruct(p.shape, F32) for p in params]
    return pl.pallas_call(
        body, name=name, grid=(m // tm,), in_specs=in_specs, out_specs=out_specs, out_shape=out_shape,
        compiler_params=_params(("arbitrary",)))(*args)


def loss_head(h, target, w, *, name, tm=256):
    m, d = h.shape
    tm = min(tm, m)

    def body(h_ref, t_ref, w_ref, loss_ref, dh_ref, dw_ref):
        y, pull = jax.vjp(_rms, h_ref[...], w_ref[...])
        err = y - t_ref[...]
        dh, dw = pull(err * (1.0 / d))

        @pl.when(pl.program_id(0) == 0)
        def _():
            loss_ref[...] = jnp.zeros_like(loss_ref)
            dw_ref[...] = jnp.zeros_like(dw_ref)

        loss_ref[...] += (0.5 / d) * jnp.sum(err * err, keepdims=True)
        dw_ref[...] += dw
        dh_ref[...] = dh

    row = pl.BlockSpec((tm, d), lambda i: (i, 0))
    return pl.pallas_call(
        body, name=name, grid=(m // tm,), in_specs=[row, row, _full_spec(w)],
        out_specs=[pl.BlockSpec((1, 1), lambda i: (0, 0)), row, _full_spec(w)],
        out_shape=[jax.ShapeDtypeStruct((1, 1), F32), jax.ShapeDtypeStruct((m, d), F32),
                   jax.ShapeDtypeStruct(w.shape, F32)],
        compiler_params=_params(("arbitrary",)))(h, target, w)


def _shift(x, s):
    if s == 0:
        return x
    n = x.shape[0]
    t = lax.broadcasted_iota(jnp.int32, x.shape, 0)
    rolled = pltpu.roll(x, (-s) % n, 0)
    return jnp.where((t + s >= 0) & (t + s < n), rolled, 0.0)


def _conv(x, w, b):
    k = w.shape[0]
    acc = b + w[k // 2:k // 2 + 1, :] * x
    for j in range(k):
        if j != k // 2:
            acc = acc + w[j:j + 1, :] * _shift(x, j - k // 2)
    return acc


def _conv_bwd(x, dc, w):
    k = w.shape[0]
    dx = None
    dws = []
    for j in range(k):
        s = j - k // 2
        term = w[j:j + 1, :] * _shift(dc, -s)
        dx = term if dx is None else dx + term
        dws.append(jnp.sum(dc * _shift(x, s), axis=0, keepdims=True))
    return dx, jnp.concatenate(dws, axis=0), jnp.sum(dc, axis=0, keepdims=True)


def _dsilu(c):
    s = jax.nn.sigmoid(c)
    return s * (1.0 + c * (1.0 - s))


def ssd_conv_fwd(xbc, w, b, *, bsz, name):
    t, c = xbc.shape
    seq, ct = t // bsz, 256

    def body(x_ref, w_ref, b_ref, o_ref):
        o_ref[...] = _silu(_conv(x_ref[...], w_ref[...], b_ref[...]))

    return pl.pallas_call(
        body, name=name, grid=(c // ct, bsz),
        in_specs=[pl.BlockSpec((seq, ct), lambda j, i: (i, j)), pl.BlockSpec((w.shape[0], ct), lambda j, i: (0, j)),
                  pl.BlockSpec((1, ct), lambda j, i: (0, j))],
        out_specs=pl.BlockSpec((seq, ct), lambda j, i: (i, j)),
        out_shape=jax.ShapeDtypeStruct((t, c), F32),
        compiler_params=_params(("parallel", "parallel")))(xbc, w, b)


def ssd_conv_bwd(xbc, dact, w, b, *, bsz, name):
    t, c = xbc.shape
    seq, ct, k = t // bsz, 256, w.shape[0]

    def body(x_ref, g_ref, w_ref, b_ref, dx_ref, dw_ref, db_ref):
        x, wv = x_ref[...], w_ref[...]
        dc = g_ref[...] * _dsilu(_conv(x, wv, b_ref[...]))
        dx, dw, db = _conv_bwd(x, dc, wv)
        dx_ref[...] = dx

        @pl.when(pl.program_id(1) == 0)
        def _():
            dw_ref[...] = jnp.zeros_like(dw_ref)
            db_ref[...] = jnp.zeros_like(db_ref)

        dw_ref[...] += dw
        db_ref[...] += db

    blk = pl.BlockSpec((seq, ct), lambda j, i: (i, j))
    wspec, bspec = pl.BlockSpec((k, ct), lambda j, i: (0, j)), pl.BlockSpec((1, ct), lambda j, i: (0, j))
    return pl.pallas_call(
        body, name=name, grid=(c // ct, bsz), in_specs=[blk, blk, wspec, bspec], out_specs=[blk, wspec, bspec],
        out_shape=[jax.ShapeDtypeStruct((t, c), F32), jax.ShapeDtypeStruct((k, c), F32),
                   jax.ShapeDtypeStruct((1, c), F32)],
        compiler_params=_params(("parallel", "arbitrary")))(xbc, dact, w, b)


def _ffn_specs(seq, ct, k, nblk):
    val = pl.BlockSpec((seq, ct), lambda j, i: (i, j))
    gate = pl.BlockSpec((seq, ct), lambda j, i: (i, nblk + j))
    wv, wg = pl.BlockSpec((k, ct), lambda j, i: (0, j)), pl.BlockSpec((k, ct), lambda j, i: (0, nblk + j))
    bv, bg = pl.BlockSpec((1, ct), lambda j, i: (0, j)), pl.BlockSpec((1, ct), lambda j, i: (0, nblk + j))
    return val, gate, wv, wg, bv, bg


def ffn_act_fwd(up, w, b, *, bsz, name):
    t = up.shape[0]
    half = up.shape[1] // 2
    seq, ct, k = t // bsz, 256, w.shape[0]
    val, gate, wv, wg, bv, bg = _ffn_specs(seq, ct, k, half // ct)

    def body(v_ref, g_ref, wv_ref, wg_ref, bv_ref, bg_ref, o_ref):
        vc = _conv(v_ref[...], wv_ref[...], bv_ref[...])
        gc = _conv(g_ref[...], wg_ref[...], bg_ref[...])
        o_ref[...] = (_silu(gc) * vc).astype(BF16)

    return pl.pallas_call(
        body, name=name, grid=(half // ct, bsz), in_specs=[val, gate, wv, wg, bv, bg], out_specs=val,
        out_shape=jax.ShapeDtypeStruct((t, half), BF16),
        compiler_params=_params(("parallel", "parallel")))(up, up, w, w, b, b)


def ffn_act_bwd(up, dact, w, b, *, bsz, name):
    t = up.shape[0]
    half = up.shape[1] // 2
    seq, ct, k = t // bsz, 256, w.shape[0]
    val, gate, wv, wg, bv, bg = _ffn_specs(seq, ct, k, half // ct)

    def body(v_ref, g_ref, wv_ref, wg_ref, bv_ref, bg_ref, d_ref, dv_ref, dg_ref, dwv_ref, dwg_ref, dbv_ref, dbg_ref):
        v, g = v_ref[...], g_ref[...]
        vc = _conv(v, wv_ref[...], bv_ref[...])
        gc = _conv(g, wg_ref[...], bg_ref[...])
        d = d_ref[...].astype(F32)
        dv, dwv, dbv = _conv_bwd(v, d * _silu(gc), wv_ref[...])
        dg, dwg, dbg = _conv_bwd(g, d * vc * _dsilu(gc), wg_ref[...])
        dv_ref[...] = dv.astype(BF16)
        dg_ref[...] = dg.astype(BF16)

        @pl.when(pl.program_id(1) == 0)
        def _():
            for r in (dwv_ref, dwg_ref, dbv_ref, dbg_ref):
                r[...] = jnp.zeros_like(r)

        dwv_ref[...] += dwv
        dwg_ref[...] += dwg
        dbv_ref[...] += dbv
        dbg_ref[...] += dbg

    return pl.pallas_call(
        body, name=name, grid=(half // ct, bsz), in_specs=[val, gate, wv, wg, bv, bg, val],
        out_specs=[val, val, wv, wv, bv, bv],
        out_shape=[jax.ShapeDtypeStruct((t, half), BF16), jax.ShapeDtypeStruct((t, half), BF16),
                   jax.ShapeDtypeStruct((k, half), F32), jax.ShapeDtypeStruct((k, half), F32),
                   jax.ShapeDtypeStruct((1, half), F32), jax.ShapeDtypeStruct((1, half), F32)],
        compiler_params=_params(("parallel", "arbitrary")))(up, up, w, w, b, b, dact)


def _sel_row(a, h):
    oh = (lax.broadcasted_iota(jnp.int32, (a.shape[0], 1), 0) == h).astype(F32)
    return jnp.sum(a * oh, axis=0, keepdims=True)


def _ssd_chunk(xp, dtr, bm, cm, prev, bias_r, alog_r, dskip_r, rev):
    q = dtr.shape[1]
    ri = lax.broadcasted_iota(jnp.int32, (q, q), 0)
    ci = lax.broadcasted_iota(jnp.int32, (q, q), 1)
    mask = (ci >= ri) if rev else (ci <= ri)
    lane_lo, row_lo = ci < HDIM, ri < HDIM
    dt_r = _softplus(dtr + bias_r)
    dta_r = dt_r * (-jnp.exp(alog_r))
    cs_r = cum_row(dta_r, rev)
    scores = dot_nt(cm, bm)

    def per_row(v):
        return jnp.broadcast_to(v, (q, q)).T

    ys, news = [], []
    for p in range(HPG // 2):
        ha = 2 * p + (HPG if rev else 0)
        hb = ha + 1
        cs_a, cs_b = _sel_row(cs_r, ha), _sel_row(cs_r, hb)
        csq_a, csq_b = per_row(cs_a), per_row(cs_b)
        seg_a = jnp.exp(jnp.where(mask, csq_a - cs_a, -1e30))
        seg_b = jnp.exp(jnp.where(mask, csq_b - cs_b, -1e30))
        csq = jnp.where(lane_lo, csq_a, csq_b)
        xdt = xp[p] * jnp.where(lane_lo, per_row(_sel_row(dt_r, ha)), per_row(_sel_row(dt_r, hb)))
        tot_a = jnp.sum(_sel_row(dta_r, ha), axis=1, keepdims=True)
        tot_b = jnp.sum(_sel_row(dta_r, hb), axis=1, keepdims=True)
        y = jnp.where(lane_lo, dot_nn(scores * seg_a, xdt), dot_nn(scores * seg_b, xdt))
        y = y + dot_nt(cm, prev[p]) * jnp.exp(csq)
        if not rev:
            y = y + jnp.where(lane_lo, _sel_row(dskip_r, ha), _sel_row(dskip_r, hb)) * xp[p]
        ys.append(y)
        st = dot_tn(xdt * jnp.exp(jnp.where(lane_lo, tot_a, tot_b) - csq), bm)
        news.append(jnp.exp(jnp.where(row_lo, tot_a, tot_b)) * prev[p] + st)
    return tuple(ys), tuple(news)


NPAIR = HPG // 2


def _ssd_specs(seq, nc):
    xs = pl.BlockSpec((None, seq, HPG * HDIM), lambda b, g: (b, 0, g))
    bm = pl.BlockSpec((None, seq, NSTATE), lambda b, g: (b, 0, SSD_W // NSTATE + g))
    cm = pl.BlockSpec((None, seq, NSTATE), lambda b, g: (b, 0, SSD_W // NSTATE + SGROUPS + g))
    dtr = pl.BlockSpec((None, None, 2 * HPG, seq), lambda b, g: (b, g, 0, 0))
    pr = pl.BlockSpec((None, 2 * HPG, 1), lambda b, g: (g, 0, 0))
    st = pl.BlockSpec((None, None, 2, nc, NPAIR, 2 * HDIM, NSTATE), lambda b, g: (b, g, 0, 0, 0, 0, 0))
    return xs, bm, cm, dtr, pr, st


def _pair_cols(p):
    return slice(2 * HDIM * p, 2 * HDIM * (p + 1))


def ssd_scan_fwd(act, dtr, prs, *, name):
    bsz, seq, _ = act.shape
    nc = seq // QC
    xs, bm, cm, dtrs, pr, st = _ssd_specs(seq, nc)

    def body(x_ref, b_ref, c_ref, dtr_ref, br_ref, ar_ref, dk_ref, y_ref, st_ref):
        par = (br_ref[...], ar_ref[...], dk_ref[...])
        y_ref[...] = jnp.zeros_like(y_ref)

        def step(i, carry):
            new = []
            for rev in (False, True):
                k = (nc - 1 - i) if rev else i
                rows = pl.ds(pl.multiple_of(k * QC, QC), QC)
                xp = tuple(x_ref[rows, _pair_cols(p)] for p in range(NPAIR))
                for p in range(NPAIR):
                    st_ref[int(rev), k, p] = carry[rev][p]
                ys, nw = _ssd_chunk(xp, dtr_ref[:, rows], b_ref[rows, :], c_ref[rows, :], carry[rev], *par, rev)
                for p in range(NPAIR):
                    y_ref[rows, _pair_cols(p)] += ys[p]
                new.append(nw)
            return tuple(new)

        zero = tuple(jnp.zeros((2 * HDIM, NSTATE), F32) for _ in range(NPAIR))
        lax.fori_loop(0, nc, step, (zero, zero))

    return pl.pallas_call(
        body, name=name, grid=(bsz, SGROUPS), in_specs=[xs, bm, cm, dtrs, pr, pr, pr], out_specs=[xs, st],
        out_shape=[jax.ShapeDtypeStruct((bsz, seq, SSD_W), F32),
                   jax.ShapeDtypeStruct((bsz, SGROUPS, 2, nc, NPAIR, 2 * HDIM, NSTATE), F32)],
        compiler_params=_params(("parallel", "parallel")))(act, act, act, dtr, *prs)


def ssd_scan_bwd(act, dtr, prs, states, dy, *, name):
    bsz, seq, _ = act.shape
    nc = seq // QC
    xs, bm, cm, dtrs, pr, st = _ssd_specs(seq, nc)
    grp = pl.BlockSpec((None, seq, NSTATE), lambda b, g: (b, 0, g))
    dpr = pl.BlockSpec((None, None, 2 * HPG, 1), lambda b, g: (b, g, 0, 0))

    def body(x_ref, b_ref, c_ref, dtr_ref, br_ref, ar_ref, dk_ref, st_ref, dy_ref,
             dx_ref, db_ref, dc_ref, ddtr_ref, gbr_ref, gar_ref, gdk_ref):
        par = (br_ref[...], ar_ref[...], dk_ref[...])
        pgrads = (gbr_ref, gar_ref, gdk_ref)
        for r in pgrads + (dx_ref, db_ref, dc_ref, ddtr_ref):
            r[...] = jnp.zeros_like(r)

        def bstep(i, dcarry):
            new = []
            for rev in (False, True):
                k = i if rev else (nc - 1 - i)
                rows = pl.ds(pl.multiple_of(k * QC, QC), QC)
                xp = tuple(x_ref[rows, _pair_cols(p)] for p in range(NPAIR))
                prev = tuple(st_ref[int(rev), k, p] for p in range(NPAIR))
                _, pull = jax.vjp(functools.partial(_ssd_chunk, rev=rev), xp, dtr_ref[:, rows], b_ref[rows, :],
                                  c_ref[rows, :], prev, *par)
                dyp = tuple(dy_ref[rows, _pair_cols(p)] for p in range(NPAIR))
                gx, gdt, gb, gc, gprev, *gpar = pull((dyp, dcarry[rev]))
                for p in range(NPAIR):
                    dx_ref[rows, _pair_cols(p)] += gx[p]
                ddtr_ref[:, rows] += gdt
                db_ref[rows, :] += gb
                dc_ref[rows, :] += gc
                for r, g in zip(pgrads, gpar):
                    r[...] += g
                new.append(gprev)
            return tuple(new)

        zero = tuple(jnp.zeros((2 * HDIM, NSTATE), F32) for _ in range(NPAIR))
        lax.fori_loop(0, nc, bstep, (zero, zero))

    out_shape = [jax.ShapeDtypeStruct((bsz, seq, SSD_W), F32),
                 jax.ShapeDtypeStruct((bsz, seq, SGROUPS * NSTATE), F32),
                 jax.ShapeDtypeStruct((bsz, seq, SGROUPS * NSTATE), F32),
                 jax.ShapeDtypeStruct(dtr.shape, F32)]
    out_shape += [jax.ShapeDtypeStruct((bsz, SGROUPS, 2 * HPG, 1), F32)] * 3
    return pl.pallas_call(
        body, name=name, grid=(bsz, SGROUPS), in_specs=[xs, bm, cm, dtrs, pr, pr, pr, st, xs],
        out_specs=[xs, grp, grp, dtrs, dpr, dpr, dpr], out_shape=out_shape,
        compiler_params=_params(("parallel", "parallel")))(act, act, act, dtr, *prs, states, dy)


def _s5_direction(lam_re, lam_im, log_step, b_re, b_im, c_re, c_im, rev):
    q = S5_Q
    step = jnp.exp(log_step)[:, None]
    lr, li = lam_re * step, lam_im * step
    mag = jnp.exp(lr)
    ar, ai = mag * jnp.cos(li), mag * jnp.sin(li)
    den = lam_re * lam_re + lam_im * lam_im
    cr = ((ar - 1.0) * lam_re + ai * lam_im) / den
    ci = (ai * lam_re - (ar - 1.0) * lam_im) / den
    bbr = cr[..., None] * b_re - ci[..., None] * b_im
    bbi = cr[..., None] * b_im + ci[..., None] * b_re
    d = jnp.arange(q + 1, dtype=F32)[None, :, None]
    pm = jnp.exp(d * lr[:, None, :])
    pr, pi = pm * jnp.cos(d * li[:, None, :]), pm * jnp.sin(d * li[:, None, :])
    er = pr[..., None] * bbr[:, None] - pi[..., None] * bbi[:, None]
    ei = pr[..., None] * bbi[:, None] + pi[..., None] * bbr[:, None]
    hp = lax.Precision.HIGHEST
    k = (jnp.einsum('gcp,gdpz->gdcz', c_re, er[:, :q], precision=hp)
         - jnp.einsum('gcp,gdpz->gdcz', c_im, ei[:, :q], precision=hp))
    e = jnp.concatenate([er[:, :q], ei[:, :q]], axis=2)
    wt = jnp.transpose(e if rev else e[:, ::-1], (0, 1, 3, 2))
    p1r, p1i = pr[:, 1:], pi[:, 1:]
    if rev:
        p1r, p1i = p1r[:, ::-1], p1i[:, ::-1]
    m_re = c_re[:, None] * p1r[:, :, None, :] - c_im[:, None] * p1i[:, :, None, :]
    m_im = -c_re[:, None] * p1i[:, :, None, :] - c_im[:, None] * p1r[:, :, None, :]
    mt = jnp.transpose(jnp.concatenate([m_re, m_im], axis=-1), (0, 3, 1, 2))
    da = jnp.concatenate([pr[:, q], pr[:, q]], axis=-1)
    db = jnp.concatenate([-pi[:, q], pi[:, q]], axis=-1)
    return k, wt, mt, da, db


def _s5_operators(lf_re, lf_im, lsf, lb_re, lb_im, lsb, b_re, b_im, cf_re, cf_im, cb_re, cb_im):
    kf, wtf, mtf, daf, dbf = _s5_direction(lf_re, lf_im, lsf, b_re, b_im, cf_re, cf_im, False)
    kb, wtb, mtb, dab, dbb = _s5_direction(lb_re, lb_im, lsb, b_re, b_im, cb_re, cb_im, True)
    g = kf.shape[0]
    lags = jnp.concatenate([kb[:, :0:-1], kf[:, :1] + kb[:, :1], kf[:, 1:]], axis=1)
    tt = jnp.transpose(lags, (0, 1, 3, 2))
    wt = jnp.concatenate([wtf.reshape(g, S5_QC, 2 * S5_P), wtb.reshape(g, S5_QC, 2 * S5_P)], axis=-1)
    mt = jnp.concatenate([mtf.reshape(g, 2 * S5_P, S5_QC), mtb.reshape(g, 2 * S5_P, S5_QC)], axis=1)
    return tt, wt, mt, jnp.concatenate([daf, dab], -1), jnp.concatenate([dbf, dbb], -1)


def _gspec(*shape):
    return pl.BlockSpec((None,) + shape, lambda g: (g,) + (0,) * len(shape))


S5_HALVES = S5_QC // LANES


def _toeplitz_block(s, t):
    per = LANES // S5_C
    return t // per, slice(s * S5_C, (s + 1) * S5_C), slice((t % per) * S5_C, (t % per + 1) * S5_C)


def s5_toeplitz(kt, *, name):
    g = kt.shape[0]

    def body(k_ref, t_ref):
        for s in range(S5_Q):
            for t in range(S5_Q):
                t_ref[_toeplitz_block(s, t)] = k_ref[t - s + S5_Q - 1]

    return pl.pallas_call(
        body, name=name, grid=(g,), in_specs=[_gspec(2 * S5_Q - 1, S5_C, S5_C)],
        out_specs=_gspec(S5_HALVES, S5_QC, LANES), out_shape=jax.ShapeDtypeStruct((g, S5_HALVES, S5_QC, LANES), F32),
        compiler_params=_params(("parallel",)))(kt)


def s5_toeplitz_bwd(dtt, *, name):
    g = dtt.shape[0]

    def body(d_ref, k_ref):
        for j in range(2 * S5_Q - 1):
            acc = None
            for s in range(S5_Q):
                t = j - (S5_Q - 1) + s
                if 0 <= t < S5_Q:
                    blk = d_ref[_toeplitz_block(s, t)]
                    acc = blk if acc is None else acc + blk
            k_ref[j] = acc

    return pl.pallas_call(
        body, name=name, grid=(g,), in_specs=[_gspec(S5_HALVES, S5_QC, LANES)],
        out_specs=_gspec(2 * S5_Q - 1, S5_C, S5_C), out_shape=jax.ShapeDtypeStruct((g, 2 * S5_Q - 1, S5_C, S5_C), F32),
        compiler_params=_params(("parallel",)))(dtt)


S5_RT = 64


def _chunk_piece(q):
    per = LANES // S5_C
    return q // per, slice((q % per) * S5_C, (q % per + 1) * S5_C)


def to_chunks(u, *, name):
    t = u.shape[0]
    r = t // S5_Q
    rt = min(S5_RT, r)

    per = LANES // S5_C
    nblk = S5_W // LANES

    def body(*refs):
        o_ref = refs[-1]
        for k in range(nblk):
            for q in range(S5_Q):
                rows = refs[k][pl.ds(q, rt, stride=S5_Q), :]
                half, lanes = _chunk_piece(q)
                for j in range(per):
                    o_ref[k * per + j, half, :, lanes] = rows[:, j * S5_C:(j + 1) * S5_C]

    return pl.pallas_call(
        body, name=name, grid=(r // rt,),
        in_specs=[pl.BlockSpec((rt * S5_Q, LANES), lambda i, k=k: (i, k)) for k in range(nblk)],
        out_specs=pl.BlockSpec((S5_G, S5_HALVES, rt, LANES), lambda i: (0, 0, i, 0)),
        out_shape=jax.ShapeDtypeStruct((S5_G, S5_HALVES, r, LANES), F32),
        compiler_params=_params(("parallel",)))(*[u] * nblk)


def from_chunks(y, *, name, add=None):
    r = y.shape[2]
    rt = min(S5_RT, r)
    per = LANES // S5_C

    nblk = S5_W // LANES

    def body(*refs):
        y_ref, tmp_ref = refs[0], refs[-1]
        adds, outs = refs[1:-1 - nblk], refs[-1 - nblk:-1]
        for k in range(nblk):
            for q in range(S5_Q):
                half, lanes = _chunk_piece(q)
                for j in range(per):
                    tmp_ref[:, j * S5_C:(j + 1) * S5_C] = y_ref[k * per + j, half, :, lanes]
                row = tmp_ref[...]
                if add is not None:
                    row = row + adds[k][pl.ds(q, rt, stride=S5_Q), :]
                outs[k][pl.ds(q, rt, stride=S5_Q), :] = row

    in_specs = [pl.BlockSpec((S5_G, S5_HALVES, rt, LANES), lambda i: (0, 0, i, 0))]
    if add is not None:
        in_specs += [pl.BlockSpec((rt * S5_Q, LANES), lambda i, k=k: (i, k)) for k in range(nblk)]
    blocks = pl.pallas_call(
        body, name=name, grid=(r // rt,), in_specs=in_specs,
        out_specs=[pl.BlockSpec((rt * S5_Q, LANES), lambda i: (i, 0))] * nblk,
        out_shape=[jax.ShapeDtypeStruct((r * S5_Q, LANES), F32)] * nblk,
        scratch_shapes=[pltpu.VMEM((rt, LANES), F32)],
        compiler_params=_params(("parallel",)))(*([y] if add is None else [y] + [add] * nblk))
    return jnp.concatenate(blocks, axis=1)


def _cat(ref):
    return jnp.concatenate([ref[h] for h in range(S5_HALVES)], axis=1)


def _put(ref, v):
    for h in range(S5_HALVES):
        ref[h] = v[:, h * LANES:(h + 1) * LANES]


def _cspec(r):
    return _gspec(S5_HALVES, r, LANES)


def s5_state_in(u, wt, *, name):
    g, _, r, _ = u.shape

    def body(u_ref, w_ref, o_ref):
        o_ref[...] = _bd(_cat(u_ref), w_ref[...], 1, 0)

    return pl.pallas_call(
        body, name=name, grid=(g,), in_specs=[_cspec(r), _gspec(S5_QC, 4 * S5_P)],
        out_specs=_gspec(r, 4 * S5_P), out_shape=jax.ShapeDtypeStruct((g, r, 4 * S5_P), F32),
        compiler_params=_params(("parallel",)))(u, wt)


def _swap(h):
    return pltpu.roll(h, S5_P, 1)


def s5_carry_fwd(s, da, db, *, name):
    nck, rows, _ = s.shape
    w = 2 * S5_P

    def body(s_ref, da_ref, db_ref, h_ref):
        dirs = ((False, slice(0, w)), (True, slice(w, 2 * w)))
        coef = [(da_ref[:, cols], db_ref[:, cols]) for _, cols in dirs]

        def step(i, hs):
            new = []
            for (rev, cols), (a, b), h in zip(dirs, coef, hs):
                k = (nck - 1 - i) if rev else i
                h_ref[k, :, cols] = h
                new.append(a * h + b * _swap(h) + s_ref[k, :, cols])
            return tuple(new)

        z = jnp.zeros((rows, w), F32)
        lax.fori_loop(0, nck, step, (z, z), unroll=2)

    rt = min(CARRY_ROWS, rows)
    big, small = pl.BlockSpec((nck, rt, 2 * w), lambda i: (0, i, 0)), pl.BlockSpec((rt, 2 * w), lambda i: (i, 0))
    rows = rt
    return pl.pallas_call(
        body, name=name, grid=(s.shape[1] // rt,), in_specs=[big, small, small], out_specs=big,
        out_shape=jax.ShapeDtypeStruct(s.shape, F32), compiler_params=_params(("parallel",)))(s, da, db)


def s5_carry_bwd(hin, dh, da, db, *, name):
    nck, rows, _ = hin.shape
    w = 2 * S5_P

    def body(h_ref, dh_ref, da_ref, db_ref, ds_ref, gda_ref, gdb_ref):
        dirs = ((False, slice(0, w)), (True, slice(w, 2 * w)))
        coef = [(da_ref[:, cols], db_ref[:, cols]) for _, cols in dirs]

        def step(i, carries):
            new = []
            for (rev, cols), (a, b), (g, ga, gb) in zip(dirs, coef, carries):
                k = i if rev else (nck - 1 - i)
                ds_ref[k, :, cols] = g
                h = h_ref[k, :, cols]
                new.append((dh_ref[k, :, cols] + a * g + _swap(b * g), ga + g * h, gb + g * _swap(h)))
            return tuple(new)

        z = jnp.zeros((rows, w), F32)
        res = lax.fori_loop(0, nck, step, ((z, z, z), (z, z, z)), unroll=2)
        for (_, cols), (_, ga, gb) in zip(dirs, res):
            gda_ref[:, cols] = ga
            gdb_ref[:, cols] = gb

    rt = min(CARRY_ROWS, rows)
    big, small = pl.BlockSpec((nck, rt, 2 * w), lambda i: (0, i, 0)), pl.BlockSpec((rt, 2 * w), lambda i: (i, 0))
    rows = rt
    return pl.pallas_call(
        body, name=name, grid=(hin.shape[1] // rt,), in_specs=[big, big, small, small], out_specs=[big, small, small],
        out_shape=[jax.ShapeDtypeStruct(hin.shape, F32), jax.ShapeDtypeStruct(da.shape, F32),
                   jax.ShapeDtypeStruct(da.shape, F32)],
        compiler_params=_params(("parallel",)))(hin, dh, da, db)


def s5_out(u, hin, tt, mt, *, name):
    g, _, r, _ = u.shape

    def body(u_ref, h_ref, t_ref, m_ref, o_ref):
        u_v, h_v = _cat(u_ref), h_ref[...]
        for half in range(S5_HALVES):
            cols = slice(half * LANES, (half + 1) * LANES)
            o_ref[half] = _bd(u_v, t_ref[half], 1, 0) + _bd(h_v, m_ref[:, cols], 1, 0)

    return pl.pallas_call(
        body, name=name, grid=(g,),
        in_specs=[_cspec(r), _gspec(r, 4 * S5_P), _gspec(S5_HALVES, S5_QC, LANES), _gspec(4 * S5_P, S5_QC)],
        out_specs=_cspec(r), out_shape=jax.ShapeDtypeStruct((g, S5_HALVES, r, LANES), F32),
        compiler_params=_params(("parallel",)))(u, hin, tt, mt)


def s5_out_bwd(dy, u, hin, tt, mt, *, name):
    g, _, r, _ = u.shape

    def body(dy_ref, u_ref, h_ref, t_ref, m_ref, dh_ref, dt_ref, dm_ref, du_ref):
        dy_v, u_v = _cat(dy_ref), _cat(u_ref)
        dh_ref[...] = _bd(dy_v, m_ref[...], 1, 1)
        dm_ref[...] = _bd(h_ref[...], dy_v, 0, 0)
        du = None
        for half in range(S5_HALVES):
            dy_h = dy_ref[half]
            dt_ref[half] = _bd(u_v, dy_h, 0, 0)
            part = _bd(dy_h, t_ref[half], 1, 1)
            du = part if du is None else du + part
        _put(du_ref, du)

    tspec = _gspec(S5_HALVES, S5_QC, LANES)
    return pl.pallas_call(
        body, name=name, grid=(g,),
        in_specs=[_cspec(r), _cspec(r), _gspec(r, 4 * S5_P), tspec, _gspec(4 * S5_P, S5_QC)],
        out_specs=[_gspec(r, 4 * S5_P), tspec, _gspec(4 * S5_P, S5_QC), _cspec(r)],
        out_shape=[jax.ShapeDtypeStruct((g, r, 4 * S5_P), F32), jax.ShapeDtypeStruct((g, S5_HALVES, S5_QC, LANES), F32),
                   jax.ShapeDtypeStruct((g, 4 * S5_P, S5_QC), F32), jax.ShapeDtypeStruct((g, S5_HALVES, r, LANES), F32)],
        compiler_params=_params(("parallel",)))(dy, u, hin, tt, mt)


def s5_state_in_bwd(ds, u, wt, du1, *, name):
    g, _, r, _ = u.shape

    def body(ds_ref, u_ref, w_ref, du1_ref, du_ref, dw_ref):
        ds_v = ds_ref[...]
        _put(du_ref, _cat(du1_ref) + _bd(ds_v, w_ref[...], 1, 1))
        dw_ref[...] = _bd(_cat(u_ref), ds_v, 0, 0)

    return pl.pallas_call(
        body, name=name, grid=(g,),
        in_specs=[_gspec(r, 4 * S5_P), _cspec(r), _gspec(S5_QC, 4 * S5_P), _cspec(r)],
        out_specs=[_cspec(r), _gspec(S5_QC, 4 * S5_P)],
        out_shape=[jax.ShapeDtypeStruct((g, S5_HALVES, r, LANES), F32), jax.ShapeDtypeStruct((g, S5_QC, 4 * S5_P), F32)],
        compiler_params=_params(("parallel",)))(ds, u, wt, du1)


def _s5_post(ypre, u, dvec, wv, wg, bv, bg, nw):
    g = _gelu(ypre + dvec * u)
    out = (dot_nn(g, wv) + bv) * jax.nn.sigmoid(dot_nn(g, wg) + bg)
    return (_rms(out, nw),)


def _ssd_post(y, z, nw):
    return (_rms(y * _silu(z), nw),)


def _to_carry(s, bsz):
    nck = s.shape[1] // bsz
    return jnp.transpose(s.reshape(S5_G, bsz, nck, -1), (2, 0, 1, 3)).reshape(nck, S5_G * bsz, -1)


def _from_carry(h, bsz):
    nck = h.shape[0]
    return jnp.transpose(h.reshape(nck, S5_G, bsz, -1), (1, 2, 0, 3)).reshape(S5_G, bsz * nck, -1)


def _block_diag(w):
    eye = jnp.eye(S5_G, dtype=w.dtype)
    return jnp.einsum('gcd,gh->gchd', w, eye).reshape(S5_W, S5_W)


def _diag_blocks(w):
    v = w.reshape(S5_G, S5_C, S5_G, S5_C)
    return v[jnp.arange(S5_G), :, jnp.arange(S5_G), :]


def _dt_rows(dt, bsz):
    seq = dt.shape[0] // bsz
    return jnp.transpose(dt.reshape(bsz, seq, 2, SGROUPS, HPG), (0, 3, 2, 4, 1)).reshape(bsz, SGROUPS, 2 * HPG, seq)


def _dt_from_rows(dr):
    bsz, _, _, seq = dr.shape
    return jnp.transpose(dr.reshape(bsz, SGROUPS, 2, HPG, seq), (0, 4, 2, 1, 3)).reshape(bsz * seq, 2 * HEADS)


def _head_params(f, b):
    return jnp.concatenate([f.reshape(SGROUPS, HPG), b.reshape(SGROUPS, HPG)], axis=1)[:, :, None]


def _head_grads(gr):
    v = gr.sum(0)[:, :, 0]
    return v[:, :HPG].reshape(HEADS), v[:, HPG:].reshape(HEADS)


def local_step(x, target, w):
    bsz, seq, d = x.shape
    t = bsz * seq
    x2, tgt2 = x.reshape(t, d), target.reshape(t, d)
    g = {}
    row = lambda v: v.reshape(1, -1)
    bf = lambda v: v.astype(BF16)

    w_in = _unshard(bf(w['w_in']), SHARDED['w_in'])
    cuts = [0, SSD_W, SSD_W + XBC_W, SSD_W + XBC_W + 2 * HEADS, w_in.shape[1]]
    w_in_parts = [w_in[:, a:b] for a, b in zip(cuts[:-1], cuts[1:])]
    norm_mix = row(w['norm_mix_w']) + w.get('token', 0.0)
    (hn,) = rowmap_fwd(lambda a, nw: (_rms(a, nw),), [x2], [norm_mix], [(d, BF16)], name="rms_mix")
    z, xbc, dt, u = [matmul_sum([hn], [p], tm=1024, name=f"in_proj_{i}") for i, p in enumerate(w_in_parts)]

    conv_w, conv_b = _unshard(w['ssd_conv_w'], SHARDED['ssd_conv_w']), row(w['ssd_conv_b'])
    act = ssd_conv_fwd(xbc, conv_w, conv_b, bsz=bsz, name="ssd_conv")
    dtr = _dt_rows(dt, bsz)
    prs = (_head_params(w['ssd_dt_bias_fwd'], w['ssd_dt_bias_bwd']),
           _head_params(w['ssd_a_log_fwd'], w['ssd_a_log_bwd']),
           _head_params(w['ssd_d'], jnp.zeros_like(w['ssd_d'])))
    act3 = act.reshape(bsz, seq, XBC_W)
    y_scan, ssd_states = ssd_scan_fwd(act3, dtr, prs, name="ssd_scan")
    y_scan = y_scan.reshape(t, SSD_W)
    ssd_nw = row(w['ssd_norm_w'])
    (y_ssd,) = rowmap_fwd(_ssd_post, [y_scan, z], [ssd_nw], [(SSD_W, BF16)], name="ssd_post")

    s5_names = ['s5_lambda_re_fwd', 's5_lambda_im_fwd', 's5_log_step_fwd', 's5_lambda_re_bwd', 's5_lambda_im_bwd',
                's5_log_step_bwd', 's5_b_re', 's5_b_im', 's5_c_re_fwd', 's5_c_im_fwd', 's5_c_re_bwd', 's5_c_im_bwd']
    (kt, wt, mt, da, db), s5_pull = jax.vjp(_s5_operators, *[w[n] for n in s5_names])
    tt_b, wt_b, mt_b = s5_toeplitz(kt, name="s5_toeplitz"), bf(wt), bf(mt)
    da_r, db_r = jnp.repeat(da, bsz, axis=0), jnp.repeat(db, bsz, axis=0)
    uc = to_chunks(u, name="s5_to_chunks_u")
    s_in = _to_carry(s5_state_in(uc, wt_b, name="s5_state_in"), bsz)
    hin_c = s5_carry_fwd(s_in, da_r, db_r, name="s5_carry")
    hin = _from_carry(hin_c, bsz)
    ypre = from_chunks(s5_out(uc, hin, tt_b, mt_b, name="s5_out"), name="s5_from_chunks_y")
    glu_w = w['s5_glu_w']
    s5_par = [row(w['s5_d']), _block_diag(glu_w[:, :, :S5_C]), _block_diag(glu_w[:, :, S5_C:]),
              row(w['s5_glu_b'][:, :S5_C]), row(w['s5_glu_b'][:, S5_C:]), row(w['s5_norm_w'])]
    (y_s5,) = rowmap_fwd(_s5_post, [ypre, u], s5_par, [(S5_W, BF16)], name="s5_post")

    if 'late' in w:
        w = {**w, **w['late'](y_s5)}
    w_out = bf(w['w_out']).reshape(SSD_W + S5_W, d)
    h1 = matmul_sum([y_ssd, y_s5], [w_out[:SSD_W], w_out[SSD_W:]], add=x2, name="out_proj")
    norm_ffn = row(w['norm_ffn_w'])
    (hn2,) = rowmap_fwd(lambda a, nw: (_rms(a, nw),), [h1], [norm_ffn], [(d, BF16)], name="rms_ffn")
    pad_c = FFN_PAD - FFN_BLK
    half = N_DEV // 2
    w_up3 = jnp.pad(bf(w['ffn_w_up']), ((0, 0), (0, 0), (0, pad_c)))
    w_down = jnp.pad(bf(w['ffn_w_down']).reshape(half, FFN_BLK, d), ((0, 0), (0, pad_c), (0, 0)))
    w_down = w_down.reshape(half * FFN_PAD, d)
    fconv_w = jnp.pad(w['ffn_conv_w'], ((0, 0), (0, 0), (0, pad_c)))
    fconv_w = jnp.transpose(fconv_w, (1, 0, 2)).reshape(FCONV, N_DEV * FFN_PAD)
    fconv_b = row(jnp.pad(w['ffn_conv_b'].reshape(N_DEV, FFN_BLK), ((0, 0), (0, pad_c))))
    up = matmul_cols(hn2, w_up3, name="ffn_up")
    fact = ffn_act_fwd(up, fconv_w, fconv_b, bsz=bsz, name="ffn_act")
    h2 = matmul_sum([fact], [w_down], add=h1, name="ffn_down")
    loss, dh2, g_nf = loss_head(h2, tgt2, row(w['norm_final_w']), name="loss_head")
    g['norm_final_w'] = g_nf.reshape(-1)

    dfact = matmul_sum([dh2], [w_down], nt=True, tm=1024, name="ffn_down_dx")
    g_down = matmul_tn(fact, dh2, name="ffn_down_dw").reshape(half, FFN_PAD, d)[:, :FFN_BLK]
    g['ffn_w_down'] = g_down.reshape(N_DEV, FFN_BLK // 2, d)
    dval, dgate, dwv, dwg, dbv, dbg = ffn_act_bwd(up, dfact, fconv_w, fconv_b, bsz=bsz, name="ffn_act_bwd")
    g_cw = jnp.concatenate([dwv, dwg], axis=1).reshape(FCONV, N_DEV, FFN_PAD)[:, :, :FFN_BLK]
    g['ffn_conv_w'] = jnp.transpose(g_cw, (1, 0, 2))
    g['ffn_conv_b'] = jnp.concatenate([dbv, dbg], axis=1).reshape(N_DEV, FFN_PAD)[:, :FFN_BLK].reshape(-1)
    windows = [(dval, FFN_PAD, p) for p in range(half)] + [(dgate, FFN_PAD, p) for p in range(half)]
    dhn2 = matmul_sum(windows, [(w_up3, p) for p in range(N_DEV)], nt=True, name="ffn_up_dx")
    g['ffn_w_up'] = jnp.concatenate([matmul_tn(hn2, dval, out_blocks=half, name="ffn_up_dw_val"),
                                     matmul_tn(hn2, dgate, out_blocks=half, name="ffn_up_dw_gate")],
                                    axis=0)[:, :, :FFN_BLK]
    send_early = w.get('on_grads')
    if send_early:
        norm_ffn = norm_ffn + send_early(g, ['ffn_w_up', 'ffn_w_down'])
    dh1, g_nffn = rowmap_bwd(lambda a, nw: (_rms(a, nw),), [h1], [norm_ffn], [dhn2], add=dh2, name="rms_ffn_bwd")
    g['norm_ffn_w'] = g_nffn.reshape(-1)

    dycat = matmul_sum([dh1], [w_out], nt=True, tm=1024, name="out_proj_dx")
    g['w_out'] = jnp.concatenate([matmul_tn(y_ssd, dh1, name="out_proj_dw_ssd"),
                                  matmul_tn(y_s5, dh1, name="out_proj_dw_s5")], axis=0).reshape(w['w_out'].shape)
    if send_early:
        ssd_nw = ssd_nw + send_early(g, ['w_out'])
    dy_scan, dz, g_snw = rowmap_bwd(_ssd_post, [y_scan, z], [ssd_nw], [(dycat, SSD_W, 0)], name="ssd_post_bwd")
    g['ssd_norm_w'] = g_snw.reshape(-1)
    dypre, du_a, g_d, g_wv, g_wg, g_bv, g_bg, g_s5nw = rowmap_bwd(
        _s5_post, [ypre, u], s5_par, [(dycat, S5_W, SSD_W // S5_W)], name="s5_post_bwd")
    g['s5_d'], g['s5_norm_w'] = g_d.reshape(-1), g_s5nw.reshape(-1)
    g['s5_glu_w'] = jnp.concatenate([_diag_blocks(g_wv), _diag_blocks(g_wg)], axis=-1)
    g['s5_glu_b'] = jnp.concatenate([g_bv.reshape(S5_G, S5_C), g_bg.reshape(S5_G, S5_C)], axis=-1)

    dyc = to_chunks(dypre, name="s5_to_chunks_dy")
    dhin, dtt, dmt, du1 = s5_out_bwd(dyc, uc, hin, tt_b, mt_b, name="s5_out_bwd")
    ds_c, gda, gdb = s5_carry_bwd(hin_c, _to_carry(dhin, bsz), da_r, db_r, name="s5_carry_bwd")
    duc, dwt = s5_state_in_bwd(_from_carry(ds_c, bsz), uc, wt_b, du1, name="s5_state_in_bwd")
    du = from_chunks(duc, add=du_a, name="s5_from_chunks_du")
    fold = lambda v: v.reshape(S5_G, bsz, -1).sum(1)
    dkt = s5_toeplitz_bwd(dtt, name="s5_toeplitz_bwd")
    for n, gv in zip(s5_names, s5_pull((dkt, dwt, dmt, fold(gda), fold(gdb)))):
        g[n] = gv

    dxs, dbm, dcm, ddtr, gbr, gar, gdk = ssd_scan_bwd(
        act3, dtr, prs, ssd_states, dy_scan.reshape(bsz, seq, SSD_W), name="ssd_scan_bwd")
    g['ssd_dt_bias_fwd'], g['ssd_dt_bias_bwd'] = _head_grads(gbr)
    g['ssd_a_log_fwd'], g['ssd_a_log_bwd'] = _head_grads(gar)
    g['ssd_d'] = _head_grads(gdk)[0]
    dact = jnp.concatenate([dxs, dbm, dcm], axis=-1).reshape(t, XBC_W)
    dxbc, g_cw, g_cb = ssd_conv_bwd(xbc, dact, conv_w, conv_b, bsz=bsz, name="ssd_conv_bwd")
    g['ssd_conv_w'] = _shard_rows(g_cw, SHARDED['ssd_conv_w']).reshape(w['ssd_conv_w'].shape)
    g['ssd_conv_b'] = g_cb.reshape(-1)
    ddt = _dt_from_rows(ddtr)

    dparts = [dz, dxbc, ddt, du]
    g_in = jnp.concatenate([matmul_tn(hn, dp, name=f"in_proj_dw_{i}") for i, dp in enumerate(dparts)], axis=1)
    g['w_in'] = _shard_rows(g_in, SHARDED['w_in']).reshape(w['w_in'].shape)
    if send_early:
        dparts[2] = ddt + send_early(g, ['w_in'], loss=loss)
    dhn = matmul_sum(dparts, w_in_parts, nt=True, name="in_proj_dx")
    dx, g_nmix = rowmap_bwd(lambda a, nw: (_rms(a, nw),), [x2], [norm_mix], [dhn], add=dh1, name="rms_mix_bwd")
    g['norm_mix_w'] = g_nmix.reshape(-1)
    return loss, dx.reshape(bsz, seq, d), g


ANY = pl.BlockSpec(memory_space=pl.ANY)


def all_gather(shards, *, name):
    n = len(shards)

    def body(*refs):
        x_refs, out_refs = refs[:n], refs[n:2 * n]
        send_sems, recv_sems, local_sems = refs[2 * n:]
        x, y, c = lax.axis_index("x"), lax.axis_index("y"), lax.axis_index("c")
        me, sibling = (x, y, c), (x, y, 1 - c)
        chips = [(1 - x, y), (x, 1 - y), (1 - x, 1 - y)]

        def copy(k, j, block, to, own=False):
            dst = out_refs[j].at[4 * block[0] + 2 * block[1] + block[2]]
            return pltpu.make_async_remote_copy(
                src_ref=x_refs[j] if own else dst, dst_ref=dst,
                send_sem=send_sems.at[k, j], recv_sem=recv_sems.at[k, j], device_id=to, device_id_type=MESH)

        mine = [pltpu.make_async_copy(x_refs[j], out_refs[j].at[4 * x + 2 * y + c], local_sems.at[j]) for j in range(n)]
        first = [copy(0, j, me, sibling, own=True) for j in range(n)]
        first += [copy(1 + i, j, me, (*chip, c), own=True) for i, chip in enumerate(chips) for j in range(n)]
        for cp in mine + first:
            cp.start()
        passed = []
        for i, chip in enumerate(chips):
            for j in range(n):
                copy(1 + i, j, (*chip, c), me).wait_recv()
                passed.append(copy(4 + i, j, (*chip, c), sibling))
                passed[-1].start()
        for j in range(n):
            copy(0, j, sibling, me).wait_recv()
        for i, chip in enumerate(chips):
            for j in range(n):
                copy(4 + i, j, (*chip, 1 - c), me).wait_recv()
        for cp in first + passed:
            cp.wait_send()
        for cp in mine:
            cp.wait()

    return pl.pallas_call(
        body, name=name, out_shape=[jax.ShapeDtypeStruct((N_DEV,) + s.shape, s.dtype) for s in shards],
        in_specs=[ANY] * n, out_specs=[ANY] * n,
        scratch_shapes=[pltpu.SemaphoreType.DMA((7, n)), pltpu.SemaphoreType.DMA((7, n)),
                        pltpu.SemaphoreType.DMA((n,))],
    )(*shards)


def exchange(sends, *, name):
    n = len(sends)

    def body(*refs):
        send_refs, recv_refs = refs[:n], refs[n:2 * n]
        send_sems, recv_sems, local_sems = refs[2 * n:]
        x, y, c = lax.axis_index("x"), lax.axis_index("y"), lax.axis_index("c")
        me = 4 * x + 2 * y + c
        local = [pltpu.make_async_copy(send_refs[j].at[me], recv_refs[j].at[me], local_sems.at[j]) for j in range(n)]
        for cp in local:
            cp.start()
        copies = []
        for k in range(1, N_DEV):
            px = (1 - x) if k & 4 else x
            py = (1 - y) if k & 2 else y
            pc = (1 - c) if k & 1 else c
            for j in range(n):
                copies.append(pltpu.make_async_remote_copy(
                    src_ref=send_refs[j].at[4 * px + 2 * py + pc], dst_ref=recv_refs[j].at[me],
                    send_sem=send_sems.at[k - 1, j], recv_sem=recv_sems.at[k - 1, j],
                    device_id=(px, py, pc), device_id_type=MESH))
        for cp in copies:
            cp.start()
        for cp in copies:
            cp.wait()
        for cp in local:
            cp.wait()

    return pl.pallas_call(
        body, name=name, out_shape=[jax.ShapeDtypeStruct(s.shape, s.dtype) for s in sends],
        in_specs=[ANY] * n, out_specs=[ANY] * n,
        scratch_shapes=[pltpu.SemaphoreType.DMA((N_DEV - 1, n)), pltpu.SemaphoreType.DMA((N_DEV - 1, n)),
                        pltpu.SemaphoreType.DMA((n,))],
    )(*sends)


HBM_SPEC = pl.BlockSpec(memory_space=pltpu.HBM)
SEM_SPEC = pl.BlockSpec(memory_space=pltpu.SEMAPHORE)
SPLIT_PARAMS = pltpu.CompilerParams(has_side_effects=pltpu.SideEffectType.DATAFLOW_SIDE_EFFECTING)


def _peer_copies(src_refs, land_refs, send_sems, recv_sems, indexed):
    x, y, c = lax.axis_index("x"), lax.axis_index("y"), lax.axis_index("c")
    me = 4 * x + 2 * y + c
    copies = []
    for k in range(1, N_DEV):
        px = (1 - x) if k & 4 else x
        py = (1 - y) if k & 2 else y
        pc = (1 - c) if k & 1 else c
        for j, (src, land) in enumerate(zip(src_refs, land_refs)):
            sem = (k - 1) * len(src_refs) + j
            copies.append(pltpu.make_async_remote_copy(
                src_ref=src.at[4 * px + 2 * py + pc] if indexed else src, dst_ref=land.at[me],
                send_sem=send_sems.at[sem], recv_sem=recv_sems.at[sem],
                device_id=(px, py, pc), device_id_type=MESH))
    return copies


def scatter_start(srcs, *, name, indexed):
    n = len(srcs)
    lands = [lax.empty(s.shape if indexed else (N_DEV,) + s.shape, s.dtype) for s in srcs]

    def body(*refs):
        send_sems, recv_sems = refs[2 * n], refs[2 * n + 1]
        for cp in _peer_copies(refs[:n], refs[n:2 * n], send_sems, recv_sems, indexed):
            cp.start()
        refs[-1][...] = jnp.zeros_like(refs[-1])

    hbm = lambda a: pltpu.HBM(a.shape, a.dtype)
    sems = pltpu.SemaphoreType.DMA(((N_DEV - 1) * n,))
    res = pl.pallas_call(
        body, name=name,
        out_shape=(sems, sems, *[hbm(a) for a in srcs + lands], jax.ShapeDtypeStruct((8, LANES), F32)),
        in_specs=[HBM_SPEC] * (2 * n),
        out_specs=(SEM_SPEC, SEM_SPEC, *[HBM_SPEC] * (2 * n), pl.BlockSpec(memory_space=pltpu.VMEM)),
        input_output_aliases={i: 2 + i for i in range(2 * n)}, compiler_params=SPLIT_PARAMS,
    )(*[pltpu.with_memory_space_constraint(a, pltpu.HBM) for a in srcs + lands])
    return res[0], res[1], list(res[2:2 + n]), list(res[2 + n:2 + 2 * n]), res[-1]


def scatter_wait(send_sems, recv_sems, srcs, lands, after, *, name, indexed):
    n = len(srcs)

    def body(*refs):
        for cp in _peer_copies(refs[:n], refs[n:2 * n], refs[2 * n], refs[2 * n + 1], indexed):
            cp.wait_send()
            cp.wait_recv()

    hbm = lambda a: pltpu.HBM(a.shape, a.dtype)
    res = pl.pallas_call(
        body, name=name, out_shape=tuple(hbm(a) for a in srcs + lands),
        in_specs=[HBM_SPEC] * (2 * n) + [SEM_SPEC, SEM_SPEC, ANY], out_specs=tuple([HBM_SPEC] * (2 * n)),
        input_output_aliases={i: i for i in range(2 * n)}, compiler_params=SPLIT_PARAMS,
    )(*srcs, *lands, send_sems, recv_sems, after)
    return list(res[:n]), list(res[n:])


def _adam_rows(r, c):
    fits = [t for t in range(8, r + 1, 8) if r % t == 0 and N_DEV * t * c * 4 <= 6 * 2 ** 20]
    return max(fits) if fits else r


def adamw(recv, w, m, v, *, name):
    _, r, n = recv.shape
    tr = _adam_rows(r, n)

    def body(r_ref, w_ref, m_ref, v_ref, g_ref, d_ref, nm_ref, nv_ref):
        g = r_ref[0].astype(F32)
        for s in range(1, N_DEV):
            g = g + r_ref[s].astype(F32)
        m_new = ADAM_B1 * m_ref[...] + (1.0 - ADAM_B1) * g
        v_new = ADAM_B2 * v_ref[...] + (1.0 - ADAM_B2) * jnp.square(g)
        m_hat = m_new / (1.0 - ADAM_B1 ** ADAM_STEP)
        v_hat = v_new / (1.0 - ADAM_B2 ** ADAM_STEP)
        g_ref[...] = g
        d_ref[...] = -ADAM_LR * (m_hat / (jnp.sqrt(v_hat) + ADAM_EPS) + ADAM_WD * w_ref[...])
        nm_ref[...] = m_new
        nv_ref[...] = v_new

    blk = pl.BlockSpec((tr, n), lambda i: (i, 0))
    return pl.pallas_call(
        body, name=name, grid=(r // tr,), in_specs=[pl.BlockSpec((N_DEV, tr, n), lambda i: (0, i, 0)), blk, blk, blk],
        out_specs=[blk] * 4, out_shape=[jax.ShapeDtypeStruct((r, n), F32)] * 4,
        compiler_params=_params(("parallel",)))(recv, w, m, v)


def _shard_rows(full, axis):
    if axis == 0:
        return full.reshape(N_DEV, -1)
    r, c = full.shape
    return jnp.transpose(full.reshape(r, N_DEV, c // N_DEV), (1, 0, 2)).reshape(N_DEV, -1)


def _unshard(blocks, axis):
    if axis == 0:
        return blocks.reshape(-1, blocks.shape[-1])
    return jnp.transpose(blocks, (1, 0, 2)).reshape(blocks.shape[1], -1)


def kernel(x, norm_mix_w, w_in, ssd_conv_w, ssd_conv_b, ssd_dt_bias_fwd, ssd_dt_bias_bwd, ssd_a_log_fwd, ssd_a_log_bwd, ssd_d, ssd_norm_w, s5_lambda_re_fwd, s5_lambda_im_fwd, s5_log_step_fwd, s5_lambda_re_bwd, s5_lambda_im_bwd, s5_log_step_bwd, s5_b_re, s5_b_im, s5_c_re_fwd, s5_c_im_fwd, s5_c_re_bwd, s5_c_im_bwd, s5_d, s5_glu_w, s5_glu_b, s5_norm_w, w_out, norm_ffn_w, ffn_w_up, ffn_conv_w, ffn_conv_b, ffn_w_down, norm_final_w, loss_target, m_norm_mix_w, m_w_in, m_ssd_conv_w, m_ssd_conv_b, m_ssd_dt_bias_fwd, m_ssd_dt_bias_bwd, m_ssd_a_log_fwd, m_ssd_a_log_bwd, m_ssd_d, m_ssd_norm_w, m_s5_lambda_re_fwd, m_s5_lambda_im_fwd, m_s5_log_step_fwd, m_s5_lambda_re_bwd, m_s5_lambda_im_bwd, m_s5_log_step_bwd, m_s5_b_re, m_s5_b_im, m_s5_c_re_fwd, m_s5_c_im_fwd, m_s5_c_re_bwd, m_s5_c_im_bwd, m_s5_d, m_s5_glu_w, m_s5_glu_b, m_s5_norm_w, m_w_out, m_norm_ffn_w, m_ffn_w_up, m_ffn_conv_w, m_ffn_conv_b, m_ffn_w_down, m_norm_final_w, v_norm_mix_w, v_w_in, v_ssd_conv_w, v_ssd_conv_b, v_ssd_dt_bias_fwd, v_ssd_dt_bias_bwd, v_ssd_a_log_fwd, v_ssd_a_log_bwd, v_ssd_d, v_ssd_norm_w, v_s5_lambda_re_fwd, v_s5_lambda_im_fwd, v_s5_log_step_fwd, v_s5_lambda_re_bwd, v_s5_lambda_im_bwd, v_s5_log_step_bwd, v_s5_b_re, v_s5_b_im, v_s5_c_re_fwd, v_s5_c_im_fwd, v_s5_c_re_bwd, v_s5_c_im_bwd, v_s5_d, v_s5_glu_w, v_s5_glu_b, v_s5_norm_w, v_w_out, v_norm_ffn_w, v_ffn_w_up, v_ffn_conv_w, v_ffn_conv_b, v_ffn_w_down, v_norm_final_w):
    args = dict(locals())
    strip = lambda n, v: v if n == 'norm_final_w' else v[0]
    w = {n: strip(n, args[n]) for n in WEIGHTS}

    mats = ['w_in', 'w_out', 'ffn_w_up', 'ffn_w_down']
    convs = ['ssd_conv_w', 'ffn_conv_w']
    shard = lambda n: w[n].astype(BF16) if n in mats else w[n]
    early, late = ['w_in', 'ssd_conv_w'], ['w_out', 'ffn_w_up', 'ffn_w_down', 'ffn_conv_w']
    full = dict(w)
    full.update(zip(early, all_gather([shard(n) for n in early], name="weight_all_gather")))
    ssem, rsem, src_thru, land_thru, token = scatter_start([shard(n) for n in late], name="weight_gather_start",
                                                           indexed=False)
    me = 4 * lax.axis_index("x") + 2 * lax.axis_index("y") + lax.axis_index("c")

    def late_weights(after):
        own, landed = scatter_wait(ssem, rsem, src_thru, land_thru, after, name="weight_gather_wait", indexed=False)
        return {n: lax.dynamic_update_index_in_dim(l, o, me, 0) for n, o, l in zip(late, own, landed)}

    full['late'], full['token'] = late_weights, token[:1, :1]

    pending = []
    last = 'norm_mix_w'
    small = convs + [n for n in WEIGHTS if n not in SHARDED and n != last]
    total = sum(w[n].size for n in small) + 1
    nrow = -(-total // (PACK_ROWS * LANES)) * PACK_ROWS

    def send_early(grads, names, loss=None):
        srcs = [grads[n].astype(BF16) for n in names]
        if loss is not None:
            pieces = [grads[n].reshape(N_DEV, -1) if n in SHARDED else
                      jnp.broadcast_to(grads[n].reshape(1, -1), (N_DEV, grads[n].size)) for n in small]
            pieces += [jnp.broadcast_to(loss.reshape(1, 1), (N_DEV, 1)), jnp.zeros((N_DEV, nrow * LANES - total), F32)]
            srcs.append(jnp.concatenate(pieces, axis=1).reshape(N_DEV, nrow, LANES))
            names = names + ['small']
        started = scatter_start(srcs, name="grad_start_" + names[0], indexed=True)
        pending.append((names,) + started[:4])
        return started[4][:1, :1]

    full['on_grads'] = send_early
    loss, grad_x, g = local_step(x, loss_target, full)

    last_send = jnp.broadcast_to(g[last].reshape(1, -1, LANES), (N_DEV, g[last].size // LANES, LANES))
    recv = {last: exchange([last_send], name="grad_exchange")[0]}
    for names, ssem, rsem, src_thru, land_thru in pending:
        own, landed = scatter_wait(ssem, rsem, src_thru, land_thru, recv[last], name="grad_wait_" + names[0],
                                   indexed=True)
        for n, o, l in zip(names, own, landed):
            recv[n] = lax.dynamic_update_index_in_dim(l, lax.dynamic_index_in_dim(o, me, 0, keepdims=False), me, 0)

    outs = [{}, {}, {}, {}]
    for n in mats + [last]:
        shape = recv[n].shape[1:]
        res = adamw(recv[n], *[strip(n, args[p + n]).reshape(shape) for p in ('', 'm_', 'v_')], name="adamw_" + n)
        for o, p in zip(outs, res):
            o[n] = p.reshape(args[n].shape)

    def pack(prefix):
        vals = [strip(n, args[prefix + n]).reshape(-1) for n in small]
        return jnp.pad(jnp.concatenate(vals), (0, nrow * LANES - total + 1)).reshape(nrow, LANES)

    packed = adamw(recv['small'], pack(''), pack('m_'), pack('v_'), name="adamw_small")
    packed = [p.reshape(-1) for p in packed]
    off = 0
    for n in small:
        size = w[n].size
        for o, p in zip(outs, packed):
            o[n] = p[off:off + size].reshape(args[n].shape)
        off += size
    loss_out = packed[0][off].reshape(())
    return (loss_out, grad_x, *[o[n] for o in outs for n in WEIGHTS])
```

```python
import functools

import jax
import jax.numpy as jnp
from jax import lax
from jax.experimental import pallas as pl
from jax.experimental.pallas import tpu as pltpu

F32, BF16 = jnp.float32, jnp.bfloat16
N_DEV = 8
D_MODEL = 1024
SSD_W, HEADS, HDIM, SGROUPS, HPG, NSTATE, SCONV, QC = 1024, 16, 64, 4, 4, 128, 5, 128
XBC_W = SSD_W + 2 * SGROUPS * NSTATE
S5_W, S5_G, S5_C, S5_P, S5_Q = 512, 32, 16, 64, 16
S5_QC = S5_Q * S5_C
CARRY_ROWS = 32
DFF, FCONV = 2816, 3
FFN_BLK, FFN_PAD = 704, 768
EPS = 1e-6
ADAM_LR, ADAM_B1, ADAM_B2, ADAM_EPS, ADAM_WD, ADAM_STEP = 0.001, 0.9, 0.999, 1e-08, 0.01, 10
LANES = 128
MESH = pl.DeviceIdType.MESH

WEIGHTS = ['norm_mix_w', 'w_in', 'ssd_conv_w', 'ssd_conv_b', 'ssd_dt_bias_fwd', 'ssd_dt_bias_bwd', 'ssd_a_log_fwd',
           'ssd_a_log_bwd', 'ssd_d', 'ssd_norm_w', 's5_lambda_re_fwd', 's5_lambda_im_fwd', 's5_log_step_fwd',
           's5_lambda_re_bwd', 's5_lambda_im_bwd', 's5_log_step_bwd', 's5_b_re', 's5_b_im', 's5_c_re_fwd', 's5_c_im_fwd',
           's5_c_re_bwd', 's5_c_im_bwd', 's5_d', 's5_glu_w', 's5_glu_b', 's5_norm_w', 'w_out', 'norm_ffn_w', 'ffn_w_up',
           'ffn_conv_w', 'ffn_conv_b', 'ffn_w_down', 'norm_final_w']
SHARDED = {'w_in': 1, 'ssd_conv_w': 1, 'w_out': 0, 'ffn_w_up': 1, 'ffn_conv_w': 1, 'ffn_w_down': 0}
FULL_SHAPE = {'w_in': (1024, 3616), 'ssd_conv_w': (5, 2048), 'w_out': (1536, 1024), 'ffn_w_up': (1024, 5632),
              'ffn_conv_w': (3, 5632), 'ffn_w_down': (2816, 1024)}
PACK_ROWS = 512


def _pick(n, cap=1536):
    if n <= cap:
        return n
    return max(t for t in range(LANES, cap + 1, LANES) if n % t == 0)


def _params(sem):
    return pltpu.CompilerParams(dimension_semantics=sem)


def _bd(a, b, ca, cb):
    return lax.dot_general(a.astype(BF16), b.astype(BF16), (((ca,), (cb,)), ((), ())), preferred_element_type=F32)


@jax.custom_vjp
def dot_nn(a, b):
    return _bd(a, b, 1, 0)


dot_nn.defvjp(lambda a, b: (_bd(a, b, 1, 0), (a, b)),
              lambda r, g: (_bd(g, r[1], 1, 1).astype(r[0].dtype), _bd(r[0], g, 0, 0).astype(r[1].dtype)))


@jax.custom_vjp
def dot_nt(a, b):
    return _bd(a, b, 1, 1)


dot_nt.defvjp(lambda a, b: (_bd(a, b, 1, 1), (a, b)),
              lambda r, g: (_bd(g, r[1], 1, 0).astype(r[0].dtype), _bd(g, r[0], 0, 0).astype(r[1].dtype)))


@jax.custom_vjp
def dot_tn(a, b):
    return _bd(a, b, 0, 0)


dot_tn.defvjp(lambda a, b: (_bd(a, b, 0, 0), (a, b)),
              lambda r, g: (_bd(r[1], g, 1, 1).astype(r[0].dtype), _bd(r[0], g, 1, 0).astype(r[1].dtype)))


def _rows2(v):
    h = v.shape[0] // 2
    return v[:h], v[h:]


def _cols2(v):
    h = v.shape[1] // 2
    return v[:, :h], v[:, h:]


@jax.custom_vjp
def dot2_nn(la, lb, x):
    return _rows2(_bd(jnp.concatenate([la, lb], axis=0), x, 1, 0))


def _dot2_nn_bwd(res, g):
    la, lb, x = res
    gcat, lcat = jnp.concatenate(g, axis=0), jnp.concatenate([la, lb], axis=0)
    return (*_rows2(_bd(gcat, x, 1, 1)), _bd(lcat, gcat, 0, 0))


dot2_nn.defvjp(lambda la, lb, x: (dot2_nn(la, lb, x), (la, lb, x)), _dot2_nn_bwd)


@jax.custom_vjp
def dot_nt2(c, p0, p1):
    return _cols2(_bd(c, jnp.concatenate([p0, p1], axis=0), 1, 1))


def _dot_nt2_bwd(res, g):
    c, p0, p1 = res
    gcat = jnp.concatenate(g, axis=1)
    return (_bd(gcat, jnp.concatenate([p0, p1], axis=0), 1, 0), *_rows2(_bd(gcat, c, 0, 0)))


dot_nt2.defvjp(lambda c, p0, p1: (dot_nt2(c, p0, p1), (c, p0, p1)), _dot_nt2_bwd)


@jax.custom_vjp
def dot_tn2(a0, a1, b):
    return _rows2(_bd(jnp.concatenate([a0, a1], axis=1), b, 0, 0))


def _dot_tn2_bwd(res, g):
    a0, a1, b = res
    gcat, acat = jnp.concatenate(g, axis=0), jnp.concatenate([a0, a1], axis=1)
    return (*_cols2(_bd(b, gcat, 1, 1)), _bd(acat, gcat, 1, 0))


dot_tn2.defvjp(lambda a0, a1, b: (dot_tn2(a0, a1, b), (a0, a1, b)), _dot_tn2_bwd)


def _split3(x):
    hi = x.astype(BF16)
    r = x - hi.astype(F32)
    mid = r.astype(BF16)
    lo = (r - mid.astype(F32)).astype(BF16)
    return hi, mid, lo


def _cum_matrix(q, upper):
    ri = lax.broadcasted_iota(jnp.int32, (q, q), 0)
    ci = lax.broadcasted_iota(jnp.int32, (q, q), 1)
    return jnp.where((ci >= ri) if upper else (ci <= ri), 1.0, 0.0).astype(BF16)


def _exact_right(x, mat):
    return sum(jnp.dot(p, mat, preferred_element_type=F32) for p in _split3(x))


@functools.partial(jax.custom_vjp, nondiff_argnums=(1,))
def cum_row(x, rev):
    return _exact_right(x, _cum_matrix(x.shape[1], not rev))


cum_row.defvjp(lambda x, rev: (cum_row(x, rev), None),
               lambda rev, _, g: (_exact_right(g, _cum_matrix(g.shape[1], rev)),))


def _softplus(x):
    return jnp.maximum(x, 0.0) + jnp.log(1.0 + jnp.exp(-jnp.abs(x)))


def _silu(x):
    return x * jax.nn.sigmoid(x)


def _gelu(x):
    return 0.5 * x * (1.0 + jnp.tanh(0.7978845608028654 * (x + 0.044715 * (x * x * x))))


def _rms(x, w):
    xf = x.astype(F32)
    return xf * lax.rsqrt(jnp.mean(xf * xf, axis=-1, keepdims=True) + EPS) * w


def matmul_sum(a_list, b_list, *, name, out_dtype=F32, add=None, tm=512, nt=False):
    a_arrs = [a[0] if isinstance(a, tuple) else a for a in a_list]
    b_arrs = [b[0] if isinstance(b, tuple) else b for b in b_list]
    m, n = a_arrs[0].shape[0], b_arrs[0].shape[-2 if nt else -1]
    tm, tn, k = min(tm, m), _pick(n), len(a_list)

    def body(*refs):
        acc = None
        for a_ref, b_ref in zip(refs[:k], refs[k:2 * k]):
            p = _bd(a_ref[...], b_ref[...], 1, 1 if nt else 0)
            acc = p if acc is None else acc + p
        if add is not None:
            acc = acc + refs[2 * k][...]
        refs[-1][...] = acc.astype(out_dtype)

    def a_spec(a):
        if isinstance(a, tuple):
            return pl.BlockSpec((tm, a[1]), lambda i, j, blk=a[2]: (i, blk))
        return pl.BlockSpec((tm, a.shape[1]), lambda i, j: (i, 0))

    def b_spec(b):
        arr, p = b if isinstance(b, tuple) else (b, None)
        kk = arr.shape[-1 if nt else -2]
        shape, idx = ((tn, kk), lambda j: (j, 0)) if nt else ((kk, tn), lambda j: (0, j))
        if p is None:
            return pl.BlockSpec(shape, lambda i, j: idx(j))
        return pl.BlockSpec((None,) + shape, lambda i, j, p=p: (p,) + idx(j))

    in_specs = [a_spec(a) for a in a_list] + [b_spec(b) for b in b_list]
    args = a_arrs + b_arrs
    if add is not None:
        in_specs.append(pl.BlockSpec((tm, tn), lambda i, j: (i, j)))
        args.append(add)
    return pl.pallas_call(
        body, name=name, grid=(m // tm, n // tn), in_specs=in_specs,
        out_specs=pl.BlockSpec((tm, tn), lambda i, j: (i, j)),
        out_shape=jax.ShapeDtypeStruct((m, n), out_dtype),
        compiler_params=_params(("parallel", "parallel")))(*args)


def matmul_cols(a, b3, *, name, out_dtype=F32, tm=1024):
    m, kk = a.shape
    p, _, nb = b3.shape
    tm, tn = min(tm, m), _pick(nb, 768)
    per = nb // tn

    def body(a_ref, b_ref, o_ref):
        o_ref[...] = _bd(a_ref[...], b_ref[...], 1, 0).astype(out_dtype)

    return pl.pallas_call(
        body, name=name, grid=(m // tm, p * per),
        in_specs=[pl.BlockSpec((tm, kk), lambda i, j: (i, 0)),
                  pl.BlockSpec((None, kk, tn), lambda i, j: (j // per, 0, j % per))],
        out_specs=pl.BlockSpec((tm, tn), lambda i, j: (i, j)),
        out_shape=jax.ShapeDtypeStruct((m, p * nb), out_dtype),
        compiler_params=_params(("parallel", "parallel")))(a, b3)


def matmul_tn(a, b, *, name, tm=1024, out_blocks=None):
    m, k = a.shape
    n = b.shape[1]
    nb = n // (out_blocks or 1)
    tm, tk, tn = min(tm, m), _pick(k), _pick(nb, 768 if out_blocks else 1536)
    per = nb // tn

    def body(a_ref, b_ref, o_ref):
        @pl.when(pl.program_id(2) == 0)
        def _():
            o_ref[...] = jnp.zeros_like(o_ref)

        o_ref[...] += _bd(a_ref[...], b_ref[...], 0, 0)

    if out_blocks:
        out_spec = pl.BlockSpec((None, tk, tn), lambda i, j, t: (j // per, i, j % per))
        out_shape = jax.ShapeDtypeStruct((out_blocks, k, nb), F32)
    else:
        out_spec = pl.BlockSpec((tk, tn), lambda i, j, t: (i, j))
        out_shape = jax.ShapeDtypeStruct((k, n), F32)
    return pl.pallas_call(
        body, name=name, grid=(k // tk, n // tn, m // tm),
        in_specs=[pl.BlockSpec((tm, tk), lambda i, j, t: (t, i)), pl.BlockSpec((tm, tn), lambda i, j, t: (t, j))],
        out_specs=out_spec, out_shape=out_shape,
        compiler_params=_params(("parallel", "parallel", "arbitrary")))(a, b)


def _row_spec(r, tm):
    if isinstance(r, tuple):
        arr, width, blk = r
        return arr, pl.BlockSpec((tm, width), lambda i, blk=blk: (i, blk))
    return r, pl.BlockSpec((tm, r.shape[1]), lambda i: (i, 0))


def _full_spec(p):
    return pl.BlockSpec(p.shape, lambda i: (0,) * p.ndim)


def rowmap_fwd(fn, rows, params, outs, *, name, tm=256):
    pairs = [_row_spec(r, tm) for r in rows]
    m = pairs[0][0].shape[0]
    tm = min(tm, m)
    pairs = [_row_spec(r, tm) for r in rows]
    nr, npar = len(rows), len(params)

    def body(*refs):
        res = fn(*[r[...] for r in refs[:nr + npar]])
        for o_ref, v in zip(refs[nr + npar:], res):
            o_ref[...] = v.astype(o_ref.dtype)

    return pl.pallas_call(
        body, name=name, grid=(m // tm,),
        in_specs=[s for _, s in pairs] + [_full_spec(p) for p in params],
        out_specs=[pl.BlockSpec((tm, c), lambda i: (i, 0)) for c, _ in outs],
        out_shape=[jax.ShapeDtypeStruct((m, c), dt) for c, dt in outs],
        compiler_params=_params(("parallel",)))(*[a for a, _ in pairs], *params)


def rowmap_bwd(fn, rows, params, cts, *, name, row_dtypes=None, add=None, tm=256):
    m = _row_spec(rows[0], tm)[0].shape[0]
    tm = min(tm, m)
    rp = [_row_spec(r, tm) for r in rows]
    cp = [_row_spec(c, tm) for c in cts]
    nr, npar, nc = len(rows), len(params), len(cts)
    row_dtypes = row_dtypes or [F32] * nr
    widths = [s.block_shape[1] for _, s in rp]

    def body(*refs):
        ins = [r[...] for r in refs[:nr + npar]]
        ins = [v.astype(F32) for v in ins]
        ct = tuple(r[...].astype(F32) for r in refs[nr + npar:nr + npar + nc])
        base = nr + npar + nc
        extra = None
        if add is not None:
            extra = refs[base][...]
            base += 1
        _, pull = jax.vjp(fn, *ins)
        grads = pull(ct)
        for j in range(nr):
            g = grads[j]
            if j == 0 and extra is not None:
                g = g + extra
            refs[base + j][...] = g.astype(refs[base + j].dtype)

        @pl.when(pl.program_id(0) == 0)
        def _():
            for j in range(npar):
                refs[base + nr + j][...] = jnp.zeros_like(refs[base + nr + j])

        for j in range(npar):
            refs[base + nr + j][...] += grads[nr + j]

    in_specs = [s for _, s in rp] + [_full_spec(p) for p in params] + [s for _, s in cp]
    args = [a for a, _ in rp] + list(params) + [a for a, _ in cp]
    if add is not None:
        in_specs.append(pl.BlockSpec((tm, widths[0]), lambda i: (i, 0)))
        args.append(add)
    out_specs = [pl.BlockSpec((tm, w), lambda i: (i, 0)) for w in widths] + [_full_spec(p) for p in params]
    out_shape = [jax.ShapeDtypeStruct((m, w), dt) for w, dt in zip(widths, row_dtypes)]
    out_shape += [jax.ShapeDtypeStruct(p.shape, F32) for p in params]
    return pl.pallas_call(
        body, name=name, grid=(m // tm,), in_specs=in_specs, out_specs=out_specs, out_shape=out_shape,
        compiler_params=_params(("arbitrary",)))(*args)


def loss_head(h, target, w, *, name, tm=256):
    m, d = h.shape
    tm = min(tm, m)

    def body(h_ref, t_ref, w_ref, loss_ref, dh_ref, dw_ref):
        y, pull = jax.vjp(_rms, h_ref[...], w_ref[...])
        err = y - t_ref[...]
        dh, dw = pull(err * (1.0 / d))

        @pl.when(pl.program_id(0) == 0)
        def _():
            loss_ref[...] = jnp.zeros_like(loss_ref)
            dw_ref[...] = jnp.zeros_like(dw_ref)

        loss_ref[...] += (0.5 / d) * jnp.sum(err * err, keepdims=True)
        dw_ref[...] += dw
        dh_ref[...] = dh

    row = pl.BlockSpec((tm, d), lambda i: (i, 0))
    return pl.pallas_call(
        body, name=name, grid=(m // tm,), in_specs=[row, row, _full_spec(w)],
        out_specs=[pl.BlockSpec((1, 1), lambda i: (0, 0)), row, _full_spec(w)],
        out_shape=[jax.ShapeDtypeStruct((1, 1), F32), jax.ShapeDtypeStruct((m, d), F32),
                   jax.ShapeDtypeStruct(w.shape, F32)],
        compiler_params=_params(("arbitrary",)))(h, target, w)


def _shift(x, s):
    if s == 0:
        return x
    n = x.shape[0]
    t = lax.broadcasted_iota(jnp.int32, x.shape, 0)
    rolled = pltpu.roll(x, (-s) % n, 0)
    return jnp.where((t + s >= 0) & (t + s < n), rolled, 0.0)


def _conv(x, w, b):
    k = w.shape[0]
    acc = b + w[k // 2:k // 2 + 1, :] * x
    for j in range(k):
        if j != k // 2:
            acc = acc + w[j:j + 1, :] * _shift(x, j - k // 2)
    return acc


def _conv_bwd(x, dc, w):
    k = w.shape[0]
    dx = None
    dws = []
    for j in range(k):
        s = j - k // 2
        term = w[j:j + 1, :] * _shift(dc, -s)
        dx = term if dx is None else dx + term
        dws.append(jnp.sum(dc * _shift(x, s), axis=0, keepdims=True))
    return dx, jnp.concatenate(dws, axis=0), jnp.sum(dc, axis=0, keepdims=True)


def _dsilu(c):
    s = jax.nn.sigmoid(c)
    return s * (1.0 + c * (1.0 - s))


def ssd_conv_fwd(xbc, w, b, *, bsz, name):
    t, c = xbc.shape
    seq, ct = t // bsz, 256

    def body(x_ref, w_ref, b_ref, o_ref):
        o_ref[...] = _silu(_conv(x_ref[...], w_ref[...], b_ref[...]))

    return pl.pallas_call(
        body, name=name, grid=(c // ct, bsz),
        in_specs=[pl.BlockSpec((seq, ct), lambda j, i: (i, j)), pl.BlockSpec((w.shape[0], ct), lambda j, i: (0, j)),
                  pl.BlockSpec((1, ct), lambda j, i: (0, j))],
        out_specs=pl.BlockSpec((seq, ct), lambda j, i: (i, j)),
        out_shape=jax.ShapeDtypeStruct((t, c), F32),
        compiler_params=_params(("parallel", "parallel")))(xbc, w, b)


def ssd_conv_bwd(xbc, dparts, w, b, *, bsz, name):
    t, c = xbc.shape
    seq, ct, k = t // bsz, 256, w.shape[0]
    starts = [0]
    for p in dparts:
        starts.append(starts[-1] + p.shape[1] // ct)

    def body(x_ref, *refs):
        g_refs, (w_ref, b_ref, dx_ref, dw_ref, db_ref) = refs[:len(dparts)], refs[len(dparts):]
        j = pl.program_id(0)
        g = g_refs[-1][...]
        for n in range(len(dparts) - 2, -1, -1):
            g = jnp.where(j < starts[n + 1], g_refs[n][...], g)
        x, wv = x_ref[...], w_ref[...]
        dc = g * _dsilu(_conv(x, wv, b_ref[...]))
        dx, dw, db = _conv_bwd(x, dc, wv)
        dx_ref[...] = dx

        @pl.when(pl.program_id(1) == 0)
        def _():
            dw_ref[...] = jnp.zeros_like(dw_ref)
            db_ref[...] = jnp.zeros_like(db_ref)

        dw_ref[...] += dw
        db_ref[...] += db

    def part_spec(n):
        lo, hi = starts[n], starts[n + 1]

        def index(j, i):
            inside = (j >= lo) & (j < hi)
            return jnp.where(inside, i, 0), jnp.where(inside, j - lo, 0)

        return pl.BlockSpec((seq, ct), index)

    blk = pl.BlockSpec((seq, ct), lambda j, i: (i, j))
    wspec, bspec = pl.BlockSpec((k, ct), lambda j, i: (0, j)), pl.BlockSpec((1, ct), lambda j, i: (0, j))
    return pl.pallas_call(
        body, name=name, grid=(c // ct, bsz),
        in_specs=[blk] + [part_spec(n) for n in range(len(dparts))] + [wspec, bspec], out_specs=[blk, wspec, bspec],
        out_shape=[jax.ShapeDtypeStruct((t, c), F32), jax.ShapeDtypeStruct((k, c), F32),
                   jax.ShapeDtypeStruct((1, c), F32)],
        compiler_params=_params(("parallel", "arbitrary")))(xbc, *dparts, w, b)


def _ffn_specs(seq, ct, k, nblk):
    val = pl.BlockSpec((seq, ct), lambda j, i: (i, j))
    gate = pl.BlockSpec((seq, ct), lambda j, i: (i, nblk + j))
    wv, wg = pl.BlockSpec((k, ct), lambda j, i: (0, j)), pl.BlockSpec((k, ct), lambda j, i: (0, nblk + j))
    bv, bg = pl.BlockSpec((1, ct), lambda j, i: (0, j)), pl.BlockSpec((1, ct), lambda j, i: (0, nblk + j))
    return val, gate, wv, wg, bv, bg


def ffn_act_fwd(up, w, b, *, bsz, name):
    t = up.shape[0]
    half = up.shape[1] // 2
    seq, ct, k = t // bsz, 256, w.shape[0]
    val, gate, wv, wg, bv, bg = _ffn_specs(seq, ct, k, half // ct)

    def body(v_ref, g_ref, wv_ref, wg_ref, bv_ref, bg_ref, o_ref):
        vc = _conv(v_ref[...], wv_ref[...], bv_ref[...])
        gc = _conv(g_ref[...], wg_ref[...], bg_ref[...])
        o_ref[...] = (_silu(gc) * vc).astype(BF16)

    return pl.pallas_call(
        body, name=name, grid=(half // ct, bsz), in_specs=[val, gate, wv, wg, bv, bg], out_specs=val,
        out_shape=jax.ShapeDtypeStruct((t, half), BF16),
        compiler_params=_params(("parallel", "parallel")))(up, up, w, w, b, b)


def ffn_act_bwd(up, dact, w, b, *, bsz, name):
    t = up.shape[0]
    half = up.shape[1] // 2
    seq, ct, k = t // bsz, 256, w.shape[0]
    val, gate, wv, wg, bv, bg = _ffn_specs(seq, ct, k, half // ct)

    def body(v_ref, g_ref, wv_ref, wg_ref, bv_ref, bg_ref, d_ref, dv_ref, dg_ref, dwv_ref, dwg_ref, dbv_ref, dbg_ref):
        v, g = v_ref[...], g_ref[...]
        vc = _conv(v, wv_ref[...], bv_ref[...])
        gc = _conv(g, wg_ref[...], bg_ref[...])
        d = d_ref[...].astype(F32)
        dv, dwv, dbv = _conv_bwd(v, d * _silu(gc), wv_ref[...])
        dg, dwg, dbg = _conv_bwd(g, d * vc * _dsilu(gc), wg_ref[...])
        dv_ref[...] = dv.astype(BF16)
        dg_ref[...] = dg.astype(BF16)

        @pl.when(pl.program_id(1) == 0)
        def _():
            for r in (dwv_ref, dwg_ref, dbv_ref, dbg_ref):
                r[...] = jnp.zeros_like(r)

        dwv_ref[...] += dwv
        dwg_ref[...] += dwg
        dbv_ref[...] += dbv
        dbg_ref[...] += dbg

    return pl.pallas_call(
        body, name=name, grid=(half // ct, bsz), in_specs=[val, gate, wv, wg, bv, bg, val],
        out_specs=[val, val, wv, wv, bv, bv],
        out_shape=[jax.ShapeDtypeStruct((t, half), BF16), jax.ShapeDtypeStruct((t, half), BF16),
                   jax.ShapeDtypeStruct((k, half), F32), jax.ShapeDtypeStruct((k, half), F32),
                   jax.ShapeDtypeStruct((1, half), F32), jax.ShapeDtypeStruct((1, half), F32)],
        compiler_params=_params(("parallel", "arbitrary")))(up, up, w, w, b, b, dact)


def _sel_row(a, h):
    oh = (lax.broadcasted_iota(jnp.int32, (a.shape[0], 1), 0) == h).astype(F32)
    return jnp.sum(a * oh, axis=0, keepdims=True)


def _ssd_chunk(xp, dtr, bm, cm, prev, bias_r, alog_r, dskip_r, rev):
    q = dtr.shape[1]
    ri = lax.broadcasted_iota(jnp.int32, (q, q), 0)
    ci = lax.broadcasted_iota(jnp.int32, (q, q), 1)
    mask = (ci >= ri) if rev else (ci <= ri)
    lane_lo, row_lo = ci < HDIM, ri < HDIM
    dt_r = _softplus(dtr + bias_r)
    dta_r = dt_r * (-jnp.exp(alog_r))
    cs_r = cum_row(dta_r, rev)
    scores = dot_nt(cm, bm)

    def per_row(v):
        return jnp.broadcast_to(v, (q, q)).T

    assert len(xp) == 2
    y_diag, csqs, decayed, tots = [], [], [], []
    for p in range(2):
        ha = 2 * p + (HPG if rev else 0)
        hb = ha + 1
        cs_a, cs_b = _sel_row(cs_r, ha), _sel_row(cs_r, hb)
        csq_a, csq_b = per_row(cs_a), per_row(cs_b)
        seg_a = jnp.exp(jnp.where(mask, csq_a - cs_a, -1e30))
        seg_b = jnp.exp(jnp.where(mask, csq_b - cs_b, -1e30))
        csq = jnp.where(lane_lo, csq_a, csq_b)
        xdt = xp[p] * jnp.where(lane_lo, per_row(_sel_row(dt_r, ha)), per_row(_sel_row(dt_r, hb)))
        tot_a = jnp.sum(_sel_row(dta_r, ha), axis=1, keepdims=True)
        tot_b = jnp.sum(_sel_row(dta_r, hb), axis=1, keepdims=True)
        y_diag.append(jnp.where(lane_lo, *dot2_nn(scores * seg_a, scores * seg_b, xdt)))
        csqs.append(csq)
        decayed.append(xdt * jnp.exp(jnp.where(lane_lo, tot_a, tot_b) - csq))
        tots.append((tot_a, tot_b, ha, hb))
    y_off = dot_nt2(cm, *prev)
    states = dot_tn2(*decayed, bm)
    ys, news = [], []
    for p, (tot_a, tot_b, ha, hb) in enumerate(tots):
        y = y_diag[p] + y_off[p] * jnp.exp(csqs[p])
        if not rev:
            y = y + jnp.where(lane_lo, _sel_row(dskip_r, ha), _sel_row(dskip_r, hb)) * xp[p]
        ys.append(y)
        news.append(jnp.exp(jnp.where(row_lo, tot_a, tot_b)) * prev[p] + states[p])
    return tuple(ys), tuple(news)


NPAIR = HPG // 2


def _ssd_specs(seq, nc):
    xs = pl.BlockSpec((None, seq, HPG * HDIM), lambda b, g: (b, 0, g))
    bm = pl.BlockSpec((None, seq, NSTATE), lambda b, g: (b, 0, SSD_W // NSTATE + g))
    cm = pl.BlockSpec((None, seq, NSTATE), lambda b, g: (b, 0, SSD_W // NSTATE + SGROUPS + g))
    dtr = pl.BlockSpec((None, None, 2 * HPG, seq), lambda b, g: (b, g, 0, 0))
    pr = pl.BlockSpec((None, 2 * HPG, 1), lambda b, g: (g, 0, 0))
    st = pl.BlockSpec((None, None, 2, nc, NPAIR, 2 * HDIM, NSTATE), lambda b, g: (b, g, 0, 0, 0, 0, 0))
    return xs, bm, cm, dtr, pr, st


def _pair_cols(p):
    return slice(2 * HDIM * p, 2 * HDIM * (p + 1))


def ssd_scan_fwd(act, dtr, prs, *, name):
    bsz, seq, _ = act.shape
    nc = seq // QC
    xs, bm, cm, dtrs, pr, st = _ssd_specs(seq, nc)

    def body(x_ref, b_ref, c_ref, dtr_ref, br_ref, ar_ref, dk_ref, y_ref, st_ref):
        par = (br_ref[...], ar_ref[...], dk_ref[...])
        y_ref[...] = jnp.zeros_like(y_ref)

        def step(i, carry):
            new = []
            for rev in (False, True):
                k = (nc - 1 - i) if rev else i
                rows = pl.ds(pl.multiple_of(k * QC, QC), QC)
                xp = tuple(x_ref[rows, _pair_cols(p)] for p in range(NPAIR))
                for p in range(NPAIR):
                    st_ref[int(rev), k, p] = carry[rev][p]
                ys, nw = _ssd_chunk(xp, dtr_ref[:, rows], b_ref[rows, :], c_ref[rows, :], carry[rev], *par, rev)
                for p in range(NPAIR):
                    y_ref[rows, _pair_cols(p)] += ys[p]
                new.append(nw)
            return tuple(new)

        zero = tuple(jnp.zeros((2 * HDIM, NSTATE), F32) for _ in range(NPAIR))
        lax.fori_loop(0, nc // 2, lambda i, c: step(2 * i + 1, step(2 * i, c)), (zero, zero))

    return pl.pallas_call(
        body, name=name, grid=(bsz, SGROUPS), in_specs=[xs, bm, cm, dtrs, pr, pr, pr], out_specs=[xs, st],
        out_shape=[jax.ShapeDtypeStruct((bsz, seq, SSD_W), F32),
                   jax.ShapeDtypeStruct((bsz, SGROUPS, 2, nc, NPAIR, 2 * HDIM, NSTATE), F32)],
        compiler_params=_params(("parallel", "parallel")))(act, act, act, dtr, *prs)


def ssd_scan_bwd(act, dtr, prs, states, dy, *, name):
    bsz, seq, _ = act.shape
    nc = seq // QC
    xs, bm, cm, dtrs, pr, st = _ssd_specs(seq, nc)
    grp = pl.BlockSpec((None, seq, NSTATE), lambda b, g: (b, 0, g))
    dpr = pl.BlockSpec((None, None, 2 * HPG, 1), lambda b, g: (b, g, 0, 0))

    def body(x_ref, b_ref, c_ref, dtr_ref, br_ref, ar_ref, dk_ref, st_ref, dy_ref,
             dx_ref, db_ref, dc_ref, ddtr_ref, gbr_ref, gar_ref, gdk_ref):
        par = (br_ref[...], ar_ref[...], dk_ref[...])
        pgrads = (gbr_ref, gar_ref, gdk_ref)
        for r in pgrads + (dx_ref, db_ref, dc_ref, ddtr_ref):
            r[...] = jnp.zeros_like(r)

        def bstep(i, dcarry):
            new = []
            for rev in (False, True):
                k = i if rev else (nc - 1 - i)
                rows = pl.ds(pl.multiple_of(k * QC, QC), QC)
                xp = tuple(x_ref[rows, _pair_cols(p)] for p in range(NPAIR))
                prev = tuple(st_ref[int(rev), k, p] for p in range(NPAIR))
                _, pull = jax.vjp(functools.partial(_ssd_chunk, rev=rev), xp, dtr_ref[:, rows], b_ref[rows, :],
                                  c_ref[rows, :], prev, *par)
                dyp = tuple(dy_ref[rows, _pair_cols(p)] for p in range(NPAIR))
                gx, gdt, gb, gc, gprev, *gpar = pull((dyp, dcarry[rev]))
                for p in range(NPAIR):
                    dx_ref[rows, _pair_cols(p)] += gx[p]
                ddtr_ref[:, rows] += gdt
                db_ref[rows, :] += gb
                dc_ref[rows, :] += gc
                for r, g in zip(pgrads, gpar):
                    r[...] += g
                new.append(gprev)
            return tuple(new)

        zero = tuple(jnp.zeros((2 * HDIM, NSTATE), F32) for _ in range(NPAIR))
        lax.fori_loop(0, nc, bstep, (zero, zero))

    out_shape = [jax.ShapeDtypeStruct((bsz, seq, SSD_W), F32),
                 jax.ShapeDtypeStruct((bsz, seq, SGROUPS * NSTATE), F32),
                 jax.ShapeDtypeStruct((bsz, seq, SGROUPS * NSTATE), F32),
                 jax.ShapeDtypeStruct(dtr.shape, F32)]
    out_shape += [jax.ShapeDtypeStruct((bsz, SGROUPS, 2 * HPG, 1), F32)] * 3
    return pl.pallas_call(
        body, name=name, grid=(bsz, SGROUPS), in_specs=[xs, bm, cm, dtrs, pr, pr, pr, st, xs],
        out_specs=[xs, grp, grp, dtrs, dpr, dpr, dpr], out_shape=out_shape,
        compiler_params=_params(("parallel", "parallel")))(act, act, act, dtr, *prs, states, dy)


def _s5_direction(lam_re, lam_im, log_step, b_re, b_im, c_re, c_im, rev):
    q = S5_Q
    step = jnp.exp(log_step)[:, None]
    lr, li = lam_re * step, lam_im * step
    mag = jnp.exp(lr)
    ar, ai = mag * jnp.cos(li), mag * jnp.sin(li)
    den = lam_re * lam_re + lam_im * lam_im
    cr = ((ar - 1.0) * lam_re + ai * lam_im) / den
    ci = (ai * lam_re - (ar - 1.0) * lam_im) / den
    bbr = cr[..., None] * b_re - ci[..., None] * b_im
    bbi = cr[..., None] * b_im + ci[..., None] * b_re
    d = jnp.arange(q + 1, dtype=F32)[None, :, None]
    pm = jnp.exp(d * lr[:, None, :])
    pr, pi = pm * jnp.cos(d * li[:, None, :]), pm * jnp.sin(d * li[:, None, :])
    er = pr[..., None] * bbr[:, None] - pi[..., None] * bbi[:, None]
    ei = pr[..., None] * bbi[:, None] + pi[..., None] * bbr[:, None]
    hp = lax.Precision.HIGHEST
    k = (jnp.einsum('gcp,gdpz->gdcz', c_re, er[:, :q], precision=hp)
         - jnp.einsum('gcp,gdpz->gdcz', c_im, ei[:, :q], precision=hp))
    e = jnp.concatenate([er[:, :q], ei[:, :q]], axis=2)
    wt = jnp.transpose(e if rev else e[:, ::-1], (0, 1, 3, 2))
    p1r, p1i = pr[:, 1:], pi[:, 1:]
    if rev:
        p1r, p1i = p1r[:, ::-1], p1i[:, ::-1]
    m_re = c_re[:, None] * p1r[:, :, None, :] - c_im[:, None] * p1i[:, :, None, :]
    m_im = -c_re[:, None] * p1i[:, :, None, :] - c_im[:, None] * p1r[:, :, None, :]
    mt = jnp.transpose(jnp.concatenate([m_re, m_im], axis=-1), (0, 3, 1, 2))
    da = jnp.concatenate([pr[:, q], pr[:, q]], axis=-1)
    db = jnp.concatenate([-pi[:, q], pi[:, q]], axis=-1)
    return k, wt, mt, da, db


def _s5_operators(lf_re, lf_im, lsf, lb_re, lb_im, lsb, b_re, b_im, cf_re, cf_im, cb_re, cb_im):
    kf, wtf, mtf, daf, dbf = _s5_direction(lf_re, lf_im, lsf, b_re, b_im, cf_re, cf_im, False)
    kb, wtb, mtb, dab, dbb = _s5_direction(lb_re, lb_im, lsb, b_re, b_im, cb_re, cb_im, True)
    g = kf.shape[0]
    lags = jnp.concatenate([kb[:, :0:-1], kf[:, :1] + kb[:, :1], kf[:, 1:]], axis=1)
    tt = jnp.transpose(lags, (0, 1, 3, 2))
    wt = jnp.concatenate([wtf.reshape(g, S5_QC, 2 * S5_P), wtb.reshape(g, S5_QC, 2 * S5_P)], axis=-1)
    mt = jnp.concatenate([mtf.reshape(g, 2 * S5_P, S5_QC), mtb.reshape(g, 2 * S5_P, S5_QC)], axis=1)
    return tt, wt, mt, jnp.concatenate([daf, dab], -1), jnp.concatenate([dbf, dbb], -1)


def _gspec(*shape):
    return pl.BlockSpec((None,) + shape, lambda g: (g,) + (0,) * len(shape))


S5_HALVES = S5_QC // LANES


def _toeplitz_block(s, t):
    per = LANES // S5_C
    return t // per, slice(s * S5_C, (s + 1) * S5_C), slice((t % per) * S5_C, (t % per + 1) * S5_C)


def s5_toeplitz(kt, *, name):
    g = kt.shape[0]

    def body(k_ref, t_ref):
        for s in range(S5_Q):
            for t in range(S5_Q):
                t_ref[_toeplitz_block(s, t)] = k_ref[t - s + S5_Q - 1]

    return pl.pallas_call(
        body, name=name, grid=(g,), in_specs=[_gspec(2 * S5_Q - 1, S5_C, S5_C)],
        out_specs=_gspec(S5_HALVES, S5_QC, LANES), out_shape=jax.ShapeDtypeStruct((g, S5_HALVES, S5_QC, LANES), F32),
        compiler_params=_params(("parallel",)))(kt)


def s5_toeplitz_bwd(dtt, *, name):
    g = dtt.shape[0]

    def body(d_ref, k_ref):
        for j in range(2 * S5_Q - 1):
            acc = None
            for s in range(S5_Q):
                t = j - (S5_Q - 1) + s
                if 0 <= t < S5_Q:
                    blk = d_ref[_toeplitz_block(s, t)]
                    acc = blk if acc is None else acc + blk
            k_ref[j] = acc

    return pl.pallas_call(
        body, name=name, grid=(g,), in_specs=[_gspec(S5_HALVES, S5_QC, LANES)],
        out_specs=_gspec(2 * S5_Q - 1, S5_C, S5_C), out_shape=jax.ShapeDtypeStruct((g, 2 * S5_Q - 1, S5_C, S5_C), F32),
        compiler_params=_params(("parallel",)))(dtt)


S5_RT = 64


def _chunk_piece(q):
    per = LANES // S5_C
    return q // per, slice((q % per) * S5_C, (q % per + 1) * S5_C)


def to_chunks(u, *, name):
    t = u.shape[0]
    r = t // S5_Q
    rt = min(S5_RT, r)

    per = LANES // S5_C
    nblk = S5_W // LANES

    def body(*refs):
        o_ref = refs[-1]
        for k in range(nblk):
            for q in range(S5_Q):
                rows = refs[k][pl.ds(q, rt, stride=S5_Q), :]
                half, lanes = _chunk_piece(q)
                for j in range(per):
                    o_ref[k * per + j, half, :, lanes] = rows[:, j * S5_C:(j + 1) * S5_C]

    return pl.pallas_call(
        body, name=name, grid=(r // rt,),
        in_specs=[pl.BlockSpec((rt * S5_Q, LANES), lambda i, k=k: (i, k)) for k in range(nblk)],
        out_specs=pl.BlockSpec((S5_G, S5_HALVES, rt, LANES), lambda i: (0, 0, i, 0)),
        out_shape=jax.ShapeDtypeStruct((S5_G, S5_HALVES, r, LANES), F32),
        compiler_params=_params(("parallel",)))(*[u] * nblk)


def from_chunks(y, *, name, add=None):
    r = y.shape[2]
    rt = min(S5_RT, r)
    per = LANES // S5_C

    nblk = S5_W // LANES

    def body(*refs):
        y_ref, tmp_ref = refs[0], refs[-1]
        adds, outs = refs[1:-1 - nblk], refs[-1 - nblk:-1]
        for k in range(nblk):
            for q in range(S5_Q):
                half, lanes = _chunk_piece(q)
                for j in range(per):
                    tmp_ref[:, j * S5_C:(j + 1) * S5_C] = y_ref[k * per + j, half, :, lanes]
                row = tmp_ref[...]
                if add is not None:
                    row = row + adds[k][pl.ds(q, rt, stride=S5_Q), :]
                outs[k][pl.ds(q, rt, stride=S5_Q), :] = row

    in_specs = [pl.BlockSpec((S5_G, S5_HALVES, rt, LANES), lambda i: (0, 0, i, 0))]
    if add is not None:
        in_specs += [pl.BlockSpec((rt * S5_Q, LANES), lambda i, k=k: (i, k)) for k in range(nblk)]
    blocks = pl.pallas_call(
        body, name=name, grid=(r // rt,), in_specs=in_specs,
        out_specs=[pl.BlockSpec((rt * S5_Q, LANES), lambda i: (i, 0))] * nblk,
        out_shape=[jax.ShapeDtypeStruct((r * S5_Q, LANES), F32)] * nblk,
        scratch_shapes=[pltpu.VMEM((rt, LANES), F32)],
        compiler_params=_params(("parallel",)))(*([y] if add is None else [y] + [add] * nblk))
    return jnp.concatenate(blocks, axis=1)


def _cat(ref):
    return jnp.concatenate([ref[h] for h in range(S5_HALVES)], axis=1)


def _put(ref, v):
    for h in range(S5_HALVES):
        ref[h] = v[:, h * LANES:(h + 1) * LANES]


def _cspec(r):
    return _gspec(S5_HALVES, r, LANES)


def s5_state_in(u, wt, *, name):
    g, _, r, _ = u.shape

    def body(u_ref, w_ref, o_ref):
        o_ref[...] = _bd(_cat(u_ref), w_ref[...], 1, 0)

    return pl.pallas_call(
        body, name=name, grid=(g,), in_specs=[_cspec(r), _gspec(S5_QC, 4 * S5_P)],
        out_specs=_gspec(r, 4 * S5_P), out_shape=jax.ShapeDtypeStruct((g, r, 4 * S5_P), F32),
        compiler_params=_params(("parallel",)))(u, wt)


def _swap(h):
    return pltpu.roll(h, S5_P, 1)


def s5_carry_fwd(s, da, db, *, name):
    nck, rows, _ = s.shape
    w = 2 * S5_P

    def body(s_ref, da_ref, db_ref, h_ref):
        dirs = ((False, slice(0, w)), (True, slice(w, 2 * w)))
        coef = [(da_ref[:, cols], db_ref[:, cols]) for _, cols in dirs]

        def step(i, hs):
            new = []
            for (rev, cols), (a, b), h in zip(dirs, coef, hs):
                k = (nck - 1 - i) if rev else i
                h_ref[k, :, cols] = h
                new.append(a * h + b * _swap(h) + s_ref[k, :, cols])
            return tuple(new)

        z = jnp.zeros((rows, w), F32)
        lax.fori_loop(0, nck, step, (z, z), unroll=2)

    rt = min(CARRY_ROWS, rows)
    big, small = pl.BlockSpec((nck, rt, 2 * w), lambda i: (0, i, 0)), pl.BlockSpec((rt, 2 * w), lambda i: (i, 0))
    rows = rt
    return pl.pallas_call(
        body, name=name, grid=(s.shape[1] // rt,), in_specs=[big, small, small], out_specs=big,
        out_shape=jax.ShapeDtypeStruct(s.shape, F32), compiler_params=_params(("parallel",)))(s, da, db)


def s5_carry_bwd(hin, dh, da, db, *, name):
    nck, rows, _ = hin.shape
    w = 2 * S5_P

    def body(h_ref, dh_ref, da_ref, db_ref, ds_ref, gda_ref, gdb_ref):
        dirs = ((False, slice(0, w)), (True, slice(w, 2 * w)))
        coef = [(da_ref[:, cols], db_ref[:, cols]) for _, cols in dirs]

        def step(i, carries):
            new = []
            for (rev, cols), (a, b), (g, ga, gb) in zip(dirs, coef, carries):
                k = i if rev else (nck - 1 - i)
                ds_ref[k, :, cols] = g
                h = h_ref[k, :, cols]
                new.append((dh_ref[k, :, cols] + a * g + _swap(b * g), ga + g * h, gb + g * _swap(h)))
            return tuple(new)

        z = jnp.zeros((rows, w), F32)
        res = lax.fori_loop(0, nck, step, ((z, z, z), (z, z, z)), unroll=2)
        for (_, cols), (_, ga, gb) in zip(dirs, res):
            gda_ref[:, cols] = ga
            gdb_ref[:, cols] = gb

    rt = min(CARRY_ROWS, rows)
    big, small = pl.BlockSpec((nck, rt, 2 * w), lambda i: (0, i, 0)), pl.BlockSpec((rt, 2 * w), lambda i: (i, 0))
    rows = rt
    return pl.pallas_call(
        body, name=name, grid=(hin.shape[1] // rt,), in_specs=[big, big, small, small], out_specs=[big, small, small],
        out_shape=[jax.ShapeDtypeStruct(hin.shape, F32), jax.ShapeDtypeStruct(da.shape, F32),
                   jax.ShapeDtypeStruct(da.shape, F32)],
        compiler_params=_params(("parallel",)))(hin, dh, da, db)


def s5_out(u, hin, tt, mt, *, name):
    g, _, r, _ = u.shape

    def body(u_ref, h_ref, t_ref, m_ref, o_ref):
        u_v, h_v = _cat(u_ref), h_ref[...]
        for half in range(S5_HALVES):
            cols = slice(half * LANES, (half + 1) * LANES)
            o_ref[half] = _bd(u_v, t_ref[half], 1, 0) + _bd(h_v, m_ref[:, cols], 1, 0)

    return pl.pallas_call(
        body, name=name, grid=(g,),
        in_specs=[_cspec(r), _gspec(r, 4 * S5_P), _gspec(S5_HALVES, S5_QC, LANES), _gspec(4 * S5_P, S5_QC)],
        out_specs=_cspec(r), out_shape=jax.ShapeDtypeStruct((g, S5_HALVES, r, LANES), F32),
        compiler_params=_params(("parallel",)))(u, hin, tt, mt)


def s5_out_bwd(dy, u, hin, tt, mt, *, name):
    g, _, r, _ = u.shape

    def body(dy_ref, u_ref, h_ref, t_ref, m_ref, dh_ref, dt_ref, dm_ref, du_ref):
        dy_v, u_v = _cat(dy_ref), _cat(u_ref)
        dh_ref[...] = _bd(dy_v, m_ref[...], 1, 1)
        dm_ref[...] = _bd(h_ref[...], dy_v, 0, 0)
        du = None
        for half in range(S5_HALVES):
            dy_h = dy_ref[half]
            dt_ref[half] = _bd(u_v, dy_h, 0, 0)
            part = _bd(dy_h, t_ref[half], 1, 1)
            du = part if du is None else du + part
        _put(du_ref, du)

    tspec = _gspec(S5_HALVES, S5_QC, LANES)
    return pl.pallas_call(
        body, name=name, grid=(g,),
        in_specs=[_cspec(r), _cspec(r), _gspec(r, 4 * S5_P), tspec, _gspec(4 * S5_P, S5_QC)],
        out_specs=[_gspec(r, 4 * S5_P), tspec, _gspec(4 * S5_P, S5_QC), _cspec(r)],
        out_shape=[jax.ShapeDtypeStruct((g, r, 4 * S5_P), F32), jax.ShapeDtypeStruct((g, S5_HALVES, S5_QC, LANES), F32),
                   jax.ShapeDtypeStruct((g, 4 * S5_P, S5_QC), F32), jax.ShapeDtypeStruct((g, S5_HALVES, r, LANES), F32)],
        compiler_params=_params(("parallel",)))(dy, u, hin, tt, mt)


def s5_state_in_bwd(ds, u, wt, du1, *, name):
    g, _, r, _ = u.shape

    def body(ds_ref, u_ref, w_ref, du1_ref, du_ref, dw_ref):
        ds_v = ds_ref[...]
        _put(du_ref, _cat(du1_ref) + _bd(ds_v, w_ref[...], 1, 1))
        dw_ref[...] = _bd(_cat(u_ref), ds_v, 0, 0)

    return pl.pallas_call(
        body, name=name, grid=(g,),
        in_specs=[_gspec(r, 4 * S5_P), _cspec(r), _gspec(S5_QC, 4 * S5_P), _cspec(r)],
        out_specs=[_cspec(r), _gspec(S5_QC, 4 * S5_P)],
        out_shape=[jax.ShapeDtypeStruct((g, S5_HALVES, r, LANES), F32), jax.ShapeDtypeStruct((g, S5_QC, 4 * S5_P), F32)],
        compiler_params=_params(("parallel",)))(ds, u, wt, du1)


def _s5_post(ypre, u, dvec, wv, wg, bv, bg, nw):
    g = _gelu(ypre + dvec * u)
    out = (dot_nn(g, wv) + bv) * jax.nn.sigmoid(dot_nn(g, wg) + bg)
    return (_rms(out, nw),)


def _ssd_post(y, z, nw):
    return (_rms(y * _silu(z), nw),)


def _to_carry(s, bsz):
    nck = s.shape[1] // bsz
    return jnp.transpose(s.reshape(S5_G, bsz, nck, -1), (2, 0, 1, 3)).reshape(nck, S5_G * bsz, -1)


def _from_carry(h, bsz):
    nck = h.shape[0]
    return jnp.transpose(h.reshape(nck, S5_G, bsz, -1), (1, 2, 0, 3)).reshape(S5_G, bsz * nck, -1)


def _block_diag(w):
    eye = jnp.eye(S5_G, dtype=w.dtype)
    return jnp.einsum('gcd,gh->gchd', w, eye).reshape(S5_W, S5_W)


def _diag_blocks(w):
    v = w.reshape(S5_G, S5_C, S5_G, S5_C)
    return v[jnp.arange(S5_G), :, jnp.arange(S5_G), :]


def _dt_rows(dt, bsz):
    seq = dt.shape[0] // bsz
    return jnp.transpose(dt.reshape(bsz, seq, 2, SGROUPS, HPG), (0, 3, 2, 4, 1)).reshape(bsz, SGROUPS, 2 * HPG, seq)


def _dt_from_rows(dr):
    bsz, _, _, seq = dr.shape
    return jnp.transpose(dr.reshape(bsz, SGROUPS, 2, HPG, seq), (0, 4, 2, 1, 3)).reshape(bsz * seq, 2 * HEADS)


def _head_params(f, b):
    return jnp.concatenate([f.reshape(SGROUPS, HPG), b.reshape(SGROUPS, HPG)], axis=1)[:, :, None]


def _head_grads(gr):
    v = gr.sum(0)[:, :, 0]
    return v[:, :HPG].reshape(HEADS), v[:, HPG:].reshape(HEADS)


def local_step(x, target, w):
    bsz, seq, d = x.shape
    t = bsz * seq
    x2, tgt2 = x.reshape(t, d), target.reshape(t, d)
    g = {}
    row = lambda v: v.reshape(1, -1)
    bf = lambda v: v.astype(BF16)

    w_in = _unshard(bf(w['w_in']), SHARDED['w_in'])
    cuts = [0, SSD_W, SSD_W + XBC_W, SSD_W + XBC_W + 2 * HEADS, w_in.shape[1]]
    w_in_parts = [w_in[:, a:b] for a, b in zip(cuts[:-1], cuts[1:])]
    norm_mix = row(w['norm_mix_w']) + w.get('token', 0.0)
    (hn,) = rowmap_fwd(lambda a, nw: (_rms(a, nw),), [x2], [norm_mix], [(d, BF16)], name="rms_mix")
    z, xbc, dt, u = [matmul_sum([hn], [p], tm=1024, name=f"in_proj_{i}") for i, p in enumerate(w_in_parts)]

    conv_w, conv_b = _unshard(w['ssd_conv_w'], SHARDED['ssd_conv_w']), row(w['ssd_conv_b'])
    act = ssd_conv_fwd(xbc, conv_w, conv_b, bsz=bsz, name="ssd_conv")
    dtr = _dt_rows(dt, bsz)
    prs = (_head_params(w['ssd_dt_bias_fwd'], w['ssd_dt_bias_bwd']),
           _head_params(w['ssd_a_log_fwd'], w['ssd_a_log_bwd']),
           _head_params(w['ssd_d'], jnp.zeros_like(w['ssd_d'])))
    act3 = act.reshape(bsz, seq, XBC_W)
    y_scan, ssd_states = ssd_scan_fwd(act3, dtr, prs, name="ssd_scan")
    y_scan = y_scan.reshape(t, SSD_W)
    ssd_nw = row(w['ssd_norm_w'])
    (y_ssd,) = rowmap_fwd(_ssd_post, [y_scan, z], [ssd_nw], [(SSD_W, BF16)], name="ssd_post")

    s5_names = ['s5_lambda_re_fwd', 's5_lambda_im_fwd', 's5_log_step_fwd', 's5_lambda_re_bwd', 's5_lambda_im_bwd',
                's5_log_step_bwd', 's5_b_re', 's5_b_im', 's5_c_re_fwd', 's5_c_im_fwd', 's5_c_re_bwd', 's5_c_im_bwd']
    (kt, wt, mt, da, db), s5_pull = jax.vjp(_s5_operators, *[w[n] for n in s5_names])
    tt_b, wt_b, mt_b = s5_toeplitz(kt, name="s5_toeplitz"), bf(wt), bf(mt)
    da_r, db_r = jnp.repeat(da, bsz, axis=0), jnp.repeat(db, bsz, axis=0)
    uc = to_chunks(u, name="s5_to_chunks_u")
    s_in = _to_carry(s5_state_in(uc, wt_b, name="s5_state_in"), bsz)
    hin_c = s5_carry_fwd(s_in, da_r, db_r, name="s5_carry")
    hin = _from_carry(hin_c, bsz)
    ypre = from_chunks(s5_out(uc, hin, tt_b, mt_b, name="s5_out"), name="s5_from_chunks_y")
    glu_w = w['s5_glu_w']
    s5_par = [row(w['s5_d']), _block_diag(glu_w[:, :, :S5_C]), _block_diag(glu_w[:, :, S5_C:]),
              row(w['s5_glu_b'][:, :S5_C]), row(w['s5_glu_b'][:, S5_C:]), row(w['s5_norm_w'])]
    (y_s5,) = rowmap_fwd(_s5_post, [ypre, u], s5_par, [(S5_W, BF16)], name="s5_post")

    if 'late' in w:
        w = {**w, **w['late'](y_s5)}
    w_out = bf(w['w_out']).reshape(SSD_W + S5_W, d)
    h1 = matmul_sum([y_ssd, y_s5], [w_out[:SSD_W], w_out[SSD_W:]], add=x2, name="out_proj")
    norm_ffn = row(w['norm_ffn_w'])
    (hn2,) = rowmap_fwd(lambda a, nw: (_rms(a, nw),), [h1], [norm_ffn], [(d, BF16)], name="rms_ffn")
    pad_c = FFN_PAD - FFN_BLK
    half = N_DEV // 2
    w_up3 = jnp.pad(bf(w['ffn_w_up']), ((0, 0), (0, 0), (0, pad_c)))
    w_down = jnp.pad(bf(w['ffn_w_down']).reshape(half, FFN_BLK, d), ((0, 0), (0, pad_c), (0, 0)))
    w_down = w_down.reshape(half * FFN_PAD, d)
    fconv_w = jnp.pad(w['ffn_conv_w'], ((0, 0), (0, 0), (0, pad_c)))
    fconv_w = jnp.transpose(fconv_w, (1, 0, 2)).reshape(FCONV, N_DEV * FFN_PAD)
    fconv_b = row(jnp.pad(w['ffn_conv_b'].reshape(N_DEV, FFN_BLK), ((0, 0), (0, pad_c))))
    up = matmul_cols(hn2, w_up3, name="ffn_up")
    fact = ffn_act_fwd(up, fconv_w, fconv_b, bsz=bsz, name="ffn_act")
    h2 = matmul_sum([fact], [w_down], add=h1, name="ffn_down")
    loss, dh2, g_nf = loss_head(h2, tgt2, row(w['norm_final_w']), name="loss_head")
    g['norm_final_w'] = g_nf.reshape(-1)

    dfact = matmul_sum([dh2], [w_down], nt=True, tm=1024, name="ffn_down_dx")
    g_down = matmul_tn(fact, dh2, name="ffn_down_dw").reshape(half, FFN_PAD, d)[:, :FFN_BLK]
    g['ffn_w_down'] = g_down.reshape(N_DEV, FFN_BLK // 2, d)
    dval, dgate, dwv, dwg, dbv, dbg = ffn_act_bwd(up, dfact, fconv_w, fconv_b, bsz=bsz, name="ffn_act_bwd")
    g_cw = jnp.concatenate([dwv, dwg], axis=1).reshape(FCONV, N_DEV, FFN_PAD)[:, :, :FFN_BLK]
    g['ffn_conv_w'] = jnp.transpose(g_cw, (1, 0, 2))
    g['ffn_conv_b'] = jnp.concatenate([dbv, dbg], axis=1).reshape(N_DEV, FFN_PAD)[:, :FFN_BLK].reshape(-1)
    windows = [(dval, FFN_PAD, p) for p in range(half)] + [(dgate, FFN_PAD, p) for p in range(half)]
    dhn2 = matmul_sum(windows, [(w_up3, p) for p in range(N_DEV)], nt=True, name="ffn_up_dx")
    g['ffn_w_up'] = jnp.concatenate([matmul_tn(hn2, dval, out_blocks=half, name="ffn_up_dw_val"),
                                     matmul_tn(hn2, dgate, out_blocks=half, name="ffn_up_dw_gate")],
                                    axis=0)[:, :, :FFN_BLK]
    send_early = w.get('on_grads')
    if send_early:
        norm_ffn = norm_ffn + send_early(g, ['ffn_w_up', 'ffn_w_down'])
    dh1, g_nffn = rowmap_bwd(lambda a, nw: (_rms(a, nw),), [h1], [norm_ffn], [dhn2], add=dh2, name="rms_ffn_bwd")
    g['norm_ffn_w'] = g_nffn.reshape(-1)

    dycat = matmul_sum([dh1], [w_out], nt=True, tm=1024, name="out_proj_dx")
    g['w_out'] = jnp.concatenate([matmul_tn(y_ssd, dh1, name="out_proj_dw_ssd"),
                                  matmul_tn(y_s5, dh1, name="out_proj_dw_s5")], axis=0).reshape(w['w_out'].shape)
    if send_early:
        ssd_nw = ssd_nw + send_early(g, ['w_out'])
    dy_scan, dz, g_snw = rowmap_bwd(_ssd_post, [y_scan, z], [ssd_nw], [(dycat, SSD_W, 0)], name="ssd_post_bwd")
    g['ssd_norm_w'] = g_snw.reshape(-1)
    dypre, du_a, g_d, g_wv, g_wg, g_bv, g_bg, g_s5nw = rowmap_bwd(
        _s5_post, [ypre, u], s5_par, [(dycat, S5_W, SSD_W // S5_W)], name="s5_post_bwd")
    g['s5_d'], g['s5_norm_w'] = g_d.reshape(-1), g_s5nw.reshape(-1)
    g['s5_glu_w'] = jnp.concatenate([_diag_blocks(g_wv), _diag_blocks(g_wg)], axis=-1)
    g['s5_glu_b'] = jnp.concatenate([g_bv.reshape(S5_G, S5_C), g_bg.reshape(S5_G, S5_C)], axis=-1)

    dyc = to_chunks(dypre, name="s5_to_chunks_dy")
    dhin, dtt, dmt, du1 = s5_out_bwd(dyc, uc, hin, tt_b, mt_b, name="s5_out_bwd")
    ds_c, gda, gdb = s5_carry_bwd(hin_c, _to_carry(dhin, bsz), da_r, db_r, name="s5_carry_bwd")
    duc, dwt = s5_state_in_bwd(_from_carry(ds_c, bsz), uc, wt_b, du1, name="s5_state_in_bwd")
    du = from_chunks(duc, add=du_a, name="s5_from_chunks_du")
    fold = lambda v: v.reshape(S5_G, bsz, -1).sum(1)
    dkt = s5_toeplitz_bwd(dtt, name="s5_toeplitz_bwd")
    for n, gv in zip(s5_names, s5_pull((dkt, dwt, dmt, fold(gda), fold(gdb)))):
        g[n] = gv

    dxs, dbm, dcm, ddtr, gbr, gar, gdk = ssd_scan_bwd(
        act3, dtr, prs, ssd_states, dy_scan.reshape(bsz, seq, SSD_W), name="ssd_scan_bwd")
    g['ssd_dt_bias_fwd'], g['ssd_dt_bias_bwd'] = _head_grads(gbr)
    g['ssd_a_log_fwd'], g['ssd_a_log_bwd'] = _head_grads(gar)
    g['ssd_d'] = _head_grads(gdk)[0]
    dparts_act = [v.reshape(t, v.shape[-1]) for v in (dxs, dbm, dcm)]
    dxbc, g_cw, g_cb = ssd_conv_bwd(xbc, dparts_act, conv_w, conv_b, bsz=bsz, name="ssd_conv_bwd")
    g['ssd_conv_w'] = _shard_rows(g_cw, SHARDED['ssd_conv_w']).reshape(w['ssd_conv_w'].shape)
    g['ssd_conv_b'] = g_cb.reshape(-1)
    ddt = _dt_from_rows(ddtr)

    dparts = [dz, dxbc, ddt, du]
    g_in = jnp.concatenate([matmul_tn(hn, dp, name=f"in_proj_dw_{i}") for i, dp in enumerate(dparts)], axis=1)
    g['w_in'] = _shard_rows(g_in, SHARDED['w_in']).reshape(w['w_in'].shape)
    if send_early:
        dparts[2] = ddt + send_early(g, ['w_in'], loss=loss)
    dhn = matmul_sum(dparts, w_in_parts, nt=True, name="in_proj_dx")
    dx, g_nmix = rowmap_bwd(lambda a, nw: (_rms(a, nw),), [x2], [norm_mix], [dhn], add=dh1, name="rms_mix_bwd")
    g['norm_mix_w'] = g_nmix.reshape(-1)
    return loss, dx.reshape(bsz, seq, d), g


ANY = pl.BlockSpec(memory_space=pl.ANY)


def all_gather(shards, *, name):
    n = len(shards)

    def body(*refs):
        x_refs, out_refs = refs[:n], refs[n:2 * n]
        send_sems, recv_sems, local_sems = refs[2 * n:]
        x, y, c = lax.axis_index("x"), lax.axis_index("y"), lax.axis_index("c")
        me, sibling = (x, y, c), (x, y, 1 - c)
        chips = [(1 - x, y), (x, 1 - y), (1 - x, 1 - y)]

        def copy(k, j, block, to, own=False):
            dst = out_refs[j].at[4 * block[0] + 2 * block[1] + block[2]]
            return pltpu.make_async_remote_copy(
                src_ref=x_refs[j] if own else dst, dst_ref=dst,
                send_sem=send_sems.at[k, j], recv_sem=recv_sems.at[k, j], device_id=to, device_id_type=MESH)

        mine = [pltpu.make_async_copy(x_refs[j], out_refs[j].at[4 * x + 2 * y + c], local_sems.at[j]) for j in range(n)]
        first = [copy(0, j, me, sibling, own=True) for j in range(n)]
        first += [copy(1 + i, j, me, (*chip, c), own=True) for i, chip in enumerate(chips) for j in range(n)]
        for cp in mine + first:
            cp.start()
        passed = []
        for i, chip in enumerate(chips):
            for j in range(n):
                copy(1 + i, j, (*chip, c), me).wait_recv()
                passed.append(copy(4 + i, j, (*chip, c), sibling))
                passed[-1].start()
        for j in range(n):
            copy(0, j, sibling, me).wait_recv()
        for i, chip in enumerate(chips):
            for j in range(n):
                copy(4 + i, j, (*chip, 1 - c), me).wait_recv()
        for cp in first + passed:
            cp.wait_send()
        for cp in mine:
            cp.wait()

    return pl.pallas_call(
        body, name=name, out_shape=[jax.ShapeDtypeStruct((N_DEV,) + s.shape, s.dtype) for s in shards],
        in_specs=[ANY] * n, out_specs=[ANY] * n,
        scratch_shapes=[pltpu.SemaphoreType.DMA((7, n)), pltpu.SemaphoreType.DMA((7, n)),
                        pltpu.SemaphoreType.DMA((n,))],
    )(*shards)


def exchange(sends, *, name):
    n = len(sends)

    def body(*refs):
        send_refs, recv_refs = refs[:n], refs[n:2 * n]
        send_sems, recv_sems, local_sems = refs[2 * n:]
        x, y, c = lax.axis_index("x"), lax.axis_index("y"), lax.axis_index("c")
        me = 4 * x + 2 * y + c
        local = [pltpu.make_async_copy(send_refs[j].at[me], recv_refs[j].at[me], local_sems.at[j]) for j in range(n)]
        for cp in local:
            cp.start()
        copies = []
        for k in range(1, N_DEV):
            px = (1 - x) if k & 4 else x
            py = (1 - y) if k & 2 else y
            pc = (1 - c) if k & 1 else c
            for j in range(n):
                copies.append(pltpu.make_async_remote_copy(
                    src_ref=send_refs[j].at[4 * px + 2 * py + pc], dst_ref=recv_refs[j].at[me],
                    send_sem=send_sems.at[k - 1, j], recv_sem=recv_sems.at[k - 1, j],
                    device_id=(px, py, pc), device_id_type=MESH))
        for cp in copies:
            cp.start()
        for cp in copies:
            cp.wait()
        for cp in local:
            cp.wait()

    return pl.pallas_call(
        body, name=name, out_shape=[jax.ShapeDtypeStruct(s.shape, s.dtype) for s in sends],
        in_specs=[ANY] * n, out_specs=[ANY] * n,
        scratch_shapes=[pltpu.SemaphoreType.DMA((N_DEV - 1, n)), pltpu.SemaphoreType.DMA((N_DEV - 1, n)),
                        pltpu.SemaphoreType.DMA((n,))],
    )(*sends)


HBM_SPEC = pl.BlockSpec(memory_space=pltpu.HBM)
SEM_SPEC = pl.BlockSpec(memory_space=pltpu.SEMAPHORE)
SPLIT_PARAMS = pltpu.CompilerParams(has_side_effects=pltpu.SideEffectType.DATAFLOW_SIDE_EFFECTING)


def _peer_copies(src_refs, land_refs, send_sems, recv_sems, indexed):
    x, y, c = lax.axis_index("x"), lax.axis_index("y"), lax.axis_index("c")
    me = 4 * x + 2 * y + c
    copies = []
    for k in range(1, N_DEV):
        px = (1 - x) if k & 4 else x
        py = (1 - y) if k & 2 else y
        pc = (1 - c) if k & 1 else c
        for j, (src, land) in enumerate(zip(src_refs, land_refs)):
            sem = (k - 1) * len(src_refs) + j
            copies.append(pltpu.make_async_remote_copy(
                src_ref=src.at[4 * px + 2 * py + pc] if indexed else src, dst_ref=land.at[me],
                send_sem=send_sems.at[sem], recv_sem=recv_sems.at[sem],
                device_id=(px, py, pc), device_id_type=MESH))
    return copies


def scatter_start(srcs, *, name, indexed):
    n = len(srcs)
    lands = [lax.empty(s.shape if indexed else (N_DEV,) + s.shape, s.dtype) for s in srcs]

    def body(*refs):
        send_sems, recv_sems = refs[2 * n], refs[2 * n + 1]
        for cp in _peer_copies(refs[:n], refs[n:2 * n], send_sems, recv_sems, indexed):
            cp.start()
        refs[-1][...] = jnp.zeros_like(refs[-1])

    hbm = lambda a: pltpu.HBM(a.shape, a.dtype)
    sems = pltpu.SemaphoreType.DMA(((N_DEV - 1) * n,))
    res = pl.pallas_call(
        body, name=name,
        out_shape=(sems, sems, *[hbm(a) for a in srcs + lands], jax.ShapeDtypeStruct((8, LANES), F32)),
        in_specs=[HBM_SPEC] * (2 * n),
        out_specs=(SEM_SPEC, SEM_SPEC, *[HBM_SPEC] * (2 * n), pl.BlockSpec(memory_space=pltpu.VMEM)),
        input_output_aliases={i: 2 + i for i in range(2 * n)}, compiler_params=SPLIT_PARAMS,
    )(*[pltpu.with_memory_space_constraint(a, pltpu.HBM) for a in srcs + lands])
    return res[0], res[1], list(res[2:2 + n]), list(res[2 + n:2 + 2 * n]), res[-1]


def scatter_wait(send_sems, recv_sems, srcs, lands, after, *, name, indexed):
    n = len(srcs)

    def body(*refs):
        for cp in _peer_copies(refs[:n], refs[n:2 * n], refs[2 * n], refs[2 * n + 1], indexed):
            cp.wait_send()
            cp.wait_recv()

    hbm = lambda a: pltpu.HBM(a.shape, a.dtype)
    res = pl.pallas_call(
        body, name=name, out_shape=tuple(hbm(a) for a in srcs + lands),
        in_specs=[HBM_SPEC] * (2 * n) + [SEM_SPEC, SEM_SPEC, ANY], out_specs=tuple([HBM_SPEC] * (2 * n)),
        input_output_aliases={i: i for i in range(2 * n)}, compiler_params=SPLIT_PARAMS,
    )(*srcs, *lands, send_sems, recv_sems, after)
    return list(res[:n]), list(res[n:])


def _adam_rows(r, c):
    fits = [t for t in range(8, r + 1, 8) if r % t == 0 and N_DEV * t * c * 4 <= 6 * 2 ** 20]
    return max(fits) if fits else r


def adamw(recv, w, m, v, *, name):
    _, r, n = recv.shape
    tr = _adam_rows(r, n)

    def body(r_ref, w_ref, m_ref, v_ref, g_ref, d_ref, nm_ref, nv_ref):
        g = r_ref[0].astype(F32)
        for s in range(1, N_DEV):
            g = g + r_ref[s].astype(F32)
        m_new = ADAM_B1 * m_ref[...] + (1.0 - ADAM_B1) * g
        v_new = ADAM_B2 * v_ref[...] + (1.0 - ADAM_B2) * jnp.square(g)
        m_hat = m_new / (1.0 - ADAM_B1 ** ADAM_STEP)
        v_hat = v_new / (1.0 - ADAM_B2 ** ADAM_STEP)
        g_ref[...] = g
        d_ref[...] = -ADAM_LR * (m_hat / (jnp.sqrt(v_hat) + ADAM_EPS) + ADAM_WD * w_ref[...])
        nm_ref[...] = m_new
        nv_ref[...] = v_new

    blk = pl.BlockSpec((tr, n), lambda i: (i, 0))
    return pl.pallas_call(
        body, name=name, grid=(r // tr,), in_specs=[pl.BlockSpec((N_DEV, tr, n), lambda i: (0, i, 0)), blk, blk, blk],
        out_specs=[blk] * 4, out_shape=[jax.ShapeDtypeStruct((r, n), F32)] * 4,
        compiler_params=_params(("parallel",)))(recv, w, m, v)


def _shard_rows(full, axis):
    if axis == 0:
        return full.reshape(N_DEV, -1)
    r, c = full.shape
    return jnp.transpose(full.reshape(r, N_DEV, c // N_DEV), (1, 0, 2)).reshape(N_DEV, -1)


def _unshard(blocks, axis):
    if axis == 0:
        return blocks.reshape(-1, blocks.shape[-1])
    return jnp.transpose(blocks, (1, 0, 2)).reshape(blocks.shape[1], -1)


def kernel(x, norm_mix_w, w_in, ssd_conv_w, ssd_conv_b, ssd_dt_bias_fwd, ssd_dt_bias_bwd, ssd_a_log_fwd, ssd_a_log_bwd, ssd_d, ssd_norm_w, s5_lambda_re_fwd, s5_lambda_im_fwd, s5_log_step_fwd, s5_lambda_re_bwd, s5_lambda_im_bwd, s5_log_step_bwd, s5_b_re, s5_b_im, s5_c_re_fwd, s5_c_im_fwd, s5_c_re_bwd, s5_c_im_bwd, s5_d, s5_glu_w, s5_glu_b, s5_norm_w, w_out, norm_ffn_w, ffn_w_up, ffn_conv_w, ffn_conv_b, ffn_w_down, norm_final_w, loss_target, m_norm_mix_w, m_w_in, m_ssd_conv_w, m_ssd_conv_b, m_ssd_dt_bias_fwd, m_ssd_dt_bias_bwd, m_ssd_a_log_fwd, m_ssd_a_log_bwd, m_ssd_d, m_ssd_norm_w, m_s5_lambda_re_fwd, m_s5_lambda_im_fwd, m_s5_log_step_fwd, m_s5_lambda_re_bwd, m_s5_lambda_im_bwd, m_s5_log_step_bwd, m_s5_b_re, m_s5_b_im, m_s5_c_re_fwd, m_s5_c_im_fwd, m_s5_c_re_bwd, m_s5_c_im_bwd, m_s5_d, m_s5_glu_w, m_s5_glu_b, m_s5_norm_w, m_w_out, m_norm_ffn_w, m_ffn_w_up, m_ffn_conv_w, m_ffn_conv_b, m_ffn_w_down, m_norm_final_w, v_norm_mix_w, v_w_in, v_ssd_conv_w, v_ssd_conv_b, v_ssd_dt_bias_fwd, v_ssd_dt_bias_bwd, v_ssd_a_log_fwd, v_ssd_a_log_bwd, v_ssd_d, v_ssd_norm_w, v_s5_lambda_re_fwd, v_s5_lambda_im_fwd, v_s5_log_step_fwd, v_s5_lambda_re_bwd, v_s5_lambda_im_bwd, v_s5_log_step_bwd, v_s5_b_re, v_s5_b_im, v_s5_c_re_fwd, v_s5_c_im_fwd, v_s5_c_re_bwd, v_s5_c_im_bwd, v_s5_d, v_s5_glu_w, v_s5_glu_b, v_s5_norm_w, v_w_out, v_norm_ffn_w, v_ffn_w_up, v_ffn_conv_w, v_ffn_conv_b, v_ffn_w_down, v_norm_final_w):
    args = dict(locals())
    strip = lambda n, v: v if n == 'norm_final_w' else v[0]
    w = {n: strip(n, args[n]) for n in WEIGHTS}

    mats = ['w_in', 'w_out', 'ffn_w_up', 'ffn_w_down']
    convs = ['ssd_conv_w', 'ffn_conv_w']
    shard = lambda n: w[n].astype(BF16) if n in mats else w[n]
    early, late = ['w_in', 'ssd_conv_w'], ['w_out', 'ffn_w_up', 'ffn_w_down', 'ffn_conv_w']
    full = dict(w)
    full.update(zip(early, all_gather([shard(n) for n in early], name="weight_all_gather")))
    ssem, rsem, src_thru, land_thru, token = scatter_start([shard(n) for n in late], name="weight_gather_start",
                                                           indexed=False)
    me = 4 * lax.axis_index("x") + 2 * lax.axis_index("y") + lax.axis_index("c")

    def late_weights(after):
        own, landed = scatter_wait(ssem, rsem, src_thru, land_thru, after, name="weight_gather_wait", indexed=False)
        return {n: lax.dynamic_update_index_in_dim(l, o, me, 0) for n, o, l in zip(late, own, landed)}

    full['late'], full['token'] = late_weights, token[:1, :1]

    pending = []
    last = 'norm_mix_w'
    small = convs + [n for n in WEIGHTS if n not in SHARDED and n != last]
    total = sum(w[n].size for n in small) + 1
    nrow = -(-total // (PACK_ROWS * LANES)) * PACK_ROWS

    def send_early(grads, names, loss=None):
        srcs = [grads[n].astype(BF16) for n in names]
        if loss is not None:
            pieces = [grads[n].reshape(N_DEV, -1) if n in SHARDED else
                      jnp.broadcast_to(grads[n].reshape(1, -1), (N_DEV, grads[n].size)) for n in small]
            pieces += [jnp.broadcast_to(loss.reshape(1, 1), (N_DEV, 1)), jnp.zeros((N_DEV, nrow * LANES - total), F32)]
            srcs.append(jnp.concatenate(pieces, axis=1).reshape(N_DEV, nrow, LANES))
            names = names + ['small']
        started = scatter_start(srcs, name="grad_start_" + names[0], indexed=True)
        pending.append((names,) + started[:4])
        return started[4][:1, :1]

    full['on_grads'] = send_early
    loss, grad_x, g = local_step(x, loss_target, full)

    last_send = jnp.broadcast_to(g[last].reshape(1, -1, LANES), (N_DEV, g[last].size // LANES, LANES))
    recv = {last: exchange([last_send], name="grad_exchange")[0]}
    for names, ssem, rsem, src_thru, land_thru in pending:
        own, landed = scatter_wait(ssem, rsem, src_thru, land_thru, recv[last], name="grad_wait_" + names[0],
                                   indexed=True)
        for n, o, l in zip(names, own, landed):
            recv[n] = lax.dynamic_update_index_in_dim(l, lax.dynamic_index_in_dim(o, me, 0, keepdims=False), me, 0)

    outs = [{}, {}, {}, {}]
    for n in mats + [last]:
        shape = recv[n].shape[1:]
        res = adamw(recv[n], *[strip(n, args[p + n]).reshape(shape) for p in ('', 'm_', 'v_')], name="adamw_" + n)
        for o, p in zip(outs, res):
            o[n] = p.reshape(args[n].shape)

    def pack(prefix):
        vals = [strip(n, args[prefix + n]).reshape(-1) for n in small]
        return jnp.pad(jnp.concatenate(vals), (0, nrow * LANES - total + 1)).reshape(nrow, LANES)

    packed = adamw(recv['small'], pack(''), pack('m_'), pack('v_'), name="adamw_small")
    packed = [p.reshape(-1) for p in packed]
    off = 0
    for n in small:
        size = w[n].size
        for o, p in zip(outs, packed):
            o[n] = p[off:off + size].reshape(args[n].shape)
        off += size
    loss_out = packed[0][off].reshape(())
    return (loss_out, grad_x, *[o[n] for o in outs for n in WEIGHTS])
```

```python
import functools

import jax
import jax.numpy as jnp
from jax import lax
from jax.experimental import pallas as pl
from jax.experimental.pallas import tpu as pltpu

F32, BF16 = jnp.float32, jnp.bfloat16
N_DEV = 8
D_MODEL = 1024
SSD_W, HEADS, HDIM, SGROUPS, HPG, NSTATE, SCONV, QC = 1024, 16, 64, 4, 4, 128, 5, 128
XBC_W = SSD_W + 2 * SGROUPS * NSTATE
S5_W, S5_G, S5_C, S5_P, S5_Q = 512, 32, 16, 64, 16
S5_QC = S5_Q * S5_C
CARRY_ROWS = 32
DFF, FCONV = 2816, 3
FFN_BLK, FFN_PAD = 704, 768
EPS = 1e-6
ADAM_LR, ADAM_B1, ADAM_B2, ADAM_EPS, ADAM_WD, ADAM_STEP = 0.001, 0.9, 0.999, 1e-08, 0.01, 10
LANES = 128
MESH = pl.DeviceIdType.MESH

WEIGHTS = ['norm_mix_w', 'w_in', 'ssd_conv_w', 'ssd_conv_b', 'ssd_dt_bias_fwd', 'ssd_dt_bias_bwd', 'ssd_a_log_fwd',
           'ssd_a_log_bwd', 'ssd_d', 'ssd_norm_w', 's5_lambda_re_fwd', 's5_lambda_im_fwd', 's5_log_step_fwd',
           's5_lambda_re_bwd', 's5_lambda_im_bwd', 's5_log_step_bwd', 's5_b_re', 's5_b_im', 's5_c_re_fwd', 's5_c_im_fwd',
           's5_c_re_bwd', 's5_c_im_bwd', 's5_d', 's5_glu_w', 's5_glu_b', 's5_norm_w', 'w_out', 'norm_ffn_w', 'ffn_w_up',
           'ffn_conv_w', 'ffn_conv_b', 'ffn_w_down', 'norm_final_w']
SHARDED = {'w_in': 1, 'ssd_conv_w': 1, 'w_out': 0, 'ffn_w_up': 1, 'ffn_conv_w': 1, 'ffn_w_down': 0}
FULL_SHAPE = {'w_in': (1024, 3616), 'ssd_conv_w': (5, 2048), 'w_out': (1536, 1024), 'ffn_w_up': (1024, 5632),
              'ffn_conv_w': (3, 5632), 'ffn_w_down': (2816, 1024)}
PACK_ROWS = 512


def _pick(n, cap=1536):
    if n <= cap:
        return n
    return max(t for t in range(LANES, cap + 1, LANES) if n % t == 0)


def _params(sem):
    return pltpu.CompilerParams(dimension_semantics=sem)


def _bd(a, b, ca, cb):
    return lax.dot_general(a.astype(BF16), b.astype(BF16), (((ca,), (cb,)), ((), ())), preferred_element_type=F32)


@jax.custom_vjp
def dot_nn(a, b):
    return _bd(a, b, 1, 0)


dot_nn.defvjp(lambda a, b: (_bd(a, b, 1, 0), (a, b)),
              lambda r, g: (_bd(g, r[1], 1, 1).astype(r[0].dtype), _bd(r[0], g, 0, 0).astype(r[1].dtype)))


@jax.custom_vjp
def dot_nt(a, b):
    return _bd(a, b, 1, 1)


dot_nt.defvjp(lambda a, b: (_bd(a, b, 1, 1), (a, b)),
              lambda r, g: (_bd(g, r[1], 1, 0).astype(r[0].dtype), _bd(g, r[0], 0, 0).astype(r[1].dtype)))


@jax.custom_vjp
def dot_tn(a, b):
    return _bd(a, b, 0, 0)


dot_tn.defvjp(lambda a, b: (_bd(a, b, 0, 0), (a, b)),
              lambda r, g: (_bd(r[1], g, 1, 1).astype(r[0].dtype), _bd(r[0], g, 1, 0).astype(r[1].dtype)))


def _rows2(v):
    h = v.shape[0] // 2
    return v[:h], v[h:]


def _cols2(v):
    h = v.shape[1] // 2
    return v[:, :h], v[:, h:]


@jax.custom_vjp
def dot2_nn(la, lb, x):
    return _rows2(_bd(jnp.concatenate([la, lb], axis=0), x, 1, 0))


def _dot2_nn_bwd(res, g):
    la, lb, x = res
    gcat, lcat = jnp.concatenate(g, axis=0), jnp.concatenate([la, lb], axis=0)
    return (*_rows2(_bd(gcat, x, 1, 1)), _bd(lcat, gcat, 0, 0))


dot2_nn.defvjp(lambda la, lb, x: (dot2_nn(la, lb, x), (la, lb, x)), _dot2_nn_bwd)


@jax.custom_vjp
def dot_nt2(c, p0, p1):
    return _cols2(_bd(c, jnp.concatenate([p0, p1], axis=0), 1, 1))


def _dot_nt2_bwd(res, g):
    c, p0, p1 = res
    gcat = jnp.concatenate(g, axis=1)
    return (_bd(gcat, jnp.concatenate([p0, p1], axis=0), 1, 0), *_rows2(_bd(gcat, c, 0, 0)))


dot_nt2.defvjp(lambda c, p0, p1: (dot_nt2(c, p0, p1), (c, p0, p1)), _dot_nt2_bwd)


@jax.custom_vjp
def dot_tn2(a0, a1, b):
    return _rows2(_bd(jnp.concatenate([a0, a1], axis=1), b, 0, 0))


def _dot_tn2_bwd(res, g):
    a0, a1, b = res
    gcat, acat = jnp.concatenate(g, axis=0), jnp.concatenate([a0, a1], axis=1)
    return (*_cols2(_bd(b, gcat, 1, 1)), _bd(acat, gcat, 1, 0))


dot_tn2.defvjp(lambda a0, a1, b: (dot_tn2(a0, a1, b), (a0, a1, b)), _dot_tn2_bwd)


def _split3(x):
    hi = x.astype(BF16)
    r = x - hi.astype(F32)
    mid = r.astype(BF16)
    lo = (r - mid.astype(F32)).astype(BF16)
    return hi, mid, lo


def _cum_matrix(q, upper):
    ri = lax.broadcasted_iota(jnp.int32, (q, q), 0)
    ci = lax.broadcasted_iota(jnp.int32, (q, q), 1)
    return jnp.where((ci >= ri) if upper else (ci <= ri), 1.0, 0.0).astype(BF16)


def _exact_right(x, mat):
    return sum(jnp.dot(p, mat, preferred_element_type=F32) for p in _split3(x))


@functools.partial(jax.custom_vjp, nondiff_argnums=(1,))
def cum_row(x, rev):
    return _exact_right(x, _cum_matrix(x.shape[1], not rev))


cum_row.defvjp(lambda x, rev: (cum_row(x, rev), None),
               lambda rev, _, g: (_exact_right(g, _cum_matrix(g.shape[1], rev)),))


def _softplus(x):
    return jnp.maximum(x, 0.0) + jnp.log(1.0 + jnp.exp(-jnp.abs(x)))


def _silu(x):
    return x * jax.nn.sigmoid(x)


def _gelu(x):
    return 0.5 * x * (1.0 + jnp.tanh(0.7978845608028654 * (x + 0.044715 * (x * x * x))))


def _rms(x, w):
    xf = x.astype(F32)
    return xf * lax.rsqrt(jnp.mean(xf * xf, axis=-1, keepdims=True) + EPS) * w


def matmul_sum(a_list, b_list, *, name, out_dtype=F32, add=None, tm=512, nt=False):
    a_arrs = [a[0] if isinstance(a, tuple) else a for a in a_list]
    b_arrs = [b[0] if isinstance(b, tuple) else b for b in b_list]
    m, n = a_arrs[0].shape[0], b_arrs[0].shape[-2 if nt else -1]
    tm, tn, k = min(tm, m), _pick(n), len(a_list)

    def body(*refs):
        acc = None
        for a_ref, b_ref in zip(refs[:k], refs[k:2 * k]):
            p = _bd(a_ref[...], b_ref[...], 1, 1 if nt else 0)
            acc = p if acc is None else acc + p
        if add is not None:
            acc = acc + refs[2 * k][...]
        refs[-1][...] = acc.astype(out_dtype)

    def a_spec(a):
        if isinstance(a, tuple):
            return pl.BlockSpec((tm, a[1]), lambda i, j, blk=a[2]: (i, blk))
        return pl.BlockSpec((tm, a.shape[1]), lambda i, j: (i, 0))

    def b_spec(b):
        arr, p = b if isinstance(b, tuple) else (b, None)
        kk = arr.shape[-1 if nt else -2]
        shape, idx = ((tn, kk), lambda j: (j, 0)) if nt else ((kk, tn), lambda j: (0, j))
        if p is None:
            return pl.BlockSpec(shape, lambda i, j: idx(j))
        return pl.BlockSpec((None,) + shape, lambda i, j, p=p: (p,) + idx(j))

    in_specs = [a_spec(a) for a in a_list] + [b_spec(b) for b in b_list]
    args = a_arrs + b_arrs
    if add is not None:
        in_specs.append(pl.BlockSpec((tm, tn), lambda i, j: (i, j)))
        args.append(add)
    return pl.pallas_call(
        body, name=name, grid=(m // tm, n // tn), in_specs=in_specs,
        out_specs=pl.BlockSpec((tm, tn), lambda i, j: (i, j)),
        out_shape=jax.ShapeDtypeStruct((m, n), out_dtype),
        compiler_params=_params(("parallel", "parallel")))(*args)


def matmul_cols(a, b3, *, name, out_dtype=F32, tm=1024):
    m, kk = a.shape
    p, _, nb = b3.shape
    tm, tn = min(tm, m), _pick(nb, 768)
    per = nb // tn

    def body(a_ref, b_ref, o_ref):
        o_ref[...] = _bd(a_ref[...], b_ref[...], 1, 0).astype(out_dtype)

    return pl.pallas_call(
        body, name=name, grid=(m // tm, p * per),
        in_specs=[pl.BlockSpec((tm, kk), lambda i, j: (i, 0)),
                  pl.BlockSpec((None, kk, tn), lambda i, j: (j // per, 0, j % per))],
        out_specs=pl.BlockSpec((tm, tn), lambda i, j: (i, j)),
        out_shape=jax.ShapeDtypeStruct((m, p * nb), out_dtype),
        compiler_params=_params(("parallel", "parallel")))(a, b3)


def matmul_tn(a, b, *, name, tm=1024, out_blocks=None):
    m, k = a.shape
    n = b.shape[1]
    nb = n // (out_blocks or 1)
    tm, tk, tn = min(tm, m), _pick(k), _pick(nb, 768 if out_blocks else 1536)
    per = nb // tn

    def body(a_ref, b_ref, o_ref):
        @pl.when(pl.program_id(2) == 0)
        def _():
            o_ref[...] = jnp.zeros_like(o_ref)

        o_ref[...] += _bd(a_ref[...], b_ref[...], 0, 0)

    if out_blocks:
        out_spec = pl.BlockSpec((None, tk, tn), lambda i, j, t: (j // per, i, j % per))
        out_shape = jax.ShapeDtypeStruct((out_blocks, k, nb), F32)
    else:
        out_spec = pl.BlockSpec((tk, tn), lambda i, j, t: (i, j))
        out_shape = jax.ShapeDtypeStruct((k, n), F32)
    return pl.pallas_call(
        body, name=name, grid=(k // tk, n // tn, m // tm),
        in_specs=[pl.BlockSpec((tm, tk), lambda i, j, t: (t, i)), pl.BlockSpec((tm, tn), lambda i, j, t: (t, j))],
        out_specs=out_spec, out_shape=out_shape,
        compiler_params=_params(("parallel", "parallel", "arbitrary")))(a, b)


def _row_spec(r, tm):
    if isinstance(r, tuple):
        arr, width, blk = r
        return arr, pl.BlockSpec((tm, width), lambda i, blk=blk: (i, blk))
    return r, pl.BlockSpec((tm, r.shape[1]), lambda i: (i, 0))


def _full_spec(p):
    return pl.BlockSpec(p.shape, lambda i: (0,) * p.ndim)


def rowmap_fwd(fn, rows, params, outs, *, name, tm=256):
    pairs = [_row_spec(r, tm) for r in rows]
    m = pairs[0][0].shape[0]
    tm = min(tm, m)
    pairs = [_row_spec(r, tm) for r in rows]
    nr, npar = len(rows), len(params)

    def body(*refs):
        res = fn(*[r[...] for r in refs[:nr + npar]])
        for o_ref, v in zip(refs[nr + npar:], res):
            o_ref[...] = v.astype(o_ref.dtype)

    return pl.pallas_call(
        body, name=name, grid=(m // tm,),
        in_specs=[s for _, s in pairs] + [_full_spec(p) for p in params],
        out_specs=[pl.BlockSpec((tm, c), lambda i: (i, 0)) for c, _ in outs],
        out_shape=[jax.ShapeDtypeStruct((m, c), dt) for c, dt in outs],
        compiler_params=_params(("parallel",)))(*[a for a, _ in pairs], *params)


def rowmap_bwd(fn, rows, params, cts, *, name, row_dtypes=None, add=None, tm=256):
    m = _row_spec(rows[0], tm)[0].shape[0]
    tm = min(tm, m)
    rp = [_row_spec(r, tm) for r in rows]
    cp = [_row_spec(c, tm) for c in cts]
    nr, npar, nc = len(rows), len(params), len(cts)
    row_dtypes = row_dtypes or [F32] * nr
    widths = [s.block_shape[1] for _, s in rp]

    def body(*refs):
        ins = [r[...] for r in refs[:nr + npar]]
        ins = [v.astype(F32) for v in ins]
        ct = tuple(r[...].astype(F32) for r in refs[nr + npar:nr + npar + nc])
        base = nr + npar + nc
        extra = None
        if add is not None:
            extra = refs[base][...]
            base += 1
        _, pull = jax.vjp(fn, *ins)
        grads = pull(ct)
        for j in range(nr):
            g = grads[j]
            if j == 0 and extra is not None:
                g = g + extra
            refs[base + j][...] = g.astype(refs[base + j].dtype)

        @pl.when(pl.program_id(0) == 0)
        def _():
            for j in range(npar):
                refs[base + nr + j][...] = jnp.zeros_like(refs[base + nr + j])

        for j in range(npar):
            refs[base + nr + j][...] += grads[nr + j]

    in_specs = [s for _, s in rp] + [_full_spec(p) for p in params] + [s for _, s in cp]
    args = [a for a, _ in rp] + list(params) + [a for a, _ in cp]
    if add is not None:
        in_specs.append(pl.BlockSpec((tm, widths[0]), lambda i: (i, 0)))
        args.append(add)
    out_specs = [pl.BlockSpec((tm, w), lambda i: (i, 0)) for w in widths] + [_full_spec(p) for p in params]
    out_shape = [jax.ShapeDtypeStruct((m, w), dt) for w, dt in zip(widths, row_dtypes)]
    out_shape += [jax.ShapeDtypeStruct(p.shape, F32) for p in params]
    return pl.pallas_call(
        body, name=name, grid=(m // tm,), in_specs=in_specs, out_specs=out_specs, out_shape=out_shape,
        compiler_params=_params(("arbitrary",)))(*args)


def loss_head(h, target, w, *, name, tm=256):
    m, d = h.shape
    tm = min(tm, m)

    def body(h_ref, t_ref, w_ref, loss_ref, dh_ref, dw_ref):
        y, pull = jax.vjp(_rms, h_ref[...], w_ref[...])
        err = y - t_ref[...]
        dh, dw = pull(err * (1.0 / d))

        @pl.when(pl.program_id(0) == 0)
        def _():
            loss_ref[...] = jnp.zeros_like(loss_ref)
            dw_ref[...] = jnp.zeros_like(dw_ref)

        loss_ref[...] += (0.5 / d) * jnp.sum(err * err, keepdims=True)
        dw_ref[...] += dw
        dh_ref[...] = dh

    row = pl.BlockSpec((tm, d), lambda i: (i, 0))
    return pl.pallas_call(
        body, name=name, grid=(m // tm,), in_specs=[row, row, _full_spec(w)],
        out_specs=[pl.BlockSpec((1, 1), lambda i: (0, 0)), row, _full_spec(w)],
        out_shape=[jax.ShapeDtypeStruct((1, 1), F32), jax.ShapeDtypeStruct((m, d), F32),
                   jax.ShapeDtypeStruct(w.shape, F32)],
        compiler_params=_params(("arbitrary",)))(h, target, w)


def _shift(x, s):
    if s == 0:
        return x
    n = x.shape[0]
    t = lax.broadcasted_iota(jnp.int32, x.shape, 0)
    rolled = pltpu.roll(x, (-s) % n, 0)
    return jnp.where((t + s >= 0) & (t + s < n), rolled, 0.0)


def _conv(x, w, b):
    k = w.shape[0]
    acc = b + w[k // 2:k // 2 + 1, :] * x
    for j in range(k):
        if j != k // 2:
            acc = acc + w[j:j + 1, :] * _shift(x, j - k // 2)
    return acc


def _conv_bwd(x, dc, w):
    k = w.shape[0]
    dx = None
    dws = []
    for j in range(k):
        s = j - k // 2
        term = w[j:j + 1, :] * _shift(dc, -s)
        dx = term if dx is None else dx + term
        dws.append(jnp.sum(dc * _shift(x, s), axis=0, keepdims=True))
    return dx, jnp.concatenate(dws, axis=0), jnp.sum(dc, axis=0, keepdims=True)


def _dsilu(c):
    s = jax.nn.sigmoid(c)
    return s * (1.0 + c * (1.0 - s))


def ssd_conv_fwd(xbc, w, b, *, bsz, name):
    t, c = xbc.shape
    seq, ct = t // bsz, 256

    def body(x_ref, w_ref, b_ref, o_ref):
        o_ref[...] = _silu(_conv(x_ref[...], w_ref[...], b_ref[...]))

    return pl.pallas_call(
        body, name=name, grid=(c // ct, bsz),
        in_specs=[pl.BlockSpec((seq, ct), lambda j, i: (i, j)), pl.BlockSpec((w.shape[0], ct), lambda j, i: (0, j)),
                  pl.BlockSpec((1, ct), lambda j, i: (0, j))],
        out_specs=pl.BlockSpec((seq, ct), lambda j, i: (i, j)),
        out_shape=jax.ShapeDtypeStruct((t, c), F32),
        compiler_params=_params(("parallel", "parallel")))(xbc, w, b)


def ssd_conv_bwd(xbc, dparts, w, b, *, bsz, name):
    t, c = xbc.shape
    seq, ct, k = t // bsz, 256, w.shape[0]
    starts = [0]
    for p in dparts:
        starts.append(starts[-1] + p.shape[1] // ct)

    def body(x_ref, *refs):
        g_refs, (w_ref, b_ref, dx_ref, dw_ref, db_ref) = refs[:len(dparts)], refs[len(dparts):]
        j = pl.program_id(0)
        g = g_refs[-1][...]
        for n in range(len(dparts) - 2, -1, -1):
            g = jnp.where(j < starts[n + 1], g_refs[n][...], g)
        x, wv = x_ref[...], w_ref[...]
        dc = g * _dsilu(_conv(x, wv, b_ref[...]))
        dx, dw, db = _conv_bwd(x, dc, wv)
        dx_ref[...] = dx

        @pl.when(pl.program_id(1) == 0)
        def _():
            dw_ref[...] = jnp.zeros_like(dw_ref)
            db_ref[...] = jnp.zeros_like(db_ref)

        dw_ref[...] += dw
        db_ref[...] += db

    def part_spec(n):
        lo, hi = starts[n], starts[n + 1]

        def index(j, i):
            inside = (j >= lo) & (j < hi)
            return jnp.where(inside, i, 0), jnp.where(inside, j - lo, 0)

        return pl.BlockSpec((seq, ct), index)

    blk = pl.BlockSpec((seq, ct), lambda j, i: (i, j))
    wspec, bspec = pl.BlockSpec((k, ct), lambda j, i: (0, j)), pl.BlockSpec((1, ct), lambda j, i: (0, j))
    return pl.pallas_call(
        body, name=name, grid=(c // ct, bsz),
        in_specs=[blk] + [part_spec(n) for n in range(len(dparts))] + [wspec, bspec], out_specs=[blk, wspec, bspec],
        out_shape=[jax.ShapeDtypeStruct((t, c), F32), jax.ShapeDtypeStruct((k, c), F32),
                   jax.ShapeDtypeStruct((1, c), F32)],
        compiler_params=_params(("parallel", "arbitrary")))(xbc, *dparts, w, b)


def _ffn_specs(seq, ct, k, nblk):
    val = pl.BlockSpec((seq, ct), lambda j, i: (i, j))
    gate = pl.BlockSpec((seq, ct), lambda j, i: (i, nblk + j))
    wv, wg = pl.BlockSpec((k, ct), lambda j, i: (0, j)), pl.BlockSpec((k, ct), lambda j, i: (0, nblk + j))
    bv, bg = pl.BlockSpec((1, ct), lambda j, i: (0, j)), pl.BlockSpec((1, ct), lambda j, i: (0, nblk + j))
    return val, gate, wv, wg, bv, bg


def ffn_act_fwd(up, w, b, *, bsz, name):
    t = up.shape[0]
    half = up.shape[1] // 2
    seq, ct, k = t // bsz, 256, w.shape[0]
    val, gate, wv, wg, bv, bg = _ffn_specs(seq, ct, k, half // ct)

    def body(v_ref, g_ref, wv_ref, wg_ref, bv_ref, bg_ref, o_ref):
        vc = _conv(v_ref[...].astype(F32), wv_ref[...], bv_ref[...])
        gc = _conv(g_ref[...].astype(F32), wg_ref[...], bg_ref[...])
        o_ref[...] = (_silu(gc) * vc).astype(BF16)

    return pl.pallas_call(
        body, name=name, grid=(half // ct, bsz), in_specs=[val, gate, wv, wg, bv, bg], out_specs=val,
        out_shape=jax.ShapeDtypeStruct((t, half), BF16),
        compiler_params=_params(("parallel", "parallel")))(up, up, w, w, b, b)


def ffn_act_bwd(up, dact, w, b, *, bsz, name):
    t = up.shape[0]
    half = up.shape[1] // 2
    seq, ct, k = t // bsz, 256, w.shape[0]
    val, gate, wv, wg, bv, bg = _ffn_specs(seq, ct, k, half // ct)

    def body(v_ref, g_ref, wv_ref, wg_ref, bv_ref, bg_ref, d_ref, dv_ref, dg_ref, dwv_ref, dwg_ref, dbv_ref, dbg_ref):
        v, g = v_ref[...].astype(F32), g_ref[...].astype(F32)
        vc = _conv(v, wv_ref[...], bv_ref[...])
        gc = _conv(g, wg_ref[...], bg_ref[...])
        d = d_ref[...].astype(F32)
        dv, dwv, dbv = _conv_bwd(v, d * _silu(gc), wv_ref[...])
        dg, dwg, dbg = _conv_bwd(g, d * vc * _dsilu(gc), wg_ref[...])
        dv_ref[...] = dv.astype(BF16)
        dg_ref[...] = dg.astype(BF16)

        @pl.when(pl.program_id(1) == 0)
        def _():
            for r in (dwv_ref, dwg_ref, dbv_ref, dbg_ref):
                r[...] = jnp.zeros_like(r)

        dwv_ref[...] += dwv
        dwg_ref[...] += dwg
        dbv_ref[...] += dbv
        dbg_ref[...] += dbg

    return pl.pallas_call(
        body, name=name, grid=(half // ct, bsz), in_specs=[val, gate, wv, wg, bv, bg, val],
        out_specs=[val, val, wv, wv, bv, bv],
        out_shape=[jax.ShapeDtypeStruct((t, half), BF16), jax.ShapeDtypeStruct((t, half), BF16),
                   jax.ShapeDtypeStruct((k, half), F32), jax.ShapeDtypeStruct((k, half), F32),
                   jax.ShapeDtypeStruct((1, half), F32), jax.ShapeDtypeStruct((1, half), F32)],
        compiler_params=_params(("parallel", "arbitrary")))(up, up, w, w, b, b, dact)


def _sel_row(a, h):
    oh = (lax.broadcasted_iota(jnp.int32, (a.shape[0], 1), 0) == h).astype(F32)
    return jnp.sum(a * oh, axis=0, keepdims=True)


def _ssd_chunk(xp, dtr, bm, cm, prev, bias_r, alog_r, dskip_r, rev):
    q = dtr.shape[1]
    ri = lax.broadcasted_iota(jnp.int32, (q, q), 0)
    ci = lax.broadcasted_iota(jnp.int32, (q, q), 1)
    mask = (ci >= ri) if rev else (ci <= ri)
    lane_lo, row_lo = ci < HDIM, ri < HDIM
    dt_r = _softplus(dtr + bias_r)
    dta_r = dt_r * (-jnp.exp(alog_r))
    cs_r = cum_row(dta_r, rev)
    scores = dot_nt(cm, bm)

    def per_row(v):
        return jnp.broadcast_to(v, (q, q)).T

    assert len(xp) == 2
    y_diag, csqs, decayed, tots = [], [], [], []
    for p in range(2):
        ha = 2 * p + (HPG if rev else 0)
        hb = ha + 1
        cs_a, cs_b = _sel_row(cs_r, ha), _sel_row(cs_r, hb)
        csq_a, csq_b = per_row(cs_a), per_row(cs_b)
        seg_a = jnp.exp(jnp.where(mask, csq_a - cs_a, -1e30))
        seg_b = jnp.exp(jnp.where(mask, csq_b - cs_b, -1e30))
        csq = jnp.where(lane_lo, csq_a, csq_b)
        xdt = xp[p] * jnp.where(lane_lo, per_row(_sel_row(dt_r, ha)), per_row(_sel_row(dt_r, hb)))
        tot_a = jnp.sum(_sel_row(dta_r, ha), axis=1, keepdims=True)
        tot_b = jnp.sum(_sel_row(dta_r, hb), axis=1, keepdims=True)
        y_diag.append(jnp.where(lane_lo, *dot2_nn(scores * seg_a, scores * seg_b, xdt)))
        csqs.append(csq)
        decayed.append(xdt * jnp.exp(jnp.where(lane_lo, tot_a, tot_b) - csq))
        tots.append((tot_a, tot_b, ha, hb))
    y_off = dot_nt2(cm, *prev)
    states = dot_tn2(*decayed, bm)
    ys, news = [], []
    for p, (tot_a, tot_b, ha, hb) in enumerate(tots):
        y = y_diag[p] + y_off[p] * jnp.exp(csqs[p])
        if not rev:
            y = y + jnp.where(lane_lo, _sel_row(dskip_r, ha), _sel_row(dskip_r, hb)) * xp[p]
        ys.append(y)
        news.append(jnp.exp(jnp.where(row_lo, tot_a, tot_b)) * prev[p] + states[p])
    return tuple(ys), tuple(news)


NPAIR = HPG // 2


def _ssd_specs(seq, nc):
    xs = pl.BlockSpec((None, seq, HPG * HDIM), lambda b, g: (b, 0, g))
    bm = pl.BlockSpec((None, seq, NSTATE), lambda b, g: (b, 0, SSD_W // NSTATE + g))
    cm = pl.BlockSpec((None, seq, NSTATE), lambda b, g: (b, 0, SSD_W // NSTATE + SGROUPS + g))
    dtr = pl.BlockSpec((None, None, 2 * HPG, seq), lambda b, g: (b, g, 0, 0))
    pr = pl.BlockSpec((None, 2 * HPG, 1), lambda b, g: (g, 0, 0))
    st = pl.BlockSpec((None, None, 2, nc, NPAIR, 2 * HDIM, NSTATE), lambda b, g: (b, g, 0, 0, 0, 0, 0))
    return xs, bm, cm, dtr, pr, st


def _pair_cols(p):
    return slice(2 * HDIM * p, 2 * HDIM * (p + 1))


def ssd_scan_fwd(act, dtr, prs, *, name):
    bsz, seq, _ = act.shape
    nc = seq // QC
    xs, bm, cm, dtrs, pr, st = _ssd_specs(seq, nc)

    def body(x_ref, b_ref, c_ref, dtr_ref, br_ref, ar_ref, dk_ref, y_ref, st_ref):
        par = (br_ref[...], ar_ref[...], dk_ref[...])
        y_ref[...] = jnp.zeros_like(y_ref)

        def step(i, carry):
            new = []
            for rev in (False, True):
                k = (nc - 1 - i) if rev else i
                rows = pl.ds(pl.multiple_of(k * QC, QC), QC)
                xp = tuple(x_ref[rows, _pair_cols(p)] for p in range(NPAIR))
                for p in range(NPAIR):
                    st_ref[int(rev), k, p] = carry[rev][p]
                ys, nw = _ssd_chunk(xp, dtr_ref[:, rows], b_ref[rows, :], c_ref[rows, :], carry[rev], *par, rev)
                for p in range(NPAIR):
                    y_ref[rows, _pair_cols(p)] += ys[p]
                new.append(nw)
            return tuple(new)

        zero = tuple(jnp.zeros((2 * HDIM, NSTATE), F32) for _ in range(NPAIR))
        lax.fori_loop(0, nc // 2, lambda i, c: step(2 * i + 1, step(2 * i, c)), (zero, zero))

    return pl.pallas_call(
        body, name=name, grid=(bsz, SGROUPS), in_specs=[xs, bm, cm, dtrs, pr, pr, pr], out_specs=[xs, st],
        out_shape=[jax.ShapeDtypeStruct((bsz, seq, SSD_W), F32),
                   jax.ShapeDtypeStruct((bsz, SGROUPS, 2, nc, NPAIR, 2 * HDIM, NSTATE), F32)],
        compiler_params=_params(("parallel", "parallel")))(act, act, act, dtr, *prs)


def ssd_scan_bwd(act, dtr, prs, states, dy, *, name):
    bsz, seq, _ = act.shape
    nc = seq // QC
    xs, bm, cm, dtrs, pr, st = _ssd_specs(seq, nc)
    grp = pl.BlockSpec((None, seq, NSTATE), lambda b, g: (b, 0, g))
    dpr = pl.BlockSpec((None, None, 2 * HPG, 1), lambda b, g: (b, g, 0, 0))

    def body(x_ref, b_ref, c_ref, dtr_ref, br_ref, ar_ref, dk_ref, st_ref, dy_ref,
             dx_ref, db_ref, dc_ref, ddtr_ref, gbr_ref, gar_ref, gdk_ref):
        par = (br_ref[...], ar_ref[...], dk_ref[...])
        pgrads = (gbr_ref, gar_ref, gdk_ref)
        for r in pgrads + (dx_ref, db_ref, dc_ref, ddtr_ref):
            r[...] = jnp.zeros_like(r)

        def bstep(i, dcarry):
            new = []
            for rev in (False, True):
                k = i if rev else (nc - 1 - i)
                rows = pl.ds(pl.multiple_of(k * QC, QC), QC)
                xp = tuple(x_ref[rows, _pair_cols(p)] for p in range(NPAIR))
                prev = tuple(st_ref[int(rev), k, p] for p in range(NPAIR))
                _, pull = jax.vjp(functools.partial(_ssd_chunk, rev=rev), xp, dtr_ref[:, rows], b_ref[rows, :],
                                  c_ref[rows, :], prev, *par)
                dyp = tuple(dy_ref[rows, _pair_cols(p)] for p in range(NPAIR))
                gx, gdt, gb, gc, gprev, *gpar = pull((dyp, dcarry[rev]))
                for p in range(NPAIR):
                    dx_ref[rows, _pair_cols(p)] += gx[p]
                ddtr_ref[:, rows] += gdt
                db_ref[rows, :] += gb
                dc_ref[rows, :] += gc
                for r, g in zip(pgrads, gpar):
                    r[...] += g
                new.append(gprev)
            return tuple(new)

        zero = tuple(jnp.zeros((2 * HDIM, NSTATE), F32) for _ in range(NPAIR))
        lax.fori_loop(0, nc, bstep, (zero, zero))

    out_shape = [jax.ShapeDtypeStruct((bsz, seq, SSD_W), F32),
                 jax.ShapeDtypeStruct((bsz, seq, SGROUPS * NSTATE), F32),
                 jax.ShapeDtypeStruct((bsz, seq, SGROUPS * NSTATE), F32),
                 jax.ShapeDtypeStruct(dtr.shape, F32)]
    out_shape += [jax.ShapeDtypeStruct((bsz, SGROUPS, 2 * HPG, 1), F32)] * 3
    return pl.pallas_call(
        body, name=name, grid=(bsz, SGROUPS), in_specs=[xs, bm, cm, dtrs, pr, pr, pr, st, xs],
        out_specs=[xs, grp, grp, dtrs, dpr, dpr, dpr], out_shape=out_shape,
        compiler_params=_params(("parallel", "parallel")))(act, act, act, dtr, *prs, states, dy)


def _s5_core(lam_re, lam_im, log_step, b_re, b_im, c_re, c_im):
    q = S5_Q
    step = jnp.exp(log_step)[:, None]
    lr, li = lam_re * step, lam_im * step
    mag = jnp.exp(lr)
    ar, ai = mag * jnp.cos(li), mag * jnp.sin(li)
    den = lam_re * lam_re + lam_im * lam_im
    cr = ((ar - 1.0) * lam_re + ai * lam_im) / den
    ci = (ai * lam_re - (ar - 1.0) * lam_im) / den
    bbr = cr[..., None] * b_re - ci[..., None] * b_im
    bbi = cr[..., None] * b_im + ci[..., None] * b_re
    d = jnp.arange(q + 1, dtype=F32)[None, :, None]
    pm = jnp.exp(d * lr[:, None, :])
    pr, pi = pm * jnp.cos(d * li[:, None, :]), pm * jnp.sin(d * li[:, None, :])
    er = pr[..., None] * bbr[:, None] - pi[..., None] * bbi[:, None]
    ei = pr[..., None] * bbi[:, None] + pi[..., None] * bbr[:, None]
    hp = lax.Precision.HIGHEST
    k = (jnp.einsum('gcp,gdpz->gdcz', c_re, er[:, :q], precision=hp)
         - jnp.einsum('gcp,gdpz->gdcz', c_im, ei[:, :q], precision=hp))
    e = jnp.concatenate([er[:, :q], ei[:, :q]], axis=2)
    p1r, p1i = pr[:, 1:], pi[:, 1:]
    m_re = c_re[:, None] * p1r[:, :, None, :] - c_im[:, None] * p1i[:, :, None, :]
    m_im = -c_re[:, None] * p1i[:, :, None, :] - c_im[:, None] * p1r[:, :, None, :]
    da = jnp.concatenate([pr[:, q], pr[:, q]], axis=-1)
    db = jnp.concatenate([-pi[:, q], pi[:, q]], axis=-1)
    return k, e, jnp.concatenate([m_re, m_im], axis=-1), da, db


def _s5_operators(lf_re, lf_im, lsf, lb_re, lb_im, lsb, b_re, b_im, cf_re, cf_im, cb_re, cb_im):
    g = lf_re.shape[0]
    both = lambda f, b: jnp.concatenate([f, b], axis=0)
    k, e, m, da, db = _s5_core(both(lf_re, lb_re), both(lf_im, lb_im), both(lsf, lsb), both(b_re, b_re),
                               both(b_im, b_im), both(cf_re, cb_re), both(cf_im, cb_im))
    kf, kb = k[:g], k[g:]
    wtf, wtb = jnp.transpose(e[:g, ::-1], (0, 1, 3, 2)), jnp.transpose(e[g:], (0, 1, 3, 2))
    mtf, mtb = jnp.transpose(m[:g], (0, 3, 1, 2)), jnp.transpose(m[g:, ::-1], (0, 3, 1, 2))
    daf, dab, dbf, dbb = da[:g], da[g:], db[:g], db[g:]
    lags = jnp.concatenate([kb[:, :0:-1], kf[:, :1] + kb[:, :1], kf[:, 1:]], axis=1)
    tt = jnp.transpose(lags, (0, 1, 3, 2))
    wt = jnp.concatenate([wtf.reshape(g, S5_QC, 2 * S5_P), wtb.reshape(g, S5_QC, 2 * S5_P)], axis=-1)
    mt = jnp.concatenate([mtf.reshape(g, 2 * S5_P, S5_QC), mtb.reshape(g, 2 * S5_P, S5_QC)], axis=1)
    return tt, wt, mt, jnp.concatenate([daf, dab], -1), jnp.concatenate([dbf, dbb], -1)


def _gspec(*shape):
    return pl.BlockSpec((None,) + shape, lambda g: (g,) + (0,) * len(shape))


S5_HALVES = S5_QC // LANES


def _toeplitz_block(s, t):
    per = LANES // S5_C
    return t // per, slice(s * S5_C, (s + 1) * S5_C), slice((t % per) * S5_C, (t % per + 1) * S5_C)


def s5_toeplitz(kt, *, name):
    g = kt.shape[0]

    def body(k_ref, t_ref):
        for s in range(S5_Q):
            for t in range(S5_Q):
                t_ref[_toeplitz_block(s, t)] = k_ref[t - s + S5_Q - 1]

    return pl.pallas_call(
        body, name=name, grid=(g,), in_specs=[_gspec(2 * S5_Q - 1, S5_C, S5_C)],
        out_specs=_gspec(S5_HALVES, S5_QC, LANES), out_shape=jax.ShapeDtypeStruct((g, S5_HALVES, S5_QC, LANES), F32),
        compiler_params=_params(("parallel",)))(kt)


def s5_toeplitz_bwd(dtt, *, name):
    g = dtt.shape[0]

    def body(d_ref, k_ref):
        for j in range(2 * S5_Q - 1):
            acc = None
            for s in range(S5_Q):
                t = j - (S5_Q - 1) + s
                if 0 <= t < S5_Q:
                    blk = d_ref[_toeplitz_block(s, t)]
                    acc = blk if acc is None else acc + blk
            k_ref[j] = acc

    return pl.pallas_call(
        body, name=name, grid=(g,), in_specs=[_gspec(S5_HALVES, S5_QC, LANES)],
        out_specs=_gspec(2 * S5_Q - 1, S5_C, S5_C), out_shape=jax.ShapeDtypeStruct((g, 2 * S5_Q - 1, S5_C, S5_C), F32),
        compiler_params=_params(("parallel",)))(dtt)


S5_RT = 64


def _chunk_piece(q):
    per = LANES // S5_C
    return q // per, slice((q % per) * S5_C, (q % per + 1) * S5_C)


def to_chunks(u, *, name):
    t = u.shape[0]
    r = t // S5_Q
    rt = min(S5_RT, r)

    per = LANES // S5_C
    nblk = S5_W // LANES

    def body(*refs):
        o_ref = refs[-1]
        for k in range(nblk):
            for q in range(S5_Q):
                rows = refs[k][pl.ds(q, rt, stride=S5_Q), :]
                half, lanes = _chunk_piece(q)
                for j in range(per):
                    o_ref[k * per + j, half, :, lanes] = rows[:, j * S5_C:(j + 1) * S5_C]

    return pl.pallas_call(
        body, name=name, grid=(r // rt,),
        in_specs=[pl.BlockSpec((rt * S5_Q, LANES), lambda i, k=k: (i, k)) for k in range(nblk)],
        out_specs=pl.BlockSpec((S5_G, S5_HALVES, rt, LANES), lambda i: (0, 0, i, 0)),
        out_shape=jax.ShapeDtypeStruct((S5_G, S5_HALVES, r, LANES), F32),
        compiler_params=_params(("parallel",)))(*[u] * nblk)


def from_chunks(y, *, name, add=None):
    r = y.shape[2]
    rt = min(S5_RT, r)
    per = LANES // S5_C

    nblk = S5_W // LANES

    def body(*refs):
        y_ref, tmp_ref = refs[0], refs[-1]
        adds, outs = refs[1:-1 - nblk], refs[-1 - nblk:-1]
        for k in range(nblk):
            for q in range(S5_Q):
                half, lanes = _chunk_piece(q)
                for j in range(per):
                    tmp_ref[:, j * S5_C:(j + 1) * S5_C] = y_ref[k * per + j, half, :, lanes]
                row = tmp_ref[...]
                if add is not None:
                    row = row + adds[k][pl.ds(q, rt, stride=S5_Q), :]
                outs[k][pl.ds(q, rt, stride=S5_Q), :] = row

    in_specs = [pl.BlockSpec((S5_G, S5_HALVES, rt, LANES), lambda i: (0, 0, i, 0))]
    if add is not None:
        in_specs += [pl.BlockSpec((rt * S5_Q, LANES), lambda i, k=k: (i, k)) for k in range(nblk)]
    blocks = pl.pallas_call(
        body, name=name, grid=(r // rt,), in_specs=in_specs,
        out_specs=[pl.BlockSpec((rt * S5_Q, LANES), lambda i: (i, 0))] * nblk,
        out_shape=[jax.ShapeDtypeStruct((r * S5_Q, LANES), F32)] * nblk,
        scratch_shapes=[pltpu.VMEM((rt, LANES), F32)],
        compiler_params=_params(("parallel",)))(*([y] if add is None else [y] + [add] * nblk))
    return jnp.concatenate(blocks, axis=1)


def _cat(ref):
    return jnp.concatenate([ref[h] for h in range(S5_HALVES)], axis=1)


def _put(ref, v):
    for h in range(S5_HALVES):
        ref[h] = v[:, h * LANES:(h + 1) * LANES]


def _cspec(r):
    return _gspec(S5_HALVES, r, LANES)


def s5_state_in(u, wt, *, name):
    g, _, r, _ = u.shape

    def body(u_ref, w_ref, o_ref):
        o_ref[...] = _bd(_cat(u_ref), w_ref[...], 1, 0)

    return pl.pallas_call(
        body, name=name, grid=(g,), in_specs=[_cspec(r), _gspec(S5_QC, 4 * S5_P)],
        out_specs=_gspec(r, 4 * S5_P), out_shape=jax.ShapeDtypeStruct((g, r, 4 * S5_P), F32),
        compiler_params=_params(("parallel",)))(u, wt)


def _swap(h):
    return pltpu.roll(h, S5_P, 1)


def s5_carry_fwd(s, da, db, *, name):
    nck, rows, _ = s.shape
    w = 2 * S5_P

    def body(s_ref, da_ref, db_ref, h_ref):
        dirs = ((False, slice(0, w)), (True, slice(w, 2 * w)))
        coef = [(da_ref[:, cols], db_ref[:, cols]) for _, cols in dirs]

        def step(i, hs):
            new = []
            for (rev, cols), (a, b), h in zip(dirs, coef, hs):
                k = (nck - 1 - i) if rev else i
                h_ref[k, :, cols] = h
                new.append(a * h + b * _swap(h) + s_ref[k, :, cols])
            return tuple(new)

        z = jnp.zeros((rows, w), F32)
        lax.fori_loop(0, nck, step, (z, z), unroll=2)

    rt = min(CARRY_ROWS, rows)
    big, small = pl.BlockSpec((nck, rt, 2 * w), lambda i: (0, i, 0)), pl.BlockSpec((rt, 2 * w), lambda i: (i, 0))
    rows = rt
    return pl.pallas_call(
        body, name=name, grid=(s.shape[1] // rt,), in_specs=[big, small, small], out_specs=big,
        out_shape=jax.ShapeDtypeStruct(s.shape, F32), compiler_params=_params(("parallel",)))(s, da, db)


def s5_carry_bwd(hin, dh, da, db, *, name):
    nck, rows, _ = hin.shape
    w = 2 * S5_P

    def body(h_ref, dh_ref, da_ref, db_ref, ds_ref, gda_ref, gdb_ref):
        dirs = ((False, slice(0, w)), (True, slice(w, 2 * w)))
        coef = [(da_ref[:, cols], db_ref[:, cols]) for _, cols in dirs]

        def step(i, carries):
            new = []
            for (rev, cols), (a, b), (g, ga, gb) in zip(dirs, coef, carries):
                k = i if rev else (nck - 1 - i)
                ds_ref[k, :, cols] = g
                h = h_ref[k, :, cols]
                new.append((dh_ref[k, :, cols] + a * g + _swap(b * g), ga + g * h, gb + g * _swap(h)))
            return tuple(new)

        z = jnp.zeros((rows, w), F32)
        res = lax.fori_loop(0, nck, step, ((z, z, z), (z, z, z)), unroll=2)
        for (_, cols), (_, ga, gb) in zip(dirs, res):
            gda_ref[:, cols] = ga
            gdb_ref[:, cols] = gb

    rt = min(CARRY_ROWS, rows)
    big, small = pl.BlockSpec((nck, rt, 2 * w), lambda i: (0, i, 0)), pl.BlockSpec((rt, 2 * w), lambda i: (i, 0))
    rows = rt
    return pl.pallas_call(
        body, name=name, grid=(hin.shape[1] // rt,), in_specs=[big, big, small, small], out_specs=[big, small, small],
        out_shape=[jax.ShapeDtypeStruct(hin.shape, F32), jax.ShapeDtypeStruct(da.shape, F32),
                   jax.ShapeDtypeStruct(da.shape, F32)],
        compiler_params=_params(("parallel",)))(hin, dh, da, db)


def s5_out(u, hin, tt, mt, *, name):
    g, _, r, _ = u.shape

    def body(u_ref, h_ref, t_ref, m_ref, o_ref):
        u_v, h_v = _cat(u_ref), h_ref[...]
        for half in range(S5_HALVES):
            cols = slice(half * LANES, (half + 1) * LANES)
            o_ref[half] = _bd(u_v, t_ref[half], 1, 0) + _bd(h_v, m_ref[:, cols], 1, 0)

    return pl.pallas_call(
        body, name=name, grid=(g,),
        in_specs=[_cspec(r), _gspec(r, 4 * S5_P), _gspec(S5_HALVES, S5_QC, LANES), _gspec(4 * S5_P, S5_QC)],
        out_specs=_cspec(r), out_shape=jax.ShapeDtypeStruct((g, S5_HALVES, r, LANES), F32),
        compiler_params=_params(("parallel",)))(u, hin, tt, mt)


def s5_out_bwd(dy, u, hin, tt, mt, *, name):
    g, _, r, _ = u.shape

    def body(dy_ref, u_ref, h_ref, t_ref, m_ref, dh_ref, dt_ref, dm_ref, du_ref):
        dy_v, u_v = _cat(dy_ref), _cat(u_ref)
        dh_ref[...] = _bd(dy_v, m_ref[...], 1, 1)
        dm_ref[...] = _bd(h_ref[...], dy_v, 0, 0)
        du = None
        for half in range(S5_HALVES):
            dy_h = dy_ref[half]
            dt_ref[half] = _bd(u_v, dy_h, 0, 0)
            part = _bd(dy_h, t_ref[half], 1, 1)
            du = part if du is None else du + part
        _put(du_ref, du)

    tspec = _gspec(S5_HALVES, S5_QC, LANES)
    return pl.pallas_call(
        body, name=name, grid=(g,),
        in_specs=[_cspec(r), _cspec(r), _gspec(r, 4 * S5_P), tspec, _gspec(4 * S5_P, S5_QC)],
        out_specs=[_gspec(r, 4 * S5_P), tspec, _gspec(4 * S5_P, S5_QC), _cspec(r)],
        out_shape=[jax.ShapeDtypeStruct((g, r, 4 * S5_P), F32), jax.ShapeDtypeStruct((g, S5_HALVES, S5_QC, LANES), F32),
                   jax.ShapeDtypeStruct((g, 4 * S5_P, S5_QC), F32), jax.ShapeDtypeStruct((g, S5_HALVES, r, LANES), F32)],
        compiler_params=_params(("parallel",)))(dy, u, hin, tt, mt)


def s5_state_in_bwd(ds, u, wt, du1, *, name):
    g, _, r, _ = u.shape

    def body(ds_ref, u_ref, w_ref, du1_ref, du_ref, dw_ref):
        ds_v = ds_ref[...]
        _put(du_ref, _cat(du1_ref) + _bd(ds_v, w_ref[...], 1, 1))
        dw_ref[...] = _bd(_cat(u_ref), ds_v, 0, 0)

    return pl.pallas_call(
        body, name=name, grid=(g,),
        in_specs=[_gspec(r, 4 * S5_P), _cspec(r), _gspec(S5_QC, 4 * S5_P), _cspec(r)],
        out_specs=[_cspec(r), _gspec(S5_QC, 4 * S5_P)],
        out_shape=[jax.ShapeDtypeStruct((g, S5_HALVES, r, LANES), F32), jax.ShapeDtypeStruct((g, S5_QC, 4 * S5_P), F32)],
        compiler_params=_params(("parallel",)))(ds, u, wt, du1)


def _s5_post(ypre, u, dvec, wv, wg, bv, bg, nw):
    g = _gelu(ypre + dvec * u)
    out = (dot_nn(g, wv) + bv) * jax.nn.sigmoid(dot_nn(g, wg) + bg)
    return (_rms(out, nw),)


def _ssd_post(y, z, nw):
    return (_rms(y * _silu(z), nw),)


def _to_carry(s, bsz):
    nck = s.shape[1] // bsz
    return jnp.transpose(s.reshape(S5_G, bsz, nck, -1), (2, 0, 1, 3)).reshape(nck, S5_G * bsz, -1)


def _from_carry(h, bsz):
    nck = h.shape[0]
    return jnp.transpose(h.reshape(nck, S5_G, bsz, -1), (1, 2, 0, 3)).reshape(S5_G, bsz * nck, -1)


def _block_diag(w):
    eye = jnp.eye(S5_G, dtype=w.dtype)
    return jnp.einsum('gcd,gh->gchd', w, eye).reshape(S5_W, S5_W)


def _diag_blocks(w):
    v = w.reshape(S5_G, S5_C, S5_G, S5_C)
    return v[jnp.arange(S5_G), :, jnp.arange(S5_G), :]


def _dt_rows(dt, bsz):
    seq = dt.shape[0] // bsz
    return jnp.transpose(dt.reshape(bsz, seq, 2, SGROUPS, HPG), (0, 3, 2, 4, 1)).reshape(bsz, SGROUPS, 2 * HPG, seq)


def _dt_from_rows(dr):
    bsz, _, _, seq = dr.shape
    return jnp.transpose(dr.reshape(bsz, SGROUPS, 2, HPG, seq), (0, 4, 2, 1, 3)).reshape(bsz * seq, 2 * HEADS)


def _head_params(f, b):
    return jnp.concatenate([f.reshape(SGROUPS, HPG), b.reshape(SGROUPS, HPG)], axis=1)[:, :, None]


def _head_grads(gr):
    v = gr.sum(0)[:, :, 0]
    return v[:, :HPG].reshape(HEADS), v[:, HPG:].reshape(HEADS)


def local_step(x, target, w):
    bsz, seq, d = x.shape
    t = bsz * seq
    x2, tgt2 = x.reshape(t, d), target.reshape(t, d)
    g = {}
    row = lambda v: v.reshape(1, -1)
    bf = lambda v: v.astype(BF16)

    w_in = _unshard(bf(w['w_in']), SHARDED['w_in'])
    cuts = [0, SSD_W, SSD_W + XBC_W, SSD_W + XBC_W + 2 * HEADS, w_in.shape[1]]
    w_in_parts = [w_in[:, a:b] for a, b in zip(cuts[:-1], cuts[1:])]
    norm_mix = row(w['norm_mix_w']) + w.get('token', 0.0)
    (hn,) = rowmap_fwd(lambda a, nw: (_rms(a, nw),), [x2], [norm_mix], [(d, BF16)], name="rms_mix")
    z, xbc, dt, u = [matmul_sum([hn], [p], tm=1024, name=f"in_proj_{i}") for i, p in enumerate(w_in_parts)]

    conv_w, conv_b = _unshard(w['ssd_conv_w'], SHARDED['ssd_conv_w']), row(w['ssd_conv_b'])
    act = ssd_conv_fwd(xbc, conv_w, conv_b, bsz=bsz, name="ssd_conv")
    dtr = _dt_rows(dt, bsz)
    prs = (_head_params(w['ssd_dt_bias_fwd'], w['ssd_dt_bias_bwd']),
           _head_params(w['ssd_a_log_fwd'], w['ssd_a_log_bwd']),
           _head_params(w['ssd_d'], jnp.zeros_like(w['ssd_d'])))
    act3 = act.reshape(bsz, seq, XBC_W)
    y_scan, ssd_states = ssd_scan_fwd(act3, dtr, prs, name="ssd_scan")
    y_scan = y_scan.reshape(t, SSD_W)
    ssd_nw = row(w['ssd_norm_w'])
    (y_ssd,) = rowmap_fwd(_ssd_post, [y_scan, z], [ssd_nw], [(SSD_W, BF16)], name="ssd_post")

    s5_names = ['s5_lambda_re_fwd', 's5_lambda_im_fwd', 's5_log_step_fwd', 's5_lambda_re_bwd', 's5_lambda_im_bwd',
                's5_log_step_bwd', 's5_b_re', 's5_b_im', 's5_c_re_fwd', 's5_c_im_fwd', 's5_c_re_bwd', 's5_c_im_bwd']
    (kt, wt, mt, da, db), s5_pull = jax.vjp(_s5_operators, *[w[n] for n in s5_names])
    tt_b, wt_b, mt_b = s5_toeplitz(kt, name="s5_toeplitz"), bf(wt), bf(mt)
    da_r, db_r = jnp.repeat(da, bsz, axis=0), jnp.repeat(db, bsz, axis=0)
    uc = to_chunks(u, name="s5_to_chunks_u")
    s_in = _to_carry(s5_state_in(uc, wt_b, name="s5_state_in"), bsz)
    hin_c = s5_carry_fwd(s_in, da_r, db_r, name="s5_carry")
    hin = _from_carry(hin_c, bsz)
    ypre = from_chunks(s5_out(uc, hin, tt_b, mt_b, name="s5_out"), name="s5_from_chunks_y")
    glu_w = w['s5_glu_w']
    s5_par = [row(w['s5_d']), _block_diag(glu_w[:, :, :S5_C]), _block_diag(glu_w[:, :, S5_C:]),
              row(w['s5_glu_b'][:, :S5_C]), row(w['s5_glu_b'][:, S5_C:]), row(w['s5_norm_w'])]
    (y_s5,) = rowmap_fwd(_s5_post, [ypre, u], s5_par, [(S5_W, BF16)], name="s5_post")

    if 'late' in w:
        w = {**w, **w['late'](y_s5)}
    w_out = bf(w['w_out']).reshape(SSD_W + S5_W, d)
    h1 = matmul_sum([y_ssd, y_s5], [w_out[:SSD_W], w_out[SSD_W:]], add=x2, name="out_proj")
    norm_ffn = row(w['norm_ffn_w'])
    (hn2,) = rowmap_fwd(lambda a, nw: (_rms(a, nw),), [h1], [norm_ffn], [(d, BF16)], name="rms_ffn")
    pad_c = FFN_PAD - FFN_BLK
    half = N_DEV // 2
    w_up3 = jnp.pad(bf(w['ffn_w_up']), ((0, 0), (0, 0), (0, pad_c)))
    w_down = jnp.pad(bf(w['ffn_w_down']).reshape(half, FFN_BLK, d), ((0, 0), (0, pad_c), (0, 0)))
    w_down = w_down.reshape(half * FFN_PAD, d)
    fconv_w = jnp.pad(w['ffn_conv_w'], ((0, 0), (0, 0), (0, pad_c)))
    fconv_w = jnp.transpose(fconv_w, (1, 0, 2)).reshape(FCONV, N_DEV * FFN_PAD)
    fconv_b = row(jnp.pad(w['ffn_conv_b'].reshape(N_DEV, FFN_BLK), ((0, 0), (0, pad_c))))
    up = matmul_cols(hn2, w_up3, out_dtype=BF16, name="ffn_up")
    fact = ffn_act_fwd(up, fconv_w, fconv_b, bsz=bsz, name="ffn_act")
    h2 = matmul_sum([fact], [w_down], add=h1, name="ffn_down")
    loss, dh2, g_nf = loss_head(h2, tgt2, row(w['norm_final_w']), name="loss_head")
    g['norm_final_w'] = g_nf.reshape(-1)

    dfact = matmul_sum([dh2], [w_down], nt=True, tm=1024, name="ffn_down_dx")
    g_down = matmul_tn(fact, dh2, name="ffn_down_dw").reshape(half, FFN_PAD, d)[:, :FFN_BLK]
    g['ffn_w_down'] = g_down.reshape(N_DEV, FFN_BLK // 2, d)
    dval, dgate, dwv, dwg, dbv, dbg = ffn_act_bwd(up, dfact, fconv_w, fconv_b, bsz=bsz, name="ffn_act_bwd")
    g_cw = jnp.concatenate([dwv, dwg], axis=1).reshape(FCONV, N_DEV, FFN_PAD)[:, :, :FFN_BLK]
    g['ffn_conv_w'] = jnp.transpose(g_cw, (1, 0, 2))
    g['ffn_conv_b'] = jnp.concatenate([dbv, dbg], axis=1).reshape(N_DEV, FFN_PAD)[:, :FFN_BLK].reshape(-1)
    windows = [(dval, FFN_PAD, p) for p in range(half)] + [(dgate, FFN_PAD, p) for p in range(half)]
    dhn2 = matmul_sum(windows, [(w_up3, p) for p in range(N_DEV)], nt=True, name="ffn_up_dx")
    g['ffn_w_up'] = jnp.concatenate([matmul_tn(hn2, dval, out_blocks=half, name="ffn_up_dw_val"),
                                     matmul_tn(hn2, dgate, out_blocks=half, name="ffn_up_dw_gate")],
                                    axis=0)[:, :, :FFN_BLK]
    send_early = w.get('on_grads')
    if send_early:
        norm_ffn = norm_ffn + send_early(g, ['ffn_w_up', 'ffn_w_down'])
    dh1, g_nffn = rowmap_bwd(lambda a, nw: (_rms(a, nw),), [h1], [norm_ffn], [dhn2], add=dh2, name="rms_ffn_bwd")
    g['norm_ffn_w'] = g_nffn.reshape(-1)

    dycat = matmul_sum([dh1], [w_out], nt=True, tm=1024, name="out_proj_dx")
    g['w_out'] = jnp.concatenate([matmul_tn(y_ssd, dh1, name="out_proj_dw_ssd"),
                                  matmul_tn(y_s5, dh1, name="out_proj_dw_s5")], axis=0).reshape(w['w_out'].shape)
    if send_early:
        ssd_nw = ssd_nw + send_early(g, ['w_out'])
    dy_scan, dz, g_snw = rowmap_bwd(_ssd_post, [y_scan, z], [ssd_nw], [(dycat, SSD_W, 0)], name="ssd_post_bwd")
    g['ssd_norm_w'] = g_snw.reshape(-1)
    dypre, du_a, g_d, g_wv, g_wg, g_bv, g_bg, g_s5nw = rowmap_bwd(
        _s5_post, [ypre, u], s5_par, [(dycat, S5_W, SSD_W // S5_W)], name="s5_post_bwd")
    g['s5_d'], g['s5_norm_w'] = g_d.reshape(-1), g_s5nw.reshape(-1)
    g['s5_glu_w'] = jnp.concatenate([_diag_blocks(g_wv), _diag_blocks(g_wg)], axis=-1)
    g['s5_glu_b'] = jnp.concatenate([g_bv.reshape(S5_G, S5_C), g_bg.reshape(S5_G, S5_C)], axis=-1)

    dyc = to_chunks(dypre, name="s5_to_chunks_dy")
    dhin, dtt, dmt, du1 = s5_out_bwd(dyc, uc, hin, tt_b, mt_b, name="s5_out_bwd")
    ds_c, gda, gdb = s5_carry_bwd(hin_c, _to_carry(dhin, bsz), da_r, db_r, name="s5_carry_bwd")
    duc, dwt = s5_state_in_bwd(_from_carry(ds_c, bsz), uc, wt_b, du1, name="s5_state_in_bwd")
    du = from_chunks(duc, add=du_a, name="s5_from_chunks_du")
    fold = lambda v: v.reshape(S5_G, bsz, -1).sum(1)
    dkt = s5_toeplitz_bwd(dtt, name="s5_toeplitz_bwd")
    for n, gv in zip(s5_names, s5_pull((dkt, dwt, dmt, fold(gda), fold(gdb)))):
        g[n] = gv

    dxs, dbm, dcm, ddtr, gbr, gar, gdk = ssd_scan_bwd(
        act3, dtr, prs, ssd_states, dy_scan.reshape(bsz, seq, SSD_W), name="ssd_scan_bwd")
    g['ssd_dt_bias_fwd'], g['ssd_dt_bias_bwd'] = _head_grads(gbr)
    g['ssd_a_log_fwd'], g['ssd_a_log_bwd'] = _head_grads(gar)
    g['ssd_d'] = _head_grads(gdk)[0]
    dparts_act = [v.reshape(t, v.shape[-1]) for v in (dxs, dbm, dcm)]
    dxbc, g_cw, g_cb = ssd_conv_bwd(xbc, dparts_act, conv_w, conv_b, bsz=bsz, name="ssd_conv_bwd")
    g['ssd_conv_w'] = _shard_rows(g_cw, SHARDED['ssd_conv_w']).reshape(w['ssd_conv_w'].shape)
    g['ssd_conv_b'] = g_cb.reshape(-1)
    ddt = _dt_from_rows(ddtr)

    dparts = [dz, dxbc, ddt, du]
    g_in = jnp.concatenate([matmul_tn(hn, dp, name=f"in_proj_dw_{i}") for i, dp in enumerate(dparts)], axis=1)
    g['w_in'] = _shard_rows(g_in, SHARDED['w_in']).reshape(w['w_in'].shape)
    if send_early:
        dparts[2] = ddt + send_early(g, ['w_in'], loss=loss)
    dhn = matmul_sum(dparts, w_in_parts, nt=True, name="in_proj_dx")
    dx, g_nmix = rowmap_bwd(lambda a, nw: (_rms(a, nw),), [x2], [norm_mix], [dhn], add=dh1, name="rms_mix_bwd")
    g['norm_mix_w'] = g_nmix.reshape(-1)
    return loss, dx.reshape(bsz, seq, d), g


ANY = pl.BlockSpec(memory_space=pl.ANY)


def all_gather(shards, *, name):
    n = len(shards)

    def body(*refs):
        x_refs, out_refs = refs[:n], refs[n:2 * n]
        send_sems, recv_sems, local_sems = refs[2 * n:]
        x, y, c = lax.axis_index("x"), lax.axis_index("y"), lax.axis_index("c")
        me, sibling = (x, y, c), (x, y, 1 - c)
        chips = [(1 - x, y), (x, 1 - y), (1 - x, 1 - y)]

        def copy(k, j, block, to, own=False):
            dst = out_refs[j].at[4 * block[0] + 2 * block[1] + block[2]]
            return pltpu.make_async_remote_copy(
                src_ref=x_refs[j] if own else dst, dst_ref=dst,
                send_sem=send_sems.at[k, j], recv_sem=recv_sems.at[k, j], device_id=to, device_id_type=MESH)

        mine = [pltpu.make_async_copy(x_refs[j], out_refs[j].at[4 * x + 2 * y + c], local_sems.at[j]) for j in range(n)]
        first = [copy(0, j, me, sibling, own=True) for j in range(n)]
        first += [copy(1 + i, j, me, (*chip, c), own=True) for i, chip in enumerate(chips) for j in range(n)]
        for cp in mine + first:
            cp.start()
        passed = []
        for i, chip in enumerate(chips):
            for j in range(n):
                copy(1 + i, j, (*chip, c), me).wait_recv()
                passed.append(copy(4 + i, j, (*chip, c), sibling))
                passed[-1].start()
        for j in range(n):
            copy(0, j, sibling, me).wait_recv()
        for i, chip in enumerate(chips):
            for j in range(n):
                copy(4 + i, j, (*chip, 1 - c), me).wait_recv()
        for cp in first + passed:
            cp.wait_send()
        for cp in mine:
            cp.wait()

    return pl.pallas_call(
        body, name=name, out_shape=[jax.ShapeDtypeStruct((N_DEV,) + s.shape, s.dtype) for s in shards],
        in_specs=[ANY] * n, out_specs=[ANY] * n,
        scratch_shapes=[pltpu.SemaphoreType.DMA((7, n)), pltpu.SemaphoreType.DMA((7, n)),
                        pltpu.SemaphoreType.DMA((n,))],
    )(*shards)


HBM_SPEC = pl.BlockSpec(memory_space=pltpu.HBM)
SEM_SPEC = pl.BlockSpec(memory_space=pltpu.SEMAPHORE)
SPLIT_PARAMS = pltpu.CompilerParams(has_side_effects=pltpu.SideEffectType.DATAFLOW_SIDE_EFFECTING)


def _peer_copies(src_refs, land_refs, send_sems, recv_sems, indexed):
    x, y, c = lax.axis_index("x"), lax.axis_index("y"), lax.axis_index("c")
    me = 4 * x + 2 * y + c
    copies = []
    for k in range(1, N_DEV):
        px = (1 - x) if k & 4 else x
        py = (1 - y) if k & 2 else y
        pc = (1 - c) if k & 1 else c
        for j, (src, land) in enumerate(zip(src_refs, land_refs)):
            sem = (k - 1) * len(src_refs) + j
            copies.append(pltpu.make_async_remote_copy(
                src_ref=src.at[4 * px + 2 * py + pc] if indexed else src, dst_ref=land.at[me],
                send_sem=send_sems.at[sem], recv_sem=recv_sems.at[sem],
                device_id=(px, py, pc), device_id_type=MESH))
    return copies


def scatter_start(srcs, *, name, indexed):
    n = len(srcs)
    lands = [lax.empty(s.shape if indexed else (N_DEV,) + s.shape, s.dtype) for s in srcs]

    def body(*refs):
        send_sems, recv_sems = refs[2 * n], refs[2 * n + 1]
        for cp in _peer_copies(refs[:n], refs[n:2 * n], send_sems, recv_sems, indexed):
            cp.start()
        refs[-1][...] = jnp.zeros_like(refs[-1])

    hbm = lambda a: pltpu.HBM(a.shape, a.dtype)
    sems = pltpu.SemaphoreType.DMA(((N_DEV - 1) * n,))
    res = pl.pallas_call(
        body, name=name,
        out_shape=(sems, sems, *[hbm(a) for a in srcs + lands], jax.ShapeDtypeStruct((8, LANES), F32)),
        in_specs=[HBM_SPEC] * (2 * n),
        out_specs=(SEM_SPEC, SEM_SPEC, *[HBM_SPEC] * (2 * n), pl.BlockSpec(memory_space=pltpu.VMEM)),
        input_output_aliases={i: 2 + i for i in range(2 * n)}, compiler_params=SPLIT_PARAMS,
    )(*[pltpu.with_memory_space_constraint(a, pltpu.HBM) for a in srcs + lands])
    return res[0], res[1], list(res[2:2 + n]), list(res[2 + n:2 + 2 * n]), res[-1]


def scatter_wait(send_sems, recv_sems, srcs, lands, after, *, name, indexed):
    n = len(srcs)

    def body(*refs):
        for cp in _peer_copies(refs[:n], refs[n:2 * n], refs[2 * n], refs[2 * n + 1], indexed):
            cp.wait_send()
            cp.wait_recv()

    hbm = lambda a: pltpu.HBM(a.shape, a.dtype)
    res = pl.pallas_call(
        body, name=name, out_shape=tuple(hbm(a) for a in srcs + lands),
        in_specs=[HBM_SPEC] * (2 * n) + [SEM_SPEC, SEM_SPEC, ANY], out_specs=tuple([HBM_SPEC] * (2 * n)),
        input_output_aliases={i: i for i in range(2 * n)}, compiler_params=SPLIT_PARAMS,
    )(*srcs, *lands, send_sems, recv_sems, after)
    return list(res[:n]), list(res[n:])


def _adam_rows(r, c):
    fits = [t for t in range(8, r + 1, 8) if r % t == 0 and N_DEV * t * c * 4 <= 6 * 2 ** 20]
    return max(fits) if fits else r


def adamw(recv, w, m, v, *, name):
    _, r, n = recv.shape
    tr = _adam_rows(r, n)

    def body(r_ref, w_ref, m_ref, v_ref, g_ref, d_ref, nm_ref, nv_ref):
        g = r_ref[0].astype(F32)
        for s in range(1, N_DEV):
            g = g + r_ref[s].astype(F32)
        m_new = ADAM_B1 * m_ref[...] + (1.0 - ADAM_B1) * g
        v_new = ADAM_B2 * v_ref[...] + (1.0 - ADAM_B2) * jnp.square(g)
        m_hat = m_new / (1.0 - ADAM_B1 ** ADAM_STEP)
        v_hat = v_new / (1.0 - ADAM_B2 ** ADAM_STEP)
        g_ref[...] = g
        d_ref[...] = -ADAM_LR * (m_hat / (jnp.sqrt(v_hat) + ADAM_EPS) + ADAM_WD * w_ref[...])
        nm_ref[...] = m_new
        nv_ref[...] = v_new

    blk = pl.BlockSpec((tr, n), lambda i: (i, 0))
    return pl.pallas_call(
        body, name=name, grid=(r // tr,), in_specs=[pl.BlockSpec((N_DEV, tr, n), lambda i: (0, i, 0)), blk, blk, blk],
        out_specs=[blk] * 4, out_shape=[jax.ShapeDtypeStruct((r, n), F32)] * 4,
        compiler_params=_params(("parallel",)))(recv, w, m, v)


def _shard_rows(full, axis):
    if axis == 0:
        return full.reshape(N_DEV, -1)
    r, c = full.shape
    return jnp.transpose(full.reshape(r, N_DEV, c // N_DEV), (1, 0, 2)).reshape(N_DEV, -1)


def _unshard(blocks, axis):
    if axis == 0:
        return blocks.reshape(-1, blocks.shape[-1])
    return jnp.transpose(blocks, (1, 0, 2)).reshape(blocks.shape[1], -1)


def kernel(x, norm_mix_w, w_in, ssd_conv_w, ssd_conv_b, ssd_dt_bias_fwd, ssd_dt_bias_bwd, ssd_a_log_fwd, ssd_a_log_bwd, ssd_d, ssd_norm_w, s5_lambda_re_fwd, s5_lambda_im_fwd, s5_log_step_fwd, s5_lambda_re_bwd, s5_lambda_im_bwd, s5_log_step_bwd, s5_b_re, s5_b_im, s5_c_re_fwd, s5_c_im_fwd, s5_c_re_bwd, s5_c_im_bwd, s5_d, s5_glu_w, s5_glu_b, s5_norm_w, w_out, norm_ffn_w, ffn_w_up, ffn_conv_w, ffn_conv_b, ffn_w_down, norm_final_w, loss_target, m_norm_mix_w, m_w_in, m_ssd_conv_w, m_ssd_conv_b, m_ssd_dt_bias_fwd, m_ssd_dt_bias_bwd, m_ssd_a_log_fwd, m_ssd_a_log_bwd, m_ssd_d, m_ssd_norm_w, m_s5_lambda_re_fwd, m_s5_lambda_im_fwd, m_s5_log_step_fwd, m_s5_lambda_re_bwd, m_s5_lambda_im_bwd, m_s5_log_step_bwd, m_s5_b_re, m_s5_b_im, m_s5_c_re_fwd, m_s5_c_im_fwd, m_s5_c_re_bwd, m_s5_c_im_bwd, m_s5_d, m_s5_glu_w, m_s5_glu_b, m_s5_norm_w, m_w_out, m_norm_ffn_w, m_ffn_w_up, m_ffn_conv_w, m_ffn_conv_b, m_ffn_w_down, m_norm_final_w, v_norm_mix_w, v_w_in, v_ssd_conv_w, v_ssd_conv_b, v_ssd_dt_bias_fwd, v_ssd_dt_bias_bwd, v_ssd_a_log_fwd, v_ssd_a_log_bwd, v_ssd_d, v_ssd_norm_w, v_s5_lambda_re_fwd, v_s5_lambda_im_fwd, v_s5_log_step_fwd, v_s5_lambda_re_bwd, v_s5_lambda_im_bwd, v_s5_log_step_bwd, v_s5_b_re, v_s5_b_im, v_s5_c_re_fwd, v_s5_c_im_fwd, v_s5_c_re_bwd, v_s5_c_im_bwd, v_s5_d, v_s5_glu_w, v_s5_glu_b, v_s5_norm_w, v_w_out, v_norm_ffn_w, v_ffn_w_up, v_ffn_conv_w, v_ffn_conv_b, v_ffn_w_down, v_norm_final_w):
    args = dict(locals())
    strip = lambda n, v: v if n == 'norm_final_w' else v[0]
    w = {n: strip(n, args[n]) for n in WEIGHTS}

    mats = ['w_in', 'w_out', 'ffn_w_up', 'ffn_w_down']
    convs = ['ssd_conv_w', 'ffn_conv_w']
    shard = lambda n: w[n].astype(BF16) if n in mats else w[n]
    early, late = ['w_in', 'ssd_conv_w'], ['w_out', 'ffn_w_up', 'ffn_w_down', 'ffn_conv_w']
    full = dict(w)
    full.update(zip(early, all_gather([shard(n) for n in early], name="weight_all_gather")))
    ssem, rsem, src_thru, land_thru, token = scatter_start([shard(n) for n in late], name="weight_gather_start",
                                                           indexed=False)
    me = 4 * lax.axis_index("x") + 2 * lax.axis_index("y") + lax.axis_index("c")

    def late_weights(after):
        own, landed = scatter_wait(ssem, rsem, src_thru, land_thru, after, name="weight_gather_wait", indexed=False)
        return {n: lax.dynamic_update_index_in_dim(l, o, me, 0) for n, o, l in zip(late, own, landed)}

    full['late'], full['token'] = late_weights, token[:1, :1]

    pending = []
    last = 'norm_mix_w'
    small = convs + [n for n in WEIGHTS if n not in SHARDED and n != last]
    total = sum(w[n].size for n in small) + 1
    nrow = -(-total // (PACK_ROWS * LANES)) * PACK_ROWS

    def send_early(grads, names, loss=None):
        srcs = [grads[n].astype(BF16) for n in names]
        if loss is not None:
            pieces = [grads[n].reshape(N_DEV, -1) if n in SHARDED else
                      jnp.broadcast_to(grads[n].reshape(1, -1), (N_DEV, grads[n].size)) for n in small]
            pieces += [jnp.broadcast_to(loss.reshape(1, 1), (N_DEV, 1)), jnp.zeros((N_DEV, nrow * LANES - total), F32)]
            srcs.append(jnp.concatenate(pieces, axis=1).reshape(N_DEV, nrow, LANES))
            names = names + ['small']
        started = scatter_start(srcs, name="grad_start_" + names[0], indexed=True)
        pending.append((names,) + started[:4])
        return started[4][:1, :1]

    full['on_grads'] = send_early
    loss, grad_x, g = local_step(x, loss_target, full)

    last_send = jnp.broadcast_to(g[last].reshape(1, -1, LANES), (N_DEV, g[last].size // LANES, LANES))
    last_started = scatter_start([last_send], name="grad_start_" + last, indexed=True)
    recv, outs = {}, [{}, {}, {}, {}]

    def arrived(names, started, after):
        own, landed = scatter_wait(*started, after, name="grad_wait_" + names[0], indexed=True)
        for n, o, l in zip(names, own, landed):
            recv[n] = lax.dynamic_update_index_in_dim(l, lax.dynamic_index_in_dim(o, me, 0, keepdims=False), me, 0)

    def update(n):
        shape = recv[n].shape[1:]
        res = adamw(recv[n], *[strip(n, args[p + n]).reshape(shape) for p in ('', 'm_', 'v_')], name="adamw_" + n)
        for o, p in zip(outs, res):
            o[n] = p.reshape(args[n].shape)

    for names, *started in pending:
        arrived(names, started, last_started[4])
    for n in mats:
        update(n)

    def pack(prefix):
        vals = [strip(n, args[prefix + n]).reshape(-1) for n in small]
        return jnp.pad(jnp.concatenate(vals), (0, nrow * LANES - total + 1)).reshape(nrow, LANES)

    packed = adamw(recv['small'], pack(''), pack('m_'), pack('v_'), name="adamw_small")
    arrived([last], last_started[:4], packed[1])
    update(last)
    packed = [p.reshape(-1) for p in packed]
    off = 0
    for n in small:
        size = w[n].size
        for o, p in zip(outs, packed):
            o[n] = p[off:off + size].reshape(args[n].shape)
        off += size
    loss_out = packed[0][off].reshape(())
    return (loss_out, grad_x, *[o[n] for o in outs for n in WEIGHTS])
```

```python
import functools

import jax
import jax.numpy as jnp
from jax import lax
from jax.experimental import pallas as pl
from jax.experimental.pallas import tpu as pltpu

F32, BF16 = jnp.float32, jnp.bfloat16
N_DEV = 8
D_MODEL = 1024
SSD_W, HEADS, HDIM, SGROUPS, HPG, NSTATE, SCONV, QC = 1024, 16, 64, 4, 4, 128, 5, 128
XBC_W = SSD_W + 2 * SGROUPS * NSTATE
S5_W, S5_G, S5_C, S5_P, S5_Q = 512, 32, 16, 64, 16
S5_QC = S5_Q * S5_C
CARRY_ROWS = 32
DFF, FCONV = 2816, 3
FFN_BLK, FFN_PAD = 704, 768
EPS = 1e-6
ADAM_LR, ADAM_B1, ADAM_B2, ADAM_EPS, ADAM_WD, ADAM_STEP = 0.001, 0.9, 0.999, 1e-08, 0.01, 10
LANES = 128
MESH = pl.DeviceIdType.MESH

WEIGHTS = ['norm_mix_w', 'w_in', 'ssd_conv_w', 'ssd_conv_b', 'ssd_dt_bias_fwd', 'ssd_dt_bias_bwd', 'ssd_a_log_fwd',
           'ssd_a_log_bwd', 'ssd_d', 'ssd_norm_w', 's5_lambda_re_fwd', 's5_lambda_im_fwd', 's5_log_step_fwd',
           's5_lambda_re_bwd', 's5_lambda_im_bwd', 's5_log_step_bwd', 's5_b_re', 's5_b_im', 's5_c_re_fwd', 's5_c_im_fwd',
           's5_c_re_bwd', 's5_c_im_bwd', 's5_d', 's5_glu_w', 's5_glu_b', 's5_norm_w', 'w_out', 'norm_ffn_w', 'ffn_w_up',
           'ffn_conv_w', 'ffn_conv_b', 'ffn_w_down', 'norm_final_w']
SHARDED = {'w_in': 1, 'ssd_conv_w': 1, 'w_out': 0, 'ffn_w_up': 1, 'ffn_conv_w': 1, 'ffn_w_down': 0}
FULL_SHAPE = {'w_in': (1024, 3616), 'ssd_conv_w': (5, 2048), 'w_out': (1536, 1024), 'ffn_w_up': (1024, 5632),
              'ffn_conv_w': (3, 5632), 'ffn_w_down': (2816, 1024)}
PACK_ROWS = 512


def _pick(n, cap=1536):
    if n <= cap:
        return n
    return max(t for t in range(LANES, cap + 1, LANES) if n % t == 0)


def _params(sem):
    return pltpu.CompilerParams(dimension_semantics=sem)


def _bd(a, b, ca, cb):
    return lax.dot_general(a.astype(BF16), b.astype(BF16), (((ca,), (cb,)), ((), ())), preferred_element_type=F32)


@jax.custom_vjp
def dot_nn(a, b):
    return _bd(a, b, 1, 0)


dot_nn.defvjp(lambda a, b: (_bd(a, b, 1, 0), (a, b)),
              lambda r, g: (_bd(g, r[1], 1, 1).astype(r[0].dtype), _bd(r[0], g, 0, 0).astype(r[1].dtype)))


@jax.custom_vjp
def dot_nt(a, b):
    return _bd(a, b, 1, 1)


dot_nt.defvjp(lambda a, b: (_bd(a, b, 1, 1), (a, b)),
              lambda r, g: (_bd(g, r[1], 1, 0).astype(r[0].dtype), _bd(g, r[0], 0, 0).astype(r[1].dtype)))


@jax.custom_vjp
def dot_tn(a, b):
    return _bd(a, b, 0, 0)


dot_tn.defvjp(lambda a, b: (_bd(a, b, 0, 0), (a, b)),
              lambda r, g: (_bd(r[1], g, 1, 1).astype(r[0].dtype), _bd(r[0], g, 1, 0).astype(r[1].dtype)))


def _rows2(v):
    h = v.shape[0] // 2
    return v[:h], v[h:]


def _cols2(v):
    h = v.shape[1] // 2
    return v[:, :h], v[:, h:]


@jax.custom_vjp
def dot2_nn(la, lb, x):
    return _rows2(_bd(jnp.concatenate([la, lb], axis=0), x, 1, 0))


def _dot2_nn_bwd(res, g):
    la, lb, x = res
    gcat, lcat = jnp.concatenate(g, axis=0), jnp.concatenate([la, lb], axis=0)
    return (*_rows2(_bd(gcat, x, 1, 1)), _bd(lcat, gcat, 0, 0))


dot2_nn.defvjp(lambda la, lb, x: (dot2_nn(la, lb, x), (la, lb, x)), _dot2_nn_bwd)


@jax.custom_vjp
def dot_nt2(c, p0, p1):
    return _cols2(_bd(c, jnp.concatenate([p0, p1], axis=0), 1, 1))


def _dot_nt2_bwd(res, g):
    c, p0, p1 = res
    gcat = jnp.concatenate(g, axis=1)
    return (_bd(gcat, jnp.concatenate([p0, p1], axis=0), 1, 0), *_rows2(_bd(gcat, c, 0, 0)))


dot_nt2.defvjp(lambda c, p0, p1: (dot_nt2(c, p0, p1), (c, p0, p1)), _dot_nt2_bwd)


@jax.custom_vjp
def dot_tn2(a0, a1, b):
    return _rows2(_bd(jnp.concatenate([a0, a1], axis=1), b, 0, 0))


def _dot_tn2_bwd(res, g):
    a0, a1, b = res
    gcat, acat = jnp.concatenate(g, axis=0), jnp.concatenate([a0, a1], axis=1)
    return (*_cols2(_bd(b, gcat, 1, 1)), _bd(acat, gcat, 1, 0))


dot_tn2.defvjp(lambda a0, a1, b: (dot_tn2(a0, a1, b), (a0, a1, b)), _dot_tn2_bwd)


def _split3(x):
    hi = x.astype(BF16)
    r = x - hi.astype(F32)
    mid = r.astype(BF16)
    lo = (r - mid.astype(F32)).astype(BF16)
    return hi, mid, lo


def _cum_matrix(q, upper):
    ri = lax.broadcasted_iota(jnp.int32, (q, q), 0)
    ci = lax.broadcasted_iota(jnp.int32, (q, q), 1)
    return jnp.where((ci >= ri) if upper else (ci <= ri), 1.0, 0.0).astype(BF16)


def _exact_right(x, mat):
    return sum(jnp.dot(p, mat, preferred_element_type=F32) for p in _split3(x))


@functools.partial(jax.custom_vjp, nondiff_argnums=(1,))
def cum_row(x, rev):
    return _exact_right(x, _cum_matrix(x.shape[1], not rev))


cum_row.defvjp(lambda x, rev: (cum_row(x, rev), None),
               lambda rev, _, g: (_exact_right(g, _cum_matrix(g.shape[1], rev)),))


def _softplus(x):
    return jnp.maximum(x, 0.0) + jnp.log(1.0 + jnp.exp(-jnp.abs(x)))


def _silu(x):
    return x * jax.nn.sigmoid(x)


def _gelu(x):
    return 0.5 * x * (1.0 + jnp.tanh(0.7978845608028654 * (x + 0.044715 * (x * x * x))))


def _rms(x, w):
    xf = x.astype(F32)
    return xf * lax.rsqrt(jnp.mean(xf * xf, axis=-1, keepdims=True) + EPS) * w


def matmul_sum(a_list, b_list, *, name, out_dtype=F32, add=None, tm=512, nt=False, norm_w=None):
    a_arrs = [a[0] if isinstance(a, tuple) else a for a in a_list]
    b_arrs = [b[0] if isinstance(b, tuple) else b for b in b_list]
    m, n = a_arrs[0].shape[0], b_arrs[0].shape[-2 if nt else -1]
    tm, tn, k = min(tm, m), _pick(n), len(a_list)
    assert norm_w is None or tn == n

    def body(*refs):
        acc = None
        for a_ref, b_ref in zip(refs[:k], refs[k:2 * k]):
            p = _bd(a_ref[...], b_ref[...], 1, 1 if nt else 0)
            acc = p if acc is None else acc + p
        if add is not None:
            acc = acc + refs[2 * k][...]
        if norm_w is not None:
            refs[-2][...] = acc.astype(out_dtype)
            refs[-1][...] = _rms(acc, refs[-3][...]).astype(BF16)
        else:
            refs[-1][...] = acc.astype(out_dtype)

    def a_spec(a):
        if isinstance(a, tuple):
            return pl.BlockSpec((tm, a[1]), lambda i, j, blk=a[2]: (i, blk))
        return pl.BlockSpec((tm, a.shape[1]), lambda i, j: (i, 0))

    def b_spec(b):
        arr, p = b if isinstance(b, tuple) else (b, None)
        kk = arr.shape[-1 if nt else -2]
        shape, idx = ((tn, kk), lambda j: (j, 0)) if nt else ((kk, tn), lambda j: (0, j))
        if p is None:
            return pl.BlockSpec(shape, lambda i, j: idx(j))
        return pl.BlockSpec((None,) + shape, lambda i, j, p=p: (p,) + idx(j))

    in_specs = [a_spec(a) for a in a_list] + [b_spec(b) for b in b_list]
    args = a_arrs + b_arrs
    if add is not None:
        in_specs.append(pl.BlockSpec((tm, tn), lambda i, j: (i, j)))
        args.append(add)
    out_spec, out_shape = pl.BlockSpec((tm, tn), lambda i, j: (i, j)), jax.ShapeDtypeStruct((m, n), out_dtype)
    if norm_w is not None:
        in_specs.append(pl.BlockSpec(norm_w.shape, lambda i, j: (0, 0)))
        args.append(norm_w)
        out_spec, out_shape = [out_spec, out_spec], [out_shape, jax.ShapeDtypeStruct((m, n), BF16)]
    return pl.pallas_call(
        body, name=name, grid=(m // tm, n // tn), in_specs=in_specs, out_specs=out_spec, out_shape=out_shape,
        compiler_params=_params(("parallel", "parallel")))(*args)


def matmul_cols(a, b3, *, name, out_dtype=F32, tm=1024):
    m, kk = a.shape
    p, _, nb = b3.shape
    tm, tn = min(tm, m), _pick(nb, 768)
    per = nb // tn

    def body(a_ref, b_ref, o_ref):
        o_ref[...] = _bd(a_ref[...], b_ref[...], 1, 0).astype(out_dtype)

    return pl.pallas_call(
        body, name=name, grid=(m // tm, p * per),
        in_specs=[pl.BlockSpec((tm, kk), lambda i, j: (i, 0)),
                  pl.BlockSpec((None, kk, tn), lambda i, j: (j // per, 0, j % per))],
        out_specs=pl.BlockSpec((tm, tn), lambda i, j: (i, j)),
        out_shape=jax.ShapeDtypeStruct((m, p * nb), out_dtype),
        compiler_params=_params(("parallel", "parallel")))(a, b3)


def matmul_tn(a, b, *, name, tm=1024, out_blocks=None):
    m, k = a.shape
    n = b.shape[1]
    nb = n // (out_blocks or 1)
    tm, tk, tn = min(tm, m), _pick(k), _pick(nb, 768 if out_blocks else 1536)
    per = nb // tn

    def body(a_ref, b_ref, o_ref):
        @pl.when(pl.program_id(2) == 0)
        def _():
            o_ref[...] = jnp.zeros_like(o_ref)

        o_ref[...] += _bd(a_ref[...], b_ref[...], 0, 0)

    if out_blocks:
        out_spec = pl.BlockSpec((None, tk, tn), lambda i, j, t: (j // per, i, j % per))
        out_shape = jax.ShapeDtypeStruct((out_blocks, k, nb), F32)
    else:
        out_spec = pl.BlockSpec((tk, tn), lambda i, j, t: (i, j))
        out_shape = jax.ShapeDtypeStruct((k, n), F32)
    return pl.pallas_call(
        body, name=name, grid=(k // tk, n // tn, m // tm),
        in_specs=[pl.BlockSpec((tm, tk), lambda i, j, t: (t, i)), pl.BlockSpec((tm, tn), lambda i, j, t: (t, j))],
        out_specs=out_spec, out_shape=out_shape,
        compiler_params=_params(("parallel", "parallel", "arbitrary")))(a, b)


def _row_spec(r, tm):
    if isinstance(r, tuple):
        arr, width, blk = r
        return arr, pl.BlockSpec((tm, width), lambda i, blk=blk: (i, blk))
    return r, pl.BlockSpec((tm, r.shape[1]), lambda i: (i, 0))


def _full_spec(p):
    return pl.BlockSpec(p.shape, lambda i: (0,) * p.ndim)


def _expand_rows(rows, tm):
    arrays, specs, counts, widths = [], [], [], []
    for r in rows:
        parts = [_row_spec(p, tm) for p in (r if isinstance(r, list) else [r])]
        arrays += [a for a, _ in parts]
        specs += [s for _, s in parts]
        counts.append(len(parts))
        widths.append(sum(s.block_shape[1] for _, s in parts))
    return arrays, specs, counts, widths


def _row_values(refs, counts):
    vals, k = [], 0
    for c in counts:
        parts = [refs[k + j][...] for j in range(c)]
        vals.append(parts[0] if c == 1 else jnp.concatenate(parts, axis=1))
        k += c
    return vals


def _rows_of(rows):
    first = rows[0][0] if isinstance(rows[0], list) else rows[0]
    return (first[0] if isinstance(first, tuple) else first).shape[0]


def rowmap_fwd(fn, rows, params, outs, *, name, tm=256):
    m = _rows_of(rows)
    tm = min(tm, m)
    arrays, specs, counts, _ = _expand_rows(rows, tm)
    nin, npar = len(arrays), len(params)

    def body(*refs):
        res = fn(*_row_values(refs[:nin], counts), *[r[...] for r in refs[nin:nin + npar]])
        for o_ref, v in zip(refs[nin + npar:], res):
            o_ref[...] = v.astype(o_ref.dtype)

    return pl.pallas_call(
        body, name=name, grid=(m // tm,), in_specs=specs + [_full_spec(p) for p in params],
        out_specs=[pl.BlockSpec((tm, c), lambda i: (i, 0)) for c, _ in outs],
        out_shape=[jax.ShapeDtypeStruct((m, c), dt) for c, dt in outs],
        compiler_params=_params(("parallel",)))(*arrays, *params)


def rowmap_bwd(fn, rows, params, cts, *, name, row_dtypes=None, add=None, tm=256):
    m = _rows_of(rows)
    tm = min(tm, m)
    arrays, specs, counts, widths = _expand_rows(rows, tm)
    cp = [_row_spec(c, tm) for c in cts]
    nin, nr, npar, nc = len(arrays), len(rows), len(params), len(cts)
    row_dtypes = row_dtypes or [F32] * nr

    def body(*refs):
        ins = _row_values(refs[:nin], counts) + [r[...] for r in refs[nin:nin + npar]]
        ins = [v.astype(F32) for v in ins]
        ct = tuple(r[...].astype(F32) for r in refs[nin + npar:nin + npar + nc])
        base = nin + npar + nc
        extra = None
        if add is not None:
            extra = refs[base][...]
            base += 1
        _, pull = jax.vjp(fn, *ins)
        grads = pull(ct)
        for j in range(nr):
            g = grads[j]
            if j == 0 and extra is not None:
                g = g + extra
            refs[base + j][...] = g.astype(refs[base + j].dtype)

        @pl.when(pl.program_id(0) == 0)
        def _():
            for j in range(npar):
                refs[base + nr + j][...] = jnp.zeros_like(refs[base + nr + j])

        for j in range(npar):
            refs[base + nr + j][...] += grads[nr + j]

    in_specs = specs + [_full_spec(p) for p in params] + [s for _, s in cp]
    args = arrays + list(params) + [a for a, _ in cp]
    if add is not None:
        in_specs.append(pl.BlockSpec((tm, widths[0]), lambda i: (i, 0)))
        args.append(add)
    out_specs = [pl.BlockSpec((tm, w), lambda i: (i, 0)) for w in widths] + [_full_spec(p) for p in params]
    out_shape = [jax.ShapeDtypeStruct((m, w), dt) for w, dt in zip(widths, row_dtypes)]
    out_shape += [jax.ShapeDtypeStruct(p.shape, F32) for p in params]
    return pl.pallas_call(
        body, name=name, grid=(m // tm,), in_specs=in_specs, out_specs=out_specs, out_shape=out_shape,
        compiler_params=_params(("arbitrary",)))(*args)


def loss_head(h, target, w, *, name, tm=256, matmul=None):
    m, d = h.shape
    tm = min(tm, m)

    def body(h_ref, t_ref, w_ref, *refs):
        loss_ref, dh_ref, dw_ref = refs[-3:]
        rows = h_ref[...]
        if matmul is not None:
            rows = rows + _bd(refs[0][...], refs[1][...], 1, 0)
        y, pull = jax.vjp(_rms, rows, w_ref[...])
        err = y - t_ref[...]
        dh, dw = pull(err * (1.0 / d))

        @pl.when(pl.program_id(0) == 0)
        def _():
            loss_ref[...] = jnp.zeros_like(loss_ref)
            dw_ref[...] = jnp.zeros_like(dw_ref)

        loss_ref[...] += (0.5 / d) * jnp.sum(err * err, keepdims=True)
        dw_ref[...] += dw
        dh_ref[...] = dh

    row = pl.BlockSpec((tm, d), lambda i: (i, 0))
    in_specs, args = [row, row, _full_spec(w)], [h, target, w]
    if matmul is not None:
        in_specs += [pl.BlockSpec((tm, matmul[0].shape[1]), lambda i: (i, 0)), _full_spec(matmul[1])]
        args += list(matmul)
    return pl.pallas_call(
        body, name=name, grid=(m // tm,), in_specs=in_specs,
        out_specs=[pl.BlockSpec((1, 1), lambda i: (0, 0)), row, _full_spec(w)],
        out_shape=[jax.ShapeDtypeStruct((1, 1), F32), jax.ShapeDtypeStruct((m, d), F32),
                   jax.ShapeDtypeStruct(w.shape, F32)],
        compiler_params=_params(("arbitrary",)))(*args)


def _shift(x, s):
    if s == 0:
        return x
    n = x.shape[0]
    t = lax.broadcasted_iota(jnp.int32, x.shape, 0)
    rolled = pltpu.roll(x, (-s) % n, 0)
    return jnp.where((t + s >= 0) & (t + s < n), rolled, 0.0)


def _conv(x, w, b):
    k = w.shape[0]
    acc = b + w[k // 2:k // 2 + 1, :] * x
    for j in range(k):
        if j != k // 2:
            acc = acc + w[j:j + 1, :] * _shift(x, j - k // 2)
    return acc


def _conv_bwd(x, dc, w):
    k = w.shape[0]
    dx = None
    dws = []
    for j in range(k):
        s = j - k // 2
        term = w[j:j + 1, :] * _shift(dc, -s)
        dx = term if dx is None else dx + term
        dws.append(jnp.sum(dc * _shift(x, s), axis=0, keepdims=True))
    return dx, jnp.concatenate(dws, axis=0), jnp.sum(dc, axis=0, keepdims=True)


def _dsilu(c):
    s = jax.nn.sigmoid(c)
    return s * (1.0 + c * (1.0 - s))


def ssd_conv_fwd(xbc, w, b, *, bsz, name):
    t, c = xbc.shape
    seq, ct = t // bsz, 256

    def body(x_ref, w_ref, b_ref, o_ref):
        o_ref[...] = _silu(_conv(x_ref[...], w_ref[...], b_ref[...]))

    return pl.pallas_call(
        body, name=name, grid=(c // ct, bsz),
        in_specs=[pl.BlockSpec((seq, ct), lambda j, i: (i, j)), pl.BlockSpec((w.shape[0], ct), lambda j, i: (0, j)),
                  pl.BlockSpec((1, ct), lambda j, i: (0, j))],
        out_specs=pl.BlockSpec((seq, ct), lambda j, i: (i, j)),
        out_shape=jax.ShapeDtypeStruct((t, c), F32),
        compiler_params=_params(("parallel", "parallel")))(xbc, w, b)


def ssd_conv_bwd(xbc, dparts, w, b, *, bsz, name):
    t, c = xbc.shape
    seq, ct, k = t // bsz, 256, w.shape[0]
    starts = [0]
    for p in dparts:
        starts.append(starts[-1] + p.shape[1] // ct)

    def body(x_ref, *refs):
        g_refs, (w_ref, b_ref, dx_ref, dw_ref, db_ref) = refs[:len(dparts)], refs[len(dparts):]
        j = pl.program_id(0)
        g = g_refs[-1][...]
        for n in range(len(dparts) - 2, -1, -1):
            g = jnp.where(j < starts[n + 1], g_refs[n][...], g)
        x, wv = x_ref[...], w_ref[...]
        dc = g * _dsilu(_conv(x, wv, b_ref[...]))
        dx, dw, db = _conv_bwd(x, dc, wv)
        dx_ref[...] = dx

        @pl.when(pl.program_id(1) == 0)
        def _():
            dw_ref[...] = jnp.zeros_like(dw_ref)
            db_ref[...] = jnp.zeros_like(db_ref)

        dw_ref[...] += dw
        db_ref[...] += db

    def part_spec(n):
        lo, hi = starts[n], starts[n + 1]

        def index(j, i):
            inside = (j >= lo) & (j < hi)
            return jnp.where(inside, i, 0), jnp.where(inside, j - lo, 0)

        return pl.BlockSpec((seq, ct), index)

    blk = pl.BlockSpec((seq, ct), lambda j, i: (i, j))
    wspec, bspec = pl.BlockSpec((k, ct), lambda j, i: (0, j)), pl.BlockSpec((1, ct), lambda j, i: (0, j))
    return pl.pallas_call(
        body, name=name, grid=(c // ct, bsz),
        in_specs=[blk] + [part_spec(n) for n in range(len(dparts))] + [wspec, bspec], out_specs=[blk, wspec, bspec],
        out_shape=[jax.ShapeDtypeStruct((t, c), F32), jax.ShapeDtypeStruct((k, c), F32),
                   jax.ShapeDtypeStruct((1, c), F32)],
        compiler_params=_params(("parallel", "arbitrary")))(xbc, *dparts, w, b)


def _ffn_specs(seq, ct, k, nblk):
    val = pl.BlockSpec((seq, ct), lambda j, i: (i, j))
    gate = pl.BlockSpec((seq, ct), lambda j, i: (i, nblk + j))
    wv, wg = pl.BlockSpec((k, ct), lambda j, i: (0, j)), pl.BlockSpec((k, ct), lambda j, i: (0, nblk + j))
    bv, bg = pl.BlockSpec((1, ct), lambda j, i: (0, j)), pl.BlockSpec((1, ct), lambda j, i: (0, nblk + j))
    return val, gate, wv, wg, bv, bg


def ffn_act_fwd(up, w, b, *, bsz, name):
    t = up.shape[0]
    half = up.shape[1] // 2
    seq, ct, k = t // bsz, 256, w.shape[0]
    val, gate, wv, wg, bv, bg = _ffn_specs(seq, ct, k, half // ct)

    def body(v_ref, g_ref, wv_ref, wg_ref, bv_ref, bg_ref, o_ref):
        vc = _conv(v_ref[...].astype(F32), wv_ref[...], bv_ref[...])
        gc = _conv(g_ref[...].astype(F32), wg_ref[...], bg_ref[...])
        o_ref[...] = (_silu(gc) * vc).astype(BF16)

    return pl.pallas_call(
        body, name=name, grid=(half // ct, bsz), in_specs=[val, gate, wv, wg, bv, bg], out_specs=val,
        out_shape=jax.ShapeDtypeStruct((t, half), BF16),
        compiler_params=_params(("parallel", "parallel")))(up, up, w, w, b, b)


def ffn_act_bwd(up, dact, w, b, *, bsz, name):
    t = up.shape[0]
    half = up.shape[1] // 2
    seq, ct, k = t // bsz, 256, w.shape[0]
    val, gate, wv, wg, bv, bg = _ffn_specs(seq, ct, k, half // ct)

    def body(v_ref, g_ref, wv_ref, wg_ref, bv_ref, bg_ref, d_ref, dv_ref, dg_ref, dwv_ref, dwg_ref, dbv_ref, dbg_ref):
        v, g = v_ref[...].astype(F32), g_ref[...].astype(F32)
        vc = _conv(v, wv_ref[...], bv_ref[...])
        gc = _conv(g, wg_ref[...], bg_ref[...])
        d = d_ref[...].astype(F32)
        sg = jax.nn.sigmoid(gc)
        dv, dwv, dbv = _conv_bwd(v, d * (gc * sg), wv_ref[...])
        dg, dwg, dbg = _conv_bwd(g, d * vc * (sg * (1.0 + gc * (1.0 - sg))), wg_ref[...])
        dv_ref[...] = dv.astype(BF16)
        dg_ref[...] = dg.astype(BF16)

        @pl.when(pl.program_id(1) == 0)
        def _():
            for r in (dwv_ref, dwg_ref, dbv_ref, dbg_ref):
                r[...] = jnp.zeros_like(r)

        dwv_ref[...] += dwv
        dwg_ref[...] += dwg
        dbv_ref[...] += dbv
        dbg_ref[...] += dbg

    return pl.pallas_call(
        body, name=name, grid=(half // ct, bsz), in_specs=[val, gate, wv, wg, bv, bg, val],
        out_specs=[val, val, wv, wv, bv, bv],
        out_shape=[jax.ShapeDtypeStruct((t, half), BF16), jax.ShapeDtypeStruct((t, half), BF16),
                   jax.ShapeDtypeStruct((k, half), F32), jax.ShapeDtypeStruct((k, half), F32),
                   jax.ShapeDtypeStruct((1, half), F32), jax.ShapeDtypeStruct((1, half), F32)],
        compiler_params=_params(("parallel", "arbitrary")))(up, up, w, w, b, b, dact)


def _sel_row(a, h):
    oh = (lax.broadcasted_iota(jnp.int32, (a.shape[0], 1), 0) == h).astype(F32)
    return jnp.sum(a * oh, axis=0, keepdims=True)


def _ssd_chunk(xp, dtr, bm, cm, prev, bias_r, alog_r, dskip_r, rev):
    q = dtr.shape[1]
    ri = lax.broadcasted_iota(jnp.int32, (q, q), 0)
    ci = lax.broadcasted_iota(jnp.int32, (q, q), 1)
    mask = (ci >= ri) if rev else (ci <= ri)
    lane_lo, row_lo = ci < HDIM, ri < HDIM
    dt_r = _softplus(dtr + bias_r)
    dta_r = dt_r * (-jnp.exp(alog_r))
    cs_r = cum_row(dta_r, rev)
    scores = dot_nt(cm, bm)

    def per_row(v):
        return jnp.broadcast_to(v, (q, q)).T

    assert len(xp) == 2
    y_diag, csqs, decayed, tots = [], [], [], []
    for p in range(2):
        ha = 2 * p + (HPG if rev else 0)
        hb = ha + 1
        cs_a, cs_b = _sel_row(cs_r, ha), _sel_row(cs_r, hb)
        csq_a, csq_b = per_row(cs_a), per_row(cs_b)
        seg_a = jnp.exp(jnp.where(mask, csq_a - cs_a, -1e30))
        seg_b = jnp.exp(jnp.where(mask, csq_b - cs_b, -1e30))
        csq = jnp.where(lane_lo, csq_a, csq_b)
        xdt = xp[p] * jnp.where(lane_lo, per_row(_sel_row(dt_r, ha)), per_row(_sel_row(dt_r, hb)))
        tot_a = jnp.sum(_sel_row(dta_r, ha), axis=1, keepdims=True)
        tot_b = jnp.sum(_sel_row(dta_r, hb), axis=1, keepdims=True)
        y_diag.append(jnp.where(lane_lo, *dot2_nn(scores * seg_a, scores * seg_b, xdt)))
        csqs.append(csq)
        decayed.append(xdt * jnp.exp(jnp.where(lane_lo, tot_a, tot_b) - csq))
        tots.append((tot_a, tot_b, ha, hb))
    y_off = dot_nt2(cm, *prev)
    states = dot_tn2(*decayed, bm)
    ys, news = [], []
    for p, (tot_a, tot_b, ha, hb) in enumerate(tots):
        y = y_diag[p] + y_off[p] * jnp.exp(csqs[p])
        if not rev:
            y = y + jnp.where(lane_lo, _sel_row(dskip_r, ha), _sel_row(dskip_r, hb)) * xp[p]
        ys.append(y)
        news.append(jnp.exp(jnp.where(row_lo, tot_a, tot_b)) * prev[p] + states[p])
    return tuple(ys), tuple(news)


NPAIR = HPG // 2


def _ssd_specs(seq, nc):
    xs = pl.BlockSpec((None, seq, HPG * HDIM), lambda b, g: (b, 0, g))
    bm = pl.BlockSpec((None, seq, NSTATE), lambda b, g: (b, 0, SSD_W // NSTATE + g))
    cm = pl.BlockSpec((None, seq, NSTATE), lambda b, g: (b, 0, SSD_W // NSTATE + SGROUPS + g))
    dtr = pl.BlockSpec((None, None, 2 * HPG, seq), lambda b, g: (b, g, 0, 0))
    pr = pl.BlockSpec((None, 2 * HPG, 1), lambda b, g: (g, 0, 0))
    st = pl.BlockSpec((None, None, 2, nc, NPAIR, 2 * HDIM, NSTATE), lambda b, g: (b, g, 0, 0, 0, 0, 0))
    return xs, bm, cm, dtr, pr, st


def _pair_cols(p):
    return slice(2 * HDIM * p, 2 * HDIM * (p + 1))


def ssd_scan_fwd(act, dtr, prs, *, name):
    bsz, seq, _ = act.shape
    nc = seq // QC
    xs, bm, cm, dtrs, pr, st = _ssd_specs(seq, nc)

    def body(x_ref, b_ref, c_ref, dtr_ref, br_ref, ar_ref, dk_ref, y_ref, st_ref):
        par = (br_ref[...], ar_ref[...], dk_ref[...])
        y_ref[...] = jnp.zeros_like(y_ref)

        def step(i, carry):
            new = []
            for rev in (False, True):
                k = (nc - 1 - i) if rev else i
                rows = pl.ds(pl.multiple_of(k * QC, QC), QC)
                xp = tuple(x_ref[rows, _pair_cols(p)] for p in range(NPAIR))
                for p in range(NPAIR):
                    st_ref[int(rev), k, p] = carry[rev][p]
                ys, nw = _ssd_chunk(xp, dtr_ref[:, rows], b_ref[rows, :], c_ref[rows, :], carry[rev], *par, rev)
                for p in range(NPAIR):
                    y_ref[rows, _pair_cols(p)] += ys[p]
                new.append(nw)
            return tuple(new)

        zero = tuple(jnp.zeros((2 * HDIM, NSTATE), F32) for _ in range(NPAIR))
        lax.fori_loop(0, nc // 2, lambda i, c: step(2 * i + 1, step(2 * i, c)), (zero, zero))

    return pl.pallas_call(
        body, name=name, grid=(bsz, SGROUPS), in_specs=[xs, bm, cm, dtrs, pr, pr, pr], out_specs=[xs, st],
        out_shape=[jax.ShapeDtypeStruct((bsz, seq, SSD_W), F32),
                   jax.ShapeDtypeStruct((bsz, SGROUPS, 2, nc, NPAIR, 2 * HDIM, NSTATE), F32)],
        compiler_params=_params(("parallel", "parallel")))(act, act, act, dtr, *prs)


def ssd_scan_bwd(act, dtr, prs, states, dy, *, name):
    bsz, seq, _ = act.shape
    nc = seq // QC
    xs, bm, cm, dtrs, pr, st = _ssd_specs(seq, nc)
    grp = pl.BlockSpec((None, seq, NSTATE), lambda b, g: (b, 0, g))
    dpr = pl.BlockSpec((None, None, 2 * HPG, 1), lambda b, g: (b, g, 0, 0))

    def body(x_ref, b_ref, c_ref, dtr_ref, br_ref, ar_ref, dk_ref, st_ref, dy_ref,
             dx_ref, db_ref, dc_ref, ddtr_ref, gbr_ref, gar_ref, gdk_ref):
        par = (br_ref[...], ar_ref[...], dk_ref[...])
        pgrads = (gbr_ref, gar_ref, gdk_ref)
        for r in pgrads + (dx_ref, db_ref, dc_ref, ddtr_ref):
            r[...] = jnp.zeros_like(r)

        def bstep(i, dcarry):
            new = []
            for rev in (False, True):
                k = i if rev else (nc - 1 - i)
                rows = pl.ds(pl.multiple_of(k * QC, QC), QC)
                xp = tuple(x_ref[rows, _pair_cols(p)] for p in range(NPAIR))
                prev = tuple(st_ref[int(rev), k, p] for p in range(NPAIR))
                _, pull = jax.vjp(functools.partial(_ssd_chunk, rev=rev), xp, dtr_ref[:, rows], b_ref[rows, :],
                                  c_ref[rows, :], prev, *par)
                dyp = tuple(dy_ref[rows, _pair_cols(p)] for p in range(NPAIR))
                gx, gdt, gb, gc, gprev, *gpar = pull((dyp, dcarry[rev]))
                for p in range(NPAIR):
                    dx_ref[rows, _pair_cols(p)] += gx[p]
                ddtr_ref[:, rows] += gdt
                db_ref[rows, :] += gb
                dc_ref[rows, :] += gc
                for r, g in zip(pgrads, gpar):
                    r[...] += g
                new.append(gprev)
            return tuple(new)

        zero = tuple(jnp.zeros((2 * HDIM, NSTATE), F32) for _ in range(NPAIR))
        lax.fori_loop(0, nc, bstep, (zero, zero))

    out_shape = [jax.ShapeDtypeStruct((bsz, seq, SSD_W), F32),
                 jax.ShapeDtypeStruct((bsz, seq, SGROUPS * NSTATE), F32),
                 jax.ShapeDtypeStruct((bsz, seq, SGROUPS * NSTATE), F32),
                 jax.ShapeDtypeStruct(dtr.shape, F32)]
    out_shape += [jax.ShapeDtypeStruct((bsz, SGROUPS, 2 * HPG, 1), F32)] * 3
    return pl.pallas_call(
        body, name=name, grid=(bsz, SGROUPS), in_specs=[xs, bm, cm, dtrs, pr, pr, pr, st, xs],
        out_specs=[xs, grp, grp, dtrs, dpr, dpr, dpr], out_shape=out_shape,
        compiler_params=_params(("parallel", "parallel")))(act, act, act, dtr, *prs, states, dy)


def _s5_core(lam_re, lam_im, log_step, b_re, b_im, c_re, c_im):
    q = S5_Q
    step = jnp.exp(log_step)[:, None]
    lr, li = lam_re * step, lam_im * step
    mag = jnp.exp(lr)
    ar, ai = mag * jnp.cos(li), mag * jnp.sin(li)
    den = lam_re * lam_re + lam_im * lam_im
    cr = ((ar - 1.0) * lam_re + ai * lam_im) / den
    ci = (ai * lam_re - (ar - 1.0) * lam_im) / den
    bbr = cr[..., None] * b_re - ci[..., None] * b_im
    bbi = cr[..., None] * b_im + ci[..., None] * b_re
    d = jnp.arange(q + 1, dtype=F32)[None, :, None]
    pm = jnp.exp(d * lr[:, None, :])
    pr, pi = pm * jnp.cos(d * li[:, None, :]), pm * jnp.sin(d * li[:, None, :])
    er = pr[..., None] * bbr[:, None] - pi[..., None] * bbi[:, None]
    ei = pr[..., None] * bbi[:, None] + pi[..., None] * bbr[:, None]
    hp = lax.Precision.HIGHEST
    k = (jnp.einsum('gcp,gdpz->gdcz', c_re, er[:, :q], precision=hp)
         - jnp.einsum('gcp,gdpz->gdcz', c_im, ei[:, :q], precision=hp))
    e = jnp.concatenate([er[:, :q], ei[:, :q]], axis=2)
    p1r, p1i = pr[:, 1:], pi[:, 1:]
    m_re = c_re[:, None] * p1r[:, :, None, :] - c_im[:, None] * p1i[:, :, None, :]
    m_im = -c_re[:, None] * p1i[:, :, None, :] - c_im[:, None] * p1r[:, :, None, :]
    da = jnp.concatenate([pr[:, q], pr[:, q]], axis=-1)
    db = jnp.concatenate([-pi[:, q], pi[:, q]], axis=-1)
    return k, e, jnp.concatenate([m_re, m_im], axis=-1), da, db


def _s5_operators(lf_re, lf_im, lsf, lb_re, lb_im, lsb, b_re, b_im, cf_re, cf_im, cb_re, cb_im):
    g = lf_re.shape[0]
    both = lambda f, b: jnp.concatenate([f, b], axis=0)
    k, e, m, da, db = _s5_core(both(lf_re, lb_re), both(lf_im, lb_im), both(lsf, lsb), both(b_re, b_re),
                               both(b_im, b_im), both(cf_re, cb_re), both(cf_im, cb_im))
    kf, kb = k[:g], k[g:]
    wtf, wtb = jnp.transpose(e[:g, ::-1], (0, 1, 3, 2)), jnp.transpose(e[g:], (0, 1, 3, 2))
    mtf, mtb = jnp.transpose(m[:g], (0, 3, 1, 2)), jnp.transpose(m[g:, ::-1], (0, 3, 1, 2))
    daf, dab, dbf, dbb = da[:g], da[g:], db[:g], db[g:]
    lags = jnp.concatenate([kb[:, :0:-1], kf[:, :1] + kb[:, :1], kf[:, 1:]], axis=1)
    tt = jnp.transpose(lags, (0, 1, 3, 2))
    wt = jnp.concatenate([wtf.reshape(g, S5_QC, 2 * S5_P), wtb.reshape(g, S5_QC, 2 * S5_P)], axis=-1)
    mt = jnp.concatenate([mtf.reshape(g, 2 * S5_P, S5_QC), mtb.reshape(g, 2 * S5_P, S5_QC)], axis=1)
    return tt, wt, mt, jnp.concatenate([daf, dab], -1), jnp.concatenate([dbf, dbb], -1)


def _gspec(*shape):
    return pl.BlockSpec((None,) + shape, lambda g: (g,) + (0,) * len(shape))


S5_HALVES = S5_QC // LANES


def _toeplitz_block(s, t):
    per = LANES // S5_C
    return t // per, slice(s * S5_C, (s + 1) * S5_C), slice((t % per) * S5_C, (t % per + 1) * S5_C)


def s5_toeplitz(kt, *, name):
    g = kt.shape[0]

    def body(k_ref, t_ref):
        for s in range(S5_Q):
            for t in range(S5_Q):
                t_ref[_toeplitz_block(s, t)] = k_ref[t - s + S5_Q - 1]

    return pl.pallas_call(
        body, name=name, grid=(g,), in_specs=[_gspec(2 * S5_Q - 1, S5_C, S5_C)],
        out_specs=_gspec(S5_HALVES, S5_QC, LANES), out_shape=jax.ShapeDtypeStruct((g, S5_HALVES, S5_QC, LANES), F32),
        compiler_params=_params(("parallel",)))(kt)


def s5_toeplitz_bwd(dtt, *, name):
    g = dtt.shape[0]

    def body(d_ref, k_ref):
        for j in range(2 * S5_Q - 1):
            acc = None
            for s in range(S5_Q):
                t = j - (S5_Q - 1) + s
                if 0 <= t < S5_Q:
                    blk = d_ref[_toeplitz_block(s, t)]
                    acc = blk if acc is None else acc + blk
            k_ref[j] = acc

    return pl.pallas_call(
        body, name=name, grid=(g,), in_specs=[_gspec(S5_HALVES, S5_QC, LANES)],
        out_specs=_gspec(2 * S5_Q - 1, S5_C, S5_C), out_shape=jax.ShapeDtypeStruct((g, 2 * S5_Q - 1, S5_C, S5_C), F32),
        compiler_params=_params(("parallel",)))(dtt)


S5_RT = 64


def _chunk_piece(q):
    per = LANES // S5_C
    return q // per, slice((q % per) * S5_C, (q % per + 1) * S5_C)


def to_chunks(u, *, name):
    t = u.shape[0]
    r = t // S5_Q
    rt = min(S5_RT, r)

    per = LANES // S5_C
    nblk = S5_W // LANES

    def body(*refs):
        o_ref = refs[-1]
        for k in range(nblk):
            for q in range(S5_Q):
                rows = refs[k][pl.ds(q, rt, stride=S5_Q), :]
                half, lanes = _chunk_piece(q)
                for j in range(per):
                    o_ref[k * per + j, half, :, lanes] = rows[:, j * S5_C:(j + 1) * S5_C]

    return pl.pallas_call(
        body, name=name, grid=(r // rt,),
        in_specs=[pl.BlockSpec((rt * S5_Q, LANES), lambda i, k=k: (i, k)) for k in range(nblk)],
        out_specs=pl.BlockSpec((S5_G, S5_HALVES, rt, LANES), lambda i: (0, 0, i, 0)),
        out_shape=jax.ShapeDtypeStruct((S5_G, S5_HALVES, r, LANES), F32),
        compiler_params=_params(("parallel",)))(*[u] * nblk)


def from_chunks(y, *, name, add=None, as_blocks=False):
    r = y.shape[2]
    rt = min(S5_RT, r)
    per = LANES // S5_C

    nblk = S5_W // LANES

    def body(*refs):
        y_ref, tmp_ref = refs[0], refs[-1]
        adds, outs = refs[1:-1 - nblk], refs[-1 - nblk:-1]
        for k in range(nblk):
            for q in range(S5_Q):
                half, lanes = _chunk_piece(q)
                for j in range(per):
                    tmp_ref[:, j * S5_C:(j + 1) * S5_C] = y_ref[k * per + j, half, :, lanes]
                row = tmp_ref[...]
                if add is not None:
                    row = row + adds[k][pl.ds(q, rt, stride=S5_Q), :]
                outs[k][pl.ds(q, rt, stride=S5_Q), :] = row

    in_specs = [pl.BlockSpec((S5_G, S5_HALVES, rt, LANES), lambda i: (0, 0, i, 0))]
    if add is not None:
        in_specs += [pl.BlockSpec((rt * S5_Q, LANES), lambda i, k=k: (i, k)) for k in range(nblk)]
    blocks = pl.pallas_call(
        body, name=name, grid=(r // rt,), in_specs=in_specs,
        out_specs=[pl.BlockSpec((rt * S5_Q, LANES), lambda i: (i, 0))] * nblk,
        out_shape=[jax.ShapeDtypeStruct((r * S5_Q, LANES), F32)] * nblk,
        scratch_shapes=[pltpu.VMEM((rt, LANES), F32)],
        compiler_params=_params(("parallel",)))(*([y] if add is None else [y] + [add] * nblk))
    return list(blocks) if as_blocks else jnp.concatenate(blocks, axis=1)


def _cat(ref):
    return jnp.concatenate([ref[h] for h in range(S5_HALVES)], axis=1)


def _put(ref, v):
    for h in range(S5_HALVES):
        ref[h] = v[:, h * LANES:(h + 1) * LANES]


def _cspec(r):
    return _gspec(S5_HALVES, r, LANES)


def s5_state_in(u, wt, *, name):
    g, _, r, _ = u.shape

    def body(u_ref, w_ref, o_ref):
        o_ref[...] = _bd(_cat(u_ref), w_ref[...], 1, 0)

    return pl.pallas_call(
        body, name=name, grid=(g,), in_specs=[_cspec(r), _gspec(S5_QC, 4 * S5_P)],
        out_specs=_gspec(r, 4 * S5_P), out_shape=jax.ShapeDtypeStruct((g, r, 4 * S5_P), F32),
        compiler_params=_params(("parallel",)))(u, wt)


def _swap(h):
    return pltpu.roll(h, S5_P, 1)


def s5_carry_fwd(s, da, db, *, name):
    nck, rows, _ = s.shape
    w = 2 * S5_P

    def body(s_ref, da_ref, db_ref, h_ref):
        dirs = ((False, slice(0, w)), (True, slice(w, 2 * w)))
        coef = [(da_ref[:, cols], db_ref[:, cols]) for _, cols in dirs]

        def step(i, hs):
            new = []
            for (rev, cols), (a, b), h in zip(dirs, coef, hs):
                k = (nck - 1 - i) if rev else i
                h_ref[k, :, cols] = h
                new.append(a * h + b * _swap(h) + s_ref[k, :, cols])
            return tuple(new)

        z = jnp.zeros((rows, w), F32)
        lax.fori_loop(0, nck, step, (z, z), unroll=2)

    rt = min(CARRY_ROWS, rows)
    big, small = pl.BlockSpec((nck, rt, 2 * w), lambda i: (0, i, 0)), pl.BlockSpec((rt, 2 * w), lambda i: (i, 0))
    rows = rt
    return pl.pallas_call(
        body, name=name, grid=(s.shape[1] // rt,), in_specs=[big, small, small], out_specs=big,
        out_shape=jax.ShapeDtypeStruct(s.shape, F32), compiler_params=_params(("parallel",)))(s, da, db)


def s5_carry_bwd(hin, dh, da, db, *, name):
    nck, rows, _ = hin.shape
    w = 2 * S5_P

    def body(h_ref, dh_ref, da_ref, db_ref, ds_ref, gda_ref, gdb_ref):
        dirs = ((False, slice(0, w)), (True, slice(w, 2 * w)))
        coef = [(da_ref[:, cols], db_ref[:, cols]) for _, cols in dirs]

        def step(i, carries):
            new = []
            for (rev, cols), (a, b), (g, ga, gb) in zip(dirs, coef, carries):
                k = i if rev else (nck - 1 - i)
                ds_ref[k, :, cols] = g
                h = h_ref[k, :, cols]
                new.append((dh_ref[k, :, cols] + a * g + _swap(b * g), ga + g * h, gb + g * _swap(h)))
            return tuple(new)

        z = jnp.zeros((rows, w), F32)
        res = lax.fori_loop(0, nck, step, ((z, z, z), (z, z, z)), unroll=2)
        for (_, cols), (_, ga, gb) in zip(dirs, res):
            gda_ref[:, cols] = ga
            gdb_ref[:, cols] = gb

    rt = min(CARRY_ROWS, rows)
    big, small = pl.BlockSpec((nck, rt, 2 * w), lambda i: (0, i, 0)), pl.BlockSpec((rt, 2 * w), lambda i: (i, 0))
    rows = rt
    return pl.pallas_call(
        body, name=name, grid=(hin.shape[1] // rt,), in_specs=[big, big, small, small], out_specs=[big, small, small],
        out_shape=[jax.ShapeDtypeStruct(hin.shape, F32), jax.ShapeDtypeStruct(da.shape, F32),
                   jax.ShapeDtypeStruct(da.shape, F32)],
        compiler_params=_params(("parallel",)))(hin, dh, da, db)


def s5_out(u, hin, tt, mt, *, name):
    g, _, r, _ = u.shape

    def body(u_ref, h_ref, t_ref, m_ref, o_ref):
        u_v, h_v = _cat(u_ref), h_ref[...]
        for half in range(S5_HALVES):
            cols = slice(half * LANES, (half + 1) * LANES)
            o_ref[half] = _bd(u_v, t_ref[half], 1, 0) + _bd(h_v, m_ref[:, cols], 1, 0)

    return pl.pallas_call(
        body, name=name, grid=(g,),
        in_specs=[_cspec(r), _gspec(r, 4 * S5_P), _gspec(S5_HALVES, S5_QC, LANES), _gspec(4 * S5_P, S5_QC)],
        out_specs=_cspec(r), out_shape=jax.ShapeDtypeStruct((g, S5_HALVES, r, LANES), F32),
        compiler_params=_params(("parallel",)))(u, hin, tt, mt)


def s5_out_bwd(dy, u, hin, tt, mt, *, name):
    g, _, r, _ = u.shape

    def body(dy_ref, u_ref, h_ref, t_ref, m_ref, dh_ref, dt_ref, dm_ref, du_ref):
        dy_v, u_v = _cat(dy_ref), _cat(u_ref)
        dh_ref[...] = _bd(dy_v, m_ref[...], 1, 1)
        dm_ref[...] = _bd(h_ref[...], dy_v, 0, 0)
        du = None
        for half in range(S5_HALVES):
            dy_h = dy_ref[half]
            dt_ref[half] = _bd(u_v, dy_h, 0, 0)
            part = _bd(dy_h, t_ref[half], 1, 1)
            du = part if du is None else du + part
        _put(du_ref, du)

    tspec = _gspec(S5_HALVES, S5_QC, LANES)
    return pl.pallas_call(
        body, name=name, grid=(g,),
        in_specs=[_cspec(r), _cspec(r), _gspec(r, 4 * S5_P), tspec, _gspec(4 * S5_P, S5_QC)],
        out_specs=[_gspec(r, 4 * S5_P), tspec, _gspec(4 * S5_P, S5_QC), _cspec(r)],
        out_shape=[jax.ShapeDtypeStruct((g, r, 4 * S5_P), F32), jax.ShapeDtypeStruct((g, S5_HALVES, S5_QC, LANES), F32),
                   jax.ShapeDtypeStruct((g, 4 * S5_P, S5_QC), F32), jax.ShapeDtypeStruct((g, S5_HALVES, r, LANES), F32)],
        compiler_params=_params(("parallel",)))(dy, u, hin, tt, mt)


def s5_state_in_bwd(ds, u, wt, du1, *, name):
    g, _, r, _ = u.shape

    def body(ds_ref, u_ref, w_ref, du1_ref, du_ref, dw_ref):
        ds_v = ds_ref[...]
        _put(du_ref, _cat(du1_ref) + _bd(ds_v, w_ref[...], 1, 1))
        dw_ref[...] = _bd(_cat(u_ref), ds_v, 0, 0)

    return pl.pallas_call(
        body, name=name, grid=(g,),
        in_specs=[_gspec(r, 4 * S5_P), _cspec(r), _gspec(S5_QC, 4 * S5_P), _cspec(r)],
        out_specs=[_cspec(r), _gspec(S5_QC, 4 * S5_P)],
        out_shape=[jax.ShapeDtypeStruct((g, S5_HALVES, r, LANES), F32), jax.ShapeDtypeStruct((g, S5_QC, 4 * S5_P), F32)],
        compiler_params=_params(("parallel",)))(ds, u, wt, du1)


def _s5_post(ypre, u, dvec, wv, wg, bv, bg, nw):
    g = _gelu(ypre + dvec * u)
    out = (dot_nn(g, wv) + bv) * jax.nn.sigmoid(dot_nn(g, wg) + bg)
    return (_rms(out, nw),)


def _ssd_post(y, z, nw):
    return (_rms(y * _silu(z), nw),)


def _to_carry(s, bsz):
    nck = s.shape[1] // bsz
    return jnp.transpose(s.reshape(S5_G, bsz, nck, -1), (2, 0, 1, 3)).reshape(nck, S5_G * bsz, -1)


def _from_carry(h, bsz):
    nck = h.shape[0]
    return jnp.transpose(h.reshape(nck, S5_G, bsz, -1), (1, 2, 0, 3)).reshape(S5_G, bsz * nck, -1)


def _block_diag(w):
    eye = jnp.eye(S5_G, dtype=w.dtype)
    return jnp.einsum('gcd,gh->gchd', w, eye).reshape(S5_W, S5_W)


def _diag_blocks(w):
    v = w.reshape(S5_G, S5_C, S5_G, S5_C)
    return v[jnp.arange(S5_G), :, jnp.arange(S5_G), :]


def _dt_rows(dt, bsz):
    seq = dt.shape[0] // bsz
    return jnp.transpose(dt.reshape(bsz, seq, 2, SGROUPS, HPG), (0, 3, 2, 4, 1)).reshape(bsz, SGROUPS, 2 * HPG, seq)


def _dt_from_rows(dr):
    bsz, _, _, seq = dr.shape
    return jnp.transpose(dr.reshape(bsz, SGROUPS, 2, HPG, seq), (0, 4, 2, 1, 3)).reshape(bsz * seq, 2 * HEADS)


def _head_params(f, b):
    return jnp.concatenate([f.reshape(SGROUPS, HPG), b.reshape(SGROUPS, HPG)], axis=1)[:, :, None]


def _head_grads(gr):
    v = gr.sum(0)[:, :, 0]
    return v[:, :HPG].reshape(HEADS), v[:, HPG:].reshape(HEADS)


def local_step(x, target, w):
    bsz, seq, d = x.shape
    t = bsz * seq
    x2, tgt2 = x.reshape(t, d), target.reshape(t, d)
    g = {}
    row = lambda v: v.reshape(1, -1)
    bf = lambda v: v.astype(BF16)

    w_in = _unshard(bf(w['w_in']), SHARDED['w_in'])
    cuts = [0, SSD_W, SSD_W + XBC_W, SSD_W + XBC_W + 2 * HEADS, w_in.shape[1]]
    w_in_parts = [w_in[:, a:b] for a, b in zip(cuts[:-1], cuts[1:])]
    norm_mix = row(w['norm_mix_w']) + w.get('token', 0.0)
    (hn,) = rowmap_fwd(lambda a, nw: (_rms(a, nw),), [x2], [norm_mix], [(d, BF16)], name="rms_mix")
    z, xbc, dt, u = [matmul_sum([hn], [p], tm=1024, name=f"in_proj_{i}") for i, p in enumerate(w_in_parts)]

    conv_w, conv_b = _unshard(w['ssd_conv_w'], SHARDED['ssd_conv_w']), row(w['ssd_conv_b'])
    act = ssd_conv_fwd(xbc, conv_w, conv_b, bsz=bsz, name="ssd_conv")
    dtr = _dt_rows(dt, bsz)
    prs = (_head_params(w['ssd_dt_bias_fwd'], w['ssd_dt_bias_bwd']),
           _head_params(w['ssd_a_log_fwd'], w['ssd_a_log_bwd']),
           _head_params(w['ssd_d'], jnp.zeros_like(w['ssd_d'])))
    act3 = act.reshape(bsz, seq, XBC_W)
    y_scan, ssd_states = ssd_scan_fwd(act3, dtr, prs, name="ssd_scan")
    y_scan = y_scan.reshape(t, SSD_W)
    ssd_nw = row(w['ssd_norm_w'])
    (y_ssd,) = rowmap_fwd(_ssd_post, [y_scan, z], [ssd_nw], [(SSD_W, BF16)], name="ssd_post")

    s5_names = ['s5_lambda_re_fwd', 's5_lambda_im_fwd', 's5_log_step_fwd', 's5_lambda_re_bwd', 's5_lambda_im_bwd',
                's5_log_step_bwd', 's5_b_re', 's5_b_im', 's5_c_re_fwd', 's5_c_im_fwd', 's5_c_re_bwd', 's5_c_im_bwd']
    (kt, wt, mt, da, db), s5_pull = jax.vjp(_s5_operators, *[w[n] for n in s5_names])
    tt_b, wt_b, mt_b = s5_toeplitz(kt, name="s5_toeplitz"), bf(wt), bf(mt)
    da_r, db_r = jnp.repeat(da, bsz, axis=0), jnp.repeat(db, bsz, axis=0)
    uc = to_chunks(u, name="s5_to_chunks_u")
    s_in = _to_carry(s5_state_in(uc, wt_b, name="s5_state_in"), bsz)
    hin_c = s5_carry_fwd(s_in, da_r, db_r, name="s5_carry")
    hin = _from_carry(hin_c, bsz)
    ypre = from_chunks(s5_out(uc, hin, tt_b, mt_b, name="s5_out"), name="s5_from_chunks_y", as_blocks=True)
    glu_w = w['s5_glu_w']
    s5_par = [row(w['s5_d']), _block_diag(glu_w[:, :, :S5_C]), _block_diag(glu_w[:, :, S5_C:]),
              row(w['s5_glu_b'][:, :S5_C]), row(w['s5_glu_b'][:, S5_C:]), row(w['s5_norm_w'])]
    (y_s5,) = rowmap_fwd(_s5_post, [ypre, u], s5_par, [(S5_W, BF16)], name="s5_post")

    if 'late' in w:
        w = {**w, **w['late'](y_s5)}
    w_out = bf(w['w_out']).reshape(SSD_W + S5_W, d)
    norm_ffn = row(w['norm_ffn_w'])
    h1, hn2 = matmul_sum([y_ssd, y_s5], [w_out[:SSD_W], w_out[SSD_W:]], add=x2, norm_w=norm_ffn, name="out_proj")
    pad_c = FFN_PAD - FFN_BLK
    half = N_DEV // 2
    w_up3 = jnp.pad(bf(w['ffn_w_up']), ((0, 0), (0, 0), (0, pad_c)))
    w_down = jnp.pad(bf(w['ffn_w_down']).reshape(half, FFN_BLK, d), ((0, 0), (0, pad_c), (0, 0)))
    w_down = w_down.reshape(half * FFN_PAD, d)
    fconv_w = jnp.pad(w['ffn_conv_w'], ((0, 0), (0, 0), (0, pad_c)))
    fconv_w = jnp.transpose(fconv_w, (1, 0, 2)).reshape(FCONV, N_DEV * FFN_PAD)
    fconv_b = row(jnp.pad(w['ffn_conv_b'].reshape(N_DEV, FFN_BLK), ((0, 0), (0, pad_c))))
    up = matmul_cols(hn2, w_up3, out_dtype=BF16, name="ffn_up")
    fact = ffn_act_fwd(up, fconv_w, fconv_b, bsz=bsz, name="ffn_act")
    loss, dh2, g_nf = loss_head(h1, tgt2, row(w['norm_final_w']), matmul=(fact, w_down), tm=512, name="ffn_down_loss")
    g['norm_final_w'] = g_nf.reshape(-1)

    dfact = matmul_sum([dh2], [w_down], nt=True, tm=1024, name="ffn_down_dx")
    g_down = matmul_tn(fact, dh2, name="ffn_down_dw").reshape(half, FFN_PAD, d)[:, :FFN_BLK]
    g['ffn_w_down'] = g_down.reshape(N_DEV, FFN_BLK // 2, d)
    dval, dgate, dwv, dwg, dbv, dbg = ffn_act_bwd(up, dfact, fconv_w, fconv_b, bsz=bsz, name="ffn_act_bwd")
    g_cw = jnp.concatenate([dwv, dwg], axis=1).reshape(FCONV, N_DEV, FFN_PAD)[:, :, :FFN_BLK]
    g['ffn_conv_w'] = jnp.transpose(g_cw, (1, 0, 2))
    g['ffn_conv_b'] = jnp.concatenate([dbv, dbg], axis=1).reshape(N_DEV, FFN_PAD)[:, :FFN_BLK].reshape(-1)
    windows = [(dval, FFN_PAD, p) for p in range(half)] + [(dgate, FFN_PAD, p) for p in range(half)]
    dhn2 = matmul_sum(windows, [(w_up3, p) for p in range(N_DEV)], nt=True, name="ffn_up_dx")
    g['ffn_w_up'] = jnp.concatenate([matmul_tn(hn2, dval, out_blocks=half, name="ffn_up_dw_val"),
                                     matmul_tn(hn2, dgate, out_blocks=half, name="ffn_up_dw_gate")],
                                    axis=0)[:, :, :FFN_BLK]
    send_early = w.get('on_grads')
    if send_early:
        norm_ffn = norm_ffn + send_early(g, ['ffn_w_up', 'ffn_w_down'])
    dh1, g_nffn = rowmap_bwd(lambda a, nw: (_rms(a, nw),), [h1], [norm_ffn], [dhn2], add=dh2, name="rms_ffn_bwd")
    g['norm_ffn_w'] = g_nffn.reshape(-1)

    dycat = matmul_sum([dh1], [w_out], nt=True, tm=1024, name="out_proj_dx")
    g['w_out'] = jnp.concatenate([matmul_tn(y_ssd, dh1, name="out_proj_dw_ssd"),
                                  matmul_tn(y_s5, dh1, name="out_proj_dw_s5")], axis=0).reshape(w['w_out'].shape)
    if send_early:
        ssd_nw = ssd_nw + send_early(g, ['w_out'])
    dy_scan, dz, g_snw = rowmap_bwd(_ssd_post, [y_scan, z], [ssd_nw], [(dycat, SSD_W, 0)], name="ssd_post_bwd")
    g['ssd_norm_w'] = g_snw.reshape(-1)
    dypre, du_a, g_d, g_wv, g_wg, g_bv, g_bg, g_s5nw = rowmap_bwd(
        _s5_post, [ypre, u], s5_par, [(dycat, S5_W, SSD_W // S5_W)], name="s5_post_bwd")
    g['s5_d'], g['s5_norm_w'] = g_d.reshape(-1), g_s5nw.reshape(-1)
    g['s5_glu_w'] = jnp.concatenate([_diag_blocks(g_wv), _diag_blocks(g_wg)], axis=-1)
    g['s5_glu_b'] = jnp.concatenate([g_bv.reshape(S5_G, S5_C), g_bg.reshape(S5_G, S5_C)], axis=-1)

    dyc = to_chunks(dypre, name="s5_to_chunks_dy")
    dhin, dtt, dmt, du1 = s5_out_bwd(dyc, uc, hin, tt_b, mt_b, name="s5_out_bwd")
    ds_c, gda, gdb = s5_carry_bwd(hin_c, _to_carry(dhin, bsz), da_r, db_r, name="s5_carry_bwd")
    duc, dwt = s5_state_in_bwd(_from_carry(ds_c, bsz), uc, wt_b, du1, name="s5_state_in_bwd")
    du = from_chunks(duc, add=du_a, name="s5_from_chunks_du")
    fold = lambda v: v.reshape(S5_G, bsz, -1).sum(1)
    dkt = s5_toeplitz_bwd(dtt, name="s5_toeplitz_bwd")
    for n, gv in zip(s5_names, s5_pull((dkt, dwt, dmt, fold(gda), fold(gdb)))):
        g[n] = gv

    dxs, dbm, dcm, ddtr, gbr, gar, gdk = ssd_scan_bwd(
        act3, dtr, prs, ssd_states, dy_scan.reshape(bsz, seq, SSD_W), name="ssd_scan_bwd")
    g['ssd_dt_bias_fwd'], g['ssd_dt_bias_bwd'] = _head_grads(gbr)
    g['ssd_a_log_fwd'], g['ssd_a_log_bwd'] = _head_grads(gar)
    g['ssd_d'] = _head_grads(gdk)[0]
    dparts_act = [v.reshape(t, v.shape[-1]) for v in (dxs, dbm, dcm)]
    dxbc, g_cw, g_cb = ssd_conv_bwd(xbc, dparts_act, conv_w, conv_b, bsz=bsz, name="ssd_conv_bwd")
    g['ssd_conv_w'] = _shard_rows(g_cw, SHARDED['ssd_conv_w']).reshape(w['ssd_conv_w'].shape)
    g['ssd_conv_b'] = g_cb.reshape(-1)
    ddt = _dt_from_rows(ddtr)

    dparts = [dz, dxbc, ddt, du]
    g_in = jnp.concatenate([matmul_tn(hn, dp, name=f"in_proj_dw_{i}") for i, dp in enumerate(dparts)], axis=1)
    g['w_in'] = _shard_rows(g_in, SHARDED['w_in']).reshape(w['w_in'].shape)
    if send_early:
        dparts[2] = ddt + send_early(g, ['w_in'], loss=loss)
    dhn = matmul_sum(dparts, w_in_parts, nt=True, name="in_proj_dx")
    dx, g_nmix = rowmap_bwd(lambda a, nw: (_rms(a, nw),), [x2], [norm_mix], [dhn], add=dh1, name="rms_mix_bwd")
    g['norm_mix_w'] = g_nmix.reshape(-1)
    return loss, dx.reshape(bsz, seq, d), g


ANY = pl.BlockSpec(memory_space=pl.ANY)


def all_gather(shards, *, name):
    n = len(shards)

    def body(*refs):
        x_refs, out_refs = refs[:n], refs[n:2 * n]
        send_sems, recv_sems, local_sems = refs[2 * n:]
        x, y, c = lax.axis_index("x"), lax.axis_index("y"), lax.axis_index("c")
        me, sibling = (x, y, c), (x, y, 1 - c)
        chips = [(1 - x, y), (x, 1 - y), (1 - x, 1 - y)]

        def copy(k, j, block, to, own=False):
            dst = out_refs[j].at[4 * block[0] + 2 * block[1] + block[2]]
            return pltpu.make_async_remote_copy(
                src_ref=x_refs[j] if own else dst, dst_ref=dst,
                send_sem=send_sems.at[k, j], recv_sem=recv_sems.at[k, j], device_id=to, device_id_type=MESH)

        mine = [pltpu.make_async_copy(x_refs[j], out_refs[j].at[4 * x + 2 * y + c], local_sems.at[j]) for j in range(n)]
        first = [copy(0, j, me, sibling, own=True) for j in range(n)]
        first += [copy(1 + i, j, me, (*chip, c), own=True) for i, chip in enumerate(chips) for j in range(n)]
        for cp in mine + first:
            cp.start()
        passed = []
        for i, chip in enumerate(chips):
            for j in range(n):
                copy(1 + i, j, (*chip, c), me).wait_recv()
                passed.append(copy(4 + i, j, (*chip, c), sibling))
                passed[-1].start()
        for j in range(n):
            copy(0, j, sibling, me).wait_recv()
        for i, chip in enumerate(chips):
            for j in range(n):
                copy(4 + i, j, (*chip, 1 - c), me).wait_recv()
        for cp in first + passed:
            cp.wait_send()
        for cp in mine:
            cp.wait()

    return pl.pallas_call(
        body, name=name, out_shape=[jax.ShapeDtypeStruct((N_DEV,) + s.shape, s.dtype) for s in shards],
        in_specs=[ANY] * n, out_specs=[ANY] * n,
        scratch_shapes=[pltpu.SemaphoreType.DMA((7, n)), pltpu.SemaphoreType.DMA((7, n)),
                        pltpu.SemaphoreType.DMA((n,))],
    )(*shards)


HBM_SPEC = pl.BlockSpec(memory_space=pltpu.HBM)
SEM_SPEC = pl.BlockSpec(memory_space=pltpu.SEMAPHORE)
SPLIT_PARAMS = pltpu.CompilerParams(has_side_effects=pltpu.SideEffectType.DATAFLOW_SIDE_EFFECTING)


def _peer_copies(src_refs, land_refs, send_sems, recv_sems, indexed):
    x, y, c = lax.axis_index("x"), lax.axis_index("y"), lax.axis_index("c")
    me = 4 * x + 2 * y + c
    copies = []
    for k in range(1, N_DEV):
        px = (1 - x) if k & 4 else x
        py = (1 - y) if k & 2 else y
        pc = (1 - c) if k & 1 else c
        for j, (src, land) in enumerate(zip(src_refs, land_refs)):
            sem = (k - 1) * len(src_refs) + j
            copies.append(pltpu.make_async_remote_copy(
                src_ref=src.at[4 * px + 2 * py + pc] if indexed else src, dst_ref=land.at[me],
                send_sem=send_sems.at[sem], recv_sem=recv_sems.at[sem],
                device_id=(px, py, pc), device_id_type=MESH))
    return copies


def scatter_start(srcs, *, name, indexed):
    n = len(srcs)
    lands = [lax.empty(s.shape if indexed else (N_DEV,) + s.shape, s.dtype) for s in srcs]

    def body(*refs):
        send_sems, recv_sems = refs[2 * n], refs[2 * n + 1]
        for cp in _peer_copies(refs[:n], refs[n:2 * n], send_sems, recv_sems, indexed):
            cp.start()
        refs[-1][...] = jnp.zeros_like(refs[-1])

    hbm = lambda a: pltpu.HBM(a.shape, a.dtype)
    sems = pltpu.SemaphoreType.DMA(((N_DEV - 1) * n,))
    res = pl.pallas_call(
        body, name=name,
        out_shape=(sems, sems, *[hbm(a) for a in srcs + lands], jax.ShapeDtypeStruct((8, LANES), F32)),
        in_specs=[HBM_SPEC] * (2 * n),
        out_specs=(SEM_SPEC, SEM_SPEC, *[HBM_SPEC] * (2 * n), pl.BlockSpec(memory_space=pltpu.VMEM)),
        input_output_aliases={i: 2 + i for i in range(2 * n)}, compiler_params=SPLIT_PARAMS,
    )(*[pltpu.with_memory_space_constraint(a, pltpu.HBM) for a in srcs + lands])
    return res[0], res[1], list(res[2:2 + n]), list(res[2 + n:2 + 2 * n]), res[-1]


def scatter_wait(send_sems, recv_sems, srcs, lands, after, *, name, indexed):
    n = len(srcs)

    def body(*refs):
        for cp in _peer_copies(refs[:n], refs[n:2 * n], refs[2 * n], refs[2 * n + 1], indexed):
            cp.wait_send()
            cp.wait_recv()

    hbm = lambda a: pltpu.HBM(a.shape, a.dtype)
    res = pl.pallas_call(
        body, name=name, out_shape=tuple(hbm(a) for a in srcs + lands),
        in_specs=[HBM_SPEC] * (2 * n) + [SEM_SPEC, SEM_SPEC, ANY], out_specs=tuple([HBM_SPEC] * (2 * n)),
        input_output_aliases={i: i for i in range(2 * n)}, compiler_params=SPLIT_PARAMS,
    )(*srcs, *lands, send_sems, recv_sems, after)
    return list(res[:n]), list(res[n:])


def _adam_rows(r, c):
    fits = [t for t in range(8, r + 1, 8) if r % t == 0 and N_DEV * t * c * 4 <= 6 * 2 ** 20]
    return max(fits) if fits else r


def adamw(recv, w, m, v, *, name):
    _, r, n = recv.shape
    tr = _adam_rows(r, n)

    def body(r_ref, w_ref, m_ref, v_ref, g_ref, d_ref, nm_ref, nv_ref):
        g = r_ref[0].astype(F32)
        for s in range(1, N_DEV):
            g = g + r_ref[s].astype(F32)
        m_new = ADAM_B1 * m_ref[...] + (1.0 - ADAM_B1) * g
        v_new = ADAM_B2 * v_ref[...] + (1.0 - ADAM_B2) * jnp.square(g)
        m_hat = m_new / (1.0 - ADAM_B1 ** ADAM_STEP)
        v_hat = v_new / (1.0 - ADAM_B2 ** ADAM_STEP)
        g_ref[...] = g
        d_ref[...] = -ADAM_LR * (m_hat / (jnp.sqrt(v_hat) + ADAM_EPS) + ADAM_WD * w_ref[...])
        nm_ref[...] = m_new
        nv_ref[...] = v_new

    blk = pl.BlockSpec((tr, n), lambda i: (i, 0))
    return pl.pallas_call(
        body, name=name, grid=(r // tr,), in_specs=[pl.BlockSpec((N_DEV, tr, n), lambda i: (0, i, 0)), blk, blk, blk],
        out_specs=[blk] * 4, out_shape=[jax.ShapeDtypeStruct((r, n), F32)] * 4,
        compiler_params=_params(("parallel",)))(recv, w, m, v)


def _shard_rows(full, axis):
    if axis == 0:
        return full.reshape(N_DEV, -1)
    r, c = full.shape
    return jnp.transpose(full.reshape(r, N_DEV, c // N_DEV), (1, 0, 2)).reshape(N_DEV, -1)


def _unshard(blocks, axis):
    if axis == 0:
        return blocks.reshape(-1, blocks.shape[-1])
    return jnp.transpose(blocks, (1, 0, 2)).reshape(blocks.shape[1], -1)


def kernel(x, norm_mix_w, w_in, ssd_conv_w, ssd_conv_b, ssd_dt_bias_fwd, ssd_dt_bias_bwd, ssd_a_log_fwd, ssd_a_log_bwd, ssd_d, ssd_norm_w, s5_lambda_re_fwd, s5_lambda_im_fwd, s5_log_step_fwd, s5_lambda_re_bwd, s5_lambda_im_bwd, s5_log_step_bwd, s5_b_re, s5_b_im, s5_c_re_fwd, s5_c_im_fwd, s5_c_re_bwd, s5_c_im_bwd, s5_d, s5_glu_w, s5_glu_b, s5_norm_w, w_out, norm_ffn_w, ffn_w_up, ffn_conv_w, ffn_conv_b, ffn_w_down, norm_final_w, loss_target, m_norm_mix_w, m_w_in, m_ssd_conv_w, m_ssd_conv_b, m_ssd_dt_bias_fwd, m_ssd_dt_bias_bwd, m_ssd_a_log_fwd, m_ssd_a_log_bwd, m_ssd_d, m_ssd_norm_w, m_s5_lambda_re_fwd, m_s5_lambda_im_fwd, m_s5_log_step_fwd, m_s5_lambda_re_bwd, m_s5_lambda_im_bwd, m_s5_log_step_bwd, m_s5_b_re, m_s5_b_im, m_s5_c_re_fwd, m_s5_c_im_fwd, m_s5_c_re_bwd, m_s5_c_im_bwd, m_s5_d, m_s5_glu_w, m_s5_glu_b, m_s5_norm_w, m_w_out, m_norm_ffn_w, m_ffn_w_up, m_ffn_conv_w, m_ffn_conv_b, m_ffn_w_down, m_norm_final_w, v_norm_mix_w, v_w_in, v_ssd_conv_w, v_ssd_conv_b, v_ssd_dt_bias_fwd, v_ssd_dt_bias_bwd, v_ssd_a_log_fwd, v_ssd_a_log_bwd, v_ssd_d, v_ssd_norm_w, v_s5_lambda_re_fwd, v_s5_lambda_im_fwd, v_s5_log_step_fwd, v_s5_lambda_re_bwd, v_s5_lambda_im_bwd, v_s5_log_step_bwd, v_s5_b_re, v_s5_b_im, v_s5_c_re_fwd, v_s5_c_im_fwd, v_s5_c_re_bwd, v_s5_c_im_bwd, v_s5_d, v_s5_glu_w, v_s5_glu_b, v_s5_norm_w, v_w_out, v_norm_ffn_w, v_ffn_w_up, v_ffn_conv_w, v_ffn_conv_b, v_ffn_w_down, v_norm_final_w):
    args = dict(locals())
    strip = lambda n, v: v if n == 'norm_final_w' else v[0]
    w = {n: strip(n, args[n]) for n in WEIGHTS}

    mats = ['w_in', 'w_out', 'ffn_w_up', 'ffn_w_down']
    convs = ['ssd_conv_w', 'ffn_conv_w']
    shard = lambda n: w[n].astype(BF16) if n in mats else w[n]
    early, late = ['w_in', 'ssd_conv_w'], ['w_out', 'ffn_w_up', 'ffn_w_down', 'ffn_conv_w']
    full = dict(w)
    full.update(zip(early, all_gather([shard(n) for n in early], name="weight_all_gather")))
    ssem, rsem, src_thru, land_thru, token = scatter_start([shard(n) for n in late], name="weight_gather_start",
                                                           indexed=False)
    me = 4 * lax.axis_index("x") + 2 * lax.axis_index("y") + lax.axis_index("c")

    def late_weights(after):
        own, landed = scatter_wait(ssem, rsem, src_thru, land_thru, after, name="weight_gather_wait", indexed=False)
        return {n: lax.dynamic_update_index_in_dim(l, o, me, 0) for n, o, l in zip(late, own, landed)}

    full['late'], full['token'] = late_weights, token[:1, :1]

    pending = []
    last = 'norm_mix_w'
    small = convs + [n for n in WEIGHTS if n not in SHARDED and n != last]
    total = sum(w[n].size for n in small) + 1
    nrow = -(-total // (PACK_ROWS * LANES)) * PACK_ROWS

    def send_early(grads, names, loss=None):
        srcs = [grads[n].astype(BF16) for n in names]
        if loss is not None:
            pieces = [grads[n].reshape(N_DEV, -1) if n in SHARDED else
                      jnp.broadcast_to(grads[n].reshape(1, -1), (N_DEV, grads[n].size)) for n in small]
            pieces += [jnp.broadcast_to(loss.reshape(1, 1), (N_DEV, 1)), jnp.zeros((N_DEV, nrow * LANES - total), F32)]
            srcs.append(jnp.concatenate(pieces, axis=1).reshape(N_DEV, nrow, LANES))
            names = names + ['small']
        started = scatter_start(srcs, name="grad_start_" + names[0], indexed=True)
        pending.append((names,) + started[:4])
        return started[4][:1, :1]

    full['on_grads'] = send_early
    loss, grad_x, g = local_step(x, loss_target, full)

    last_send = jnp.broadcast_to(g[last].reshape(1, -1, LANES), (N_DEV, g[last].size // LANES, LANES))
    last_started = scatter_start([last_send], name="grad_start_" + last, indexed=True)
    recv, outs = {}, [{}, {}, {}, {}]

    def arrived(names, started, after):
        own, landed = scatter_wait(*started, after, name="grad_wait_" + names[0], indexed=True)
        for n, o, l in zip(names, own, landed):
            recv[n] = lax.dynamic_update_index_in_dim(l, lax.dynamic_index_in_dim(o, me, 0, keepdims=False), me, 0)

    def update(n):
        shape = recv[n].shape[1:]
        res = adamw(recv[n], *[strip(n, args[p + n]).reshape(shape) for p in ('', 'm_', 'v_')], name="adamw_" + n)
        for o, p in zip(outs, res):
            o[n] = p.reshape(args[n].shape)

    for names, *started in pending:
        arrived(names, started, last_started[4])
    for n in mats:
        update(n)

    def pack(prefix):
        vals = [strip(n, args[prefix + n]).reshape(-1) for n in small]
        return jnp.pad(jnp.concatenate(vals), (0, nrow * LANES - total + 1)).reshape(nrow, LANES)

    packed = adamw(recv['small'], pack(''), pack('m_'), pack('v_'), name="adamw_small")
    arrived([last], last_started[:4], packed[1])
    update(last)
    packed = [p.reshape(-1) for p in packed]
    off = 0
    for n in small:
        size = w[n].size
        for o, p in zip(outs, packed):
            o[n] = p[off:off + size].reshape(args[n].shape)
        off += size
    loss_out = packed[0][off].reshape(())
    return (loss_out, grad_x, *[o[n] for o in outs for n in WEIGHTS])
```

```python
import functools

import jax
import jax.numpy as jnp
from jax import lax
from jax.experimental import pallas as pl
from jax.experimental.pallas import tpu as pltpu

F32, BF16 = jnp.float32, jnp.bfloat16
N_DEV = 8
D_MODEL = 1024
SSD_W, HEADS, HDIM, SGROUPS, HPG, NSTATE, SCONV, QC = 1024, 16, 64, 4, 4, 128, 5, 128
XBC_W = SSD_W + 2 * SGROUPS * NSTATE
S5_W, S5_G, S5_C, S5_P, S5_Q = 512, 32, 16, 64, 16
S5_QC = S5_Q * S5_C
CARRY_ROWS = 32
DFF, FCONV = 2816, 3
FFN_BLK, FFN_PAD = 704, 768
EPS = 1e-6
ADAM_LR, ADAM_B1, ADAM_B2, ADAM_EPS, ADAM_WD, ADAM_STEP = 0.001, 0.9, 0.999, 1e-08, 0.01, 10
LANES = 128
MESH = pl.DeviceIdType.MESH

WEIGHTS = ['norm_mix_w', 'w_in', 'ssd_conv_w', 'ssd_conv_b', 'ssd_dt_bias_fwd', 'ssd_dt_bias_bwd', 'ssd_a_log_fwd',
           'ssd_a_log_bwd', 'ssd_d', 'ssd_norm_w', 's5_lambda_re_fwd', 's5_lambda_im_fwd', 's5_log_step_fwd',
           's5_lambda_re_bwd', 's5_lambda_im_bwd', 's5_log_step_bwd', 's5_b_re', 's5_b_im', 's5_c_re_fwd', 's5_c_im_fwd',
           's5_c_re_bwd', 's5_c_im_bwd', 's5_d', 's5_glu_w', 's5_glu_b', 's5_norm_w', 'w_out', 'norm_ffn_w', 'ffn_w_up',
           'ffn_conv_w', 'ffn_conv_b', 'ffn_w_down', 'norm_final_w']
SHARDED = {'w_in': 1, 'ssd_conv_w': 1, 'w_out': 0, 'ffn_w_up': 1, 'ffn_conv_w': 1, 'ffn_w_down': 0}
FULL_SHAPE = {'w_in': (1024, 3616), 'ssd_conv_w': (5, 2048), 'w_out': (1536, 1024), 'ffn_w_up': (1024, 5632),
              'ffn_conv_w': (3, 5632), 'ffn_w_down': (2816, 1024)}
PACK_ROWS = 512


def _pick(n, cap=1536):
    if n <= cap:
        return n
    return max(t for t in range(LANES, cap + 1, LANES) if n % t == 0)


def _params(sem):
    return pltpu.CompilerParams(dimension_semantics=sem)


def _bd(a, b, ca, cb):
    return lax.dot_general(a.astype(BF16), b.astype(BF16), (((ca,), (cb,)), ((), ())), preferred_element_type=F32)


@jax.custom_vjp
def dot_nn(a, b):
    return _bd(a, b, 1, 0)


dot_nn.defvjp(lambda a, b: (_bd(a, b, 1, 0), (a, b)),
              lambda r, g: (_bd(g, r[1], 1, 1).astype(r[0].dtype), _bd(r[0], g, 0, 0).astype(r[1].dtype)))


@jax.custom_vjp
def dot_nt(a, b):
    return _bd(a, b, 1, 1)


dot_nt.defvjp(lambda a, b: (_bd(a, b, 1, 1), (a, b)),
              lambda r, g: (_bd(g, r[1], 1, 0).astype(r[0].dtype), _bd(g, r[0], 0, 0).astype(r[1].dtype)))


@jax.custom_vjp
def dot_tn(a, b):
    return _bd(a, b, 0, 0)


dot_tn.defvjp(lambda a, b: (_bd(a, b, 0, 0), (a, b)),
              lambda r, g: (_bd(r[1], g, 1, 1).astype(r[0].dtype), _bd(r[0], g, 1, 0).astype(r[1].dtype)))


def _rows2(v):
    h = v.shape[0] // 2
    return v[:h], v[h:]


def _cols2(v):
    h = v.shape[1] // 2
    return v[:, :h], v[:, h:]


@jax.custom_vjp
def dot2_nn(la, lb, x):
    return _rows2(_bd(jnp.concatenate([la, lb], axis=0), x, 1, 0))


def _dot2_nn_bwd(res, g):
    la, lb, x = res
    gcat, lcat = jnp.concatenate(g, axis=0), jnp.concatenate([la, lb], axis=0)
    return (*_rows2(_bd(gcat, x, 1, 1)), _bd(lcat, gcat, 0, 0))


dot2_nn.defvjp(lambda la, lb, x: (dot2_nn(la, lb, x), (la, lb, x)), _dot2_nn_bwd)


@jax.custom_vjp
def dot_nt2(c, p0, p1):
    return _cols2(_bd(c, jnp.concatenate([p0, p1], axis=0), 1, 1))


def _dot_nt2_bwd(res, g):
    c, p0, p1 = res
    gcat = jnp.concatenate(g, axis=1)
    return (_bd(gcat, jnp.concatenate([p0, p1], axis=0), 1, 0), *_rows2(_bd(gcat, c, 0, 0)))


dot_nt2.defvjp(lambda c, p0, p1: (dot_nt2(c, p0, p1), (c, p0, p1)), _dot_nt2_bwd)


@jax.custom_vjp
def dot_tn2(a0, a1, b):
    return _rows2(_bd(jnp.concatenate([a0, a1], axis=1), b, 0, 0))


def _dot_tn2_bwd(res, g):
    a0, a1, b = res
    gcat, acat = jnp.concatenate(g, axis=0), jnp.concatenate([a0, a1], axis=1)
    return (*_cols2(_bd(b, gcat, 1, 1)), _bd(acat, gcat, 1, 0))


dot_tn2.defvjp(lambda a0, a1, b: (dot_tn2(a0, a1, b), (a0, a1, b)), _dot_tn2_bwd)


def _split3(x):
    hi = x.astype(BF16)
    r = x - hi.astype(F32)
    mid = r.astype(BF16)
    lo = (r - mid.astype(F32)).astype(BF16)
    return hi, mid, lo


def _cum_matrix(q, upper):
    ri = lax.broadcasted_iota(jnp.int32, (q, q), 0)
    ci = lax.broadcasted_iota(jnp.int32, (q, q), 1)
    return jnp.where((ci >= ri) if upper else (ci <= ri), 1.0, 0.0).astype(BF16)


def _exact_right(x, mat):
    return sum(jnp.dot(p, mat, preferred_element_type=F32) for p in _split3(x))


@functools.partial(jax.custom_vjp, nondiff_argnums=(1,))
def cum_row(x, rev):
    return _exact_right(x, _cum_matrix(x.shape[1], not rev))


cum_row.defvjp(lambda x, rev: (cum_row(x, rev), None),
               lambda rev, _, g: (_exact_right(g, _cum_matrix(g.shape[1], rev)),))


def _softplus(x):
    return jnp.maximum(x, 0.0) + jnp.log(1.0 + jnp.exp(-jnp.abs(x)))


def _silu(x):
    return x * jax.nn.sigmoid(x)


def _gelu(x):
    return 0.5 * x * (1.0 + jnp.tanh(0.7978845608028654 * (x + 0.044715 * (x * x * x))))


def _rms(x, w):
    xf = x.astype(F32)
    return xf * lax.rsqrt(jnp.mean(xf * xf, axis=-1, keepdims=True) + EPS) * w


def matmul_sum(a_list, b_list, *, name, out_dtype=F32, add=None, tm=512, nt=False, norm_w=None, norm_bwd=None):
    a_arrs = [a[0] if isinstance(a, tuple) else a for a in a_list]
    b_arrs = [b[0] if isinstance(b, tuple) else b for b in b_list]
    m, n = a_arrs[0].shape[0], b_arrs[0].shape[-2 if nt else -1]
    tm, tn, k = min(tm, m), _pick(n), len(a_list)
    assert (norm_w is None and norm_bwd is None) or tn == n

    def body(*refs):
        acc = None
        for a_ref, b_ref in zip(refs[:k], refs[k:2 * k]):
            p = _bd(a_ref[...], b_ref[...], 1, 1 if nt else 0)
            acc = p if acc is None else acc + p
        if add is not None:
            acc = acc + refs[2 * k][...]
        if norm_bwd is not None:
            x_ref, w_ref, res_ref, dx_ref, dw_ref = refs[-5:]
            dx, dw = jax.vjp(_rms, x_ref[...], w_ref[...])[1](acc)
            dx_ref[...] = dx + res_ref[...]

            @pl.when(pl.program_id(0) == 0)
            def _():
                dw_ref[...] = jnp.zeros_like(dw_ref)

            dw_ref[...] += dw
        elif norm_w is not None:
            refs[-2][...] = acc.astype(out_dtype)
            refs[-1][...] = _rms(acc, refs[-3][...]).astype(BF16)
        else:
            refs[-1][...] = acc.astype(out_dtype)

    def a_spec(a):
        if isinstance(a, tuple):
            return pl.BlockSpec((tm, a[1]), lambda i, j, blk=a[2]: (i, blk))
        return pl.BlockSpec((tm, a.shape[1]), lambda i, j: (i, 0))

    def b_spec(b):
        arr, p = b if isinstance(b, tuple) else (b, None)
        kk = arr.shape[-1 if nt else -2]
        shape, idx = ((tn, kk), lambda j: (j, 0)) if nt else ((kk, tn), lambda j: (0, j))
        if p is None:
            return pl.BlockSpec(shape, lambda i, j: idx(j))
        return pl.BlockSpec((None,) + shape, lambda i, j, p=p: (p,) + idx(j))

    in_specs = [a_spec(a) for a in a_list] + [b_spec(b) for b in b_list]
    args = a_arrs + b_arrs
    if add is not None:
        in_specs.append(pl.BlockSpec((tm, tn), lambda i, j: (i, j)))
        args.append(add)
    out_spec, out_shape = pl.BlockSpec((tm, tn), lambda i, j: (i, j)), jax.ShapeDtypeStruct((m, n), out_dtype)
    if norm_w is not None:
        in_specs.append(pl.BlockSpec(norm_w.shape, lambda i, j: (0, 0)))
        args.append(norm_w)
        out_spec, out_shape = [out_spec, out_spec], [out_shape, jax.ShapeDtypeStruct((m, n), BF16)]
    sem = ("parallel", "parallel")
    if norm_bwd is not None:
        x, w, res = norm_bwd
        wspec = pl.BlockSpec(w.shape, lambda i, j: (0, 0))
        in_specs += [out_spec, wspec, out_spec]
        args += [x, w, res]
        out_spec, out_shape = [out_spec, wspec], [jax.ShapeDtypeStruct((m, n), F32), jax.ShapeDtypeStruct(w.shape, F32)]
        sem = ("arbitrary", "arbitrary")
    return pl.pallas_call(
        body, name=name, grid=(m // tm, n // tn), in_specs=in_specs, out_specs=out_spec, out_shape=out_shape,
        compiler_params=_params(sem))(*args)


def matmul_cols(a, b3, *, name, out_dtype=F32, tm=1024):
    m, kk = a.shape
    p, _, nb = b3.shape
    tm, tn = min(tm, m), _pick(nb, 768)
    per = nb // tn

    def body(a_ref, b_ref, o_ref):
        o_ref[...] = _bd(a_ref[...], b_ref[...], 1, 0).astype(out_dtype)

    return pl.pallas_call(
        body, name=name, grid=(m // tm, p * per),
        in_specs=[pl.BlockSpec((tm, kk), lambda i, j: (i, 0)),
                  pl.BlockSpec((None, kk, tn), lambda i, j: (j // per, 0, j % per))],
        out_specs=pl.BlockSpec((tm, tn), lambda i, j: (i, j)),
        out_shape=jax.ShapeDtypeStruct((m, p * nb), out_dtype),
        compiler_params=_params(("parallel", "parallel")))(a, b3)


def matmul_tn(a, b, *, name, tm=1024, out_blocks=None):
    m, k = a.shape
    n = b.shape[1]
    nb = n // (out_blocks or 1)
    tm, tk, tn = min(tm, m), _pick(k), _pick(nb, 768 if out_blocks else 1536)
    per = nb // tn

    def body(a_ref, b_ref, o_ref):
        @pl.when(pl.program_id(2) == 0)
        def _():
            o_ref[...] = jnp.zeros_like(o_ref)

        o_ref[...] += _bd(a_ref[...], b_ref[...], 0, 0)

    if out_blocks:
        out_spec = pl.BlockSpec((None, tk, tn), lambda i, j, t: (j // per, i, j % per))
        out_shape = jax.ShapeDtypeStruct((out_blocks, k, nb), F32)
    else:
        out_spec = pl.BlockSpec((tk, tn), lambda i, j, t: (i, j))
        out_shape = jax.ShapeDtypeStruct((k, n), F32)
    return pl.pallas_call(
        body, name=name, grid=(k // tk, n // tn, m // tm),
        in_specs=[pl.BlockSpec((tm, tk), lambda i, j, t: (t, i)), pl.BlockSpec((tm, tn), lambda i, j, t: (t, j))],
        out_specs=out_spec, out_shape=out_shape,
        compiler_params=_params(("parallel", "parallel", "arbitrary")))(a, b)


def _row_spec(r, tm):
    if isinstance(r, tuple):
        arr, width, blk = r
        return arr, pl.BlockSpec((tm, width), lambda i, blk=blk: (i, blk))
    return r, pl.BlockSpec((tm, r.shape[1]), lambda i: (i, 0))


def _full_spec(p):
    return pl.BlockSpec(p.shape, lambda i: (0,) * p.ndim)


def _expand_rows(rows, tm):
    arrays, specs, counts, widths = [], [], [], []
    for r in rows:
        parts = [_row_spec(p, tm) for p in (r if isinstance(r, list) else [r])]
        arrays += [a for a, _ in parts]
        specs += [s for _, s in parts]
        counts.append(len(parts))
        widths.append(sum(s.block_shape[1] for _, s in parts))
    return arrays, specs, counts, widths


def _row_values(refs, counts):
    vals, k = [], 0
    for c in counts:
        parts = [refs[k + j][...] for j in range(c)]
        vals.append(parts[0] if c == 1 else jnp.concatenate(parts, axis=1))
        k += c
    return vals


def _rows_of(rows):
    first = rows[0][0] if isinstance(rows[0], list) else rows[0]
    return (first[0] if isinstance(first, tuple) else first).shape[0]


def rowmap_fwd(fn, rows, params, outs, *, name, tm=256):
    m = _rows_of(rows)
    tm = min(tm, m)
    arrays, specs, counts, _ = _expand_rows(rows, tm)
    nin, npar = len(arrays), len(params)

    def body(*refs):
        res = fn(*_row_values(refs[:nin], counts), *[r[...] for r in refs[nin:nin + npar]])
        for o_ref, v in zip(refs[nin + npar:], res):
            o_ref[...] = v.astype(o_ref.dtype)

    return pl.pallas_call(
        body, name=name, grid=(m // tm,), in_specs=specs + [_full_spec(p) for p in params],
        out_specs=[pl.BlockSpec((tm, c), lambda i: (i, 0)) for c, _ in outs],
        out_shape=[jax.ShapeDtypeStruct((m, c), dt) for c, dt in outs],
        compiler_params=_params(("parallel",)))(*arrays, *params)


def rowmap_bwd(fn, rows, params, cts, *, name, row_dtypes=None, add=None, tm=256):
    m = _rows_of(rows)
    tm = min(tm, m)
    arrays, specs, counts, widths = _expand_rows(rows, tm)
    cp = [_row_spec(c, tm) for c in cts]
    nin, nr, npar, nc = len(arrays), len(rows), len(params), len(cts)
    row_dtypes = row_dtypes or [F32] * nr

    def body(*refs):
        ins = _row_values(refs[:nin], counts) + [r[...] for r in refs[nin:nin + npar]]
        ins = [v.astype(F32) for v in ins]
        ct = tuple(r[...].astype(F32) for r in refs[nin + npar:nin + npar + nc])
        base = nin + npar + nc
        extra = None
        if add is not None:
            extra = refs[base][...]
            base += 1
        _, pull = jax.vjp(fn, *ins)
        grads = pull(ct)
        for j in range(nr):
            g = grads[j]
            if j == 0 and extra is not None:
                g = g + extra
            refs[base + j][...] = g.astype(refs[base + j].dtype)

        @pl.when(pl.program_id(0) == 0)
        def _():
            for j in range(npar):
                refs[base + nr + j][...] = jnp.zeros_like(refs[base + nr + j])

        for j in range(npar):
            refs[base + nr + j][...] += grads[nr + j]

    in_specs = specs + [_full_spec(p) for p in params] + [s for _, s in cp]
    args = arrays + list(params) + [a for a, _ in cp]
    if add is not None:
        in_specs.append(pl.BlockSpec((tm, widths[0]), lambda i: (i, 0)))
        args.append(add)
    out_specs = [pl.BlockSpec((tm, w), lambda i: (i, 0)) for w in widths] + [_full_spec(p) for p in params]
    out_shape = [jax.ShapeDtypeStruct((m, w), dt) for w, dt in zip(widths, row_dtypes)]
    out_shape += [jax.ShapeDtypeStruct(p.shape, F32) for p in params]
    return pl.pallas_call(
        body, name=name, grid=(m // tm,), in_specs=in_specs, out_specs=out_specs, out_shape=out_shape,
        compiler_params=_params(("arbitrary",)))(*args)


def loss_head(h, target, w, *, name, tm=256, matmul=None):
    m, d = h.shape
    tm = min(tm, m)

    def body(h_ref, t_ref, w_ref, *refs):
        loss_ref, dh_ref, dw_ref = refs[-3:]
        rows = h_ref[...]
        if matmul is not None:
            rows = rows + _bd(refs[0][...], refs[1][...], 1, 0)
        y, pull = jax.vjp(_rms, rows, w_ref[...])
        err = y - t_ref[...]
        dh, dw = pull(err * (1.0 / d))

        @pl.when(pl.program_id(0) == 0)
        def _():
            loss_ref[...] = jnp.zeros_like(loss_ref)
            dw_ref[...] = jnp.zeros_like(dw_ref)

        loss_ref[...] += (0.5 / d) * jnp.sum(err * err, keepdims=True)
        dw_ref[...] += dw
        dh_ref[...] = dh

    row = pl.BlockSpec((tm, d), lambda i: (i, 0))
    in_specs, args = [row, row, _full_spec(w)], [h, target, w]
    if matmul is not None:
        in_specs += [pl.BlockSpec((tm, matmul[0].shape[1]), lambda i: (i, 0)), _full_spec(matmul[1])]
        args += list(matmul)
    return pl.pallas_call(
        body, name=name, grid=(m // tm,), in_specs=in_specs,
        out_specs=[pl.BlockSpec((1, 1), lambda i: (0, 0)), row, _full_spec(w)],
        out_shape=[jax.ShapeDtypeStruct((1, 1), F32), jax.ShapeDtypeStruct((m, d), F32),
                   jax.ShapeDtypeStruct(w.shape, F32)],
        compiler_params=_params(("arbitrary",)))(*args)


def _shift(x, s):
    if s == 0:
        return x
    n = x.shape[0]
    t = lax.broadcasted_iota(jnp.int32, x.shape, 0)
    rolled = pltpu.roll(x, (-s) % n, 0)
    return jnp.where((t + s >= 0) & (t + s < n), rolled, 0.0)


def _conv(x, w, b):
    k = w.shape[0]
    acc = b + w[k // 2:k // 2 + 1, :] * x
    for j in range(k):
        if j != k // 2:
            acc = acc + w[j:j + 1, :] * _shift(x, j - k // 2)
    return acc


def _conv_bwd(x, dc, w):
    k = w.shape[0]
    dx = None
    dws = []
    for j in range(k):
        s = j - k // 2
        term = w[j:j + 1, :] * _shift(dc, -s)
        dx = term if dx is None else dx + term
        dws.append(jnp.sum(dc * _shift(x, s), axis=0, keepdims=True))
    return dx, jnp.concatenate(dws, axis=0), jnp.sum(dc, axis=0, keepdims=True)


def _dsilu(c):
    s = jax.nn.sigmoid(c)
    return s * (1.0 + c * (1.0 - s))


def ssd_conv_fwd(xbc, w, b, *, bsz, name):
    t, c = xbc.shape
    seq, ct = t // bsz, 256

    def body(x_ref, w_ref, b_ref, o_ref):
        o_ref[...] = _silu(_conv(x_ref[...], w_ref[...], b_ref[...]))

    return pl.pallas_call(
        body, name=name, grid=(c // ct, bsz),
        in_specs=[pl.BlockSpec((seq, ct), lambda j, i: (i, j)), pl.BlockSpec((w.shape[0], ct), lambda j, i: (0, j)),
                  pl.BlockSpec((1, ct), lambda j, i: (0, j))],
        out_specs=pl.BlockSpec((seq, ct), lambda j, i: (i, j)),
        out_shape=jax.ShapeDtypeStruct((t, c), F32),
        compiler_params=_params(("parallel", "parallel")))(xbc, w, b)


def ssd_conv_bwd(xbc, dparts, w, b, *, bsz, name):
    t, c = xbc.shape
    seq, ct, k = t // bsz, 256, w.shape[0]
    starts = [0]
    for p in dparts:
        starts.append(starts[-1] + p.shape[1] // ct)

    def body(x_ref, *refs):
        g_refs, (w_ref, b_ref, dx_ref, dw_ref, db_ref) = refs[:len(dparts)], refs[len(dparts):]
        j = pl.program_id(0)
        g = g_refs[-1][...]
        for n in range(len(dparts) - 2, -1, -1):
            g = jnp.where(j < starts[n + 1], g_refs[n][...], g)
        x, wv = x_ref[...], w_ref[...]
        dc = g * _dsilu(_conv(x, wv, b_ref[...]))
        dx, dw, db = _conv_bwd(x, dc, wv)
        dx_ref[...] = dx

        @pl.when(pl.program_id(1) == 0)
        def _():
            dw_ref[...] = jnp.zeros_like(dw_ref)
            db_ref[...] = jnp.zeros_like(db_ref)

        dw_ref[...] += dw
        db_ref[...] += db

    def part_spec(n):
        lo, hi = starts[n], starts[n + 1]

        def index(j, i):
            inside = (j >= lo) & (j < hi)
            return jnp.where(inside, i, 0), jnp.where(inside, j - lo, 0)

        return pl.BlockSpec((seq, ct), index)

    blk = pl.BlockSpec((seq, ct), lambda j, i: (i, j))
    wspec, bspec = pl.BlockSpec((k, ct), lambda j, i: (0, j)), pl.BlockSpec((1, ct), lambda j, i: (0, j))
    return pl.pallas_call(
        body, name=name, grid=(c // ct, bsz),
        in_specs=[blk] + [part_spec(n) for n in range(len(dparts))] + [wspec, bspec], out_specs=[blk, wspec, bspec],
        out_shape=[jax.ShapeDtypeStruct((t, c), F32), jax.ShapeDtypeStruct((k, c), F32),
                   jax.ShapeDtypeStruct((1, c), F32)],
        compiler_params=_params(("parallel", "arbitrary")))(xbc, *dparts, w, b)


def _ffn_specs(seq, ct, k, nblk):
    val = pl.BlockSpec((seq, ct), lambda j, i: (i, j))
    gate = pl.BlockSpec((seq, ct), lambda j, i: (i, nblk + j))
    wv, wg = pl.BlockSpec((k, ct), lambda j, i: (0, j)), pl.BlockSpec((k, ct), lambda j, i: (0, nblk + j))
    bv, bg = pl.BlockSpec((1, ct), lambda j, i: (0, j)), pl.BlockSpec((1, ct), lambda j, i: (0, nblk + j))
    return val, gate, wv, wg, bv, bg


def ffn_act_fwd(up, w, b, *, bsz, name):
    t = up.shape[0]
    half = up.shape[1] // 2
    seq, ct, k = t // bsz, 256, w.shape[0]
    val, gate, wv, wg, bv, bg = _ffn_specs(seq, ct, k, half // ct)

    def body(v_ref, g_ref, wv_ref, wg_ref, bv_ref, bg_ref, o_ref):
        vc = _conv(v_ref[...].astype(F32), wv_ref[...], bv_ref[...])
        gc = _conv(g_ref[...].astype(F32), wg_ref[...], bg_ref[...])
        o_ref[...] = (_silu(gc) * vc).astype(BF16)

    return pl.pallas_call(
        body, name=name, grid=(half // ct, bsz), in_specs=[val, gate, wv, wg, bv, bg], out_specs=val,
        out_shape=jax.ShapeDtypeStruct((t, half), BF16),
        compiler_params=_params(("parallel", "parallel")))(up, up, w, w, b, b)


def ffn_act_bwd(up, dact, w, b, *, bsz, name):
    t = up.shape[0]
    half = up.shape[1] // 2
    seq, ct, k = t // bsz, 256, w.shape[0]
    val, gate, wv, wg, bv, bg = _ffn_specs(seq, ct, k, half // ct)

    def body(v_ref, g_ref, wv_ref, wg_ref, bv_ref, bg_ref, d_ref, dv_ref, dg_ref, dwv_ref, dwg_ref, dbv_ref, dbg_ref):
        v, g = v_ref[...].astype(F32), g_ref[...].astype(F32)
        vc = _conv(v, wv_ref[...], bv_ref[...])
        gc = _conv(g, wg_ref[...], bg_ref[...])
        d = d_ref[...].astype(F32)
        sg = jax.nn.sigmoid(gc)
        dv, dwv, dbv = _conv_bwd(v, d * (gc * sg), wv_ref[...])
        dg, dwg, dbg = _conv_bwd(g, d * vc * (sg * (1.0 + gc * (1.0 - sg))), wg_ref[...])
        dv_ref[...] = dv.astype(BF16)
        dg_ref[...] = dg.astype(BF16)

        @pl.when(pl.program_id(1) == 0)
        def _():
            for r in (dwv_ref, dwg_ref, dbv_ref, dbg_ref):
                r[...] = jnp.zeros_like(r)

        dwv_ref[...] += dwv
        dwg_ref[...] += dwg
        dbv_ref[...] += dbv
        dbg_ref[...] += dbg

    return pl.pallas_call(
        body, name=name, grid=(half // ct, bsz), in_specs=[val, gate, wv, wg, bv, bg, val],
        out_specs=[val, val, wv, wv, bv, bv],
        out_shape=[jax.ShapeDtypeStruct((t, half), BF16), jax.ShapeDtypeStruct((t, half), BF16),
                   jax.ShapeDtypeStruct((k, half), F32), jax.ShapeDtypeStruct((k, half), F32),
                   jax.ShapeDtypeStruct((1, half), F32), jax.ShapeDtypeStruct((1, half), F32)],
        compiler_params=_params(("parallel", "arbitrary")))(up, up, w, w, b, b, dact)


def _sel_row(a, h):
    oh = (lax.broadcasted_iota(jnp.int32, (a.shape[0], 1), 0) == h).astype(F32)
    return jnp.sum(a * oh, axis=0, keepdims=True)


def _ssd_chunk(xp, dtr, bm, cm, prev, bias_r, alog_r, dskip_r, rev):
    q = dtr.shape[1]
    ri = lax.broadcasted_iota(jnp.int32, (q, q), 0)
    ci = lax.broadcasted_iota(jnp.int32, (q, q), 1)
    mask = (ci >= ri) if rev else (ci <= ri)
    lane_lo, row_lo = ci < HDIM, ri < HDIM
    dt_r = _softplus(dtr + bias_r)
    dta_r = dt_r * (-jnp.exp(alog_r))
    cs_r = cum_row(dta_r, rev)
    scores = dot_nt(cm, bm)

    def per_row(v):
        return jnp.broadcast_to(v, (q, q)).T

    assert len(xp) == 2
    y_diag, csqs, decayed, tots = [], [], [], []
    for p in range(2):
        ha = 2 * p + (HPG if rev else 0)
        hb = ha + 1
        cs_a, cs_b = _sel_row(cs_r, ha), _sel_row(cs_r, hb)
        csq_a, csq_b = per_row(cs_a), per_row(cs_b)
        seg_a = jnp.exp(jnp.where(mask, csq_a - cs_a, -1e30))
        seg_b = jnp.exp(jnp.where(mask, csq_b - cs_b, -1e30))
        csq = jnp.where(lane_lo, csq_a, csq_b)
        xdt = xp[p] * jnp.where(lane_lo, per_row(_sel_row(dt_r, ha)), per_row(_sel_row(dt_r, hb)))
        tot_a = jnp.sum(_sel_row(dta_r, ha), axis=1, keepdims=True)
        tot_b = jnp.sum(_sel_row(dta_r, hb), axis=1, keepdims=True)
        y_diag.append(jnp.where(lane_lo, *dot2_nn(scores * seg_a, scores * seg_b, xdt)))
        csqs.append(csq)
        decayed.append(xdt * jnp.exp(jnp.where(lane_lo, tot_a, tot_b) - csq))
        tots.append((tot_a, tot_b, ha, hb))
    y_off = dot_nt2(cm, *prev)
    states = dot_tn2(*decayed, bm)
    ys, news = [], []
    for p, (tot_a, tot_b, ha, hb) in enumerate(tots):
        y = y_diag[p] + y_off[p] * jnp.exp(csqs[p])
        if not rev:
            y = y + jnp.where(lane_lo, _sel_row(dskip_r, ha), _sel_row(dskip_r, hb)) * xp[p]
        ys.append(y)
        news.append(jnp.exp(jnp.where(row_lo, tot_a, tot_b)) * prev[p] + states[p])
    return tuple(ys), tuple(news)


NPAIR = HPG // 2


def _ssd_specs(seq, nc):
    xs = pl.BlockSpec((None, seq, HPG * HDIM), lambda b, g: (b, 0, g))
    bm = pl.BlockSpec((None, seq, NSTATE), lambda b, g: (b, 0, SSD_W // NSTATE + g))
    cm = pl.BlockSpec((None, seq, NSTATE), lambda b, g: (b, 0, SSD_W // NSTATE + SGROUPS + g))
    dtr = pl.BlockSpec((None, None, 2 * HPG, seq), lambda b, g: (b, g, 0, 0))
    pr = pl.BlockSpec((None, 2 * HPG, 1), lambda b, g: (g, 0, 0))
    st = pl.BlockSpec((None, None, 2, nc, NPAIR, 2 * HDIM, NSTATE), lambda b, g: (b, g, 0, 0, 0, 0, 0))
    return xs, bm, cm, dtr, pr, st


def _pair_cols(p):
    return slice(2 * HDIM * p, 2 * HDIM * (p + 1))


def ssd_scan_fwd(act, dtr, prs, *, name):
    bsz, seq, _ = act.shape
    nc = seq // QC
    xs, bm, cm, dtrs, pr, st = _ssd_specs(seq, nc)

    def body(x_ref, b_ref, c_ref, dtr_ref, br_ref, ar_ref, dk_ref, y_ref, st_ref):
        par = (br_ref[...], ar_ref[...], dk_ref[...])
        y_ref[...] = jnp.zeros_like(y_ref)

        def step(i, carry):
            new = []
            for rev in (False, True):
                k = (nc - 1 - i) if rev else i
                rows = pl.ds(pl.multiple_of(k * QC, QC), QC)
                xp = tuple(x_ref[rows, _pair_cols(p)] for p in range(NPAIR))
                for p in range(NPAIR):
                    st_ref[int(rev), k, p] = carry[rev][p]
                ys, nw = _ssd_chunk(xp, dtr_ref[:, rows], b_ref[rows, :], c_ref[rows, :], carry[rev], *par, rev)
                for p in range(NPAIR):
                    y_ref[rows, _pair_cols(p)] += ys[p]
                new.append(nw)
            return tuple(new)

        zero = tuple(jnp.zeros((2 * HDIM, NSTATE), F32) for _ in range(NPAIR))
        lax.fori_loop(0, nc // 2, lambda i, c: step(2 * i + 1, step(2 * i, c)), (zero, zero))

    return pl.pallas_call(
        body, name=name, grid=(bsz, SGROUPS), in_specs=[xs, bm, cm, dtrs, pr, pr, pr], out_specs=[xs, st],
        out_shape=[jax.ShapeDtypeStruct((bsz, seq, SSD_W), F32),
                   jax.ShapeDtypeStruct((bsz, SGROUPS, 2, nc, NPAIR, 2 * HDIM, NSTATE), F32)],
        compiler_params=_params(("parallel", "parallel")))(act, act, act, dtr, *prs)


def ssd_scan_bwd(act, dtr, prs, states, dy, *, name):
    bsz, seq, _ = act.shape
    nc = seq // QC
    xs, bm, cm, dtrs, pr, st = _ssd_specs(seq, nc)
    grp = pl.BlockSpec((None, seq, NSTATE), lambda b, g: (b, 0, g))
    dpr = pl.BlockSpec((None, None, 2 * HPG, 1), lambda b, g: (b, g, 0, 0))

    def body(x_ref, b_ref, c_ref, dtr_ref, br_ref, ar_ref, dk_ref, st_ref, dy_ref,
             dx_ref, db_ref, dc_ref, ddtr_ref, gbr_ref, gar_ref, gdk_ref):
        par = (br_ref[...], ar_ref[...], dk_ref[...])
        pgrads = (gbr_ref, gar_ref, gdk_ref)
        for r in pgrads + (dx_ref, db_ref, dc_ref, ddtr_ref):
            r[...] = jnp.zeros_like(r)

        def bstep(i, dcarry):
            new = []
            for rev in (False, True):
                k = i if rev else (nc - 1 - i)
                rows = pl.ds(pl.multiple_of(k * QC, QC), QC)
                xp = tuple(x_ref[rows, _pair_cols(p)] for p in range(NPAIR))
                prev = tuple(st_ref[int(rev), k, p] for p in range(NPAIR))
                _, pull = jax.vjp(functools.partial(_ssd_chunk, rev=rev), xp, dtr_ref[:, rows], b_ref[rows, :],
                                  c_ref[rows, :], prev, *par)
                dyp = tuple(dy_ref[rows, _pair_cols(p)] for p in range(NPAIR))
                gx, gdt, gb, gc, gprev, *gpar = pull((dyp, dcarry[rev]))
                for p in range(NPAIR):
                    dx_ref[rows, _pair_cols(p)] += gx[p]
                ddtr_ref[:, rows] += gdt
                db_ref[rows, :] += gb
                dc_ref[rows, :] += gc
                for r, g in zip(pgrads, gpar):
                    r[...] += g
                new.append(gprev)
            return tuple(new)

        zero = tuple(jnp.zeros((2 * HDIM, NSTATE), F32) for _ in range(NPAIR))
        lax.fori_loop(0, nc, bstep, (zero, zero))

    out_shape = [jax.ShapeDtypeStruct((bsz, seq, SSD_W), F32),
                 jax.ShapeDtypeStruct((bsz, seq, SGROUPS * NSTATE), F32),
                 jax.ShapeDtypeStruct((bsz, seq, SGROUPS * NSTATE), F32),
                 jax.ShapeDtypeStruct(dtr.shape, F32)]
    out_shape += [jax.ShapeDtypeStruct((bsz, SGROUPS, 2 * HPG, 1), F32)] * 3
    return pl.pallas_call(
        body, name=name, grid=(bsz, SGROUPS), in_specs=[xs, bm, cm, dtrs, pr, pr, pr, st, xs],
        out_specs=[xs, grp, grp, dtrs, dpr, dpr, dpr], out_shape=out_shape,
        compiler_params=_params(("parallel", "parallel")))(act, act, act, dtr, *prs, states, dy)


def _s5_core(lam_re, lam_im, log_step, b_re, b_im, c_re, c_im):
    q = S5_Q
    step = jnp.exp(log_step)[:, None]
    lr, li = lam_re * step, lam_im * step
    mag = jnp.exp(lr)
    ar, ai = mag * jnp.cos(li), mag * jnp.sin(li)
    den = lam_re * lam_re + lam_im * lam_im
    cr = ((ar - 1.0) * lam_re + ai * lam_im) / den
    ci = (ai * lam_re - (ar - 1.0) * lam_im) / den
    bbr = cr[..., None] * b_re - ci[..., None] * b_im
    bbi = cr[..., None] * b_im + ci[..., None] * b_re
    d = jnp.arange(q + 1, dtype=F32)[None, :, None]
    pm = jnp.exp(d * lr[:, None, :])
    pr, pi = pm * jnp.cos(d * li[:, None, :]), pm * jnp.sin(d * li[:, None, :])
    er = pr[..., None] * bbr[:, None] - pi[..., None] * bbi[:, None]
    ei = pr[..., None] * bbi[:, None] + pi[..., None] * bbr[:, None]
    hp = lax.Precision.HIGHEST
    k = (jnp.einsum('gcp,gdpz->gdcz', c_re, er[:, :q], precision=hp)
         - jnp.einsum('gcp,gdpz->gdcz', c_im, ei[:, :q], precision=hp))
    e = jnp.concatenate([er[:, :q], ei[:, :q]], axis=2)
    p1r, p1i = pr[:, 1:], pi[:, 1:]
    m_re = c_re[:, None] * p1r[:, :, None, :] - c_im[:, None] * p1i[:, :, None, :]
    m_im = -c_re[:, None] * p1i[:, :, None, :] - c_im[:, None] * p1r[:, :, None, :]
    da = jnp.concatenate([pr[:, q], pr[:, q]], axis=-1)
    db = jnp.concatenate([-pi[:, q], pi[:, q]], axis=-1)
    return k, e, jnp.concatenate([m_re, m_im], axis=-1), da, db


def _s5_operators(lf_re, lf_im, lsf, lb_re, lb_im, lsb, b_re, b_im, cf_re, cf_im, cb_re, cb_im):
    g = lf_re.shape[0]
    both = lambda f, b: jnp.concatenate([f, b], axis=0)
    k, e, m, da, db = _s5_core(both(lf_re, lb_re), both(lf_im, lb_im), both(lsf, lsb), both(b_re, b_re),
                               both(b_im, b_im), both(cf_re, cb_re), both(cf_im, cb_im))
    kf, kb = k[:g], k[g:]
    wtf, wtb = jnp.transpose(e[:g, ::-1], (0, 1, 3, 2)), jnp.transpose(e[g:], (0, 1, 3, 2))
    mtf, mtb = jnp.transpose(m[:g], (0, 3, 1, 2)), jnp.transpose(m[g:, ::-1], (0, 3, 1, 2))
    daf, dab, dbf, dbb = da[:g], da[g:], db[:g], db[g:]
    lags = jnp.concatenate([kb[:, :0:-1], kf[:, :1] + kb[:, :1], kf[:, 1:]], axis=1)
    tt = jnp.transpose(lags, (0, 1, 3, 2))
    wt = jnp.concatenate([wtf.reshape(g, S5_QC, 2 * S5_P), wtb.reshape(g, S5_QC, 2 * S5_P)], axis=-1)
    mt = jnp.concatenate([mtf.reshape(g, 2 * S5_P, S5_QC), mtb.reshape(g, 2 * S5_P, S5_QC)], axis=1)
    return tt, wt, mt, jnp.concatenate([daf, dab], -1), jnp.concatenate([dbf, dbb], -1)


def _gspec(*shape):
    return pl.BlockSpec((None,) + shape, lambda g: (g,) + (0,) * len(shape))


S5_HALVES = S5_QC // LANES


def _toeplitz_block(s, t):
    per = LANES // S5_C
    return t // per, slice(s * S5_C, (s + 1) * S5_C), slice((t % per) * S5_C, (t % per + 1) * S5_C)


def s5_toeplitz(kt, *, name):
    g = kt.shape[0]

    def body(k_ref, t_ref):
        for s in range(S5_Q):
            for t in range(S5_Q):
                t_ref[_toeplitz_block(s, t)] = k_ref[t - s + S5_Q - 1]

    return pl.pallas_call(
        body, name=name, grid=(g,), in_specs=[_gspec(2 * S5_Q - 1, S5_C, S5_C)],
        out_specs=_gspec(S5_HALVES, S5_QC, LANES), out_shape=jax.ShapeDtypeStruct((g, S5_HALVES, S5_QC, LANES), F32),
        compiler_params=_params(("parallel",)))(kt)


def s5_toeplitz_bwd(dtt, *, name):
    g = dtt.shape[0]

    def body(d_ref, k_ref):
        for j in range(2 * S5_Q - 1):
            acc = None
            for s in range(S5_Q):
                t = j - (S5_Q - 1) + s
                if 0 <= t < S5_Q:
                    blk = d_ref[_toeplitz_block(s, t)]
                    acc = blk if acc is None else acc + blk
            k_ref[j] = acc

    return pl.pallas_call(
        body, name=name, grid=(g,), in_specs=[_gspec(S5_HALVES, S5_QC, LANES)],
        out_specs=_gspec(2 * S5_Q - 1, S5_C, S5_C), out_shape=jax.ShapeDtypeStruct((g, 2 * S5_Q - 1, S5_C, S5_C), F32),
        compiler_params=_params(("parallel",)))(dtt)


S5_RT = 64


def _chunk_piece(q):
    per = LANES // S5_C
    return q // per, slice((q % per) * S5_C, (q % per + 1) * S5_C)


def to_chunks(u, *, name):
    t = u.shape[0]
    r = t // S5_Q
    rt = min(S5_RT, r)

    per = LANES // S5_C
    nblk = S5_W // LANES

    def body(*refs):
        o_ref = refs[-1]
        for k in range(nblk):
            for q in range(S5_Q):
                rows = refs[k][pl.ds(q, rt, stride=S5_Q), :]
                half, lanes = _chunk_piece(q)
                for j in range(per):
                    o_ref[k * per + j, half, :, lanes] = rows[:, j * S5_C:(j + 1) * S5_C]

    return pl.pallas_call(
        body, name=name, grid=(r // rt,),
        in_specs=[pl.BlockSpec((rt * S5_Q, LANES), lambda i, k=k: (i, k)) for k in range(nblk)],
        out_specs=pl.BlockSpec((S5_G, S5_HALVES, rt, LANES), lambda i: (0, 0, i, 0)),
        out_shape=jax.ShapeDtypeStruct((S5_G, S5_HALVES, r, LANES), F32),
        compiler_params=_params(("parallel",)))(*[u] * nblk)


def from_chunks(y, *, name, add=None, as_blocks=False):
    r = y.shape[2]
    rt = min(S5_RT, r)
    per = LANES // S5_C

    nblk = S5_W // LANES

    def body(*refs):
        y_ref, tmp_ref = refs[0], refs[-1]
        adds, outs = refs[1:-1 - nblk], refs[-1 - nblk:-1]
        for k in range(nblk):
            for q in range(S5_Q):
                half, lanes = _chunk_piece(q)
                for j in range(per):
                    tmp_ref[:, j * S5_C:(j + 1) * S5_C] = y_ref[k * per + j, half, :, lanes]
                row = tmp_ref[...]
                if add is not None:
                    row = row + adds[k][pl.ds(q, rt, stride=S5_Q), :]
                outs[k][pl.ds(q, rt, stride=S5_Q), :] = row

    in_specs = [pl.BlockSpec((S5_G, S5_HALVES, rt, LANES), lambda i: (0, 0, i, 0))]
    if add is not None:
        in_specs += [pl.BlockSpec((rt * S5_Q, LANES), lambda i, k=k: (i, k)) for k in range(nblk)]
    blocks = pl.pallas_call(
        body, name=name, grid=(r // rt,), in_specs=in_specs,
        out_specs=[pl.BlockSpec((rt * S5_Q, LANES), lambda i: (i, 0))] * nblk,
        out_shape=[jax.ShapeDtypeStruct((r * S5_Q, LANES), F32)] * nblk,
        scratch_shapes=[pltpu.VMEM((rt, LANES), F32)],
        compiler_params=_params(("parallel",)))(*([y] if add is None else [y] + [add] * nblk))
    return list(blocks) if as_blocks else jnp.concatenate(blocks, axis=1)


def _cat(ref):
    return jnp.concatenate([ref[h] for h in range(S5_HALVES)], axis=1)


def _put(ref, v):
    for h in range(S5_HALVES):
        ref[h] = v[:, h * LANES:(h + 1) * LANES]


def _cspec(r):
    return _gspec(S5_HALVES, r, LANES)


def s5_state_in(u, wt, *, name):
    g, _, r, _ = u.shape

    def body(u_ref, w_ref, o_ref):
        o_ref[...] = _bd(_cat(u_ref), w_ref[...], 1, 0)

    return pl.pallas_call(
        body, name=name, grid=(g,), in_specs=[_cspec(r), _gspec(S5_QC, 4 * S5_P)],
        out_specs=_gspec(r, 4 * S5_P), out_shape=jax.ShapeDtypeStruct((g, r, 4 * S5_P), F32),
        compiler_params=_params(("parallel",)))(u, wt)


def _swap(h):
    return pltpu.roll(h, S5_P, 1)


def s5_carry_fwd(s, da, db, *, name):
    nck, rows, _ = s.shape
    w = 2 * S5_P

    def body(s_ref, da_ref, db_ref, h_ref):
        dirs = ((False, slice(0, w)), (True, slice(w, 2 * w)))
        coef = [(da_ref[:, cols], db_ref[:, cols]) for _, cols in dirs]

        def step(i, hs):
            new = []
            for (rev, cols), (a, b), h in zip(dirs, coef, hs):
                k = (nck - 1 - i) if rev else i
                h_ref[k, :, cols] = h
                new.append(a * h + b * _swap(h) + s_ref[k, :, cols])
            return tuple(new)

        z = jnp.zeros((rows, w), F32)
        lax.fori_loop(0, nck, step, (z, z), unroll=2)

    rt = min(CARRY_ROWS, rows)
    big, small = pl.BlockSpec((nck, rt, 2 * w), lambda i: (0, i, 0)), pl.BlockSpec((rt, 2 * w), lambda i: (i, 0))
    rows = rt
    return pl.pallas_call(
        body, name=name, grid=(s.shape[1] // rt,), in_specs=[big, small, small], out_specs=big,
        out_shape=jax.ShapeDtypeStruct(s.shape, F32), compiler_params=_params(("parallel",)))(s, da, db)


def s5_carry_bwd(hin, dh, da, db, *, name):
    nck, rows, _ = hin.shape
    w = 2 * S5_P

    def body(h_ref, dh_ref, da_ref, db_ref, ds_ref, gda_ref, gdb_ref):
        dirs = ((False, slice(0, w)), (True, slice(w, 2 * w)))
        coef = [(da_ref[:, cols], db_ref[:, cols]) for _, cols in dirs]

        def step(i, carries):
            new = []
            for (rev, cols), (a, b), (g, ga, gb) in zip(dirs, coef, carries):
                k = i if rev else (nck - 1 - i)
                ds_ref[k, :, cols] = g
                h = h_ref[k, :, cols]
                new.append((dh_ref[k, :, cols] + a * g + _swap(b * g), ga + g * h, gb + g * _swap(h)))
            return tuple(new)

        z = jnp.zeros((rows, w), F32)
        res = lax.fori_loop(0, nck, step, ((z, z, z), (z, z, z)), unroll=2)
        for (_, cols), (_, ga, gb) in zip(dirs, res):
            gda_ref[:, cols] = ga
            gdb_ref[:, cols] = gb

    rt = min(CARRY_ROWS, rows)
    big, small = pl.BlockSpec((nck, rt, 2 * w), lambda i: (0, i, 0)), pl.BlockSpec((rt, 2 * w), lambda i: (i, 0))
    rows = rt
    return pl.pallas_call(
        body, name=name, grid=(hin.shape[1] // rt,), in_specs=[big, big, small, small], out_specs=[big, small, small],
        out_shape=[jax.ShapeDtypeStruct(hin.shape, F32), jax.ShapeDtypeStruct(da.shape, F32),
                   jax.ShapeDtypeStruct(da.shape, F32)],
        compiler_params=_params(("parallel",)))(hin, dh, da, db)


def s5_out(u, hin, tt, mt, *, name):
    g, _, r, _ = u.shape

    def body(u_ref, h_ref, t_ref, m_ref, o_ref):
        u_v, h_v = _cat(u_ref), h_ref[...]
        for half in range(S5_HALVES):
            cols = slice(half * LANES, (half + 1) * LANES)
            o_ref[half] = _bd(u_v, t_ref[half], 1, 0) + _bd(h_v, m_ref[:, cols], 1, 0)

    return pl.pallas_call(
        body, name=name, grid=(g,),
        in_specs=[_cspec(r), _gspec(r, 4 * S5_P), _gspec(S5_HALVES, S5_QC, LANES), _gspec(4 * S5_P, S5_QC)],
        out_specs=_cspec(r), out_shape=jax.ShapeDtypeStruct((g, S5_HALVES, r, LANES), F32),
        compiler_params=_params(("parallel",)))(u, hin, tt, mt)


def s5_out_bwd(dy, u, hin, tt, mt, *, name):
    g, _, r, _ = u.shape

    def body(dy_ref, u_ref, h_ref, t_ref, m_ref, dh_ref, dt_ref, dm_ref, du_ref):
        dy_v, u_v = _cat(dy_ref), _cat(u_ref)
        dh_ref[...] = _bd(dy_v, m_ref[...], 1, 1)
        dm_ref[...] = _bd(h_ref[...], dy_v, 0, 0)
        du = None
        for half in range(S5_HALVES):
            dy_h = dy_ref[half]
            dt_ref[half] = _bd(u_v, dy_h, 0, 0)
            part = _bd(dy_h, t_ref[half], 1, 1)
            du = part if du is None else du + part
        _put(du_ref, du)

    tspec = _gspec(S5_HALVES, S5_QC, LANES)
    return pl.pallas_call(
        body, name=name, grid=(g,),
        in_specs=[_cspec(r), _cspec(r), _gspec(r, 4 * S5_P), tspec, _gspec(4 * S5_P, S5_QC)],
        out_specs=[_gspec(r, 4 * S5_P), tspec, _gspec(4 * S5_P, S5_QC), _cspec(r)],
        out_shape=[jax.ShapeDtypeStruct((g, r, 4 * S5_P), F32), jax.ShapeDtypeStruct((g, S5_HALVES, S5_QC, LANES), F32),
                   jax.ShapeDtypeStruct((g, 4 * S5_P, S5_QC), F32), jax.ShapeDtypeStruct((g, S5_HALVES, r, LANES), F32)],
        compiler_params=_params(("parallel",)))(dy, u, hin, tt, mt)


def s5_state_in_bwd(ds, u, wt, du1, *, name):
    g, _, r, _ = u.shape

    def body(ds_ref, u_ref, w_ref, du1_ref, du_ref, dw_ref):
        ds_v = ds_ref[...]
        _put(du_ref, _cat(du1_ref) + _bd(ds_v, w_ref[...], 1, 1))
        dw_ref[...] = _bd(_cat(u_ref), ds_v, 0, 0)

    return pl.pallas_call(
        body, name=name, grid=(g,),
        in_specs=[_gspec(r, 4 * S5_P), _cspec(r), _gspec(S5_QC, 4 * S5_P), _cspec(r)],
        out_specs=[_cspec(r), _gspec(S5_QC, 4 * S5_P)],
        out_shape=[jax.ShapeDtypeStruct((g, S5_HALVES, r, LANES), F32), jax.ShapeDtypeStruct((g, S5_QC, 4 * S5_P), F32)],
        compiler_params=_params(("parallel",)))(ds, u, wt, du1)


def _s5_post(ypre, u, dvec, wv, wg, bv, bg, nw):
    g = _gelu(ypre + dvec * u)
    out = (dot_nn(g, wv) + bv) * jax.nn.sigmoid(dot_nn(g, wg) + bg)
    return (_rms(out, nw),)


def _ssd_post(y, z, nw):
    return (_rms(y * _silu(z), nw),)


def _to_carry(s, bsz):
    nck = s.shape[1] // bsz
    return jnp.transpose(s.reshape(S5_G, bsz, nck, -1), (2, 0, 1, 3)).reshape(nck, S5_G * bsz, -1)


def _from_carry(h, bsz):
    nck = h.shape[0]
    return jnp.transpose(h.reshape(nck, S5_G, bsz, -1), (1, 2, 0, 3)).reshape(S5_G, bsz * nck, -1)


def _block_diag(w):
    eye = jnp.eye(S5_G, dtype=w.dtype)
    return jnp.einsum('gcd,gh->gchd', w, eye).reshape(S5_W, S5_W)


def _diag_blocks(w):
    v = w.reshape(S5_G, S5_C, S5_G, S5_C)
    return v[jnp.arange(S5_G), :, jnp.arange(S5_G), :]


def _dt_rows(dt, bsz):
    seq = dt.shape[0] // bsz
    return jnp.transpose(dt.reshape(bsz, seq, 2, SGROUPS, HPG), (0, 3, 2, 4, 1)).reshape(bsz, SGROUPS, 2 * HPG, seq)


def _dt_from_rows(dr):
    bsz, _, _, seq = dr.shape
    return jnp.transpose(dr.reshape(bsz, SGROUPS, 2, HPG, seq), (0, 4, 2, 1, 3)).reshape(bsz * seq, 2 * HEADS)


def _head_params(f, b):
    return jnp.concatenate([f.reshape(SGROUPS, HPG), b.reshape(SGROUPS, HPG)], axis=1)[:, :, None]


def _head_grads(gr):
    v = gr.sum(0)[:, :, 0]
    return v[:, :HPG].reshape(HEADS), v[:, HPG:].reshape(HEADS)


def local_step(x, target, w):
    bsz, seq, d = x.shape
    t = bsz * seq
    x2, tgt2 = x.reshape(t, d), target.reshape(t, d)
    g = {}
    row = lambda v: v.reshape(1, -1)
    bf = lambda v: v.astype(BF16)

    w_in = _unshard(bf(w['w_in']), SHARDED['w_in'])
    cuts = [0, SSD_W, SSD_W + XBC_W, SSD_W + XBC_W + 2 * HEADS, w_in.shape[1]]
    w_in_parts = [w_in[:, a:b] for a, b in zip(cuts[:-1], cuts[1:])]
    norm_mix = row(w['norm_mix_w']) + w.get('token', 0.0)
    (hn,) = rowmap_fwd(lambda a, nw: (_rms(a, nw),), [x2], [norm_mix], [(d, BF16)], name="rms_mix")
    z, xbc, dt, u = [matmul_sum([hn], [p], tm=1024, name=f"in_proj_{i}") for i, p in enumerate(w_in_parts)]

    conv_w, conv_b = _unshard(w['ssd_conv_w'], SHARDED['ssd_conv_w']), row(w['ssd_conv_b'])
    act = ssd_conv_fwd(xbc, conv_w, conv_b, bsz=bsz, name="ssd_conv")
    dtr = _dt_rows(dt, bsz)
    prs = (_head_params(w['ssd_dt_bias_fwd'], w['ssd_dt_bias_bwd']),
           _head_params(w['ssd_a_log_fwd'], w['ssd_a_log_bwd']),
           _head_params(w['ssd_d'], jnp.zeros_like(w['ssd_d'])))
    act3 = act.reshape(bsz, seq, XBC_W)
    y_scan, ssd_states = ssd_scan_fwd(act3, dtr, prs, name="ssd_scan")
    y_scan = y_scan.reshape(t, SSD_W)
    ssd_nw = row(w['ssd_norm_w'])
    (y_ssd,) = rowmap_fwd(_ssd_post, [y_scan, z], [ssd_nw], [(SSD_W, BF16)], name="ssd_post")

    s5_names = ['s5_lambda_re_fwd', 's5_lambda_im_fwd', 's5_log_step_fwd', 's5_lambda_re_bwd', 's5_lambda_im_bwd',
                's5_log_step_bwd', 's5_b_re', 's5_b_im', 's5_c_re_fwd', 's5_c_im_fwd', 's5_c_re_bwd', 's5_c_im_bwd']
    (kt, wt, mt, da, db), s5_pull = jax.vjp(_s5_operators, *[w[n] for n in s5_names])
    tt_b, wt_b, mt_b = s5_toeplitz(kt, name="s5_toeplitz"), bf(wt), bf(mt)
    da_r, db_r = jnp.repeat(da, bsz, axis=0), jnp.repeat(db, bsz, axis=0)
    uc = to_chunks(u, name="s5_to_chunks_u")
    s_in = _to_carry(s5_state_in(uc, wt_b, name="s5_state_in"), bsz)
    hin_c = s5_carry_fwd(s_in, da_r, db_r, name="s5_carry")
    hin = _from_carry(hin_c, bsz)
    ypre = from_chunks(s5_out(uc, hin, tt_b, mt_b, name="s5_out"), name="s5_from_chunks_y", as_blocks=True)
    glu_w = w['s5_glu_w']
    s5_par = [row(w['s5_d']), _block_diag(glu_w[:, :, :S5_C]), _block_diag(glu_w[:, :, S5_C:]),
              row(w['s5_glu_b'][:, :S5_C]), row(w['s5_glu_b'][:, S5_C:]), row(w['s5_norm_w'])]
    (y_s5,) = rowmap_fwd(_s5_post, [ypre, u], s5_par, [(S5_W, BF16)], name="s5_post")

    if 'late' in w:
        w = {**w, **w['late'](y_s5)}
    w_out = bf(w['w_out']).reshape(SSD_W + S5_W, d)
    norm_ffn = row(w['norm_ffn_w'])
    h1, hn2 = matmul_sum([y_ssd, y_s5], [w_out[:SSD_W], w_out[SSD_W:]], add=x2, norm_w=norm_ffn, name="out_proj")
    pad_c = FFN_PAD - FFN_BLK
    half = N_DEV // 2
    w_up3 = jnp.pad(bf(w['ffn_w_up']), ((0, 0), (0, 0), (0, pad_c)))
    w_down = jnp.pad(bf(w['ffn_w_down']).reshape(half, FFN_BLK, d), ((0, 0), (0, pad_c), (0, 0)))
    w_down = w_down.reshape(half * FFN_PAD, d)
    fconv_w = jnp.pad(w['ffn_conv_w'], ((0, 0), (0, 0), (0, pad_c)))
    fconv_w = jnp.transpose(fconv_w, (1, 0, 2)).reshape(FCONV, N_DEV * FFN_PAD)
    fconv_b = row(jnp.pad(w['ffn_conv_b'].reshape(N_DEV, FFN_BLK), ((0, 0), (0, pad_c))))
    up = matmul_cols(hn2, w_up3, out_dtype=BF16, name="ffn_up")
    fact = ffn_act_fwd(up, fconv_w, fconv_b, bsz=bsz, name="ffn_act")
    loss, dh2, g_nf = loss_head(h1, tgt2, row(w['norm_final_w']), matmul=(fact, w_down), tm=512, name="ffn_down_loss")
    g['norm_final_w'] = g_nf.reshape(-1)

    dfact = matmul_sum([dh2], [w_down], nt=True, tm=1024, name="ffn_down_dx")
    g_down = matmul_tn(fact, dh2, name="ffn_down_dw").reshape(half, FFN_PAD, d)[:, :FFN_BLK]
    g['ffn_w_down'] = g_down.reshape(N_DEV, FFN_BLK // 2, d)
    dval, dgate, dwv, dwg, dbv, dbg = ffn_act_bwd(up, dfact, fconv_w, fconv_b, bsz=bsz, name="ffn_act_bwd")
    g_cw = jnp.concatenate([dwv, dwg], axis=1).reshape(FCONV, N_DEV, FFN_PAD)[:, :, :FFN_BLK]
    g['ffn_conv_w'] = jnp.transpose(g_cw, (1, 0, 2))
    g['ffn_conv_b'] = jnp.concatenate([dbv, dbg], axis=1).reshape(N_DEV, FFN_PAD)[:, :FFN_BLK].reshape(-1)
    windows = [(dval, FFN_PAD, p) for p in range(half)] + [(dgate, FFN_PAD, p) for p in range(half)]
    g['ffn_w_up'] = jnp.concatenate([matmul_tn(hn2, dval, out_blocks=half, name="ffn_up_dw_val"),
                                     matmul_tn(hn2, dgate, out_blocks=half, name="ffn_up_dw_gate")],
                                    axis=0)[:, :, :FFN_BLK]
    send_early = w.get('on_grads')
    if send_early:
        norm_ffn = norm_ffn + send_early(g, ['ffn_w_up', 'ffn_w_down'])
    dh1, g_nffn = matmul_sum(windows, [(w_up3, p) for p in range(N_DEV)], nt=True, tm=256,
                             norm_bwd=(h1, norm_ffn, dh2), name="ffn_up_dx")
    g['norm_ffn_w'] = g_nffn.reshape(-1)

    dycat = matmul_sum([dh1], [w_out], nt=True, tm=1024, name="out_proj_dx")
    g['w_out'] = jnp.concatenate([matmul_tn(y_ssd, dh1, name="out_proj_dw_ssd"),
                                  matmul_tn(y_s5, dh1, name="out_proj_dw_s5")], axis=0).reshape(w['w_out'].shape)
    if send_early:
        ssd_nw = ssd_nw + send_early(g, ['w_out'])
    dy_scan, dz, g_snw = rowmap_bwd(_ssd_post, [y_scan, z], [ssd_nw], [(dycat, SSD_W, 0)], name="ssd_post_bwd")
    g['ssd_norm_w'] = g_snw.reshape(-1)
    dypre, du_a, g_d, g_wv, g_wg, g_bv, g_bg, g_s5nw = rowmap_bwd(
        _s5_post, [ypre, u], s5_par, [(dycat, S5_W, SSD_W // S5_W)], name="s5_post_bwd")
    g['s5_d'], g['s5_norm_w'] = g_d.reshape(-1), g_s5nw.reshape(-1)
    g['s5_glu_w'] = jnp.concatenate([_diag_blocks(g_wv), _diag_blocks(g_wg)], axis=-1)
    g['s5_glu_b'] = jnp.concatenate([g_bv.reshape(S5_G, S5_C), g_bg.reshape(S5_G, S5_C)], axis=-1)

    dyc = to_chunks(dypre, name="s5_to_chunks_dy")
    dhin, dtt, dmt, du1 = s5_out_bwd(dyc, uc, hin, tt_b, mt_b, name="s5_out_bwd")
    ds_c, gda, gdb = s5_carry_bwd(hin_c, _to_carry(dhin, bsz), da_r, db_r, name="s5_carry_bwd")
    duc, dwt = s5_state_in_bwd(_from_carry(ds_c, bsz), uc, wt_b, du1, name="s5_state_in_bwd")
    du = from_chunks(duc, add=du_a, name="s5_from_chunks_du")
    fold = lambda v: v.reshape(S5_G, bsz, -1).sum(1)
    dkt = s5_toeplitz_bwd(dtt, name="s5_toeplitz_bwd")
    for n, gv in zip(s5_names, s5_pull((dkt, dwt, dmt, fold(gda), fold(gdb)))):
        g[n] = gv

    dxs, dbm, dcm, ddtr, gbr, gar, gdk = ssd_scan_bwd(
        act3, dtr, prs, ssd_states, dy_scan.reshape(bsz, seq, SSD_W), name="ssd_scan_bwd")
    g['ssd_dt_bias_fwd'], g['ssd_dt_bias_bwd'] = _head_grads(gbr)
    g['ssd_a_log_fwd'], g['ssd_a_log_bwd'] = _head_grads(gar)
    g['ssd_d'] = _head_grads(gdk)[0]
    dparts_act = [v.reshape(t, v.shape[-1]) for v in (dxs, dbm, dcm)]
    dxbc, g_cw, g_cb = ssd_conv_bwd(xbc, dparts_act, conv_w, conv_b, bsz=bsz, name="ssd_conv_bwd")
    g['ssd_conv_w'] = _shard_rows(g_cw, SHARDED['ssd_conv_w']).reshape(w['ssd_conv_w'].shape)
    g['ssd_conv_b'] = g_cb.reshape(-1)
    ddt = _dt_from_rows(ddtr)

    dparts = [dz, dxbc, ddt, du]
    g_in = jnp.concatenate([matmul_tn(hn, dp, name=f"in_proj_dw_{i}") for i, dp in enumerate(dparts)], axis=1)
    g['w_in'] = _shard_rows(g_in, SHARDED['w_in']).reshape(w['w_in'].shape)
    if send_early:
        dparts[2] = ddt + send_early(g, ['w_in'], loss=loss)
    dx, g_nmix = matmul_sum(dparts, w_in_parts, nt=True, tm=256, norm_bwd=(x2, norm_mix, dh1), name="in_proj_dx")
    g['norm_mix_w'] = g_nmix.reshape(-1)
    return loss, dx.reshape(bsz, seq, d), g


ANY = pl.BlockSpec(memory_space=pl.ANY)


def all_gather(shards, *, name):
    n = len(shards)

    def body(*refs):
        x_refs, out_refs = refs[:n], refs[n:2 * n]
        send_sems, recv_sems, local_sems = refs[2 * n:]
        x, y, c = lax.axis_index("x"), lax.axis_index("y"), lax.axis_index("c")
        me, sibling = (x, y, c), (x, y, 1 - c)
        chips = [(1 - x, y), (x, 1 - y), (1 - x, 1 - y)]

        def copy(k, j, block, to, own=False):
            dst = out_refs[j].at[4 * block[0] + 2 * block[1] + block[2]]
            return pltpu.make_async_remote_copy(
                src_ref=x_refs[j] if own else dst, dst_ref=dst,
                send_sem=send_sems.at[k, j], recv_sem=recv_sems.at[k, j], device_id=to, device_id_type=MESH)

        mine = [pltpu.make_async_copy(x_refs[j], out_refs[j].at[4 * x + 2 * y + c], local_sems.at[j]) for j in range(n)]
        first = [copy(0, j, me, sibling, own=True) for j in range(n)]
        first += [copy(1 + i, j, me, (*chip, c), own=True) for i, chip in enumerate(chips) for j in range(n)]
        for cp in mine + first:
            cp.start()
        passed = []
        for i, chip in enumerate(chips):
            for j in range(n):
                copy(1 + i, j, (*chip, c), me).wait_recv()
                passed.append(copy(4 + i, j, (*chip, c), sibling))
                passed[-1].start()
        for j in range(n):
            copy(0, j, sibling, me).wait_recv()
        for i, chip in enumerate(chips):
            for j in range(n):
                copy(4 + i, j, (*chip, 1 - c), me).wait_recv()
        for cp in first + passed:
            cp.wait_send()
        for cp in mine:
            cp.wait()

    return pl.pallas_call(
        body, name=name, out_shape=[jax.ShapeDtypeStruct((N_DEV,) + s.shape, s.dtype) for s in shards],
        in_specs=[ANY] * n, out_specs=[ANY] * n,
        scratch_shapes=[pltpu.SemaphoreType.DMA((7, n)), pltpu.SemaphoreType.DMA((7, n)),
                        pltpu.SemaphoreType.DMA((n,))],
    )(*shards)


HBM_SPEC = pl.BlockSpec(memory_space=pltpu.HBM)
SEM_SPEC = pl.BlockSpec(memory_space=pltpu.SEMAPHORE)
SPLIT_PARAMS = pltpu.CompilerParams(has_side_effects=pltpu.SideEffectType.DATAFLOW_SIDE_EFFECTING)


def _peer_copies(src_refs, land_refs, send_sems, recv_sems, indexed):
    x, y, c = lax.axis_index("x"), lax.axis_index("y"), lax.axis_index("c")
    me = 4 * x + 2 * y + c
    copies = []
    for k in range(1, N_DEV):
        px = (1 - x) if k & 4 else x
        py = (1 - y) if k & 2 else y
        pc = (1 - c) if k & 1 else c
        for j, (src, land) in enumerate(zip(src_refs, land_refs)):
            sem = (k - 1) * len(src_refs) + j
            copies.append(pltpu.make_async_remote_copy(
                src_ref=src.at[4 * px + 2 * py + pc] if indexed else src, dst_ref=land.at[me],
                send_sem=send_sems.at[sem], recv_sem=recv_sems.at[sem],
                device_id=(px, py, pc), device_id_type=MESH))
    return copies


def scatter_start(srcs, *, name, indexed):
    n = len(srcs)
    lands = [lax.empty(s.shape if indexed else (N_DEV,) + s.shape, s.dtype) for s in srcs]

    def body(*refs):
        send_sems, recv_sems = refs[2 * n], refs[2 * n + 1]
        for cp in _peer_copies(refs[:n], refs[n:2 * n], send_sems, recv_sems, indexed):
            cp.start()
        refs[-1][...] = jnp.zeros_like(refs[-1])

    hbm = lambda a: pltpu.HBM(a.shape, a.dtype)
    sems = pltpu.SemaphoreType.DMA(((N_DEV - 1) * n,))
    res = pl.pallas_call(
        body, name=name,
        out_shape=(sems, sems, *[hbm(a) for a in srcs + lands], jax.ShapeDtypeStruct((8, LANES), F32)),
        in_specs=[HBM_SPEC] * (2 * n),
        out_specs=(SEM_SPEC, SEM_SPEC, *[HBM_SPEC] * (2 * n), pl.BlockSpec(memory_space=pltpu.VMEM)),
        input_output_aliases={i: 2 + i for i in range(2 * n)}, compiler_params=SPLIT_PARAMS,
    )(*[pltpu.with_memory_space_constraint(a, pltpu.HBM) for a in srcs + lands])
    return res[0], res[1], list(res[2:2 + n]), list(res[2 + n:2 + 2 * n]), res[-1]


def scatter_wait(send_sems, recv_sems, srcs, lands, after, *, name, indexed):
    n = len(srcs)

    def body(*refs):
        for cp in _peer_copies(refs[:n], refs[n:2 * n], refs[2 * n], refs[2 * n + 1], indexed):
            cp.wait_send()
            cp.wait_recv()

    hbm = lambda a: pltpu.HBM(a.shape, a.dtype)
    res = pl.pallas_call(
        body, name=name, out_shape=tuple(hbm(a) for a in srcs + lands),
        in_specs=[HBM_SPEC] * (2 * n) + [SEM_SPEC, SEM_SPEC, ANY], out_specs=tuple([HBM_SPEC] * (2 * n)),
        input_output_aliases={i: i for i in range(2 * n)}, compiler_params=SPLIT_PARAMS,
    )(*srcs, *lands, send_sems, recv_sems, after)
    return list(res[:n]), list(res[n:])


def _adam_rows(r, c):
    fits = [t for t in range(8, r + 1, 8) if r % t == 0 and N_DEV * t * c * 4 <= 6 * 2 ** 20]
    return max(fits) if fits else r


def adamw(recv, w, m, v, *, name):
    _, r, n = recv.shape
    tr = _adam_rows(r, n)

    def body(r_ref, w_ref, m_ref, v_ref, g_ref, d_ref, nm_ref, nv_ref):
        g = r_ref[0].astype(F32)
        for s in range(1, N_DEV):
            g = g + r_ref[s].astype(F32)
        m_new = ADAM_B1 * m_ref[...] + (1.0 - ADAM_B1) * g
        v_new = ADAM_B2 * v_ref[...] + (1.0 - ADAM_B2) * jnp.square(g)
        m_hat = m_new / (1.0 - ADAM_B1 ** ADAM_STEP)
        v_hat = v_new / (1.0 - ADAM_B2 ** ADAM_STEP)
        g_ref[...] = g
        d_ref[...] = -ADAM_LR * (m_hat / (jnp.sqrt(v_hat) + ADAM_EPS) + ADAM_WD * w_ref[...])
        nm_ref[...] = m_new
        nv_ref[...] = v_new

    blk = pl.BlockSpec((tr, n), lambda i: (i, 0))
    return pl.pallas_call(
        body, name=name, grid=(r // tr,), in_specs=[pl.BlockSpec((N_DEV, tr, n), lambda i: (0, i, 0)), blk, blk, blk],
        out_specs=[blk] * 4, out_shape=[jax.ShapeDtypeStruct((r, n), F32)] * 4,
        compiler_params=_params(("parallel",)))(recv, w, m, v)


def _shard_rows(full, axis):
    if axis == 0:
        return full.reshape(N_DEV, -1)
    r, c = full.shape
    return jnp.transpose(full.reshape(r, N_DEV, c // N_DEV), (1, 0, 2)).reshape(N_DEV, -1)


def _unshard(blocks, axis):
    if axis == 0:
        return blocks.reshape(-1, blocks.shape[-1])
    return jnp.transpose(blocks, (1, 0, 2)).reshape(blocks.shape[1], -1)


def kernel(x, norm_mix_w, w_in, ssd_conv_w, ssd_conv_b, ssd_dt_bias_fwd, ssd_dt_bias_bwd, ssd_a_log_fwd, ssd_a_log_bwd, ssd_d, ssd_norm_w, s5_lambda_re_fwd, s5_lambda_im_fwd, s5_log_step_fwd, s5_lambda_re_bwd, s5_lambda_im_bwd, s5_log_step_bwd, s5_b_re, s5_b_im, s5_c_re_fwd, s5_c_im_fwd, s5_c_re_bwd, s5_c_im_bwd, s5_d, s5_glu_w, s5_glu_b, s5_norm_w, w_out, norm_ffn_w, ffn_w_up, ffn_conv_w, ffn_conv_b, ffn_w_down, norm_final_w, loss_target, m_norm_mix_w, m_w_in, m_ssd_conv_w, m_ssd_conv_b, m_ssd_dt_bias_fwd, m_ssd_dt_bias_bwd, m_ssd_a_log_fwd, m_ssd_a_log_bwd, m_ssd_d, m_ssd_norm_w, m_s5_lambda_re_fwd, m_s5_lambda_im_fwd, m_s5_log_step_fwd, m_s5_lambda_re_bwd, m_s5_lambda_im_bwd, m_s5_log_step_bwd, m_s5_b_re, m_s5_b_im, m_s5_c_re_fwd, m_s5_c_im_fwd, m_s5_c_re_bwd, m_s5_c_im_bwd, m_s5_d, m_s5_glu_w, m_s5_glu_b, m_s5_norm_w, m_w_out, m_norm_ffn_w, m_ffn_w_up, m_ffn_conv_w, m_ffn_conv_b, m_ffn_w_down, m_norm_final_w, v_norm_mix_w, v_w_in, v_ssd_conv_w, v_ssd_conv_b, v_ssd_dt_bias_fwd, v_ssd_dt_bias_bwd, v_ssd_a_log_fwd, v_ssd_a_log_bwd, v_ssd_d, v_ssd_norm_w, v_s5_lambda_re_fwd, v_s5_lambda_im_fwd, v_s5_log_step_fwd, v_s5_lambda_re_bwd, v_s5_lambda_im_bwd, v_s5_log_step_bwd, v_s5_b_re, v_s5_b_im, v_s5_c_re_fwd, v_s5_c_im_fwd, v_s5_c_re_bwd, v_s5_c_im_bwd, v_s5_d, v_s5_glu_w, v_s5_glu_b, v_s5_norm_w, v_w_out, v_norm_ffn_w, v_ffn_w_up, v_ffn_conv_w, v_ffn_conv_b, v_ffn_w_down, v_norm_final_w):
    args = dict(locals())
    strip = lambda n, v: v if n == 'norm_final_w' else v[0]
    w = {n: strip(n, args[n]) for n in WEIGHTS}

    mats = ['w_in', 'w_out', 'ffn_w_up', 'ffn_w_down']
    convs = ['ssd_conv_w', 'ffn_conv_w']
    shard = lambda n: w[n].astype(BF16) if n in mats else w[n]
    early, late = ['w_in', 'ssd_conv_w'], ['w_out', 'ffn_w_up', 'ffn_w_down', 'ffn_conv_w']
    full = dict(w)
    full.update(zip(early, all_gather([shard(n) for n in early], name="weight_all_gather")))
    ssem, rsem, src_thru, land_thru, token = scatter_start([shard(n) for n in late], name="weight_gather_start",
                                                           indexed=False)
    me = 4 * lax.axis_index("x") + 2 * lax.axis_index("y") + lax.axis_index("c")

    def late_weights(after):
        own, landed = scatter_wait(ssem, rsem, src_thru, land_thru, after, name="weight_gather_wait", indexed=False)
        return {n: lax.dynamic_update_index_in_dim(l, o, me, 0) for n, o, l in zip(late, own, landed)}

    full['late'], full['token'] = late_weights, token[:1, :1]

    pending = []
    last = 'norm_mix_w'
    small = convs + [n for n in WEIGHTS if n not in SHARDED and n != last]
    slot = {n: -(-w[n].size // (8 * LANES)) * 8 for n in small}
    used = sum(slot.values()) + 8
    nrow = -(-used // PACK_ROWS) * PACK_ROWS

    def tiles(v, n):
        return jnp.pad(v, ((0, 0), (0, slot[n] * LANES - v.shape[1]))).reshape(v.shape[0], slot[n], LANES)

    def send_early(grads, names, loss=None):
        srcs = [grads[n].astype(BF16) for n in names]
        if loss is not None:
            pieces = [tiles(grads[n].reshape(N_DEV, -1), n) if n in SHARDED else
                      jnp.broadcast_to(tiles(grads[n].reshape(1, -1), n), (N_DEV, slot[n], LANES)) for n in small]
            pieces.append(jnp.broadcast_to(jnp.pad(loss.reshape(1, 1, 1), ((0, 0), (0, 7), (0, LANES - 1))),
                                           (N_DEV, 8, LANES)))
            pieces.append(jnp.zeros((N_DEV, nrow - used, LANES), F32))
            srcs.append(jnp.concatenate(pieces, axis=1))
            names = names + ['small']
        started = scatter_start(srcs, name="grad_start_" + names[0], indexed=True)
        pending.append((names,) + started[:4])
        return started[4][:1, :1]

    full['on_grads'] = send_early
    loss, grad_x, g = local_step(x, loss_target, full)

    last_send = jnp.broadcast_to(g[last].reshape(1, -1, LANES), (N_DEV, g[last].size // LANES, LANES))
    last_started = scatter_start([last_send], name="grad_start_" + last, indexed=True)
    recv, outs = {}, [{}, {}, {}, {}]

    def arrived(names, started, after):
        own, landed = scatter_wait(*started, after, name="grad_wait_" + names[0], indexed=True)
        for n, o, l in zip(names, own, landed):
            recv[n] = lax.dynamic_update_index_in_dim(l, lax.dynamic_index_in_dim(o, me, 0, keepdims=False), me, 0)

    def update(n):
        shape = recv[n].shape[1:]
        res = adamw(recv[n], *[strip(n, args[p + n]).reshape(shape) for p in ('', 'm_', 'v_')], name="adamw_" + n)
        for o, p in zip(outs, res):
            o[n] = p.reshape(args[n].shape)

    for names, *started in pending:
        arrived(names, started, last_started[4])
    for n in mats:
        update(n)

    def pack(prefix):
        vals = [tiles(strip(n, args[prefix + n]).reshape(1, -1), n)[0] for n in small]
        return jnp.concatenate(vals + [jnp.zeros((nrow - used + 8, LANES), F32)], axis=0)

    packed = adamw(recv['small'], pack(''), pack('m_'), pack('v_'), name="adamw_small")
    arrived([last], last_started[:4], packed[1])
    update(last)
    off = 0
    for n in small:
        for o, p in zip(outs, packed):
            o[n] = p[off:off + slot[n]].reshape(-1)[:w[n].size].reshape(args[n].shape)
        off += slot[n]
    loss_out = packed[0][off, 0].reshape(())
    return (loss_out, grad_x, *[o[n] for o in outs for n in WEIGHTS])
```

```python
import functools

import jax
import jax.numpy as jnp
from jax import lax
from jax.experimental import pallas as pl
from jax.experimental.pallas import tpu as pltpu

F32, BF16 = jnp.float32, jnp.bfloat16
N_DEV = 8
D_MODEL = 1024
SSD_W, HEADS, HDIM, SGROUPS, HPG, NSTATE, SCONV, QC = 1024, 16, 64, 4, 4, 128, 5, 128
XBC_W = SSD_W + 2 * SGROUPS * NSTATE
S5_W, S5_G, S5_C, S5_P, S5_Q = 512, 32, 16, 64, 16
S5_QC = S5_Q * S5_C
CARRY_ROWS = 32
DFF, FCONV = 2816, 3
FFN_BLK, FFN_PAD = 704, 768
EPS = 1e-6
ADAM_LR, ADAM_B1, ADAM_B2, ADAM_EPS, ADAM_WD, ADAM_STEP = 0.001, 0.9, 0.999, 1e-08, 0.01, 10
LANES = 128
MESH = pl.DeviceIdType.MESH

WEIGHTS = ['norm_mix_w', 'w_in', 'ssd_conv_w', 'ssd_conv_b', 'ssd_dt_bias_fwd', 'ssd_dt_bias_bwd', 'ssd_a_log_fwd',
           'ssd_a_log_bwd', 'ssd_d', 'ssd_norm_w', 's5_lambda_re_fwd', 's5_lambda_im_fwd', 's5_log_step_fwd',
           's5_lambda_re_bwd', 's5_lambda_im_bwd', 's5_log_step_bwd', 's5_b_re', 's5_b_im', 's5_c_re_fwd', 's5_c_im_fwd',
           's5_c_re_bwd', 's5_c_im_bwd', 's5_d', 's5_glu_w', 's5_glu_b', 's5_norm_w', 'w_out', 'norm_ffn_w', 'ffn_w_up',
           'ffn_conv_w', 'ffn_conv_b', 'ffn_w_down', 'norm_final_w']
SHARDED = {'w_in': 1, 'ssd_conv_w': 1, 'w_out': 0, 'ffn_w_up': 1, 'ffn_conv_w': 1, 'ffn_w_down': 0}
FULL_SHAPE = {'w_in': (1024, 3616), 'ssd_conv_w': (5, 2048), 'w_out': (1536, 1024), 'ffn_w_up': (1024, 5632),
              'ffn_conv_w': (3, 5632), 'ffn_w_down': (2816, 1024)}
PACK_ROWS = 512


def _pick(n, cap=1536):
    if n <= cap:
        return n
    return max(t for t in range(LANES, cap + 1, LANES) if n % t == 0)


def _params(sem):
    return pltpu.CompilerParams(dimension_semantics=sem)


def _bd(a, b, ca, cb):
    return lax.dot_general(a.astype(BF16), b.astype(BF16), (((ca,), (cb,)), ((), ())), preferred_element_type=F32)


@jax.custom_vjp
def dot_nn(a, b):
    return _bd(a, b, 1, 0)


dot_nn.defvjp(lambda a, b: (_bd(a, b, 1, 0), (a, b)),
              lambda r, g: (_bd(g, r[1], 1, 1).astype(r[0].dtype), _bd(r[0], g, 0, 0).astype(r[1].dtype)))


@jax.custom_vjp
def dot_nt(a, b):
    return _bd(a, b, 1, 1)


dot_nt.defvjp(lambda a, b: (_bd(a, b, 1, 1), (a, b)),
              lambda r, g: (_bd(g, r[1], 1, 0).astype(r[0].dtype), _bd(g, r[0], 0, 0).astype(r[1].dtype)))


@jax.custom_vjp
def dot_tn(a, b):
    return _bd(a, b, 0, 0)


dot_tn.defvjp(lambda a, b: (_bd(a, b, 0, 0), (a, b)),
              lambda r, g: (_bd(r[1], g, 1, 1).astype(r[0].dtype), _bd(r[0], g, 1, 0).astype(r[1].dtype)))


def _rows2(v):
    h = v.shape[0] // 2
    return v[:h], v[h:]


def _cols2(v):
    h = v.shape[1] // 2
    return v[:, :h], v[:, h:]


@jax.custom_vjp
def dot2_nn(la, lb, x):
    return _rows2(_bd(jnp.concatenate([la, lb], axis=0), x, 1, 0))


def _dot2_nn_bwd(res, g):
    la, lb, x = res
    gcat, lcat = jnp.concatenate(g, axis=0), jnp.concatenate([la, lb], axis=0)
    return (*_rows2(_bd(gcat, x, 1, 1)), _bd(lcat, gcat, 0, 0))


dot2_nn.defvjp(lambda la, lb, x: (dot2_nn(la, lb, x), (la, lb, x)), _dot2_nn_bwd)


@jax.custom_vjp
def dot_nt2(c, p0, p1):
    return _cols2(_bd(c, jnp.concatenate([p0, p1], axis=0), 1, 1))


def _dot_nt2_bwd(res, g):
    c, p0, p1 = res
    gcat = jnp.concatenate(g, axis=1)
    return (_bd(gcat, jnp.concatenate([p0, p1], axis=0), 1, 0), *_rows2(_bd(gcat, c, 0, 0)))


dot_nt2.defvjp(lambda c, p0, p1: (dot_nt2(c, p0, p1), (c, p0, p1)), _dot_nt2_bwd)


@jax.custom_vjp
def dot_tn2(a0, a1, b):
    return _rows2(_bd(jnp.concatenate([a0, a1], axis=1), b, 0, 0))


def _dot_tn2_bwd(res, g):
    a0, a1, b = res
    gcat, acat = jnp.concatenate(g, axis=0), jnp.concatenate([a0, a1], axis=1)
    return (*_cols2(_bd(b, gcat, 1, 1)), _bd(acat, gcat, 1, 0))


dot_tn2.defvjp(lambda a0, a1, b: (dot_tn2(a0, a1, b), (a0, a1, b)), _dot_tn2_bwd)


def _split3(x):
    hi = x.astype(BF16)
    r = x - hi.astype(F32)
    mid = r.astype(BF16)
    lo = (r - mid.astype(F32)).astype(BF16)
    return hi, mid, lo


def _cum_matrix(q, upper):
    ri = lax.broadcasted_iota(jnp.int32, (q, q), 0)
    ci = lax.broadcasted_iota(jnp.int32, (q, q), 1)
    return jnp.where((ci >= ri) if upper else (ci <= ri), 1.0, 0.0).astype(BF16)


def _exact_right(x, mat):
    return sum(jnp.dot(p, mat, preferred_element_type=F32) for p in _split3(x))


@functools.partial(jax.custom_vjp, nondiff_argnums=(1,))
def cum_row(x, rev):
    return _exact_right(x, _cum_matrix(x.shape[1], not rev))


cum_row.defvjp(lambda x, rev: (cum_row(x, rev), None),
               lambda rev, _, g: (_exact_right(g, _cum_matrix(g.shape[1], rev)),))


def _softplus(x):
    return jnp.maximum(x, 0.0) + jnp.log(1.0 + jnp.exp(-jnp.abs(x)))


def _silu(x):
    return x * jax.nn.sigmoid(x)


def _gelu(x):
    return 0.5 * x * (1.0 + jnp.tanh(0.7978845608028654 * (x + 0.044715 * (x * x * x))))


def _rms(x, w):
    xf = x.astype(F32)
    return xf * lax.rsqrt(jnp.mean(xf * xf, axis=-1, keepdims=True) + EPS) * w


def matmul_sum(a_list, b_list, *, name, out_dtype=F32, add=None, tm=512, nt=False, norm_w=None, norm_bwd=None):
    a_arrs = [a[0] if isinstance(a, tuple) else a for a in a_list]
    b_arrs = [b[0] if isinstance(b, tuple) else b for b in b_list]
    m, n = a_arrs[0].shape[0], b_arrs[0].shape[-2 if nt else -1]
    tm, tn, k = min(tm, m), _pick(n), len(a_list)
    assert (norm_w is None and norm_bwd is None) or tn == n

    def body(*refs):
        acc = None
        for a_ref, b_ref in zip(refs[:k], refs[k:2 * k]):
            p = _bd(a_ref[...], b_ref[...], 1, 1 if nt else 0)
            acc = p if acc is None else acc + p
        if add is not None:
            acc = acc + refs[2 * k][...]
        if norm_bwd is not None:
            x_ref, w_ref, res_ref, dx_ref, dw_ref = refs[-5:]
            dx, dw = jax.vjp(_rms, x_ref[...], w_ref[...])[1](acc)
            dx_ref[...] = dx + res_ref[...]

            @pl.when(pl.program_id(0) == 0)
            def _():
                dw_ref[...] = jnp.zeros_like(dw_ref)

            dw_ref[...] += dw
        elif norm_w is not None:
            refs[-2][...] = acc.astype(out_dtype)
            refs[-1][...] = _rms(acc, refs[-3][...]).astype(BF16)
        else:
            refs[-1][...] = acc.astype(out_dtype)

    def a_spec(a):
        if isinstance(a, tuple):
            return pl.BlockSpec((tm, a[1]), lambda i, j, blk=a[2]: (i, blk))
        return pl.BlockSpec((tm, a.shape[1]), lambda i, j: (i, 0))

    def b_spec(b):
        arr, p = b if isinstance(b, tuple) else (b, None)
        kk = arr.shape[-1 if nt else -2]
        shape, idx = ((tn, kk), lambda j: (j, 0)) if nt else ((kk, tn), lambda j: (0, j))
        if p is None:
            return pl.BlockSpec(shape, lambda i, j: idx(j))
        return pl.BlockSpec((None,) + shape, lambda i, j, p=p: (p,) + idx(j))

    in_specs = [a_spec(a) for a in a_list] + [b_spec(b) for b in b_list]
    args = a_arrs + b_arrs
    if add is not None:
        in_specs.append(pl.BlockSpec((tm, tn), lambda i, j: (i, j)))
        args.append(add)
    out_spec, out_shape = pl.BlockSpec((tm, tn), lambda i, j: (i, j)), jax.ShapeDtypeStruct((m, n), out_dtype)
    if norm_w is not None:
        in_specs.append(pl.BlockSpec(norm_w.shape, lambda i, j: (0, 0)))
        args.append(norm_w)
        out_spec, out_shape = [out_spec, out_spec], [out_shape, jax.ShapeDtypeStruct((m, n), BF16)]
    sem = ("parallel", "parallel")
    if norm_bwd is not None:
        x, w, res = norm_bwd
        wspec = pl.BlockSpec(w.shape, lambda i, j: (0, 0))
        in_specs += [out_spec, wspec, out_spec]
        args += [x, w, res]
        out_spec, out_shape = [out_spec, wspec], [jax.ShapeDtypeStruct((m, n), F32), jax.ShapeDtypeStruct(w.shape, F32)]
        sem = ("arbitrary", "arbitrary")
    return pl.pallas_call(
        body, name=name, grid=(m // tm, n // tn), in_specs=in_specs, out_specs=out_spec, out_shape=out_shape,
        compiler_params=_params(sem))(*args)


def matmul_cols(a, b3, *, name, out_dtype=F32, tm=1024):
    m, kk = a.shape
    p, _, nb = b3.shape
    tm, tn = min(tm, m), _pick(nb, 768)
    per = nb // tn

    def body(a_ref, b_ref, o_ref):
        o_ref[...] = _bd(a_ref[...], b_ref[...], 1, 0).astype(out_dtype)

    return pl.pallas_call(
        body, name=name, grid=(m // tm, p * per),
        in_specs=[pl.BlockSpec((tm, kk), lambda i, j: (i, 0)),
                  pl.BlockSpec((None, kk, tn), lambda i, j: (j // per, 0, j % per))],
        out_specs=pl.BlockSpec((tm, tn), lambda i, j: (i, j)),
        out_shape=jax.ShapeDtypeStruct((m, p * nb), out_dtype),
        compiler_params=_params(("parallel", "parallel")))(a, b3)


def matmul_tn(a, b, *, name, tm=1024, out_blocks=None):
    m, k = a.shape
    n = b.shape[1]
    nb = n // (out_blocks or 1)
    tm, tk, tn = min(tm, m), _pick(k), _pick(nb, 768 if out_blocks else 1536)
    per = nb // tn

    def body(a_ref, b_ref, o_ref):
        @pl.when(pl.program_id(2) == 0)
        def _():
            o_ref[...] = jnp.zeros_like(o_ref)

        o_ref[...] += _bd(a_ref[...], b_ref[...], 0, 0)

    if out_blocks:
        out_spec = pl.BlockSpec((None, tk, tn), lambda i, j, t: (j // per, i, j % per))
        out_shape = jax.ShapeDtypeStruct((out_blocks, k, nb), F32)
    else:
        out_spec = pl.BlockSpec((tk, tn), lambda i, j, t: (i, j))
        out_shape = jax.ShapeDtypeStruct((k, n), F32)
    return pl.pallas_call(
        body, name=name, grid=(k // tk, n // tn, m // tm),
        in_specs=[pl.BlockSpec((tm, tk), lambda i, j, t: (t, i)), pl.BlockSpec((tm, tn), lambda i, j, t: (t, j))],
        out_specs=out_spec, out_shape=out_shape,
        compiler_params=_params(("parallel", "parallel", "arbitrary")))(a, b)


def _row_spec(r, tm):
    if isinstance(r, tuple):
        arr, width, blk = r
        return arr, pl.BlockSpec((tm, width), lambda i, blk=blk: (i, blk))
    return r, pl.BlockSpec((tm, r.shape[1]), lambda i: (i, 0))


def _full_spec(p):
    return pl.BlockSpec(p.shape, lambda i: (0,) * p.ndim)


def _expand_rows(rows, tm):
    arrays, specs, counts, widths = [], [], [], []
    for r in rows:
        parts = [_row_spec(p, tm) for p in (r if isinstance(r, list) else [r])]
        arrays += [a for a, _ in parts]
        specs += [s for _, s in parts]
        counts.append(len(parts))
        widths.append(sum(s.block_shape[1] for _, s in parts))
    return arrays, specs, counts, widths


def _row_values(refs, counts):
    vals, k = [], 0
    for c in counts:
        parts = [refs[k + j][...] for j in range(c)]
        vals.append(parts[0] if c == 1 else jnp.concatenate(parts, axis=1))
        k += c
    return vals


def _rows_of(rows):
    first = rows[0][0] if isinstance(rows[0], list) else rows[0]
    return (first[0] if isinstance(first, tuple) else first).shape[0]


def rowmap_fwd(fn, rows, params, outs, *, name, tm=256):
    m = _rows_of(rows)
    tm = min(tm, m)
    arrays, specs, counts, _ = _expand_rows(rows, tm)
    nin, npar = len(arrays), len(params)

    def body(*refs):
        res = fn(*_row_values(refs[:nin], counts), *[r[...] for r in refs[nin:nin + npar]])
        for o_ref, v in zip(refs[nin + npar:], res):
            o_ref[...] = v.astype(o_ref.dtype)

    return pl.pallas_call(
        body, name=name, grid=(m // tm,), in_specs=specs + [_full_spec(p) for p in params],
        out_specs=[pl.BlockSpec((tm, c), lambda i: (i, 0)) for c, _ in outs],
        out_shape=[jax.ShapeDtypeStruct((m, c), dt) for c, dt in outs],
        compiler_params=_params(("parallel",)))(*arrays, *params)


def rowmap_bwd(fn, rows, params, cts, *, name, row_dtypes=None, add=None, tm=256):
    m = _rows_of(rows)
    tm = min(tm, m)
    arrays, specs, counts, widths = _expand_rows(rows, tm)
    cp = [_row_spec(c, tm) for c in cts]
    nin, nr, npar, nc = len(arrays), len(rows), len(params), len(cts)
    row_dtypes = row_dtypes or [F32] * nr

    def body(*refs):
        ins = _row_values(refs[:nin], counts) + [r[...] for r in refs[nin:nin + npar]]
        ins = [v.astype(F32) for v in ins]
        ct = tuple(r[...].astype(F32) for r in refs[nin + npar:nin + npar + nc])
        base = nin + npar + nc
        extra = None
        if add is not None:
            extra = refs[base][...]
            base += 1
        _, pull = jax.vjp(fn, *ins)
        grads = pull(ct)
        for j in range(nr):
            g = grads[j]
            if j == 0 and extra is not None:
                g = g + extra
            refs[base + j][...] = g.astype(refs[base + j].dtype)

        @pl.when(pl.program_id(0) == 0)
        def _():
            for j in range(npar):
                refs[base + nr + j][...] = jnp.zeros_like(refs[base + nr + j])

        for j in range(npar):
            refs[base + nr + j][...] += grads[nr + j]

    in_specs = specs + [_full_spec(p) for p in params] + [s for _, s in cp]
    args = arrays + list(params) + [a for a, _ in cp]
    if add is not None:
        in_specs.append(pl.BlockSpec((tm, widths[0]), lambda i: (i, 0)))
        args.append(add)
    out_specs = [pl.BlockSpec((tm, w), lambda i: (i, 0)) for w in widths] + [_full_spec(p) for p in params]
    out_shape = [jax.ShapeDtypeStruct((m, w), dt) for w, dt in zip(widths, row_dtypes)]
    out_shape += [jax.ShapeDtypeStruct(p.shape, F32) for p in params]
    return pl.pallas_call(
        body, name=name, grid=(m // tm,), in_specs=in_specs, out_specs=out_specs, out_shape=out_shape,
        compiler_params=_params(("arbitrary",)))(*args)


def loss_head(h, target, w, *, name, tm=256, matmul=None):
    m, d = h.shape
    tm = min(tm, m)

    def body(h_ref, t_ref, w_ref, *refs):
        loss_ref, dh_ref, dw_ref = refs[-3:]
        rows = h_ref[...]
        if matmul is not None:
            rows = rows + _bd(refs[0][...], refs[1][...], 1, 0)
        y, pull = jax.vjp(_rms, rows, w_ref[...])
        err = y - t_ref[...]
        dh, dw = pull(err * (1.0 / d))

        @pl.when(pl.program_id(0) == 0)
        def _():
            loss_ref[...] = jnp.zeros_like(loss_ref)
            dw_ref[...] = jnp.zeros_like(dw_ref)

        loss_ref[...] += (0.5 / d) * jnp.sum(err * err, keepdims=True)
        dw_ref[...] += dw
        dh_ref[...] = dh

    row = pl.BlockSpec((tm, d), lambda i: (i, 0))
    in_specs, args = [row, row, _full_spec(w)], [h, target, w]
    if matmul is not None:
        in_specs += [pl.BlockSpec((tm, matmul[0].shape[1]), lambda i: (i, 0)), _full_spec(matmul[1])]
        args += list(matmul)
    return pl.pallas_call(
        body, name=name, grid=(m // tm,), in_specs=in_specs,
        out_specs=[pl.BlockSpec((1, 1), lambda i: (0, 0)), row, _full_spec(w)],
        out_shape=[jax.ShapeDtypeStruct((1, 1), F32), jax.ShapeDtypeStruct((m, d), F32),
                   jax.ShapeDtypeStruct(w.shape, F32)],
        compiler_params=_params(("arbitrary",)))(*args)


def _shift(x, s):
    if s == 0:
        return x
    n = x.shape[0]
    t = lax.broadcasted_iota(jnp.int32, x.shape, 0)
    rolled = pltpu.roll(x, (-s) % n, 0)
    return jnp.where((t + s >= 0) & (t + s < n), rolled, 0.0)


def _conv(x, w, b):
    k = w.shape[0]
    acc = b + w[k // 2:k // 2 + 1, :] * x
    for j in range(k):
        if j != k // 2:
            acc = acc + w[j:j + 1, :] * _shift(x, j - k // 2)
    return acc


def _conv_bwd(x, dc, w):
    k = w.shape[0]
    dx = None
    dws = []
    for j in range(k):
        s = j - k // 2
        term = w[j:j + 1, :] * _shift(dc, -s)
        dx = term if dx is None else dx + term
        dws.append(jnp.sum(dc * _shift(x, s), axis=0, keepdims=True))
    return dx, jnp.concatenate(dws, axis=0), jnp.sum(dc, axis=0, keepdims=True)


def _dsilu(c):
    s = jax.nn.sigmoid(c)
    return s * (1.0 + c * (1.0 - s))


def ssd_conv_fwd(xbc, w, b, *, bsz, name):
    t, c = xbc.shape
    seq, ct = t // bsz, 256

    def body(x_ref, w_ref, b_ref, o_ref):
        o_ref[...] = _silu(_conv(x_ref[...], w_ref[...], b_ref[...]))

    return pl.pallas_call(
        body, name=name, grid=(c // ct, bsz),
        in_specs=[pl.BlockSpec((seq, ct), lambda j, i: (i, j)), pl.BlockSpec((w.shape[0], ct), lambda j, i: (0, j)),
                  pl.BlockSpec((1, ct), lambda j, i: (0, j))],
        out_specs=pl.BlockSpec((seq, ct), lambda j, i: (i, j)),
        out_shape=jax.ShapeDtypeStruct((t, c), F32),
        compiler_params=_params(("parallel", "parallel")))(xbc, w, b)


def ssd_conv_bwd(xbc, dparts, w, b, *, bsz, name):
    t, c = xbc.shape
    seq, ct, k = t // bsz, 256, w.shape[0]
    starts = [0]
    for p in dparts:
        starts.append(starts[-1] + p.shape[1] // ct)

    def body(x_ref, *refs):
        g_refs, (w_ref, b_ref, dx_ref, dw_ref, db_ref) = refs[:len(dparts)], refs[len(dparts):]
        j = pl.program_id(0)
        g = g_refs[-1][...]
        for n in range(len(dparts) - 2, -1, -1):
            g = jnp.where(j < starts[n + 1], g_refs[n][...], g)
        x, wv = x_ref[...], w_ref[...]
        dc = g * _dsilu(_conv(x, wv, b_ref[...]))
        dx, dw, db = _conv_bwd(x, dc, wv)
        dx_ref[...] = dx

        @pl.when(pl.program_id(1) == 0)
        def _():
            dw_ref[...] = jnp.zeros_like(dw_ref)
            db_ref[...] = jnp.zeros_like(db_ref)

        dw_ref[...] += dw
        db_ref[...] += db

    def part_spec(n):
        lo, hi = starts[n], starts[n + 1]

        def index(j, i):
            inside = (j >= lo) & (j < hi)
            return jnp.where(inside, i, 0), jnp.where(inside, j - lo, 0)

        return pl.BlockSpec((seq, ct), index)

    blk = pl.BlockSpec((seq, ct), lambda j, i: (i, j))
    wspec, bspec = pl.BlockSpec((k, ct), lambda j, i: (0, j)), pl.BlockSpec((1, ct), lambda j, i: (0, j))
    return pl.pallas_call(
        body, name=name, grid=(c // ct, bsz),
        in_specs=[blk] + [part_spec(n) for n in range(len(dparts))] + [wspec, bspec], out_specs=[blk, wspec, bspec],
        out_shape=[jax.ShapeDtypeStruct((t, c), F32), jax.ShapeDtypeStruct((k, c), F32),
                   jax.ShapeDtypeStruct((1, c), F32)],
        compiler_params=_params(("parallel", "arbitrary")))(xbc, *dparts, w, b)


def _ffn_specs(seq, ct, k, nblk):
    val = pl.BlockSpec((seq, ct), lambda j, i: (i, j))
    gate = pl.BlockSpec((seq, ct), lambda j, i: (i, nblk + j))
    wv, wg = pl.BlockSpec((k, ct), lambda j, i: (0, j)), pl.BlockSpec((k, ct), lambda j, i: (0, nblk + j))
    bv, bg = pl.BlockSpec((1, ct), lambda j, i: (0, j)), pl.BlockSpec((1, ct), lambda j, i: (0, nblk + j))
    return val, gate, wv, wg, bv, bg


def ffn_act_fwd(up, w, b, *, bsz, name):
    t = up.shape[0]
    half = up.shape[1] // 2
    seq, ct, k = t // bsz, 256, w.shape[0]
    val, gate, wv, wg, bv, bg = _ffn_specs(seq, ct, k, half // ct)

    def body(v_ref, g_ref, wv_ref, wg_ref, bv_ref, bg_ref, o_ref):
        vc = _conv(v_ref[...].astype(F32), wv_ref[...], bv_ref[...])
        gc = _conv(g_ref[...].astype(F32), wg_ref[...], bg_ref[...])
        o_ref[...] = (_silu(gc) * vc).astype(BF16)

    return pl.pallas_call(
        body, name=name, grid=(half // ct, bsz), in_specs=[val, gate, wv, wg, bv, bg], out_specs=val,
        out_shape=jax.ShapeDtypeStruct((t, half), BF16),
        compiler_params=_params(("parallel", "parallel")))(up, up, w, w, b, b)


def ffn_act_bwd(up, dact, w, b, *, bsz, name):
    t = up.shape[0]
    half = up.shape[1] // 2
    seq, ct, k = t // bsz, 256, w.shape[0]
    val, gate, wv, wg, bv, bg = _ffn_specs(seq, ct, k, half // ct)

    def body(v_ref, g_ref, wv_ref, wg_ref, bv_ref, bg_ref, d_ref, dv_ref, dg_ref, dwv_ref, dwg_ref, dbv_ref, dbg_ref):
        v, g = v_ref[...].astype(F32), g_ref[...].astype(F32)
        vc = _conv(v, wv_ref[...], bv_ref[...])
        gc = _conv(g, wg_ref[...], bg_ref[...])
        d = d_ref[...].astype(F32)
        sg = jax.nn.sigmoid(gc)
        dv, dwv, dbv = _conv_bwd(v, d * (gc * sg), wv_ref[...])
        dg, dwg, dbg = _conv_bwd(g, d * vc * (sg * (1.0 + gc * (1.0 - sg))), wg_ref[...])
        dv_ref[...] = dv.astype(BF16)
        dg_ref[...] = dg.astype(BF16)

        @pl.when(pl.program_id(1) == 0)
        def _():
            for r in (dwv_ref, dwg_ref, dbv_ref, dbg_ref):
                r[...] = jnp.zeros_like(r)

        dwv_ref[...] += dwv
        dwg_ref[...] += dwg
        dbv_ref[...] += dbv
        dbg_ref[...] += dbg

    return pl.pallas_call(
        body, name=name, grid=(half // ct, bsz), in_specs=[val, gate, wv, wg, bv, bg, val],
        out_specs=[val, val, wv, wv, bv, bv],
        out_shape=[jax.ShapeDtypeStruct((t, half), BF16), jax.ShapeDtypeStruct((t, half), BF16),
                   jax.ShapeDtypeStruct((k, half), F32), jax.ShapeDtypeStruct((k, half), F32),
                   jax.ShapeDtypeStruct((1, half), F32), jax.ShapeDtypeStruct((1, half), F32)],
        compiler_params=_params(("parallel", "arbitrary")))(up, up, w, w, b, b, dact)


def _sel_row(a, h):
    oh = (lax.broadcasted_iota(jnp.int32, (a.shape[0], 1), 0) == h).astype(F32)
    return jnp.sum(a * oh, axis=0, keepdims=True)


def _ssd_chunk(xp, dtr, bm, cm, prev, bias_r, alog_r, dskip_r, rev):
    q = dtr.shape[1]
    ri = lax.broadcasted_iota(jnp.int32, (q, q), 0)
    ci = lax.broadcasted_iota(jnp.int32, (q, q), 1)
    mask = (ci >= ri) if rev else (ci <= ri)
    lane_lo, row_lo = ci < HDIM, ri < HDIM
    dt_r = _softplus(dtr + bias_r)
    dta_r = dt_r * (-jnp.exp(alog_r))
    cs_r = cum_row(dta_r, rev)
    scores = dot_nt(cm, bm)

    def per_row(v):
        return jnp.broadcast_to(v, (q, q)).T

    assert len(xp) == 2
    y_diag, csqs, decayed, tots = [], [], [], []
    for p in range(2):
        ha = 2 * p + (HPG if rev else 0)
        hb = ha + 1
        cs_a, cs_b = _sel_row(cs_r, ha), _sel_row(cs_r, hb)
        csq_a, csq_b = per_row(cs_a), per_row(cs_b)
        seg_a = jnp.exp(jnp.where(mask, csq_a - cs_a, -1e30))
        seg_b = jnp.exp(jnp.where(mask, csq_b - cs_b, -1e30))
        csq = jnp.where(lane_lo, csq_a, csq_b)
        xdt = xp[p] * jnp.where(lane_lo, per_row(_sel_row(dt_r, ha)), per_row(_sel_row(dt_r, hb)))
        tot_a = jnp.sum(_sel_row(dta_r, ha), axis=1, keepdims=True)
        tot_b = jnp.sum(_sel_row(dta_r, hb), axis=1, keepdims=True)
        y_diag.append(jnp.where(lane_lo, *dot2_nn(scores * seg_a, scores * seg_b, xdt)))
        csqs.append(csq)
        decayed.append(xdt * jnp.exp(jnp.where(lane_lo, tot_a, tot_b) - csq))
        tots.append((tot_a, tot_b, ha, hb))
    y_off = dot_nt2(cm, *prev)
    states = dot_tn2(*decayed, bm)
    ys, news = [], []
    for p, (tot_a, tot_b, ha, hb) in enumerate(tots):
        y = y_diag[p] + y_off[p] * jnp.exp(csqs[p])
        if not rev:
            y = y + jnp.where(lane_lo, _sel_row(dskip_r, ha), _sel_row(dskip_r, hb)) * xp[p]
        ys.append(y)
        news.append(jnp.exp(jnp.where(row_lo, tot_a, tot_b)) * prev[p] + states[p])
    return tuple(ys), tuple(news)


NPAIR = HPG // 2


def _ssd_specs(seq, nc):
    xs = pl.BlockSpec((None, seq, HPG * HDIM), lambda b, g: (b, 0, g))
    bm = pl.BlockSpec((None, seq, NSTATE), lambda b, g: (b, 0, SSD_W // NSTATE + g))
    cm = pl.BlockSpec((None, seq, NSTATE), lambda b, g: (b, 0, SSD_W // NSTATE + SGROUPS + g))
    dtr = pl.BlockSpec((None, None, 2 * HPG, seq), lambda b, g: (b, g, 0, 0))
    pr = pl.BlockSpec((None, 2 * HPG, 1), lambda b, g: (g, 0, 0))
    st = pl.BlockSpec((None, None, 2, nc, NPAIR, 2 * HDIM, NSTATE), lambda b, g: (b, g, 0, 0, 0, 0, 0))
    return xs, bm, cm, dtr, pr, st


def _pair_cols(p):
    return slice(2 * HDIM * p, 2 * HDIM * (p + 1))


def ssd_scan_fwd(act, dtr, prs, *, name):
    bsz, seq, _ = act.shape
    nc = seq // QC
    xs, bm, cm, dtrs, pr, st = _ssd_specs(seq, nc)

    def body(x_ref, b_ref, c_ref, dtr_ref, br_ref, ar_ref, dk_ref, y_ref, st_ref):
        par = (br_ref[...], ar_ref[...], dk_ref[...])
        y_ref[...] = jnp.zeros_like(y_ref)

        def step(i, carry):
            new = []
            for rev in (False, True):
                k = (nc - 1 - i) if rev else i
                rows = pl.ds(pl.multiple_of(k * QC, QC), QC)
                xp = tuple(x_ref[rows, _pair_cols(p)] for p in range(NPAIR))
                for p in range(NPAIR):
                    st_ref[int(rev), k, p] = carry[rev][p]
                ys, nw = _ssd_chunk(xp, dtr_ref[:, rows], b_ref[rows, :], c_ref[rows, :], carry[rev], *par, rev)
                for p in range(NPAIR):
                    y_ref[rows, _pair_cols(p)] += ys[p]
                new.append(nw)
            return tuple(new)

        zero = tuple(jnp.zeros((2 * HDIM, NSTATE), F32) for _ in range(NPAIR))
        lax.fori_loop(0, nc // 2, lambda i, c: step(2 * i + 1, step(2 * i, c)), (zero, zero))

    return pl.pallas_call(
        body, name=name, grid=(bsz, SGROUPS), in_specs=[xs, bm, cm, dtrs, pr, pr, pr], out_specs=[xs, st],
        out_shape=[jax.ShapeDtypeStruct((bsz, seq, SSD_W), F32),
                   jax.ShapeDtypeStruct((bsz, SGROUPS, 2, nc, NPAIR, 2 * HDIM, NSTATE), F32)],
        compiler_params=_params(("parallel", "parallel")))(act, act, act, dtr, *prs)


def ssd_scan_bwd(act, dtr, prs, states, dy, *, name):
    bsz, seq, _ = act.shape
    nc = seq // QC
    xs, bm, cm, dtrs, pr, st = _ssd_specs(seq, nc)
    grp = pl.BlockSpec((None, seq, NSTATE), lambda b, g: (b, 0, g))
    dpr = pl.BlockSpec((None, None, 2 * HPG, 1), lambda b, g: (b, g, 0, 0))

    def body(x_ref, b_ref, c_ref, dtr_ref, br_ref, ar_ref, dk_ref, st_ref, dy_ref,
             dx_ref, db_ref, dc_ref, ddtr_ref, gbr_ref, gar_ref, gdk_ref):
        par = (br_ref[...], ar_ref[...], dk_ref[...])
        pgrads = (gbr_ref, gar_ref, gdk_ref)
        for r in pgrads + (dx_ref, db_ref, dc_ref, ddtr_ref):
            r[...] = jnp.zeros_like(r)

        def bstep(i, dcarry):
            new = []
            for rev in (False, True):
                k = i if rev else (nc - 1 - i)
                rows = pl.ds(pl.multiple_of(k * QC, QC), QC)
                xp = tuple(x_ref[rows, _pair_cols(p)] for p in range(NPAIR))
                prev = tuple(st_ref[int(rev), k, p] for p in range(NPAIR))
                _, pull = jax.vjp(functools.partial(_ssd_chunk, rev=rev), xp, dtr_ref[:, rows], b_ref[rows, :],
                                  c_ref[rows, :], prev, *par)
                dyp = tuple(dy_ref[rows, _pair_cols(p)] for p in range(NPAIR))
                gx, gdt, gb, gc, gprev, *gpar = pull((dyp, dcarry[rev]))
                for p in range(NPAIR):
                    dx_ref[rows, _pair_cols(p)] += gx[p]
                ddtr_ref[:, rows] += gdt
                db_ref[rows, :] += gb
                dc_ref[rows, :] += gc
                for r, g in zip(pgrads, gpar):
                    r[...] += g
                new.append(gprev)
            return tuple(new)

        zero = tuple(jnp.zeros((2 * HDIM, NSTATE), F32) for _ in range(NPAIR))
        lax.fori_loop(0, nc, bstep, (zero, zero))

    out_shape = [jax.ShapeDtypeStruct((bsz, seq, SSD_W), F32),
                 jax.ShapeDtypeStruct((bsz, seq, SGROUPS * NSTATE), F32),
                 jax.ShapeDtypeStruct((bsz, seq, SGROUPS * NSTATE), F32),
                 jax.ShapeDtypeStruct(dtr.shape, F32)]
    out_shape += [jax.ShapeDtypeStruct((bsz, SGROUPS, 2 * HPG, 1), F32)] * 3
    return pl.pallas_call(
        body, name=name, grid=(bsz, SGROUPS), in_specs=[xs, bm, cm, dtrs, pr, pr, pr, st, xs],
        out_specs=[xs, grp, grp, dtrs, dpr, dpr, dpr], out_shape=out_shape,
        compiler_params=_params(("parallel", "parallel")))(act, act, act, dtr, *prs, states, dy)


def _s5_core(lam_re, lam_im, log_step, b_re, b_im, c_re, c_im):
    q = S5_Q
    step = jnp.exp(log_step)[:, None]
    lr, li = lam_re * step, lam_im * step
    mag = jnp.exp(lr)
    ar, ai = mag * jnp.cos(li), mag * jnp.sin(li)
    den = lam_re * lam_re + lam_im * lam_im
    cr = ((ar - 1.0) * lam_re + ai * lam_im) / den
    ci = (ai * lam_re - (ar - 1.0) * lam_im) / den
    bbr = cr[..., None] * b_re - ci[..., None] * b_im
    bbi = cr[..., None] * b_im + ci[..., None] * b_re
    d = jnp.arange(q + 1, dtype=F32)[None, :, None]
    pm = jnp.exp(d * lr[:, None, :])
    pr, pi = pm * jnp.cos(d * li[:, None, :]), pm * jnp.sin(d * li[:, None, :])
    er = pr[..., None] * bbr[:, None] - pi[..., None] * bbi[:, None]
    ei = pr[..., None] * bbi[:, None] + pi[..., None] * bbr[:, None]
    hp = lax.Precision.HIGHEST
    k = (jnp.einsum('gcp,gdpz->gdcz', c_re, er[:, :q], precision=hp)
         - jnp.einsum('gcp,gdpz->gdcz', c_im, ei[:, :q], precision=hp))
    e = jnp.concatenate([er[:, :q], ei[:, :q]], axis=2)
    p1r, p1i = pr[:, 1:], pi[:, 1:]
    m_re = c_re[:, None] * p1r[:, :, None, :] - c_im[:, None] * p1i[:, :, None, :]
    m_im = -c_re[:, None] * p1i[:, :, None, :] - c_im[:, None] * p1r[:, :, None, :]
    da = jnp.concatenate([pr[:, q], pr[:, q]], axis=-1)
    db = jnp.concatenate([-pi[:, q], pi[:, q]], axis=-1)
    return k, e, jnp.concatenate([m_re, m_im], axis=-1), da, db


def _s5_operators(lf_re, lf_im, lsf, lb_re, lb_im, lsb, b_re, b_im, cf_re, cf_im, cb_re, cb_im):
    g = lf_re.shape[0]
    both = lambda f, b: jnp.concatenate([f, b], axis=0)
    k, e, m, da, db = _s5_core(both(lf_re, lb_re), both(lf_im, lb_im), both(lsf, lsb), both(b_re, b_re),
                               both(b_im, b_im), both(cf_re, cb_re), both(cf_im, cb_im))
    kf, kb = k[:g], k[g:]
    wtf, wtb = jnp.transpose(e[:g, ::-1], (0, 1, 3, 2)), jnp.transpose(e[g:], (0, 1, 3, 2))
    mtf, mtb = jnp.transpose(m[:g], (0, 3, 1, 2)), jnp.transpose(m[g:, ::-1], (0, 3, 1, 2))
    daf, dab, dbf, dbb = da[:g], da[g:], db[:g], db[g:]
    lags = jnp.concatenate([kb[:, :0:-1], kf[:, :1] + kb[:, :1], kf[:, 1:]], axis=1)
    tt = jnp.transpose(lags, (0, 1, 3, 2))
    wt = jnp.concatenate([wtf.reshape(g, S5_QC, 2 * S5_P), wtb.reshape(g, S5_QC, 2 * S5_P)], axis=-1)
    mt = jnp.concatenate([mtf.reshape(g, 2 * S5_P, S5_QC), mtb.reshape(g, 2 * S5_P, S5_QC)], axis=1)
    return tt, wt, mt, jnp.concatenate([daf, dab], -1), jnp.concatenate([dbf, dbb], -1)


def _gspec(*shape):
    return pl.BlockSpec((None,) + shape, lambda g: (g,) + (0,) * len(shape))


S5_HALVES = S5_QC // LANES


def _toeplitz_block(s, t):
    per = LANES // S5_C
    return t // per, slice(s * S5_C, (s + 1) * S5_C), slice((t % per) * S5_C, (t % per + 1) * S5_C)


def s5_toeplitz(kt, *, name):
    g = kt.shape[0]

    def body(k_ref, t_ref):
        for s in range(S5_Q):
            for t in range(S5_Q):
                t_ref[_toeplitz_block(s, t)] = k_ref[t - s + S5_Q - 1]

    return pl.pallas_call(
        body, name=name, grid=(g,), in_specs=[_gspec(2 * S5_Q - 1, S5_C, S5_C)],
        out_specs=_gspec(S5_HALVES, S5_QC, LANES), out_shape=jax.ShapeDtypeStruct((g, S5_HALVES, S5_QC, LANES), F32),
        compiler_params=_params(("parallel",)))(kt)


def s5_toeplitz_bwd(dtt, *, name):
    g = dtt.shape[0]

    def body(d_ref, k_ref):
        for j in range(2 * S5_Q - 1):
            acc = None
            for s in range(S5_Q):
                t = j - (S5_Q - 1) + s
                if 0 <= t < S5_Q:
                    blk = d_ref[_toeplitz_block(s, t)]
                    acc = blk if acc is None else acc + blk
            k_ref[j] = acc

    return pl.pallas_call(
        body, name=name, grid=(g,), in_specs=[_gspec(S5_HALVES, S5_QC, LANES)],
        out_specs=_gspec(2 * S5_Q - 1, S5_C, S5_C), out_shape=jax.ShapeDtypeStruct((g, 2 * S5_Q - 1, S5_C, S5_C), F32),
        compiler_params=_params(("parallel",)))(dtt)


S5_RT = 64


def _chunk_piece(q):
    per = LANES // S5_C
    return q // per, slice((q % per) * S5_C, (q % per + 1) * S5_C)


def to_chunks(u, *, name):
    t = u.shape[0]
    r = t // S5_Q
    rt = min(S5_RT, r)

    per = LANES // S5_C
    nblk = S5_W // LANES

    def body(*refs):
        o_ref = refs[-1]
        for k in range(nblk):
            for q in range(S5_Q):
                rows = refs[k][pl.ds(q, rt, stride=S5_Q), :]
                half, lanes = _chunk_piece(q)
                for j in range(per):
                    o_ref[k * per + j, half, :, lanes] = rows[:, j * S5_C:(j + 1) * S5_C]

    return pl.pallas_call(
        body, name=name, grid=(r // rt,),
        in_specs=[pl.BlockSpec((rt * S5_Q, LANES), lambda i, k=k: (i, k)) for k in range(nblk)],
        out_specs=pl.BlockSpec((S5_G, S5_HALVES, rt, LANES), lambda i: (0, 0, i, 0)),
        out_shape=jax.ShapeDtypeStruct((S5_G, S5_HALVES, r, LANES), F32),
        compiler_params=_params(("parallel",)))(*[u] * nblk)


def from_chunks(y, *, name, add=None, as_blocks=False):
    r = y.shape[2]
    rt = min(S5_RT, r)
    per = LANES // S5_C

    nblk = S5_W // LANES

    def body(*refs):
        y_ref, tmp_ref = refs[0], refs[-1]
        adds, outs = refs[1:-1 - nblk], refs[-1 - nblk:-1]
        for k in range(nblk):
            for q in range(S5_Q):
                half, lanes = _chunk_piece(q)
                for j in range(per):
                    tmp_ref[:, j * S5_C:(j + 1) * S5_C] = y_ref[k * per + j, half, :, lanes]
                row = tmp_ref[...]
                if add is not None:
                    row = row + adds[k][pl.ds(q, rt, stride=S5_Q), :]
                outs[k][pl.ds(q, rt, stride=S5_Q), :] = row

    in_specs = [pl.BlockSpec((S5_G, S5_HALVES, rt, LANES), lambda i: (0, 0, i, 0))]
    if add is not None:
        in_specs += [pl.BlockSpec((rt * S5_Q, LANES), lambda i, k=k: (i, k)) for k in range(nblk)]
    blocks = pl.pallas_call(
        body, name=name, grid=(r // rt,), in_specs=in_specs,
        out_specs=[pl.BlockSpec((rt * S5_Q, LANES), lambda i: (i, 0))] * nblk,
        out_shape=[jax.ShapeDtypeStruct((r * S5_Q, LANES), F32)] * nblk,
        scratch_shapes=[pltpu.VMEM((rt, LANES), F32)],
        compiler_params=_params(("parallel",)))(*([y] if add is None else [y] + [add] * nblk))
    return list(blocks) if as_blocks else jnp.concatenate(blocks, axis=1)


def _cat(ref):
    return jnp.concatenate([ref[h] for h in range(S5_HALVES)], axis=1)


def _put(ref, v):
    for h in range(S5_HALVES):
        ref[h] = v[:, h * LANES:(h + 1) * LANES]


def _cspec(r):
    return _gspec(S5_HALVES, r, LANES)


def s5_state_in(u, wt, *, name):
    g, _, r, _ = u.shape

    def body(u_ref, w_ref, o_ref):
        o_ref[...] = _bd(_cat(u_ref), w_ref[...], 1, 0)

    return pl.pallas_call(
        body, name=name, grid=(g,), in_specs=[_cspec(r), _gspec(S5_QC, 4 * S5_P)],
        out_specs=_gspec(r, 4 * S5_P), out_shape=jax.ShapeDtypeStruct((g, r, 4 * S5_P), F32),
        compiler_params=_params(("parallel",)))(u, wt)


def _swap(h):
    return pltpu.roll(h, S5_P, 1)


def s5_carry_fwd(s, da, db, *, name):
    nck, rows, _ = s.shape
    w = 2 * S5_P

    def body(s_ref, da_ref, db_ref, h_ref):
        dirs = ((False, slice(0, w)), (True, slice(w, 2 * w)))
        coef = [(da_ref[:, cols], db_ref[:, cols]) for _, cols in dirs]

        def step(i, hs):
            new = []
            for (rev, cols), (a, b), h in zip(dirs, coef, hs):
                k = (nck - 1 - i) if rev else i
                h_ref[k, :, cols] = h
                new.append(a * h + b * _swap(h) + s_ref[k, :, cols])
            return tuple(new)

        z = jnp.zeros((rows, w), F32)
        lax.fori_loop(0, nck, step, (z, z), unroll=2)

    rt = min(CARRY_ROWS, rows)
    big, small = pl.BlockSpec((nck, rt, 2 * w), lambda i: (0, i, 0)), pl.BlockSpec((rt, 2 * w), lambda i: (i, 0))
    rows = rt
    return pl.pallas_call(
        body, name=name, grid=(s.shape[1] // rt,), in_specs=[big, small, small], out_specs=big,
        out_shape=jax.ShapeDtypeStruct(s.shape, F32), compiler_params=_params(("parallel",)))(s, da, db)


def s5_carry_bwd(hin, dh, da, db, *, name):
    nck, rows, _ = hin.shape
    w = 2 * S5_P

    def body(h_ref, dh_ref, da_ref, db_ref, ds_ref, gda_ref, gdb_ref):
        dirs = ((False, slice(0, w)), (True, slice(w, 2 * w)))
        coef = [(da_ref[:, cols], db_ref[:, cols]) for _, cols in dirs]

        def step(i, carries):
            new = []
            for (rev, cols), (a, b), (g, ga, gb) in zip(dirs, coef, carries):
                k = i if rev else (nck - 1 - i)
                ds_ref[k, :, cols] = g
                h = h_ref[k, :, cols]
                new.append((dh_ref[k, :, cols] + a * g + _swap(b * g), ga + g * h, gb + g * _swap(h)))
            return tuple(new)

        z = jnp.zeros((rows, w), F32)
        res = lax.fori_loop(0, nck, step, ((z, z, z), (z, z, z)), unroll=2)
        for (_, cols), (_, ga, gb) in zip(dirs, res):
            gda_ref[:, cols] = ga
            gdb_ref[:, cols] = gb

    rt = min(CARRY_ROWS, rows)
    big, small = pl.BlockSpec((nck, rt, 2 * w), lambda i: (0, i, 0)), pl.BlockSpec((rt, 2 * w), lambda i: (i, 0))
    rows = rt
    return pl.pallas_call(
        body, name=name, grid=(hin.shape[1] // rt,), in_specs=[big, big, small, small], out_specs=[big, small, small],
        out_shape=[jax.ShapeDtypeStruct(hin.shape, F32), jax.ShapeDtypeStruct(da.shape, F32),
                   jax.ShapeDtypeStruct(da.shape, F32)],
        compiler_params=_params(("parallel",)))(hin, dh, da, db)


def s5_out(u, hin, tt, mt, *, name):
    g, _, r, _ = u.shape

    def body(u_ref, h_ref, t_ref, m_ref, o_ref):
        u_v, h_v = _cat(u_ref), h_ref[...]
        for half in range(S5_HALVES):
            cols = slice(half * LANES, (half + 1) * LANES)
            o_ref[half] = _bd(u_v, t_ref[half], 1, 0) + _bd(h_v, m_ref[:, cols], 1, 0)

    return pl.pallas_call(
        body, name=name, grid=(g,),
        in_specs=[_cspec(r), _gspec(r, 4 * S5_P), _gspec(S5_HALVES, S5_QC, LANES), _gspec(4 * S5_P, S5_QC)],
        out_specs=_cspec(r), out_shape=jax.ShapeDtypeStruct((g, S5_HALVES, r, LANES), F32),
        compiler_params=_params(("parallel",)))(u, hin, tt, mt)


def s5_out_bwd(dy, u, hin, tt, mt, *, name):
    g, _, r, _ = u.shape

    def body(dy_ref, u_ref, h_ref, t_ref, m_ref, dh_ref, dt_ref, dm_ref, du_ref):
        dy_v, u_v = _cat(dy_ref), _cat(u_ref)
        dh_ref[...] = _bd(dy_v, m_ref[...], 1, 1)
        dm_ref[...] = _bd(h_ref[...], dy_v, 0, 0)
        du = None
        for half in range(S5_HALVES):
            dy_h = dy_ref[half]
            dt_ref[half] = _bd(u_v, dy_h, 0, 0)
            part = _bd(dy_h, t_ref[half], 1, 1)
            du = part if du is None else du + part
        _put(du_ref, du)

    tspec = _gspec(S5_HALVES, S5_QC, LANES)
    return pl.pallas_call(
        body, name=name, grid=(g,),
        in_specs=[_cspec(r), _cspec(r), _gspec(r, 4 * S5_P), tspec, _gspec(4 * S5_P, S5_QC)],
        out_specs=[_gspec(r, 4 * S5_P), tspec, _gspec(4 * S5_P, S5_QC), _cspec(r)],
        out_shape=[jax.ShapeDtypeStruct((g, r, 4 * S5_P), F32), jax.ShapeDtypeStruct((g, S5_HALVES, S5_QC, LANES), F32),
                   jax.ShapeDtypeStruct((g, 4 * S5_P, S5_QC), F32), jax.ShapeDtypeStruct((g, S5_HALVES, r, LANES), F32)],
        compiler_params=_params(("parallel",)))(dy, u, hin, tt, mt)


def s5_state_in_bwd(ds, u, wt, du1, *, name):
    g, _, r, _ = u.shape

    def body(ds_ref, u_ref, w_ref, du1_ref, du_ref, dw_ref):
        ds_v = ds_ref[...]
        _put(du_ref, _cat(du1_ref) + _bd(ds_v, w_ref[...], 1, 1))
        dw_ref[...] = _bd(_cat(u_ref), ds_v, 0, 0)

    return pl.pallas_call(
        body, name=name, grid=(g,),
        in_specs=[_gspec(r, 4 * S5_P), _cspec(r), _gspec(S5_QC, 4 * S5_P), _cspec(r)],
        out_specs=[_cspec(r), _gspec(S5_QC, 4 * S5_P)],
        out_shape=[jax.ShapeDtypeStruct((g, S5_HALVES, r, LANES), F32), jax.ShapeDtypeStruct((g, S5_QC, 4 * S5_P), F32)],
        compiler_params=_params(("parallel",)))(ds, u, wt, du1)


def _s5_post(ypre, u, dvec, wv, wg, bv, bg, nw):
    g = _gelu(ypre + dvec * u)
    out = (dot_nn(g, wv) + bv) * jax.nn.sigmoid(dot_nn(g, wg) + bg)
    return (_rms(out, nw),)


def _ssd_post(y, z, nw):
    return (_rms(y * _silu(z), nw),)


def _to_carry(s, bsz):
    nck = s.shape[1] // bsz
    return jnp.transpose(s.reshape(S5_G, bsz, nck, -1), (2, 0, 1, 3)).reshape(nck, S5_G * bsz, -1)


def _from_carry(h, bsz):
    nck = h.shape[0]
    return jnp.transpose(h.reshape(nck, S5_G, bsz, -1), (1, 2, 0, 3)).reshape(S5_G, bsz * nck, -1)


def _block_diag(w):
    eye = jnp.eye(S5_G, dtype=w.dtype)
    return jnp.einsum('gcd,gh->gchd', w, eye).reshape(S5_W, S5_W)


def _diag_blocks(w):
    v = w.reshape(S5_G, S5_C, S5_G, S5_C)
    return v[jnp.arange(S5_G), :, jnp.arange(S5_G), :]


def _dt_rows(dt, bsz):
    seq = dt.shape[0] // bsz
    return jnp.transpose(dt.reshape(bsz, seq, 2, SGROUPS, HPG), (0, 3, 2, 4, 1)).reshape(bsz, SGROUPS, 2 * HPG, seq)


def _dt_from_rows(dr):
    bsz, _, _, seq = dr.shape
    return jnp.transpose(dr.reshape(bsz, SGROUPS, 2, HPG, seq), (0, 4, 2, 1, 3)).reshape(bsz * seq, 2 * HEADS)


def _head_params(f, b):
    return jnp.concatenate([f.reshape(SGROUPS, HPG), b.reshape(SGROUPS, HPG)], axis=1)[:, :, None]


def _head_grads(gr):
    v = gr.sum(0)[:, :, 0]
    return v[:, :HPG].reshape(HEADS), v[:, HPG:].reshape(HEADS)


def local_step(x, target, w):
    bsz, seq, d = x.shape
    t = bsz * seq
    x2, tgt2 = x.reshape(t, d), target.reshape(t, d)
    g = {}
    row = lambda v: v.reshape(1, -1)
    bf = lambda v: v.astype(BF16)

    w_in = _unshard(bf(w['w_in']), SHARDED['w_in'])
    cuts = [0, SSD_W, SSD_W + XBC_W, SSD_W + XBC_W + 2 * HEADS, w_in.shape[1]]
    w_in_parts = [w_in[:, a:b] for a, b in zip(cuts[:-1], cuts[1:])]
    norm_mix = row(w['norm_mix_w']) + w.get('token', 0.0)
    (hn,) = rowmap_fwd(lambda a, nw: (_rms(a, nw),), [x2], [norm_mix], [(d, BF16)], name="rms_mix")
    z, xbc, dt, u = [matmul_sum([hn], [p], tm=1024, name=f"in_proj_{i}") for i, p in enumerate(w_in_parts)]

    conv_w, conv_b = _unshard(w['ssd_conv_w'], SHARDED['ssd_conv_w']), row(w['ssd_conv_b'])
    act = ssd_conv_fwd(xbc, conv_w, conv_b, bsz=bsz, name="ssd_conv")
    dtr = _dt_rows(dt, bsz)
    prs = (_head_params(w['ssd_dt_bias_fwd'], w['ssd_dt_bias_bwd']),
           _head_params(w['ssd_a_log_fwd'], w['ssd_a_log_bwd']),
           _head_params(w['ssd_d'], jnp.zeros_like(w['ssd_d'])))
    act3 = act.reshape(bsz, seq, XBC_W)
    y_scan, ssd_states = ssd_scan_fwd(act3, dtr, prs, name="ssd_scan")
    y_scan = y_scan.reshape(t, SSD_W)
    ssd_nw = row(w['ssd_norm_w'])
    (y_ssd,) = rowmap_fwd(_ssd_post, [y_scan, z], [ssd_nw], [(SSD_W, BF16)], name="ssd_post")

    s5_names = ['s5_lambda_re_fwd', 's5_lambda_im_fwd', 's5_log_step_fwd', 's5_lambda_re_bwd', 's5_lambda_im_bwd',
                's5_log_step_bwd', 's5_b_re', 's5_b_im', 's5_c_re_fwd', 's5_c_im_fwd', 's5_c_re_bwd', 's5_c_im_bwd']
    (kt, wt, mt, da, db), s5_pull = jax.vjp(_s5_operators, *[w[n] for n in s5_names])
    tt_b, wt_b, mt_b = s5_toeplitz(kt, name="s5_toeplitz"), bf(wt), bf(mt)
    da_r, db_r = jnp.repeat(da, bsz, axis=0), jnp.repeat(db, bsz, axis=0)
    uc = to_chunks(u, name="s5_to_chunks_u")
    s_in = _to_carry(s5_state_in(uc, wt_b, name="s5_state_in"), bsz)
    hin_c = s5_carry_fwd(s_in, da_r, db_r, name="s5_carry")
    hin = _from_carry(hin_c, bsz)
    ypre = from_chunks(s5_out(uc, hin, tt_b, mt_b, name="s5_out"), name="s5_from_chunks_y", as_blocks=True)
    glu_w = w['s5_glu_w']
    s5_par = [row(w['s5_d']), _block_diag(glu_w[:, :, :S5_C]), _block_diag(glu_w[:, :, S5_C:]),
              row(w['s5_glu_b'][:, :S5_C]), row(w['s5_glu_b'][:, S5_C:]), row(w['s5_norm_w'])]
    (y_s5,) = rowmap_fwd(_s5_post, [ypre, u], s5_par, [(S5_W, BF16)], name="s5_post")

    if 'late' in w:
        w = {**w, **w['late'](y_s5)}
    w_out = bf(w['w_out']).reshape(SSD_W + S5_W, d)
    norm_ffn = row(w['norm_ffn_w'])
    h1, hn2 = matmul_sum([y_ssd, y_s5], [w_out[:SSD_W], w_out[SSD_W:]], add=x2, norm_w=norm_ffn, name="out_proj")
    pad_c = FFN_PAD - FFN_BLK
    half = N_DEV // 2
    w_up3 = jnp.pad(bf(w['ffn_w_up']), ((0, 0), (0, 0), (0, pad_c)))
    w_down = jnp.pad(bf(w['ffn_w_down']).reshape(half, FFN_BLK, d), ((0, 0), (0, pad_c), (0, 0)))
    w_down = w_down.reshape(half * FFN_PAD, d)
    fconv_w = jnp.pad(w['ffn_conv_w'], ((0, 0), (0, 0), (0, pad_c)))
    fconv_w = jnp.transpose(fconv_w, (1, 0, 2)).reshape(FCONV, N_DEV * FFN_PAD)
    fconv_b = row(jnp.pad(w['ffn_conv_b'].reshape(N_DEV, FFN_BLK), ((0, 0), (0, pad_c))))
    up = matmul_cols(hn2, w_up3, out_dtype=BF16, name="ffn_up")
    fact = ffn_act_fwd(up, fconv_w, fconv_b, bsz=bsz, name="ffn_act")
    loss, dh2, g_nf = loss_head(h1, tgt2, row(w['norm_final_w']), matmul=(fact, w_down), tm=512, name="ffn_down_loss")
    g['norm_final_w'] = g_nf.reshape(-1)

    dfact = matmul_sum([dh2], [w_down], nt=True, tm=1024, name="ffn_down_dx")
    g_down = matmul_tn(fact, dh2, name="ffn_down_dw").reshape(half, FFN_PAD, d)[:, :FFN_BLK]
    g['ffn_w_down'] = g_down.reshape(N_DEV, FFN_BLK // 2, d)
    dval, dgate, dwv, dwg, dbv, dbg = ffn_act_bwd(up, dfact, fconv_w, fconv_b, bsz=bsz, name="ffn_act_bwd")
    g_cw = jnp.concatenate([dwv, dwg], axis=1).reshape(FCONV, N_DEV, FFN_PAD)[:, :, :FFN_BLK]
    g['ffn_conv_w'] = jnp.transpose(g_cw, (1, 0, 2))
    g['ffn_conv_b'] = jnp.concatenate([dbv, dbg], axis=1).reshape(N_DEV, FFN_PAD)[:, :FFN_BLK].reshape(-1)
    windows = [(dval, FFN_PAD, p) for p in range(half)] + [(dgate, FFN_PAD, p) for p in range(half)]
    g['ffn_w_up'] = jnp.concatenate([matmul_tn(hn2, dval, out_blocks=half, name="ffn_up_dw_val"),
                                     matmul_tn(hn2, dgate, out_blocks=half, name="ffn_up_dw_gate")],
                                    axis=0)[:, :, :FFN_BLK]
    send_early = w.get('on_grads')
    if send_early:
        norm_ffn = norm_ffn + send_early(g, ['ffn_w_up', 'ffn_w_down'])
    dh1, g_nffn = matmul_sum(windows, [(w_up3, p) for p in range(N_DEV)], nt=True, tm=256,
                             norm_bwd=(h1, norm_ffn, dh2), name="ffn_up_dx")
    g['norm_ffn_w'] = g_nffn.reshape(-1)

    dycat = matmul_sum([dh1], [w_out], nt=True, tm=1024, name="out_proj_dx")
    g['w_out'] = jnp.concatenate([matmul_tn(y_ssd, dh1, name="out_proj_dw_ssd"),
                                  matmul_tn(y_s5, dh1, name="out_proj_dw_s5")], axis=0).reshape(w['w_out'].shape)
    if send_early:
        ssd_nw = ssd_nw + send_early(g, ['w_out'])
    dy_scan, dz, g_snw = rowmap_bwd(_ssd_post, [y_scan, z], [ssd_nw], [(dycat, SSD_W, 0)], name="ssd_post_bwd")
    g['ssd_norm_w'] = g_snw.reshape(-1)
    dypre, du_a, g_d, g_wv, g_wg, g_bv, g_bg, g_s5nw = rowmap_bwd(
        _s5_post, [ypre, u], s5_par, [(dycat, S5_W, SSD_W // S5_W)], name="s5_post_bwd")
    g['s5_d'], g['s5_norm_w'] = g_d.reshape(-1), g_s5nw.reshape(-1)
    g['s5_glu_w'] = jnp.concatenate([_diag_blocks(g_wv), _diag_blocks(g_wg)], axis=-1)
    g['s5_glu_b'] = jnp.concatenate([g_bv.reshape(S5_G, S5_C), g_bg.reshape(S5_G, S5_C)], axis=-1)

    dyc = to_chunks(dypre, name="s5_to_chunks_dy")
    dhin, dtt, dmt, du1 = s5_out_bwd(dyc, uc, hin, tt_b, mt_b, name="s5_out_bwd")
    ds_c, gda, gdb = s5_carry_bwd(hin_c, _to_carry(dhin, bsz), da_r, db_r, name="s5_carry_bwd")
    duc, dwt = s5_state_in_bwd(_from_carry(ds_c, bsz), uc, wt_b, du1, name="s5_state_in_bwd")
    du = from_chunks(duc, add=du_a, name="s5_from_chunks_du")
    fold = lambda v: v.reshape(S5_G, bsz, -1).sum(1)
    dkt = s5_toeplitz_bwd(dtt, name="s5_toeplitz_bwd")
    for n, gv in zip(s5_names, s5_pull((dkt, dwt, dmt, fold(gda), fold(gdb)))):
        g[n] = gv

    dxs, dbm, dcm, ddtr, gbr, gar, gdk = ssd_scan_bwd(
        act3, dtr, prs, ssd_states, dy_scan.reshape(bsz, seq, SSD_W), name="ssd_scan_bwd")
    g['ssd_dt_bias_fwd'], g['ssd_dt_bias_bwd'] = _head_grads(gbr)
    g['ssd_a_log_fwd'], g['ssd_a_log_bwd'] = _head_grads(gar)
    g['ssd_d'] = _head_grads(gdk)[0]
    dparts_act = [v.reshape(t, v.shape[-1]) for v in (dxs, dbm, dcm)]
    dxbc, g_cw, g_cb = ssd_conv_bwd(xbc, dparts_act, conv_w, conv_b, bsz=bsz, name="ssd_conv_bwd")
    g['ssd_conv_w'] = _shard_rows(g_cw, SHARDED['ssd_conv_w']).reshape(w['ssd_conv_w'].shape)
    g['ssd_conv_b'] = g_cb.reshape(-1)
    ddt = _dt_from_rows(ddtr)

    if send_early:
        ddt = ddt + send_early(g, [], loss=loss)
    dparts = [dz, dxbc, ddt, du]
    g_in = jnp.concatenate([matmul_tn(hn, dp, name=f"in_proj_dw_{i}") for i, dp in enumerate(dparts)], axis=1)
    g['w_in'] = _shard_rows(g_in, SHARDED['w_in']).reshape(w['w_in'].shape)
    if send_early:
        dparts[2] = ddt + send_early(g, ['w_in'])
    dx, g_nmix = matmul_sum(dparts, w_in_parts, nt=True, tm=256, norm_bwd=(x2, norm_mix, dh1), name="in_proj_dx")
    g['norm_mix_w'] = g_nmix.reshape(-1)
    return loss, dx.reshape(bsz, seq, d), g


ANY = pl.BlockSpec(memory_space=pl.ANY)


def all_gather(shards, *, name):
    n = len(shards)

    def body(*refs):
        x_refs, out_refs = refs[:n], refs[n:2 * n]
        send_sems, recv_sems, local_sems = refs[2 * n:]
        x, y, c = lax.axis_index("x"), lax.axis_index("y"), lax.axis_index("c")
        me, sibling = (x, y, c), (x, y, 1 - c)
        chips = [(1 - x, y), (x, 1 - y), (1 - x, 1 - y)]

        def copy(k, j, block, to, own=False):
            dst = out_refs[j].at[4 * block[0] + 2 * block[1] + block[2]]
            return pltpu.make_async_remote_copy(
                src_ref=x_refs[j] if own else dst, dst_ref=dst,
                send_sem=send_sems.at[k, j], recv_sem=recv_sems.at[k, j], device_id=to, device_id_type=MESH)

        mine = [pltpu.make_async_copy(x_refs[j], out_refs[j].at[4 * x + 2 * y + c], local_sems.at[j]) for j in range(n)]
        first = [copy(0, j, me, sibling, own=True) for j in range(n)]
        first += [copy(1 + i, j, me, (*chip, c), own=True) for i, chip in enumerate(chips) for j in range(n)]
        for cp in mine + first:
            cp.start()
        passed = []
        for i, chip in enumerate(chips):
            for j in range(n):
                copy(1 + i, j, (*chip, c), me).wait_recv()
                passed.append(copy(4 + i, j, (*chip, c), sibling))
                passed[-1].start()
        for j in range(n):
            copy(0, j, sibling, me).wait_recv()
        for i, chip in enumerate(chips):
            for j in range(n):
                copy(4 + i, j, (*chip, 1 - c), me).wait_recv()
        for cp in first + passed:
            cp.wait_send()
        for cp in mine:
            cp.wait()

    return pl.pallas_call(
        body, name=name, out_shape=[jax.ShapeDtypeStruct((N_DEV,) + s.shape, s.dtype) for s in shards],
        in_specs=[ANY] * n, out_specs=[ANY] * n,
        scratch_shapes=[pltpu.SemaphoreType.DMA((7, n)), pltpu.SemaphoreType.DMA((7, n)),
                        pltpu.SemaphoreType.DMA((n,))],
    )(*shards)


HBM_SPEC = pl.BlockSpec(memory_space=pltpu.HBM)
SEM_SPEC = pl.BlockSpec(memory_space=pltpu.SEMAPHORE)
SPLIT_PARAMS = pltpu.CompilerParams(has_side_effects=pltpu.SideEffectType.DATAFLOW_SIDE_EFFECTING)


def _peer_copies(src_refs, land_refs, send_sems, recv_sems, indexed):
    x, y, c = lax.axis_index("x"), lax.axis_index("y"), lax.axis_index("c")
    me = 4 * x + 2 * y + c
    copies = []
    for k in range(1, N_DEV):
        px = (1 - x) if k & 4 else x
        py = (1 - y) if k & 2 else y
        pc = (1 - c) if k & 1 else c
        for j, (src, land) in enumerate(zip(src_refs, land_refs)):
            sem = (k - 1) * len(src_refs) + j
            copies.append(pltpu.make_async_remote_copy(
                src_ref=src.at[4 * px + 2 * py + pc] if indexed else src, dst_ref=land.at[me],
                send_sem=send_sems.at[sem], recv_sem=recv_sems.at[sem],
                device_id=(px, py, pc), device_id_type=MESH))
    return copies


def scatter_start(srcs, *, name, indexed):
    n = len(srcs)
    lands = [lax.empty(s.shape if indexed else (N_DEV,) + s.shape, s.dtype) for s in srcs]

    def body(*refs):
        send_sems, recv_sems = refs[2 * n], refs[2 * n + 1]
        for cp in _peer_copies(refs[:n], refs[n:2 * n], send_sems, recv_sems, indexed):
            cp.start()
        refs[-1][...] = jnp.zeros_like(refs[-1])

    hbm = lambda a: pltpu.HBM(a.shape, a.dtype)
    sems = pltpu.SemaphoreType.DMA(((N_DEV - 1) * n,))
    res = pl.pallas_call(
        body, name=name,
        out_shape=(sems, sems, *[hbm(a) for a in srcs + lands], jax.ShapeDtypeStruct((8, LANES), F32)),
        in_specs=[HBM_SPEC] * (2 * n),
        out_specs=(SEM_SPEC, SEM_SPEC, *[HBM_SPEC] * (2 * n), pl.BlockSpec(memory_space=pltpu.VMEM)),
        input_output_aliases={i: 2 + i for i in range(2 * n)}, compiler_params=SPLIT_PARAMS,
    )(*[pltpu.with_memory_space_constraint(a, pltpu.HBM) for a in srcs + lands])
    return res[0], res[1], list(res[2:2 + n]), list(res[2 + n:2 + 2 * n]), res[-1]


def scatter_wait(send_sems, recv_sems, srcs, lands, after, *, name, indexed):
    n = len(srcs)

    def body(*refs):
        for cp in _peer_copies(refs[:n], refs[n:2 * n], refs[2 * n], refs[2 * n + 1], indexed):
            cp.wait_send()
            cp.wait_recv()

    hbm = lambda a: pltpu.HBM(a.shape, a.dtype)
    res = pl.pallas_call(
        body, name=name, out_shape=tuple(hbm(a) for a in srcs + lands),
        in_specs=[HBM_SPEC] * (2 * n) + [SEM_SPEC, SEM_SPEC, ANY], out_specs=tuple([HBM_SPEC] * (2 * n)),
        input_output_aliases={i: i for i in range(2 * n)}, compiler_params=SPLIT_PARAMS,
    )(*srcs, *lands, send_sems, recv_sems, after)
    return list(res[:n]), list(res[n:])


def _adam_rows(r, c):
    fits = [t for t in range(8, r + 1, 8) if r % t == 0 and N_DEV * t * c * 4 <= 6 * 2 ** 20]
    return max(fits) if fits else r


def adamw(recv, w, m, v, *, name):
    _, r, n = recv.shape
    tr = _adam_rows(r, n)

    def body(r_ref, w_ref, m_ref, v_ref, g_ref, d_ref, nm_ref, nv_ref):
        g = r_ref[0].astype(F32)
        for s in range(1, N_DEV):
            g = g + r_ref[s].astype(F32)
        m_new = ADAM_B1 * m_ref[...] + (1.0 - ADAM_B1) * g
        v_new = ADAM_B2 * v_ref[...] + (1.0 - ADAM_B2) * jnp.square(g)
        m_hat = m_new / (1.0 - ADAM_B1 ** ADAM_STEP)
        v_hat = v_new / (1.0 - ADAM_B2 ** ADAM_STEP)
        g_ref[...] = g
        d_ref[...] = -ADAM_LR * (m_hat / (jnp.sqrt(v_hat) + ADAM_EPS) + ADAM_WD * w_ref[...])
        nm_ref[...] = m_new
        nv_ref[...] = v_new

    blk = pl.BlockSpec((tr, n), lambda i: (i, 0))
    return pl.pallas_call(
        body, name=name, grid=(r // tr,), in_specs=[pl.BlockSpec((N_DEV, tr, n), lambda i: (0, i, 0)), blk, blk, blk],
        out_specs=[blk] * 4, out_shape=[jax.ShapeDtypeStruct((r, n), F32)] * 4,
        compiler_params=_params(("parallel",)))(recv, w, m, v)


def _shard_rows(full, axis):
    if axis == 0:
        return full.reshape(N_DEV, -1)
    r, c = full.shape
    return jnp.transpose(full.reshape(r, N_DEV, c // N_DEV), (1, 0, 2)).reshape(N_DEV, -1)


def _unshard(blocks, axis):
    if axis == 0:
        return blocks.reshape(-1, blocks.shape[-1])
    return jnp.transpose(blocks, (1, 0, 2)).reshape(blocks.shape[1], -1)


def kernel(x, norm_mix_w, w_in, ssd_conv_w, ssd_conv_b, ssd_dt_bias_fwd, ssd_dt_bias_bwd, ssd_a_log_fwd, ssd_a_log_bwd, ssd_d, ssd_norm_w, s5_lambda_re_fwd, s5_lambda_im_fwd, s5_log_step_fwd, s5_lambda_re_bwd, s5_lambda_im_bwd, s5_log_step_bwd, s5_b_re, s5_b_im, s5_c_re_fwd, s5_c_im_fwd, s5_c_re_bwd, s5_c_im_bwd, s5_d, s5_glu_w, s5_glu_b, s5_norm_w, w_out, norm_ffn_w, ffn_w_up, ffn_conv_w, ffn_conv_b, ffn_w_down, norm_final_w, loss_target, m_norm_mix_w, m_w_in, m_ssd_conv_w, m_ssd_conv_b, m_ssd_dt_bias_fwd, m_ssd_dt_bias_bwd, m_ssd_a_log_fwd, m_ssd_a_log_bwd, m_ssd_d, m_ssd_norm_w, m_s5_lambda_re_fwd, m_s5_lambda_im_fwd, m_s5_log_step_fwd, m_s5_lambda_re_bwd, m_s5_lambda_im_bwd, m_s5_log_step_bwd, m_s5_b_re, m_s5_b_im, m_s5_c_re_fwd, m_s5_c_im_fwd, m_s5_c_re_bwd, m_s5_c_im_bwd, m_s5_d, m_s5_glu_w, m_s5_glu_b, m_s5_norm_w, m_w_out, m_norm_ffn_w, m_ffn_w_up, m_ffn_conv_w, m_ffn_conv_b, m_ffn_w_down, m_norm_final_w, v_norm_mix_w, v_w_in, v_ssd_conv_w, v_ssd_conv_b, v_ssd_dt_bias_fwd, v_ssd_dt_bias_bwd, v_ssd_a_log_fwd, v_ssd_a_log_bwd, v_ssd_d, v_ssd_norm_w, v_s5_lambda_re_fwd, v_s5_lambda_im_fwd, v_s5_log_step_fwd, v_s5_lambda_re_bwd, v_s5_lambda_im_bwd, v_s5_log_step_bwd, v_s5_b_re, v_s5_b_im, v_s5_c_re_fwd, v_s5_c_im_fwd, v_s5_c_re_bwd, v_s5_c_im_bwd, v_s5_d, v_s5_glu_w, v_s5_glu_b, v_s5_norm_w, v_w_out, v_norm_ffn_w, v_ffn_w_up, v_ffn_conv_w, v_ffn_conv_b, v_ffn_w_down, v_norm_final_w):
    args = dict(locals())
    strip = lambda n, v: v if n == 'norm_final_w' else v[0]
    w = {n: strip(n, args[n]) for n in WEIGHTS}

    mats = ['w_in', 'w_out', 'ffn_w_up', 'ffn_w_down']
    convs = ['ssd_conv_w', 'ffn_conv_w']
    shard = lambda n: w[n].astype(BF16) if n in mats else w[n]
    early, late = ['w_in', 'ssd_conv_w'], ['w_out', 'ffn_w_up', 'ffn_w_down', 'ffn_conv_w']
    full = dict(w)
    full.update(zip(early, all_gather([shard(n) for n in early], name="weight_all_gather")))
    ssem, rsem, src_thru, land_thru, token = scatter_start([shard(n) for n in late], name="weight_gather_start",
                                                           indexed=False)
    me = 4 * lax.axis_index("x") + 2 * lax.axis_index("y") + lax.axis_index("c")

    def late_weights(after):
        own, landed = scatter_wait(ssem, rsem, src_thru, land_thru, after, name="weight_gather_wait", indexed=False)
        return {n: lax.dynamic_update_index_in_dim(l, o, me, 0) for n, o, l in zip(late, own, landed)}

    full['late'], full['token'] = late_weights, token[:1, :1]

    pending = []
    last = 'norm_mix_w'
    small = convs + [n for n in WEIGHTS if n not in SHARDED and n != last]
    slot = {n: -(-w[n].size // (8 * LANES)) * 8 for n in small}
    used = sum(slot.values()) + 8
    nrow = -(-used // PACK_ROWS) * PACK_ROWS

    def tiles(v, n):
        return jnp.pad(v, ((0, 0), (0, slot[n] * LANES - v.shape[1]))).reshape(v.shape[0], slot[n], LANES)

    def send_early(grads, names, loss=None):
        srcs = [grads[n].astype(BF16) for n in names]
        if loss is not None:
            pieces = [tiles(grads[n].reshape(N_DEV, -1), n) if n in SHARDED else
                      jnp.broadcast_to(tiles(grads[n].reshape(1, -1), n), (N_DEV, slot[n], LANES)) for n in small]
            pieces.append(jnp.broadcast_to(jnp.pad(loss.reshape(1, 1, 1), ((0, 0), (0, 7), (0, LANES - 1))),
                                           (N_DEV, 8, LANES)))
            pieces.append(jnp.zeros((N_DEV, nrow - used, LANES), F32))
            srcs.append(jnp.concatenate(pieces, axis=1))
            names = names + ['small']
        started = scatter_start(srcs, name="grad_start_" + names[0], indexed=True)
        pending.append((names,) + started[:4])
        return started[4][:1, :1]

    full['on_grads'] = send_early
    loss, grad_x, g = local_step(x, loss_target, full)

    last_send = jnp.broadcast_to(g[last].reshape(1, -1, LANES), (N_DEV, g[last].size // LANES, LANES))
    last_started = scatter_start([last_send], name="grad_start_" + last, indexed=True)
    recv, outs = {}, [{}, {}, {}, {}]

    def arrived(names, started, after):
        own, landed = scatter_wait(*started, after, name="grad_wait_" + names[0], indexed=True)
        for n, o, l in zip(names, own, landed):
            recv[n] = lax.dynamic_update_index_in_dim(l, lax.dynamic_index_in_dim(o, me, 0, keepdims=False), me, 0)

    def update(n):
        shape = recv[n].shape[1:]
        res = adamw(recv[n], *[strip(n, args[p + n]).reshape(shape) for p in ('', 'm_', 'v_')], name="adamw_" + n)
        for o, p in zip(outs, res):
            o[n] = p.reshape(args[n].shape)

    for names, *started in pending:
        arrived(names, started, last_started[4])
    for n in mats:
        update(n)

    def pack(prefix):
        vals = [tiles(strip(n, args[prefix + n]).reshape(1, -1), n)[0] for n in small]
        return jnp.concatenate(vals + [jnp.zeros((nrow - used + 8, LANES), F32)], axis=0)

    packed = adamw(recv['small'], pack(''), pack('m_'), pack('v_'), name="adamw_small")
    arrived([last], last_started[:4], packed[1])
    update(last)
    off = 0
    for n in small:
        for o, p in zip(outs, packed):
            o[n] = p[off:off + slot[n]].reshape(-1)[:w[n].size].reshape(args[n].shape)
        off += slot[n]
    loss_out = packed[0][off, 0].reshape(())
    return (loss_out, grad_x, *[o[n] for o in outs for n in WEIGHTS])
```

```python
import functools

import jax
import jax.numpy as jnp
from jax import lax
from jax.experimental import pallas as pl
from jax.experimental.pallas import tpu as pltpu

F32, BF16 = jnp.float32, jnp.bfloat16
N_DEV = 8
D_MODEL = 1024
SSD_W, HEADS, HDIM, SGROUPS, HPG, NSTATE, SCONV, QC = 1024, 16, 64, 4, 4, 128, 5, 128
XBC_W = SSD_W + 2 * SGROUPS * NSTATE
S5_W, S5_G, S5_C, S5_P, S5_Q = 512, 32, 16, 64, 16
S5_QC = S5_Q * S5_C
CARRY_ROWS = 32
DFF, FCONV = 2816, 3
FFN_BLK, FFN_PAD = 704, 768
EPS = 1e-6
ADAM_LR, ADAM_B1, ADAM_B2, ADAM_EPS, ADAM_WD, ADAM_STEP = 0.001, 0.9, 0.999, 1e-08, 0.01, 10
LANES = 128
MESH = pl.DeviceIdType.MESH

WEIGHTS = ['norm_mix_w', 'w_in', 'ssd_conv_w', 'ssd_conv_b', 'ssd_dt_bias_fwd', 'ssd_dt_bias_bwd', 'ssd_a_log_fwd',
           'ssd_a_log_bwd', 'ssd_d', 'ssd_norm_w', 's5_lambda_re_fwd', 's5_lambda_im_fwd', 's5_log_step_fwd',
           's5_lambda_re_bwd', 's5_lambda_im_bwd', 's5_log_step_bwd', 's5_b_re', 's5_b_im', 's5_c_re_fwd', 's5_c_im_fwd',
           's5_c_re_bwd', 's5_c_im_bwd', 's5_d', 's5_glu_w', 's5_glu_b', 's5_norm_w', 'w_out', 'norm_ffn_w', 'ffn_w_up',
           'ffn_conv_w', 'ffn_conv_b', 'ffn_w_down', 'norm_final_w']
SHARDED = {'w_in': 1, 'ssd_conv_w': 1, 'w_out': 0, 'ffn_w_up': 1, 'ffn_conv_w': 1, 'ffn_w_down': 0}
FULL_SHAPE = {'w_in': (1024, 3616), 'ssd_conv_w': (5, 2048), 'w_out': (1536, 1024), 'ffn_w_up': (1024, 5632),
              'ffn_conv_w': (3, 5632), 'ffn_w_down': (2816, 1024)}
PACK_ROWS = 512


def _pick(n, cap=1536):
    if n <= cap:
        return n
    return max(t for t in range(LANES, cap + 1, LANES) if n % t == 0)


def _params(sem):
    return pltpu.CompilerParams(dimension_semantics=sem)


def _bd(a, b, ca, cb):
    return lax.dot_general(a.astype(BF16), b.astype(BF16), (((ca,), (cb,)), ((), ())), preferred_element_type=F32)


@jax.custom_vjp
def dot_nn(a, b):
    return _bd(a, b, 1, 0)


dot_nn.defvjp(lambda a, b: (_bd(a, b, 1, 0), (a, b)),
              lambda r, g: (_bd(g, r[1], 1, 1).astype(r[0].dtype), _bd(r[0], g, 0, 0).astype(r[1].dtype)))


@jax.custom_vjp
def dot_nt(a, b):
    return _bd(a, b, 1, 1)


dot_nt.defvjp(lambda a, b: (_bd(a, b, 1, 1), (a, b)),
              lambda r, g: (_bd(g, r[1], 1, 0).astype(r[0].dtype), _bd(g, r[0], 0, 0).astype(r[1].dtype)))


@jax.custom_vjp
def dot_tn(a, b):
    return _bd(a, b, 0, 0)


dot_tn.defvjp(lambda a, b: (_bd(a, b, 0, 0), (a, b)),
              lambda r, g: (_bd(r[1], g, 1, 1).astype(r[0].dtype), _bd(r[0], g, 1, 0).astype(r[1].dtype)))


def _rows2(v):
    h = v.shape[0] // 2
    return v[:h], v[h:]


def _cols2(v):
    h = v.shape[1] // 2
    return v[:, :h], v[:, h:]


@jax.custom_vjp
def dot2_nn(la, lb, x):
    return _rows2(_bd(jnp.concatenate([la, lb], axis=0), x, 1, 0))


def _dot2_nn_bwd(res, g):
    la, lb, x = res
    gcat, lcat = jnp.concatenate(g, axis=0), jnp.concatenate([la, lb], axis=0)
    return (*_rows2(_bd(gcat, x, 1, 1)), _bd(lcat, gcat, 0, 0))


dot2_nn.defvjp(lambda la, lb, x: (dot2_nn(la, lb, x), (la, lb, x)), _dot2_nn_bwd)


@jax.custom_vjp
def dot_nt2(c, p0, p1):
    return _cols2(_bd(c, jnp.concatenate([p0, p1], axis=0), 1, 1))


def _dot_nt2_bwd(res, g):
    c, p0, p1 = res
    gcat = jnp.concatenate(g, axis=1)
    return (_bd(gcat, jnp.concatenate([p0, p1], axis=0), 1, 0), *_rows2(_bd(gcat, c, 0, 0)))


dot_nt2.defvjp(lambda c, p0, p1: (dot_nt2(c, p0, p1), (c, p0, p1)), _dot_nt2_bwd)


@jax.custom_vjp
def dot_tn2(a0, a1, b):
    return _rows2(_bd(jnp.concatenate([a0, a1], axis=1), b, 0, 0))


def _dot_tn2_bwd(res, g):
    a0, a1, b = res
    gcat, acat = jnp.concatenate(g, axis=0), jnp.concatenate([a0, a1], axis=1)
    return (*_cols2(_bd(b, gcat, 1, 1)), _bd(acat, gcat, 1, 0))


dot_tn2.defvjp(lambda a0, a1, b: (dot_tn2(a0, a1, b), (a0, a1, b)), _dot_tn2_bwd)


def _split3(x):
    hi = x.astype(BF16)
    r = x - hi.astype(F32)
    mid = r.astype(BF16)
    lo = (r - mid.astype(F32)).astype(BF16)
    return hi, mid, lo


def _cum_matrix(q, upper):
    ri = lax.broadcasted_iota(jnp.int32, (q, q), 0)
    ci = lax.broadcasted_iota(jnp.int32, (q, q), 1)
    return jnp.where((ci >= ri) if upper else (ci <= ri), 1.0, 0.0).astype(BF16)


def _exact_right(x, mat):
    return sum(jnp.dot(p, mat, preferred_element_type=F32) for p in _split3(x))


@functools.partial(jax.custom_vjp, nondiff_argnums=(1,))
def cum_row(x, rev):
    return _exact_right(x, _cum_matrix(x.shape[1], not rev))


cum_row.defvjp(lambda x, rev: (cum_row(x, rev), None),
               lambda rev, _, g: (_exact_right(g, _cum_matrix(g.shape[1], rev)),))


def _softplus(x):
    return jnp.maximum(x, 0.0) + jnp.log(1.0 + jnp.exp(-jnp.abs(x)))


def _silu(x):
    return x * jax.nn.sigmoid(x)


def _gelu(x):
    return 0.5 * x * (1.0 + jnp.tanh(0.7978845608028654 * (x + 0.044715 * (x * x * x))))


def _rms(x, w):
    xf = x.astype(F32)
    return xf * lax.rsqrt(jnp.mean(xf * xf, axis=-1, keepdims=True) + EPS) * w


def matmul_sum(a_list, b_list, *, name, out_dtype=F32, add=None, tm=512, nt=False, norm_w=None, norm_bwd=None):
    a_arrs = [a[0] if isinstance(a, tuple) else a for a in a_list]
    b_arrs = [b[0] if isinstance(b, tuple) else b for b in b_list]
    m, n = a_arrs[0].shape[0], b_arrs[0].shape[-2 if nt else -1]
    tm, tn, k = min(tm, m), _pick(n), len(a_list)
    assert (norm_w is None and norm_bwd is None) or tn == n

    def body(*refs):
        acc = None
        for a_ref, b_ref in zip(refs[:k], refs[k:2 * k]):
            p = _bd(a_ref[...], b_ref[...], 1, 1 if nt else 0)
            acc = p if acc is None else acc + p
        if add is not None:
            acc = acc + refs[2 * k][...]
        if norm_bwd is not None:
            x_ref, w_ref, res_ref, dx_ref, dw_ref = refs[-5:]
            dx, dw = jax.vjp(_rms, x_ref[...], w_ref[...])[1](acc)
            dx_ref[...] = dx + res_ref[...]

            @pl.when(pl.program_id(0) == 0)
            def _():
                dw_ref[...] = jnp.zeros_like(dw_ref)

            dw_ref[...] += dw
        elif norm_w is not None:
            refs[-2][...] = acc.astype(out_dtype)
            refs[-1][...] = _rms(acc, refs[-3][...]).astype(BF16)
        else:
            refs[-1][...] = acc.astype(out_dtype)

    def a_spec(a):
        if isinstance(a, tuple):
            return pl.BlockSpec((tm, a[1]), lambda i, j, blk=a[2]: (i, blk))
        return pl.BlockSpec((tm, a.shape[1]), lambda i, j: (i, 0))

    def b_spec(b):
        arr, p = b if isinstance(b, tuple) else (b, None)
        kk = arr.shape[-1 if nt else -2]
        shape, idx = ((tn, kk), lambda j: (j, 0)) if nt else ((kk, tn), lambda j: (0, j))
        if p is None:
            return pl.BlockSpec(shape, lambda i, j: idx(j))
        return pl.BlockSpec((None,) + shape, lambda i, j, p=p: (p,) + idx(j))

    in_specs = [a_spec(a) for a in a_list] + [b_spec(b) for b in b_list]
    args = a_arrs + b_arrs
    if add is not None:
        in_specs.append(pl.BlockSpec((tm, tn), lambda i, j: (i, j)))
        args.append(add)
    out_spec, out_shape = pl.BlockSpec((tm, tn), lambda i, j: (i, j)), jax.ShapeDtypeStruct((m, n), out_dtype)
    if norm_w is not None:
        in_specs.append(pl.BlockSpec(norm_w.shape, lambda i, j: (0, 0)))
        args.append(norm_w)
        out_spec, out_shape = [out_spec, out_spec], [out_shape, jax.ShapeDtypeStruct((m, n), BF16)]
    sem = ("parallel", "parallel")
    if norm_bwd is not None:
        x, w, res = norm_bwd
        wspec = pl.BlockSpec(w.shape, lambda i, j: (0, 0))
        in_specs += [out_spec, wspec, out_spec]
        args += [x, w, res]
        out_spec, out_shape = [out_spec, wspec], [jax.ShapeDtypeStruct((m, n), F32), jax.ShapeDtypeStruct(w.shape, F32)]
        sem = ("arbitrary", "arbitrary")
    return pl.pallas_call(
        body, name=name, grid=(m // tm, n // tn), in_specs=in_specs, out_specs=out_spec, out_shape=out_shape,
        compiler_params=_params(sem))(*args)


def matmul_multi(a, b_list, *, name, tm=512):
    m, kk = a.shape
    tm, nb = min(tm, m), len(b_list)

    def body(a_ref, *refs):
        a_v = a_ref[...]
        for b_ref, o_ref in zip(refs[:nb], refs[nb:]):
            o_ref[...] = _bd(a_v, b_ref[...], 1, 0)

    return pl.pallas_call(
        body, name=name, grid=(m // tm,),
        in_specs=[pl.BlockSpec((tm, kk), lambda i: (i, 0))] + [_full_spec(b) for b in b_list],
        out_specs=[pl.BlockSpec((tm, b.shape[1]), lambda i: (i, 0)) for b in b_list],
        out_shape=[jax.ShapeDtypeStruct((m, b.shape[1]), F32) for b in b_list],
        compiler_params=_params(("parallel",)))(a, *b_list)


def matmul_cols(a, b3, *, name, out_dtype=F32, tm=1024):
    m, kk = a.shape
    p, _, nb = b3.shape
    tm, tn = min(tm, m), _pick(nb, 768)
    per = nb // tn

    def body(a_ref, b_ref, o_ref):
        o_ref[...] = _bd(a_ref[...], b_ref[...], 1, 0).astype(out_dtype)

    return pl.pallas_call(
        body, name=name, grid=(m // tm, p * per),
        in_specs=[pl.BlockSpec((tm, kk), lambda i, j: (i, 0)),
                  pl.BlockSpec((None, kk, tn), lambda i, j: (j // per, 0, j % per))],
        out_specs=pl.BlockSpec((tm, tn), lambda i, j: (i, j)),
        out_shape=jax.ShapeDtypeStruct((m, p * nb), out_dtype),
        compiler_params=_params(("parallel", "parallel")))(a, b3)


def matmul_tn(a, b, *, name, tm=1024, out_blocks=None):
    m, k = a.shape
    n = b.shape[1]
    nb = n // (out_blocks or 1)
    tm, tk, tn = min(tm, m), _pick(k), _pick(nb, 768 if out_blocks else 1536)
    per = nb // tn

    def body(a_ref, b_ref, o_ref):
        @pl.when(pl.program_id(2) == 0)
        def _():
            o_ref[...] = jnp.zeros_like(o_ref)

        o_ref[...] += _bd(a_ref[...], b_ref[...], 0, 0)

    if out_blocks:
        out_spec = pl.BlockSpec((None, tk, tn), lambda i, j, t: (j // per, i, j % per))
        out_shape = jax.ShapeDtypeStruct((out_blocks, k, nb), F32)
    else:
        out_spec = pl.BlockSpec((tk, tn), lambda i, j, t: (i, j))
        out_shape = jax.ShapeDtypeStruct((k, n), F32)
    return pl.pallas_call(
        body, name=name, grid=(k // tk, n // tn, m // tm),
        in_specs=[pl.BlockSpec((tm, tk), lambda i, j, t: (t, i)), pl.BlockSpec((tm, tn), lambda i, j, t: (t, j))],
        out_specs=out_spec, out_shape=out_shape,
        compiler_params=_params(("parallel", "parallel", "arbitrary")))(a, b)


def _row_spec(r, tm):
    if isinstance(r, tuple):
        arr, width, blk = r
        return arr, pl.BlockSpec((tm, width), lambda i, blk=blk: (i, blk))
    return r, pl.BlockSpec((tm, r.shape[1]), lambda i: (i, 0))


def _full_spec(p):
    return pl.BlockSpec(p.shape, lambda i: (0,) * p.ndim)


def _expand_rows(rows, tm):
    arrays, specs, counts, widths = [], [], [], []
    for r in rows:
        parts = [_row_spec(p, tm) for p in (r if isinstance(r, list) else [r])]
        arrays += [a for a, _ in parts]
        specs += [s for _, s in parts]
        counts.append(len(parts))
        widths.append(sum(s.block_shape[1] for _, s in parts))
    return arrays, specs, counts, widths


def _row_values(refs, counts):
    vals, k = [], 0
    for c in counts:
        parts = [refs[k + j][...] for j in range(c)]
        vals.append(parts[0] if c == 1 else jnp.concatenate(parts, axis=1))
        k += c
    return vals


def _rows_of(rows):
    first = rows[0][0] if isinstance(rows[0], list) else rows[0]
    return (first[0] if isinstance(first, tuple) else first).shape[0]


def rowmap_fwd(fn, rows, params, outs, *, name, tm=256):
    m = _rows_of(rows)
    tm = min(tm, m)
    arrays, specs, counts, _ = _expand_rows(rows, tm)
    nin, npar = len(arrays), len(params)

    def body(*refs):
        res = fn(*_row_values(refs[:nin], counts), *[r[...] for r in refs[nin:nin + npar]])
        for o_ref, v in zip(refs[nin + npar:], res):
            o_ref[...] = v.astype(o_ref.dtype)

    return pl.pallas_call(
        body, name=name, grid=(m // tm,), in_specs=specs + [_full_spec(p) for p in params],
        out_specs=[pl.BlockSpec((tm, c), lambda i: (i, 0)) for c, _ in outs],
        out_shape=[jax.ShapeDtypeStruct((m, c), dt) for c, dt in outs],
        compiler_params=_params(("parallel",)))(*arrays, *params)


def rowmap_bwd(fn, rows, params, cts, *, name, row_dtypes=None, add=None, tm=256):
    m = _rows_of(rows)
    tm = min(tm, m)
    arrays, specs, counts, widths = _expand_rows(rows, tm)
    cp = [_row_spec(c, tm) for c in cts]
    nin, nr, npar, nc = len(arrays), len(rows), len(params), len(cts)
    row_dtypes = row_dtypes or [F32] * nr

    def body(*refs):
        ins = _row_values(refs[:nin], counts) + [r[...] for r in refs[nin:nin + npar]]
        ins = [v.astype(F32) for v in ins]
        ct = tuple(r[...].astype(F32) for r in refs[nin + npar:nin + npar + nc])
        base = nin + npar + nc
        extra = None
        if add is not None:
            extra = refs[base][...]
            base += 1
        _, pull = jax.vjp(fn, *ins)
        grads = pull(ct)
        for j in range(nr):
            g = grads[j]
            if j == 0 and extra is not None:
                g = g + extra
            refs[base + j][...] = g.astype(refs[base + j].dtype)

        @pl.when(pl.program_id(0) == 0)
        def _():
            for j in range(npar):
                refs[base + nr + j][...] = jnp.zeros_like(refs[base + nr + j])

        for j in range(npar):
            refs[base + nr + j][...] += grads[nr + j]

    in_specs = specs + [_full_spec(p) for p in params] + [s for _, s in cp]
    args = arrays + list(params) + [a for a, _ in cp]
    if add is not None:
        in_specs.append(pl.BlockSpec((tm, widths[0]), lambda i: (i, 0)))
        args.append(add)
    out_specs = [pl.BlockSpec((tm, w), lambda i: (i, 0)) for w in widths] + [_full_spec(p) for p in params]
    out_shape = [jax.ShapeDtypeStruct((m, w), dt) for w, dt in zip(widths, row_dtypes)]
    out_shape += [jax.ShapeDtypeStruct(p.shape, F32) for p in params]
    return pl.pallas_call(
        body, name=name, grid=(m // tm,), in_specs=in_specs, out_specs=out_specs, out_shape=out_shape,
        compiler_params=_params(("arbitrary",)))(*args)


def loss_head(h, target, w, *, name, tm=256, matmul=None):
    m, d = h.shape
    tm = min(tm, m)

    def body(h_ref, t_ref, w_ref, *refs):
        loss_ref, dh_ref, dw_ref = refs[-3:]
        rows = h_ref[...]
        if matmul is not None:
            rows = rows + _bd(refs[0][...], refs[1][...], 1, 0)
        y, pull = jax.vjp(_rms, rows, w_ref[...])
        err = y - t_ref[...]
        dh, dw = pull(err * (1.0 / d))

        @pl.when(pl.program_id(0) == 0)
        def _():
            loss_ref[...] = jnp.zeros_like(loss_ref)
            dw_ref[...] = jnp.zeros_like(dw_ref)

        loss_ref[...] += (0.5 / d) * jnp.sum(err * err, keepdims=True)
        dw_ref[...] += dw
        dh_ref[...] = dh

    row = pl.BlockSpec((tm, d), lambda i: (i, 0))
    in_specs, args = [row, row, _full_spec(w)], [h, target, w]
    if matmul is not None:
        in_specs += [pl.BlockSpec((tm, matmul[0].shape[1]), lambda i: (i, 0)), _full_spec(matmul[1])]
        args += list(matmul)
    return pl.pallas_call(
        body, name=name, grid=(m // tm,), in_specs=in_specs,
        out_specs=[pl.BlockSpec((1, 1), lambda i: (0, 0)), row, _full_spec(w)],
        out_shape=[jax.ShapeDtypeStruct((1, 1), F32), jax.ShapeDtypeStruct((m, d), F32),
                   jax.ShapeDtypeStruct(w.shape, F32)],
        compiler_params=_params(("arbitrary",)))(*args)


def _shift(x, s):
    if s == 0:
        return x
    n = x.shape[0]
    t = lax.broadcasted_iota(jnp.int32, x.shape, 0)
    rolled = pltpu.roll(x, (-s) % n, 0)
    return jnp.where((t + s >= 0) & (t + s < n), rolled, 0.0)


def _conv(x, w, b):
    k = w.shape[0]
    acc = b + w[k // 2:k // 2 + 1, :] * x
    for j in range(k):
        if j != k // 2:
            acc = acc + w[j:j + 1, :] * _shift(x, j - k // 2)
    return acc


def _conv_bwd(x, dc, w):
    k = w.shape[0]
    dx = None
    dws = []
    for j in range(k):
        s = j - k // 2
        term = w[j:j + 1, :] * _shift(dc, -s)
        dx = term if dx is None else dx + term
        dws.append(jnp.sum(dc * _shift(x, s), axis=0, keepdims=True))
    return dx, jnp.concatenate(dws, axis=0), jnp.sum(dc, axis=0, keepdims=True)


def _dsilu(c):
    s = jax.nn.sigmoid(c)
    return s * (1.0 + c * (1.0 - s))


def ssd_conv_fwd(xbc, w, b, *, bsz, name):
    t, c = xbc.shape
    seq, ct = t // bsz, 256

    def body(x_ref, w_ref, b_ref, o_ref):
        o_ref[...] = _silu(_conv(x_ref[...], w_ref[...], b_ref[...]))

    return pl.pallas_call(
        body, name=name, grid=(c // ct, bsz),
        in_specs=[pl.BlockSpec((seq, ct), lambda j, i: (i, j)), pl.BlockSpec((w.shape[0], ct), lambda j, i: (0, j)),
                  pl.BlockSpec((1, ct), lambda j, i: (0, j))],
        out_specs=pl.BlockSpec((seq, ct), lambda j, i: (i, j)),
        out_shape=jax.ShapeDtypeStruct((t, c), F32),
        compiler_params=_params(("parallel", "parallel")))(xbc, w, b)


def ssd_conv_bwd(xbc, dparts, w, b, *, bsz, name):
    t, c = xbc.shape
    seq, ct, k = t // bsz, 256, w.shape[0]
    starts = [0]
    for p in dparts:
        starts.append(starts[-1] + p.shape[1] // ct)

    def body(x_ref, *refs):
        g_refs, (w_ref, b_ref, dx_ref, dw_ref, db_ref) = refs[:len(dparts)], refs[len(dparts):]
        j = pl.program_id(0)
        g = g_refs[-1][...]
        for n in range(len(dparts) - 2, -1, -1):
            g = jnp.where(j < starts[n + 1], g_refs[n][...], g)
        x, wv = x_ref[...], w_ref[...]
        dc = g * _dsilu(_conv(x, wv, b_ref[...]))
        dx, dw, db = _conv_bwd(x, dc, wv)
        dx_ref[...] = dx

        @pl.when(pl.program_id(1) == 0)
        def _():
            dw_ref[...] = jnp.zeros_like(dw_ref)
            db_ref[...] = jnp.zeros_like(db_ref)

        dw_ref[...] += dw
        db_ref[...] += db

    def part_spec(n):
        lo, hi = starts[n], starts[n + 1]

        def index(j, i):
            inside = (j >= lo) & (j < hi)
            return jnp.where(inside, i, 0), jnp.where(inside, j - lo, 0)

        return pl.BlockSpec((seq, ct), index)

    blk = pl.BlockSpec((seq, ct), lambda j, i: (i, j))
    wspec, bspec = pl.BlockSpec((k, ct), lambda j, i: (0, j)), pl.BlockSpec((1, ct), lambda j, i: (0, j))
    return pl.pallas_call(
        body, name=name, grid=(c // ct, bsz),
        in_specs=[blk] + [part_spec(n) for n in range(len(dparts))] + [wspec, bspec], out_specs=[blk, wspec, bspec],
        out_shape=[jax.ShapeDtypeStruct((t, c), F32), jax.ShapeDtypeStruct((k, c), F32),
                   jax.ShapeDtypeStruct((1, c), F32)],
        compiler_params=_params(("parallel", "arbitrary")))(xbc, *dparts, w, b)


def _ffn_specs(seq, ct, k, nblk):
    val = pl.BlockSpec((seq, ct), lambda j, i: (i, j))
    gate = pl.BlockSpec((seq, ct), lambda j, i: (i, nblk + j))
    wv, wg = pl.BlockSpec((k, ct), lambda j, i: (0, j)), pl.BlockSpec((k, ct), lambda j, i: (0, nblk + j))
    bv, bg = pl.BlockSpec((1, ct), lambda j, i: (0, j)), pl.BlockSpec((1, ct), lambda j, i: (0, nblk + j))
    return val, gate, wv, wg, bv, bg


def ffn_act_fwd(up, w, b, *, bsz, name):
    t = up.shape[0]
    half = up.shape[1] // 2
    seq, ct, k = t // bsz, 256, w.shape[0]
    val, gate, wv, wg, bv, bg = _ffn_specs(seq, ct, k, half // ct)

    def body(v_ref, g_ref, wv_ref, wg_ref, bv_ref, bg_ref, o_ref):
        vc = _conv(v_ref[...].astype(F32), wv_ref[...], bv_ref[...])
        gc = _conv(g_ref[...].astype(F32), wg_ref[...], bg_ref[...])
        o_ref[...] = (_silu(gc) * vc).astype(BF16)

    return pl.pallas_call(
        body, name=name, grid=(half // ct, bsz), in_specs=[val, gate, wv, wg, bv, bg], out_specs=val,
        out_shape=jax.ShapeDtypeStruct((t, half), BF16),
        compiler_params=_params(("parallel", "parallel")))(up, up, w, w, b, b)


def ffn_act_bwd(up, dact, w, b, *, bsz, name):
    t = up.shape[0]
    half = up.shape[1] // 2
    seq, ct, k = t // bsz, 256, w.shape[0]
    val, gate, wv, wg, bv, bg = _ffn_specs(seq, ct, k, half // ct)

    def body(v_ref, g_ref, wv_ref, wg_ref, bv_ref, bg_ref, d_ref, dv_ref, dg_ref, dwv_ref, dwg_ref, dbv_ref, dbg_ref):
        v, g = v_ref[...].astype(F32), g_ref[...].astype(F32)
        vc = _conv(v, wv_ref[...], bv_ref[...])
        gc = _conv(g, wg_ref[...], bg_ref[...])
        d = d_ref[...].astype(F32)
        sg = jax.nn.sigmoid(gc)
        dv, dwv, dbv = _conv_bwd(v, d * (gc * sg), wv_ref[...])
        dg, dwg, dbg = _conv_bwd(g, d * vc * (sg * (1.0 + gc * (1.0 - sg))), wg_ref[...])
        dv_ref[...] = dv.astype(BF16)
        dg_ref[...] = dg.astype(BF16)

        @pl.when(pl.program_id(1) == 0)
        def _():
            for r in (dwv_ref, dwg_ref, dbv_ref, dbg_ref):
                r[...] = jnp.zeros_like(r)

        dwv_ref[...] += dwv
        dwg_ref[...] += dwg
        dbv_ref[...] += dbv
        dbg_ref[...] += dbg

    return pl.pallas_call(
        body, name=name, grid=(half // ct, bsz), in_specs=[val, gate, wv, wg, bv, bg, val],
        out_specs=[val, val, wv, wv, bv, bv],
        out_shape=[jax.ShapeDtypeStruct((t, half), BF16), jax.ShapeDtypeStruct((t, half), BF16),
                   jax.ShapeDtypeStruct((k, half), F32), jax.ShapeDtypeStruct((k, half), F32),
                   jax.ShapeDtypeStruct((1, half), F32), jax.ShapeDtypeStruct((1, half), F32)],
        compiler_params=_params(("parallel", "arbitrary")))(up, up, w, w, b, b, dact)


def _sel_row(a, h):
    oh = (lax.broadcasted_iota(jnp.int32, (a.shape[0], 1), 0) == h).astype(F32)
    return jnp.sum(a * oh, axis=0, keepdims=True)


def _ssd_chunk(xp, dtr, bm, cm, prev, bias_r, alog_r, dskip_r, rev):
    q = dtr.shape[1]
    ri = lax.broadcasted_iota(jnp.int32, (q, q), 0)
    ci = lax.broadcasted_iota(jnp.int32, (q, q), 1)
    mask = (ci >= ri) if rev else (ci <= ri)
    lane_lo, row_lo = ci < HDIM, ri < HDIM
    dt_r = _softplus(dtr + bias_r)
    dta_r = dt_r * (-jnp.exp(alog_r))
    cs_r = cum_row(dta_r, rev)
    scores = dot_nt(cm, bm)

    def per_row(v):
        return jnp.broadcast_to(v, (q, q)).T

    assert len(xp) == 2
    y_diag, csqs, decayed, tots = [], [], [], []
    for p in range(2):
        ha = 2 * p + (HPG if rev else 0)
        hb = ha + 1
        cs_a, cs_b = _sel_row(cs_r, ha), _sel_row(cs_r, hb)
        csq_a, csq_b = per_row(cs_a), per_row(cs_b)
        seg_a = jnp.exp(jnp.where(mask, csq_a - cs_a, -1e30))
        seg_b = jnp.exp(jnp.where(mask, csq_b - cs_b, -1e30))
        csq = jnp.where(lane_lo, csq_a, csq_b)
        xdt = xp[p] * jnp.where(lane_lo, per_row(_sel_row(dt_r, ha)), per_row(_sel_row(dt_r, hb)))
        tot_a = jnp.sum(_sel_row(dta_r, ha), axis=1, keepdims=True)
        tot_b = jnp.sum(_sel_row(dta_r, hb), axis=1, keepdims=True)
        y_diag.append(jnp.where(lane_lo, *dot2_nn(scores * seg_a, scores * seg_b, xdt)))
        csqs.append(csq)
        decayed.append(xdt * jnp.exp(jnp.where(lane_lo, tot_a, tot_b) - csq))
        tots.append((tot_a, tot_b, ha, hb))
    y_off = dot_nt2(cm, *prev)
    states = dot_tn2(*decayed, bm)
    ys, news = [], []
    for p, (tot_a, tot_b, ha, hb) in enumerate(tots):
        y = y_diag[p] + y_off[p] * jnp.exp(csqs[p])
        if not rev:
            y = y + jnp.where(lane_lo, _sel_row(dskip_r, ha), _sel_row(dskip_r, hb)) * xp[p]
        ys.append(y)
        news.append(jnp.exp(jnp.where(row_lo, tot_a, tot_b)) * prev[p] + states[p])
    return tuple(ys), tuple(news)


NPAIR = HPG // 2


def _ssd_specs(seq, nc):
    xs = pl.BlockSpec((None, seq, HPG * HDIM), lambda b, g: (b, 0, g))
    bm = pl.BlockSpec((None, seq, NSTATE), lambda b, g: (b, 0, SSD_W // NSTATE + g))
    cm = pl.BlockSpec((None, seq, NSTATE), lambda b, g: (b, 0, SSD_W // NSTATE + SGROUPS + g))
    dtr = pl.BlockSpec((None, None, 2 * HPG, seq), lambda b, g: (b, g, 0, 0))
    pr = pl.BlockSpec((None, 2 * HPG, 1), lambda b, g: (g, 0, 0))
    st = pl.BlockSpec((None, None, 2, nc, NPAIR, 2 * HDIM, NSTATE), lambda b, g: (b, g, 0, 0, 0, 0, 0))
    return xs, bm, cm, dtr, pr, st


def _pair_cols(p):
    return slice(2 * HDIM * p, 2 * HDIM * (p + 1))


def ssd_scan_fwd(act, dtr, prs, *, name):
    bsz, seq, _ = act.shape
    nc = seq // QC
    xs, bm, cm, dtrs, pr, st = _ssd_specs(seq, nc)

    def body(x_ref, b_ref, c_ref, dtr_ref, br_ref, ar_ref, dk_ref, y_ref, st_ref):
        par = (br_ref[...], ar_ref[...], dk_ref[...])
        y_ref[...] = jnp.zeros_like(y_ref)

        def step(i, carry):
            new = []
            for rev in (False, True):
                k = (nc - 1 - i) if rev else i
                rows = pl.ds(pl.multiple_of(k * QC, QC), QC)
                xp = tuple(x_ref[rows, _pair_cols(p)] for p in range(NPAIR))
                for p in range(NPAIR):
                    st_ref[int(rev), k, p] = carry[rev][p]
                ys, nw = _ssd_chunk(xp, dtr_ref[:, rows], b_ref[rows, :], c_ref[rows, :], carry[rev], *par, rev)
                for p in range(NPAIR):
                    y_ref[rows, _pair_cols(p)] += ys[p]
                new.append(nw)
            return tuple(new)

        zero = tuple(jnp.zeros((2 * HDIM, NSTATE), F32) for _ in range(NPAIR))
        lax.fori_loop(0, nc // 2, lambda i, c: step(2 * i + 1, step(2 * i, c)), (zero, zero))

    return pl.pallas_call(
        body, name=name, grid=(bsz, SGROUPS), in_specs=[xs, bm, cm, dtrs, pr, pr, pr], out_specs=[xs, st],
        out_shape=[jax.ShapeDtypeStruct((bsz, seq, SSD_W), F32),
                   jax.ShapeDtypeStruct((bsz, SGROUPS, 2, nc, NPAIR, 2 * HDIM, NSTATE), F32)],
        compiler_params=_params(("parallel", "parallel")))(act, act, act, dtr, *prs)


def ssd_scan_bwd(act, dtr, prs, states, dy, *, name):
    bsz, seq, _ = act.shape
    nc = seq // QC
    xs, bm, cm, dtrs, pr, st = _ssd_specs(seq, nc)
    grp = pl.BlockSpec((None, seq, NSTATE), lambda b, g: (b, 0, g))
    dpr = pl.BlockSpec((None, None, 2 * HPG, 1), lambda b, g: (b, g, 0, 0))

    def body(x_ref, b_ref, c_ref, dtr_ref, br_ref, ar_ref, dk_ref, st_ref, dy_ref,
             dx_ref, db_ref, dc_ref, ddtr_ref, gbr_ref, gar_ref, gdk_ref):
        par = (br_ref[...], ar_ref[...], dk_ref[...])
        pgrads = (gbr_ref, gar_ref, gdk_ref)
        for r in pgrads + (dx_ref, db_ref, dc_ref, ddtr_ref):
            r[...] = jnp.zeros_like(r)

        def bstep(i, dcarry):
            new = []
            for rev in (False, True):
                k = i if rev else (nc - 1 - i)
                rows = pl.ds(pl.multiple_of(k * QC, QC), QC)
                xp = tuple(x_ref[rows, _pair_cols(p)] for p in range(NPAIR))
                prev = tuple(st_ref[int(rev), k, p] for p in range(NPAIR))
                _, pull = jax.vjp(functools.partial(_ssd_chunk, rev=rev), xp, dtr_ref[:, rows], b_ref[rows, :],
                                  c_ref[rows, :], prev, *par)
                dyp = tuple(dy_ref[rows, _pair_cols(p)] for p in range(NPAIR))
                gx, gdt, gb, gc, gprev, *gpar = pull((dyp, dcarry[rev]))
                for p in range(NPAIR):
                    dx_ref[rows, _pair_cols(p)] += gx[p]
                ddtr_ref[:, rows] += gdt
                db_ref[rows, :] += gb
                dc_ref[rows, :] += gc
                for r, g in zip(pgrads, gpar):
                    r[...] += g
                new.append(gprev)
            return tuple(new)

        zero = tuple(jnp.zeros((2 * HDIM, NSTATE), F32) for _ in range(NPAIR))
        lax.fori_loop(0, nc, bstep, (zero, zero))

    out_shape = [jax.ShapeDtypeStruct((bsz, seq, SSD_W), F32),
                 jax.ShapeDtypeStruct((bsz, seq, SGROUPS * NSTATE), F32),
                 jax.ShapeDtypeStruct((bsz, seq, SGROUPS * NSTATE), F32),
                 jax.ShapeDtypeStruct(dtr.shape, F32)]
    out_shape += [jax.ShapeDtypeStruct((bsz, SGROUPS, 2 * HPG, 1), F32)] * 3
    return pl.pallas_call(
        body, name=name, grid=(bsz, SGROUPS), in_specs=[xs, bm, cm, dtrs, pr, pr, pr, st, xs],
        out_specs=[xs, grp, grp, dtrs, dpr, dpr, dpr], out_shape=out_shape,
        compiler_params=_params(("parallel", "parallel")))(act, act, act, dtr, *prs, states, dy)


def _s5_core(lam_re, lam_im, log_step, b_re, b_im, c_re, c_im):
    q = S5_Q
    step = jnp.exp(log_step)[:, None]
    lr, li = lam_re * step, lam_im * step
    mag = jnp.exp(lr)
    ar, ai = mag * jnp.cos(li), mag * jnp.sin(li)
    den = lam_re * lam_re + lam_im * lam_im
    cr = ((ar - 1.0) * lam_re + ai * lam_im) / den
    ci = (ai * lam_re - (ar - 1.0) * lam_im) / den
    bbr = cr[..., None] * b_re - ci[..., None] * b_im
    bbi = cr[..., None] * b_im + ci[..., None] * b_re
    d = jnp.arange(q + 1, dtype=F32)[None, :, None]
    pm = jnp.exp(d * lr[:, None, :])
    pr, pi = pm * jnp.cos(d * li[:, None, :]), pm * jnp.sin(d * li[:, None, :])
    er = pr[..., None] * bbr[:, None] - pi[..., None] * bbi[:, None]
    ei = pr[..., None] * bbi[:, None] + pi[..., None] * bbr[:, None]
    hp = lax.Precision.HIGHEST
    k = (jnp.einsum('gcp,gdpz->gdcz', c_re, er[:, :q], precision=hp)
         - jnp.einsum('gcp,gdpz->gdcz', c_im, ei[:, :q], precision=hp))
    e = jnp.concatenate([er[:, :q], ei[:, :q]], axis=2)
    p1r, p1i = pr[:, 1:], pi[:, 1:]
    m_re = c_re[:, None] * p1r[:, :, None, :] - c_im[:, None] * p1i[:, :, None, :]
    m_im = -c_re[:, None] * p1i[:, :, None, :] - c_im[:, None] * p1r[:, :, None, :]
    da = jnp.concatenate([pr[:, q], pr[:, q]], axis=-1)
    db = jnp.concatenate([-pi[:, q], pi[:, q]], axis=-1)
    return k, e, jnp.concatenate([m_re, m_im], axis=-1), da, db


def _s5_operators(lf_re, lf_im, lsf, lb_re, lb_im, lsb, b_re, b_im, cf_re, cf_im, cb_re, cb_im):
    g = lf_re.shape[0]
    both = lambda f, b: jnp.concatenate([f, b], axis=0)
    k, e, m, da, db = _s5_core(both(lf_re, lb_re), both(lf_im, lb_im), both(lsf, lsb), both(b_re, b_re),
                               both(b_im, b_im), both(cf_re, cb_re), both(cf_im, cb_im))
    kf, kb = k[:g], k[g:]
    wtf, wtb = jnp.transpose(e[:g, ::-1], (0, 1, 3, 2)), jnp.transpose(e[g:], (0, 1, 3, 2))
    mtf, mtb = jnp.transpose(m[:g], (0, 3, 1, 2)), jnp.transpose(m[g:, ::-1], (0, 3, 1, 2))
    daf, dab, dbf, dbb = da[:g], da[g:], db[:g], db[g:]
    lags = jnp.concatenate([kb[:, :0:-1], kf[:, :1] + kb[:, :1], kf[:, 1:]], axis=1)
    tt = jnp.transpose(lags, (0, 1, 3, 2))
    wt = jnp.concatenate([wtf.reshape(g, S5_QC, 2 * S5_P), wtb.reshape(g, S5_QC, 2 * S5_P)], axis=-1)
    mt = jnp.concatenate([mtf.reshape(g, 2 * S5_P, S5_QC), mtb.reshape(g, 2 * S5_P, S5_QC)], axis=1)
    return tt, wt, mt, jnp.concatenate([daf, dab], -1), jnp.concatenate([dbf, dbb], -1)


def _gspec(*shape):
    return pl.BlockSpec((None,) + shape, lambda g: (g,) + (0,) * len(shape))


S5_HALVES = S5_QC // LANES


def _toeplitz_block(s, t):
    per = LANES // S5_C
    return t // per, slice(s * S5_C, (s + 1) * S5_C), slice((t % per) * S5_C, (t % per + 1) * S5_C)


def s5_toeplitz(kt, *, name):
    g = kt.shape[0]

    def body(k_ref, t_ref):
        for s in range(S5_Q):
            for t in range(S5_Q):
                t_ref[_toeplitz_block(s, t)] = k_ref[t - s + S5_Q - 1]

    return pl.pallas_call(
        body, name=name, grid=(g,), in_specs=[_gspec(2 * S5_Q - 1, S5_C, S5_C)],
        out_specs=_gspec(S5_HALVES, S5_QC, LANES), out_shape=jax.ShapeDtypeStruct((g, S5_HALVES, S5_QC, LANES), F32),
        compiler_params=_params(("parallel",)))(kt)


def s5_toeplitz_bwd(dtt, *, name):
    g = dtt.shape[0]

    def body(d_ref, k_ref):
        for j in range(2 * S5_Q - 1):
            acc = None
            for s in range(S5_Q):
                t = j - (S5_Q - 1) + s
                if 0 <= t < S5_Q:
                    blk = d_ref[_toeplitz_block(s, t)]
                    acc = blk if acc is None else acc + blk
            k_ref[j] = acc

    return pl.pallas_call(
        body, name=name, grid=(g,), in_specs=[_gspec(S5_HALVES, S5_QC, LANES)],
        out_specs=_gspec(2 * S5_Q - 1, S5_C, S5_C), out_shape=jax.ShapeDtypeStruct((g, 2 * S5_Q - 1, S5_C, S5_C), F32),
        compiler_params=_params(("parallel",)))(dtt)


S5_RT = 64


def _chunk_piece(q):
    per = LANES // S5_C
    return q // per, slice((q % per) * S5_C, (q % per + 1) * S5_C)


def to_chunks(u, *, name):
    t = u.shape[0]
    r = t // S5_Q
    rt = min(S5_RT, r)

    per = LANES // S5_C
    nblk = S5_W // LANES

    def body(*refs):
        o_ref = refs[-1]
        for k in range(nblk):
            for q in range(S5_Q):
                rows = refs[k][pl.ds(q, rt, stride=S5_Q), :]
                half, lanes = _chunk_piece(q)
                for j in range(per):
                    o_ref[k * per + j, half, :, lanes] = rows[:, j * S5_C:(j + 1) * S5_C]

    return pl.pallas_call(
        body, name=name, grid=(r // rt,),
        in_specs=[pl.BlockSpec((rt * S5_Q, LANES), lambda i, k=k: (i, k)) for k in range(nblk)],
        out_specs=pl.BlockSpec((S5_G, S5_HALVES, rt, LANES), lambda i: (0, 0, i, 0)),
        out_shape=jax.ShapeDtypeStruct((S5_G, S5_HALVES, r, LANES), F32),
        compiler_params=_params(("parallel",)))(*[u] * nblk)


def from_chunks(y, *, name, add=None, as_blocks=False):
    r = y.shape[2]
    rt = min(S5_RT, r)
    per = LANES // S5_C

    nblk = S5_W // LANES

    def body(*refs):
        y_ref, tmp_ref = refs[0], refs[-1]
        adds, outs = refs[1:-1 - nblk], refs[-1 - nblk:-1]
        for k in range(nblk):
            for q in range(S5_Q):
                half, lanes = _chunk_piece(q)
                for j in range(per):
                    tmp_ref[:, j * S5_C:(j + 1) * S5_C] = y_ref[k * per + j, half, :, lanes]
                row = tmp_ref[...]
                if add is not None:
                    row = row + adds[k][pl.ds(q, rt, stride=S5_Q), :]
                outs[k][pl.ds(q, rt, stride=S5_Q), :] = row

    in_specs = [pl.BlockSpec((S5_G, S5_HALVES, rt, LANES), lambda i: (0, 0, i, 0))]
    if add is not None:
        in_specs += [pl.BlockSpec((rt * S5_Q, LANES), lambda i, k=k: (i, k)) for k in range(nblk)]
    blocks = pl.pallas_call(
        body, name=name, grid=(r // rt,), in_specs=in_specs,
        out_specs=[pl.BlockSpec((rt * S5_Q, LANES), lambda i: (i, 0))] * nblk,
        out_shape=[jax.ShapeDtypeStruct((r * S5_Q, LANES), F32)] * nblk,
        scratch_shapes=[pltpu.VMEM((rt, LANES), F32)],
        compiler_params=_params(("parallel",)))(*([y] if add is None else [y] + [add] * nblk))
    return list(blocks) if as_blocks else jnp.concatenate(blocks, axis=1)


def _cat(ref):
    return jnp.concatenate([ref[h] for h in range(S5_HALVES)], axis=1)


def _put(ref, v):
    for h in range(S5_HALVES):
        ref[h] = v[:, h * LANES:(h + 1) * LANES]


def _cspec(r):
    return _gspec(S5_HALVES, r, LANES)


def s5_state_in(u, wt, *, name):
    g, _, r, _ = u.shape

    def body(u_ref, w_ref, o_ref):
        o_ref[...] = _bd(_cat(u_ref), w_ref[...], 1, 0)

    return pl.pallas_call(
        body, name=name, grid=(g,), in_specs=[_cspec(r), _gspec(S5_QC, 4 * S5_P)],
        out_specs=_gspec(r, 4 * S5_P), out_shape=jax.ShapeDtypeStruct((g, r, 4 * S5_P), F32),
        compiler_params=_params(("parallel",)))(u, wt)


def _swap(h):
    return pltpu.roll(h, S5_P, 1)


def s5_carry_fwd(s, da, db, *, name):
    nck, rows, _ = s.shape
    w = 2 * S5_P

    def body(s_ref, da_ref, db_ref, h_ref):
        dirs = ((False, slice(0, w)), (True, slice(w, 2 * w)))
        coef = [(da_ref[:, cols], db_ref[:, cols]) for _, cols in dirs]

        def step(i, hs):
            new = []
            for (rev, cols), (a, b), h in zip(dirs, coef, hs):
                k = (nck - 1 - i) if rev else i
                h_ref[k, :, cols] = h
                new.append(a * h + b * _swap(h) + s_ref[k, :, cols])
            return tuple(new)

        z = jnp.zeros((rows, w), F32)
        lax.fori_loop(0, nck, step, (z, z), unroll=2)

    rt = min(2 * CARRY_ROWS, rows)
    big, small = pl.BlockSpec((nck, rt, 2 * w), lambda i: (0, i, 0)), pl.BlockSpec((rt, 2 * w), lambda i: (i, 0))
    rows = rt
    return pl.pallas_call(
        body, name=name, grid=(s.shape[1] // rt,), in_specs=[big, small, small], out_specs=big,
        out_shape=jax.ShapeDtypeStruct(s.shape, F32), compiler_params=_params(("parallel",)))(s, da, db)


def s5_carry_bwd(hin, dh, da, db, *, name):
    nck, rows, _ = hin.shape
    w = 2 * S5_P

    def body(h_ref, dh_ref, da_ref, db_ref, ds_ref, gda_ref, gdb_ref):
        dirs = ((False, slice(0, w)), (True, slice(w, 2 * w)))
        coef = [(da_ref[:, cols], db_ref[:, cols]) for _, cols in dirs]

        def step(i, carries):
            new = []
            for (rev, cols), (a, b), (g, ga, gb) in zip(dirs, coef, carries):
                k = i if rev else (nck - 1 - i)
                ds_ref[k, :, cols] = g
                h = h_ref[k, :, cols]
                new.append((dh_ref[k, :, cols] + a * g + _swap(b * g), ga + g * h, gb + g * _swap(h)))
            return tuple(new)

        z = jnp.zeros((rows, w), F32)
        res = lax.fori_loop(0, nck, step, ((z, z, z), (z, z, z)), unroll=2)
        for (_, cols), (_, ga, gb) in zip(dirs, res):
            gda_ref[:, cols] = ga
            gdb_ref[:, cols] = gb

    rt = min(CARRY_ROWS, rows)
    big, small = pl.BlockSpec((nck, rt, 2 * w), lambda i: (0, i, 0)), pl.BlockSpec((rt, 2 * w), lambda i: (i, 0))
    rows = rt
    return pl.pallas_call(
        body, name=name, grid=(hin.shape[1] // rt,), in_specs=[big, big, small, small], out_specs=[big, small, small],
        out_shape=[jax.ShapeDtypeStruct(hin.shape, F32), jax.ShapeDtypeStruct(da.shape, F32),
                   jax.ShapeDtypeStruct(da.shape, F32)],
        compiler_params=_params(("parallel",)))(hin, dh, da, db)


def s5_out(u, hin, tt, mt, *, name):
    g, _, r, _ = u.shape

    def body(u_ref, h_ref, t_ref, m_ref, o_ref):
        u_v, h_v = _cat(u_ref), h_ref[...]
        for half in range(S5_HALVES):
            cols = slice(half * LANES, (half + 1) * LANES)
            o_ref[half] = _bd(u_v, t_ref[half], 1, 0) + _bd(h_v, m_ref[:, cols], 1, 0)

    return pl.pallas_call(
        body, name=name, grid=(g,),
        in_specs=[_cspec(r), _gspec(r, 4 * S5_P), _gspec(S5_HALVES, S5_QC, LANES), _gspec(4 * S5_P, S5_QC)],
        out_specs=_cspec(r), out_shape=jax.ShapeDtypeStruct((g, S5_HALVES, r, LANES), F32),
        compiler_params=_params(("parallel",)))(u, hin, tt, mt)


def s5_out_bwd(dy, u, hin, tt, mt, *, name):
    g, _, r, _ = u.shape

    def body(dy_ref, u_ref, h_ref, t_ref, m_ref, dh_ref, dt_ref, dm_ref, du_ref):
        dy_v, u_v = _cat(dy_ref), _cat(u_ref)
        dh_ref[...] = _bd(dy_v, m_ref[...], 1, 1)
        dm_ref[...] = _bd(h_ref[...], dy_v, 0, 0)
        du = None
        for half in range(S5_HALVES):
            dy_h = dy_ref[half]
            dt_ref[half] = _bd(u_v, dy_h, 0, 0)
            part = _bd(dy_h, t_ref[half], 1, 1)
            du = part if du is None else du + part
        _put(du_ref, du)

    tspec = _gspec(S5_HALVES, S5_QC, LANES)
    return pl.pallas_call(
        body, name=name, grid=(g,),
        in_specs=[_cspec(r), _cspec(r), _gspec(r, 4 * S5_P), tspec, _gspec(4 * S5_P, S5_QC)],
        out_specs=[_gspec(r, 4 * S5_P), tspec, _gspec(4 * S5_P, S5_QC), _cspec(r)],
        out_shape=[jax.ShapeDtypeStruct((g, r, 4 * S5_P), F32), jax.ShapeDtypeStruct((g, S5_HALVES, S5_QC, LANES), F32),
                   jax.ShapeDtypeStruct((g, 4 * S5_P, S5_QC), F32), jax.ShapeDtypeStruct((g, S5_HALVES, r, LANES), F32)],
        compiler_params=_params(("parallel",)))(dy, u, hin, tt, mt)


def s5_state_in_bwd(ds, u, wt, du1, *, name):
    g, _, r, _ = u.shape

    def body(ds_ref, u_ref, w_ref, du1_ref, du_ref, dw_ref):
        ds_v = ds_ref[...]
        _put(du_ref, _cat(du1_ref) + _bd(ds_v, w_ref[...], 1, 1))
        dw_ref[...] = _bd(_cat(u_ref), ds_v, 0, 0)

    return pl.pallas_call(
        body, name=name, grid=(g,),
        in_specs=[_gspec(r, 4 * S5_P), _cspec(r), _gspec(S5_QC, 4 * S5_P), _cspec(r)],
        out_specs=[_cspec(r), _gspec(S5_QC, 4 * S5_P)],
        out_shape=[jax.ShapeDtypeStruct((g, S5_HALVES, r, LANES), F32), jax.ShapeDtypeStruct((g, S5_QC, 4 * S5_P), F32)],
        compiler_params=_params(("parallel",)))(ds, u, wt, du1)


def _s5_post(ypre, u, dvec, wv, wg, bv, bg, nw):
    g = _gelu(ypre + dvec * u)
    out = (dot_nn(g, wv) + bv) * jax.nn.sigmoid(dot_nn(g, wg) + bg)
    return (_rms(out, nw),)


def _ssd_post(y, z, nw):
    return (_rms(y * _silu(z), nw),)


def _to_carry(s, bsz):
    nck = s.shape[1] // bsz
    return jnp.transpose(s.reshape(S5_G, bsz, nck, -1), (2, 0, 1, 3)).reshape(nck, S5_G * bsz, -1)


def _from_carry(h, bsz):
    nck = h.shape[0]
    return jnp.transpose(h.reshape(nck, S5_G, bsz, -1), (1, 2, 0, 3)).reshape(S5_G, bsz * nck, -1)


def _block_diag(w):
    eye = jnp.eye(S5_G, dtype=w.dtype)
    return jnp.einsum('gcd,gh->gchd', w, eye).reshape(S5_W, S5_W)


def _diag_blocks(w):
    v = w.reshape(S5_G, S5_C, S5_G, S5_C)
    return v[jnp.arange(S5_G), :, jnp.arange(S5_G), :]


def _dt_rows(dt, bsz):
    seq = dt.shape[0] // bsz
    return jnp.transpose(dt.reshape(bsz, seq, 2, SGROUPS, HPG), (0, 3, 2, 4, 1)).reshape(bsz, SGROUPS, 2 * HPG, seq)


def _dt_from_rows(dr):
    bsz, _, _, seq = dr.shape
    return jnp.transpose(dr.reshape(bsz, SGROUPS, 2, HPG, seq), (0, 4, 2, 1, 3)).reshape(bsz * seq, 2 * HEADS)


def _head_params(f, b):
    return jnp.concatenate([f.reshape(SGROUPS, HPG), b.reshape(SGROUPS, HPG)], axis=1)[:, :, None]


def _head_grads(gr):
    v = gr.sum(0)[:, :, 0]
    return v[:, :HPG].reshape(HEADS), v[:, HPG:].reshape(HEADS)


def local_step(x, target, w):
    bsz, seq, d = x.shape
    t = bsz * seq
    x2, tgt2 = x.reshape(t, d), target.reshape(t, d)
    g = {}
    row = lambda v: v.reshape(1, -1)
    bf = lambda v: v.astype(BF16)

    w_in = _unshard(bf(w['w_in']), SHARDED['w_in'])
    cuts = [0, SSD_W, SSD_W + XBC_W, SSD_W + XBC_W + 2 * HEADS, w_in.shape[1]]
    w_in_parts = [w_in[:, a:b] for a, b in zip(cuts[:-1], cuts[1:])]
    norm_mix = row(w['norm_mix_w']) + w.get('token', 0.0)
    (hn,) = rowmap_fwd(lambda a, nw: (_rms(a, nw),), [x2], [norm_mix], [(d, BF16)], tm=512, name="rms_mix")
    z, xbc, dt, u = matmul_multi(hn, w_in_parts, name="in_proj")

    conv_w, conv_b = _unshard(w['ssd_conv_w'], SHARDED['ssd_conv_w']), row(w['ssd_conv_b'])
    act = ssd_conv_fwd(xbc, conv_w, conv_b, bsz=bsz, name="ssd_conv")
    dtr = _dt_rows(dt, bsz)
    prs = (_head_params(w['ssd_dt_bias_fwd'], w['ssd_dt_bias_bwd']),
           _head_params(w['ssd_a_log_fwd'], w['ssd_a_log_bwd']),
           _head_params(w['ssd_d'], jnp.zeros_like(w['ssd_d'])))
    act3 = act.reshape(bsz, seq, XBC_W)
    y_scan, ssd_states = ssd_scan_fwd(act3, dtr, prs, name="ssd_scan")
    y_scan = y_scan.reshape(t, SSD_W)
    ssd_nw = row(w['ssd_norm_w'])
    (y_ssd,) = rowmap_fwd(_ssd_post, [y_scan, z], [ssd_nw], [(SSD_W, BF16)], tm=512, name="ssd_post")

    s5_names = ['s5_lambda_re_fwd', 's5_lambda_im_fwd', 's5_log_step_fwd', 's5_lambda_re_bwd', 's5_lambda_im_bwd',
                's5_log_step_bwd', 's5_b_re', 's5_b_im', 's5_c_re_fwd', 's5_c_im_fwd', 's5_c_re_bwd', 's5_c_im_bwd']
    (kt, wt, mt, da, db), s5_pull = jax.vjp(_s5_operators, *[w[n] for n in s5_names])
    tt_b, wt_b, mt_b = s5_toeplitz(kt, name="s5_toeplitz"), bf(wt), bf(mt)
    da_r, db_r = jnp.repeat(da, bsz, axis=0), jnp.repeat(db, bsz, axis=0)
    uc = to_chunks(u, name="s5_to_chunks_u")
    s_in = _to_carry(s5_state_in(uc, wt_b, name="s5_state_in"), bsz)
    hin_c = s5_carry_fwd(s_in, da_r, db_r, name="s5_carry")
    hin = _from_carry(hin_c, bsz)
    ypre = from_chunks(s5_out(uc, hin, tt_b, mt_b, name="s5_out"), name="s5_from_chunks_y", as_blocks=True)
    glu_w = w['s5_glu_w']
    s5_par = [row(w['s5_d']), _block_diag(glu_w[:, :, :S5_C]), _block_diag(glu_w[:, :, S5_C:]),
              row(w['s5_glu_b'][:, :S5_C]), row(w['s5_glu_b'][:, S5_C:]), row(w['s5_norm_w'])]
    (y_s5,) = rowmap_fwd(_s5_post, [ypre, u], s5_par, [(S5_W, BF16)], name="s5_post")

    if 'late' in w:
        w = {**w, **w['late'](y_s5)}
    w_out = bf(w['w_out']).reshape(SSD_W + S5_W, d)
    norm_ffn = row(w['norm_ffn_w'])
    h1, hn2 = matmul_sum([y_ssd, y_s5], [w_out[:SSD_W], w_out[SSD_W:]], add=x2, norm_w=norm_ffn, name="out_proj")
    pad_c = FFN_PAD - FFN_BLK
    half = N_DEV // 2
    w_up3 = jnp.pad(bf(w['ffn_w_up']), ((0, 0), (0, 0), (0, pad_c)))
    w_down = jnp.pad(bf(w['ffn_w_down']).reshape(half, FFN_BLK, d), ((0, 0), (0, pad_c), (0, 0)))
    w_down = w_down.reshape(half * FFN_PAD, d)
    fconv_w = jnp.pad(w['ffn_conv_w'], ((0, 0), (0, 0), (0, pad_c)))
    fconv_w = jnp.transpose(fconv_w, (1, 0, 2)).reshape(FCONV, N_DEV * FFN_PAD)
    fconv_b = row(jnp.pad(w['ffn_conv_b'].reshape(N_DEV, FFN_BLK), ((0, 0), (0, pad_c))))
    up = matmul_cols(hn2, w_up3, out_dtype=BF16, name="ffn_up")
    fact = ffn_act_fwd(up, fconv_w, fconv_b, bsz=bsz, name="ffn_act")
    loss, dh2, g_nf = loss_head(h1, tgt2, row(w['norm_final_w']), matmul=(fact, w_down), tm=512, name="ffn_down_loss")
    g['norm_final_w'] = g_nf.reshape(-1)

    dfact = matmul_sum([dh2], [w_down], nt=True, tm=1024, name="ffn_down_dx")
    g_down = matmul_tn(fact, dh2, name="ffn_down_dw").reshape(half, FFN_PAD, d)[:, :FFN_BLK]
    g['ffn_w_down'] = g_down.reshape(N_DEV, FFN_BLK // 2, d)
    dval, dgate, dwv, dwg, dbv, dbg = ffn_act_bwd(up, dfact, fconv_w, fconv_b, bsz=bsz, name="ffn_act_bwd")
    g_cw = jnp.concatenate([dwv, dwg], axis=1).reshape(FCONV, N_DEV, FFN_PAD)[:, :, :FFN_BLK]
    g['ffn_conv_w'] = jnp.transpose(g_cw, (1, 0, 2))
    g['ffn_conv_b'] = jnp.concatenate([dbv, dbg], axis=1).reshape(N_DEV, FFN_PAD)[:, :FFN_BLK].reshape(-1)
    windows = [(dval, FFN_PAD, p) for p in range(half)] + [(dgate, FFN_PAD, p) for p in range(half)]
    g['ffn_w_up'] = jnp.concatenate([matmul_tn(hn2, dval, out_blocks=half, name="ffn_up_dw_val"),
                                     matmul_tn(hn2, dgate, out_blocks=half, name="ffn_up_dw_gate")],
                                    axis=0)[:, :, :FFN_BLK]
    send_early = w.get('on_grads')
    if send_early:
        norm_ffn = norm_ffn + send_early(g, ['ffn_w_up', 'ffn_w_down'])
    dh1, g_nffn = matmul_sum(windows, [(w_up3, p) for p in range(N_DEV)], nt=True, tm=256,
                             norm_bwd=(h1, norm_ffn, dh2), name="ffn_up_dx")
    g['norm_ffn_w'] = g_nffn.reshape(-1)

    dycat = matmul_sum([dh1], [w_out], nt=True, tm=1024, name="out_proj_dx")
    g['w_out'] = jnp.concatenate([matmul_tn(y_ssd, dh1, name="out_proj_dw_ssd"),
                                  matmul_tn(y_s5, dh1, name="out_proj_dw_s5")], axis=0).reshape(w['w_out'].shape)
    if send_early:
        ssd_nw = ssd_nw + send_early(g, ['w_out'])
    dy_scan, dz, g_snw = rowmap_bwd(_ssd_post, [y_scan, z], [ssd_nw], [(dycat, SSD_W, 0)], tm=512,
                                    name="ssd_post_bwd")
    g['ssd_norm_w'] = g_snw.reshape(-1)
    dypre, du_a, g_d, g_wv, g_wg, g_bv, g_bg, g_s5nw = rowmap_bwd(
        _s5_post, [ypre, u], s5_par, [(dycat, S5_W, SSD_W // S5_W)], name="s5_post_bwd")
    g['s5_d'], g['s5_norm_w'] = g_d.reshape(-1), g_s5nw.reshape(-1)
    g['s5_glu_w'] = jnp.concatenate([_diag_blocks(g_wv), _diag_blocks(g_wg)], axis=-1)
    g['s5_glu_b'] = jnp.concatenate([g_bv.reshape(S5_G, S5_C), g_bg.reshape(S5_G, S5_C)], axis=-1)

    dyc = to_chunks(dypre, name="s5_to_chunks_dy")
    dhin, dtt, dmt, du1 = s5_out_bwd(dyc, uc, hin, tt_b, mt_b, name="s5_out_bwd")
    ds_c, gda, gdb = s5_carry_bwd(hin_c, _to_carry(dhin, bsz), da_r, db_r, name="s5_carry_bwd")
    duc, dwt = s5_state_in_bwd(_from_carry(ds_c, bsz), uc, wt_b, du1, name="s5_state_in_bwd")
    du = from_chunks(duc, add=du_a, name="s5_from_chunks_du")
    fold = lambda v: v.reshape(S5_G, bsz, -1).sum(1)
    dkt = s5_toeplitz_bwd(dtt, name="s5_toeplitz_bwd")
    for n, gv in zip(s5_names, s5_pull((dkt, dwt, dmt, fold(gda), fold(gdb)))):
        g[n] = gv

    dxs, dbm, dcm, ddtr, gbr, gar, gdk = ssd_scan_bwd(
        act3, dtr, prs, ssd_states, dy_scan.reshape(bsz, seq, SSD_W), name="ssd_scan_bwd")
    g['ssd_dt_bias_fwd'], g['ssd_dt_bias_bwd'] = _head_grads(gbr)
    g['ssd_a_log_fwd'], g['ssd_a_log_bwd'] = _head_grads(gar)
    g['ssd_d'] = _head_grads(gdk)[0]
    dparts_act = [v.reshape(t, v.shape[-1]) for v in (dxs, dbm, dcm)]
    dxbc, g_cw, g_cb = ssd_conv_bwd(xbc, dparts_act, conv_w, conv_b, bsz=bsz, name="ssd_conv_bwd")
    g['ssd_conv_w'] = _shard_rows(g_cw, SHARDED['ssd_conv_w']).reshape(w['ssd_conv_w'].shape)
    g['ssd_conv_b'] = g_cb.reshape(-1)
    ddt = _dt_from_rows(ddtr)

    if send_early:
        ddt = ddt + send_early(g, [], loss=loss)
    dparts = [dz, dxbc, ddt, du]
    g_in = jnp.concatenate([matmul_tn(hn, dp, name=f"in_proj_dw_{i}") for i, dp in enumerate(dparts)], axis=1)
    g['w_in'] = _shard_rows(g_in, SHARDED['w_in']).reshape(w['w_in'].shape)
    if send_early:
        dparts[2] = ddt + send_early(g, ['w_in'])
    dx, g_nmix = matmul_sum(dparts, w_in_parts, nt=True, tm=256, norm_bwd=(x2, norm_mix, dh1), name="in_proj_dx")
    g['norm_mix_w'] = g_nmix.reshape(-1)
    return loss, dx.reshape(bsz, seq, d), g


ANY = pl.BlockSpec(memory_space=pl.ANY)


def all_gather(shards, *, name):
    n = len(shards)

    def body(*refs):
        x_refs, out_refs = refs[:n], refs[n:2 * n]
        send_sems, recv_sems, local_sems = refs[2 * n:]
        x, y, c = lax.axis_index("x"), lax.axis_index("y"), lax.axis_index("c")
        me, sibling = (x, y, c), (x, y, 1 - c)
        chips = [(1 - x, y), (x, 1 - y), (1 - x, 1 - y)]

        def copy(k, j, block, to, own=False):
            dst = out_refs[j].at[4 * block[0] + 2 * block[1] + block[2]]
            return pltpu.make_async_remote_copy(
                src_ref=x_refs[j] if own else dst, dst_ref=dst,
                send_sem=send_sems.at[k, j], recv_sem=recv_sems.at[k, j], device_id=to, device_id_type=MESH)

        mine = [pltpu.make_async_copy(x_refs[j], out_refs[j].at[4 * x + 2 * y + c], local_sems.at[j]) for j in range(n)]
        first = [copy(0, j, me, sibling, own=True) for j in range(n)]
        first += [copy(1 + i, j, me, (*chip, c), own=True) for i, chip in enumerate(chips) for j in range(n)]
        for cp in mine + first:
            cp.start()
        passed = []
        for i, chip in enumerate(chips):
            for j in range(n):
                copy(1 + i, j, (*chip, c), me).wait_recv()
                passed.append(copy(4 + i, j, (*chip, c), sibling))
                passed[-1].start()
        for j in range(n):
            copy(0, j, sibling, me).wait_recv()
        for i, chip in enumerate(chips):
            for j in range(n):
                copy(4 + i, j, (*chip, 1 - c), me).wait_recv()
        for cp in first + passed:
            cp.wait_send()
        for cp in mine:
            cp.wait()

    return pl.pallas_call(
        body, name=name, out_shape=[jax.ShapeDtypeStruct((N_DEV,) + s.shape, s.dtype) for s in shards],
        in_specs=[ANY] * n, out_specs=[ANY] * n,
        scratch_shapes=[pltpu.SemaphoreType.DMA((7, n)), pltpu.SemaphoreType.DMA((7, n)),
                        pltpu.SemaphoreType.DMA((n,))],
    )(*shards)


HBM_SPEC = pl.BlockSpec(memory_space=pltpu.HBM)
SEM_SPEC = pl.BlockSpec(memory_space=pltpu.SEMAPHORE)
SPLIT_PARAMS = pltpu.CompilerParams(has_side_effects=pltpu.SideEffectType.DATAFLOW_SIDE_EFFECTING)


def _peer_copies(src_refs, land_refs, send_sems, recv_sems, indexed):
    x, y, c = lax.axis_index("x"), lax.axis_index("y"), lax.axis_index("c")
    me = 4 * x + 2 * y + c
    copies = []
    for k in range(1, N_DEV):
        px = (1 - x) if k & 4 else x
        py = (1 - y) if k & 2 else y
        pc = (1 - c) if k & 1 else c
        for j, (src, land) in enumerate(zip(src_refs, land_refs)):
            sem = (k - 1) * len(src_refs) + j
            copies.append(pltpu.make_async_remote_copy(
                src_ref=src.at[4 * px + 2 * py + pc] if indexed else src, dst_ref=land.at[me],
                send_sem=send_sems.at[sem], recv_sem=recv_sems.at[sem],
                device_id=(px, py, pc), device_id_type=MESH))
    return copies


def scatter_start(srcs, *, name, indexed):
    n = len(srcs)
    lands = [lax.empty(s.shape if indexed else (N_DEV,) + s.shape, s.dtype) for s in srcs]

    def body(*refs):
        send_sems, recv_sems = refs[2 * n], refs[2 * n + 1]
        for cp in _peer_copies(refs[:n], refs[n:2 * n], send_sems, recv_sems, indexed):
            cp.start()
        refs[-1][...] = jnp.zeros_like(refs[-1])

    hbm = lambda a: pltpu.HBM(a.shape, a.dtype)
    sems = pltpu.SemaphoreType.DMA(((N_DEV - 1) * n,))
    res = pl.pallas_call(
        body, name=name,
        out_shape=(sems, sems, *[hbm(a) for a in srcs + lands], jax.ShapeDtypeStruct((8, LANES), F32)),
        in_specs=[HBM_SPEC] * (2 * n),
        out_specs=(SEM_SPEC, SEM_SPEC, *[HBM_SPEC] * (2 * n), pl.BlockSpec(memory_space=pltpu.VMEM)),
        input_output_aliases={i: 2 + i for i in range(2 * n)}, compiler_params=SPLIT_PARAMS,
    )(*[pltpu.with_memory_space_constraint(a, pltpu.HBM) for a in srcs + lands])
    return res[0], res[1], list(res[2:2 + n]), list(res[2 + n:2 + 2 * n]), res[-1]


def scatter_wait(send_sems, recv_sems, srcs, lands, after, *, name, indexed):
    n = len(srcs)

    def body(*refs):
        for cp in _peer_copies(refs[:n], refs[n:2 * n], refs[2 * n], refs[2 * n + 1], indexed):
            cp.wait_send()
            cp.wait_recv()

    hbm = lambda a: pltpu.HBM(a.shape, a.dtype)
    res = pl.pallas_call(
        body, name=name, out_shape=tuple(hbm(a) for a in srcs + lands),
        in_specs=[HBM_SPEC] * (2 * n) + [SEM_SPEC, SEM_SPEC, ANY], out_specs=tuple([HBM_SPEC] * (2 * n)),
        input_output_aliases={i: i for i in range(2 * n)}, compiler_params=SPLIT_PARAMS,
    )(*srcs, *lands, send_sems, recv_sems, after)
    return list(res[:n]), list(res[n:])


def _adam_rows(r, c):
    fits = [t for t in range(8, r + 1, 8) if r % t == 0 and N_DEV * t * c * 4 <= 6 * 2 ** 20]
    return max(fits) if fits else r


def adamw(recv, w, m, v, *, name):
    _, r, n = recv.shape
    tr = _adam_rows(r, n)

    def body(r_ref, w_ref, m_ref, v_ref, g_ref, d_ref, nm_ref, nv_ref):
        g = r_ref[0].astype(F32)
        for s in range(1, N_DEV):
            g = g + r_ref[s].astype(F32)
        m_new = ADAM_B1 * m_ref[...] + (1.0 - ADAM_B1) * g
        v_new = ADAM_B2 * v_ref[...] + (1.0 - ADAM_B2) * jnp.square(g)
        m_hat = m_new / (1.0 - ADAM_B1 ** ADAM_STEP)
        v_hat = v_new / (1.0 - ADAM_B2 ** ADAM_STEP)
        g_ref[...] = g
        d_ref[...] = -ADAM_LR * (m_hat / (jnp.sqrt(v_hat) + ADAM_EPS) + ADAM_WD * w_ref[...])
        nm_ref[...] = m_new
        nv_ref[...] = v_new

    blk = pl.BlockSpec((tr, n), lambda i: (i, 0))
    return pl.pallas_call(
        body, name=name, grid=(r // tr,), in_specs=[pl.BlockSpec((N_DEV, tr, n), lambda i: (0, i, 0)), blk, blk, blk],
        out_specs=[blk] * 4, out_shape=[jax.ShapeDtypeStruct((r, n), F32)] * 4,
        compiler_params=_params(("parallel",)))(recv, w, m, v)


def _shard_rows(full, axis):
    if axis == 0:
        return full.reshape(N_DEV, -1)
    r, c = full.shape
    return jnp.transpose(full.reshape(r, N_DEV, c // N_DEV), (1, 0, 2)).reshape(N_DEV, -1)


def _unshard(blocks, axis):
    if axis == 0:
        return blocks.reshape(-1, blocks.shape[-1])
    return jnp.transpose(blocks, (1, 0, 2)).reshape(blocks.shape[1], -1)


def kernel(x, norm_mix_w, w_in, ssd_conv_w, ssd_conv_b, ssd_dt_bias_fwd, ssd_dt_bias_bwd, ssd_a_log_fwd, ssd_a_log_bwd, ssd_d, ssd_norm_w, s5_lambda_re_fwd, s5_lambda_im_fwd, s5_log_step_fwd, s5_lambda_re_bwd, s5_lambda_im_bwd, s5_log_step_bwd, s5_b_re, s5_b_im, s5_c_re_fwd, s5_c_im_fwd, s5_c_re_bwd, s5_c_im_bwd, s5_d, s5_glu_w, s5_glu_b, s5_norm_w, w_out, norm_ffn_w, ffn_w_up, ffn_conv_w, ffn_conv_b, ffn_w_down, norm_final_w, loss_target, m_norm_mix_w, m_w_in, m_ssd_conv_w, m_ssd_conv_b, m_ssd_dt_bias_fwd, m_ssd_dt_bias_bwd, m_ssd_a_log_fwd, m_ssd_a_log_bwd, m_ssd_d, m_ssd_norm_w, m_s5_lambda_re_fwd, m_s5_lambda_im_fwd, m_s5_log_step_fwd, m_s5_lambda_re_bwd, m_s5_lambda_im_bwd, m_s5_log_step_bwd, m_s5_b_re, m_s5_b_im, m_s5_c_re_fwd, m_s5_c_im_fwd, m_s5_c_re_bwd, m_s5_c_im_bwd, m_s5_d, m_s5_glu_w, m_s5_glu_b, m_s5_norm_w, m_w_out, m_norm_ffn_w, m_ffn_w_up, m_ffn_conv_w, m_ffn_conv_b, m_ffn_w_down, m_norm_final_w, v_norm_mix_w, v_w_in, v_ssd_conv_w, v_ssd_conv_b, v_ssd_dt_bias_fwd, v_ssd_dt_bias_bwd, v_ssd_a_log_fwd, v_ssd_a_log_bwd, v_ssd_d, v_ssd_norm_w, v_s5_lambda_re_fwd, v_s5_lambda_im_fwd, v_s5_log_step_fwd, v_s5_lambda_re_bwd, v_s5_lambda_im_bwd, v_s5_log_step_bwd, v_s5_b_re, v_s5_b_im, v_s5_c_re_fwd, v_s5_c_im_fwd, v_s5_c_re_bwd, v_s5_c_im_bwd, v_s5_d, v_s5_glu_w, v_s5_glu_b, v_s5_norm_w, v_w_out, v_norm_ffn_w, v_ffn_w_up, v_ffn_conv_w, v_ffn_conv_b, v_ffn_w_down, v_norm_final_w):
    args = dict(locals())
    strip = lambda n, v: v if n == 'norm_final_w' else v[0]
    w = {n: strip(n, args[n]) for n in WEIGHTS}

    mats = ['w_in', 'w_out', 'ffn_w_up', 'ffn_w_down']
    convs = ['ssd_conv_w', 'ffn_conv_w']
    shard = lambda n: w[n].astype(BF16) if n in mats else w[n]
    early, late = ['w_in', 'ssd_conv_w'], ['w_out', 'ffn_w_up', 'ffn_w_down', 'ffn_conv_w']
    full = dict(w)
    full.update(zip(early, all_gather([shard(n) for n in early], name="weight_all_gather")))
    ssem, rsem, src_thru, land_thru, token = scatter_start([shard(n) for n in late], name="weight_gather_start",
                                                           indexed=False)
    me = 4 * lax.axis_index("x") + 2 * lax.axis_index("y") + lax.axis_index("c")

    def late_weights(after):
        own, landed = scatter_wait(ssem, rsem, src_thru, land_thru, after, name="weight_gather_wait", indexed=False)
        return {n: lax.dynamic_update_index_in_dim(l, o, me, 0) for n, o, l in zip(late, own, landed)}

    full['late'], full['token'] = late_weights, token[:1, :1]

    pending = []
    last = 'norm_mix_w'
    small = convs + [n for n in WEIGHTS if n not in SHARDED and n != last]
    slot = {n: -(-w[n].size // (8 * LANES)) * 8 for n in small}
    used = sum(slot.values()) + 8
    nrow = -(-used // PACK_ROWS) * PACK_ROWS

    def tiles(v, n):
        return jnp.pad(v, ((0, 0), (0, slot[n] * LANES - v.shape[1]))).reshape(v.shape[0], slot[n], LANES)

    def send_early(grads, names, loss=None):
        srcs = [grads[n].astype(BF16) for n in names]
        if loss is not None:
            pieces = [tiles(grads[n].reshape(N_DEV, -1), n) if n in SHARDED else
                      jnp.broadcast_to(tiles(grads[n].reshape(1, -1), n), (N_DEV, slot[n], LANES)) for n in small]
            pieces.append(jnp.broadcast_to(jnp.pad(loss.reshape(1, 1, 1), ((0, 0), (0, 7), (0, LANES - 1))),
                                           (N_DEV, 8, LANES)))
            pieces.append(jnp.zeros((N_DEV, nrow - used, LANES), F32))
            srcs.append(jnp.concatenate(pieces, axis=1))
            names = names + ['small']
        started = scatter_start(srcs, name="grad_start_" + names[0], indexed=True)
        pending.append((names,) + started[:4])
        return started[4][:1, :1]

    full['on_grads'] = send_early
    loss, grad_x, g = local_step(x, loss_target, full)

    last_send = jnp.broadcast_to(g[last].reshape(1, -1, LANES), (N_DEV, g[last].size // LANES, LANES))
    last_started = scatter_start([last_send], name="grad_start_" + last, indexed=True)
    recv, outs = {}, [{}, {}, {}, {}]

    def arrived(names, started, after):
        own, landed = scatter_wait(*started, after, name="grad_wait_" + names[0], indexed=True)
        for n, o, l in zip(names, own, landed):
            recv[n] = lax.dynamic_update_index_in_dim(l, lax.dynamic_index_in_dim(o, me, 0, keepdims=False), me, 0)

    def update(n):
        shape = recv[n].shape[1:]
        res = adamw(recv[n], *[strip(n, args[p + n]).reshape(shape) for p in ('', 'm_', 'v_')], name="adamw_" + n)
        for o, p in zip(outs, res):
            o[n] = p.reshape(args[n].shape)

    for names, *started in pending:
        arrived(names, started, last_started[4])
    for n in mats:
        update(n)

    def pack(prefix):
        vals = [tiles(strip(n, args[prefix + n]).reshape(1, -1), n)[0] for n in small]
        return jnp.concatenate(vals + [jnp.zeros((nrow - used + 8, LANES), F32)], axis=0)

    packed = adamw(recv['small'], pack(''), pack('m_'), pack('v_'), name="adamw_small")
    arrived([last], last_started[:4], packed[1])
    update(last)
    off = 0
    for n in small:
        for o, p in zip(outs, packed):
            o[n] = p[off:off + slot[n]].reshape(-1)[:w[n].size].reshape(args[n].shape)
        off += slot[n]
    loss_out = packed[0][off, 0].reshape(())
    return (loss_out, grad_x, *[o[n] for o in outs for n in WEIGHTS])
```

```python
import functools

import jax
import jax.numpy as jnp
from jax import lax
from jax.experimental import pallas as pl
from jax.experimental.pallas import tpu as pltpu

F32, BF16 = jnp.float32, jnp.bfloat16
N_DEV = 8
D_MODEL = 1024
SSD_W, HEADS, HDIM, SGROUPS, HPG, NSTATE, SCONV, QC = 1024, 16, 64, 4, 4, 128, 5, 128
XBC_W = SSD_W + 2 * SGROUPS * NSTATE
S5_W, S5_G, S5_C, S5_P, S5_Q = 512, 32, 16, 64, 16
S5_QC = S5_Q * S5_C
CARRY_ROWS = 32
DFF, FCONV = 2816, 3
FFN_BLK, FFN_PAD = 704, 768
EPS = 1e-6
ADAM_LR, ADAM_B1, ADAM_B2, ADAM_EPS, ADAM_WD, ADAM_STEP = 0.001, 0.9, 0.999, 1e-08, 0.01, 10
LANES = 128
MESH = pl.DeviceIdType.MESH

WEIGHTS = ['norm_mix_w', 'w_in', 'ssd_conv_w', 'ssd_conv_b', 'ssd_dt_bias_fwd', 'ssd_dt_bias_bwd', 'ssd_a_log_fwd',
           'ssd_a_log_bwd', 'ssd_d', 'ssd_norm_w', 's5_lambda_re_fwd', 's5_lambda_im_fwd', 's5_log_step_fwd',
           's5_lambda_re_bwd', 's5_lambda_im_bwd', 's5_log_step_bwd', 's5_b_re', 's5_b_im', 's5_c_re_fwd', 's5_c_im_fwd',
           's5_c_re_bwd', 's5_c_im_bwd', 's5_d', 's5_glu_w', 's5_glu_b', 's5_norm_w', 'w_out', 'norm_ffn_w', 'ffn_w_up',
           'ffn_conv_w', 'ffn_conv_b', 'ffn_w_down', 'norm_final_w']
SHARDED = {'w_in': 1, 'ssd_conv_w': 1, 'w_out': 0, 'ffn_w_up': 1, 'ffn_conv_w': 1, 'ffn_w_down': 0}
FULL_SHAPE = {'w_in': (1024, 3616), 'ssd_conv_w': (5, 2048), 'w_out': (1536, 1024), 'ffn_w_up': (1024, 5632),
              'ffn_conv_w': (3, 5632), 'ffn_w_down': (2816, 1024)}
PACK_ROWS = 512


def _pick(n, cap=1536):
    if n <= cap:
        return n
    return max(t for t in range(LANES, cap + 1, LANES) if n % t == 0)


def _params(sem):
    return pltpu.CompilerParams(dimension_semantics=sem)


def _bd(a, b, ca, cb):
    return lax.dot_general(a.astype(BF16), b.astype(BF16), (((ca,), (cb,)), ((), ())), preferred_element_type=F32)


@jax.custom_vjp
def dot_nn(a, b):
    return _bd(a, b, 1, 0)


dot_nn.defvjp(lambda a, b: (_bd(a, b, 1, 0), (a, b)),
              lambda r, g: (_bd(g, r[1], 1, 1).astype(r[0].dtype), _bd(r[0], g, 0, 0).astype(r[1].dtype)))


@jax.custom_vjp
def dot_nt(a, b):
    return _bd(a, b, 1, 1)


dot_nt.defvjp(lambda a, b: (_bd(a, b, 1, 1), (a, b)),
              lambda r, g: (_bd(g, r[1], 1, 0).astype(r[0].dtype), _bd(g, r[0], 0, 0).astype(r[1].dtype)))


@jax.custom_vjp
def dot_tn(a, b):
    return _bd(a, b, 0, 0)


dot_tn.defvjp(lambda a, b: (_bd(a, b, 0, 0), (a, b)),
              lambda r, g: (_bd(r[1], g, 1, 1).astype(r[0].dtype), _bd(r[0], g, 1, 0).astype(r[1].dtype)))


def _rows2(v):
    h = v.shape[0] // 2
    return v[:h], v[h:]


def _cols2(v):
    h = v.shape[1] // 2
    return v[:, :h], v[:, h:]


@jax.custom_vjp
def dot2_nn(la, lb, x):
    return _rows2(_bd(jnp.concatenate([la, lb], axis=0), x, 1, 0))


def _dot2_nn_bwd(res, g):
    la, lb, x = res
    gcat, lcat = jnp.concatenate(g, axis=0), jnp.concatenate([la, lb], axis=0)
    return (*_rows2(_bd(gcat, x, 1, 1)), _bd(lcat, gcat, 0, 0))


dot2_nn.defvjp(lambda la, lb, x: (dot2_nn(la, lb, x), (la, lb, x)), _dot2_nn_bwd)


@jax.custom_vjp
def dot_nt2(c, p0, p1):
    return _cols2(_bd(c, jnp.concatenate([p0, p1], axis=0), 1, 1))


def _dot_nt2_bwd(res, g):
    c, p0, p1 = res
    gcat = jnp.concatenate(g, axis=1)
    return (_bd(gcat, jnp.concatenate([p0, p1], axis=0), 1, 0), *_rows2(_bd(gcat, c, 0, 0)))


dot_nt2.defvjp(lambda c, p0, p1: (dot_nt2(c, p0, p1), (c, p0, p1)), _dot_nt2_bwd)


@jax.custom_vjp
def dot_tn2(a0, a1, b):
    return _rows2(_bd(jnp.concatenate([a0, a1], axis=1), b, 0, 0))


def _dot_tn2_bwd(res, g):
    a0, a1, b = res
    gcat, acat = jnp.concatenate(g, axis=0), jnp.concatenate([a0, a1], axis=1)
    return (*_cols2(_bd(b, gcat, 1, 1)), _bd(acat, gcat, 1, 0))


dot_tn2.defvjp(lambda a0, a1, b: (dot_tn2(a0, a1, b), (a0, a1, b)), _dot_tn2_bwd)


def _split3(x):
    hi = x.astype(BF16)
    r = x - hi.astype(F32)
    mid = r.astype(BF16)
    lo = (r - mid.astype(F32)).astype(BF16)
    return hi, mid, lo


def _cum_matrix(q, upper):
    ri = lax.broadcasted_iota(jnp.int32, (q, q), 0)
    ci = lax.broadcasted_iota(jnp.int32, (q, q), 1)
    return jnp.where((ci >= ri) if upper else (ci <= ri), 1.0, 0.0).astype(BF16)


def _exact_right(x, mat):
    return sum(jnp.dot(p, mat, preferred_element_type=F32) for p in _split3(x))


@functools.partial(jax.custom_vjp, nondiff_argnums=(1,))
def cum_row(x, rev):
    return _exact_right(x, _cum_matrix(x.shape[1], not rev))


cum_row.defvjp(lambda x, rev: (cum_row(x, rev), None),
               lambda rev, _, g: (_exact_right(g, _cum_matrix(g.shape[1], rev)),))


def _softplus(x):
    return jnp.maximum(x, 0.0) + jnp.log(1.0 + jnp.exp(-jnp.abs(x)))


def _silu(x):
    return x * jax.nn.sigmoid(x)


def _gelu(x):
    return 0.5 * x * (1.0 + jnp.tanh(0.7978845608028654 * (x + 0.044715 * (x * x * x))))


def _rms(x, w):
    xf = x.astype(F32)
    return xf * lax.rsqrt(jnp.mean(xf * xf, axis=-1, keepdims=True) + EPS) * w


def matmul_sum(a_list, b_list, *, name, out_dtype=F32, add=None, tm=512, nt=False, norm_w=None, norm_bwd=None):
    a_arrs = [a[0] if isinstance(a, tuple) else a for a in a_list]
    b_arrs = [b[0] if isinstance(b, tuple) else b for b in b_list]
    m, n = a_arrs[0].shape[0], b_arrs[0].shape[-2 if nt else -1]
    tm, tn, k = min(tm, m), _pick(n), len(a_list)
    assert (norm_w is None and norm_bwd is None) or tn == n

    def body(*refs):
        acc = None
        for a_ref, b_ref in zip(refs[:k], refs[k:2 * k]):
            p = _bd(a_ref[...], b_ref[...], 1, 1 if nt else 0)
            acc = p if acc is None else acc + p
        if add is not None:
            acc = acc + refs[2 * k][...]
        if norm_bwd is not None:
            x_ref, w_ref, res_ref, dx_ref, dw_ref = refs[-5:]
            dx, dw = jax.vjp(_rms, x_ref[...], w_ref[...])[1](acc)
            dx_ref[...] = dx + res_ref[...]

            @pl.when(pl.program_id(0) == 0)
            def _():
                dw_ref[...] = jnp.zeros_like(dw_ref)

            dw_ref[...] += dw
        elif norm_w is not None:
            refs[-2][...] = acc.astype(out_dtype)
            refs[-1][...] = _rms(acc, refs[-3][...]).astype(BF16)
        else:
            refs[-1][...] = acc.astype(out_dtype)

    def a_spec(a):
        if isinstance(a, tuple):
            return pl.BlockSpec((tm, a[1]), lambda i, j, blk=a[2]: (i, blk))
        return pl.BlockSpec((tm, a.shape[1]), lambda i, j: (i, 0))

    def b_spec(b):
        arr, p = b if isinstance(b, tuple) else (b, None)
        kk = arr.shape[-1 if nt else -2]
        shape, idx = ((tn, kk), lambda j: (j, 0)) if nt else ((kk, tn), lambda j: (0, j))
        if p is None:
            return pl.BlockSpec(shape, lambda i, j: idx(j))
        return pl.BlockSpec((None,) + shape, lambda i, j, p=p: (p,) + idx(j))

    in_specs = [a_spec(a) for a in a_list] + [b_spec(b) for b in b_list]
    args = a_arrs + b_arrs
    if add is not None:
        in_specs.append(pl.BlockSpec((tm, tn), lambda i, j: (i, j)))
        args.append(add)
    out_spec, out_shape = pl.BlockSpec((tm, tn), lambda i, j: (i, j)), jax.ShapeDtypeStruct((m, n), out_dtype)
    if norm_w is not None:
        in_specs.append(pl.BlockSpec(norm_w.shape, lambda i, j: (0, 0)))
        args.append(norm_w)
        out_spec, out_shape = [out_spec, out_spec], [out_shape, jax.ShapeDtypeStruct((m, n), BF16)]
    sem = ("parallel", "parallel")
    if norm_bwd is not None:
        x, w, res = norm_bwd
        wspec = pl.BlockSpec(w.shape, lambda i, j: (0, 0))
        in_specs += [out_spec, wspec, out_spec]
        args += [x, w, res]
        out_spec, out_shape = [out_spec, wspec], [jax.ShapeDtypeStruct((m, n), F32), jax.ShapeDtypeStruct(w.shape, F32)]
        sem = ("arbitrary", "arbitrary")
    return pl.pallas_call(
        body, name=name, grid=(m // tm, n // tn), in_specs=in_specs, out_specs=out_spec, out_shape=out_shape,
        compiler_params=_params(sem))(*args)


def matmul_multi(a, b_list, *, name, tm=512):
    m, kk = a.shape
    tm, nb = min(tm, m), len(b_list)

    def body(a_ref, *refs):
        a_v = a_ref[...]
        for b_ref, o_ref in zip(refs[:nb], refs[nb:]):
            o_ref[...] = _bd(a_v, b_ref[...], 1, 0)

    return pl.pallas_call(
        body, name=name, grid=(m // tm,),
        in_specs=[pl.BlockSpec((tm, kk), lambda i: (i, 0))] + [_full_spec(b) for b in b_list],
        out_specs=[pl.BlockSpec((tm, b.shape[1]), lambda i: (i, 0)) for b in b_list],
        out_shape=[jax.ShapeDtypeStruct((m, b.shape[1]), F32) for b in b_list],
        compiler_params=_params(("parallel",)))(a, *b_list)


def matmul_cols(a, b3, *, name, out_dtype=F32, tm=1024):
    m, kk = a.shape
    p, _, nb = b3.shape
    tm, tn = min(tm, m), _pick(nb, 768)
    per = nb // tn

    def body(a_ref, b_ref, o_ref):
        o_ref[...] = _bd(a_ref[...], b_ref[...], 1, 0).astype(out_dtype)

    return pl.pallas_call(
        body, name=name, grid=(m // tm, p * per),
        in_specs=[pl.BlockSpec((tm, kk), lambda i, j: (i, 0)),
                  pl.BlockSpec((None, kk, tn), lambda i, j: (j // per, 0, j % per))],
        out_specs=pl.BlockSpec((tm, tn), lambda i, j: (i, j)),
        out_shape=jax.ShapeDtypeStruct((m, p * nb), out_dtype),
        compiler_params=_params(("parallel", "parallel")))(a, b3)


def matmul_tn(a, b, *, name, tm=1024, out_blocks=None):
    m, k = a.shape
    n = b.shape[1]
    nb = n // (out_blocks or 1)
    tm, tk, tn = min(tm, m), _pick(k), _pick(nb, 768 if out_blocks else 1536)
    per = nb // tn

    def body(a_ref, b_ref, o_ref):
        @pl.when(pl.program_id(2) == 0)
        def _():
            o_ref[...] = jnp.zeros_like(o_ref)

        o_ref[...] += _bd(a_ref[...], b_ref[...], 0, 0)

    if out_blocks:
        out_spec = pl.BlockSpec((None, tk, tn), lambda i, j, t: (j // per, i, j % per))
        out_shape = jax.ShapeDtypeStruct((out_blocks, k, nb), F32)
    else:
        out_spec = pl.BlockSpec((tk, tn), lambda i, j, t: (i, j))
        out_shape = jax.ShapeDtypeStruct((k, n), F32)
    return pl.pallas_call(
        body, name=name, grid=(k // tk, n // tn, m // tm),
        in_specs=[pl.BlockSpec((tm, tk), lambda i, j, t: (t, i)), pl.BlockSpec((tm, tn), lambda i, j, t: (t, j))],
        out_specs=out_spec, out_shape=out_shape,
        compiler_params=_params(("parallel", "parallel", "arbitrary")))(a, b)


def _row_spec(r, tm):
    if isinstance(r, tuple):
        arr, width, blk = r
        return arr, pl.BlockSpec((tm, width), lambda i, blk=blk: (i, blk))
    return r, pl.BlockSpec((tm, r.shape[1]), lambda i: (i, 0))


def _full_spec(p):
    return pl.BlockSpec(p.shape, lambda i: (0,) * p.ndim)


def _expand_rows(rows, tm):
    arrays, specs, counts, widths = [], [], [], []
    for r in rows:
        parts = [_row_spec(p, tm) for p in (r if isinstance(r, list) else [r])]
        arrays += [a for a, _ in parts]
        specs += [s for _, s in parts]
        counts.append(len(parts))
        widths.append(sum(s.block_shape[1] for _, s in parts))
    return arrays, specs, counts, widths


def _row_values(refs, counts):
    vals, k = [], 0
    for c in counts:
        parts = [refs[k + j][...] for j in range(c)]
        vals.append(parts[0] if c == 1 else jnp.concatenate(parts, axis=1))
        k += c
    return vals


def _rows_of(rows):
    first = rows[0][0] if isinstance(rows[0], list) else rows[0]
    return (first[0] if isinstance(first, tuple) else first).shape[0]


def rowmap_fwd(fn, rows, params, outs, *, name, tm=256):
    m = _rows_of(rows)
    tm = min(tm, m)
    arrays, specs, counts, _ = _expand_rows(rows, tm)
    nin, npar = len(arrays), len(params)

    def body(*refs):
        res = fn(*_row_values(refs[:nin], counts), *[r[...] for r in refs[nin:nin + npar]])
        for o_ref, v in zip(refs[nin + npar:], res):
            o_ref[...] = v.astype(o_ref.dtype)

    return pl.pallas_call(
        body, name=name, grid=(m // tm,), in_specs=specs + [_full_spec(p) for p in params],
        out_specs=[pl.BlockSpec((tm, c), lambda i: (i, 0)) for c, _ in outs],
        out_shape=[jax.ShapeDtypeStruct((m, c), dt) for c, dt in outs],
        compiler_params=_params(("parallel",)))(*arrays, *params)


def rowmap_bwd(fn, rows, params, cts, *, name, row_dtypes=None, add=None, tm=256):
    m = _rows_of(rows)
    tm = min(tm, m)
    arrays, specs, counts, widths = _expand_rows(rows, tm)
    cp = [_row_spec(c, tm) for c in cts]
    nin, nr, npar, nc = len(arrays), len(rows), len(params), len(cts)
    row_dtypes = row_dtypes or [F32] * nr

    def body(*refs):
        ins = _row_values(refs[:nin], counts) + [r[...] for r in refs[nin:nin + npar]]
        ins = [v.astype(F32) for v in ins]
        ct = tuple(r[...].astype(F32) for r in refs[nin + npar:nin + npar + nc])
        base = nin + npar + nc
        extra = None
        if add is not None:
            extra = refs[base][...]
            base += 1
        _, pull = jax.vjp(fn, *ins)
        grads = pull(ct)
        for j in range(nr):
            g = grads[j]
            if j == 0 and extra is not None:
                g = g + extra
            refs[base + j][...] = g.astype(refs[base + j].dtype)

        @pl.when(pl.program_id(0) == 0)
        def _():
            for j in range(npar):
                refs[base + nr + j][...] = jnp.zeros_like(refs[base + nr + j])

        for j in range(npar):
            refs[base + nr + j][...] += grads[nr + j]

    in_specs = specs + [_full_spec(p) for p in params] + [s for _, s in cp]
    args = arrays + list(params) + [a for a, _ in cp]
    if add is not None:
        in_specs.append(pl.BlockSpec((tm, widths[0]), lambda i: (i, 0)))
        args.append(add)
    out_specs = [pl.BlockSpec((tm, w), lambda i: (i, 0)) for w in widths] + [_full_spec(p) for p in params]
    out_shape = [jax.ShapeDtypeStruct((m, w), dt) for w, dt in zip(widths, row_dtypes)]
    out_shape += [jax.ShapeDtypeStruct(p.shape, F32) for p in params]
    return pl.pallas_call(
        body, name=name, grid=(m // tm,), in_specs=in_specs, out_specs=out_specs, out_shape=out_shape,
        compiler_params=_params(("arbitrary",)))(*args)


def loss_head(h, target, w, *, name, tm=256, matmul=None):
    m, d = h.shape
    tm = min(tm, m)

    def body(h_ref, t_ref, w_ref, *refs):
        loss_ref, dh_ref, dw_ref = refs[-3:]
        rows = h_ref[...]
        if matmul is not None:
            rows = rows + _bd(refs[0][...], refs[1][...], 1, 0)
        y, pull = jax.vjp(_rms, rows, w_ref[...])
        err = y - t_ref[...]
        dh, dw = pull(err * (1.0 / d))

        @pl.when(pl.program_id(0) == 0)
        def _():
            loss_ref[...] = jnp.zeros_like(loss_ref)
            dw_ref[...] = jnp.zeros_like(dw_ref)

        loss_ref[...] += (0.5 / d) * jnp.sum(err * err, keepdims=True)
        dw_ref[...] += dw
        dh_ref[...] = dh

    row = pl.BlockSpec((tm, d), lambda i: (i, 0))
    in_specs, args = [row, row, _full_spec(w)], [h, target, w]
    if matmul is not None:
        in_specs += [pl.BlockSpec((tm, matmul[0].shape[1]), lambda i: (i, 0)), _full_spec(matmul[1])]
        args += list(matmul)
    return pl.pallas_call(
        body, name=name, grid=(m // tm,), in_specs=in_specs,
        out_specs=[pl.BlockSpec((1, 1), lambda i: (0, 0)), row, _full_spec(w)],
        out_shape=[jax.ShapeDtypeStruct((1, 1), F32), jax.ShapeDtypeStruct((m, d), F32),
                   jax.ShapeDtypeStruct(w.shape, F32)],
        compiler_params=_params(("arbitrary",)))(*args)


def _shift(x, s):
    if s == 0:
        return x
    n = x.shape[0]
    t = lax.broadcasted_iota(jnp.int32, x.shape, 0)
    rolled = pltpu.roll(x, (-s) % n, 0)
    return jnp.where((t + s >= 0) & (t + s < n), rolled, 0.0)


def _conv(x, w, b):
    k = w.shape[0]
    acc = b + w[k // 2:k // 2 + 1, :] * x
    for j in range(k):
        if j != k // 2:
            acc = acc + w[j:j + 1, :] * _shift(x, j - k // 2)
    return acc


def _conv_bwd(x, dc, w):
    k = w.shape[0]
    dx = None
    dws = []
    for j in range(k):
        s = j - k // 2
        term = w[j:j + 1, :] * _shift(dc, -s)
        dx = term if dx is None else dx + term
        dws.append(jnp.sum(dc * _shift(x, s), axis=0, keepdims=True))
    return dx, jnp.concatenate(dws, axis=0), jnp.sum(dc, axis=0, keepdims=True)


def _dsilu(c):
    s = jax.nn.sigmoid(c)
    return s * (1.0 + c * (1.0 - s))


def ssd_conv_fwd(xbc, w, b, *, bsz, name):
    t, c = xbc.shape
    seq, ct = t // bsz, 256

    def body(x_ref, w_ref, b_ref, o_ref):
        o_ref[...] = _silu(_conv(x_ref[...], w_ref[...], b_ref[...]))

    return pl.pallas_call(
        body, name=name, grid=(c // ct, bsz),
        in_specs=[pl.BlockSpec((seq, ct), lambda j, i: (i, j)), pl.BlockSpec((w.shape[0], ct), lambda j, i: (0, j)),
                  pl.BlockSpec((1, ct), lambda j, i: (0, j))],
        out_specs=pl.BlockSpec((seq, ct), lambda j, i: (i, j)),
        out_shape=jax.ShapeDtypeStruct((t, c), F32),
        compiler_params=_params(("parallel", "parallel")))(xbc, w, b)


def ssd_conv_bwd(xbc, dparts, w, b, *, bsz, name):
    t, c = xbc.shape
    seq, ct, k = t // bsz, 256, w.shape[0]
    starts = [0]
    for p in dparts:
        starts.append(starts[-1] + p.shape[1] // ct)

    def body(x_ref, *refs):
        g_refs, (w_ref, b_ref, dx_ref, dw_ref, db_ref) = refs[:len(dparts)], refs[len(dparts):]
        j = pl.program_id(0)
        g = g_refs[-1][...]
        for n in range(len(dparts) - 2, -1, -1):
            g = jnp.where(j < starts[n + 1], g_refs[n][...], g)
        x, wv = x_ref[...], w_ref[...]
        dc = g * _dsilu(_conv(x, wv, b_ref[...]))
        dx, dw, db = _conv_bwd(x, dc, wv)
        dx_ref[...] = dx

        @pl.when(pl.program_id(1) == 0)
        def _():
            dw_ref[...] = jnp.zeros_like(dw_ref)
            db_ref[...] = jnp.zeros_like(db_ref)

        dw_ref[...] += dw
        db_ref[...] += db

    def part_spec(n):
        lo, hi = starts[n], starts[n + 1]

        def index(j, i):
            inside = (j >= lo) & (j < hi)
            return jnp.where(inside, i, 0), jnp.where(inside, j - lo, 0)

        return pl.BlockSpec((seq, ct), index)

    blk = pl.BlockSpec((seq, ct), lambda j, i: (i, j))
    wspec, bspec = pl.BlockSpec((k, ct), lambda j, i: (0, j)), pl.BlockSpec((1, ct), lambda j, i: (0, j))
    return pl.pallas_call(
        body, name=name, grid=(c // ct, bsz),
        in_specs=[blk] + [part_spec(n) for n in range(len(dparts))] + [wspec, bspec], out_specs=[blk, wspec, bspec],
        out_shape=[jax.ShapeDtypeStruct((t, c), F32), jax.ShapeDtypeStruct((k, c), F32),
                   jax.ShapeDtypeStruct((1, c), F32)],
        compiler_params=_params(("parallel", "arbitrary")))(xbc, *dparts, w, b)


def _ffn_specs(seq, ct, k, nblk):
    val = pl.BlockSpec((seq, ct), lambda j, i: (i, j))
    gate = pl.BlockSpec((seq, ct), lambda j, i: (i, nblk + j))
    wv, wg = pl.BlockSpec((k, ct), lambda j, i: (0, j)), pl.BlockSpec((k, ct), lambda j, i: (0, nblk + j))
    bv, bg = pl.BlockSpec((1, ct), lambda j, i: (0, j)), pl.BlockSpec((1, ct), lambda j, i: (0, nblk + j))
    return val, gate, wv, wg, bv, bg


def ffn_act_fwd(up, w, b, *, bsz, name):
    t = up.shape[0]
    half = up.shape[1] // 2
    seq, ct, k = t // bsz, 256, w.shape[0]
    val, gate, wv, wg, bv, bg = _ffn_specs(seq, ct, k, half // ct)

    def body(v_ref, g_ref, wv_ref, wg_ref, bv_ref, bg_ref, o_ref):
        vc = _conv(v_ref[...].astype(F32), wv_ref[...], bv_ref[...])
        gc = _conv(g_ref[...].astype(F32), wg_ref[...], bg_ref[...])
        o_ref[...] = (_silu(gc) * vc).astype(BF16)

    return pl.pallas_call(
        body, name=name, grid=(half // ct, bsz), in_specs=[val, gate, wv, wg, bv, bg], out_specs=val,
        out_shape=jax.ShapeDtypeStruct((t, half), BF16),
        compiler_params=_params(("parallel", "parallel")))(up, up, w, w, b, b)


def ffn_act_bwd(up, dact, w, b, *, bsz, name):
    t = up.shape[0]
    half = up.shape[1] // 2
    seq, ct, k = t // bsz, 256, w.shape[0]
    val, gate, wv, wg, bv, bg = _ffn_specs(seq, ct, k, half // ct)

    def body(v_ref, g_ref, wv_ref, wg_ref, bv_ref, bg_ref, d_ref, dv_ref, dg_ref, dwv_ref, dwg_ref, dbv_ref, dbg_ref):
        v, g = v_ref[...].astype(F32), g_ref[...].astype(F32)
        vc = _conv(v, wv_ref[...], bv_ref[...])
        gc = _conv(g, wg_ref[...], bg_ref[...])
        d = d_ref[...].astype(F32)
        sg = jax.nn.sigmoid(gc)
        dv, dwv, dbv = _conv_bwd(v, d * (gc * sg), wv_ref[...])
        dg, dwg, dbg = _conv_bwd(g, d * vc * (sg * (1.0 + gc * (1.0 - sg))), wg_ref[...])
        dv_ref[...] = dv.astype(BF16)
        dg_ref[...] = dg.astype(BF16)

        @pl.when(pl.program_id(1) == 0)
        def _():
            for r in (dwv_ref, dwg_ref, dbv_ref, dbg_ref):
                r[...] = jnp.zeros_like(r)

        dwv_ref[...] += dwv
        dwg_ref[...] += dwg
        dbv_ref[...] += dbv
        dbg_ref[...] += dbg

    return pl.pallas_call(
        body, name=name, grid=(half // ct, bsz), in_specs=[val, gate, wv, wg, bv, bg, val],
        out_specs=[val, val, wv, wv, bv, bv],
        out_shape=[jax.ShapeDtypeStruct((t, half), BF16), jax.ShapeDtypeStruct((t, half), BF16),
                   jax.ShapeDtypeStruct((k, half), F32), jax.ShapeDtypeStruct((k, half), F32),
                   jax.ShapeDtypeStruct((1, half), F32), jax.ShapeDtypeStruct((1, half), F32)],
        compiler_params=_params(("parallel", "arbitrary")))(up, up, w, w, b, b, dact)


def _sel_row(a, h):
    oh = (lax.broadcasted_iota(jnp.int32, (a.shape[0], 1), 0) == h).astype(F32)
    return jnp.sum(a * oh, axis=0, keepdims=True)


def _ssd_chunk(xp, dtr, bm, cm, prev, bias_r, alog_r, dskip_r, rev):
    q = dtr.shape[1]
    ri = lax.broadcasted_iota(jnp.int32, (q, q), 0)
    ci = lax.broadcasted_iota(jnp.int32, (q, q), 1)
    mask = (ci >= ri) if rev else (ci <= ri)
    lane_lo, row_lo = ci < HDIM, ri < HDIM
    dt_r = _softplus(dtr + bias_r)
    dta_r = dt_r * (-jnp.exp(alog_r))
    cs_r = cum_row(dta_r, rev)
    scores = dot_nt(cm, bm)

    def per_row(v):
        return jnp.broadcast_to(v, (q, q)).T

    assert len(xp) == 2
    y_diag, csqs, decayed, tots = [], [], [], []
    for p in range(2):
        ha = 2 * p + (HPG if rev else 0)
        hb = ha + 1
        cs_a, cs_b = _sel_row(cs_r, ha), _sel_row(cs_r, hb)
        csq_a, csq_b = per_row(cs_a), per_row(cs_b)
        seg_a = jnp.exp(jnp.where(mask, csq_a - cs_a, -1e30))
        seg_b = jnp.exp(jnp.where(mask, csq_b - cs_b, -1e30))
        csq = jnp.where(lane_lo, csq_a, csq_b)
        xdt = xp[p] * jnp.where(lane_lo, per_row(_sel_row(dt_r, ha)), per_row(_sel_row(dt_r, hb)))
        tot_a = jnp.sum(_sel_row(dta_r, ha), axis=1, keepdims=True)
        tot_b = jnp.sum(_sel_row(dta_r, hb), axis=1, keepdims=True)
        y_diag.append(jnp.where(lane_lo, *dot2_nn(scores * seg_a, scores * seg_b, xdt)))
        csqs.append(csq)
        decayed.append(xdt * jnp.exp(jnp.where(lane_lo, tot_a, tot_b) - csq))
        tots.append((tot_a, tot_b, ha, hb))
    y_off = dot_nt2(cm, *prev)
    states = dot_tn2(*decayed, bm)
    ys, news = [], []
    for p, (tot_a, tot_b, ha, hb) in enumerate(tots):
        y = y_diag[p] + y_off[p] * jnp.exp(csqs[p])
        if not rev:
            y = y + jnp.where(lane_lo, _sel_row(dskip_r, ha), _sel_row(dskip_r, hb)) * xp[p]
        ys.append(y)
        news.append(jnp.exp(jnp.where(row_lo, tot_a, tot_b)) * prev[p] + states[p])
    return tuple(ys), tuple(news)


NPAIR = HPG // 2


def _ssd_specs(seq, nc):
    xs = pl.BlockSpec((None, seq, HPG * HDIM), lambda b, g: (b, 0, g))
    bm = pl.BlockSpec((None, seq, NSTATE), lambda b, g: (b, 0, SSD_W // NSTATE + g))
    cm = pl.BlockSpec((None, seq, NSTATE), lambda b, g: (b, 0, SSD_W // NSTATE + SGROUPS + g))
    dtr = pl.BlockSpec((None, None, 2 * HPG, seq), lambda b, g: (b, g, 0, 0))
    pr = pl.BlockSpec((None, 2 * HPG, 1), lambda b, g: (g, 0, 0))
    st = pl.BlockSpec((None, None, 2, nc, NPAIR, 2 * HDIM, NSTATE), lambda b, g: (b, g, 0, 0, 0, 0, 0))
    return xs, bm, cm, dtr, pr, st


def _pair_cols(p):
    return slice(2 * HDIM * p, 2 * HDIM * (p + 1))


def ssd_scan_fwd(act, dtr, prs, *, name):
    bsz, seq, _ = act.shape
    nc = seq // QC
    xs, bm, cm, dtrs, pr, st = _ssd_specs(seq, nc)

    def body(x_ref, b_ref, c_ref, dtr_ref, br_ref, ar_ref, dk_ref, y_ref, st_ref):
        par = (br_ref[...], ar_ref[...], dk_ref[...])
        y_ref[...] = jnp.zeros_like(y_ref)

        def step(i, carry):
            new = []
            for rev in (False, True):
                k = (nc - 1 - i) if rev else i
                rows = pl.ds(pl.multiple_of(k * QC, QC), QC)
                xp = tuple(x_ref[rows, _pair_cols(p)] for p in range(NPAIR))
                for p in range(NPAIR):
                    st_ref[int(rev), k, p] = carry[rev][p]
                ys, nw = _ssd_chunk(xp, dtr_ref[:, rows], b_ref[rows, :], c_ref[rows, :], carry[rev], *par, rev)
                for p in range(NPAIR):
                    y_ref[rows, _pair_cols(p)] += ys[p]
                new.append(nw)
            return tuple(new)

        zero = tuple(jnp.zeros((2 * HDIM, NSTATE), F32) for _ in range(NPAIR))
        lax.fori_loop(0, nc // 2, lambda i, c: step(2 * i + 1, step(2 * i, c)), (zero, zero))

    return pl.pallas_call(
        body, name=name, grid=(bsz, SGROUPS), in_specs=[xs, bm, cm, dtrs, pr, pr, pr], out_specs=[xs, st],
        out_shape=[jax.ShapeDtypeStruct((bsz, seq, SSD_W), F32),
                   jax.ShapeDtypeStruct((bsz, SGROUPS, 2, nc, NPAIR, 2 * HDIM, NSTATE), F32)],
        compiler_params=_params(("parallel", "parallel")))(act, act, act, dtr, *prs)


def ssd_scan_bwd(act, dtr, prs, states, dy, *, name):
    bsz, seq, _ = act.shape
    nc = seq // QC
    xs, bm, cm, dtrs, pr, st = _ssd_specs(seq, nc)
    grp = pl.BlockSpec((None, seq, NSTATE), lambda b, g: (b, 0, g))
    dpr = pl.BlockSpec((None, None, 2 * HPG, 1), lambda b, g: (b, g, 0, 0))

    def body(x_ref, b_ref, c_ref, dtr_ref, br_ref, ar_ref, dk_ref, st_ref, dy_ref,
             dx_ref, db_ref, dc_ref, ddtr_ref, gbr_ref, gar_ref, gdk_ref):
        par = (br_ref[...], ar_ref[...], dk_ref[...])
        pgrads = (gbr_ref, gar_ref, gdk_ref)
        for r in pgrads + (dx_ref, db_ref, dc_ref, ddtr_ref):
            r[...] = jnp.zeros_like(r)

        def bstep(i, dcarry):
            new = []
            for rev in (False, True):
                k = i if rev else (nc - 1 - i)
                rows = pl.ds(pl.multiple_of(k * QC, QC), QC)
                xp = tuple(x_ref[rows, _pair_cols(p)] for p in range(NPAIR))
                prev = tuple(st_ref[int(rev), k, p] for p in range(NPAIR))
                _, pull = jax.vjp(functools.partial(_ssd_chunk, rev=rev), xp, dtr_ref[:, rows], b_ref[rows, :],
                                  c_ref[rows, :], prev, *par)
                dyp = tuple(dy_ref[rows, _pair_cols(p)] for p in range(NPAIR))
                gx, gdt, gb, gc, gprev, *gpar = pull((dyp, dcarry[rev]))
                for p in range(NPAIR):
                    dx_ref[rows, _pair_cols(p)] += gx[p]
                ddtr_ref[:, rows] += gdt
                db_ref[rows, :] += gb
                dc_ref[rows, :] += gc
                for r, g in zip(pgrads, gpar):
                    r[...] += g
                new.append(gprev)
            return tuple(new)

        zero = tuple(jnp.zeros((2 * HDIM, NSTATE), F32) for _ in range(NPAIR))
        lax.fori_loop(0, nc, bstep, (zero, zero))

    out_shape = [jax.ShapeDtypeStruct((bsz, seq, SSD_W), F32),
                 jax.ShapeDtypeStruct((bsz, seq, SGROUPS * NSTATE), F32),
                 jax.ShapeDtypeStruct((bsz, seq, SGROUPS * NSTATE), F32),
                 jax.ShapeDtypeStruct(dtr.shape, F32)]
    out_shape += [jax.ShapeDtypeStruct((bsz, SGROUPS, 2 * HPG, 1), F32)] * 3
    return pl.pallas_call(
        body, name=name, grid=(bsz, SGROUPS), in_specs=[xs, bm, cm, dtrs, pr, pr, pr, st, xs],
        out_specs=[xs, grp, grp, dtrs, dpr, dpr, dpr], out_shape=out_shape,
        compiler_params=_params(("parallel", "parallel")))(act, act, act, dtr, *prs, states, dy)


def _s5_core(lam_re, lam_im, log_step, b_re, b_im, c_re, c_im):
    q = S5_Q
    step = jnp.exp(log_step)[:, None]
    lr, li = lam_re * step, lam_im * step
    mag = jnp.exp(lr)
    ar, ai = mag * jnp.cos(li), mag * jnp.sin(li)
    den = lam_re * lam_re + lam_im * lam_im
    cr = ((ar - 1.0) * lam_re + ai * lam_im) / den
    ci = (ai * lam_re - (ar - 1.0) * lam_im) / den
    bbr = cr[..., None] * b_re - ci[..., None] * b_im
    bbi = cr[..., None] * b_im + ci[..., None] * b_re
    d = jnp.arange(q + 1, dtype=F32)[None, :, None]
    pm = jnp.exp(d * lr[:, None, :])
    pr, pi = pm * jnp.cos(d * li[:, None, :]), pm * jnp.sin(d * li[:, None, :])
    er = pr[..., None] * bbr[:, None] - pi[..., None] * bbi[:, None]
    ei = pr[..., None] * bbi[:, None] + pi[..., None] * bbr[:, None]
    hp = lax.Precision.HIGHEST
    k = (jnp.einsum('gcp,gdpz->gdcz', c_re, er[:, :q], precision=hp)
         - jnp.einsum('gcp,gdpz->gdcz', c_im, ei[:, :q], precision=hp))
    e = jnp.concatenate([er[:, :q], ei[:, :q]], axis=2)
    p1r, p1i = pr[:, 1:], pi[:, 1:]
    m_re = c_re[:, None] * p1r[:, :, None, :] - c_im[:, None] * p1i[:, :, None, :]
    m_im = -c_re[:, None] * p1i[:, :, None, :] - c_im[:, None] * p1r[:, :, None, :]
    da = jnp.concatenate([pr[:, q], pr[:, q]], axis=-1)
    db = jnp.concatenate([-pi[:, q], pi[:, q]], axis=-1)
    return k, e, jnp.concatenate([m_re, m_im], axis=-1), da, db


def _s5_operators(lf_re, lf_im, lsf, lb_re, lb_im, lsb, b_re, b_im, cf_re, cf_im, cb_re, cb_im):
    g = lf_re.shape[0]
    both = lambda f, b: jnp.concatenate([f, b], axis=0)
    k, e, m, da, db = _s5_core(both(lf_re, lb_re), both(lf_im, lb_im), both(lsf, lsb), both(b_re, b_re),
                               both(b_im, b_im), both(cf_re, cb_re), both(cf_im, cb_im))
    kf, kb = k[:g], k[g:]
    wtf, wtb = jnp.transpose(e[:g, ::-1], (0, 1, 3, 2)), jnp.transpose(e[g:], (0, 1, 3, 2))
    mtf, mtb = jnp.transpose(m[:g], (0, 3, 1, 2)), jnp.transpose(m[g:, ::-1], (0, 3, 1, 2))
    daf, dab, dbf, dbb = da[:g], da[g:], db[:g], db[g:]
    lags = jnp.concatenate([kb[:, :0:-1], kf[:, :1] + kb[:, :1], kf[:, 1:]], axis=1)
    tt = jnp.transpose(lags, (0, 1, 3, 2))
    wt = jnp.concatenate([wtf.reshape(g, S5_QC, 2 * S5_P), wtb.reshape(g, S5_QC, 2 * S5_P)], axis=-1)
    mt = jnp.concatenate([mtf.reshape(g, 2 * S5_P, S5_QC), mtb.reshape(g, 2 * S5_P, S5_QC)], axis=1)
    return tt, wt, mt, jnp.concatenate([daf, dab], -1), jnp.concatenate([dbf, dbb], -1)


def _gspec(*shape):
    return pl.BlockSpec((None,) + shape, lambda g: (g,) + (0,) * len(shape))


S5_HALVES = S5_QC // LANES


def _toeplitz_block(s, t):
    per = LANES // S5_C
    return t // per, slice(s * S5_C, (s + 1) * S5_C), slice((t % per) * S5_C, (t % per + 1) * S5_C)


def s5_toeplitz(kt, *, name):
    g = kt.shape[0]

    def body(k_ref, t_ref):
        for s in range(S5_Q):
            for t in range(S5_Q):
                t_ref[_toeplitz_block(s, t)] = k_ref[t - s + S5_Q - 1]

    return pl.pallas_call(
        body, name=name, grid=(g,), in_specs=[_gspec(2 * S5_Q - 1, S5_C, S5_C)],
        out_specs=_gspec(S5_HALVES, S5_QC, LANES), out_shape=jax.ShapeDtypeStruct((g, S5_HALVES, S5_QC, LANES), F32),
        compiler_params=_params(("parallel",)))(kt)


def s5_toeplitz_bwd(dtt, *, name):
    g = dtt.shape[0]

    def body(d_ref, k_ref):
        for j in range(2 * S5_Q - 1):
            acc = None
            for s in range(S5_Q):
                t = j - (S5_Q - 1) + s
                if 0 <= t < S5_Q:
                    blk = d_ref[_toeplitz_block(s, t)]
                    acc = blk if acc is None else acc + blk
            k_ref[j] = acc

    return pl.pallas_call(
        body, name=name, grid=(g,), in_specs=[_gspec(S5_HALVES, S5_QC, LANES)],
        out_specs=_gspec(2 * S5_Q - 1, S5_C, S5_C), out_shape=jax.ShapeDtypeStruct((g, 2 * S5_Q - 1, S5_C, S5_C), F32),
        compiler_params=_params(("parallel",)))(dtt)


S5_RT = 128


def _chunk_piece(q):
    per = LANES // S5_C
    return q // per, slice((q % per) * S5_C, (q % per + 1) * S5_C)


def to_chunks(u, *, name):
    t = u.shape[0]
    r = t // S5_Q
    rt = min(S5_RT, r)

    per = LANES // S5_C
    nblk = S5_W // LANES

    def body(*refs):
        o_ref = refs[-1]
        for k in range(nblk):
            for q in range(S5_Q):
                rows = refs[k][pl.ds(q, rt, stride=S5_Q), :]
                half, lanes = _chunk_piece(q)
                for j in range(per):
                    o_ref[k * per + j, half, :, lanes] = rows[:, j * S5_C:(j + 1) * S5_C]

    return pl.pallas_call(
        body, name=name, grid=(r // rt,),
        in_specs=[pl.BlockSpec((rt * S5_Q, LANES), lambda i, k=k: (i, k)) for k in range(nblk)],
        out_specs=pl.BlockSpec((S5_G, S5_HALVES, rt, LANES), lambda i: (0, 0, i, 0)),
        out_shape=jax.ShapeDtypeStruct((S5_G, S5_HALVES, r, LANES), F32),
        compiler_params=_params(("parallel",)))(*[u] * nblk)


def from_chunks(y, *, name, add=None, as_blocks=False):
    r = y.shape[2]
    rt = min(S5_RT, r)
    per = LANES // S5_C

    nblk = S5_W // LANES

    def body(*refs):
        y_ref, tmp_ref = refs[0], refs[-1]
        adds, outs = refs[1:-1 - nblk], refs[-1 - nblk:-1]
        for k in range(nblk):
            for q in range(S5_Q):
                half, lanes = _chunk_piece(q)
                for j in range(per):
                    tmp_ref[:, j * S5_C:(j + 1) * S5_C] = y_ref[k * per + j, half, :, lanes]
                row = tmp_ref[...]
                if add is not None:
                    row = row + adds[k][pl.ds(q, rt, stride=S5_Q), :]
                outs[k][pl.ds(q, rt, stride=S5_Q), :] = row

    in_specs = [pl.BlockSpec((S5_G, S5_HALVES, rt, LANES), lambda i: (0, 0, i, 0))]
    if add is not None:
        in_specs += [pl.BlockSpec((rt * S5_Q, LANES), lambda i, k=k: (i, k)) for k in range(nblk)]
    blocks = pl.pallas_call(
        body, name=name, grid=(r // rt,), in_specs=in_specs,
        out_specs=[pl.BlockSpec((rt * S5_Q, LANES), lambda i: (i, 0))] * nblk,
        out_shape=[jax.ShapeDtypeStruct((r * S5_Q, LANES), F32)] * nblk,
        scratch_shapes=[pltpu.VMEM((rt, LANES), F32)],
        compiler_params=_params(("parallel",)))(*([y] if add is None else [y] + [add] * nblk))
    return list(blocks) if as_blocks else jnp.concatenate(blocks, axis=1)


def _cat(ref):
    return jnp.concatenate([ref[h] for h in range(S5_HALVES)], axis=1)


def _put(ref, v):
    for h in range(S5_HALVES):
        ref[h] = v[:, h * LANES:(h + 1) * LANES]


def _mspec(gp, *shape):
    return pl.BlockSpec((gp,) + shape, lambda i: (i,) + (0,) * len(shape))


def _carry_spec(nck):
    return pl.BlockSpec((nck, 8, 4 * S5_P), lambda i: (0, i, 0))


def _carry_rows(ref, gl, bsz):
    return jnp.concatenate([ref[:, gl * bsz + b, :] for b in range(bsz)], axis=0)


def _carry_put(ref, gl, bsz, v):
    nck = v.shape[0] // bsz
    for b in range(bsz):
        ref[:, gl * bsz + b, :] = v[b * nck:(b + 1) * nck, :]


def s5_state_in(u, wt, *, bsz, name):
    g, _, r, _ = u.shape
    gp, nck = 8 // bsz, r // bsz

    def body(u_ref, w_ref, o_ref):
        for gl in range(gp):
            _carry_put(o_ref, gl, bsz, _bd(_cat(u_ref.at[gl]), w_ref[gl], 1, 0))

    return pl.pallas_call(
        body, name=name, grid=(g // gp,), in_specs=[_mspec(gp, S5_HALVES, r, LANES), _mspec(gp, S5_QC, 4 * S5_P)],
        out_specs=_carry_spec(nck), out_shape=jax.ShapeDtypeStruct((nck, g * bsz, 4 * S5_P), F32),
        compiler_params=_params(("parallel",)))(u, wt)


def _swap(h):
    return pltpu.roll(h, S5_P, 1)


def s5_carry_fwd(s, da, db, *, name):
    nck, rows, _ = s.shape
    w = 2 * S5_P

    def body(s_ref, da_ref, db_ref, h_ref):
        dirs = ((False, slice(0, w)), (True, slice(w, 2 * w)))
        coef = [(da_ref[:, cols], db_ref[:, cols]) for _, cols in dirs]

        def step(i, hs):
            new = []
            for (rev, cols), (a, b), h in zip(dirs, coef, hs):
                k = (nck - 1 - i) if rev else i
                h_ref[k, :, cols] = h
                new.append(a * h + b * _swap(h) + s_ref[k, :, cols])
            return tuple(new)

        z = jnp.zeros((rows, w), F32)
        lax.fori_loop(0, nck, step, (z, z), unroll=2)

    rt = min(2 * CARRY_ROWS, rows)
    big, small = pl.BlockSpec((nck, rt, 2 * w), lambda i: (0, i, 0)), pl.BlockSpec((rt, 2 * w), lambda i: (i, 0))
    rows = rt
    return pl.pallas_call(
        body, name=name, grid=(s.shape[1] // rt,), in_specs=[big, small, small], out_specs=big,
        out_shape=jax.ShapeDtypeStruct(s.shape, F32), compiler_params=_params(("parallel",)))(s, da, db)


def s5_carry_bwd(hin, dh, da, db, *, name):
    nck, rows, _ = hin.shape
    w = 2 * S5_P

    def body(h_ref, dh_ref, da_ref, db_ref, ds_ref, gda_ref, gdb_ref):
        dirs = ((False, slice(0, w)), (True, slice(w, 2 * w)))
        coef = [(da_ref[:, cols], db_ref[:, cols]) for _, cols in dirs]

        def step(i, carries):
            new = []
            for (rev, cols), (a, b), (g, ga, gb) in zip(dirs, coef, carries):
                k = i if rev else (nck - 1 - i)
                ds_ref[k, :, cols] = g
                h = h_ref[k, :, cols]
                new.append((dh_ref[k, :, cols] + a * g + _swap(b * g), ga + g * h, gb + g * _swap(h)))
            return tuple(new)

        z = jnp.zeros((rows, w), F32)
        res = lax.fori_loop(0, nck, step, ((z, z, z), (z, z, z)), unroll=2)
        for (_, cols), (_, ga, gb) in zip(dirs, res):
            gda_ref[:, cols] = ga
            gdb_ref[:, cols] = gb

    rt = min(CARRY_ROWS, rows)
    big, small = pl.BlockSpec((nck, rt, 2 * w), lambda i: (0, i, 0)), pl.BlockSpec((rt, 2 * w), lambda i: (i, 0))
    rows = rt
    return pl.pallas_call(
        body, name=name, grid=(hin.shape[1] // rt,), in_specs=[big, big, small, small], out_specs=[big, small, small],
        out_shape=[jax.ShapeDtypeStruct(hin.shape, F32), jax.ShapeDtypeStruct(da.shape, F32),
                   jax.ShapeDtypeStruct(da.shape, F32)],
        compiler_params=_params(("parallel",)))(hin, dh, da, db)


def s5_out(u, hin, tt, mt, *, bsz, name):
    g, _, r, _ = u.shape
    gp, nck = 8 // bsz, r // bsz

    def body(u_ref, h_ref, t_ref, m_ref, o_ref):
        for gl in range(gp):
            u_v, h_v = _cat(u_ref.at[gl]), _carry_rows(h_ref, gl, bsz)
            for half in range(S5_HALVES):
                cols = slice(half * LANES, (half + 1) * LANES)
                o_ref[gl, half] = _bd(u_v, t_ref[gl, half], 1, 0) + _bd(h_v, m_ref[gl, :, cols], 1, 0)

    cspec = _mspec(gp, S5_HALVES, r, LANES)
    return pl.pallas_call(
        body, name=name, grid=(g // gp,),
        in_specs=[cspec, _carry_spec(nck), _mspec(gp, S5_HALVES, S5_QC, LANES), _mspec(gp, 4 * S5_P, S5_QC)],
        out_specs=cspec, out_shape=jax.ShapeDtypeStruct((g, S5_HALVES, r, LANES), F32),
        compiler_params=_params(("parallel",)))(u, hin, tt, mt)


def s5_out_bwd(dy, u, hin, tt, mt, *, bsz, name):
    g, _, r, _ = u.shape
    gp, nck = 8 // bsz, r // bsz

    def body(dy_ref, u_ref, h_ref, t_ref, m_ref, dh_ref, dt_ref, dm_ref, du_ref):
        for gl in range(gp):
            dy_v, u_v = _cat(dy_ref.at[gl]), _cat(u_ref.at[gl])
            _carry_put(dh_ref, gl, bsz, _bd(dy_v, m_ref[gl], 1, 1))
            dm_ref[gl] = _bd(_carry_rows(h_ref, gl, bsz), dy_v, 0, 0)
            du = None
            for half in range(S5_HALVES):
                dy_h = dy_ref[gl, half]
                dt_ref[gl, half] = _bd(u_v, dy_h, 0, 0)
                part = _bd(dy_h, t_ref[gl, half], 1, 1)
                du = part if du is None else du + part
            _put(du_ref.at[gl], du)

    cspec, tspec = _mspec(gp, S5_HALVES, r, LANES), _mspec(gp, S5_HALVES, S5_QC, LANES)
    mspec = _mspec(gp, 4 * S5_P, S5_QC)
    return pl.pallas_call(
        body, name=name, grid=(g // gp,),
        in_specs=[cspec, cspec, _carry_spec(nck), tspec, mspec],
        out_specs=[_carry_spec(nck), tspec, mspec, cspec],
        out_shape=[jax.ShapeDtypeStruct((nck, g * bsz, 4 * S5_P), F32),
                   jax.ShapeDtypeStruct((g, S5_HALVES, S5_QC, LANES), F32),
                   jax.ShapeDtypeStruct((g, 4 * S5_P, S5_QC), F32), jax.ShapeDtypeStruct((g, S5_HALVES, r, LANES), F32)],
        compiler_params=_params(("parallel",)))(dy, u, hin, tt, mt)


def s5_state_in_bwd(ds, u, wt, du1, *, bsz, name):
    g, _, r, _ = u.shape
    gp, nck = 8 // bsz, r // bsz

    def body(ds_ref, u_ref, w_ref, du1_ref, du_ref, dw_ref):
        for gl in range(gp):
            ds_v = _carry_rows(ds_ref, gl, bsz)
            _put(du_ref.at[gl], _cat(du1_ref.at[gl]) + _bd(ds_v, w_ref[gl], 1, 1))
            dw_ref[gl] = _bd(_cat(u_ref.at[gl]), ds_v, 0, 0)

    cspec, wspec = _mspec(gp, S5_HALVES, r, LANES), _mspec(gp, S5_QC, 4 * S5_P)
    return pl.pallas_call(
        body, name=name, grid=(g // gp,),
        in_specs=[_carry_spec(nck), cspec, wspec, cspec], out_specs=[cspec, wspec],
        out_shape=[jax.ShapeDtypeStruct((g, S5_HALVES, r, LANES), F32), jax.ShapeDtypeStruct((g, S5_QC, 4 * S5_P), F32)],
        compiler_params=_params(("parallel",)))(ds, u, wt, du1)


def _s5_post(ypre, u, dvec, wv, wg, bv, bg, nw):
    g = _gelu(ypre + dvec * u)
    out = (dot_nn(g, wv) + bv) * jax.nn.sigmoid(dot_nn(g, wg) + bg)
    return (_rms(out, nw),)


def _ssd_post(y, z, nw):
    return (_rms(y * _silu(z), nw),)


def _block_diag(w):
    eye = jnp.eye(S5_G, dtype=w.dtype)
    return jnp.einsum('gcd,gh->gchd', w, eye).reshape(S5_W, S5_W)


def _diag_blocks(w):
    v = w.reshape(S5_G, S5_C, S5_G, S5_C)
    return v[jnp.arange(S5_G), :, jnp.arange(S5_G), :]


def _dt_rows(dt, bsz):
    seq = dt.shape[0] // bsz
    return jnp.transpose(dt.reshape(bsz, seq, 2, SGROUPS, HPG), (0, 3, 2, 4, 1)).reshape(bsz, SGROUPS, 2 * HPG, seq)


def _dt_from_rows(dr):
    bsz, _, _, seq = dr.shape
    return jnp.transpose(dr.reshape(bsz, SGROUPS, 2, HPG, seq), (0, 4, 2, 1, 3)).reshape(bsz * seq, 2 * HEADS)


def _head_params(f, b):
    return jnp.concatenate([f.reshape(SGROUPS, HPG), b.reshape(SGROUPS, HPG)], axis=1)[:, :, None]


def _head_grads(gr):
    v = gr.sum(0)[:, :, 0]
    return v[:, :HPG].reshape(HEADS), v[:, HPG:].reshape(HEADS)


def local_step(x, target, w):
    bsz, seq, d = x.shape
    t = bsz * seq
    x2, tgt2 = x.reshape(t, d), target.reshape(t, d)
    g = {}
    row = lambda v: v.reshape(1, -1)
    bf = lambda v: v.astype(BF16)

    w_in = _unshard(bf(w['w_in']), SHARDED['w_in'])
    cuts = [0, SSD_W, SSD_W + XBC_W, SSD_W + XBC_W + 2 * HEADS, w_in.shape[1]]
    w_in_parts = [w_in[:, a:b] for a, b in zip(cuts[:-1], cuts[1:])]
    norm_mix = row(w['norm_mix_w']) + w.get('token', 0.0)
    (hn,) = rowmap_fwd(lambda a, nw: (_rms(a, nw),), [x2], [norm_mix], [(d, BF16)], tm=512, name="rms_mix")
    z, xbc, dt, u = matmul_multi(hn, w_in_parts, name="in_proj")

    conv_w, conv_b = _unshard(w['ssd_conv_w'], SHARDED['ssd_conv_w']), row(w['ssd_conv_b'])
    act = ssd_conv_fwd(xbc, conv_w, conv_b, bsz=bsz, name="ssd_conv")
    dtr = _dt_rows(dt, bsz)
    prs = (_head_params(w['ssd_dt_bias_fwd'], w['ssd_dt_bias_bwd']),
           _head_params(w['ssd_a_log_fwd'], w['ssd_a_log_bwd']),
           _head_params(w['ssd_d'], jnp.zeros_like(w['ssd_d'])))
    act3 = act.reshape(bsz, seq, XBC_W)
    y_scan, ssd_states = ssd_scan_fwd(act3, dtr, prs, name="ssd_scan")
    y_scan = y_scan.reshape(t, SSD_W)
    ssd_nw = row(w['ssd_norm_w'])
    (y_ssd,) = rowmap_fwd(_ssd_post, [y_scan, z], [ssd_nw], [(SSD_W, BF16)], tm=512, name="ssd_post")

    s5_names = ['s5_lambda_re_fwd', 's5_lambda_im_fwd', 's5_log_step_fwd', 's5_lambda_re_bwd', 's5_lambda_im_bwd',
                's5_log_step_bwd', 's5_b_re', 's5_b_im', 's5_c_re_fwd', 's5_c_im_fwd', 's5_c_re_bwd', 's5_c_im_bwd']
    (kt, wt, mt, da, db), s5_pull = jax.vjp(_s5_operators, *[w[n] for n in s5_names])
    tt_b, wt_b, mt_b = s5_toeplitz(kt, name="s5_toeplitz"), bf(wt), bf(mt)
    da_r, db_r = jnp.repeat(da, bsz, axis=0), jnp.repeat(db, bsz, axis=0)
    uc = to_chunks(u, name="s5_to_chunks_u")
    hin = s5_carry_fwd(s5_state_in(uc, wt_b, bsz=bsz, name="s5_state_in"), da_r, db_r, name="s5_carry")
    ypre = from_chunks(s5_out(uc, hin, tt_b, mt_b, bsz=bsz, name="s5_out"), name="s5_from_chunks_y", as_blocks=True)
    glu_w = w['s5_glu_w']
    s5_par = [row(w['s5_d']), _block_diag(glu_w[:, :, :S5_C]), _block_diag(glu_w[:, :, S5_C:]),
              row(w['s5_glu_b'][:, :S5_C]), row(w['s5_glu_b'][:, S5_C:]), row(w['s5_norm_w'])]
    (y_s5,) = rowmap_fwd(_s5_post, [ypre, u], s5_par, [(S5_W, BF16)], name="s5_post")

    if 'late' in w:
        w = {**w, **w['late'](y_s5)}
    w_out = bf(w['w_out']).reshape(SSD_W + S5_W, d)
    norm_ffn = row(w['norm_ffn_w'])
    h1, hn2 = matmul_sum([y_ssd, y_s5], [w_out[:SSD_W], w_out[SSD_W:]], add=x2, norm_w=norm_ffn, name="out_proj")
    pad_c = FFN_PAD - FFN_BLK
    half = N_DEV // 2
    w_up3 = jnp.pad(bf(w['ffn_w_up']), ((0, 0), (0, 0), (0, pad_c)))
    w_down = jnp.pad(bf(w['ffn_w_down']).reshape(half, FFN_BLK, d), ((0, 0), (0, pad_c), (0, 0)))
    w_down = w_down.reshape(half * FFN_PAD, d)
    fconv_w = jnp.pad(w['ffn_conv_w'], ((0, 0), (0, 0), (0, pad_c)))
    fconv_w = jnp.transpose(fconv_w, (1, 0, 2)).reshape(FCONV, N_DEV * FFN_PAD)
    fconv_b = row(jnp.pad(w['ffn_conv_b'].reshape(N_DEV, FFN_BLK), ((0, 0), (0, pad_c))))
    up = matmul_cols(hn2, w_up3, out_dtype=BF16, name="ffn_up")
    fact = ffn_act_fwd(up, fconv_w, fconv_b, bsz=bsz, name="ffn_act")
    loss, dh2, g_nf = loss_head(h1, tgt2, row(w['norm_final_w']), matmul=(fact, w_down), tm=512, name="ffn_down_loss")
    g['norm_final_w'] = g_nf.reshape(-1)

    dfact = matmul_sum([dh2], [w_down], nt=True, tm=1024, name="ffn_down_dx")
    g_down = matmul_tn(fact, dh2, name="ffn_down_dw").reshape(half, FFN_PAD, d)[:, :FFN_BLK]
    g['ffn_w_down'] = g_down.reshape(N_DEV, FFN_BLK // 2, d)
    dval, dgate, dwv, dwg, dbv, dbg = ffn_act_bwd(up, dfact, fconv_w, fconv_b, bsz=bsz, name="ffn_act_bwd")
    g_cw = jnp.concatenate([dwv, dwg], axis=1).reshape(FCONV, N_DEV, FFN_PAD)[:, :, :FFN_BLK]
    g['ffn_conv_w'] = jnp.transpose(g_cw, (1, 0, 2))
    g['ffn_conv_b'] = jnp.concatenate([dbv, dbg], axis=1).reshape(N_DEV, FFN_PAD)[:, :FFN_BLK].reshape(-1)
    windows = [(dval, FFN_PAD, p) for p in range(half)] + [(dgate, FFN_PAD, p) for p in range(half)]
    g['ffn_w_up'] = jnp.concatenate([matmul_tn(hn2, dval, out_blocks=half, name="ffn_up_dw_val"),
                                     matmul_tn(hn2, dgate, out_blocks=half, name="ffn_up_dw_gate")],
                                    axis=0)[:, :, :FFN_BLK]
    send_early = w.get('on_grads')
    if send_early:
        norm_ffn = norm_ffn + send_early(g, ['ffn_w_up', 'ffn_w_down'])
    dh1, g_nffn = matmul_sum(windows, [(w_up3, p) for p in range(N_DEV)], nt=True, tm=256,
                             norm_bwd=(h1, norm_ffn, dh2), name="ffn_up_dx")
    g['norm_ffn_w'] = g_nffn.reshape(-1)

    dycat = matmul_sum([dh1], [w_out], nt=True, tm=1024, name="out_proj_dx")
    g['w_out'] = jnp.concatenate([matmul_tn(y_ssd, dh1, name="out_proj_dw_ssd"),
                                  matmul_tn(y_s5, dh1, name="out_proj_dw_s5")], axis=0).reshape(w['w_out'].shape)
    if send_early:
        ssd_nw = ssd_nw + send_early(g, ['w_out'])
    dy_scan, dz, g_snw = rowmap_bwd(_ssd_post, [y_scan, z], [ssd_nw], [(dycat, SSD_W, 0)], tm=512,
                                    name="ssd_post_bwd")
    g['ssd_norm_w'] = g_snw.reshape(-1)
    dypre, du_a, g_d, g_wv, g_wg, g_bv, g_bg, g_s5nw = rowmap_bwd(
        _s5_post, [ypre, u], s5_par, [(dycat, S5_W, SSD_W // S5_W)], name="s5_post_bwd")
    g['s5_d'], g['s5_norm_w'] = g_d.reshape(-1), g_s5nw.reshape(-1)
    g['s5_glu_w'] = jnp.concatenate([_diag_blocks(g_wv), _diag_blocks(g_wg)], axis=-1)
    g['s5_glu_b'] = jnp.concatenate([g_bv.reshape(S5_G, S5_C), g_bg.reshape(S5_G, S5_C)], axis=-1)

    dyc = to_chunks(dypre, name="s5_to_chunks_dy")
    dhin, dtt, dmt, du1 = s5_out_bwd(dyc, uc, hin, tt_b, mt_b, bsz=bsz, name="s5_out_bwd")
    ds, gda, gdb = s5_carry_bwd(hin, dhin, da_r, db_r, name="s5_carry_bwd")
    duc, dwt = s5_state_in_bwd(ds, uc, wt_b, du1, bsz=bsz, name="s5_state_in_bwd")
    du = from_chunks(duc, add=du_a, name="s5_from_chunks_du")
    fold = lambda v: v.reshape(S5_G, bsz, -1).sum(1)
    dkt = s5_toeplitz_bwd(dtt, name="s5_toeplitz_bwd")
    for n, gv in zip(s5_names, s5_pull((dkt, dwt, dmt, fold(gda), fold(gdb)))):
        g[n] = gv

    dxs, dbm, dcm, ddtr, gbr, gar, gdk = ssd_scan_bwd(
        act3, dtr, prs, ssd_states, dy_scan.reshape(bsz, seq, SSD_W), name="ssd_scan_bwd")
    g['ssd_dt_bias_fwd'], g['ssd_dt_bias_bwd'] = _head_grads(gbr)
    g['ssd_a_log_fwd'], g['ssd_a_log_bwd'] = _head_grads(gar)
    g['ssd_d'] = _head_grads(gdk)[0]
    dparts_act = [v.reshape(t, v.shape[-1]) for v in (dxs, dbm, dcm)]
    dxbc, g_cw, g_cb = ssd_conv_bwd(xbc, dparts_act, conv_w, conv_b, bsz=bsz, name="ssd_conv_bwd")
    g['ssd_conv_w'] = _shard_rows(g_cw, SHARDED['ssd_conv_w']).reshape(w['ssd_conv_w'].shape)
    g['ssd_conv_b'] = g_cb.reshape(-1)
    ddt = _dt_from_rows(ddtr)

    if send_early:
        ddt = ddt + send_early(g, [], loss=loss)
    dparts = [dz, dxbc, ddt, du]
    g_in = jnp.concatenate([matmul_tn(hn, dp, name=f"in_proj_dw_{i}") for i, dp in enumerate(dparts)], axis=1)
    g['w_in'] = _shard_rows(g_in, SHARDED['w_in']).reshape(w['w_in'].shape)
    if send_early:
        dparts[2] = ddt + send_early(g, ['w_in'])
    dx, g_nmix = matmul_sum(dparts, w_in_parts, nt=True, tm=256, norm_bwd=(x2, norm_mix, dh1), name="in_proj_dx")
    g['norm_mix_w'] = g_nmix.reshape(-1)
    return loss, dx.reshape(bsz, seq, d), g


ANY = pl.BlockSpec(memory_space=pl.ANY)


def all_gather(shards, *, name):
    n = len(shards)

    def body(*refs):
        x_refs, out_refs = refs[:n], refs[n:2 * n]
        send_sems, recv_sems, local_sems = refs[2 * n:]
        x, y, c = lax.axis_index("x"), lax.axis_index("y"), lax.axis_index("c")
        me, sibling = (x, y, c), (x, y, 1 - c)
        chips = [(1 - x, y), (x, 1 - y), (1 - x, 1 - y)]

        def copy(k, j, block, to, own=False):
            dst = out_refs[j].at[4 * block[0] + 2 * block[1] + block[2]]
            return pltpu.make_async_remote_copy(
                src_ref=x_refs[j] if own else dst, dst_ref=dst,
                send_sem=send_sems.at[k, j], recv_sem=recv_sems.at[k, j], device_id=to, device_id_type=MESH)

        mine = [pltpu.make_async_copy(x_refs[j], out_refs[j].at[4 * x + 2 * y + c], local_sems.at[j]) for j in range(n)]
        first = [copy(0, j, me, sibling, own=True) for j in range(n)]
        first += [copy(1 + i, j, me, (*chip, c), own=True) for i, chip in enumerate(chips) for j in range(n)]
        for cp in mine + first:
            cp.start()
        passed = []
        for i, chip in enumerate(chips):
            for j in range(n):
                copy(1 + i, j, (*chip, c), me).wait_recv()
                passed.append(copy(4 + i, j, (*chip, c), sibling))
                passed[-1].start()
        for j in range(n):
            copy(0, j, sibling, me).wait_recv()
        for i, chip in enumerate(chips):
            for j in range(n):
                copy(4 + i, j, (*chip, 1 - c), me).wait_recv()
        for cp in first + passed:
            cp.wait_send()
        for cp in mine:
            cp.wait()

    return pl.pallas_call(
        body, name=name, out_shape=[jax.ShapeDtypeStruct((N_DEV,) + s.shape, s.dtype) for s in shards],
        in_specs=[ANY] * n, out_specs=[ANY] * n,
        scratch_shapes=[pltpu.SemaphoreType.DMA((7, n)), pltpu.SemaphoreType.DMA((7, n)),
                        pltpu.SemaphoreType.DMA((n,))],
    )(*shards)


HBM_SPEC = pl.BlockSpec(memory_space=pltpu.HBM)
SEM_SPEC = pl.BlockSpec(memory_space=pltpu.SEMAPHORE)
SPLIT_PARAMS = pltpu.CompilerParams(has_side_effects=pltpu.SideEffectType.DATAFLOW_SIDE_EFFECTING)


def _peer_copies(src_refs, land_refs, send_sems, recv_sems, indexed):
    x, y, c = lax.axis_index("x"), lax.axis_index("y"), lax.axis_index("c")
    me = 4 * x + 2 * y + c
    copies = []
    for k in range(1, N_DEV):
        px = (1 - x) if k & 4 else x
        py = (1 - y) if k & 2 else y
        pc = (1 - c) if k & 1 else c
        for j, (src, land) in enumerate(zip(src_refs, land_refs)):
            sem = (k - 1) * len(src_refs) + j
            copies.append(pltpu.make_async_remote_copy(
                src_ref=src.at[4 * px + 2 * py + pc] if indexed else src, dst_ref=land.at[me],
                send_sem=send_sems.at[sem], recv_sem=recv_sems.at[sem],
                device_id=(px, py, pc), device_id_type=MESH))
    return copies


def scatter_start(srcs, *, name, indexed):
    n = len(srcs)
    lands = [lax.empty(s.shape if indexed else (N_DEV,) + s.shape, s.dtype) for s in srcs]

    def body(*refs):
        send_sems, recv_sems = refs[2 * n], refs[2 * n + 1]
        for cp in _peer_copies(refs[:n], refs[n:2 * n], send_sems, recv_sems, indexed):
            cp.start()
        refs[-1][...] = jnp.zeros_like(refs[-1])

    hbm = lambda a: pltpu.HBM(a.shape, a.dtype)
    sems = pltpu.SemaphoreType.DMA(((N_DEV - 1) * n,))
    res = pl.pallas_call(
        body, name=name,
        out_shape=(sems, sems, *[hbm(a) for a in srcs + lands], jax.ShapeDtypeStruct((8, LANES), F32)),
        in_specs=[HBM_SPEC] * (2 * n),
        out_specs=(SEM_SPEC, SEM_SPEC, *[HBM_SPEC] * (2 * n), pl.BlockSpec(memory_space=pltpu.VMEM)),
        input_output_aliases={i: 2 + i for i in range(2 * n)}, compiler_params=SPLIT_PARAMS,
    )(*[pltpu.with_memory_space_constraint(a, pltpu.HBM) for a in srcs + lands])
    return res[0], res[1], list(res[2:2 + n]), list(res[2 + n:2 + 2 * n]), res[-1]


def scatter_wait(send_sems, recv_sems, srcs, lands, after, *, name, indexed):
    n = len(srcs)

    def body(*refs):
        for cp in _peer_copies(refs[:n], refs[n:2 * n], refs[2 * n], refs[2 * n + 1], indexed):
            cp.wait_send()
            cp.wait_recv()

    hbm = lambda a: pltpu.HBM(a.shape, a.dtype)
    res = pl.pallas_call(
        body, name=name, out_shape=tuple(hbm(a) for a in srcs + lands),
        in_specs=[HBM_SPEC] * (2 * n) + [SEM_SPEC, SEM_SPEC, ANY], out_specs=tuple([HBM_SPEC] * (2 * n)),
        input_output_aliases={i: i for i in range(2 * n)}, compiler_params=SPLIT_PARAMS,
    )(*srcs, *lands, send_sems, recv_sems, after)
    return list(res[:n]), list(res[n:])


def _adam_rows(r, c):
    fits = [t for t in range(8, r + 1, 8) if r % t == 0 and N_DEV * t * c * 4 <= 6 * 2 ** 20]
    return max(fits) if fits else r


def adamw(recv, w, m, v, *, name):
    _, r, n = recv.shape
    tr = _adam_rows(r, n)

    def body(r_ref, w_ref, m_ref, v_ref, g_ref, d_ref, nm_ref, nv_ref):
        g = r_ref[0].astype(F32)
        for s in range(1, N_DEV):
            g = g + r_ref[s].astype(F32)
        m_new = ADAM_B1 * m_ref[...] + (1.0 - ADAM_B1) * g
        v_new = ADAM_B2 * v_ref[...] + (1.0 - ADAM_B2) * jnp.square(g)
        m_hat = m_new / (1.0 - ADAM_B1 ** ADAM_STEP)
        v_hat = v_new / (1.0 - ADAM_B2 ** ADAM_STEP)
        g_ref[...] = g
        d_ref[...] = -ADAM_LR * (m_hat / (jnp.sqrt(v_hat) + ADAM_EPS) + ADAM_WD * w_ref[...])
        nm_ref[...] = m_new
        nv_ref[...] = v_new

    blk = pl.BlockSpec((tr, n), lambda i: (i, 0))
    return pl.pallas_call(
        body, name=name, grid=(r // tr,), in_specs=[pl.BlockSpec((N_DEV, tr, n), lambda i: (0, i, 0)), blk, blk, blk],
        out_specs=[blk] * 4, out_shape=[jax.ShapeDtypeStruct((r, n), F32)] * 4,
        compiler_params=_params(("parallel",)))(recv, w, m, v)


def _shard_rows(full, axis):
    if axis == 0:
        return full.reshape(N_DEV, -1)
    r, c = full.shape
    return jnp.transpose(full.reshape(r, N_DEV, c // N_DEV), (1, 0, 2)).reshape(N_DEV, -1)


def _unshard(blocks, axis):
    if axis == 0:
        return blocks.reshape(-1, blocks.shape[-1])
    return jnp.transpose(blocks, (1, 0, 2)).reshape(blocks.shape[1], -1)


def kernel(x, norm_mix_w, w_in, ssd_conv_w, ssd_conv_b, ssd_dt_bias_fwd, ssd_dt_bias_bwd, ssd_a_log_fwd, ssd_a_log_bwd, ssd_d, ssd_norm_w, s5_lambda_re_fwd, s5_lambda_im_fwd, s5_log_step_fwd, s5_lambda_re_bwd, s5_lambda_im_bwd, s5_log_step_bwd, s5_b_re, s5_b_im, s5_c_re_fwd, s5_c_im_fwd, s5_c_re_bwd, s5_c_im_bwd, s5_d, s5_glu_w, s5_glu_b, s5_norm_w, w_out, norm_ffn_w, ffn_w_up, ffn_conv_w, ffn_conv_b, ffn_w_down, norm_final_w, loss_target, m_norm_mix_w, m_w_in, m_ssd_conv_w, m_ssd_conv_b, m_ssd_dt_bias_fwd, m_ssd_dt_bias_bwd, m_ssd_a_log_fwd, m_ssd_a_log_bwd, m_ssd_d, m_ssd_norm_w, m_s5_lambda_re_fwd, m_s5_lambda_im_fwd, m_s5_log_step_fwd, m_s5_lambda_re_bwd, m_s5_lambda_im_bwd, m_s5_log_step_bwd, m_s5_b_re, m_s5_b_im, m_s5_c_re_fwd, m_s5_c_im_fwd, m_s5_c_re_bwd, m_s5_c_im_bwd, m_s5_d, m_s5_glu_w, m_s5_glu_b, m_s5_norm_w, m_w_out, m_norm_ffn_w, m_ffn_w_up, m_ffn_conv_w, m_ffn_conv_b, m_ffn_w_down, m_norm_final_w, v_norm_mix_w, v_w_in, v_ssd_conv_w, v_ssd_conv_b, v_ssd_dt_bias_fwd, v_ssd_dt_bias_bwd, v_ssd_a_log_fwd, v_ssd_a_log_bwd, v_ssd_d, v_ssd_norm_w, v_s5_lambda_re_fwd, v_s5_lambda_im_fwd, v_s5_log_step_fwd, v_s5_lambda_re_bwd, v_s5_lambda_im_bwd, v_s5_log_step_bwd, v_s5_b_re, v_s5_b_im, v_s5_c_re_fwd, v_s5_c_im_fwd, v_s5_c_re_bwd, v_s5_c_im_bwd, v_s5_d, v_s5_glu_w, v_s5_glu_b, v_s5_norm_w, v_w_out, v_norm_ffn_w, v_ffn_w_up, v_ffn_conv_w, v_ffn_conv_b, v_ffn_w_down, v_norm_final_w):
    args = dict(locals())
    strip = lambda n, v: v if n == 'norm_final_w' else v[0]
    w = {n: strip(n, args[n]) for n in WEIGHTS}

    mats = ['w_in', 'w_out', 'ffn_w_up', 'ffn_w_down']
    convs = ['ssd_conv_w', 'ffn_conv_w']
    shard = lambda n: w[n].astype(BF16) if n in mats else w[n]
    early, late = ['w_in', 'ssd_conv_w'], ['w_out', 'ffn_w_up', 'ffn_w_down', 'ffn_conv_w']
    full = dict(w)
    full.update(zip(early, all_gather([shard(n) for n in early], name="weight_all_gather")))
    ssem, rsem, src_thru, land_thru, token = scatter_start([shard(n) for n in late], name="weight_gather_start",
                                                           indexed=False)
    me = 4 * lax.axis_index("x") + 2 * lax.axis_index("y") + lax.axis_index("c")

    def late_weights(after):
        own, landed = scatter_wait(ssem, rsem, src_thru, land_thru, after, name="weight_gather_wait", indexed=False)
        return {n: lax.dynamic_update_index_in_dim(l, o, me, 0) for n, o, l in zip(late, own, landed)}

    full['late'], full['token'] = late_weights, token[:1, :1]

    pending = []
    last = 'norm_mix_w'
    small = convs + [n for n in WEIGHTS if n not in SHARDED and n != last]
    slot = {n: -(-w[n].size // (8 * LANES)) * 8 for n in small}
    used = sum(slot.values()) + 8
    nrow = -(-used // PACK_ROWS) * PACK_ROWS

    def tiles(v, n):
        return jnp.pad(v, ((0, 0), (0, slot[n] * LANES - v.shape[1]))).reshape(v.shape[0], slot[n], LANES)

    def send_early(grads, names, loss=None):
        srcs = [grads[n].astype(BF16) for n in names]
        if loss is not None:
            pieces = [tiles(grads[n].reshape(N_DEV, -1), n) if n in SHARDED else
                      jnp.broadcast_to(tiles(grads[n].reshape(1, -1), n), (N_DEV, slot[n], LANES)) for n in small]
            pieces.append(jnp.broadcast_to(jnp.pad(loss.reshape(1, 1, 1), ((0, 0), (0, 7), (0, LANES - 1))),
                                           (N_DEV, 8, LANES)))
            pieces.append(jnp.zeros((N_DEV, nrow - used, LANES), F32))
            srcs.append(jnp.concatenate(pieces, axis=1))
            names = names + ['small']
        started = scatter_start(srcs, name="grad_start_" + names[0], indexed=True)
        pending.append((names,) + started[:4])
        return started[4][:1, :1]

    full['on_grads'] = send_early
    loss, grad_x, g = local_step(x, loss_target, full)

    last_send = jnp.broadcast_to(g[last].reshape(1, -1, LANES), (N_DEV, g[last].size // LANES, LANES))
    last_started = scatter_start([last_send], name="grad_start_" + last, indexed=True)
    recv, outs = {}, [{}, {}, {}, {}]

    def arrived(names, started, after):
        own, landed = scatter_wait(*started, after, name="grad_wait_" + names[0], indexed=True)
        for n, o, l in zip(names, own, landed):
            recv[n] = lax.dynamic_update_index_in_dim(l, lax.dynamic_index_in_dim(o, me, 0, keepdims=False), me, 0)

    def update(n):
        shape = recv[n].shape[1:]
        res = adamw(recv[n], *[strip(n, args[p + n]).reshape(shape) for p in ('', 'm_', 'v_')], name="adamw_" + n)
        for o, p in zip(outs, res):
            o[n] = p.reshape(args[n].shape)

    for names, *started in pending:
        arrived(names, started, last_started[4])
    for n in mats:
        update(n)

    def pack(prefix):
        vals = [tiles(strip(n, args[prefix + n]).reshape(1, -1), n)[0] for n in small]
        return jnp.concatenate(vals + [jnp.zeros((nrow - used + 8, LANES), F32)], axis=0)

    packed = adamw(recv['small'], pack(''), pack('m_'), pack('v_'), name="adamw_small")
    arrived([last], last_started[:4], packed[1])
    update(last)
    off = 0
    for n in small:
        for o, p in zip(outs, packed):
            o[n] = p[off:off + slot[n]].reshape(-1)[:w[n].size].reshape(args[n].shape)
        off += slot[n]
    loss_out = packed[0][off, 0].reshape(())
    return (loss_out, grad_x, *[o[n] for o in outs for n in WEIGHTS])
```

```python
import functools

import jax
import jax.numpy as jnp
from jax import lax
from jax.experimental import pallas as pl
from jax.experimental.pallas import tpu as pltpu

F32, BF16 = jnp.float32, jnp.bfloat16
N_DEV = 8
D_MODEL = 1024
SSD_W, HEADS, HDIM, SGROUPS, HPG, NSTATE, SCONV, QC = 1024, 16, 64, 4, 4, 128, 5, 128
XBC_W = SSD_W + 2 * SGROUPS * NSTATE
S5_W, S5_G, S5_C, S5_P, S5_Q = 512, 32, 16, 64, 16
S5_QC = S5_Q * S5_C
CARRY_ROWS = 32
DFF, FCONV = 2816, 3
FFN_BLK, FFN_PAD = 704, 768
EPS = 1e-6
ADAM_LR, ADAM_B1, ADAM_B2, ADAM_EPS, ADAM_WD, ADAM_STEP = 0.001, 0.9, 0.999, 1e-08, 0.01, 10
LANES = 128
MESH = pl.DeviceIdType.MESH

WEIGHTS = ['norm_mix_w', 'w_in', 'ssd_conv_w', 'ssd_conv_b', 'ssd_dt_bias_fwd', 'ssd_dt_bias_bwd', 'ssd_a_log_fwd',
           'ssd_a_log_bwd', 'ssd_d', 'ssd_norm_w', 's5_lambda_re_fwd', 's5_lambda_im_fwd', 's5_log_step_fwd',
           's5_lambda_re_bwd', 's5_lambda_im_bwd', 's5_log_step_bwd', 's5_b_re', 's5_b_im', 's5_c_re_fwd', 's5_c_im_fwd',
           's5_c_re_bwd', 's5_c_im_bwd', 's5_d', 's5_glu_w', 's5_glu_b', 's5_norm_w', 'w_out', 'norm_ffn_w', 'ffn_w_up',
           'ffn_conv_w', 'ffn_conv_b', 'ffn_w_down', 'norm_final_w']
SHARDED = {'w_in': 1, 'ssd_conv_w': 1, 'w_out': 0, 'ffn_w_up': 1, 'ffn_conv_w': 1, 'ffn_w_down': 0}
FULL_SHAPE = {'w_in': (1024, 3616), 'ssd_conv_w': (5, 2048), 'w_out': (1536, 1024), 'ffn_w_up': (1024, 5632),
              'ffn_conv_w': (3, 5632), 'ffn_w_down': (2816, 1024)}
PACK_ROWS = 512


def _pick(n, cap=1536):
    if n <= cap:
        return n
    return max(t for t in range(LANES, cap + 1, LANES) if n % t == 0)


def _params(sem):
    return pltpu.CompilerParams(dimension_semantics=sem)


def _bd(a, b, ca, cb):
    return lax.dot_general(a.astype(BF16), b.astype(BF16), (((ca,), (cb,)), ((), ())), preferred_element_type=F32)


@jax.custom_vjp
def dot_nn(a, b):
    return _bd(a, b, 1, 0)


dot_nn.defvjp(lambda a, b: (_bd(a, b, 1, 0), (a, b)),
              lambda r, g: (_bd(g, r[1], 1, 1).astype(r[0].dtype), _bd(r[0], g, 0, 0).astype(r[1].dtype)))


@jax.custom_vjp
def dot_nt(a, b):
    return _bd(a, b, 1, 1)


dot_nt.defvjp(lambda a, b: (_bd(a, b, 1, 1), (a, b)),
              lambda r, g: (_bd(g, r[1], 1, 0).astype(r[0].dtype), _bd(g, r[0], 0, 0).astype(r[1].dtype)))


@jax.custom_vjp
def dot_tn(a, b):
    return _bd(a, b, 0, 0)


dot_tn.defvjp(lambda a, b: (_bd(a, b, 0, 0), (a, b)),
              lambda r, g: (_bd(r[1], g, 1, 1).astype(r[0].dtype), _bd(r[0], g, 1, 0).astype(r[1].dtype)))


def _rows2(v):
    h = v.shape[0] // 2
    return v[:h], v[h:]


def _cols2(v):
    h = v.shape[1] // 2
    return v[:, :h], v[:, h:]


@jax.custom_vjp
def dot2_nn(la, lb, x):
    return _rows2(_bd(jnp.concatenate([la, lb], axis=0), x, 1, 0))


def _dot2_nn_bwd(res, g):
    la, lb, x = res
    gcat, lcat = jnp.concatenate(g, axis=0), jnp.concatenate([la, lb], axis=0)
    return (*_rows2(_bd(gcat, x, 1, 1)), _bd(lcat, gcat, 0, 0))


dot2_nn.defvjp(lambda la, lb, x: (dot2_nn(la, lb, x), (la, lb, x)), _dot2_nn_bwd)


@jax.custom_vjp
def dot_nt2(c, p0, p1):
    return _cols2(_bd(c, jnp.concatenate([p0, p1], axis=0), 1, 1))


def _dot_nt2_bwd(res, g):
    c, p0, p1 = res
    gcat = jnp.concatenate(g, axis=1)
    return (_bd(gcat, jnp.concatenate([p0, p1], axis=0), 1, 0), *_rows2(_bd(gcat, c, 0, 0)))


dot_nt2.defvjp(lambda c, p0, p1: (dot_nt2(c, p0, p1), (c, p0, p1)), _dot_nt2_bwd)


@jax.custom_vjp
def dot_tn2(a0, a1, b):
    return _rows2(_bd(jnp.concatenate([a0, a1], axis=1), b, 0, 0))


def _dot_tn2_bwd(res, g):
    a0, a1, b = res
    gcat, acat = jnp.concatenate(g, axis=0), jnp.concatenate([a0, a1], axis=1)
    return (*_cols2(_bd(b, gcat, 1, 1)), _bd(acat, gcat, 1, 0))


dot_tn2.defvjp(lambda a0, a1, b: (dot_tn2(a0, a1, b), (a0, a1, b)), _dot_tn2_bwd)


def _split3(x):
    hi = x.astype(BF16)
    r = x - hi.astype(F32)
    mid = r.astype(BF16)
    lo = (r - mid.astype(F32)).astype(BF16)
    return hi, mid, lo


def _cum_matrix(q, upper):
    ri = lax.broadcasted_iota(jnp.int32, (q, q), 0)
    ci = lax.broadcasted_iota(jnp.int32, (q, q), 1)
    return jnp.where((ci >= ri) if upper else (ci <= ri), 1.0, 0.0).astype(BF16)


def _exact_right(x, mat):
    return sum(jnp.dot(p, mat, preferred_element_type=F32) for p in _split3(x))


@functools.partial(jax.custom_vjp, nondiff_argnums=(1,))
def cum_row(x, rev):
    return _exact_right(x, _cum_matrix(x.shape[1], not rev))


cum_row.defvjp(lambda x, rev: (cum_row(x, rev), None),
               lambda rev, _, g: (_exact_right(g, _cum_matrix(g.shape[1], rev)),))


def _softplus(x):
    return jnp.maximum(x, 0.0) + jnp.log(1.0 + jnp.exp(-jnp.abs(x)))


def _silu(x):
    return x * jax.nn.sigmoid(x)


def _gelu(x):
    return 0.5 * x * (1.0 + jnp.tanh(0.7978845608028654 * (x + 0.044715 * (x * x * x))))


def _rms(x, w):
    xf = x.astype(F32)
    return xf * lax.rsqrt(jnp.mean(xf * xf, axis=-1, keepdims=True) + EPS) * w


def matmul_sum(a_list, b_list, *, name, out_dtype=F32, add=None, tm=512, nt=False, norm_w=None, norm_bwd=None):
    a_arrs = [a[0] if isinstance(a, tuple) else a for a in a_list]
    b_arrs = [b[0] if isinstance(b, tuple) else b for b in b_list]
    m, n = a_arrs[0].shape[0], b_arrs[0].shape[-2 if nt else -1]
    tm, tn, k = min(tm, m), _pick(n), len(a_list)
    assert (norm_w is None and norm_bwd is None) or tn == n

    def body(*refs):
        acc = None
        for a_ref, b_ref in zip(refs[:k], refs[k:2 * k]):
            p = _bd(a_ref[...], b_ref[...], 1, 1 if nt else 0)
            acc = p if acc is None else acc + p
        if add is not None:
            acc = acc + refs[2 * k][...]
        if norm_bwd is not None:
            x_ref, w_ref, res_ref, dx_ref, dw_ref = refs[-5:]
            dx, dw = jax.vjp(_rms, x_ref[...], w_ref[...])[1](acc)
            dx_ref[...] = dx + res_ref[...]

            @pl.when(pl.program_id(0) == 0)
            def _():
                dw_ref[...] = jnp.zeros_like(dw_ref)

            dw_ref[...] += dw
        elif norm_w is not None:
            refs[-2][...] = acc.astype(out_dtype)
            refs[-1][...] = _rms(acc, refs[-3][...]).astype(BF16)
        else:
            refs[-1][...] = acc.astype(out_dtype)

    def a_spec(a):
        if isinstance(a, tuple):
            return pl.BlockSpec((tm, a[1]), lambda i, j, blk=a[2]: (i, blk))
        return pl.BlockSpec((tm, a.shape[1]), lambda i, j: (i, 0))

    def b_spec(b):
        arr, p = b if isinstance(b, tuple) else (b, None)
        kk = arr.shape[-1 if nt else -2]
        shape, idx = ((tn, kk), lambda j: (j, 0)) if nt else ((kk, tn), lambda j: (0, j))
        if p is None:
            return pl.BlockSpec(shape, lambda i, j: idx(j))
        return pl.BlockSpec((None,) + shape, lambda i, j, p=p: (p,) + idx(j))

    in_specs = [a_spec(a) for a in a_list] + [b_spec(b) for b in b_list]
    args = a_arrs + b_arrs
    if add is not None:
        in_specs.append(pl.BlockSpec((tm, tn), lambda i, j: (i, j)))
        args.append(add)
    out_spec, out_shape = pl.BlockSpec((tm, tn), lambda i, j: (i, j)), jax.ShapeDtypeStruct((m, n), out_dtype)
    if norm_w is not None:
        in_specs.append(pl.BlockSpec(norm_w.shape, lambda i, j: (0, 0)))
        args.append(norm_w)
        out_spec, out_shape = [out_spec, out_spec], [out_shape, jax.ShapeDtypeStruct((m, n), BF16)]
    sem = ("parallel", "parallel")
    if norm_bwd is not None:
        x, w, res = norm_bwd
        wspec = pl.BlockSpec(w.shape, lambda i, j: (0, 0))
        in_specs += [out_spec, wspec, out_spec]
        args += [x, w, res]
        out_spec, out_shape = [out_spec, wspec], [jax.ShapeDtypeStruct((m, n), F32), jax.ShapeDtypeStruct(w.shape, F32)]
        sem = ("arbitrary", "arbitrary")
    return pl.pallas_call(
        body, name=name, grid=(m // tm, n // tn), in_specs=in_specs, out_specs=out_spec, out_shape=out_shape,
        compiler_params=_params(sem))(*args)


def matmul_multi(a, b_list, *, name, tm=512):
    m, kk = a.shape
    tm, nb = min(tm, m), len(b_list)

    def body(a_ref, *refs):
        a_v = a_ref[...]
        for b_ref, o_ref in zip(refs[:nb], refs[nb:]):
            o_ref[...] = _bd(a_v, b_ref[...], 1, 0)

    return pl.pallas_call(
        body, name=name, grid=(m // tm,),
        in_specs=[pl.BlockSpec((tm, kk), lambda i: (i, 0))] + [_full_spec(b) for b in b_list],
        out_specs=[pl.BlockSpec((tm, b.shape[1]), lambda i: (i, 0)) for b in b_list],
        out_shape=[jax.ShapeDtypeStruct((m, b.shape[1]), F32) for b in b_list],
        compiler_params=_params(("parallel",)))(a, *b_list)


def matmul_cols(a, b3, *, name, out_dtype=F32, tm=2048):
    m, kk = a.shape
    p, _, nb = b3.shape
    tm, tn = min(tm, m), _pick(nb, 768)
    per = nb // tn

    def body(a_ref, b_ref, o_ref):
        o_ref[...] = _bd(a_ref[...], b_ref[...], 1, 0).astype(out_dtype)

    return pl.pallas_call(
        body, name=name, grid=(m // tm, p * per),
        in_specs=[pl.BlockSpec((tm, kk), lambda i, j: (i, 0)),
                  pl.BlockSpec((None, kk, tn), lambda i, j: (j // per, 0, j % per))],
        out_specs=pl.BlockSpec((tm, tn), lambda i, j: (i, j)),
        out_shape=jax.ShapeDtypeStruct((m, p * nb), out_dtype),
        compiler_params=_params(("parallel", "parallel")))(a, b3)


def matmul_tn(a, b, *, name, tm=1024, out_blocks=None):
    m, k = a.shape
    n = b.shape[1]
    nb = n // (out_blocks or 1)
    tm, tk, tn = min(tm, m), _pick(k), _pick(nb, 768 if out_blocks else 1536)
    per = nb // tn

    def body(a_ref, b_ref, o_ref):
        @pl.when(pl.program_id(2) == 0)
        def _():
            o_ref[...] = jnp.zeros_like(o_ref)

        o_ref[...] += _bd(a_ref[...], b_ref[...], 0, 0)

    if out_blocks:
        out_spec = pl.BlockSpec((None, tk, tn), lambda i, j, t: (j // per, i, j % per))
        out_shape = jax.ShapeDtypeStruct((out_blocks, k, nb), F32)
    else:
        out_spec = pl.BlockSpec((tk, tn), lambda i, j, t: (i, j))
        out_shape = jax.ShapeDtypeStruct((k, n), F32)
    return pl.pallas_call(
        body, name=name, grid=(k // tk, n // tn, m // tm),
        in_specs=[pl.BlockSpec((tm, tk), lambda i, j, t: (t, i)), pl.BlockSpec((tm, tn), lambda i, j, t: (t, j))],
        out_specs=out_spec, out_shape=out_shape,
        compiler_params=_params(("parallel", "parallel", "arbitrary")))(a, b)


def _row_spec(r, tm):
    if isinstance(r, tuple):
        arr, width, blk = r
        return arr, pl.BlockSpec((tm, width), lambda i, blk=blk: (i, blk))
    return r, pl.BlockSpec((tm, r.shape[1]), lambda i: (i, 0))


def _full_spec(p):
    return pl.BlockSpec(p.shape, lambda i: (0,) * p.ndim)


def _expand_rows(rows, tm):
    arrays, specs, counts, widths = [], [], [], []
    for r in rows:
        parts = [_row_spec(p, tm) for p in (r if isinstance(r, list) else [r])]
        arrays += [a for a, _ in parts]
        specs += [s for _, s in parts]
        counts.append(len(parts))
        widths.append(sum(s.block_shape[1] for _, s in parts))
    return arrays, specs, counts, widths


def _row_values(refs, counts):
    vals, k = [], 0
    for c in counts:
        parts = [refs[k + j][...] for j in range(c)]
        vals.append(parts[0] if c == 1 else jnp.concatenate(parts, axis=1))
        k += c
    return vals


def _rows_of(rows):
    first = rows[0][0] if isinstance(rows[0], list) else rows[0]
    return (first[0] if isinstance(first, tuple) else first).shape[0]


def rowmap_fwd(fn, rows, params, outs, *, name, tm=256):
    m = _rows_of(rows)
    tm = min(tm, m)
    arrays, specs, counts, _ = _expand_rows(rows, tm)
    nin, npar = len(arrays), len(params)

    def body(*refs):
        res = fn(*_row_values(refs[:nin], counts), *[r[...] for r in refs[nin:nin + npar]])
        for o_ref, v in zip(refs[nin + npar:], res):
            o_ref[...] = v.astype(o_ref.dtype)

    return pl.pallas_call(
        body, name=name, grid=(m // tm,), in_specs=specs + [_full_spec(p) for p in params],
        out_specs=[pl.BlockSpec((tm, c), lambda i: (i, 0)) for c, _ in outs],
        out_shape=[jax.ShapeDtypeStruct((m, c), dt) for c, dt in outs],
        compiler_params=_params(("parallel",)))(*arrays, *params)


def rowmap_bwd(fn, rows, params, cts, *, name, row_dtypes=None, add=None, tm=256):
    m = _rows_of(rows)
    tm = min(tm, m)
    arrays, specs, counts, widths = _expand_rows(rows, tm)
    cp = [_row_spec(c, tm) for c in cts]
    nin, nr, npar, nc = len(arrays), len(rows), len(params), len(cts)
    row_dtypes = row_dtypes or [F32] * nr

    def body(*refs):
        ins = _row_values(refs[:nin], counts) + [r[...] for r in refs[nin:nin + npar]]
        ins = [v.astype(F32) for v in ins]
        ct = tuple(r[...].astype(F32) for r in refs[nin + npar:nin + npar + nc])
        base = nin + npar + nc
        extra = None
        if add is not None:
            extra = refs[base][...]
            base += 1
        _, pull = jax.vjp(fn, *ins)
        grads = pull(ct)
        for j in range(nr):
            g = grads[j]
            if j == 0 and extra is not None:
                g = g + extra
            refs[base + j][...] = g.astype(refs[base + j].dtype)

        @pl.when(pl.program_id(0) == 0)
        def _():
            for j in range(npar):
                refs[base + nr + j][...] = jnp.zeros_like(refs[base + nr + j])

        for j in range(npar):
            refs[base + nr + j][...] += grads[nr + j]

    in_specs = specs + [_full_spec(p) for p in params] + [s for _, s in cp]
    args = arrays + list(params) + [a for a, _ in cp]
    if add is not None:
        in_specs.append(pl.BlockSpec((tm, widths[0]), lambda i: (i, 0)))
        args.append(add)
    out_specs = [pl.BlockSpec((tm, w), lambda i: (i, 0)) for w in widths] + [_full_spec(p) for p in params]
    out_shape = [jax.ShapeDtypeStruct((m, w), dt) for w, dt in zip(widths, row_dtypes)]
    out_shape += [jax.ShapeDtypeStruct(p.shape, F32) for p in params]
    return pl.pallas_call(
        body, name=name, grid=(m // tm,), in_specs=in_specs, out_specs=out_specs, out_shape=out_shape,
        compiler_params=_params(("arbitrary",)))(*args)


def loss_head(h, target, w, *, name, tm=256, matmul=None):
    m, d = h.shape
    tm = min(tm, m)

    def body(h_ref, t_ref, w_ref, *refs):
        loss_ref, dh_ref, dw_ref = refs[-3:]
        rows = h_ref[...]
        if matmul is not None:
            rows = rows + _bd(refs[0][...], refs[1][...], 1, 0)
        y, pull = jax.vjp(_rms, rows, w_ref[...])
        err = y - t_ref[...]
        dh, dw = pull(err * (1.0 / d))

        @pl.when(pl.program_id(0) == 0)
        def _():
            loss_ref[...] = jnp.zeros_like(loss_ref)
            dw_ref[...] = jnp.zeros_like(dw_ref)

        loss_ref[...] += (0.5 / d) * jnp.sum(err * err, keepdims=True)
        dw_ref[...] += dw
        dh_ref[...] = dh

    row = pl.BlockSpec((tm, d), lambda i: (i, 0))
    in_specs, args = [row, row, _full_spec(w)], [h, target, w]
    if matmul is not None:
        in_specs += [pl.BlockSpec((tm, matmul[0].shape[1]), lambda i: (i, 0)), _full_spec(matmul[1])]
        args += list(matmul)
    return pl.pallas_call(
        body, name=name, grid=(m // tm,), in_specs=in_specs,
        out_specs=[pl.BlockSpec((1, 1), lambda i: (0, 0)), row, _full_spec(w)],
        out_shape=[jax.ShapeDtypeStruct((1, 1), F32), jax.ShapeDtypeStruct((m, d), F32),
                   jax.ShapeDtypeStruct(w.shape, F32)],
        compiler_params=_params(("arbitrary",)))(*args)


def _shift(x, s):
    if s == 0:
        return x
    n = x.shape[0]
    t = lax.broadcasted_iota(jnp.int32, x.shape, 0)
    rolled = pltpu.roll(x, (-s) % n, 0)
    return jnp.where((t + s >= 0) & (t + s < n), rolled, 0.0)


def _conv(x, w, b):
    k = w.shape[0]
    acc = b + w[k // 2:k // 2 + 1, :] * x
    for j in range(k):
        if j != k // 2:
            acc = acc + w[j:j + 1, :] * _shift(x, j - k // 2)
    return acc


def _conv_bwd(x, dc, w):
    k = w.shape[0]
    dx = None
    dws = []
    for j in range(k):
        s = j - k // 2
        term = w[j:j + 1, :] * _shift(dc, -s)
        dx = term if dx is None else dx + term
        dws.append(jnp.sum(dc * _shift(x, s), axis=0, keepdims=True))
    return dx, jnp.concatenate(dws, axis=0), jnp.sum(dc, axis=0, keepdims=True)


def _dsilu(c):
    s = jax.nn.sigmoid(c)
    return s * (1.0 + c * (1.0 - s))


def ssd_conv_fwd(xbc, w, b, *, bsz, name):
    t, c = xbc.shape
    seq, ct = t // bsz, 256

    def body(x_ref, w_ref, b_ref, o_ref):
        o_ref[...] = _silu(_conv(x_ref[...], w_ref[...], b_ref[...]))

    return pl.pallas_call(
        body, name=name, grid=(c // ct, bsz),
        in_specs=[pl.BlockSpec((seq, ct), lambda j, i: (i, j)), pl.BlockSpec((w.shape[0], ct), lambda j, i: (0, j)),
                  pl.BlockSpec((1, ct), lambda j, i: (0, j))],
        out_specs=pl.BlockSpec((seq, ct), lambda j, i: (i, j)),
        out_shape=jax.ShapeDtypeStruct((t, c), F32),
        compiler_params=_params(("parallel", "parallel")))(xbc, w, b)


def ssd_conv_bwd(xbc, dparts, w, b, *, bsz, name):
    t, c = xbc.shape
    seq, ct, k = t // bsz, 256, w.shape[0]
    starts = [0]
    for p in dparts:
        starts.append(starts[-1] + p.shape[1] // ct)

    def body(x_ref, *refs):
        g_refs, (w_ref, b_ref, dx_ref, dw_ref, db_ref) = refs[:len(dparts)], refs[len(dparts):]
        j = pl.program_id(0)

        @pl.when(pl.program_id(1) == 0)
        def _():
            dw_ref[...] = jnp.zeros_like(dw_ref)
            db_ref[...] = jnp.zeros_like(db_ref)

        def run(g_ref):
            x, wv = x_ref[...], w_ref[...]
            dc = g_ref[...] * _dsilu(_conv(x, wv, b_ref[...]))
            dx, dw, db = _conv_bwd(x, dc, wv)
            dx_ref[...] = dx
            dw_ref[...] += dw
            db_ref[...] += db

        for n, g_ref in enumerate(g_refs):
            pl.when((j >= starts[n]) & (j < starts[n + 1]))(functools.partial(run, g_ref))

    def part_spec(n):
        lo, hi = starts[n], starts[n + 1]

        def index(j, i):
            inside = (j >= lo) & (j < hi)
            return jnp.where(inside, i, 0), jnp.where(inside, j - lo, 0)

        return pl.BlockSpec((seq, ct), index)

    blk = pl.BlockSpec((seq, ct), lambda j, i: (i, j))
    wspec, bspec = pl.BlockSpec((k, ct), lambda j, i: (0, j)), pl.BlockSpec((1, ct), lambda j, i: (0, j))
    return pl.pallas_call(
        body, name=name, grid=(c // ct, bsz),
        in_specs=[blk] + [part_spec(n) for n in range(len(dparts))] + [wspec, bspec], out_specs=[blk, wspec, bspec],
        out_shape=[jax.ShapeDtypeStruct((t, c), F32), jax.ShapeDtypeStruct((k, c), F32),
                   jax.ShapeDtypeStruct((1, c), F32)],
        compiler_params=_params(("parallel", "arbitrary")))(xbc, *dparts, w, b)


def _ffn_specs(seq, ct, k, nblk):
    val = pl.BlockSpec((seq, ct), lambda j, i: (i, j))
    gate = pl.BlockSpec((seq, ct), lambda j, i: (i, nblk + j))
    wv, wg = pl.BlockSpec((k, ct), lambda j, i: (0, j)), pl.BlockSpec((k, ct), lambda j, i: (0, nblk + j))
    bv, bg = pl.BlockSpec((1, ct), lambda j, i: (0, j)), pl.BlockSpec((1, ct), lambda j, i: (0, nblk + j))
    return val, gate, wv, wg, bv, bg


def ffn_act_fwd(up, w, b, *, bsz, name):
    t = up.shape[0]
    half = up.shape[1] // 2
    seq, ct, k = t // bsz, 256, w.shape[0]
    val, gate, wv, wg, bv, bg = _ffn_specs(seq, ct, k, half // ct)

    def body(v_ref, g_ref, wv_ref, wg_ref, bv_ref, bg_ref, o_ref):
        vc = _conv(v_ref[...].astype(F32), wv_ref[...], bv_ref[...])
        gc = _conv(g_ref[...].astype(F32), wg_ref[...], bg_ref[...])
        o_ref[...] = (_silu(gc) * vc).astype(BF16)

    return pl.pallas_call(
        body, name=name, grid=(half // ct, bsz), in_specs=[val, gate, wv, wg, bv, bg], out_specs=val,
        out_shape=jax.ShapeDtypeStruct((t, half), BF16),
        compiler_params=_params(("parallel", "parallel")))(up, up, w, w, b, b)


def ffn_act_bwd(up, dact, w, b, *, bsz, name):
    t = up.shape[0]
    half = up.shape[1] // 2
    seq, ct, k = t // bsz, 256, w.shape[0]
    val, gate, wv, wg, bv, bg = _ffn_specs(seq, ct, k, half // ct)

    def body(v_ref, g_ref, wv_ref, wg_ref, bv_ref, bg_ref, d_ref, dv_ref, dg_ref, dwv_ref, dwg_ref, dbv_ref, dbg_ref):
        v, g = v_ref[...].astype(F32), g_ref[...].astype(F32)
        vc = _conv(v, wv_ref[...], bv_ref[...])
        gc = _conv(g, wg_ref[...], bg_ref[...])
        d = d_ref[...].astype(F32)
        sg = jax.nn.sigmoid(gc)
        dv, dwv, dbv = _conv_bwd(v, d * (gc * sg), wv_ref[...])
        dg, dwg, dbg = _conv_bwd(g, d * vc * (sg * (1.0 + gc * (1.0 - sg))), wg_ref[...])
        dv_ref[...] = dv.astype(BF16)
        dg_ref[...] = dg.astype(BF16)

        @pl.when(pl.program_id(1) == 0)
        def _():
            for r in (dwv_ref, dwg_ref, dbv_ref, dbg_ref):
                r[...] = jnp.zeros_like(r)

        dwv_ref[...] += dwv
        dwg_ref[...] += dwg
        dbv_ref[...] += dbv
        dbg_ref[...] += dbg

    return pl.pallas_call(
        body, name=name, grid=(half // ct, bsz), in_specs=[val, gate, wv, wg, bv, bg, val],
        out_specs=[val, val, wv, wv, bv, bv],
        out_shape=[jax.ShapeDtypeStruct((t, half), BF16), jax.ShapeDtypeStruct((t, half), BF16),
                   jax.ShapeDtypeStruct((k, half), F32), jax.ShapeDtypeStruct((k, half), F32),
                   jax.ShapeDtypeStruct((1, half), F32), jax.ShapeDtypeStruct((1, half), F32)],
        compiler_params=_params(("parallel", "arbitrary")))(up, up, w, w, b, b, dact)


def _sel_row(a, h):
    oh = (lax.broadcasted_iota(jnp.int32, (a.shape[0], 1), 0) == h).astype(F32)
    return jnp.sum(a * oh, axis=0, keepdims=True)


def _ssd_chunk(xp, dtr, bm, cm, prev, bias_r, alog_r, dskip_r, rev):
    q = dtr.shape[1]
    ri = lax.broadcasted_iota(jnp.int32, (q, q), 0)
    ci = lax.broadcasted_iota(jnp.int32, (q, q), 1)
    mask = (ci >= ri) if rev else (ci <= ri)
    lane_lo, row_lo = ci < HDIM, ri < HDIM
    dt_r = _softplus(dtr + bias_r)
    dta_r = dt_r * (-jnp.exp(alog_r))
    cs_r = cum_row(dta_r, rev)
    scores = dot_nt(cm, bm)

    def per_row(v):
        return jnp.broadcast_to(v, (q, q)).T

    assert len(xp) == 2
    y_diag, csqs, decayed, tots = [], [], [], []
    for p in range(2):
        ha = 2 * p + (HPG if rev else 0)
        hb = ha + 1
        cs_a, cs_b = _sel_row(cs_r, ha), _sel_row(cs_r, hb)
        csq_a, csq_b = per_row(cs_a), per_row(cs_b)
        seg_a = jnp.exp(jnp.where(mask, csq_a - cs_a, -1e30))
        seg_b = jnp.exp(jnp.where(mask, csq_b - cs_b, -1e30))
        csq = jnp.where(lane_lo, csq_a, csq_b)
        xdt = xp[p] * jnp.where(lane_lo, per_row(_sel_row(dt_r, ha)), per_row(_sel_row(dt_r, hb)))
        tot_a = jnp.sum(_sel_row(dta_r, ha), axis=1, keepdims=True)
        tot_b = jnp.sum(_sel_row(dta_r, hb), axis=1, keepdims=True)
        y_diag.append(jnp.where(lane_lo, *dot2_nn(scores * seg_a, scores * seg_b, xdt)))
        csqs.append(csq)
        decayed.append(xdt * jnp.exp(jnp.where(lane_lo, tot_a, tot_b) - csq))
        tots.append((tot_a, tot_b, ha, hb))
    y_off = dot_nt2(cm, *prev)
    states = dot_tn2(*decayed, bm)
    ys, news = [], []
    for p, (tot_a, tot_b, ha, hb) in enumerate(tots):
        y = y_diag[p] + y_off[p] * jnp.exp(csqs[p])
        if not rev:
            y = y + jnp.where(lane_lo, _sel_row(dskip_r, ha), _sel_row(dskip_r, hb)) * xp[p]
        ys.append(y)
        news.append(jnp.exp(jnp.where(row_lo, tot_a, tot_b)) * prev[p] + states[p])
    return tuple(ys), tuple(news)


NPAIR = HPG // 2


def _ssd_specs(seq, nc):
    xs = pl.BlockSpec((None, seq, HPG * HDIM), lambda b, g: (b, 0, g))
    bm = pl.BlockSpec((None, seq, NSTATE), lambda b, g: (b, 0, SSD_W // NSTATE + g))
    cm = pl.BlockSpec((None, seq, NSTATE), lambda b, g: (b, 0, SSD_W // NSTATE + SGROUPS + g))
    dtr = pl.BlockSpec((None, None, 2 * HPG, seq), lambda b, g: (b, g, 0, 0))
    pr = pl.BlockSpec((None, 2 * HPG, 1), lambda b, g: (g, 0, 0))
    st = pl.BlockSpec((None, None, 2, nc, NPAIR, 2 * HDIM, NSTATE), lambda b, g: (b, g, 0, 0, 0, 0, 0))
    return xs, bm, cm, dtr, pr, st


def _pair_cols(p):
    return slice(2 * HDIM * p, 2 * HDIM * (p + 1))


def ssd_scan_fwd(act, dtr, prs, *, name):
    bsz, seq, _ = act.shape
    nc = seq // QC
    xs, bm, cm, dtrs, pr, st = _ssd_specs(seq, nc)

    def body(x_ref, b_ref, c_ref, dtr_ref, br_ref, ar_ref, dk_ref, y_ref, st_ref):
        par = (br_ref[...], ar_ref[...], dk_ref[...])
        y_ref[...] = jnp.zeros_like(y_ref)

        def step(i, carry):
            new = []
            for rev in (False, True):
                k = (nc - 1 - i) if rev else i
                rows = pl.ds(pl.multiple_of(k * QC, QC), QC)
                xp = tuple(x_ref[rows, _pair_cols(p)] for p in range(NPAIR))
                for p in range(NPAIR):
                    st_ref[int(rev), k, p] = carry[rev][p]
                ys, nw = _ssd_chunk(xp, dtr_ref[:, rows], b_ref[rows, :], c_ref[rows, :], carry[rev], *par, rev)
                for p in range(NPAIR):
                    y_ref[rows, _pair_cols(p)] += ys[p]
                new.append(nw)
            return tuple(new)

        zero = tuple(jnp.zeros((2 * HDIM, NSTATE), F32) for _ in range(NPAIR))
        lax.fori_loop(0, nc // 2, lambda i, c: step(2 * i + 1, step(2 * i, c)), (zero, zero))

    return pl.pallas_call(
        body, name=name, grid=(bsz, SGROUPS), in_specs=[xs, bm, cm, dtrs, pr, pr, pr], out_specs=[xs, st],
        out_shape=[jax.ShapeDtypeStruct((bsz, seq, SSD_W), F32),
                   jax.ShapeDtypeStruct((bsz, SGROUPS, 2, nc, NPAIR, 2 * HDIM, NSTATE), F32)],
        compiler_params=_params(("parallel", "parallel")))(act, act, act, dtr, *prs)


def ssd_scan_bwd(act, dtr, prs, states, dy, *, name):
    bsz, seq, _ = act.shape
    nc = seq // QC
    xs, bm, cm, dtrs, pr, st = _ssd_specs(seq, nc)
    grp = pl.BlockSpec((None, seq, NSTATE), lambda b, g: (b, 0, g))
    dpr = pl.BlockSpec((None, None, 2 * HPG, 1), lambda b, g: (b, g, 0, 0))

    def body(x_ref, b_ref, c_ref, dtr_ref, br_ref, ar_ref, dk_ref, st_ref, dy_ref,
             dx_ref, db_ref, dc_ref, ddtr_ref, gbr_ref, gar_ref, gdk_ref):
        par = (br_ref[...], ar_ref[...], dk_ref[...])
        pgrads = (gbr_ref, gar_ref, gdk_ref)
        for r in pgrads + (dx_ref, db_ref, dc_ref, ddtr_ref):
            r[...] = jnp.zeros_like(r)

        def bstep(i, dcarry):
            new = []
            for rev in (False, True):
                k = i if rev else (nc - 1 - i)
                rows = pl.ds(pl.multiple_of(k * QC, QC), QC)
                xp = tuple(x_ref[rows, _pair_cols(p)] for p in range(NPAIR))
                prev = tuple(st_ref[int(rev), k, p] for p in range(NPAIR))
                _, pull = jax.vjp(functools.partial(_ssd_chunk, rev=rev), xp, dtr_ref[:, rows], b_ref[rows, :],
                                  c_ref[rows, :], prev, *par)
                dyp = tuple(dy_ref[rows, _pair_cols(p)] for p in range(NPAIR))
                gx, gdt, gb, gc, gprev, *gpar = pull((dyp, dcarry[rev]))
                for p in range(NPAIR):
                    dx_ref[rows, _pair_cols(p)] += gx[p]
                ddtr_ref[:, rows] += gdt
                db_ref[rows, :] += gb
                dc_ref[rows, :] += gc
                for r, g in zip(pgrads, gpar):
                    r[...] += g
                new.append(gprev)
            return tuple(new)

        zero = tuple(jnp.zeros((2 * HDIM, NSTATE), F32) for _ in range(NPAIR))
        lax.fori_loop(0, nc, bstep, (zero, zero))

    out_shape = [jax.ShapeDtypeStruct((bsz, seq, SSD_W), F32),
                 jax.ShapeDtypeStruct((bsz, seq, SGROUPS * NSTATE), F32),
                 jax.ShapeDtypeStruct((bsz, seq, SGROUPS * NSTATE), F32),
                 jax.ShapeDtypeStruct(dtr.shape, F32)]
    out_shape += [jax.ShapeDtypeStruct((bsz, SGROUPS, 2 * HPG, 1), F32)] * 3
    return pl.pallas_call(
        body, name=name, grid=(bsz, SGROUPS), in_specs=[xs, bm, cm, dtrs, pr, pr, pr, st, xs],
        out_specs=[xs, grp, grp, dtrs, dpr, dpr, dpr], out_shape=out_shape,
        compiler_params=_params(("parallel", "parallel")))(act, act, act, dtr, *prs, states, dy)


def _s5_core(lam_re, lam_im, log_step, b_re, b_im, c_re, c_im):
    q = S5_Q
    step = jnp.exp(log_step)[:, None]
    lr, li = lam_re * step, lam_im * step
    mag = jnp.exp(lr)
    ar, ai = mag * jnp.cos(li), mag * jnp.sin(li)
    den = lam_re * lam_re + lam_im * lam_im
    cr = ((ar - 1.0) * lam_re + ai * lam_im) / den
    ci = (ai * lam_re - (ar - 1.0) * lam_im) / den
    bbr = cr[..., None] * b_re - ci[..., None] * b_im
    bbi = cr[..., None] * b_im + ci[..., None] * b_re
    d = jnp.arange(q + 1, dtype=F32)[None, :, None]
    pm = jnp.exp(d * lr[:, None, :])
    pr, pi = pm * jnp.cos(d * li[:, None, :]), pm * jnp.sin(d * li[:, None, :])
    er = pr[..., None] * bbr[:, None] - pi[..., None] * bbi[:, None]
    ei = pr[..., None] * bbi[:, None] + pi[..., None] * bbr[:, None]
    hp = lax.Precision.HIGHEST
    k = (jnp.einsum('gcp,gdpz->gdcz', c_re, er[:, :q], precision=hp)
         - jnp.einsum('gcp,gdpz->gdcz', c_im, ei[:, :q], precision=hp))
    e = jnp.concatenate([er[:, :q], ei[:, :q]], axis=2)
    p1r, p1i = pr[:, 1:], pi[:, 1:]
    m_re = c_re[:, None] * p1r[:, :, None, :] - c_im[:, None] * p1i[:, :, None, :]
    m_im = -c_re[:, None] * p1i[:, :, None, :] - c_im[:, None] * p1r[:, :, None, :]
    da = jnp.concatenate([pr[:, q], pr[:, q]], axis=-1)
    db = jnp.concatenate([-pi[:, q], pi[:, q]], axis=-1)
    return k, e, jnp.concatenate([m_re, m_im], axis=-1), da, db


def _s5_operators(lf_re, lf_im, lsf, lb_re, lb_im, lsb, b_re, b_im, cf_re, cf_im, cb_re, cb_im):
    g = lf_re.shape[0]
    both = lambda f, b: jnp.concatenate([f, b], axis=0)
    k, e, m, da, db = _s5_core(both(lf_re, lb_re), both(lf_im, lb_im), both(lsf, lsb), both(b_re, b_re),
                               both(b_im, b_im), both(cf_re, cb_re), both(cf_im, cb_im))
    kf, kb = k[:g], k[g:]
    wtf, wtb = jnp.transpose(e[:g, ::-1], (0, 1, 3, 2)), jnp.transpose(e[g:], (0, 1, 3, 2))
    mtf, mtb = jnp.transpose(m[:g], (0, 3, 1, 2)), jnp.transpose(m[g:, ::-1], (0, 3, 1, 2))
    daf, dab, dbf, dbb = da[:g], da[g:], db[:g], db[g:]
    lags = jnp.concatenate([kb[:, :0:-1], kf[:, :1] + kb[:, :1], kf[:, 1:]], axis=1)
    tt = jnp.transpose(lags, (0, 1, 3, 2))
    wt = jnp.concatenate([wtf.reshape(g, S5_QC, 2 * S5_P), wtb.reshape(g, S5_QC, 2 * S5_P)], axis=-1)
    mt = jnp.concatenate([mtf.reshape(g, 2 * S5_P, S5_QC), mtb.reshape(g, 2 * S5_P, S5_QC)], axis=1)
    return tt, wt, mt, jnp.concatenate([daf, dab], -1), jnp.concatenate([dbf, dbb], -1)


def _gspec(*shape):
    return pl.BlockSpec((None,) + shape, lambda g: (g,) + (0,) * len(shape))


S5_HALVES = S5_QC // LANES


def _toeplitz_block(s, t):
    per = LANES // S5_C
    return t // per, slice(s * S5_C, (s + 1) * S5_C), slice((t % per) * S5_C, (t % per + 1) * S5_C)


def s5_toeplitz(kt, *, name):
    g = kt.shape[0]

    def body(k_ref, t_ref):
        for s in range(S5_Q):
            for t in range(S5_Q):
                t_ref[_toeplitz_block(s, t)] = k_ref[t - s + S5_Q - 1]

    return pl.pallas_call(
        body, name=name, grid=(g,), in_specs=[_gspec(2 * S5_Q - 1, S5_C, S5_C)],
        out_specs=_gspec(S5_HALVES, S5_QC, LANES), out_shape=jax.ShapeDtypeStruct((g, S5_HALVES, S5_QC, LANES), F32),
        compiler_params=_params(("parallel",)))(kt)


def s5_toeplitz_bwd(dtt, *, name):
    g = dtt.shape[0]

    def body(d_ref, k_ref):
        for j in range(2 * S5_Q - 1):
            acc = None
            for s in range(S5_Q):
                t = j - (S5_Q - 1) + s
                if 0 <= t < S5_Q:
                    blk = d_ref[_toeplitz_block(s, t)]
                    acc = blk if acc is None else acc + blk
            k_ref[j] = acc

    return pl.pallas_call(
        body, name=name, grid=(g,), in_specs=[_gspec(S5_HALVES, S5_QC, LANES)],
        out_specs=_gspec(2 * S5_Q - 1, S5_C, S5_C), out_shape=jax.ShapeDtypeStruct((g, 2 * S5_Q - 1, S5_C, S5_C), F32),
        compiler_params=_params(("parallel",)))(dtt)


S5_RT = 128


def _chunk_piece(q):
    per = LANES // S5_C
    return q // per, slice((q % per) * S5_C, (q % per + 1) * S5_C)


def to_chunks(u, *, name):
    t = u.shape[0]
    r = t // S5_Q
    rt = min(S5_RT, r)

    per = LANES // S5_C
    nblk = S5_W // LANES

    def body(*refs):
        o_ref = refs[-1]
        for k in range(nblk):
            for q in range(S5_Q):
                rows = refs[k][pl.ds(q, rt, stride=S5_Q), :]
                half, lanes = _chunk_piece(q)
                for j in range(per):
                    o_ref[k * per + j, half, :, lanes] = rows[:, j * S5_C:(j + 1) * S5_C]

    return pl.pallas_call(
        body, name=name, grid=(r // rt,),
        in_specs=[pl.BlockSpec((rt * S5_Q, LANES), lambda i, k=k: (i, k)) for k in range(nblk)],
        out_specs=pl.BlockSpec((S5_G, S5_HALVES, rt, LANES), lambda i: (0, 0, i, 0)),
        out_shape=jax.ShapeDtypeStruct((S5_G, S5_HALVES, r, LANES), F32),
        compiler_params=_params(("parallel",)))(*[u] * nblk)


def from_chunks(y, *, name, add=None, as_blocks=False):
    r = y.shape[2]
    rt = min(S5_RT, r)
    per = LANES // S5_C

    nblk = S5_W // LANES

    def body(*refs):
        y_ref, tmp_ref = refs[0], refs[-1]
        adds, outs = refs[1:-1 - nblk], refs[-1 - nblk:-1]
        for k in range(nblk):
            for q in range(S5_Q):
                half, lanes = _chunk_piece(q)
                for j in range(per):
                    tmp_ref[:, j * S5_C:(j + 1) * S5_C] = y_ref[k * per + j, half, :, lanes]
                row = tmp_ref[...]
                if add is not None:
                    row = row + adds[k][pl.ds(q, rt, stride=S5_Q), :]
                outs[k][pl.ds(q, rt, stride=S5_Q), :] = row

    in_specs = [pl.BlockSpec((S5_G, S5_HALVES, rt, LANES), lambda i: (0, 0, i, 0))]
    if add is not None:
        in_specs += [pl.BlockSpec((rt * S5_Q, LANES), lambda i, k=k: (i, k)) for k in range(nblk)]
    blocks = pl.pallas_call(
        body, name=name, grid=(r // rt,), in_specs=in_specs,
        out_specs=[pl.BlockSpec((rt * S5_Q, LANES), lambda i: (i, 0))] * nblk,
        out_shape=[jax.ShapeDtypeStruct((r * S5_Q, LANES), F32)] * nblk,
        scratch_shapes=[pltpu.VMEM((rt, LANES), F32)],
        compiler_params=_params(("parallel",)))(*([y] if add is None else [y] + [add] * nblk))
    return list(blocks) if as_blocks else jnp.concatenate(blocks, axis=1)


def _cat(ref):
    return jnp.concatenate([ref[h] for h in range(S5_HALVES)], axis=1)


def _put(ref, v):
    for h in range(S5_HALVES):
        ref[h] = v[:, h * LANES:(h + 1) * LANES]


def _mspec(gp, *shape):
    return pl.BlockSpec((gp,) + shape, lambda i: (i,) + (0,) * len(shape))


def _carry_spec(nck):
    return pl.BlockSpec((nck, 8, 4 * S5_P), lambda i: (0, i, 0))


def _carry_rows(ref, gl, bsz):
    return jnp.concatenate([ref[:, gl * bsz + b, :] for b in range(bsz)], axis=0)


def _carry_put(ref, gl, bsz, v):
    nck = v.shape[0] // bsz
    for b in range(bsz):
        ref[:, gl * bsz + b, :] = v[b * nck:(b + 1) * nck, :]


def s5_state_in(u, wt, *, bsz, name):
    g, _, r, _ = u.shape
    gp, nck = 8 // bsz, r // bsz

    def body(u_ref, w_ref, o_ref):
        for gl in range(gp):
            _carry_put(o_ref, gl, bsz, _bd(_cat(u_ref.at[gl]), w_ref[gl], 1, 0))

    return pl.pallas_call(
        body, name=name, grid=(g // gp,), in_specs=[_mspec(gp, S5_HALVES, r, LANES), _mspec(gp, S5_QC, 4 * S5_P)],
        out_specs=_carry_spec(nck), out_shape=jax.ShapeDtypeStruct((nck, g * bsz, 4 * S5_P), F32),
        compiler_params=_params(("parallel",)))(u, wt)


def _swap(h):
    return pltpu.roll(h, S5_P, 1)


def s5_carry_fwd(s, da, db, *, name):
    nck, rows, _ = s.shape
    w = 2 * S5_P

    def body(s_ref, da_ref, db_ref, h_ref):
        dirs = ((False, slice(0, w)), (True, slice(w, 2 * w)))
        coef = [(da_ref[:, cols], db_ref[:, cols]) for _, cols in dirs]

        def step(i, hs):
            new = []
            for (rev, cols), (a, b), h in zip(dirs, coef, hs):
                k = (nck - 1 - i) if rev else i
                h_ref[k, :, cols] = h
                new.append(a * h + b * _swap(h) + s_ref[k, :, cols])
            return tuple(new)

        z = jnp.zeros((rows, w), F32)
        lax.fori_loop(0, nck, step, (z, z), unroll=2)

    rt = min(2 * CARRY_ROWS, rows)
    big, small = pl.BlockSpec((nck, rt, 2 * w), lambda i: (0, i, 0)), pl.BlockSpec((rt, 2 * w), lambda i: (i, 0))
    rows = rt
    return pl.pallas_call(
        body, name=name, grid=(s.shape[1] // rt,), in_specs=[big, small, small], out_specs=big,
        out_shape=jax.ShapeDtypeStruct(s.shape, F32), compiler_params=_params(("parallel",)))(s, da, db)


def s5_carry_bwd(hin, dh, da, db, *, name):
    nck, rows, _ = hin.shape
    w = 2 * S5_P

    def body(h_ref, dh_ref, da_ref, db_ref, ds_ref, gda_ref, gdb_ref):
        dirs = ((False, slice(0, w)), (True, slice(w, 2 * w)))
        coef = [(da_ref[:, cols], db_ref[:, cols]) for _, cols in dirs]

        def step(i, carries):
            new = []
            for (rev, cols), (a, b), (g, ga, gb) in zip(dirs, coef, carries):
                k = i if rev else (nck - 1 - i)
                ds_ref[k, :, cols] = g
                h = h_ref[k, :, cols]
                new.append((dh_ref[k, :, cols] + a * g + _swap(b * g), ga + g * h, gb + g * _swap(h)))
            return tuple(new)

        z = jnp.zeros((rows, w), F32)
        res = lax.fori_loop(0, nck, step, ((z, z, z), (z, z, z)), unroll=2)
        for (_, cols), (_, ga, gb) in zip(dirs, res):
            gda_ref[:, cols] = ga
            gdb_ref[:, cols] = gb

    rt = min(CARRY_ROWS, rows)
    big, small = pl.BlockSpec((nck, rt, 2 * w), lambda i: (0, i, 0)), pl.BlockSpec((rt, 2 * w), lambda i: (i, 0))
    rows = rt
    return pl.pallas_call(
        body, name=name, grid=(hin.shape[1] // rt,), in_specs=[big, big, small, small], out_specs=[big, small, small],
        out_shape=[jax.ShapeDtypeStruct(hin.shape, F32), jax.ShapeDtypeStruct(da.shape, F32),
                   jax.ShapeDtypeStruct(da.shape, F32)],
        compiler_params=_params(("parallel",)))(hin, dh, da, db)


def s5_out(u, hin, tt, mt, *, bsz, name):
    g, _, r, _ = u.shape
    gp, nck = 8 // bsz, r // bsz

    def body(u_ref, h_ref, t_ref, m_ref, o_ref):
        for gl in range(gp):
            u_v, h_v = _cat(u_ref.at[gl]), _carry_rows(h_ref, gl, bsz)
            for half in range(S5_HALVES):
                cols = slice(half * LANES, (half + 1) * LANES)
                o_ref[gl, half] = _bd(u_v, t_ref[gl, half], 1, 0) + _bd(h_v, m_ref[gl, :, cols], 1, 0)

    cspec = _mspec(gp, S5_HALVES, r, LANES)
    return pl.pallas_call(
        body, name=name, grid=(g // gp,),
        in_specs=[cspec, _carry_spec(nck), _mspec(gp, S5_HALVES, S5_QC, LANES), _mspec(gp, 4 * S5_P, S5_QC)],
        out_specs=cspec, out_shape=jax.ShapeDtypeStruct((g, S5_HALVES, r, LANES), F32),
        compiler_params=_params(("parallel",)))(u, hin, tt, mt)


def s5_out_bwd(dy, u, hin, tt, mt, *, bsz, name):
    g, _, r, _ = u.shape
    gp, nck = 8 // bsz, r // bsz

    def body(dy_ref, u_ref, h_ref, t_ref, m_ref, dh_ref, dt_ref, dm_ref, du_ref):
        for gl in range(gp):
            dy_v, u_v = _cat(dy_ref.at[gl]), _cat(u_ref.at[gl])
            _carry_put(dh_ref, gl, bsz, _bd(dy_v, m_ref[gl], 1, 1))
            dm_ref[gl] = _bd(_carry_rows(h_ref, gl, bsz), dy_v, 0, 0)
            du = None
            for half in range(S5_HALVES):
                dy_h = dy_ref[gl, half]
                dt_ref[gl, half] = _bd(u_v, dy_h, 0, 0)
                part = _bd(dy_h, t_ref[gl, half], 1, 1)
                du = part if du is None else du + part
            _put(du_ref.at[gl], du)

    cspec, tspec = _mspec(gp, S5_HALVES, r, LANES), _mspec(gp, S5_HALVES, S5_QC, LANES)
    mspec = _mspec(gp, 4 * S5_P, S5_QC)
    return pl.pallas_call(
        body, name=name, grid=(g // gp,),
        in_specs=[cspec, cspec, _carry_spec(nck), tspec, mspec],
        out_specs=[_carry_spec(nck), tspec, mspec, cspec],
        out_shape=[jax.ShapeDtypeStruct((nck, g * bsz, 4 * S5_P), F32),
                   jax.ShapeDtypeStruct((g, S5_HALVES, S5_QC, LANES), F32),
                   jax.ShapeDtypeStruct((g, 4 * S5_P, S5_QC), F32), jax.ShapeDtypeStruct((g, S5_HALVES, r, LANES), F32)],
        compiler_params=_params(("parallel",)))(dy, u, hin, tt, mt)


def s5_state_in_bwd(ds, u, wt, du1, *, bsz, name):
    g, _, r, _ = u.shape
    gp, nck = 8 // bsz, r // bsz

    def body(ds_ref, u_ref, w_ref, du1_ref, du_ref, dw_ref):
        for gl in range(gp):
            ds_v = _carry_rows(ds_ref, gl, bsz)
            _put(du_ref.at[gl], _cat(du1_ref.at[gl]) + _bd(ds_v, w_ref[gl], 1, 1))
            dw_ref[gl] = _bd(_cat(u_ref.at[gl]), ds_v, 0, 0)

    cspec, wspec = _mspec(gp, S5_HALVES, r, LANES), _mspec(gp, S5_QC, 4 * S5_P)
    return pl.pallas_call(
        body, name=name, grid=(g // gp,),
        in_specs=[_carry_spec(nck), cspec, wspec, cspec], out_specs=[cspec, wspec],
        out_shape=[jax.ShapeDtypeStruct((g, S5_HALVES, r, LANES), F32), jax.ShapeDtypeStruct((g, S5_QC, 4 * S5_P), F32)],
        compiler_params=_params(("parallel",)))(ds, u, wt, du1)


def _s5_post(ypre, u, dvec, wv, wg, bv, bg, nw):
    g = _gelu(ypre + dvec * u)
    out = (dot_nn(g, wv) + bv) * jax.nn.sigmoid(dot_nn(g, wg) + bg)
    return (_rms(out, nw),)


def _ssd_post(y, z, nw):
    return (_rms(y * _silu(z), nw),)


def _block_diag(w):
    eye = jnp.eye(S5_G, dtype=w.dtype)
    return jnp.einsum('gcd,gh->gchd', w, eye).reshape(S5_W, S5_W)


def _diag_blocks(w):
    v = w.reshape(S5_G, S5_C, S5_G, S5_C)
    return v[jnp.arange(S5_G), :, jnp.arange(S5_G), :]


def _dt_rows(dt, bsz):
    seq = dt.shape[0] // bsz
    return jnp.transpose(dt.reshape(bsz, seq, 2, SGROUPS, HPG), (0, 3, 2, 4, 1)).reshape(bsz, SGROUPS, 2 * HPG, seq)


def _dt_from_rows(dr):
    bsz, _, _, seq = dr.shape
    return jnp.transpose(dr.reshape(bsz, SGROUPS, 2, HPG, seq), (0, 4, 2, 1, 3)).reshape(bsz * seq, 2 * HEADS)


def _head_params(f, b):
    return jnp.concatenate([f.reshape(SGROUPS, HPG), b.reshape(SGROUPS, HPG)], axis=1)[:, :, None]


def _head_grads(gr):
    v = gr.sum(0)[:, :, 0]
    return v[:, :HPG].reshape(HEADS), v[:, HPG:].reshape(HEADS)


def local_step(x, target, w):
    bsz, seq, d = x.shape
    t = bsz * seq
    x2, tgt2 = x.reshape(t, d), target.reshape(t, d)
    g = {}
    row = lambda v: v.reshape(1, -1)
    bf = lambda v: v.astype(BF16)

    w_in = _unshard(bf(w['w_in']), SHARDED['w_in'])
    cuts = [0, SSD_W, SSD_W + XBC_W, SSD_W + XBC_W + 2 * HEADS, w_in.shape[1]]
    w_in_parts = [w_in[:, a:b] for a, b in zip(cuts[:-1], cuts[1:])]
    norm_mix = row(w['norm_mix_w']) + w.get('token', 0.0)
    (hn,) = rowmap_fwd(lambda a, nw: (_rms(a, nw),), [x2], [norm_mix], [(d, BF16)], tm=512, name="rms_mix")
    z, xbc, dt, u = matmul_multi(hn, w_in_parts, name="in_proj")

    conv_w, conv_b = _unshard(w['ssd_conv_w'], SHARDED['ssd_conv_w']), row(w['ssd_conv_b'])
    act = ssd_conv_fwd(xbc, conv_w, conv_b, bsz=bsz, name="ssd_conv")
    dtr = _dt_rows(dt, bsz)
    prs = (_head_params(w['ssd_dt_bias_fwd'], w['ssd_dt_bias_bwd']),
           _head_params(w['ssd_a_log_fwd'], w['ssd_a_log_bwd']),
           _head_params(w['ssd_d'], jnp.zeros_like(w['ssd_d'])))
    act3 = act.reshape(bsz, seq, XBC_W)
    y_scan, ssd_states = ssd_scan_fwd(act3, dtr, prs, name="ssd_scan")
    y_scan = y_scan.reshape(t, SSD_W)
    ssd_nw = row(w['ssd_norm_w'])
    (y_ssd,) = rowmap_fwd(_ssd_post, [y_scan, z], [ssd_nw], [(SSD_W, BF16)], tm=512, name="ssd_post")

    s5_names = ['s5_lambda_re_fwd', 's5_lambda_im_fwd', 's5_log_step_fwd', 's5_lambda_re_bwd', 's5_lambda_im_bwd',
                's5_log_step_bwd', 's5_b_re', 's5_b_im', 's5_c_re_fwd', 's5_c_im_fwd', 's5_c_re_bwd', 's5_c_im_bwd']
    (kt, wt, mt, da, db), s5_pull = jax.vjp(_s5_operators, *[w[n] for n in s5_names])
    tt_b, wt_b, mt_b = s5_toeplitz(kt, name="s5_toeplitz"), bf(wt), bf(mt)
    da_r, db_r = jnp.repeat(da, bsz, axis=0), jnp.repeat(db, bsz, axis=0)
    uc = to_chunks(u, name="s5_to_chunks_u")
    hin = s5_carry_fwd(s5_state_in(uc, wt_b, bsz=bsz, name="s5_state_in"), da_r, db_r, name="s5_carry")
    ypre = from_chunks(s5_out(uc, hin, tt_b, mt_b, bsz=bsz, name="s5_out"), name="s5_from_chunks_y", as_blocks=True)
    glu_w = w['s5_glu_w']
    s5_par = [row(w['s5_d']), _block_diag(glu_w[:, :, :S5_C]), _block_diag(glu_w[:, :, S5_C:]),
              row(w['s5_glu_b'][:, :S5_C]), row(w['s5_glu_b'][:, S5_C:]), row(w['s5_norm_w'])]
    (y_s5,) = rowmap_fwd(_s5_post, [ypre, u], s5_par, [(S5_W, BF16)], tm=512, name="s5_post")

    if 'late' in w:
        w = {**w, **w['late'](y_s5)}
    w_out = bf(w['w_out']).reshape(SSD_W + S5_W, d)
    norm_ffn = row(w['norm_ffn_w'])
    h1, hn2 = matmul_sum([y_ssd, y_s5], [w_out[:SSD_W], w_out[SSD_W:]], add=x2, norm_w=norm_ffn, name="out_proj")
    pad_c = FFN_PAD - FFN_BLK
    half = N_DEV // 2
    w_up3 = jnp.pad(bf(w['ffn_w_up']), ((0, 0), (0, 0), (0, pad_c)))
    w_down = jnp.pad(bf(w['ffn_w_down']).reshape(half, FFN_BLK, d), ((0, 0), (0, pad_c), (0, 0)))
    w_down = w_down.reshape(half * FFN_PAD, d)
    fconv_w = jnp.pad(w['ffn_conv_w'], ((0, 0), (0, 0), (0, pad_c)))
    fconv_w = jnp.transpose(fconv_w, (1, 0, 2)).reshape(FCONV, N_DEV * FFN_PAD)
    fconv_b = row(jnp.pad(w['ffn_conv_b'].reshape(N_DEV, FFN_BLK), ((0, 0), (0, pad_c))))
    up = matmul_cols(hn2, w_up3, out_dtype=BF16, name="ffn_up")
    fact = ffn_act_fwd(up, fconv_w, fconv_b, bsz=bsz, name="ffn_act")
    loss, dh2, g_nf = loss_head(h1, tgt2, row(w['norm_final_w']), matmul=(fact, w_down), tm=512, name="ffn_down_loss")
    g['norm_final_w'] = g_nf.reshape(-1)

    dfact = matmul_sum([dh2], [w_down], nt=True, tm=1024, name="ffn_down_dx")
    g_down = matmul_tn(fact, dh2, name="ffn_down_dw").reshape(half, FFN_PAD, d)[:, :FFN_BLK]
    g['ffn_w_down'] = g_down.reshape(N_DEV, FFN_BLK // 2, d)
    dval, dgate, dwv, dwg, dbv, dbg = ffn_act_bwd(up, dfact, fconv_w, fconv_b, bsz=bsz, name="ffn_act_bwd")
    g_cw = jnp.concatenate([dwv, dwg], axis=1).reshape(FCONV, N_DEV, FFN_PAD)[:, :, :FFN_BLK]
    g['ffn_conv_w'] = jnp.transpose(g_cw, (1, 0, 2))
    g['ffn_conv_b'] = jnp.concatenate([dbv, dbg], axis=1).reshape(N_DEV, FFN_PAD)[:, :FFN_BLK].reshape(-1)
    windows = [(dval, FFN_PAD, p) for p in range(half)] + [(dgate, FFN_PAD, p) for p in range(half)]
    g['ffn_w_up'] = jnp.concatenate([matmul_tn(hn2, dval, out_blocks=half, name="ffn_up_dw_val"),
                                     matmul_tn(hn2, dgate, out_blocks=half, name="ffn_up_dw_gate")],
                                    axis=0)[:, :, :FFN_BLK]
    send_early = w.get('on_grads')
    if send_early:
        norm_ffn = norm_ffn + send_early(g, ['ffn_w_up', 'ffn_w_down'])
    dh1, g_nffn = matmul_sum(windows, [(w_up3, p) for p in range(N_DEV)], nt=True, tm=256,
                             norm_bwd=(h1, norm_ffn, dh2), name="ffn_up_dx")
    g['norm_ffn_w'] = g_nffn.reshape(-1)

    dycat = matmul_sum([dh1], [w_out], nt=True, tm=1024, name="out_proj_dx")
    g['w_out'] = jnp.concatenate([matmul_tn(y_ssd, dh1, name="out_proj_dw_ssd"),
                                  matmul_tn(y_s5, dh1, name="out_proj_dw_s5")], axis=0).reshape(w['w_out'].shape)
    if send_early:
        ssd_nw = ssd_nw + send_early(g, ['w_out'])
    dy_scan, dz, g_snw = rowmap_bwd(_ssd_post, [y_scan, z], [ssd_nw], [(dycat, SSD_W, 0)], tm=512,
                                    name="ssd_post_bwd")
    g['ssd_norm_w'] = g_snw.reshape(-1)
    dypre, du_a, g_d, g_wv, g_wg, g_bv, g_bg, g_s5nw = rowmap_bwd(
        _s5_post, [ypre, u], s5_par, [(dycat, S5_W, SSD_W // S5_W)], tm=512, name="s5_post_bwd")
    g['s5_d'], g['s5_norm_w'] = g_d.reshape(-1), g_s5nw.reshape(-1)
    g['s5_glu_w'] = jnp.concatenate([_diag_blocks(g_wv), _diag_blocks(g_wg)], axis=-1)
    g['s5_glu_b'] = jnp.concatenate([g_bv.reshape(S5_G, S5_C), g_bg.reshape(S5_G, S5_C)], axis=-1)

    dyc = to_chunks(dypre, name="s5_to_chunks_dy")
    dhin, dtt, dmt, du1 = s5_out_bwd(dyc, uc, hin, tt_b, mt_b, bsz=bsz, name="s5_out_bwd")
    ds, gda, gdb = s5_carry_bwd(hin, dhin, da_r, db_r, name="s5_carry_bwd")
    duc, dwt = s5_state_in_bwd(ds, uc, wt_b, du1, bsz=bsz, name="s5_state_in_bwd")
    du = from_chunks(duc, add=du_a, name="s5_from_chunks_du")
    fold = lambda v: v.reshape(S5_G, bsz, -1).sum(1)
    dkt = s5_toeplitz_bwd(dtt, name="s5_toeplitz_bwd")
    for n, gv in zip(s5_names, s5_pull((dkt, dwt, dmt, fold(gda), fold(gdb)))):
        g[n] = gv

    dxs, dbm, dcm, ddtr, gbr, gar, gdk = ssd_scan_bwd(
        act3, dtr, prs, ssd_states, dy_scan.reshape(bsz, seq, SSD_W), name="ssd_scan_bwd")
    g['ssd_dt_bias_fwd'], g['ssd_dt_bias_bwd'] = _head_grads(gbr)
    g['ssd_a_log_fwd'], g['ssd_a_log_bwd'] = _head_grads(gar)
    g['ssd_d'] = _head_grads(gdk)[0]
    dparts_act = [v.reshape(t, v.shape[-1]) for v in (dxs, dbm, dcm)]
    dxbc, g_cw, g_cb = ssd_conv_bwd(xbc, dparts_act, conv_w, conv_b, bsz=bsz, name="ssd_conv_bwd")
    g['ssd_conv_w'] = _shard_rows(g_cw, SHARDED['ssd_conv_w']).reshape(w['ssd_conv_w'].shape)
    g['ssd_conv_b'] = g_cb.reshape(-1)
    ddt = _dt_from_rows(ddtr)

    if send_early:
        ddt = ddt + send_early(g, [], loss=loss)
    dparts = [dz, dxbc, ddt, du]
    g_in = jnp.concatenate([matmul_tn(hn, dp, name=f"in_proj_dw_{i}") for i, dp in enumerate(dparts)], axis=1)
    g['w_in'] = _shard_rows(g_in, SHARDED['w_in']).reshape(w['w_in'].shape)
    if send_early:
        dparts[2] = ddt + send_early(g, ['w_in'])
    dx, g_nmix = matmul_sum(dparts, w_in_parts, nt=True, tm=256, norm_bwd=(x2, norm_mix, dh1), name="in_proj_dx")
    g['norm_mix_w'] = g_nmix.reshape(-1)
    return loss, dx.reshape(bsz, seq, d), g


ANY = pl.BlockSpec(memory_space=pl.ANY)


def all_gather(shards, *, name):
    n = len(shards)

    def body(*refs):
        x_refs, out_refs = refs[:n], refs[n:2 * n]
        send_sems, recv_sems, local_sems = refs[2 * n:]
        x, y, c = lax.axis_index("x"), lax.axis_index("y"), lax.axis_index("c")
        me, sibling = (x, y, c), (x, y, 1 - c)
        chips = [(1 - x, y), (x, 1 - y), (1 - x, 1 - y)]

        def copy(k, j, block, to, own=False):
            dst = out_refs[j].at[4 * block[0] + 2 * block[1] + block[2]]
            return pltpu.make_async_remote_copy(
                src_ref=x_refs[j] if own else dst, dst_ref=dst,
                send_sem=send_sems.at[k, j], recv_sem=recv_sems.at[k, j], device_id=to, device_id_type=MESH)

        mine = [pltpu.make_async_copy(x_refs[j], out_refs[j].at[4 * x + 2 * y + c], local_sems.at[j]) for j in range(n)]
        first = [copy(0, j, me, sibling, own=True) for j in range(n)]
        first += [copy(1 + i, j, me, (*chip, c), own=True) for i, chip in enumerate(chips) for j in range(n)]
        for cp in mine + first:
            cp.start()
        passed = []
        for i, chip in enumerate(chips):
            for j in range(n):
                copy(1 + i, j, (*chip, c), me).wait_recv()
                passed.append(copy(4 + i, j, (*chip, c), sibling))
                passed[-1].start()
        for j in range(n):
            copy(0, j, sibling, me).wait_recv()
        for i, chip in enumerate(chips):
            for j in range(n):
                copy(4 + i, j, (*chip, 1 - c), me).wait_recv()
        for cp in first + passed:
            cp.wait_send()
        for cp in mine:
            cp.wait()

    return pl.pallas_call(
        body, name=name, out_shape=[jax.ShapeDtypeStruct((N_DEV,) + s.shape, s.dtype) for s in shards],
        in_specs=[ANY] * n, out_specs=[ANY] * n,
        scratch_shapes=[pltpu.SemaphoreType.DMA((7, n)), pltpu.SemaphoreType.DMA((7, n)),
                        pltpu.SemaphoreType.DMA((n,))],
    )(*shards)


HBM_SPEC = pl.BlockSpec(memory_space=pltpu.HBM)
SEM_SPEC = pl.BlockSpec(memory_space=pltpu.SEMAPHORE)
SPLIT_PARAMS = pltpu.CompilerParams(has_side_effects=pltpu.SideEffectType.DATAFLOW_SIDE_EFFECTING)


def _peer_copies(src_refs, land_refs, send_sems, recv_sems, indexed):
    x, y, c = lax.axis_index("x"), lax.axis_index("y"), lax.axis_index("c")
    me = 4 * x + 2 * y + c
    copies = []
    for k in range(1, N_DEV):
        px = (1 - x) if k & 4 else x
        py = (1 - y) if k & 2 else y
        pc = (1 - c) if k & 1 else c
        for j, (src, land) in enumerate(zip(src_refs, land_refs)):
            sem = (k - 1) * len(src_refs) + j
            copies.append(pltpu.make_async_remote_copy(
                src_ref=src.at[4 * px + 2 * py + pc] if indexed else src, dst_ref=land.at[me],
                send_sem=send_sems.at[sem], recv_sem=recv_sems.at[sem],
                device_id=(px, py, pc), device_id_type=MESH))
    return copies


def scatter_start(srcs, *, name, indexed):
    n = len(srcs)
    lands = [lax.empty(s.shape if indexed else (N_DEV,) + s.shape, s.dtype) for s in srcs]

    def body(*refs):
        send_sems, recv_sems = refs[2 * n], refs[2 * n + 1]
        for cp in _peer_copies(refs[:n], refs[n:2 * n], send_sems, recv_sems, indexed):
            cp.start()
        refs[-1][...] = jnp.zeros_like(refs[-1])

    hbm = lambda a: pltpu.HBM(a.shape, a.dtype)
    sems = pltpu.SemaphoreType.DMA(((N_DEV - 1) * n,))
    res = pl.pallas_call(
        body, name=name,
        out_shape=(sems, sems, *[hbm(a) for a in srcs + lands], jax.ShapeDtypeStruct((8, LANES), F32)),
        in_specs=[HBM_SPEC] * (2 * n),
        out_specs=(SEM_SPEC, SEM_SPEC, *[HBM_SPEC] * (2 * n), pl.BlockSpec(memory_space=pltpu.VMEM)),
        input_output_aliases={i: 2 + i for i in range(2 * n)}, compiler_params=SPLIT_PARAMS,
    )(*[pltpu.with_memory_space_constraint(a, pltpu.HBM) for a in srcs + lands])
    return res[0], res[1], list(res[2:2 + n]), list(res[2 + n:2 + 2 * n]), res[-1]


def scatter_wait(send_sems, recv_sems, srcs, lands, after, *, name, indexed):
    n = len(srcs)

    def body(*refs):
        for cp in _peer_copies(refs[:n], refs[n:2 * n], refs[2 * n], refs[2 * n + 1], indexed):
            cp.wait_send()
            cp.wait_recv()

    hbm = lambda a: pltpu.HBM(a.shape, a.dtype)
    res = pl.pallas_call(
        body, name=name, out_shape=tuple(hbm(a) for a in srcs + lands),
        in_specs=[HBM_SPEC] * (2 * n) + [SEM_SPEC, SEM_SPEC, ANY], out_specs=tuple([HBM_SPEC] * (2 * n)),
        input_output_aliases={i: i for i in range(2 * n)}, compiler_params=SPLIT_PARAMS,
    )(*srcs, *lands, send_sems, recv_sems, after)
    return list(res[:n]), list(res[n:])


def _adam_rows(r, c):
    fits = [t for t in range(8, r + 1, 8) if r % t == 0 and N_DEV * t * c * 4 <= 6 * 2 ** 20]
    return max(fits) if fits else r


def adamw(recv, w, m, v, *, name):
    _, r, n = recv.shape
    tr = _adam_rows(r, n)

    def body(r_ref, w_ref, m_ref, v_ref, g_ref, d_ref, nm_ref, nv_ref):
        g = r_ref[0].astype(F32)
        for s in range(1, N_DEV):
            g = g + r_ref[s].astype(F32)
        m_new = ADAM_B1 * m_ref[...] + (1.0 - ADAM_B1) * g
        v_new = ADAM_B2 * v_ref[...] + (1.0 - ADAM_B2) * jnp.square(g)
        m_hat = m_new / (1.0 - ADAM_B1 ** ADAM_STEP)
        v_hat = v_new / (1.0 - ADAM_B2 ** ADAM_STEP)
        g_ref[...] = g
        d_ref[...] = -ADAM_LR * (m_hat / (jnp.sqrt(v_hat) + ADAM_EPS) + ADAM_WD * w_ref[...])
        nm_ref[...] = m_new
        nv_ref[...] = v_new

    blk = pl.BlockSpec((tr, n), lambda i: (i, 0))
    return pl.pallas_call(
        body, name=name, grid=(r // tr,), in_specs=[pl.BlockSpec((N_DEV, tr, n), lambda i: (0, i, 0)), blk, blk, blk],
        out_specs=[blk] * 4, out_shape=[jax.ShapeDtypeStruct((r, n), F32)] * 4,
        compiler_params=_params(("parallel",)))(recv, w, m, v)


def _shard_rows(full, axis):
    if axis == 0:
        return full.reshape(N_DEV, -1)
    r, c = full.shape
    return jnp.transpose(full.reshape(r, N_DEV, c // N_DEV), (1, 0, 2)).reshape(N_DEV, -1)


def _unshard(blocks, axis):
    if axis == 0:
        return blocks.reshape(-1, blocks.shape[-1])
    return jnp.transpose(blocks, (1, 0, 2)).reshape(blocks.shape[1], -1)


def kernel(x, norm_mix_w, w_in, ssd_conv_w, ssd_conv_b, ssd_dt_bias_fwd, ssd_dt_bias_bwd, ssd_a_log_fwd, ssd_a_log_bwd, ssd_d, ssd_norm_w, s5_lambda_re_fwd, s5_lambda_im_fwd, s5_log_step_fwd, s5_lambda_re_bwd, s5_lambda_im_bwd, s5_log_step_bwd, s5_b_re, s5_b_im, s5_c_re_fwd, s5_c_im_fwd, s5_c_re_bwd, s5_c_im_bwd, s5_d, s5_glu_w, s5_glu_b, s5_norm_w, w_out, norm_ffn_w, ffn_w_up, ffn_conv_w, ffn_conv_b, ffn_w_down, norm_final_w, loss_target, m_norm_mix_w, m_w_in, m_ssd_conv_w, m_ssd_conv_b, m_ssd_dt_bias_fwd, m_ssd_dt_bias_bwd, m_ssd_a_log_fwd, m_ssd_a_log_bwd, m_ssd_d, m_ssd_norm_w, m_s5_lambda_re_fwd, m_s5_lambda_im_fwd, m_s5_log_step_fwd, m_s5_lambda_re_bwd, m_s5_lambda_im_bwd, m_s5_log_step_bwd, m_s5_b_re, m_s5_b_im, m_s5_c_re_fwd, m_s5_c_im_fwd, m_s5_c_re_bwd, m_s5_c_im_bwd, m_s5_d, m_s5_glu_w, m_s5_glu_b, m_s5_norm_w, m_w_out, m_norm_ffn_w, m_ffn_w_up, m_ffn_conv_w, m_ffn_conv_b, m_ffn_w_down, m_norm_final_w, v_norm_mix_w, v_w_in, v_ssd_conv_w, v_ssd_conv_b, v_ssd_dt_bias_fwd, v_ssd_dt_bias_bwd, v_ssd_a_log_fwd, v_ssd_a_log_bwd, v_ssd_d, v_ssd_norm_w, v_s5_lambda_re_fwd, v_s5_lambda_im_fwd, v_s5_log_step_fwd, v_s5_lambda_re_bwd, v_s5_lambda_im_bwd, v_s5_log_step_bwd, v_s5_b_re, v_s5_b_im, v_s5_c_re_fwd, v_s5_c_im_fwd, v_s5_c_re_bwd, v_s5_c_im_bwd, v_s5_d, v_s5_glu_w, v_s5_glu_b, v_s5_norm_w, v_w_out, v_norm_ffn_w, v_ffn_w_up, v_ffn_conv_w, v_ffn_conv_b, v_ffn_w_down, v_norm_final_w):
    args = dict(locals())
    strip = lambda n, v: v if n == 'norm_final_w' else v[0]
    w = {n: strip(n, args[n]) for n in WEIGHTS}

    mats = ['w_in', 'w_out', 'ffn_w_up', 'ffn_w_down']
    convs = ['ssd_conv_w', 'ffn_conv_w']
    shard = lambda n: w[n].astype(BF16) if n in mats else w[n]
    early, late = ['w_in', 'ssd_conv_w'], ['w_out', 'ffn_w_up', 'ffn_w_down', 'ffn_conv_w']
    full = dict(w)
    full.update(zip(early, all_gather([shard(n) for n in early], name="weight_all_gather")))
    ssem, rsem, src_thru, land_thru, token = scatter_start([shard(n) for n in late], name="weight_gather_start",
                                                           indexed=False)
    me = 4 * lax.axis_index("x") + 2 * lax.axis_index("y") + lax.axis_index("c")

    def late_weights(after):
        own, landed = scatter_wait(ssem, rsem, src_thru, land_thru, after, name="weight_gather_wait", indexed=False)
        return {n: lax.dynamic_update_index_in_dim(l, o, me, 0) for n, o, l in zip(late, own, landed)}

    full['late'], full['token'] = late_weights, token[:1, :1]

    pending = []
    last = 'norm_mix_w'
    small = convs + [n for n in WEIGHTS if n not in SHARDED and n != last]
    slot = {n: -(-w[n].size // (8 * LANES)) * 8 for n in small}
    used = sum(slot.values()) + 8
    nrow = -(-used // PACK_ROWS) * PACK_ROWS

    def tiles(v, n):
        return jnp.pad(v, ((0, 0), (0, slot[n] * LANES - v.shape[1]))).reshape(v.shape[0], slot[n], LANES)

    def send_early(grads, names, loss=None):
        srcs = [grads[n].astype(BF16) for n in names]
        if loss is not None:
            pieces = [tiles(grads[n].reshape(N_DEV, -1), n) if n in SHARDED else
                      jnp.broadcast_to(tiles(grads[n].reshape(1, -1), n), (N_DEV, slot[n], LANES)) for n in small]
            pieces.append(jnp.broadcast_to(jnp.pad(loss.reshape(1, 1, 1), ((0, 0), (0, 7), (0, LANES - 1))),
                                           (N_DEV, 8, LANES)))
            pieces.append(jnp.zeros((N_DEV, nrow - used, LANES), F32))
            srcs.append(jnp.concatenate(pieces, axis=1))
            names = names + ['small']
        started = scatter_start(srcs, name="grad_start_" + names[0], indexed=True)
        pending.append((names,) + started[:4])
        return started[4][:1, :1]

    full['on_grads'] = send_early
    loss, grad_x, g = local_step(x, loss_target, full)

    last_send = jnp.broadcast_to(g[last].reshape(1, -1, LANES), (N_DEV, g[last].size // LANES, LANES))
    last_started = scatter_start([last_send], name="grad_start_" + last, indexed=True)
    recv, outs = {}, [{}, {}, {}, {}]

    def arrived(names, started, after):
        own, landed = scatter_wait(*started, after, name="grad_wait_" + names[0], indexed=True)
        for n, o, l in zip(names, own, landed):
            recv[n] = lax.dynamic_update_index_in_dim(l, lax.dynamic_index_in_dim(o, me, 0, keepdims=False), me, 0)

    def update(n):
        shape = recv[n].shape[1:]
        res = adamw(recv[n], *[strip(n, args[p + n]).reshape(shape) for p in ('', 'm_', 'v_')], name="adamw_" + n)
        for o, p in zip(outs, res):
            o[n] = p.reshape(args[n].shape)

    for names, *started in pending:
        arrived(names, started, last_started[4])
    for n in mats:
        update(n)

    def pack(prefix):
        vals = [tiles(strip(n, args[prefix + n]).reshape(1, -1), n)[0] for n in small]
        return jnp.concatenate(vals + [jnp.zeros((nrow - used + 8, LANES), F32)], axis=0)

    packed = adamw(recv['small'], pack(''), pack('m_'), pack('v_'), name="adamw_small")
    arrived([last], last_started[:4], packed[1])
    update(last)
    off = 0
    for n in small:
        for o, p in zip(outs, packed):
            o[n] = p[off:off + slot[n]].reshape(-1)[:w[n].size].reshape(args[n].shape)
        off += slot[n]
    loss_out = packed[0][off, 0].reshape(())
    return (loss_out, grad_x, *[o[n] for o in outs for n in WEIGHTS])
```

```python
import functools

import jax
import jax.numpy as jnp
from jax import lax
from jax.experimental import pallas as pl
from jax.experimental.pallas import tpu as pltpu

F32, BF16 = jnp.float32, jnp.bfloat16
N_DEV = 8
D_MODEL = 1024
SSD_W, HEADS, HDIM, SGROUPS, HPG, NSTATE, SCONV, QC = 1024, 16, 64, 4, 4, 128, 5, 128
XBC_W = SSD_W + 2 * SGROUPS * NSTATE
S5_W, S5_G, S5_C, S5_P, S5_Q = 512, 32, 16, 64, 16
S5_QC = S5_Q * S5_C
CARRY_ROWS = 32
DFF, FCONV = 2816, 3
FFN_BLK, FFN_PAD = 704, 768
EPS = 1e-6
ADAM_LR, ADAM_B1, ADAM_B2, ADAM_EPS, ADAM_WD, ADAM_STEP = 0.001, 0.9, 0.999, 1e-08, 0.01, 10
LANES = 128
MESH = pl.DeviceIdType.MESH

WEIGHTS = ['norm_mix_w', 'w_in', 'ssd_conv_w', 'ssd_conv_b', 'ssd_dt_bias_fwd', 'ssd_dt_bias_bwd', 'ssd_a_log_fwd',
           'ssd_a_log_bwd', 'ssd_d', 'ssd_norm_w', 's5_lambda_re_fwd', 's5_lambda_im_fwd', 's5_log_step_fwd',
           's5_lambda_re_bwd', 's5_lambda_im_bwd', 's5_log_step_bwd', 's5_b_re', 's5_b_im', 's5_c_re_fwd', 's5_c_im_fwd',
           's5_c_re_bwd', 's5_c_im_bwd', 's5_d', 's5_glu_w', 's5_glu_b', 's5_norm_w', 'w_out', 'norm_ffn_w', 'ffn_w_up',
           'ffn_conv_w', 'ffn_conv_b', 'ffn_w_down', 'norm_final_w']
SHARDED = {'w_in': 1, 'ssd_conv_w': 1, 'w_out': 0, 'ffn_w_up': 1, 'ffn_conv_w': 1, 'ffn_w_down': 0}
FULL_SHAPE = {'w_in': (1024, 3616), 'ssd_conv_w': (5, 2048), 'w_out': (1536, 1024), 'ffn_w_up': (1024, 5632),
              'ffn_conv_w': (3, 5632), 'ffn_w_down': (2816, 1024)}
PACK_ROWS = 512


def _pick(n, cap=1536):
    if n <= cap:
        return n
    return max(t for t in range(LANES, cap + 1, LANES) if n % t == 0)


def _params(sem):
    return pltpu.CompilerParams(dimension_semantics=sem)


def _bd(a, b, ca, cb):
    return lax.dot_general(a.astype(BF16), b.astype(BF16), (((ca,), (cb,)), ((), ())), preferred_element_type=F32)


@jax.custom_vjp
def dot_nn(a, b):
    return _bd(a, b, 1, 0)


dot_nn.defvjp(lambda a, b: (_bd(a, b, 1, 0), (a, b)),
              lambda r, g: (_bd(g, r[1], 1, 1).astype(r[0].dtype), _bd(r[0], g, 0, 0).astype(r[1].dtype)))


@jax.custom_vjp
def dot_nt(a, b):
    return _bd(a, b, 1, 1)


dot_nt.defvjp(lambda a, b: (_bd(a, b, 1, 1), (a, b)),
              lambda r, g: (_bd(g, r[1], 1, 0).astype(r[0].dtype), _bd(g, r[0], 0, 0).astype(r[1].dtype)))


@jax.custom_vjp
def dot_tn(a, b):
    return _bd(a, b, 0, 0)


dot_tn.defvjp(lambda a, b: (_bd(a, b, 0, 0), (a, b)),
              lambda r, g: (_bd(r[1], g, 1, 1).astype(r[0].dtype), _bd(r[0], g, 1, 0).astype(r[1].dtype)))


def _rows2(v):
    h = v.shape[0] // 2
    return v[:h], v[h:]


def _cols2(v):
    h = v.shape[1] // 2
    return v[:, :h], v[:, h:]


@jax.custom_vjp
def dot2_nn(la, lb, x):
    return _rows2(_bd(jnp.concatenate([la, lb], axis=0), x, 1, 0))


def _dot2_nn_bwd(res, g):
    la, lb, x = res
    gcat, lcat = jnp.concatenate(g, axis=0), jnp.concatenate([la, lb], axis=0)
    return (*_rows2(_bd(gcat, x, 1, 1)), _bd(lcat, gcat, 0, 0))


dot2_nn.defvjp(lambda la, lb, x: (dot2_nn(la, lb, x), (la, lb, x)), _dot2_nn_bwd)


@jax.custom_vjp
def dot_nt2(c, p0, p1):
    return _cols2(_bd(c, jnp.concatenate([p0, p1], axis=0), 1, 1))


def _dot_nt2_bwd(res, g):
    c, p0, p1 = res
    gcat = jnp.concatenate(g, axis=1)
    return (_bd(gcat, jnp.concatenate([p0, p1], axis=0), 1, 0), *_rows2(_bd(gcat, c, 0, 0)))


dot_nt2.defvjp(lambda c, p0, p1: (dot_nt2(c, p0, p1), (c, p0, p1)), _dot_nt2_bwd)


@jax.custom_vjp
def dot_tn2(a0, a1, b):
    return _rows2(_bd(jnp.concatenate([a0, a1], axis=1), b, 0, 0))


def _dot_tn2_bwd(res, g):
    a0, a1, b = res
    gcat, acat = jnp.concatenate(g, axis=0), jnp.concatenate([a0, a1], axis=1)
    return (*_cols2(_bd(b, gcat, 1, 1)), _bd(acat, gcat, 1, 0))


dot_tn2.defvjp(lambda a0, a1, b: (dot_tn2(a0, a1, b), (a0, a1, b)), _dot_tn2_bwd)


def _split3(x):
    hi = x.astype(BF16)
    r = x - hi.astype(F32)
    mid = r.astype(BF16)
    lo = (r - mid.astype(F32)).astype(BF16)
    return hi, mid, lo


def _cum_matrix(q, upper):
    ri = lax.broadcasted_iota(jnp.int32, (q, q), 0)
    ci = lax.broadcasted_iota(jnp.int32, (q, q), 1)
    return jnp.where((ci >= ri) if upper else (ci <= ri), 1.0, 0.0).astype(BF16)


def _exact_right(x, mat):
    return sum(jnp.dot(p, mat, preferred_element_type=F32) for p in _split3(x))


@functools.partial(jax.custom_vjp, nondiff_argnums=(1,))
def cum_row(x, rev):
    return _exact_right(x, _cum_matrix(x.shape[1], not rev))


cum_row.defvjp(lambda x, rev: (cum_row(x, rev), None),
               lambda rev, _, g: (_exact_right(g, _cum_matrix(g.shape[1], rev)),))


def _softplus(x):
    return jnp.maximum(x, 0.0) + jnp.log(1.0 + jnp.exp(-jnp.abs(x)))


def _silu(x):
    return x * jax.nn.sigmoid(x)


def _gelu(x):
    return 0.5 * x * (1.0 + jnp.tanh(0.7978845608028654 * (x + 0.044715 * (x * x * x))))


def _rms(x, w):
    xf = x.astype(F32)
    return xf * lax.rsqrt(jnp.mean(xf * xf, axis=-1, keepdims=True) + EPS) * w


def matmul_sum(a_list, b_list, *, name, out_dtype=F32, add=None, tm=512, nt=False, norm_w=None, norm_bwd=None):
    a_arrs = [a[0] if isinstance(a, tuple) else a for a in a_list]
    b_arrs = [b[0] if isinstance(b, tuple) else b for b in b_list]
    m, n = a_arrs[0].shape[0], b_arrs[0].shape[-2 if nt else -1]
    tm, tn, k = min(tm, m), _pick(n), len(a_list)
    assert (norm_w is None and norm_bwd is None) or tn == n

    def body(*refs):
        acc = None
        for a_ref, b_ref in zip(refs[:k], refs[k:2 * k]):
            p = _bd(a_ref[...], b_ref[...], 1, 1 if nt else 0)
            acc = p if acc is None else acc + p
        if add is not None:
            acc = acc + refs[2 * k][...]
        if norm_bwd is not None:
            x_ref, w_ref, res_ref, dx_ref, dw_ref = refs[-5:]
            dx, dw = jax.vjp(_rms, x_ref[...], w_ref[...])[1](acc)
            dx_ref[...] = dx + res_ref[...]

            @pl.when(pl.program_id(0) == 0)
            def _():
                dw_ref[...] = jnp.zeros_like(dw_ref)

            dw_ref[...] += dw
        elif norm_w is not None:
            refs[-2][...] = acc.astype(out_dtype)
            refs[-1][...] = _rms(acc, refs[-3][...]).astype(BF16)
        else:
            refs[-1][...] = acc.astype(out_dtype)

    def a_spec(a):
        if isinstance(a, tuple):
            return pl.BlockSpec((tm, a[1]), lambda i, j, blk=a[2]: (i, blk))
        return pl.BlockSpec((tm, a.shape[1]), lambda i, j: (i, 0))

    def b_spec(b):
        arr, p = b if isinstance(b, tuple) else (b, None)
        kk = arr.shape[-1 if nt else -2]
        shape, idx = ((tn, kk), lambda j: (j, 0)) if nt else ((kk, tn), lambda j: (0, j))
        if p is None:
            return pl.BlockSpec(shape, lambda i, j: idx(j))
        return pl.BlockSpec((None,) + shape, lambda i, j, p=p: (p,) + idx(j))

    in_specs = [a_spec(a) for a in a_list] + [b_spec(b) for b in b_list]
    args = a_arrs + b_arrs
    if add is not None:
        in_specs.append(pl.BlockSpec((tm, tn), lambda i, j: (i, j)))
        args.append(add)
    out_spec, out_shape = pl.BlockSpec((tm, tn), lambda i, j: (i, j)), jax.ShapeDtypeStruct((m, n), out_dtype)
    if norm_w is not None:
        in_specs.append(pl.BlockSpec(norm_w.shape, lambda i, j: (0, 0)))
        args.append(norm_w)
        out_spec, out_shape = [out_spec, out_spec], [out_shape, jax.ShapeDtypeStruct((m, n), BF16)]
    sem = ("parallel", "parallel")
    if norm_bwd is not None:
        x, w, res = norm_bwd
        wspec = pl.BlockSpec(w.shape, lambda i, j: (0, 0))
        in_specs += [out_spec, wspec, out_spec]
        args += [x, w, res]
        out_spec, out_shape = [out_spec, wspec], [jax.ShapeDtypeStruct((m, n), F32), jax.ShapeDtypeStruct(w.shape, F32)]
        sem = ("arbitrary", "arbitrary")
    return pl.pallas_call(
        body, name=name, grid=(m // tm, n // tn), in_specs=in_specs, out_specs=out_spec, out_shape=out_shape,
        compiler_params=_params(sem))(*args)


def matmul_multi(a, b_list, *, name, tm=512):
    m, kk = a.shape
    tm, nb = min(tm, m), len(b_list)

    def body(a_ref, *refs):
        a_v = a_ref[...]
        for b_ref, o_ref in zip(refs[:nb], refs[nb:]):
            o_ref[...] = _bd(a_v, b_ref[...], 1, 0)

    return pl.pallas_call(
        body, name=name, grid=(m // tm,),
        in_specs=[pl.BlockSpec((tm, kk), lambda i: (i, 0))] + [_full_spec(b) for b in b_list],
        out_specs=[pl.BlockSpec((tm, b.shape[1]), lambda i: (i, 0)) for b in b_list],
        out_shape=[jax.ShapeDtypeStruct((m, b.shape[1]), F32) for b in b_list],
        compiler_params=_params(("parallel",)))(a, *b_list)


def matmul_cols(a, b3, *, name, out_dtype=F32, tm=2048):
    m, kk = a.shape
    p, _, nb = b3.shape
    tm, tn = min(tm, m), _pick(nb, 768)
    per = nb // tn

    def body(a_ref, b_ref, o_ref):
        o_ref[...] = _bd(a_ref[...], b_ref[...], 1, 0).astype(out_dtype)

    return pl.pallas_call(
        body, name=name, grid=(m // tm, p * per),
        in_specs=[pl.BlockSpec((tm, kk), lambda i, j: (i, 0)),
                  pl.BlockSpec((None, kk, tn), lambda i, j: (j // per, 0, j % per))],
        out_specs=pl.BlockSpec((tm, tn), lambda i, j: (i, j)),
        out_shape=jax.ShapeDtypeStruct((m, p * nb), out_dtype),
        compiler_params=_params(("parallel", "parallel")))(a, b3)


def matmul_tn(a, b, *, name, tm=1024, out_blocks=None):
    m, k = a.shape
    n = b.shape[1]
    nb = n // (out_blocks or 1)
    tm, tk, tn = min(tm, m), _pick(k), _pick(nb, 768 if out_blocks else 1536)
    per = nb // tn

    def body(a_ref, b_ref, o_ref):
        @pl.when(pl.program_id(2) == 0)
        def _():
            o_ref[...] = jnp.zeros_like(o_ref)

        o_ref[...] += _bd(a_ref[...], b_ref[...], 0, 0)

    if out_blocks:
        out_spec = pl.BlockSpec((None, tk, tn), lambda i, j, t: (j // per, i, j % per))
        out_shape = jax.ShapeDtypeStruct((out_blocks, k, nb), F32)
    else:
        out_spec = pl.BlockSpec((tk, tn), lambda i, j, t: (i, j))
        out_shape = jax.ShapeDtypeStruct((k, n), F32)
    return pl.pallas_call(
        body, name=name, grid=(k // tk, n // tn, m // tm),
        in_specs=[pl.BlockSpec((tm, tk), lambda i, j, t: (t, i)), pl.BlockSpec((tm, tn), lambda i, j, t: (t, j))],
        out_specs=out_spec, out_shape=out_shape,
        compiler_params=_params(("parallel", "parallel", "arbitrary")))(a, b)


def _row_spec(r, tm):
    if isinstance(r, tuple):
        arr, width, blk = r
        return arr, pl.BlockSpec((tm, width), lambda i, blk=blk: (i, blk))
    return r, pl.BlockSpec((tm, r.shape[1]), lambda i: (i, 0))


def _full_spec(p):
    return pl.BlockSpec(p.shape, lambda i: (0,) * p.ndim)


def _expand_rows(rows, tm):
    arrays, specs, counts, widths = [], [], [], []
    for r in rows:
        parts = [_row_spec(p, tm) for p in (r if isinstance(r, list) else [r])]
        arrays += [a for a, _ in parts]
        specs += [s for _, s in parts]
        counts.append(len(parts))
        widths.append(sum(s.block_shape[1] for _, s in parts))
    return arrays, specs, counts, widths


def _row_values(refs, counts):
    vals, k = [], 0
    for c in counts:
        parts = [refs[k + j][...] for j in range(c)]
        vals.append(parts[0] if c == 1 else jnp.concatenate(parts, axis=1))
        k += c
    return vals


def _rows_of(rows):
    first = rows[0][0] if isinstance(rows[0], list) else rows[0]
    return (first[0] if isinstance(first, tuple) else first).shape[0]


def rowmap_fwd(fn, rows, params, outs, *, name, tm=256):
    m = _rows_of(rows)
    tm = min(tm, m)
    arrays, specs, counts, _ = _expand_rows(rows, tm)
    nin, npar = len(arrays), len(params)

    def body(*refs):
        res = fn(*_row_values(refs[:nin], counts), *[r[...] for r in refs[nin:nin + npar]])
        for o_ref, v in zip(refs[nin + npar:], res):
            o_ref[...] = v.astype(o_ref.dtype)

    return pl.pallas_call(
        body, name=name, grid=(m // tm,), in_specs=specs + [_full_spec(p) for p in params],
        out_specs=[pl.BlockSpec((tm, c), lambda i: (i, 0)) for c, _ in outs],
        out_shape=[jax.ShapeDtypeStruct((m, c), dt) for c, dt in outs],
        compiler_params=_params(("parallel",)))(*arrays, *params)


def rowmap_bwd(fn, rows, params, cts, *, name, row_dtypes=None, add=None, tm=256):
    m = _rows_of(rows)
    tm = min(tm, m)
    arrays, specs, counts, widths = _expand_rows(rows, tm)
    cp = [_row_spec(c, tm) for c in cts]
    nin, nr, npar, nc = len(arrays), len(rows), len(params), len(cts)
    row_dtypes = row_dtypes or [F32] * nr

    def body(*refs):
        ins = _row_values(refs[:nin], counts) + [r[...] for r in refs[nin:nin + npar]]
        ins = [v.astype(F32) for v in ins]
        ct = tuple(r[...].astype(F32) for r in refs[nin + npar:nin + npar + nc])
        base = nin + npar + nc
        extra = None
        if add is not None:
            extra = refs[base][...]
            base += 1
        _, pull = jax.vjp(fn, *ins)
        grads = pull(ct)
        for j in range(nr):
            g = grads[j]
            if j == 0 and extra is not None:
                g = g + extra
            refs[base + j][...] = g.astype(refs[base + j].dtype)

        @pl.when(pl.program_id(0) == 0)
        def _():
            for j in range(npar):
                refs[base + nr + j][...] = jnp.zeros_like(refs[base + nr + j])

        for j in range(npar):
            refs[base + nr + j][...] += grads[nr + j]

    in_specs = specs + [_full_spec(p) for p in params] + [s for _, s in cp]
    args = arrays + list(params) + [a for a, _ in cp]
    if add is not None:
        in_specs.append(pl.BlockSpec((tm, widths[0]), lambda i: (i, 0)))
        args.append(add)
    out_specs = [pl.BlockSpec((tm, w), lambda i: (i, 0)) for w in widths] + [_full_spec(p) for p in params]
    out_shape = [jax.ShapeDtypeStruct((m, w), dt) for w, dt in zip(widths, row_dtypes)]
    out_shape += [jax.ShapeDtypeStruct(p.shape, F32) for p in params]
    return pl.pallas_call(
        body, name=name, grid=(m // tm,), in_specs=in_specs, out_specs=out_specs, out_shape=out_shape,
        compiler_params=_params(("arbitrary",)))(*args)


def loss_head(h, target, w, *, name, tm=256, matmul=None):
    m, d = h.shape
    tm = min(tm, m)

    def body(h_ref, t_ref, w_ref, *refs):
        loss_ref, dh_ref, dw_ref = refs[-3:]
        rows = h_ref[...]
        if matmul is not None:
            rows = rows + _bd(refs[0][...], refs[1][...], 1, 0)
        y, pull = jax.vjp(_rms, rows, w_ref[...])
        err = y - t_ref[...]
        dh, dw = pull(err * (1.0 / d))

        @pl.when(pl.program_id(0) == 0)
        def _():
            loss_ref[...] = jnp.zeros_like(loss_ref)
            dw_ref[...] = jnp.zeros_like(dw_ref)

        loss_ref[...] += (0.5 / d) * jnp.sum(err * err, keepdims=True)
        dw_ref[...] += dw
        dh_ref[...] = dh

    row = pl.BlockSpec((tm, d), lambda i: (i, 0))
    in_specs, args = [row, row, _full_spec(w)], [h, target, w]
    if matmul is not None:
        in_specs += [pl.BlockSpec((tm, matmul[0].shape[1]), lambda i: (i, 0)), _full_spec(matmul[1])]
        args += list(matmul)
    return pl.pallas_call(
        body, name=name, grid=(m // tm,), in_specs=in_specs,
        out_specs=[pl.BlockSpec((1, 1), lambda i: (0, 0)), row, _full_spec(w)],
        out_shape=[jax.ShapeDtypeStruct((1, 1), F32), jax.ShapeDtypeStruct((m, d), F32),
                   jax.ShapeDtypeStruct(w.shape, F32)],
        compiler_params=_params(("arbitrary",)))(*args)


HALO = 8


def _padded(x):
    return jnp.concatenate([x, jnp.zeros((HALO, x.shape[1]), x.dtype)], axis=0)


def _shift(x, s):
    return x if s == 0 else pltpu.roll(x, (-s) % x.shape[0], 0)


def _conv(x, w, b):
    k = w.shape[0]
    acc = b + w[k // 2:k // 2 + 1, :] * x
    for j in range(k):
        if j != k // 2:
            acc = acc + w[j:j + 1, :] * _shift(x, j - k // 2)
    return acc


def _conv_bwd(x, dc, w):
    k = w.shape[0]
    dx = None
    dws = []
    for j in range(k):
        s = j - k // 2
        term = w[j:j + 1, :] * _shift(dc, -s)
        dx = term if dx is None else dx + term
        dws.append(jnp.sum(dc * _shift(x, s), axis=0, keepdims=True))
    return dx, jnp.concatenate(dws, axis=0), jnp.sum(dc, axis=0, keepdims=True)


def _dsilu(c):
    s = jax.nn.sigmoid(c)
    return s * (1.0 + c * (1.0 - s))


def ssd_conv_fwd(xbc, w, b, *, bsz, name):
    t, c = xbc.shape
    seq, ct = t // bsz, 256

    def body(x_ref, w_ref, b_ref, o_ref):
        o_ref[...] = _silu(_conv(_padded(x_ref[...]), w_ref[...], b_ref[...]))[:seq]

    return pl.pallas_call(
        body, name=name, grid=(c // ct, bsz),
        in_specs=[pl.BlockSpec((seq, ct), lambda j, i: (i, j)), pl.BlockSpec((w.shape[0], ct), lambda j, i: (0, j)),
                  pl.BlockSpec((1, ct), lambda j, i: (0, j))],
        out_specs=pl.BlockSpec((seq, ct), lambda j, i: (i, j)),
        out_shape=jax.ShapeDtypeStruct((t, c), F32),
        compiler_params=_params(("parallel", "parallel")))(xbc, w, b)


def ssd_conv_bwd(xbc, dparts, w, b, *, bsz, name):
    t, c = xbc.shape
    seq, ct, k = t // bsz, 256, w.shape[0]
    starts = [0]
    for p in dparts:
        starts.append(starts[-1] + p.shape[1] // ct)

    def body(x_ref, *refs):
        g_refs, (w_ref, b_ref, dx_ref, dw_ref, db_ref) = refs[:len(dparts)], refs[len(dparts):]
        j = pl.program_id(0)

        @pl.when(pl.program_id(1) == 0)
        def _():
            dw_ref[...] = jnp.zeros_like(dw_ref)
            db_ref[...] = jnp.zeros_like(db_ref)

        def run(g_ref):
            x, wv = _padded(x_ref[...]), w_ref[...]
            dc = _padded(g_ref[...]) * _dsilu(_conv(x, wv, b_ref[...]))
            dx, dw, db = _conv_bwd(x, dc, wv)
            dx_ref[...] = dx[:seq]
            dw_ref[...] += dw
            db_ref[...] += db

        for n, g_ref in enumerate(g_refs):
            pl.when((j >= starts[n]) & (j < starts[n + 1]))(functools.partial(run, g_ref))

    def part_spec(n):
        lo, hi = starts[n], starts[n + 1]

        def index(j, i):
            inside = (j >= lo) & (j < hi)
            return jnp.where(inside, i, 0), jnp.where(inside, j - lo, 0)

        return pl.BlockSpec((seq, ct), index)

    blk = pl.BlockSpec((seq, ct), lambda j, i: (i, j))
    wspec, bspec = pl.BlockSpec((k, ct), lambda j, i: (0, j)), pl.BlockSpec((1, ct), lambda j, i: (0, j))
    return pl.pallas_call(
        body, name=name, grid=(c // ct, bsz),
        in_specs=[blk] + [part_spec(n) for n in range(len(dparts))] + [wspec, bspec], out_specs=[blk, wspec, bspec],
        out_shape=[jax.ShapeDtypeStruct((t, c), F32), jax.ShapeDtypeStruct((k, c), F32),
                   jax.ShapeDtypeStruct((1, c), F32)],
        compiler_params=_params(("parallel", "arbitrary")))(xbc, *dparts, w, b)


def _ffn_specs(seq, ct, k, nblk):
    val = pl.BlockSpec((seq, ct), lambda j, i: (i, j))
    gate = pl.BlockSpec((seq, ct), lambda j, i: (i, nblk + j))
    wv, wg = pl.BlockSpec((k, ct), lambda j, i: (0, j)), pl.BlockSpec((k, ct), lambda j, i: (0, nblk + j))
    bv, bg = pl.BlockSpec((1, ct), lambda j, i: (0, j)), pl.BlockSpec((1, ct), lambda j, i: (0, nblk + j))
    return val, gate, wv, wg, bv, bg


def ffn_act_fwd(up, w, b, *, bsz, name):
    t = up.shape[0]
    half = up.shape[1] // 2
    seq, ct, k = t // bsz, 256, w.shape[0]
    val, gate, wv, wg, bv, bg = _ffn_specs(seq, ct, k, half // ct)

    def body(v_ref, g_ref, wv_ref, wg_ref, bv_ref, bg_ref, o_ref):
        vc = _conv(_padded(v_ref[...].astype(F32)), wv_ref[...], bv_ref[...])
        gc = _conv(_padded(g_ref[...].astype(F32)), wg_ref[...], bg_ref[...])
        o_ref[...] = (_silu(gc) * vc)[:seq].astype(BF16)

    return pl.pallas_call(
        body, name=name, grid=(half // ct, bsz), in_specs=[val, gate, wv, wg, bv, bg], out_specs=val,
        out_shape=jax.ShapeDtypeStruct((t, half), BF16),
        compiler_params=_params(("parallel", "parallel")))(up, up, w, w, b, b)


def ffn_act_bwd(up, dact, w, b, *, bsz, name):
    t = up.shape[0]
    half = up.shape[1] // 2
    seq, ct, k = t // bsz, 256, w.shape[0]
    val, gate, wv, wg, bv, bg = _ffn_specs(seq, ct, k, half // ct)

    def body(v_ref, g_ref, wv_ref, wg_ref, bv_ref, bg_ref, d_ref, dv_ref, dg_ref, dwv_ref, dwg_ref, dbv_ref, dbg_ref):
        v, g = _padded(v_ref[...].astype(F32)), _padded(g_ref[...].astype(F32))
        vc = _conv(v, wv_ref[...], bv_ref[...])
        gc = _conv(g, wg_ref[...], bg_ref[...])
        d = _padded(d_ref[...].astype(F32))
        sg = jax.nn.sigmoid(gc)
        dv, dwv, dbv = _conv_bwd(v, d * (gc * sg), wv_ref[...])
        dg, dwg, dbg = _conv_bwd(g, d * vc * (sg * (1.0 + gc * (1.0 - sg))), wg_ref[...])
        dv_ref[...] = dv[:seq].astype(BF16)
        dg_ref[...] = dg[:seq].astype(BF16)

        @pl.when(pl.program_id(1) == 0)
        def _():
            for r in (dwv_ref, dwg_ref, dbv_ref, dbg_ref):
                r[...] = jnp.zeros_like(r)

        dwv_ref[...] += dwv
        dwg_ref[...] += dwg
        dbv_ref[...] += dbv
        dbg_ref[...] += dbg

    return pl.pallas_call(
        body, name=name, grid=(half // ct, bsz), in_specs=[val, gate, wv, wg, bv, bg, val],
        out_specs=[val, val, wv, wv, bv, bv],
        out_shape=[jax.ShapeDtypeStruct((t, half), BF16), jax.ShapeDtypeStruct((t, half), BF16),
                   jax.ShapeDtypeStruct((k, half), F32), jax.ShapeDtypeStruct((k, half), F32),
                   jax.ShapeDtypeStruct((1, half), F32), jax.ShapeDtypeStruct((1, half), F32)],
        compiler_params=_params(("parallel", "arbitrary")))(up, up, w, w, b, b, dact)


def _sel_row(a, h):
    oh = (lax.broadcasted_iota(jnp.int32, (a.shape[0], 1), 0) == h).astype(F32)
    return jnp.sum(a * oh, axis=0, keepdims=True)


def _ssd_chunk(xp, dtr, bm, cm, prev, bias_r, alog_r, dskip_r, rev):
    q = dtr.shape[1]
    ri = lax.broadcasted_iota(jnp.int32, (q, q), 0)
    ci = lax.broadcasted_iota(jnp.int32, (q, q), 1)
    mask = (ci >= ri) if rev else (ci <= ri)
    lane_lo, row_lo = ci < HDIM, ri < HDIM
    dt_r = _softplus(dtr + bias_r)
    dta_r = dt_r * (-jnp.exp(alog_r))
    cs_r = cum_row(dta_r, rev)
    scores = dot_nt(cm, bm)

    def per_row(v):
        return jnp.broadcast_to(v, (q, q)).T

    assert len(xp) == 2
    y_diag, csqs, decayed, tots = [], [], [], []
    for p in range(2):
        ha = 2 * p + (HPG if rev else 0)
        hb = ha + 1
        cs_a, cs_b = _sel_row(cs_r, ha), _sel_row(cs_r, hb)
        csq_a, csq_b = per_row(cs_a), per_row(cs_b)
        seg_a = jnp.exp(jnp.where(mask, csq_a - cs_a, -1e30))
        seg_b = jnp.exp(jnp.where(mask, csq_b - cs_b, -1e30))
        csq = jnp.where(lane_lo, csq_a, csq_b)
        xdt = xp[p] * jnp.where(lane_lo, per_row(_sel_row(dt_r, ha)), per_row(_sel_row(dt_r, hb)))
        tot_a = jnp.sum(_sel_row(dta_r, ha), axis=1, keepdims=True)
        tot_b = jnp.sum(_sel_row(dta_r, hb), axis=1, keepdims=True)
        y_diag.append(jnp.where(lane_lo, *dot2_nn(scores * seg_a, scores * seg_b, xdt)))
        csqs.append(csq)
        decayed.append(xdt * jnp.exp(jnp.where(lane_lo, tot_a, tot_b) - csq))
        tots.append((tot_a, tot_b, ha, hb))
    y_off = dot_nt2(cm, *prev)
    states = dot_tn2(*decayed, bm)
    ys, news = [], []
    for p, (tot_a, tot_b, ha, hb) in enumerate(tots):
        y = y_diag[p] + y_off[p] * jnp.exp(csqs[p])
        if not rev:
            y = y + jnp.where(lane_lo, _sel_row(dskip_r, ha), _sel_row(dskip_r, hb)) * xp[p]
        ys.append(y)
        news.append(jnp.exp(jnp.where(row_lo, tot_a, tot_b)) * prev[p] + states[p])
    return tuple(ys), tuple(news)


NPAIR = HPG // 2


def _ssd_specs(seq, nc):
    xs = pl.BlockSpec((None, seq, HPG * HDIM), lambda b, g: (b, 0, g))
    bm = pl.BlockSpec((None, seq, NSTATE), lambda b, g: (b, 0, SSD_W // NSTATE + g))
    cm = pl.BlockSpec((None, seq, NSTATE), lambda b, g: (b, 0, SSD_W // NSTATE + SGROUPS + g))
    dtr = pl.BlockSpec((None, None, 2 * HPG, seq), lambda b, g: (b, g, 0, 0))
    pr = pl.BlockSpec((None, 2 * HPG, 1), lambda b, g: (g, 0, 0))
    st = pl.BlockSpec((None, None, 2, nc, NPAIR, 2 * HDIM, NSTATE), lambda b, g: (b, g, 0, 0, 0, 0, 0))
    return xs, bm, cm, dtr, pr, st


def _pair_cols(p):
    return slice(2 * HDIM * p, 2 * HDIM * (p + 1))


def ssd_scan_fwd(act, dtr, prs, *, name):
    bsz, seq, _ = act.shape
    nc = seq // QC
    xs, bm, cm, dtrs, pr, st = _ssd_specs(seq, nc)

    def body(x_ref, b_ref, c_ref, dtr_ref, br_ref, ar_ref, dk_ref, y_ref, st_ref):
        par = (br_ref[...], ar_ref[...], dk_ref[...])
        y_ref[...] = jnp.zeros_like(y_ref)

        def step(i, carry):
            new = []
            for rev in (False, True):
                k = (nc - 1 - i) if rev else i
                rows = pl.ds(pl.multiple_of(k * QC, QC), QC)
                xp = tuple(x_ref[rows, _pair_cols(p)] for p in range(NPAIR))
                for p in range(NPAIR):
                    st_ref[int(rev), k, p] = carry[rev][p]
                ys, nw = _ssd_chunk(xp, dtr_ref[:, rows], b_ref[rows, :], c_ref[rows, :], carry[rev], *par, rev)
                for p in range(NPAIR):
                    y_ref[rows, _pair_cols(p)] += ys[p]
                new.append(nw)
            return tuple(new)

        zero = tuple(jnp.zeros((2 * HDIM, NSTATE), F32) for _ in range(NPAIR))
        lax.fori_loop(0, nc // 2, lambda i, c: step(2 * i + 1, step(2 * i, c)), (zero, zero))

    return pl.pallas_call(
        body, name=name, grid=(bsz, SGROUPS), in_specs=[xs, bm, cm, dtrs, pr, pr, pr], out_specs=[xs, st],
        out_shape=[jax.ShapeDtypeStruct((bsz, seq, SSD_W), F32),
                   jax.ShapeDtypeStruct((bsz, SGROUPS, 2, nc, NPAIR, 2 * HDIM, NSTATE), F32)],
        compiler_params=_params(("parallel", "parallel")))(act, act, act, dtr, *prs)


def ssd_scan_bwd(act, dtr, prs, states, dy, *, name):
    bsz, seq, _ = act.shape
    nc = seq // QC
    xs, bm, cm, dtrs, pr, st = _ssd_specs(seq, nc)
    grp = pl.BlockSpec((None, seq, NSTATE), lambda b, g: (b, 0, g))
    dpr = pl.BlockSpec((None, None, 2 * HPG, 1), lambda b, g: (b, g, 0, 0))

    def body(x_ref, b_ref, c_ref, dtr_ref, br_ref, ar_ref, dk_ref, st_ref, dy_ref,
             dx_ref, db_ref, dc_ref, ddtr_ref, gbr_ref, gar_ref, gdk_ref):
        par = (br_ref[...], ar_ref[...], dk_ref[...])
        pgrads = (gbr_ref, gar_ref, gdk_ref)
        for r in pgrads + (dx_ref, db_ref, dc_ref, ddtr_ref):
            r[...] = jnp.zeros_like(r)

        def bstep(i, dcarry):
            new = []
            for rev in (False, True):
                k = i if rev else (nc - 1 - i)
                rows = pl.ds(pl.multiple_of(k * QC, QC), QC)
                xp = tuple(x_ref[rows, _pair_cols(p)] for p in range(NPAIR))
                prev = tuple(st_ref[int(rev), k, p] for p in range(NPAIR))
                _, pull = jax.vjp(functools.partial(_ssd_chunk, rev=rev), xp, dtr_ref[:, rows], b_ref[rows, :],
                                  c_ref[rows, :], prev, *par)
                dyp = tuple(dy_ref[rows, _pair_cols(p)] for p in range(NPAIR))
                gx, gdt, gb, gc, gprev, *gpar = pull((dyp, dcarry[rev]))
                for p in range(NPAIR):
                    dx_ref[rows, _pair_cols(p)] += gx[p]
                ddtr_ref[:, rows] += gdt
                db_ref[rows, :] += gb
                dc_ref[rows, :] += gc
                for r, g in zip(pgrads, gpar):
                    r[...] += g
                new.append(gprev)
            return tuple(new)

        zero = tuple(jnp.zeros((2 * HDIM, NSTATE), F32) for _ in range(NPAIR))
        lax.fori_loop(0, nc, bstep, (zero, zero))

    out_shape = [jax.ShapeDtypeStruct((bsz, seq, SSD_W), F32),
                 jax.ShapeDtypeStruct((bsz, seq, SGROUPS * NSTATE), F32),
                 jax.ShapeDtypeStruct((bsz, seq, SGROUPS * NSTATE), F32),
                 jax.ShapeDtypeStruct(dtr.shape, F32)]
    out_shape += [jax.ShapeDtypeStruct((bsz, SGROUPS, 2 * HPG, 1), F32)] * 3
    return pl.pallas_call(
        body, name=name, grid=(bsz, SGROUPS), in_specs=[xs, bm, cm, dtrs, pr, pr, pr, st, xs],
        out_specs=[xs, grp, grp, dtrs, dpr, dpr, dpr], out_shape=out_shape,
        compiler_params=_params(("parallel", "parallel")))(act, act, act, dtr, *prs, states, dy)


def _s5_core(lam_re, lam_im, log_step, b_re, b_im, c_re, c_im):
    q = S5_Q
    step = jnp.exp(log_step)[:, None]
    lr, li = lam_re * step, lam_im * step
    mag = jnp.exp(lr)
    ar, ai = mag * jnp.cos(li), mag * jnp.sin(li)
    den = lam_re * lam_re + lam_im * lam_im
    cr = ((ar - 1.0) * lam_re + ai * lam_im) / den
    ci = (ai * lam_re - (ar - 1.0) * lam_im) / den
    bbr = cr[..., None] * b_re - ci[..., None] * b_im
    bbi = cr[..., None] * b_im + ci[..., None] * b_re
    d = jnp.arange(q + 1, dtype=F32)[None, :, None]
    pm = jnp.exp(d * lr[:, None, :])
    pr, pi = pm * jnp.cos(d * li[:, None, :]), pm * jnp.sin(d * li[:, None, :])
    er = pr[..., None] * bbr[:, None] - pi[..., None] * bbi[:, None]
    ei = pr[..., None] * bbi[:, None] + pi[..., None] * bbr[:, None]
    hp = lax.Precision.HIGHEST
    k = (jnp.einsum('gcp,gdpz->gdcz', c_re, er[:, :q], precision=hp)
         - jnp.einsum('gcp,gdpz->gdcz', c_im, ei[:, :q], precision=hp))
    e = jnp.concatenate([er[:, :q], ei[:, :q]], axis=2)
    p1r, p1i = pr[:, 1:], pi[:, 1:]
    m_re = c_re[:, None] * p1r[:, :, None, :] - c_im[:, None] * p1i[:, :, None, :]
    m_im = -c_re[:, None] * p1i[:, :, None, :] - c_im[:, None] * p1r[:, :, None, :]
    da = jnp.concatenate([pr[:, q], pr[:, q]], axis=-1)
    db = jnp.concatenate([-pi[:, q], pi[:, q]], axis=-1)
    return k, e, jnp.concatenate([m_re, m_im], axis=-1), da, db


def _s5_operators(lf_re, lf_im, lsf, lb_re, lb_im, lsb, b_re, b_im, cf_re, cf_im, cb_re, cb_im):
    g = lf_re.shape[0]
    both = lambda f, b: jnp.concatenate([f, b], axis=0)
    k, e, m, da, db = _s5_core(both(lf_re, lb_re), both(lf_im, lb_im), both(lsf, lsb), both(b_re, b_re),
                               both(b_im, b_im), both(cf_re, cb_re), both(cf_im, cb_im))
    kf, kb = k[:g], k[g:]
    wtf, wtb = jnp.transpose(e[:g, ::-1], (0, 1, 3, 2)), jnp.transpose(e[g:], (0, 1, 3, 2))
    mtf, mtb = jnp.transpose(m[:g], (0, 3, 1, 2)), jnp.transpose(m[g:, ::-1], (0, 3, 1, 2))
    daf, dab, dbf, dbb = da[:g], da[g:], db[:g], db[g:]
    lags = jnp.concatenate([kb[:, :0:-1], kf[:, :1] + kb[:, :1], kf[:, 1:]], axis=1)
    tt = jnp.transpose(lags, (0, 1, 3, 2))
    wt = jnp.concatenate([wtf.reshape(g, S5_QC, 2 * S5_P), wtb.reshape(g, S5_QC, 2 * S5_P)], axis=-1)
    mt = jnp.concatenate([mtf.reshape(g, 2 * S5_P, S5_QC), mtb.reshape(g, 2 * S5_P, S5_QC)], axis=1)
    return tt, wt, mt, jnp.concatenate([daf, dab], -1), jnp.concatenate([dbf, dbb], -1)


def _gspec(*shape):
    return pl.BlockSpec((None,) + shape, lambda g: (g,) + (0,) * len(shape))


S5_HALVES = S5_QC // LANES


def _toeplitz_block(s, t):
    per = LANES // S5_C
    return t // per, slice(s * S5_C, (s + 1) * S5_C), slice((t % per) * S5_C, (t % per + 1) * S5_C)


def s5_toeplitz(kt, *, name):
    g = kt.shape[0]

    def body(k_ref, t_ref):
        for s in range(S5_Q):
            for t in range(S5_Q):
                t_ref[_toeplitz_block(s, t)] = k_ref[t - s + S5_Q - 1]

    return pl.pallas_call(
        body, name=name, grid=(g,), in_specs=[_gspec(2 * S5_Q - 1, S5_C, S5_C)],
        out_specs=_gspec(S5_HALVES, S5_QC, LANES), out_shape=jax.ShapeDtypeStruct((g, S5_HALVES, S5_QC, LANES), F32),
        compiler_params=_params(("parallel",)))(kt)


def s5_toeplitz_bwd(dtt, *, name):
    g = dtt.shape[0]

    def body(d_ref, k_ref):
        for j in range(2 * S5_Q - 1):
            acc = None
            for s in range(S5_Q):
                t = j - (S5_Q - 1) + s
                if 0 <= t < S5_Q:
                    blk = d_ref[_toeplitz_block(s, t)]
                    acc = blk if acc is None else acc + blk
            k_ref[j] = acc

    return pl.pallas_call(
        body, name=name, grid=(g,), in_specs=[_gspec(S5_HALVES, S5_QC, LANES)],
        out_specs=_gspec(2 * S5_Q - 1, S5_C, S5_C), out_shape=jax.ShapeDtypeStruct((g, 2 * S5_Q - 1, S5_C, S5_C), F32),
        compiler_params=_params(("parallel",)))(dtt)


S5_RT = 128


def _chunk_piece(q):
    per = LANES // S5_C
    return q // per, slice((q % per) * S5_C, (q % per + 1) * S5_C)


def to_chunks(u, *, name):
    t = u.shape[0]
    r = t // S5_Q
    rt = min(S5_RT, r)

    per = LANES // S5_C
    nblk = S5_W // LANES

    def body(*refs):
        o_ref = refs[-1]
        for k in range(nblk):
            for q in range(S5_Q):
                rows = refs[k][pl.ds(q, rt, stride=S5_Q), :]
                half, lanes = _chunk_piece(q)
                for j in range(per):
                    o_ref[k * per + j, half, :, lanes] = rows[:, j * S5_C:(j + 1) * S5_C]

    return pl.pallas_call(
        body, name=name, grid=(r // rt,),
        in_specs=[pl.BlockSpec((rt * S5_Q, LANES), lambda i, k=k: (i, k)) for k in range(nblk)],
        out_specs=pl.BlockSpec((S5_G, S5_HALVES, rt, LANES), lambda i: (0, 0, i, 0)),
        out_shape=jax.ShapeDtypeStruct((S5_G, S5_HALVES, r, LANES), F32),
        compiler_params=_params(("parallel",)))(*[u] * nblk)


def from_chunks(y, *, name, add=None, as_blocks=False):
    r = y.shape[2]
    rt = min(S5_RT, r)
    per = LANES // S5_C

    nblk = S5_W // LANES

    def body(*refs):
        y_ref, tmp_ref = refs[0], refs[-1]
        adds, outs = refs[1:-1 - nblk], refs[-1 - nblk:-1]
        for k in range(nblk):
            for q in range(S5_Q):
                half, lanes = _chunk_piece(q)
                for j in range(per):
                    tmp_ref[:, j * S5_C:(j + 1) * S5_C] = y_ref[k * per + j, half, :, lanes]
                row = tmp_ref[...]
                if add is not None:
                    row = row + adds[k][pl.ds(q, rt, stride=S5_Q), :]
                outs[k][pl.ds(q, rt, stride=S5_Q), :] = row

    in_specs = [pl.BlockSpec((S5_G, S5_HALVES, rt, LANES), lambda i: (0, 0, i, 0))]
    if add is not None:
        in_specs += [pl.BlockSpec((rt * S5_Q, LANES), lambda i, k=k: (i, k)) for k in range(nblk)]
    blocks = pl.pallas_call(
        body, name=name, grid=(r // rt,), in_specs=in_specs,
        out_specs=[pl.BlockSpec((rt * S5_Q, LANES), lambda i: (i, 0))] * nblk,
        out_shape=[jax.ShapeDtypeStruct((r * S5_Q, LANES), F32)] * nblk,
        scratch_shapes=[pltpu.VMEM((rt, LANES), F32)],
        compiler_params=_params(("parallel",)))(*([y] if add is None else [y] + [add] * nblk))
    return list(blocks) if as_blocks else jnp.concatenate(blocks, axis=1)


def _cat(ref):
    return jnp.concatenate([ref[h] for h in range(S5_HALVES)], axis=1)


def _put(ref, v):
    for h in range(S5_HALVES):
        ref[h] = v[:, h * LANES:(h + 1) * LANES]


def _mspec(gp, *shape):
    return pl.BlockSpec((gp,) + shape, lambda i: (i,) + (0,) * len(shape))


def _carry_spec(nck):
    return pl.BlockSpec((nck, 8, 4 * S5_P), lambda i: (0, i, 0))


def _carry_rows(ref, gl, bsz):
    return jnp.concatenate([ref[:, gl * bsz + b, :] for b in range(bsz)], axis=0)


def _carry_put(ref, gl, bsz, v):
    nck = v.shape[0] // bsz
    for b in range(bsz):
        ref[:, gl * bsz + b, :] = v[b * nck:(b + 1) * nck, :]


def s5_state_in(u, wt, *, bsz, name):
    g, _, r, _ = u.shape
    gp, nck = 8 // bsz, r // bsz

    def body(u_ref, w_ref, o_ref):
        for gl in range(gp):
            _carry_put(o_ref, gl, bsz, _bd(_cat(u_ref.at[gl]), w_ref[gl], 1, 0))

    return pl.pallas_call(
        body, name=name, grid=(g // gp,), in_specs=[_mspec(gp, S5_HALVES, r, LANES), _mspec(gp, S5_QC, 4 * S5_P)],
        out_specs=_carry_spec(nck), out_shape=jax.ShapeDtypeStruct((nck, g * bsz, 4 * S5_P), F32),
        compiler_params=_params(("parallel",)))(u, wt)


def _swap(h):
    return pltpu.roll(h, S5_P, 1)


def s5_carry_fwd(s, da, db, *, name):
    nck, rows, _ = s.shape
    w = 2 * S5_P

    def body(s_ref, da_ref, db_ref, h_ref):
        dirs = ((False, slice(0, w)), (True, slice(w, 2 * w)))
        coef = [(da_ref[:, cols], db_ref[:, cols]) for _, cols in dirs]

        def step(i, hs):
            new = []
            for (rev, cols), (a, b), h in zip(dirs, coef, hs):
                k = (nck - 1 - i) if rev else i
                h_ref[k, :, cols] = h
                new.append(a * h + b * _swap(h) + s_ref[k, :, cols])
            return tuple(new)

        z = jnp.zeros((rows, w), F32)
        lax.fori_loop(0, nck, step, (z, z), unroll=2)

    rt = min(2 * CARRY_ROWS, rows)
    big, small = pl.BlockSpec((nck, rt, 2 * w), lambda i: (0, i, 0)), pl.BlockSpec((rt, 2 * w), lambda i: (i, 0))
    rows = rt
    return pl.pallas_call(
        body, name=name, grid=(s.shape[1] // rt,), in_specs=[big, small, small], out_specs=big,
        out_shape=jax.ShapeDtypeStruct(s.shape, F32), compiler_params=_params(("parallel",)))(s, da, db)


def s5_carry_bwd(hin, dh, da, db, *, name):
    nck, rows, _ = hin.shape
    w = 2 * S5_P

    def body(h_ref, dh_ref, da_ref, db_ref, ds_ref, gda_ref, gdb_ref):
        dirs = ((False, slice(0, w)), (True, slice(w, 2 * w)))
        coef = [(da_ref[:, cols], db_ref[:, cols]) for _, cols in dirs]

        def step(i, carries):
            new = []
            for (rev, cols), (a, b), (g, ga, gb) in zip(dirs, coef, carries):
                k = i if rev else (nck - 1 - i)
                ds_ref[k, :, cols] = g
                h = h_ref[k, :, cols]
                new.append((dh_ref[k, :, cols] + a * g + _swap(b * g), ga + g * h, gb + g * _swap(h)))
            return tuple(new)

        z = jnp.zeros((rows, w), F32)
        res = lax.fori_loop(0, nck, step, ((z, z, z), (z, z, z)), unroll=2)
        for (_, cols), (_, ga, gb) in zip(dirs, res):
            gda_ref[:, cols] = ga
            gdb_ref[:, cols] = gb

    rt = min(CARRY_ROWS, rows)
    big, small = pl.BlockSpec((nck, rt, 2 * w), lambda i: (0, i, 0)), pl.BlockSpec((rt, 2 * w), lambda i: (i, 0))
    rows = rt
    return pl.pallas_call(
        body, name=name, grid=(hin.shape[1] // rt,), in_specs=[big, big, small, small], out_specs=[big, small, small],
        out_shape=[jax.ShapeDtypeStruct(hin.shape, F32), jax.ShapeDtypeStruct(da.shape, F32),
                   jax.ShapeDtypeStruct(da.shape, F32)],
        compiler_params=_params(("parallel",)))(hin, dh, da, db)


def s5_out(u, hin, tt, mt, *, bsz, name):
    g, _, r, _ = u.shape
    gp, nck = 8 // bsz, r // bsz

    def body(u_ref, h_ref, t_ref, m_ref, o_ref):
        for gl in range(gp):
            u_v, h_v = _cat(u_ref.at[gl]), _carry_rows(h_ref, gl, bsz)
            for half in range(S5_HALVES):
                cols = slice(half * LANES, (half + 1) * LANES)
                o_ref[gl, half] = _bd(u_v, t_ref[gl, half], 1, 0) + _bd(h_v, m_ref[gl, :, cols], 1, 0)

    cspec = _mspec(gp, S5_HALVES, r, LANES)
    return pl.pallas_call(
        body, name=name, grid=(g // gp,),
        in_specs=[cspec, _carry_spec(nck), _mspec(gp, S5_HALVES, S5_QC, LANES), _mspec(gp, 4 * S5_P, S5_QC)],
        out_specs=cspec, out_shape=jax.ShapeDtypeStruct((g, S5_HALVES, r, LANES), F32),
        compiler_params=_params(("parallel",)))(u, hin, tt, mt)


def s5_out_bwd(dy, u, hin, tt, mt, *, bsz, name):
    g, _, r, _ = u.shape
    gp, nck = 8 // bsz, r // bsz

    def body(dy_ref, u_ref, h_ref, t_ref, m_ref, dh_ref, dt_ref, dm_ref, du_ref):
        for gl in range(gp):
            dy_v, u_v = _cat(dy_ref.at[gl]), _cat(u_ref.at[gl])
            _carry_put(dh_ref, gl, bsz, _bd(dy_v, m_ref[gl], 1, 1))
            dm_ref[gl] = _bd(_carry_rows(h_ref, gl, bsz), dy_v, 0, 0)
            du = None
            for half in range(S5_HALVES):
                dy_h = dy_ref[gl, half]
                dt_ref[gl, half] = _bd(u_v, dy_h, 0, 0)
                part = _bd(dy_h, t_ref[gl, half], 1, 1)
                du = part if du is None else du + part
            _put(du_ref.at[gl], du)

    cspec, tspec = _mspec(gp, S5_HALVES, r, LANES), _mspec(gp, S5_HALVES, S5_QC, LANES)
    mspec = _mspec(gp, 4 * S5_P, S5_QC)
    return pl.pallas_call(
        body, name=name, grid=(g // gp,),
        in_specs=[cspec, cspec, _carry_spec(nck), tspec, mspec],
        out_specs=[_carry_spec(nck), tspec, mspec, cspec],
        out_shape=[jax.ShapeDtypeStruct((nck, g * bsz, 4 * S5_P), F32),
                   jax.ShapeDtypeStruct((g, S5_HALVES, S5_QC, LANES), F32),
                   jax.ShapeDtypeStruct((g, 4 * S5_P, S5_QC), F32), jax.ShapeDtypeStruct((g, S5_HALVES, r, LANES), F32)],
        compiler_params=_params(("parallel",)))(dy, u, hin, tt, mt)


def s5_state_in_bwd(ds, u, wt, du1, *, bsz, name):
    g, _, r, _ = u.shape
    gp, nck = 8 // bsz, r // bsz

    def body(ds_ref, u_ref, w_ref, du1_ref, du_ref, dw_ref):
        for gl in range(gp):
            ds_v = _carry_rows(ds_ref, gl, bsz)
            _put(du_ref.at[gl], _cat(du1_ref.at[gl]) + _bd(ds_v, w_ref[gl], 1, 1))
            dw_ref[gl] = _bd(_cat(u_ref.at[gl]), ds_v, 0, 0)

    cspec, wspec = _mspec(gp, S5_HALVES, r, LANES), _mspec(gp, S5_QC, 4 * S5_P)
    return pl.pallas_call(
        body, name=name, grid=(g // gp,),
        in_specs=[_carry_spec(nck), cspec, wspec, cspec], out_specs=[cspec, wspec],
        out_shape=[jax.ShapeDtypeStruct((g, S5_HALVES, r, LANES), F32), jax.ShapeDtypeStruct((g, S5_QC, 4 * S5_P), F32)],
        compiler_params=_params(("parallel",)))(ds, u, wt, du1)


def _s5_post(ypre, u, dvec, wv, wg, bv, bg, nw):
    g = _gelu(ypre + dvec * u)
    out = (dot_nn(g, wv) + bv) * jax.nn.sigmoid(dot_nn(g, wg) + bg)
    return (_rms(out, nw),)


def _ssd_post(y, z, nw):
    return (_rms(y * _silu(z), nw),)


def _block_diag(w):
    eye = jnp.eye(S5_G, dtype=w.dtype)
    return jnp.einsum('gcd,gh->gchd', w, eye).reshape(S5_W, S5_W)


def _diag_blocks(w):
    v = w.reshape(S5_G, S5_C, S5_G, S5_C)
    return v[jnp.arange(S5_G), :, jnp.arange(S5_G), :]


def _dt_rows(dt, bsz):
    seq = dt.shape[0] // bsz
    return jnp.transpose(dt.reshape(bsz, seq, 2, SGROUPS, HPG), (0, 3, 2, 4, 1)).reshape(bsz, SGROUPS, 2 * HPG, seq)


def _dt_from_rows(dr):
    bsz, _, _, seq = dr.shape
    return jnp.transpose(dr.reshape(bsz, SGROUPS, 2, HPG, seq), (0, 4, 2, 1, 3)).reshape(bsz * seq, 2 * HEADS)


def _head_params(f, b):
    return jnp.concatenate([f.reshape(SGROUPS, HPG), b.reshape(SGROUPS, HPG)], axis=1)[:, :, None]


def _head_grads(gr):
    v = gr.sum(0)[:, :, 0]
    return v[:, :HPG].reshape(HEADS), v[:, HPG:].reshape(HEADS)


def local_step(x, target, w):
    bsz, seq, d = x.shape
    t = bsz * seq
    x2, tgt2 = x.reshape(t, d), target.reshape(t, d)
    g = {}
    row = lambda v: v.reshape(1, -1)
    bf = lambda v: v.astype(BF16)

    w_in = _unshard(bf(w['w_in']), SHARDED['w_in'])
    cuts = [0, SSD_W, SSD_W + XBC_W, SSD_W + XBC_W + 2 * HEADS, w_in.shape[1]]
    w_in_parts = [w_in[:, a:b] for a, b in zip(cuts[:-1], cuts[1:])]
    norm_mix = row(w['norm_mix_w']) + w.get('token', 0.0)
    (hn,) = rowmap_fwd(lambda a, nw: (_rms(a, nw),), [x2], [norm_mix], [(d, BF16)], tm=512, name="rms_mix")
    z, xbc, dt, u = matmul_multi(hn, w_in_parts, name="in_proj")

    conv_w, conv_b = _unshard(w['ssd_conv_w'], SHARDED['ssd_conv_w']), row(w['ssd_conv_b'])
    act = ssd_conv_fwd(xbc, conv_w, conv_b, bsz=bsz, name="ssd_conv")
    dtr = _dt_rows(dt, bsz)
    prs = (_head_params(w['ssd_dt_bias_fwd'], w['ssd_dt_bias_bwd']),
           _head_params(w['ssd_a_log_fwd'], w['ssd_a_log_bwd']),
           _head_params(w['ssd_d'], jnp.zeros_like(w['ssd_d'])))
    act3 = act.reshape(bsz, seq, XBC_W)
    y_scan, ssd_states = ssd_scan_fwd(act3, dtr, prs, name="ssd_scan")
    y_scan = y_scan.reshape(t, SSD_W)
    ssd_nw = row(w['ssd_norm_w'])
    (y_ssd,) = rowmap_fwd(_ssd_post, [y_scan, z], [ssd_nw], [(SSD_W, BF16)], tm=512, name="ssd_post")

    s5_names = ['s5_lambda_re_fwd', 's5_lambda_im_fwd', 's5_log_step_fwd', 's5_lambda_re_bwd', 's5_lambda_im_bwd',
                's5_log_step_bwd', 's5_b_re', 's5_b_im', 's5_c_re_fwd', 's5_c_im_fwd', 's5_c_re_bwd', 's5_c_im_bwd']
    (kt, wt, mt, da, db), s5_pull = jax.vjp(_s5_operators, *[w[n] for n in s5_names])
    tt_b, wt_b, mt_b = s5_toeplitz(kt, name="s5_toeplitz"), bf(wt), bf(mt)
    da_r, db_r = jnp.repeat(da, bsz, axis=0), jnp.repeat(db, bsz, axis=0)
    uc = to_chunks(u, name="s5_to_chunks_u")
    hin = s5_carry_fwd(s5_state_in(uc, wt_b, bsz=bsz, name="s5_state_in"), da_r, db_r, name="s5_carry")
    ypre = from_chunks(s5_out(uc, hin, tt_b, mt_b, bsz=bsz, name="s5_out"), name="s5_from_chunks_y", as_blocks=True)
    glu_w = w['s5_glu_w']
    s5_par = [row(w['s5_d']), _block_diag(glu_w[:, :, :S5_C]), _block_diag(glu_w[:, :, S5_C:]),
              row(w['s5_glu_b'][:, :S5_C]), row(w['s5_glu_b'][:, S5_C:]), row(w['s5_norm_w'])]
    (y_s5,) = rowmap_fwd(_s5_post, [ypre, u], s5_par, [(S5_W, BF16)], tm=512, name="s5_post")

    if 'late' in w:
        w = {**w, **w['late'](y_s5)}
    w_out = bf(w['w_out']).reshape(SSD_W + S5_W, d)
    norm_ffn = row(w['norm_ffn_w'])
    h1, hn2 = matmul_sum([y_ssd, y_s5], [w_out[:SSD_W], w_out[SSD_W:]], add=x2, norm_w=norm_ffn, name="out_proj")
    pad_c = FFN_PAD - FFN_BLK
    half = N_DEV // 2
    w_up3 = jnp.pad(bf(w['ffn_w_up']), ((0, 0), (0, 0), (0, pad_c)))
    w_down = jnp.pad(bf(w['ffn_w_down']).reshape(half, FFN_BLK, d), ((0, 0), (0, pad_c), (0, 0)))
    w_down = w_down.reshape(half * FFN_PAD, d)
    fconv_w = jnp.pad(w['ffn_conv_w'], ((0, 0), (0, 0), (0, pad_c)))
    fconv_w = jnp.transpose(fconv_w, (1, 0, 2)).reshape(FCONV, N_DEV * FFN_PAD)
    fconv_b = row(jnp.pad(w['ffn_conv_b'].reshape(N_DEV, FFN_BLK), ((0, 0), (0, pad_c))))
    up = matmul_cols(hn2, w_up3, out_dtype=BF16, name="ffn_up")
    fact = ffn_act_fwd(up, fconv_w, fconv_b, bsz=bsz, name="ffn_act")
    loss, dh2, g_nf = loss_head(h1, tgt2, row(w['norm_final_w']), matmul=(fact, w_down), tm=512, name="ffn_down_loss")
    g['norm_final_w'] = g_nf.reshape(-1)

    dfact = matmul_sum([dh2], [w_down], nt=True, tm=1024, name="ffn_down_dx")
    g_down = matmul_tn(fact, dh2, name="ffn_down_dw").reshape(half, FFN_PAD, d)[:, :FFN_BLK]
    g['ffn_w_down'] = g_down.reshape(N_DEV, FFN_BLK // 2, d)
    dval, dgate, dwv, dwg, dbv, dbg = ffn_act_bwd(up, dfact, fconv_w, fconv_b, bsz=bsz, name="ffn_act_bwd")
    g_cw = jnp.concatenate([dwv, dwg], axis=1).reshape(FCONV, N_DEV, FFN_PAD)[:, :, :FFN_BLK]
    g['ffn_conv_w'] = jnp.transpose(g_cw, (1, 0, 2))
    g['ffn_conv_b'] = jnp.concatenate([dbv, dbg], axis=1).reshape(N_DEV, FFN_PAD)[:, :FFN_BLK].reshape(-1)
    windows = [(dval, FFN_PAD, p) for p in range(half)] + [(dgate, FFN_PAD, p) for p in range(half)]
    g['ffn_w_up'] = jnp.concatenate([matmul_tn(hn2, dval, out_blocks=half, name="ffn_up_dw_val"),
                                     matmul_tn(hn2, dgate, out_blocks=half, name="ffn_up_dw_gate")],
                                    axis=0)[:, :, :FFN_BLK]
    send_early = w.get('on_grads')
    if send_early:
        norm_ffn = norm_ffn + send_early(g, ['ffn_w_up', 'ffn_w_down'])
    dh1, g_nffn = matmul_sum(windows, [(w_up3, p) for p in range(N_DEV)], nt=True, tm=256,
                             norm_bwd=(h1, norm_ffn, dh2), name="ffn_up_dx")
    g['norm_ffn_w'] = g_nffn.reshape(-1)

    dycat = matmul_sum([dh1], [w_out], nt=True, tm=1024, name="out_proj_dx")
    g['w_out'] = jnp.concatenate([matmul_tn(y_ssd, dh1, name="out_proj_dw_ssd"),
                                  matmul_tn(y_s5, dh1, name="out_proj_dw_s5")], axis=0).reshape(w['w_out'].shape)
    if send_early:
        ssd_nw = ssd_nw + send_early(g, ['w_out'])
    dy_scan, dz, g_snw = rowmap_bwd(_ssd_post, [y_scan, z], [ssd_nw], [(dycat, SSD_W, 0)], tm=512,
                                    name="ssd_post_bwd")
    g['ssd_norm_w'] = g_snw.reshape(-1)
    dypre, du_a, g_d, g_wv, g_wg, g_bv, g_bg, g_s5nw = rowmap_bwd(
        _s5_post, [ypre, u], s5_par, [(dycat, S5_W, SSD_W // S5_W)], tm=512, name="s5_post_bwd")
    g['s5_d'], g['s5_norm_w'] = g_d.reshape(-1), g_s5nw.reshape(-1)
    g['s5_glu_w'] = jnp.concatenate([_diag_blocks(g_wv), _diag_blocks(g_wg)], axis=-1)
    g['s5_glu_b'] = jnp.concatenate([g_bv.reshape(S5_G, S5_C), g_bg.reshape(S5_G, S5_C)], axis=-1)

    dyc = to_chunks(dypre, name="s5_to_chunks_dy")
    dhin, dtt, dmt, du1 = s5_out_bwd(dyc, uc, hin, tt_b, mt_b, bsz=bsz, name="s5_out_bwd")
    ds, gda, gdb = s5_carry_bwd(hin, dhin, da_r, db_r, name="s5_carry_bwd")
    duc, dwt = s5_state_in_bwd(ds, uc, wt_b, du1, bsz=bsz, name="s5_state_in_bwd")
    du = from_chunks(duc, add=du_a, name="s5_from_chunks_du")
    fold = lambda v: v.reshape(S5_G, bsz, -1).sum(1)
    dkt = s5_toeplitz_bwd(dtt, name="s5_toeplitz_bwd")
    for n, gv in zip(s5_names, s5_pull((dkt, dwt, dmt, fold(gda), fold(gdb)))):
        g[n] = gv

    dxs, dbm, dcm, ddtr, gbr, gar, gdk = ssd_scan_bwd(
        act3, dtr, prs, ssd_states, dy_scan.reshape(bsz, seq, SSD_W), name="ssd_scan_bwd")
    g['ssd_dt_bias_fwd'], g['ssd_dt_bias_bwd'] = _head_grads(gbr)
    g['ssd_a_log_fwd'], g['ssd_a_log_bwd'] = _head_grads(gar)
    g['ssd_d'] = _head_grads(gdk)[0]
    dparts_act = [v.reshape(t, v.shape[-1]) for v in (dxs, dbm, dcm)]
    dxbc, g_cw, g_cb = ssd_conv_bwd(xbc, dparts_act, conv_w, conv_b, bsz=bsz, name="ssd_conv_bwd")
    g['ssd_conv_w'] = _shard_rows(g_cw, SHARDED['ssd_conv_w']).reshape(w['ssd_conv_w'].shape)
    g['ssd_conv_b'] = g_cb.reshape(-1)
    ddt = _dt_from_rows(ddtr)

    if send_early:
        ddt = ddt + send_early(g, [], loss=loss)
    dparts = [dz, dxbc, ddt, du]
    g_in = jnp.concatenate([matmul_tn(hn, dp, name=f"in_proj_dw_{i}") for i, dp in enumerate(dparts)], axis=1)
    g['w_in'] = _shard_rows(g_in, SHARDED['w_in']).reshape(w['w_in'].shape)
    if send_early:
        dparts[2] = ddt + send_early(g, ['w_in'])
    dx, g_nmix = matmul_sum(dparts, w_in_parts, nt=True, tm=256, norm_bwd=(x2, norm_mix, dh1), name="in_proj_dx")
    g['norm_mix_w'] = g_nmix.reshape(-1)
    return loss, dx.reshape(bsz, seq, d), g


ANY = pl.BlockSpec(memory_space=pl.ANY)


def all_gather(shards, *, name):
    n = len(shards)

    def body(*refs):
        x_refs, out_refs = refs[:n], refs[n:2 * n]
        send_sems, recv_sems, local_sems = refs[2 * n:]
        x, y, c = lax.axis_index("x"), lax.axis_index("y"), lax.axis_index("c")
        me, sibling = (x, y, c), (x, y, 1 - c)
        chips = [(1 - x, y), (x, 1 - y), (1 - x, 1 - y)]

        def copy(k, j, block, to, own=False):
            dst = out_refs[j].at[4 * block[0] + 2 * block[1] + block[2]]
            return pltpu.make_async_remote_copy(
                src_ref=x_refs[j] if own else dst, dst_ref=dst,
                send_sem=send_sems.at[k, j], recv_sem=recv_sems.at[k, j], device_id=to, device_id_type=MESH)

        mine = [pltpu.make_async_copy(x_refs[j], out_refs[j].at[4 * x + 2 * y + c], local_sems.at[j]) for j in range(n)]
        first = [copy(0, j, me, sibling, own=True) for j in range(n)]
        first += [copy(1 + i, j, me, (*chip, c), own=True) for i, chip in enumerate(chips) for j in range(n)]
        for cp in mine + first:
            cp.start()
        passed = []
        for i, chip in enumerate(chips):
            for j in range(n):
                copy(1 + i, j, (*chip, c), me).wait_recv()
                passed.append(copy(4 + i, j, (*chip, c), sibling))
                passed[-1].start()
        for j in range(n):
            copy(0, j, sibling, me).wait_recv()
        for i, chip in enumerate(chips):
            for j in range(n):
                copy(4 + i, j, (*chip, 1 - c), me).wait_recv()
        for cp in first + passed:
            cp.wait_send()
        for cp in mine:
            cp.wait()

    return pl.pallas_call(
        body, name=name, out_shape=[jax.ShapeDtypeStruct((N_DEV,) + s.shape, s.dtype) for s in shards],
        in_specs=[ANY] * n, out_specs=[ANY] * n,
        scratch_shapes=[pltpu.SemaphoreType.DMA((7, n)), pltpu.SemaphoreType.DMA((7, n)),
                        pltpu.SemaphoreType.DMA((n,))],
    )(*shards)


HBM_SPEC = pl.BlockSpec(memory_space=pltpu.HBM)
SEM_SPEC = pl.BlockSpec(memory_space=pltpu.SEMAPHORE)
SPLIT_PARAMS = pltpu.CompilerParams(has_side_effects=pltpu.SideEffectType.DATAFLOW_SIDE_EFFECTING)


def _peer_copies(src_refs, land_refs, send_sems, recv_sems, indexed):
    x, y, c = lax.axis_index("x"), lax.axis_index("y"), lax.axis_index("c")
    me = 4 * x + 2 * y + c
    copies = []
    for k in range(1, N_DEV):
        px = (1 - x) if k & 4 else x
        py = (1 - y) if k & 2 else y
        pc = (1 - c) if k & 1 else c
        for j, (src, land) in enumerate(zip(src_refs, land_refs)):
            sem = (k - 1) * len(src_refs) + j
            copies.append(pltpu.make_async_remote_copy(
                src_ref=src.at[4 * px + 2 * py + pc] if indexed else src, dst_ref=land.at[me],
                send_sem=send_sems.at[sem], recv_sem=recv_sems.at[sem],
                device_id=(px, py, pc), device_id_type=MESH))
    return copies


def scatter_start(srcs, *, name, indexed):
    n = len(srcs)
    lands = [lax.empty(s.shape if indexed else (N_DEV,) + s.shape, s.dtype) for s in srcs]

    def body(*refs):
        send_sems, recv_sems = refs[2 * n], refs[2 * n + 1]
        for cp in _peer_copies(refs[:n], refs[n:2 * n], send_sems, recv_sems, indexed):
            cp.start()
        refs[-1][...] = jnp.zeros_like(refs[-1])

    hbm = lambda a: pltpu.HBM(a.shape, a.dtype)
    sems = pltpu.SemaphoreType.DMA(((N_DEV - 1) * n,))
    res = pl.pallas_call(
        body, name=name,
        out_shape=(sems, sems, *[hbm(a) for a in srcs + lands], jax.ShapeDtypeStruct((8, LANES), F32)),
        in_specs=[HBM_SPEC] * (2 * n),
        out_specs=(SEM_SPEC, SEM_SPEC, *[HBM_SPEC] * (2 * n), pl.BlockSpec(memory_space=pltpu.VMEM)),
        input_output_aliases={i: 2 + i for i in range(2 * n)}, compiler_params=SPLIT_PARAMS,
    )(*[pltpu.with_memory_space_constraint(a, pltpu.HBM) for a in srcs + lands])
    return res[0], res[1], list(res[2:2 + n]), list(res[2 + n:2 + 2 * n]), res[-1]


def scatter_wait(send_sems, recv_sems, srcs, lands, after, *, name, indexed):
    n = len(srcs)

    def body(*refs):
        for cp in _peer_copies(refs[:n], refs[n:2 * n], refs[2 * n], refs[2 * n + 1], indexed):
            cp.wait_send()
            cp.wait_recv()

    hbm = lambda a: pltpu.HBM(a.shape, a.dtype)
    res = pl.pallas_call(
        body, name=name, out_shape=tuple(hbm(a) for a in srcs + lands),
        in_specs=[HBM_SPEC] * (2 * n) + [SEM_SPEC, SEM_SPEC, ANY], out_specs=tuple([HBM_SPEC] * (2 * n)),
        input_output_aliases={i: i for i in range(2 * n)}, compiler_params=SPLIT_PARAMS,
    )(*srcs, *lands, send_sems, recv_sems, after)
    return list(res[:n]), list(res[n:])


def _adam_rows(r, c):
    fits = [t for t in range(8, r + 1, 8) if r % t == 0 and N_DEV * t * c * 4 <= 6 * 2 ** 20]
    return max(fits) if fits else r


def adamw(recv, w, m, v, *, name):
    _, r, n = recv.shape
    tr = _adam_rows(r, n)

    def body(r_ref, w_ref, m_ref, v_ref, g_ref, d_ref, nm_ref, nv_ref):
        g = r_ref[0].astype(F32)
        for s in range(1, N_DEV):
            g = g + r_ref[s].astype(F32)
        m_new = ADAM_B1 * m_ref[...] + (1.0 - ADAM_B1) * g
        v_new = ADAM_B2 * v_ref[...] + (1.0 - ADAM_B2) * jnp.square(g)
        m_hat = m_new / (1.0 - ADAM_B1 ** ADAM_STEP)
        v_hat = v_new / (1.0 - ADAM_B2 ** ADAM_STEP)
        g_ref[...] = g
        d_ref[...] = -ADAM_LR * (m_hat / (jnp.sqrt(v_hat) + ADAM_EPS) + ADAM_WD * w_ref[...])
        nm_ref[...] = m_new
        nv_ref[...] = v_new

    blk = pl.BlockSpec((tr, n), lambda i: (i, 0))
    return pl.pallas_call(
        body, name=name, grid=(r // tr,), in_specs=[pl.BlockSpec((N_DEV, tr, n), lambda i: (0, i, 0)), blk, blk, blk],
        out_specs=[blk] * 4, out_shape=[jax.ShapeDtypeStruct((r, n), F32)] * 4,
        compiler_params=_params(("parallel",)))(recv, w, m, v)


def _shard_rows(full, axis):
    if axis == 0:
        return full.reshape(N_DEV, -1)
    r, c = full.shape
    return jnp.transpose(full.reshape(r, N_DEV, c // N_DEV), (1, 0, 2)).reshape(N_DEV, -1)


def _unshard(blocks, axis):
    if axis == 0:
        return blocks.reshape(-1, blocks.shape[-1])
    return jnp.transpose(blocks, (1, 0, 2)).reshape(blocks.shape[1], -1)


def kernel(x, norm_mix_w, w_in, ssd_conv_w, ssd_conv_b, ssd_dt_bias_fwd, ssd_dt_bias_bwd, ssd_a_log_fwd, ssd_a_log_bwd, ssd_d, ssd_norm_w, s5_lambda_re_fwd, s5_lambda_im_fwd, s5_log_step_fwd, s5_lambda_re_bwd, s5_lambda_im_bwd, s5_log_step_bwd, s5_b_re, s5_b_im, s5_c_re_fwd, s5_c_im_fwd, s5_c_re_bwd, s5_c_im_bwd, s5_d, s5_glu_w, s5_glu_b, s5_norm_w, w_out, norm_ffn_w, ffn_w_up, ffn_conv_w, ffn_conv_b, ffn_w_down, norm_final_w, loss_target, m_norm_mix_w, m_w_in, m_ssd_conv_w, m_ssd_conv_b, m_ssd_dt_bias_fwd, m_ssd_dt_bias_bwd, m_ssd_a_log_fwd, m_ssd_a_log_bwd, m_ssd_d, m_ssd_norm_w, m_s5_lambda_re_fwd, m_s5_lambda_im_fwd, m_s5_log_step_fwd, m_s5_lambda_re_bwd, m_s5_lambda_im_bwd, m_s5_log_step_bwd, m_s5_b_re, m_s5_b_im, m_s5_c_re_fwd, m_s5_c_im_fwd, m_s5_c_re_bwd, m_s5_c_im_bwd, m_s5_d, m_s5_glu_w, m_s5_glu_b, m_s5_norm_w, m_w_out, m_norm_ffn_w, m_ffn_w_up, m_ffn_conv_w, m_ffn_conv_b, m_ffn_w_down, m_norm_final_w, v_norm_mix_w, v_w_in, v_ssd_conv_w, v_ssd_conv_b, v_ssd_dt_bias_fwd, v_ssd_dt_bias_bwd, v_ssd_a_log_fwd, v_ssd_a_log_bwd, v_ssd_d, v_ssd_norm_w, v_s5_lambda_re_fwd, v_s5_lambda_im_fwd, v_s5_log_step_fwd, v_s5_lambda_re_bwd, v_s5_lambda_im_bwd, v_s5_log_step_bwd, v_s5_b_re, v_s5_b_im, v_s5_c_re_fwd, v_s5_c_im_fwd, v_s5_c_re_bwd, v_s5_c_im_bwd, v_s5_d, v_s5_glu_w, v_s5_glu_b, v_s5_norm_w, v_w_out, v_norm_ffn_w, v_ffn_w_up, v_ffn_conv_w, v_ffn_conv_b, v_ffn_w_down, v_norm_final_w):
    args = dict(locals())
    strip = lambda n, v: v if n == 'norm_final_w' else v[0]
    w = {n: strip(n, args[n]) for n in WEIGHTS}

    mats = ['w_in', 'w_out', 'ffn_w_up', 'ffn_w_down']
    convs = ['ssd_conv_w', 'ffn_conv_w']
    shard = lambda n: w[n].astype(BF16) if n in mats else w[n]
    early, late = ['w_in', 'ssd_conv_w'], ['w_out', 'ffn_w_up', 'ffn_w_down', 'ffn_conv_w']
    full = dict(w)
    full.update(zip(early, all_gather([shard(n) for n in early], name="weight_all_gather")))
    ssem, rsem, src_thru, land_thru, token = scatter_start([shard(n) for n in late], name="weight_gather_start",
                                                           indexed=False)
    me = 4 * lax.axis_index("x") + 2 * lax.axis_index("y") + lax.axis_index("c")

    def late_weights(after):
        own, landed = scatter_wait(ssem, rsem, src_thru, land_thru, after, name="weight_gather_wait", indexed=False)
        return {n: lax.dynamic_update_index_in_dim(l, o, me, 0) for n, o, l in zip(late, own, landed)}

    full['late'], full['token'] = late_weights, token[:1, :1]

    pending = []
    last = 'norm_mix_w'
    small = convs + [n for n in WEIGHTS if n not in SHARDED and n != last]
    slot = {n: -(-w[n].size // (8 * LANES)) * 8 for n in small}
    used = sum(slot.values()) + 8
    nrow = -(-used // PACK_ROWS) * PACK_ROWS

    def tiles(v, n):
        return jnp.pad(v, ((0, 0), (0, slot[n] * LANES - v.shape[1]))).reshape(v.shape[0], slot[n], LANES)

    def send_early(grads, names, loss=None):
        srcs = [grads[n].astype(BF16) for n in names]
        if loss is not None:
            pieces = [tiles(grads[n].reshape(N_DEV, -1), n) if n in SHARDED else
                      jnp.broadcast_to(tiles(grads[n].reshape(1, -1), n), (N_DEV, slot[n], LANES)) for n in small]
            pieces.append(jnp.broadcast_to(jnp.pad(loss.reshape(1, 1, 1), ((0, 0), (0, 7), (0, LANES - 1))),
                                           (N_DEV, 8, LANES)))
            pieces.append(jnp.zeros((N_DEV, nrow - used, LANES), F32))
            srcs.append(jnp.concatenate(pieces, axis=1))
            names = names + ['small']
        started = scatter_start(srcs, name="grad_start_" + names[0], indexed=True)
        pending.append((names,) + started[:4])
        return started[4][:1, :1]

    full['on_grads'] = send_early
    loss, grad_x, g = local_step(x, loss_target, full)

    last_send = jnp.broadcast_to(g[last].reshape(1, -1, LANES), (N_DEV, g[last].size // LANES, LANES))
    last_started = scatter_start([last_send], name="grad_start_" + last, indexed=True)
    recv, outs = {}, [{}, {}, {}, {}]

    def arrived(names, started, after):
        own, landed = scatter_wait(*started, after, name="grad_wait_" + names[0], indexed=True)
        for n, o, l in zip(names, own, landed):
            recv[n] = lax.dynamic_update_index_in_dim(l, lax.dynamic_index_in_dim(o, me, 0, keepdims=False), me, 0)

    def update(n):
        shape = recv[n].shape[1:]
        res = adamw(recv[n], *[strip(n, args[p + n]).reshape(shape) for p in ('', 'm_', 'v_')], name="adamw_" + n)
        for o, p in zip(outs, res):
            o[n] = p.reshape(args[n].shape)

    for names, *started in pending:
        arrived(names, started, last_started[4])
    for n in mats:
        update(n)

    def pack(prefix):
        vals = [tiles(strip(n, args[prefix + n]).reshape(1, -1), n)[0] for n in small]
        return jnp.concatenate(vals + [jnp.zeros((nrow - used + 8, LANES), F32)], axis=0)

    packed = adamw(recv['small'], pack(''), pack('m_'), pack('v_'), name="adamw_small")
    arrived([last], last_started[:4], packed[1])
    update(last)
    off = 0
    for n in small:
        for o, p in zip(outs, packed):
            o[n] = p[off:off + slot[n]].reshape(-1)[:w[n].size].reshape(args[n].shape)
        off += slot[n]
    loss_out = packed[0][off, 0].reshape(())
    return (loss_out, grad_x, *[o[n] for o in outs for n in WEIGHTS])
```

```python
import functools

import jax
import jax.numpy as jnp
from jax import lax
from jax.experimental import pallas as pl
from jax.experimental.pallas import tpu as pltpu

F32, BF16 = jnp.float32, jnp.bfloat16
N_DEV = 8
D_MODEL = 1024
SSD_W, HEADS, HDIM, SGROUPS, HPG, NSTATE, SCONV, QC = 1024, 16, 64, 4, 4, 128, 5, 128
XBC_W = SSD_W + 2 * SGROUPS * NSTATE
S5_W, S5_G, S5_C, S5_P, S5_Q = 512, 32, 16, 64, 16
S5_QC = S5_Q * S5_C
CARRY_ROWS = 32
DFF, FCONV = 2816, 3
FFN_BLK, FFN_PAD = 704, 768
EPS = 1e-6
ADAM_LR, ADAM_B1, ADAM_B2, ADAM_EPS, ADAM_WD, ADAM_STEP = 0.001, 0.9, 0.999, 1e-08, 0.01, 10
LANES = 128
MESH = pl.DeviceIdType.MESH

WEIGHTS = ['norm_mix_w', 'w_in', 'ssd_conv_w', 'ssd_conv_b', 'ssd_dt_bias_fwd', 'ssd_dt_bias_bwd', 'ssd_a_log_fwd',
           'ssd_a_log_bwd', 'ssd_d', 'ssd_norm_w', 's5_lambda_re_fwd', 's5_lambda_im_fwd', 's5_log_step_fwd',
           's5_lambda_re_bwd', 's5_lambda_im_bwd', 's5_log_step_bwd', 's5_b_re', 's5_b_im', 's5_c_re_fwd', 's5_c_im_fwd',
           's5_c_re_bwd', 's5_c_im_bwd', 's5_d', 's5_glu_w', 's5_glu_b', 's5_norm_w', 'w_out', 'norm_ffn_w', 'ffn_w_up',
           'ffn_conv_w', 'ffn_conv_b', 'ffn_w_down', 'norm_final_w']
SHARDED = {'w_in': 1, 'ssd_conv_w': 1, 'w_out': 0, 'ffn_w_up': 1, 'ffn_conv_w': 1, 'ffn_w_down': 0}
FULL_SHAPE = {'w_in': (1024, 3616), 'ssd_conv_w': (5, 2048), 'w_out': (1536, 1024), 'ffn_w_up': (1024, 5632),
              'ffn_conv_w': (3, 5632), 'ffn_w_down': (2816, 1024)}
PACK_ROWS = 512


def _pick(n, cap=1536):
    if n <= cap:
        return n
    return max(t for t in range(LANES, cap + 1, LANES) if n % t == 0)


def _params(sem):
    return pltpu.CompilerParams(dimension_semantics=sem)


def _bd(a, b, ca, cb):
    return lax.dot_general(a.astype(BF16), b.astype(BF16), (((ca,), (cb,)), ((), ())), preferred_element_type=F32)


@jax.custom_vjp
def dot_nn(a, b):
    return _bd(a, b, 1, 0)


dot_nn.defvjp(lambda a, b: (_bd(a, b, 1, 0), (a, b)),
              lambda r, g: (_bd(g, r[1], 1, 1).astype(r[0].dtype), _bd(r[0], g, 0, 0).astype(r[1].dtype)))


@jax.custom_vjp
def dot_nt(a, b):
    return _bd(a, b, 1, 1)


dot_nt.defvjp(lambda a, b: (_bd(a, b, 1, 1), (a, b)),
              lambda r, g: (_bd(g, r[1], 1, 0).astype(r[0].dtype), _bd(g, r[0], 0, 0).astype(r[1].dtype)))


@jax.custom_vjp
def dot_tn(a, b):
    return _bd(a, b, 0, 0)


dot_tn.defvjp(lambda a, b: (_bd(a, b, 0, 0), (a, b)),
              lambda r, g: (_bd(r[1], g, 1, 1).astype(r[0].dtype), _bd(r[0], g, 1, 0).astype(r[1].dtype)))


def _rows2(v):
    h = v.shape[0] // 2
    return v[:h], v[h:]


def _cols2(v):
    h = v.shape[1] // 2
    return v[:, :h], v[:, h:]


@jax.custom_vjp
def dot2_nn(la, lb, x):
    return _rows2(_bd(jnp.concatenate([la, lb], axis=0), x, 1, 0))


def _dot2_nn_bwd(res, g):
    la, lb, x = res
    gcat, lcat = jnp.concatenate(g, axis=0), jnp.concatenate([la, lb], axis=0)
    return (*_rows2(_bd(gcat, x, 1, 1)), _bd(lcat, gcat, 0, 0))


dot2_nn.defvjp(lambda la, lb, x: (dot2_nn(la, lb, x), (la, lb, x)), _dot2_nn_bwd)


@jax.custom_vjp
def dot_nt2(c, p0, p1):
    return _cols2(_bd(c, jnp.concatenate([p0, p1], axis=0), 1, 1))


def _dot_nt2_bwd(res, g):
    c, p0, p1 = res
    gcat = jnp.concatenate(g, axis=1)
    return (_bd(gcat, jnp.concatenate([p0, p1], axis=0), 1, 0), *_rows2(_bd(gcat, c, 0, 0)))


dot_nt2.defvjp(lambda c, p0, p1: (dot_nt2(c, p0, p1), (c, p0, p1)), _dot_nt2_bwd)


@jax.custom_vjp
def dot_tn2(a0, a1, b):
    return _rows2(_bd(jnp.concatenate([a0, a1], axis=1), b, 0, 0))


def _dot_tn2_bwd(res, g):
    a0, a1, b = res
    gcat, acat = jnp.concatenate(g, axis=0), jnp.concatenate([a0, a1], axis=1)
    return (*_cols2(_bd(b, gcat, 1, 1)), _bd(acat, gcat, 1, 0))


dot_tn2.defvjp(lambda a0, a1, b: (dot_tn2(a0, a1, b), (a0, a1, b)), _dot_tn2_bwd)


def _split3(x):
    hi = x.astype(BF16)
    r = x - hi.astype(F32)
    mid = r.astype(BF16)
    lo = (r - mid.astype(F32)).astype(BF16)
    return hi, mid, lo


def _cum_matrix(q, upper):
    ri = lax.broadcasted_iota(jnp.int32, (q, q), 0)
    ci = lax.broadcasted_iota(jnp.int32, (q, q), 1)
    return jnp.where((ci >= ri) if upper else (ci <= ri), 1.0, 0.0).astype(BF16)


def _exact_right(x, mat):
    return sum(jnp.dot(p, mat, preferred_element_type=F32) for p in _split3(x))


@functools.partial(jax.custom_vjp, nondiff_argnums=(1,))
def cum_row(x, rev):
    return _exact_right(x, _cum_matrix(x.shape[1], not rev))


cum_row.defvjp(lambda x, rev: (cum_row(x, rev), None),
               lambda rev, _, g: (_exact_right(g, _cum_matrix(g.shape[1], rev)),))


def _softplus(x):
    return jnp.maximum(x, 0.0) + jnp.log(1.0 + jnp.exp(-jnp.abs(x)))


def _silu(x):
    return x * jax.nn.sigmoid(x)


def _gelu(x):
    return 0.5 * x * (1.0 + jnp.tanh(0.7978845608028654 * (x + 0.044715 * (x * x * x))))


def _rms(x, w):
    xf = x.astype(F32)
    return xf * lax.rsqrt(jnp.mean(xf * xf, axis=-1, keepdims=True) + EPS) * w


def matmul_sum(a_list, b_list, *, name, out_dtype=F32, add=None, tm=512, nt=False, norm_w=None, norm_bwd=None):
    a_arrs = [a[0] if isinstance(a, tuple) else a for a in a_list]
    b_arrs = [b[0] if isinstance(b, tuple) else b for b in b_list]
    m, n = a_arrs[0].shape[0], b_arrs[0].shape[-2 if nt else -1]
    tm, tn, k = min(tm, m), _pick(n), len(a_list)
    assert (norm_w is None and norm_bwd is None) or tn == n

    def body(*refs):
        acc = None
        for a_ref, b_ref in zip(refs[:k], refs[k:2 * k]):
            p = _bd(a_ref[...], b_ref[...], 1, 1 if nt else 0)
            acc = p if acc is None else acc + p
        if add is not None:
            acc = acc + refs[2 * k][...]
        if norm_bwd is not None:
            x_ref, w_ref, res_ref, dx_ref, dw_ref = refs[-5:]
            dx, dw = jax.vjp(_rms, x_ref[...], w_ref[...])[1](acc)
            dx_ref[...] = dx + res_ref[...]

            @pl.when(pl.program_id(0) == 0)
            def _():
                dw_ref[...] = jnp.zeros_like(dw_ref)

            dw_ref[...] += dw
        elif norm_w is not None:
            refs[-2][...] = acc.astype(out_dtype)
            refs[-1][...] = _rms(acc, refs[-3][...]).astype(BF16)
        else:
            refs[-1][...] = acc.astype(out_dtype)

    def a_spec(a):
        if isinstance(a, tuple):
            return pl.BlockSpec((tm, a[1]), lambda i, j, blk=a[2]: (i, blk))
        return pl.BlockSpec((tm, a.shape[1]), lambda i, j: (i, 0))

    def b_spec(b):
        arr, p = b if isinstance(b, tuple) else (b, None)
        kk = arr.shape[-1 if nt else -2]
        shape, idx = ((tn, kk), lambda j: (j, 0)) if nt else ((kk, tn), lambda j: (0, j))
        mode = {'pipeline_mode': pl.Buffered(1)} if tn == n else {}
        if p is None:
            return pl.BlockSpec(shape, lambda i, j: idx(j), **mode)
        return pl.BlockSpec((None,) + shape, lambda i, j, p=p: (p,) + idx(j), **mode)

    in_specs = [a_spec(a) for a in a_list] + [b_spec(b) for b in b_list]
    args = a_arrs + b_arrs
    if add is not None:
        in_specs.append(pl.BlockSpec((tm, tn), lambda i, j: (i, j)))
        args.append(add)
    out_spec, out_shape = pl.BlockSpec((tm, tn), lambda i, j: (i, j)), jax.ShapeDtypeStruct((m, n), out_dtype)
    if norm_w is not None:
        in_specs.append(pl.BlockSpec(norm_w.shape, lambda i, j: (0, 0)))
        args.append(norm_w)
        out_spec, out_shape = [out_spec, out_spec], [out_shape, jax.ShapeDtypeStruct((m, n), BF16)]
    sem = ("parallel", "parallel")
    if norm_bwd is not None:
        x, w, res = norm_bwd
        wspec = pl.BlockSpec(w.shape, lambda i, j: (0, 0))
        in_specs += [out_spec, wspec, out_spec]
        args += [x, w, res]
        out_spec, out_shape = [out_spec, wspec], [jax.ShapeDtypeStruct((m, n), F32), jax.ShapeDtypeStruct(w.shape, F32)]
        sem = ("arbitrary", "arbitrary")
    return pl.pallas_call(
        body, name=name, grid=(m // tm, n // tn), in_specs=in_specs, out_specs=out_spec, out_shape=out_shape,
        compiler_params=_params(sem))(*args)


def matmul_multi(a, b_list, *, name, tm=512):
    m, kk = a.shape
    tm, nb = min(tm, m), len(b_list)

    def body(a_ref, *refs):
        a_v = a_ref[...]
        for b_ref, o_ref in zip(refs[:nb], refs[nb:]):
            o_ref[...] = _bd(a_v, b_ref[...], 1, 0)

    return pl.pallas_call(
        body, name=name, grid=(m // tm,),
        in_specs=[pl.BlockSpec((tm, kk), lambda i: (i, 0))] + [_full_spec(b) for b in b_list],
        out_specs=[pl.BlockSpec((tm, b.shape[1]), lambda i: (i, 0)) for b in b_list],
        out_shape=[jax.ShapeDtypeStruct((m, b.shape[1]), F32) for b in b_list],
        compiler_params=_params(("parallel",)))(a, *b_list)


def matmul_cols(a, b3, *, name, out_dtype=F32, tm=2048):
    m, kk = a.shape
    p, _, nb = b3.shape
    tm, tn = min(tm, m), _pick(nb, 768)
    per = nb // tn

    def body(a_ref, b_ref, o_ref):
        o_ref[...] = _bd(a_ref[...], b_ref[...], 1, 0).astype(out_dtype)

    return pl.pallas_call(
        body, name=name, grid=(m // tm, p * per),
        in_specs=[pl.BlockSpec((tm, kk), lambda i, j: (i, 0)),
                  pl.BlockSpec((None, kk, tn), lambda i, j: (j // per, 0, j % per))],
        out_specs=pl.BlockSpec((tm, tn), lambda i, j: (i, j)),
        out_shape=jax.ShapeDtypeStruct((m, p * nb), out_dtype),
        compiler_params=_params(("parallel", "parallel")))(a, b3)


def matmul_tn(a, b, *, name, tm=1024, out_blocks=None):
    m, k = a.shape
    n = b.shape[1]
    nb = n // (out_blocks or 1)
    tm, tk, tn = min(tm, m), _pick(k), _pick(nb, 768 if out_blocks else 1536)
    per = nb // tn

    def body(a_ref, b_ref, o_ref):
        @pl.when(pl.program_id(2) == 0)
        def _():
            o_ref[...] = jnp.zeros_like(o_ref)

        o_ref[...] += _bd(a_ref[...], b_ref[...], 0, 0)

    if out_blocks:
        out_spec = pl.BlockSpec((None, tk, tn), lambda i, j, t: (j // per, i, j % per))
        out_shape = jax.ShapeDtypeStruct((out_blocks, k, nb), F32)
    else:
        out_spec = pl.BlockSpec((tk, tn), lambda i, j, t: (i, j))
        out_shape = jax.ShapeDtypeStruct((k, n), F32)
    return pl.pallas_call(
        body, name=name, grid=(k // tk, n // tn, m // tm),
        in_specs=[pl.BlockSpec((tm, tk), lambda i, j, t: (t, i)), pl.BlockSpec((tm, tn), lambda i, j, t: (t, j))],
        out_specs=out_spec, out_shape=out_shape,
        compiler_params=_params(("parallel", "parallel", "arbitrary")))(a, b)


def _row_spec(r, tm):
    if isinstance(r, tuple):
        arr, width, blk = r
        return arr, pl.BlockSpec((tm, width), lambda i, blk=blk: (i, blk))
    return r, pl.BlockSpec((tm, r.shape[1]), lambda i: (i, 0))


def _full_spec(p):
    return pl.BlockSpec(p.shape, lambda i: (0,) * p.ndim)


def _expand_rows(rows, tm):
    arrays, specs, counts, widths = [], [], [], []
    for r in rows:
        parts = [_row_spec(p, tm) for p in (r if isinstance(r, list) else [r])]
        arrays += [a for a, _ in parts]
        specs += [s for _, s in parts]
        counts.append(len(parts))
        widths.append(sum(s.block_shape[1] for _, s in parts))
    return arrays, specs, counts, widths


def _row_values(refs, counts):
    vals, k = [], 0
    for c in counts:
        parts = [refs[k + j][...] for j in range(c)]
        vals.append(parts[0] if c == 1 else jnp.concatenate(parts, axis=1))
        k += c
    return vals


def _rows_of(rows):
    first = rows[0][0] if isinstance(rows[0], list) else rows[0]
    return (first[0] if isinstance(first, tuple) else first).shape[0]


def rowmap_fwd(fn, rows, params, outs, *, name, tm=256):
    m = _rows_of(rows)
    tm = min(tm, m)
    arrays, specs, counts, _ = _expand_rows(rows, tm)
    nin, npar = len(arrays), len(params)

    def body(*refs):
        res = fn(*_row_values(refs[:nin], counts), *[r[...] for r in refs[nin:nin + npar]])
        for o_ref, v in zip(refs[nin + npar:], res):
            o_ref[...] = v.astype(o_ref.dtype)

    return pl.pallas_call(
        body, name=name, grid=(m // tm,), in_specs=specs + [_full_spec(p) for p in params],
        out_specs=[pl.BlockSpec((tm, c), lambda i: (i, 0)) for c, _ in outs],
        out_shape=[jax.ShapeDtypeStruct((m, c), dt) for c, dt in outs],
        compiler_params=_params(("parallel",)))(*arrays, *params)


def rowmap_bwd(fn, rows, params, cts, *, name, row_dtypes=None, add=None, tm=256):
    m = _rows_of(rows)
    tm = min(tm, m)
    arrays, specs, counts, widths = _expand_rows(rows, tm)
    cp = [_row_spec(c, tm) for c in cts]
    nin, nr, npar, nc = len(arrays), len(rows), len(params), len(cts)
    row_dtypes = row_dtypes or [F32] * nr

    def body(*refs):
        ins = _row_values(refs[:nin], counts) + [r[...] for r in refs[nin:nin + npar]]
        ins = [v.astype(F32) for v in ins]
        ct = tuple(r[...].astype(F32) for r in refs[nin + npar:nin + npar + nc])
        base = nin + npar + nc
        extra = None
        if add is not None:
            extra = refs[base][...]
            base += 1
        _, pull = jax.vjp(fn, *ins)
        grads = pull(ct)
        for j in range(nr):
            g = grads[j]
            if j == 0 and extra is not None:
                g = g + extra
            refs[base + j][...] = g.astype(refs[base + j].dtype)

        @pl.when(pl.program_id(0) == 0)
        def _():
            for j in range(npar):
                refs[base + nr + j][...] = jnp.zeros_like(refs[base + nr + j])

        for j in range(npar):
            refs[base + nr + j][...] += grads[nr + j]

    in_specs = specs + [_full_spec(p) for p in params] + [s for _, s in cp]
    args = arrays + list(params) + [a for a, _ in cp]
    if add is not None:
        in_specs.append(pl.BlockSpec((tm, widths[0]), lambda i: (i, 0)))
        args.append(add)
    out_specs = [pl.BlockSpec((tm, w), lambda i: (i, 0)) for w in widths] + [_full_spec(p) for p in params]
    out_shape = [jax.ShapeDtypeStruct((m, w), dt) for w, dt in zip(widths, row_dtypes)]
    out_shape += [jax.ShapeDtypeStruct(p.shape, F32) for p in params]
    return pl.pallas_call(
        body, name=name, grid=(m // tm,), in_specs=in_specs, out_specs=out_specs, out_shape=out_shape,
        compiler_params=_params(("arbitrary",)))(*args)


def loss_head(h, target, w, *, name, tm=256, matmul=None):
    m, d = h.shape
    tm = min(tm, m)

    def body(h_ref, t_ref, w_ref, *refs):
        loss_ref, dh_ref, dw_ref = refs[-3:]
        rows = h_ref[...]
        if matmul is not None:
            rows = rows + _bd(refs[0][...], refs[1][...], 1, 0)
        y, pull = jax.vjp(_rms, rows, w_ref[...])
        err = y - t_ref[...]
        dh, dw = pull(err * (1.0 / d))

        @pl.when(pl.program_id(0) == 0)
        def _():
            loss_ref[...] = jnp.zeros_like(loss_ref)
            dw_ref[...] = jnp.zeros_like(dw_ref)

        loss_ref[...] += (0.5 / d) * jnp.sum(err * err, keepdims=True)
        dw_ref[...] += dw
        dh_ref[...] = dh

    row = pl.BlockSpec((tm, d), lambda i: (i, 0))
    in_specs, args = [row, row, _full_spec(w)], [h, target, w]
    if matmul is not None:
        in_specs += [pl.BlockSpec((tm, matmul[0].shape[1]), lambda i: (i, 0)), _full_spec(matmul[1])]
        args += list(matmul)
    return pl.pallas_call(
        body, name=name, grid=(m // tm,), in_specs=in_specs,
        out_specs=[pl.BlockSpec((1, 1), lambda i: (0, 0)), row, _full_spec(w)],
        out_shape=[jax.ShapeDtypeStruct((1, 1), F32), jax.ShapeDtypeStruct((m, d), F32),
                   jax.ShapeDtypeStruct(w.shape, F32)],
        compiler_params=_params(("arbitrary",)))(*args)


def _shift(x, s):
    if s == 0:
        return x
    n = x.shape[0]
    t = lax.broadcasted_iota(jnp.int32, x.shape, 0)
    rolled = pltpu.roll(x, (-s) % n, 0)
    return jnp.where((t + s >= 0) & (t + s < n), rolled, 0.0)


def _conv(x, w, b):
    k = w.shape[0]
    acc = b + w[k // 2:k // 2 + 1, :] * x
    for j in range(k):
        if j != k // 2:
            acc = acc + w[j:j + 1, :] * _shift(x, j - k // 2)
    return acc


def _conv_bwd(x, dc, w):
    k = w.shape[0]
    dx = None
    dws = []
    for j in range(k):
        s = j - k // 2
        term = w[j:j + 1, :] * _shift(dc, -s)
        dx = term if dx is None else dx + term
        dws.append(jnp.sum(dc * _shift(x, s), axis=0, keepdims=True))
    return dx, jnp.concatenate(dws, axis=0), jnp.sum(dc, axis=0, keepdims=True)


def _dsilu(c):
    s = jax.nn.sigmoid(c)
    return s * (1.0 + c * (1.0 - s))


def ssd_conv_fwd(xbc, w, b, *, bsz, name):
    t, c = xbc.shape
    seq, ct = t // bsz, 256

    def body(x_ref, w_ref, b_ref, o_ref):
        o_ref[...] = _silu(_conv(x_ref[...], w_ref[...], b_ref[...]))

    return pl.pallas_call(
        body, name=name, grid=(c // ct, bsz),
        in_specs=[pl.BlockSpec((seq, ct), lambda j, i: (i, j)), pl.BlockSpec((w.shape[0], ct), lambda j, i: (0, j)),
                  pl.BlockSpec((1, ct), lambda j, i: (0, j))],
        out_specs=pl.BlockSpec((seq, ct), lambda j, i: (i, j)),
        out_shape=jax.ShapeDtypeStruct((t, c), F32),
        compiler_params=_params(("parallel", "parallel")))(xbc, w, b)


def ssd_conv_bwd(xbc, dparts, w, b, *, bsz, name):
    t, c = xbc.shape
    seq, ct, k = t // bsz, 256, w.shape[0]
    starts = [0]
    for p in dparts:
        starts.append(starts[-1] + p.shape[1] // ct)

    def body(x_ref, *refs):
        g_refs, (w_ref, b_ref, dx_ref, dw_ref, db_ref) = refs[:len(dparts)], refs[len(dparts):]
        j = pl.program_id(0)

        @pl.when(pl.program_id(1) == 0)
        def _():
            dw_ref[...] = jnp.zeros_like(dw_ref)
            db_ref[...] = jnp.zeros_like(db_ref)

        def run(g_ref):
            x, wv = x_ref[...], w_ref[...]
            dc = g_ref[...] * _dsilu(_conv(x, wv, b_ref[...]))
            dx, dw, db = _conv_bwd(x, dc, wv)
            dx_ref[...] = dx
            dw_ref[...] += dw
            db_ref[...] += db

        for n, g_ref in enumerate(g_refs):
            pl.when((j >= starts[n]) & (j < starts[n + 1]))(functools.partial(run, g_ref))

    def part_spec(n):
        lo, hi = starts[n], starts[n + 1]

        def index(j, i):
            inside = (j >= lo) & (j < hi)
            return jnp.where(inside, i, 0), jnp.where(inside, j - lo, 0)

        return pl.BlockSpec((seq, ct), index)

    blk = pl.BlockSpec((seq, ct), lambda j, i: (i, j))
    wspec, bspec = pl.BlockSpec((k, ct), lambda j, i: (0, j)), pl.BlockSpec((1, ct), lambda j, i: (0, j))
    return pl.pallas_call(
        body, name=name, grid=(c // ct, bsz),
        in_specs=[blk] + [part_spec(n) for n in range(len(dparts))] + [wspec, bspec], out_specs=[blk, wspec, bspec],
        out_shape=[jax.ShapeDtypeStruct((t, c), F32), jax.ShapeDtypeStruct((k, c), F32),
                   jax.ShapeDtypeStruct((1, c), F32)],
        compiler_params=_params(("parallel", "arbitrary")))(xbc, *dparts, w, b)


def _ffn_specs(seq, ct, k, nblk):
    val = pl.BlockSpec((seq, ct), lambda j, i: (i, j))
    gate = pl.BlockSpec((seq, ct), lambda j, i: (i, nblk + j))
    wv, wg = pl.BlockSpec((k, ct), lambda j, i: (0, j)), pl.BlockSpec((k, ct), lambda j, i: (0, nblk + j))
    bv, bg = pl.BlockSpec((1, ct), lambda j, i: (0, j)), pl.BlockSpec((1, ct), lambda j, i: (0, nblk + j))
    return val, gate, wv, wg, bv, bg


def ffn_act_fwd(up, w, b, *, bsz, name):
    t = up.shape[0]
    half = up.shape[1] // 2
    seq, ct, k = t // bsz, 256, w.shape[0]
    val, gate, wv, wg, bv, bg = _ffn_specs(seq, ct, k, half // ct)

    def body(v_ref, g_ref, wv_ref, wg_ref, bv_ref, bg_ref, o_ref):
        vc = _conv(v_ref[...].astype(F32), wv_ref[...], bv_ref[...])
        gc = _conv(g_ref[...].astype(F32), wg_ref[...], bg_ref[...])
        o_ref[...] = (_silu(gc) * vc).astype(BF16)

    return pl.pallas_call(
        body, name=name, grid=(half // ct, bsz), in_specs=[val, gate, wv, wg, bv, bg], out_specs=val,
        out_shape=jax.ShapeDtypeStruct((t, half), BF16),
        compiler_params=_params(("parallel", "parallel")))(up, up, w, w, b, b)


def ffn_act_bwd(up, dact, w, b, *, bsz, name):
    t = up.shape[0]
    half = up.shape[1] // 2
    seq, ct, k = t // bsz, 256, w.shape[0]
    val, gate, wv, wg, bv, bg = _ffn_specs(seq, ct, k, half // ct)

    def body(v_ref, g_ref, wv_ref, wg_ref, bv_ref, bg_ref, d_ref, dv_ref, dg_ref, dwv_ref, dwg_ref, dbv_ref, dbg_ref):
        v, g = v_ref[...].astype(F32), g_ref[...].astype(F32)
        vc = _conv(v, wv_ref[...], bv_ref[...])
        gc = _conv(g, wg_ref[...], bg_ref[...])
        d = d_ref[...].astype(F32)
        sg = jax.nn.sigmoid(gc)
        dv, dwv, dbv = _conv_bwd(v, d * (gc * sg), wv_ref[...])
        dg, dwg, dbg = _conv_bwd(g, d * vc * (sg * (1.0 + gc * (1.0 - sg))), wg_ref[...])
        dv_ref[...] = dv.astype(BF16)
        dg_ref[...] = dg.astype(BF16)

        @pl.when(pl.program_id(1) == 0)
        def _():
            for r in (dwv_ref, dwg_ref, dbv_ref, dbg_ref):
                r[...] = jnp.zeros_like(r)

        dwv_ref[...] += dwv
        dwg_ref[...] += dwg
        dbv_ref[...] += dbv
        dbg_ref[...] += dbg

    return pl.pallas_call(
        body, name=name, grid=(half // ct, bsz), in_specs=[val, gate, wv, wg, bv, bg, val],
        out_specs=[val, val, wv, wv, bv, bv],
        out_shape=[jax.ShapeDtypeStruct((t, half), BF16), jax.ShapeDtypeStruct((t, half), BF16),
                   jax.ShapeDtypeStruct((k, half), F32), jax.ShapeDtypeStruct((k, half), F32),
                   jax.ShapeDtypeStruct((1, half), F32), jax.ShapeDtypeStruct((1, half), F32)],
        compiler_params=_params(("parallel", "arbitrary")))(up, up, w, w, b, b, dact)


def _sel_row(a, h):
    oh = (lax.broadcasted_iota(jnp.int32, (a.shape[0], 1), 0) == h).astype(F32)
    return jnp.sum(a * oh, axis=0, keepdims=True)


def _ssd_chunk(xp, dtr, bm, cm, prev, bias_r, alog_r, dskip_r, rev):
    q = dtr.shape[1]
    ri = lax.broadcasted_iota(jnp.int32, (q, q), 0)
    ci = lax.broadcasted_iota(jnp.int32, (q, q), 1)
    mask = (ci >= ri) if rev else (ci <= ri)
    lane_lo, row_lo = ci < HDIM, ri < HDIM
    dt_r = _softplus(dtr + bias_r)
    dta_r = dt_r * (-jnp.exp(alog_r))
    cs_r = cum_row(dta_r, rev)
    scores = dot_nt(cm, bm)

    def per_row(v):
        return jnp.broadcast_to(v, (q, q)).T

    assert len(xp) == 2
    y_diag, csqs, decayed, tots = [], [], [], []
    for p in range(2):
        ha = 2 * p + (HPG if rev else 0)
        hb = ha + 1
        cs_a, cs_b = _sel_row(cs_r, ha), _sel_row(cs_r, hb)
        csq_a, csq_b = per_row(cs_a), per_row(cs_b)
        seg_a = jnp.exp(jnp.where(mask, csq_a - cs_a, -1e30))
        seg_b = jnp.exp(jnp.where(mask, csq_b - cs_b, -1e30))
        csq = jnp.where(lane_lo, csq_a, csq_b)
        xdt = xp[p] * jnp.where(lane_lo, per_row(_sel_row(dt_r, ha)), per_row(_sel_row(dt_r, hb)))
        tot_a = jnp.sum(_sel_row(dta_r, ha), axis=1, keepdims=True)
        tot_b = jnp.sum(_sel_row(dta_r, hb), axis=1, keepdims=True)
        y_diag.append(jnp.where(lane_lo, *dot2_nn(scores * seg_a, scores * seg_b, xdt)))
        csqs.append(csq)
        decayed.append(xdt * jnp.exp(jnp.where(lane_lo, tot_a, tot_b) - csq))
        tots.append((tot_a, tot_b, ha, hb))
    y_off = dot_nt2(cm, *prev)
    states = dot_tn2(*decayed, bm)
    ys, news = [], []
    for p, (tot_a, tot_b, ha, hb) in enumerate(tots):
        y = y_diag[p] + y_off[p] * jnp.exp(csqs[p])
        if not rev:
            y = y + jnp.where(lane_lo, _sel_row(dskip_r, ha), _sel_row(dskip_r, hb)) * xp[p]
        ys.append(y)
        news.append(jnp.exp(jnp.where(row_lo, tot_a, tot_b)) * prev[p] + states[p])
    return tuple(ys), tuple(news)


NPAIR = HPG // 2


def _ssd_specs(seq, nc):
    xs = pl.BlockSpec((None, seq, HPG * HDIM), lambda b, g: (b, 0, g))
    bm = pl.BlockSpec((None, seq, NSTATE), lambda b, g: (b, 0, SSD_W // NSTATE + g))
    cm = pl.BlockSpec((None, seq, NSTATE), lambda b, g: (b, 0, SSD_W // NSTATE + SGROUPS + g))
    dtr = pl.BlockSpec((None, None, 2 * HPG, seq), lambda b, g: (b, g, 0, 0))
    pr = pl.BlockSpec((None, 2 * HPG, 1), lambda b, g: (g, 0, 0))
    st = pl.BlockSpec((None, None, 2, nc, NPAIR, 2 * HDIM, NSTATE), lambda b, g: (b, g, 0, 0, 0, 0, 0))
    return xs, bm, cm, dtr, pr, st


def _pair_cols(p):
    return slice(2 * HDIM * p, 2 * HDIM * (p + 1))


def ssd_scan_fwd(act, dtr, prs, *, name):
    bsz, seq, _ = act.shape
    nc = seq // QC
    xs, bm, cm, dtrs, pr, st = _ssd_specs(seq, nc)

    def body(x_ref, b_ref, c_ref, dtr_ref, br_ref, ar_ref, dk_ref, y_ref, st_ref):
        par = (br_ref[...], ar_ref[...], dk_ref[...])
        y_ref[...] = jnp.zeros_like(y_ref)

        def step(i, carry):
            new = []
            for rev in (False, True):
                k = (nc - 1 - i) if rev else i
                rows = pl.ds(pl.multiple_of(k * QC, QC), QC)
                xp = tuple(x_ref[rows, _pair_cols(p)] for p in range(NPAIR))
                for p in range(NPAIR):
                    st_ref[int(rev), k, p] = carry[rev][p]
                ys, nw = _ssd_chunk(xp, dtr_ref[:, rows], b_ref[rows, :], c_ref[rows, :], carry[rev], *par, rev)
                for p in range(NPAIR):
                    y_ref[rows, _pair_cols(p)] += ys[p]
                new.append(nw)
            return tuple(new)

        zero = tuple(jnp.zeros((2 * HDIM, NSTATE), F32) for _ in range(NPAIR))
        lax.fori_loop(0, nc // 2, lambda i, c: step(2 * i + 1, step(2 * i, c)), (zero, zero))

    return pl.pallas_call(
        body, name=name, grid=(bsz, SGROUPS), in_specs=[xs, bm, cm, dtrs, pr, pr, pr], out_specs=[xs, st],
        out_shape=[jax.ShapeDtypeStruct((bsz, seq, SSD_W), F32),
                   jax.ShapeDtypeStruct((bsz, SGROUPS, 2, nc, NPAIR, 2 * HDIM, NSTATE), F32)],
        compiler_params=_params(("parallel", "parallel")))(act, act, act, dtr, *prs)


def ssd_scan_bwd(act, dtr, prs, states, dy, *, name):
    bsz, seq, _ = act.shape
    nc = seq // QC
    xs, bm, cm, dtrs, pr, st = _ssd_specs(seq, nc)
    grp = pl.BlockSpec((None, seq, NSTATE), lambda b, g: (b, 0, g))
    dpr = pl.BlockSpec((None, None, 2 * HPG, 1), lambda b, g: (b, g, 0, 0))

    def body(x_ref, b_ref, c_ref, dtr_ref, br_ref, ar_ref, dk_ref, st_ref, dy_ref,
             dx_ref, db_ref, dc_ref, ddtr_ref, gbr_ref, gar_ref, gdk_ref):
        par = (br_ref[...], ar_ref[...], dk_ref[...])
        pgrads = (gbr_ref, gar_ref, gdk_ref)
        for r in pgrads + (dx_ref, db_ref, dc_ref, ddtr_ref):
            r[...] = jnp.zeros_like(r)

        def bstep(i, dcarry):
            new = []
            for rev in (False, True):
                k = i if rev else (nc - 1 - i)
                rows = pl.ds(pl.multiple_of(k * QC, QC), QC)
                xp = tuple(x_ref[rows, _pair_cols(p)] for p in range(NPAIR))
                prev = tuple(st_ref[int(rev), k, p] for p in range(NPAIR))
                _, pull = jax.vjp(functools.partial(_ssd_chunk, rev=rev), xp, dtr_ref[:, rows], b_ref[rows, :],
                                  c_ref[rows, :], prev, *par)
                dyp = tuple(dy_ref[rows, _pair_cols(p)] for p in range(NPAIR))
                gx, gdt, gb, gc, gprev, *gpar = pull((dyp, dcarry[rev]))
                for p in range(NPAIR):
                    dx_ref[rows, _pair_cols(p)] += gx[p]
                ddtr_ref[:, rows] += gdt
                db_ref[rows, :] += gb
                dc_ref[rows, :] += gc
                for r, g in zip(pgrads, gpar):
                    r[...] += g
                new.append(gprev)
            return tuple(new)

        zero = tuple(jnp.zeros((2 * HDIM, NSTATE), F32) for _ in range(NPAIR))
        lax.fori_loop(0, nc, bstep, (zero, zero))

    out_shape = [jax.ShapeDtypeStruct((bsz, seq, SSD_W), F32),
                 jax.ShapeDtypeStruct((bsz, seq, SGROUPS * NSTATE), F32),
                 jax.ShapeDtypeStruct((bsz, seq, SGROUPS * NSTATE), F32),
                 jax.ShapeDtypeStruct(dtr.shape, F32)]
    out_shape += [jax.ShapeDtypeStruct((bsz, SGROUPS, 2 * HPG, 1), F32)] * 3
    return pl.pallas_call(
        body, name=name, grid=(bsz, SGROUPS), in_specs=[xs, bm, cm, dtrs, pr, pr, pr, st, xs],
        out_specs=[xs, grp, grp, dtrs, dpr, dpr, dpr], out_shape=out_shape,
        compiler_params=_params(("parallel", "parallel")))(act, act, act, dtr, *prs, states, dy)


def _s5_core(lam_re, lam_im, log_step, b_re, b_im, c_re, c_im):
    q = S5_Q
    step = jnp.exp(log_step)[:, None]
    lr, li = lam_re * step, lam_im * step
    mag = jnp.exp(lr)
    ar, ai = mag * jnp.cos(li), mag * jnp.sin(li)
    den = lam_re * lam_re + lam_im * lam_im
    cr = ((ar - 1.0) * lam_re + ai * lam_im) / den
    ci = (ai * lam_re - (ar - 1.0) * lam_im) / den
    bbr = cr[..., None] * b_re - ci[..., None] * b_im
    bbi = cr[..., None] * b_im + ci[..., None] * b_re
    d = jnp.arange(q + 1, dtype=F32)[None, :, None]
    pm = jnp.exp(d * lr[:, None, :])
    pr, pi = pm * jnp.cos(d * li[:, None, :]), pm * jnp.sin(d * li[:, None, :])
    er = pr[..., None] * bbr[:, None] - pi[..., None] * bbi[:, None]
    ei = pr[..., None] * bbi[:, None] + pi[..., None] * bbr[:, None]
    hp = lax.Precision.HIGHEST
    k = (jnp.einsum('gcp,gdpz->gdcz', c_re, er[:, :q], precision=hp)
         - jnp.einsum('gcp,gdpz->gdcz', c_im, ei[:, :q], precision=hp))
    e = jnp.concatenate([er[:, :q], ei[:, :q]], axis=2)
    p1r, p1i = pr[:, 1:], pi[:, 1:]
    m_re = c_re[:, None] * p1r[:, :, None, :] - c_im[:, None] * p1i[:, :, None, :]
    m_im = -c_re[:, None] * p1i[:, :, None, :] - c_im[:, None] * p1r[:, :, None, :]
    da = jnp.concatenate([pr[:, q], pr[:, q]], axis=-1)
    db = jnp.concatenate([-pi[:, q], pi[:, q]], axis=-1)
    return k, e, jnp.concatenate([m_re, m_im], axis=-1), da, db


def _s5_operators(lf_re, lf_im, lsf, lb_re, lb_im, lsb, b_re, b_im, cf_re, cf_im, cb_re, cb_im):
    g = lf_re.shape[0]
    both = lambda f, b: jnp.concatenate([f, b], axis=0)
    k, e, m, da, db = _s5_core(both(lf_re, lb_re), both(lf_im, lb_im), both(lsf, lsb), both(b_re, b_re),
                               both(b_im, b_im), both(cf_re, cb_re), both(cf_im, cb_im))
    kf, kb = k[:g], k[g:]
    wtf, wtb = jnp.transpose(e[:g, ::-1], (0, 1, 3, 2)), jnp.transpose(e[g:], (0, 1, 3, 2))
    mtf, mtb = jnp.transpose(m[:g], (0, 3, 1, 2)), jnp.transpose(m[g:, ::-1], (0, 3, 1, 2))
    daf, dab, dbf, dbb = da[:g], da[g:], db[:g], db[g:]
    lags = jnp.concatenate([kb[:, :0:-1], kf[:, :1] + kb[:, :1], kf[:, 1:]], axis=1)
    tt = jnp.transpose(lags, (0, 1, 3, 2))
    wt = jnp.concatenate([wtf.reshape(g, S5_QC, 2 * S5_P), wtb.reshape(g, S5_QC, 2 * S5_P)], axis=-1)
    mt = jnp.concatenate([mtf.reshape(g, 2 * S5_P, S5_QC), mtb.reshape(g, 2 * S5_P, S5_QC)], axis=1)
    return tt, wt, mt, jnp.concatenate([daf, dab], -1), jnp.concatenate([dbf, dbb], -1)


def _gspec(*shape):
    return pl.BlockSpec((None,) + shape, lambda g: (g,) + (0,) * len(shape))


S5_HALVES = S5_QC // LANES


def _toeplitz_block(s, t):
    per = LANES // S5_C
    return t // per, slice(s * S5_C, (s + 1) * S5_C), slice((t % per) * S5_C, (t % per + 1) * S5_C)


def s5_toeplitz(kt, *, name):
    g = kt.shape[0]

    def body(k_ref, t_ref):
        for s in range(S5_Q):
            for t in range(S5_Q):
                t_ref[_toeplitz_block(s, t)] = k_ref[t - s + S5_Q - 1]

    return pl.pallas_call(
        body, name=name, grid=(g,), in_specs=[_gspec(2 * S5_Q - 1, S5_C, S5_C)],
        out_specs=_gspec(S5_HALVES, S5_QC, LANES), out_shape=jax.ShapeDtypeStruct((g, S5_HALVES, S5_QC, LANES), F32),
        compiler_params=_params(("parallel",)))(kt)


def s5_toeplitz_bwd(dtt, *, name):
    g = dtt.shape[0]

    def body(d_ref, k_ref):
        for j in range(2 * S5_Q - 1):
            acc = None
            for s in range(S5_Q):
                t = j - (S5_Q - 1) + s
                if 0 <= t < S5_Q:
                    blk = d_ref[_toeplitz_block(s, t)]
                    acc = blk if acc is None else acc + blk
            k_ref[j] = acc

    return pl.pallas_call(
        body, name=name, grid=(g,), in_specs=[_gspec(S5_HALVES, S5_QC, LANES)],
        out_specs=_gspec(2 * S5_Q - 1, S5_C, S5_C), out_shape=jax.ShapeDtypeStruct((g, 2 * S5_Q - 1, S5_C, S5_C), F32),
        compiler_params=_params(("parallel",)))(dtt)


S5_RT = 128


def _chunk_piece(q):
    per = LANES // S5_C
    return q // per, slice((q % per) * S5_C, (q % per + 1) * S5_C)


def to_chunks(u, *, name):
    t = u.shape[0]
    r = t // S5_Q
    rt = min(S5_RT, r)

    per = LANES // S5_C
    nblk = S5_W // LANES

    def body(*refs):
        o_ref = refs[-1]
        for k in range(nblk):
            for q in range(S5_Q):
                rows = refs[k][pl.ds(q, rt, stride=S5_Q), :]
                half, lanes = _chunk_piece(q)
                for j in range(per):
                    o_ref[k * per + j, half, :, lanes] = rows[:, j * S5_C:(j + 1) * S5_C]

    return pl.pallas_call(
        body, name=name, grid=(r // rt,),
        in_specs=[pl.BlockSpec((rt * S5_Q, LANES), lambda i, k=k: (i, k)) for k in range(nblk)],
        out_specs=pl.BlockSpec((S5_G, S5_HALVES, rt, LANES), lambda i: (0, 0, i, 0)),
        out_shape=jax.ShapeDtypeStruct((S5_G, S5_HALVES, r, LANES), F32),
        compiler_params=_params(("parallel",)))(*[u] * nblk)


def from_chunks(y, *, name, add=None, as_blocks=False):
    r = y.shape[2]
    rt = min(S5_RT, r)
    per = LANES // S5_C

    nblk = S5_W // LANES

    def body(*refs):
        y_ref, tmp_ref = refs[0], refs[-1]
        adds, outs = refs[1:-1 - nblk], refs[-1 - nblk:-1]
        for k in range(nblk):
            for q in range(S5_Q):
                half, lanes = _chunk_piece(q)
                for j in range(per):
                    tmp_ref[:, j * S5_C:(j + 1) * S5_C] = y_ref[k * per + j, half, :, lanes]
                row = tmp_ref[...]
                if add is not None:
                    row = row + adds[k][pl.ds(q, rt, stride=S5_Q), :]
                outs[k][pl.ds(q, rt, stride=S5_Q), :] = row

    in_specs = [pl.BlockSpec((S5_G, S5_HALVES, rt, LANES), lambda i: (0, 0, i, 0))]
    if add is not None:
        in_specs += [pl.BlockSpec((rt * S5_Q, LANES), lambda i, k=k: (i, k)) for k in range(nblk)]
    blocks = pl.pallas_call(
        body, name=name, grid=(r // rt,), in_specs=in_specs,
        out_specs=[pl.BlockSpec((rt * S5_Q, LANES), lambda i: (i, 0))] * nblk,
        out_shape=[jax.ShapeDtypeStruct((r * S5_Q, LANES), F32)] * nblk,
        scratch_shapes=[pltpu.VMEM((rt, LANES), F32)],
        compiler_params=_params(("parallel",)))(*([y] if add is None else [y] + [add] * nblk))
    return list(blocks) if as_blocks else jnp.concatenate(blocks, axis=1)


def _cat(ref):
    return jnp.concatenate([ref[h] for h in range(S5_HALVES)], axis=1)


def _put(ref, v):
    for h in range(S5_HALVES):
        ref[h] = v[:, h * LANES:(h + 1) * LANES]


def _mspec(gp, *shape):
    return pl.BlockSpec((gp,) + shape, lambda i: (i,) + (0,) * len(shape))


def _carry_spec(nck):
    return pl.BlockSpec((nck, 8, 4 * S5_P), lambda i: (0, i, 0))


def _carry_rows(ref, gl, bsz):
    return jnp.concatenate([ref[:, gl * bsz + b, :] for b in range(bsz)], axis=0)


def _carry_put(ref, gl, bsz, v):
    nck = v.shape[0] // bsz
    for b in range(bsz):
        ref[:, gl * bsz + b, :] = v[b * nck:(b + 1) * nck, :]


def s5_state_in(u, wt, *, bsz, name):
    g, _, r, _ = u.shape
    gp, nck = 8 // bsz, r // bsz

    def body(u_ref, w_ref, o_ref):
        for gl in range(gp):
            _carry_put(o_ref, gl, bsz, _bd(_cat(u_ref.at[gl]), w_ref[gl], 1, 0))

    return pl.pallas_call(
        body, name=name, grid=(g // gp,), in_specs=[_mspec(gp, S5_HALVES, r, LANES), _mspec(gp, S5_QC, 4 * S5_P)],
        out_specs=_carry_spec(nck), out_shape=jax.ShapeDtypeStruct((nck, g * bsz, 4 * S5_P), F32),
        compiler_params=_params(("parallel",)))(u, wt)


def _swap(h):
    return pltpu.roll(h, S5_P, 1)


def s5_carry_fwd(s, da, db, *, name):
    nck, rows, _ = s.shape
    w = 2 * S5_P

    def body(s_ref, da_ref, db_ref, h_ref):
        dirs = ((False, slice(0, w)), (True, slice(w, 2 * w)))
        coef = [(da_ref[:, cols], db_ref[:, cols]) for _, cols in dirs]

        def step(i, hs):
            new = []
            for (rev, cols), (a, b), h in zip(dirs, coef, hs):
                k = (nck - 1 - i) if rev else i
                h_ref[k, :, cols] = h
                new.append(a * h + b * _swap(h) + s_ref[k, :, cols])
            return tuple(new)

        z = jnp.zeros((rows, w), F32)
        lax.fori_loop(0, nck, step, (z, z), unroll=2)

    rt = min(2 * CARRY_ROWS, rows)
    big, small = pl.BlockSpec((nck, rt, 2 * w), lambda i: (0, i, 0)), pl.BlockSpec((rt, 2 * w), lambda i: (i, 0))
    rows = rt
    return pl.pallas_call(
        body, name=name, grid=(s.shape[1] // rt,), in_specs=[big, small, small], out_specs=big,
        out_shape=jax.ShapeDtypeStruct(s.shape, F32), compiler_params=_params(("parallel",)))(s, da, db)


def s5_carry_bwd(hin, dh, da, db, *, name):
    nck, rows, _ = hin.shape
    w = 2 * S5_P

    def body(h_ref, dh_ref, da_ref, db_ref, ds_ref, gda_ref, gdb_ref):
        dirs = ((False, slice(0, w)), (True, slice(w, 2 * w)))
        coef = [(da_ref[:, cols], db_ref[:, cols]) for _, cols in dirs]

        def step(i, carries):
            new = []
            for (rev, cols), (a, b), (g, ga, gb) in zip(dirs, coef, carries):
                k = i if rev else (nck - 1 - i)
                ds_ref[k, :, cols] = g
                h = h_ref[k, :, cols]
                new.append((dh_ref[k, :, cols] + a * g + _swap(b * g), ga + g * h, gb + g * _swap(h)))
            return tuple(new)

        z = jnp.zeros((rows, w), F32)
        res = lax.fori_loop(0, nck, step, ((z, z, z), (z, z, z)), unroll=2)
        for (_, cols), (_, ga, gb) in zip(dirs, res):
            gda_ref[:, cols] = ga
            gdb_ref[:, cols] = gb

    rt = min(CARRY_ROWS, rows)
    big, small = pl.BlockSpec((nck, rt, 2 * w), lambda i: (0, i, 0)), pl.BlockSpec((rt, 2 * w), lambda i: (i, 0))
    rows = rt
    return pl.pallas_call(
        body, name=name, grid=(hin.shape[1] // rt,), in_specs=[big, big, small, small], out_specs=[big, small, small],
        out_shape=[jax.ShapeDtypeStruct(hin.shape, F32), jax.ShapeDtypeStruct(da.shape, F32),
                   jax.ShapeDtypeStruct(da.shape, F32)],
        compiler_params=_params(("parallel",)))(hin, dh, da, db)


def s5_out(u, hin, tt, mt, *, bsz, name):
    g, _, r, _ = u.shape
    gp, nck = 8 // bsz, r // bsz

    def body(u_ref, h_ref, t_ref, m_ref, o_ref):
        for gl in range(gp):
            u_v, h_v = _cat(u_ref.at[gl]), _carry_rows(h_ref, gl, bsz)
            for half in range(S5_HALVES):
                cols = slice(half * LANES, (half + 1) * LANES)
                o_ref[gl, half] = _bd(u_v, t_ref[gl, half], 1, 0) + _bd(h_v, m_ref[gl, :, cols], 1, 0)

    cspec = _mspec(gp, S5_HALVES, r, LANES)
    return pl.pallas_call(
        body, name=name, grid=(g // gp,),
        in_specs=[cspec, _carry_spec(nck), _mspec(gp, S5_HALVES, S5_QC, LANES), _mspec(gp, 4 * S5_P, S5_QC)],
        out_specs=cspec, out_shape=jax.ShapeDtypeStruct((g, S5_HALVES, r, LANES), F32),
        compiler_params=_params(("parallel",)))(u, hin, tt, mt)


def s5_out_bwd(dy, u, hin, tt, mt, *, bsz, name):
    g, _, r, _ = u.shape
    gp, nck = 8 // bsz, r // bsz

    def body(dy_ref, u_ref, h_ref, t_ref, m_ref, dh_ref, dt_ref, dm_ref, du_ref):
        for gl in range(gp):
            dy_v, u_v = _cat(dy_ref.at[gl]), _cat(u_ref.at[gl])
            _carry_put(dh_ref, gl, bsz, _bd(dy_v, m_ref[gl], 1, 1))
            dm_ref[gl] = _bd(_carry_rows(h_ref, gl, bsz), dy_v, 0, 0)
            du = None
            for half in range(S5_HALVES):
                dy_h = dy_ref[gl, half]
                dt_ref[gl, half] = _bd(u_v, dy_h, 0, 0)
                part = _bd(dy_h, t_ref[gl, half], 1, 1)
                du = part if du is None else du + part
            _put(du_ref.at[gl], du)

    cspec, tspec = _mspec(gp, S5_HALVES, r, LANES), _mspec(gp, S5_HALVES, S5_QC, LANES)
    mspec = _mspec(gp, 4 * S5_P, S5_QC)
    return pl.pallas_call(
        body, name=name, grid=(g // gp,),
        in_specs=[cspec, cspec, _carry_spec(nck), tspec, mspec],
        out_specs=[_carry_spec(nck), tspec, mspec, cspec],
        out_shape=[jax.ShapeDtypeStruct((nck, g * bsz, 4 * S5_P), F32),
                   jax.ShapeDtypeStruct((g, S5_HALVES, S5_QC, LANES), F32),
                   jax.ShapeDtypeStruct((g, 4 * S5_P, S5_QC), F32), jax.ShapeDtypeStruct((g, S5_HALVES, r, LANES), F32)],
        compiler_params=_params(("parallel",)))(dy, u, hin, tt, mt)


def s5_state_in_bwd(ds, u, wt, du1, *, bsz, name):
    g, _, r, _ = u.shape
    gp, nck = 8 // bsz, r // bsz

    def body(ds_ref, u_ref, w_ref, du1_ref, du_ref, dw_ref):
        for gl in range(gp):
            ds_v = _carry_rows(ds_ref, gl, bsz)
            _put(du_ref.at[gl], _cat(du1_ref.at[gl]) + _bd(ds_v, w_ref[gl], 1, 1))
            dw_ref[gl] = _bd(_cat(u_ref.at[gl]), ds_v, 0, 0)

    cspec, wspec = _mspec(gp, S5_HALVES, r, LANES), _mspec(gp, S5_QC, 4 * S5_P)
    return pl.pallas_call(
        body, name=name, grid=(g // gp,),
        in_specs=[_carry_spec(nck), cspec, wspec, cspec], out_specs=[cspec, wspec],
        out_shape=[jax.ShapeDtypeStruct((g, S5_HALVES, r, LANES), F32), jax.ShapeDtypeStruct((g, S5_QC, 4 * S5_P), F32)],
        compiler_params=_params(("parallel",)))(ds, u, wt, du1)


def _s5_post(ypre, u, dvec, wv, wg, bv, bg, nw):
    g = _gelu(ypre + dvec * u)
    out = (dot_nn(g, wv) + bv) * jax.nn.sigmoid(dot_nn(g, wg) + bg)
    return (_rms(out, nw),)


def _ssd_post(y, z, nw):
    return (_rms(y * _silu(z), nw),)


def _block_diag(w):
    eye = jnp.eye(S5_G, dtype=w.dtype)
    return jnp.einsum('gcd,gh->gchd', w, eye).reshape(S5_W, S5_W)


def _diag_blocks(w):
    v = w.reshape(S5_G, S5_C, S5_G, S5_C)
    return v[jnp.arange(S5_G), :, jnp.arange(S5_G), :]


def _dt_rows(dt, bsz):
    seq = dt.shape[0] // bsz
    return jnp.transpose(dt.reshape(bsz, seq, 2, SGROUPS, HPG), (0, 3, 2, 4, 1)).reshape(bsz, SGROUPS, 2 * HPG, seq)


def _dt_from_rows(dr):
    bsz, _, _, seq = dr.shape
    return jnp.transpose(dr.reshape(bsz, SGROUPS, 2, HPG, seq), (0, 4, 2, 1, 3)).reshape(bsz * seq, 2 * HEADS)


def _head_params(f, b):
    return jnp.concatenate([f.reshape(SGROUPS, HPG), b.reshape(SGROUPS, HPG)], axis=1)[:, :, None]


def _head_grads(gr):
    v = gr.sum(0)[:, :, 0]
    return v[:, :HPG].reshape(HEADS), v[:, HPG:].reshape(HEADS)


def local_step(x, target, w):
    bsz, seq, d = x.shape
    t = bsz * seq
    x2, tgt2 = x.reshape(t, d), target.reshape(t, d)
    g = {}
    row = lambda v: v.reshape(1, -1)
    bf = lambda v: v.astype(BF16)

    w_in = _unshard(bf(w['w_in']), SHARDED['w_in'])
    cuts = [0, SSD_W, SSD_W + XBC_W, SSD_W + XBC_W + 2 * HEADS, w_in.shape[1]]
    w_in_parts = [w_in[:, a:b] for a, b in zip(cuts[:-1], cuts[1:])]
    norm_mix = row(w['norm_mix_w']) + w.get('token', 0.0)
    (hn,) = rowmap_fwd(lambda a, nw: (_rms(a, nw),), [x2], [norm_mix], [(d, BF16)], tm=512, name="rms_mix")
    z, xbc, dt, u = matmul_multi(hn, w_in_parts, name="in_proj")

    conv_w, conv_b = _unshard(w['ssd_conv_w'], SHARDED['ssd_conv_w']), row(w['ssd_conv_b'])
    act = ssd_conv_fwd(xbc, conv_w, conv_b, bsz=bsz, name="ssd_conv")
    dtr = _dt_rows(dt, bsz)
    prs = (_head_params(w['ssd_dt_bias_fwd'], w['ssd_dt_bias_bwd']),
           _head_params(w['ssd_a_log_fwd'], w['ssd_a_log_bwd']),
           _head_params(w['ssd_d'], jnp.zeros_like(w['ssd_d'])))
    act3 = act.reshape(bsz, seq, XBC_W)
    y_scan, ssd_states = ssd_scan_fwd(act3, dtr, prs, name="ssd_scan")
    y_scan = y_scan.reshape(t, SSD_W)
    ssd_nw = row(w['ssd_norm_w'])
    (y_ssd,) = rowmap_fwd(_ssd_post, [y_scan, z], [ssd_nw], [(SSD_W, BF16)], tm=512, name="ssd_post")

    s5_names = ['s5_lambda_re_fwd', 's5_lambda_im_fwd', 's5_log_step_fwd', 's5_lambda_re_bwd', 's5_lambda_im_bwd',
                's5_log_step_bwd', 's5_b_re', 's5_b_im', 's5_c_re_fwd', 's5_c_im_fwd', 's5_c_re_bwd', 's5_c_im_bwd']
    (kt, wt, mt, da, db), s5_pull = jax.vjp(_s5_operators, *[w[n] for n in s5_names])
    tt_b, wt_b, mt_b = s5_toeplitz(kt, name="s5_toeplitz"), bf(wt), bf(mt)
    da_r, db_r = jnp.repeat(da, bsz, axis=0), jnp.repeat(db, bsz, axis=0)
    uc = to_chunks(u, name="s5_to_chunks_u")
    hin = s5_carry_fwd(s5_state_in(uc, wt_b, bsz=bsz, name="s5_state_in"), da_r, db_r, name="s5_carry")
    ypre = from_chunks(s5_out(uc, hin, tt_b, mt_b, bsz=bsz, name="s5_out"), name="s5_from_chunks_y", as_blocks=True)
    glu_w = w['s5_glu_w']
    s5_par = [row(w['s5_d']), _block_diag(glu_w[:, :, :S5_C]), _block_diag(glu_w[:, :, S5_C:]),
              row(w['s5_glu_b'][:, :S5_C]), row(w['s5_glu_b'][:, S5_C:]), row(w['s5_norm_w'])]
    (y_s5,) = rowmap_fwd(_s5_post, [ypre, u], s5_par, [(S5_W, BF16)], tm=512, name="s5_post")

    if 'late' in w:
        w = {**w, **w['late'](y_s5)}
    w_out = bf(w['w_out']).reshape(SSD_W + S5_W, d)
    norm_ffn = row(w['norm_ffn_w'])
    h1, hn2 = matmul_sum([y_ssd, y_s5], [w_out[:SSD_W], w_out[SSD_W:]], add=x2, norm_w=norm_ffn, name="out_proj")
    pad_c = FFN_PAD - FFN_BLK
    half = N_DEV // 2
    w_up3 = jnp.pad(bf(w['ffn_w_up']), ((0, 0), (0, 0), (0, pad_c)))
    w_down = jnp.pad(bf(w['ffn_w_down']).reshape(half, FFN_BLK, d), ((0, 0), (0, pad_c), (0, 0)))
    w_down = w_down.reshape(half * FFN_PAD, d)
    fconv_w = jnp.pad(w['ffn_conv_w'], ((0, 0), (0, 0), (0, pad_c)))
    fconv_w = jnp.transpose(fconv_w, (1, 0, 2)).reshape(FCONV, N_DEV * FFN_PAD)
    fconv_b = row(jnp.pad(w['ffn_conv_b'].reshape(N_DEV, FFN_BLK), ((0, 0), (0, pad_c))))
    up = matmul_cols(hn2, w_up3, out_dtype=BF16, name="ffn_up")
    fact = ffn_act_fwd(up, fconv_w, fconv_b, bsz=bsz, name="ffn_act")
    loss, dh2, g_nf = loss_head(h1, tgt2, row(w['norm_final_w']), matmul=(fact, w_down), tm=512, name="ffn_down_loss")
    g['norm_final_w'] = g_nf.reshape(-1)

    dfact = matmul_sum([dh2], [w_down], nt=True, tm=1024, name="ffn_down_dx")
    g_down = matmul_tn(fact, dh2, name="ffn_down_dw").reshape(half, FFN_PAD, d)[:, :FFN_BLK]
    g['ffn_w_down'] = g_down.reshape(N_DEV, FFN_BLK // 2, d)
    dval, dgate, dwv, dwg, dbv, dbg = ffn_act_bwd(up, dfact, fconv_w, fconv_b, bsz=bsz, name="ffn_act_bwd")
    g_cw = jnp.concatenate([dwv, dwg], axis=1).reshape(FCONV, N_DEV, FFN_PAD)[:, :, :FFN_BLK]
    g['ffn_conv_w'] = jnp.transpose(g_cw, (1, 0, 2))
    g['ffn_conv_b'] = jnp.concatenate([dbv, dbg], axis=1).reshape(N_DEV, FFN_PAD)[:, :FFN_BLK].reshape(-1)
    windows = [(dval, FFN_PAD, p) for p in range(half)] + [(dgate, FFN_PAD, p) for p in range(half)]
    g['ffn_w_up'] = jnp.concatenate([matmul_tn(hn2, dval, out_blocks=half, name="ffn_up_dw_val"),
                                     matmul_tn(hn2, dgate, out_blocks=half, name="ffn_up_dw_gate")],
                                    axis=0)[:, :, :FFN_BLK]
    send_early = w.get('on_grads')
    if send_early:
        norm_ffn = norm_ffn + send_early(g, ['ffn_w_up', 'ffn_w_down'])
    dh1, g_nffn = matmul_sum(windows, [(w_up3, p) for p in range(N_DEV)], nt=True, tm=512,
                             norm_bwd=(h1, norm_ffn, dh2), name="ffn_up_dx")
    g['norm_ffn_w'] = g_nffn.reshape(-1)

    dycat = matmul_sum([dh1], [w_out], nt=True, tm=1024, name="out_proj_dx")
    g['w_out'] = jnp.concatenate([matmul_tn(y_ssd, dh1, name="out_proj_dw_ssd"),
                                  matmul_tn(y_s5, dh1, name="out_proj_dw_s5")], axis=0).reshape(w['w_out'].shape)
    if send_early:
        ssd_nw = ssd_nw + send_early(g, ['w_out'])
    dy_scan, dz, g_snw = rowmap_bwd(_ssd_post, [y_scan, z], [ssd_nw], [(dycat, SSD_W, 0)], tm=512,
                                    name="ssd_post_bwd")
    g['ssd_norm_w'] = g_snw.reshape(-1)
    dypre, du_a, g_d, g_wv, g_wg, g_bv, g_bg, g_s5nw = rowmap_bwd(
        _s5_post, [ypre, u], s5_par, [(dycat, S5_W, SSD_W // S5_W)], tm=512, name="s5_post_bwd")
    g['s5_d'], g['s5_norm_w'] = g_d.reshape(-1), g_s5nw.reshape(-1)
    g['s5_glu_w'] = jnp.concatenate([_diag_blocks(g_wv), _diag_blocks(g_wg)], axis=-1)
    g['s5_glu_b'] = jnp.concatenate([g_bv.reshape(S5_G, S5_C), g_bg.reshape(S5_G, S5_C)], axis=-1)

    dyc = to_chunks(dypre, name="s5_to_chunks_dy")
    dhin, dtt, dmt, du1 = s5_out_bwd(dyc, uc, hin, tt_b, mt_b, bsz=bsz, name="s5_out_bwd")
    ds, gda, gdb = s5_carry_bwd(hin, dhin, da_r, db_r, name="s5_carry_bwd")
    duc, dwt = s5_state_in_bwd(ds, uc, wt_b, du1, bsz=bsz, name="s5_state_in_bwd")
    du = from_chunks(duc, add=du_a, name="s5_from_chunks_du")
    fold = lambda v: v.reshape(S5_G, bsz, -1).sum(1)
    dkt = s5_toeplitz_bwd(dtt, name="s5_toeplitz_bwd")
    for n, gv in zip(s5_names, s5_pull((dkt, dwt, dmt, fold(gda), fold(gdb)))):
        g[n] = gv

    dxs, dbm, dcm, ddtr, gbr, gar, gdk = ssd_scan_bwd(
        act3, dtr, prs, ssd_states, dy_scan.reshape(bsz, seq, SSD_W), name="ssd_scan_bwd")
    g['ssd_dt_bias_fwd'], g['ssd_dt_bias_bwd'] = _head_grads(gbr)
    g['ssd_a_log_fwd'], g['ssd_a_log_bwd'] = _head_grads(gar)
    g['ssd_d'] = _head_grads(gdk)[0]
    dparts_act = [v.reshape(t, v.shape[-1]) for v in (dxs, dbm, dcm)]
    dxbc, g_cw, g_cb = ssd_conv_bwd(xbc, dparts_act, conv_w, conv_b, bsz=bsz, name="ssd_conv_bwd")
    g['ssd_conv_w'] = _shard_rows(g_cw, SHARDED['ssd_conv_w']).reshape(w['ssd_conv_w'].shape)
    g['ssd_conv_b'] = g_cb.reshape(-1)
    ddt = _dt_from_rows(ddtr)

    if send_early:
        ddt = ddt + send_early(g, [], loss=loss)
    dparts = [dz, dxbc, ddt, du]
    g_in = jnp.concatenate([matmul_tn(hn, dp, name=f"in_proj_dw_{i}") for i, dp in enumerate(dparts)], axis=1)
    g['w_in'] = _shard_rows(g_in, SHARDED['w_in']).reshape(w['w_in'].shape)
    if send_early:
        dparts[2] = ddt + send_early(g, ['w_in'])
    dx, g_nmix = matmul_sum(dparts, w_in_parts, nt=True, tm=512, norm_bwd=(x2, norm_mix, dh1), name="in_proj_dx")
    g['norm_mix_w'] = g_nmix.reshape(-1)
    return loss, dx.reshape(bsz, seq, d), g


ANY = pl.BlockSpec(memory_space=pl.ANY)


def all_gather(shards, *, name):
    n = len(shards)

    def body(*refs):
        x_refs, out_refs = refs[:n], refs[n:2 * n]
        send_sems, recv_sems, local_sems = refs[2 * n:]
        x, y, c = lax.axis_index("x"), lax.axis_index("y"), lax.axis_index("c")
        me, sibling = (x, y, c), (x, y, 1 - c)
        chips = [(1 - x, y), (x, 1 - y), (1 - x, 1 - y)]

        def copy(k, j, block, to, own=False):
            dst = out_refs[j].at[4 * block[0] + 2 * block[1] + block[2]]
            return pltpu.make_async_remote_copy(
                src_ref=x_refs[j] if own else dst, dst_ref=dst,
                send_sem=send_sems.at[k, j], recv_sem=recv_sems.at[k, j], device_id=to, device_id_type=MESH)

        mine = [pltpu.make_async_copy(x_refs[j], out_refs[j].at[4 * x + 2 * y + c], local_sems.at[j]) for j in range(n)]
        first = [copy(0, j, me, sibling, own=True) for j in range(n)]
        first += [copy(1 + i, j, me, (*chip, c), own=True) for i, chip in enumerate(chips) for j in range(n)]
        for cp in mine + first:
            cp.start()
        passed = []
        for i, chip in enumerate(chips):
            for j in range(n):
                copy(1 + i, j, (*chip, c), me).wait_recv()
                passed.append(copy(4 + i, j, (*chip, c), sibling))
                passed[-1].start()
        for j in range(n):
            copy(0, j, sibling, me).wait_recv()
        for i, chip in enumerate(chips):
            for j in range(n):
                copy(4 + i, j, (*chip, 1 - c), me).wait_recv()
        for cp in first + passed:
            cp.wait_send()
        for cp in mine:
            cp.wait()

    return pl.pallas_call(
        body, name=name, out_shape=[jax.ShapeDtypeStruct((N_DEV,) + s.shape, s.dtype) for s in shards],
        in_specs=[ANY] * n, out_specs=[ANY] * n,
        scratch_shapes=[pltpu.SemaphoreType.DMA((7, n)), pltpu.SemaphoreType.DMA((7, n)),
                        pltpu.SemaphoreType.DMA((n,))],
    )(*shards)


HBM_SPEC = pl.BlockSpec(memory_space=pltpu.HBM)
SEM_SPEC = pl.BlockSpec(memory_space=pltpu.SEMAPHORE)
SPLIT_PARAMS = pltpu.CompilerParams(has_side_effects=pltpu.SideEffectType.DATAFLOW_SIDE_EFFECTING)


def _peer_copies(src_refs, land_refs, send_sems, recv_sems, indexed):
    x, y, c = lax.axis_index("x"), lax.axis_index("y"), lax.axis_index("c")
    me = 4 * x + 2 * y + c
    copies = []
    for k in range(1, N_DEV):
        px = (1 - x) if k & 4 else x
        py = (1 - y) if k & 2 else y
        pc = (1 - c) if k & 1 else c
        for j, (src, land) in enumerate(zip(src_refs, land_refs)):
            sem = (k - 1) * len(src_refs) + j
            copies.append(pltpu.make_async_remote_copy(
                src_ref=src.at[4 * px + 2 * py + pc] if indexed else src, dst_ref=land.at[me],
                send_sem=send_sems.at[sem], recv_sem=recv_sems.at[sem],
                device_id=(px, py, pc), device_id_type=MESH))
    return copies


def scatter_start(srcs, *, name, indexed):
    n = len(srcs)
    lands = [lax.empty(s.shape if indexed else (N_DEV,) + s.shape, s.dtype) for s in srcs]

    def body(*refs):
        send_sems, recv_sems = refs[2 * n], refs[2 * n + 1]
        for cp in _peer_copies(refs[:n], refs[n:2 * n], send_sems, recv_sems, indexed):
            cp.start()
        refs[-1][...] = jnp.zeros_like(refs[-1])

    hbm = lambda a: pltpu.HBM(a.shape, a.dtype)
    sems = pltpu.SemaphoreType.DMA(((N_DEV - 1) * n,))
    res = pl.pallas_call(
        body, name=name,
        out_shape=(sems, sems, *[hbm(a) for a in srcs + lands], jax.ShapeDtypeStruct((8, LANES), F32)),
        in_specs=[HBM_SPEC] * (2 * n),
        out_specs=(SEM_SPEC, SEM_SPEC, *[HBM_SPEC] * (2 * n), pl.BlockSpec(memory_space=pltpu.VMEM)),
        input_output_aliases={i: 2 + i for i in range(2 * n)}, compiler_params=SPLIT_PARAMS,
    )(*[pltpu.with_memory_space_constraint(a, pltpu.HBM) for a in srcs + lands])
    return res[0], res[1], list(res[2:2 + n]), list(res[2 + n:2 + 2 * n]), res[-1]


def scatter_wait(send_sems, recv_sems, srcs, lands, after, *, name, indexed):
    n = len(srcs)

    def body(*refs):
        for cp in _peer_copies(refs[:n], refs[n:2 * n], refs[2 * n], refs[2 * n + 1], indexed):
            cp.wait_send()
            cp.wait_recv()

    hbm = lambda a: pltpu.HBM(a.shape, a.dtype)
    res = pl.pallas_call(
        body, name=name, out_shape=tuple(hbm(a) for a in srcs + lands),
        in_specs=[HBM_SPEC] * (2 * n) + [SEM_SPEC, SEM_SPEC, ANY], out_specs=tuple([HBM_SPEC] * (2 * n)),
        input_output_aliases={i: i for i in range(2 * n)}, compiler_params=SPLIT_PARAMS,
    )(*srcs, *lands, send_sems, recv_sems, after)
    return list(res[:n]), list(res[n:])


def _adam_rows(r, c):
    fits = [t for t in range(8, r + 1, 8) if r % t == 0 and N_DEV * t * c * 4 <= 6 * 2 ** 20]
    return max(fits) if fits else r


def adamw(recv, w, m, v, *, name):
    _, r, n = recv.shape
    tr = _adam_rows(r, n)

    def body(r_ref, w_ref, m_ref, v_ref, g_ref, d_ref, nm_ref, nv_ref):
        g = r_ref[0].astype(F32)
        for s in range(1, N_DEV):
            g = g + r_ref[s].astype(F32)
        m_new = ADAM_B1 * m_ref[...] + (1.0 - ADAM_B1) * g
        v_new = ADAM_B2 * v_ref[...] + (1.0 - ADAM_B2) * jnp.square(g)
        m_hat = m_new / (1.0 - ADAM_B1 ** ADAM_STEP)
        v_hat = v_new / (1.0 - ADAM_B2 ** ADAM_STEP)
        g_ref[...] = g
        d_ref[...] = -ADAM_LR * (m_hat / (jnp.sqrt(v_hat) + ADAM_EPS) + ADAM_WD * w_ref[...])
        nm_ref[...] = m_new
        nv_ref[...] = v_new

    blk = pl.BlockSpec((tr, n), lambda i: (i, 0))
    return pl.pallas_call(
        body, name=name, grid=(r // tr,), in_specs=[pl.BlockSpec((N_DEV, tr, n), lambda i: (0, i, 0)), blk, blk, blk],
        out_specs=[blk] * 4, out_shape=[jax.ShapeDtypeStruct((r, n), F32)] * 4,
        compiler_params=_params(("parallel",)))(recv, w, m, v)


def _shard_rows(full, axis):
    if axis == 0:
        return full.reshape(N_DEV, -1)
    r, c = full.shape
    return jnp.transpose(full.reshape(r, N_DEV, c // N_DEV), (1, 0, 2)).reshape(N_DEV, -1)


def _unshard(blocks, axis):
    if axis == 0:
        return blocks.reshape(-1, blocks.shape[-1])
    return jnp.transpose(blocks, (1, 0, 2)).reshape(blocks.shape[1], -1)


def kernel(x, norm_mix_w, w_in, ssd_conv_w, ssd_conv_b, ssd_dt_bias_fwd, ssd_dt_bias_bwd, ssd_a_log_fwd, ssd_a_log_bwd, ssd_d, ssd_norm_w, s5_lambda_re_fwd, s5_lambda_im_fwd, s5_log_step_fwd, s5_lambda_re_bwd, s5_lambda_im_bwd, s5_log_step_bwd, s5_b_re, s5_b_im, s5_c_re_fwd, s5_c_im_fwd, s5_c_re_bwd, s5_c_im_bwd, s5_d, s5_glu_w, s5_glu_b, s5_norm_w, w_out, norm_ffn_w, ffn_w_up, ffn_conv_w, ffn_conv_b, ffn_w_down, norm_final_w, loss_target, m_norm_mix_w, m_w_in, m_ssd_conv_w, m_ssd_conv_b, m_ssd_dt_bias_fwd, m_ssd_dt_bias_bwd, m_ssd_a_log_fwd, m_ssd_a_log_bwd, m_ssd_d, m_ssd_norm_w, m_s5_lambda_re_fwd, m_s5_lambda_im_fwd, m_s5_log_step_fwd, m_s5_lambda_re_bwd, m_s5_lambda_im_bwd, m_s5_log_step_bwd, m_s5_b_re, m_s5_b_im, m_s5_c_re_fwd, m_s5_c_im_fwd, m_s5_c_re_bwd, m_s5_c_im_bwd, m_s5_d, m_s5_glu_w, m_s5_glu_b, m_s5_norm_w, m_w_out, m_norm_ffn_w, m_ffn_w_up, m_ffn_conv_w, m_ffn_conv_b, m_ffn_w_down, m_norm_final_w, v_norm_mix_w, v_w_in, v_ssd_conv_w, v_ssd_conv_b, v_ssd_dt_bias_fwd, v_ssd_dt_bias_bwd, v_ssd_a_log_fwd, v_ssd_a_log_bwd, v_ssd_d, v_ssd_norm_w, v_s5_lambda_re_fwd, v_s5_lambda_im_fwd, v_s5_log_step_fwd, v_s5_lambda_re_bwd, v_s5_lambda_im_bwd, v_s5_log_step_bwd, v_s5_b_re, v_s5_b_im, v_s5_c_re_fwd, v_s5_c_im_fwd, v_s5_c_re_bwd, v_s5_c_im_bwd, v_s5_d, v_s5_glu_w, v_s5_glu_b, v_s5_norm_w, v_w_out, v_norm_ffn_w, v_ffn_w_up, v_ffn_conv_w, v_ffn_conv_b, v_ffn_w_down, v_norm_final_w):
    args = dict(locals())
    strip = lambda n, v: v if n == 'norm_final_w' else v[0]
    w = {n: strip(n, args[n]) for n in WEIGHTS}

    mats = ['w_in', 'w_out', 'ffn_w_up', 'ffn_w_down']
    convs = ['ssd_conv_w', 'ffn_conv_w']
    shard = lambda n: w[n].astype(BF16) if n in mats else w[n]
    early, late = ['w_in', 'ssd_conv_w'], ['w_out', 'ffn_w_up', 'ffn_w_down', 'ffn_conv_w']
    full = dict(w)
    full.update(zip(early, all_gather([shard(n) for n in early], name="weight_all_gather")))
    ssem, rsem, src_thru, land_thru, token = scatter_start([shard(n) for n in late], name="weight_gather_start",
                                                           indexed=False)
    me = 4 * lax.axis_index("x") + 2 * lax.axis_index("y") + lax.axis_index("c")

    def late_weights(after):
        own, landed = scatter_wait(ssem, rsem, src_thru, land_thru, after, name="weight_gather_wait", indexed=False)
        return {n: lax.dynamic_update_index_in_dim(l, o, me, 0) for n, o, l in zip(late, own, landed)}

    full['late'], full['token'] = late_weights, token[:1, :1]

    pending = []
    last = 'norm_mix_w'
    small = convs + [n for n in WEIGHTS if n not in SHARDED and n != last]
    slot = {n: -(-w[n].size // (8 * LANES)) * 8 for n in small}
    used = sum(slot.values()) + 8
    nrow = -(-used // PACK_ROWS) * PACK_ROWS

    def tiles(v, n):
        return jnp.pad(v, ((0, 0), (0, slot[n] * LANES - v.shape[1]))).reshape(v.shape[0], slot[n], LANES)

    def send_early(grads, names, loss=None):
        srcs = [grads[n].astype(BF16) for n in names]
        if loss is not None:
            pieces = [tiles(grads[n].reshape(N_DEV, -1), n) if n in SHARDED else
                      jnp.broadcast_to(tiles(grads[n].reshape(1, -1), n), (N_DEV, slot[n], LANES)) for n in small]
            pieces.append(jnp.broadcast_to(jnp.pad(loss.reshape(1, 1, 1), ((0, 0), (0, 7), (0, LANES - 1))),
                                           (N_DEV, 8, LANES)))
            pieces.append(jnp.zeros((N_DEV, nrow - used, LANES), F32))
            srcs.append(jnp.concatenate(pieces, axis=1))
            names = names + ['small']
        started = scatter_start(srcs, name="grad_start_" + names[0], indexed=True)
        pending.append((names,) + started[:4])
        return started[4][:1, :1]

    full['on_grads'] = send_early
    loss, grad_x, g = local_step(x, loss_target, full)

    last_send = jnp.broadcast_to(g[last].reshape(1, -1, LANES), (N_DEV, g[last].size // LANES, LANES))
    last_started = scatter_start([last_send], name="grad_start_" + last, indexed=True)
    recv, outs = {}, [{}, {}, {}, {}]

    def arrived(names, started, after):
        own, landed = scatter_wait(*started, after, name="grad_wait_" + names[0], indexed=True)
        for n, o, l in zip(names, own, landed):
            recv[n] = lax.dynamic_update_index_in_dim(l, lax.dynamic_index_in_dim(o, me, 0, keepdims=False), me, 0)

    def update(n):
        shape = recv[n].shape[1:]
        res = adamw(recv[n], *[strip(n, args[p + n]).reshape(shape) for p in ('', 'm_', 'v_')], name="adamw_" + n)
        for o, p in zip(outs, res):
            o[n] = p.reshape(args[n].shape)

    for names, *started in pending:
        arrived(names, started, last_started[4])
    for n in mats:
        update(n)

    def pack(prefix):
        vals = [tiles(strip(n, args[prefix + n]).reshape(1, -1), n)[0] for n in small]
        return jnp.concatenate(vals + [jnp.zeros((nrow - used + 8, LANES), F32)], axis=0)

    packed = adamw(recv['small'], pack(''), pack('m_'), pack('v_'), name="adamw_small")
    arrived([last], last_started[:4], packed[1])
    update(last)
    off = 0
    for n in small:
        for o, p in zip(outs, packed):
            o[n] = p[off:off + slot[n]].reshape(-1)[:w[n].size].reshape(args[n].shape)
        off += slot[n]
    loss_out = packed[0][off, 0].reshape(())
    return (loss_out, grad_x, *[o[n] for o in outs for n in WEIGHTS])
```

```python
import functools

import jax
import jax.numpy as jnp
from jax import lax
from jax.experimental import pallas as pl
from jax.experimental.pallas import tpu as pltpu

F32, BF16 = jnp.float32, jnp.bfloat16
N_DEV = 8
D_MODEL = 1024
SSD_W, HEADS, HDIM, SGROUPS, HPG, NSTATE, SCONV, QC = 1024, 16, 64, 4, 4, 128, 5, 128
XBC_W = SSD_W + 2 * SGROUPS * NSTATE
S5_W, S5_G, S5_C, S5_P, S5_Q = 512, 32, 16, 64, 16
S5_QC = S5_Q * S5_C
CARRY_ROWS = 32
DFF, FCONV = 2816, 3
FFN_BLK, FFN_PAD = 704, 768
EPS = 1e-6
ADAM_LR, ADAM_B1, ADAM_B2, ADAM_EPS, ADAM_WD, ADAM_STEP = 0.001, 0.9, 0.999, 1e-08, 0.01, 10
LANES = 128
MESH = pl.DeviceIdType.MESH

WEIGHTS = ['norm_mix_w', 'w_in', 'ssd_conv_w', 'ssd_conv_b', 'ssd_dt_bias_fwd', 'ssd_dt_bias_bwd', 'ssd_a_log_fwd',
           'ssd_a_log_bwd', 'ssd_d', 'ssd_norm_w', 's5_lambda_re_fwd', 's5_lambda_im_fwd', 's5_log_step_fwd',
           's5_lambda_re_bwd', 's5_lambda_im_bwd', 's5_log_step_bwd', 's5_b_re', 's5_b_im', 's5_c_re_fwd', 's5_c_im_fwd',
           's5_c_re_bwd', 's5_c_im_bwd', 's5_d', 's5_glu_w', 's5_glu_b', 's5_norm_w', 'w_out', 'norm_ffn_w', 'ffn_w_up',
           'ffn_conv_w', 'ffn_conv_b', 'ffn_w_down', 'norm_final_w']
SHARDED = {'w_in': 1, 'ssd_conv_w': 1, 'w_out': 0, 'ffn_w_up': 1, 'ffn_conv_w': 1, 'ffn_w_down': 0}
FULL_SHAPE = {'w_in': (1024, 3616), 'ssd_conv_w': (5, 2048), 'w_out': (1536, 1024), 'ffn_w_up': (1024, 5632),
              'ffn_conv_w': (3, 5632), 'ffn_w_down': (2816, 1024)}
PACK_ROWS = 512


def _pick(n, cap=1536):
    if n <= cap:
        return n
    return max(t for t in range(LANES, cap + 1, LANES) if n % t == 0)


def _params(sem):
    return pltpu.CompilerParams(dimension_semantics=sem)


def _bd(a, b, ca, cb):
    return lax.dot_general(a.astype(BF16), b.astype(BF16), (((ca,), (cb,)), ((), ())), preferred_element_type=F32)


@jax.custom_vjp
def dot_nn(a, b):
    return _bd(a, b, 1, 0)


dot_nn.defvjp(lambda a, b: (_bd(a, b, 1, 0), (a, b)),
              lambda r, g: (_bd(g, r[1], 1, 1).astype(r[0].dtype), _bd(r[0], g, 0, 0).astype(r[1].dtype)))


@jax.custom_vjp
def dot_nt(a, b):
    return _bd(a, b, 1, 1)


dot_nt.defvjp(lambda a, b: (_bd(a, b, 1, 1), (a, b)),
              lambda r, g: (_bd(g, r[1], 1, 0).astype(r[0].dtype), _bd(g, r[0], 0, 0).astype(r[1].dtype)))


@jax.custom_vjp
def dot_tn(a, b):
    return _bd(a, b, 0, 0)


dot_tn.defvjp(lambda a, b: (_bd(a, b, 0, 0), (a, b)),
              lambda r, g: (_bd(r[1], g, 1, 1).astype(r[0].dtype), _bd(r[0], g, 1, 0).astype(r[1].dtype)))


def _rows2(v):
    h = v.shape[0] // 2
    return v[:h], v[h:]


def _cols2(v):
    h = v.shape[1] // 2
    return v[:, :h], v[:, h:]


@jax.custom_vjp
def dot2_nn(la, lb, x):
    return _rows2(_bd(jnp.concatenate([la, lb], axis=0), x, 1, 0))


def _dot2_nn_bwd(res, g):
    la, lb, x = res
    gcat, lcat = jnp.concatenate(g, axis=0), jnp.concatenate([la, lb], axis=0)
    return (*_rows2(_bd(gcat, x, 1, 1)), _bd(lcat, gcat, 0, 0))


dot2_nn.defvjp(lambda la, lb, x: (dot2_nn(la, lb, x), (la, lb, x)), _dot2_nn_bwd)


@jax.custom_vjp
def dot_nt2(c, p0, p1):
    return _cols2(_bd(c, jnp.concatenate([p0, p1], axis=0), 1, 1))


def _dot_nt2_bwd(res, g):
    c, p0, p1 = res
    gcat = jnp.concatenate(g, axis=1)
    return (_bd(gcat, jnp.concatenate([p0, p1], axis=0), 1, 0), *_rows2(_bd(gcat, c, 0, 0)))


dot_nt2.defvjp(lambda c, p0, p1: (dot_nt2(c, p0, p1), (c, p0, p1)), _dot_nt2_bwd)


@jax.custom_vjp
def dot_tn2(a0, a1, b):
    return _rows2(_bd(jnp.concatenate([a0, a1], axis=1), b, 0, 0))


def _dot_tn2_bwd(res, g):
    a0, a1, b = res
    gcat, acat = jnp.concatenate(g, axis=0), jnp.concatenate([a0, a1], axis=1)
    return (*_cols2(_bd(b, gcat, 1, 1)), _bd(acat, gcat, 1, 0))


dot_tn2.defvjp(lambda a0, a1, b: (dot_tn2(a0, a1, b), (a0, a1, b)), _dot_tn2_bwd)


def _split3(x):
    hi = x.astype(BF16)
    r = x - hi.astype(F32)
    mid = r.astype(BF16)
    lo = (r - mid.astype(F32)).astype(BF16)
    return hi, mid, lo


def _cum_matrix(q, upper):
    ri = lax.broadcasted_iota(jnp.int32, (q, q), 0)
    ci = lax.broadcasted_iota(jnp.int32, (q, q), 1)
    return jnp.where((ci >= ri) if upper else (ci <= ri), 1.0, 0.0).astype(BF16)


def _exact_right(x, mat):
    return sum(jnp.dot(p, mat, preferred_element_type=F32) for p in _split3(x))


@functools.partial(jax.custom_vjp, nondiff_argnums=(1,))
def cum_row(x, rev):
    return _exact_right(x, _cum_matrix(x.shape[1], not rev))


cum_row.defvjp(lambda x, rev: (cum_row(x, rev), None),
               lambda rev, _, g: (_exact_right(g, _cum_matrix(g.shape[1], rev)),))


def _softplus(x):
    return jnp.maximum(x, 0.0) + jnp.log(1.0 + jnp.exp(-jnp.abs(x)))


def _silu(x):
    return x * jax.nn.sigmoid(x)


def _gelu(x):
    return 0.5 * x * (1.0 + jnp.tanh(0.7978845608028654 * (x + 0.044715 * (x * x * x))))


def _rms(x, w):
    xf = x.astype(F32)
    return xf * lax.rsqrt(jnp.mean(xf * xf, axis=-1, keepdims=True) + EPS) * w


def matmul_sum(a_list, b_list, *, name, out_dtype=F32, add=None, tm=512, nt=False, norm_w=None, norm_bwd=None):
    a_arrs = [a[0] if isinstance(a, tuple) else a for a in a_list]
    b_arrs = [b[0] if isinstance(b, tuple) else b for b in b_list]
    m, n = a_arrs[0].shape[0], b_arrs[0].shape[-2 if nt else -1]
    tm, tn, k = min(tm, m), _pick(n), len(a_list)
    assert (norm_w is None and norm_bwd is None) or tn == n

    def body(*refs):
        acc = None
        for a_ref, b_ref in zip(refs[:k], refs[k:2 * k]):
            p = _bd(a_ref[...], b_ref[...], 1, 1 if nt else 0)
            acc = p if acc is None else acc + p
        if add is not None:
            acc = acc + refs[2 * k][...]
        if norm_bwd is not None:
            x_ref, w_ref, res_ref, dx_ref, dw_ref = refs[-5:]
            dx, dw = jax.vjp(_rms, x_ref[...], w_ref[...])[1](acc)
            dx_ref[...] = dx + res_ref[...]

            @pl.when(pl.program_id(0) == 0)
            def _():
                dw_ref[...] = jnp.zeros_like(dw_ref)

            dw_ref[...] += dw
        elif norm_w is not None:
            refs[-2][...] = acc.astype(out_dtype)
            refs[-1][...] = _rms(acc, refs[-3][...]).astype(BF16)
        else:
            refs[-1][...] = acc.astype(out_dtype)

    def a_spec(a):
        if isinstance(a, tuple):
            return pl.BlockSpec((tm, a[1]), lambda i, j, blk=a[2]: (i, blk))
        return pl.BlockSpec((tm, a.shape[1]), lambda i, j: (i, 0))

    def b_spec(b):
        arr, p = b if isinstance(b, tuple) else (b, None)
        kk = arr.shape[-1 if nt else -2]
        shape, idx = ((tn, kk), lambda j: (j, 0)) if nt else ((kk, tn), lambda j: (0, j))
        mode = {'pipeline_mode': pl.Buffered(1)} if tn == n else {}
        if p is None:
            return pl.BlockSpec(shape, lambda i, j: idx(j), **mode)
        return pl.BlockSpec((None,) + shape, lambda i, j, p=p: (p,) + idx(j), **mode)

    in_specs = [a_spec(a) for a in a_list] + [b_spec(b) for b in b_list]
    args = a_arrs + b_arrs
    if add is not None:
        in_specs.append(pl.BlockSpec((tm, tn), lambda i, j: (i, j)))
        args.append(add)
    out_spec, out_shape = pl.BlockSpec((tm, tn), lambda i, j: (i, j)), jax.ShapeDtypeStruct((m, n), out_dtype)
    if norm_w is not None:
        in_specs.append(pl.BlockSpec(norm_w.shape, lambda i, j: (0, 0)))
        args.append(norm_w)
        out_spec, out_shape = [out_spec, out_spec], [out_shape, jax.ShapeDtypeStruct((m, n), BF16)]
    sem = ("parallel", "parallel")
    if norm_bwd is not None:
        x, w, res = norm_bwd
        wspec = pl.BlockSpec(w.shape, lambda i, j: (0, 0))
        in_specs += [out_spec, wspec, out_spec]
        args += [x, w, res]
        out_spec, out_shape = [out_spec, wspec], [jax.ShapeDtypeStruct((m, n), F32), jax.ShapeDtypeStruct(w.shape, F32)]
        sem = ("arbitrary", "arbitrary")
    return pl.pallas_call(
        body, name=name, grid=(m // tm, n // tn), in_specs=in_specs, out_specs=out_spec, out_shape=out_shape,
        compiler_params=_params(sem))(*args)


def matmul_multi(a, b_list, *, name, tm=512):
    m, kk = a.shape
    tm, nb = min(tm, m), len(b_list)

    def body(a_ref, *refs):
        a_v = a_ref[...]
        for b_ref, o_ref in zip(refs[:nb], refs[nb:]):
            o_ref[...] = _bd(a_v, b_ref[...], 1, 0)

    return pl.pallas_call(
        body, name=name, grid=(m // tm,),
        in_specs=[pl.BlockSpec((tm, kk), lambda i: (i, 0))] + [_full_spec(b) for b in b_list],
        out_specs=[pl.BlockSpec((tm, b.shape[1]), lambda i: (i, 0)) for b in b_list],
        out_shape=[jax.ShapeDtypeStruct((m, b.shape[1]), F32) for b in b_list],
        compiler_params=_params(("parallel",)))(a, *b_list)


def matmul_cols(a, b3, *, name, out_dtype=F32, tm=2048):
    m, kk = a.shape
    p, _, nb = b3.shape
    tm, tn = min(tm, m), _pick(nb, 768)
    per = nb // tn

    def body(a_ref, b_ref, o_ref):
        o_ref[...] = _bd(a_ref[...], b_ref[...], 1, 0).astype(out_dtype)

    return pl.pallas_call(
        body, name=name, grid=(m // tm, p * per),
        in_specs=[pl.BlockSpec((tm, kk), lambda i, j: (i, 0)),
                  pl.BlockSpec((None, kk, tn), lambda i, j: (j // per, 0, j % per))],
        out_specs=pl.BlockSpec((tm, tn), lambda i, j: (i, j)),
        out_shape=jax.ShapeDtypeStruct((m, p * nb), out_dtype),
        compiler_params=_params(("parallel", "parallel")))(a, b3)


def matmul_tn(a, b, *, name, tm=2048, out_blocks=None):
    m, k = a.shape
    n = b.shape[1]
    nb = n // (out_blocks or 1)
    tm, tk, tn = min(tm, m), _pick(k), _pick(nb, 768 if out_blocks else 1536)
    per = nb // tn

    def body(a_ref, b_ref, o_ref):
        @pl.when(pl.program_id(2) == 0)
        def _():
            o_ref[...] = jnp.zeros_like(o_ref)

        o_ref[...] += _bd(a_ref[...], b_ref[...], 0, 0)

    if out_blocks:
        out_spec = pl.BlockSpec((None, tk, tn), lambda i, j, t: (j // per, i, j % per))
        out_shape = jax.ShapeDtypeStruct((out_blocks, k, nb), F32)
    else:
        out_spec = pl.BlockSpec((tk, tn), lambda i, j, t: (i, j))
        out_shape = jax.ShapeDtypeStruct((k, n), F32)
    return pl.pallas_call(
        body, name=name, grid=(k // tk, n // tn, m // tm),
        in_specs=[pl.BlockSpec((tm, tk), lambda i, j, t: (t, i)), pl.BlockSpec((tm, tn), lambda i, j, t: (t, j))],
        out_specs=out_spec, out_shape=out_shape,
        compiler_params=_params(("parallel", "parallel", "arbitrary")))(a, b)


def _row_spec(r, tm):
    if isinstance(r, tuple):
        arr, width, blk = r
        return arr, pl.BlockSpec((tm, width), lambda i, blk=blk: (i, blk))
    return r, pl.BlockSpec((tm, r.shape[1]), lambda i: (i, 0))


def _full_spec(p):
    return pl.BlockSpec(p.shape, lambda i: (0,) * p.ndim)


def _expand_rows(rows, tm):
    arrays, specs, counts, widths = [], [], [], []
    for r in rows:
        parts = [_row_spec(p, tm) for p in (r if isinstance(r, list) else [r])]
        arrays += [a for a, _ in parts]
        specs += [s for _, s in parts]
        counts.append(len(parts))
        widths.append(sum(s.block_shape[1] for _, s in parts))
    return arrays, specs, counts, widths


def _row_values(refs, counts):
    vals, k = [], 0
    for c in counts:
        parts = [refs[k + j][...] for j in range(c)]
        vals.append(parts[0] if c == 1 else jnp.concatenate(parts, axis=1))
        k += c
    return vals


def _rows_of(rows):
    first = rows[0][0] if isinstance(rows[0], list) else rows[0]
    return (first[0] if isinstance(first, tuple) else first).shape[0]


def rowmap_fwd(fn, rows, params, outs, *, name, tm=256):
    m = _rows_of(rows)
    tm = min(tm, m)
    arrays, specs, counts, _ = _expand_rows(rows, tm)
    nin, npar = len(arrays), len(params)

    def body(*refs):
        res = fn(*_row_values(refs[:nin], counts), *[r[...] for r in refs[nin:nin + npar]])
        for o_ref, v in zip(refs[nin + npar:], res):
            o_ref[...] = v.astype(o_ref.dtype)

    return pl.pallas_call(
        body, name=name, grid=(m // tm,), in_specs=specs + [_full_spec(p) for p in params],
        out_specs=[pl.BlockSpec((tm, c), lambda i: (i, 0)) for c, _ in outs],
        out_shape=[jax.ShapeDtypeStruct((m, c), dt) for c, dt in outs],
        compiler_params=_params(("parallel",)))(*arrays, *params)


def rowmap_bwd(fn, rows, params, cts, *, name, row_dtypes=None, add=None, tm=256):
    m = _rows_of(rows)
    tm = min(tm, m)
    arrays, specs, counts, widths = _expand_rows(rows, tm)
    cp = [_row_spec(c, tm) for c in cts]
    nin, nr, npar, nc = len(arrays), len(rows), len(params), len(cts)
    row_dtypes = row_dtypes or [F32] * nr

    def body(*refs):
        ins = _row_values(refs[:nin], counts) + [r[...] for r in refs[nin:nin + npar]]
        ins = [v.astype(F32) for v in ins]
        ct = tuple(r[...].astype(F32) for r in refs[nin + npar:nin + npar + nc])
        base = nin + npar + nc
        extra = None
        if add is not None:
            extra = refs[base][...]
            base += 1
        _, pull = jax.vjp(fn, *ins)
        grads = pull(ct)
        for j in range(nr):
            g = grads[j]
            if j == 0 and extra is not None:
                g = g + extra
            refs[base + j][...] = g.astype(refs[base + j].dtype)

        @pl.when(pl.program_id(0) == 0)
        def _():
            for j in range(npar):
                refs[base + nr + j][...] = jnp.zeros_like(refs[base + nr + j])

        for j in range(npar):
            refs[base + nr + j][...] += grads[nr + j]

    in_specs = specs + [_full_spec(p) for p in params] + [s for _, s in cp]
    args = arrays + list(params) + [a for a, _ in cp]
    if add is not None:
        in_specs.append(pl.BlockSpec((tm, widths[0]), lambda i: (i, 0)))
        args.append(add)
    out_specs = [pl.BlockSpec((tm, w), lambda i: (i, 0)) for w in widths] + [_full_spec(p) for p in params]
    out_shape = [jax.ShapeDtypeStruct((m, w), dt) for w, dt in zip(widths, row_dtypes)]
    out_shape += [jax.ShapeDtypeStruct(p.shape, F32) for p in params]
    return pl.pallas_call(
        body, name=name, grid=(m // tm,), in_specs=in_specs, out_specs=out_specs, out_shape=out_shape,
        compiler_params=_params(("arbitrary",)))(*args)


def loss_head(h, target, w, *, name, tm=256, matmul=None):
    m, d = h.shape
    tm = min(tm, m)

    def body(h_ref, t_ref, w_ref, *refs):
        loss_ref, dh_ref, dw_ref = refs[-3:]
        rows = h_ref[...]
        if matmul is not None:
            rows = rows + _bd(refs[0][...], refs[1][...], 1, 0)
        y, pull = jax.vjp(_rms, rows, w_ref[...])
        err = y - t_ref[...]
        dh, dw = pull(err * (1.0 / d))

        @pl.when(pl.program_id(0) == 0)
        def _():
            loss_ref[...] = jnp.zeros_like(loss_ref)
            dw_ref[...] = jnp.zeros_like(dw_ref)

        loss_ref[...] += (0.5 / d) * jnp.sum(err * err, keepdims=True)
        dw_ref[...] += dw
        dh_ref[...] = dh

    row = pl.BlockSpec((tm, d), lambda i: (i, 0))
    in_specs, args = [row, row, _full_spec(w)], [h, target, w]
    if matmul is not None:
        in_specs += [pl.BlockSpec((tm, matmul[0].shape[1]), lambda i: (i, 0)), _full_spec(matmul[1])]
        args += list(matmul)
    return pl.pallas_call(
        body, name=name, grid=(m // tm,), in_specs=in_specs,
        out_specs=[pl.BlockSpec((1, 1), lambda i: (0, 0)), row, _full_spec(w)],
        out_shape=[jax.ShapeDtypeStruct((1, 1), F32), jax.ShapeDtypeStruct((m, d), F32),
                   jax.ShapeDtypeStruct(w.shape, F32)],
        compiler_params=_params(("arbitrary",)))(*args)


def _shift(x, s):
    if s == 0:
        return x
    n = x.shape[0]
    t = lax.broadcasted_iota(jnp.int32, x.shape, 0)
    rolled = pltpu.roll(x, (-s) % n, 0)
    return jnp.where((t + s >= 0) & (t + s < n), rolled, 0.0)


def _conv(x, w, b):
    k = w.shape[0]
    acc = b + w[k // 2:k // 2 + 1, :] * x
    for j in range(k):
        if j != k // 2:
            acc = acc + w[j:j + 1, :] * _shift(x, j - k // 2)
    return acc


def _conv_bwd(x, dc, w):
    k = w.shape[0]
    dx = None
    dws = []
    for j in range(k):
        s = j - k // 2
        term = w[j:j + 1, :] * _shift(dc, -s)
        dx = term if dx is None else dx + term
        dws.append(jnp.sum(dc * _shift(x, s), axis=0, keepdims=True))
    return dx, jnp.concatenate(dws, axis=0), jnp.sum(dc, axis=0, keepdims=True)


def _dsilu(c):
    s = jax.nn.sigmoid(c)
    return s * (1.0 + c * (1.0 - s))


def ssd_conv_fwd(xbc, w, b, *, bsz, name):
    t, c = xbc.shape
    seq, ct = t // bsz, 256

    def body(x_ref, w_ref, b_ref, o_ref):
        o_ref[...] = _silu(_conv(x_ref[...], w_ref[...], b_ref[...]))

    return pl.pallas_call(
        body, name=name, grid=(c // ct, bsz),
        in_specs=[pl.BlockSpec((seq, ct), lambda j, i: (i, j)), pl.BlockSpec((w.shape[0], ct), lambda j, i: (0, j)),
                  pl.BlockSpec((1, ct), lambda j, i: (0, j))],
        out_specs=pl.BlockSpec((seq, ct), lambda j, i: (i, j)),
        out_shape=jax.ShapeDtypeStruct((t, c), F32),
        compiler_params=_params(("parallel", "parallel")))(xbc, w, b)


def ssd_conv_bwd(xbc, dparts, w, b, *, bsz, name):
    t, c = xbc.shape
    seq, ct, k = t // bsz, 256, w.shape[0]
    starts = [0]
    for p in dparts:
        starts.append(starts[-1] + p.shape[1] // ct)

    def body(x_ref, *refs):
        g_refs, (w_ref, b_ref, dx_ref, dw_ref, db_ref) = refs[:len(dparts)], refs[len(dparts):]
        j = pl.program_id(0)

        @pl.when(pl.program_id(1) == 0)
        def _():
            dw_ref[...] = jnp.zeros_like(dw_ref)
            db_ref[...] = jnp.zeros_like(db_ref)

        def run(g_ref):
            x, wv = x_ref[...], w_ref[...]
            dc = g_ref[...] * _dsilu(_conv(x, wv, b_ref[...]))
            dx, dw, db = _conv_bwd(x, dc, wv)
            dx_ref[...] = dx
            dw_ref[...] += dw
            db_ref[...] += db

        for n, g_ref in enumerate(g_refs):
            pl.when((j >= starts[n]) & (j < starts[n + 1]))(functools.partial(run, g_ref))

    def part_spec(n):
        lo, hi = starts[n], starts[n + 1]

        def index(j, i):
            inside = (j >= lo) & (j < hi)
            return jnp.where(inside, i, 0), jnp.where(inside, j - lo, 0)

        return pl.BlockSpec((seq, ct), index)

    blk = pl.BlockSpec((seq, ct), lambda j, i: (i, j))
    wspec, bspec = pl.BlockSpec((k, ct), lambda j, i: (0, j)), pl.BlockSpec((1, ct), lambda j, i: (0, j))
    return pl.pallas_call(
        body, name=name, grid=(c // ct, bsz),
        in_specs=[blk] + [part_spec(n) for n in range(len(dparts))] + [wspec, bspec], out_specs=[blk, wspec, bspec],
        out_shape=[jax.ShapeDtypeStruct((t, c), F32), jax.ShapeDtypeStruct((k, c), F32),
                   jax.ShapeDtypeStruct((1, c), F32)],
        compiler_params=_params(("parallel", "arbitrary")))(xbc, *dparts, w, b)


def _ffn_specs(seq, ct, k, nblk):
    val = pl.BlockSpec((seq, ct), lambda j, i: (i, j))
    gate = pl.BlockSpec((seq, ct), lambda j, i: (i, nblk + j))
    wv, wg = pl.BlockSpec((k, ct), lambda j, i: (0, j)), pl.BlockSpec((k, ct), lambda j, i: (0, nblk + j))
    bv, bg = pl.BlockSpec((1, ct), lambda j, i: (0, j)), pl.BlockSpec((1, ct), lambda j, i: (0, nblk + j))
    return val, gate, wv, wg, bv, bg


def ffn_act_fwd(up, w, b, *, bsz, name):
    t = up.shape[0]
    half = up.shape[1] // 2
    seq, ct, k = t // bsz, 256, w.shape[0]
    val, gate, wv, wg, bv, bg = _ffn_specs(seq, ct, k, half // ct)

    def body(v_ref, g_ref, wv_ref, wg_ref, bv_ref, bg_ref, o_ref):
        vc = _conv(v_ref[...].astype(F32), wv_ref[...], bv_ref[...])
        gc = _conv(g_ref[...].astype(F32), wg_ref[...], bg_ref[...])
        o_ref[...] = (_silu(gc) * vc).astype(BF16)

    return pl.pallas_call(
        body, name=name, grid=(half // ct, bsz), in_specs=[val, gate, wv, wg, bv, bg], out_specs=val,
        out_shape=jax.ShapeDtypeStruct((t, half), BF16),
        compiler_params=_params(("parallel", "parallel")))(up, up, w, w, b, b)


def ffn_act_bwd(up, dact, w, b, *, bsz, name):
    t = up.shape[0]
    half = up.shape[1] // 2
    seq, ct, k = t // bsz, 256, w.shape[0]
    val, gate, wv, wg, bv, bg = _ffn_specs(seq, ct, k, half // ct)

    def body(v_ref, g_ref, wv_ref, wg_ref, bv_ref, bg_ref, d_ref, dv_ref, dg_ref, dwv_ref, dwg_ref, dbv_ref, dbg_ref):
        v, g = v_ref[...].astype(F32), g_ref[...].astype(F32)
        vc = _conv(v, wv_ref[...], bv_ref[...])
        gc = _conv(g, wg_ref[...], bg_ref[...])
        d = d_ref[...].astype(F32)
        sg = jax.nn.sigmoid(gc)
        dv, dwv, dbv = _conv_bwd(v, d * (gc * sg), wv_ref[...])
        dg, dwg, dbg = _conv_bwd(g, d * vc * (sg * (1.0 + gc * (1.0 - sg))), wg_ref[...])
        dv_ref[...] = dv.astype(BF16)
        dg_ref[...] = dg.astype(BF16)

        @pl.when(pl.program_id(1) == 0)
        def _():
            for r in (dwv_ref, dwg_ref, dbv_ref, dbg_ref):
                r[...] = jnp.zeros_like(r)

        dwv_ref[...] += dwv
        dwg_ref[...] += dwg
        dbv_ref[...] += dbv
        dbg_ref[...] += dbg

    return pl.pallas_call(
        body, name=name, grid=(half // ct, bsz), in_specs=[val, gate, wv, wg, bv, bg, val],
        out_specs=[val, val, wv, wv, bv, bv],
        out_shape=[jax.ShapeDtypeStruct((t, half), BF16), jax.ShapeDtypeStruct((t, half), BF16),
                   jax.ShapeDtypeStruct((k, half), F32), jax.ShapeDtypeStruct((k, half), F32),
                   jax.ShapeDtypeStruct((1, half), F32), jax.ShapeDtypeStruct((1, half), F32)],
        compiler_params=_params(("parallel", "arbitrary")))(up, up, w, w, b, b, dact)


def _sel_row(a, h):
    oh = (lax.broadcasted_iota(jnp.int32, (a.shape[0], 1), 0) == h).astype(F32)
    return jnp.sum(a * oh, axis=0, keepdims=True)


def _ssd_chunk(xp, dtr, bm, cm, prev, bias_r, alog_r, dskip_r, rev):
    q = dtr.shape[1]
    ri = lax.broadcasted_iota(jnp.int32, (q, q), 0)
    ci = lax.broadcasted_iota(jnp.int32, (q, q), 1)
    mask = (ci >= ri) if rev else (ci <= ri)
    lane_lo, row_lo = ci < HDIM, ri < HDIM
    dt_r = _softplus(dtr + bias_r)
    dta_r = dt_r * (-jnp.exp(alog_r))
    cs_r = cum_row(dta_r, rev)
    scores = dot_nt(cm, bm)

    def per_row(v):
        return jnp.broadcast_to(v, (q, q)).T

    assert len(xp) == 2
    y_diag, csqs, decayed, tots = [], [], [], []
    for p in range(2):
        ha = 2 * p + (HPG if rev else 0)
        hb = ha + 1
        cs_a, cs_b = _sel_row(cs_r, ha), _sel_row(cs_r, hb)
        csq_a, csq_b = per_row(cs_a), per_row(cs_b)
        seg_a = jnp.exp(jnp.where(mask, csq_a - cs_a, -1e30))
        seg_b = jnp.exp(jnp.where(mask, csq_b - cs_b, -1e30))
        csq = jnp.where(lane_lo, csq_a, csq_b)
        xdt = xp[p] * jnp.where(lane_lo, per_row(_sel_row(dt_r, ha)), per_row(_sel_row(dt_r, hb)))
        tot_a = jnp.sum(_sel_row(dta_r, ha), axis=1, keepdims=True)
        tot_b = jnp.sum(_sel_row(dta_r, hb), axis=1, keepdims=True)
        y_diag.append(jnp.where(lane_lo, *dot2_nn(scores * seg_a, scores * seg_b, xdt)))
        csqs.append(csq)
        decayed.append(xdt * jnp.exp(jnp.where(lane_lo, tot_a, tot_b) - csq))
        tots.append((tot_a, tot_b, ha, hb))
    y_off = dot_nt2(cm, *prev)
    states = dot_tn2(*decayed, bm)
    ys, news = [], []
    for p, (tot_a, tot_b, ha, hb) in enumerate(tots):
        y = y_diag[p] + y_off[p] * jnp.exp(csqs[p])
        if not rev:
            y = y + jnp.where(lane_lo, _sel_row(dskip_r, ha), _sel_row(dskip_r, hb)) * xp[p]
        ys.append(y)
        news.append(jnp.exp(jnp.where(row_lo, tot_a, tot_b)) * prev[p] + states[p])
    return tuple(ys), tuple(news)


NPAIR = HPG // 2


def _ssd_specs(seq, nc):
    xs = pl.BlockSpec((None, seq, HPG * HDIM), lambda b, g: (b, 0, g))
    bm = pl.BlockSpec((None, seq, NSTATE), lambda b, g: (b, 0, SSD_W // NSTATE + g))
    cm = pl.BlockSpec((None, seq, NSTATE), lambda b, g: (b, 0, SSD_W // NSTATE + SGROUPS + g))
    dtr = pl.BlockSpec((None, None, 2 * HPG, seq), lambda b, g: (b, g, 0, 0))
    pr = pl.BlockSpec((None, 2 * HPG, 1), lambda b, g: (g, 0, 0))
    st = pl.BlockSpec((None, None, 2, nc, NPAIR, 2 * HDIM, NSTATE), lambda b, g: (b, g, 0, 0, 0, 0, 0))
    return xs, bm, cm, dtr, pr, st


def _pair_cols(p):
    return slice(2 * HDIM * p, 2 * HDIM * (p + 1))


def ssd_scan_fwd(act, dtr, prs, *, name):
    bsz, seq, _ = act.shape
    nc = seq // QC
    xs, bm, cm, dtrs, pr, st = _ssd_specs(seq, nc)

    def body(x_ref, b_ref, c_ref, dtr_ref, br_ref, ar_ref, dk_ref, y_ref, st_ref):
        par = (br_ref[...], ar_ref[...], dk_ref[...])
        y_ref[...] = jnp.zeros_like(y_ref)

        def step(i, carry):
            new = []
            for rev in (False, True):
                k = (nc - 1 - i) if rev else i
                rows = pl.ds(pl.multiple_of(k * QC, QC), QC)
                xp = tuple(x_ref[rows, _pair_cols(p)] for p in range(NPAIR))
                for p in range(NPAIR):
                    st_ref[int(rev), k, p] = carry[rev][p]
                ys, nw = _ssd_chunk(xp, dtr_ref[:, rows], b_ref[rows, :], c_ref[rows, :], carry[rev], *par, rev)
                for p in range(NPAIR):
                    y_ref[rows, _pair_cols(p)] += ys[p]
                new.append(nw)
            return tuple(new)

        zero = tuple(jnp.zeros((2 * HDIM, NSTATE), F32) for _ in range(NPAIR))
        lax.fori_loop(0, nc // 2, lambda i, c: step(2 * i + 1, step(2 * i, c)), (zero, zero))

    return pl.pallas_call(
        body, name=name, grid=(bsz, SGROUPS), in_specs=[xs, bm, cm, dtrs, pr, pr, pr], out_specs=[xs, st],
        out_shape=[jax.ShapeDtypeStruct((bsz, seq, SSD_W), F32),
                   jax.ShapeDtypeStruct((bsz, SGROUPS, 2, nc, NPAIR, 2 * HDIM, NSTATE), F32)],
        compiler_params=_params(("parallel", "parallel")))(act, act, act, dtr, *prs)


def ssd_scan_bwd(act, dtr, prs, states, dy, *, name):
    bsz, seq, _ = act.shape
    nc = seq // QC
    xs, bm, cm, dtrs, pr, st = _ssd_specs(seq, nc)
    grp = pl.BlockSpec((None, seq, NSTATE), lambda b, g: (b, 0, g))
    dpr = pl.BlockSpec((None, None, 2 * HPG, 1), lambda b, g: (b, g, 0, 0))

    def body(x_ref, b_ref, c_ref, dtr_ref, br_ref, ar_ref, dk_ref, st_ref, dy_ref,
             dx_ref, db_ref, dc_ref, ddtr_ref, gbr_ref, gar_ref, gdk_ref):
        par = (br_ref[...], ar_ref[...], dk_ref[...])
        pgrads = (gbr_ref, gar_ref, gdk_ref)
        for r in pgrads + (dx_ref, db_ref, dc_ref, ddtr_ref):
            r[...] = jnp.zeros_like(r)

        def bstep(i, dcarry):
            new = []
            for rev in (False, True):
                k = i if rev else (nc - 1 - i)
                rows = pl.ds(pl.multiple_of(k * QC, QC), QC)
                xp = tuple(x_ref[rows, _pair_cols(p)] for p in range(NPAIR))
                prev = tuple(st_ref[int(rev), k, p] for p in range(NPAIR))
                _, pull = jax.vjp(functools.partial(_ssd_chunk, rev=rev), xp, dtr_ref[:, rows], b_ref[rows, :],
                                  c_ref[rows, :], prev, *par)
                dyp = tuple(dy_ref[rows, _pair_cols(p)] for p in range(NPAIR))
                gx, gdt, gb, gc, gprev, *gpar = pull((dyp, dcarry[rev]))
                for p in range(NPAIR):
                    dx_ref[rows, _pair_cols(p)] += gx[p]
                ddtr_ref[:, rows] += gdt
                db_ref[rows, :] += gb
                dc_ref[rows, :] += gc
                for r, g in zip(pgrads, gpar):
                    r[...] += g
                new.append(gprev)
            return tuple(new)

        zero = tuple(jnp.zeros((2 * HDIM, NSTATE), F32) for _ in range(NPAIR))
        lax.fori_loop(0, nc, bstep, (zero, zero))

    out_shape = [jax.ShapeDtypeStruct((bsz, seq, SSD_W), F32),
                 jax.ShapeDtypeStruct((bsz, seq, SGROUPS * NSTATE), F32),
                 jax.ShapeDtypeStruct((bsz, seq, SGROUPS * NSTATE), F32),
                 jax.ShapeDtypeStruct(dtr.shape, F32)]
    out_shape += [jax.ShapeDtypeStruct((bsz, SGROUPS, 2 * HPG, 1), F32)] * 3
    return pl.pallas_call(
        body, name=name, grid=(bsz, SGROUPS), in_specs=[xs, bm, cm, dtrs, pr, pr, pr, st, xs],
        out_specs=[xs, grp, grp, dtrs, dpr, dpr, dpr], out_shape=out_shape,
        compiler_params=_params(("parallel", "parallel")))(act, act, act, dtr, *prs, states, dy)


def _s5_core(lam_re, lam_im, log_step, b_re, b_im, c_re, c_im):
    q = S5_Q
    step = jnp.exp(log_step)[:, None]
    lr, li = lam_re * step, lam_im * step
    mag = jnp.exp(lr)
    ar, ai = mag * jnp.cos(li), mag * jnp.sin(li)
    den = lam_re * lam_re + lam_im * lam_im
    cr = ((ar - 1.0) * lam_re + ai * lam_im) / den
    ci = (ai * lam_re - (ar - 1.0) * lam_im) / den
    bbr = cr[..., None] * b_re - ci[..., None] * b_im
    bbi = cr[..., None] * b_im + ci[..., None] * b_re
    d = jnp.arange(q + 1, dtype=F32)[None, :, None]
    pm = jnp.exp(d * lr[:, None, :])
    pr, pi = pm * jnp.cos(d * li[:, None, :]), pm * jnp.sin(d * li[:, None, :])
    er = pr[..., None] * bbr[:, None] - pi[..., None] * bbi[:, None]
    ei = pr[..., None] * bbi[:, None] + pi[..., None] * bbr[:, None]
    hp = lax.Precision.HIGHEST
    k = (jnp.einsum('gcp,gdpz->gdcz', c_re, er[:, :q], precision=hp)
         - jnp.einsum('gcp,gdpz->gdcz', c_im, ei[:, :q], precision=hp))
    e = jnp.concatenate([er[:, :q], ei[:, :q]], axis=2)
    p1r, p1i = pr[:, 1:], pi[:, 1:]
    m_re = c_re[:, None] * p1r[:, :, None, :] - c_im[:, None] * p1i[:, :, None, :]
    m_im = -c_re[:, None] * p1i[:, :, None, :] - c_im[:, None] * p1r[:, :, None, :]
    da = jnp.concatenate([pr[:, q], pr[:, q]], axis=-1)
    db = jnp.concatenate([-pi[:, q], pi[:, q]], axis=-1)
    return k, e, jnp.concatenate([m_re, m_im], axis=-1), da, db


def _s5_operators(lf_re, lf_im, lsf, lb_re, lb_im, lsb, b_re, b_im, cf_re, cf_im, cb_re, cb_im):
    g = lf_re.shape[0]
    both = lambda f, b: jnp.concatenate([f, b], axis=0)
    k, e, m, da, db = _s5_core(both(lf_re, lb_re), both(lf_im, lb_im), both(lsf, lsb), both(b_re, b_re),
                               both(b_im, b_im), both(cf_re, cb_re), both(cf_im, cb_im))
    kf, kb = k[:g], k[g:]
    wtf, wtb = jnp.transpose(e[:g, ::-1], (0, 1, 3, 2)), jnp.transpose(e[g:], (0, 1, 3, 2))
    mtf, mtb = jnp.transpose(m[:g], (0, 3, 1, 2)), jnp.transpose(m[g:, ::-1], (0, 3, 1, 2))
    daf, dab, dbf, dbb = da[:g], da[g:], db[:g], db[g:]
    lags = jnp.concatenate([kb[:, :0:-1], kf[:, :1] + kb[:, :1], kf[:, 1:]], axis=1)
    tt = jnp.transpose(lags, (0, 1, 3, 2))
    wt = jnp.concatenate([wtf.reshape(g, S5_QC, 2 * S5_P), wtb.reshape(g, S5_QC, 2 * S5_P)], axis=-1)
    mt = jnp.concatenate([mtf.reshape(g, 2 * S5_P, S5_QC), mtb.reshape(g, 2 * S5_P, S5_QC)], axis=1)
    return tt, wt, mt, jnp.concatenate([daf, dab], -1), jnp.concatenate([dbf, dbb], -1)


def _gspec(*shape):
    return pl.BlockSpec((None,) + shape, lambda g: (g,) + (0,) * len(shape))


S5_HALVES = S5_QC // LANES


def _toeplitz_block(s, t):
    per = LANES // S5_C
    return t // per, slice(s * S5_C, (s + 1) * S5_C), slice((t % per) * S5_C, (t % per + 1) * S5_C)


def s5_toeplitz(kt, *, name):
    g = kt.shape[0]

    def body(k_ref, t_ref):
        for s in range(S5_Q):
            for t in range(S5_Q):
                t_ref[_toeplitz_block(s, t)] = k_ref[t - s + S5_Q - 1]

    return pl.pallas_call(
        body, name=name, grid=(g,), in_specs=[_gspec(2 * S5_Q - 1, S5_C, S5_C)],
        out_specs=_gspec(S5_HALVES, S5_QC, LANES), out_shape=jax.ShapeDtypeStruct((g, S5_HALVES, S5_QC, LANES), F32),
        compiler_params=_params(("parallel",)))(kt)


def s5_toeplitz_bwd(dtt, *, name):
    g = dtt.shape[0]

    def body(d_ref, k_ref):
        for j in range(2 * S5_Q - 1):
            acc = None
            for s in range(S5_Q):
                t = j - (S5_Q - 1) + s
                if 0 <= t < S5_Q:
                    blk = d_ref[_toeplitz_block(s, t)]
                    acc = blk if acc is None else acc + blk
            k_ref[j] = acc

    return pl.pallas_call(
        body, name=name, grid=(g,), in_specs=[_gspec(S5_HALVES, S5_QC, LANES)],
        out_specs=_gspec(2 * S5_Q - 1, S5_C, S5_C), out_shape=jax.ShapeDtypeStruct((g, 2 * S5_Q - 1, S5_C, S5_C), F32),
        compiler_params=_params(("parallel",)))(dtt)


S5_RT = 128


def _chunk_piece(q):
    per = LANES // S5_C
    return q // per, slice((q % per) * S5_C, (q % per + 1) * S5_C)


def to_chunks(u, *, name):
    t = u.shape[0]
    r = t // S5_Q
    rt = min(S5_RT, r)

    per = LANES // S5_C
    nblk = S5_W // LANES

    def body(*refs):
        o_ref = refs[-1]
        for k in range(nblk):
            for q in range(S5_Q):
                rows = refs[k][pl.ds(q, rt, stride=S5_Q), :]
                half, lanes = _chunk_piece(q)
                for j in range(per):
                    o_ref[k * per + j, half, :, lanes] = rows[:, j * S5_C:(j + 1) * S5_C]

    return pl.pallas_call(
        body, name=name, grid=(r // rt,),
        in_specs=[pl.BlockSpec((rt * S5_Q, LANES), lambda i, k=k: (i, k)) for k in range(nblk)],
        out_specs=pl.BlockSpec((S5_G, S5_HALVES, rt, LANES), lambda i: (0, 0, i, 0)),
        out_shape=jax.ShapeDtypeStruct((S5_G, S5_HALVES, r, LANES), F32),
        compiler_params=_params(("parallel",)))(*[u] * nblk)


def from_chunks(y, *, name, add=None, as_blocks=False):
    r = y.shape[2]
    rt = min(S5_RT, r)
    per = LANES // S5_C

    nblk = S5_W // LANES

    def body(*refs):
        y_ref, tmp_ref = refs[0], refs[-1]
        adds, outs = refs[1:-1 - nblk], refs[-1 - nblk:-1]
        for k in range(nblk):
            for q in range(S5_Q):
                half, lanes = _chunk_piece(q)
                for j in range(per):
                    tmp_ref[:, j * S5_C:(j + 1) * S5_C] = y_ref[k * per + j, half, :, lanes]
                row = tmp_ref[...]
                if add is not None:
                    row = row + adds[k][pl.ds(q, rt, stride=S5_Q), :]
                outs[k][pl.ds(q, rt, stride=S5_Q), :] = row

    in_specs = [pl.BlockSpec((S5_G, S5_HALVES, rt, LANES), lambda i: (0, 0, i, 0))]
    if add is not None:
        in_specs += [pl.BlockSpec((rt * S5_Q, LANES), lambda i, k=k: (i, k)) for k in range(nblk)]
    blocks = pl.pallas_call(
        body, name=name, grid=(r // rt,), in_specs=in_specs,
        out_specs=[pl.BlockSpec((rt * S5_Q, LANES), lambda i: (i, 0))] * nblk,
        out_shape=[jax.ShapeDtypeStruct((r * S5_Q, LANES), F32)] * nblk,
        scratch_shapes=[pltpu.VMEM((rt, LANES), F32)],
        compiler_params=_params(("parallel",)))(*([y] if add is None else [y] + [add] * nblk))
    return list(blocks) if as_blocks else jnp.concatenate(blocks, axis=1)


def _cat(ref):
    return jnp.concatenate([ref[h] for h in range(S5_HALVES)], axis=1)


def _put(ref, v):
    for h in range(S5_HALVES):
        ref[h] = v[:, h * LANES:(h + 1) * LANES]


def _mspec(gp, *shape):
    return pl.BlockSpec((gp,) + shape, lambda i: (i,) + (0,) * len(shape))


def _carry_spec(nck):
    return pl.BlockSpec((nck, 8, 4 * S5_P), lambda i: (0, i, 0))


def _carry_rows(ref, gl, bsz):
    return jnp.concatenate([ref[:, gl * bsz + b, :] for b in range(bsz)], axis=0)


def _carry_put(ref, gl, bsz, v):
    nck = v.shape[0] // bsz
    for b in range(bsz):
        ref[:, gl * bsz + b, :] = v[b * nck:(b + 1) * nck, :]


def s5_state_in(u, wt, *, bsz, name):
    g, _, r, _ = u.shape
    gp, nck = 8 // bsz, r // bsz

    def body(u_ref, w_ref, o_ref):
        for gl in range(gp):
            _carry_put(o_ref, gl, bsz, _bd(_cat(u_ref.at[gl]), w_ref[gl], 1, 0))

    return pl.pallas_call(
        body, name=name, grid=(g // gp,), in_specs=[_mspec(gp, S5_HALVES, r, LANES), _mspec(gp, S5_QC, 4 * S5_P)],
        out_specs=_carry_spec(nck), out_shape=jax.ShapeDtypeStruct((nck, g * bsz, 4 * S5_P), F32),
        compiler_params=_params(("parallel",)))(u, wt)


def _swap(h):
    return pltpu.roll(h, S5_P, 1)


def s5_carry_fwd(s, da, db, *, name):
    nck, rows, _ = s.shape
    w = 2 * S5_P

    def body(s_ref, da_ref, db_ref, h_ref):
        dirs = ((False, slice(0, w)), (True, slice(w, 2 * w)))
        coef = [(da_ref[:, cols], db_ref[:, cols]) for _, cols in dirs]

        def step(i, hs):
            new = []
            for (rev, cols), (a, b), h in zip(dirs, coef, hs):
                k = (nck - 1 - i) if rev else i
                h_ref[k, :, cols] = h
                new.append(a * h + b * _swap(h) + s_ref[k, :, cols])
            return tuple(new)

        z = jnp.zeros((rows, w), F32)
        lax.fori_loop(0, nck, step, (z, z), unroll=2)

    rt = min(2 * CARRY_ROWS, rows)
    big, small = pl.BlockSpec((nck, rt, 2 * w), lambda i: (0, i, 0)), pl.BlockSpec((rt, 2 * w), lambda i: (i, 0))
    rows = rt
    return pl.pallas_call(
        body, name=name, grid=(s.shape[1] // rt,), in_specs=[big, small, small], out_specs=big,
        out_shape=jax.ShapeDtypeStruct(s.shape, F32), compiler_params=_params(("parallel",)))(s, da, db)


def s5_carry_bwd(hin, dh, da, db, *, name):
    nck, rows, _ = hin.shape
    w = 2 * S5_P

    def body(h_ref, dh_ref, da_ref, db_ref, ds_ref, gda_ref, gdb_ref):
        dirs = ((False, slice(0, w)), (True, slice(w, 2 * w)))
        coef = [(da_ref[:, cols], db_ref[:, cols]) for _, cols in dirs]

        def step(i, carries):
            new = []
            for (rev, cols), (a, b), (g, ga, gb) in zip(dirs, coef, carries):
                k = i if rev else (nck - 1 - i)
                ds_ref[k, :, cols] = g
                h = h_ref[k, :, cols]
                new.append((dh_ref[k, :, cols] + a * g + _swap(b * g), ga + g * h, gb + g * _swap(h)))
            return tuple(new)

        z = jnp.zeros((rows, w), F32)
        res = lax.fori_loop(0, nck, step, ((z, z, z), (z, z, z)), unroll=2)
        for (_, cols), (_, ga, gb) in zip(dirs, res):
            gda_ref[:, cols] = ga
            gdb_ref[:, cols] = gb

    rt = min(CARRY_ROWS, rows)
    big, small = pl.BlockSpec((nck, rt, 2 * w), lambda i: (0, i, 0)), pl.BlockSpec((rt, 2 * w), lambda i: (i, 0))
    rows = rt
    return pl.pallas_call(
        body, name=name, grid=(hin.shape[1] // rt,), in_specs=[big, big, small, small], out_specs=[big, small, small],
        out_shape=[jax.ShapeDtypeStruct(hin.shape, F32), jax.ShapeDtypeStruct(da.shape, F32),
                   jax.ShapeDtypeStruct(da.shape, F32)],
        compiler_params=_params(("parallel",)))(hin, dh, da, db)


def s5_out(u, hin, tt, mt, *, bsz, name):
    g, _, r, _ = u.shape
    gp, nck = 8 // bsz, r // bsz

    def body(u_ref, h_ref, t_ref, m_ref, o_ref):
        for gl in range(gp):
            u_v, h_v = _cat(u_ref.at[gl]), _carry_rows(h_ref, gl, bsz)
            for half in range(S5_HALVES):
                cols = slice(half * LANES, (half + 1) * LANES)
                o_ref[gl, half] = _bd(u_v, t_ref[gl, half], 1, 0) + _bd(h_v, m_ref[gl, :, cols], 1, 0)

    cspec = _mspec(gp, S5_HALVES, r, LANES)
    return pl.pallas_call(
        body, name=name, grid=(g // gp,),
        in_specs=[cspec, _carry_spec(nck), _mspec(gp, S5_HALVES, S5_QC, LANES), _mspec(gp, 4 * S5_P, S5_QC)],
        out_specs=cspec, out_shape=jax.ShapeDtypeStruct((g, S5_HALVES, r, LANES), F32),
        compiler_params=_params(("parallel",)))(u, hin, tt, mt)


def s5_out_bwd(dy, u, hin, tt, mt, *, bsz, name):
    g, _, r, _ = u.shape
    gp, nck = 8 // bsz, r // bsz

    def body(dy_ref, u_ref, h_ref, t_ref, m_ref, dh_ref, dt_ref, dm_ref, du_ref):
        for gl in range(gp):
            dy_v, u_v = _cat(dy_ref.at[gl]), _cat(u_ref.at[gl])
            _carry_put(dh_ref, gl, bsz, _bd(dy_v, m_ref[gl], 1, 1))
            dm_ref[gl] = _bd(_carry_rows(h_ref, gl, bsz), dy_v, 0, 0)
            du = None
            for half in range(S5_HALVES):
                dy_h = dy_ref[gl, half]
                dt_ref[gl, half] = _bd(u_v, dy_h, 0, 0)
                part = _bd(dy_h, t_ref[gl, half], 1, 1)
                du = part if du is None else du + part
            _put(du_ref.at[gl], du)

    cspec, tspec = _mspec(gp, S5_HALVES, r, LANES), _mspec(gp, S5_HALVES, S5_QC, LANES)
    mspec = _mspec(gp, 4 * S5_P, S5_QC)
    return pl.pallas_call(
        body, name=name, grid=(g // gp,),
        in_specs=[cspec, cspec, _carry_spec(nck), tspec, mspec],
        out_specs=[_carry_spec(nck), tspec, mspec, cspec],
        out_shape=[jax.ShapeDtypeStruct((nck, g * bsz, 4 * S5_P), F32),
                   jax.ShapeDtypeStruct((g, S5_HALVES, S5_QC, LANES), F32),
                   jax.ShapeDtypeStruct((g, 4 * S5_P, S5_QC), F32), jax.ShapeDtypeStruct((g, S5_HALVES, r, LANES), F32)],
        compiler_params=_params(("parallel",)))(dy, u, hin, tt, mt)


def s5_state_in_bwd(ds, u, wt, du1, *, bsz, name):
    g, _, r, _ = u.shape
    gp, nck = 8 // bsz, r // bsz

    def body(ds_ref, u_ref, w_ref, du1_ref, du_ref, dw_ref):
        for gl in range(gp):
            ds_v = _carry_rows(ds_ref, gl, bsz)
            _put(du_ref.at[gl], _cat(du1_ref.at[gl]) + _bd(ds_v, w_ref[gl], 1, 1))
            dw_ref[gl] = _bd(_cat(u_ref.at[gl]), ds_v, 0, 0)

    cspec, wspec = _mspec(gp, S5_HALVES, r, LANES), _mspec(gp, S5_QC, 4 * S5_P)
    return pl.pallas_call(
        body, name=name, grid=(g // gp,),
        in_specs=[_carry_spec(nck), cspec, wspec, cspec], out_specs=[cspec, wspec],
        out_shape=[jax.ShapeDtypeStruct((g, S5_HALVES, r, LANES), F32), jax.ShapeDtypeStruct((g, S5_QC, 4 * S5_P), F32)],
        compiler_params=_params(("parallel",)))(ds, u, wt, du1)


def _s5_post(ypre, u, dvec, wv, wg, bv, bg, nw):
    g = _gelu(ypre + dvec * u)
    out = (dot_nn(g, wv) + bv) * jax.nn.sigmoid(dot_nn(g, wg) + bg)
    return (_rms(out, nw),)


def _ssd_post(y, z, nw):
    return (_rms(y * _silu(z), nw),)


def _block_diag(w):
    eye = jnp.eye(S5_G, dtype=w.dtype)
    return jnp.einsum('gcd,gh->gchd', w, eye).reshape(S5_W, S5_W)


def _diag_blocks(w):
    v = w.reshape(S5_G, S5_C, S5_G, S5_C)
    return v[jnp.arange(S5_G), :, jnp.arange(S5_G), :]


def _dt_rows(dt, bsz):
    seq = dt.shape[0] // bsz
    return jnp.transpose(dt.reshape(bsz, seq, 2, SGROUPS, HPG), (0, 3, 2, 4, 1)).reshape(bsz, SGROUPS, 2 * HPG, seq)


def _dt_from_rows(dr):
    bsz, _, _, seq = dr.shape
    return jnp.transpose(dr.reshape(bsz, SGROUPS, 2, HPG, seq), (0, 4, 2, 1, 3)).reshape(bsz * seq, 2 * HEADS)


def _head_params(f, b):
    return jnp.concatenate([f.reshape(SGROUPS, HPG), b.reshape(SGROUPS, HPG)], axis=1)[:, :, None]


def _head_grads(gr):
    v = gr.sum(0)[:, :, 0]
    return v[:, :HPG].reshape(HEADS), v[:, HPG:].reshape(HEADS)


def local_step(x, target, w):
    bsz, seq, d = x.shape
    t = bsz * seq
    x2, tgt2 = x.reshape(t, d), target.reshape(t, d)
    g = {}
    row = lambda v: v.reshape(1, -1)
    bf = lambda v: v.astype(BF16)

    w_in = _unshard(bf(w['w_in']), SHARDED['w_in'])
    cuts = [0, SSD_W, SSD_W + XBC_W, SSD_W + XBC_W + 2 * HEADS, w_in.shape[1]]
    w_in_parts = [w_in[:, a:b] for a, b in zip(cuts[:-1], cuts[1:])]
    norm_mix = row(w['norm_mix_w']) + w.get('token', 0.0)
    (hn,) = rowmap_fwd(lambda a, nw: (_rms(a, nw),), [x2], [norm_mix], [(d, BF16)], tm=512, name="rms_mix")
    z, xbc, dt, u = matmul_multi(hn, w_in_parts, name="in_proj")

    conv_w, conv_b = _unshard(w['ssd_conv_w'], SHARDED['ssd_conv_w']), row(w['ssd_conv_b'])
    act = ssd_conv_fwd(xbc, conv_w, conv_b, bsz=bsz, name="ssd_conv")
    dtr = _dt_rows(dt, bsz)
    prs = (_head_params(w['ssd_dt_bias_fwd'], w['ssd_dt_bias_bwd']),
           _head_params(w['ssd_a_log_fwd'], w['ssd_a_log_bwd']),
           _head_params(w['ssd_d'], jnp.zeros_like(w['ssd_d'])))
    act3 = act.reshape(bsz, seq, XBC_W)
    y_scan, ssd_states = ssd_scan_fwd(act3, dtr, prs, name="ssd_scan")
    y_scan = y_scan.reshape(t, SSD_W)
    ssd_nw = row(w['ssd_norm_w'])
    (y_ssd,) = rowmap_fwd(_ssd_post, [y_scan, z], [ssd_nw], [(SSD_W, BF16)], tm=512, name="ssd_post")

    s5_names = ['s5_lambda_re_fwd', 's5_lambda_im_fwd', 's5_log_step_fwd', 's5_lambda_re_bwd', 's5_lambda_im_bwd',
                's5_log_step_bwd', 's5_b_re', 's5_b_im', 's5_c_re_fwd', 's5_c_im_fwd', 's5_c_re_bwd', 's5_c_im_bwd']
    (kt, wt, mt, da, db), s5_pull = jax.vjp(_s5_operators, *[w[n] for n in s5_names])
    tt_b, wt_b, mt_b = s5_toeplitz(kt, name="s5_toeplitz"), bf(wt), bf(mt)
    da_r, db_r = jnp.repeat(da, bsz, axis=0), jnp.repeat(db, bsz, axis=0)
    uc = to_chunks(u, name="s5_to_chunks_u")
    hin = s5_carry_fwd(s5_state_in(uc, wt_b, bsz=bsz, name="s5_state_in"), da_r, db_r, name="s5_carry")
    ypre = from_chunks(s5_out(uc, hin, tt_b, mt_b, bsz=bsz, name="s5_out"), name="s5_from_chunks_y", as_blocks=True)
    glu_w = w['s5_glu_w']
    s5_par = [row(w['s5_d']), _block_diag(glu_w[:, :, :S5_C]), _block_diag(glu_w[:, :, S5_C:]),
              row(w['s5_glu_b'][:, :S5_C]), row(w['s5_glu_b'][:, S5_C:]), row(w['s5_norm_w'])]
    (y_s5,) = rowmap_fwd(_s5_post, [ypre, u], s5_par, [(S5_W, BF16)], tm=512, name="s5_post")

    if 'late' in w:
        w = {**w, **w['late'](y_s5)}
    w_out = bf(w['w_out']).reshape(SSD_W + S5_W, d)
    norm_ffn = row(w['norm_ffn_w'])
    h1, hn2 = matmul_sum([y_ssd, y_s5], [w_out[:SSD_W], w_out[SSD_W:]], add=x2, norm_w=norm_ffn, name="out_proj")
    pad_c = FFN_PAD - FFN_BLK
    half = N_DEV // 2
    w_up3 = jnp.pad(bf(w['ffn_w_up']), ((0, 0), (0, 0), (0, pad_c)))
    w_down = jnp.pad(bf(w['ffn_w_down']).reshape(half, FFN_BLK, d), ((0, 0), (0, pad_c), (0, 0)))
    w_down = w_down.reshape(half * FFN_PAD, d)
    fconv_w = jnp.pad(w['ffn_conv_w'], ((0, 0), (0, 0), (0, pad_c)))
    fconv_w = jnp.transpose(fconv_w, (1, 0, 2)).reshape(FCONV, N_DEV * FFN_PAD)
    fconv_b = row(jnp.pad(w['ffn_conv_b'].reshape(N_DEV, FFN_BLK), ((0, 0), (0, pad_c))))
    up = matmul_cols(hn2, w_up3, out_dtype=BF16, name="ffn_up")
    fact = ffn_act_fwd(up, fconv_w, fconv_b, bsz=bsz, name="ffn_act")
    loss, dh2, g_nf = loss_head(h1, tgt2, row(w['norm_final_w']), matmul=(fact, w_down), tm=512, name="ffn_down_loss")
    g['norm_final_w'] = g_nf.reshape(-1)

    dfact = matmul_sum([dh2], [w_down], nt=True, tm=1024, out_dtype=BF16, name="ffn_down_dx")
    g_down = matmul_tn(fact, dh2, name="ffn_down_dw").reshape(half, FFN_PAD, d)[:, :FFN_BLK]
    g['ffn_w_down'] = g_down.reshape(N_DEV, FFN_BLK // 2, d)
    dval, dgate, dwv, dwg, dbv, dbg = ffn_act_bwd(up, dfact, fconv_w, fconv_b, bsz=bsz, name="ffn_act_bwd")
    g_cw = jnp.concatenate([dwv, dwg], axis=1).reshape(FCONV, N_DEV, FFN_PAD)[:, :, :FFN_BLK]
    g['ffn_conv_w'] = jnp.transpose(g_cw, (1, 0, 2))
    g['ffn_conv_b'] = jnp.concatenate([dbv, dbg], axis=1).reshape(N_DEV, FFN_PAD)[:, :FFN_BLK].reshape(-1)
    windows = [(dval, FFN_PAD, p) for p in range(half)] + [(dgate, FFN_PAD, p) for p in range(half)]
    g['ffn_w_up'] = jnp.concatenate([matmul_tn(hn2, dval, out_blocks=half, name="ffn_up_dw_val"),
                                     matmul_tn(hn2, dgate, out_blocks=half, name="ffn_up_dw_gate")],
                                    axis=0)[:, :, :FFN_BLK]
    send_early = w.get('on_grads')
    if send_early:
        norm_ffn = norm_ffn + send_early(g, ['ffn_w_up', 'ffn_w_down'])
    dh1, g_nffn = matmul_sum(windows, [(w_up3, p) for p in range(N_DEV)], nt=True, tm=512,
                             norm_bwd=(h1, norm_ffn, dh2), name="ffn_up_dx")
    g['norm_ffn_w'] = g_nffn.reshape(-1)

    dycat = matmul_sum([dh1], [w_out], nt=True, tm=1024, name="out_proj_dx")
    g['w_out'] = jnp.concatenate([matmul_tn(y_ssd, dh1, name="out_proj_dw_ssd"),
                                  matmul_tn(y_s5, dh1, name="out_proj_dw_s5")], axis=0).reshape(w['w_out'].shape)
    if send_early:
        ssd_nw = ssd_nw + send_early(g, ['w_out'])
    dy_scan, dz, g_snw = rowmap_bwd(_ssd_post, [y_scan, z], [ssd_nw], [(dycat, SSD_W, 0)], tm=512,
                                    name="ssd_post_bwd")
    g['ssd_norm_w'] = g_snw.reshape(-1)
    dypre, du_a, g_d, g_wv, g_wg, g_bv, g_bg, g_s5nw = rowmap_bwd(
        _s5_post, [ypre, u], s5_par, [(dycat, S5_W, SSD_W // S5_W)], tm=512, name="s5_post_bwd")
    g['s5_d'], g['s5_norm_w'] = g_d.reshape(-1), g_s5nw.reshape(-1)
    g['s5_glu_w'] = jnp.concatenate([_diag_blocks(g_wv), _diag_blocks(g_wg)], axis=-1)
    g['s5_glu_b'] = jnp.concatenate([g_bv.reshape(S5_G, S5_C), g_bg.reshape(S5_G, S5_C)], axis=-1)

    dyc = to_chunks(dypre, name="s5_to_chunks_dy")
    dhin, dtt, dmt, du1 = s5_out_bwd(dyc, uc, hin, tt_b, mt_b, bsz=bsz, name="s5_out_bwd")
    ds, gda, gdb = s5_carry_bwd(hin, dhin, da_r, db_r, name="s5_carry_bwd")
    duc, dwt = s5_state_in_bwd(ds, uc, wt_b, du1, bsz=bsz, name="s5_state_in_bwd")
    du = from_chunks(duc, add=du_a, name="s5_from_chunks_du")
    fold = lambda v: v.reshape(S5_G, bsz, -1).sum(1)
    dkt = s5_toeplitz_bwd(dtt, name="s5_toeplitz_bwd")
    for n, gv in zip(s5_names, s5_pull((dkt, dwt, dmt, fold(gda), fold(gdb)))):
        g[n] = gv

    dxs, dbm, dcm, ddtr, gbr, gar, gdk = ssd_scan_bwd(
        act3, dtr, prs, ssd_states, dy_scan.reshape(bsz, seq, SSD_W), name="ssd_scan_bwd")
    g['ssd_dt_bias_fwd'], g['ssd_dt_bias_bwd'] = _head_grads(gbr)
    g['ssd_a_log_fwd'], g['ssd_a_log_bwd'] = _head_grads(gar)
    g['ssd_d'] = _head_grads(gdk)[0]
    dparts_act = [v.reshape(t, v.shape[-1]) for v in (dxs, dbm, dcm)]
    dxbc, g_cw, g_cb = ssd_conv_bwd(xbc, dparts_act, conv_w, conv_b, bsz=bsz, name="ssd_conv_bwd")
    g['ssd_conv_w'] = _shard_rows(g_cw, SHARDED['ssd_conv_w']).reshape(w['ssd_conv_w'].shape)
    g['ssd_conv_b'] = g_cb.reshape(-1)
    ddt = _dt_from_rows(ddtr)

    if send_early:
        ddt = ddt + send_early(g, [], loss=loss)
    dparts = [dz, dxbc, ddt, du]
    g_in = jnp.concatenate([matmul_tn(hn, dp, name=f"in_proj_dw_{i}") for i, dp in enumerate(dparts)], axis=1)
    g['w_in'] = _shard_rows(g_in, SHARDED['w_in']).reshape(w['w_in'].shape)
    if send_early:
        dparts[2] = ddt + send_early(g, ['w_in'])
    dx, g_nmix = matmul_sum(dparts, w_in_parts, nt=True, tm=512, norm_bwd=(x2, norm_mix, dh1), name="in_proj_dx")
    g['norm_mix_w'] = g_nmix.reshape(-1)
    return loss, dx.reshape(bsz, seq, d), g


ANY = pl.BlockSpec(memory_space=pl.ANY)


def all_gather(shards, *, name):
    n = len(shards)

    def body(*refs):
        x_refs, out_refs = refs[:n], refs[n:2 * n]
        send_sems, recv_sems, local_sems = refs[2 * n:]
        x, y, c = lax.axis_index("x"), lax.axis_index("y"), lax.axis_index("c")
        me, sibling = (x, y, c), (x, y, 1 - c)
        chips = [(1 - x, y), (x, 1 - y), (1 - x, 1 - y)]

        def copy(k, j, block, to, own=False):
            dst = out_refs[j].at[4 * block[0] + 2 * block[1] + block[2]]
            return pltpu.make_async_remote_copy(
                src_ref=x_refs[j] if own else dst, dst_ref=dst,
                send_sem=send_sems.at[k, j], recv_sem=recv_sems.at[k, j], device_id=to, device_id_type=MESH)

        mine = [pltpu.make_async_copy(x_refs[j], out_refs[j].at[4 * x + 2 * y + c], local_sems.at[j]) for j in range(n)]
        first = [copy(0, j, me, sibling, own=True) for j in range(n)]
        first += [copy(1 + i, j, me, (*chip, c), own=True) for i, chip in enumerate(chips) for j in range(n)]
        for cp in mine + first:
            cp.start()
        passed = []
        for i, chip in enumerate(chips):
            for j in range(n):
                copy(1 + i, j, (*chip, c), me).wait_recv()
                passed.append(copy(4 + i, j, (*chip, c), sibling))
                passed[-1].start()
        for j in range(n):
            copy(0, j, sibling, me).wait_recv()
        for i, chip in enumerate(chips):
            for j in range(n):
                copy(4 + i, j, (*chip, 1 - c), me).wait_recv()
        for cp in first + passed:
            cp.wait_send()
        for cp in mine:
            cp.wait()

    return pl.pallas_call(
        body, name=name, out_shape=[jax.ShapeDtypeStruct((N_DEV,) + s.shape, s.dtype) for s in shards],
        in_specs=[ANY] * n, out_specs=[ANY] * n,
        scratch_shapes=[pltpu.SemaphoreType.DMA((7, n)), pltpu.SemaphoreType.DMA((7, n)),
                        pltpu.SemaphoreType.DMA((n,))],
    )(*shards)


HBM_SPEC = pl.BlockSpec(memory_space=pltpu.HBM)
SEM_SPEC = pl.BlockSpec(memory_space=pltpu.SEMAPHORE)
SPLIT_PARAMS = pltpu.CompilerParams(has_side_effects=pltpu.SideEffectType.DATAFLOW_SIDE_EFFECTING)


def _peer_copies(src_refs, land_refs, send_sems, recv_sems, indexed):
    x, y, c = lax.axis_index("x"), lax.axis_index("y"), lax.axis_index("c")
    me = 4 * x + 2 * y + c
    copies = []
    for k in range(1, N_DEV):
        px = (1 - x) if k & 4 else x
        py = (1 - y) if k & 2 else y
        pc = (1 - c) if k & 1 else c
        for j, (src, land) in enumerate(zip(src_refs, land_refs)):
            sem = (k - 1) * len(src_refs) + j
            copies.append(pltpu.make_async_remote_copy(
                src_ref=src.at[4 * px + 2 * py + pc] if indexed else src, dst_ref=land.at[me],
                send_sem=send_sems.at[sem], recv_sem=recv_sems.at[sem],
                device_id=(px, py, pc), device_id_type=MESH))
    return copies


def scatter_start(srcs, *, name, indexed):
    n = len(srcs)
    lands = [lax.empty(s.shape if indexed else (N_DEV,) + s.shape, s.dtype) for s in srcs]

    def body(*refs):
        send_sems, recv_sems = refs[2 * n], refs[2 * n + 1]
        for cp in _peer_copies(refs[:n], refs[n:2 * n], send_sems, recv_sems, indexed):
            cp.start()
        refs[-1][...] = jnp.zeros_like(refs[-1])

    hbm = lambda a: pltpu.HBM(a.shape, a.dtype)
    sems = pltpu.SemaphoreType.DMA(((N_DEV - 1) * n,))
    res = pl.pallas_call(
        body, name=name,
        out_shape=(sems, sems, *[hbm(a) for a in srcs + lands], jax.ShapeDtypeStruct((8, LANES), F32)),
        in_specs=[HBM_SPEC] * (2 * n),
        out_specs=(SEM_SPEC, SEM_SPEC, *[HBM_SPEC] * (2 * n), pl.BlockSpec(memory_space=pltpu.VMEM)),
        input_output_aliases={i: 2 + i for i in range(2 * n)}, compiler_params=SPLIT_PARAMS,
    )(*[pltpu.with_memory_space_constraint(a, pltpu.HBM) for a in srcs + lands])
    return res[0], res[1], list(res[2:2 + n]), list(res[2 + n:2 + 2 * n]), res[-1]


def scatter_wait(send_sems, recv_sems, srcs, lands, after, *, name, indexed):
    n = len(srcs)

    def body(*refs):
        for cp in _peer_copies(refs[:n], refs[n:2 * n], refs[2 * n], refs[2 * n + 1], indexed):
            cp.wait_send()
            cp.wait_recv()

    hbm = lambda a: pltpu.HBM(a.shape, a.dtype)
    res = pl.pallas_call(
        body, name=name, out_shape=tuple(hbm(a) for a in srcs + lands),
        in_specs=[HBM_SPEC] * (2 * n) + [SEM_SPEC, SEM_SPEC, ANY], out_specs=tuple([HBM_SPEC] * (2 * n)),
        input_output_aliases={i: i for i in range(2 * n)}, compiler_params=SPLIT_PARAMS,
    )(*srcs, *lands, send_sems, recv_sems, after)
    return list(res[:n]), list(res[n:])


def _adam_rows(r, c):
    fits = [t for t in range(8, r + 1, 8) if r % t == 0 and N_DEV * t * c * 4 <= 6 * 2 ** 20]
    return max(fits) if fits else r


def adamw(recv, w, m, v, *, name):
    _, r, n = recv.shape
    tr = _adam_rows(r, n)

    def body(r_ref, w_ref, m_ref, v_ref, g_ref, d_ref, nm_ref, nv_ref):
        g = r_ref[0].astype(F32)
        for s in range(1, N_DEV):
            g = g + r_ref[s].astype(F32)
        m_new = ADAM_B1 * m_ref[...] + (1.0 - ADAM_B1) * g
        v_new = ADAM_B2 * v_ref[...] + (1.0 - ADAM_B2) * jnp.square(g)
        m_hat = m_new / (1.0 - ADAM_B1 ** ADAM_STEP)
        v_hat = v_new / (1.0 - ADAM_B2 ** ADAM_STEP)
        g_ref[...] = g
        d_ref[...] = -ADAM_LR * (m_hat / (jnp.sqrt(v_hat) + ADAM_EPS) + ADAM_WD * w_ref[...])
        nm_ref[...] = m_new
        nv_ref[...] = v_new

    blk = pl.BlockSpec((tr, n), lambda i: (i, 0))
    return pl.pallas_call(
        body, name=name, grid=(r // tr,), in_specs=[pl.BlockSpec((N_DEV, tr, n), lambda i: (0, i, 0)), blk, blk, blk],
        out_specs=[blk] * 4, out_shape=[jax.ShapeDtypeStruct((r, n), F32)] * 4,
        compiler_params=_params(("parallel",)))(recv, w, m, v)


def _shard_rows(full, axis):
    if axis == 0:
        return full.reshape(N_DEV, -1)
    r, c = full.shape
    return jnp.transpose(full.reshape(r, N_DEV, c // N_DEV), (1, 0, 2)).reshape(N_DEV, -1)


def _unshard(blocks, axis):
    if axis == 0:
        return blocks.reshape(-1, blocks.shape[-1])
    return jnp.transpose(blocks, (1, 0, 2)).reshape(blocks.shape[1], -1)


def kernel(x, norm_mix_w, w_in, ssd_conv_w, ssd_conv_b, ssd_dt_bias_fwd, ssd_dt_bias_bwd, ssd_a_log_fwd, ssd_a_log_bwd, ssd_d, ssd_norm_w, s5_lambda_re_fwd, s5_lambda_im_fwd, s5_log_step_fwd, s5_lambda_re_bwd, s5_lambda_im_bwd, s5_log_step_bwd, s5_b_re, s5_b_im, s5_c_re_fwd, s5_c_im_fwd, s5_c_re_bwd, s5_c_im_bwd, s5_d, s5_glu_w, s5_glu_b, s5_norm_w, w_out, norm_ffn_w, ffn_w_up, ffn_conv_w, ffn_conv_b, ffn_w_down, norm_final_w, loss_target, m_norm_mix_w, m_w_in, m_ssd_conv_w, m_ssd_conv_b, m_ssd_dt_bias_fwd, m_ssd_dt_bias_bwd, m_ssd_a_log_fwd, m_ssd_a_log_bwd, m_ssd_d, m_ssd_norm_w, m_s5_lambda_re_fwd, m_s5_lambda_im_fwd, m_s5_log_step_fwd, m_s5_lambda_re_bwd, m_s5_lambda_im_bwd, m_s5_log_step_bwd, m_s5_b_re, m_s5_b_im, m_s5_c_re_fwd, m_s5_c_im_fwd, m_s5_c_re_bwd, m_s5_c_im_bwd, m_s5_d, m_s5_glu_w, m_s5_glu_b, m_s5_norm_w, m_w_out, m_norm_ffn_w, m_ffn_w_up, m_ffn_conv_w, m_ffn_conv_b, m_ffn_w_down, m_norm_final_w, v_norm_mix_w, v_w_in, v_ssd_conv_w, v_ssd_conv_b, v_ssd_dt_bias_fwd, v_ssd_dt_bias_bwd, v_ssd_a_log_fwd, v_ssd_a_log_bwd, v_ssd_d, v_ssd_norm_w, v_s5_lambda_re_fwd, v_s5_lambda_im_fwd, v_s5_log_step_fwd, v_s5_lambda_re_bwd, v_s5_lambda_im_bwd, v_s5_log_step_bwd, v_s5_b_re, v_s5_b_im, v_s5_c_re_fwd, v_s5_c_im_fwd, v_s5_c_re_bwd, v_s5_c_im_bwd, v_s5_d, v_s5_glu_w, v_s5_glu_b, v_s5_norm_w, v_w_out, v_norm_ffn_w, v_ffn_w_up, v_ffn_conv_w, v_ffn_conv_b, v_ffn_w_down, v_norm_final_w):
    args = dict(locals())
    strip = lambda n, v: v if n == 'norm_final_w' else v[0]
    w = {n: strip(n, args[n]) for n in WEIGHTS}

    mats = ['w_in', 'w_out', 'ffn_w_up', 'ffn_w_down']
    convs = ['ssd_conv_w', 'ffn_conv_w']
    shard = lambda n: w[n].astype(BF16) if n in mats else w[n]
    early, late = ['w_in', 'ssd_conv_w'], ['w_out', 'ffn_w_up', 'ffn_w_down', 'ffn_conv_w']
    full = dict(w)
    full.update(zip(early, all_gather([shard(n) for n in early], name="weight_all_gather")))
    ssem, rsem, src_thru, land_thru, token = scatter_start([shard(n) for n in late], name="weight_gather_start",
                                                           indexed=False)
    me = 4 * lax.axis_index("x") + 2 * lax.axis_index("y") + lax.axis_index("c")

    def late_weights(after):
        own, landed = scatter_wait(ssem, rsem, src_thru, land_thru, after, name="weight_gather_wait", indexed=False)
        return {n: lax.dynamic_update_index_in_dim(l, o, me, 0) for n, o, l in zip(late, own, landed)}

    full['late'], full['token'] = late_weights, token[:1, :1]

    pending = []
    last = 'norm_mix_w'
    small = convs + [n for n in WEIGHTS if n not in SHARDED and n != last]
    slot = {n: -(-w[n].size // (8 * LANES)) * 8 for n in small}
    used = sum(slot.values()) + 8
    nrow = -(-used // PACK_ROWS) * PACK_ROWS

    def tiles(v, n):
        return jnp.pad(v, ((0, 0), (0, slot[n] * LANES - v.shape[1]))).reshape(v.shape[0], slot[n], LANES)

    def send_early(grads, names, loss=None):
        srcs = [grads[n].astype(BF16) for n in names]
        if loss is not None:
            pieces = [tiles(grads[n].reshape(N_DEV, -1), n) if n in SHARDED else
                      jnp.broadcast_to(tiles(grads[n].reshape(1, -1), n), (N_DEV, slot[n], LANES)) for n in small]
            pieces.append(jnp.broadcast_to(jnp.pad(loss.reshape(1, 1, 1), ((0, 0), (0, 7), (0, LANES - 1))),
                                           (N_DEV, 8, LANES)))
            pieces.append(jnp.zeros((N_DEV, nrow - used, LANES), F32))
            srcs.append(jnp.concatenate(pieces, axis=1))
            names = names + ['small']
        started = scatter_start(srcs, name="grad_start_" + names[0], indexed=True)
        pending.append((names,) + started[:4])
        return started[4][:1, :1]

    full['on_grads'] = send_early
    loss, grad_x, g = local_step(x, loss_target, full)

    last_send = jnp.broadcast_to(g[last].reshape(1, -1, LANES), (N_DEV, g[last].size // LANES, LANES))
    last_started = scatter_start([last_send], name="grad_start_" + last, indexed=True)
    recv, outs = {}, [{}, {}, {}, {}]

    def arrived(names, started, after):
        own, landed = scatter_wait(*started, after, name="grad_wait_" + names[0], indexed=True)
        for n, o, l in zip(names, own, landed):
            recv[n] = lax.dynamic_update_index_in_dim(l, lax.dynamic_index_in_dim(o, me, 0, keepdims=False), me, 0)

    def update(n):
        shape = recv[n].shape[1:]
        res = adamw(recv[n], *[strip(n, args[p + n]).reshape(shape) for p in ('', 'm_', 'v_')], name="adamw_" + n)
        for o, p in zip(outs, res):
            o[n] = p.reshape(args[n].shape)

    for names, *started in pending:
        arrived(names, started, last_started[4])
    for n in mats:
        update(n)

    def pack(prefix):
        vals = [tiles(strip(n, args[prefix + n]).reshape(1, -1), n)[0] for n in small]
        return jnp.concatenate(vals + [jnp.zeros((nrow - used + 8, LANES), F32)], axis=0)

    packed = adamw(recv['small'], pack(''), pack('m_'), pack('v_'), name="adamw_small")
    arrived([last], last_started[:4], packed[1])
    update(last)
    off = 0
    for n in small:
        for o, p in zip(outs, packed):
            o[n] = p[off:off + slot[n]].reshape(-1)[:w[n].size].reshape(args[n].shape)
        off += slot[n]
    loss_out = packed[0][off, 0].reshape(())
    return (loss_out, grad_x, *[o[n] for o in outs for n in WEIGHTS])
```

```python
import functools

import jax
import jax.numpy as jnp
from jax import lax
from jax.experimental import pallas as pl
from jax.experimental.pallas import tpu as pltpu

F32, BF16 = jnp.float32, jnp.bfloat16
N_DEV = 8
D_MODEL = 1024
SSD_W, HEADS, HDIM, SGROUPS, HPG, NSTATE, SCONV, QC = 1024, 16, 64, 4, 4, 128, 5, 128
XBC_W = SSD_W + 2 * SGROUPS * NSTATE
S5_W, S5_G, S5_C, S5_P, S5_Q = 512, 32, 16, 64, 16
S5_QC = S5_Q * S5_C
CARRY_ROWS = 32
DFF, FCONV = 2816, 3
FFN_BLK, FFN_PAD = 704, 768
EPS = 1e-6
ADAM_LR, ADAM_B1, ADAM_B2, ADAM_EPS, ADAM_WD, ADAM_STEP = 0.001, 0.9, 0.999, 1e-08, 0.01, 10
LANES = 128
MESH = pl.DeviceIdType.MESH

WEIGHTS = ['norm_mix_w', 'w_in', 'ssd_conv_w', 'ssd_conv_b', 'ssd_dt_bias_fwd', 'ssd_dt_bias_bwd', 'ssd_a_log_fwd',
           'ssd_a_log_bwd', 'ssd_d', 'ssd_norm_w', 's5_lambda_re_fwd', 's5_lambda_im_fwd', 's5_log_step_fwd',
           's5_lambda_re_bwd', 's5_lambda_im_bwd', 's5_log_step_bwd', 's5_b_re', 's5_b_im', 's5_c_re_fwd', 's5_c_im_fwd',
           's5_c_re_bwd', 's5_c_im_bwd', 's5_d', 's5_glu_w', 's5_glu_b', 's5_norm_w', 'w_out', 'norm_ffn_w', 'ffn_w_up',
           'ffn_conv_w', 'ffn_conv_b', 'ffn_w_down', 'norm_final_w']
SHARDED = {'w_in': 1, 'ssd_conv_w': 1, 'w_out': 0, 'ffn_w_up': 1, 'ffn_conv_w': 1, 'ffn_w_down': 0}
FULL_SHAPE = {'w_in': (1024, 3616), 'ssd_conv_w': (5, 2048), 'w_out': (1536, 1024), 'ffn_w_up': (1024, 5632),
              'ffn_conv_w': (3, 5632), 'ffn_w_down': (2816, 1024)}
PACK_ROWS = 512


def _pick(n, cap=1536):
    if n <= cap:
        return n
    return max(t for t in range(LANES, cap + 1, LANES) if n % t == 0)


def _params(sem):
    return pltpu.CompilerParams(dimension_semantics=sem)


def _bd(a, b, ca, cb):
    return lax.dot_general(a.astype(BF16), b.astype(BF16), (((ca,), (cb,)), ((), ())), preferred_element_type=F32)


@jax.custom_vjp
def dot_nn(a, b):
    return _bd(a, b, 1, 0)


dot_nn.defvjp(lambda a, b: (_bd(a, b, 1, 0), (a, b)),
              lambda r, g: (_bd(g, r[1], 1, 1).astype(r[0].dtype), _bd(r[0], g, 0, 0).astype(r[1].dtype)))


@jax.custom_vjp
def dot_nt(a, b):
    return _bd(a, b, 1, 1)


dot_nt.defvjp(lambda a, b: (_bd(a, b, 1, 1), (a, b)),
              lambda r, g: (_bd(g, r[1], 1, 0).astype(r[0].dtype), _bd(g, r[0], 0, 0).astype(r[1].dtype)))


@jax.custom_vjp
def dot_tn(a, b):
    return _bd(a, b, 0, 0)


dot_tn.defvjp(lambda a, b: (_bd(a, b, 0, 0), (a, b)),
              lambda r, g: (_bd(r[1], g, 1, 1).astype(r[0].dtype), _bd(r[0], g, 1, 0).astype(r[1].dtype)))


def _rows2(v):
    h = v.shape[0] // 2
    return v[:h], v[h:]


def _cols2(v):
    h = v.shape[1] // 2
    return v[:, :h], v[:, h:]


@jax.custom_vjp
def dot2_nn(la, lb, x):
    return _rows2(_bd(jnp.concatenate([la, lb], axis=0), x, 1, 0))


def _dot2_nn_bwd(res, g):
    la, lb, x = res
    gcat, lcat = jnp.concatenate(g, axis=0), jnp.concatenate([la, lb], axis=0)
    return (*_rows2(_bd(gcat, x, 1, 1)), _bd(lcat, gcat, 0, 0))


dot2_nn.defvjp(lambda la, lb, x: (dot2_nn(la, lb, x), (la, lb, x)), _dot2_nn_bwd)


@jax.custom_vjp
def dot_nt2(c, p0, p1):
    return _cols2(_bd(c, jnp.concatenate([p0, p1], axis=0), 1, 1))


def _dot_nt2_bwd(res, g):
    c, p0, p1 = res
    gcat = jnp.concatenate(g, axis=1)
    return (_bd(gcat, jnp.concatenate([p0, p1], axis=0), 1, 0), *_rows2(_bd(gcat, c, 0, 0)))


dot_nt2.defvjp(lambda c, p0, p1: (dot_nt2(c, p0, p1), (c, p0, p1)), _dot_nt2_bwd)


@jax.custom_vjp
def dot_tn2(a0, a1, b):
    return _rows2(_bd(jnp.concatenate([a0, a1], axis=1), b, 0, 0))


def _dot_tn2_bwd(res, g):
    a0, a1, b = res
    gcat, acat = jnp.concatenate(g, axis=0), jnp.concatenate([a0, a1], axis=1)
    return (*_cols2(_bd(b, gcat, 1, 1)), _bd(acat, gcat, 1, 0))


dot_tn2.defvjp(lambda a0, a1, b: (dot_tn2(a0, a1, b), (a0, a1, b)), _dot_tn2_bwd)


def _split3(x):
    hi = x.astype(BF16)
    r = x - hi.astype(F32)
    mid = r.astype(BF16)
    lo = (r - mid.astype(F32)).astype(BF16)
    return hi, mid, lo


def _cum_matrix(q, upper):
    ri = lax.broadcasted_iota(jnp.int32, (q, q), 0)
    ci = lax.broadcasted_iota(jnp.int32, (q, q), 1)
    return jnp.where((ci >= ri) if upper else (ci <= ri), 1.0, 0.0).astype(BF16)


def _exact_right(x, mat):
    return sum(jnp.dot(p, mat, preferred_element_type=F32) for p in _split3(x))


@functools.partial(jax.custom_vjp, nondiff_argnums=(1,))
def cum_row(x, rev):
    return _exact_right(x, _cum_matrix(x.shape[1], not rev))


cum_row.defvjp(lambda x, rev: (cum_row(x, rev), None),
               lambda rev, _, g: (_exact_right(g, _cum_matrix(g.shape[1], rev)),))


def _softplus(x):
    return jnp.maximum(x, 0.0) + jnp.log(1.0 + jnp.exp(-jnp.abs(x)))


def _silu(x):
    return x * jax.nn.sigmoid(x)


def _gelu(x):
    return 0.5 * x * (1.0 + jnp.tanh(0.7978845608028654 * (x + 0.044715 * (x * x * x))))


def _rms(x, w):
    xf = x.astype(F32)
    return xf * lax.rsqrt(jnp.mean(xf * xf, axis=-1, keepdims=True) + EPS) * w


def matmul_sum(a_list, b_list, *, name, out_dtype=F32, add=None, tm=512, nt=False, norm_w=None, norm_bwd=None):
    a_arrs = [a[0] if isinstance(a, tuple) else a for a in a_list]
    b_arrs = [b[0] if isinstance(b, tuple) else b for b in b_list]
    m, n = a_arrs[0].shape[0], b_arrs[0].shape[-2 if nt else -1]
    tm, tn, k = min(tm, m), _pick(n), len(a_list)
    assert (norm_w is None and norm_bwd is None) or tn == n

    def body(*refs):
        acc = None
        for a_ref, b_ref in zip(refs[:k], refs[k:2 * k]):
            p = _bd(a_ref[...], b_ref[...], 1, 1 if nt else 0)
            acc = p if acc is None else acc + p
        if add is not None:
            acc = acc + refs[2 * k][...]
        if norm_bwd is not None:
            x_ref, w_ref, res_ref, dx_ref, dw_ref = refs[-5:]
            dx, dw = jax.vjp(_rms, x_ref[...], w_ref[...])[1](acc)
            dx_ref[...] = dx + res_ref[...]

            @pl.when(pl.program_id(0) == 0)
            def _():
                dw_ref[...] = jnp.zeros_like(dw_ref)

            dw_ref[...] += dw
        elif norm_w is not None:
            refs[-2][...] = acc.astype(out_dtype)
            refs[-1][...] = _rms(acc, refs[-3][...]).astype(BF16)
        else:
            refs[-1][...] = acc.astype(out_dtype)

    def a_spec(a):
        if isinstance(a, tuple):
            return pl.BlockSpec((tm, a[1]), lambda i, j, blk=a[2]: (i, blk))
        return pl.BlockSpec((tm, a.shape[1]), lambda i, j: (i, 0))

    def b_spec(b):
        arr, p = b if isinstance(b, tuple) else (b, None)
        kk = arr.shape[-1 if nt else -2]
        shape, idx = ((tn, kk), lambda j: (j, 0)) if nt else ((kk, tn), lambda j: (0, j))
        mode = {'pipeline_mode': pl.Buffered(1)} if tn == n else {}
        if p is None:
            return pl.BlockSpec(shape, lambda i, j: idx(j), **mode)
        return pl.BlockSpec((None,) + shape, lambda i, j, p=p: (p,) + idx(j), **mode)

    in_specs = [a_spec(a) for a in a_list] + [b_spec(b) for b in b_list]
    args = a_arrs + b_arrs
    if add is not None:
        in_specs.append(pl.BlockSpec((tm, tn), lambda i, j: (i, j)))
        args.append(add)
    out_spec, out_shape = pl.BlockSpec((tm, tn), lambda i, j: (i, j)), jax.ShapeDtypeStruct((m, n), out_dtype)
    if norm_w is not None:
        in_specs.append(pl.BlockSpec(norm_w.shape, lambda i, j: (0, 0)))
        args.append(norm_w)
        out_spec, out_shape = [out_spec, out_spec], [out_shape, jax.ShapeDtypeStruct((m, n), BF16)]
    sem = ("parallel", "parallel")
    if norm_bwd is not None:
        x, w, res = norm_bwd
        wspec = pl.BlockSpec(w.shape, lambda i, j: (0, 0))
        in_specs += [out_spec, wspec, out_spec]
        args += [x, w, res]
        out_spec, out_shape = [out_spec, wspec], [jax.ShapeDtypeStruct((m, n), F32), jax.ShapeDtypeStruct(w.shape, F32)]
        sem = ("arbitrary", "arbitrary")
    return pl.pallas_call(
        body, name=name, grid=(m // tm, n // tn), in_specs=in_specs, out_specs=out_spec, out_shape=out_shape,
        compiler_params=_params(sem))(*args)


def matmul_multi(a, b_list, *, name, tm=512):
    m, kk = a.shape
    tm, nb = min(tm, m), len(b_list)

    def body(a_ref, *refs):
        a_v = a_ref[...]
        for b_ref, o_ref in zip(refs[:nb], refs[nb:]):
            o_ref[...] = _bd(a_v, b_ref[...], 1, 0)

    return pl.pallas_call(
        body, name=name, grid=(m // tm,),
        in_specs=[pl.BlockSpec((tm, kk), lambda i: (i, 0))] + [_full_spec(b) for b in b_list],
        out_specs=[pl.BlockSpec((tm, b.shape[1]), lambda i: (i, 0)) for b in b_list],
        out_shape=[jax.ShapeDtypeStruct((m, b.shape[1]), F32) for b in b_list],
        compiler_params=_params(("parallel",)))(a, *b_list)


def matmul_cols(a, b3, *, name, out_dtype=F32, tm=2048):
    m, kk = a.shape
    p, _, nb = b3.shape
    tm, tn = min(tm, m), _pick(nb, 768)
    per = nb // tn

    def body(a_ref, b_ref, o_ref):
        o_ref[...] = _bd(a_ref[...], b_ref[...], 1, 0).astype(out_dtype)

    return pl.pallas_call(
        body, name=name, grid=(m // tm, p * per),
        in_specs=[pl.BlockSpec((tm, kk), lambda i, j: (i, 0)),
                  pl.BlockSpec((None, kk, tn), lambda i, j: (j // per, 0, j % per))],
        out_specs=pl.BlockSpec((tm, tn), lambda i, j: (i, j)),
        out_shape=jax.ShapeDtypeStruct((m, p * nb), out_dtype),
        compiler_params=_params(("parallel", "parallel")))(a, b3)


def matmul_tn(a, b, *, name, tm=2048, out_blocks=None):
    m, k = a.shape
    n = b.shape[1]
    nb = n // (out_blocks or 1)
    tm, tk, tn = min(tm, m), _pick(k), _pick(nb, 768 if out_blocks else 1536)
    per = nb // tn

    def body(a_ref, b_ref, o_ref):
        @pl.when(pl.program_id(2) == 0)
        def _():
            o_ref[...] = jnp.zeros_like(o_ref)

        o_ref[...] += _bd(a_ref[...], b_ref[...], 0, 0)

    if out_blocks:
        out_spec = pl.BlockSpec((None, tk, tn), lambda i, j, t: (j // per, i, j % per))
        out_shape = jax.ShapeDtypeStruct((out_blocks, k, nb), F32)
    else:
        out_spec = pl.BlockSpec((tk, tn), lambda i, j, t: (i, j))
        out_shape = jax.ShapeDtypeStruct((k, n), F32)
    return pl.pallas_call(
        body, name=name, grid=(k // tk, n // tn, m // tm),
        in_specs=[pl.BlockSpec((tm, tk), lambda i, j, t: (t, i)), pl.BlockSpec((tm, tn), lambda i, j, t: (t, j))],
        out_specs=out_spec, out_shape=out_shape,
        compiler_params=_params(("parallel", "parallel", "arbitrary")))(a, b)


def _row_spec(r, tm):
    if isinstance(r, tuple):
        arr, width, blk = r
        return arr, pl.BlockSpec((tm, width), lambda i, blk=blk: (i, blk))
    return r, pl.BlockSpec((tm, r.shape[1]), lambda i: (i, 0))


def _full_spec(p):
    return pl.BlockSpec(p.shape, lambda i: (0,) * p.ndim)


def _expand_rows(rows, tm):
    arrays, specs, counts, widths = [], [], [], []
    for r in rows:
        parts = [_row_spec(p, tm) for p in (r if isinstance(r, list) else [r])]
        arrays += [a for a, _ in parts]
        specs += [s for _, s in parts]
        counts.append(len(parts))
        widths.append(sum(s.block_shape[1] for _, s in parts))
    return arrays, specs, counts, widths


def _row_values(refs, counts):
    vals, k = [], 0
    for c in counts:
        parts = [refs[k + j][...] for j in range(c)]
        vals.append(parts[0] if c == 1 else jnp.concatenate(parts, axis=1))
        k += c
    return vals


def _rows_of(rows):
    first = rows[0][0] if isinstance(rows[0], list) else rows[0]
    return (first[0] if isinstance(first, tuple) else first).shape[0]


def rowmap_fwd(fn, rows, params, outs, *, name, tm=256):
    m = _rows_of(rows)
    tm = min(tm, m)
    arrays, specs, counts, _ = _expand_rows(rows, tm)
    nin, npar = len(arrays), len(params)

    def body(*refs):
        res = fn(*_row_values(refs[:nin], counts), *[r[...] for r in refs[nin:nin + npar]])
        for o_ref, v in zip(refs[nin + npar:], res):
            o_ref[...] = v.astype(o_ref.dtype)

    return pl.pallas_call(
        body, name=name, grid=(m // tm,), in_specs=specs + [_full_spec(p) for p in params],
        out_specs=[pl.BlockSpec((tm, c), lambda i: (i, 0)) for c, _ in outs],
        out_shape=[jax.ShapeDtypeStruct((m, c), dt) for c, dt in outs],
        compiler_params=_params(("parallel",)))(*arrays, *params)


def rowmap_bwd(fn, rows, params, cts, *, name, row_dtypes=None, add=None, tm=256):
    m = _rows_of(rows)
    tm = min(tm, m)
    arrays, specs, counts, widths = _expand_rows(rows, tm)
    cp = [_row_spec(c, tm) for c in cts]
    nin, nr, npar, nc = len(arrays), len(rows), len(params), len(cts)
    row_dtypes = row_dtypes or [F32] * nr

    def body(*refs):
        ins = _row_values(refs[:nin], counts) + [r[...] for r in refs[nin:nin + npar]]
        ins = [v.astype(F32) for v in ins]
        ct = tuple(r[...].astype(F32) for r in refs[nin + npar:nin + npar + nc])
        base = nin + npar + nc
        extra = None
        if add is not None:
            extra = refs[base][...]
            base += 1
        _, pull = jax.vjp(fn, *ins)
        grads = pull(ct)
        for j in range(nr):
            g = grads[j]
            if j == 0 and extra is not None:
                g = g + extra
            refs[base + j][...] = g.astype(refs[base + j].dtype)

        @pl.when(pl.program_id(0) == 0)
        def _():
            for j in range(npar):
                refs[base + nr + j][...] = jnp.zeros_like(refs[base + nr + j])

        for j in range(npar):
            refs[base + nr + j][...] += grads[nr + j]

    in_specs = specs + [_full_spec(p) for p in params] + [s for _, s in cp]
    args = arrays + list(params) + [a for a, _ in cp]
    if add is not None:
        in_specs.append(pl.BlockSpec((tm, widths[0]), lambda i: (i, 0)))
        args.append(add)
    out_specs = [pl.BlockSpec((tm, w), lambda i: (i, 0)) for w in widths] + [_full_spec(p) for p in params]
    out_shape = [jax.ShapeDtypeStruct((m, w), dt) for w, dt in zip(widths, row_dtypes)]
    out_shape += [jax.ShapeDtypeStruct(p.shape, F32) for p in params]
    return pl.pallas_call(
        body, name=name, grid=(m // tm,), in_specs=in_specs, out_specs=out_specs, out_shape=out_shape,
        compiler_params=_params(("arbitrary",)))(*args)


def loss_head(h, target, w, *, name, tm=256, matmul=None):
    m, d = h.shape
    tm = min(tm, m)

    def body(h_ref, t_ref, w_ref, *refs):
        loss_ref, dh_ref, dw_ref = refs[-3:]
        rows = h_ref[...]
        if matmul is not None:
            rows = rows + _bd(refs[0][...], refs[1][...], 1, 0)
        y, pull = jax.vjp(_rms, rows, w_ref[...])
        err = y - t_ref[...]
        dh, dw = pull(err * (1.0 / d))

        @pl.when(pl.program_id(0) == 0)
        def _():
            loss_ref[...] = jnp.zeros_like(loss_ref)
            dw_ref[...] = jnp.zeros_like(dw_ref)

        loss_ref[...] += (0.5 / d) * jnp.sum(err * err, keepdims=True)
        dw_ref[...] += dw
        dh_ref[...] = dh

    row = pl.BlockSpec((tm, d), lambda i: (i, 0))
    in_specs, args = [row, row, _full_spec(w)], [h, target, w]
    if matmul is not None:
        in_specs += [pl.BlockSpec((tm, matmul[0].shape[1]), lambda i: (i, 0)), _full_spec(matmul[1])]
        args += list(matmul)
    return pl.pallas_call(
        body, name=name, grid=(m // tm,), in_specs=in_specs,
        out_specs=[pl.BlockSpec((1, 1), lambda i: (0, 0)), row, _full_spec(w)],
        out_shape=[jax.ShapeDtypeStruct((1, 1), F32), jax.ShapeDtypeStruct((m, d), F32),
                   jax.ShapeDtypeStruct(w.shape, F32)],
        compiler_params=_params(("arbitrary",)))(*args)


def _shift(x, s):
    if s == 0:
        return x
    n = x.shape[0]
    t = lax.broadcasted_iota(jnp.int32, x.shape, 0)
    rolled = pltpu.roll(x, (-s) % n, 0)
    return jnp.where((t + s >= 0) & (t + s < n), rolled, 0.0)


def _conv(x, w, b):
    k = w.shape[0]
    acc = b + w[k // 2:k // 2 + 1, :] * x
    for j in range(k):
        if j != k // 2:
            acc = acc + w[j:j + 1, :] * _shift(x, j - k // 2)
    return acc


def _conv_bwd(x, dc, w):
    k = w.shape[0]
    dx = None
    dws = []
    for j in range(k):
        s = j - k // 2
        term = w[j:j + 1, :] * _shift(dc, -s)
        dx = term if dx is None else dx + term
        dws.append(jnp.sum(dc * _shift(x, s), axis=0, keepdims=True))
    return dx, jnp.concatenate(dws, axis=0), jnp.sum(dc, axis=0, keepdims=True)


def _dsilu(c):
    s = jax.nn.sigmoid(c)
    return s * (1.0 + c * (1.0 - s))


def ssd_conv_fwd(xbc, w, b, *, bsz, name):
    t, c = xbc.shape
    seq, ct = t // bsz, 256

    def body(x_ref, w_ref, b_ref, o_ref):
        o_ref[...] = _silu(_conv(x_ref[...], w_ref[...], b_ref[...]))

    return pl.pallas_call(
        body, name=name, grid=(c // ct, bsz),
        in_specs=[pl.BlockSpec((seq, ct), lambda j, i: (i, j)), pl.BlockSpec((w.shape[0], ct), lambda j, i: (0, j)),
                  pl.BlockSpec((1, ct), lambda j, i: (0, j))],
        out_specs=pl.BlockSpec((seq, ct), lambda j, i: (i, j)),
        out_shape=jax.ShapeDtypeStruct((t, c), F32),
        compiler_params=_params(("parallel", "parallel")))(xbc, w, b)


def ssd_conv_bwd(xbc, dparts, w, b, *, bsz, name):
    t, c = xbc.shape
    seq, ct, k = t // bsz, 256, w.shape[0]
    starts = [0]
    for p in dparts:
        starts.append(starts[-1] + p.shape[1] // ct)

    def body(x_ref, *refs):
        g_refs, (w_ref, b_ref, dx_ref, dw_ref, db_ref) = refs[:len(dparts)], refs[len(dparts):]
        j = pl.program_id(0)

        @pl.when(pl.program_id(1) == 0)
        def _():
            dw_ref[...] = jnp.zeros_like(dw_ref)
            db_ref[...] = jnp.zeros_like(db_ref)

        def run(g_ref):
            x, wv = x_ref[...], w_ref[...]
            dc = g_ref[...] * _dsilu(_conv(x, wv, b_ref[...]))
            dx, dw, db = _conv_bwd(x, dc, wv)
            dx_ref[...] = dx.astype(BF16)
            dw_ref[...] += dw
            db_ref[...] += db

        for n, g_ref in enumerate(g_refs):
            pl.when((j >= starts[n]) & (j < starts[n + 1]))(functools.partial(run, g_ref))

    def part_spec(n):
        lo, hi = starts[n], starts[n + 1]

        def index(j, i):
            inside = (j >= lo) & (j < hi)
            return jnp.where(inside, i, 0), jnp.where(inside, j - lo, 0)

        return pl.BlockSpec((seq, ct), index)

    blk = pl.BlockSpec((seq, ct), lambda j, i: (i, j))
    wspec, bspec = pl.BlockSpec((k, ct), lambda j, i: (0, j)), pl.BlockSpec((1, ct), lambda j, i: (0, j))
    return pl.pallas_call(
        body, name=name, grid=(c // ct, bsz),
        in_specs=[blk] + [part_spec(n) for n in range(len(dparts))] + [wspec, bspec], out_specs=[blk, wspec, bspec],
        out_shape=[jax.ShapeDtypeStruct((t, c), BF16), jax.ShapeDtypeStruct((k, c), F32),
                   jax.ShapeDtypeStruct((1, c), F32)],
        compiler_params=_params(("parallel", "arbitrary")))(xbc, *dparts, w, b)


def _ffn_specs(seq, ct, k, nblk):
    val = pl.BlockSpec((seq, ct), lambda j, i: (i, j))
    gate = pl.BlockSpec((seq, ct), lambda j, i: (i, nblk + j))
    wv, wg = pl.BlockSpec((k, ct), lambda j, i: (0, j)), pl.BlockSpec((k, ct), lambda j, i: (0, nblk + j))
    bv, bg = pl.BlockSpec((1, ct), lambda j, i: (0, j)), pl.BlockSpec((1, ct), lambda j, i: (0, nblk + j))
    return val, gate, wv, wg, bv, bg


def ffn_act_fwd(up, w, b, *, bsz, name):
    t = up.shape[0]
    half = up.shape[1] // 2
    seq, ct, k = t // bsz, 256, w.shape[0]
    val, gate, wv, wg, bv, bg = _ffn_specs(seq, ct, k, half // ct)

    def body(v_ref, g_ref, wv_ref, wg_ref, bv_ref, bg_ref, o_ref):
        vc = _conv(v_ref[...].astype(F32), wv_ref[...], bv_ref[...])
        gc = _conv(g_ref[...].astype(F32), wg_ref[...], bg_ref[...])
        o_ref[...] = (_silu(gc) * vc).astype(BF16)

    return pl.pallas_call(
        body, name=name, grid=(half // ct, bsz), in_specs=[val, gate, wv, wg, bv, bg], out_specs=val,
        out_shape=jax.ShapeDtypeStruct((t, half), BF16),
        compiler_params=_params(("parallel", "parallel")))(up, up, w, w, b, b)


def ffn_act_bwd(up, dact, w, b, *, bsz, name):
    t = up.shape[0]
    half = up.shape[1] // 2
    seq, ct, k = t // bsz, 256, w.shape[0]
    val, gate, wv, wg, bv, bg = _ffn_specs(seq, ct, k, half // ct)

    def body(v_ref, g_ref, wv_ref, wg_ref, bv_ref, bg_ref, d_ref, dv_ref, dg_ref, dwv_ref, dwg_ref, dbv_ref, dbg_ref):
        v, g = v_ref[...].astype(F32), g_ref[...].astype(F32)
        vc = _conv(v, wv_ref[...], bv_ref[...])
        gc = _conv(g, wg_ref[...], bg_ref[...])
        d = d_ref[...].astype(F32)
        sg = jax.nn.sigmoid(gc)
        dv, dwv, dbv = _conv_bwd(v, d * (gc * sg), wv_ref[...])
        dg, dwg, dbg = _conv_bwd(g, d * vc * (sg * (1.0 + gc * (1.0 - sg))), wg_ref[...])
        dv_ref[...] = dv.astype(BF16)
        dg_ref[...] = dg.astype(BF16)

        @pl.when(pl.program_id(1) == 0)
        def _():
            for r in (dwv_ref, dwg_ref, dbv_ref, dbg_ref):
                r[...] = jnp.zeros_like(r)

        dwv_ref[...] += dwv
        dwg_ref[...] += dwg
        dbv_ref[...] += dbv
        dbg_ref[...] += dbg

    return pl.pallas_call(
        body, name=name, grid=(half // ct, bsz), in_specs=[val, gate, wv, wg, bv, bg, val],
        out_specs=[val, val, wv, wv, bv, bv],
        out_shape=[jax.ShapeDtypeStruct((t, half), BF16), jax.ShapeDtypeStruct((t, half), BF16),
                   jax.ShapeDtypeStruct((k, half), F32), jax.ShapeDtypeStruct((k, half), F32),
                   jax.ShapeDtypeStruct((1, half), F32), jax.ShapeDtypeStruct((1, half), F32)],
        compiler_params=_params(("parallel", "arbitrary")))(up, up, w, w, b, b, dact)


def _sel_row(a, h):
    oh = (lax.broadcasted_iota(jnp.int32, (a.shape[0], 1), 0) == h).astype(F32)
    return jnp.sum(a * oh, axis=0, keepdims=True)


def _ssd_chunk(xp, dtr, bm, cm, prev, bias_r, alog_r, dskip_r, rev):
    q = dtr.shape[1]
    ri = lax.broadcasted_iota(jnp.int32, (q, q), 0)
    ci = lax.broadcasted_iota(jnp.int32, (q, q), 1)
    mask = (ci >= ri) if rev else (ci <= ri)
    lane_lo, row_lo = ci < HDIM, ri < HDIM
    dt_r = _softplus(dtr + bias_r)
    dta_r = dt_r * (-jnp.exp(alog_r))
    cs_r = cum_row(dta_r, rev)
    scores = dot_nt(cm, bm)

    def per_row(v):
        return jnp.broadcast_to(v, (q, q)).T

    assert len(xp) == 2
    y_diag, csqs, decayed, tots = [], [], [], []
    for p in range(2):
        ha = 2 * p + (HPG if rev else 0)
        hb = ha + 1
        cs_a, cs_b = _sel_row(cs_r, ha), _sel_row(cs_r, hb)
        csq_a, csq_b = per_row(cs_a), per_row(cs_b)
        seg_a = jnp.exp(jnp.where(mask, csq_a - cs_a, -1e30))
        seg_b = jnp.exp(jnp.where(mask, csq_b - cs_b, -1e30))
        csq = jnp.where(lane_lo, csq_a, csq_b)
        xdt = xp[p] * jnp.where(lane_lo, per_row(_sel_row(dt_r, ha)), per_row(_sel_row(dt_r, hb)))
        tot_a = jnp.sum(_sel_row(dta_r, ha), axis=1, keepdims=True)
        tot_b = jnp.sum(_sel_row(dta_r, hb), axis=1, keepdims=True)
        y_diag.append(jnp.where(lane_lo, *dot2_nn(scores * seg_a, scores * seg_b, xdt)))
        csqs.append(csq)
        decayed.append(xdt * jnp.exp(jnp.where(lane_lo, tot_a, tot_b) - csq))
        tots.append((tot_a, tot_b, ha, hb))
    y_off = dot_nt2(cm, *prev)
    states = dot_tn2(*decayed, bm)
    ys, news = [], []
    for p, (tot_a, tot_b, ha, hb) in enumerate(tots):
        y = y_diag[p] + y_off[p] * jnp.exp(csqs[p])
        if not rev:
            y = y + jnp.where(lane_lo, _sel_row(dskip_r, ha), _sel_row(dskip_r, hb)) * xp[p]
        ys.append(y)
        news.append(jnp.exp(jnp.where(row_lo, tot_a, tot_b)) * prev[p] + states[p])
    return tuple(ys), tuple(news)


NPAIR = HPG // 2


def _ssd_specs(seq, nc):
    xs = pl.BlockSpec((None, seq, HPG * HDIM), lambda b, g: (b, 0, g))
    bm = pl.BlockSpec((None, seq, NSTATE), lambda b, g: (b, 0, SSD_W // NSTATE + g))
    cm = pl.BlockSpec((None, seq, NSTATE), lambda b, g: (b, 0, SSD_W // NSTATE + SGROUPS + g))
    dtr = pl.BlockSpec((None, None, 2 * HPG, seq), lambda b, g: (b, g, 0, 0))
    pr = pl.BlockSpec((None, 2 * HPG, 1), lambda b, g: (g, 0, 0))
    st = pl.BlockSpec((None, None, 2, nc, NPAIR, 2 * HDIM, NSTATE), lambda b, g: (b, g, 0, 0, 0, 0, 0))
    return xs, bm, cm, dtr, pr, st


def _pair_cols(p):
    return slice(2 * HDIM * p, 2 * HDIM * (p + 1))


def ssd_scan_fwd(act, dtr, prs, *, name):
    bsz, seq, _ = act.shape
    nc = seq // QC
    xs, bm, cm, dtrs, pr, st = _ssd_specs(seq, nc)

    def body(x_ref, b_ref, c_ref, dtr_ref, br_ref, ar_ref, dk_ref, y_ref, st_ref):
        par = (br_ref[...], ar_ref[...], dk_ref[...])
        y_ref[...] = jnp.zeros_like(y_ref)

        def step(i, carry):
            new = []
            for rev in (False, True):
                k = (nc - 1 - i) if rev else i
                rows = pl.ds(pl.multiple_of(k * QC, QC), QC)
                xp = tuple(x_ref[rows, _pair_cols(p)] for p in range(NPAIR))
                for p in range(NPAIR):
                    st_ref[int(rev), k, p] = carry[rev][p]
                ys, nw = _ssd_chunk(xp, dtr_ref[:, rows], b_ref[rows, :], c_ref[rows, :], carry[rev], *par, rev)
                for p in range(NPAIR):
                    y_ref[rows, _pair_cols(p)] += ys[p]
                new.append(nw)
            return tuple(new)

        zero = tuple(jnp.zeros((2 * HDIM, NSTATE), F32) for _ in range(NPAIR))
        lax.fori_loop(0, nc // 2, lambda i, c: step(2 * i + 1, step(2 * i, c)), (zero, zero))

    return pl.pallas_call(
        body, name=name, grid=(bsz, SGROUPS), in_specs=[xs, bm, cm, dtrs, pr, pr, pr], out_specs=[xs, st],
        out_shape=[jax.ShapeDtypeStruct((bsz, seq, SSD_W), F32),
                   jax.ShapeDtypeStruct((bsz, SGROUPS, 2, nc, NPAIR, 2 * HDIM, NSTATE), F32)],
        compiler_params=_params(("parallel", "parallel")))(act, act, act, dtr, *prs)


def ssd_scan_bwd(act, dtr, prs, states, dy, *, name):
    bsz, seq, _ = act.shape
    nc = seq // QC
    xs, bm, cm, dtrs, pr, st = _ssd_specs(seq, nc)
    grp = pl.BlockSpec((None, seq, NSTATE), lambda b, g: (b, 0, g))
    dpr = pl.BlockSpec((None, None, 2 * HPG, 1), lambda b, g: (b, g, 0, 0))

    def body(x_ref, b_ref, c_ref, dtr_ref, br_ref, ar_ref, dk_ref, st_ref, dy_ref,
             dx_ref, db_ref, dc_ref, ddtr_ref, gbr_ref, gar_ref, gdk_ref):
        par = (br_ref[...], ar_ref[...], dk_ref[...])
        pgrads = (gbr_ref, gar_ref, gdk_ref)
        for r in pgrads + (dx_ref, db_ref, dc_ref, ddtr_ref):
            r[...] = jnp.zeros_like(r)

        def bstep(i, dcarry):
            new = []
            for rev in (False, True):
                k = i if rev else (nc - 1 - i)
                rows = pl.ds(pl.multiple_of(k * QC, QC), QC)
                xp = tuple(x_ref[rows, _pair_cols(p)] for p in range(NPAIR))
                prev = tuple(st_ref[int(rev), k, p] for p in range(NPAIR))
                _, pull = jax.vjp(functools.partial(_ssd_chunk, rev=rev), xp, dtr_ref[:, rows], b_ref[rows, :],
                                  c_ref[rows, :], prev, *par)
                dyp = tuple(dy_ref[rows, _pair_cols(p)] for p in range(NPAIR))
                gx, gdt, gb, gc, gprev, *gpar = pull((dyp, dcarry[rev]))
                for p in range(NPAIR):
                    dx_ref[rows, _pair_cols(p)] += gx[p]
                ddtr_ref[:, rows] += gdt
                db_ref[rows, :] += gb
                dc_ref[rows, :] += gc
                for r, g in zip(pgrads, gpar):
                    r[...] += g
                new.append(gprev)
            return tuple(new)

        zero = tuple(jnp.zeros((2 * HDIM, NSTATE), F32) for _ in range(NPAIR))
        lax.fori_loop(0, nc, bstep, (zero, zero))

    out_shape = [jax.ShapeDtypeStruct((bsz, seq, SSD_W), F32),
                 jax.ShapeDtypeStruct((bsz, seq, SGROUPS * NSTATE), F32),
                 jax.ShapeDtypeStruct((bsz, seq, SGROUPS * NSTATE), F32),
                 jax.ShapeDtypeStruct(dtr.shape, F32)]
    out_shape += [jax.ShapeDtypeStruct((bsz, SGROUPS, 2 * HPG, 1), F32)] * 3
    return pl.pallas_call(
        body, name=name, grid=(bsz, SGROUPS), in_specs=[xs, bm, cm, dtrs, pr, pr, pr, st, xs],
        out_specs=[xs, grp, grp, dtrs, dpr, dpr, dpr], out_shape=out_shape,
        compiler_params=_params(("parallel", "parallel")))(act, act, act, dtr, *prs, states, dy)


def _s5_core(lam_re, lam_im, log_step, b_re, b_im, c_re, c_im):
    q = S5_Q
    step = jnp.exp(log_step)[:, None]
    lr, li = lam_re * step, lam_im * step
    mag = jnp.exp(lr)
    ar, ai = mag * jnp.cos(li), mag * jnp.sin(li)
    den = lam_re * lam_re + lam_im * lam_im
    cr = ((ar - 1.0) * lam_re + ai * lam_im) / den
    ci = (ai * lam_re - (ar - 1.0) * lam_im) / den
    bbr = cr[..., None] * b_re - ci[..., None] * b_im
    bbi = cr[..., None] * b_im + ci[..., None] * b_re
    d = jnp.arange(q + 1, dtype=F32)[None, :, None]
    pm = jnp.exp(d * lr[:, None, :])
    pr, pi = pm * jnp.cos(d * li[:, None, :]), pm * jnp.sin(d * li[:, None, :])
    er = pr[..., None] * bbr[:, None] - pi[..., None] * bbi[:, None]
    ei = pr[..., None] * bbi[:, None] + pi[..., None] * bbr[:, None]
    hp = lax.Precision.HIGHEST
    k = (jnp.einsum('gcp,gdpz->gdcz', c_re, er[:, :q], precision=hp)
         - jnp.einsum('gcp,gdpz->gdcz', c_im, ei[:, :q], precision=hp))
    e = jnp.concatenate([er[:, :q], ei[:, :q]], axis=2)
    p1r, p1i = pr[:, 1:], pi[:, 1:]
    m_re = c_re[:, None] * p1r[:, :, None, :] - c_im[:, None] * p1i[:, :, None, :]
    m_im = -c_re[:, None] * p1i[:, :, None, :] - c_im[:, None] * p1r[:, :, None, :]
    da = jnp.concatenate([pr[:, q], pr[:, q]], axis=-1)
    db = jnp.concatenate([-pi[:, q], pi[:, q]], axis=-1)
    return k, e, jnp.concatenate([m_re, m_im], axis=-1), da, db


def _s5_operators(lf_re, lf_im, lsf, lb_re, lb_im, lsb, b_re, b_im, cf_re, cf_im, cb_re, cb_im):
    g = lf_re.shape[0]
    both = lambda f, b: jnp.concatenate([f, b], axis=0)
    k, e, m, da, db = _s5_core(both(lf_re, lb_re), both(lf_im, lb_im), both(lsf, lsb), both(b_re, b_re),
                               both(b_im, b_im), both(cf_re, cb_re), both(cf_im, cb_im))
    kf, kb = k[:g], k[g:]
    wtf, wtb = jnp.transpose(e[:g, ::-1], (0, 1, 3, 2)), jnp.transpose(e[g:], (0, 1, 3, 2))
    mtf, mtb = jnp.transpose(m[:g], (0, 3, 1, 2)), jnp.transpose(m[g:, ::-1], (0, 3, 1, 2))
    daf, dab, dbf, dbb = da[:g], da[g:], db[:g], db[g:]
    lags = jnp.concatenate([kb[:, :0:-1], kf[:, :1] + kb[:, :1], kf[:, 1:]], axis=1)
    tt = jnp.transpose(lags, (0, 1, 3, 2))
    wt = jnp.concatenate([wtf.reshape(g, S5_QC, 2 * S5_P), wtb.reshape(g, S5_QC, 2 * S5_P)], axis=-1)
    mt = jnp.concatenate([mtf.reshape(g, 2 * S5_P, S5_QC), mtb.reshape(g, 2 * S5_P, S5_QC)], axis=1)
    return tt, wt, mt, jnp.concatenate([daf, dab], -1), jnp.concatenate([dbf, dbb], -1)


def _gspec(*shape):
    return pl.BlockSpec((None,) + shape, lambda g: (g,) + (0,) * len(shape))


S5_HALVES = S5_QC // LANES


def _toeplitz_block(s, t):
    per = LANES // S5_C
    return t // per, slice(s * S5_C, (s + 1) * S5_C), slice((t % per) * S5_C, (t % per + 1) * S5_C)


def s5_toeplitz(kt, *, name):
    g = kt.shape[0]

    def body(k_ref, t_ref):
        for s in range(S5_Q):
            for t in range(S5_Q):
                t_ref[_toeplitz_block(s, t)] = k_ref[t - s + S5_Q - 1]

    return pl.pallas_call(
        body, name=name, grid=(g,), in_specs=[_gspec(2 * S5_Q - 1, S5_C, S5_C)],
        out_specs=_gspec(S5_HALVES, S5_QC, LANES), out_shape=jax.ShapeDtypeStruct((g, S5_HALVES, S5_QC, LANES), F32),
        compiler_params=_params(("parallel",)))(kt)


def s5_toeplitz_bwd(dtt, *, name):
    g = dtt.shape[0]

    def body(d_ref, k_ref):
        for j in range(2 * S5_Q - 1):
            acc = None
            for s in range(S5_Q):
                t = j - (S5_Q - 1) + s
                if 0 <= t < S5_Q:
                    blk = d_ref[_toeplitz_block(s, t)]
                    acc = blk if acc is None else acc + blk
            k_ref[j] = acc

    return pl.pallas_call(
        body, name=name, grid=(g,), in_specs=[_gspec(S5_HALVES, S5_QC, LANES)],
        out_specs=_gspec(2 * S5_Q - 1, S5_C, S5_C), out_shape=jax.ShapeDtypeStruct((g, 2 * S5_Q - 1, S5_C, S5_C), F32),
        compiler_params=_params(("parallel",)))(dtt)


S5_RT = 128


def _chunk_piece(q):
    per = LANES // S5_C
    return q // per, slice((q % per) * S5_C, (q % per + 1) * S5_C)


def to_chunks(u, *, name):
    t = u.shape[0]
    r = t // S5_Q
    rt = min(S5_RT, r)

    per = LANES // S5_C
    nblk = S5_W // LANES

    def body(*refs):
        o_ref = refs[-1]
        for k in range(nblk):
            for q in range(S5_Q):
                rows = refs[k][pl.ds(q, rt, stride=S5_Q), :]
                half, lanes = _chunk_piece(q)
                for j in range(per):
                    o_ref[k * per + j, half, :, lanes] = rows[:, j * S5_C:(j + 1) * S5_C]

    return pl.pallas_call(
        body, name=name, grid=(r // rt,),
        in_specs=[pl.BlockSpec((rt * S5_Q, LANES), lambda i, k=k: (i, k)) for k in range(nblk)],
        out_specs=pl.BlockSpec((S5_G, S5_HALVES, rt, LANES), lambda i: (0, 0, i, 0)),
        out_shape=jax.ShapeDtypeStruct((S5_G, S5_HALVES, r, LANES), F32),
        compiler_params=_params(("parallel",)))(*[u] * nblk)


def from_chunks(y, *, name, add=None, as_blocks=False):
    r = y.shape[2]
    rt = min(S5_RT, r)
    per = LANES // S5_C

    nblk = S5_W // LANES

    def body(*refs):
        y_ref, tmp_ref = refs[0], refs[-1]
        adds, outs = refs[1:-1 - nblk], refs[-1 - nblk:-1]
        for k in range(nblk):
            for q in range(S5_Q):
                half, lanes = _chunk_piece(q)
                for j in range(per):
                    tmp_ref[:, j * S5_C:(j + 1) * S5_C] = y_ref[k * per + j, half, :, lanes]
                row = tmp_ref[...]
                if add is not None:
                    row = row + adds[k][pl.ds(q, rt, stride=S5_Q), :]
                outs[k][pl.ds(q, rt, stride=S5_Q), :] = row

    in_specs = [pl.BlockSpec((S5_G, S5_HALVES, rt, LANES), lambda i: (0, 0, i, 0))]
    if add is not None:
        in_specs += [pl.BlockSpec((rt * S5_Q, LANES), lambda i, k=k: (i, k)) for k in range(nblk)]
    blocks = pl.pallas_call(
        body, name=name, grid=(r // rt,), in_specs=in_specs,
        out_specs=[pl.BlockSpec((rt * S5_Q, LANES), lambda i: (i, 0))] * nblk,
        out_shape=[jax.ShapeDtypeStruct((r * S5_Q, LANES), F32)] * nblk,
        scratch_shapes=[pltpu.VMEM((rt, LANES), F32)],
        compiler_params=_params(("parallel",)))(*([y] if add is None else [y] + [add] * nblk))
    return list(blocks) if as_blocks else jnp.concatenate(blocks, axis=1)


def _cat(ref):
    return jnp.concatenate([ref[h] for h in range(S5_HALVES)], axis=1)


def _put(ref, v):
    for h in range(S5_HALVES):
        ref[h] = v[:, h * LANES:(h + 1) * LANES]


def _mspec(gp, *shape):
    return pl.BlockSpec((gp,) + shape, lambda i: (i,) + (0,) * len(shape))


def _carry_spec(nck):
    return pl.BlockSpec((nck, 8, 4 * S5_P), lambda i: (0, i, 0))


def _carry_rows(ref, gl, bsz):
    return jnp.concatenate([ref[:, gl * bsz + b, :] for b in range(bsz)], axis=0)


def _carry_put(ref, gl, bsz, v):
    nck = v.shape[0] // bsz
    for b in range(bsz):
        ref[:, gl * bsz + b, :] = v[b * nck:(b + 1) * nck, :]


def s5_state_in(u, wt, *, bsz, name):
    g, _, r, _ = u.shape
    gp, nck = 8 // bsz, r // bsz

    def body(u_ref, w_ref, o_ref):
        for gl in range(gp):
            _carry_put(o_ref, gl, bsz, _bd(_cat(u_ref.at[gl]), w_ref[gl], 1, 0))

    return pl.pallas_call(
        body, name=name, grid=(g // gp,), in_specs=[_mspec(gp, S5_HALVES, r, LANES), _mspec(gp, S5_QC, 4 * S5_P)],
        out_specs=_carry_spec(nck), out_shape=jax.ShapeDtypeStruct((nck, g * bsz, 4 * S5_P), F32),
        compiler_params=_params(("parallel",)))(u, wt)


def _swap(h):
    return pltpu.roll(h, S5_P, 1)


def s5_carry_fwd(s, da, db, *, name):
    nck, rows, _ = s.shape
    w = 2 * S5_P

    def body(s_ref, da_ref, db_ref, h_ref):
        dirs = ((False, slice(0, w)), (True, slice(w, 2 * w)))
        coef = [(da_ref[:, cols], db_ref[:, cols]) for _, cols in dirs]

        def step(i, hs):
            new = []
            for (rev, cols), (a, b), h in zip(dirs, coef, hs):
                k = (nck - 1 - i) if rev else i
                h_ref[k, :, cols] = h
                new.append(a * h + b * _swap(h) + s_ref[k, :, cols])
            return tuple(new)

        z = jnp.zeros((rows, w), F32)
        lax.fori_loop(0, nck, step, (z, z), unroll=2)

    rt = min(2 * CARRY_ROWS, rows)
    big, small = pl.BlockSpec((nck, rt, 2 * w), lambda i: (0, i, 0)), pl.BlockSpec((rt, 2 * w), lambda i: (i, 0))
    rows = rt
    return pl.pallas_call(
        body, name=name, grid=(s.shape[1] // rt,), in_specs=[big, small, small], out_specs=big,
        out_shape=jax.ShapeDtypeStruct(s.shape, F32), compiler_params=_params(("parallel",)))(s, da, db)


def s5_carry_bwd(hin, dh, da, db, *, name):
    nck, rows, _ = hin.shape
    w = 2 * S5_P

    def body(h_ref, dh_ref, da_ref, db_ref, ds_ref, gda_ref, gdb_ref):
        dirs = ((False, slice(0, w)), (True, slice(w, 2 * w)))
        coef = [(da_ref[:, cols], db_ref[:, cols]) for _, cols in dirs]

        def step(i, carries):
            new = []
            for (rev, cols), (a, b), (g, ga, gb) in zip(dirs, coef, carries):
                k = i if rev else (nck - 1 - i)
                ds_ref[k, :, cols] = g
                h = h_ref[k, :, cols]
                new.append((dh_ref[k, :, cols] + a * g + _swap(b * g), ga + g * h, gb + g * _swap(h)))
            return tuple(new)

        z = jnp.zeros((rows, w), F32)
        res = lax.fori_loop(0, nck, step, ((z, z, z), (z, z, z)), unroll=2)
        for (_, cols), (_, ga, gb) in zip(dirs, res):
            gda_ref[:, cols] = ga
            gdb_ref[:, cols] = gb

    rt = min(CARRY_ROWS, rows)
    big, small = pl.BlockSpec((nck, rt, 2 * w), lambda i: (0, i, 0)), pl.BlockSpec((rt, 2 * w), lambda i: (i, 0))
    rows = rt
    return pl.pallas_call(
        body, name=name, grid=(hin.shape[1] // rt,), in_specs=[big, big, small, small], out_specs=[big, small, small],
        out_shape=[jax.ShapeDtypeStruct(hin.shape, F32), jax.ShapeDtypeStruct(da.shape, F32),
                   jax.ShapeDtypeStruct(da.shape, F32)],
        compiler_params=_params(("parallel",)))(hin, dh, da, db)


def s5_out(u, hin, tt, mt, *, bsz, name):
    g, _, r, _ = u.shape
    gp, nck = 8 // bsz, r // bsz

    def body(u_ref, h_ref, t_ref, m_ref, o_ref):
        for gl in range(gp):
            u_v, h_v = _cat(u_ref.at[gl]), _carry_rows(h_ref, gl, bsz)
            for half in range(S5_HALVES):
                cols = slice(half * LANES, (half + 1) * LANES)
                o_ref[gl, half] = _bd(u_v, t_ref[gl, half], 1, 0) + _bd(h_v, m_ref[gl, :, cols], 1, 0)

    cspec = _mspec(gp, S5_HALVES, r, LANES)
    return pl.pallas_call(
        body, name=name, grid=(g // gp,),
        in_specs=[cspec, _carry_spec(nck), _mspec(gp, S5_HALVES, S5_QC, LANES), _mspec(gp, 4 * S5_P, S5_QC)],
        out_specs=cspec, out_shape=jax.ShapeDtypeStruct((g, S5_HALVES, r, LANES), F32),
        compiler_params=_params(("parallel",)))(u, hin, tt, mt)


def s5_out_bwd(dy, u, hin, tt, mt, *, bsz, name):
    g, _, r, _ = u.shape
    gp, nck = 8 // bsz, r // bsz

    def body(dy_ref, u_ref, h_ref, t_ref, m_ref, dh_ref, dt_ref, dm_ref, du_ref):
        for gl in range(gp):
            dy_v, u_v = _cat(dy_ref.at[gl]), _cat(u_ref.at[gl])
            _carry_put(dh_ref, gl, bsz, _bd(dy_v, m_ref[gl], 1, 1))
            dm_ref[gl] = _bd(_carry_rows(h_ref, gl, bsz), dy_v, 0, 0)
            du = None
            for half in range(S5_HALVES):
                dy_h = dy_ref[gl, half]
                dt_ref[gl, half] = _bd(u_v, dy_h, 0, 0)
                part = _bd(dy_h, t_ref[gl, half], 1, 1)
                du = part if du is None else du + part
            _put(du_ref.at[gl], du)

    cspec, tspec = _mspec(gp, S5_HALVES, r, LANES), _mspec(gp, S5_HALVES, S5_QC, LANES)
    mspec = _mspec(gp, 4 * S5_P, S5_QC)
    return pl.pallas_call(
        body, name=name, grid=(g // gp,),
        in_specs=[cspec, cspec, _carry_spec(nck), tspec, mspec],
        out_specs=[_carry_spec(nck), tspec, mspec, cspec],
        out_shape=[jax.ShapeDtypeStruct((nck, g * bsz, 4 * S5_P), F32),
                   jax.ShapeDtypeStruct((g, S5_HALVES, S5_QC, LANES), F32),
                   jax.ShapeDtypeStruct((g, 4 * S5_P, S5_QC), F32), jax.ShapeDtypeStruct((g, S5_HALVES, r, LANES), F32)],
        compiler_params=_params(("parallel",)))(dy, u, hin, tt, mt)


def s5_state_in_bwd(ds, u, wt, du1, *, bsz, name):
    g, _, r, _ = u.shape
    gp, nck = 8 // bsz, r // bsz

    def body(ds_ref, u_ref, w_ref, du1_ref, du_ref, dw_ref):
        for gl in range(gp):
            ds_v = _carry_rows(ds_ref, gl, bsz)
            _put(du_ref.at[gl], _cat(du1_ref.at[gl]) + _bd(ds_v, w_ref[gl], 1, 1))
            dw_ref[gl] = _bd(_cat(u_ref.at[gl]), ds_v, 0, 0)

    cspec, wspec = _mspec(gp, S5_HALVES, r, LANES), _mspec(gp, S5_QC, 4 * S5_P)
    return pl.pallas_call(
        body, name=name, grid=(g // gp,),
        in_specs=[_carry_spec(nck), cspec, wspec, cspec], out_specs=[cspec, wspec],
        out_shape=[jax.ShapeDtypeStruct((g, S5_HALVES, r, LANES), F32), jax.ShapeDtypeStruct((g, S5_QC, 4 * S5_P), F32)],
        compiler_params=_params(("parallel",)))(ds, u, wt, du1)


def _s5_post(ypre, u, dvec, wv, wg, bv, bg, nw):
    g = _gelu(ypre + dvec * u)
    out = (dot_nn(g, wv) + bv) * jax.nn.sigmoid(dot_nn(g, wg) + bg)
    return (_rms(out, nw),)


def _ssd_post(y, z, nw):
    return (_rms(y * _silu(z), nw),)


def _block_diag(w):
    eye = jnp.eye(S5_G, dtype=w.dtype)
    return jnp.einsum('gcd,gh->gchd', w, eye).reshape(S5_W, S5_W)


def _diag_blocks(w):
    v = w.reshape(S5_G, S5_C, S5_G, S5_C)
    return v[jnp.arange(S5_G), :, jnp.arange(S5_G), :]


def _dt_rows(dt, bsz):
    seq = dt.shape[0] // bsz
    return jnp.transpose(dt.reshape(bsz, seq, 2, SGROUPS, HPG), (0, 3, 2, 4, 1)).reshape(bsz, SGROUPS, 2 * HPG, seq)


def _dt_from_rows(dr):
    bsz, _, _, seq = dr.shape
    return jnp.transpose(dr.reshape(bsz, SGROUPS, 2, HPG, seq), (0, 4, 2, 1, 3)).reshape(bsz * seq, 2 * HEADS)


def _head_params(f, b):
    return jnp.concatenate([f.reshape(SGROUPS, HPG), b.reshape(SGROUPS, HPG)], axis=1)[:, :, None]


def _head_grads(gr):
    v = gr.sum(0)[:, :, 0]
    return v[:, :HPG].reshape(HEADS), v[:, HPG:].reshape(HEADS)


def local_step(x, target, w):
    bsz, seq, d = x.shape
    t = bsz * seq
    x2, tgt2 = x.reshape(t, d), target.reshape(t, d)
    g = {}
    row = lambda v: v.reshape(1, -1)
    bf = lambda v: v.astype(BF16)

    w_in = _unshard(bf(w['w_in']), SHARDED['w_in'])
    cuts = [0, SSD_W, SSD_W + XBC_W, SSD_W + XBC_W + 2 * HEADS, w_in.shape[1]]
    w_in_parts = [w_in[:, a:b] for a, b in zip(cuts[:-1], cuts[1:])]
    norm_mix = row(w['norm_mix_w']) + w.get('token', 0.0)
    (hn,) = rowmap_fwd(lambda a, nw: (_rms(a, nw),), [x2], [norm_mix], [(d, BF16)], tm=512, name="rms_mix")
    z, xbc, dt, u = matmul_multi(hn, w_in_parts, name="in_proj")

    conv_w, conv_b = _unshard(w['ssd_conv_w'], SHARDED['ssd_conv_w']), row(w['ssd_conv_b'])
    act = ssd_conv_fwd(xbc, conv_w, conv_b, bsz=bsz, name="ssd_conv")
    dtr = _dt_rows(dt, bsz)
    prs = (_head_params(w['ssd_dt_bias_fwd'], w['ssd_dt_bias_bwd']),
           _head_params(w['ssd_a_log_fwd'], w['ssd_a_log_bwd']),
           _head_params(w['ssd_d'], jnp.zeros_like(w['ssd_d'])))
    act3 = act.reshape(bsz, seq, XBC_W)
    y_scan, ssd_states = ssd_scan_fwd(act3, dtr, prs, name="ssd_scan")
    y_scan = y_scan.reshape(t, SSD_W)
    ssd_nw = row(w['ssd_norm_w'])
    (y_ssd,) = rowmap_fwd(_ssd_post, [y_scan, z], [ssd_nw], [(SSD_W, BF16)], tm=512, name="ssd_post")

    s5_names = ['s5_lambda_re_fwd', 's5_lambda_im_fwd', 's5_log_step_fwd', 's5_lambda_re_bwd', 's5_lambda_im_bwd',
                's5_log_step_bwd', 's5_b_re', 's5_b_im', 's5_c_re_fwd', 's5_c_im_fwd', 's5_c_re_bwd', 's5_c_im_bwd']
    (kt, wt, mt, da, db), s5_pull = jax.vjp(_s5_operators, *[w[n] for n in s5_names])
    tt_b, wt_b, mt_b = s5_toeplitz(kt, name="s5_toeplitz"), bf(wt), bf(mt)
    da_r, db_r = jnp.repeat(da, bsz, axis=0), jnp.repeat(db, bsz, axis=0)
    uc = to_chunks(u, name="s5_to_chunks_u")
    hin = s5_carry_fwd(s5_state_in(uc, wt_b, bsz=bsz, name="s5_state_in"), da_r, db_r, name="s5_carry")
    ypre = from_chunks(s5_out(uc, hin, tt_b, mt_b, bsz=bsz, name="s5_out"), name="s5_from_chunks_y", as_blocks=True)
    glu_w = w['s5_glu_w']
    s5_par = [row(w['s5_d']), _block_diag(glu_w[:, :, :S5_C]), _block_diag(glu_w[:, :, S5_C:]),
              row(w['s5_glu_b'][:, :S5_C]), row(w['s5_glu_b'][:, S5_C:]), row(w['s5_norm_w'])]
    (y_s5,) = rowmap_fwd(_s5_post, [ypre, u], s5_par, [(S5_W, BF16)], tm=512, name="s5_post")

    if 'late' in w:
        w = {**w, **w['late'](y_s5)}
    w_out = bf(w['w_out']).reshape(SSD_W + S5_W, d)
    norm_ffn = row(w['norm_ffn_w'])
    h1, hn2 = matmul_sum([y_ssd, y_s5], [w_out[:SSD_W], w_out[SSD_W:]], add=x2, norm_w=norm_ffn, name="out_proj")
    pad_c = FFN_PAD - FFN_BLK
    half = N_DEV // 2
    w_up3 = jnp.pad(bf(w['ffn_w_up']), ((0, 0), (0, 0), (0, pad_c)))
    w_down = jnp.pad(bf(w['ffn_w_down']).reshape(half, FFN_BLK, d), ((0, 0), (0, pad_c), (0, 0)))
    w_down = w_down.reshape(half * FFN_PAD, d)
    fconv_w = jnp.pad(w['ffn_conv_w'], ((0, 0), (0, 0), (0, pad_c)))
    fconv_w = jnp.transpose(fconv_w, (1, 0, 2)).reshape(FCONV, N_DEV * FFN_PAD)
    fconv_b = row(jnp.pad(w['ffn_conv_b'].reshape(N_DEV, FFN_BLK), ((0, 0), (0, pad_c))))
    up = matmul_cols(hn2, w_up3, out_dtype=BF16, name="ffn_up")
    fact = ffn_act_fwd(up, fconv_w, fconv_b, bsz=bsz, name="ffn_act")
    loss, dh2, g_nf = loss_head(h1, tgt2, row(w['norm_final_w']), matmul=(fact, w_down), tm=512, name="ffn_down_loss")
    g['norm_final_w'] = g_nf.reshape(-1)

    dfact = matmul_sum([dh2], [w_down], nt=True, tm=1024, out_dtype=BF16, name="ffn_down_dx")
    g_down = matmul_tn(fact, dh2, name="ffn_down_dw").reshape(half, FFN_PAD, d)[:, :FFN_BLK]
    g['ffn_w_down'] = g_down.reshape(N_DEV, FFN_BLK // 2, d)
    dval, dgate, dwv, dwg, dbv, dbg = ffn_act_bwd(up, dfact, fconv_w, fconv_b, bsz=bsz, name="ffn_act_bwd")
    g_cw = jnp.concatenate([dwv, dwg], axis=1).reshape(FCONV, N_DEV, FFN_PAD)[:, :, :FFN_BLK]
    g['ffn_conv_w'] = jnp.transpose(g_cw, (1, 0, 2))
    g['ffn_conv_b'] = jnp.concatenate([dbv, dbg], axis=1).reshape(N_DEV, FFN_PAD)[:, :FFN_BLK].reshape(-1)
    windows = [(dval, FFN_PAD, p) for p in range(half)] + [(dgate, FFN_PAD, p) for p in range(half)]
    g['ffn_w_up'] = jnp.concatenate([matmul_tn(hn2, dval, out_blocks=half, name="ffn_up_dw_val"),
                                     matmul_tn(hn2, dgate, out_blocks=half, name="ffn_up_dw_gate")],
                                    axis=0)[:, :, :FFN_BLK]
    send_early = w.get('on_grads')
    if send_early:
        norm_ffn = norm_ffn + send_early(g, ['ffn_w_up', 'ffn_w_down'])
    dh1, g_nffn = matmul_sum(windows, [(w_up3, p) for p in range(N_DEV)], nt=True, tm=512,
                             norm_bwd=(h1, norm_ffn, dh2), name="ffn_up_dx")
    g['norm_ffn_w'] = g_nffn.reshape(-1)

    dycat = matmul_sum([dh1], [w_out], nt=True, tm=1024, name="out_proj_dx")
    g['w_out'] = jnp.concatenate([matmul_tn(y_ssd, dh1, name="out_proj_dw_ssd"),
                                  matmul_tn(y_s5, dh1, name="out_proj_dw_s5")], axis=0).reshape(w['w_out'].shape)
    if send_early:
        ssd_nw = ssd_nw + send_early(g, ['w_out'])
    dy_scan, dz, g_snw = rowmap_bwd(_ssd_post, [y_scan, z], [ssd_nw], [(dycat, SSD_W, 0)], tm=512,
                                    row_dtypes=[F32, BF16], name="ssd_post_bwd")
    g['ssd_norm_w'] = g_snw.reshape(-1)
    dypre, du_a, g_d, g_wv, g_wg, g_bv, g_bg, g_s5nw = rowmap_bwd(
        _s5_post, [ypre, u], s5_par, [(dycat, S5_W, SSD_W // S5_W)], tm=512, name="s5_post_bwd")
    g['s5_d'], g['s5_norm_w'] = g_d.reshape(-1), g_s5nw.reshape(-1)
    g['s5_glu_w'] = jnp.concatenate([_diag_blocks(g_wv), _diag_blocks(g_wg)], axis=-1)
    g['s5_glu_b'] = jnp.concatenate([g_bv.reshape(S5_G, S5_C), g_bg.reshape(S5_G, S5_C)], axis=-1)

    dyc = to_chunks(dypre, name="s5_to_chunks_dy")
    dhin, dtt, dmt, du1 = s5_out_bwd(dyc, uc, hin, tt_b, mt_b, bsz=bsz, name="s5_out_bwd")
    ds, gda, gdb = s5_carry_bwd(hin, dhin, da_r, db_r, name="s5_carry_bwd")
    duc, dwt = s5_state_in_bwd(ds, uc, wt_b, du1, bsz=bsz, name="s5_state_in_bwd")
    du = from_chunks(duc, add=du_a, name="s5_from_chunks_du")
    fold = lambda v: v.reshape(S5_G, bsz, -1).sum(1)
    dkt = s5_toeplitz_bwd(dtt, name="s5_toeplitz_bwd")
    for n, gv in zip(s5_names, s5_pull((dkt, dwt, dmt, fold(gda), fold(gdb)))):
        g[n] = gv

    dxs, dbm, dcm, ddtr, gbr, gar, gdk = ssd_scan_bwd(
        act3, dtr, prs, ssd_states, dy_scan.reshape(bsz, seq, SSD_W), name="ssd_scan_bwd")
    g['ssd_dt_bias_fwd'], g['ssd_dt_bias_bwd'] = _head_grads(gbr)
    g['ssd_a_log_fwd'], g['ssd_a_log_bwd'] = _head_grads(gar)
    g['ssd_d'] = _head_grads(gdk)[0]
    dparts_act = [v.reshape(t, v.shape[-1]) for v in (dxs, dbm, dcm)]
    dxbc, g_cw, g_cb = ssd_conv_bwd(xbc, dparts_act, conv_w, conv_b, bsz=bsz, name="ssd_conv_bwd")
    g['ssd_conv_w'] = _shard_rows(g_cw, SHARDED['ssd_conv_w']).reshape(w['ssd_conv_w'].shape)
    g['ssd_conv_b'] = g_cb.reshape(-1)
    ddt = _dt_from_rows(ddtr)

    if send_early:
        ddt = ddt + send_early(g, [], loss=loss)
    dparts = [dz, dxbc, ddt, du]
    g_in = jnp.concatenate([matmul_tn(hn, dp, name=f"in_proj_dw_{i}") for i, dp in enumerate(dparts)], axis=1)
    g['w_in'] = _shard_rows(g_in, SHARDED['w_in']).reshape(w['w_in'].shape)
    if send_early:
        dparts[2] = ddt + send_early(g, ['w_in'])
    dx, g_nmix = matmul_sum(dparts, w_in_parts, nt=True, tm=512, norm_bwd=(x2, norm_mix, dh1), name="in_proj_dx")
    g['norm_mix_w'] = g_nmix.reshape(-1)
    return loss, dx.reshape(bsz, seq, d), g


ANY = pl.BlockSpec(memory_space=pl.ANY)


def all_gather(shards, *, name):
    n = len(shards)

    def body(*refs):
        x_refs, out_refs = refs[:n], refs[n:2 * n]
        send_sems, recv_sems, local_sems = refs[2 * n:]
        x, y, c = lax.axis_index("x"), lax.axis_index("y"), lax.axis_index("c")
        me, sibling = (x, y, c), (x, y, 1 - c)
        chips = [(1 - x, y), (x, 1 - y), (1 - x, 1 - y)]

        def copy(k, j, block, to, own=False):
            dst = out_refs[j].at[4 * block[0] + 2 * block[1] + block[2]]
            return pltpu.make_async_remote_copy(
                src_ref=x_refs[j] if own else dst, dst_ref=dst,
                send_sem=send_sems.at[k, j], recv_sem=recv_sems.at[k, j], device_id=to, device_id_type=MESH)

        mine = [pltpu.make_async_copy(x_refs[j], out_refs[j].at[4 * x + 2 * y + c], local_sems.at[j]) for j in range(n)]
        first = [copy(0, j, me, sibling, own=True) for j in range(n)]
        first += [copy(1 + i, j, me, (*chip, c), own=True) for i, chip in enumerate(chips) for j in range(n)]
        for cp in mine + first:
            cp.start()
        passed = []
        for i, chip in enumerate(chips):
            for j in range(n):
                copy(1 + i, j, (*chip, c), me).wait_recv()
                passed.append(copy(4 + i, j, (*chip, c), sibling))
                passed[-1].start()
        for j in range(n):
            copy(0, j, sibling, me).wait_recv()
        for i, chip in enumerate(chips):
            for j in range(n):
                copy(4 + i, j, (*chip, 1 - c), me).wait_recv()
        for cp in first + passed:
            cp.wait_send()
        for cp in mine:
            cp.wait()

    return pl.pallas_call(
        body, name=name, out_shape=[jax.ShapeDtypeStruct((N_DEV,) + s.shape, s.dtype) for s in shards],
        in_specs=[ANY] * n, out_specs=[ANY] * n,
        scratch_shapes=[pltpu.SemaphoreType.DMA((7, n)), pltpu.SemaphoreType.DMA((7, n)),
                        pltpu.SemaphoreType.DMA((n,))],
    )(*shards)


HBM_SPEC = pl.BlockSpec(memory_space=pltpu.HBM)
SEM_SPEC = pl.BlockSpec(memory_space=pltpu.SEMAPHORE)
SPLIT_PARAMS = pltpu.CompilerParams(has_side_effects=pltpu.SideEffectType.DATAFLOW_SIDE_EFFECTING)


def _peer_copies(src_refs, land_refs, send_sems, recv_sems, indexed):
    x, y, c = lax.axis_index("x"), lax.axis_index("y"), lax.axis_index("c")
    me = 4 * x + 2 * y + c
    copies = []
    for k in range(1, N_DEV):
        px = (1 - x) if k & 4 else x
        py = (1 - y) if k & 2 else y
        pc = (1 - c) if k & 1 else c
        for j, (src, land) in enumerate(zip(src_refs, land_refs)):
            sem = (k - 1) * len(src_refs) + j
            copies.append(pltpu.make_async_remote_copy(
                src_ref=src.at[4 * px + 2 * py + pc] if indexed else src, dst_ref=land.at[me],
                send_sem=send_sems.at[sem], recv_sem=recv_sems.at[sem],
                device_id=(px, py, pc), device_id_type=MESH))
    return copies


def scatter_start(srcs, *, name, indexed):
    n = len(srcs)
    lands = [lax.empty(s.shape if indexed else (N_DEV,) + s.shape, s.dtype) for s in srcs]

    def body(*refs):
        send_sems, recv_sems = refs[2 * n], refs[2 * n + 1]
        for cp in _peer_copies(refs[:n], refs[n:2 * n], send_sems, recv_sems, indexed):
            cp.start()
        refs[-1][...] = jnp.zeros_like(refs[-1])

    hbm = lambda a: pltpu.HBM(a.shape, a.dtype)
    sems = pltpu.SemaphoreType.DMA(((N_DEV - 1) * n,))
    res = pl.pallas_call(
        body, name=name,
        out_shape=(sems, sems, *[hbm(a) for a in srcs + lands], jax.ShapeDtypeStruct((8, LANES), F32)),
        in_specs=[HBM_SPEC] * (2 * n),
        out_specs=(SEM_SPEC, SEM_SPEC, *[HBM_SPEC] * (2 * n), pl.BlockSpec(memory_space=pltpu.VMEM)),
        input_output_aliases={i: 2 + i for i in range(2 * n)}, compiler_params=SPLIT_PARAMS,
    )(*[pltpu.with_memory_space_constraint(a, pltpu.HBM) for a in srcs + lands])
    return res[0], res[1], list(res[2:2 + n]), list(res[2 + n:2 + 2 * n]), res[-1]


def scatter_wait(send_sems, recv_sems, srcs, lands, after, *, name, indexed):
    n = len(srcs)

    def body(*refs):
        for cp in _peer_copies(refs[:n], refs[n:2 * n], refs[2 * n], refs[2 * n + 1], indexed):
            cp.wait_send()
            cp.wait_recv()

    hbm = lambda a: pltpu.HBM(a.shape, a.dtype)
    res = pl.pallas_call(
        body, name=name, out_shape=tuple(hbm(a) for a in srcs + lands),
        in_specs=[HBM_SPEC] * (2 * n) + [SEM_SPEC, SEM_SPEC, ANY], out_specs=tuple([HBM_SPEC] * (2 * n)),
        input_output_aliases={i: i for i in range(2 * n)}, compiler_params=SPLIT_PARAMS,
    )(*srcs, *lands, send_sems, recv_sems, after)
    return list(res[:n]), list(res[n:])


def _adam_rows(r, c):
    fits = [t for t in range(8, r + 1, 8) if r % t == 0 and N_DEV * t * c * 4 <= 6 * 2 ** 20]
    return max(fits) if fits else r


def adamw(recv, w, m, v, *, name):
    _, r, n = recv.shape
    tr = _adam_rows(r, n)

    def body(r_ref, w_ref, m_ref, v_ref, g_ref, d_ref, nm_ref, nv_ref):
        g = r_ref[0].astype(F32)
        for s in range(1, N_DEV):
            g = g + r_ref[s].astype(F32)
        m_new = ADAM_B1 * m_ref[...] + (1.0 - ADAM_B1) * g
        v_new = ADAM_B2 * v_ref[...] + (1.0 - ADAM_B2) * jnp.square(g)
        m_hat = m_new / (1.0 - ADAM_B1 ** ADAM_STEP)
        v_hat = v_new / (1.0 - ADAM_B2 ** ADAM_STEP)
        g_ref[...] = g
        d_ref[...] = -ADAM_LR * (m_hat / (jnp.sqrt(v_hat) + ADAM_EPS) + ADAM_WD * w_ref[...])
        nm_ref[...] = m_new
        nv_ref[...] = v_new

    blk = pl.BlockSpec((tr, n), lambda i: (i, 0))
    return pl.pallas_call(
        body, name=name, grid=(r // tr,), in_specs=[pl.BlockSpec((N_DEV, tr, n), lambda i: (0, i, 0)), blk, blk, blk],
        out_specs=[blk] * 4, out_shape=[jax.ShapeDtypeStruct((r, n), F32)] * 4,
        compiler_params=_params(("parallel",)))(recv, w, m, v)


def _shard_rows(full, axis):
    if axis == 0:
        return full.reshape(N_DEV, -1)
    r, c = full.shape
    return jnp.transpose(full.reshape(r, N_DEV, c // N_DEV), (1, 0, 2)).reshape(N_DEV, -1)


def _unshard(blocks, axis):
    if axis == 0:
        return blocks.reshape(-1, blocks.shape[-1])
    return jnp.transpose(blocks, (1, 0, 2)).reshape(blocks.shape[1], -1)


def kernel(x, norm_mix_w, w_in, ssd_conv_w, ssd_conv_b, ssd_dt_bias_fwd, ssd_dt_bias_bwd, ssd_a_log_fwd, ssd_a_log_bwd, ssd_d, ssd_norm_w, s5_lambda_re_fwd, s5_lambda_im_fwd, s5_log_step_fwd, s5_lambda_re_bwd, s5_lambda_im_bwd, s5_log_step_bwd, s5_b_re, s5_b_im, s5_c_re_fwd, s5_c_im_fwd, s5_c_re_bwd, s5_c_im_bwd, s5_d, s5_glu_w, s5_glu_b, s5_norm_w, w_out, norm_ffn_w, ffn_w_up, ffn_conv_w, ffn_conv_b, ffn_w_down, norm_final_w, loss_target, m_norm_mix_w, m_w_in, m_ssd_conv_w, m_ssd_conv_b, m_ssd_dt_bias_fwd, m_ssd_dt_bias_bwd, m_ssd_a_log_fwd, m_ssd_a_log_bwd, m_ssd_d, m_ssd_norm_w, m_s5_lambda_re_fwd, m_s5_lambda_im_fwd, m_s5_log_step_fwd, m_s5_lambda_re_bwd, m_s5_lambda_im_bwd, m_s5_log_step_bwd, m_s5_b_re, m_s5_b_im, m_s5_c_re_fwd, m_s5_c_im_fwd, m_s5_c_re_bwd, m_s5_c_im_bwd, m_s5_d, m_s5_glu_w, m_s5_glu_b, m_s5_norm_w, m_w_out, m_norm_ffn_w, m_ffn_w_up, m_ffn_conv_w, m_ffn_conv_b, m_ffn_w_down, m_norm_final_w, v_norm_mix_w, v_w_in, v_ssd_conv_w, v_ssd_conv_b, v_ssd_dt_bias_fwd, v_ssd_dt_bias_bwd, v_ssd_a_log_fwd, v_ssd_a_log_bwd, v_ssd_d, v_ssd_norm_w, v_s5_lambda_re_fwd, v_s5_lambda_im_fwd, v_s5_log_step_fwd, v_s5_lambda_re_bwd, v_s5_lambda_im_bwd, v_s5_log_step_bwd, v_s5_b_re, v_s5_b_im, v_s5_c_re_fwd, v_s5_c_im_fwd, v_s5_c_re_bwd, v_s5_c_im_bwd, v_s5_d, v_s5_glu_w, v_s5_glu_b, v_s5_norm_w, v_w_out, v_norm_ffn_w, v_ffn_w_up, v_ffn_conv_w, v_ffn_conv_b, v_ffn_w_down, v_norm_final_w):
    args = dict(locals())
    strip = lambda n, v: v if n == 'norm_final_w' else v[0]
    w = {n: strip(n, args[n]) for n in WEIGHTS}

    mats = ['w_in', 'w_out', 'ffn_w_up', 'ffn_w_down']
    convs = ['ssd_conv_w', 'ffn_conv_w']
    shard = lambda n: w[n].astype(BF16) if n in mats else w[n]
    early, late = ['w_in', 'ssd_conv_w'], ['w_out', 'ffn_w_up', 'ffn_w_down', 'ffn_conv_w']
    full = dict(w)
    full.update(zip(early, all_gather([shard(n) for n in early], name="weight_all_gather")))
    ssem, rsem, src_thru, land_thru, token = scatter_start([shard(n) for n in late], name="weight_gather_start",
                                                           indexed=False)
    me = 4 * lax.axis_index("x") + 2 * lax.axis_index("y") + lax.axis_index("c")

    def late_weights(after):
        own, landed = scatter_wait(ssem, rsem, src_thru, land_thru, after, name="weight_gather_wait", indexed=False)
        return {n: lax.dynamic_update_index_in_dim(l, o, me, 0) for n, o, l in zip(late, own, landed)}

    full['late'], full['token'] = late_weights, token[:1, :1]

    pending = []
    last = 'norm_mix_w'
    small = convs + [n for n in WEIGHTS if n not in SHARDED and n != last]
    slot = {n: -(-w[n].size // (8 * LANES)) * 8 for n in small}
    used = sum(slot.values()) + 8
    nrow = -(-used // PACK_ROWS) * PACK_ROWS

    def tiles(v, n):
        return jnp.pad(v, ((0, 0), (0, slot[n] * LANES - v.shape[1]))).reshape(v.shape[0], slot[n], LANES)

    def send_early(grads, names, loss=None):
        srcs = [grads[n].astype(BF16) for n in names]
        if loss is not None:
            pieces = [tiles(grads[n].reshape(N_DEV, -1), n) if n in SHARDED else
                      jnp.broadcast_to(tiles(grads[n].reshape(1, -1), n), (N_DEV, slot[n], LANES)) for n in small]
            pieces.append(jnp.broadcast_to(jnp.pad(loss.reshape(1, 1, 1), ((0, 0), (0, 7), (0, LANES - 1))),
                                           (N_DEV, 8, LANES)))
            pieces.append(jnp.zeros((N_DEV, nrow - used, LANES), F32))
            srcs.append(jnp.concatenate(pieces, axis=1))
            names = names + ['small']
        started = scatter_start(srcs, name="grad_start_" + names[0], indexed=True)
        pending.append((names,) + started[:4])
        return started[4][:1, :1]

    full['on_grads'] = send_early
    loss, grad_x, g = local_step(x, loss_target, full)

    last_send = jnp.broadcast_to(g[last].reshape(1, -1, LANES), (N_DEV, g[last].size // LANES, LANES))
    last_started = scatter_start([last_send], name="grad_start_" + last, indexed=True)
    recv, outs = {}, [{}, {}, {}, {}]

    def arrived(names, started, after):
        own, landed = scatter_wait(*started, after, name="grad_wait_" + names[0], indexed=True)
        for n, o, l in zip(names, own, landed):
            recv[n] = lax.dynamic_update_index_in_dim(l, lax.dynamic_index_in_dim(o, me, 0, keepdims=False), me, 0)

    def update(n):
        shape = recv[n].shape[1:]
        res = adamw(recv[n], *[strip(n, args[p + n]).reshape(shape) for p in ('', 'm_', 'v_')], name="adamw_" + n)
        for o, p in zip(outs, res):
            o[n] = p.reshape(args[n].shape)

    for names, *started in pending:
        arrived(names, started, last_started[4])
    for n in mats:
        update(n)

    def pack(prefix):
        vals = [tiles(strip(n, args[prefix + n]).reshape(1, -1), n)[0] for n in small]
        return jnp.concatenate(vals + [jnp.zeros((nrow - used + 8, LANES), F32)], axis=0)

    packed = adamw(recv['small'], pack(''), pack('m_'), pack('v_'), name="adamw_small")
    arrived([last], last_started[:4], packed[1])
    update(last)
    off = 0
    for n in small:
        for o, p in zip(outs, packed):
            o[n] = p[off:off + slot[n]].reshape(-1)[:w[n].size].reshape(args[n].shape)
        off += slot[n]
    loss_out = packed[0][off, 0].reshape(())
    return (loss_out, grad_x, *[o[n] for o in outs for n in WEIGHTS])
```

```python
import functools

import jax
import jax.numpy as jnp
from jax import lax
from jax.experimental import pallas as pl
from jax.experimental.pallas import tpu as pltpu

F32, BF16 = jnp.float32, jnp.bfloat16
N_DEV = 8
D_MODEL = 1024
SSD_W, HEADS, HDIM, SGROUPS, HPG, NSTATE, SCONV, QC = 1024, 16, 64, 4, 4, 128, 5, 128
XBC_W = SSD_W + 2 * SGROUPS * NSTATE
S5_W, S5_G, S5_C, S5_P, S5_Q = 512, 32, 16, 64, 16
S5_QC = S5_Q * S5_C
CARRY_ROWS = 32
DFF, FCONV = 2816, 3
FFN_BLK, FFN_PAD = 704, 768
EPS = 1e-6
ADAM_LR, ADAM_B1, ADAM_B2, ADAM_EPS, ADAM_WD, ADAM_STEP = 0.001, 0.9, 0.999, 1e-08, 0.01, 10
LANES = 128
MESH = pl.DeviceIdType.MESH

WEIGHTS = ['norm_mix_w', 'w_in', 'ssd_conv_w', 'ssd_conv_b', 'ssd_dt_bias_fwd', 'ssd_dt_bias_bwd', 'ssd_a_log_fwd',
           'ssd_a_log_bwd', 'ssd_d', 'ssd_norm_w', 's5_lambda_re_fwd', 's5_lambda_im_fwd', 's5_log_step_fwd',
           's5_lambda_re_bwd', 's5_lambda_im_bwd', 's5_log_step_bwd', 's5_b_re', 's5_b_im', 's5_c_re_fwd', 's5_c_im_fwd',
           's5_c_re_bwd', 's5_c_im_bwd', 's5_d', 's5_glu_w', 's5_glu_b', 's5_norm_w', 'w_out', 'norm_ffn_w', 'ffn_w_up',
           'ffn_conv_w', 'ffn_conv_b', 'ffn_w_down', 'norm_final_w']
SHARDED = {'w_in': 1, 'ssd_conv_w': 1, 'w_out': 0, 'ffn_w_up': 1, 'ffn_conv_w': 1, 'ffn_w_down': 0}
FULL_SHAPE = {'w_in': (1024, 3616), 'ssd_conv_w': (5, 2048), 'w_out': (1536, 1024), 'ffn_w_up': (1024, 5632),
              'ffn_conv_w': (3, 5632), 'ffn_w_down': (2816, 1024)}
PACK_ROWS = 512


def _pick(n, cap=1536):
    if n <= cap:
        return n
    return max(t for t in range(LANES, cap + 1, LANES) if n % t == 0)


def _params(sem):
    return pltpu.CompilerParams(dimension_semantics=sem)


def _bd(a, b, ca, cb):
    return lax.dot_general(a.astype(BF16), b.astype(BF16), (((ca,), (cb,)), ((), ())), preferred_element_type=F32)


@jax.custom_vjp
def dot_nn(a, b):
    return _bd(a, b, 1, 0)


dot_nn.defvjp(lambda a, b: (_bd(a, b, 1, 0), (a, b)),
              lambda r, g: (_bd(g, r[1], 1, 1).astype(r[0].dtype), _bd(r[0], g, 0, 0).astype(r[1].dtype)))


@jax.custom_vjp
def dot_nt(a, b):
    return _bd(a, b, 1, 1)


dot_nt.defvjp(lambda a, b: (_bd(a, b, 1, 1), (a, b)),
              lambda r, g: (_bd(g, r[1], 1, 0).astype(r[0].dtype), _bd(g, r[0], 0, 0).astype(r[1].dtype)))


@jax.custom_vjp
def dot_tn(a, b):
    return _bd(a, b, 0, 0)


dot_tn.defvjp(lambda a, b: (_bd(a, b, 0, 0), (a, b)),
              lambda r, g: (_bd(r[1], g, 1, 1).astype(r[0].dtype), _bd(r[0], g, 1, 0).astype(r[1].dtype)))


def _rows2(v):
    h = v.shape[0] // 2
    return v[:h], v[h:]


def _cols2(v):
    h = v.shape[1] // 2
    return v[:, :h], v[:, h:]


@jax.custom_vjp
def dot2_nn(la, lb, x):
    return _rows2(_bd(jnp.concatenate([la, lb], axis=0), x, 1, 0))


def _dot2_nn_bwd(res, g):
    la, lb, x = res
    gcat, lcat = jnp.concatenate(g, axis=0), jnp.concatenate([la, lb], axis=0)
    return (*_rows2(_bd(gcat, x, 1, 1)), _bd(lcat, gcat, 0, 0))


dot2_nn.defvjp(lambda la, lb, x: (dot2_nn(la, lb, x), (la, lb, x)), _dot2_nn_bwd)


@jax.custom_vjp
def dot_nt2(c, p0, p1):
    return _cols2(_bd(c, jnp.concatenate([p0, p1], axis=0), 1, 1))


def _dot_nt2_bwd(res, g):
    c, p0, p1 = res
    gcat = jnp.concatenate(g, axis=1)
    return (_bd(gcat, jnp.concatenate([p0, p1], axis=0), 1, 0), *_rows2(_bd(gcat, c, 0, 0)))


dot_nt2.defvjp(lambda c, p0, p1: (dot_nt2(c, p0, p1), (c, p0, p1)), _dot_nt2_bwd)


@jax.custom_vjp
def dot_tn2(a0, a1, b):
    return _rows2(_bd(jnp.concatenate([a0, a1], axis=1), b, 0, 0))


def _dot_tn2_bwd(res, g):
    a0, a1, b = res
    gcat, acat = jnp.concatenate(g, axis=0), jnp.concatenate([a0, a1], axis=1)
    return (*_cols2(_bd(b, gcat, 1, 1)), _bd(acat, gcat, 1, 0))


dot_tn2.defvjp(lambda a0, a1, b: (dot_tn2(a0, a1, b), (a0, a1, b)), _dot_tn2_bwd)


def _split3(x):
    hi = x.astype(BF16)
    r = x - hi.astype(F32)
    mid = r.astype(BF16)
    lo = (r - mid.astype(F32)).astype(BF16)
    return hi, mid, lo


def _cum_matrix(q, upper):
    ri = lax.broadcasted_iota(jnp.int32, (q, q), 0)
    ci = lax.broadcasted_iota(jnp.int32, (q, q), 1)
    return jnp.where((ci >= ri) if upper else (ci <= ri), 1.0, 0.0).astype(BF16)


def _exact_right(x, mat):
    return sum(jnp.dot(p, mat, preferred_element_type=F32) for p in _split3(x))


@functools.partial(jax.custom_vjp, nondiff_argnums=(1,))
def cum_row(x, rev):
    return _exact_right(x, _cum_matrix(x.shape[1], not rev))


cum_row.defvjp(lambda x, rev: (cum_row(x, rev), None),
               lambda rev, _, g: (_exact_right(g, _cum_matrix(g.shape[1], rev)),))


def _softplus(x):
    return jnp.maximum(x, 0.0) + jnp.log(1.0 + jnp.exp(-jnp.abs(x)))


def _silu(x):
    return x * jax.nn.sigmoid(x)


def _gelu(x):
    return 0.5 * x * (1.0 + jnp.tanh(0.7978845608028654 * (x + 0.044715 * (x * x * x))))


def _rms(x, w):
    xf = x.astype(F32)
    return xf * lax.rsqrt(jnp.mean(xf * xf, axis=-1, keepdims=True) + EPS) * w


def matmul_sum(a_list, b_list, *, name, out_dtype=F32, add=None, tm=512, nt=False, norm_w=None, norm_bwd=None):
    a_arrs = [a[0] if isinstance(a, tuple) else a for a in a_list]
    b_arrs = [b[0] if isinstance(b, tuple) else b for b in b_list]
    m, n = a_arrs[0].shape[0], b_arrs[0].shape[-2 if nt else -1]
    tm, tn, k = min(tm, m), _pick(n), len(a_list)
    assert (norm_w is None and norm_bwd is None) or tn == n

    def body(*refs):
        acc = None
        for a_ref, b_ref in zip(refs[:k], refs[k:2 * k]):
            p = _bd(a_ref[...], b_ref[...], 1, 1 if nt else 0)
            acc = p if acc is None else acc + p
        if add is not None:
            acc = acc + refs[2 * k][...]
        if norm_bwd is not None:
            x_ref, w_ref, res_ref, dx_ref, dw_ref = refs[-5:]
            dx, dw = jax.vjp(_rms, x_ref[...], w_ref[...])[1](acc)
            dx_ref[...] = dx + res_ref[...]

            @pl.when(pl.program_id(0) == 0)
            def _():
                dw_ref[...] = jnp.zeros_like(dw_ref)

            dw_ref[...] += dw
        elif norm_w is not None:
            refs[-2][...] = acc.astype(out_dtype)
            refs[-1][...] = _rms(acc, refs[-3][...]).astype(BF16)
        else:
            refs[-1][...] = acc.astype(out_dtype)

    def a_spec(a):
        if isinstance(a, tuple):
            return pl.BlockSpec((tm, a[1]), lambda i, j, blk=a[2]: (i, blk))
        return pl.BlockSpec((tm, a.shape[1]), lambda i, j: (i, 0))

    def b_spec(b):
        arr, p = b if isinstance(b, tuple) else (b, None)
        kk = arr.shape[-1 if nt else -2]
        shape, idx = ((tn, kk), lambda j: (j, 0)) if nt else ((kk, tn), lambda j: (0, j))
        mode = {'pipeline_mode': pl.Buffered(1)} if tn == n else {}
        if p is None:
            return pl.BlockSpec(shape, lambda i, j: idx(j), **mode)
        return pl.BlockSpec((None,) + shape, lambda i, j, p=p: (p,) + idx(j), **mode)

    in_specs = [a_spec(a) for a in a_list] + [b_spec(b) for b in b_list]
    args = a_arrs + b_arrs
    if add is not None:
        in_specs.append(pl.BlockSpec((tm, tn), lambda i, j: (i, j)))
        args.append(add)
    out_spec, out_shape = pl.BlockSpec((tm, tn), lambda i, j: (i, j)), jax.ShapeDtypeStruct((m, n), out_dtype)
    if norm_w is not None:
        in_specs.append(pl.BlockSpec(norm_w.shape, lambda i, j: (0, 0)))
        args.append(norm_w)
        out_spec, out_shape = [out_spec, out_spec], [out_shape, jax.ShapeDtypeStruct((m, n), BF16)]
    sem = ("parallel", "parallel")
    if norm_bwd is not None:
        x, w, res = norm_bwd
        wspec = pl.BlockSpec(w.shape, lambda i, j: (0, 0))
        in_specs += [out_spec, wspec, out_spec]
        args += [x, w, res]
        out_spec, out_shape = [out_spec, wspec], [jax.ShapeDtypeStruct((m, n), F32), jax.ShapeDtypeStruct(w.shape, F32)]
        sem = ("arbitrary", "arbitrary")
    return pl.pallas_call(
        body, name=name, grid=(m // tm, n // tn), in_specs=in_specs, out_specs=out_spec, out_shape=out_shape,
        compiler_params=_params(sem))(*args)


def matmul_multi(a, b_list, *, name, tm=512):
    m, kk = a.shape
    tm, nb = min(tm, m), len(b_list)

    def body(a_ref, *refs):
        a_v = a_ref[...]
        for b_ref, o_ref in zip(refs[:nb], refs[nb:]):
            o_ref[...] = _bd(a_v, b_ref[...], 1, 0)

    return pl.pallas_call(
        body, name=name, grid=(m // tm,),
        in_specs=[pl.BlockSpec((tm, kk), lambda i: (i, 0))] + [_full_spec(b) for b in b_list],
        out_specs=[pl.BlockSpec((tm, b.shape[1]), lambda i: (i, 0)) for b in b_list],
        out_shape=[jax.ShapeDtypeStruct((m, b.shape[1]), F32) for b in b_list],
        compiler_params=_params(("parallel",)))(a, *b_list)


def matmul_cols(a, b3, *, name, out_dtype=F32, tm=2048):
    m, kk = a.shape
    p, _, nb = b3.shape
    tm, tn = min(tm, m), _pick(nb, 768)
    per = nb // tn

    def body(a_ref, b_ref, o_ref):
        o_ref[...] = _bd(a_ref[...], b_ref[...], 1, 0).astype(out_dtype)

    return pl.pallas_call(
        body, name=name, grid=(m // tm, p * per),
        in_specs=[pl.BlockSpec((tm, kk), lambda i, j: (i, 0)),
                  pl.BlockSpec((None, kk, tn), lambda i, j: (j // per, 0, j % per))],
        out_specs=pl.BlockSpec((tm, tn), lambda i, j: (i, j)),
        out_shape=jax.ShapeDtypeStruct((m, p * nb), out_dtype),
        compiler_params=_params(("parallel", "parallel")))(a, b3)


def matmul_tn(a, b, *, name, tm=2048, out_blocks=None):
    m, k = a.shape
    n = b.shape[1]
    nb = n // (out_blocks or 1)
    tm, tk, tn = min(tm, m), _pick(k), _pick(nb, 768 if out_blocks else 1536)
    per = nb // tn

    def body(a_ref, b_ref, o_ref):
        @pl.when(pl.program_id(2) == 0)
        def _():
            o_ref[...] = jnp.zeros_like(o_ref)

        o_ref[...] += _bd(a_ref[...], b_ref[...], 0, 0)

    if out_blocks:
        out_spec = pl.BlockSpec((None, tk, tn), lambda i, j, t: (j // per, i, j % per))
        out_shape = jax.ShapeDtypeStruct((out_blocks, k, nb), F32)
    else:
        out_spec = pl.BlockSpec((tk, tn), lambda i, j, t: (i, j))
        out_shape = jax.ShapeDtypeStruct((k, n), F32)
    return pl.pallas_call(
        body, name=name, grid=(k // tk, n // tn, m // tm),
        in_specs=[pl.BlockSpec((tm, tk), lambda i, j, t: (t, i)), pl.BlockSpec((tm, tn), lambda i, j, t: (t, j))],
        out_specs=out_spec, out_shape=out_shape,
        compiler_params=_params(("parallel", "parallel", "arbitrary")))(a, b)


def _row_spec(r, tm):
    if isinstance(r, tuple):
        arr, width, blk = r
        return arr, pl.BlockSpec((tm, width), lambda i, blk=blk: (i, blk))
    return r, pl.BlockSpec((tm, r.shape[1]), lambda i: (i, 0))


def _full_spec(p):
    return pl.BlockSpec(p.shape, lambda i: (0,) * p.ndim)


def _expand_rows(rows, tm):
    arrays, specs, counts, widths = [], [], [], []
    for r in rows:
        parts = [_row_spec(p, tm) for p in (r if isinstance(r, list) else [r])]
        arrays += [a for a, _ in parts]
        specs += [s for _, s in parts]
        counts.append(len(parts))
        widths.append(sum(s.block_shape[1] for _, s in parts))
    return arrays, specs, counts, widths


def _row_values(refs, counts):
    vals, k = [], 0
    for c in counts:
        parts = [refs[k + j][...] for j in range(c)]
        vals.append(parts[0] if c == 1 else jnp.concatenate(parts, axis=1))
        k += c
    return vals


def _rows_of(rows):
    first = rows[0][0] if isinstance(rows[0], list) else rows[0]
    return (first[0] if isinstance(first, tuple) else first).shape[0]


def rowmap_fwd(fn, rows, params, outs, *, name, tm=256):
    m = _rows_of(rows)
    tm = min(tm, m)
    arrays, specs, counts, _ = _expand_rows(rows, tm)
    nin, npar = len(arrays), len(params)

    def body(*refs):
        res = fn(*_row_values(refs[:nin], counts), *[r[...] for r in refs[nin:nin + npar]])
        for o_ref, v in zip(refs[nin + npar:], res):
            o_ref[...] = v.astype(o_ref.dtype)

    return pl.pallas_call(
        body, name=name, grid=(m // tm,), in_specs=specs + [_full_spec(p) for p in params],
        out_specs=[pl.BlockSpec((tm, c), lambda i: (i, 0)) for c, _ in outs],
        out_shape=[jax.ShapeDtypeStruct((m, c), dt) for c, dt in outs],
        compiler_params=_params(("parallel",)))(*arrays, *params)


def rowmap_bwd(fn, rows, params, cts, *, name, row_dtypes=None, add=None, tm=256):
    m = _rows_of(rows)
    tm = min(tm, m)
    arrays, specs, counts, widths = _expand_rows(rows, tm)
    cp = [_row_spec(c, tm) for c in cts]
    nin, nr, npar, nc = len(arrays), len(rows), len(params), len(cts)
    row_dtypes = row_dtypes or [F32] * nr

    def body(*refs):
        ins = _row_values(refs[:nin], counts) + [r[...] for r in refs[nin:nin + npar]]
        ins = [v.astype(F32) for v in ins]
        ct = tuple(r[...].astype(F32) for r in refs[nin + npar:nin + npar + nc])
        base = nin + npar + nc
        extra = None
        if add is not None:
            extra = refs[base][...]
            base += 1
        _, pull = jax.vjp(fn, *ins)
        grads = pull(ct)
        for j in range(nr):
            g = grads[j]
            if j == 0 and extra is not None:
                g = g + extra
            refs[base + j][...] = g.astype(refs[base + j].dtype)

        @pl.when(pl.program_id(0) == 0)
        def _():
            for j in range(npar):
                refs[base + nr + j][...] = jnp.zeros_like(refs[base + nr + j])

        for j in range(npar):
            refs[base + nr + j][...] += grads[nr + j]

    in_specs = specs + [_full_spec(p) for p in params] + [s for _, s in cp]
    args = arrays + list(params) + [a for a, _ in cp]
    if add is not None:
        in_specs.append(pl.BlockSpec((tm, widths[0]), lambda i: (i, 0)))
        args.append(add)
    out_specs = [pl.BlockSpec((tm, w), lambda i: (i, 0)) for w in widths] + [_full_spec(p) for p in params]
    out_shape = [jax.ShapeDtypeStruct((m, w), dt) for w, dt in zip(widths, row_dtypes)]
    out_shape += [jax.ShapeDtypeStruct(p.shape, F32) for p in params]
    return pl.pallas_call(
        body, name=name, grid=(m // tm,), in_specs=in_specs, out_specs=out_specs, out_shape=out_shape,
        compiler_params=_params(("arbitrary",)))(*args)


def loss_head(h, target, w, *, name, tm=256, matmul=None):
    m, d = h.shape
    tm = min(tm, m)

    def body(h_ref, t_ref, w_ref, *refs):
        loss_ref, dh_ref, dw_ref = refs[-3:]
        rows = h_ref[...]
        if matmul is not None:
            rows = rows + _bd(refs[0][...], refs[1][...], 1, 0)
        y, pull = jax.vjp(_rms, rows, w_ref[...])
        err = y - t_ref[...]
        dh, dw = pull(err * (1.0 / d))

        @pl.when(pl.program_id(0) == 0)
        def _():
            loss_ref[...] = jnp.zeros_like(loss_ref)
            dw_ref[...] = jnp.zeros_like(dw_ref)

        loss_ref[...] += (0.5 / d) * jnp.sum(err * err, keepdims=True)
        dw_ref[...] += dw
        dh_ref[...] = dh

    row = pl.BlockSpec((tm, d), lambda i: (i, 0))
    in_specs, args = [row, row, _full_spec(w)], [h, target, w]
    if matmul is not None:
        in_specs += [pl.BlockSpec((tm, matmul[0].shape[1]), lambda i: (i, 0)), _full_spec(matmul[1])]
        args += list(matmul)
    return pl.pallas_call(
        body, name=name, grid=(m // tm,), in_specs=in_specs,
        out_specs=[pl.BlockSpec((1, 1), lambda i: (0, 0)), row, _full_spec(w)],
        out_shape=[jax.ShapeDtypeStruct((1, 1), F32), jax.ShapeDtypeStruct((m, d), F32),
                   jax.ShapeDtypeStruct(w.shape, F32)],
        compiler_params=_params(("arbitrary",)))(*args)


def _shift(x, s):
    if s == 0:
        return x
    n = x.shape[0]
    t = lax.broadcasted_iota(jnp.int32, x.shape, 0)
    rolled = pltpu.roll(x, (-s) % n, 0)
    return jnp.where((t + s >= 0) & (t + s < n), rolled, 0.0)


def _conv(x, w, b):
    k = w.shape[0]
    acc = b + w[k // 2:k // 2 + 1, :] * x
    for j in range(k):
        if j != k // 2:
            acc = acc + w[j:j + 1, :] * _shift(x, j - k // 2)
    return acc


def _conv_bwd(x, dc, w):
    k = w.shape[0]
    dx = None
    dws = []
    for j in range(k):
        s = j - k // 2
        term = w[j:j + 1, :] * _shift(dc, -s)
        dx = term if dx is None else dx + term
        dws.append(jnp.sum(dc * _shift(x, s), axis=0, keepdims=True))
    return dx, jnp.concatenate(dws, axis=0), jnp.sum(dc, axis=0, keepdims=True)


def _dsilu(c):
    s = jax.nn.sigmoid(c)
    return s * (1.0 + c * (1.0 - s))


def ssd_conv_fwd(xbc, w, b, *, bsz, name):
    t, c = xbc.shape
    seq, ct = t // bsz, 256

    def body(x_ref, w_ref, b_ref, o_ref):
        o_ref[...] = _silu(_conv(x_ref[...], w_ref[...], b_ref[...]))

    return pl.pallas_call(
        body, name=name, grid=(c // ct, bsz),
        in_specs=[pl.BlockSpec((seq, ct), lambda j, i: (i, j)), pl.BlockSpec((w.shape[0], ct), lambda j, i: (0, j)),
                  pl.BlockSpec((1, ct), lambda j, i: (0, j))],
        out_specs=pl.BlockSpec((seq, ct), lambda j, i: (i, j)),
        out_shape=jax.ShapeDtypeStruct((t, c), F32),
        compiler_params=_params(("parallel", "parallel")))(xbc, w, b)


def ssd_conv_bwd(xbc, dparts, w, b, *, bsz, name):
    t, c = xbc.shape
    seq, ct, k = t // bsz, 256, w.shape[0]
    starts = [0]
    for p in dparts:
        starts.append(starts[-1] + p.shape[1] // ct)

    def body(x_ref, *refs):
        g_refs, (w_ref, b_ref, dx_ref, dw_ref, db_ref) = refs[:len(dparts)], refs[len(dparts):]
        j = pl.program_id(0)

        @pl.when(pl.program_id(1) == 0)
        def _():
            dw_ref[...] = jnp.zeros_like(dw_ref)
            db_ref[...] = jnp.zeros_like(db_ref)

        def run(g_ref):
            x, wv = x_ref[...], w_ref[...]
            dc = g_ref[...] * _dsilu(_conv(x, wv, b_ref[...]))
            dx, dw, db = _conv_bwd(x, dc, wv)
            dx_ref[...] = dx.astype(BF16)
            dw_ref[...] += dw
            db_ref[...] += db

        for n, g_ref in enumerate(g_refs):
            pl.when((j >= starts[n]) & (j < starts[n + 1]))(functools.partial(run, g_ref))

    def part_spec(n):
        lo, hi = starts[n], starts[n + 1]

        def index(j, i):
            inside = (j >= lo) & (j < hi)
            return jnp.where(inside, i, 0), jnp.where(inside, j - lo, 0)

        return pl.BlockSpec((seq, ct), index)

    blk = pl.BlockSpec((seq, ct), lambda j, i: (i, j))
    wspec, bspec = pl.BlockSpec((k, ct), lambda j, i: (0, j)), pl.BlockSpec((1, ct), lambda j, i: (0, j))
    return pl.pallas_call(
        body, name=name, grid=(c // ct, bsz),
        in_specs=[blk] + [part_spec(n) for n in range(len(dparts))] + [wspec, bspec], out_specs=[blk, wspec, bspec],
        out_shape=[jax.ShapeDtypeStruct((t, c), BF16), jax.ShapeDtypeStruct((k, c), F32),
                   jax.ShapeDtypeStruct((1, c), F32)],
        compiler_params=_params(("parallel", "arbitrary")))(xbc, *dparts, w, b)


def _ffn_specs(seq, ct, k, nblk):
    val = pl.BlockSpec((seq, ct), lambda j, i: (i, j))
    gate = pl.BlockSpec((seq, ct), lambda j, i: (i, nblk + j))
    wv, wg = pl.BlockSpec((k, ct), lambda j, i: (0, j)), pl.BlockSpec((k, ct), lambda j, i: (0, nblk + j))
    bv, bg = pl.BlockSpec((1, ct), lambda j, i: (0, j)), pl.BlockSpec((1, ct), lambda j, i: (0, nblk + j))
    return val, gate, wv, wg, bv, bg


def ffn_act_fwd(up, w, b, *, bsz, name):
    t = up.shape[0]
    half = up.shape[1] // 2
    seq, ct, k = t // bsz, 256, w.shape[0]
    val, gate, wv, wg, bv, bg = _ffn_specs(seq, ct, k, half // ct)

    def body(v_ref, g_ref, wv_ref, wg_ref, bv_ref, bg_ref, o_ref):
        vc = _conv(v_ref[...].astype(F32), wv_ref[...], bv_ref[...])
        gc = _conv(g_ref[...].astype(F32), wg_ref[...], bg_ref[...])
        o_ref[...] = (_silu(gc) * vc).astype(BF16)

    return pl.pallas_call(
        body, name=name, grid=(half // ct, bsz), in_specs=[val, gate, wv, wg, bv, bg], out_specs=val,
        out_shape=jax.ShapeDtypeStruct((t, half), BF16),
        compiler_params=_params(("parallel", "parallel")))(up, up, w, w, b, b)


def ffn_act_bwd(up, dact, w, b, *, bsz, name):
    t = up.shape[0]
    half = up.shape[1] // 2
    seq, ct, k = t // bsz, 256, w.shape[0]
    val, gate, wv, wg, bv, bg = _ffn_specs(seq, ct, k, half // ct)

    def body(v_ref, g_ref, wv_ref, wg_ref, bv_ref, bg_ref, d_ref, dv_ref, dg_ref, dwv_ref, dwg_ref, dbv_ref, dbg_ref):
        v, g = v_ref[...].astype(F32), g_ref[...].astype(F32)
        vc = _conv(v, wv_ref[...], bv_ref[...])
        gc = _conv(g, wg_ref[...], bg_ref[...])
        d = d_ref[...].astype(F32)
        sg = jax.nn.sigmoid(gc)
        dv, dwv, dbv = _conv_bwd(v, d * (gc * sg), wv_ref[...])
        dg, dwg, dbg = _conv_bwd(g, d * vc * (sg * (1.0 + gc * (1.0 - sg))), wg_ref[...])
        dv_ref[...] = dv.astype(BF16)
        dg_ref[...] = dg.astype(BF16)

        @pl.when(pl.program_id(1) == 0)
        def _():
            for r in (dwv_ref, dwg_ref, dbv_ref, dbg_ref):
                r[...] = jnp.zeros_like(r)

        dwv_ref[...] += dwv
        dwg_ref[...] += dwg
        dbv_ref[...] += dbv
        dbg_ref[...] += dbg

    return pl.pallas_call(
        body, name=name, grid=(half // ct, bsz), in_specs=[val, gate, wv, wg, bv, bg, val],
        out_specs=[val, val, wv, wv, bv, bv],
        out_shape=[jax.ShapeDtypeStruct((t, half), BF16), jax.ShapeDtypeStruct((t, half), BF16),
                   jax.ShapeDtypeStruct((k, half), F32), jax.ShapeDtypeStruct((k, half), F32),
                   jax.ShapeDtypeStruct((1, half), F32), jax.ShapeDtypeStruct((1, half), F32)],
        compiler_params=_params(("parallel", "arbitrary")))(up, up, w, w, b, b, dact)


def _sel_row(a, h):
    oh = (lax.broadcasted_iota(jnp.int32, (a.shape[0], 1), 0) == h).astype(F32)
    return jnp.sum(a * oh, axis=0, keepdims=True)


def _ssd_chunk(xp, dtr, bm, cm, prev, bias_r, alog_r, dskip_r, rev):
    q = dtr.shape[1]
    ri = lax.broadcasted_iota(jnp.int32, (q, q), 0)
    ci = lax.broadcasted_iota(jnp.int32, (q, q), 1)
    mask = (ci >= ri) if rev else (ci <= ri)
    lane_lo, row_lo = ci < HDIM, ri < HDIM
    dt_r = _softplus(dtr + bias_r)
    dta_r = dt_r * (-jnp.exp(alog_r))
    cs_r = cum_row(dta_r, rev)
    scores = dot_nt(cm, bm)

    def per_row(v):
        return jnp.broadcast_to(v, (q, q)).T

    assert len(xp) == 2
    y_diag, csqs, decayed, tots = [], [], [], []
    for p in range(2):
        ha = 2 * p + (HPG if rev else 0)
        hb = ha + 1
        cs_a, cs_b = _sel_row(cs_r, ha), _sel_row(cs_r, hb)
        csq_a, csq_b = per_row(cs_a), per_row(cs_b)
        seg_a = jnp.exp(jnp.where(mask, csq_a - cs_a, -1e30))
        seg_b = jnp.exp(jnp.where(mask, csq_b - cs_b, -1e30))
        csq = jnp.where(lane_lo, csq_a, csq_b)
        xdt = xp[p] * jnp.where(lane_lo, per_row(_sel_row(dt_r, ha)), per_row(_sel_row(dt_r, hb)))
        tot_a = jnp.sum(_sel_row(dta_r, ha), axis=1, keepdims=True)
        tot_b = jnp.sum(_sel_row(dta_r, hb), axis=1, keepdims=True)
        y_diag.append(jnp.where(lane_lo, *dot2_nn(scores * seg_a, scores * seg_b, xdt)))
        csqs.append(csq)
        decayed.append(xdt * jnp.exp(jnp.where(lane_lo, tot_a, tot_b) - csq))
        tots.append((tot_a, tot_b, ha, hb))
    y_off = dot_nt2(cm, *prev)
    states = dot_tn2(*decayed, bm)
    ys, news = [], []
    for p, (tot_a, tot_b, ha, hb) in enumerate(tots):
        y = y_diag[p] + y_off[p] * jnp.exp(csqs[p])
        if not rev:
            y = y + jnp.where(lane_lo, _sel_row(dskip_r, ha), _sel_row(dskip_r, hb)) * xp[p]
        ys.append(y)
        news.append(jnp.exp(jnp.where(row_lo, tot_a, tot_b)) * prev[p] + states[p])
    return tuple(ys), tuple(news)


NPAIR = HPG // 2


def _ssd_specs(seq, nc):
    xs = pl.BlockSpec((None, seq, HPG * HDIM), lambda b, g: (b, 0, g))
    bm = pl.BlockSpec((None, seq, NSTATE), lambda b, g: (b, 0, SSD_W // NSTATE + g))
    cm = pl.BlockSpec((None, seq, NSTATE), lambda b, g: (b, 0, SSD_W // NSTATE + SGROUPS + g))
    dtr = pl.BlockSpec((None, None, 2 * HPG, seq), lambda b, g: (b, g, 0, 0))
    pr = pl.BlockSpec((None, 2 * HPG, 1), lambda b, g: (g, 0, 0))
    st = pl.BlockSpec((None, None, 2, nc, NPAIR, 2 * HDIM, NSTATE), lambda b, g: (b, g, 0, 0, 0, 0, 0))
    return xs, bm, cm, dtr, pr, st


def _pair_cols(p):
    return slice(2 * HDIM * p, 2 * HDIM * (p + 1))


def ssd_scan_fwd(act, dtr, prs, *, name):
    bsz, seq, _ = act.shape
    nc = seq // QC
    xs, bm, cm, dtrs, pr, st = _ssd_specs(seq, nc)

    def body(x_ref, b_ref, c_ref, dtr_ref, br_ref, ar_ref, dk_ref, y_ref, st_ref):
        par = (br_ref[...], ar_ref[...], dk_ref[...])
        y_ref[...] = jnp.zeros_like(y_ref)

        def step(i, carry):
            new = []
            for rev in (False, True):
                k = (nc - 1 - i) if rev else i
                rows = pl.ds(pl.multiple_of(k * QC, QC), QC)
                xp = tuple(x_ref[rows, _pair_cols(p)] for p in range(NPAIR))
                for p in range(NPAIR):
                    st_ref[int(rev), k, p] = carry[rev][p]
                ys, nw = _ssd_chunk(xp, dtr_ref[:, rows], b_ref[rows, :], c_ref[rows, :], carry[rev], *par, rev)
                for p in range(NPAIR):
                    y_ref[rows, _pair_cols(p)] += ys[p]
                new.append(nw)
            return tuple(new)

        zero = tuple(jnp.zeros((2 * HDIM, NSTATE), F32) for _ in range(NPAIR))
        lax.fori_loop(0, nc // 2, lambda i, c: step(2 * i + 1, step(2 * i, c)), (zero, zero))

    return pl.pallas_call(
        body, name=name, grid=(bsz, SGROUPS), in_specs=[xs, bm, cm, dtrs, pr, pr, pr], out_specs=[xs, st],
        out_shape=[jax.ShapeDtypeStruct((bsz, seq, SSD_W), F32),
                   jax.ShapeDtypeStruct((bsz, SGROUPS, 2, nc, NPAIR, 2 * HDIM, NSTATE), F32)],
        compiler_params=_params(("parallel", "parallel")))(act, act, act, dtr, *prs)


def ssd_scan_bwd(act, dtr, prs, states, dy, *, name):
    bsz, seq, _ = act.shape
    nc = seq // QC
    xs, bm, cm, dtrs, pr, st = _ssd_specs(seq, nc)
    grp = pl.BlockSpec((None, seq, NSTATE), lambda b, g: (b, 0, g))
    dpr = pl.BlockSpec((None, None, 2 * HPG, 1), lambda b, g: (b, g, 0, 0))

    def body(x_ref, b_ref, c_ref, dtr_ref, br_ref, ar_ref, dk_ref, st_ref, dy_ref,
             dx_ref, db_ref, dc_ref, ddtr_ref, gbr_ref, gar_ref, gdk_ref):
        par = (br_ref[...], ar_ref[...], dk_ref[...])
        pgrads = (gbr_ref, gar_ref, gdk_ref)
        for r in pgrads + (dx_ref, db_ref, dc_ref, ddtr_ref):
            r[...] = jnp.zeros_like(r)

        def bstep(i, dcarry):
            new = []
            for rev in (False, True):
                k = i if rev else (nc - 1 - i)
                rows = pl.ds(pl.multiple_of(k * QC, QC), QC)
                xp = tuple(x_ref[rows, _pair_cols(p)] for p in range(NPAIR))
                prev = tuple(st_ref[int(rev), k, p] for p in range(NPAIR))
                _, pull = jax.vjp(functools.partial(_ssd_chunk, rev=rev), xp, dtr_ref[:, rows], b_ref[rows, :],
                                  c_ref[rows, :], prev, *par)
                dyp = tuple(dy_ref[rows, _pair_cols(p)] for p in range(NPAIR))
                gx, gdt, gb, gc, gprev, *gpar = pull((dyp, dcarry[rev]))
                for p in range(NPAIR):
                    dx_ref[rows, _pair_cols(p)] += gx[p]
                ddtr_ref[:, rows] += gdt
                db_ref[rows, :] += gb
                dc_ref[rows, :] += gc
                for r, g in zip(pgrads, gpar):
                    r[...] += g
                new.append(gprev)
            return tuple(new)

        zero = tuple(jnp.zeros((2 * HDIM, NSTATE), F32) for _ in range(NPAIR))
        lax.fori_loop(0, nc, bstep, (zero, zero))

    out_shape = [jax.ShapeDtypeStruct((bsz, seq, SSD_W), F32),
                 jax.ShapeDtypeStruct((bsz, seq, SGROUPS * NSTATE), F32),
                 jax.ShapeDtypeStruct((bsz, seq, SGROUPS * NSTATE), F32),
                 jax.ShapeDtypeStruct(dtr.shape, F32)]
    out_shape += [jax.ShapeDtypeStruct((bsz, SGROUPS, 2 * HPG, 1), F32)] * 3
    return pl.pallas_call(
        body, name=name, grid=(bsz, SGROUPS), in_specs=[xs, bm, cm, dtrs, pr, pr, pr, st, xs],
        out_specs=[xs, grp, grp, dtrs, dpr, dpr, dpr], out_shape=out_shape,
        compiler_params=_params(("parallel", "parallel")))(act, act, act, dtr, *prs, states, dy)


def _s5_core(lam_re, lam_im, log_step, b_re, b_im, c_re, c_im):
    q = S5_Q
    step = jnp.exp(log_step)[:, None]
    lr, li = lam_re * step, lam_im * step
    mag = jnp.exp(lr)
    ar, ai = mag * jnp.cos(li), mag * jnp.sin(li)
    den = lam_re * lam_re + lam_im * lam_im
    cr = ((ar - 1.0) * lam_re + ai * lam_im) / den
    ci = (ai * lam_re - (ar - 1.0) * lam_im) / den
    bbr = cr[..., None] * b_re - ci[..., None] * b_im
    bbi = cr[..., None] * b_im + ci[..., None] * b_re
    d = jnp.arange(q + 1, dtype=F32)[None, :, None]
    pm = jnp.exp(d * lr[:, None, :])
    pr, pi = pm * jnp.cos(d * li[:, None, :]), pm * jnp.sin(d * li[:, None, :])
    er = pr[..., None] * bbr[:, None] - pi[..., None] * bbi[:, None]
    ei = pr[..., None] * bbi[:, None] + pi[..., None] * bbr[:, None]
    hp = lax.Precision.HIGHEST
    k = (jnp.einsum('gcp,gdpz->gdcz', c_re, er[:, :q], precision=hp)
         - jnp.einsum('gcp,gdpz->gdcz', c_im, ei[:, :q], precision=hp))
    e = jnp.concatenate([er[:, :q], ei[:, :q]], axis=2)
    p1r, p1i = pr[:, 1:], pi[:, 1:]
    m_re = c_re[:, None] * p1r[:, :, None, :] - c_im[:, None] * p1i[:, :, None, :]
    m_im = -c_re[:, None] * p1i[:, :, None, :] - c_im[:, None] * p1r[:, :, None, :]
    da = jnp.concatenate([pr[:, q], pr[:, q]], axis=-1)
    db = jnp.concatenate([-pi[:, q], pi[:, q]], axis=-1)
    return k, e, jnp.concatenate([m_re, m_im], axis=-1), da, db


def _s5_operators(lf_re, lf_im, lsf, lb_re, lb_im, lsb, b_re, b_im, cf_re, cf_im, cb_re, cb_im):
    g = lf_re.shape[0]
    both = lambda f, b: jnp.concatenate([f, b], axis=0)
    k, e, m, da, db = _s5_core(both(lf_re, lb_re), both(lf_im, lb_im), both(lsf, lsb), both(b_re, b_re),
                               both(b_im, b_im), both(cf_re, cb_re), both(cf_im, cb_im))
    kf, kb = k[:g], k[g:]
    wtf, wtb = jnp.transpose(e[:g, ::-1], (0, 1, 3, 2)), jnp.transpose(e[g:], (0, 1, 3, 2))
    mtf, mtb = jnp.transpose(m[:g], (0, 3, 1, 2)), jnp.transpose(m[g:, ::-1], (0, 3, 1, 2))
    daf, dab, dbf, dbb = da[:g], da[g:], db[:g], db[g:]
    lags = jnp.concatenate([kb[:, :0:-1], kf[:, :1] + kb[:, :1], kf[:, 1:]], axis=1)
    tt = jnp.transpose(lags, (0, 1, 3, 2))
    wt = jnp.concatenate([wtf.reshape(g, S5_QC, 2 * S5_P), wtb.reshape(g, S5_QC, 2 * S5_P)], axis=-1)
    mt = jnp.concatenate([mtf.reshape(g, 2 * S5_P, S5_QC), mtb.reshape(g, 2 * S5_P, S5_QC)], axis=1)
    return tt, wt, mt, jnp.concatenate([daf, dab], -1), jnp.concatenate([dbf, dbb], -1)


def _gspec(*shape):
    return pl.BlockSpec((None,) + shape, lambda g: (g,) + (0,) * len(shape))


S5_HALVES = S5_QC // LANES


def _toeplitz_block(s, t):
    per = LANES // S5_C
    return t // per, slice(s * S5_C, (s + 1) * S5_C), slice((t % per) * S5_C, (t % per + 1) * S5_C)


def s5_toeplitz(kt, *, name):
    g = kt.shape[0]

    def body(k_ref, t_ref):
        for s in range(S5_Q):
            for t in range(S5_Q):
                t_ref[_toeplitz_block(s, t)] = k_ref[t - s + S5_Q - 1]

    return pl.pallas_call(
        body, name=name, grid=(g,), in_specs=[_gspec(2 * S5_Q - 1, S5_C, S5_C)],
        out_specs=_gspec(S5_HALVES, S5_QC, LANES), out_shape=jax.ShapeDtypeStruct((g, S5_HALVES, S5_QC, LANES), F32),
        compiler_params=_params(("parallel",)))(kt)


def s5_toeplitz_bwd(dtt, *, name):
    g = dtt.shape[0]

    def body(d_ref, k_ref):
        for j in range(2 * S5_Q - 1):
            acc = None
            for s in range(S5_Q):
                t = j - (S5_Q - 1) + s
                if 0 <= t < S5_Q:
                    blk = d_ref[_toeplitz_block(s, t)]
                    acc = blk if acc is None else acc + blk
            k_ref[j] = acc

    return pl.pallas_call(
        body, name=name, grid=(g,), in_specs=[_gspec(S5_HALVES, S5_QC, LANES)],
        out_specs=_gspec(2 * S5_Q - 1, S5_C, S5_C), out_shape=jax.ShapeDtypeStruct((g, 2 * S5_Q - 1, S5_C, S5_C), F32),
        compiler_params=_params(("parallel",)))(dtt)


S5_RT = 128


def _chunk_piece(q):
    per = LANES // S5_C
    return q // per, slice((q % per) * S5_C, (q % per + 1) * S5_C)


def to_chunks(u, *, name):
    t = u.shape[0]
    r = t // S5_Q
    rt = min(S5_RT, r)

    per = LANES // S5_C
    nblk = S5_W // LANES

    def body(*refs):
        o_ref = refs[-1]
        for k in range(nblk):
            for q in range(S5_Q):
                rows = refs[k][pl.ds(q, rt, stride=S5_Q), :]
                half, lanes = _chunk_piece(q)
                for j in range(per):
                    o_ref[k * per + j, half, :, lanes] = rows[:, j * S5_C:(j + 1) * S5_C].astype(BF16)

    return pl.pallas_call(
        body, name=name, grid=(r // rt,),
        in_specs=[pl.BlockSpec((rt * S5_Q, LANES), lambda i, k=k: (i, k)) for k in range(nblk)],
        out_specs=pl.BlockSpec((S5_G, S5_HALVES, rt, LANES), lambda i: (0, 0, i, 0)),
        out_shape=jax.ShapeDtypeStruct((S5_G, S5_HALVES, r, LANES), BF16),
        compiler_params=_params(("parallel",)))(*[u] * nblk)


def from_chunks(y, *, name, add=None, as_blocks=False):
    r = y.shape[2]
    rt = min(S5_RT, r)
    per = LANES // S5_C

    nblk = S5_W // LANES

    def body(*refs):
        y_ref, tmp_ref = refs[0], refs[-1]
        adds, outs = refs[1:-1 - nblk], refs[-1 - nblk:-1]
        for k in range(nblk):
            for q in range(S5_Q):
                half, lanes = _chunk_piece(q)
                for j in range(per):
                    tmp_ref[:, j * S5_C:(j + 1) * S5_C] = y_ref[k * per + j, half, :, lanes]
                row = tmp_ref[...]
                if add is not None:
                    row = row + adds[k][pl.ds(q, rt, stride=S5_Q), :]
                outs[k][pl.ds(q, rt, stride=S5_Q), :] = row

    in_specs = [pl.BlockSpec((S5_G, S5_HALVES, rt, LANES), lambda i: (0, 0, i, 0))]
    if add is not None:
        in_specs += [pl.BlockSpec((rt * S5_Q, LANES), lambda i, k=k: (i, k)) for k in range(nblk)]
    blocks = pl.pallas_call(
        body, name=name, grid=(r // rt,), in_specs=in_specs,
        out_specs=[pl.BlockSpec((rt * S5_Q, LANES), lambda i: (i, 0))] * nblk,
        out_shape=[jax.ShapeDtypeStruct((r * S5_Q, LANES), F32)] * nblk,
        scratch_shapes=[pltpu.VMEM((rt, LANES), F32)],
        compiler_params=_params(("parallel",)))(*([y] if add is None else [y] + [add] * nblk))
    return list(blocks) if as_blocks else jnp.concatenate(blocks, axis=1)


def _cat(ref):
    return jnp.concatenate([ref[h] for h in range(S5_HALVES)], axis=1)


def _put(ref, v):
    for h in range(S5_HALVES):
        ref[h] = v[:, h * LANES:(h + 1) * LANES]


def _mspec(gp, *shape):
    return pl.BlockSpec((gp,) + shape, lambda i: (i,) + (0,) * len(shape))


def _carry_spec(nck):
    return pl.BlockSpec((nck, 8, 4 * S5_P), lambda i: (0, i, 0))


def _carry_rows(ref, gl, bsz):
    return jnp.concatenate([ref[:, gl * bsz + b, :] for b in range(bsz)], axis=0)


def _carry_put(ref, gl, bsz, v):
    nck = v.shape[0] // bsz
    for b in range(bsz):
        ref[:, gl * bsz + b, :] = v[b * nck:(b + 1) * nck, :]


def s5_state_in(u, wt, *, bsz, name):
    g, _, r, _ = u.shape
    gp, nck = 8 // bsz, r // bsz

    def body(u_ref, w_ref, o_ref):
        for gl in range(gp):
            _carry_put(o_ref, gl, bsz, _bd(_cat(u_ref.at[gl]), w_ref[gl], 1, 0))

    return pl.pallas_call(
        body, name=name, grid=(g // gp,), in_specs=[_mspec(gp, S5_HALVES, r, LANES), _mspec(gp, S5_QC, 4 * S5_P)],
        out_specs=_carry_spec(nck), out_shape=jax.ShapeDtypeStruct((nck, g * bsz, 4 * S5_P), F32),
        compiler_params=_params(("parallel",)))(u, wt)


def _swap(h):
    return pltpu.roll(h, S5_P, 1)


def s5_carry_fwd(s, da, db, *, name):
    nck, rows, _ = s.shape
    w = 2 * S5_P

    def body(s_ref, da_ref, db_ref, h_ref):
        dirs = ((False, slice(0, w)), (True, slice(w, 2 * w)))
        coef = [(da_ref[:, cols], db_ref[:, cols]) for _, cols in dirs]

        def step(i, hs):
            new = []
            for (rev, cols), (a, b), h in zip(dirs, coef, hs):
                k = (nck - 1 - i) if rev else i
                h_ref[k, :, cols] = h
                new.append(a * h + b * _swap(h) + s_ref[k, :, cols])
            return tuple(new)

        z = jnp.zeros((rows, w), F32)
        lax.fori_loop(0, nck, step, (z, z), unroll=2)

    rt = min(2 * CARRY_ROWS, rows)
    big, small = pl.BlockSpec((nck, rt, 2 * w), lambda i: (0, i, 0)), pl.BlockSpec((rt, 2 * w), lambda i: (i, 0))
    rows = rt
    return pl.pallas_call(
        body, name=name, grid=(s.shape[1] // rt,), in_specs=[big, small, small], out_specs=big,
        out_shape=jax.ShapeDtypeStruct(s.shape, F32), compiler_params=_params(("parallel",)))(s, da, db)


def s5_carry_bwd(hin, dh, da, db, *, name):
    nck, rows, _ = hin.shape
    w = 2 * S5_P

    def body(h_ref, dh_ref, da_ref, db_ref, ds_ref, gda_ref, gdb_ref):
        dirs = ((False, slice(0, w)), (True, slice(w, 2 * w)))
        coef = [(da_ref[:, cols], db_ref[:, cols]) for _, cols in dirs]

        def step(i, carries):
            new = []
            for (rev, cols), (a, b), (g, ga, gb) in zip(dirs, coef, carries):
                k = i if rev else (nck - 1 - i)
                ds_ref[k, :, cols] = g
                h = h_ref[k, :, cols]
                new.append((dh_ref[k, :, cols] + a * g + _swap(b * g), ga + g * h, gb + g * _swap(h)))
            return tuple(new)

        z = jnp.zeros((rows, w), F32)
        res = lax.fori_loop(0, nck, step, ((z, z, z), (z, z, z)), unroll=2)
        for (_, cols), (_, ga, gb) in zip(dirs, res):
            gda_ref[:, cols] = ga
            gdb_ref[:, cols] = gb

    rt = min(CARRY_ROWS, rows)
    big, small = pl.BlockSpec((nck, rt, 2 * w), lambda i: (0, i, 0)), pl.BlockSpec((rt, 2 * w), lambda i: (i, 0))
    rows = rt
    return pl.pallas_call(
        body, name=name, grid=(hin.shape[1] // rt,), in_specs=[big, big, small, small], out_specs=[big, small, small],
        out_shape=[jax.ShapeDtypeStruct(hin.shape, F32), jax.ShapeDtypeStruct(da.shape, F32),
                   jax.ShapeDtypeStruct(da.shape, F32)],
        compiler_params=_params(("parallel",)))(hin, dh, da, db)


def s5_out(u, hin, tt, mt, *, bsz, name):
    g, _, r, _ = u.shape
    gp, nck = 8 // bsz, r // bsz

    def body(u_ref, h_ref, t_ref, m_ref, o_ref):
        for gl in range(gp):
            u_v, h_v = _cat(u_ref.at[gl]), _carry_rows(h_ref, gl, bsz)
            for half in range(S5_HALVES):
                cols = slice(half * LANES, (half + 1) * LANES)
                o_ref[gl, half] = _bd(u_v, t_ref[gl, half], 1, 0) + _bd(h_v, m_ref[gl, :, cols], 1, 0)

    cspec = _mspec(gp, S5_HALVES, r, LANES)
    return pl.pallas_call(
        body, name=name, grid=(g // gp,),
        in_specs=[cspec, _carry_spec(nck), _mspec(gp, S5_HALVES, S5_QC, LANES), _mspec(gp, 4 * S5_P, S5_QC)],
        out_specs=cspec, out_shape=jax.ShapeDtypeStruct((g, S5_HALVES, r, LANES), F32),
        compiler_params=_params(("parallel",)))(u, hin, tt, mt)


def s5_out_bwd(dy, u, hin, tt, mt, *, bsz, name):
    g, _, r, _ = u.shape
    gp, nck = 8 // bsz, r // bsz

    def body(dy_ref, u_ref, h_ref, t_ref, m_ref, dh_ref, dt_ref, dm_ref, du_ref):
        for gl in range(gp):
            dy_v, u_v = _cat(dy_ref.at[gl]), _cat(u_ref.at[gl])
            _carry_put(dh_ref, gl, bsz, _bd(dy_v, m_ref[gl], 1, 1))
            dm_ref[gl] = _bd(_carry_rows(h_ref, gl, bsz), dy_v, 0, 0)
            du = None
            for half in range(S5_HALVES):
                dy_h = dy_ref[gl, half]
                dt_ref[gl, half] = _bd(u_v, dy_h, 0, 0)
                part = _bd(dy_h, t_ref[gl, half], 1, 1)
                du = part if du is None else du + part
            _put(du_ref.at[gl], du)

    cspec, tspec = _mspec(gp, S5_HALVES, r, LANES), _mspec(gp, S5_HALVES, S5_QC, LANES)
    mspec = _mspec(gp, 4 * S5_P, S5_QC)
    return pl.pallas_call(
        body, name=name, grid=(g // gp,),
        in_specs=[cspec, cspec, _carry_spec(nck), tspec, mspec],
        out_specs=[_carry_spec(nck), tspec, mspec, cspec],
        out_shape=[jax.ShapeDtypeStruct((nck, g * bsz, 4 * S5_P), F32),
                   jax.ShapeDtypeStruct((g, S5_HALVES, S5_QC, LANES), F32),
                   jax.ShapeDtypeStruct((g, 4 * S5_P, S5_QC), F32), jax.ShapeDtypeStruct((g, S5_HALVES, r, LANES), F32)],
        compiler_params=_params(("parallel",)))(dy, u, hin, tt, mt)


def s5_state_in_bwd(ds, u, wt, du1, *, bsz, name):
    g, _, r, _ = u.shape
    gp, nck = 8 // bsz, r // bsz

    def body(ds_ref, u_ref, w_ref, du1_ref, du_ref, dw_ref):
        for gl in range(gp):
            ds_v = _carry_rows(ds_ref, gl, bsz)
            _put(du_ref.at[gl], _cat(du1_ref.at[gl]) + _bd(ds_v, w_ref[gl], 1, 1))
            dw_ref[gl] = _bd(_cat(u_ref.at[gl]), ds_v, 0, 0)

    cspec, wspec = _mspec(gp, S5_HALVES, r, LANES), _mspec(gp, S5_QC, 4 * S5_P)
    return pl.pallas_call(
        body, name=name, grid=(g // gp,),
        in_specs=[_carry_spec(nck), cspec, wspec, cspec], out_specs=[cspec, wspec],
        out_shape=[jax.ShapeDtypeStruct((g, S5_HALVES, r, LANES), F32), jax.ShapeDtypeStruct((g, S5_QC, 4 * S5_P), F32)],
        compiler_params=_params(("parallel",)))(ds, u, wt, du1)


def _s5_post(ypre, u, dvec, wv, wg, bv, bg, nw):
    g = _gelu(ypre + dvec * u)
    out = (dot_nn(g, wv) + bv) * jax.nn.sigmoid(dot_nn(g, wg) + bg)
    return (_rms(out, nw),)


def _ssd_post(y, z, nw):
    return (_rms(y * _silu(z), nw),)


def _block_diag(w):
    eye = jnp.eye(S5_G, dtype=w.dtype)
    return jnp.einsum('gcd,gh->gchd', w, eye).reshape(S5_W, S5_W)


def _diag_blocks(w):
    v = w.reshape(S5_G, S5_C, S5_G, S5_C)
    return v[jnp.arange(S5_G), :, jnp.arange(S5_G), :]


def _dt_rows(dt, bsz):
    seq = dt.shape[0] // bsz
    return jnp.transpose(dt.reshape(bsz, seq, 2, SGROUPS, HPG), (0, 3, 2, 4, 1)).reshape(bsz, SGROUPS, 2 * HPG, seq)


def _dt_from_rows(dr):
    bsz, _, _, seq = dr.shape
    return jnp.transpose(dr.reshape(bsz, SGROUPS, 2, HPG, seq), (0, 4, 2, 1, 3)).reshape(bsz * seq, 2 * HEADS)


def _head_params(f, b):
    return jnp.concatenate([f.reshape(SGROUPS, HPG), b.reshape(SGROUPS, HPG)], axis=1)[:, :, None]


def _head_grads(gr):
    v = gr.sum(0)[:, :, 0]
    return v[:, :HPG].reshape(HEADS), v[:, HPG:].reshape(HEADS)


def local_step(x, target, w):
    bsz, seq, d = x.shape
    t = bsz * seq
    x2, tgt2 = x.reshape(t, d), target.reshape(t, d)
    g = {}
    row = lambda v: v.reshape(1, -1)
    bf = lambda v: v.astype(BF16)

    w_in = _unshard(bf(w['w_in']), SHARDED['w_in'])
    cuts = [0, SSD_W, SSD_W + XBC_W, SSD_W + XBC_W + 2 * HEADS, w_in.shape[1]]
    w_in_parts = [w_in[:, a:b] for a, b in zip(cuts[:-1], cuts[1:])]
    norm_mix = row(w['norm_mix_w']) + w.get('token', 0.0)
    (hn,) = rowmap_fwd(lambda a, nw: (_rms(a, nw),), [x2], [norm_mix], [(d, BF16)], tm=512, name="rms_mix")
    z, xbc, dt, u = matmul_multi(hn, w_in_parts, name="in_proj")

    conv_w, conv_b = _unshard(w['ssd_conv_w'], SHARDED['ssd_conv_w']), row(w['ssd_conv_b'])
    act = ssd_conv_fwd(xbc, conv_w, conv_b, bsz=bsz, name="ssd_conv")
    dtr = _dt_rows(dt, bsz)
    prs = (_head_params(w['ssd_dt_bias_fwd'], w['ssd_dt_bias_bwd']),
           _head_params(w['ssd_a_log_fwd'], w['ssd_a_log_bwd']),
           _head_params(w['ssd_d'], jnp.zeros_like(w['ssd_d'])))
    act3 = act.reshape(bsz, seq, XBC_W)
    y_scan, ssd_states = ssd_scan_fwd(act3, dtr, prs, name="ssd_scan")
    y_scan = y_scan.reshape(t, SSD_W)
    ssd_nw = row(w['ssd_norm_w'])
    (y_ssd,) = rowmap_fwd(_ssd_post, [y_scan, z], [ssd_nw], [(SSD_W, BF16)], tm=512, name="ssd_post")

    s5_names = ['s5_lambda_re_fwd', 's5_lambda_im_fwd', 's5_log_step_fwd', 's5_lambda_re_bwd', 's5_lambda_im_bwd',
                's5_log_step_bwd', 's5_b_re', 's5_b_im', 's5_c_re_fwd', 's5_c_im_fwd', 's5_c_re_bwd', 's5_c_im_bwd']
    (kt, wt, mt, da, db), s5_pull = jax.vjp(_s5_operators, *[w[n] for n in s5_names])
    tt_b, wt_b, mt_b = s5_toeplitz(kt, name="s5_toeplitz"), bf(wt), bf(mt)
    da_r, db_r = jnp.repeat(da, bsz, axis=0), jnp.repeat(db, bsz, axis=0)
    uc = to_chunks(u, name="s5_to_chunks_u")
    hin = s5_carry_fwd(s5_state_in(uc, wt_b, bsz=bsz, name="s5_state_in"), da_r, db_r, name="s5_carry")
    ypre = from_chunks(s5_out(uc, hin, tt_b, mt_b, bsz=bsz, name="s5_out"), name="s5_from_chunks_y", as_blocks=True)
    glu_w = w['s5_glu_w']
    s5_par = [row(w['s5_d']), _block_diag(glu_w[:, :, :S5_C]), _block_diag(glu_w[:, :, S5_C:]),
              row(w['s5_glu_b'][:, :S5_C]), row(w['s5_glu_b'][:, S5_C:]), row(w['s5_norm_w'])]
    (y_s5,) = rowmap_fwd(_s5_post, [ypre, u], s5_par, [(S5_W, BF16)], tm=512, name="s5_post")

    if 'late' in w:
        w = {**w, **w['late'](y_s5)}
    w_out = bf(w['w_out']).reshape(SSD_W + S5_W, d)
    norm_ffn = row(w['norm_ffn_w'])
    h1, hn2 = matmul_sum([y_ssd, y_s5], [w_out[:SSD_W], w_out[SSD_W:]], add=x2, norm_w=norm_ffn, name="out_proj")
    pad_c = FFN_PAD - FFN_BLK
    half = N_DEV // 2
    w_up3 = jnp.pad(bf(w['ffn_w_up']), ((0, 0), (0, 0), (0, pad_c)))
    w_down = jnp.pad(bf(w['ffn_w_down']).reshape(half, FFN_BLK, d), ((0, 0), (0, pad_c), (0, 0)))
    w_down = w_down.reshape(half * FFN_PAD, d)
    fconv_w = jnp.pad(w['ffn_conv_w'], ((0, 0), (0, 0), (0, pad_c)))
    fconv_w = jnp.transpose(fconv_w, (1, 0, 2)).reshape(FCONV, N_DEV * FFN_PAD)
    fconv_b = row(jnp.pad(w['ffn_conv_b'].reshape(N_DEV, FFN_BLK), ((0, 0), (0, pad_c))))
    up = matmul_cols(hn2, w_up3, out_dtype=BF16, name="ffn_up")
    fact = ffn_act_fwd(up, fconv_w, fconv_b, bsz=bsz, name="ffn_act")
    loss, dh2, g_nf = loss_head(h1, tgt2, row(w['norm_final_w']), matmul=(fact, w_down), tm=512, name="ffn_down_loss")
    g['norm_final_w'] = g_nf.reshape(-1)

    dfact = matmul_sum([dh2], [w_down], nt=True, tm=1024, out_dtype=BF16, name="ffn_down_dx")
    g_down = matmul_tn(fact, dh2, name="ffn_down_dw").reshape(half, FFN_PAD, d)[:, :FFN_BLK]
    g['ffn_w_down'] = g_down.reshape(N_DEV, FFN_BLK // 2, d)
    dval, dgate, dwv, dwg, dbv, dbg = ffn_act_bwd(up, dfact, fconv_w, fconv_b, bsz=bsz, name="ffn_act_bwd")
    g_cw = jnp.concatenate([dwv, dwg], axis=1).reshape(FCONV, N_DEV, FFN_PAD)[:, :, :FFN_BLK]
    g['ffn_conv_w'] = jnp.transpose(g_cw, (1, 0, 2))
    g['ffn_conv_b'] = jnp.concatenate([dbv, dbg], axis=1).reshape(N_DEV, FFN_PAD)[:, :FFN_BLK].reshape(-1)
    windows = [(dval, FFN_PAD, p) for p in range(half)] + [(dgate, FFN_PAD, p) for p in range(half)]
    g['ffn_w_up'] = jnp.concatenate([matmul_tn(hn2, dval, out_blocks=half, name="ffn_up_dw_val"),
                                     matmul_tn(hn2, dgate, out_blocks=half, name="ffn_up_dw_gate")],
                                    axis=0)[:, :, :FFN_BLK]
    send_early = w.get('on_grads')
    if send_early:
        norm_ffn = norm_ffn + send_early(g, ['ffn_w_up', 'ffn_w_down'])
    dh1, g_nffn = matmul_sum(windows, [(w_up3, p) for p in range(N_DEV)], nt=True, tm=512,
                             norm_bwd=(h1, norm_ffn, dh2), name="ffn_up_dx")
    g['norm_ffn_w'] = g_nffn.reshape(-1)

    dycat = matmul_sum([dh1], [w_out], nt=True, tm=1024, name="out_proj_dx")
    g['w_out'] = jnp.concatenate([matmul_tn(y_ssd, dh1, name="out_proj_dw_ssd"),
                                  matmul_tn(y_s5, dh1, name="out_proj_dw_s5")], axis=0).reshape(w['w_out'].shape)
    if send_early:
        ssd_nw = ssd_nw + send_early(g, ['w_out'])
    dy_scan, dz, g_snw = rowmap_bwd(_ssd_post, [y_scan, z], [ssd_nw], [(dycat, SSD_W, 0)], tm=512,
                                    row_dtypes=[F32, BF16], name="ssd_post_bwd")
    g['ssd_norm_w'] = g_snw.reshape(-1)
    dypre, du_a, g_d, g_wv, g_wg, g_bv, g_bg, g_s5nw = rowmap_bwd(
        _s5_post, [ypre, u], s5_par, [(dycat, S5_W, SSD_W // S5_W)], tm=512, name="s5_post_bwd")
    g['s5_d'], g['s5_norm_w'] = g_d.reshape(-1), g_s5nw.reshape(-1)
    g['s5_glu_w'] = jnp.concatenate([_diag_blocks(g_wv), _diag_blocks(g_wg)], axis=-1)
    g['s5_glu_b'] = jnp.concatenate([g_bv.reshape(S5_G, S5_C), g_bg.reshape(S5_G, S5_C)], axis=-1)

    dyc = to_chunks(dypre, name="s5_to_chunks_dy")
    dhin, dtt, dmt, du1 = s5_out_bwd(dyc, uc, hin, tt_b, mt_b, bsz=bsz, name="s5_out_bwd")
    ds, gda, gdb = s5_carry_bwd(hin, dhin, da_r, db_r, name="s5_carry_bwd")
    duc, dwt = s5_state_in_bwd(ds, uc, wt_b, du1, bsz=bsz, name="s5_state_in_bwd")
    du = from_chunks(duc, add=du_a, name="s5_from_chunks_du")
    fold = lambda v: v.reshape(S5_G, bsz, -1).sum(1)
    dkt = s5_toeplitz_bwd(dtt, name="s5_toeplitz_bwd")
    for n, gv in zip(s5_names, s5_pull((dkt, dwt, dmt, fold(gda), fold(gdb)))):
        g[n] = gv

    dxs, dbm, dcm, ddtr, gbr, gar, gdk = ssd_scan_bwd(
        act3, dtr, prs, ssd_states, dy_scan.reshape(bsz, seq, SSD_W), name="ssd_scan_bwd")
    g['ssd_dt_bias_fwd'], g['ssd_dt_bias_bwd'] = _head_grads(gbr)
    g['ssd_a_log_fwd'], g['ssd_a_log_bwd'] = _head_grads(gar)
    g['ssd_d'] = _head_grads(gdk)[0]
    dparts_act = [v.reshape(t, v.shape[-1]) for v in (dxs, dbm, dcm)]
    dxbc, g_cw, g_cb = ssd_conv_bwd(xbc, dparts_act, conv_w, conv_b, bsz=bsz, name="ssd_conv_bwd")
    g['ssd_conv_w'] = _shard_rows(g_cw, SHARDED['ssd_conv_w']).reshape(w['ssd_conv_w'].shape)
    g['ssd_conv_b'] = g_cb.reshape(-1)
    ddt = _dt_from_rows(ddtr)

    if send_early:
        ddt = ddt + send_early(g, [], loss=loss)
    dparts = [dz, dxbc, ddt, du]
    g_in = jnp.concatenate([matmul_tn(hn, dp, name=f"in_proj_dw_{i}") for i, dp in enumerate(dparts)], axis=1)
    g['w_in'] = _shard_rows(g_in, SHARDED['w_in']).reshape(w['w_in'].shape)
    if send_early:
        dparts[2] = ddt + send_early(g, ['w_in'])
    dx, g_nmix = matmul_sum(dparts, w_in_parts, nt=True, tm=512, norm_bwd=(x2, norm_mix, dh1), name="in_proj_dx")
    g['norm_mix_w'] = g_nmix.reshape(-1)
    return loss, dx.reshape(bsz, seq, d), g


ANY = pl.BlockSpec(memory_space=pl.ANY)


def all_gather(shards, *, name):
    n = len(shards)

    def body(*refs):
        x_refs, out_refs = refs[:n], refs[n:2 * n]
        send_sems, recv_sems, local_sems = refs[2 * n:]
        x, y, c = lax.axis_index("x"), lax.axis_index("y"), lax.axis_index("c")
        me, sibling = (x, y, c), (x, y, 1 - c)
        chips = [(1 - x, y), (x, 1 - y), (1 - x, 1 - y)]

        def copy(k, j, block, to, own=False):
            dst = out_refs[j].at[4 * block[0] + 2 * block[1] + block[2]]
            return pltpu.make_async_remote_copy(
                src_ref=x_refs[j] if own else dst, dst_ref=dst,
                send_sem=send_sems.at[k, j], recv_sem=recv_sems.at[k, j], device_id=to, device_id_type=MESH)

        mine = [pltpu.make_async_copy(x_refs[j], out_refs[j].at[4 * x + 2 * y + c], local_sems.at[j]) for j in range(n)]
        first = [copy(0, j, me, sibling, own=True) for j in range(n)]
        first += [copy(1 + i, j, me, (*chip, c), own=True) for i, chip in enumerate(chips) for j in range(n)]
        for cp in mine + first:
            cp.start()
        passed = []
        for i, chip in enumerate(chips):
            for j in range(n):
                copy(1 + i, j, (*chip, c), me).wait_recv()
                passed.append(copy(4 + i, j, (*chip, c), sibling))
                passed[-1].start()
        for j in range(n):
            copy(0, j, sibling, me).wait_recv()
        for i, chip in enumerate(chips):
            for j in range(n):
                copy(4 + i, j, (*chip, 1 - c), me).wait_recv()
        for cp in first + passed:
            cp.wait_send()
        for cp in mine:
            cp.wait()

    return pl.pallas_call(
        body, name=name, out_shape=[jax.ShapeDtypeStruct((N_DEV,) + s.shape, s.dtype) for s in shards],
        in_specs=[ANY] * n, out_specs=[ANY] * n,
        scratch_shapes=[pltpu.SemaphoreType.DMA((7, n)), pltpu.SemaphoreType.DMA((7, n)),
                        pltpu.SemaphoreType.DMA((n,))],
    )(*shards)


HBM_SPEC = pl.BlockSpec(memory_space=pltpu.HBM)
SEM_SPEC = pl.BlockSpec(memory_space=pltpu.SEMAPHORE)
SPLIT_PARAMS = pltpu.CompilerParams(has_side_effects=pltpu.SideEffectType.DATAFLOW_SIDE_EFFECTING)


def _peer_copies(src_refs, land_refs, send_sems, recv_sems, indexed):
    x, y, c = lax.axis_index("x"), lax.axis_index("y"), lax.axis_index("c")
    me = 4 * x + 2 * y + c
    copies = []
    for k in range(1, N_DEV):
        px = (1 - x) if k & 4 else x
        py = (1 - y) if k & 2 else y
        pc = (1 - c) if k & 1 else c
        for j, (src, land) in enumerate(zip(src_refs, land_refs)):
            sem = (k - 1) * len(src_refs) + j
            copies.append(pltpu.make_async_remote_copy(
                src_ref=src.at[4 * px + 2 * py + pc] if indexed else src, dst_ref=land.at[me],
                send_sem=send_sems.at[sem], recv_sem=recv_sems.at[sem],
                device_id=(px, py, pc), device_id_type=MESH))
    return copies


def scatter_start(srcs, *, name, indexed):
    n = len(srcs)
    lands = [lax.empty(s.shape if indexed else (N_DEV,) + s.shape, s.dtype) for s in srcs]

    def body(*refs):
        send_sems, recv_sems = refs[2 * n], refs[2 * n + 1]
        for cp in _peer_copies(refs[:n], refs[n:2 * n], send_sems, recv_sems, indexed):
            cp.start()
        refs[-1][...] = jnp.zeros_like(refs[-1])

    hbm = lambda a: pltpu.HBM(a.shape, a.dtype)
    sems = pltpu.SemaphoreType.DMA(((N_DEV - 1) * n,))
    res = pl.pallas_call(
        body, name=name,
        out_shape=(sems, sems, *[hbm(a) for a in srcs + lands], jax.ShapeDtypeStruct((8, LANES), F32)),
        in_specs=[HBM_SPEC] * (2 * n),
        out_specs=(SEM_SPEC, SEM_SPEC, *[HBM_SPEC] * (2 * n), pl.BlockSpec(memory_space=pltpu.VMEM)),
        input_output_aliases={i: 2 + i for i in range(2 * n)}, compiler_params=SPLIT_PARAMS,
    )(*[pltpu.with_memory_space_constraint(a, pltpu.HBM) for a in srcs + lands])
    return res[0], res[1], list(res[2:2 + n]), list(res[2 + n:2 + 2 * n]), res[-1]


def scatter_wait(send_sems, recv_sems, srcs, lands, after, *, name, indexed):
    n = len(srcs)

    def body(*refs):
        for cp in _peer_copies(refs[:n], refs[n:2 * n], refs[2 * n], refs[2 * n + 1], indexed):
            cp.wait_send()
            cp.wait_recv()

    hbm = lambda a: pltpu.HBM(a.shape, a.dtype)
    res = pl.pallas_call(
        body, name=name, out_shape=tuple(hbm(a) for a in srcs + lands),
        in_specs=[HBM_SPEC] * (2 * n) + [SEM_SPEC, SEM_SPEC, ANY], out_specs=tuple([HBM_SPEC] * (2 * n)),
        input_output_aliases={i: i for i in range(2 * n)}, compiler_params=SPLIT_PARAMS,
    )(*srcs, *lands, send_sems, recv_sems, after)
    return list(res[:n]), list(res[n:])


def _adam_rows(r, c):
    fits = [t for t in range(8, r + 1, 8) if r % t == 0 and N_DEV * t * c * 4 <= 6 * 2 ** 20]
    return max(fits) if fits else r


def adamw(recv, w, m, v, *, name):
    _, r, n = recv.shape
    tr = _adam_rows(r, n)

    def body(r_ref, w_ref, m_ref, v_ref, g_ref, d_ref, nm_ref, nv_ref):
        g = r_ref[0].astype(F32)
        for s in range(1, N_DEV):
            g = g + r_ref[s].astype(F32)
        m_new = ADAM_B1 * m_ref[...] + (1.0 - ADAM_B1) * g
        v_new = ADAM_B2 * v_ref[...] + (1.0 - ADAM_B2) * jnp.square(g)
        m_hat = m_new / (1.0 - ADAM_B1 ** ADAM_STEP)
        v_hat = v_new / (1.0 - ADAM_B2 ** ADAM_STEP)
        g_ref[...] = g
        d_ref[...] = -ADAM_LR * (m_hat / (jnp.sqrt(v_hat) + ADAM_EPS) + ADAM_WD * w_ref[...])
        nm_ref[...] = m_new
        nv_ref[...] = v_new

    blk = pl.BlockSpec((tr, n), lambda i: (i, 0))
    return pl.pallas_call(
        body, name=name, grid=(r // tr,), in_specs=[pl.BlockSpec((N_DEV, tr, n), lambda i: (0, i, 0)), blk, blk, blk],
        out_specs=[blk] * 4, out_shape=[jax.ShapeDtypeStruct((r, n), F32)] * 4,
        compiler_params=_params(("parallel",)))(recv, w, m, v)


def _shard_rows(full, axis):
    if axis == 0:
        return full.reshape(N_DEV, -1)
    r, c = full.shape
    return jnp.transpose(full.reshape(r, N_DEV, c // N_DEV), (1, 0, 2)).reshape(N_DEV, -1)


def _unshard(blocks, axis):
    if axis == 0:
        return blocks.reshape(-1, blocks.shape[-1])
    return jnp.transpose(blocks, (1, 0, 2)).reshape(blocks.shape[1], -1)


def kernel(x, norm_mix_w, w_in, ssd_conv_w, ssd_conv_b, ssd_dt_bias_fwd, ssd_dt_bias_bwd, ssd_a_log_fwd, ssd_a_log_bwd, ssd_d, ssd_norm_w, s5_lambda_re_fwd, s5_lambda_im_fwd, s5_log_step_fwd, s5_lambda_re_bwd, s5_lambda_im_bwd, s5_log_step_bwd, s5_b_re, s5_b_im, s5_c_re_fwd, s5_c_im_fwd, s5_c_re_bwd, s5_c_im_bwd, s5_d, s5_glu_w, s5_glu_b, s5_norm_w, w_out, norm_ffn_w, ffn_w_up, ffn_conv_w, ffn_conv_b, ffn_w_down, norm_final_w, loss_target, m_norm_mix_w, m_w_in, m_ssd_conv_w, m_ssd_conv_b, m_ssd_dt_bias_fwd, m_ssd_dt_bias_bwd, m_ssd_a_log_fwd, m_ssd_a_log_bwd, m_ssd_d, m_ssd_norm_w, m_s5_lambda_re_fwd, m_s5_lambda_im_fwd, m_s5_log_step_fwd, m_s5_lambda_re_bwd, m_s5_lambda_im_bwd, m_s5_log_step_bwd, m_s5_b_re, m_s5_b_im, m_s5_c_re_fwd, m_s5_c_im_fwd, m_s5_c_re_bwd, m_s5_c_im_bwd, m_s5_d, m_s5_glu_w, m_s5_glu_b, m_s5_norm_w, m_w_out, m_norm_ffn_w, m_ffn_w_up, m_ffn_conv_w, m_ffn_conv_b, m_ffn_w_down, m_norm_final_w, v_norm_mix_w, v_w_in, v_ssd_conv_w, v_ssd_conv_b, v_ssd_dt_bias_fwd, v_ssd_dt_bias_bwd, v_ssd_a_log_fwd, v_ssd_a_log_bwd, v_ssd_d, v_ssd_norm_w, v_s5_lambda_re_fwd, v_s5_lambda_im_fwd, v_s5_log_step_fwd, v_s5_lambda_re_bwd, v_s5_lambda_im_bwd, v_s5_log_step_bwd, v_s5_b_re, v_s5_b_im, v_s5_c_re_fwd, v_s5_c_im_fwd, v_s5_c_re_bwd, v_s5_c_im_bwd, v_s5_d, v_s5_glu_w, v_s5_glu_b, v_s5_norm_w, v_w_out, v_norm_ffn_w, v_ffn_w_up, v_ffn_conv_w, v_ffn_conv_b, v_ffn_w_down, v_norm_final_w):
    args = dict(locals())
    strip = lambda n, v: v if n == 'norm_final_w' else v[0]
    w = {n: strip(n, args[n]) for n in WEIGHTS}

    mats = ['w_in', 'w_out', 'ffn_w_up', 'ffn_w_down']
    convs = ['ssd_conv_w', 'ffn_conv_w']
    shard = lambda n: w[n].astype(BF16) if n in mats else w[n]
    early, late = ['w_in', 'ssd_conv_w'], ['w_out', 'ffn_w_up', 'ffn_w_down', 'ffn_conv_w']
    full = dict(w)
    full.update(zip(early, all_gather([shard(n) for n in early], name="weight_all_gather")))
    ssem, rsem, src_thru, land_thru, token = scatter_start([shard(n) for n in late], name="weight_gather_start",
                                                           indexed=False)
    me = 4 * lax.axis_index("x") + 2 * lax.axis_index("y") + lax.axis_index("c")

    def late_weights(after):
        own, landed = scatter_wait(ssem, rsem, src_thru, land_thru, after, name="weight_gather_wait", indexed=False)
        return {n: lax.dynamic_update_index_in_dim(l, o, me, 0) for n, o, l in zip(late, own, landed)}

    full['late'], full['token'] = late_weights, token[:1, :1]

    pending = []
    last = 'norm_mix_w'
    small = convs + [n for n in WEIGHTS if n not in SHARDED and n != last]
    slot = {n: -(-w[n].size // (8 * LANES)) * 8 for n in small}
    used = sum(slot.values()) + 8
    nrow = -(-used // PACK_ROWS) * PACK_ROWS

    def tiles(v, n):
        return jnp.pad(v, ((0, 0), (0, slot[n] * LANES - v.shape[1]))).reshape(v.shape[0], slot[n], LANES)

    def send_early(grads, names, loss=None):
        srcs = [grads[n].astype(BF16) for n in names]
        if loss is not None:
            pieces = [tiles(grads[n].reshape(N_DEV, -1), n) if n in SHARDED else
                      jnp.broadcast_to(tiles(grads[n].reshape(1, -1), n), (N_DEV, slot[n], LANES)) for n in small]
            pieces.append(jnp.broadcast_to(jnp.pad(loss.reshape(1, 1, 1), ((0, 0), (0, 7), (0, LANES - 1))),
                                           (N_DEV, 8, LANES)))
            pieces.append(jnp.zeros((N_DEV, nrow - used, LANES), F32))
            srcs.append(jnp.concatenate(pieces, axis=1))
            names = names + ['small']
        started = scatter_start(srcs, name="grad_start_" + names[0], indexed=True)
        pending.append((names,) + started[:4])
        return started[4][:1, :1]

    full['on_grads'] = send_early
    loss, grad_x, g = local_step(x, loss_target, full)

    last_send = jnp.broadcast_to(g[last].reshape(1, -1, LANES), (N_DEV, g[last].size // LANES, LANES))
    last_started = scatter_start([last_send], name="grad_start_" + last, indexed=True)
    recv, outs = {}, [{}, {}, {}, {}]

    def arrived(names, started, after):
        own, landed = scatter_wait(*started, after, name="grad_wait_" + names[0], indexed=True)
        for n, o, l in zip(names, own, landed):
            recv[n] = lax.dynamic_update_index_in_dim(l, lax.dynamic_index_in_dim(o, me, 0, keepdims=False), me, 0)

    def update(n):
        shape = recv[n].shape[1:]
        res = adamw(recv[n], *[strip(n, args[p + n]).reshape(shape) for p in ('', 'm_', 'v_')], name="adamw_" + n)
        for o, p in zip(outs, res):
            o[n] = p.reshape(args[n].shape)

    for names, *started in pending:
        arrived(names, started, last_started[4])
    for n in mats:
        update(n)

    def pack(prefix):
        vals = [tiles(strip(n, args[prefix + n]).reshape(1, -1), n)[0] for n in small]
        return jnp.concatenate(vals + [jnp.zeros((nrow - used + 8, LANES), F32)], axis=0)

    packed = adamw(recv['small'], pack(''), pack('m_'), pack('v_'), name="adamw_small")
    arrived([last], last_started[:4], packed[1])
    update(last)
    off = 0
    for n in small:
        for o, p in zip(outs, packed):
            o[n] = p[off:off + slot[n]].reshape(-1)[:w[n].size].reshape(args[n].shape)
        off += slot[n]
    loss_out = packed[0][off, 0].reshape(())
    return (loss_out, grad_x, *[o[n] for o in outs for n in WEIGHTS])
```
